```python
import math
import jax, jax.numpy as jnp
from jax import lax
import numpy as np

D_MODEL = 1024
BATCH = 16
SEQ = 2048
DEPTH = 2

S5_WIDTH = D_MODEL // 4
S5_GROUP = 16
S5_GROUPS = S5_WIDTH // S5_GROUP
S5_STATE = 64
GLA_HEADS = 4
GLA_WIDTH = D_MODEL // 4
GLA_DV = GLA_WIDTH // GLA_HEADS
GLA_DK = GLA_DV // 2
GLA_QK = GLA_HEADS * GLA_DK
GLA_RANK = 16
GLA_TAU = 16.0
GLA_CHUNK = 64
SWA_WIDTH = D_MODEL - S5_WIDTH - GLA_WIDTH
SWA_HEAD_DIM = 64
SWA_HEADS = SWA_WIDTH // SWA_HEAD_DIM
SWA_KV_HEADS = 2
SWA_KV = SWA_KV_HEADS * SWA_HEAD_DIM
SWA_WINDOW = 128
SWA_BLOCK = 128
ROT_DIM = SWA_HEAD_DIM // 4
ROPE_THETA = 500000.0
D_FF = 4 * D_MODEL
LN_EPS = 1e-5
DEEPNORM_ALPHA = (2 * DEPTH) ** 0.25
DEEPNORM_BETA = (8 * DEPTH) ** -0.25
NEG_BIG = -1e30

IN_SIZES = (S5_WIDTH, GLA_QK, GLA_QK, GLA_WIDTH, GLA_WIDTH, GLA_RANK, GLA_RANK,
            SWA_WIDTH, SWA_KV, SWA_KV)
D_IN = sum(IN_SIZES)

kernel_name = "hybrid_s5_gla_swa_deepnorm_encoder"


def layer_norm(x, g, b):
    xf = x.astype(jnp.float32)
    mu = jnp.mean(xf, axis=-1, keepdims=True)
    var = jnp.mean(jnp.square(xf - mu), axis=-1, keepdims=True)
    y = (xf - mu) * lax.rsqrt(var + LN_EPS) * g.astype(jnp.float32) + b.astype(jnp.float32)
    return y.astype(x.dtype)


def _linear_rec(lhs, rhs):
    a1, b1 = lhs
    a2, b2 = rhs
    return a1 * a2, a2 * b1 + b2


def s5_mixer(u, a_re, a_im, log_step, b_re, b_im, c_re, c_im, d_skip, w_glu, b_glu):
    f32 = jnp.float32
    bsz, L, _ = u.shape
    u = u.astype(f32).reshape(bsz, L, S5_GROUPS, S5_GROUP)
    lam = lax.complex(a_re.astype(f32), a_im.astype(f32))
    step = jnp.exp(log_step.astype(f32))
    lam_bar = jnp.exp(lam * step)
    b_c = lax.complex(b_re.astype(f32), b_im.astype(f32))
    b_bar = ((lam_bar - 1.0) / lam)[..., None] * b_c
    bu = jnp.einsum('zgph,blgh->zblgp', b_bar, u.astype(jnp.complex64))
    a_f = jnp.broadcast_to(lam_bar[0], (1, L, S5_GROUPS, S5_STATE))
    a_b = jnp.broadcast_to(lam_bar[1], (1, L, S5_GROUPS, S5_STATE))
    _, h_f = lax.associative_scan(_linear_rec, (a_f, bu[0]), axis=1)
    _, h_b = lax.associative_scan(_linear_rec, (a_b, bu[1]), axis=1, reverse=True)
    c_c = lax.complex(c_re.astype(f32), c_im.astype(f32))
    y = (jnp.einsum('ghp,blgp->blgh', c_c[0], h_f).real
         + jnp.einsum('ghp,blgp->blgh', c_c[1], h_b).real
         + d_skip.astype(f32) * u)
    z = jax.nn.gelu(y.reshape(bsz, L, S5_WIDTH))
    val, gate = jnp.split(z @ w_glu.astype(f32) + b_glu.astype(f32), 2, axis=-1)
    return val * jax.nn.sigmoid(gate)


def gla_chunked(q, k, v, log_a, strict):
    bsz, nh, L, dk = q.shape
    dv = v.shape[-1]
    c = GLA_CHUNK
    n = L // c
    q, k, log_a = [t.reshape(bsz, nh, n, c, dk) for t in (q, k, log_a)]
    v = v.reshape(bsz, nh, n, c, dv)
    b = jnp.cumsum(log_a, axis=3)
    b_last = b[:, :, :, -1:]
    q_in = q * jnp.exp(b)
    k_in = k * jnp.exp(-b)
    k_st = k * jnp.exp(b_last - b)
    scores = jnp.einsum('bhnid,bhnjd->bhnij', q_in, k_in)
    mask = jnp.tril(jnp.ones((c, c), dtype=bool), k=-1 if strict else 0)
    o_intra = jnp.einsum('bhnij,bhnjv->bhniv', jnp.where(mask, scores, 0.0), v)
    kv = jnp.einsum('bhnjd,bhnjv->bhndv', k_st, v)
    decay = jnp.exp(b_last[:, :, :, 0])

    def step(state, inp):
        kv_n, dec_n = inp
        return dec_n[..., None] * state + kv_n, state

    init = jnp.zeros((bsz, nh, dk, dv), q.dtype)
    _, states = lax.scan(step, init, (jnp.moveaxis(kv, 2, 0), jnp.moveaxis(decay, 2, 0)))
    states = jnp.moveaxis(states, 0, 2)
    o_inter = jnp.einsum('bhnid,bhndv->bhniv', q_in, states)
    return (o_intra + o_inter).reshape(bsz, nh, L, dv)


def gla_mixer(q, k, v, r, lr_f, lr_b, w_a, b_a, ln_g):
    f32 = jnp.float32
    bsz, L, _ = q.shape

    def heads(t, d):
        return t.astype(f32).reshape(bsz, L, GLA_HEADS, d).transpose(0, 2, 1, 3)

    q = heads(q, GLA_DK) * (GLA_DK ** -0.5)
    k = heads(k, GLA_DK)
    v = heads(v, GLA_DV)
    w_a = w_a.astype(f32)
    b_a = b_a.astype(f32)
    la_f = heads(jax.nn.log_sigmoid(lr_f.astype(f32) @ w_a[0] + b_a[0]) / GLA_TAU, GLA_DK)
    la_b = heads(jax.nn.log_sigmoid(lr_b.astype(f32) @ w_a[1] + b_a[1]) / GLA_TAU, GLA_DK)
    flip = lambda t: jnp.flip(t, axis=2)
    o = (gla_chunked(q, k, v, la_f, strict=False)
         + flip(gla_chunked(flip(q), flip(k), flip(v), flip(la_b), strict=True)))
    mu = jnp.mean(o, axis=-1, keepdims=True)
    var = jnp.mean(jnp.square(o - mu), axis=-1, keepdims=True)
    o = ((o - mu) * lax.rsqrt(var + LN_EPS)).transpose(0, 2, 1, 3).reshape(bsz, L, GLA_WIDTH)
    return o * ln_g.astype(f32) * jax.nn.silu(r.astype(f32))


def rope_partial(t, cos, sin):
    rot, rest = t[..., :ROT_DIM], t[..., ROT_DIM:]
    x1, x2 = rot[..., :ROT_DIM // 2], rot[..., ROT_DIM // 2:]
    rotated = jnp.concatenate([x1 * cos - x2 * sin, x2 * cos + x1 * sin], axis=-1)
    return jnp.concatenate([rotated.astype(t.dtype), rest], axis=-1)


def swa_mixer(q, k, v, sink):
    f32 = jnp.float32
    bsz, L, _ = q.shape
    nb = L // SWA_BLOCK
    grp = SWA_HEADS // SWA_KV_HEADS
    q = q.reshape(bsz, L, SWA_HEADS, SWA_HEAD_DIM)
    k = k.reshape(bsz, L, SWA_KV_HEADS, SWA_HEAD_DIM)
    v = v.reshape(bsz, L, SWA_KV_HEADS, SWA_HEAD_DIM)
    pos = jnp.arange(L, dtype=f32)
    inv_freq = ROPE_THETA ** (-jnp.arange(0, ROT_DIM, 2, dtype=f32) / ROT_DIM)
    ang = pos[:, None] * inv_freq[None, :]
    cos, sin = jnp.cos(ang)[:, None, :], jnp.sin(ang)[:, None, :]
    q = rope_partial(q, cos, sin)
    k = rope_partial(k, cos, sin)

    def band(t):
        tp = jnp.pad(t, ((0, 0), (SWA_BLOCK, SWA_BLOCK), (0, 0), (0, 0)))
        tp = tp.reshape(bsz, nb + 2, SWA_BLOCK, SWA_KV_HEADS, SWA_HEAD_DIM)
        return jnp.concatenate([tp[:, :-2], tp[:, 1:-1], tp[:, 2:]], axis=2)

    kb, vb = band(k), band(v)
    qb = q.reshape(bsz, nb, SWA_BLOCK, SWA_KV_HEADS, grp, SWA_HEAD_DIM)
    s = jnp.einsum('bnqhgd,bnkhd->bnhgqk', qb, kb).astype(f32) * (SWA_HEAD_DIM ** -0.5)
    blk = jnp.arange(nb)[:, None] * SWA_BLOCK
    qpos = blk + jnp.arange(SWA_BLOCK)[None, :]
    kpos = blk - SWA_BLOCK + jnp.arange(3 * SWA_BLOCK)[None, :]
    valid = ((jnp.abs(qpos[:, :, None] - kpos[:, None, :]) <= SWA_WINDOW)
             & (kpos >= 0)[:, None, :] & (kpos < L)[:, None, :])
    s = jnp.where(valid[None, :, None, None], s, NEG_BIG)
    sink_col = jnp.broadcast_to(sink.astype(f32).reshape(1, 1, SWA_KV_HEADS, grp, 1, 1),
                                s.shape[:-1] + (1,))
    p = jax.nn.softmax(jnp.concatenate([s, sink_col], axis=-1), axis=-1)[..., :-1]
    o = jnp.einsum('bnhgqk,bnkhd->bnqhgd', p.astype(vb.dtype), vb)
    return o.reshape(bsz, L, SWA_WIDTH)


def hybrid_mixer(x, w_in, a_re, a_im, log_step, b_re, b_im, c_re, c_im, d_skip,
                 w_glu, b_glu, gla_w_a, gla_b_a, gla_ln_g, swa_sink, w_out):
    h = x @ w_in
    points = np.cumsum(IN_SIZES)[:-1].tolist()
    (s5_u, g_q, g_k, g_v, g_r, g_lf, g_lb, a_q, a_k, a_v) = jnp.split(h, points, axis=-1)
    y_a = s5_mixer(s5_u, a_re, a_im, log_step, b_re, b_im, c_re, c_im, d_skip, w_glu, b_glu)
    y_b = gla_mixer(g_q, g_k, g_v, g_r, g_lf, g_lb, gla_w_a, gla_b_a, gla_ln_g)
    y_c = swa_mixer(a_q, a_k, a_v, swa_sink)
    y = jnp.concatenate([y_a.astype(x.dtype), y_b.astype(x.dtype), y_c.astype(x.dtype)], axis=-1)
    return y @ w_out


def _fwd_setup_inputs(seed: int = 0) -> dict:
    key = jax.random.key(seed)
    ks = jax.random.split(key, 24)
    f32 = jnp.float32
    nrm = lambda k, shape, scale: jax.random.normal(k, shape, f32) * scale
    L2 = (DEPTH, 2, S5_GROUPS, S5_STATE)
    x = nrm(ks[0], (BATCH, SEQ, D_MODEL), 1.0)
    w_in = nrm(ks[1], (DEPTH, D_MODEL, D_IN), D_MODEL ** -0.5)
    s5_a_re = -0.5 + nrm(ks[2], L2, 0.01)
    s5_a_im = math.pi * jnp.arange(S5_STATE, dtype=f32) + nrm(ks[3], L2, 0.01)
    s5_log_step = jax.random.uniform(ks[4], L2, f32, math.log(1e-3), math.log(1e-1))
    s5_b_re = nrm(ks[5], L2 + (S5_GROUP,), (2 * S5_GROUP) ** -0.5)
    s5_b_im = nrm(ks[6], L2 + (S5_GROUP,), (2 * S5_GROUP) ** -0.5)
    s5_c_re = nrm(ks[7], (DEPTH, 2, S5_GROUPS, S5_GROUP, S5_STATE), S5_STATE ** -0.5)
    s5_c_im = nrm(ks[8], (DEPTH, 2, S5_GROUPS, S5_GROUP, S5_STATE), S5_STATE ** -0.5)
    s5_d = nrm(ks[9], (DEPTH, S5_GROUPS, S5_GROUP), 1.0)
    s5_w_glu = nrm(ks[10], (DEPTH, S5_WIDTH, 2 * S5_WIDTH), S5_WIDTH ** -0.5)
    s5_b_glu = nrm(ks[11], (DEPTH, 2 * S5_WIDTH), 0.01)
    gla_w_a = nrm(ks[12], (DEPTH, 2, GLA_RANK, GLA_QK), GLA_RANK ** -0.5)
    gla_b_a = nrm(ks[13], (DEPTH, 2, GLA_QK), 0.01)
    gla_ln_g = 1.0 + nrm(ks[14], (DEPTH, GLA_WIDTH), 0.02)
    swa_sink = nrm(ks[15], (DEPTH, SWA_HEADS), 0.5)
    w_out = nrm(ks[16], (DEPTH, D_MODEL, D_MODEL), D_MODEL ** -0.5 * DEEPNORM_BETA)
    ln1_g = 1.0 + nrm(ks[17], (DEPTH, D_MODEL), 0.02)
    ln1_b = nrm(ks[18], (DEPTH, D_MODEL), 0.01)
    w_ff1 = nrm(ks[19], (DEPTH, D_MODEL, D_FF), D_MODEL ** -0.5)
    w_ff2 = nrm(ks[20], (DEPTH, D_FF, D_MODEL), D_FF ** -0.5 * DEEPNORM_BETA)
    ln2_g = 1.0 + nrm(ks[21], (DEPTH, D_MODEL), 0.02)
    ln2_b = nrm(ks[22], (DEPTH, D_MODEL), 0.01)
    return {"x": x, "w_in": w_in, "s5_a_re": s5_a_re, "s5_a_im": s5_a_im,
            "s5_log_step": s5_log_step, "s5_b_re": s5_b_re, "s5_b_im": s5_b_im,
            "s5_c_re": s5_c_re, "s5_c_im": s5_c_im, "s5_d": s5_d,
            "s5_w_glu": s5_w_glu, "s5_b_glu": s5_b_glu, "gla_w_a": gla_w_a,
            "gla_b_a": gla_b_a, "gla_ln_g": gla_ln_g, "swa_sink": swa_sink,
            "w_out": w_out, "ln1_g": ln1_g, "ln1_b": ln1_b, "w_ff1": w_ff1,
            "w_ff2": w_ff2, "ln2_g": ln2_g, "ln2_b": ln2_b}


def _fwd_reference(x, w_in, s5_a_re, s5_a_im, s5_log_step, s5_b_re, s5_b_im, s5_c_re, s5_c_im,
              s5_d, s5_w_glu, s5_b_glu, gla_w_a, gla_b_a, gla_ln_g, swa_sink, w_out,
              ln1_g, ln1_b, w_ff1, w_ff2, ln2_g, ln2_b):
    for l in range(DEPTH):
        mix = hybrid_mixer(x, w_in[l], s5_a_re[l], s5_a_im[l], s5_log_step[l],
                           s5_b_re[l], s5_b_im[l], s5_c_re[l], s5_c_im[l], s5_d[l],
                           s5_w_glu[l], s5_b_glu[l], gla_w_a[l], gla_b_a[l], gla_ln_g[l],
                           swa_sink[l], w_out[l])
        x = layer_norm(DEEPNORM_ALPHA * x + mix, ln1_g[l], ln1_b[l])
        hid = jnp.square(jax.nn.relu(x @ w_ff1[l]))
        x = layer_norm(DEEPNORM_ALPHA * x + hid @ w_ff2[l], ln2_g[l], ln2_b[l])
    return x


import jax as _jax
import jax.numpy as _jnp

TWIN_FORMAT = 'train_step'
FWD_PARAMS = ['x', 'w_in', 's5_a_re', 's5_a_im', 's5_log_step', 's5_b_re', 's5_b_im', 's5_c_re', 's5_c_im', 's5_d', 's5_w_glu', 's5_b_glu', 'gla_w_a', 'gla_b_a', 'gla_ln_g', 'swa_sink', 'w_out', 'ln1_g', 'ln1_b', 'w_ff1', 'w_ff2', 'ln2_g', 'ln2_b']
TWIN_WEIGHTS = ['w_in', 's5_a_re', 's5_a_im', 's5_log_step', 's5_b_re', 's5_b_im', 's5_c_re', 's5_c_im', 's5_d', 's5_w_glu', 's5_b_glu', 'gla_w_a', 'gla_b_a', 'gla_ln_g', 'swa_sink', 'w_out', 'ln1_g', 'ln1_b', 'w_ff1', 'w_ff2', 'ln2_g', 'ln2_b']
TWIN_DIFF_INPUT = 'x'
TWIN_INPUTS = ['x', 'w_in', 's5_a_re', 's5_a_im', 's5_log_step', 's5_b_re', 's5_b_im', 's5_c_re', 's5_c_im', 's5_d', 's5_w_glu', 's5_b_glu', 'gla_w_a', 'gla_b_a', 'gla_ln_g', 'swa_sink', 'w_out', 'ln1_g', 'ln1_b', 'w_ff1', 'w_ff2', 'ln2_g', 'ln2_b', 'loss_target', 'm_w_in', 'm_s5_a_re', 'm_s5_a_im', 'm_s5_log_step', 'm_s5_b_re', 'm_s5_b_im', 'm_s5_c_re', 'm_s5_c_im', 'm_s5_d', 'm_s5_w_glu', 'm_s5_b_glu', 'm_gla_w_a', 'm_gla_b_a', 'm_gla_ln_g', 'm_swa_sink', 'm_w_out', 'm_ln1_g', 'm_ln1_b', 'm_w_ff1', 'm_w_ff2', 'm_ln2_g', 'm_ln2_b', 'v_w_in', 'v_s5_a_re', 'v_s5_a_im', 'v_s5_log_step', 'v_s5_b_re', 'v_s5_b_im', 'v_s5_c_re', 'v_s5_c_im', 'v_s5_d', 'v_s5_w_glu', 'v_s5_b_glu', 'v_gla_w_a', 'v_gla_b_a', 'v_gla_ln_g', 'v_swa_sink', 'v_w_out', 'v_ln1_g', 'v_ln1_b', 'v_w_ff1', 'v_w_ff2', 'v_ln2_g', 'v_ln2_b']
TWIN_OUTPUTS = ['loss', 'grad_x', 'grad_w_in', 'grad_s5_a_re', 'grad_s5_a_im', 'grad_s5_log_step', 'grad_s5_b_re', 'grad_s5_b_im', 'grad_s5_c_re', 'grad_s5_c_im', 'grad_s5_d', 'grad_s5_w_glu', 'grad_s5_b_glu', 'grad_gla_w_a', 'grad_gla_b_a', 'grad_gla_ln_g', 'grad_swa_sink', 'grad_w_out', 'grad_ln1_g', 'grad_ln1_b', 'grad_w_ff1', 'grad_w_ff2', 'grad_ln2_g', 'grad_ln2_b', 'delta_w_in', 'delta_s5_a_re', 'delta_s5_a_im', 'delta_s5_log_step', 'delta_s5_b_re', 'delta_s5_b_im', 'delta_s5_c_re', 'delta_s5_c_im', 'delta_s5_d', 'delta_s5_w_glu', 'delta_s5_b_glu', 'delta_gla_w_a', 'delta_gla_b_a', 'delta_gla_ln_g', 'delta_swa_sink', 'delta_w_out', 'delta_ln1_g', 'delta_ln1_b', 'delta_w_ff1', 'delta_w_ff2', 'delta_ln2_g', 'delta_ln2_b', 'new_m_w_in', 'new_m_s5_a_re', 'new_m_s5_a_im', 'new_m_s5_log_step', 'new_m_s5_b_re', 'new_m_s5_b_im', 'new_m_s5_c_re', 'new_m_s5_c_im', 'new_m_s5_d', 'new_m_s5_w_glu', 'new_m_s5_b_glu', 'new_m_gla_w_a', 'new_m_gla_b_a', 'new_m_gla_ln_g', 'new_m_swa_sink', 'new_m_w_out', 'new_m_ln1_g', 'new_m_ln1_b', 'new_m_w_ff1', 'new_m_w_ff2', 'new_m_ln2_g', 'new_m_ln2_b', 'new_v_w_in', 'new_v_s5_a_re', 'new_v_s5_a_im', 'new_v_s5_log_step', 'new_v_s5_b_re', 'new_v_s5_b_im', 'new_v_s5_c_re', 'new_v_s5_c_im', 'new_v_s5_d', 'new_v_s5_w_glu', 'new_v_s5_b_glu', 'new_v_gla_w_a', 'new_v_gla_b_a', 'new_v_gla_ln_g', 'new_v_swa_sink', 'new_v_w_out', 'new_v_ln1_g', 'new_v_ln1_b', 'new_v_w_ff1', 'new_v_w_ff2', 'new_v_ln2_g', 'new_v_ln2_b']
TWIN_LEAF_KINDS = {'loss': 'loss', 'grad_x': 'grad_x', 'grad_w_in': 'grad_w', 'grad_s5_a_re': 'grad_w', 'grad_s5_a_im': 'grad_w', 'grad_s5_log_step': 'grad_w', 'grad_s5_b_re': 'grad_w', 'grad_s5_b_im': 'grad_w', 'grad_s5_c_re': 'grad_w', 'grad_s5_c_im': 'grad_w', 'grad_s5_d': 'grad_w', 'grad_s5_w_glu': 'grad_w', 'grad_s5_b_glu': 'grad_w', 'grad_gla_w_a': 'grad_w', 'grad_gla_b_a': 'grad_w', 'grad_gla_ln_g': 'grad_w', 'grad_swa_sink': 'grad_w', 'grad_w_out': 'grad_w', 'grad_ln1_g': 'grad_w', 'grad_ln1_b': 'grad_w', 'grad_w_ff1': 'grad_w', 'grad_w_ff2': 'grad_w', 'grad_ln2_g': 'grad_w', 'grad_ln2_b': 'grad_w', 'delta_w_in': 'delta_w', 'delta_s5_a_re': 'delta_w', 'delta_s5_a_im': 'delta_w', 'delta_s5_log_step': 'delta_w', 'delta_s5_b_re': 'delta_w', 'delta_s5_b_im': 'delta_w', 'delta_s5_c_re': 'delta_w', 'delta_s5_c_im': 'delta_w', 'delta_s5_d': 'delta_w', 'delta_s5_w_glu': 'delta_w', 'delta_s5_b_glu': 'delta_w', 'delta_gla_w_a': 'delta_w', 'delta_gla_b_a': 'delta_w', 'delta_gla_ln_g': 'delta_w', 'delta_swa_sink': 'delta_w', 'delta_w_out': 'delta_w', 'delta_ln1_g': 'delta_w', 'delta_ln1_b': 'delta_w', 'delta_w_ff1': 'delta_w', 'delta_w_ff2': 'delta_w', 'delta_ln2_g': 'delta_w', 'delta_ln2_b': 'delta_w', 'new_m_w_in': 'new_m', 'new_m_s5_a_re': 'new_m', 'new_m_s5_a_im': 'new_m', 'new_m_s5_log_step': 'new_m', 'new_m_s5_b_re': 'new_m', 'new_m_s5_b_im': 'new_m', 'new_m_s5_c_re': 'new_m', 'new_m_s5_c_im': 'new_m', 'new_m_s5_d': 'new_m', 'new_m_s5_w_glu': 'new_m', 'new_m_s5_b_glu': 'new_m', 'new_m_gla_w_a': 'new_m', 'new_m_gla_b_a': 'new_m', 'new_m_gla_ln_g': 'new_m', 'new_m_swa_sink': 'new_m', 'new_m_w_out': 'new_m', 'new_m_ln1_g': 'new_m', 'new_m_ln1_b': 'new_m', 'new_m_w_ff1': 'new_m', 'new_m_w_ff2': 'new_m', 'new_m_ln2_g': 'new_m', 'new_m_ln2_b': 'new_m', 'new_v_w_in': 'new_v', 'new_v_s5_a_re': 'new_v', 'new_v_s5_a_im': 'new_v', 'new_v_s5_log_step': 'new_v', 'new_v_s5_b_re': 'new_v', 'new_v_s5_b_im': 'new_v', 'new_v_s5_c_re': 'new_v', 'new_v_s5_c_im': 'new_v', 'new_v_s5_d': 'new_v', 'new_v_s5_w_glu': 'new_v', 'new_v_s5_b_glu': 'new_v', 'new_v_gla_w_a': 'new_v', 'new_v_gla_b_a': 'new_v', 'new_v_gla_ln_g': 'new_v', 'new_v_swa_sink': 'new_v', 'new_v_w_out': 'new_v', 'new_v_ln1_g': 'new_v', 'new_v_ln1_b': 'new_v', 'new_v_w_ff1': 'new_v', 'new_v_w_ff2': 'new_v', 'new_v_ln2_g': 'new_v', 'new_v_ln2_b': 'new_v'}


def _forward(args):
    return _fwd_reference(*[args[k] for k in FWD_PARAMS])


def _output_shape():
    out = _jax.eval_shape(lambda: _forward(_fwd_setup_inputs(0)))
    return out.shape, out.dtype

N_MICROBATCH = 1
ADAM_LR = 0.001
ADAM_B1 = 0.9
ADAM_B2 = 0.999
ADAM_EPS = 1e-08
ADAM_WD = 0.01
ADAM_STEP = 10
PER_EXAMPLE_BATCH_AXIS = {'x': 0, 'loss_target': 0}
SHARED_INPUTS = []
_WEIGHT_DTYPES = {'w_in': _jnp.float32, 's5_a_re': _jnp.float32, 's5_a_im': _jnp.float32, 's5_log_step': _jnp.float32, 's5_b_re': _jnp.float32, 's5_b_im': _jnp.float32, 's5_c_re': _jnp.float32, 's5_c_im': _jnp.float32, 's5_d': _jnp.float32, 's5_w_glu': _jnp.float32, 's5_b_glu': _jnp.float32, 'gla_w_a': _jnp.float32, 'gla_b_a': _jnp.float32, 'gla_ln_g': _jnp.float32, 'swa_sink': _jnp.float32, 'w_out': _jnp.float32, 'ln1_g': _jnp.float32, 'ln1_b': _jnp.float32, 'w_ff1': _jnp.float32, 'w_ff2': _jnp.float32, 'ln2_g': _jnp.float32, 'ln2_b': _jnp.float32}
MOMENT_SCALE = {'w_in': 3.250667e-02, 's5_a_re': 2.144191e-03, 's5_a_im': 1.879319e-03, 's5_log_step': 1.802935e-01, 's5_b_re': 1.240019e-03, 's5_b_im': 1.206614e-03, 's5_c_re': 1.756593e-03, 's5_c_im': 1.686648e-03, 's5_d': 2.698119e-02, 's5_w_glu': 1.900653e-02, 's5_b_glu': 3.724368e-02, 'gla_w_a': 4.847183e-03, 'gla_b_a': 1.892159e-02, 'gla_ln_g': 3.955857e-02, 'swa_sink': 4.804701e-04, 'w_out': 4.946348e-02, 'ln1_g': 8.252754e-01, 'ln1_b': 2.823080e-01, 'w_ff1': 4.162458e-02, 'w_ff2': 1.621040e-01, 'ln2_g': 2.270821e+01, 'ln2_b': 5.044858e+00}


def _to_microbatches(a, axis):
    t = _jnp.moveaxis(a, axis, 0)
    t = t.reshape((N_MICROBATCH, t.shape[0] // N_MICROBATCH) + t.shape[1:])
    return _jnp.moveaxis(t, 1, axis + 1)


def setup_inputs(seed: int = 0) -> dict:
    inp = _fwd_setup_inputs(seed)
    key = _jax.random.fold_in(_jax.random.key(seed), 7919)
    shape, _ = _output_shape()
    out = dict(inp)
    out["loss_target"] = _jax.random.normal(_jax.random.fold_in(key, 0), shape, _jnp.float32)
    for i, name in enumerate(TWIN_WEIGHTS):
        w = inp[name].astype(_jnp.float32)
        if MOMENT_SCALE is None:
            s = _jnp.sqrt(_jnp.mean(_jnp.square(w)) + 1e-30)
        else:
            s = MOMENT_SCALE[name]
        km, kv = _jax.random.split(_jax.random.fold_in(key, i + 1))
        out[name] = w
        out["m_" + name] = s * _jax.random.normal(km, w.shape, _jnp.float32)
        out["v_" + name] = (s * s) * _jax.random.uniform(kv, w.shape, _jnp.float32, 0.5, 1.5)
    if N_MICROBATCH > 1:
        for name, axis in PER_EXAMPLE_BATCH_AXIS.items():
            out[name] = _to_microbatches(out[name], axis)
    return {'x': out['x'], 'w_in': out['w_in'], 's5_a_re': out['s5_a_re'], 's5_a_im': out['s5_a_im'], 's5_log_step': out['s5_log_step'], 's5_b_re': out['s5_b_re'], 's5_b_im': out['s5_b_im'], 's5_c_re': out['s5_c_re'], 's5_c_im': out['s5_c_im'], 's5_d': out['s5_d'], 's5_w_glu': out['s5_w_glu'], 's5_b_glu': out['s5_b_glu'], 'gla_w_a': out['gla_w_a'], 'gla_b_a': out['gla_b_a'], 'gla_ln_g': out['gla_ln_g'], 'swa_sink': out['swa_sink'], 'w_out': out['w_out'], 'ln1_g': out['ln1_g'], 'ln1_b': out['ln1_b'], 'w_ff1': out['w_ff1'], 'w_ff2': out['w_ff2'], 'ln2_g': out['ln2_g'], 'ln2_b': out['ln2_b'], 'loss_target': out['loss_target'], 'm_w_in': out['m_w_in'], 'm_s5_a_re': out['m_s5_a_re'], 'm_s5_a_im': out['m_s5_a_im'], 'm_s5_log_step': out['m_s5_log_step'], 'm_s5_b_re': out['m_s5_b_re'], 'm_s5_b_im': out['m_s5_b_im'], 'm_s5_c_re': out['m_s5_c_re'], 'm_s5_c_im': out['m_s5_c_im'], 'm_s5_d': out['m_s5_d'], 'm_s5_w_glu': out['m_s5_w_glu'], 'm_s5_b_glu': out['m_s5_b_glu'], 'm_gla_w_a': out['m_gla_w_a'], 'm_gla_b_a': out['m_gla_b_a'], 'm_gla_ln_g': out['m_gla_ln_g'], 'm_swa_sink': out['m_swa_sink'], 'm_w_out': out['m_w_out'], 'm_ln1_g': out['m_ln1_g'], 'm_ln1_b': out['m_ln1_b'], 'm_w_ff1': out['m_w_ff1'], 'm_w_ff2': out['m_w_ff2'], 'm_ln2_g': out['m_ln2_g'], 'm_ln2_b': out['m_ln2_b'], 'v_w_in': out['v_w_in'], 'v_s5_a_re': out['v_s5_a_re'], 'v_s5_a_im': out['v_s5_a_im'], 'v_s5_log_step': out['v_s5_log_step'], 'v_s5_b_re': out['v_s5_b_re'], 'v_s5_b_im': out['v_s5_b_im'], 'v_s5_c_re': out['v_s5_c_re'], 'v_s5_c_im': out['v_s5_c_im'], 'v_s5_d': out['v_s5_d'], 'v_s5_w_glu': out['v_s5_w_glu'], 'v_s5_b_glu': out['v_s5_b_glu'], 'v_gla_w_a': out['v_gla_w_a'], 'v_gla_b_a': out['v_gla_b_a'], 'v_gla_ln_g': out['v_gla_ln_g'], 'v_swa_sink': out['v_swa_sink'], 'v_w_out': out['v_w_out'], 'v_ln1_g': out['v_ln1_g'], 'v_ln1_b': out['v_ln1_b'], 'v_w_ff1': out['v_w_ff1'], 'v_w_ff2': out['v_w_ff2'], 'v_ln2_g': out['v_ln2_g'], 'v_ln2_b': out['v_ln2_b']}


def _loss(weights, diff, rest, loss_target):
    with _jax.named_scope("forward"):
        args = {**rest, TWIN_DIFF_INPUT: diff, **{k: w.astype(_WEIGHT_DTYPES[k]) for k, w in weights.items()}}
        y = _forward(args)
    with _jax.named_scope("loss_head"):
        err = _jnp.square(y.astype(_jnp.float32) - loss_target)
        return 0.5 * _jnp.sum(_jnp.mean(err, axis=-1)) if err.ndim else 0.5 * err


def _adamw(w, g, m, v):
    m = ADAM_B1 * m + (1.0 - ADAM_B1) * g
    v = ADAM_B2 * v + (1.0 - ADAM_B2) * _jnp.square(g)
    m_hat = m / (1.0 - ADAM_B1 ** ADAM_STEP)
    v_hat = v / (1.0 - ADAM_B2 ** ADAM_STEP)
    delta = -ADAM_LR * (m_hat / (_jnp.sqrt(v_hat) + ADAM_EPS) + ADAM_WD * w)
    return delta, m, v


def reference(x, w_in, s5_a_re, s5_a_im, s5_log_step, s5_b_re, s5_b_im, s5_c_re, s5_c_im, s5_d, s5_w_glu, s5_b_glu, gla_w_a, gla_b_a, gla_ln_g, swa_sink, w_out, ln1_g, ln1_b, w_ff1, w_ff2, ln2_g, ln2_b, loss_target, m_w_in, m_s5_a_re, m_s5_a_im, m_s5_log_step, m_s5_b_re, m_s5_b_im, m_s5_c_re, m_s5_c_im, m_s5_d, m_s5_w_glu, m_s5_b_glu, m_gla_w_a, m_gla_b_a, m_gla_ln_g, m_swa_sink, m_w_out, m_ln1_g, m_ln1_b, m_w_ff1, m_w_ff2, m_ln2_g, m_ln2_b, v_w_in, v_s5_a_re, v_s5_a_im, v_s5_log_step, v_s5_b_re, v_s5_b_im, v_s5_c_re, v_s5_c_im, v_s5_d, v_s5_w_glu, v_s5_b_glu, v_gla_w_a, v_gla_b_a, v_gla_ln_g, v_swa_sink, v_w_out, v_ln1_g, v_ln1_b, v_w_ff1, v_w_ff2, v_ln2_g, v_ln2_b):
    given = dict(x=x, w_in=w_in, s5_a_re=s5_a_re, s5_a_im=s5_a_im, s5_log_step=s5_log_step, s5_b_re=s5_b_re, s5_b_im=s5_b_im, s5_c_re=s5_c_re, s5_c_im=s5_c_im, s5_d=s5_d, s5_w_glu=s5_w_glu, s5_b_glu=s5_b_glu, gla_w_a=gla_w_a, gla_b_a=gla_b_a, gla_ln_g=gla_ln_g, swa_sink=swa_sink, w_out=w_out, ln1_g=ln1_g, ln1_b=ln1_b, w_ff1=w_ff1, w_ff2=w_ff2, ln2_g=ln2_g, ln2_b=ln2_b, loss_target=loss_target, m_w_in=m_w_in, m_s5_a_re=m_s5_a_re, m_s5_a_im=m_s5_a_im, m_s5_log_step=m_s5_log_step, m_s5_b_re=m_s5_b_re, m_s5_b_im=m_s5_b_im, m_s5_c_re=m_s5_c_re, m_s5_c_im=m_s5_c_im, m_s5_d=m_s5_d, m_s5_w_glu=m_s5_w_glu, m_s5_b_glu=m_s5_b_glu, m_gla_w_a=m_gla_w_a, m_gla_b_a=m_gla_b_a, m_gla_ln_g=m_gla_ln_g, m_swa_sink=m_swa_sink, m_w_out=m_w_out, m_ln1_g=m_ln1_g, m_ln1_b=m_ln1_b, m_w_ff1=m_w_ff1, m_w_ff2=m_w_ff2, m_ln2_g=m_ln2_g, m_ln2_b=m_ln2_b, v_w_in=v_w_in, v_s5_a_re=v_s5_a_re, v_s5_a_im=v_s5_a_im, v_s5_log_step=v_s5_log_step, v_s5_b_re=v_s5_b_re, v_s5_b_im=v_s5_b_im, v_s5_c_re=v_s5_c_re, v_s5_c_im=v_s5_c_im, v_s5_d=v_s5_d, v_s5_w_glu=v_s5_w_glu, v_s5_b_glu=v_s5_b_glu, v_gla_w_a=v_gla_w_a, v_gla_b_a=v_gla_b_a, v_gla_ln_g=v_gla_ln_g, v_swa_sink=v_swa_sink, v_w_out=v_w_out, v_ln1_g=v_ln1_g, v_ln1_b=v_ln1_b, v_w_ff1=v_w_ff1, v_w_ff2=v_w_ff2, v_ln2_g=v_ln2_g, v_ln2_b=v_ln2_b)
    weights = {n: given[n] for n in TWIN_WEIGHTS}
    shared = {n: given[n] for n in SHARED_INPUTS}
    per_example = {n: given[n] for n in ['x']}
    grad_fn = _jax.value_and_grad(_loss, argnums=(0, 1))

    def one_microbatch(ex, loss_target):
        ex = dict(ex)
        diff = ex.pop(TWIN_DIFF_INPUT)
        return grad_fn(weights, diff, {**shared, **ex}, loss_target)

    if N_MICROBATCH == 1:
        loss, (grad_w, grad_x) = one_microbatch(per_example, given["loss_target"])
    else:
        def body(carry, xs):
            loss_sum, grad_sum = carry
            l_k, (gw_k, gx_k) = one_microbatch(xs[0], xs[1])
            with _jax.named_scope("update"):
                return (loss_sum + l_k, _jax.tree.map(_jnp.add, grad_sum, gw_k)), gx_k

        init = (_jnp.zeros((), _jnp.float32), _jax.tree.map(_jnp.zeros_like, weights))
        (loss, grad_w), grad_x = _jax.lax.scan(body, init, (per_example, given["loss_target"]))
    with _jax.named_scope("update"):
        delta_w, new_m, new_v = {}, {}, {}
        for n in TWIN_WEIGHTS:
            delta_w[n], new_m[n], new_v[n] = _adamw(weights[n], grad_w[n], given["m_" + n], given["v_" + n])
    return (loss, grad_x, *[grad_w[n] for n in TWIN_WEIGHTS], *[delta_w[n] for n in TWIN_WEIGHTS],
            *[new_m[n] for n in TWIN_WEIGHTS], *[new_v[n] for n in TWIN_WEIGHTS])
```

```python
import functools
import math

import jax
import jax.numpy as jnp
from jax import lax
from jax.experimental import pallas as pl
from jax.experimental.pallas import tpu as pltpu

F32 = jnp.float32
MX = jnp.bfloat16
MESH = pl.DeviceIdType.MESH

DEPTH = 2
NSEQ = 2
L = 2048
N = NSEQ * L
D = 1024
DFF = 4096
NSHARD = 4
S5_G, S5_H, S5_P = 16, 16, 64
GLA_CHUNK = 64
NCHUNK = L // GLA_CHUNK
SWA_BLK = 128
NBLK = L // SWA_BLK
ROT = 16
ROPE_THETA = 500000.0
LN_EPS = 1e-5
ALPHA = (2 * DEPTH) ** 0.25
NEG_BIG = -1e30
DIN = 1824
DINP = 1920
ADAM_LR, ADAM_B1, ADAM_B2, ADAM_EPS, ADAM_WD, ADAM_STEP = 0.001, 0.9, 0.999, 1e-08, 0.01, 10
VMEM_LIMIT = 56 * 1024 * 1024
TT = 512
SW = 512


def _cp(sem, vmem=VMEM_LIMIT):
    return pltpu.CompilerParams(dimension_semantics=sem, vmem_limit_bytes=vmem)


def _mm(a, b):
    return jnp.dot(a.astype(MX), b.astype(MX), preferred_element_type=F32)


def _mm_nt(a, b):
    return lax.dot_general(a.astype(MX), b.astype(MX), (((1,), (1,)), ((), ())), preferred_element_type=F32)


def _mm_tn(a, b):
    return lax.dot_general(a.astype(MX), b.astype(MX), (((0,), (0,)), ((), ())), preferred_element_type=F32)


@jax.custom_vjp
def _dmm(a, b):
    return _mm(a, b)


_dmm.defvjp(lambda a, b: (_mm(a, b), (a, b)), lambda r, g: (_mm_nt(g, r[1]), _mm_tn(r[0], g)))


@jax.custom_vjp
def _dmm_nt(a, b):
    return _mm_nt(a, b)


_dmm_nt.defvjp(lambda a, b: (_mm_nt(a, b), (a, b)), lambda r, g: (_mm(g, r[1]), _mm_tn(g, r[0])))


@jax.custom_vjp
def _dmm_tn(a, b):
    return _mm_tn(a, b)


_dmm_tn.defvjp(lambda a, b: (_mm_tn(a, b), (a, b)), lambda r, g: (_mm_nt(r[1], g), _mm(r[0], g)))


def _split3(x):
    hi = x.astype(MX)
    r1 = x - hi.astype(F32)
    mid = r1.astype(MX)
    lo = (r1 - mid.astype(F32)).astype(MX)
    return hi, mid, lo


def _tri(rev):
    r = lax.broadcasted_iota(jnp.int32, (GLA_CHUNK, GLA_CHUNK), 0)
    c = lax.broadcasted_iota(jnp.int32, (GLA_CHUNK, GLA_CHUNK), 1)
    return jnp.where((c >= r) if rev else (c <= r), 1.0, 0.0).astype(MX)


def _cums_impl(x, rev):
    t = _tri(rev)
    return sum(jnp.dot(t, p, preferred_element_type=F32) for p in _split3(x))


@functools.partial(jax.custom_vjp, nondiff_argnums=(1,))
def _cums(x, rev):
    return _cums_impl(x, rev)


_cums.defvjp(lambda x, rev: (_cums_impl(x, rev), None), lambda rev, r, g: (_cums_impl(g, not rev),))


def _ln_fwd(s, g, b):
    mu = jnp.mean(s, axis=-1, keepdims=True)
    xc = s - mu
    var = jnp.mean(xc * xc, axis=-1, keepdims=True)
    return xc * lax.rsqrt(var + LN_EPS) * g + b


def _ln_bwd(dy, s, g):
    mu = jnp.mean(s, axis=-1, keepdims=True)
    xc = s - mu
    var = jnp.mean(xc * xc, axis=-1, keepdims=True)
    rstd = lax.rsqrt(var + LN_EPS)
    xhat = xc * rstd
    dxh = dy * g
    ds = rstd * (dxh - jnp.mean(dxh, axis=-1, keepdims=True) - xhat * jnp.mean(dxh * xhat, axis=-1, keepdims=True))
    return ds, jnp.sum(dy * xhat, axis=0, keepdims=True), jnp.sum(dy, axis=0, keepdims=True)


def _sds(shape, dtype=F32):
    return jax.ShapeDtypeStruct(shape, dtype)


def _inproj_fwd(x, w):
    tm = 512

    def body(x_ref, w_ref, h_ref):
        h_ref[...] = _mm(x_ref[...], w_ref[...])

    return pl.pallas_call(
        body, grid=(N // tm,),
        in_specs=[pl.BlockSpec((tm, D), lambda i: (i, 0)), pl.BlockSpec((D, DINP), lambda i: (0, 0))],
        out_specs=pl.BlockSpec((tm, DINP), lambda i: (i, 0)),
        out_shape=_sds((N, DINP)), name="inproj_fwd", compiler_params=_cp(("parallel",)))(x, w)


def _inproj_bwd(x, w, dxp, du2, dud, gq_f, gq_b, gk_f, gk_b, gv_f, gv_b, gr, daq, dakv, dhl):
    tm = 256
    nt = N // tm

    def body(x_ref, w_ref, dxp_ref, du2_ref, dud_ref, gqf, gqb, gkf, gkb, gvf, gvb, gr_ref, daq_ref, dakv_ref, dhl_ref,
             dx_ref, dw_ref):
        i = pl.program_id(0)
        dh = jnp.concatenate([
            du2_ref[0] + du2_ref[1] + dud_ref[...], gqf[...] + gqb[...], gkf[...] + gkb[...], gvf[...] + gvb[...],
            gr_ref[...], daq_ref[...], dakv_ref[...], dhl_ref[...]], axis=1)
        dx_ref[...] = dxp_ref[...] + _mm_nt(dh, w_ref[...])
        contrib = _mm_tn(x_ref[...], dh)

        @pl.when(i == 0)
        def _():
            dw_ref[...] = contrib

        @pl.when(i > 0)
        def _():
            dw_ref[...] += contrib

    row = lambda w_: pl.BlockSpec((tm, w_), lambda i: (i, 0))
    return pl.pallas_call(
        body, grid=(nt,),
        in_specs=[row(D), pl.BlockSpec((D, DINP), lambda i: (0, 0)), row(D),
                  pl.BlockSpec((2, tm, 256), lambda i: (0, i, 0)), row(256), row(128), row(128), row(128), row(128),
                  row(256), row(256), row(256), row(512), row(256), row(128)],
        out_specs=[row(D), pl.BlockSpec((D, DINP), lambda i: (0, 0))],
        out_shape=[_sds((N, D)), _sds((D, DINP))],
        name="inproj_bwd", compiler_params=_cp(("arbitrary",)))(
            x, w, dxp, du2, dud, gq_f, gq_b, gk_f, gk_b, gv_f, gv_b, gr, daq, dakv, dhl)


def _scan_tables(mr, mi, reverse):
    pw = [(mr, mi)]
    for _ in range(7):
        pr, pi = pw[-1]
        pw.append((pr * mr - pi * mi, pr * mi + pi * mr))
    rows = jnp.arange(8)[:, None]
    out = []
    for d in (1, 2, 4):
        keep = rows >= d
        out += [jnp.where(keep, pw[d - 1][0][None], 0.0), jnp.where(keep, pw[d - 1][1][None], 0.0)]
    out += [jnp.stack([p[0] for p in pw]), jnp.stack([p[1] for p in pw])]
    t = jnp.stack(out)
    if reverse:
        t = t[:, ::-1, :]
    return t.reshape(8, 8, 2, SW).transpose(2, 0, 1, 3)


def _tile_scan(xr, xi, a, cr, ci, reverse):
    for lvl, d in enumerate((1, 2, 4)):
        sh = 8 - d if reverse else d
        sr = pltpu.roll(xr, sh, 0)
        si = pltpu.roll(xi, sh, 0)
        ar, ai = a[2 * lvl], a[2 * lvl + 1]
        xr, xi = xr + ar * sr - ai * si, xi + ar * si + ai * sr
    pr, pi = a[6], a[7]
    return xr + pr * cr - pi * ci, xi + pr * ci + pi * cr


def _s5_time_block(z, s, t, adjoint):
    flip = (1 - z) if adjoint else z
    return s * (L // TT) + t + flip * (L // TT - 1 - 2 * t)


def _s5_fwd(h, bre, bim, cre, cim, tab):
    nt = L // TT

    def body(u_ref, bre_ref, bim_ref, cre_ref, cim_ref, tab_ref, hre_ref, him_ref, y_ref, car):
        z = pl.program_id(1)
        tc = pl.program_id(3)

        @pl.when(tc == 0)
        def _():
            car[...] = jnp.zeros_like(car)

        u = u_ref[...]
        hre_ref[0] = _mm(u, bre_ref[0, 0])
        him_ref[0] = _mm(u, bim_ref[0, 0])

        def run(reverse):
            a = [tab_ref[0, 0, k] for k in range(8)]

            def step(i, carry):
                cr, ci = carry
                r0 = pl.multiple_of((TT // 8 - 1 - i if reverse else i) * 8, 8)
                xr, xi = _tile_scan(hre_ref[0, pl.ds(r0, 8), :], him_ref[0, pl.ds(r0, 8), :], a, cr, ci, reverse)
                hre_ref[0, pl.ds(r0, 8), :] = xr
                him_ref[0, pl.ds(r0, 8), :] = xi
                row = 0 if reverse else 7
                return (jnp.broadcast_to(xr[row:row + 1, :], (8, SW)), jnp.broadcast_to(xi[row:row + 1, :], (8, SW)))

            cr, ci = lax.fori_loop(0, TT // 8, step, (car[0], car[1]))
            car[0] = cr
            car[1] = ci

        @pl.when(z == 0)
        def _():
            run(False)

        @pl.when(z == 1)
        def _():
            run(True)

        y_ref[0] = _mm(hre_ref[0], cre_ref[0, 0]) - _mm(him_ref[0], cim_ref[0, 0])

    tb = lambda b, z, s, t: _s5_time_block(z, s, t, False)
    wspec = lambda r, c: pl.BlockSpec((1, 1, r, c), lambda b, z, s, t: (z, b, 0, 0))
    return pl.pallas_call(
        body, grid=(2, 2, NSEQ, nt),
        in_specs=[pl.BlockSpec((TT, 128), lambda b, z, s, t: (tb(b, z, s, t), b)),
                  wspec(128, SW), wspec(128, SW), wspec(SW, 128), wspec(SW, 128),
                  pl.BlockSpec((1, 1, 8, 8, SW), lambda b, z, s, t: (z, b, 0, 0, 0))],
        out_specs=[pl.BlockSpec((1, TT, SW), lambda b, z, s, t: (z, tb(b, z, s, t), b)),
                   pl.BlockSpec((1, TT, SW), lambda b, z, s, t: (z, tb(b, z, s, t), b)),
                   pl.BlockSpec((1, TT, 128), lambda b, z, s, t: (z, tb(b, z, s, t), b))],
        out_shape=[_sds((2, N, 2 * SW)), _sds((2, N, 2 * SW)), _sds((2, N, 256))],
        scratch_shapes=[pltpu.VMEM((2, 8, SW), F32)],
        name="s5_fwd", compiler_params=_cp(("arbitrary",) * 4))(h, bre, bim, cre, cim, tab)


def _s5_bwd(h, dyp, hre, him, bre, bim, cre, cim, tabc):
    nt = L // TT

    def body(u_ref, dy_ref, hre_ref, him_ref, bre_ref, bim_ref, cre_ref, cim_ref, tab_ref,
             du_ref, dbre_ref, dbim_ref, dcre_ref, dcim_ref, dmu_ref, gre, gim, car):
        z = pl.program_id(1)
        s = pl.program_id(2)
        tc = pl.program_id(3)

        @pl.when(tc == 0)
        def _():
            car[...] = jnp.zeros_like(car)

        @pl.when((tc == 0) & (s == 0))
        def _():
            dbre_ref[...] = jnp.zeros_like(dbre_ref)
            dbim_ref[...] = jnp.zeros_like(dbim_ref)
            dcre_ref[...] = jnp.zeros_like(dcre_ref)
            dcim_ref[...] = jnp.zeros_like(dcim_ref)
            dmu_ref[...] = jnp.zeros_like(dmu_ref)

        dy = dy_ref[...]
        gre[...] = _mm_nt(dy, cre_ref[0, 0])
        gim[...] = -_mm_nt(dy, cim_ref[0, 0])
        rowid = lax.broadcasted_iota(jnp.int32, (8, SW), 0)

        def run(reverse):
            a = [tab_ref[0, 0, k] for k in range(8)]
            first = 7 if reverse else 0

            def step(i, carry):
                cr, ci, dmr, dmi = carry
                r0 = pl.multiple_of((TT // 8 - 1 - i if reverse else i) * 8, 8)
                xr, xi = _tile_scan(gre[pl.ds(r0, 8), :], gim[pl.ds(r0, 8), :], a, cr, ci, reverse)
                gre[pl.ds(r0, 8), :] = xr
                gim[pl.ds(r0, 8), :] = xi
                sh = 7 if reverse else 1
                gpr = jnp.where(rowid == first, cr, pltpu.roll(xr, sh, 0))
                gpi = jnp.where(rowid == first, ci, pltpu.roll(xi, sh, 0))
                hr = hre_ref[0, pl.ds(r0, 8), :]
                hi = him_ref[0, pl.ds(r0, 8), :]
                dmr = dmr + gpr * hr + gpi * hi
                dmi = dmi + gpi * hr - gpr * hi
                row = 0 if reverse else 7
                return (jnp.broadcast_to(xr[row:row + 1, :], (8, SW)), jnp.broadcast_to(xi[row:row + 1, :], (8, SW)),
                        dmr, dmi)

            cr, ci, dmr, dmi = lax.fori_loop(0, TT // 8, step, (car[0], car[1], dmu_ref[0, 0, 0], dmu_ref[0, 0, 1]))
            car[0] = cr
            car[1] = ci
            dmu_ref[0, 0, 0] = dmr
            dmu_ref[0, 0, 1] = dmi

        @pl.when(z == 0)
        def _():
            run(True)

        @pl.when(z == 1)
        def _():
            run(False)

        gr = gre[...]
        gi = gim[...]
        u = u_ref[...]
        du_ref[0] = _mm_nt(gr, bre_ref[0, 0]) + _mm_nt(gi, bim_ref[0, 0])
        dbre_ref[0, 0] += _mm_tn(u, gr)
        dbim_ref[0, 0] += _mm_tn(u, gi)
        dcre_ref[0, 0] += _mm_tn(hre_ref[0], dy)
        dcim_ref[0, 0] -= _mm_tn(him_ref[0], dy)

    tb = lambda b, z, s, t: _s5_time_block(z, s, t, True)
    wspec = lambda r, c: pl.BlockSpec((1, 1, r, c), lambda b, z, s, t: (z, b, 0, 0))
    tok = lambda w_: pl.BlockSpec((TT, w_), lambda b, z, s, t: (tb(b, z, s, t), b))
    st = pl.BlockSpec((1, TT, SW), lambda b, z, s, t: (z, tb(b, z, s, t), b))
    return pl.pallas_call(
        body, grid=(2, 2, NSEQ, nt),
        in_specs=[tok(128), tok(128), st, st, wspec(128, SW), wspec(128, SW), wspec(SW, 128), wspec(SW, 128),
                  pl.BlockSpec((1, 1, 8, 8, SW), lambda b, z, s, t: (z, b, 0, 0, 0))],
        out_specs=[pl.BlockSpec((1, TT, 128), lambda b, z, s, t: (z, tb(b, z, s, t), b)),
                   wspec(128, SW), wspec(128, SW), wspec(SW, 128), wspec(SW, 128),
                   pl.BlockSpec((1, 1, 2, 8, SW), lambda b, z, s, t: (z, b, 0, 0, 0))],
        out_shape=[_sds((2, N, 256)), _sds((2, 2, 128, SW)), _sds((2, 2, 128, SW)), _sds((2, 2, SW, 128)),
                   _sds((2, 2, SW, 128)), _sds((2, 2, 2, 8, SW))],
        scratch_shapes=[pltpu.VMEM((TT, SW), F32), pltpu.VMEM((TT, SW), F32), pltpu.VMEM((2, 8, SW), F32)],
        name="s5_bwd", compiler_params=_cp(("arbitrary",) * 4))(h, dyp, hre, him, bre, bim, cre, cim, tabc)


_GELU_C = math.sqrt(2.0 / math.pi)


def _gelu(y):
    return 0.5 * y * (1.0 + jnp.tanh(_GELU_C * (y + 0.044715 * y * y * y)))


def _gelu_grad(y):
    t = jnp.tanh(_GELU_C * (y + 0.044715 * y * y * y))
    return 0.5 * (1.0 + t) + 0.5 * y * (1.0 - t * t) * _GELU_C * (1.0 + 3 * 0.044715 * y * y)


def _s5_glu_fwd(y2, h, dsk, wv, wg, bv, bg):
    tm = 512

    def body(y2_ref, u_ref, d_ref, wv_ref, wg_ref, bv_ref, bg_ref, ya_ref):
        z = _gelu(y2_ref[0] + y2_ref[1] + d_ref[...] * u_ref[...])
        val = _mm(z, wv_ref[...]) + bv_ref[...]
        gate = _mm(z, wg_ref[...]) + bg_ref[...]
        ya_ref[...] = val * jax.nn.sigmoid(gate)

    full = lambda r, c: pl.BlockSpec((r, c), lambda i: (0, 0))
    return pl.pallas_call(
        body, grid=(N // tm,),
        in_specs=[pl.BlockSpec((2, tm, 256), lambda i: (0, i, 0)), pl.BlockSpec((tm, 256), lambda i: (i, 0)),
                  full(1, 256), full(256, 256), full(256, 256), full(1, 256), full(1, 256)],
        out_specs=pl.BlockSpec((tm, 256), lambda i: (i, 0)),
        out_shape=_sds((N, 256)), name="s5_glu_fwd", compiler_params=_cp(("parallel",)))(y2, h, dsk, wv, wg, bv, bg)


def _s5_glu_bwd(y2, h, dsk, wv, wg, bv, bg, dya):
    tm = 512

    def body(y2_ref, u_ref, d_ref, wv_ref, wg_ref, bv_ref, bg_ref, dya_ref,
             dyp_ref, dud_ref, dd_ref, dwv_ref, dwg_ref, dbv_ref, dbg_ref):
        i = pl.program_id(0)

        @pl.when(i == 0)
        def _():
            for r in (dd_ref, dwv_ref, dwg_ref, dbv_ref, dbg_ref):
                r[...] = jnp.zeros_like(r)

        u = u_ref[...]
        y = y2_ref[0] + y2_ref[1] + d_ref[...] * u
        z = _gelu(y)
        val = _mm(z, wv_ref[...]) + bv_ref[...]
        sig = jax.nn.sigmoid(_mm(z, wg_ref[...]) + bg_ref[...])
        dya = dya_ref[...]
        dval = dya * sig
        dgate = dya * val * sig * (1.0 - sig)
        dz = _mm_nt(dval, wv_ref[...]) + _mm_nt(dgate, wg_ref[...])
        dy = dz * _gelu_grad(y)
        dyp_ref[...] = dy
        dud_ref[...] = dy * d_ref[...]
        dd_ref[...] += jnp.sum(dy * u, axis=0, keepdims=True)
        dwv_ref[...] += _mm_tn(z, dval)
        dwg_ref[...] += _mm_tn(z, dgate)
        dbv_ref[...] += jnp.sum(dval, axis=0, keepdims=True)
        dbg_ref[...] += jnp.sum(dgate, axis=0, keepdims=True)

    full = lambda r, c: pl.BlockSpec((r, c), lambda i: (0, 0))
    row = pl.BlockSpec((tm, 256), lambda i: (i, 0))
    return pl.pallas_call(
        body, grid=(N // tm,),
        in_specs=[pl.BlockSpec((2, tm, 256), lambda i: (0, i, 0)), row,
                  full(1, 256), full(256, 256), full(256, 256), full(1, 256), full(1, 256), row],
        out_specs=[row, row, full(1, 256), full(256, 256), full(256, 256), full(1, 256), full(1, 256)],
        out_shape=[_sds((N, 256)), _sds((N, 256)), _sds((1, 256)), _sds((256, 256)), _sds((256, 256)),
                   _sds((1, 256)), _sds((1, 256))],
        name="s5_glu_bwd", compiler_params=_cp(("arbitrary",)))(y2, h, dsk, wv, wg, bv, bg, dya)


def _logsig(x):
    return jnp.minimum(x, 0.0) - jnp.log(1.0 + jnp.exp(-jnp.abs(x)))


def _gla_gate_fwd(h, wa, ba):
    tm = 512

    def body(hl_ref, wa_ref, ba_ref, la_ref):
        la_ref[...] = _logsig(_mm(hl_ref[...], wa_ref[...]) + ba_ref[...]) * (1.0 / 16.0)

    return pl.pallas_call(
        body, grid=(N // tm,),
        in_specs=[pl.BlockSpec((tm, 128), lambda i: (i, 14)), pl.BlockSpec((128, 256), lambda i: (0, 0)),
                  pl.BlockSpec((1, 256), lambda i: (0, 0))],
        out_specs=pl.BlockSpec((tm, 256), lambda i: (i, 0)),
        out_shape=_sds((N, 256)), name="gla_gate_fwd", compiler_params=_cp(("parallel",)))(h, wa, ba)


def _gla_gate_bwd(h, wa, ba, dla_f, dla_b):
    tm = 512

    def body(hl_ref, wa_ref, ba_ref, df_ref, db_ref, dhl_ref, dwa_ref, dba_ref):
        i = pl.program_id(0)

        @pl.when(i == 0)
        def _():
            dwa_ref[...] = jnp.zeros_like(dwa_ref)
            dba_ref[...] = jnp.zeros_like(dba_ref)

        hl = hl_ref[...]
        pre = _mm(hl, wa_ref[...]) + ba_ref[...]
        dpre = jnp.concatenate([df_ref[...], db_ref[...]], axis=1) * (1.0 / 16.0) * jax.nn.sigmoid(-pre)
        dhl_ref[...] = _mm_nt(dpre, wa_ref[...])
        dwa_ref[...] += _mm_tn(hl, dpre)
        dba_ref[...] += jnp.sum(dpre, axis=0, keepdims=True)

    row = pl.BlockSpec((tm, 128), lambda i: (i, 0))
    return pl.pallas_call(
        body, grid=(N // tm,),
        in_specs=[pl.BlockSpec((tm, 128), lambda i: (i, 14)), pl.BlockSpec((128, 256), lambda i: (0, 0)),
                  pl.BlockSpec((1, 256), lambda i: (0, 0)), row, row],
        out_specs=[row, pl.BlockSpec((128, 256), lambda i: (0, 0)), pl.BlockSpec((1, 256), lambda i: (0, 0))],
        out_shape=[_sds((N, 128)), _sds((128, 256)), _sds((1, 256))],
        name="gla_gate_bwd", compiler_params=_cp(("arbitrary",)))(h, wa, ba, dla_f, dla_b)


def _gla_chunk(q, k, v, la, st, rev):
    c = GLA_CHUNK
    b = _cums(la, rev)
    bl = jnp.sum(la, axis=0, keepdims=True)
    q_in = q * (32.0 ** -0.5) * jnp.exp(b)
    k_in = k * jnp.exp(-b)
    k_st = k * jnp.exp(bl - b)
    lane_k = lax.broadcasted_iota(jnp.int32, (1, 128), 1) // 32
    lane_v = lax.broadcasted_iota(jnp.int32, (1, 256), 1) // 64
    r = lax.broadcasted_iota(jnp.int32, (c, c), 0)
    cc = lax.broadcasted_iota(jnp.int32, (c, c), 1)
    keep = (cc > r) if rev else (cc <= r)
    qs = jnp.concatenate([jnp.where(lane_k == hd, q_in, 0.0) for hd in range(4)], axis=0)
    a = _dmm_nt(qs, k_in)
    a = jnp.where(jnp.concatenate([keep] * 4, axis=0), a, 0.0)
    o4 = _dmm(a, v)
    o = _dmm_nt(q_in, st)
    for hd in range(4):
        o = o + jnp.where(lane_v == hd, o4[hd * c:(hd + 1) * c], 0.0)
    bd = (lax.broadcasted_iota(jnp.int32, (256, 128), 0) // 64) == (lax.broadcasted_iota(jnp.int32, (256, 128), 1) // 32)
    st_new = jnp.exp(bl) * st + jnp.where(bd, _dmm_tn(v, k_st), 0.0)
    return o, st_new


def _gla_rows(s, c, rev):
    return s * NCHUNK + (NCHUNK - 1 - c if rev else c)


def _gla_fwd(h, la2):
    def body(qf, kf, vf, laf, qb, kb, vb, lab, of_ref, ob_ref, sf_ref, sb_ref, stf, stb):
        @pl.when(pl.program_id(1) == 0)
        def _():
            stf[...] = jnp.zeros_like(stf)
            stb[...] = jnp.zeros_like(stb)

        sf_ref[0] = stf[...]
        sb_ref[0] = stb[...]
        o, sn = _gla_chunk(qf[...], kf[...], vf[...], laf[...], stf[...], False)
        of_ref[...] = o
        stf[...] = sn
        o, sn = _gla_chunk(qb[...], kb[...], vb[...], lab[...], stb[...], True)
        ob_ref[...] = o
        stb[...] = sn

    def specs(rev):
        rw = lambda s, c: _gla_rows(s, c, rev)
        return [pl.BlockSpec((64, 128), lambda s, c: (rw(s, c), 2)), pl.BlockSpec((64, 128), lambda s, c: (rw(s, c), 3)),
                pl.BlockSpec((64, 256), lambda s, c: (rw(s, c), 2)),
                pl.BlockSpec((64, 128), lambda s, c: (rw(s, c), 1 if rev else 0))]

    orow = lambda rev: pl.BlockSpec((64, 256), lambda s, c: (_gla_rows(s, c, rev), 0))
    srow = lambda rev: pl.BlockSpec((1, 256, 128), lambda s, c: (_gla_rows(s, c, rev), 0, 0))
    return pl.pallas_call(
        body, grid=(NSEQ, NCHUNK),
        in_specs=specs(False) + specs(True),
        out_specs=[orow(False), orow(True), srow(False), srow(True)],
        out_shape=[_sds((N, 256)), _sds((N, 256)), _sds((NSEQ * NCHUNK, 256, 128)), _sds((NSEQ * NCHUNK, 256, 128))],
        scratch_shapes=[pltpu.VMEM((256, 128), F32), pltpu.VMEM((256, 128), F32)],
        name="gla_fwd", compiler_params=_cp(("arbitrary", "arbitrary")))(h, h, h, la2, h, h, h, la2)


def _gla_bwd(h, la2, do, sf, sb):
    def body(qf, kf, vf, laf, dof, sfr, qb, kb, vb, lab, dob, sbr,
             dqf, dkf, dvf, dlf, dqb, dkb, dvb, dlb, dstf, dstb):
        @pl.when(pl.program_id(1) == 0)
        def _():
            dstf[...] = jnp.zeros_like(dstf)
            dstb[...] = jnp.zeros_like(dstb)

        def one(q, k, v, la, do_, st, dst, rev, dq, dk, dv, dl):
            _, vjp = jax.vjp(functools.partial(_gla_chunk, rev=rev), q[...], k[...], v[...], la[...], st[0])
            gq, gk, gv, gl, gs = vjp((do_[...], dst[...]))
            dq[...] = gq
            dk[...] = gk
            dv[...] = gv
            dl[...] = gl
            dst[...] = gs

        one(qf, kf, vf, laf, dof, sfr, dstf, False, dqf, dkf, dvf, dlf)
        one(qb, kb, vb, lab, dob, sbr, dstb, True, dqb, dkb, dvb, dlb)

    def specs(rev):
        rw = lambda s, c: _gla_rows(s, c, not rev)
        return [pl.BlockSpec((64, 128), lambda s, c: (rw(s, c), 2)), pl.BlockSpec((64, 128), lambda s, c: (rw(s, c), 3)),
                pl.BlockSpec((64, 256), lambda s, c: (rw(s, c), 2)),
                pl.BlockSpec((64, 128), lambda s, c: (rw(s, c), 1 if rev else 0)),
                pl.BlockSpec((64, 256), lambda s, c: (rw(s, c), 0)),
                pl.BlockSpec((1, 256, 128), lambda s, c: (rw(s, c), 0, 0))]

    def ospecs(rev):
        rw = lambda s, c: _gla_rows(s, c, not rev)
        n = pl.BlockSpec((64, 128), lambda s, c: (rw(s, c), 0))
        return [n, n, pl.BlockSpec((64, 256), lambda s, c: (rw(s, c), 0)), n]

    oshape = [_sds((N, 128)), _sds((N, 128)), _sds((N, 256)), _sds((N, 128))]
    return pl.pallas_call(
        body, grid=(NSEQ, NCHUNK),
        in_specs=specs(False) + specs(True),
        out_specs=ospecs(False) + ospecs(True),
        out_shape=oshape + oshape,
        scratch_shapes=[pltpu.VMEM((256, 128), F32), pltpu.VMEM((256, 128), F32)],
        name="gla_bwd", compiler_params=_cp(("arbitrary", "arbitrary")))(h, h, h, la2, do, sf, h, h, h, la2, do, sb)


def _gla_post(of, ob, r, g):
    o = of + ob
    head = lax.broadcasted_iota(jnp.int32, (1, 256), 1) // 64
    mu = jnp.zeros_like(o)
    for hd in range(4):
        mu = mu + jnp.where(head == hd, jnp.sum(jnp.where(head == hd, o, 0.0), axis=-1, keepdims=True) * (1.0 / 64.0), 0.0)
    xc = o - mu
    var = jnp.zeros_like(o)
    for hd in range(4):
        var = var + jnp.where(head == hd, jnp.sum(jnp.where(head == hd, xc * xc, 0.0), axis=-1, keepdims=True) * (1.0 / 64.0), 0.0)
    return xc * lax.rsqrt(var + LN_EPS) * g * (r * jax.nn.sigmoid(r))


def _gla_post_fwd(of, ob, h, g):
    tm = 512

    def body(of_ref, ob_ref, r_ref, g_ref, y_ref):
        y_ref[...] = _gla_post(of_ref[...], ob_ref[...], r_ref[...], g_ref[...])

    row = pl.BlockSpec((tm, 256), lambda i: (i, 0))
    return pl.pallas_call(
        body, grid=(N // tm,),
        in_specs=[row, row, pl.BlockSpec((tm, 256), lambda i: (i, 3)), pl.BlockSpec((1, 256), lambda i: (0, 0))],
        out_specs=row, out_shape=_sds((N, 256)), name="gla_post_fwd", compiler_params=_cp(("parallel",)))(of, ob, h, g)


def _gla_post_bwd(of, ob, h, g, dyb):
    tm = 512

    def body(of_ref, ob_ref, r_ref, g_ref, dy_ref, do_ref, dr_ref, dg_ref):
        @pl.when(pl.program_id(0) == 0)
        def _():
            dg_ref[...] = jnp.zeros_like(dg_ref)

        _, vjp = jax.vjp(_gla_post, of_ref[...], ob_ref[...], r_ref[...], g_ref[...])
        go, _, gr, gg = vjp(dy_ref[...])
        do_ref[...] = go
        dr_ref[...] = gr
        dg_ref[...] += gg

    row = pl.BlockSpec((tm, 256), lambda i: (i, 0))
    one = pl.BlockSpec((1, 256), lambda i: (0, 0))
    return pl.pallas_call(
        body, grid=(N // tm,),
        in_specs=[row, row, pl.BlockSpec((tm, 256), lambda i: (i, 3)), one, row],
        out_specs=[row, row, one], out_shape=[_sds((N, 256)), _sds((N, 256)), _sds((1, 256))],
        name="gla_post_bwd", compiler_params=_cp(("arbitrary",)))(of, ob, h, g, dyb)


def _rope_tables(width):
    pos = jnp.arange(L, dtype=F32)
    inv_freq = ROPE_THETA ** (-jnp.arange(0, ROT, 2, dtype=F32) / ROT)
    ang = pos[:, None] * inv_freq[None, :]
    cos, sin = jnp.cos(ang), jnp.sin(ang)
    one = jnp.ones((L, 64 - ROT), F32)
    zero = jnp.zeros((L, 64 - ROT), F32)
    z8 = jnp.zeros((L, ROT // 2), F32)
    c = jnp.concatenate([cos, cos, one], axis=1)
    sa = jnp.concatenate([z8, sin, zero], axis=1)
    sb = jnp.concatenate([-sin, z8, zero], axis=1)
    rep = width // 64
    return jnp.stack([jnp.tile(c, (1, rep)), jnp.tile(sa, (1, rep)), jnp.tile(sb, (1, rep))])


def _rope(t, tab):
    w = t.shape[-1]
    return t * tab[0] + pltpu.roll(t, ROT // 2, 1) * tab[1] + pltpu.roll(t, w - ROT // 2, 1) * tab[2]


def _rope_t(g, tab):
    w = g.shape[-1]
    return g * tab[0] + pltpu.roll(g * tab[1], w - ROT // 2, 1) + pltpu.roll(g * tab[2], ROT // 2, 1)


def _swa_pad_kv(kv_ref, tk_ref, kpad, vpad):
    z = jnp.zeros((SWA_BLK, 128), F32)
    kpad[0:SWA_BLK] = z
    vpad[0:SWA_BLK] = z
    kpad[SWA_BLK + L:] = z
    vpad[SWA_BLK + L:] = z
    kpad[SWA_BLK:SWA_BLK + L] = _rope(kv_ref[:, 0:128], tk_ref[...])
    vpad[SWA_BLK:SWA_BLK + L] = kv_ref[:, 128:256]


def _swa_expand(x, hk):
    lane = lax.broadcasted_iota(jnp.int32, x.shape, 1)
    sw = pltpu.roll(x, 64, 1)
    pair = jnp.where(lane < 64, x, sw) if hk == 0 else jnp.where(lane < 64, sw, x)
    return jnp.concatenate([pair, pair], axis=1)


def _swa_fold(x, hk):
    a = x[:, 0:128] + x[:, 128:256]
    t = a + pltpu.roll(a, 64, 1)
    lane = lax.broadcasted_iota(jnp.int32, a.shape, 1)
    return jnp.where((lane < 64) if hk == 0 else (lane >= 64), t, 0.0)


def _swa_probs(q2, kexp, n, sink_ref, hk):
    slot = lax.broadcasted_iota(jnp.int32, (1, 256), 1) // 64
    qs = jnp.concatenate([jnp.where(slot == g, q2, 0.0) for g in range(4)], axis=0)
    s = _mm_nt(qs, kexp) * 0.125
    i = lax.broadcasted_iota(jnp.int32, (SWA_BLK, 3 * SWA_BLK), 0)
    j = lax.broadcasted_iota(jnp.int32, (SWA_BLK, 3 * SWA_BLK), 1)
    kpos = n * SWA_BLK - SWA_BLK + j
    ok = (j - i >= 0) & (j - i <= 2 * SWA_BLK) & (kpos >= 0) & (kpos < L)
    s = jnp.where(jnp.concatenate([ok] * 4, axis=0), s, NEG_BIG)
    rowg = lax.broadcasted_iota(jnp.int32, (4 * SWA_BLK, 1), 0) // SWA_BLK
    sink = jnp.zeros((4 * SWA_BLK, 1), F32)
    for g in range(4):
        sink = jnp.where(rowg == g, sink_ref[hk * 4 + g], sink)
    m = jnp.maximum(jnp.max(s, axis=-1, keepdims=True), sink)
    p = jnp.exp(s - m)
    ps = jnp.exp(sink - m)
    inv = 1.0 / (jnp.sum(p, axis=-1, keepdims=True) + ps)
    return qs, p * inv, ps * inv, slot, rowg


def _swa_fwd(h, tq, tk, sink):
    def body(sink_ref, q_ref, kv_ref, tq_ref, tk_ref, y_ref, kpad, vpad):
        n = pl.program_id(1)

        @pl.when(n == 0)
        def _():
            _swa_pad_kv(kv_ref, tk_ref, kpad, vpad)

        q = _rope(q_ref[...], tq_ref[...])
        r0 = pl.multiple_of(n * SWA_BLK, SWA_BLK)
        kb = kpad[pl.ds(r0, 3 * SWA_BLK), :]
        vb = vpad[pl.ds(r0, 3 * SWA_BLK), :]
        for hk in range(2):
            _, p, _, slot, _ = _swa_probs(q[:, hk * 256:(hk + 1) * 256], _swa_expand(kb, hk), n, sink_ref, hk)
            o4 = _mm(p, _swa_expand(vb, hk))
            o = jnp.zeros((SWA_BLK, 256), F32)
            for g in range(4):
                o = o + jnp.where(slot == g, o4[g * SWA_BLK:(g + 1) * SWA_BLK], 0.0)
            y_ref[:, hk * 256:(hk + 1) * 256] = o

    return pl.pallas_call(
        body,
        grid_spec=pltpu.PrefetchScalarGridSpec(
            num_scalar_prefetch=1, grid=(NSEQ, NBLK),
            in_specs=[pl.BlockSpec((SWA_BLK, 512), lambda s, n, sk: (s * NBLK + n, 2)),
                      pl.BlockSpec((L, 256), lambda s, n, sk: (s, 6)),
                      pl.BlockSpec((3, SWA_BLK, 512), lambda s, n, sk: (0, n, 0)),
                      pl.BlockSpec((3, L, 128), lambda s, n, sk: (0, 0, 0))],
            out_specs=pl.BlockSpec((SWA_BLK, 512), lambda s, n, sk: (s * NBLK + n, 0)),
            scratch_shapes=[pltpu.VMEM((L + 2 * SWA_BLK, 128), F32), pltpu.VMEM((L + 2 * SWA_BLK, 128), F32)]),
        out_shape=_sds((N, 512)), name="swa_fwd", compiler_params=_cp(("arbitrary", "arbitrary")))(sink, h, h, tq, tk)


def _swa_bwd(h, tq, tk, sink, dyc):
    def body(sink_ref, q_ref, kv_ref, tq_ref, tk_ref, dy_ref, dq_ref, dkv_ref, dsink_ref, kpad, vpad, dkacc, dvacc):
        sq = pl.program_id(0)
        n = pl.program_id(1)

        @pl.when(n == 0)
        def _():
            _swa_pad_kv(kv_ref, tk_ref, kpad, vpad)
            dkacc[...] = jnp.zeros_like(dkacc)
            dvacc[...] = jnp.zeros_like(dvacc)

        @pl.when((n == 0) & (sq == 0))
        def _():
            dsink_ref[...] = jnp.zeros_like(dsink_ref)

        q = _rope(q_ref[...], tq_ref[...])
        r0 = pl.multiple_of(n * SWA_BLK, SWA_BLK)
        kb = kpad[pl.ds(r0, 3 * SWA_BLK), :]
        vb = vpad[pl.ds(r0, 3 * SWA_BLK), :]
        dk = jnp.zeros((3 * SWA_BLK, 128), F32)
        dv = jnp.zeros((3 * SWA_BLK, 128), F32)
        hrow = lax.broadcasted_iota(jnp.int32, (8, 128), 0)
        dsk = jnp.zeros((8, 128), F32)
        for hk in range(2):
            kexp = _swa_expand(kb, hk)
            vexp = _swa_expand(vb, hk)
            qs, p, ps, slot, rowg = _swa_probs(q[:, hk * 256:(hk + 1) * 256], kexp, n, sink_ref, hk)
            dy2 = dy_ref[:, hk * 256:(hk + 1) * 256]
            dos = jnp.concatenate([jnp.where(slot == g, dy2, 0.0) for g in range(4)], axis=0)
            dp = _mm_nt(dos, vexp)
            delta = jnp.sum(p * dp, axis=-1, keepdims=True)
            ds = p * (dp - delta) * 0.125
            dsr = -ps * delta
            for g in range(4):
                dsk = dsk + jnp.where(hrow == hk * 4 + g, jnp.sum(jnp.where(rowg == g, dsr, 0.0), axis=0, keepdims=True), 0.0)
            dq4 = _mm(ds, kexp)
            dq2 = jnp.zeros((SWA_BLK, 256), F32)
            for g in range(4):
                dq2 = dq2 + jnp.where(slot == g, dq4[g * SWA_BLK:(g + 1) * SWA_BLK], 0.0)
            dq_ref[:, hk * 256:(hk + 1) * 256] = dq2
            dk = dk + _swa_fold(_mm_tn(ds, qs), hk)
            dv = dv + _swa_fold(_mm_tn(p, dos), hk)
        dq_ref[...] = _rope_t(dq_ref[...], tq_ref[...])
        dkacc[pl.ds(r0, 3 * SWA_BLK), :] += dk
        dvacc[pl.ds(r0, 3 * SWA_BLK), :] += dv
        dsink_ref[...] += dsk

        @pl.when(n == NBLK - 1)
        def _():
            dkv_ref[:, 0:128] = _rope_t(dkacc[SWA_BLK:SWA_BLK + L], tk_ref[...])
            dkv_ref[:, 128:256] = dvacc[SWA_BLK:SWA_BLK + L]

    blk = lambda col: pl.BlockSpec((SWA_BLK, 512), lambda s, n, sk: (s * NBLK + n, col))
    pad = pltpu.VMEM((L + 2 * SWA_BLK, 128), F32)
    return pl.pallas_call(
        body,
        grid_spec=pltpu.PrefetchScalarGridSpec(
            num_scalar_prefetch=1, grid=(NSEQ, NBLK),
            in_specs=[blk(2), pl.BlockSpec((L, 256), lambda s, n, sk: (s, 6)),
                      pl.BlockSpec((3, SWA_BLK, 512), lambda s, n, sk: (0, n, 0)),
                      pl.BlockSpec((3, L, 128), lambda s, n, sk: (0, 0, 0)), blk(0)],
            out_specs=[blk(0), pl.BlockSpec((L, 256), lambda s, n, sk: (s, 0)),
                       pl.BlockSpec((8, 128), lambda s, n, sk: (0, 0))],
            scratch_shapes=[pad, pad, pad, pad]),
        out_shape=[_sds((N, 512)), _sds((N, 256)), _sds((8, 128))],
        name="swa_bwd", compiler_params=_cp(("arbitrary", "arbitrary")))(sink, h, h, tq, tk, dyc)


def _outproj_fwd(ya, yb, yc, x, wo, g, b):
    tm = 512

    def body(ya_ref, yb_ref, yc_ref, x_ref, wo_ref, g_ref, b_ref, s_ref, x1_ref):
        mix = _mm(ya_ref[...], wo_ref[0:256]) + _mm(yb_ref[...], wo_ref[256:512]) + _mm(yc_ref[...], wo_ref[512:1024])
        s = ALPHA * x_ref[...] + mix
        s_ref[...] = s
        x1_ref[...] = _ln_fwd(s, g_ref[...], b_ref[...])

    row = lambda w_: pl.BlockSpec((tm, w_), lambda i: (i, 0))
    one = pl.BlockSpec((1, D), lambda i: (0, 0))
    return pl.pallas_call(
        body, grid=(N // tm,),
        in_specs=[row(256), row(256), row(512), row(D), pl.BlockSpec((D, D), lambda i: (0, 0)), one, one],
        out_specs=[row(D), row(D)], out_shape=[_sds((N, D)), _sds((N, D))],
        name="outproj_fwd", compiler_params=_cp(("parallel",)))(ya, yb, yc, x, wo, g, b)


def _outproj_bwd(dx1, s1, ya, yb, yc, wo, g):
    tm = 512
    nt = N // tm

    def body(dx1_ref, s_ref, ya_ref, yb_ref, yc_ref, wo_ref, g_ref,
             dya_ref, dyb_ref, dyc_ref, dxp_ref, dwo_ref, dg_ref, db_ref, acc):
        i = pl.program_id(0)

        @pl.when(i == 0)
        def _():
            acc[...] = jnp.zeros_like(acc)
            dg_ref[...] = jnp.zeros_like(dg_ref)
            db_ref[...] = jnp.zeros_like(db_ref)

        ds, dg, db = _ln_bwd(dx1_ref[...], s_ref[...], g_ref[...])
        dg_ref[...] += dg
        db_ref[...] += db
        dxp_ref[...] = ALPHA * ds
        dy = _mm_nt(ds, wo_ref[...])
        dya_ref[...] = dy[:, 0:256]
        dyb_ref[...] = dy[:, 256:512]
        dyc_ref[...] = dy[:, 512:1024]
        acc[0:256] += _mm_tn(ya_ref[...], ds)
        acc[256:512] += _mm_tn(yb_ref[...], ds)
        acc[512:1024] += _mm_tn(yc_ref[...], ds)

        @pl.when(i == nt - 1)
        def _():
            dwo_ref[...] = acc[...].astype(MX)

    row = lambda w_: pl.BlockSpec((tm, w_), lambda i: (i, 0))
    one = pl.BlockSpec((1, D), lambda i: (0, 0))
    full = pl.BlockSpec((D, D), lambda i: (0, 0))
    return pl.pallas_call(
        body, grid=(nt,),
        in_specs=[row(D), row(D), row(256), row(256), row(512), full, one],
        out_specs=[row(256), row(256), row(512), row(D), full, one, one],
        out_shape=[_sds((N, 256)), _sds((N, 256)), _sds((N, 512)), _sds((N, D)), _sds((D, D), MX), _sds((1, D)), _sds((1, D))],
        scratch_shapes=[pltpu.VMEM((D, D), F32)],
        name="outproj_bwd", compiler_params=_cp(("arbitrary",)))(dx1, s1, ya, yb, yc, wo, g)


def _ffn_fwd(x1, w1, w2, g, b):
    tm = 512

    def body(x_ref, w1_ref, w2_ref, g_ref, b_ref, a_ref, s_ref, x2_ref):
        j = pl.program_id(1)
        a = _mm(x_ref[...], w1_ref[0])
        a_ref[...] = a.astype(MX)
        hid = jnp.square(jnp.maximum(a, 0.0))
        contrib = _mm(hid, w2_ref[0])

        @pl.when(j == 0)
        def _():
            s_ref[...] = ALPHA * x_ref[...] + contrib

        @pl.when(j > 0)
        def _():
            s_ref[...] += contrib

        @pl.when(j == NSHARD - 1)
        def _():
            x2_ref[...] = _ln_fwd(s_ref[...], g_ref[...], b_ref[...])

    row = pl.BlockSpec((tm, D), lambda i, j: (i, 0))
    wj = pl.BlockSpec((1, D, D), lambda i, j: (j, 0, 0))
    one = pl.BlockSpec((1, D), lambda i, j: (0, 0))
    return pl.pallas_call(
        body, grid=(N // tm, NSHARD),
        in_specs=[row, wj, wj, one, one],
        out_specs=[pl.BlockSpec((tm, D), lambda i, j: (i, j)), row, row],
        out_shape=[_sds((N, DFF), MX), _sds((N, D)), _sds((N, D))],
        name="ffn_fwd", compiler_params=_cp(("parallel", "arbitrary")))(x1, w1, w2, g, b)


def _ffn_bwd_act(dy, s2, a, w1, w2, g):
    tm = 512

    def body(dy_ref, s_ref, a_ref, w1_ref, w2_ref, g_ref, da_ref, ds_ref, dx1_ref, dg_ref, db_ref, dsf):
        i = pl.program_id(0)
        j = pl.program_id(1)

        @pl.when((i == 0) & (j == 0))
        def _():
            dg_ref[...] = jnp.zeros_like(dg_ref)
            db_ref[...] = jnp.zeros_like(db_ref)

        @pl.when(j == 0)
        def _():
            ds, dg, db = _ln_bwd(dy_ref[...], s_ref[...], g_ref[...])
            dsf[...] = ds
            ds_ref[...] = ds.astype(MX)
            dg_ref[...] += dg
            db_ref[...] += db
            dx1_ref[...] = ALPHA * ds

        dhid = _mm_nt(dsf[...], w2_ref[0])
        da = dhid * 2.0 * jnp.maximum(a_ref[...].astype(F32), 0.0)
        da_ref[...] = da.astype(MX)
        dx1_ref[...] += _mm_nt(da, w1_ref[0])

    row = pl.BlockSpec((tm, D), lambda i, j: (i, 0))
    col = pl.BlockSpec((tm, D), lambda i, j: (i, j))
    wj = pl.BlockSpec((1, D, D), lambda i, j: (j, 0, 0))
    one = pl.BlockSpec((1, D), lambda i, j: (0, 0))
    return pl.pallas_call(
        body, grid=(N // tm, NSHARD),
        in_specs=[row, row, col, wj, wj, one],
        out_specs=[col, row, row, one, one],
        out_shape=[_sds((N, DFF), MX), _sds((N, D), MX), _sds((N, D)), _sds((1, D)), _sds((1, D))],
        scratch_shapes=[pltpu.VMEM((tm, D), F32)],
        name="ffn_bwd_act", compiler_params=_cp(("arbitrary", "arbitrary")))(dy, s2, a, w1, w2, g)


def _ffn_bwd_w(x1, da, a, ds):
    tm = 512
    nt = N // tm

    def body(x_ref, da_ref, a_ref, ds_ref, dw1_ref, dw2_ref, acc1, acc2):
        i = pl.program_id(1)

        @pl.when(i == 0)
        def _():
            acc1[...] = jnp.zeros_like(acc1)
            acc2[...] = jnp.zeros_like(acc2)

        acc1[...] += _mm_tn(x_ref[...], da_ref[...])
        hid = jnp.square(jnp.maximum(a_ref[...].astype(F32), 0.0))
        acc2[...] += _mm_tn(hid, ds_ref[...])

        @pl.when(i == nt - 1)
        def _():
            dw1_ref[0] = acc1[...].astype(MX)
            dw2_ref[0] = acc2[...].astype(MX)

    row = pl.BlockSpec((tm, D), lambda j, i: (i, 0))
    col = pl.BlockSpec((tm, D), lambda j, i: (i, j))
    wj = pl.BlockSpec((1, D, D), lambda j, i: (j, 0, 0))
    return pl.pallas_call(
        body, grid=(NSHARD, nt),
        in_specs=[row, col, col, row], out_specs=[wj, wj],
        out_shape=[_sds((NSHARD, D, D), MX), _sds((NSHARD, D, D), MX)],
        scratch_shapes=[pltpu.VMEM((D, D), F32), pltpu.VMEM((D, D), F32)],
        name="ffn_bwd_w", compiler_params=_cp(("parallel", "arbitrary")))(x1, da, a, ds)


def _loss_head(y, target):
    tm = 512

    def body(y_ref, t_ref, dy_ref, l_ref):
        @pl.when(pl.program_id(0) == 0)
        def _():
            l_ref[...] = jnp.zeros_like(l_ref)

        e = y_ref[...] - t_ref[...]
        dy_ref[...] = e * (1.0 / D)
        l_ref[...] += jnp.sum(jnp.sum(e * e, axis=1, keepdims=True), axis=0, keepdims=True) * (0.5 / D)

    row = pl.BlockSpec((tm, D), lambda i: (i, 0))
    return pl.pallas_call(
        body, grid=(N // tm,), in_specs=[row, row],
        out_specs=[row, pl.BlockSpec((8, 128), lambda i: (0, 0))],
        out_shape=[_sds((N, D)), _sds((8, 128))], name="loss_head", compiler_params=_cp(("arbitrary",)))(y, target)


def _s5_discretize(a_re, a_im, log_step, b_re, b_im):
    lam = lax.complex(a_re, a_im)
    lam_bar = jnp.exp(lam * jnp.exp(log_step))
    b_bar = ((lam_bar - 1.0) / lam)[..., None] * lax.complex(b_re, b_im)
    return jnp.real(lam_bar), jnp.imag(lam_bar), jnp.real(b_bar), jnp.imag(b_bar)


def _s5_in_blocks(b):
    e = jnp.eye(8, dtype=F32)
    return jnp.einsum('ij,zbjph->zbihjp', e, b.reshape(2, 2, 8, S5_P, S5_H)).reshape(2, 2, 128, SW)


def _s5_in_unblocks(d):
    return jnp.einsum('zbihip->zbiph', d.reshape(2, 2, 8, S5_H, 8, S5_P)).reshape(2, S5_G, S5_P, S5_H)


def _s5_out_blocks(c):
    e = jnp.eye(8, dtype=F32)
    return jnp.einsum('ij,zbjhp->zbjpih', e, c.reshape(2, 2, 8, S5_H, S5_P)).reshape(2, 2, SW, 128)


def _s5_out_unblocks(d):
    return jnp.einsum('zbipih->zbihp', d.reshape(2, 2, 8, S5_P, 8, S5_H)).reshape(2, S5_G, S5_H, S5_P)


def _gate_weight(w_a):
    z = jnp.zeros((16, 128), F32)
    top = jnp.concatenate([w_a[0], z], axis=1)
    bot = jnp.concatenate([z, w_a[1]], axis=1)
    return jnp.concatenate([top, bot, jnp.zeros((96, 256), F32)], axis=0)


def _layer_prep(p):
    lr, li, br, bi = _s5_discretize(p["s5_a_re"], p["s5_a_im"], p["s5_log_step"], p["s5_b_re"], p["s5_b_im"])
    q = dict(p)
    q["bre"] = _s5_in_blocks(br).astype(MX)
    q["bim"] = _s5_in_blocks(bi).astype(MX)
    q["cre"] = _s5_out_blocks(p["s5_c_re"]).astype(MX)
    q["cim"] = _s5_out_blocks(p["s5_c_im"]).astype(MX)
    mr, mi = lr.reshape(2, 1024), li.reshape(2, 1024)
    q["tab"] = jnp.stack([_scan_tables(mr[0], mi[0], False), _scan_tables(mr[1], mi[1], True)])
    q["tabc"] = jnp.stack([_scan_tables(mr[0], -mi[0], True), _scan_tables(mr[1], -mi[1], False)])
    q["dsk"] = p["s5_d"].reshape(1, 256)
    q["wa"] = _gate_weight(p["gla_w_a"]).astype(MX)
    q["ba"] = p["gla_b_a"].reshape(1, 256)
    q["lng"] = p["gla_ln_g"].reshape(1, 256)
    q["bv"] = p["s5_b_glu"][:256].reshape(1, 256)
    q["bg"] = p["s5_b_glu"][256:].reshape(1, 256)
    for k in ("ln1_g", "ln1_b", "ln2_g", "ln2_b"):
        q[k] = p[k].reshape(1, D)
    return q


def _layer_fwd(x, q, tq, tk):
    h = _inproj_fwd(x, q["w_in"])
    hre, him, y2 = _s5_fwd(h, q["bre"], q["bim"], q["cre"], q["cim"], q["tab"])
    ya = _s5_glu_fwd(y2, h, q["dsk"], q["wv"], q["wg"], q["bv"], q["bg"])
    la2 = _gla_gate_fwd(h, q["wa"], q["ba"])
    of, ob, sf, sb = _gla_fwd(h, la2)
    yb = _gla_post_fwd(of, ob, h, q["lng"])
    yc = _swa_fwd(h, tq, tk, q["swa_sink"])
    s1, x1 = _outproj_fwd(ya, yb, yc, x, q["w_out"], q["ln1_g"], q["ln1_b"])
    a, s2, x2 = _ffn_fwd(x1, q["w_ff1"], q["w_ff2"], q["ln2_g"], q["ln2_b"])
    saved = dict(x=x, h=h, hre=hre, him=him, y2=y2, ya=ya, la2=la2, of=of, ob=ob, sf=sf, sb=sb, yb=yb, yc=yc,
                 s1=s1, x1=x1, a=a, s2=s2)
    return x2, saved


def _layer_bwd(dy, q, sv, tq, tk):
    g = {}
    da, ds2, dx1, g["ln2_g"], g["ln2_b"] = _ffn_bwd_act(dy, sv["s2"], sv["a"], q["w_ff1"], q["w_ff2"], q["ln2_g"])
    dw1, dw2 = _ffn_bwd_w(sv["x1"], da, sv["a"], ds2)
    dya, dyb, dyc, dxp, dwo, g["ln1_g"], g["ln1_b"] = _outproj_bwd(dx1, sv["s1"], sv["ya"], sv["yb"], sv["yc"],
                                                                     q["w_out"], q["ln1_g"])
    h = sv["h"]
    daq, dakv, dsink = _swa_bwd(h, tq, tk, q["swa_sink"], dyc)
    g["swa_sink"] = dsink[:, 0]
    do, gr, dlng = _gla_post_bwd(sv["of"], sv["ob"], h, q["lng"], dyb)
    g["gla_ln_g"] = dlng.reshape(256)
    gq_f, gk_f, gv_f, gl_f, gq_b, gk_b, gv_b, gl_b = _gla_bwd(h, sv["la2"], do, sv["sf"], sv["sb"])
    dhl, dwa, dba = _gla_gate_bwd(h, q["wa"], q["ba"], gl_f, gl_b)
    g["gla_w_a"] = jnp.stack([dwa[0:16, 0:128], dwa[16:32, 128:256]])
    g["gla_b_a"] = dba.reshape(2, 128)
    dyp, dud, dd, dwv, dwg, dbv, dbg = _s5_glu_bwd(sv["y2"], h, q["dsk"], q["wv"], q["wg"], q["bv"], q["bg"], dya)
    g["s5_d"] = dd.reshape(S5_G, S5_H)
    g["s5_b_glu"] = jnp.concatenate([dbv, dbg], axis=1).reshape(512)
    du2, dbre, dbim, dcre, dcim, dmu = _s5_bwd(h, dyp, sv["hre"], sv["him"], q["bre"], q["bim"], q["cre"], q["cim"],
                                               q["tabc"])
    g["s5_c_re"] = _s5_out_unblocks(dcre)
    g["s5_c_im"] = _s5_out_unblocks(dcim)
    dmu = jnp.sum(dmu, axis=3)
    dlr = dmu[:, :, 0].reshape(2, S5_G, S5_P)
    dli = dmu[:, :, 1].reshape(2, S5_G, S5_P)
    _, vjp = jax.vjp(_s5_discretize, q["s5_a_re"], q["s5_a_im"], q["s5_log_step"], q["s5_b_re"], q["s5_b_im"])
    (g["s5_a_re"], g["s5_a_im"], g["s5_log_step"], g["s5_b_re"], g["s5_b_im"]) = vjp(
        (dlr, dli, _s5_in_unblocks(dbre), _s5_in_unblocks(dbim)))
    dx, dwin = _inproj_bwd(sv["x"], q["w_in"], dxp, du2, dud, gq_f, gq_b, gk_f, gk_b, gv_f, gv_b, gr, daq, dakv, dhl)
    big = dict(w_in=dwin, s5_w_glu=(dwv, dwg), w_out=dwo, w_ff1=dw1, w_ff2=dw2)
    return dx, big, g


def _local_step(x, target, qs, tq, tk):
    saved = []
    for q in qs:
        x, sv = _layer_fwd(x, q, tq, tk)
        saved.append(sv)
    dy, lacc = _loss_head(x, target)
    bigs, smalls = [None] * DEPTH, [None] * DEPTH
    for l in reversed(range(DEPTH)):
        dy, bigs[l], smalls[l] = _layer_bwd(dy, qs[l], saved[l], tq, tk)
    return lacc[0, 0], dy, bigs, smalls


BIG = ("w_in", "s5_w_glu", "w_out", "w_ff1", "w_ff2")
SMALL = ("s5_a_re", "s5_a_im", "s5_log_step", "s5_b_re", "s5_b_im", "s5_c_re", "s5_c_im", "s5_d", "s5_b_glu",
         "gla_w_a", "gla_b_a", "gla_ln_g", "swa_sink", "ln1_g", "ln1_b", "ln2_g", "ln2_b")
ANY = pl.BlockSpec(memory_space=pl.ANY)


def _place():
    x, y, c = lax.axis_index("x"), lax.axis_index("y"), lax.axis_index("c")
    return x, y, c, [(1 - x, y), (x, 1 - y), (1 - x, 1 - y)]


def _gather_weights(shards):
    n = len(shards)

    def body(*refs):
        ins, outs = refs[:n], refs[n:2 * n]
        send, recv, loc = refs[2 * n:]
        x, y, c, chips = _place()
        me = 2 * x + y
        local = [pltpu.make_async_copy(ins[a], outs[a].at[me], loc.at[a]) for a in range(n)]
        for cp in local:
            cp.start()

        def remote(a, j, slot):
            px, py = chips[j]
            return pltpu.make_async_remote_copy(src_ref=ins[a], dst_ref=outs[a].at[slot], send_sem=send.at[3 * a + j],
                                                recv_sem=recv.at[3 * a + j], device_id=(px, py, c), device_id_type=MESH)

        sends = [remote(a, j, me) for a in range(n) for j in range(3)]
        for cp in sends:
            cp.start()
        for a in range(n):
            for j in range(3):
                remote(a, j, 2 * chips[j][0] + chips[j][1]).wait_recv()
        for cp in sends:
            cp.wait_send()
        for cp in local:
            cp.wait()

    return pl.pallas_call(
        body, in_specs=[ANY] * n, out_specs=[ANY] * n,
        out_shape=[_sds((NSHARD,) + s.shape, s.dtype) for s in shards],
        scratch_shapes=[pltpu.SemaphoreType.DMA((3 * n,)), pltpu.SemaphoreType.DMA((3 * n,)), pltpu.SemaphoreType.DMA((n,))],
        name="gather_weights")(*shards)


def _scatter_grads(parts):
    na = len(parts[0])
    n = DEPTH * na

    def body(*refs):
        ins, outs = refs[:n], refs[n:n + na]
        send, recv, loc = refs[n + na:]
        x, y, c, chips = _place()
        me = 2 * x + y
        local = [pltpu.make_async_copy(ins[l * na + a].at[me], outs[a].at[me, l], loc.at[l * na + a])
                 for l in range(DEPTH) for a in range(na)]
        for cp in local:
            cp.start()

        def remote(l, a, j, src_slot, dst_slot):
            px, py = chips[j]
            k = 3 * (l * na + a) + j
            return pltpu.make_async_remote_copy(src_ref=ins[l * na + a].at[src_slot], dst_ref=outs[a].at[dst_slot, l],
                                                send_sem=send.at[k], recv_sem=recv.at[k], device_id=(px, py, c),
                                                device_id_type=MESH)

        sends = [remote(l, a, j, 2 * chips[j][0] + chips[j][1], me)
                 for l in range(DEPTH) for a in range(na) for j in range(3)]
        for cp in sends:
            cp.start()
        for l in range(DEPTH):
            for a in range(na):
                for j in range(3):
                    remote(l, a, j, me, 2 * chips[j][0] + chips[j][1]).wait_recv()
        for cp in sends:
            cp.wait_send()
        for cp in local:
            cp.wait()

    flat = [p for layer in parts for p in layer]
    return pl.pallas_call(
        body, in_specs=[ANY] * n, out_specs=[ANY] * na,
        out_shape=[_sds((NSHARD, DEPTH) + p.shape[1:], p.dtype) for p in parts[0]],
        scratch_shapes=[pltpu.SemaphoreType.DMA((3 * n,)), pltpu.SemaphoreType.DMA((3 * n,)), pltpu.SemaphoreType.DMA((n,))],
        name="scatter_grads")(*flat)


def _sum_sources(r):
    _, rows, cols = r.shape
    tr = 256

    def body(r_ref, o_ref):
        o_ref[...] = ((r_ref[0].astype(F32) + r_ref[1].astype(F32)) + r_ref[2].astype(F32)) + r_ref[3].astype(F32)

    return pl.pallas_call(
        body, grid=(rows // tr,), in_specs=[pl.BlockSpec((NSHARD, tr, cols), lambda i: (0, i, 0))],
        out_specs=pl.BlockSpec((tr, cols), lambda i: (i, 0)), out_shape=_sds((rows, cols)),
        name="sum_sources", compiler_params=_cp(("parallel",)))(r)


def _swap_sibling(arrs):
    n = len(arrs)

    def body(*refs):
        ins, outs = refs[:n], refs[n:2 * n]
        send, recv = refs[2 * n:]
        x, y, c, _ = _place()
        cps = [pltpu.make_async_remote_copy(src_ref=ins[a], dst_ref=outs[a], send_sem=send.at[a], recv_sem=recv.at[a],
                                            device_id=(x, y, 1 - c), device_id_type=MESH) for a in range(n)]
        for cp in cps:
            cp.start()
        for cp in cps:
            cp.wait()

    return pl.pallas_call(
        body, in_specs=[ANY] * n, out_specs=[ANY] * n, out_shape=[_sds(a.shape, a.dtype) for a in arrs],
        scratch_shapes=[pltpu.SemaphoreType.DMA((n,)), pltpu.SemaphoreType.DMA((n,))],
        name="swap_sibling")(*arrs)


def _allreduce_small(v):
    rows = v.shape[0]

    def body(v_ref, o_ref, sib, slots, send, recv):
        x, y, c, chips = _place()
        me = 2 * x + y
        d2d = pltpu.make_async_remote_copy(src_ref=v_ref, dst_ref=sib, send_sem=send.at[0], recv_sem=recv.at[0],
                                           device_id=(x, y, 1 - c), device_id_type=MESH)
        d2d.start()
        d2d.wait()
        slots[me] = v_ref[...] + sib[...]

        def remote(j, slot):
            px, py = chips[j]
            return pltpu.make_async_remote_copy(src_ref=slots.at[me], dst_ref=slots.at[slot], send_sem=send.at[1 + j],
                                                recv_sem=recv.at[1 + j], device_id=(px, py, c), device_id_type=MESH)

        sends = [remote(j, me) for j in range(3)]
        for cp in sends:
            cp.start()
        for j in range(3):
            remote(j, 2 * chips[j][0] + chips[j][1]).wait_recv()
        for cp in sends:
            cp.wait_send()
        o_ref[...] = ((slots[0] + slots[1]) + slots[2]) + slots[3]

    vm = pl.BlockSpec(memory_space=pltpu.VMEM)
    return pl.pallas_call(
        body, in_specs=[vm], out_specs=vm, out_shape=_sds((rows, 128)),
        scratch_shapes=[pltpu.VMEM((rows, 128), F32), pltpu.VMEM((NSHARD, rows, 128), F32),
                        pltpu.SemaphoreType.DMA((4,)), pltpu.SemaphoreType.DMA((4,))],
        name="allreduce_small", compiler_params=pltpu.CompilerParams(vmem_limit_bytes=VMEM_LIMIT))(v)


def _adamw_math(w, g, m, v):
    m = ADAM_B1 * m + (1.0 - ADAM_B1) * g
    v = ADAM_B2 * v + (1.0 - ADAM_B2) * jnp.square(g)
    m_hat = m / (1.0 - ADAM_B1 ** ADAM_STEP)
    v_hat = v / (1.0 - ADAM_B2 ** ADAM_STEP)
    delta = -ADAM_LR * (m_hat / (jnp.sqrt(v_hat) + ADAM_EPS) + ADAM_WD * w)
    return delta, m, v


def _adamw(g_parts, w, m, v):
    rows, cols = w.shape
    tr = 256 if rows % 256 == 0 else rows
    k = len(g_parts)

    def body(*refs):
        g = refs[0][...]
        for r in refs[1:k]:
            g = g + r[...]
        w_ref, m_ref, v_ref, go, do, mo, vo = refs[k:]
        d, mn, vn = _adamw_math(w_ref[...], g, m_ref[...], v_ref[...])
        go[...] = g
        do[...] = d
        mo[...] = mn
        vo[...] = vn

    spec = pl.BlockSpec((tr, cols), lambda i: (i, 0))
    return pl.pallas_call(
        body, grid=(rows // tr,), in_specs=[spec] * (k + 3), out_specs=[spec] * 4,
        out_shape=[_sds((rows, cols))] * 4, name="adamw", compiler_params=_cp(("parallel",)))(*g_parts, w, m, v)


def _pack(arrs, rows):
    flat = jnp.concatenate([a.reshape(-1) for a in arrs])
    return jnp.pad(flat, (0, rows * 128 - flat.shape[0])).reshape(rows, 128)


def _unpack(p, like):
    flat = p.reshape(-1)
    out, o = [], 0
    for a in like:
        out.append(flat[o:o + a.size].reshape(a.shape))
        o += a.size
    return out


_ARGS = ("x", "w_in", "s5_a_re", "s5_a_im", "s5_log_step", "s5_b_re", "s5_b_im", "s5_c_re", "s5_c_im", "s5_d",
         "s5_w_glu", "s5_b_glu", "gla_w_a", "gla_b_a", "gla_ln_g", "swa_sink", "w_out", "ln1_g", "ln1_b", "w_ff1",
         "w_ff2", "ln2_g", "ln2_b")
_WEIGHTS = _ARGS[1:]


def _in_cols(w):
    return jnp.concatenate([w[:, 0:1024], w[:, 1056:DIN], w[:, 1024:1056], jnp.zeros((D, DINP - DIN), w.dtype)], axis=1)


def _in_cols_back(d):
    return jnp.concatenate([d[:, 0:1024], d[:, 1792:1824], d[:, 1024:1792]], axis=1)


def _unshard_cols(g):
    return g.transpose(1, 0, 2).reshape(g.shape[1], NSHARD * g.shape[2])


def _shard_cols(d):
    return d.reshape(d.shape[0], NSHARD, d.shape[1] // NSHARD).transpose(1, 0, 2)


def kernel(x, w_in, s5_a_re, s5_a_im, s5_log_step, s5_b_re, s5_b_im, s5_c_re, s5_c_im, s5_d, s5_w_glu, s5_b_glu, gla_w_a, gla_b_a, gla_ln_g, swa_sink, w_out, ln1_g, ln1_b, w_ff1, w_ff2, ln2_g, ln2_b, loss_target, m_w_in, m_s5_a_re, m_s5_a_im, m_s5_log_step, m_s5_b_re, m_s5_b_im, m_s5_c_re, m_s5_c_im, m_s5_d, m_s5_w_glu, m_s5_b_glu, m_gla_w_a, m_gla_b_a, m_gla_ln_g, m_swa_sink, m_w_out, m_ln1_g, m_ln1_b, m_w_ff1, m_w_ff2, m_ln2_g, m_ln2_b, v_w_in, v_s5_a_re, v_s5_a_im, v_s5_log_step, v_s5_b_re, v_s5_b_im, v_s5_c_re, v_s5_c_im, v_s5_d, v_s5_w_glu, v_s5_b_glu, v_gla_w_a, v_gla_b_a, v_gla_ln_g, v_swa_sink, v_w_out, v_ln1_g, v_ln1_b, v_w_ff1, v_w_ff2, v_ln2_g, v_ln2_b):
    given = dict(locals())
    w = {k: given[k] for k in _WEIGHTS}
    mom = {k: given["m_" + k] for k in _WEIGHTS}
    var = {k: given["v_" + k] for k in _WEIGHTS}

    gathered = dict(zip(BIG, _gather_weights([w[k].astype(MX) for k in BIG])))
    qs = []
    for l in range(DEPTH):
        p = {k: w[k][l] for k in SMALL}
        glu = _unshard_cols(gathered["s5_w_glu"][:, l])
        p["wv"], p["wg"] = glu[:, :256], glu[:, 256:]
        p["w_in"] = _in_cols(_unshard_cols(gathered["w_in"][:, l]))
        p["w_out"] = gathered["w_out"][:, l].reshape(D, D)
        p["w_ff1"] = gathered["w_ff1"][:, l]
        p["w_ff2"] = gathered["w_ff2"][:, l]
        qs.append(_layer_prep(p))
    tq, tk = _rope_tables(512), _rope_tables(128)

    loss, dx, bigs, smalls = _local_step(x.reshape(N, D), loss_target.reshape(N, D), qs, tq, tk)
    loss = lax.psum(loss, ("x", "y", "c"))

    parts = []
    for l in range(DEPTH):
        b = bigs[l]
        parts.append([
            _shard_cols(_in_cols_back(b["w_in"])).astype(MX),
            _shard_cols(jnp.concatenate(b["s5_w_glu"], axis=1)).astype(MX),
            b["w_out"].reshape(NSHARD, D // NSHARD, D), b["w_ff1"], b["w_ff2"]])
    recv = _scatter_grads(parts)
    sums = [_sum_sources(r.reshape(NSHARD, -1, r.shape[-1])) for r in recv]
    others = _swap_sibling(sums)

    out = {}
    for k, own, other in zip(BIG, sums, others):
        shp = w[k].shape
        res = _adamw([own, other], *(t[k].reshape(-1, shp[-1]) for t in (w, mom, var)))
        out[k] = [r.reshape(shp) for r in res]

    rows = 2304
    gsmall = _allreduce_small(_pack([jnp.stack([smalls[l][k] for l in range(DEPTH)]) for k in SMALL], rows))
    res = _adamw([gsmall], *(_pack([t[k] for k in SMALL], rows) for t in (w, mom, var)))
    for k, vals in zip(SMALL, zip(*(_unpack(r, [w[k] for k in SMALL]) for r in res))):
        out[k] = list(vals)

    return (loss, dx.reshape(NSEQ, L, D), *[out[k][0] for k in _WEIGHTS], *[out[k][1] for k in _WEIGHTS],
            *[out[k][2] for k in _WEIGHTS], *[out[k][3] for k in _WEIGHTS])
```

```python
import functools
import math

import jax
import jax.numpy as jnp
from jax import lax
from jax.experimental import pallas as pl
from jax.experimental.pallas import tpu as pltpu

F32 = jnp.float32
MX = jnp.bfloat16
MESH = pl.DeviceIdType.MESH

DEPTH = 2
NSEQ = 2
L = 2048
N = NSEQ * L
D = 1024
DFF = 4096
NSHARD = 4
S5_G, S5_H, S5_P = 16, 16, 64
GLA_CHUNK = 64
NCHUNK = L // GLA_CHUNK
SWA_BLK = 128
NBLK = L // SWA_BLK
ROT = 16
ROPE_THETA = 500000.0
LN_EPS = 1e-5
ALPHA = (2 * DEPTH) ** 0.25
NEG_BIG = -1e30
DIN = 1824
DINP = 1920
ADAM_LR, ADAM_B1, ADAM_B2, ADAM_EPS, ADAM_WD, ADAM_STEP = 0.001, 0.9, 0.999, 1e-08, 0.01, 10
VMEM_LIMIT = 56 * 1024 * 1024
TT = 512
SW = 512


def _cp(sem, vmem=VMEM_LIMIT):
    return pltpu.CompilerParams(dimension_semantics=sem, vmem_limit_bytes=vmem)


def _mm(a, b):
    return jnp.dot(a.astype(MX), b.astype(MX), preferred_element_type=F32)


def _mm_nt(a, b):
    return lax.dot_general(a.astype(MX), b.astype(MX), (((1,), (1,)), ((), ())), preferred_element_type=F32)


def _mm_tn(a, b):
    return lax.dot_general(a.astype(MX), b.astype(MX), (((0,), (0,)), ((), ())), preferred_element_type=F32)


@jax.custom_vjp
def _dmm(a, b):
    return _mm(a, b)


_dmm.defvjp(lambda a, b: (_mm(a, b), (a, b)), lambda r, g: (_mm_nt(g, r[1]), _mm_tn(r[0], g)))


@jax.custom_vjp
def _dmm_nt(a, b):
    return _mm_nt(a, b)


_dmm_nt.defvjp(lambda a, b: (_mm_nt(a, b), (a, b)), lambda r, g: (_mm(g, r[1]), _mm_tn(g, r[0])))


@jax.custom_vjp
def _dmm_tn(a, b):
    return _mm_tn(a, b)


_dmm_tn.defvjp(lambda a, b: (_mm_tn(a, b), (a, b)), lambda r, g: (_mm_nt(r[1], g), _mm(r[0], g)))


def _split3(x):
    hi = x.astype(MX)
    r1 = x - hi.astype(F32)
    mid = r1.astype(MX)
    lo = (r1 - mid.astype(F32)).astype(MX)
    return hi, mid, lo


def _tri(rev):
    r = lax.broadcasted_iota(jnp.int32, (GLA_CHUNK, GLA_CHUNK), 0)
    c = lax.broadcasted_iota(jnp.int32, (GLA_CHUNK, GLA_CHUNK), 1)
    return jnp.where((c >= r) if rev else (c <= r), 1.0, 0.0).astype(MX)


def _cums_impl(x, rev):
    t = _tri(rev)
    return sum(jnp.dot(t, p, preferred_element_type=F32) for p in _split3(x))


@functools.partial(jax.custom_vjp, nondiff_argnums=(1,))
def _cums(x, rev):
    return _cums_impl(x, rev)


_cums.defvjp(lambda x, rev: (_cums_impl(x, rev), None), lambda rev, r, g: (_cums_impl(g, not rev),))


def _ln_fwd(s, g, b):
    mu = jnp.mean(s, axis=-1, keepdims=True)
    xc = s - mu
    var = jnp.mean(xc * xc, axis=-1, keepdims=True)
    return xc * lax.rsqrt(var + LN_EPS) * g + b


def _ln_bwd(dy, s, g):
    mu = jnp.mean(s, axis=-1, keepdims=True)
    xc = s - mu
    var = jnp.mean(xc * xc, axis=-1, keepdims=True)
    rstd = lax.rsqrt(var + LN_EPS)
    xhat = xc * rstd
    dxh = dy * g
    ds = rstd * (dxh - jnp.mean(dxh, axis=-1, keepdims=True) - xhat * jnp.mean(dxh * xhat, axis=-1, keepdims=True))
    return ds, jnp.sum(dy * xhat, axis=0, keepdims=True), jnp.sum(dy, axis=0, keepdims=True)


def _sds(shape, dtype=F32):
    return jax.ShapeDtypeStruct(shape, dtype)


def _inproj_fwd(x, w):
    tm = 512

    def body(x_ref, w_ref, h_ref):
        h_ref[...] = _mm(x_ref[...], w_ref[...])

    return pl.pallas_call(
        body, grid=(N // tm,),
        in_specs=[pl.BlockSpec((tm, D), lambda i: (i, 0)), pl.BlockSpec((D, DINP), lambda i: (0, 0))],
        out_specs=pl.BlockSpec((tm, DINP), lambda i: (i, 0)),
        out_shape=_sds((N, DINP)), name="inproj_fwd", compiler_params=_cp(("parallel",)))(x, w)


def _inproj_bwd(x, w, dxp, du2, dud, gq_f, gq_b, gk_f, gk_b, gv_f, gv_b, gr, daq, dakv, dhl):
    tm = 256
    nt = N // tm

    def body(x_ref, w_ref, dxp_ref, du2_ref, dud_ref, gqf, gqb, gkf, gkb, gvf, gvb, gr_ref, daq_ref, dakv_ref, dhl_ref,
             dx_ref, dw_ref):
        i = pl.program_id(0)
        dh = jnp.concatenate([
            du2_ref[0] + du2_ref[1] + dud_ref[...], gqf[...] + gqb[...], gkf[...] + gkb[...], gvf[...] + gvb[...],
            gr_ref[...], daq_ref[...], dakv_ref[...], dhl_ref[...]], axis=1)
        dx_ref[...] = dxp_ref[...] + _mm_nt(dh, w_ref[...])
        contrib = _mm_tn(x_ref[...], dh)

        @pl.when(i == 0)
        def _():
            dw_ref[...] = contrib

        @pl.when(i > 0)
        def _():
            dw_ref[...] += contrib

    row = lambda w_: pl.BlockSpec((tm, w_), lambda i: (i, 0))
    return pl.pallas_call(
        body, grid=(nt,),
        in_specs=[row(D), pl.BlockSpec((D, DINP), lambda i: (0, 0)), row(D),
                  pl.BlockSpec((2, tm, 256), lambda i: (0, i, 0)), row(256), row(128), row(128), row(128), row(128),
                  row(256), row(256), row(256), row(512), row(256), row(128)],
        out_specs=[row(D), pl.BlockSpec((D, DINP), lambda i: (0, 0))],
        out_shape=[_sds((N, D)), _sds((D, DINP))],
        name="inproj_bwd", compiler_params=_cp(("arbitrary",)))(
            x, w, dxp, du2, dud, gq_f, gq_b, gk_f, gk_b, gv_f, gv_b, gr, daq, dakv, dhl)


def _scan_tables(mr, mi, reverse):
    pw = [(mr, mi)]
    for _ in range(7):
        pr, pi = pw[-1]
        pw.append((pr * mr - pi * mi, pr * mi + pi * mr))
    rows = jnp.arange(8)[:, None]
    out = []
    for d in (1, 2, 4):
        keep = rows >= d
        out += [jnp.where(keep, pw[d - 1][0][None], 0.0), jnp.where(keep, pw[d - 1][1][None], 0.0)]
    out += [jnp.stack([p[0] for p in pw]), jnp.stack([p[1] for p in pw])]
    t = jnp.stack(out)
    if reverse:
        t = t[:, ::-1, :]
    return t.reshape(8, 8, 2, SW).transpose(2, 0, 1, 3)


def _tile_scan(xr, xi, a, cr, ci, reverse):
    for lvl, d in enumerate((1, 2, 4)):
        sh = 8 - d if reverse else d
        sr = pltpu.roll(xr, sh, 0)
        si = pltpu.roll(xi, sh, 0)
        ar, ai = a[2 * lvl], a[2 * lvl + 1]
        xr, xi = xr + ar * sr - ai * si, xi + ar * si + ai * sr
    pr, pi = a[6], a[7]
    return xr + pr * cr - pi * ci, xi + pr * ci + pi * cr


def _s5_time_block(z, s, t, adjoint):
    flip = (1 - z) if adjoint else z
    return s * (L // TT) + t + flip * (L // TT - 1 - 2 * t)


def _s5_fwd(h, bre, bim, cre, cim, tab):
    nt = L // TT

    def body(u_ref, bre_ref, bim_ref, cre_ref, cim_ref, tab_ref, hre_ref, him_ref, y_ref, car):
        z = pl.program_id(1)
        tc = pl.program_id(3)

        @pl.when(tc == 0)
        def _():
            car[...] = jnp.zeros_like(car)

        u = u_ref[...]
        hre_ref[0] = _mm(u, bre_ref[0, 0])
        him_ref[0] = _mm(u, bim_ref[0, 0])

        def run(reverse):
            a = [tab_ref[0, 0, k] for k in range(8)]

            def step(i, carry):
                cr, ci = carry
                r0 = pl.multiple_of((TT // 8 - 1 - i if reverse else i) * 8, 8)
                xr, xi = _tile_scan(hre_ref[0, pl.ds(r0, 8), :], him_ref[0, pl.ds(r0, 8), :], a, cr, ci, reverse)
                hre_ref[0, pl.ds(r0, 8), :] = xr
                him_ref[0, pl.ds(r0, 8), :] = xi
                row = 0 if reverse else 7
                return (jnp.broadcast_to(xr[row:row + 1, :], (8, SW)), jnp.broadcast_to(xi[row:row + 1, :], (8, SW)))

            cr, ci = lax.fori_loop(0, TT // 8, step, (car[0], car[1]))
            car[0] = cr
            car[1] = ci

        @pl.when(z == 0)
        def _():
            run(False)

        @pl.when(z == 1)
        def _():
            run(True)

        y_ref[0] = _mm(hre_ref[0], cre_ref[0, 0]) - _mm(him_ref[0], cim_ref[0, 0])

    tb = lambda b, z, s, t: _s5_time_block(z, s, t, False)
    wspec = lambda r, c: pl.BlockSpec((1, 1, r, c), lambda b, z, s, t: (z, b, 0, 0))
    return pl.pallas_call(
        body, grid=(2, 2, NSEQ, nt),
        in_specs=[pl.BlockSpec((TT, 128), lambda b, z, s, t: (tb(b, z, s, t), b)),
                  wspec(128, SW), wspec(128, SW), wspec(SW, 128), wspec(SW, 128),
                  pl.BlockSpec((1, 1, 8, 8, SW), lambda b, z, s, t: (z, b, 0, 0, 0))],
        out_specs=[pl.BlockSpec((1, TT, SW), lambda b, z, s, t: (z, tb(b, z, s, t), b)),
                   pl.BlockSpec((1, TT, SW), lambda b, z, s, t: (z, tb(b, z, s, t), b)),
                   pl.BlockSpec((1, TT, 128), lambda b, z, s, t: (z, tb(b, z, s, t), b))],
        out_shape=[_sds((2, N, 2 * SW)), _sds((2, N, 2 * SW)), _sds((2, N, 256))],
        scratch_shapes=[pltpu.VMEM((2, 8, SW), F32)],
        name="s5_fwd", compiler_params=_cp(("arbitrary",) * 4))(h, bre, bim, cre, cim, tab)


def _s5_bwd(h, dyp, hre, him, bre, bim, cre, cim, tabc):
    nt = L // TT

    def body(u_ref, dy_ref, hre_ref, him_ref, bre_ref, bim_ref, cre_ref, cim_ref, tab_ref,
             du_ref, dbre_ref, dbim_ref, dcre_ref, dcim_ref, dmu_ref, gre, gim, car):
        z = pl.program_id(1)
        s = pl.program_id(2)
        tc = pl.program_id(3)

        @pl.when(tc == 0)
        def _():
            car[...] = jnp.zeros_like(car)

        @pl.when((tc == 0) & (s == 0))
        def _():
            dbre_ref[...] = jnp.zeros_like(dbre_ref)
            dbim_ref[...] = jnp.zeros_like(dbim_ref)
            dcre_ref[...] = jnp.zeros_like(dcre_ref)
            dcim_ref[...] = jnp.zeros_like(dcim_ref)
            dmu_ref[...] = jnp.zeros_like(dmu_ref)

        dy = dy_ref[...]
        gre[...] = _mm_nt(dy, cre_ref[0, 0])
        gim[...] = -_mm_nt(dy, cim_ref[0, 0])
        rowid = lax.broadcasted_iota(jnp.int32, (8, SW), 0)

        def run(reverse):
            a = [tab_ref[0, 0, k] for k in range(8)]
            first = 7 if reverse else 0

            def step(i, carry):
                cr, ci, dmr, dmi = carry
                r0 = pl.multiple_of((TT // 8 - 1 - i if reverse else i) * 8, 8)
                xr, xi = _tile_scan(gre[pl.ds(r0, 8), :], gim[pl.ds(r0, 8), :], a, cr, ci, reverse)
                gre[pl.ds(r0, 8), :] = xr
                gim[pl.ds(r0, 8), :] = xi
                sh = 7 if reverse else 1
                gpr = jnp.where(rowid == first, cr, pltpu.roll(xr, sh, 0))
                gpi = jnp.where(rowid == first, ci, pltpu.roll(xi, sh, 0))
                hr = hre_ref[0, pl.ds(r0, 8), :]
                hi = him_ref[0, pl.ds(r0, 8), :]
                dmr = dmr + gpr * hr + gpi * hi
                dmi = dmi + gpi * hr - gpr * hi
                row = 0 if reverse else 7
                return (jnp.broadcast_to(xr[row:row + 1, :], (8, SW)), jnp.broadcast_to(xi[row:row + 1, :], (8, SW)),
                        dmr, dmi)

            cr, ci, dmr, dmi = lax.fori_loop(0, TT // 8, step, (car[0], car[1], dmu_ref[0, 0, 0], dmu_ref[0, 0, 1]))
            car[0] = cr
            car[1] = ci
            dmu_ref[0, 0, 0] = dmr
            dmu_ref[0, 0, 1] = dmi

        @pl.when(z == 0)
        def _():
            run(True)

        @pl.when(z == 1)
        def _():
            run(False)

        gr = gre[...]
        gi = gim[...]
        u = u_ref[...]
        du_ref[0] = _mm_nt(gr, bre_ref[0, 0]) + _mm_nt(gi, bim_ref[0, 0])
        dbre_ref[0, 0] += _mm_tn(u, gr)
        dbim_ref[0, 0] += _mm_tn(u, gi)
        dcre_ref[0, 0] += _mm_tn(hre_ref[0], dy)
        dcim_ref[0, 0] -= _mm_tn(him_ref[0], dy)

    tb = lambda b, z, s, t: _s5_time_block(z, s, t, True)
    wspec = lambda r, c: pl.BlockSpec((1, 1, r, c), lambda b, z, s, t: (z, b, 0, 0))
    tok = lambda w_: pl.BlockSpec((TT, w_), lambda b, z, s, t: (tb(b, z, s, t), b))
    st = pl.BlockSpec((1, TT, SW), lambda b, z, s, t: (z, tb(b, z, s, t), b))
    return pl.pallas_call(
        body, grid=(2, 2, NSEQ, nt),
        in_specs=[tok(128), tok(128), st, st, wspec(128, SW), wspec(128, SW), wspec(SW, 128), wspec(SW, 128),
                  pl.BlockSpec((1, 1, 8, 8, SW), lambda b, z, s, t: (z, b, 0, 0, 0))],
        out_specs=[pl.BlockSpec((1, TT, 128), lambda b, z, s, t: (z, tb(b, z, s, t), b)),
                   wspec(128, SW), wspec(128, SW), wspec(SW, 128), wspec(SW, 128),
                   pl.BlockSpec((1, 1, 2, 8, SW), lambda b, z, s, t: (z, b, 0, 0, 0))],
        out_shape=[_sds((2, N, 256)), _sds((2, 2, 128, SW)), _sds((2, 2, 128, SW)), _sds((2, 2, SW, 128)),
                   _sds((2, 2, SW, 128)), _sds((2, 2, 2, 8, SW))],
        scratch_shapes=[pltpu.VMEM((TT, SW), F32), pltpu.VMEM((TT, SW), F32), pltpu.VMEM((2, 8, SW), F32)],
        name="s5_bwd", compiler_params=_cp(("arbitrary",) * 4))(h, dyp, hre, him, bre, bim, cre, cim, tabc)


_GELU_C = math.sqrt(2.0 / math.pi)


def _gelu(y):
    return 0.5 * y * (1.0 + jnp.tanh(_GELU_C * (y + 0.044715 * y * y * y)))


def _gelu_grad(y):
    t = jnp.tanh(_GELU_C * (y + 0.044715 * y * y * y))
    return 0.5 * (1.0 + t) + 0.5 * y * (1.0 - t * t) * _GELU_C * (1.0 + 3 * 0.044715 * y * y)


def _s5_glu_fwd(y2, h, dsk, wv, wg, bv, bg):
    tm = 512

    def body(y2_ref, u_ref, d_ref, wv_ref, wg_ref, bv_ref, bg_ref, ya_ref):
        z = _gelu(y2_ref[0] + y2_ref[1] + d_ref[...] * u_ref[...])
        val = _mm(z, wv_ref[...]) + bv_ref[...]
        gate = _mm(z, wg_ref[...]) + bg_ref[...]
        ya_ref[...] = val * jax.nn.sigmoid(gate)

    full = lambda r, c: pl.BlockSpec((r, c), lambda i: (0, 0))
    return pl.pallas_call(
        body, grid=(N // tm,),
        in_specs=[pl.BlockSpec((2, tm, 256), lambda i: (0, i, 0)), pl.BlockSpec((tm, 256), lambda i: (i, 0)),
                  full(1, 256), full(256, 256), full(256, 256), full(1, 256), full(1, 256)],
        out_specs=pl.BlockSpec((tm, 256), lambda i: (i, 0)),
        out_shape=_sds((N, 256)), name="s5_glu_fwd", compiler_params=_cp(("parallel",)))(y2, h, dsk, wv, wg, bv, bg)


def _s5_glu_bwd(y2, h, dsk, wv, wg, bv, bg, dya):
    tm = 512

    def body(y2_ref, u_ref, d_ref, wv_ref, wg_ref, bv_ref, bg_ref, dya_ref,
             dyp_ref, dud_ref, dd_ref, dwv_ref, dwg_ref, dbv_ref, dbg_ref):
        i = pl.program_id(0)

        @pl.when(i == 0)
        def _():
            for r in (dd_ref, dwv_ref, dwg_ref, dbv_ref, dbg_ref):
                r[...] = jnp.zeros_like(r)

        u = u_ref[...]
        y = y2_ref[0] + y2_ref[1] + d_ref[...] * u
        z = _gelu(y)
        val = _mm(z, wv_ref[...]) + bv_ref[...]
        sig = jax.nn.sigmoid(_mm(z, wg_ref[...]) + bg_ref[...])
        dya = dya_ref[...]
        dval = dya * sig
        dgate = dya * val * sig * (1.0 - sig)
        dz = _mm_nt(dval, wv_ref[...]) + _mm_nt(dgate, wg_ref[...])
        dy = dz * _gelu_grad(y)
        dyp_ref[...] = dy
        dud_ref[...] = dy * d_ref[...]
        dd_ref[...] += jnp.sum(dy * u, axis=0, keepdims=True)
        dwv_ref[...] += _mm_tn(z, dval)
        dwg_ref[...] += _mm_tn(z, dgate)
        dbv_ref[...] += jnp.sum(dval, axis=0, keepdims=True)
        dbg_ref[...] += jnp.sum(dgate, axis=0, keepdims=True)

    full = lambda r, c: pl.BlockSpec((r, c), lambda i: (0, 0))
    row = pl.BlockSpec((tm, 256), lambda i: (i, 0))
    return pl.pallas_call(
        body, grid=(N // tm,),
        in_specs=[pl.BlockSpec((2, tm, 256), lambda i: (0, i, 0)), row,
                  full(1, 256), full(256, 256), full(256, 256), full(1, 256), full(1, 256), row],
        out_specs=[row, row, full(1, 256), full(256, 256), full(256, 256), full(1, 256), full(1, 256)],
        out_shape=[_sds((N, 256)), _sds((N, 256)), _sds((1, 256)), _sds((256, 256)), _sds((256, 256)),
                   _sds((1, 256)), _sds((1, 256))],
        name="s5_glu_bwd", compiler_params=_cp(("arbitrary",)))(y2, h, dsk, wv, wg, bv, bg, dya)


def _logsig(x):
    return jnp.minimum(x, 0.0) - jnp.log(1.0 + jnp.exp(-jnp.abs(x)))


def _gla_gate_fwd(h, wa, ba):
    tm = 512

    def body(hl_ref, wa_ref, ba_ref, la_ref):
        la_ref[...] = _logsig(_mm(hl_ref[...], wa_ref[...]) + ba_ref[...]) * (1.0 / 16.0)

    return pl.pallas_call(
        body, grid=(N // tm,),
        in_specs=[pl.BlockSpec((tm, 128), lambda i: (i, 14)), pl.BlockSpec((128, 256), lambda i: (0, 0)),
                  pl.BlockSpec((1, 256), lambda i: (0, 0))],
        out_specs=pl.BlockSpec((tm, 256), lambda i: (i, 0)),
        out_shape=_sds((N, 256)), name="gla_gate_fwd", compiler_params=_cp(("parallel",)))(h, wa, ba)


def _gla_gate_bwd(h, wa, ba, dla_f, dla_b):
    tm = 512

    def body(hl_ref, wa_ref, ba_ref, df_ref, db_ref, dhl_ref, dwa_ref, dba_ref):
        i = pl.program_id(0)

        @pl.when(i == 0)
        def _():
            dwa_ref[...] = jnp.zeros_like(dwa_ref)
            dba_ref[...] = jnp.zeros_like(dba_ref)

        hl = hl_ref[...]
        pre = _mm(hl, wa_ref[...]) + ba_ref[...]
        dpre = jnp.concatenate([df_ref[...], db_ref[...]], axis=1) * (1.0 / 16.0) * jax.nn.sigmoid(-pre)
        dhl_ref[...] = _mm_nt(dpre, wa_ref[...])
        dwa_ref[...] += _mm_tn(hl, dpre)
        dba_ref[...] += jnp.sum(dpre, axis=0, keepdims=True)

    row = pl.BlockSpec((tm, 128), lambda i: (i, 0))
    return pl.pallas_call(
        body, grid=(N // tm,),
        in_specs=[pl.BlockSpec((tm, 128), lambda i: (i, 14)), pl.BlockSpec((128, 256), lambda i: (0, 0)),
                  pl.BlockSpec((1, 256), lambda i: (0, 0)), row, row],
        out_specs=[row, pl.BlockSpec((128, 256), lambda i: (0, 0)), pl.BlockSpec((1, 256), lambda i: (0, 0))],
        out_shape=[_sds((N, 128)), _sds((128, 256)), _sds((1, 256))],
        name="gla_gate_bwd", compiler_params=_cp(("arbitrary",)))(h, wa, ba, dla_f, dla_b)


def _gla_chunk(q, k, v, la, st, rev):
    c = GLA_CHUNK
    b = _cums(la, rev)
    bl = jnp.sum(la, axis=0, keepdims=True)
    q_in = q * (32.0 ** -0.5) * jnp.exp(b)
    k_in = k * jnp.exp(-b)
    k_st = k * jnp.exp(bl - b)
    lane_k = lax.broadcasted_iota(jnp.int32, (1, 128), 1) // 32
    lane_v = lax.broadcasted_iota(jnp.int32, (1, 256), 1) // 64
    r = lax.broadcasted_iota(jnp.int32, (c, c), 0)
    cc = lax.broadcasted_iota(jnp.int32, (c, c), 1)
    keep = (cc > r) if rev else (cc <= r)
    qs = jnp.concatenate([jnp.where(lane_k == hd, q_in, 0.0) for hd in range(4)], axis=0)
    a = _dmm_nt(qs, k_in)
    a = jnp.where(jnp.concatenate([keep] * 4, axis=0), a, 0.0)
    o4 = _dmm(a, v)
    o = _dmm_nt(q_in, st)
    for hd in range(4):
        o = o + jnp.where(lane_v == hd, o4[hd * c:(hd + 1) * c], 0.0)
    bd = (lax.broadcasted_iota(jnp.int32, (256, 128), 0) // 64) == (lax.broadcasted_iota(jnp.int32, (256, 128), 1) // 32)
    st_new = jnp.exp(bl) * st + jnp.where(bd, _dmm_tn(v, k_st), 0.0)
    return o, st_new


def _gla_rows(s, c, rev):
    return s * NCHUNK + (NCHUNK - 1 - c if rev else c)


def _gla_fwd(h, la2):
    def body(qf, kf, vf, laf, qb, kb, vb, lab, of_ref, ob_ref, sf_ref, sb_ref, stf, stb):
        @pl.when(pl.program_id(1) == 0)
        def _():
            stf[...] = jnp.zeros_like(stf)
            stb[...] = jnp.zeros_like(stb)

        sf_ref[0] = stf[...]
        sb_ref[0] = stb[...]
        o, sn = _gla_chunk(qf[...], kf[...], vf[...], laf[...], stf[...], False)
        of_ref[...] = o
        stf[...] = sn
        o, sn = _gla_chunk(qb[...], kb[...], vb[...], lab[...], stb[...], True)
        ob_ref[...] = o
        stb[...] = sn

    def specs(rev):
        rw = lambda s, c: _gla_rows(s, c, rev)
        return [pl.BlockSpec((64, 128), lambda s, c: (rw(s, c), 2)), pl.BlockSpec((64, 128), lambda s, c: (rw(s, c), 3)),
                pl.BlockSpec((64, 256), lambda s, c: (rw(s, c), 2)),
                pl.BlockSpec((64, 128), lambda s, c: (rw(s, c), 1 if rev else 0))]

    orow = lambda rev: pl.BlockSpec((64, 256), lambda s, c: (_gla_rows(s, c, rev), 0))
    srow = lambda rev: pl.BlockSpec((1, 256, 128), lambda s, c: (_gla_rows(s, c, rev), 0, 0))
    return pl.pallas_call(
        body, grid=(NSEQ, NCHUNK),
        in_specs=specs(False) + specs(True),
        out_specs=[orow(False), orow(True), srow(False), srow(True)],
        out_shape=[_sds((N, 256)), _sds((N, 256)), _sds((NSEQ * NCHUNK, 256, 128)), _sds((NSEQ * NCHUNK, 256, 128))],
        scratch_shapes=[pltpu.VMEM((256, 128), F32), pltpu.VMEM((256, 128), F32)],
        name="gla_fwd", compiler_params=_cp(("arbitrary", "arbitrary")))(h, h, h, la2, h, h, h, la2)


def _gla_bwd(h, la2, do, sf, sb):
    def body(qf, kf, vf, laf, dof, sfr, qb, kb, vb, lab, dob, sbr,
             dqf, dkf, dvf, dlf, dqb, dkb, dvb, dlb, dstf, dstb):
        @pl.when(pl.program_id(1) == 0)
        def _():
            dstf[...] = jnp.zeros_like(dstf)
            dstb[...] = jnp.zeros_like(dstb)

        def one(q, k, v, la, do_, st, dst, rev, dq, dk, dv, dl):
            _, vjp = jax.vjp(functools.partial(_gla_chunk, rev=rev), q[...], k[...], v[...], la[...], st[0])
            gq, gk, gv, gl, gs = vjp((do_[...], dst[...]))
            dq[...] = gq
            dk[...] = gk
            dv[...] = gv
            dl[...] = gl
            dst[...] = gs

        one(qf, kf, vf, laf, dof, sfr, dstf, False, dqf, dkf, dvf, dlf)
        one(qb, kb, vb, lab, dob, sbr, dstb, True, dqb, dkb, dvb, dlb)

    def specs(rev):
        rw = lambda s, c: _gla_rows(s, c, not rev)
        return [pl.BlockSpec((64, 128), lambda s, c: (rw(s, c), 2)), pl.BlockSpec((64, 128), lambda s, c: (rw(s, c), 3)),
                pl.BlockSpec((64, 256), lambda s, c: (rw(s, c), 2)),
                pl.BlockSpec((64, 128), lambda s, c: (rw(s, c), 1 if rev else 0)),
                pl.BlockSpec((64, 256), lambda s, c: (rw(s, c), 0)),
                pl.BlockSpec((1, 256, 128), lambda s, c: (rw(s, c), 0, 0))]

    def ospecs(rev):
        rw = lambda s, c: _gla_rows(s, c, not rev)
        n = pl.BlockSpec((64, 128), lambda s, c: (rw(s, c), 0))
        return [n, n, pl.BlockSpec((64, 256), lambda s, c: (rw(s, c), 0)), n]

    oshape = [_sds((N, 128)), _sds((N, 128)), _sds((N, 256)), _sds((N, 128))]
    return pl.pallas_call(
        body, grid=(NSEQ, NCHUNK),
        in_specs=specs(False) + specs(True),
        out_specs=ospecs(False) + ospecs(True),
        out_shape=oshape + oshape,
        scratch_shapes=[pltpu.VMEM((256, 128), F32), pltpu.VMEM((256, 128), F32)],
        name="gla_bwd", compiler_params=_cp(("arbitrary", "arbitrary")))(h, h, h, la2, do, sf, h, h, h, la2, do, sb)


def _gla_post(of, ob, r, g):
    o = of + ob
    head = lax.broadcasted_iota(jnp.int32, (1, 256), 1) // 64
    mu = jnp.zeros_like(o)
    for hd in range(4):
        mu = mu + jnp.where(head == hd, jnp.sum(jnp.where(head == hd, o, 0.0), axis=-1, keepdims=True) * (1.0 / 64.0), 0.0)
    xc = o - mu
    var = jnp.zeros_like(o)
    for hd in range(4):
        var = var + jnp.where(head == hd, jnp.sum(jnp.where(head == hd, xc * xc, 0.0), axis=-1, keepdims=True) * (1.0 / 64.0), 0.0)
    return xc * lax.rsqrt(var + LN_EPS) * g * (r * jax.nn.sigmoid(r))


def _gla_post_fwd(of, ob, h, g):
    tm = 512

    def body(of_ref, ob_ref, r_ref, g_ref, y_ref):
        y_ref[...] = _gla_post(of_ref[...], ob_ref[...], r_ref[...], g_ref[...])

    row = pl.BlockSpec((tm, 256), lambda i: (i, 0))
    return pl.pallas_call(
        body, grid=(N // tm,),
        in_specs=[row, row, pl.BlockSpec((tm, 256), lambda i: (i, 3)), pl.BlockSpec((1, 256), lambda i: (0, 0))],
        out_specs=row, out_shape=_sds((N, 256)), name="gla_post_fwd", compiler_params=_cp(("parallel",)))(of, ob, h, g)


def _gla_post_bwd(of, ob, h, g, dyb):
    tm = 512

    def body(of_ref, ob_ref, r_ref, g_ref, dy_ref, do_ref, dr_ref, dg_ref):
        @pl.when(pl.program_id(0) == 0)
        def _():
            dg_ref[...] = jnp.zeros_like(dg_ref)

        _, vjp = jax.vjp(_gla_post, of_ref[...], ob_ref[...], r_ref[...], g_ref[...])
        go, _, gr, gg = vjp(dy_ref[...])
        do_ref[...] = go
        dr_ref[...] = gr
        dg_ref[...] += gg

    row = pl.BlockSpec((tm, 256), lambda i: (i, 0))
    one = pl.BlockSpec((1, 256), lambda i: (0, 0))
    return pl.pallas_call(
        body, grid=(N // tm,),
        in_specs=[row, row, pl.BlockSpec((tm, 256), lambda i: (i, 3)), one, row],
        out_specs=[row, row, one], out_shape=[_sds((N, 256)), _sds((N, 256)), _sds((1, 256))],
        name="gla_post_bwd", compiler_params=_cp(("arbitrary",)))(of, ob, h, g, dyb)


def _rope_tables(width):
    pos = jnp.arange(L, dtype=F32)
    inv_freq = ROPE_THETA ** (-jnp.arange(0, ROT, 2, dtype=F32) / ROT)
    ang = pos[:, None] * inv_freq[None, :]
    cos, sin = jnp.cos(ang), jnp.sin(ang)
    one = jnp.ones((L, 64 - ROT), F32)
    zero = jnp.zeros((L, 64 - ROT), F32)
    z8 = jnp.zeros((L, ROT // 2), F32)
    c = jnp.concatenate([cos, cos, one], axis=1)
    sa = jnp.concatenate([z8, sin, zero], axis=1)
    sb = jnp.concatenate([-sin, z8, zero], axis=1)
    rep = width // 64
    return jnp.stack([jnp.tile(c, (1, rep)), jnp.tile(sa, (1, rep)), jnp.tile(sb, (1, rep))])


def _rope(t, tab):
    w = t.shape[-1]
    return t * tab[0] + pltpu.roll(t, ROT // 2, 1) * tab[1] + pltpu.roll(t, w - ROT // 2, 1) * tab[2]


def _rope_t(g, tab):
    w = g.shape[-1]
    return g * tab[0] + pltpu.roll(g * tab[1], w - ROT // 2, 1) + pltpu.roll(g * tab[2], ROT // 2, 1)


def _swa_pad_kv(kv_ref, tk_ref, kpad, vpad):
    z = jnp.zeros((SWA_BLK, 128), F32)
    kpad[0:SWA_BLK] = z
    vpad[0:SWA_BLK] = z
    kpad[SWA_BLK + L:] = z
    vpad[SWA_BLK + L:] = z
    kpad[SWA_BLK:SWA_BLK + L] = _rope(kv_ref[:, 0:128], tk_ref[...])
    vpad[SWA_BLK:SWA_BLK + L] = kv_ref[:, 128:256]


def _swa_expand(x, hk):
    lane = lax.broadcasted_iota(jnp.int32, x.shape, 1)
    sw = pltpu.roll(x, 64, 1)
    pair = jnp.where(lane < 64, x, sw) if hk == 0 else jnp.where(lane < 64, sw, x)
    return jnp.concatenate([pair, pair], axis=1)


def _swa_fold(x, hk):
    a = x[:, 0:128] + x[:, 128:256]
    t = a + pltpu.roll(a, 64, 1)
    lane = lax.broadcasted_iota(jnp.int32, a.shape, 1)
    return jnp.where((lane < 64) if hk == 0 else (lane >= 64), t, 0.0)


def _swa_probs(q2, kexp, n, sink_ref, hk):
    slot = lax.broadcasted_iota(jnp.int32, (1, 256), 1) // 64
    qs = jnp.concatenate([jnp.where(slot == g, q2, 0.0) for g in range(4)], axis=0)
    s = _mm_nt(qs, kexp) * 0.125
    i = lax.broadcasted_iota(jnp.int32, (SWA_BLK, 3 * SWA_BLK), 0)
    j = lax.broadcasted_iota(jnp.int32, (SWA_BLK, 3 * SWA_BLK), 1)
    kpos = n * SWA_BLK - SWA_BLK + j
    ok = (j - i >= 0) & (j - i <= 2 * SWA_BLK) & (kpos >= 0) & (kpos < L)
    s = jnp.where(jnp.concatenate([ok] * 4, axis=0), s, NEG_BIG)
    rowg = lax.broadcasted_iota(jnp.int32, (4 * SWA_BLK, 1), 0) // SWA_BLK
    sink = jnp.zeros((4 * SWA_BLK, 1), F32)
    for g in range(4):
        sink = jnp.where(rowg == g, sink_ref[hk * 4 + g], sink)
    m = jnp.maximum(jnp.max(s, axis=-1, keepdims=True), sink)
    p = jnp.exp(s - m)
    ps = jnp.exp(sink - m)
    inv = 1.0 / (jnp.sum(p, axis=-1, keepdims=True) + ps)
    return qs, p * inv, ps * inv, slot, rowg


def _swa_fwd(h, tq, tk, sink):
    def body(sink_ref, q_ref, kv_ref, tq_ref, tk_ref, y_ref, kpad, vpad):
        n = pl.program_id(1)

        @pl.when(n == 0)
        def _():
            _swa_pad_kv(kv_ref, tk_ref, kpad, vpad)

        q = _rope(q_ref[...], tq_ref[...])
        r0 = pl.multiple_of(n * SWA_BLK, SWA_BLK)
        kb = kpad[pl.ds(r0, 3 * SWA_BLK), :]
        vb = vpad[pl.ds(r0, 3 * SWA_BLK), :]
        for hk in range(2):
            _, p, _, slot, _ = _swa_probs(q[:, hk * 256:(hk + 1) * 256], _swa_expand(kb, hk), n, sink_ref, hk)
            o4 = _mm(p, _swa_expand(vb, hk))
            o = jnp.zeros((SWA_BLK, 256), F32)
            for g in range(4):
                o = o + jnp.where(slot == g, o4[g * SWA_BLK:(g + 1) * SWA_BLK], 0.0)
            y_ref[:, hk * 256:(hk + 1) * 256] = o

    return pl.pallas_call(
        body,
        grid_spec=pltpu.PrefetchScalarGridSpec(
            num_scalar_prefetch=1, grid=(NSEQ, NBLK),
            in_specs=[pl.BlockSpec((SWA_BLK, 512), lambda s, n, sk: (s * NBLK + n, 2)),
                      pl.BlockSpec((L, 256), lambda s, n, sk: (s, 6)),
                      pl.BlockSpec((3, SWA_BLK, 512), lambda s, n, sk: (0, n, 0)),
                      pl.BlockSpec((3, L, 128), lambda s, n, sk: (0, 0, 0))],
            out_specs=pl.BlockSpec((SWA_BLK, 512), lambda s, n, sk: (s * NBLK + n, 0)),
            scratch_shapes=[pltpu.VMEM((L + 2 * SWA_BLK, 128), F32), pltpu.VMEM((L + 2 * SWA_BLK, 128), F32)]),
        out_shape=_sds((N, 512)), name="swa_fwd", compiler_params=_cp(("arbitrary", "arbitrary")))(sink, h, h, tq, tk)


def _swa_bwd(h, tq, tk, sink, dyc):
    def body(sink_ref, q_ref, kv_ref, tq_ref, tk_ref, dy_ref, dq_ref, dkv_ref, dsink_ref, kpad, vpad, dkacc, dvacc):
        sq = pl.program_id(0)
        n = pl.program_id(1)

        @pl.when(n == 0)
        def _():
            _swa_pad_kv(kv_ref, tk_ref, kpad, vpad)
            dkacc[...] = jnp.zeros_like(dkacc)
            dvacc[...] = jnp.zeros_like(dvacc)

        @pl.when((n == 0) & (sq == 0))
        def _():
            dsink_ref[...] = jnp.zeros_like(dsink_ref)

        q = _rope(q_ref[...], tq_ref[...])
        r0 = pl.multiple_of(n * SWA_BLK, SWA_BLK)
        kb = kpad[pl.ds(r0, 3 * SWA_BLK), :]
        vb = vpad[pl.ds(r0, 3 * SWA_BLK), :]
        dk = jnp.zeros((3 * SWA_BLK, 128), F32)
        dv = jnp.zeros((3 * SWA_BLK, 128), F32)
        hrow = lax.broadcasted_iota(jnp.int32, (8, 128), 0)
        dsk = jnp.zeros((8, 128), F32)
        for hk in range(2):
            kexp = _swa_expand(kb, hk)
            vexp = _swa_expand(vb, hk)
            qs, p, ps, slot, rowg = _swa_probs(q[:, hk * 256:(hk + 1) * 256], kexp, n, sink_ref, hk)
            dy2 = dy_ref[:, hk * 256:(hk + 1) * 256]
            dos = jnp.concatenate([jnp.where(slot == g, dy2, 0.0) for g in range(4)], axis=0)
            dp = _mm_nt(dos, vexp)
            delta = jnp.sum(p * dp, axis=-1, keepdims=True)
            ds = p * (dp - delta) * 0.125
            dsr = -ps * delta
            for g in range(4):
                dsk = dsk + jnp.where(hrow == hk * 4 + g, jnp.sum(jnp.where(rowg == g, dsr, 0.0), axis=0, keepdims=True), 0.0)
            dq4 = _mm(ds, kexp)
            dq2 = jnp.zeros((SWA_BLK, 256), F32)
            for g in range(4):
                dq2 = dq2 + jnp.where(slot == g, dq4[g * SWA_BLK:(g + 1) * SWA_BLK], 0.0)
            dq_ref[:, hk * 256:(hk + 1) * 256] = dq2
            dk = dk + _swa_fold(_mm_tn(ds, qs), hk)
            dv = dv + _swa_fold(_mm_tn(p, dos), hk)
        dq_ref[...] = _rope_t(dq_ref[...], tq_ref[...])
        dkacc[pl.ds(r0, 3 * SWA_BLK), :] += dk
        dvacc[pl.ds(r0, 3 * SWA_BLK), :] += dv
        dsink_ref[...] += dsk

        @pl.when(n == NBLK - 1)
        def _():
            dkv_ref[:, 0:128] = _rope_t(dkacc[SWA_BLK:SWA_BLK + L], tk_ref[...])
            dkv_ref[:, 128:256] = dvacc[SWA_BLK:SWA_BLK + L]

    blk = lambda col: pl.BlockSpec((SWA_BLK, 512), lambda s, n, sk: (s * NBLK + n, col))
    pad = pltpu.VMEM((L + 2 * SWA_BLK, 128), F32)
    return pl.pallas_call(
        body,
        grid_spec=pltpu.PrefetchScalarGridSpec(
            num_scalar_prefetch=1, grid=(NSEQ, NBLK),
            in_specs=[blk(2), pl.BlockSpec((L, 256), lambda s, n, sk: (s, 6)),
                      pl.BlockSpec((3, SWA_BLK, 512), lambda s, n, sk: (0, n, 0)),
                      pl.BlockSpec((3, L, 128), lambda s, n, sk: (0, 0, 0)), blk(0)],
            out_specs=[blk(0), pl.BlockSpec((L, 256), lambda s, n, sk: (s, 0)),
                       pl.BlockSpec((8, 128), lambda s, n, sk: (0, 0))],
            scratch_shapes=[pad, pad, pad, pad]),
        out_shape=[_sds((N, 512)), _sds((N, 256)), _sds((8, 128))],
        name="swa_bwd", compiler_params=_cp(("arbitrary", "arbitrary")))(sink, h, h, tq, tk, dyc)


def _outproj_fwd(ya, yb, yc, x, wo, g, b):
    tm = 512

    def body(ya_ref, yb_ref, yc_ref, x_ref, wo_ref, g_ref, b_ref, s_ref, x1_ref):
        mix = _mm(ya_ref[...], wo_ref[0:256]) + _mm(yb_ref[...], wo_ref[256:512]) + _mm(yc_ref[...], wo_ref[512:1024])
        s = ALPHA * x_ref[...] + mix
        s_ref[...] = s
        x1_ref[...] = _ln_fwd(s, g_ref[...], b_ref[...])

    row = lambda w_: pl.BlockSpec((tm, w_), lambda i: (i, 0))
    one = pl.BlockSpec((1, D), lambda i: (0, 0))
    return pl.pallas_call(
        body, grid=(N // tm,),
        in_specs=[row(256), row(256), row(512), row(D), pl.BlockSpec((D, D), lambda i: (0, 0)), one, one],
        out_specs=[row(D), row(D)], out_shape=[_sds((N, D)), _sds((N, D))],
        name="outproj_fwd", compiler_params=_cp(("parallel",)))(ya, yb, yc, x, wo, g, b)


def _outproj_bwd(dx1, s1, ya, yb, yc, wo, g):
    tm = 512
    nt = N // tm

    def body(dx1_ref, s_ref, ya_ref, yb_ref, yc_ref, wo_ref, g_ref,
             dya_ref, dyb_ref, dyc_ref, dxp_ref, dwo_ref, dg_ref, db_ref, acc):
        i = pl.program_id(0)

        @pl.when(i == 0)
        def _():
            acc[...] = jnp.zeros_like(acc)
            dg_ref[...] = jnp.zeros_like(dg_ref)
            db_ref[...] = jnp.zeros_like(db_ref)

        ds, dg, db = _ln_bwd(dx1_ref[...], s_ref[...], g_ref[...])
        dg_ref[...] += dg
        db_ref[...] += db
        dxp_ref[...] = ALPHA * ds
        dy = _mm_nt(ds, wo_ref[...])
        dya_ref[...] = dy[:, 0:256]
        dyb_ref[...] = dy[:, 256:512]
        dyc_ref[...] = dy[:, 512:1024]
        acc[0:256] += _mm_tn(ya_ref[...], ds)
        acc[256:512] += _mm_tn(yb_ref[...], ds)
        acc[512:1024] += _mm_tn(yc_ref[...], ds)

        @pl.when(i == nt - 1)
        def _():
            dwo_ref[...] = acc[...].astype(MX)

    row = lambda w_: pl.BlockSpec((tm, w_), lambda i: (i, 0))
    one = pl.BlockSpec((1, D), lambda i: (0, 0))
    full = pl.BlockSpec((D, D), lambda i: (0, 0))
    return pl.pallas_call(
        body, grid=(nt,),
        in_specs=[row(D), row(D), row(256), row(256), row(512), full, one],
        out_specs=[row(256), row(256), row(512), row(D), full, one, one],
        out_shape=[_sds((N, 256)), _sds((N, 256)), _sds((N, 512)), _sds((N, D)), _sds((D, D), MX), _sds((1, D)), _sds((1, D))],
        scratch_shapes=[pltpu.VMEM((D, D), F32)],
        name="outproj_bwd", compiler_params=_cp(("arbitrary",)))(dx1, s1, ya, yb, yc, wo, g)


def _ffn_fwd(x1, w1, w2, g, b):
    tm = 512

    def body(x_ref, w1_ref, w2_ref, g_ref, b_ref, a_ref, s_ref, x2_ref):
        j = pl.program_id(1)
        a = _mm(x_ref[...], w1_ref[0])
        a_ref[...] = a.astype(MX)
        hid = jnp.square(jnp.maximum(a, 0.0))
        contrib = _mm(hid, w2_ref[0])

        @pl.when(j == 0)
        def _():
            s_ref[...] = ALPHA * x_ref[...] + contrib

        @pl.when(j > 0)
        def _():
            s_ref[...] += contrib

        @pl.when(j == NSHARD - 1)
        def _():
            x2_ref[...] = _ln_fwd(s_ref[...], g_ref[...], b_ref[...])

    row = pl.BlockSpec((tm, D), lambda i, j: (i, 0))
    wj = pl.BlockSpec((1, D, D), lambda i, j: (j, 0, 0))
    one = pl.BlockSpec((1, D), lambda i, j: (0, 0))
    return pl.pallas_call(
        body, grid=(N // tm, NSHARD),
        in_specs=[row, wj, wj, one, one],
        out_specs=[pl.BlockSpec((tm, D), lambda i, j: (i, j)), row, row],
        out_shape=[_sds((N, DFF), MX), _sds((N, D)), _sds((N, D))],
        name="ffn_fwd", compiler_params=_cp(("parallel", "arbitrary")))(x1, w1, w2, g, b)


def _ffn_bwd_act(dy, s2, a, w1, w2, g):
    tm = 512

    def body(dy_ref, s_ref, a_ref, w1_ref, w2_ref, g_ref, da_ref, ds_ref, dx1_ref, dg_ref, db_ref, dsf):
        i = pl.program_id(0)
        j = pl.program_id(1)

        @pl.when((i == 0) & (j == 0))
        def _():
            dg_ref[...] = jnp.zeros_like(dg_ref)
            db_ref[...] = jnp.zeros_like(db_ref)

        @pl.when(j == 0)
        def _():
            ds, dg, db = _ln_bwd(dy_ref[...], s_ref[...], g_ref[...])
            dsf[...] = ds
            ds_ref[...] = ds.astype(MX)
            dg_ref[...] += dg
            db_ref[...] += db
            dx1_ref[...] = ALPHA * ds

        dhid = _mm_nt(dsf[...], w2_ref[0])
        da = dhid * 2.0 * jnp.maximum(a_ref[...].astype(F32), 0.0)
        da_ref[...] = da.astype(MX)
        dx1_ref[...] += _mm_nt(da, w1_ref[0])

    row = pl.BlockSpec((tm, D), lambda i, j: (i, 0))
    col = pl.BlockSpec((tm, D), lambda i, j: (i, j))
    wj = pl.BlockSpec((1, D, D), lambda i, j: (j, 0, 0))
    one = pl.BlockSpec((1, D), lambda i, j: (0, 0))
    return pl.pallas_call(
        body, grid=(N // tm, NSHARD),
        in_specs=[row, row, col, wj, wj, one],
        out_specs=[col, row, row, one, one],
        out_shape=[_sds((N, DFF), MX), _sds((N, D), MX), _sds((N, D)), _sds((1, D)), _sds((1, D))],
        scratch_shapes=[pltpu.VMEM((tm, D), F32)],
        name="ffn_bwd_act", compiler_params=_cp(("arbitrary", "arbitrary")))(dy, s2, a, w1, w2, g)


def _ffn_bwd_w(x1, da, a, ds):
    tm = 512
    nt = N // tm

    def body(x_ref, da_ref, a_ref, ds_ref, dw1_ref, dw2_ref, acc1, acc2):
        i = pl.program_id(1)

        @pl.when(i == 0)
        def _():
            acc1[...] = jnp.zeros_like(acc1)
            acc2[...] = jnp.zeros_like(acc2)

        acc1[...] += _mm_tn(x_ref[...], da_ref[...])
        hid = jnp.square(jnp.maximum(a_ref[...].astype(F32), 0.0))
        acc2[...] += _mm_tn(hid, ds_ref[...])

        @pl.when(i == nt - 1)
        def _():
            dw1_ref[0] = acc1[...].astype(MX)
            dw2_ref[0] = acc2[...].astype(MX)

    row = pl.BlockSpec((tm, D), lambda j, i: (i, 0))
    col = pl.BlockSpec((tm, D), lambda j, i: (i, j))
    wj = pl.BlockSpec((1, D, D), lambda j, i: (j, 0, 0))
    return pl.pallas_call(
        body, grid=(NSHARD, nt),
        in_specs=[row, col, col, row], out_specs=[wj, wj],
        out_shape=[_sds((NSHARD, D, D), MX), _sds((NSHARD, D, D), MX)],
        scratch_shapes=[pltpu.VMEM((D, D), F32), pltpu.VMEM((D, D), F32)],
        name="ffn_bwd_w", compiler_params=_cp(("parallel", "arbitrary")))(x1, da, a, ds)


def _loss_head(y, target):
    tm = 512

    def body(y_ref, t_ref, dy_ref, l_ref):
        @pl.when(pl.program_id(0) == 0)
        def _():
            l_ref[...] = jnp.zeros_like(l_ref)

        e = y_ref[...] - t_ref[...]
        dy_ref[...] = e * (1.0 / D)
        l_ref[...] += jnp.sum(jnp.sum(e * e, axis=1, keepdims=True), axis=0, keepdims=True) * (0.5 / D)

    row = pl.BlockSpec((tm, D), lambda i: (i, 0))
    return pl.pallas_call(
        body, grid=(N // tm,), in_specs=[row, row],
        out_specs=[row, pl.BlockSpec((8, 128), lambda i: (0, 0))],
        out_shape=[_sds((N, D)), _sds((8, 128))], name="loss_head", compiler_params=_cp(("arbitrary",)))(y, target)


def _s5_discretize(a_re, a_im, log_step, b_re, b_im):
    lam = lax.complex(a_re, a_im)
    lam_bar = jnp.exp(lam * jnp.exp(log_step))
    b_bar = ((lam_bar - 1.0) / lam)[..., None] * lax.complex(b_re, b_im)
    return jnp.real(lam_bar), jnp.imag(lam_bar), jnp.real(b_bar), jnp.imag(b_bar)


def _s5_in_blocks(b):
    e = jnp.eye(8, dtype=F32)
    return jnp.einsum('ij,zbjph->zbihjp', e, b.reshape(2, 2, 8, S5_P, S5_H)).reshape(2, 2, 128, SW)


def _s5_in_unblocks(d):
    return jnp.einsum('zbihip->zbiph', d.reshape(2, 2, 8, S5_H, 8, S5_P)).reshape(2, S5_G, S5_P, S5_H)


def _s5_out_blocks(c):
    e = jnp.eye(8, dtype=F32)
    return jnp.einsum('ij,zbjhp->zbjpih', e, c.reshape(2, 2, 8, S5_H, S5_P)).reshape(2, 2, SW, 128)


def _s5_out_unblocks(d):
    return jnp.einsum('zbipih->zbihp', d.reshape(2, 2, 8, S5_P, 8, S5_H)).reshape(2, S5_G, S5_H, S5_P)


def _gate_weight(w_a):
    z = jnp.zeros((16, 128), F32)
    top = jnp.concatenate([w_a[0], z], axis=1)
    bot = jnp.concatenate([z, w_a[1]], axis=1)
    return jnp.concatenate([top, bot, jnp.zeros((96, 256), F32)], axis=0)


def _layer_prep(p):
    lr, li, br, bi = _s5_discretize(p["s5_a_re"], p["s5_a_im"], p["s5_log_step"], p["s5_b_re"], p["s5_b_im"])
    q = dict(p)
    q["bre"] = _s5_in_blocks(br).astype(MX)
    q["bim"] = _s5_in_blocks(bi).astype(MX)
    q["cre"] = _s5_out_blocks(p["s5_c_re"]).astype(MX)
    q["cim"] = _s5_out_blocks(p["s5_c_im"]).astype(MX)
    mr, mi = lr.reshape(2, 1024), li.reshape(2, 1024)
    q["tab"] = jnp.stack([_scan_tables(mr[0], mi[0], False), _scan_tables(mr[1], mi[1], True)])
    q["tabc"] = jnp.stack([_scan_tables(mr[0], -mi[0], True), _scan_tables(mr[1], -mi[1], False)])
    q["dsk"] = p["s5_d"].reshape(1, 256)
    q["wa"] = _gate_weight(p["gla_w_a"]).astype(MX)
    q["ba"] = p["gla_b_a"].reshape(1, 256)
    q["lng"] = p["gla_ln_g"].reshape(1, 256)
    q["bv"] = p["s5_b_glu"][:256].reshape(1, 256)
    q["bg"] = p["s5_b_glu"][256:].reshape(1, 256)
    for k in ("ln1_g", "ln1_b", "ln2_g", "ln2_b"):
        q[k] = p[k].reshape(1, D)
    return q


def _layer_fwd(x, q, tq, tk, fetch):
    q["w_in"] = fetch("w_in", x)
    h = _inproj_fwd(x, q["w_in"])
    hre, him, y2 = _s5_fwd(h, q["bre"], q["bim"], q["cre"], q["cim"], q["tab"])
    q["wv"], q["wg"] = fetch("s5_w_glu", y2)
    ya = _s5_glu_fwd(y2, h, q["dsk"], q["wv"], q["wg"], q["bv"], q["bg"])
    la2 = _gla_gate_fwd(h, q["wa"], q["ba"])
    of, ob, sf, sb = _gla_fwd(h, la2)
    yb = _gla_post_fwd(of, ob, h, q["lng"])
    yc = _swa_fwd(h, tq, tk, q["swa_sink"])
    q["w_out"] = fetch("w_out", yc)
    s1, x1 = _outproj_fwd(ya, yb, yc, x, q["w_out"], q["ln1_g"], q["ln1_b"])
    q["w_ff1"] = fetch("w_ff1", x1)
    q["w_ff2"] = fetch("w_ff2", x1)
    a, s2, x2 = _ffn_fwd(x1, q["w_ff1"], q["w_ff2"], q["ln2_g"], q["ln2_b"])
    saved = dict(x=x, h=h, hre=hre, him=him, y2=y2, ya=ya, la2=la2, of=of, ob=ob, sf=sf, sb=sb, yb=yb, yc=yc,
                 s1=s1, x1=x1, a=a, s2=s2)
    return x2, saved


def _layer_bwd(dy, q, sv, tq, tk, emit):
    g = {}
    da, ds2, dx1, g["ln2_g"], g["ln2_b"] = _ffn_bwd_act(dy, sv["s2"], sv["a"], q["w_ff1"], q["w_ff2"], q["ln2_g"])
    dw1, dw2 = _ffn_bwd_w(sv["x1"], da, sv["a"], ds2)
    tie = emit(dict(w_ff1=dw1, w_ff2=dw2))
    dya, dyb, dyc, dxp, dwo, g["ln1_g"], g["ln1_b"] = _outproj_bwd(dx1, sv["s1"], sv["ya"], sv["yb"], sv["yc"],
                                                                     q["w_out"], q["ln1_g"] + tie)
    h = sv["h"]
    daq, dakv, dsink = _swa_bwd(h, tq, tk, q["swa_sink"], dyc)
    g["swa_sink"] = dsink[:, 0]
    do, gr, dlng = _gla_post_bwd(sv["of"], sv["ob"], h, q["lng"], dyb)
    g["gla_ln_g"] = dlng.reshape(256)
    gq_f, gk_f, gv_f, gl_f, gq_b, gk_b, gv_b, gl_b = _gla_bwd(h, sv["la2"], do, sv["sf"], sv["sb"])
    dhl, dwa, dba = _gla_gate_bwd(h, q["wa"], q["ba"], gl_f, gl_b)
    g["gla_w_a"] = jnp.stack([dwa[0:16, 0:128], dwa[16:32, 128:256]])
    g["gla_b_a"] = dba.reshape(2, 128)
    dyp, dud, dd, dwv, dwg, dbv, dbg = _s5_glu_bwd(sv["y2"], h, q["dsk"], q["wv"], q["wg"], q["bv"], q["bg"], dya)
    g["s5_d"] = dd.reshape(S5_G, S5_H)
    g["s5_b_glu"] = jnp.concatenate([dbv, dbg], axis=1).reshape(512)
    tie = emit(dict(w_out=dwo.reshape(NSHARD, D // NSHARD, D),
                    s5_w_glu=_shard_cols(jnp.concatenate([dwv, dwg], axis=1)).astype(MX)))
    du2, dbre, dbim, dcre, dcim, dmu = _s5_bwd(h, dyp, sv["hre"], sv["him"], q["bre"], q["bim"], q["cre"], q["cim"],
                                               q["tabc"] + tie)
    g["s5_c_re"] = _s5_out_unblocks(dcre)
    g["s5_c_im"] = _s5_out_unblocks(dcim)
    dmu = jnp.sum(dmu, axis=3)
    dlr = dmu[:, :, 0].reshape(2, S5_G, S5_P)
    dli = dmu[:, :, 1].reshape(2, S5_G, S5_P)
    _, vjp = jax.vjp(_s5_discretize, q["s5_a_re"], q["s5_a_im"], q["s5_log_step"], q["s5_b_re"], q["s5_b_im"])
    (g["s5_a_re"], g["s5_a_im"], g["s5_log_step"], g["s5_b_re"], g["s5_b_im"]) = vjp(
        (dlr, dli, _s5_in_unblocks(dbre), _s5_in_unblocks(dbim)))
    dx, dwin = _inproj_bwd(sv["x"], q["w_in"], dxp, du2, dud, gq_f, gq_b, gk_f, gk_b, gv_f, gv_b, gr, daq, dakv, dhl)
    tie = emit(dict(w_in=_shard_cols(_in_cols_back(dwin)).astype(MX)))
    return dx, g, tie


def _local_step(x, target, qs, tq, tk, fetch, emit):
    saved = []
    for l, q in enumerate(qs):
        x, sv = _layer_fwd(x, q, tq, tk, functools.partial(fetch, l))
        saved.append(sv)
    dy, lacc = _loss_head(x, target)
    smalls = [None] * DEPTH
    tie = 0.0
    for l in reversed(range(DEPTH)):
        qs[l]["ln2_g"] = qs[l]["ln2_g"] + tie
        dy, smalls[l], tie = _layer_bwd(dy, qs[l], saved[l], tq, tk, functools.partial(emit, l))
    return lacc[0, 0], dy, smalls


BIG = ("w_in", "s5_w_glu", "w_out", "w_ff1", "w_ff2")
SMALL = ("s5_a_re", "s5_a_im", "s5_log_step", "s5_b_re", "s5_b_im", "s5_c_re", "s5_c_im", "s5_d", "s5_b_glu",
         "gla_w_a", "gla_b_a", "gla_ln_g", "swa_sink", "ln1_g", "ln1_b", "ln2_g", "ln2_b")
ANY = pl.BlockSpec(memory_space=pl.ANY)


def _place():
    x, y, c = lax.axis_index("x"), lax.axis_index("y"), lax.axis_index("c")
    return x, y, c, [(1 - x, y), (x, 1 - y), (1 - x, 1 - y)]


HBM = pl.BlockSpec(memory_space=pltpu.HBM)
SEMS = pl.BlockSpec(memory_space=pltpu.SEMAPHORE)
EFFECT = pltpu.SideEffectType.DATAFLOW_SIDE_EFFECTING


def _push_copies(ins, lands, send, recv, by_shard, sending):
    x, y, c, chips = _place()
    me = 2 * x + y
    out = []
    for a in range(len(ins)):
        for j, (px, py) in enumerate(chips):
            peer = 2 * px + py
            if sending:
                src, dst = (ins[a].at[peer] if by_shard else ins[a]), lands[a].at[me]
            else:
                src, dst = (ins[a].at[me] if by_shard else ins[a]), lands[a].at[peer]
            out.append(pltpu.make_async_remote_copy(src_ref=src, dst_ref=dst, send_sem=send.at[3 * a + j],
                                                    recv_sem=recv.at[3 * a + j], device_id=(px, py, c),
                                                    device_id_type=MESH))
    return out


def _push_start(name, srcs, by_shard):
    n = len(srcs)
    lands = [lax.empty(s.shape if by_shard else (NSHARD,) + s.shape, s.dtype) for s in srcs]

    def body(*refs):
        ins, lnd = refs[:n], refs[n:2 * n]
        send, recv = refs[2 * n], refs[2 * n + 1]
        for cp in _push_copies(ins, lnd, send, recv, by_shard, True):
            cp.start()
        refs[-1][...] = jnp.zeros((8, 128), F32)

    ops = [pltpu.with_memory_space_constraint(t, pltpu.HBM) for t in list(srcs) + lands]
    res = pl.pallas_call(
        body, name=name,
        out_shape=(pltpu.SemaphoreType.DMA((3 * n,)), pltpu.SemaphoreType.DMA((3 * n,)),
                   *[pltpu.HBM(t.shape, t.dtype) for t in ops], _sds((8, 128))),
        in_specs=[HBM] * (2 * n),
        out_specs=(SEMS, SEMS, *[HBM] * (2 * n), pl.BlockSpec(memory_space=pltpu.VMEM)),
        input_output_aliases={i: 2 + i for i in range(2 * n)},
        compiler_params=pltpu.CompilerParams(has_side_effects=EFFECT))(*ops)
    return res[0], res[1], list(res[2:2 + n]), list(res[2 + n:2 + 2 * n]), res[-1]


def _push_wait(name, started, after, by_shard):
    send, recv, srcs, lands, _ = started
    n = len(srcs)

    def body(*refs):
        ins, lnd = refs[:n], refs[n:2 * n]
        for cp in _push_copies(ins, lnd, refs[2 * n], refs[2 * n + 1], by_shard, False):
            cp.wait_send()
            cp.wait_recv()

    res = pl.pallas_call(
        body, name=name,
        out_shape=[pltpu.HBM(t.shape, t.dtype) for t in srcs + lands],
        in_specs=[HBM] * (2 * n) + [SEMS, SEMS, ANY], out_specs=[HBM] * (2 * n),
        input_output_aliases={i: i for i in range(2 * n)},
        compiler_params=pltpu.CompilerParams(has_side_effects=EFFECT))(*srcs, *lands, send, recv, after)
    return list(res[:n]), list(res[n:])


def _sum_sources(me, recv, own):
    _, rows, cols = recv[0].shape
    tr = min(rows, 256)
    nt = rows // tr

    def body(me_ref, *refs):
        o_ref = refs[-1]
        for l in range(DEPTH):
            @pl.when(pl.program_id(0) == l)
            def _():
                r_ref, own_ref = refs[2 * l], refs[2 * l + 1]
                part = [jnp.where(me_ref[0] == s, own_ref[0], r_ref[s]).astype(F32) for s in range(NSHARD)]
                o_ref[...] = ((part[0] + part[1]) + part[2]) + part[3]

    in_specs = []
    for l in range(DEPTH):
        pick = lambda g, i, me_, l=l: jnp.where(g == l, i, jnp.where(g < l, 0, nt - 1))
        in_specs += [pl.BlockSpec((NSHARD, tr, cols), lambda g, i, me_, pick=pick: (0, pick(g, i, me_), 0)),
                     pl.BlockSpec((1, tr, cols), lambda g, i, me_, pick=pick: (me_[0], pick(g, i, me_), 0))]
    return pl.pallas_call(
        body,
        grid_spec=pltpu.PrefetchScalarGridSpec(
            num_scalar_prefetch=1, grid=(DEPTH, nt), in_specs=in_specs,
            out_specs=pl.BlockSpec((tr, cols), lambda g, i, me_: (g * nt + i, 0))),
        out_shape=_sds((DEPTH * rows, cols)), name="sum_sources",
        compiler_params=_cp(("arbitrary", "arbitrary")))(me, *[t for l in range(DEPTH) for t in (recv[l], own[l])])


def _swap_sibling(arrs):
    n = len(arrs)

    def body(*refs):
        ins, outs = refs[:n], refs[n:2 * n]
        send, recv = refs[2 * n:]
        x, y, c, _ = _place()
        cps = [pltpu.make_async_remote_copy(src_ref=ins[a], dst_ref=outs[a], send_sem=send.at[a], recv_sem=recv.at[a],
                                            device_id=(x, y, 1 - c), device_id_type=MESH) for a in range(n)]
        for cp in cps:
            cp.start()
        for cp in cps:
            cp.wait()

    return pl.pallas_call(
        body, in_specs=[ANY] * n, out_specs=[ANY] * n, out_shape=[_sds(a.shape, a.dtype) for a in arrs],
        scratch_shapes=[pltpu.SemaphoreType.DMA((n,)), pltpu.SemaphoreType.DMA((n,))],
        name="swap_sibling")(*arrs)


def _allreduce_small(v):
    rows = v.shape[0]

    def body(v_ref, o_ref, sib, slots, send, recv):
        x, y, c, chips = _place()
        me = 2 * x + y
        d2d = pltpu.make_async_remote_copy(src_ref=v_ref, dst_ref=sib, send_sem=send.at[0], recv_sem=recv.at[0],
                                           device_id=(x, y, 1 - c), device_id_type=MESH)
        d2d.start()
        d2d.wait()
        slots[me] = v_ref[...] + sib[...]

        def remote(j, slot):
            px, py = chips[j]
            return pltpu.make_async_remote_copy(src_ref=slots.at[me], dst_ref=slots.at[slot], send_sem=send.at[1 + j],
                                                recv_sem=recv.at[1 + j], device_id=(px, py, c), device_id_type=MESH)

        sends = [remote(j, me) for j in range(3)]
        for cp in sends:
            cp.start()
        for j in range(3):
            remote(j, 2 * chips[j][0] + chips[j][1]).wait_recv()
        for cp in sends:
            cp.wait_send()
        o_ref[...] = ((slots[0] + slots[1]) + slots[2]) + slots[3]

    vm = pl.BlockSpec(memory_space=pltpu.VMEM)
    return pl.pallas_call(
        body, in_specs=[vm], out_specs=vm, out_shape=_sds((rows, 128)),
        scratch_shapes=[pltpu.VMEM((rows, 128), F32), pltpu.VMEM((NSHARD, rows, 128), F32),
                        pltpu.SemaphoreType.DMA((4,)), pltpu.SemaphoreType.DMA((4,))],
        name="allreduce_small", compiler_params=pltpu.CompilerParams(vmem_limit_bytes=VMEM_LIMIT))(v)


def _adamw_math(w, g, m, v):
    m = ADAM_B1 * m + (1.0 - ADAM_B1) * g
    v = ADAM_B2 * v + (1.0 - ADAM_B2) * jnp.square(g)
    m_hat = m / (1.0 - ADAM_B1 ** ADAM_STEP)
    v_hat = v / (1.0 - ADAM_B2 ** ADAM_STEP)
    delta = -ADAM_LR * (m_hat / (jnp.sqrt(v_hat) + ADAM_EPS) + ADAM_WD * w)
    return delta, m, v


def _adamw(g_parts, w, m, v):
    rows, cols = w.shape
    tr = 256 if rows % 256 == 0 else rows
    k = len(g_parts)

    def body(*refs):
        g = refs[0][...]
        for r in refs[1:k]:
            g = g + r[...]
        w_ref, m_ref, v_ref, go, do, mo, vo = refs[k:]
        d, mn, vn = _adamw_math(w_ref[...], g, m_ref[...], v_ref[...])
        go[...] = g
        do[...] = d
        mo[...] = mn
        vo[...] = vn

    spec = pl.BlockSpec((tr, cols), lambda i: (i, 0))
    return pl.pallas_call(
        body, grid=(rows // tr,), in_specs=[spec] * (k + 3), out_specs=[spec] * 4,
        out_shape=[_sds((rows, cols))] * 4, name="adamw", compiler_params=_cp(("parallel",)))(*g_parts, w, m, v)


def _pack(arrs, rows):
    flat = jnp.concatenate([a.reshape(-1) for a in arrs])
    return jnp.pad(flat, (0, rows * 128 - flat.shape[0])).reshape(rows, 128)


def _unpack(p, like):
    flat = p.reshape(-1)
    out, o = [], 0
    for a in like:
        out.append(flat[o:o + a.size].reshape(a.shape))
        o += a.size
    return out


_ARGS = ("x", "w_in", "s5_a_re", "s5_a_im", "s5_log_step", "s5_b_re", "s5_b_im", "s5_c_re", "s5_c_im", "s5_d",
         "s5_w_glu", "s5_b_glu", "gla_w_a", "gla_b_a", "gla_ln_g", "swa_sink", "w_out", "ln1_g", "ln1_b", "w_ff1",
         "w_ff2", "ln2_g", "ln2_b")
_WEIGHTS = _ARGS[1:]


def _in_cols(w):
    return jnp.concatenate([w[:, 0:1024], w[:, 1056:DIN], w[:, 1024:1056], jnp.zeros((D, DINP - DIN), w.dtype)], axis=1)


def _in_cols_back(d):
    return jnp.concatenate([d[:, 0:1024], d[:, 1792:1824], d[:, 1024:1792]], axis=1)


def _unshard_cols(g):
    return g.transpose(1, 0, 2).reshape(g.shape[1], NSHARD * g.shape[2])


def _shard_cols(d):
    return d.reshape(d.shape[0], NSHARD, d.shape[1] // NSHARD).transpose(1, 0, 2)


def kernel(x, w_in, s5_a_re, s5_a_im, s5_log_step, s5_b_re, s5_b_im, s5_c_re, s5_c_im, s5_d, s5_w_glu, s5_b_glu, gla_w_a, gla_b_a, gla_ln_g, swa_sink, w_out, ln1_g, ln1_b, w_ff1, w_ff2, ln2_g, ln2_b, loss_target, m_w_in, m_s5_a_re, m_s5_a_im, m_s5_log_step, m_s5_b_re, m_s5_b_im, m_s5_c_re, m_s5_c_im, m_s5_d, m_s5_w_glu, m_s5_b_glu, m_gla_w_a, m_gla_b_a, m_gla_ln_g, m_swa_sink, m_w_out, m_ln1_g, m_ln1_b, m_w_ff1, m_w_ff2, m_ln2_g, m_ln2_b, v_w_in, v_s5_a_re, v_s5_a_im, v_s5_log_step, v_s5_b_re, v_s5_b_im, v_s5_c_re, v_s5_c_im, v_s5_d, v_s5_w_glu, v_s5_b_glu, v_gla_w_a, v_gla_b_a, v_gla_ln_g, v_swa_sink, v_w_out, v_ln1_g, v_ln1_b, v_w_ff1, v_w_ff2, v_ln2_g, v_ln2_b):
    given = dict(locals())
    w = {k: given[k] for k in _WEIGHTS}
    mom = {k: given["m_" + k] for k in _WEIGHTS}
    var = {k: given["v_" + k] for k in _WEIGHTS}

    me = 2 * lax.axis_index("x") + lax.axis_index("y")

    groups = {0: (("w_in",), ("s5_w_glu", "w_out"), ("w_ff1", "w_ff2")), 1: (BIG,)}
    gathers, token = {}, 0.0
    for l in range(DEPTH):
        for names in groups[l]:
            st = _push_start(f"gather_start_{l}_{names[0]}", [w[k][l].astype(MX) for k in names], False)
            token = token + st[-1]
            for k in names:
                gathers[l, k] = [names, st, None]

    def fetch(l, name, after):
        names, st, got = gathers[l, name]
        if got is None:
            if l == 0 and name == "w_in":
                after = token
            srcs, lands = _push_wait(f"gather_wait_{l}_{names[0]}", st, after, False)
            got = {k: lax.dynamic_update_slice_in_dim(ld, src[None], me, 0) for k, ld, src in zip(names, lands, srcs)}
            for k in names:
                gathers[l, k][2] = got
        full = got[name]
        if name == "w_in":
            return _in_cols(_unshard_cols(full))
        if name == "s5_w_glu":
            glu = _unshard_cols(full)
            return glu[:, :256], glu[:, 256:]
        return full.reshape(D, D) if name == "w_out" else full

    scatters = []

    def emit(l, grads):
        names = tuple(grads)
        st = _push_start(f"scatter_start_{l}_{names[0]}", [grads[k] for k in names], True)
        scatters.append((l, names, st))
        return st[-1][0, 0]

    qs = [_layer_prep({k: w[k][l] for k in SMALL}) for l in range(DEPTH)]
    tq, tk = _rope_tables(512), _rope_tables(128)
    loss, dx, smalls = _local_step(x.reshape(N, D), loss_target.reshape(N, D), qs, tq, tk, fetch, emit)
    loss = lax.psum(loss, ("x", "y", "c"))

    recv, own = {}, {}
    for l, names, st in scatters:
        srcs, lands = _push_wait(f"scatter_wait_{l}_{names[0]}", st, dx, True)
        for k, ld, src in zip(names, lands, srcs):
            recv[l, k], own[l, k] = ld, src
    me1 = me.astype(jnp.int32).reshape(1)
    sums = [_sum_sources(me1, [recv[l, k] for l in range(DEPTH)], [own[l, k] for l in range(DEPTH)]) for k in BIG]
    others = _swap_sibling(sums)

    out = {}
    for k, mine, other in zip(BIG, sums, others):
        shp = w[k].shape
        res = _adamw([mine, other], *(t[k].reshape(-1, shp[-1]) for t in (w, mom, var)))
        out[k] = [r.reshape(shp) for r in res]

    rows = 2304
    gsmall = _allreduce_small(_pack([jnp.stack([smalls[l][k] for l in range(DEPTH)]) for k in SMALL], rows))
    res = _adamw([gsmall], *(_pack([t[k] for k in SMALL], rows) for t in (w, mom, var)))
    for k, vals in zip(SMALL, zip(*(_unpack(r, [w[k] for k in SMALL]) for r in res))):
        out[k] = list(vals)

    return (loss, dx.reshape(NSEQ, L, D), *[out[k][0] for k in _WEIGHTS], *[out[k][1] for k in _WEIGHTS],
            *[out[k][2] for k in _WEIGHTS], *[out[k][3] for k in _WEIGHTS])
```

```python
import functools
import math

import jax
import jax.numpy as jnp
from jax import lax
from jax.experimental import pallas as pl
from jax.experimental.pallas import tpu as pltpu

F32 = jnp.float32
MX = jnp.bfloat16
MESH = pl.DeviceIdType.MESH

DEPTH = 2
NSEQ = 2
L = 2048
N = NSEQ * L
D = 1024
DFF = 4096
NSHARD = 4
S5_G, S5_H, S5_P = 16, 16, 64
GLA_CHUNK = 64
NCHUNK = L // GLA_CHUNK
SWA_BLK = 128
NBLK = L // SWA_BLK
ROT = 16
ROPE_THETA = 500000.0
LN_EPS = 1e-5
ALPHA = (2 * DEPTH) ** 0.25
NEG_BIG = -1e30
DIN = 1824
DINP = 1920
ADAM_LR, ADAM_B1, ADAM_B2, ADAM_EPS, ADAM_WD, ADAM_STEP = 0.001, 0.9, 0.999, 1e-08, 0.01, 10
VMEM_LIMIT = 56 * 1024 * 1024
TT = 512
SW = 512


def _cp(sem, vmem=VMEM_LIMIT):
    return pltpu.CompilerParams(dimension_semantics=sem, vmem_limit_bytes=vmem)


def _mm(a, b):
    return jnp.dot(a.astype(MX), b.astype(MX), preferred_element_type=F32)


def _mm_nt(a, b):
    return lax.dot_general(a.astype(MX), b.astype(MX), (((1,), (1,)), ((), ())), preferred_element_type=F32)


def _mm_tn(a, b):
    return lax.dot_general(a.astype(MX), b.astype(MX), (((0,), (0,)), ((), ())), preferred_element_type=F32)


@jax.custom_vjp
def _dmm(a, b):
    return _mm(a, b)


_dmm.defvjp(lambda a, b: (_mm(a, b), (a, b)), lambda r, g: (_mm_nt(g, r[1]), _mm_tn(r[0], g)))


@jax.custom_vjp
def _dmm_nt(a, b):
    return _mm_nt(a, b)


_dmm_nt.defvjp(lambda a, b: (_mm_nt(a, b), (a, b)), lambda r, g: (_mm(g, r[1]), _mm_tn(g, r[0])))


@jax.custom_vjp
def _dmm_tn(a, b):
    return _mm_tn(a, b)


_dmm_tn.defvjp(lambda a, b: (_mm_tn(a, b), (a, b)), lambda r, g: (_mm_nt(r[1], g), _mm(r[0], g)))


def _split3(x):
    hi = x.astype(MX)
    r1 = x - hi.astype(F32)
    mid = r1.astype(MX)
    lo = (r1 - mid.astype(F32)).astype(MX)
    return hi, mid, lo


def _tri(rev):
    r = lax.broadcasted_iota(jnp.int32, (GLA_CHUNK, GLA_CHUNK), 0)
    c = lax.broadcasted_iota(jnp.int32, (GLA_CHUNK, GLA_CHUNK), 1)
    return jnp.where((c >= r) if rev else (c <= r), 1.0, 0.0).astype(MX)


def _cums_impl(x, rev):
    t = _tri(rev)
    return sum(jnp.dot(t, p, preferred_element_type=F32) for p in _split3(x))


@functools.partial(jax.custom_vjp, nondiff_argnums=(1,))
def _cums(x, rev):
    return _cums_impl(x, rev)


_cums.defvjp(lambda x, rev: (_cums_impl(x, rev), None), lambda rev, r, g: (_cums_impl(g, not rev),))


def _ln_fwd(s, g, b):
    mu = jnp.mean(s, axis=-1, keepdims=True)
    xc = s - mu
    var = jnp.mean(xc * xc, axis=-1, keepdims=True)
    return xc * lax.rsqrt(var + LN_EPS) * g + b


def _ln_bwd(dy, s, g):
    mu = jnp.mean(s, axis=-1, keepdims=True)
    xc = s - mu
    var = jnp.mean(xc * xc, axis=-1, keepdims=True)
    rstd = lax.rsqrt(var + LN_EPS)
    xhat = xc * rstd
    dxh = dy * g
    ds = rstd * (dxh - jnp.mean(dxh, axis=-1, keepdims=True) - xhat * jnp.mean(dxh * xhat, axis=-1, keepdims=True))
    return ds, jnp.sum(dy * xhat, axis=0, keepdims=True), jnp.sum(dy, axis=0, keepdims=True)


def _sds(shape, dtype=F32):
    return jax.ShapeDtypeStruct(shape, dtype)


def _inproj_fwd(x, w):
    tm = 512

    def body(x_ref, w_ref, h_ref):
        h_ref[...] = _mm(x_ref[...], w_ref[...])

    return pl.pallas_call(
        body, grid=(N // tm,),
        in_specs=[pl.BlockSpec((tm, D), lambda i: (i, 0)), pl.BlockSpec((D, DINP), lambda i: (0, 0))],
        out_specs=pl.BlockSpec((tm, DINP), lambda i: (i, 0)),
        out_shape=_sds((N, DINP)), name="inproj_fwd", compiler_params=_cp(("parallel",)))(x, w)


def _inproj_bwd(x, w, dxp, du2, dud, gq_f, gq_b, gk_f, gk_b, gv_f, gv_b, gr, daq, dakv, dhl):
    tm = 256
    nt = N // tm

    def body(x_ref, w_ref, dxp_ref, du2_ref, dud_ref, gqf, gqb, gkf, gkb, gvf, gvb, gr_ref, daq_ref, dakv_ref, dhl_ref,
             dx_ref, dw_ref):
        i = pl.program_id(0)
        dh = jnp.concatenate([
            du2_ref[0] + du2_ref[1] + dud_ref[...], gqf[...] + gqb[...], gkf[...] + gkb[...], gvf[...] + gvb[...],
            gr_ref[...], daq_ref[...], dakv_ref[...], dhl_ref[...]], axis=1)
        dx_ref[...] = dxp_ref[...] + _mm_nt(dh, w_ref[...])
        contrib = _mm_tn(x_ref[...], dh)

        @pl.when(i == 0)
        def _():
            dw_ref[...] = contrib

        @pl.when(i > 0)
        def _():
            dw_ref[...] += contrib

    row = lambda w_: pl.BlockSpec((tm, w_), lambda i: (i, 0))
    return pl.pallas_call(
        body, grid=(nt,),
        in_specs=[row(D), pl.BlockSpec((D, DINP), lambda i: (0, 0)), row(D),
                  pl.BlockSpec((2, tm, 256), lambda i: (0, i, 0)), row(256), row(128), row(128), row(128), row(128),
                  row(256), row(256), row(256), row(512), row(256), row(128)],
        out_specs=[row(D), pl.BlockSpec((D, DINP), lambda i: (0, 0))],
        out_shape=[_sds((N, D)), _sds((D, DINP))],
        name="inproj_bwd", compiler_params=_cp(("arbitrary",)))(
            x, w, dxp, du2, dud, gq_f, gq_b, gk_f, gk_b, gv_f, gv_b, gr, daq, dakv, dhl)


def _scan_tables(mr, mi, reverse):
    pw = [(mr, mi)]
    for _ in range(7):
        pr, pi = pw[-1]
        pw.append((pr * mr - pi * mi, pr * mi + pi * mr))
    rows = jnp.arange(8)[:, None]
    out = []
    for d in (1, 2, 4):
        keep = rows >= d
        out += [jnp.where(keep, pw[d - 1][0][None], 0.0), jnp.where(keep, pw[d - 1][1][None], 0.0)]
    out += [jnp.stack([p[0] for p in pw]), jnp.stack([p[1] for p in pw])]
    t = jnp.stack(out)
    if reverse:
        t = t[:, ::-1, :]
    return t.reshape(8, 8, 2, SW).transpose(2, 0, 1, 3)


def _tile_scan(xr, xi, a, cr, ci, reverse):
    for lvl, d in enumerate((1, 2, 4)):
        sh = 8 - d if reverse else d
        sr = pltpu.roll(xr, sh, 0)
        si = pltpu.roll(xi, sh, 0)
        ar, ai = a[2 * lvl], a[2 * lvl + 1]
        xr, xi = xr + ar * sr - ai * si, xi + ar * si + ai * sr
    pr, pi = a[6], a[7]
    return xr + pr * cr - pi * ci, xi + pr * ci + pi * cr


def _s5_time_block(z, s, t, adjoint):
    flip = (1 - z) if adjoint else z
    return s * (L // TT) + t + flip * (L // TT - 1 - 2 * t)


def _s5_fwd(h, bre, bim, cre, cim, tab):
    nt = L // TT

    def body(u_ref, bre_ref, bim_ref, cre_ref, cim_ref, tab_ref, hre_ref, him_ref, y_ref, car):
        z = pl.program_id(1)
        tc = pl.program_id(3)

        @pl.when(tc == 0)
        def _():
            car[...] = jnp.zeros_like(car)

        u = u_ref[...]
        hre_ref[0] = _mm(u, bre_ref[0, 0])
        him_ref[0] = _mm(u, bim_ref[0, 0])

        def run(reverse):
            a = [tab_ref[0, 0, k] for k in range(8)]

            def step(i, carry):
                cr, ci = carry
                r0 = pl.multiple_of((TT // 8 - 1 - i if reverse else i) * 8, 8)
                xr, xi = _tile_scan(hre_ref[0, pl.ds(r0, 8), :], him_ref[0, pl.ds(r0, 8), :], a, cr, ci, reverse)
                hre_ref[0, pl.ds(r0, 8), :] = xr
                him_ref[0, pl.ds(r0, 8), :] = xi
                row = 0 if reverse else 7
                return (jnp.broadcast_to(xr[row:row + 1, :], (8, SW)), jnp.broadcast_to(xi[row:row + 1, :], (8, SW)))

            cr, ci = lax.fori_loop(0, TT // 8, step, (car[0], car[1]), unroll=4)
            car[0] = cr
            car[1] = ci

        @pl.when(z == 0)
        def _():
            run(False)

        @pl.when(z == 1)
        def _():
            run(True)

        y_ref[0] = _mm(hre_ref[0], cre_ref[0, 0]) - _mm(him_ref[0], cim_ref[0, 0])

    tb = lambda b, z, s, t: _s5_time_block(z, s, t, False)
    wspec = lambda r, c: pl.BlockSpec((1, 1, r, c), lambda b, z, s, t: (z, b, 0, 0))
    return pl.pallas_call(
        body, grid=(2, 2, NSEQ, nt),
        in_specs=[pl.BlockSpec((TT, 128), lambda b, z, s, t: (tb(b, z, s, t), b)),
                  wspec(128, SW), wspec(128, SW), wspec(SW, 128), wspec(SW, 128),
                  pl.BlockSpec((1, 1, 8, 8, SW), lambda b, z, s, t: (z, b, 0, 0, 0))],
        out_specs=[pl.BlockSpec((1, TT, SW), lambda b, z, s, t: (z, tb(b, z, s, t), b)),
                   pl.BlockSpec((1, TT, SW), lambda b, z, s, t: (z, tb(b, z, s, t), b)),
                   pl.BlockSpec((1, TT, 128), lambda b, z, s, t: (z, tb(b, z, s, t), b))],
        out_shape=[_sds((2, N, 2 * SW)), _sds((2, N, 2 * SW)), _sds((2, N, 256))],
        scratch_shapes=[pltpu.VMEM((2, 8, SW), F32)],
        name="s5_fwd", compiler_params=_cp(("arbitrary",) * 4))(h, bre, bim, cre, cim, tab)


def _s5_bwd(h, dyp, hre, him, bre, bim, cre, cim, tabc):
    nt = L // TT

    def body(u_ref, dy_ref, hre_ref, him_ref, bre_ref, bim_ref, cre_ref, cim_ref, tab_ref,
             du_ref, dbre_ref, dbim_ref, dcre_ref, dcim_ref, dmu_ref, gre, gim, car):
        z = pl.program_id(1)
        s = pl.program_id(2)
        tc = pl.program_id(3)

        @pl.when(tc == 0)
        def _():
            car[...] = jnp.zeros_like(car)

        @pl.when((tc == 0) & (s == 0))
        def _():
            dbre_ref[...] = jnp.zeros_like(dbre_ref)
            dbim_ref[...] = jnp.zeros_like(dbim_ref)
            dcre_ref[...] = jnp.zeros_like(dcre_ref)
            dcim_ref[...] = jnp.zeros_like(dcim_ref)
            dmu_ref[...] = jnp.zeros_like(dmu_ref)

        dy = dy_ref[...]
        gre[...] = _mm_nt(dy, cre_ref[0, 0])
        gim[...] = -_mm_nt(dy, cim_ref[0, 0])
        rowid = lax.broadcasted_iota(jnp.int32, (8, SW), 0)

        def run(reverse):
            a = [tab_ref[0, 0, k] for k in range(8)]
            first = 7 if reverse else 0

            def step(i, carry):
                cr, ci, dmr, dmi = carry
                r0 = pl.multiple_of((TT // 8 - 1 - i if reverse else i) * 8, 8)
                xr, xi = _tile_scan(gre[pl.ds(r0, 8), :], gim[pl.ds(r0, 8), :], a, cr, ci, reverse)
                gre[pl.ds(r0, 8), :] = xr
                gim[pl.ds(r0, 8), :] = xi
                sh = 7 if reverse else 1
                gpr = jnp.where(rowid == first, cr, pltpu.roll(xr, sh, 0))
                gpi = jnp.where(rowid == first, ci, pltpu.roll(xi, sh, 0))
                hr = hre_ref[0, pl.ds(r0, 8), :]
                hi = him_ref[0, pl.ds(r0, 8), :]
                dmr = dmr + gpr * hr + gpi * hi
                dmi = dmi + gpi * hr - gpr * hi
                row = 0 if reverse else 7
                return (jnp.broadcast_to(xr[row:row + 1, :], (8, SW)), jnp.broadcast_to(xi[row:row + 1, :], (8, SW)),
                        dmr, dmi)

            cr, ci, dmr, dmi = lax.fori_loop(0, TT // 8, step, (car[0], car[1], dmu_ref[0, 0, 0], dmu_ref[0, 0, 1]),
                                             unroll=4)
            car[0] = cr
            car[1] = ci
            dmu_ref[0, 0, 0] = dmr
            dmu_ref[0, 0, 1] = dmi

        @pl.when(z == 0)
        def _():
            run(True)

        @pl.when(z == 1)
        def _():
            run(False)

        gr = gre[...]
        gi = gim[...]
        u = u_ref[...]
        du_ref[0] = _mm_nt(gr, bre_ref[0, 0]) + _mm_nt(gi, bim_ref[0, 0])
        dbre_ref[0, 0] += _mm_tn(u, gr)
        dbim_ref[0, 0] += _mm_tn(u, gi)
        dcre_ref[0, 0] += _mm_tn(hre_ref[0], dy)
        dcim_ref[0, 0] -= _mm_tn(him_ref[0], dy)

    tb = lambda b, z, s, t: _s5_time_block(z, s, t, True)
    wspec = lambda r, c: pl.BlockSpec((1, 1, r, c), lambda b, z, s, t: (z, b, 0, 0))
    tok = lambda w_: pl.BlockSpec((TT, w_), lambda b, z, s, t: (tb(b, z, s, t), b))
    st = pl.BlockSpec((1, TT, SW), lambda b, z, s, t: (z, tb(b, z, s, t), b))
    return pl.pallas_call(
        body, grid=(2, 2, NSEQ, nt),
        in_specs=[tok(128), tok(128), st, st, wspec(128, SW), wspec(128, SW), wspec(SW, 128), wspec(SW, 128),
                  pl.BlockSpec((1, 1, 8, 8, SW), lambda b, z, s, t: (z, b, 0, 0, 0))],
        out_specs=[pl.BlockSpec((1, TT, 128), lambda b, z, s, t: (z, tb(b, z, s, t), b)),
                   wspec(128, SW), wspec(128, SW), wspec(SW, 128), wspec(SW, 128),
                   pl.BlockSpec((1, 1, 2, 8, SW), lambda b, z, s, t: (z, b, 0, 0, 0))],
        out_shape=[_sds((2, N, 256)), _sds((2, 2, 128, SW)), _sds((2, 2, 128, SW)), _sds((2, 2, SW, 128)),
                   _sds((2, 2, SW, 128)), _sds((2, 2, 2, 8, SW))],
        scratch_shapes=[pltpu.VMEM((TT, SW), F32), pltpu.VMEM((TT, SW), F32), pltpu.VMEM((2, 8, SW), F32)],
        name="s5_bwd", compiler_params=_cp(("arbitrary",) * 4))(h, dyp, hre, him, bre, bim, cre, cim, tabc)


_GELU_C = math.sqrt(2.0 / math.pi)


def _gelu(y):
    return 0.5 * y * (1.0 + jnp.tanh(_GELU_C * (y + 0.044715 * y * y * y)))


def _gelu_grad(y):
    t = jnp.tanh(_GELU_C * (y + 0.044715 * y * y * y))
    return 0.5 * (1.0 + t) + 0.5 * y * (1.0 - t * t) * _GELU_C * (1.0 + 3 * 0.044715 * y * y)


def _s5_glu_fwd(y2, h, dsk, wv, wg, bv, bg):
    tm = 512

    def body(y2_ref, u_ref, d_ref, wv_ref, wg_ref, bv_ref, bg_ref, ya_ref):
        z = _gelu(y2_ref[0] + y2_ref[1] + d_ref[...] * u_ref[...])
        val = _mm(z, wv_ref[...]) + bv_ref[...]
        gate = _mm(z, wg_ref[...]) + bg_ref[...]
        ya_ref[...] = val * jax.nn.sigmoid(gate)

    full = lambda r, c: pl.BlockSpec((r, c), lambda i: (0, 0))
    return pl.pallas_call(
        body, grid=(N // tm,),
        in_specs=[pl.BlockSpec((2, tm, 256), lambda i: (0, i, 0)), pl.BlockSpec((tm, 256), lambda i: (i, 0)),
                  full(1, 256), full(256, 256), full(256, 256), full(1, 256), full(1, 256)],
        out_specs=pl.BlockSpec((tm, 256), lambda i: (i, 0)),
        out_shape=_sds((N, 256)), name="s5_glu_fwd", compiler_params=_cp(("parallel",)))(y2, h, dsk, wv, wg, bv, bg)


def _s5_glu_bwd(y2, h, dsk, wv, wg, bv, bg, dya):
    tm = 512

    def body(y2_ref, u_ref, d_ref, wv_ref, wg_ref, bv_ref, bg_ref, dya_ref,
             dyp_ref, dud_ref, dd_ref, dwv_ref, dwg_ref, dbv_ref, dbg_ref):
        i = pl.program_id(0)

        @pl.when(i == 0)
        def _():
            for r in (dd_ref, dwv_ref, dwg_ref, dbv_ref, dbg_ref):
                r[...] = jnp.zeros_like(r)

        u = u_ref[...]
        y = y2_ref[0] + y2_ref[1] + d_ref[...] * u
        z = _gelu(y)
        val = _mm(z, wv_ref[...]) + bv_ref[...]
        sig = jax.nn.sigmoid(_mm(z, wg_ref[...]) + bg_ref[...])
        dya = dya_ref[...]
        dval = dya * sig
        dgate = dya * val * sig * (1.0 - sig)
        dz = _mm_nt(dval, wv_ref[...]) + _mm_nt(dgate, wg_ref[...])
        dy = dz * _gelu_grad(y)
        dyp_ref[...] = dy
        dud_ref[...] = dy * d_ref[...]
        dd_ref[...] += jnp.sum(dy * u, axis=0, keepdims=True)
        dwv_ref[...] += _mm_tn(z, dval)
        dwg_ref[...] += _mm_tn(z, dgate)
        dbv_ref[...] += jnp.sum(dval, axis=0, keepdims=True)
        dbg_ref[...] += jnp.sum(dgate, axis=0, keepdims=True)

    full = lambda r, c: pl.BlockSpec((r, c), lambda i: (0, 0))
    row = pl.BlockSpec((tm, 256), lambda i: (i, 0))
    return pl.pallas_call(
        body, grid=(N // tm,),
        in_specs=[pl.BlockSpec((2, tm, 256), lambda i: (0, i, 0)), row,
                  full(1, 256), full(256, 256), full(256, 256), full(1, 256), full(1, 256), row],
        out_specs=[row, row, full(1, 256), full(256, 256), full(256, 256), full(1, 256), full(1, 256)],
        out_shape=[_sds((N, 256)), _sds((N, 256)), _sds((1, 256)), _sds((256, 256)), _sds((256, 256)),
                   _sds((1, 256)), _sds((1, 256))],
        name="s5_glu_bwd", compiler_params=_cp(("arbitrary",)))(y2, h, dsk, wv, wg, bv, bg, dya)


def _logsig(x):
    return jnp.minimum(x, 0.0) - jnp.log(1.0 + jnp.exp(-jnp.abs(x)))


def _gla_gate_fwd(h, wa, ba):
    tm = 512

    def body(hl_ref, wa_ref, ba_ref, la_ref):
        la_ref[...] = _logsig(_mm(hl_ref[...], wa_ref[...]) + ba_ref[...]) * (1.0 / 16.0)

    return pl.pallas_call(
        body, grid=(N // tm,),
        in_specs=[pl.BlockSpec((tm, 128), lambda i: (i, 14)), pl.BlockSpec((128, 256), lambda i: (0, 0)),
                  pl.BlockSpec((1, 256), lambda i: (0, 0))],
        out_specs=pl.BlockSpec((tm, 256), lambda i: (i, 0)),
        out_shape=_sds((N, 256)), name="gla_gate_fwd", compiler_params=_cp(("parallel",)))(h, wa, ba)


def _gla_gate_bwd(h, wa, ba, dla_f, dla_b):
    tm = 512

    def body(hl_ref, wa_ref, ba_ref, df_ref, db_ref, dhl_ref, dwa_ref, dba_ref):
        i = pl.program_id(0)

        @pl.when(i == 0)
        def _():
            dwa_ref[...] = jnp.zeros_like(dwa_ref)
            dba_ref[...] = jnp.zeros_like(dba_ref)

        hl = hl_ref[...]
        pre = _mm(hl, wa_ref[...]) + ba_ref[...]
        dpre = jnp.concatenate([df_ref[...], db_ref[...]], axis=1) * (1.0 / 16.0) * jax.nn.sigmoid(-pre)
        dhl_ref[...] = _mm_nt(dpre, wa_ref[...])
        dwa_ref[...] += _mm_tn(hl, dpre)
        dba_ref[...] += jnp.sum(dpre, axis=0, keepdims=True)

    row = pl.BlockSpec((tm, 128), lambda i: (i, 0))
    return pl.pallas_call(
        body, grid=(N // tm,),
        in_specs=[pl.BlockSpec((tm, 128), lambda i: (i, 14)), pl.BlockSpec((128, 256), lambda i: (0, 0)),
                  pl.BlockSpec((1, 256), lambda i: (0, 0)), row, row],
        out_specs=[row, pl.BlockSpec((128, 256), lambda i: (0, 0)), pl.BlockSpec((1, 256), lambda i: (0, 0))],
        out_shape=[_sds((N, 128)), _sds((128, 256)), _sds((1, 256))],
        name="gla_gate_bwd", compiler_params=_cp(("arbitrary",)))(h, wa, ba, dla_f, dla_b)


def _gla_chunk(q, k, v, la, st, rev):
    c = GLA_CHUNK
    b = _cums(la, rev)
    bl = jnp.sum(la, axis=0, keepdims=True)
    q_in = q * (32.0 ** -0.5) * jnp.exp(b)
    k_in = k * jnp.exp(-b)
    k_st = k * jnp.exp(bl - b)
    lane_k = lax.broadcasted_iota(jnp.int32, (1, 128), 1) // 32
    lane_v = lax.broadcasted_iota(jnp.int32, (1, 256), 1) // 64
    r = lax.broadcasted_iota(jnp.int32, (c, c), 0)
    cc = lax.broadcasted_iota(jnp.int32, (c, c), 1)
    keep = (cc > r) if rev else (cc <= r)
    qs = jnp.concatenate([jnp.where(lane_k == hd, q_in, 0.0) for hd in range(4)], axis=0)
    a = _dmm_nt(qs, k_in)
    a = jnp.where(jnp.concatenate([keep] * 4, axis=0), a, 0.0)
    o4 = _dmm(a, v)
    o = _dmm_nt(q_in, st)
    for hd in range(4):
        o = o + jnp.where(lane_v == hd, o4[hd * c:(hd + 1) * c], 0.0)
    bd = (lax.broadcasted_iota(jnp.int32, (256, 128), 0) // 64) == (lax.broadcasted_iota(jnp.int32, (256, 128), 1) // 32)
    st_new = jnp.exp(bl) * st + jnp.where(bd, _dmm_tn(v, k_st), 0.0)
    return o, st_new


def _gla_rows(s, c, rev):
    return s * NCHUNK + (NCHUNK - 1 - c if rev else c)


def _gla_fwd(h, la2):
    def body(qf, kf, vf, laf, qb, kb, vb, lab, of_ref, ob_ref, sf_ref, sb_ref, stf, stb):
        @pl.when(pl.program_id(1) == 0)
        def _():
            stf[...] = jnp.zeros_like(stf)
            stb[...] = jnp.zeros_like(stb)

        sf_ref[0] = stf[...]
        sb_ref[0] = stb[...]
        o, sn = _gla_chunk(qf[...], kf[...], vf[...], laf[...], stf[...], False)
        of_ref[...] = o
        stf[...] = sn
        o, sn = _gla_chunk(qb[...], kb[...], vb[...], lab[...], stb[...], True)
        ob_ref[...] = o
        stb[...] = sn

    def specs(rev):
        rw = lambda s, c: _gla_rows(s, c, rev)
        return [pl.BlockSpec((64, 128), lambda s, c: (rw(s, c), 2)), pl.BlockSpec((64, 128), lambda s, c: (rw(s, c), 3)),
                pl.BlockSpec((64, 256), lambda s, c: (rw(s, c), 2)),
                pl.BlockSpec((64, 128), lambda s, c: (rw(s, c), 1 if rev else 0))]

    orow = lambda rev: pl.BlockSpec((64, 256), lambda s, c: (_gla_rows(s, c, rev), 0))
    srow = lambda rev: pl.BlockSpec((1, 256, 128), lambda s, c: (_gla_rows(s, c, rev), 0, 0))
    return pl.pallas_call(
        body, grid=(NSEQ, NCHUNK),
        in_specs=specs(False) + specs(True),
        out_specs=[orow(False), orow(True), srow(False), srow(True)],
        out_shape=[_sds((N, 256)), _sds((N, 256)), _sds((NSEQ * NCHUNK, 256, 128)), _sds((NSEQ * NCHUNK, 256, 128))],
        scratch_shapes=[pltpu.VMEM((256, 128), F32), pltpu.VMEM((256, 128), F32)],
        name="gla_fwd", compiler_params=_cp(("arbitrary", "arbitrary")))(h, h, h, la2, h, h, h, la2)


def _gla_bwd(h, la2, do, sf, sb):
    def body(qf, kf, vf, laf, dof, sfr, qb, kb, vb, lab, dob, sbr,
             dqf, dkf, dvf, dlf, dqb, dkb, dvb, dlb, dstf, dstb):
        @pl.when(pl.program_id(1) == 0)
        def _():
            dstf[...] = jnp.zeros_like(dstf)
            dstb[...] = jnp.zeros_like(dstb)

        def one(q, k, v, la, do_, st, dst, rev, dq, dk, dv, dl):
            _, vjp = jax.vjp(functools.partial(_gla_chunk, rev=rev), q[...], k[...], v[...], la[...], st[0])
            gq, gk, gv, gl, gs = vjp((do_[...], dst[...]))
            dq[...] = gq
            dk[...] = gk
            dv[...] = gv
            dl[...] = gl
            dst[...] = gs

        one(qf, kf, vf, laf, dof, sfr, dstf, False, dqf, dkf, dvf, dlf)
        one(qb, kb, vb, lab, dob, sbr, dstb, True, dqb, dkb, dvb, dlb)

    def specs(rev):
        rw = lambda s, c: _gla_rows(s, c, not rev)
        return [pl.BlockSpec((64, 128), lambda s, c: (rw(s, c), 2)), pl.BlockSpec((64, 128), lambda s, c: (rw(s, c), 3)),
                pl.BlockSpec((64, 256), lambda s, c: (rw(s, c), 2)),
                pl.BlockSpec((64, 128), lambda s, c: (rw(s, c), 1 if rev else 0)),
                pl.BlockSpec((64, 256), lambda s, c: (rw(s, c), 0)),
                pl.BlockSpec((1, 256, 128), lambda s, c: (rw(s, c), 0, 0))]

    def ospecs(rev):
        rw = lambda s, c: _gla_rows(s, c, not rev)
        n = pl.BlockSpec((64, 128), lambda s, c: (rw(s, c), 0))
        return [n, n, pl.BlockSpec((64, 256), lambda s, c: (rw(s, c), 0)), n]

    oshape = [_sds((N, 128)), _sds((N, 128)), _sds((N, 256)), _sds((N, 128))]
    return pl.pallas_call(
        body, grid=(NSEQ, NCHUNK),
        in_specs=specs(False) + specs(True),
        out_specs=ospecs(False) + ospecs(True),
        out_shape=oshape + oshape,
        scratch_shapes=[pltpu.VMEM((256, 128), F32), pltpu.VMEM((256, 128), F32)],
        name="gla_bwd", compiler_params=_cp(("arbitrary", "arbitrary")))(h, h, h, la2, do, sf, h, h, h, la2, do, sb)


def _gla_post(of, ob, r, g):
    o = of + ob
    head = lax.broadcasted_iota(jnp.int32, (1, 256), 1) // 64
    mu = jnp.zeros_like(o)
    for hd in range(4):
        mu = mu + jnp.where(head == hd, jnp.sum(jnp.where(head == hd, o, 0.0), axis=-1, keepdims=True) * (1.0 / 64.0), 0.0)
    xc = o - mu
    var = jnp.zeros_like(o)
    for hd in range(4):
        var = var + jnp.where(head == hd, jnp.sum(jnp.where(head == hd, xc * xc, 0.0), axis=-1, keepdims=True) * (1.0 / 64.0), 0.0)
    return xc * lax.rsqrt(var + LN_EPS) * g * (r * jax.nn.sigmoid(r))


def _gla_post_fwd(of, ob, h, g):
    tm = 512

    def body(of_ref, ob_ref, r_ref, g_ref, y_ref):
        y_ref[...] = _gla_post(of_ref[...], ob_ref[...], r_ref[...], g_ref[...])

    row = pl.BlockSpec((tm, 256), lambda i: (i, 0))
    return pl.pallas_call(
        body, grid=(N // tm,),
        in_specs=[row, row, pl.BlockSpec((tm, 256), lambda i: (i, 3)), pl.BlockSpec((1, 256), lambda i: (0, 0))],
        out_specs=row, out_shape=_sds((N, 256)), name="gla_post_fwd", compiler_params=_cp(("parallel",)))(of, ob, h, g)


def _gla_post_bwd(of, ob, h, g, dyb):
    tm = 512

    def body(of_ref, ob_ref, r_ref, g_ref, dy_ref, do_ref, dr_ref, dg_ref):
        @pl.when(pl.program_id(0) == 0)
        def _():
            dg_ref[...] = jnp.zeros_like(dg_ref)

        _, vjp = jax.vjp(_gla_post, of_ref[...], ob_ref[...], r_ref[...], g_ref[...])
        go, _, gr, gg = vjp(dy_ref[...])
        do_ref[...] = go
        dr_ref[...] = gr
        dg_ref[...] += gg

    row = pl.BlockSpec((tm, 256), lambda i: (i, 0))
    one = pl.BlockSpec((1, 256), lambda i: (0, 0))
    return pl.pallas_call(
        body, grid=(N // tm,),
        in_specs=[row, row, pl.BlockSpec((tm, 256), lambda i: (i, 3)), one, row],
        out_specs=[row, row, one], out_shape=[_sds((N, 256)), _sds((N, 256)), _sds((1, 256))],
        name="gla_post_bwd", compiler_params=_cp(("arbitrary",)))(of, ob, h, g, dyb)


def _rope_tables(width):
    pos = jnp.arange(L, dtype=F32)
    inv_freq = ROPE_THETA ** (-jnp.arange(0, ROT, 2, dtype=F32) / ROT)
    ang = pos[:, None] * inv_freq[None, :]
    cos, sin = jnp.cos(ang), jnp.sin(ang)
    one = jnp.ones((L, 64 - ROT), F32)
    zero = jnp.zeros((L, 64 - ROT), F32)
    z8 = jnp.zeros((L, ROT // 2), F32)
    c = jnp.concatenate([cos, cos, one], axis=1)
    sa = jnp.concatenate([z8, sin, zero], axis=1)
    sb = jnp.concatenate([-sin, z8, zero], axis=1)
    rep = width // 64
    return jnp.stack([jnp.tile(c, (1, rep)), jnp.tile(sa, (1, rep)), jnp.tile(sb, (1, rep))])


def _rope(t, tab):
    w = t.shape[-1]
    return t * tab[0] + pltpu.roll(t, ROT // 2, 1) * tab[1] + pltpu.roll(t, w - ROT // 2, 1) * tab[2]


def _rope_t(g, tab):
    w = g.shape[-1]
    return g * tab[0] + pltpu.roll(g * tab[1], w - ROT // 2, 1) + pltpu.roll(g * tab[2], ROT // 2, 1)


def _swa_pad_kv(kv_ref, tk_ref, kpad, vpad):
    z = jnp.zeros((SWA_BLK, 128), F32)
    kpad[0:SWA_BLK] = z
    vpad[0:SWA_BLK] = z
    kpad[SWA_BLK + L:] = z
    vpad[SWA_BLK + L:] = z
    kpad[SWA_BLK:SWA_BLK + L] = _rope(kv_ref[:, 0:128], tk_ref[...])
    vpad[SWA_BLK:SWA_BLK + L] = kv_ref[:, 128:256]


def _swa_expand(x, hk):
    lane = lax.broadcasted_iota(jnp.int32, x.shape, 1)
    sw = pltpu.roll(x, 64, 1)
    pair = jnp.where(lane < 64, x, sw) if hk == 0 else jnp.where(lane < 64, sw, x)
    return jnp.concatenate([pair, pair], axis=1)


def _swa_fold(x, hk):
    a = x[:, 0:128] + x[:, 128:256]
    t = a + pltpu.roll(a, 64, 1)
    lane = lax.broadcasted_iota(jnp.int32, a.shape, 1)
    return jnp.where((lane < 64) if hk == 0 else (lane >= 64), t, 0.0)


def _swa_probs(q2, kexp, n, sink_ref, hk):
    slot = lax.broadcasted_iota(jnp.int32, (1, 256), 1) // 64
    qs = jnp.concatenate([jnp.where(slot == g, q2, 0.0) for g in range(4)], axis=0)
    s = _mm_nt(qs, kexp) * 0.125
    i = lax.broadcasted_iota(jnp.int32, (SWA_BLK, 3 * SWA_BLK), 0)
    j = lax.broadcasted_iota(jnp.int32, (SWA_BLK, 3 * SWA_BLK), 1)
    kpos = n * SWA_BLK - SWA_BLK + j
    ok = (j - i >= 0) & (j - i <= 2 * SWA_BLK) & (kpos >= 0) & (kpos < L)
    s = jnp.where(jnp.concatenate([ok] * 4, axis=0), s, NEG_BIG)
    rowg = lax.broadcasted_iota(jnp.int32, (4 * SWA_BLK, 1), 0) // SWA_BLK
    sink = jnp.zeros((4 * SWA_BLK, 1), F32)
    for g in range(4):
        sink = jnp.where(rowg == g, sink_ref[hk * 4 + g], sink)
    m = jnp.maximum(jnp.max(s, axis=-1, keepdims=True), sink)
    p = jnp.exp(s - m)
    ps = jnp.exp(sink - m)
    inv = 1.0 / (jnp.sum(p, axis=-1, keepdims=True) + ps)
    return qs, p * inv, ps * inv, slot, rowg


def _swa_qtab(tk_ref, r0):
    return [jnp.concatenate([tk_ref[i, pl.ds(r0, SWA_BLK), :]] * 4, axis=1) for i in range(3)]


def _swa_fwd(h, tk, sink):
    def body(sink_ref, q_ref, kv_ref, tk_ref, y_ref, kpad, vpad):
        n = pl.program_id(1)

        @pl.when(n == 0)
        def _():
            _swa_pad_kv(kv_ref, tk_ref, kpad, vpad)

        r0 = pl.multiple_of(n * SWA_BLK, SWA_BLK)
        q = _rope(q_ref[...], _swa_qtab(tk_ref, r0))
        kb = kpad[pl.ds(r0, 3 * SWA_BLK), :]
        vb = vpad[pl.ds(r0, 3 * SWA_BLK), :]
        for hk in range(2):
            _, p, _, slot, _ = _swa_probs(q[:, hk * 256:(hk + 1) * 256], _swa_expand(kb, hk), n, sink_ref, hk)
            o4 = _mm(p, _swa_expand(vb, hk))
            o = jnp.zeros((SWA_BLK, 256), F32)
            for g in range(4):
                o = o + jnp.where(slot == g, o4[g * SWA_BLK:(g + 1) * SWA_BLK], 0.0)
            y_ref[:, hk * 256:(hk + 1) * 256] = o

    return pl.pallas_call(
        body,
        grid_spec=pltpu.PrefetchScalarGridSpec(
            num_scalar_prefetch=1, grid=(NSEQ, NBLK),
            in_specs=[pl.BlockSpec((SWA_BLK, 512), lambda s, n, sk: (s * NBLK + n, 2)),
                      pl.BlockSpec((L, 256), lambda s, n, sk: (s, 6)),
                      pl.BlockSpec((3, L, 128), lambda s, n, sk: (0, 0, 0))],
            out_specs=pl.BlockSpec((SWA_BLK, 512), lambda s, n, sk: (s * NBLK + n, 0)),
            scratch_shapes=[pltpu.VMEM((L + 2 * SWA_BLK, 128), F32), pltpu.VMEM((L + 2 * SWA_BLK, 128), F32)]),
        out_shape=_sds((N, 512)), name="swa_fwd", compiler_params=_cp(("arbitrary", "arbitrary")))(sink, h, h, tk)


def _swa_bwd(h, tk, sink, dyc):
    def body(sink_ref, q_ref, kv_ref, tk_ref, dy_ref, dq_ref, dkv_ref, dsink_ref, kpad, vpad, dkacc, dvacc):
        sq = pl.program_id(0)
        n = pl.program_id(1)

        @pl.when(n == 0)
        def _():
            _swa_pad_kv(kv_ref, tk_ref, kpad, vpad)
            dkacc[...] = jnp.zeros_like(dkacc)
            dvacc[...] = jnp.zeros_like(dvacc)

        @pl.when((n == 0) & (sq == 0))
        def _():
            dsink_ref[...] = jnp.zeros_like(dsink_ref)

        r0 = pl.multiple_of(n * SWA_BLK, SWA_BLK)
        tq = _swa_qtab(tk_ref, r0)
        q = _rope(q_ref[...], tq)
        kb = kpad[pl.ds(r0, 3 * SWA_BLK), :]
        vb = vpad[pl.ds(r0, 3 * SWA_BLK), :]
        dk = jnp.zeros((3 * SWA_BLK, 128), F32)
        dv = jnp.zeros((3 * SWA_BLK, 128), F32)
        hrow = lax.broadcasted_iota(jnp.int32, (8, 128), 0)
        dsk = jnp.zeros((8, 128), F32)
        for hk in range(2):
            kexp = _swa_expand(kb, hk)
            vexp = _swa_expand(vb, hk)
            qs, p, ps, slot, rowg = _swa_probs(q[:, hk * 256:(hk + 1) * 256], kexp, n, sink_ref, hk)
            dy2 = dy_ref[:, hk * 256:(hk + 1) * 256]
            dos = jnp.concatenate([jnp.where(slot == g, dy2, 0.0) for g in range(4)], axis=0)
            dp = _mm_nt(dos, vexp)
            delta = jnp.sum(p * dp, axis=-1, keepdims=True)
            ds = p * (dp - delta) * 0.125
            dsr = -ps * delta
            for g in range(4):
                dsk = dsk + jnp.where(hrow == hk * 4 + g, jnp.sum(jnp.where(rowg == g, dsr, 0.0), axis=0, keepdims=True), 0.0)
            dq4 = _mm(ds, kexp)
            dq2 = jnp.zeros((SWA_BLK, 256), F32)
            for g in range(4):
                dq2 = dq2 + jnp.where(slot == g, dq4[g * SWA_BLK:(g + 1) * SWA_BLK], 0.0)
            dq_ref[:, hk * 256:(hk + 1) * 256] = dq2
            dk = dk + _swa_fold(_mm_tn(ds, qs), hk)
            dv = dv + _swa_fold(_mm_tn(p, dos), hk)
        dq_ref[...] = _rope_t(dq_ref[...], tq)
        dkacc[pl.ds(r0, 3 * SWA_BLK), :] += dk
        dvacc[pl.ds(r0, 3 * SWA_BLK), :] += dv
        dsink_ref[...] += dsk

        @pl.when(n == NBLK - 1)
        def _():
            dkv_ref[:, 0:128] = _rope_t(dkacc[SWA_BLK:SWA_BLK + L], tk_ref[...])
            dkv_ref[:, 128:256] = dvacc[SWA_BLK:SWA_BLK + L]

    blk = lambda col: pl.BlockSpec((SWA_BLK, 512), lambda s, n, sk: (s * NBLK + n, col))
    pad = pltpu.VMEM((L + 2 * SWA_BLK, 128), F32)
    return pl.pallas_call(
        body,
        grid_spec=pltpu.PrefetchScalarGridSpec(
            num_scalar_prefetch=1, grid=(NSEQ, NBLK),
            in_specs=[blk(2), pl.BlockSpec((L, 256), lambda s, n, sk: (s, 6)),
                      pl.BlockSpec((3, L, 128), lambda s, n, sk: (0, 0, 0)), blk(0)],
            out_specs=[blk(0), pl.BlockSpec((L, 256), lambda s, n, sk: (s, 0)),
                       pl.BlockSpec((8, 128), lambda s, n, sk: (0, 0))],
            scratch_shapes=[pad, pad, pad, pad]),
        out_shape=[_sds((N, 512)), _sds((N, 256)), _sds((8, 128))],
        name="swa_bwd", compiler_params=_cp(("arbitrary", "arbitrary")))(sink, h, h, tk, dyc)


def _outproj_fwd(ya, yb, yc, x, wo, g, b):
    tm = 512

    def body(ya_ref, yb_ref, yc_ref, x_ref, wo_ref, g_ref, b_ref, s_ref, x1_ref):
        mix = _mm(ya_ref[...], wo_ref[0:256]) + _mm(yb_ref[...], wo_ref[256:512]) + _mm(yc_ref[...], wo_ref[512:1024])
        s = ALPHA * x_ref[...] + mix
        s_ref[...] = s
        x1_ref[...] = _ln_fwd(s, g_ref[...], b_ref[...])

    row = lambda w_: pl.BlockSpec((tm, w_), lambda i: (i, 0))
    one = pl.BlockSpec((1, D), lambda i: (0, 0))
    return pl.pallas_call(
        body, grid=(N // tm,),
        in_specs=[row(256), row(256), row(512), row(D), pl.BlockSpec((D, D), lambda i: (0, 0)), one, one],
        out_specs=[row(D), row(D)], out_shape=[_sds((N, D)), _sds((N, D))],
        name="outproj_fwd", compiler_params=_cp(("parallel",)))(ya, yb, yc, x, wo, g, b)


def _outproj_bwd(dx1, s1, ya, yb, yc, wo, g):
    tm = 512
    nt = N // tm

    def body(dx1_ref, s_ref, ya_ref, yb_ref, yc_ref, wo_ref, g_ref,
             dya_ref, dyb_ref, dyc_ref, dxp_ref, dwo_ref, dg_ref, db_ref, acc):
        i = pl.program_id(0)

        @pl.when(i == 0)
        def _():
            acc[...] = jnp.zeros_like(acc)
            dg_ref[...] = jnp.zeros_like(dg_ref)
            db_ref[...] = jnp.zeros_like(db_ref)

        ds, dg, db = _ln_bwd(dx1_ref[...], s_ref[...], g_ref[...])
        dg_ref[...] += dg
        db_ref[...] += db
        dxp_ref[...] = ALPHA * ds
        dy = _mm_nt(ds, wo_ref[...])
        dya_ref[...] = dy[:, 0:256]
        dyb_ref[...] = dy[:, 256:512]
        dyc_ref[...] = dy[:, 512:1024]
        acc[0:256] += _mm_tn(ya_ref[...], ds)
        acc[256:512] += _mm_tn(yb_ref[...], ds)
        acc[512:1024] += _mm_tn(yc_ref[...], ds)

        @pl.when(i == nt - 1)
        def _():
            dwo_ref[...] = acc[...].astype(MX)

    row = lambda w_: pl.BlockSpec((tm, w_), lambda i: (i, 0))
    one = pl.BlockSpec((1, D), lambda i: (0, 0))
    full = pl.BlockSpec((D, D), lambda i: (0, 0))
    return pl.pallas_call(
        body, grid=(nt,),
        in_specs=[row(D), row(D), row(256), row(256), row(512), full, one],
        out_specs=[row(256), row(256), row(512), row(D), full, one, one],
        out_shape=[_sds((N, 256)), _sds((N, 256)), _sds((N, 512)), _sds((N, D)), _sds((D, D), MX), _sds((1, D)), _sds((1, D))],
        scratch_shapes=[pltpu.VMEM((D, D), F32)],
        name="outproj_bwd", compiler_params=_cp(("arbitrary",)))(dx1, s1, ya, yb, yc, wo, g)


def _ffn_fwd(x1, w1, w2, g, b):
    tm = 512

    def body(x_ref, w1_ref, w2_ref, g_ref, b_ref, a_ref, s_ref, x2_ref):
        j = pl.program_id(1)
        a = _mm(x_ref[...], w1_ref[0])
        a_ref[...] = a.astype(MX)
        hid = jnp.square(jnp.maximum(a, 0.0))
        contrib = _mm(hid, w2_ref[0])

        @pl.when(j == 0)
        def _():
            s_ref[...] = ALPHA * x_ref[...] + contrib

        @pl.when(j > 0)
        def _():
            s_ref[...] += contrib

        @pl.when(j == NSHARD - 1)
        def _():
            x2_ref[...] = _ln_fwd(s_ref[...], g_ref[...], b_ref[...])

    row = pl.BlockSpec((tm, D), lambda i, j: (i, 0))
    wj = pl.BlockSpec((1, D, D), lambda i, j: (j, 0, 0))
    one = pl.BlockSpec((1, D), lambda i, j: (0, 0))
    return pl.pallas_call(
        body, grid=(N // tm, NSHARD),
        in_specs=[row, wj, wj, one, one],
        out_specs=[pl.BlockSpec((tm, D), lambda i, j: (i, j)), row, row],
        out_shape=[_sds((N, DFF), MX), _sds((N, D)), _sds((N, D))],
        name="ffn_fwd", compiler_params=_cp(("parallel", "arbitrary")))(x1, w1, w2, g, b)


def _ffn_bwd_act(dy, s2, a, w1, w2, g):
    tm = 512

    def body(dy_ref, s_ref, a_ref, w1_ref, w2_ref, g_ref, da_ref, ds_ref, dx1_ref, dg_ref, db_ref, dsf):
        i = pl.program_id(0)
        j = pl.program_id(1)

        @pl.when((i == 0) & (j == 0))
        def _():
            dg_ref[...] = jnp.zeros_like(dg_ref)
            db_ref[...] = jnp.zeros_like(db_ref)

        @pl.when(j == 0)
        def _():
            ds, dg, db = _ln_bwd(dy_ref[...], s_ref[...], g_ref[...])
            dsf[...] = ds
            ds_ref[...] = ds.astype(MX)
            dg_ref[...] += dg
            db_ref[...] += db
            dx1_ref[...] = ALPHA * ds

        dhid = _mm_nt(dsf[...], w2_ref[0])
        da = dhid * 2.0 * jnp.maximum(a_ref[...].astype(F32), 0.0)
        da_ref[...] = da.astype(MX)
        dx1_ref[...] += _mm_nt(da, w1_ref[0])

    row = pl.BlockSpec((tm, D), lambda i, j: (i, 0))
    col = pl.BlockSpec((tm, D), lambda i, j: (i, j))
    wj = pl.BlockSpec((1, D, D), lambda i, j: (j, 0, 0))
    one = pl.BlockSpec((1, D), lambda i, j: (0, 0))
    return pl.pallas_call(
        body, grid=(N // tm, NSHARD),
        in_specs=[row, row, col, wj, wj, one],
        out_specs=[col, row, row, one, one],
        out_shape=[_sds((N, DFF), MX), _sds((N, D), MX), _sds((N, D)), _sds((1, D)), _sds((1, D))],
        scratch_shapes=[pltpu.VMEM((tm, D), F32)],
        name="ffn_bwd_act", compiler_params=_cp(("arbitrary", "arbitrary")))(dy, s2, a, w1, w2, g)


def _ffn_bwd_w(x1, da, a, ds):
    tm = 512
    nt = N // tm

    def body(x_ref, da_ref, a_ref, ds_ref, dw1_ref, dw2_ref, acc1, acc2):
        i = pl.program_id(1)

        @pl.when(i == 0)
        def _():
            acc1[...] = jnp.zeros_like(acc1)
            acc2[...] = jnp.zeros_like(acc2)

        acc1[...] += _mm_tn(x_ref[...], da_ref[...])
        hid = jnp.square(jnp.maximum(a_ref[...].astype(F32), 0.0))
        acc2[...] += _mm_tn(hid, ds_ref[...])

        @pl.when(i == nt - 1)
        def _():
            dw1_ref[0] = acc1[...].astype(MX)
            dw2_ref[0] = acc2[...].astype(MX)

    row = pl.BlockSpec((tm, D), lambda j, i: (i, 0))
    col = pl.BlockSpec((tm, D), lambda j, i: (i, j))
    wj = pl.BlockSpec((1, D, D), lambda j, i: (j, 0, 0))
    return pl.pallas_call(
        body, grid=(NSHARD, nt),
        in_specs=[row, col, col, row], out_specs=[wj, wj],
        out_shape=[_sds((NSHARD, D, D), MX), _sds((NSHARD, D, D), MX)],
        scratch_shapes=[pltpu.VMEM((D, D), F32), pltpu.VMEM((D, D), F32)],
        name="ffn_bwd_w", compiler_params=_cp(("parallel", "arbitrary")))(x1, da, a, ds)


def _loss_head(y, target):
    tm = 512

    def body(y_ref, t_ref, dy_ref, l_ref):
        @pl.when(pl.program_id(0) == 0)
        def _():
            l_ref[...] = jnp.zeros_like(l_ref)

        e = y_ref[...] - t_ref[...]
        dy_ref[...] = e * (1.0 / D)
        l_ref[...] += jnp.sum(jnp.sum(e * e, axis=1, keepdims=True), axis=0, keepdims=True) * (0.5 / D)

    row = pl.BlockSpec((tm, D), lambda i: (i, 0))
    return pl.pallas_call(
        body, grid=(N // tm,), in_specs=[row, row],
        out_specs=[row, pl.BlockSpec((8, 128), lambda i: (0, 0))],
        out_shape=[_sds((N, D)), _sds((8, 128))], name="loss_head", compiler_params=_cp(("arbitrary",)))(y, target)


def _s5_discretize(a_re, a_im, log_step, b_re, b_im):
    lam = lax.complex(a_re, a_im)
    lam_bar = jnp.exp(lam * jnp.exp(log_step))
    b_bar = ((lam_bar - 1.0) / lam)[..., None] * lax.complex(b_re, b_im)
    return jnp.real(lam_bar), jnp.imag(lam_bar), jnp.real(b_bar), jnp.imag(b_bar)


def _s5_in_blocks(b):
    e = jnp.eye(8, dtype=F32)
    return jnp.einsum('ij,zbjph->zbihjp', e, b.reshape(2, 2, 8, S5_P, S5_H)).reshape(2, 2, 128, SW)


def _s5_in_unblocks(d):
    return jnp.einsum('zbihip->zbiph', d.reshape(2, 2, 8, S5_H, 8, S5_P)).reshape(2, S5_G, S5_P, S5_H)


def _s5_out_blocks(c):
    e = jnp.eye(8, dtype=F32)
    return jnp.einsum('ij,zbjhp->zbjpih', e, c.reshape(2, 2, 8, S5_H, S5_P)).reshape(2, 2, SW, 128)


def _s5_out_unblocks(d):
    return jnp.einsum('zbipih->zbihp', d.reshape(2, 2, 8, S5_P, 8, S5_H)).reshape(2, S5_G, S5_H, S5_P)


def _gate_weight(w_a):
    z = jnp.zeros((16, 128), F32)
    top = jnp.concatenate([w_a[0], z], axis=1)
    bot = jnp.concatenate([z, w_a[1]], axis=1)
    return jnp.concatenate([top, bot, jnp.zeros((96, 256), F32)], axis=0)


def _layer_prep(p):
    lr, li, br, bi = _s5_discretize(p["s5_a_re"], p["s5_a_im"], p["s5_log_step"], p["s5_b_re"], p["s5_b_im"])
    q = dict(p)
    q["bre"] = _s5_in_blocks(br).astype(MX)
    q["bim"] = _s5_in_blocks(bi).astype(MX)
    q["cre"] = _s5_out_blocks(p["s5_c_re"]).astype(MX)
    q["cim"] = _s5_out_blocks(p["s5_c_im"]).astype(MX)
    mr, mi = lr.reshape(2, 1024), li.reshape(2, 1024)
    q["tab"] = jnp.stack([_scan_tables(mr[0], mi[0], False), _scan_tables(mr[1], mi[1], True)])
    q["tabc"] = jnp.stack([_scan_tables(mr[0], -mi[0], True), _scan_tables(mr[1], -mi[1], False)])
    q["dsk"] = p["s5_d"].reshape(1, 256)
    q["wa"] = _gate_weight(p["gla_w_a"]).astype(MX)
    q["ba"] = p["gla_b_a"].reshape(1, 256)
    q["lng"] = p["gla_ln_g"].reshape(1, 256)
    q["bv"] = p["s5_b_glu"][:256].reshape(1, 256)
    q["bg"] = p["s5_b_glu"][256:].reshape(1, 256)
    for k in ("ln1_g", "ln1_b", "ln2_g", "ln2_b"):
        q[k] = p[k].reshape(1, D)
    return q


def _layer_fwd(x, q, tk, fetch):
    q["w_in"] = fetch("w_in", x)
    h = _inproj_fwd(x, q["w_in"])
    hre, him, y2 = _s5_fwd(h, q["bre"], q["bim"], q["cre"], q["cim"], q["tab"])
    q["wv"], q["wg"] = fetch("s5_w_glu", y2)
    ya = _s5_glu_fwd(y2, h, q["dsk"], q["wv"], q["wg"], q["bv"], q["bg"])
    la2 = _gla_gate_fwd(h, q["wa"], q["ba"])
    of, ob, sf, sb = _gla_fwd(h, la2)
    yb = _gla_post_fwd(of, ob, h, q["lng"])
    yc = _swa_fwd(h, tk, q["swa_sink"])
    q["w_out"] = fetch("w_out", yc)
    s1, x1 = _outproj_fwd(ya, yb, yc, x, q["w_out"], q["ln1_g"], q["ln1_b"])
    q["w_ff1"] = fetch("w_ff1", x1)
    q["w_ff2"] = fetch("w_ff2", x1)
    a, s2, x2 = _ffn_fwd(x1, q["w_ff1"], q["w_ff2"], q["ln2_g"], q["ln2_b"])
    saved = dict(x=x, h=h, hre=hre, him=him, y2=y2, ya=ya, la2=la2, of=of, ob=ob, sf=sf, sb=sb, yb=yb, yc=yc,
                 s1=s1, x1=x1, a=a, s2=s2)
    return x2, saved


def _layer_bwd(dy, q, sv, tk, emit):
    g = {}
    da, ds2, dx1, g["ln2_g"], g["ln2_b"] = _ffn_bwd_act(dy, sv["s2"], sv["a"], q["w_ff1"], q["w_ff2"], q["ln2_g"])
    dw1, dw2 = _ffn_bwd_w(sv["x1"], da, sv["a"], ds2)
    tie = emit(dict(w_ff1=dw1, w_ff2=dw2))
    dya, dyb, dyc, dxp, dwo, g["ln1_g"], g["ln1_b"] = _outproj_bwd(dx1, sv["s1"], sv["ya"], sv["yb"], sv["yc"],
                                                                     q["w_out"], q["ln1_g"] + tie)
    h = sv["h"]
    daq, dakv, dsink = _swa_bwd(h, tk, q["swa_sink"], dyc)
    g["swa_sink"] = dsink[:, 0]
    do, gr, dlng = _gla_post_bwd(sv["of"], sv["ob"], h, q["lng"], dyb)
    g["gla_ln_g"] = dlng.reshape(256)
    gq_f, gk_f, gv_f, gl_f, gq_b, gk_b, gv_b, gl_b = _gla_bwd(h, sv["la2"], do, sv["sf"], sv["sb"])
    dhl, dwa, dba = _gla_gate_bwd(h, q["wa"], q["ba"], gl_f, gl_b)
    g["gla_w_a"] = jnp.stack([dwa[0:16, 0:128], dwa[16:32, 128:256]])
    g["gla_b_a"] = dba.reshape(2, 128)
    dyp, dud, dd, dwv, dwg, dbv, dbg = _s5_glu_bwd(sv["y2"], h, q["dsk"], q["wv"], q["wg"], q["bv"], q["bg"], dya)
    g["s5_d"] = dd.reshape(S5_G, S5_H)
    g["s5_b_glu"] = jnp.concatenate([dbv, dbg], axis=1).reshape(512)
    tie = emit(dict(w_out=dwo.reshape(NSHARD, D // NSHARD, D),
                    s5_w_glu=_shard_cols(jnp.concatenate([dwv, dwg], axis=1)).astype(MX)))
    du2, dbre, dbim, dcre, dcim, dmu = _s5_bwd(h, dyp, sv["hre"], sv["him"], q["bre"], q["bim"], q["cre"], q["cim"],
                                               q["tabc"] + tie)
    g["s5_c_re"] = _s5_out_unblocks(dcre)
    g["s5_c_im"] = _s5_out_unblocks(dcim)
    dmu = jnp.sum(dmu, axis=3)
    dlr = dmu[:, :, 0].reshape(2, S5_G, S5_P)
    dli = dmu[:, :, 1].reshape(2, S5_G, S5_P)
    _, vjp = jax.vjp(_s5_discretize, q["s5_a_re"], q["s5_a_im"], q["s5_log_step"], q["s5_b_re"], q["s5_b_im"])
    (g["s5_a_re"], g["s5_a_im"], g["s5_log_step"], g["s5_b_re"], g["s5_b_im"]) = vjp(
        (dlr, dli, _s5_in_unblocks(dbre), _s5_in_unblocks(dbim)))
    dx, dwin = _inproj_bwd(sv["x"], q["w_in"], dxp, du2, dud, gq_f, gq_b, gk_f, gk_b, gv_f, gv_b, gr, daq, dakv, dhl)
    tie = emit(dict(w_in=_shard_cols(_in_cols_back(dwin)).astype(MX)))
    return dx, g, tie


def _local_step(x, target, qs, tk, fetch, emit):
    saved = []
    for l, q in enumerate(qs):
        x, sv = _layer_fwd(x, q, tk, functools.partial(fetch, l))
        saved.append(sv)
    dy, lacc = _loss_head(x, target)
    smalls = [None] * DEPTH
    tie = 0.0
    for l in reversed(range(DEPTH)):
        qs[l]["ln2_g"] = qs[l]["ln2_g"] + tie
        dy, smalls[l], tie = _layer_bwd(dy, qs[l], saved[l], tk, functools.partial(emit, l))
    return lacc[0, 0], dy, smalls


BIG = ("w_in", "s5_w_glu", "w_out", "w_ff1", "w_ff2")
SMALL = ("s5_a_re", "s5_a_im", "s5_log_step", "s5_b_re", "s5_b_im", "s5_c_re", "s5_c_im", "s5_d", "s5_b_glu",
         "gla_w_a", "gla_b_a", "gla_ln_g", "swa_sink", "ln1_g", "ln1_b", "ln2_g", "ln2_b")
ANY = pl.BlockSpec(memory_space=pl.ANY)


def _place():
    x, y, c = lax.axis_index("x"), lax.axis_index("y"), lax.axis_index("c")
    return x, y, c, [(1 - x, y), (x, 1 - y), (1 - x, 1 - y)]


HBM = pl.BlockSpec(memory_space=pltpu.HBM)
SEMS = pl.BlockSpec(memory_space=pltpu.SEMAPHORE)
EFFECT = pltpu.SideEffectType.DATAFLOW_SIDE_EFFECTING


def _push_copies(ins, lands, send, recv, by_shard, sending):
    x, y, c, chips = _place()
    me = 2 * x + y
    out = []
    for a in range(len(ins)):
        for j, (px, py) in enumerate(chips):
            peer = 2 * px + py
            if sending:
                src, dst = (ins[a].at[peer] if by_shard else ins[a]), lands[a].at[me]
            else:
                src, dst = (ins[a].at[me] if by_shard else ins[a]), lands[a].at[peer]
            out.append(pltpu.make_async_remote_copy(src_ref=src, dst_ref=dst, send_sem=send.at[3 * a + j],
                                                    recv_sem=recv.at[3 * a + j], device_id=(px, py, c),
                                                    device_id_type=MESH))
    return out


def _push_start(name, srcs, by_shard):
    n = len(srcs)
    lands = [lax.empty(s.shape if by_shard else (NSHARD,) + s.shape, s.dtype) for s in srcs]

    def body(*refs):
        ins, lnd = refs[:n], refs[n:2 * n]
        send, recv = refs[2 * n], refs[2 * n + 1]
        for cp in _push_copies(ins, lnd, send, recv, by_shard, True):
            cp.start()
        refs[-1][...] = jnp.zeros((8, 128), F32)

    ops = [pltpu.with_memory_space_constraint(t, pltpu.HBM) for t in list(srcs) + lands]
    res = pl.pallas_call(
        body, name=name,
        out_shape=(pltpu.SemaphoreType.DMA((3 * n,)), pltpu.SemaphoreType.DMA((3 * n,)),
                   *[pltpu.HBM(t.shape, t.dtype) for t in ops], _sds((8, 128))),
        in_specs=[HBM] * (2 * n),
        out_specs=(SEMS, SEMS, *[HBM] * (2 * n), pl.BlockSpec(memory_space=pltpu.VMEM)),
        input_output_aliases={i: 2 + i for i in range(2 * n)},
        compiler_params=pltpu.CompilerParams(has_side_effects=EFFECT))(*ops)
    return res[0], res[1], list(res[2:2 + n]), list(res[2 + n:2 + 2 * n]), res[-1]


def _push_wait(name, started, after, by_shard):
    send, recv, srcs, lands, _ = started
    n = len(srcs)

    def body(*refs):
        ins, lnd = refs[:n], refs[n:2 * n]
        for cp in _push_copies(ins, lnd, refs[2 * n], refs[2 * n + 1], by_shard, False):
            cp.wait_send()
            cp.wait_recv()

    res = pl.pallas_call(
        body, name=name,
        out_shape=[pltpu.HBM(t.shape, t.dtype) for t in srcs + lands],
        in_specs=[HBM] * (2 * n) + [SEMS, SEMS, ANY], out_specs=[HBM] * (2 * n),
        input_output_aliases={i: i for i in range(2 * n)},
        compiler_params=pltpu.CompilerParams(has_side_effects=EFFECT))(*srcs, *lands, send, recv, after)
    return list(res[:n]), list(res[n:])


def _sum_sources(me, recv, own):
    _, rows, cols = recv[0].shape
    tr = min(rows, 256)
    nt = rows // tr

    def body(me_ref, *refs):
        o_ref = refs[-1]
        for l in range(DEPTH):
            @pl.when(pl.program_id(0) == l)
            def _():
                r_ref, own_ref = refs[2 * l], refs[2 * l + 1]
                part = [jnp.where(me_ref[0] == s, own_ref[0], r_ref[s]).astype(F32) for s in range(NSHARD)]
                o_ref[...] = ((part[0] + part[1]) + part[2]) + part[3]

    in_specs = []
    for l in range(DEPTH):
        pick = lambda g, i, me_, l=l: jnp.where(g == l, i, jnp.where(g < l, 0, nt - 1))
        in_specs += [pl.BlockSpec((NSHARD, tr, cols), lambda g, i, me_, pick=pick: (0, pick(g, i, me_), 0)),
                     pl.BlockSpec((1, tr, cols), lambda g, i, me_, pick=pick: (me_[0], pick(g, i, me_), 0))]
    return pl.pallas_call(
        body,
        grid_spec=pltpu.PrefetchScalarGridSpec(
            num_scalar_prefetch=1, grid=(DEPTH, nt), in_specs=in_specs,
            out_specs=pl.BlockSpec((tr, cols), lambda g, i, me_: (g * nt + i, 0))),
        out_shape=_sds((DEPTH * rows, cols)), name="sum_sources",
        compiler_params=_cp(("arbitrary", "arbitrary")))(me, *[t for l in range(DEPTH) for t in (recv[l], own[l])])


def _swap_sibling(arrs):
    n = len(arrs)

    def body(*refs):
        ins, outs = refs[:n], refs[n:2 * n]
        send, recv = refs[2 * n:]
        x, y, c, _ = _place()
        cps = [pltpu.make_async_remote_copy(src_ref=ins[a], dst_ref=outs[a], send_sem=send.at[a], recv_sem=recv.at[a],
                                            device_id=(x, y, 1 - c), device_id_type=MESH) for a in range(n)]
        for cp in cps:
            cp.start()
        for cp in cps:
            cp.wait()

    return pl.pallas_call(
        body, in_specs=[ANY] * n, out_specs=[ANY] * n, out_shape=[_sds(a.shape, a.dtype) for a in arrs],
        scratch_shapes=[pltpu.SemaphoreType.DMA((n,)), pltpu.SemaphoreType.DMA((n,))],
        name="swap_sibling")(*arrs)


def _allreduce_small(v):
    rows = v.shape[0]

    def body(v_ref, o_ref, sib, slots, send, recv):
        x, y, c, chips = _place()
        me = 2 * x + y
        d2d = pltpu.make_async_remote_copy(src_ref=v_ref, dst_ref=sib, send_sem=send.at[0], recv_sem=recv.at[0],
                                           device_id=(x, y, 1 - c), device_id_type=MESH)
        d2d.start()
        d2d.wait()
        slots[me] = v_ref[...] + sib[...]

        def remote(j, slot):
            px, py = chips[j]
            return pltpu.make_async_remote_copy(src_ref=slots.at[me], dst_ref=slots.at[slot], send_sem=send.at[1 + j],
                                                recv_sem=recv.at[1 + j], device_id=(px, py, c), device_id_type=MESH)

        sends = [remote(j, me) for j in range(3)]
        for cp in sends:
            cp.start()
        for j in range(3):
            remote(j, 2 * chips[j][0] + chips[j][1]).wait_recv()
        for cp in sends:
            cp.wait_send()
        o_ref[...] = ((slots[0] + slots[1]) + slots[2]) + slots[3]

    vm = pl.BlockSpec(memory_space=pltpu.VMEM)
    return pl.pallas_call(
        body, in_specs=[vm], out_specs=vm, out_shape=_sds((rows, 128)),
        scratch_shapes=[pltpu.VMEM((rows, 128), F32), pltpu.VMEM((NSHARD, rows, 128), F32),
                        pltpu.SemaphoreType.DMA((4,)), pltpu.SemaphoreType.DMA((4,))],
        name="allreduce_small", compiler_params=pltpu.CompilerParams(vmem_limit_bytes=VMEM_LIMIT))(v)


def _adamw_math(w, g, m, v):
    m = ADAM_B1 * m + (1.0 - ADAM_B1) * g
    v = ADAM_B2 * v + (1.0 - ADAM_B2) * jnp.square(g)
    m_hat = m / (1.0 - ADAM_B1 ** ADAM_STEP)
    v_hat = v / (1.0 - ADAM_B2 ** ADAM_STEP)
    delta = -ADAM_LR * (m_hat / (jnp.sqrt(v_hat) + ADAM_EPS) + ADAM_WD * w)
    return delta, m, v


def _adamw(g_parts, w, m, v):
    rows, cols = w.shape
    tr = 256 if rows % 256 == 0 else rows
    k = len(g_parts)

    def body(*refs):
        g = refs[0][...]
        for r in refs[1:k]:
            g = g + r[...]
        w_ref, m_ref, v_ref, go, do, mo, vo = refs[k:]
        d, mn, vn = _adamw_math(w_ref[...], g, m_ref[...], v_ref[...])
        go[...] = g
        do[...] = d
        mo[...] = mn
        vo[...] = vn

    spec = pl.BlockSpec((tr, cols), lambda i: (i, 0))
    return pl.pallas_call(
        body, grid=(rows // tr,), in_specs=[spec] * (k + 3), out_specs=[spec] * 4,
        out_shape=[_sds((rows, cols))] * 4, name="adamw", compiler_params=_cp(("parallel",)))(*g_parts, w, m, v)


def _pack(arrs, rows):
    flat = jnp.concatenate([a.reshape(-1) for a in arrs])
    return jnp.pad(flat, (0, rows * 128 - flat.shape[0])).reshape(rows, 128)


def _unpack(p, like):
    flat = p.reshape(-1)
    out, o = [], 0
    for a in like:
        out.append(flat[o:o + a.size].reshape(a.shape))
        o += a.size
    return out


_ARGS = ("x", "w_in", "s5_a_re", "s5_a_im", "s5_log_step", "s5_b_re", "s5_b_im", "s5_c_re", "s5_c_im", "s5_d",
         "s5_w_glu", "s5_b_glu", "gla_w_a", "gla_b_a", "gla_ln_g", "swa_sink", "w_out", "ln1_g", "ln1_b", "w_ff1",
         "w_ff2", "ln2_g", "ln2_b")
_WEIGHTS = _ARGS[1:]


def _in_cols(w):
    return jnp.concatenate([w[:, 0:1024], w[:, 1056:DIN], w[:, 1024:1056], jnp.zeros((D, DINP - DIN), w.dtype)], axis=1)


def _in_cols_back(d):
    return jnp.concatenate([d[:, 0:1024], d[:, 1792:1824], d[:, 1024:1792]], axis=1)


def _unshard_cols(g):
    return g.transpose(1, 0, 2).reshape(g.shape[1], NSHARD * g.shape[2])


def _shard_cols(d):
    return d.reshape(d.shape[0], NSHARD, d.shape[1] // NSHARD).transpose(1, 0, 2)


def kernel(x, w_in, s5_a_re, s5_a_im, s5_log_step, s5_b_re, s5_b_im, s5_c_re, s5_c_im, s5_d, s5_w_glu, s5_b_glu, gla_w_a, gla_b_a, gla_ln_g, swa_sink, w_out, ln1_g, ln1_b, w_ff1, w_ff2, ln2_g, ln2_b, loss_target, m_w_in, m_s5_a_re, m_s5_a_im, m_s5_log_step, m_s5_b_re, m_s5_b_im, m_s5_c_re, m_s5_c_im, m_s5_d, m_s5_w_glu, m_s5_b_glu, m_gla_w_a, m_gla_b_a, m_gla_ln_g, m_swa_sink, m_w_out, m_ln1_g, m_ln1_b, m_w_ff1, m_w_ff2, m_ln2_g, m_ln2_b, v_w_in, v_s5_a_re, v_s5_a_im, v_s5_log_step, v_s5_b_re, v_s5_b_im, v_s5_c_re, v_s5_c_im, v_s5_d, v_s5_w_glu, v_s5_b_glu, v_gla_w_a, v_gla_b_a, v_gla_ln_g, v_swa_sink, v_w_out, v_ln1_g, v_ln1_b, v_w_ff1, v_w_ff2, v_ln2_g, v_ln2_b):
    given = dict(locals())
    w = {k: given[k] for k in _WEIGHTS}
    mom = {k: given["m_" + k] for k in _WEIGHTS}
    var = {k: given["v_" + k] for k in _WEIGHTS}

    me = 2 * lax.axis_index("x") + lax.axis_index("y")

    first = ("w_in", "s5_w_glu", "w_out")
    follow = {(0, "w_in"): [(0, BIG[3:]), (1, first)], (0, "w_ff1"): [(1, BIG[3:])]}
    gathers = {}

    def start_gather(l, names, behind=None):
        srcs = [w[k][l].astype(MX) for k in names]
        if behind is not None:
            srcs, behind = lax.optimization_barrier((srcs, behind))
        st = _push_start(f"gather_start_{l}_{names[0]}", srcs, False)
        for k in names:
            gathers[l, k] = [names, st, None]
        return st[-1], behind

    token = start_gather(0, first[:1])[0] + start_gather(0, first[1:])[0]

    def fetch(l, name, after):
        names, st, got = gathers[l, name]
        if got is None:
            if l == 0 and name == "w_in":
                after = token
            srcs, lands = _push_wait(f"gather_wait_{l}_{names[0]}", st, after, False)
            tie = None
            for l2, names2 in follow.get((l, name), ()):
                tok, lands[0] = start_gather(l2, names2, lands[0])
                tie = tok[0, 0] if tie is None else tie + tok[0, 0]
            if tie is not None:
                srcs = [s + tie.astype(MX) for s in srcs]
            got = {k: lax.dynamic_update_slice_in_dim(ld, src[None], me, 0) for k, ld, src in zip(names, lands, srcs)}
            for k in names:
                gathers[l, k][2] = got
        full = got[name]
        if name == "w_in":
            return _in_cols(_unshard_cols(full))
        if name == "s5_w_glu":
            glu = _unshard_cols(full)
            return glu[:, :256], glu[:, 256:]
        return full.reshape(D, D) if name == "w_out" else full

    scatters = []

    def emit(l, grads):
        names = tuple(grads)
        st = _push_start(f"scatter_start_{l}_{names[0]}", [grads[k] for k in names], True)
        scatters.append((l, names, st))
        return st[-1][0, 0]

    qs = [_layer_prep({k: w[k][l] for k in SMALL}) for l in range(DEPTH)]
    loss, dx, smalls = _local_step(x.reshape(N, D), loss_target.reshape(N, D), qs, _rope_tables(128), fetch, emit)
    loss = lax.psum(loss, ("x", "y", "c"))

    recv, own = {}, {}
    for l, names, st in scatters:
        srcs, lands = _push_wait(f"scatter_wait_{l}_{names[0]}", st, dx, True)
        for k, ld, src in zip(names, lands, srcs):
            recv[l, k], own[l, k] = ld, src
    me1 = me.astype(jnp.int32).reshape(1)
    sums = [_sum_sources(me1, [recv[l, k] for l in range(DEPTH)], [own[l, k] for l in range(DEPTH)]) for k in BIG]
    others = _swap_sibling(sums)

    out = {}
    for k, mine, other in zip(BIG, sums, others):
        shp = w[k].shape
        res = _adamw([mine, other], *(t[k].reshape(-1, shp[-1]) for t in (w, mom, var)))
        out[k] = [r.reshape(shp) for r in res]

    rows = 2304
    gsmall = _allreduce_small(_pack([jnp.stack([smalls[l][k] for l in range(DEPTH)]) for k in SMALL], rows))
    res = _adamw([gsmall], *(_pack([t[k] for k in SMALL], rows) for t in (w, mom, var)))
    for k, vals in zip(SMALL, zip(*(_unpack(r, [w[k] for k in SMALL]) for r in res))):
        out[k] = list(vals)

    return (loss, dx.reshape(NSEQ, L, D), *[out[k][0] for k in _WEIGHTS], *[out[k][1] for k in _WEIGHTS],
            *[out[k][2] for k in _WEIGHTS], *[out[k][3] for k in _WEIGHTS])
```

```python
import functools
import math

import jax
import jax.numpy as jnp
from jax import lax
from jax.experimental import pallas as pl
from jax.experimental.pallas import tpu as pltpu

F32 = jnp.float32
MX = jnp.bfloat16
MESH = pl.DeviceIdType.MESH

DEPTH = 2
NSEQ = 2
L = 2048
N = NSEQ * L
D = 1024
DFF = 4096
NSHARD = 4
S5_G, S5_H, S5_P = 16, 16, 64
GLA_CHUNK = 64
NCHUNK = L // GLA_CHUNK
SWA_BLK = 128
NBLK = L // SWA_BLK
ROT = 16
ROPE_THETA = 500000.0
LN_EPS = 1e-5
ALPHA = (2 * DEPTH) ** 0.25
NEG_BIG = -1e30
DIN = 1824
DINP = 1920
ADAM_LR, ADAM_B1, ADAM_B2, ADAM_EPS, ADAM_WD, ADAM_STEP = 0.001, 0.9, 0.999, 1e-08, 0.01, 10
VMEM_LIMIT = 56 * 1024 * 1024
TT = 512
SW = 512


def _cp(sem, vmem=VMEM_LIMIT):
    return pltpu.CompilerParams(dimension_semantics=sem, vmem_limit_bytes=vmem)


def _mm(a, b):
    return jnp.dot(a.astype(MX), b.astype(MX), preferred_element_type=F32)


def _mm_nt(a, b):
    return lax.dot_general(a.astype(MX), b.astype(MX), (((1,), (1,)), ((), ())), preferred_element_type=F32)


def _mm_tn(a, b):
    return lax.dot_general(a.astype(MX), b.astype(MX), (((0,), (0,)), ((), ())), preferred_element_type=F32)


@jax.custom_vjp
def _dmm(a, b):
    return _mm(a, b)


_dmm.defvjp(lambda a, b: (_mm(a, b), (a, b)), lambda r, g: (_mm_nt(g, r[1]), _mm_tn(r[0], g)))


@jax.custom_vjp
def _dmm_nt(a, b):
    return _mm_nt(a, b)


_dmm_nt.defvjp(lambda a, b: (_mm_nt(a, b), (a, b)), lambda r, g: (_mm(g, r[1]), _mm_tn(g, r[0])))


@jax.custom_vjp
def _dmm_tn(a, b):
    return _mm_tn(a, b)


_dmm_tn.defvjp(lambda a, b: (_mm_tn(a, b), (a, b)), lambda r, g: (_mm_nt(r[1], g), _mm(r[0], g)))


def _split3(x):
    hi = x.astype(MX)
    r1 = x - hi.astype(F32)
    mid = r1.astype(MX)
    lo = (r1 - mid.astype(F32)).astype(MX)
    return hi, mid, lo


def _tri(rev):
    r = lax.broadcasted_iota(jnp.int32, (GLA_CHUNK, GLA_CHUNK), 0)
    c = lax.broadcasted_iota(jnp.int32, (GLA_CHUNK, GLA_CHUNK), 1)
    return jnp.where((c >= r) if rev else (c <= r), 1.0, 0.0).astype(MX)


def _cums_impl(x, rev):
    t = _tri(rev)
    return sum(jnp.dot(t, p, preferred_element_type=F32) for p in _split3(x))


@functools.partial(jax.custom_vjp, nondiff_argnums=(1,))
def _cums(x, rev):
    return _cums_impl(x, rev)


_cums.defvjp(lambda x, rev: (_cums_impl(x, rev), None), lambda rev, r, g: (_cums_impl(g, not rev),))


def _ln_fwd(s, g, b):
    mu = jnp.mean(s, axis=-1, keepdims=True)
    xc = s - mu
    var = jnp.mean(xc * xc, axis=-1, keepdims=True)
    return xc * lax.rsqrt(var + LN_EPS) * g + b


def _ln_bwd(dy, s, g):
    mu = jnp.mean(s, axis=-1, keepdims=True)
    xc = s - mu
    var = jnp.mean(xc * xc, axis=-1, keepdims=True)
    rstd = lax.rsqrt(var + LN_EPS)
    xhat = xc * rstd
    dxh = dy * g
    ds = rstd * (dxh - jnp.mean(dxh, axis=-1, keepdims=True) - xhat * jnp.mean(dxh * xhat, axis=-1, keepdims=True))
    return ds, jnp.sum(dy * xhat, axis=0, keepdims=True), jnp.sum(dy, axis=0, keepdims=True)


def _sds(shape, dtype=F32):
    return jax.ShapeDtypeStruct(shape, dtype)


def _inproj_fwd(x, wt):
    tm = 512

    def body(x_ref, w_ref, h_ref):
        h_ref[...] = _mm_nt(x_ref[...], w_ref[...])

    return pl.pallas_call(
        body, grid=(N // tm,),
        in_specs=[pl.BlockSpec((tm, D), lambda i: (i, 0)), pl.BlockSpec((DINP, D), lambda i: (0, 0))],
        out_specs=pl.BlockSpec((tm, DINP), lambda i: (i, 0)),
        out_shape=_sds((N, DINP)), name="inproj_fwd", compiler_params=_cp(("parallel",)))(x, wt)


def _inproj_bwd(x, w, dxp, du2, dud, gq_f, gq_b, gk_f, gk_b, gv_f, gv_b, gr, daq, dakv, dhl):
    tm = 256
    nt = N // tm

    def body(x_ref, w_ref, dxp_ref, du2_ref, dud_ref, gqf, gqb, gkf, gkb, gvf, gvb, gr_ref, daq_ref, dakv_ref, dhl_ref,
             dx_ref, dw_ref):
        i = pl.program_id(0)
        dh = jnp.concatenate([
            du2_ref[0] + du2_ref[1] + dud_ref[...], gqf[...] + gqb[...], gkf[...] + gkb[...], gvf[...] + gvb[...],
            gr_ref[...], daq_ref[...], dakv_ref[...], dhl_ref[...]], axis=1)
        dx_ref[...] = dxp_ref[...] + _mm(dh, w_ref[...])
        contrib = _mm_tn(dh, x_ref[...])

        @pl.when(i == 0)
        def _():
            dw_ref[...] = contrib

        @pl.when(i > 0)
        def _():
            dw_ref[...] += contrib

    row = lambda w_: pl.BlockSpec((tm, w_), lambda i: (i, 0))
    return pl.pallas_call(
        body, grid=(nt,),
        in_specs=[row(D), pl.BlockSpec((DINP, D), lambda i: (0, 0)), row(D),
                  pl.BlockSpec((2, tm, 256), lambda i: (0, i, 0)), row(256), row(128), row(128), row(128), row(128),
                  row(256), row(256), row(256), row(512), row(256), row(128)],
        out_specs=[row(D), pl.BlockSpec((DINP, D), lambda i: (0, 0))],
        out_shape=[_sds((N, D)), _sds((DINP, D))],
        name="inproj_bwd", compiler_params=_cp(("arbitrary",)))(
            x, w, dxp, du2, dud, gq_f, gq_b, gk_f, gk_b, gv_f, gv_b, gr, daq, dakv, dhl)


def _scan_tables(mr, mi, reverse):
    pw = [(mr, mi)]
    for _ in range(7):
        pr, pi = pw[-1]
        pw.append((pr * mr - pi * mi, pr * mi + pi * mr))
    rows = jnp.arange(8)[:, None]
    out = []
    for d in (1, 2, 4):
        keep = rows >= d
        out += [jnp.where(keep, pw[d - 1][0][None], 0.0), jnp.where(keep, pw[d - 1][1][None], 0.0)]
    out += [jnp.stack([p[0] for p in pw]), jnp.stack([p[1] for p in pw])]
    t = jnp.stack(out)
    if reverse:
        t = t[:, ::-1, :]
    return t.reshape(8, 8, 2, SW).transpose(2, 0, 1, 3)


def _tile_scan(xr, xi, a, cr, ci, reverse):
    for lvl, d in enumerate((1, 2, 4)):
        sh = 8 - d if reverse else d
        sr = pltpu.roll(xr, sh, 0)
        si = pltpu.roll(xi, sh, 0)
        ar, ai = a[2 * lvl], a[2 * lvl + 1]
        xr, xi = xr + ar * sr - ai * si, xi + ar * si + ai * sr
    pr, pi = a[6], a[7]
    return xr + pr * cr - pi * ci, xi + pr * ci + pi * cr


def _s5_time_block(z, s, t, adjoint):
    flip = (1 - z) if adjoint else z
    return s * (L // TT) + t + flip * (L // TT - 1 - 2 * t)


def _s5_fwd(h, bre, bim, cre, cim, tab):
    nt = L // TT

    def body(u_ref, bre_ref, bim_ref, cre_ref, cim_ref, tab_ref, hre_ref, him_ref, y_ref, car):
        z = pl.program_id(1)
        tc = pl.program_id(3)

        @pl.when(tc == 0)
        def _():
            car[...] = jnp.zeros_like(car)

        u = u_ref[...]
        hre_ref[0] = _mm(u, bre_ref[0, 0])
        him_ref[0] = _mm(u, bim_ref[0, 0])

        def run(reverse):
            a = [tab_ref[0, 0, k] for k in range(8)]

            def step(i, carry):
                cr, ci = carry
                r0 = pl.multiple_of((TT // 8 - 1 - i if reverse else i) * 8, 8)
                xr, xi = _tile_scan(hre_ref[0, pl.ds(r0, 8), :], him_ref[0, pl.ds(r0, 8), :], a, cr, ci, reverse)
                hre_ref[0, pl.ds(r0, 8), :] = xr
                him_ref[0, pl.ds(r0, 8), :] = xi
                row = 0 if reverse else 7
                return (jnp.broadcast_to(xr[row:row + 1, :], (8, SW)), jnp.broadcast_to(xi[row:row + 1, :], (8, SW)))

            cr, ci = lax.fori_loop(0, TT // 8, step, (car[0], car[1]), unroll=4)
            car[0] = cr
            car[1] = ci

        @pl.when(z == 0)
        def _():
            run(False)

        @pl.when(z == 1)
        def _():
            run(True)

        y_ref[0] = _mm(hre_ref[0], cre_ref[0, 0]) - _mm(him_ref[0], cim_ref[0, 0])

    tb = lambda b, z, s, t: _s5_time_block(z, s, t, False)
    wspec = lambda r, c: pl.BlockSpec((1, 1, r, c), lambda b, z, s, t: (z, b, 0, 0))
    return pl.pallas_call(
        body, grid=(2, 2, NSEQ, nt),
        in_specs=[pl.BlockSpec((TT, 128), lambda b, z, s, t: (tb(b, z, s, t), b)),
                  wspec(128, SW), wspec(128, SW), wspec(SW, 128), wspec(SW, 128),
                  pl.BlockSpec((1, 1, 8, 8, SW), lambda b, z, s, t: (z, b, 0, 0, 0))],
        out_specs=[pl.BlockSpec((1, TT, SW), lambda b, z, s, t: (z, tb(b, z, s, t), b)),
                   pl.BlockSpec((1, TT, SW), lambda b, z, s, t: (z, tb(b, z, s, t), b)),
                   pl.BlockSpec((1, TT, 128), lambda b, z, s, t: (z, tb(b, z, s, t), b))],
        out_shape=[_sds((2, N, 2 * SW)), _sds((2, N, 2 * SW)), _sds((2, N, 256))],
        scratch_shapes=[pltpu.VMEM((2, 8, SW), F32)],
        name="s5_fwd", compiler_params=_cp(("arbitrary",) * 4))(h, bre, bim, cre, cim, tab)


def _s5_bwd(h, dyp, hre, him, bre, bim, cre, cim, tabc):
    nt = L // TT

    def body(u_ref, dy_ref, hre_ref, him_ref, bre_ref, bim_ref, cre_ref, cim_ref, tab_ref,
             du_ref, dbre_ref, dbim_ref, dcre_ref, dcim_ref, dmu_ref, gre, gim, car, acc):
        z = pl.program_id(1)
        s = pl.program_id(2)
        tc = pl.program_id(3)

        @pl.when(tc == 0)
        def _():
            car[...] = jnp.zeros_like(car)

        @pl.when((tc == 0) & (s == 0))
        def _():
            acc[...] = jnp.zeros_like(acc)
            dmu_ref[...] = jnp.zeros_like(dmu_ref)

        dy = dy_ref[...]
        gre[...] = _mm_nt(dy, cre_ref[0, 0])
        gim[...] = -_mm_nt(dy, cim_ref[0, 0])
        rowid = lax.broadcasted_iota(jnp.int32, (8, SW), 0)

        def run(reverse):
            a = [tab_ref[0, 0, k] for k in range(8)]
            first = 7 if reverse else 0

            def step(i, carry):
                cr, ci, dmr, dmi = carry
                r0 = pl.multiple_of((TT // 8 - 1 - i if reverse else i) * 8, 8)
                xr, xi = _tile_scan(gre[pl.ds(r0, 8), :], gim[pl.ds(r0, 8), :], a, cr, ci, reverse)
                gre[pl.ds(r0, 8), :] = xr
                gim[pl.ds(r0, 8), :] = xi
                sh = 7 if reverse else 1
                gpr = jnp.where(rowid == first, cr, pltpu.roll(xr, sh, 0))
                gpi = jnp.where(rowid == first, ci, pltpu.roll(xi, sh, 0))
                hr = hre_ref[0, pl.ds(r0, 8), :]
                hi = him_ref[0, pl.ds(r0, 8), :]
                dmr = dmr + gpr * hr + gpi * hi
                dmi = dmi + gpi * hr - gpr * hi
                row = 0 if reverse else 7
                return (jnp.broadcast_to(xr[row:row + 1, :], (8, SW)), jnp.broadcast_to(xi[row:row + 1, :], (8, SW)),
                        dmr, dmi)

            cr, ci, dmr, dmi = lax.fori_loop(0, TT // 8, step, (car[0], car[1], dmu_ref[0, 0, 0], dmu_ref[0, 0, 1]),
                                             unroll=4)
            car[0] = cr
            car[1] = ci
            dmu_ref[0, 0, 0] = dmr
            dmu_ref[0, 0, 1] = dmi

        @pl.when(z == 0)
        def _():
            run(True)

        @pl.when(z == 1)
        def _():
            run(False)

        gr = gre[...]
        gi = gim[...]
        u = u_ref[...]
        du_ref[0] = _mm_nt(gr, bre_ref[0, 0]) + _mm_nt(gi, bim_ref[0, 0])
        acc[0] += _mm_tn(u, gr)
        acc[1] += _mm_tn(u, gi)
        acc[2] += _mm_tn(dy, hre_ref[0])
        acc[3] -= _mm_tn(dy, him_ref[0])

        @pl.when((tc == nt - 1) & (s == NSEQ - 1))
        def _():
            grp = lax.broadcasted_iota(jnp.int32, (S5_H, SW), 1) // S5_P
            for k, out in enumerate((dbre_ref, dbim_ref, dcre_ref, dcim_ref)):
                c = jnp.zeros((S5_H, SW), F32)
                for i in range(8):
                    c = c + jnp.where(grp == i, acc[k, i * S5_H:(i + 1) * S5_H, :], 0.0)
                out[0, 0] = c

    tb = lambda b, z, s, t: _s5_time_block(z, s, t, True)
    wspec = lambda r, c: pl.BlockSpec((1, 1, r, c), lambda b, z, s, t: (z, b, 0, 0))
    tok = lambda w_: pl.BlockSpec((TT, w_), lambda b, z, s, t: (tb(b, z, s, t), b))
    st = pl.BlockSpec((1, TT, SW), lambda b, z, s, t: (z, tb(b, z, s, t), b))
    return pl.pallas_call(
        body, grid=(2, 2, NSEQ, nt),
        in_specs=[tok(128), tok(128), st, st, wspec(128, SW), wspec(128, SW), wspec(SW, 128), wspec(SW, 128),
                  pl.BlockSpec((1, 1, 8, 8, SW), lambda b, z, s, t: (z, b, 0, 0, 0))],
        out_specs=[pl.BlockSpec((1, TT, 128), lambda b, z, s, t: (z, tb(b, z, s, t), b)),
                   wspec(S5_H, SW), wspec(S5_H, SW), wspec(S5_H, SW), wspec(S5_H, SW),
                   pl.BlockSpec((1, 1, 2, 8, SW), lambda b, z, s, t: (z, b, 0, 0, 0))],
        out_shape=[_sds((2, N, 256))] + [_sds((2, 2, S5_H, SW))] * 4 + [_sds((2, 2, 2, 8, SW))],
        scratch_shapes=[pltpu.VMEM((TT, SW), F32), pltpu.VMEM((TT, SW), F32), pltpu.VMEM((2, 8, SW), F32),
                        pltpu.VMEM((4, 128, SW), F32)],
        name="s5_bwd", compiler_params=_cp(("arbitrary",) * 4))(h, dyp, hre, him, bre, bim, cre, cim, tabc)


_GELU_C = math.sqrt(2.0 / math.pi)


def _gelu(y):
    return 0.5 * y * (1.0 + jnp.tanh(_GELU_C * (y + 0.044715 * y * y * y)))


def _gelu_grad(y):
    t = jnp.tanh(_GELU_C * (y + 0.044715 * y * y * y))
    return 0.5 * (1.0 + t) + 0.5 * y * (1.0 - t * t) * _GELU_C * (1.0 + 3 * 0.044715 * y * y)


def _glu_halves(w4_ref):
    return (jnp.concatenate([w4_ref[0], w4_ref[1]], axis=1), jnp.concatenate([w4_ref[2], w4_ref[3]], axis=1))


def _s5_glu_fwd(y2, h, dsk, w4, bv, bg):
    tm = 512

    def body(y2_ref, u_ref, d_ref, w4_ref, bv_ref, bg_ref, ya_ref):
        wv, wg = _glu_halves(w4_ref)
        z = _gelu(y2_ref[0] + y2_ref[1] + d_ref[...] * u_ref[...])
        val = _mm(z, wv) + bv_ref[...]
        gate = _mm(z, wg) + bg_ref[...]
        ya_ref[...] = val * jax.nn.sigmoid(gate)

    full = lambda r, c: pl.BlockSpec((r, c), lambda i: (0, 0))
    return pl.pallas_call(
        body, grid=(N // tm,),
        in_specs=[pl.BlockSpec((2, tm, 256), lambda i: (0, i, 0)), pl.BlockSpec((tm, 256), lambda i: (i, 0)),
                  full(1, 256), pl.BlockSpec((NSHARD, 256, 128), lambda i: (0, 0, 0)), full(1, 256), full(1, 256)],
        out_specs=pl.BlockSpec((tm, 256), lambda i: (i, 0)),
        out_shape=_sds((N, 256)), name="s5_glu_fwd", compiler_params=_cp(("parallel",)))(y2, h, dsk, w4, bv, bg)


def _s5_glu_bwd(y2, h, dsk, w4, bv, bg, dya):
    tm = 512
    nt = N // tm

    def body(y2_ref, u_ref, d_ref, w4_ref, bv_ref, bg_ref, dya_ref,
             dyp_ref, dud_ref, dd_ref, dw4_ref, dbv_ref, dbg_ref, accv, accg):
        i = pl.program_id(0)

        @pl.when(i == 0)
        def _():
            for r in (dd_ref, accv, accg, dbv_ref, dbg_ref):
                r[...] = jnp.zeros_like(r)

        wv, wg = _glu_halves(w4_ref)
        u = u_ref[...]
        y = y2_ref[0] + y2_ref[1] + d_ref[...] * u
        z = _gelu(y)
        val = _mm(z, wv) + bv_ref[...]
        sig = jax.nn.sigmoid(_mm(z, wg) + bg_ref[...])
        dya = dya_ref[...]
        dval = dya * sig
        dgate = dya * val * sig * (1.0 - sig)
        dz = _mm_nt(dval, wv) + _mm_nt(dgate, wg)
        dy = dz * _gelu_grad(y)
        dyp_ref[...] = dy
        dud_ref[...] = dy * d_ref[...]
        dd_ref[...] += jnp.sum(dy * u, axis=0, keepdims=True)
        accv[...] += _mm_tn(z, dval)
        accg[...] += _mm_tn(z, dgate)
        dbv_ref[...] += jnp.sum(dval, axis=0, keepdims=True)
        dbg_ref[...] += jnp.sum(dgate, axis=0, keepdims=True)

        @pl.when(i == nt - 1)
        def _():
            dw4_ref[0] = accv[:, 0:128].astype(MX)
            dw4_ref[1] = accv[:, 128:256].astype(MX)
            dw4_ref[2] = accg[:, 0:128].astype(MX)
            dw4_ref[3] = accg[:, 128:256].astype(MX)

    full = lambda r, c: pl.BlockSpec((r, c), lambda i: (0, 0))
    row = pl.BlockSpec((tm, 256), lambda i: (i, 0))
    wspec = pl.BlockSpec((NSHARD, 256, 128), lambda i: (0, 0, 0))
    return pl.pallas_call(
        body, grid=(nt,),
        in_specs=[pl.BlockSpec((2, tm, 256), lambda i: (0, i, 0)), row, full(1, 256), wspec, full(1, 256), full(1, 256),
                  row],
        out_specs=[row, row, full(1, 256), wspec, full(1, 256), full(1, 256)],
        out_shape=[_sds((N, 256)), _sds((N, 256)), _sds((1, 256)), _sds((NSHARD, 256, 128), MX), _sds((1, 256)),
                   _sds((1, 256))],
        scratch_shapes=[pltpu.VMEM((256, 256), F32), pltpu.VMEM((256, 256), F32)],
        name="s5_glu_bwd", compiler_params=_cp(("arbitrary",)))(y2, h, dsk, w4, bv, bg, dya)


def _logsig(x):
    return jnp.minimum(x, 0.0) - jnp.log(1.0 + jnp.exp(-jnp.abs(x)))


def _gla_gate_fwd(h, wa, ba):
    tm = 512

    def body(hl_ref, wa_ref, ba_ref, la_ref):
        la_ref[...] = _logsig(_mm(hl_ref[...], wa_ref[...]) + ba_ref[...]) * (1.0 / 16.0)

    return pl.pallas_call(
        body, grid=(N // tm,),
        in_specs=[pl.BlockSpec((tm, 128), lambda i: (i, 14)), pl.BlockSpec((128, 256), lambda i: (0, 0)),
                  pl.BlockSpec((1, 256), lambda i: (0, 0))],
        out_specs=pl.BlockSpec((tm, 256), lambda i: (i, 0)),
        out_shape=_sds((N, 256)), name="gla_gate_fwd", compiler_params=_cp(("parallel",)))(h, wa, ba)


def _gla_gate_bwd(h, wa, ba, dla_f, dla_b):
    tm = 512

    def body(hl_ref, wa_ref, ba_ref, df_ref, db_ref, dhl_ref, dwa_ref, dba_ref):
        i = pl.program_id(0)

        @pl.when(i == 0)
        def _():
            dwa_ref[...] = jnp.zeros_like(dwa_ref)
            dba_ref[...] = jnp.zeros_like(dba_ref)

        hl = hl_ref[...]
        pre = _mm(hl, wa_ref[...]) + ba_ref[...]
        dpre = jnp.concatenate([df_ref[...], db_ref[...]], axis=1) * (1.0 / 16.0) * jax.nn.sigmoid(-pre)
        dhl_ref[...] = _mm_nt(dpre, wa_ref[...])
        dwa_ref[...] += _mm_tn(hl, dpre)
        dba_ref[...] += jnp.sum(dpre, axis=0, keepdims=True)

    row = pl.BlockSpec((tm, 128), lambda i: (i, 0))
    return pl.pallas_call(
        body, grid=(N // tm,),
        in_specs=[pl.BlockSpec((tm, 128), lambda i: (i, 14)), pl.BlockSpec((128, 256), lambda i: (0, 0)),
                  pl.BlockSpec((1, 256), lambda i: (0, 0)), row, row],
        out_specs=[row, pl.BlockSpec((128, 256), lambda i: (0, 0)), pl.BlockSpec((1, 256), lambda i: (0, 0))],
        out_shape=[_sds((N, 128)), _sds((128, 256)), _sds((1, 256))],
        name="gla_gate_bwd", compiler_params=_cp(("arbitrary",)))(h, wa, ba, dla_f, dla_b)


def _gla_chunk(q, k, v, la, st, rev):
    c = GLA_CHUNK
    b = _cums(la, rev)
    bl = jnp.sum(la, axis=0, keepdims=True)
    q_in = q * (32.0 ** -0.5) * jnp.exp(b)
    k_in = k * jnp.exp(-b)
    k_st = k * jnp.exp(bl - b)
    lane_k = lax.broadcasted_iota(jnp.int32, (1, 128), 1) // 32
    lane_v = lax.broadcasted_iota(jnp.int32, (1, 256), 1) // 64
    r = lax.broadcasted_iota(jnp.int32, (c, c), 0)
    cc = lax.broadcasted_iota(jnp.int32, (c, c), 1)
    keep = (cc > r) if rev else (cc <= r)
    qs = jnp.concatenate([jnp.where(lane_k == hd, q_in, 0.0) for hd in range(4)], axis=0)
    a = _dmm_nt(qs, k_in)
    a = jnp.where(jnp.concatenate([keep] * 4, axis=0), a, 0.0)
    o4 = _dmm(a, v)
    o = _dmm_nt(q_in, st)
    for hd in range(4):
        o = o + jnp.where(lane_v == hd, o4[hd * c:(hd + 1) * c], 0.0)
    bd = (lax.broadcasted_iota(jnp.int32, (256, 128), 0) // 64) == (lax.broadcasted_iota(jnp.int32, (256, 128), 1) // 32)
    st_new = jnp.exp(bl) * st + jnp.where(bd, _dmm_tn(v, k_st), 0.0)
    return o, st_new


def _gla_rows(s, c, rev):
    return s * NCHUNK + (NCHUNK - 1 - c if rev else c)


def _gla_fwd(h, la2):
    def body(qf, kf, vf, laf, qb, kb, vb, lab, of_ref, ob_ref, sf_ref, sb_ref, stf, stb):
        @pl.when(pl.program_id(1) == 0)
        def _():
            stf[...] = jnp.zeros_like(stf)
            stb[...] = jnp.zeros_like(stb)

        sf_ref[0] = stf[...]
        sb_ref[0] = stb[...]
        o, sn = _gla_chunk(qf[...], kf[...], vf[...], laf[...], stf[...], False)
        of_ref[...] = o
        stf[...] = sn
        o, sn = _gla_chunk(qb[...], kb[...], vb[...], lab[...], stb[...], True)
        ob_ref[...] = o
        stb[...] = sn

    def specs(rev):
        rw = lambda s, c: _gla_rows(s, c, rev)
        return [pl.BlockSpec((64, 128), lambda s, c: (rw(s, c), 2)), pl.BlockSpec((64, 128), lambda s, c: (rw(s, c), 3)),
                pl.BlockSpec((64, 256), lambda s, c: (rw(s, c), 2)),
                pl.BlockSpec((64, 128), lambda s, c: (rw(s, c), 1 if rev else 0))]

    orow = lambda rev: pl.BlockSpec((64, 256), lambda s, c: (_gla_rows(s, c, rev), 0))
    srow = lambda rev: pl.BlockSpec((1, 256, 128), lambda s, c: (_gla_rows(s, c, rev), 0, 0))
    return pl.pallas_call(
        body, grid=(NSEQ, NCHUNK),
        in_specs=specs(False) + specs(True),
        out_specs=[orow(False), orow(True), srow(False), srow(True)],
        out_shape=[_sds((N, 256)), _sds((N, 256)), _sds((NSEQ * NCHUNK, 256, 128)), _sds((NSEQ * NCHUNK, 256, 128))],
        scratch_shapes=[pltpu.VMEM((256, 128), F32), pltpu.VMEM((256, 128), F32)],
        name="gla_fwd", compiler_params=_cp(("arbitrary", "arbitrary")))(h, h, h, la2, h, h, h, la2)


def _gla_bwd(h, la2, do, sf, sb):
    def body(qf, kf, vf, laf, dof, sfr, qb, kb, vb, lab, dob, sbr,
             dqf, dkf, dvf, dlf, dqb, dkb, dvb, dlb, dstf, dstb):
        @pl.when(pl.program_id(1) == 0)
        def _():
            dstf[...] = jnp.zeros_like(dstf)
            dstb[...] = jnp.zeros_like(dstb)

        def one(q, k, v, la, do_, st, dst, rev, dq, dk, dv, dl):
            _, vjp = jax.vjp(functools.partial(_gla_chunk, rev=rev), q[...], k[...], v[...], la[...], st[0])
            gq, gk, gv, gl, gs = vjp((do_[...], dst[...]))
            dq[...] = gq
            dk[...] = gk
            dv[...] = gv
            dl[...] = gl
            dst[...] = gs

        one(qf, kf, vf, laf, dof, sfr, dstf, False, dqf, dkf, dvf, dlf)
        one(qb, kb, vb, lab, dob, sbr, dstb, True, dqb, dkb, dvb, dlb)

    def specs(rev):
        rw = lambda s, c: _gla_rows(s, c, not rev)
        return [pl.BlockSpec((64, 128), lambda s, c: (rw(s, c), 2)), pl.BlockSpec((64, 128), lambda s, c: (rw(s, c), 3)),
                pl.BlockSpec((64, 256), lambda s, c: (rw(s, c), 2)),
                pl.BlockSpec((64, 128), lambda s, c: (rw(s, c), 1 if rev else 0)),
                pl.BlockSpec((64, 256), lambda s, c: (rw(s, c), 0)),
                pl.BlockSpec((1, 256, 128), lambda s, c: (rw(s, c), 0, 0))]

    def ospecs(rev):
        rw = lambda s, c: _gla_rows(s, c, not rev)
        n = pl.BlockSpec((64, 128), lambda s, c: (rw(s, c), 0))
        return [n, n, pl.BlockSpec((64, 256), lambda s, c: (rw(s, c), 0)), n]

    oshape = [_sds((N, 128)), _sds((N, 128)), _sds((N, 256)), _sds((N, 128))]
    return pl.pallas_call(
        body, grid=(NSEQ, NCHUNK),
        in_specs=specs(False) + specs(True),
        out_specs=ospecs(False) + ospecs(True),
        out_shape=oshape + oshape,
        scratch_shapes=[pltpu.VMEM((256, 128), F32), pltpu.VMEM((256, 128), F32)],
        name="gla_bwd", compiler_params=_cp(("arbitrary", "arbitrary")))(h, h, h, la2, do, sf, h, h, h, la2, do, sb)


def _gla_post(of, ob, r, g):
    o = of + ob
    head = lax.broadcasted_iota(jnp.int32, (1, 256), 1) // 64
    mu = jnp.zeros_like(o)
    for hd in range(4):
        mu = mu + jnp.where(head == hd, jnp.sum(jnp.where(head == hd, o, 0.0), axis=-1, keepdims=True) * (1.0 / 64.0), 0.0)
    xc = o - mu
    var = jnp.zeros_like(o)
    for hd in range(4):
        var = var + jnp.where(head == hd, jnp.sum(jnp.where(head == hd, xc * xc, 0.0), axis=-1, keepdims=True) * (1.0 / 64.0), 0.0)
    return xc * lax.rsqrt(var + LN_EPS) * g * (r * jax.nn.sigmoid(r))


def _gla_post_fwd(of, ob, h, g):
    tm = 512

    def body(of_ref, ob_ref, r_ref, g_ref, y_ref):
        y_ref[...] = _gla_post(of_ref[...], ob_ref[...], r_ref[...], g_ref[...])

    row = pl.BlockSpec((tm, 256), lambda i: (i, 0))
    return pl.pallas_call(
        body, grid=(N // tm,),
        in_specs=[row, row, pl.BlockSpec((tm, 256), lambda i: (i, 3)), pl.BlockSpec((1, 256), lambda i: (0, 0))],
        out_specs=row, out_shape=_sds((N, 256)), name="gla_post_fwd", compiler_params=_cp(("parallel",)))(of, ob, h, g)


def _gla_post_bwd(of, ob, h, g, dyb):
    tm = 512

    def body(of_ref, ob_ref, r_ref, g_ref, dy_ref, do_ref, dr_ref, dg_ref):
        @pl.when(pl.program_id(0) == 0)
        def _():
            dg_ref[...] = jnp.zeros_like(dg_ref)

        _, vjp = jax.vjp(_gla_post, of_ref[...], ob_ref[...], r_ref[...], g_ref[...])
        go, _, gr, gg = vjp(dy_ref[...])
        do_ref[...] = go
        dr_ref[...] = gr
        dg_ref[...] += gg

    row = pl.BlockSpec((tm, 256), lambda i: (i, 0))
    one = pl.BlockSpec((1, 256), lambda i: (0, 0))
    return pl.pallas_call(
        body, grid=(N // tm,),
        in_specs=[row, row, pl.BlockSpec((tm, 256), lambda i: (i, 3)), one, row],
        out_specs=[row, row, one], out_shape=[_sds((N, 256)), _sds((N, 256)), _sds((1, 256))],
        name="gla_post_bwd", compiler_params=_cp(("arbitrary",)))(of, ob, h, g, dyb)


def _rope_tables(width):
    pos = jnp.arange(L, dtype=F32)
    inv_freq = ROPE_THETA ** (-jnp.arange(0, ROT, 2, dtype=F32) / ROT)
    ang = pos[:, None] * inv_freq[None, :]
    cos, sin = jnp.cos(ang), jnp.sin(ang)
    one = jnp.ones((L, 64 - ROT), F32)
    zero = jnp.zeros((L, 64 - ROT), F32)
    z8 = jnp.zeros((L, ROT // 2), F32)
    c = jnp.concatenate([cos, cos, one], axis=1)
    sa = jnp.concatenate([z8, sin, zero], axis=1)
    sb = jnp.concatenate([-sin, z8, zero], axis=1)
    rep = width // 64
    return jnp.stack([jnp.tile(c, (1, rep)), jnp.tile(sa, (1, rep)), jnp.tile(sb, (1, rep))])


def _rope(t, tab):
    w = t.shape[-1]
    return t * tab[0] + pltpu.roll(t, ROT // 2, 1) * tab[1] + pltpu.roll(t, w - ROT // 2, 1) * tab[2]


def _rope_t(g, tab):
    w = g.shape[-1]
    return g * tab[0] + pltpu.roll(g * tab[1], w - ROT // 2, 1) + pltpu.roll(g * tab[2], ROT // 2, 1)


def _swa_pad_kv(kv_ref, tk_ref, kpad, vpad):
    z = jnp.zeros((SWA_BLK, 128), F32)
    kpad[0:SWA_BLK] = z
    vpad[0:SWA_BLK] = z
    kpad[SWA_BLK + L:] = z
    vpad[SWA_BLK + L:] = z
    kpad[SWA_BLK:SWA_BLK + L] = _rope(kv_ref[:, 0:128], tk_ref[...])
    vpad[SWA_BLK:SWA_BLK + L] = kv_ref[:, 128:256]


def _swa_expand(x, hk):
    lane = lax.broadcasted_iota(jnp.int32, x.shape, 1)
    sw = pltpu.roll(x, 64, 1)
    pair = jnp.where(lane < 64, x, sw) if hk == 0 else jnp.where(lane < 64, sw, x)
    return jnp.concatenate([pair, pair], axis=1)


def _swa_fold(x, hk):
    a = x[:, 0:128] + x[:, 128:256]
    t = a + pltpu.roll(a, 64, 1)
    lane = lax.broadcasted_iota(jnp.int32, a.shape, 1)
    return jnp.where((lane < 64) if hk == 0 else (lane >= 64), t, 0.0)


def _swa_probs(q2, kexp, n, sink_ref, hk):
    slot = lax.broadcasted_iota(jnp.int32, (1, 256), 1) // 64
    qs = jnp.concatenate([jnp.where(slot == g, q2, 0.0) for g in range(4)], axis=0)
    s = _mm_nt(qs, kexp) * 0.125
    i = lax.broadcasted_iota(jnp.int32, (SWA_BLK, 3 * SWA_BLK), 0)
    j = lax.broadcasted_iota(jnp.int32, (SWA_BLK, 3 * SWA_BLK), 1)
    kpos = n * SWA_BLK - SWA_BLK + j
    ok = (j - i >= 0) & (j - i <= 2 * SWA_BLK) & (kpos >= 0) & (kpos < L)
    s = jnp.where(jnp.concatenate([ok] * 4, axis=0), s, NEG_BIG)
    rowg = lax.broadcasted_iota(jnp.int32, (4 * SWA_BLK, 1), 0) // SWA_BLK
    sink = jnp.zeros((4 * SWA_BLK, 1), F32)
    for g in range(4):
        sink = jnp.where(rowg == g, sink_ref[hk * 4 + g], sink)
    m = jnp.maximum(jnp.max(s, axis=-1, keepdims=True), sink)
    p = jnp.exp(s - m)
    ps = jnp.exp(sink - m)
    inv = 1.0 / (jnp.sum(p, axis=-1, keepdims=True) + ps)
    return qs, p * inv, ps * inv, slot, rowg


def _swa_qtab(tk_ref, r0):
    return [jnp.concatenate([tk_ref[i, pl.ds(r0, SWA_BLK), :]] * 4, axis=1) for i in range(3)]


def _swa_fwd(h, tk, sink):
    def body(sink_ref, q_ref, kv_ref, tk_ref, y_ref, kpad, vpad):
        n = pl.program_id(1)

        @pl.when(n == 0)
        def _():
            _swa_pad_kv(kv_ref, tk_ref, kpad, vpad)

        r0 = pl.multiple_of(n * SWA_BLK, SWA_BLK)
        q = _rope(q_ref[...], _swa_qtab(tk_ref, r0))
        kb = kpad[pl.ds(r0, 3 * SWA_BLK), :]
        vb = vpad[pl.ds(r0, 3 * SWA_BLK), :]
        for hk in range(2):
            _, p, _, slot, _ = _swa_probs(q[:, hk * 256:(hk + 1) * 256], _swa_expand(kb, hk), n, sink_ref, hk)
            o4 = _mm(p, _swa_expand(vb, hk))
            o = jnp.zeros((SWA_BLK, 256), F32)
            for g in range(4):
                o = o + jnp.where(slot == g, o4[g * SWA_BLK:(g + 1) * SWA_BLK], 0.0)
            y_ref[:, hk * 256:(hk + 1) * 256] = o

    return pl.pallas_call(
        body,
        grid_spec=pltpu.PrefetchScalarGridSpec(
            num_scalar_prefetch=1, grid=(NSEQ, NBLK),
            in_specs=[pl.BlockSpec((SWA_BLK, 512), lambda s, n, sk: (s * NBLK + n, 2)),
                      pl.BlockSpec((L, 256), lambda s, n, sk: (s, 6)),
                      pl.BlockSpec((3, L, 128), lambda s, n, sk: (0, 0, 0))],
            out_specs=pl.BlockSpec((SWA_BLK, 512), lambda s, n, sk: (s * NBLK + n, 0)),
            scratch_shapes=[pltpu.VMEM((L + 2 * SWA_BLK, 128), F32), pltpu.VMEM((L + 2 * SWA_BLK, 128), F32)]),
        out_shape=_sds((N, 512)), name="swa_fwd", compiler_params=_cp(("arbitrary", "arbitrary")))(sink, h, h, tk)


def _swa_bwd(h, tk, sink, dyc):
    def body(sink_ref, q_ref, kv_ref, tk_ref, dy_ref, dq_ref, dkv_ref, dsink_ref, kpad, vpad, dkacc, dvacc):
        sq = pl.program_id(0)
        n = pl.program_id(1)

        @pl.when(n == 0)
        def _():
            _swa_pad_kv(kv_ref, tk_ref, kpad, vpad)
            dkacc[...] = jnp.zeros_like(dkacc)
            dvacc[...] = jnp.zeros_like(dvacc)

        @pl.when((n == 0) & (sq == 0))
        def _():
            dsink_ref[...] = jnp.zeros_like(dsink_ref)

        r0 = pl.multiple_of(n * SWA_BLK, SWA_BLK)
        tq = _swa_qtab(tk_ref, r0)
        q = _rope(q_ref[...], tq)
        kb = kpad[pl.ds(r0, 3 * SWA_BLK), :]
        vb = vpad[pl.ds(r0, 3 * SWA_BLK), :]
        dk = jnp.zeros((3 * SWA_BLK, 128), F32)
        dv = jnp.zeros((3 * SWA_BLK, 128), F32)
        hrow = lax.broadcasted_iota(jnp.int32, (8, 128), 0)
        dsk = jnp.zeros((8, 128), F32)
        for hk in range(2):
            kexp = _swa_expand(kb, hk)
            vexp = _swa_expand(vb, hk)
            qs, p, ps, slot, rowg = _swa_probs(q[:, hk * 256:(hk + 1) * 256], kexp, n, sink_ref, hk)
            dy2 = dy_ref[:, hk * 256:(hk + 1) * 256]
            dos = jnp.concatenate([jnp.where(slot == g, dy2, 0.0) for g in range(4)], axis=0)
            dp = _mm_nt(dos, vexp)
            delta = jnp.sum(p * dp, axis=-1, keepdims=True)
            ds = p * (dp - delta) * 0.125
            dsr = -ps * delta
            for g in range(4):
                dsk = dsk + jnp.where(hrow == hk * 4 + g, jnp.sum(jnp.where(rowg == g, dsr, 0.0), axis=0, keepdims=True), 0.0)
            dq4 = _mm(ds, kexp)
            dq2 = jnp.zeros((SWA_BLK, 256), F32)
            for g in range(4):
                dq2 = dq2 + jnp.where(slot == g, dq4[g * SWA_BLK:(g + 1) * SWA_BLK], 0.0)
            dq_ref[:, hk * 256:(hk + 1) * 256] = dq2
            dk = dk + _swa_fold(_mm_tn(ds, qs), hk)
            dv = dv + _swa_fold(_mm_tn(p, dos), hk)
        dq_ref[...] = _rope_t(dq_ref[...], tq)
        dkacc[pl.ds(r0, 3 * SWA_BLK), :] += dk
        dvacc[pl.ds(r0, 3 * SWA_BLK), :] += dv
        dsink_ref[...] += dsk

        @pl.when(n == NBLK - 1)
        def _():
            dkv_ref[:, 0:128] = _rope_t(dkacc[SWA_BLK:SWA_BLK + L], tk_ref[...])
            dkv_ref[:, 128:256] = dvacc[SWA_BLK:SWA_BLK + L]

    blk = lambda col: pl.BlockSpec((SWA_BLK, 512), lambda s, n, sk: (s * NBLK + n, col))
    pad = pltpu.VMEM((L + 2 * SWA_BLK, 128), F32)
    return pl.pallas_call(
        body,
        grid_spec=pltpu.PrefetchScalarGridSpec(
            num_scalar_prefetch=1, grid=(NSEQ, NBLK),
            in_specs=[blk(2), pl.BlockSpec((L, 256), lambda s, n, sk: (s, 6)),
                      pl.BlockSpec((3, L, 128), lambda s, n, sk: (0, 0, 0)), blk(0)],
            out_specs=[blk(0), pl.BlockSpec((L, 256), lambda s, n, sk: (s, 0)),
                       pl.BlockSpec((8, 128), lambda s, n, sk: (0, 0))],
            scratch_shapes=[pad, pad, pad, pad]),
        out_shape=[_sds((N, 512)), _sds((N, 256)), _sds((8, 128))],
        name="swa_bwd", compiler_params=_cp(("arbitrary", "arbitrary")))(sink, h, h, tk, dyc)


def _outproj_fwd(ya, yb, yc, x, wo, g, b):
    tm = 512

    def body(ya_ref, yb_ref, yc_ref, x_ref, wo_ref, g_ref, b_ref, s_ref, x1_ref):
        mix = _mm(ya_ref[...], wo_ref[0:256]) + _mm(yb_ref[...], wo_ref[256:512]) + _mm(yc_ref[...], wo_ref[512:1024])
        s = ALPHA * x_ref[...] + mix
        s_ref[...] = s
        x1_ref[...] = _ln_fwd(s, g_ref[...], b_ref[...])

    row = lambda w_: pl.BlockSpec((tm, w_), lambda i: (i, 0))
    one = pl.BlockSpec((1, D), lambda i: (0, 0))
    return pl.pallas_call(
        body, grid=(N // tm,),
        in_specs=[row(256), row(256), row(512), row(D), pl.BlockSpec((D, D), lambda i: (0, 0)), one, one],
        out_specs=[row(D), row(D)], out_shape=[_sds((N, D)), _sds((N, D))],
        name="outproj_fwd", compiler_params=_cp(("parallel",)))(ya, yb, yc, x, wo, g, b)


def _outproj_bwd(dx1, s1, ya, yb, yc, wo, g):
    tm = 512
    nt = N // tm

    def body(dx1_ref, s_ref, ya_ref, yb_ref, yc_ref, wo_ref, g_ref,
             dya_ref, dyb_ref, dyc_ref, dxp_ref, dwo_ref, dg_ref, db_ref, acc):
        i = pl.program_id(0)

        @pl.when(i == 0)
        def _():
            acc[...] = jnp.zeros_like(acc)
            dg_ref[...] = jnp.zeros_like(dg_ref)
            db_ref[...] = jnp.zeros_like(db_ref)

        ds, dg, db = _ln_bwd(dx1_ref[...], s_ref[...], g_ref[...])
        dg_ref[...] += dg
        db_ref[...] += db
        dxp_ref[...] = ALPHA * ds
        dy = _mm_nt(ds, wo_ref[...])
        dya_ref[...] = dy[:, 0:256]
        dyb_ref[...] = dy[:, 256:512]
        dyc_ref[...] = dy[:, 512:1024]
        acc[0:256] += _mm_tn(ya_ref[...], ds)
        acc[256:512] += _mm_tn(yb_ref[...], ds)
        acc[512:1024] += _mm_tn(yc_ref[...], ds)

        @pl.when(i == nt - 1)
        def _():
            dwo_ref[...] = acc[...].astype(MX)

    row = lambda w_: pl.BlockSpec((tm, w_), lambda i: (i, 0))
    one = pl.BlockSpec((1, D), lambda i: (0, 0))
    full = pl.BlockSpec((D, D), lambda i: (0, 0))
    return pl.pallas_call(
        body, grid=(nt,),
        in_specs=[row(D), row(D), row(256), row(256), row(512), full, one],
        out_specs=[row(256), row(256), row(512), row(D), full, one, one],
        out_shape=[_sds((N, 256)), _sds((N, 256)), _sds((N, 512)), _sds((N, D)), _sds((D, D), MX), _sds((1, D)), _sds((1, D))],
        scratch_shapes=[pltpu.VMEM((D, D), F32)],
        name="outproj_bwd", compiler_params=_cp(("arbitrary",)))(dx1, s1, ya, yb, yc, wo, g)


def _ffn_fwd(x1, w1, w2, g, b):
    tm = 512

    def body(x_ref, w1_ref, w2_ref, g_ref, b_ref, a_ref, s_ref, x2_ref):
        j = pl.program_id(1)
        a = _mm(x_ref[...], w1_ref[0])
        a_ref[...] = a.astype(MX)
        hid = jnp.square(jnp.maximum(a, 0.0))
        contrib = _mm(hid, w2_ref[0])

        @pl.when(j == 0)
        def _():
            s_ref[...] = ALPHA * x_ref[...] + contrib

        @pl.when(j > 0)
        def _():
            s_ref[...] += contrib

        @pl.when(j == NSHARD - 1)
        def _():
            x2_ref[...] = _ln_fwd(s_ref[...], g_ref[...], b_ref[...])

    row = pl.BlockSpec((tm, D), lambda i, j: (i, 0))
    wj = pl.BlockSpec((1, D, D), lambda i, j: (j, 0, 0))
    one = pl.BlockSpec((1, D), lambda i, j: (0, 0))
    return pl.pallas_call(
        body, grid=(N // tm, NSHARD),
        in_specs=[row, wj, wj, one, one],
        out_specs=[pl.BlockSpec((tm, D), lambda i, j: (i, j)), row, row],
        out_shape=[_sds((N, DFF), MX), _sds((N, D)), _sds((N, D))],
        name="ffn_fwd", compiler_params=_cp(("parallel", "arbitrary")))(x1, w1, w2, g, b)


def _ffn_bwd_act(dy, s2, a, w1, w2, g):
    tm = 512

    def body(dy_ref, s_ref, a_ref, w1_ref, w2_ref, g_ref, da_ref, ds_ref, dx1_ref, dg_ref, db_ref, dsf):
        i = pl.program_id(0)
        j = pl.program_id(1)

        @pl.when((i == 0) & (j == 0))
        def _():
            dg_ref[...] = jnp.zeros_like(dg_ref)
            db_ref[...] = jnp.zeros_like(db_ref)

        @pl.when(j == 0)
        def _():
            ds, dg, db = _ln_bwd(dy_ref[...], s_ref[...], g_ref[...])
            dsf[...] = ds
            ds_ref[...] = ds.astype(MX)
            dg_ref[...] += dg
            db_ref[...] += db
            dx1_ref[...] = ALPHA * ds

        dhid = _mm_nt(dsf[...], w2_ref[0])
        da = dhid * 2.0 * jnp.maximum(a_ref[...].astype(F32), 0.0)
        da_ref[...] = da.astype(MX)
        dx1_ref[...] += _mm_nt(da, w1_ref[0])

    row = pl.BlockSpec((tm, D), lambda i, j: (i, 0))
    col = pl.BlockSpec((tm, D), lambda i, j: (i, j))
    wj = pl.BlockSpec((1, D, D), lambda i, j: (j, 0, 0))
    one = pl.BlockSpec((1, D), lambda i, j: (0, 0))
    return pl.pallas_call(
        body, grid=(N // tm, NSHARD),
        in_specs=[row, row, col, wj, wj, one],
        out_specs=[col, row, row, one, one],
        out_shape=[_sds((N, DFF), MX), _sds((N, D), MX), _sds((N, D)), _sds((1, D)), _sds((1, D))],
        scratch_shapes=[pltpu.VMEM((tm, D), F32)],
        name="ffn_bwd_act", compiler_params=_cp(("arbitrary", "arbitrary")))(dy, s2, a, w1, w2, g)


def _ffn_bwd_w(x1, da, a, ds):
    tm = 512
    nt = N // tm

    def body(x_ref, da_ref, a_ref, ds_ref, dw1_ref, dw2_ref, acc1, acc2):
        i = pl.program_id(1)

        @pl.when(i == 0)
        def _():
            acc1[...] = jnp.zeros_like(acc1)
            acc2[...] = jnp.zeros_like(acc2)

        acc1[...] += _mm_tn(x_ref[...], da_ref[...])
        hid = jnp.square(jnp.maximum(a_ref[...].astype(F32), 0.0))
        acc2[...] += _mm_tn(hid, ds_ref[...])

        @pl.when(i == nt - 1)
        def _():
            dw1_ref[0] = acc1[...].astype(MX)
            dw2_ref[0] = acc2[...].astype(MX)

    row = pl.BlockSpec((tm, D), lambda j, i: (i, 0))
    col = pl.BlockSpec((tm, D), lambda j, i: (i, j))
    wj = pl.BlockSpec((1, D, D), lambda j, i: (j, 0, 0))
    return pl.pallas_call(
        body, grid=(NSHARD, nt),
        in_specs=[row, col, col, row], out_specs=[wj, wj],
        out_shape=[_sds((NSHARD, D, D), MX), _sds((NSHARD, D, D), MX)],
        scratch_shapes=[pltpu.VMEM((D, D), F32), pltpu.VMEM((D, D), F32)],
        name="ffn_bwd_w", compiler_params=_cp(("parallel", "arbitrary")))(x1, da, a, ds)


def _loss_head(y, target):
    tm = 512

    def body(y_ref, t_ref, dy_ref, l_ref):
        @pl.when(pl.program_id(0) == 0)
        def _():
            l_ref[...] = jnp.zeros_like(l_ref)

        e = y_ref[...] - t_ref[...]
        dy_ref[...] = e * (1.0 / D)
        l_ref[...] += jnp.sum(jnp.sum(e * e, axis=1, keepdims=True), axis=0, keepdims=True) * (0.5 / D)

    row = pl.BlockSpec((tm, D), lambda i: (i, 0))
    return pl.pallas_call(
        body, grid=(N // tm,), in_specs=[row, row],
        out_specs=[row, pl.BlockSpec((8, 128), lambda i: (0, 0))],
        out_shape=[_sds((N, D)), _sds((8, 128))], name="loss_head", compiler_params=_cp(("arbitrary",)))(y, target)


def _s5_discretize(a_re, a_im, log_step, b_re, b_im):
    lam = lax.complex(a_re, a_im)
    lam_bar = jnp.exp(lam * jnp.exp(log_step))
    b_bar = ((lam_bar - 1.0) / lam)[..., None] * lax.complex(b_re, b_im)
    return jnp.real(lam_bar), jnp.imag(lam_bar), jnp.real(b_bar), jnp.imag(b_bar)


def _s5_in_blocks(b):
    e = jnp.eye(8, dtype=F32)
    return jnp.einsum('ij,zbjph->zbihjp', e, b.reshape(2, 2, 8, S5_P, S5_H)).reshape(2, 2, 128, SW)


def _s5_in_unblocks(d):
    return jnp.einsum('zbihip->zbiph', d.reshape(2, 2, 8, S5_H, 8, S5_P)).reshape(2, S5_G, S5_P, S5_H)


def _s5_out_blocks(c):
    e = jnp.eye(8, dtype=F32)
    return jnp.einsum('ij,zbjhp->zbjpih', e, c.reshape(2, 2, 8, S5_H, S5_P)).reshape(2, 2, SW, 128)


def _s5_out_unblocks(d):
    return jnp.einsum('zbipih->zbihp', d.reshape(2, 2, 8, S5_P, 8, S5_H)).reshape(2, S5_G, S5_H, S5_P)


def _gate_weight(w_a):
    z = jnp.zeros((16, 128), F32)
    top = jnp.concatenate([w_a[0], z], axis=1)
    bot = jnp.concatenate([z, w_a[1]], axis=1)
    return jnp.concatenate([top, bot, jnp.zeros((96, 256), F32)], axis=0)


def _layer_prep(p):
    lr, li, br, bi = _s5_discretize(p["s5_a_re"], p["s5_a_im"], p["s5_log_step"], p["s5_b_re"], p["s5_b_im"])
    q = dict(p)
    q["bre"] = _s5_in_blocks(br).astype(MX)
    q["bim"] = _s5_in_blocks(bi).astype(MX)
    q["cre"] = _s5_out_blocks(p["s5_c_re"]).astype(MX)
    q["cim"] = _s5_out_blocks(p["s5_c_im"]).astype(MX)
    mr, mi = lr.reshape(2, 1024), li.reshape(2, 1024)
    q["tab"] = jnp.stack([_scan_tables(mr[0], mi[0], False), _scan_tables(mr[1], mi[1], True)])
    q["tabc"] = jnp.stack([_scan_tables(mr[0], -mi[0], True), _scan_tables(mr[1], -mi[1], False)])
    q["dsk"] = p["s5_d"].reshape(1, 256)
    q["wa"] = _gate_weight(p["gla_w_a"]).astype(MX)
    q["ba"] = p["gla_b_a"].reshape(1, 256)
    q["lng"] = p["gla_ln_g"].reshape(1, 256)
    q["bv"] = p["s5_b_glu"][:256].reshape(1, 256)
    q["bg"] = p["s5_b_glu"][256:].reshape(1, 256)
    for k in ("ln1_g", "ln1_b", "ln2_g", "ln2_b"):
        q[k] = p[k].reshape(1, D)
    return q


def _layer_fwd(x, q, tk, fetch):
    q["w_in"] = fetch("w_in", x)
    h = _inproj_fwd(x, q["w_in"])
    hre, him, y2 = _s5_fwd(h, q["bre"], q["bim"], q["cre"], q["cim"], q["tab"])
    q["w4"] = fetch("s5_w_glu", y2)
    ya = _s5_glu_fwd(y2, h, q["dsk"], q["w4"], q["bv"], q["bg"])
    la2 = _gla_gate_fwd(h, q["wa"], q["ba"])
    of, ob, sf, sb = _gla_fwd(h, la2)
    yb = _gla_post_fwd(of, ob, h, q["lng"])
    yc = _swa_fwd(h, tk, q["swa_sink"])
    q["w_out"] = fetch("w_out", yc)
    s1, x1 = _outproj_fwd(ya, yb, yc, x, q["w_out"], q["ln1_g"], q["ln1_b"])
    q["w_ff1"] = fetch("w_ff1", x1)
    q["w_ff2"] = fetch("w_ff2", x1)
    a, s2, x2 = _ffn_fwd(x1, q["w_ff1"], q["w_ff2"], q["ln2_g"], q["ln2_b"])
    saved = dict(x=x, h=h, hre=hre, him=him, y2=y2, ya=ya, la2=la2, of=of, ob=ob, sf=sf, sb=sb, yb=yb, yc=yc,
                 s1=s1, x1=x1, a=a, s2=s2)
    return x2, saved


def _layer_bwd(dy, q, sv, tk, emit):
    g = {}
    da, ds2, dx1, g["dg2"], g["db2"] = _ffn_bwd_act(dy, sv["s2"], sv["a"], q["w_ff1"], q["w_ff2"], q["ln2_g"])
    dw1, dw2 = _ffn_bwd_w(sv["x1"], da, sv["a"], ds2)
    tie = emit(dict(w_ff1=dw1, w_ff2=dw2))
    dya, dyb, dyc, dxp, dwo, g["dg1"], g["db1"] = _outproj_bwd(dx1, sv["s1"], sv["ya"], sv["yb"], sv["yc"],
                                                               q["w_out"], q["ln1_g"] + tie)
    h = sv["h"]
    daq, dakv, g["dsink"] = _swa_bwd(h, tk, q["swa_sink"], dyc)
    do, gr, g["dlng"] = _gla_post_bwd(sv["of"], sv["ob"], h, q["lng"], dyb)
    gq_f, gk_f, gv_f, gl_f, gq_b, gk_b, gv_b, gl_b = _gla_bwd(h, sv["la2"], do, sv["sf"], sv["sb"])
    dhl, g["dwa"], g["dba"] = _gla_gate_bwd(h, q["wa"], q["ba"], gl_f, gl_b)
    dyp, dud, g["dd"], dw4, g["dbv"], g["dbg"] = _s5_glu_bwd(sv["y2"], h, q["dsk"], q["w4"], q["bv"], q["bg"], dya)
    tie = emit(dict(w_out=dwo.reshape(NSHARD, D // NSHARD, D), s5_w_glu=dw4))
    du2, g["dbre"], g["dbim"], g["dcre"], g["dcim"], g["dmu"] = _s5_bwd(
        h, dyp, sv["hre"], sv["him"], q["bre"], q["bim"], q["cre"], q["cim"], q["tabc"] + tie)
    dx, dwt = _inproj_bwd(sv["x"], q["w_in"], dxp, du2, dud, gq_f, gq_b, gk_f, gk_b, gv_f, gv_b, gr, daq, dakv, dhl)
    tie = emit(dict(w_in=_in_rows_back(dwt)))
    return dx, g, tie


NATIVE = ("dmu", "dbre", "dbim", "dcre", "dcim", "dd", "dbv", "dbg", "dwa", "dba", "dlng", "dsink",
          "dg1", "db1", "dg2", "db2")


def _finish_small(n, w):
    g = {}
    dmu = jnp.sum(n["dmu"], axis=4)
    dlr = dmu[:, :, :, 0].reshape(DEPTH, 2, S5_G, S5_P)
    dli = dmu[:, :, :, 1].reshape(DEPTH, 2, S5_G, S5_P)

    def unblock(c, perm, shape):
        return c.reshape(DEPTH, 2, 2, S5_H, 8, S5_P).transpose(perm).reshape(shape)

    b_shape, c_shape = (DEPTH, 2, S5_G, S5_P, S5_H), (DEPTH, 2, S5_G, S5_H, S5_P)
    _, vjp = jax.vjp(_s5_discretize, w["s5_a_re"], w["s5_a_im"], w["s5_log_step"], w["s5_b_re"], w["s5_b_im"])
    (g["s5_a_re"], g["s5_a_im"], g["s5_log_step"], g["s5_b_re"], g["s5_b_im"]) = vjp(
        (dlr, dli, unblock(n["dbre"], (0, 1, 2, 4, 5, 3), b_shape), unblock(n["dbim"], (0, 1, 2, 4, 5, 3), b_shape)))
    g["s5_c_re"] = unblock(n["dcre"], (0, 1, 2, 4, 3, 5), c_shape)
    g["s5_c_im"] = unblock(n["dcim"], (0, 1, 2, 4, 3, 5), c_shape)
    g["s5_d"] = n["dd"].reshape(DEPTH, S5_G, S5_H)
    g["s5_b_glu"] = jnp.concatenate([n["dbv"], n["dbg"]], axis=2).reshape(DEPTH, 512)
    g["gla_w_a"] = jnp.stack([n["dwa"][:, 0:16, 0:128], n["dwa"][:, 16:32, 128:256]], axis=1)
    g["gla_b_a"] = n["dba"].reshape(DEPTH, 2, 128)
    g["gla_ln_g"] = n["dlng"].reshape(DEPTH, 256)
    g["swa_sink"] = n["dsink"][:, :, 0]
    for k, s in (("ln1_g", "dg1"), ("ln1_b", "db1"), ("ln2_g", "dg2"), ("ln2_b", "db2")):
        g[k] = n[s].reshape(DEPTH, D)
    return g


def _local_step(x, target, qs, tk, fetch, emit):
    saved = []
    for l, q in enumerate(qs):
        x, sv = _layer_fwd(x, q, tk, functools.partial(fetch, l))
        saved.append(sv)
    dy, lacc = _loss_head(x, target)
    smalls = [None] * DEPTH
    tie = 0.0
    for l in reversed(range(DEPTH)):
        qs[l]["ln2_g"] = qs[l]["ln2_g"] + tie
        dy, smalls[l], tie = _layer_bwd(dy, qs[l], saved[l], tk, functools.partial(emit, l))
    return lacc[0, 0], dy, smalls


BIG = ("w_in", "s5_w_glu", "w_out", "w_ff1", "w_ff2")
SMALL = ("s5_a_re", "s5_a_im", "s5_log_step", "s5_b_re", "s5_b_im", "s5_c_re", "s5_c_im", "s5_d", "s5_b_glu",
         "gla_w_a", "gla_b_a", "gla_ln_g", "swa_sink", "ln1_g", "ln1_b", "ln2_g", "ln2_b")
ANY = pl.BlockSpec(memory_space=pl.ANY)


def _place():
    x, y, c = lax.axis_index("x"), lax.axis_index("y"), lax.axis_index("c")
    return x, y, c, [(1 - x, y), (x, 1 - y), (1 - x, 1 - y)]


HBM = pl.BlockSpec(memory_space=pltpu.HBM)
SEMS = pl.BlockSpec(memory_space=pltpu.SEMAPHORE)
EFFECT = pltpu.SideEffectType.DATAFLOW_SIDE_EFFECTING


def _push_copies(ins, lands, send, recv, gather, sending):
    x, y, c, chips = _place()
    me = 2 * x + y
    out = []
    for a in range(len(lands)):
        for j, (px, py) in enumerate(chips):
            peer = 2 * px + py
            src = lands[a].at[me] if gather else ins[a].at[peer if sending else me]
            dst = lands[a].at[me if sending else peer]
            out.append(pltpu.make_async_remote_copy(src_ref=src, dst_ref=dst, send_sem=send.at[3 * a + j],
                                                    recv_sem=recv.at[3 * a + j], device_id=(px, py, c),
                                                    device_id_type=MESH))
    return out


def _push_start(name, arrs, gather):
    n = len(arrs)
    ops = list(arrs) if gather else list(arrs) + [lax.empty(s.shape, s.dtype) for s in arrs]
    m = len(ops)

    def body(*refs):
        ins, lnd = (refs[:n], refs[:n]) if gather else (refs[:n], refs[n:m])
        for cp in _push_copies(ins, lnd, refs[m], refs[m + 1], gather, True):
            cp.start()
        refs[-1][...] = jnp.zeros((8, 128), F32)

    ops = [pltpu.with_memory_space_constraint(t, pltpu.HBM) for t in ops]
    res = pl.pallas_call(
        body, name=name,
        out_shape=(pltpu.SemaphoreType.DMA((3 * n,)), pltpu.SemaphoreType.DMA((3 * n,)),
                   *[pltpu.HBM(t.shape, t.dtype) for t in ops], _sds((8, 128))),
        in_specs=[HBM] * m,
        out_specs=(SEMS, SEMS, *[HBM] * m, pl.BlockSpec(memory_space=pltpu.VMEM)),
        input_output_aliases={i: 2 + i for i in range(m)},
        compiler_params=pltpu.CompilerParams(has_side_effects=EFFECT))(*ops)
    return res[0], res[1], list(res[2:2 + m]), res[-1]


def _push_wait(name, started, after, gather):
    send, recv, ops, _ = started
    m = len(ops)
    n = m if gather else m // 2

    def body(*refs):
        ins, lnd = (refs[:n], refs[:n]) if gather else (refs[:n], refs[n:m])
        for cp in _push_copies(ins, lnd, refs[m], refs[m + 1], gather, False):
            cp.wait_send()
            cp.wait_recv()

    res = pl.pallas_call(
        body, name=name,
        out_shape=[pltpu.HBM(t.shape, t.dtype) for t in ops],
        in_specs=[HBM] * m + [SEMS, SEMS, ANY], out_specs=[HBM] * m,
        input_output_aliases={i: i for i in range(m)},
        compiler_params=pltpu.CompilerParams(has_side_effects=EFFECT))(*ops, send, recv, after)
    return list(res)


def _row_tile(rows):
    return max(t for t in range(8, min(rows, 512) + 1, 8) if rows % t == 0)


def _cast_to_slot(me, w, l):
    _, rows, cols = w.shape
    tr = _row_tile(rows)

    def body(me_ref, w_ref, o_ref):
        o_ref[0] = w_ref[0].astype(MX)

    return pl.pallas_call(
        body,
        grid_spec=pltpu.PrefetchScalarGridSpec(
            num_scalar_prefetch=1, grid=(rows // tr,),
            in_specs=[pl.BlockSpec((1, tr, cols), lambda i, me_: (l, i, 0))],
            out_specs=pl.BlockSpec((1, tr, cols), lambda i, me_: (me_[0], i, 0))),
        out_shape=_sds((NSHARD, rows, cols), MX), name="cast_to_slot", compiler_params=_cp(("arbitrary",)))(me, w)


def _sum_sources(me, recv, own):
    _, rows, cols = recv[0].shape
    tr = min(_row_tile(rows), 256) if rows % 256 == 0 else _row_tile(rows)
    nt = rows // tr

    def body(me_ref, *refs):
        o_ref = refs[-1]
        for l in range(DEPTH):
            @pl.when(pl.program_id(0) == l)
            def _():
                r_ref, own_ref = refs[2 * l], refs[2 * l + 1]
                part = [jnp.where(me_ref[0] == s, own_ref[0], r_ref[s]).astype(F32) for s in range(NSHARD)]
                o_ref[...] = ((part[0] + part[1]) + part[2]) + part[3]

    in_specs = []
    for l in range(DEPTH):
        pick = lambda g, i, me_, l=l: jnp.where(g == l, i, jnp.where(g < l, 0, nt - 1))
        in_specs += [pl.BlockSpec((NSHARD, tr, cols), lambda g, i, me_, pick=pick: (0, pick(g, i, me_), 0)),
                     pl.BlockSpec((1, tr, cols), lambda g, i, me_, pick=pick: (me_[0], pick(g, i, me_), 0))]
    return pl.pallas_call(
        body,
        grid_spec=pltpu.PrefetchScalarGridSpec(
            num_scalar_prefetch=1, grid=(DEPTH, nt), in_specs=in_specs,
            out_specs=pl.BlockSpec((tr, cols), lambda g, i, me_: (g * nt + i, 0))),
        out_shape=_sds((DEPTH * rows, cols)), name="sum_sources",
        compiler_params=_cp(("arbitrary", "arbitrary")))(me, *[t for l in range(DEPTH) for t in (recv[l], own[l])])


def _swap_sibling(arrs):
    n = len(arrs)

    def body(*refs):
        ins, outs = refs[:n], refs[n:2 * n]
        send, recv = refs[2 * n:]
        x, y, c, _ = _place()
        cps = [pltpu.make_async_remote_copy(src_ref=ins[a], dst_ref=outs[a], send_sem=send.at[a], recv_sem=recv.at[a],
                                            device_id=(x, y, 1 - c), device_id_type=MESH) for a in range(n)]
        for cp in cps:
            cp.start()
        for cp in cps:
            cp.wait()

    return pl.pallas_call(
        body, in_specs=[ANY] * n, out_specs=[ANY] * n, out_shape=[_sds(a.shape, a.dtype) for a in arrs],
        scratch_shapes=[pltpu.SemaphoreType.DMA((n,)), pltpu.SemaphoreType.DMA((n,))],
        name="swap_sibling")(*arrs)


def _allreduce_small(per_layer):
    nk = len(per_layer[0])
    n = DEPTH * nk
    shapes = [a.shape for a in per_layer[0]]

    def body(*refs):
        ins, outs = refs[:n], refs[n:n + nk]
        sibs, slots = refs[n + nk:n + 2 * nk], refs[n + 2 * nk:n + 3 * nk]
        send, recv = refs[n + 3 * nk:]
        x, y, c, chips = _place()
        me = 2 * x + y
        d2d = [pltpu.make_async_remote_copy(src_ref=ins[l * nk + k], dst_ref=sibs[k].at[l], send_sem=send.at[l * nk + k],
                                            recv_sem=recv.at[l * nk + k], device_id=(x, y, 1 - c), device_id_type=MESH)
               for l in range(DEPTH) for k in range(nk)]
        for cp in d2d:
            cp.start()
        for cp in d2d:
            cp.wait()
        for l in range(DEPTH):
            for k in range(nk):
                slots[k][me, l] = ins[l * nk + k][...] + sibs[k][l]

        def remote(k, j, slot):
            px, py = chips[j]
            return pltpu.make_async_remote_copy(src_ref=slots[k].at[me], dst_ref=slots[k].at[slot],
                                                send_sem=send.at[n + 3 * k + j], recv_sem=recv.at[n + 3 * k + j],
                                                device_id=(px, py, c), device_id_type=MESH)

        sends = [remote(k, j, me) for k in range(nk) for j in range(3)]
        for cp in sends:
            cp.start()
        for k in range(nk):
            for j in range(3):
                remote(k, j, 2 * chips[j][0] + chips[j][1]).wait_recv()
        for cp in sends:
            cp.wait_send()
        for k in range(nk):
            outs[k][...] = ((slots[k][0] + slots[k][1]) + slots[k][2]) + slots[k][3]

    vm = pl.BlockSpec(memory_space=pltpu.VMEM)
    return pl.pallas_call(
        body, in_specs=[vm] * n, out_specs=[vm] * nk, out_shape=[_sds((DEPTH,) + s) for s in shapes],
        scratch_shapes=([pltpu.VMEM((DEPTH,) + s, F32) for s in shapes]
                        + [pltpu.VMEM((NSHARD, DEPTH) + s, F32) for s in shapes]
                        + [pltpu.SemaphoreType.DMA((n + 3 * nk,)), pltpu.SemaphoreType.DMA((n + 3 * nk,))]),
        name="allreduce_small", compiler_params=pltpu.CompilerParams(vmem_limit_bytes=VMEM_LIMIT))(
            *[a for layer in per_layer for a in layer])


def _adamw_math(w, g, m, v):
    m = ADAM_B1 * m + (1.0 - ADAM_B1) * g
    v = ADAM_B2 * v + (1.0 - ADAM_B2) * jnp.square(g)
    m_hat = m / (1.0 - ADAM_B1 ** ADAM_STEP)
    v_hat = v / (1.0 - ADAM_B2 ** ADAM_STEP)
    delta = -ADAM_LR * (m_hat / (jnp.sqrt(v_hat) + ADAM_EPS) + ADAM_WD * w)
    return delta, m, v


def _adamw(g_parts, w, m, v):
    rows, cols = w.shape
    tr = 256 if rows % 256 == 0 else _row_tile(rows)
    k = len(g_parts)

    def body(*refs):
        g = refs[0][...]
        for r in refs[1:k]:
            g = g + r[...]
        w_ref, m_ref, v_ref, go, do, mo, vo = refs[k:]
        d, mn, vn = _adamw_math(w_ref[...], g, m_ref[...], v_ref[...])
        go[...] = g
        do[...] = d
        mo[...] = mn
        vo[...] = vn

    spec = pl.BlockSpec((tr, cols), lambda i: (i, 0))
    return pl.pallas_call(
        body, grid=(rows // tr,), in_specs=[spec] * (k + 3), out_specs=[spec] * 4,
        out_shape=[_sds((rows, cols))] * 4, name="adamw", compiler_params=_cp(("parallel",)))(*g_parts, w, m, v)


def _adamw_small(gs, ws, ms, vs):
    n = len(gs)

    def body(*refs):
        for k in range(n):
            d, mn, vn = _adamw_math(refs[n + k][...], refs[k][...], refs[2 * n + k][...], refs[3 * n + k][...])
            refs[4 * n + k][...] = d
            refs[5 * n + k][...] = mn
            refs[6 * n + k][...] = vn

    vm = pl.BlockSpec(memory_space=pltpu.VMEM)
    shapes = [_sds(a.shape) for a in ws]
    res = pl.pallas_call(
        body, in_specs=[vm] * (4 * n), out_specs=[vm] * (3 * n), out_shape=shapes * 3, name="adamw_small",
        compiler_params=pltpu.CompilerParams(vmem_limit_bytes=VMEM_LIMIT))(*gs, *ws, *ms, *vs)
    return res[:n], res[n:2 * n], res[2 * n:]


_ARGS = ("x", "w_in", "s5_a_re", "s5_a_im", "s5_log_step", "s5_b_re", "s5_b_im", "s5_c_re", "s5_c_im", "s5_d",
         "s5_w_glu", "s5_b_glu", "gla_w_a", "gla_b_a", "gla_ln_g", "swa_sink", "w_out", "ln1_g", "ln1_b", "w_ff1",
         "w_ff2", "ln2_g", "ln2_b")
_WEIGHTS = _ARGS[1:]


def _in_rows(g4):
    t = g4.reshape(DIN, D)
    return jnp.concatenate([t[0:1024], t[1056:DIN], t[1024:1056], jnp.zeros((DINP - DIN, D), t.dtype)], axis=0)


def _in_rows_back(d):
    return jnp.concatenate([d[0:1024], d[1792:1824], d[1024:1792]], axis=0).reshape(NSHARD, DIN // NSHARD, D).astype(MX)


def _shard_cols(d):
    return d.reshape(d.shape[0], NSHARD, d.shape[1] // NSHARD).transpose(1, 0, 2)


def kernel(x, w_in, s5_a_re, s5_a_im, s5_log_step, s5_b_re, s5_b_im, s5_c_re, s5_c_im, s5_d, s5_w_glu, s5_b_glu, gla_w_a, gla_b_a, gla_ln_g, swa_sink, w_out, ln1_g, ln1_b, w_ff1, w_ff2, ln2_g, ln2_b, loss_target, m_w_in, m_s5_a_re, m_s5_a_im, m_s5_log_step, m_s5_b_re, m_s5_b_im, m_s5_c_re, m_s5_c_im, m_s5_d, m_s5_w_glu, m_s5_b_glu, m_gla_w_a, m_gla_b_a, m_gla_ln_g, m_swa_sink, m_w_out, m_ln1_g, m_ln1_b, m_w_ff1, m_w_ff2, m_ln2_g, m_ln2_b, v_w_in, v_s5_a_re, v_s5_a_im, v_s5_log_step, v_s5_b_re, v_s5_b_im, v_s5_c_re, v_s5_c_im, v_s5_d, v_s5_w_glu, v_s5_b_glu, v_gla_w_a, v_gla_b_a, v_gla_ln_g, v_swa_sink, v_w_out, v_ln1_g, v_ln1_b, v_w_ff1, v_w_ff2, v_ln2_g, v_ln2_b):
    given = dict(locals())
    w = {k: given[k] for k in _WEIGHTS}
    mom = {k: given["m_" + k] for k in _WEIGHTS}
    var = {k: given["v_" + k] for k in _WEIGHTS}

    me = (2 * lax.axis_index("x") + lax.axis_index("y")).astype(jnp.int32).reshape(1)
    tr = lambda t: t.transpose(0, 2, 1)
    shard = {k: (tr(w[k]) if k == "w_in" else w[k]) for k in BIG}
    qs = [_layer_prep({k: w[k][l] for k in SMALL}) for l in range(DEPTH)]

    first = ("w_in", "s5_w_glu", "w_out")
    follow = {(0, "w_in"): [(0, BIG[3:]), (1, first)], (0, "w_ff1"): [(1, BIG[3:])]}
    gathers = {}

    def start_gather(l, names, behind=None):
        lands = [_cast_to_slot(me, shard[k], l) for k in names]
        if behind is not None:
            lands, behind = lax.optimization_barrier((lands, behind))
        st = _push_start(f"gather_start_{l}_{names[0]}", lands, True)
        for k in names:
            gathers[l, k] = [names, st, None]
        return st[-1], behind

    token = start_gather(0, first[:1])[0] + start_gather(0, first[1:])[0]

    def fetch(l, name, after):
        names, st, got = gathers[l, name]
        tie = None
        if got is None:
            if l == 0 and name == "w_in":
                after = token
            lands = _push_wait(f"gather_wait_{l}_{names[0]}", st, after, True)
            for l2, names2 in follow.get((l, name), ()):
                tok, lands[0] = start_gather(l2, names2, lands[0])
                tie = tok[0, 0] if tie is None else tie + tok[0, 0]
            got = dict(zip(names, lands))
            for k in names:
                gathers[l, k][2] = got
        full = got[name]
        if name == "w_in":
            return _in_rows(full) if tie is None else _in_rows(full) + tie.astype(MX)
        if tie is not None:
            qs[l]["ln2_b"] = qs[l]["ln2_b"] + tie
        return full.reshape(D, D) if name == "w_out" else full

    scatters = []

    def emit(l, grads):
        names = tuple(grads)
        st = _push_start(f"scatter_start_{l}_{names[0]}", [grads[k] for k in names], False)
        scatters.append((l, names, st))
        return st[-1][0, 0]

    loss, dx, smalls = _local_step(x.reshape(N, D), loss_target.reshape(N, D), qs, _rope_tables(128), fetch, emit)
    loss = lax.psum(loss, ("x", "y", "c"))

    recv, own = {}, {}
    for l, names, st in scatters:
        ops = _push_wait(f"scatter_wait_{l}_{names[0]}", st, dx, False)
        for i, k in enumerate(names):
            own[l, k], recv[l, k] = ops[i], ops[len(names) + i]
    sums = [_sum_sources(me, [recv[l, k] for l in range(DEPTH)], [own[l, k] for l in range(DEPTH)]) for k in BIG]
    others = _swap_sibling(sums)

    out = {}
    for k, mine, other in zip(BIG, sums, others):
        shp = shard[k].shape
        res = _adamw([mine, other], *((tr(t[k]) if k == "w_in" else t[k]).reshape(-1, shp[-1]) for t in (w, mom, var)))
        res = [r.reshape(shp) for r in res]
        out[k] = [tr(r) for r in res] if k == "w_in" else res

    native = _allreduce_small([[smalls[l][k] for k in NATIVE] for l in range(DEPTH)])
    gsmall = _finish_small(dict(zip(NATIVE, native)), w)
    res = _adamw_small(*([t[k] for k in SMALL] for t in (gsmall, w, mom, var)))
    for i, k in enumerate(SMALL):
        out[k] = [gsmall[k], res[0][i], res[1][i], res[2][i]]

    return (loss, dx.reshape(NSEQ, L, D), *[out[k][0] for k in _WEIGHTS], *[out[k][1] for k in _WEIGHTS],
            *[out[k][2] for k in _WEIGHTS], *[out[k][3] for k in _WEIGHTS])
```

```python
import functools
import math

import jax
import jax.numpy as jnp
from jax import lax
from jax.experimental import pallas as pl
from jax.experimental.pallas import tpu as pltpu

F32 = jnp.float32
MX = jnp.bfloat16
MESH = pl.DeviceIdType.MESH

DEPTH = 2
NSEQ = 2
L = 2048
N = NSEQ * L
D = 1024
DFF = 4096
NSHARD = 4
S5_G, S5_H, S5_P = 16, 16, 64
GLA_CHUNK = 64
NCHUNK = L // GLA_CHUNK
SWA_BLK = 128
NBLK = L // SWA_BLK
ROT = 16
ROPE_THETA = 500000.0
LN_EPS = 1e-5
ALPHA = (2 * DEPTH) ** 0.25
NEG_BIG = -1e30
DIN = 1824
DINP = 1920
ADAM_LR, ADAM_B1, ADAM_B2, ADAM_EPS, ADAM_WD, ADAM_STEP = 0.001, 0.9, 0.999, 1e-08, 0.01, 10
VMEM_LIMIT = 56 * 1024 * 1024
TT = 512
SW = 512
FFN_TM = 1024
FFN_TM_W = 1024
FFN_VMEM = 60 * 1024 * 1024


def _cp(sem, vmem=VMEM_LIMIT):
    return pltpu.CompilerParams(dimension_semantics=sem, vmem_limit_bytes=vmem)


def _mm(a, b):
    return jnp.dot(a.astype(MX), b.astype(MX), preferred_element_type=F32)


def _mm_nt(a, b):
    return lax.dot_general(a.astype(MX), b.astype(MX), (((1,), (1,)), ((), ())), preferred_element_type=F32)


def _mm_tn(a, b):
    return lax.dot_general(a.astype(MX), b.astype(MX), (((0,), (0,)), ((), ())), preferred_element_type=F32)


@jax.custom_vjp
def _dmm(a, b):
    return _mm(a, b)


_dmm.defvjp(lambda a, b: (_mm(a, b), (a, b)), lambda r, g: (_mm_nt(g, r[1]), _mm_tn(r[0], g)))


@jax.custom_vjp
def _dmm_nt(a, b):
    return _mm_nt(a, b)


_dmm_nt.defvjp(lambda a, b: (_mm_nt(a, b), (a, b)), lambda r, g: (_mm(g, r[1]), _mm_tn(g, r[0])))


@jax.custom_vjp
def _dmm_tn(a, b):
    return _mm_tn(a, b)


_dmm_tn.defvjp(lambda a, b: (_mm_tn(a, b), (a, b)), lambda r, g: (_mm_nt(r[1], g), _mm(r[0], g)))


def _split3(x):
    hi = x.astype(MX)
    r1 = x - hi.astype(F32)
    mid = r1.astype(MX)
    lo = (r1 - mid.astype(F32)).astype(MX)
    return hi, mid, lo


def _tri(rev):
    r = lax.broadcasted_iota(jnp.int32, (GLA_CHUNK, GLA_CHUNK), 0)
    c = lax.broadcasted_iota(jnp.int32, (GLA_CHUNK, GLA_CHUNK), 1)
    return jnp.where((c >= r) if rev else (c <= r), 1.0, 0.0).astype(MX)


def _cums_impl(x, rev):
    t = _tri(rev)
    return sum(jnp.dot(t, p, preferred_element_type=F32) for p in _split3(x))


@functools.partial(jax.custom_vjp, nondiff_argnums=(1,))
def _cums(x, rev):
    return _cums_impl(x, rev)


_cums.defvjp(lambda x, rev: (_cums_impl(x, rev), None), lambda rev, r, g: (_cums_impl(g, not rev),))


def _ln_fwd(s, g, b):
    mu = jnp.mean(s, axis=-1, keepdims=True)
    xc = s - mu
    var = jnp.mean(xc * xc, axis=-1, keepdims=True)
    return xc * lax.rsqrt(var + LN_EPS) * g + b


def _ln_bwd(dy, s, g):
    mu = jnp.mean(s, axis=-1, keepdims=True)
    xc = s - mu
    var = jnp.mean(xc * xc, axis=-1, keepdims=True)
    rstd = lax.rsqrt(var + LN_EPS)
    xhat = xc * rstd
    dxh = dy * g
    ds = rstd * (dxh - jnp.mean(dxh, axis=-1, keepdims=True) - xhat * jnp.mean(dxh * xhat, axis=-1, keepdims=True))
    return ds, jnp.sum(dy * xhat, axis=0, keepdims=True), jnp.sum(dy, axis=0, keepdims=True)


def _sds(shape, dtype=F32):
    return jax.ShapeDtypeStruct(shape, dtype)


def _inproj_fwd(x, wt):
    tm = 512

    def body(x_ref, w_ref, h_ref):
        h_ref[...] = _mm_nt(x_ref[...], w_ref[...])

    return pl.pallas_call(
        body, grid=(N // tm,),
        in_specs=[pl.BlockSpec((tm, D), lambda i: (i, 0)), pl.BlockSpec((DINP, D), lambda i: (0, 0))],
        out_specs=pl.BlockSpec((tm, DINP), lambda i: (i, 0)),
        out_shape=_sds((N, DINP)), name="inproj_fwd", compiler_params=_cp(("parallel",)))(x, wt)


def _inproj_bwd(x, w, dxp, du2, dud, gq_f, gq_b, gk_f, gk_b, gv_f, gv_b, gr, daq, dakv, dhl):
    tm = 256
    nt = N // tm

    def body(x_ref, w_ref, dxp_ref, du2_ref, dud_ref, gqf, gqb, gkf, gkb, gvf, gvb, gr_ref, daq_ref, dakv_ref, dhl_ref,
             dx_ref, dw_ref):
        i = pl.program_id(0)
        dh = jnp.concatenate([
            du2_ref[0] + du2_ref[1] + dud_ref[...], gqf[...] + gqb[...], gkf[...] + gkb[...], gvf[...] + gvb[...],
            gr_ref[...], daq_ref[...], dakv_ref[...], dhl_ref[...]], axis=1)
        dx_ref[...] = dxp_ref[...] + _mm(dh, w_ref[...])
        contrib = _mm_tn(dh, x_ref[...])

        @pl.when(i == 0)
        def _():
            dw_ref[...] = contrib

        @pl.when(i > 0)
        def _():
            dw_ref[...] += contrib

    row = lambda w_: pl.BlockSpec((tm, w_), lambda i: (i, 0))
    return pl.pallas_call(
        body, grid=(nt,),
        in_specs=[row(D), pl.BlockSpec((DINP, D), lambda i: (0, 0)), row(D),
                  pl.BlockSpec((2, tm, 256), lambda i: (0, i, 0)), row(256), row(128), row(128), row(128), row(128),
                  row(256), row(256), row(256), row(512), row(256), row(128)],
        out_specs=[row(D), pl.BlockSpec((DINP, D), lambda i: (0, 0))],
        out_shape=[_sds((N, D)), _sds((DINP, D))],
        name="inproj_bwd", compiler_params=_cp(("arbitrary",)))(
            x, w, dxp, du2, dud, gq_f, gq_b, gk_f, gk_b, gv_f, gv_b, gr, daq, dakv, dhl)


def _scan_tables(mr, mi, reverse):
    pw = [(mr, mi)]
    for _ in range(7):
        pr, pi = pw[-1]
        pw.append((pr * mr - pi * mi, pr * mi + pi * mr))
    rows = jnp.arange(8)[:, None]
    out = []
    for d in (1, 2, 4):
        keep = rows >= d
        out += [jnp.where(keep, pw[d - 1][0][None], 0.0), jnp.where(keep, pw[d - 1][1][None], 0.0)]
    out += [jnp.stack([p[0] for p in pw]), jnp.stack([p[1] for p in pw])]
    t = jnp.stack(out)
    if reverse:
        t = t[:, ::-1, :]
    return t.reshape(8, 8, 2, SW).transpose(2, 0, 1, 3)


def _tile_scan(xr, xi, a, cr, ci, reverse):
    for lvl, d in enumerate((1, 2, 4)):
        sh = 8 - d if reverse else d
        sr = pltpu.roll(xr, sh, 0)
        si = pltpu.roll(xi, sh, 0)
        ar, ai = a[2 * lvl], a[2 * lvl + 1]
        xr, xi = xr + ar * sr - ai * si, xi + ar * si + ai * sr
    pr, pi = a[6], a[7]
    return xr + pr * cr - pi * ci, xi + pr * ci + pi * cr


def _s5_time_block(z, s, t, adjoint):
    flip = (1 - z) if adjoint else z
    return s * (L // TT) + t + flip * (L // TT - 1 - 2 * t)


def _s5_fwd(h, bre, bim, cre, cim, tab):
    nt = L // TT

    def body(u_ref, bre_ref, bim_ref, cre_ref, cim_ref, tab_ref, hre_ref, him_ref, y_ref, car):
        z = pl.program_id(1)
        tc = pl.program_id(3)

        @pl.when(tc == 0)
        def _():
            car[...] = jnp.zeros_like(car)

        u = u_ref[...]
        hre_ref[0] = _mm(u, bre_ref[0, 0])
        him_ref[0] = _mm(u, bim_ref[0, 0])

        def run(reverse):
            a = [tab_ref[0, 0, k] for k in range(8)]

            def step(i, carry):
                cr, ci = carry
                r0 = pl.multiple_of((TT // 8 - 1 - i if reverse else i) * 8, 8)
                xr, xi = _tile_scan(hre_ref[0, pl.ds(r0, 8), :], him_ref[0, pl.ds(r0, 8), :], a, cr, ci, reverse)
                hre_ref[0, pl.ds(r0, 8), :] = xr
                him_ref[0, pl.ds(r0, 8), :] = xi
                row = 0 if reverse else 7
                return (jnp.broadcast_to(xr[row:row + 1, :], (8, SW)), jnp.broadcast_to(xi[row:row + 1, :], (8, SW)))

            cr, ci = lax.fori_loop(0, TT // 8, step, (car[0], car[1]), unroll=4)
            car[0] = cr
            car[1] = ci

        @pl.when(z == 0)
        def _():
            run(False)

        @pl.when(z == 1)
        def _():
            run(True)

        y_ref[0] = _mm(hre_ref[0], cre_ref[0, 0]) - _mm(him_ref[0], cim_ref[0, 0])

    tb = lambda b, z, s, t: _s5_time_block(z, s, t, False)
    wspec = lambda r, c: pl.BlockSpec((1, 1, r, c), lambda b, z, s, t: (z, b, 0, 0))
    return pl.pallas_call(
        body, grid=(2, 2, NSEQ, nt),
        in_specs=[pl.BlockSpec((TT, 128), lambda b, z, s, t: (tb(b, z, s, t), b)),
                  wspec(128, SW), wspec(128, SW), wspec(SW, 128), wspec(SW, 128),
                  pl.BlockSpec((1, 1, 8, 8, SW), lambda b, z, s, t: (z, b, 0, 0, 0))],
        out_specs=[pl.BlockSpec((1, TT, SW), lambda b, z, s, t: (z, tb(b, z, s, t), b)),
                   pl.BlockSpec((1, TT, SW), lambda b, z, s, t: (z, tb(b, z, s, t), b)),
                   pl.BlockSpec((1, TT, 128), lambda b, z, s, t: (z, tb(b, z, s, t), b))],
        out_shape=[_sds((2, N, 2 * SW)), _sds((2, N, 2 * SW)), _sds((2, N, 256))],
        scratch_shapes=[pltpu.VMEM((2, 8, SW), F32)],
        name="s5_fwd", compiler_params=_cp(("arbitrary",) * 4))(h, bre, bim, cre, cim, tab)


def _s5_bwd(h, dyp, hre, him, bre, bim, cre, cim, tabc):
    nt = L // TT

    def body(u_ref, dy_ref, hre_ref, him_ref, bre_ref, bim_ref, cre_ref, cim_ref, tab_ref,
             du_ref, dbre_ref, dbim_ref, dcre_ref, dcim_ref, dmu_ref, gre, gim, car, acc):
        z = pl.program_id(1)
        s = pl.program_id(2)
        tc = pl.program_id(3)

        @pl.when(tc == 0)
        def _():
            car[...] = jnp.zeros_like(car)

        @pl.when((tc == 0) & (s == 0))
        def _():
            acc[...] = jnp.zeros_like(acc)
            dmu_ref[...] = jnp.zeros_like(dmu_ref)

        dy = dy_ref[...]
        gre[...] = _mm_nt(dy, cre_ref[0, 0])
        gim[...] = -_mm_nt(dy, cim_ref[0, 0])
        rowid = lax.broadcasted_iota(jnp.int32, (8, SW), 0)

        def run(reverse):
            a = [tab_ref[0, 0, k] for k in range(8)]
            first = 7 if reverse else 0

            def step(i, carry):
                cr, ci, dmr, dmi = carry
                r0 = pl.multiple_of((TT // 8 - 1 - i if reverse else i) * 8, 8)
                xr, xi = _tile_scan(gre[pl.ds(r0, 8), :], gim[pl.ds(r0, 8), :], a, cr, ci, reverse)
                gre[pl.ds(r0, 8), :] = xr
                gim[pl.ds(r0, 8), :] = xi
                sh = 7 if reverse else 1
                gpr = jnp.where(rowid == first, cr, pltpu.roll(xr, sh, 0))
                gpi = jnp.where(rowid == first, ci, pltpu.roll(xi, sh, 0))
                hr = hre_ref[0, pl.ds(r0, 8), :]
                hi = him_ref[0, pl.ds(r0, 8), :]
                dmr = dmr + gpr * hr + gpi * hi
                dmi = dmi + gpi * hr - gpr * hi
                row = 0 if reverse else 7
                return (jnp.broadcast_to(xr[row:row + 1, :], (8, SW)), jnp.broadcast_to(xi[row:row + 1, :], (8, SW)),
                        dmr, dmi)

            cr, ci, dmr, dmi = lax.fori_loop(0, TT // 8, step, (car[0], car[1], dmu_ref[0, 0, 0], dmu_ref[0, 0, 1]),
                                             unroll=4)
            car[0] = cr
            car[1] = ci
            dmu_ref[0, 0, 0] = dmr
            dmu_ref[0, 0, 1] = dmi

        @pl.when(z == 0)
        def _():
            run(True)

        @pl.when(z == 1)
        def _():
            run(False)

        gr = gre[...]
        gi = gim[...]
        u = u_ref[...]
        du_ref[0] = _mm_nt(gr, bre_ref[0, 0]) + _mm_nt(gi, bim_ref[0, 0])
        acc[0] += _mm_tn(u, gr)
        acc[1] += _mm_tn(u, gi)
        acc[2] += _mm_tn(dy, hre_ref[0])
        acc[3] -= _mm_tn(dy, him_ref[0])

        @pl.when((tc == nt - 1) & (s == NSEQ - 1))
        def _():
            grp = lax.broadcasted_iota(jnp.int32, (S5_H, SW), 1) // S5_P
            for k, out in enumerate((dbre_ref, dbim_ref, dcre_ref, dcim_ref)):
                c = jnp.zeros((S5_H, SW), F32)
                for i in range(8):
                    c = c + jnp.where(grp == i, acc[k, i * S5_H:(i + 1) * S5_H, :], 0.0)
                out[0, 0] = c

    tb = lambda b, z, s, t: _s5_time_block(z, s, t, True)
    wspec = lambda r, c: pl.BlockSpec((1, 1, r, c), lambda b, z, s, t: (z, b, 0, 0))
    tok = lambda w_: pl.BlockSpec((TT, w_), lambda b, z, s, t: (tb(b, z, s, t), b))
    st = pl.BlockSpec((1, TT, SW), lambda b, z, s, t: (z, tb(b, z, s, t), b))
    return pl.pallas_call(
        body, grid=(2, 2, NSEQ, nt),
        in_specs=[tok(128), tok(128), st, st, wspec(128, SW), wspec(128, SW), wspec(SW, 128), wspec(SW, 128),
                  pl.BlockSpec((1, 1, 8, 8, SW), lambda b, z, s, t: (z, b, 0, 0, 0))],
        out_specs=[pl.BlockSpec((1, TT, 128), lambda b, z, s, t: (z, tb(b, z, s, t), b)),
                   wspec(S5_H, SW), wspec(S5_H, SW), wspec(S5_H, SW), wspec(S5_H, SW),
                   pl.BlockSpec((1, 1, 2, 8, SW), lambda b, z, s, t: (z, b, 0, 0, 0))],
        out_shape=[_sds((2, N, 256))] + [_sds((2, 2, S5_H, SW))] * 4 + [_sds((2, 2, 2, 8, SW))],
        scratch_shapes=[pltpu.VMEM((TT, SW), F32), pltpu.VMEM((TT, SW), F32), pltpu.VMEM((2, 8, SW), F32),
                        pltpu.VMEM((4, 128, SW), F32)],
        name="s5_bwd", compiler_params=_cp(("arbitrary",) * 4))(h, dyp, hre, him, bre, bim, cre, cim, tabc)


_GELU_C = math.sqrt(2.0 / math.pi)


def _gelu(y):
    return 0.5 * y * (1.0 + jnp.tanh(_GELU_C * (y + 0.044715 * y * y * y)))


def _gelu_grad(y):
    t = jnp.tanh(_GELU_C * (y + 0.044715 * y * y * y))
    return 0.5 * (1.0 + t) + 0.5 * y * (1.0 - t * t) * _GELU_C * (1.0 + 3 * 0.044715 * y * y)


def _glu_halves(w4_ref):
    return (jnp.concatenate([w4_ref[0], w4_ref[1]], axis=1), jnp.concatenate([w4_ref[2], w4_ref[3]], axis=1))


def _s5_glu_fwd(y2, h, dsk, w4, bv, bg):
    tm = 512

    def body(y2_ref, u_ref, d_ref, w4_ref, bv_ref, bg_ref, ya_ref):
        wv, wg = _glu_halves(w4_ref)
        z = _gelu(y2_ref[0] + y2_ref[1] + d_ref[...] * u_ref[...])
        val = _mm(z, wv) + bv_ref[...]
        gate = _mm(z, wg) + bg_ref[...]
        ya_ref[...] = val * jax.nn.sigmoid(gate)

    full = lambda r, c: pl.BlockSpec((r, c), lambda i: (0, 0))
    return pl.pallas_call(
        body, grid=(N // tm,),
        in_specs=[pl.BlockSpec((2, tm, 256), lambda i: (0, i, 0)), pl.BlockSpec((tm, 256), lambda i: (i, 0)),
                  full(1, 256), pl.BlockSpec((NSHARD, 256, 128), lambda i: (0, 0, 0)), full(1, 256), full(1, 256)],
        out_specs=pl.BlockSpec((tm, 256), lambda i: (i, 0)),
        out_shape=_sds((N, 256)), name="s5_glu_fwd", compiler_params=_cp(("parallel",)))(y2, h, dsk, w4, bv, bg)


def _s5_glu_bwd(y2, h, dsk, w4, bv, bg, dya):
    tm = 512
    nt = N // tm

    def body(y2_ref, u_ref, d_ref, w4_ref, bv_ref, bg_ref, dya_ref,
             dyp_ref, dud_ref, dd_ref, dw4_ref, dbv_ref, dbg_ref, accv, accg):
        i = pl.program_id(0)

        @pl.when(i == 0)
        def _():
            for r in (dd_ref, accv, accg, dbv_ref, dbg_ref):
                r[...] = jnp.zeros_like(r)

        wv, wg = _glu_halves(w4_ref)
        u = u_ref[...]
        y = y2_ref[0] + y2_ref[1] + d_ref[...] * u
        z = _gelu(y)
        val = _mm(z, wv) + bv_ref[...]
        sig = jax.nn.sigmoid(_mm(z, wg) + bg_ref[...])
        dya = dya_ref[...]
        dval = dya * sig
        dgate = dya * val * sig * (1.0 - sig)
        dz = _mm_nt(dval, wv) + _mm_nt(dgate, wg)
        dy = dz * _gelu_grad(y)
        dyp_ref[...] = dy
        dud_ref[...] = dy * d_ref[...]
        dd_ref[...] += jnp.sum(dy * u, axis=0, keepdims=True)
        accv[...] += _mm_tn(z, dval)
        accg[...] += _mm_tn(z, dgate)
        dbv_ref[...] += jnp.sum(dval, axis=0, keepdims=True)
        dbg_ref[...] += jnp.sum(dgate, axis=0, keepdims=True)

        @pl.when(i == nt - 1)
        def _():
            dw4_ref[0] = accv[:, 0:128].astype(MX)
            dw4_ref[1] = accv[:, 128:256].astype(MX)
            dw4_ref[2] = accg[:, 0:128].astype(MX)
            dw4_ref[3] = accg[:, 128:256].astype(MX)

    full = lambda r, c: pl.BlockSpec((r, c), lambda i: (0, 0))
    row = pl.BlockSpec((tm, 256), lambda i: (i, 0))
    wspec = pl.BlockSpec((NSHARD, 256, 128), lambda i: (0, 0, 0))
    return pl.pallas_call(
        body, grid=(nt,),
        in_specs=[pl.BlockSpec((2, tm, 256), lambda i: (0, i, 0)), row, full(1, 256), wspec, full(1, 256), full(1, 256),
                  row],
        out_specs=[row, row, full(1, 256), wspec, full(1, 256), full(1, 256)],
        out_shape=[_sds((N, 256)), _sds((N, 256)), _sds((1, 256)), _sds((NSHARD, 256, 128), MX), _sds((1, 256)),
                   _sds((1, 256))],
        scratch_shapes=[pltpu.VMEM((256, 256), F32), pltpu.VMEM((256, 256), F32)],
        name="s5_glu_bwd", compiler_params=_cp(("arbitrary",)))(y2, h, dsk, w4, bv, bg, dya)


def _logsig(x):
    return jnp.minimum(x, 0.0) - jnp.log(1.0 + jnp.exp(-jnp.abs(x)))


def _gla_gate_fwd(h, wa, ba):
    tm = 512

    def body(hl_ref, wa_ref, ba_ref, la_ref):
        la_ref[...] = _logsig(_mm(hl_ref[...], wa_ref[...]) + ba_ref[...]) * (1.0 / 16.0)

    return pl.pallas_call(
        body, grid=(N // tm,),
        in_specs=[pl.BlockSpec((tm, 128), lambda i: (i, 14)), pl.BlockSpec((128, 256), lambda i: (0, 0)),
                  pl.BlockSpec((1, 256), lambda i: (0, 0))],
        out_specs=pl.BlockSpec((tm, 256), lambda i: (i, 0)),
        out_shape=_sds((N, 256)), name="gla_gate_fwd", compiler_params=_cp(("parallel",)))(h, wa, ba)


def _gla_gate_bwd(h, wa, ba, dla_f, dla_b):
    tm = 512

    def body(hl_ref, wa_ref, ba_ref, df_ref, db_ref, dhl_ref, dwa_ref, dba_ref):
        i = pl.program_id(0)

        @pl.when(i == 0)
        def _():
            dwa_ref[...] = jnp.zeros_like(dwa_ref)
            dba_ref[...] = jnp.zeros_like(dba_ref)

        hl = hl_ref[...]
        pre = _mm(hl, wa_ref[...]) + ba_ref[...]
        dpre = jnp.concatenate([df_ref[...], db_ref[...]], axis=1) * (1.0 / 16.0) * jax.nn.sigmoid(-pre)
        dhl_ref[...] = _mm_nt(dpre, wa_ref[...])
        dwa_ref[...] += _mm_tn(hl, dpre)
        dba_ref[...] += jnp.sum(dpre, axis=0, keepdims=True)

    row = pl.BlockSpec((tm, 128), lambda i: (i, 0))
    return pl.pallas_call(
        body, grid=(N // tm,),
        in_specs=[pl.BlockSpec((tm, 128), lambda i: (i, 14)), pl.BlockSpec((128, 256), lambda i: (0, 0)),
                  pl.BlockSpec((1, 256), lambda i: (0, 0)), row, row],
        out_specs=[row, pl.BlockSpec((128, 256), lambda i: (0, 0)), pl.BlockSpec((1, 256), lambda i: (0, 0))],
        out_shape=[_sds((N, 128)), _sds((128, 256)), _sds((1, 256))],
        name="gla_gate_bwd", compiler_params=_cp(("arbitrary",)))(h, wa, ba, dla_f, dla_b)


def _gla_chunk(q, k, v, la, st, rev):
    c = GLA_CHUNK
    b = _cums(la, rev)
    bl = jnp.sum(la, axis=0, keepdims=True)
    q_in = q * (32.0 ** -0.5) * jnp.exp(b)
    k_in = k * jnp.exp(-b)
    k_st = k * jnp.exp(bl - b)
    lane_k = lax.broadcasted_iota(jnp.int32, (1, 128), 1) // 32
    lane_v = lax.broadcasted_iota(jnp.int32, (1, 256), 1) // 64
    r = lax.broadcasted_iota(jnp.int32, (c, c), 0)
    cc = lax.broadcasted_iota(jnp.int32, (c, c), 1)
    keep = (cc > r) if rev else (cc <= r)
    qs = jnp.concatenate([jnp.where(lane_k == hd, q_in, 0.0) for hd in range(4)], axis=0)
    a = _dmm_nt(qs, k_in)
    a = jnp.where(jnp.concatenate([keep] * 4, axis=0), a, 0.0)
    o4 = _dmm(a, v)
    o = _dmm_nt(q_in, st)
    for hd in range(4):
        o = o + jnp.where(lane_v == hd, o4[hd * c:(hd + 1) * c], 0.0)
    bd = (lax.broadcasted_iota(jnp.int32, (256, 128), 0) // 64) == (lax.broadcasted_iota(jnp.int32, (256, 128), 1) // 32)
    st_new = jnp.exp(bl) * st + jnp.where(bd, _dmm_tn(v, k_st), 0.0)
    return o, st_new


def _gla_rows(s, c, rev):
    return s * NCHUNK + (NCHUNK - 1 - c if rev else c)


def _gla_fwd(h, la2):
    def body(qf, kf, vf, laf, qb, kb, vb, lab, of_ref, ob_ref, sf_ref, sb_ref, stf, stb):
        @pl.when(pl.program_id(1) == 0)
        def _():
            stf[...] = jnp.zeros_like(stf)
            stb[...] = jnp.zeros_like(stb)

        sf_ref[0] = stf[...]
        sb_ref[0] = stb[...]
        o, sn = _gla_chunk(qf[...], kf[...], vf[...], laf[...], stf[...], False)
        of_ref[...] = o
        stf[...] = sn
        o, sn = _gla_chunk(qb[...], kb[...], vb[...], lab[...], stb[...], True)
        ob_ref[...] = o
        stb[...] = sn

    def specs(rev):
        rw = lambda s, c: _gla_rows(s, c, rev)
        return [pl.BlockSpec((64, 128), lambda s, c: (rw(s, c), 2)), pl.BlockSpec((64, 128), lambda s, c: (rw(s, c), 3)),
                pl.BlockSpec((64, 256), lambda s, c: (rw(s, c), 2)),
                pl.BlockSpec((64, 128), lambda s, c: (rw(s, c), 1 if rev else 0))]

    orow = lambda rev: pl.BlockSpec((64, 256), lambda s, c: (_gla_rows(s, c, rev), 0))
    srow = lambda rev: pl.BlockSpec((1, 256, 128), lambda s, c: (_gla_rows(s, c, rev), 0, 0))
    return pl.pallas_call(
        body, grid=(NSEQ, NCHUNK),
        in_specs=specs(False) + specs(True),
        out_specs=[orow(False), orow(True), srow(False), srow(True)],
        out_shape=[_sds((N, 256)), _sds((N, 256)), _sds((NSEQ * NCHUNK, 256, 128)), _sds((NSEQ * NCHUNK, 256, 128))],
        scratch_shapes=[pltpu.VMEM((256, 128), F32), pltpu.VMEM((256, 128), F32)],
        name="gla_fwd", compiler_params=_cp(("arbitrary", "arbitrary")))(h, h, h, la2, h, h, h, la2)


def _gla_bwd(h, la2, do, sf, sb):
    def body(qf, kf, vf, laf, dof, sfr, qb, kb, vb, lab, dob, sbr,
             dqf, dkf, dvf, dlf, dqb, dkb, dvb, dlb, dstf, dstb):
        @pl.when(pl.program_id(1) == 0)
        def _():
            dstf[...] = jnp.zeros_like(dstf)
            dstb[...] = jnp.zeros_like(dstb)

        def one(q, k, v, la, do_, st, dst, rev, dq, dk, dv, dl):
            _, vjp = jax.vjp(functools.partial(_gla_chunk, rev=rev), q[...], k[...], v[...], la[...], st[0])
            gq, gk, gv, gl, gs = vjp((do_[...], dst[...]))
            dq[...] = gq
            dk[...] = gk
            dv[...] = gv
            dl[...] = gl
            dst[...] = gs

        one(qf, kf, vf, laf, dof, sfr, dstf, False, dqf, dkf, dvf, dlf)
        one(qb, kb, vb, lab, dob, sbr, dstb, True, dqb, dkb, dvb, dlb)

    def specs(rev):
        rw = lambda s, c: _gla_rows(s, c, not rev)
        return [pl.BlockSpec((64, 128), lambda s, c: (rw(s, c), 2)), pl.BlockSpec((64, 128), lambda s, c: (rw(s, c), 3)),
                pl.BlockSpec((64, 256), lambda s, c: (rw(s, c), 2)),
                pl.BlockSpec((64, 128), lambda s, c: (rw(s, c), 1 if rev else 0)),
                pl.BlockSpec((64, 256), lambda s, c: (rw(s, c), 0)),
                pl.BlockSpec((1, 256, 128), lambda s, c: (rw(s, c), 0, 0))]

    def ospecs(rev):
        rw = lambda s, c: _gla_rows(s, c, not rev)
        n = pl.BlockSpec((64, 128), lambda s, c: (rw(s, c), 0))
        return [n, n, pl.BlockSpec((64, 256), lambda s, c: (rw(s, c), 0)), n]

    oshape = [_sds((N, 128)), _sds((N, 128)), _sds((N, 256)), _sds((N, 128))]
    return pl.pallas_call(
        body, grid=(NSEQ, NCHUNK),
        in_specs=specs(False) + specs(True),
        out_specs=ospecs(False) + ospecs(True),
        out_shape=oshape + oshape,
        scratch_shapes=[pltpu.VMEM((256, 128), F32), pltpu.VMEM((256, 128), F32)],
        name="gla_bwd", compiler_params=_cp(("arbitrary", "arbitrary")))(h, h, h, la2, do, sf, h, h, h, la2, do, sb)


def _gla_post(of, ob, r, g):
    o = of + ob
    head = lax.broadcasted_iota(jnp.int32, (1, 256), 1) // 64
    mu = jnp.zeros_like(o)
    for hd in range(4):
        mu = mu + jnp.where(head == hd, jnp.sum(jnp.where(head == hd, o, 0.0), axis=-1, keepdims=True) * (1.0 / 64.0), 0.0)
    xc = o - mu
    var = jnp.zeros_like(o)
    for hd in range(4):
        var = var + jnp.where(head == hd, jnp.sum(jnp.where(head == hd, xc * xc, 0.0), axis=-1, keepdims=True) * (1.0 / 64.0), 0.0)
    return xc * lax.rsqrt(var + LN_EPS) * g * (r * jax.nn.sigmoid(r))


def _gla_post_fwd(of, ob, h, g):
    tm = 512

    def body(of_ref, ob_ref, r_ref, g_ref, y_ref):
        y_ref[...] = _gla_post(of_ref[...], ob_ref[...], r_ref[...], g_ref[...])

    row = pl.BlockSpec((tm, 256), lambda i: (i, 0))
    return pl.pallas_call(
        body, grid=(N // tm,),
        in_specs=[row, row, pl.BlockSpec((tm, 256), lambda i: (i, 3)), pl.BlockSpec((1, 256), lambda i: (0, 0))],
        out_specs=row, out_shape=_sds((N, 256)), name="gla_post_fwd", compiler_params=_cp(("parallel",)))(of, ob, h, g)


def _gla_post_bwd(of, ob, h, g, dyb):
    tm = 512

    def body(of_ref, ob_ref, r_ref, g_ref, dy_ref, do_ref, dr_ref, dg_ref):
        @pl.when(pl.program_id(0) == 0)
        def _():
            dg_ref[...] = jnp.zeros_like(dg_ref)

        _, vjp = jax.vjp(_gla_post, of_ref[...], ob_ref[...], r_ref[...], g_ref[...])
        go, _, gr, gg = vjp(dy_ref[...])
        do_ref[...] = go
        dr_ref[...] = gr
        dg_ref[...] += gg

    row = pl.BlockSpec((tm, 256), lambda i: (i, 0))
    one = pl.BlockSpec((1, 256), lambda i: (0, 0))
    return pl.pallas_call(
        body, grid=(N // tm,),
        in_specs=[row, row, pl.BlockSpec((tm, 256), lambda i: (i, 3)), one, row],
        out_specs=[row, row, one], out_shape=[_sds((N, 256)), _sds((N, 256)), _sds((1, 256))],
        name="gla_post_bwd", compiler_params=_cp(("arbitrary",)))(of, ob, h, g, dyb)


def _rope_tables(width):
    pos = jnp.arange(L, dtype=F32)
    inv_freq = ROPE_THETA ** (-jnp.arange(0, ROT, 2, dtype=F32) / ROT)
    ang = pos[:, None] * inv_freq[None, :]
    cos, sin = jnp.cos(ang), jnp.sin(ang)
    one = jnp.ones((L, 64 - ROT), F32)
    zero = jnp.zeros((L, 64 - ROT), F32)
    z8 = jnp.zeros((L, ROT // 2), F32)
    c = jnp.concatenate([cos, cos, one], axis=1)
    sa = jnp.concatenate([z8, sin, zero], axis=1)
    sb = jnp.concatenate([-sin, z8, zero], axis=1)
    rep = width // 64
    return jnp.stack([jnp.tile(c, (1, rep)), jnp.tile(sa, (1, rep)), jnp.tile(sb, (1, rep))])


def _rope(t, tab):
    w = t.shape[-1]
    return t * tab[0] + pltpu.roll(t, ROT // 2, 1) * tab[1] + pltpu.roll(t, w - ROT // 2, 1) * tab[2]


def _rope_t(g, tab):
    w = g.shape[-1]
    return g * tab[0] + pltpu.roll(g * tab[1], w - ROT // 2, 1) + pltpu.roll(g * tab[2], ROT // 2, 1)


def _swa_pad_kv(kv_ref, tk_ref, kpad, vpad):
    z = jnp.zeros((SWA_BLK, 128), F32)
    kpad[0:SWA_BLK] = z
    vpad[0:SWA_BLK] = z
    kpad[SWA_BLK + L:] = z
    vpad[SWA_BLK + L:] = z
    kpad[SWA_BLK:SWA_BLK + L] = _rope(kv_ref[:, 0:128], tk_ref[...])
    vpad[SWA_BLK:SWA_BLK + L] = kv_ref[:, 128:256]


def _swa_expand(x, hk):
    lane = lax.broadcasted_iota(jnp.int32, x.shape, 1)
    sw = pltpu.roll(x, 64, 1)
    pair = jnp.where(lane < 64, x, sw) if hk == 0 else jnp.where(lane < 64, sw, x)
    return jnp.concatenate([pair, pair], axis=1)


def _swa_fold(x, hk):
    a = x[:, 0:128] + x[:, 128:256]
    t = a + pltpu.roll(a, 64, 1)
    lane = lax.broadcasted_iota(jnp.int32, a.shape, 1)
    return jnp.where((lane < 64) if hk == 0 else (lane >= 64), t, 0.0)


def _swa_probs(q2, kexp, n, sink_ref, hk):
    slot = lax.broadcasted_iota(jnp.int32, (1, 256), 1) // 64
    qs = jnp.concatenate([jnp.where(slot == g, q2, 0.0) for g in range(4)], axis=0)
    s = _mm_nt(qs, kexp) * 0.125
    i = lax.broadcasted_iota(jnp.int32, (SWA_BLK, 3 * SWA_BLK), 0)
    j = lax.broadcasted_iota(jnp.int32, (SWA_BLK, 3 * SWA_BLK), 1)
    kpos = n * SWA_BLK - SWA_BLK + j
    ok = (j - i >= 0) & (j - i <= 2 * SWA_BLK) & (kpos >= 0) & (kpos < L)
    s = jnp.where(jnp.concatenate([ok] * 4, axis=0), s, NEG_BIG)
    rowg = lax.broadcasted_iota(jnp.int32, (4 * SWA_BLK, 1), 0) // SWA_BLK
    sink = jnp.zeros((4 * SWA_BLK, 1), F32)
    for g in range(4):
        sink = jnp.where(rowg == g, sink_ref[hk * 4 + g], sink)
    m = jnp.maximum(jnp.max(s, axis=-1, keepdims=True), sink)
    p = jnp.exp(s - m)
    ps = jnp.exp(sink - m)
    inv = 1.0 / (jnp.sum(p, axis=-1, keepdims=True) + ps)
    return qs, p * inv, ps * inv, slot, rowg


def _swa_qtab(tk_ref, r0):
    return [jnp.concatenate([tk_ref[i, pl.ds(r0, SWA_BLK), :]] * 4, axis=1) for i in range(3)]


def _swa_fwd(h, tk, sink):
    def body(sink_ref, q_ref, kv_ref, tk_ref, y_ref, kpad, vpad):
        n = pl.program_id(1)

        @pl.when(n == 0)
        def _():
            _swa_pad_kv(kv_ref, tk_ref, kpad, vpad)

        r0 = pl.multiple_of(n * SWA_BLK, SWA_BLK)
        q = _rope(q_ref[...], _swa_qtab(tk_ref, r0))
        kb = kpad[pl.ds(r0, 3 * SWA_BLK), :]
        vb = vpad[pl.ds(r0, 3 * SWA_BLK), :]
        for hk in range(2):
            _, p, _, slot, _ = _swa_probs(q[:, hk * 256:(hk + 1) * 256], _swa_expand(kb, hk), n, sink_ref, hk)
            o4 = _mm(p, _swa_expand(vb, hk))
            o = jnp.zeros((SWA_BLK, 256), F32)
            for g in range(4):
                o = o + jnp.where(slot == g, o4[g * SWA_BLK:(g + 1) * SWA_BLK], 0.0)
            y_ref[:, hk * 256:(hk + 1) * 256] = o

    return pl.pallas_call(
        body,
        grid_spec=pltpu.PrefetchScalarGridSpec(
            num_scalar_prefetch=1, grid=(NSEQ, NBLK),
            in_specs=[pl.BlockSpec((SWA_BLK, 512), lambda s, n, sk: (s * NBLK + n, 2)),
                      pl.BlockSpec((L, 256), lambda s, n, sk: (s, 6)),
                      pl.BlockSpec((3, L, 128), lambda s, n, sk: (0, 0, 0))],
            out_specs=pl.BlockSpec((SWA_BLK, 512), lambda s, n, sk: (s * NBLK + n, 0)),
            scratch_shapes=[pltpu.VMEM((L + 2 * SWA_BLK, 128), F32), pltpu.VMEM((L + 2 * SWA_BLK, 128), F32)]),
        out_shape=_sds((N, 512)), name="swa_fwd", compiler_params=_cp(("arbitrary", "arbitrary")))(sink, h, h, tk)


def _swa_bwd(h, tk, sink, dyc):
    def body(sink_ref, q_ref, kv_ref, tk_ref, dy_ref, dq_ref, dkv_ref, dsink_ref, kpad, vpad, dkacc, dvacc):
        sq = pl.program_id(0)
        n = pl.program_id(1)

        @pl.when(n == 0)
        def _():
            _swa_pad_kv(kv_ref, tk_ref, kpad, vpad)
            dkacc[...] = jnp.zeros_like(dkacc)
            dvacc[...] = jnp.zeros_like(dvacc)

        @pl.when((n == 0) & (sq == 0))
        def _():
            dsink_ref[...] = jnp.zeros_like(dsink_ref)

        r0 = pl.multiple_of(n * SWA_BLK, SWA_BLK)
        tq = _swa_qtab(tk_ref, r0)
        q = _rope(q_ref[...], tq)
        kb = kpad[pl.ds(r0, 3 * SWA_BLK), :]
        vb = vpad[pl.ds(r0, 3 * SWA_BLK), :]
        dk = jnp.zeros((3 * SWA_BLK, 128), F32)
        dv = jnp.zeros((3 * SWA_BLK, 128), F32)
        hrow = lax.broadcasted_iota(jnp.int32, (8, 128), 0)
        dsk = jnp.zeros((8, 128), F32)
        for hk in range(2):
            kexp = _swa_expand(kb, hk)
            vexp = _swa_expand(vb, hk)
            qs, p, ps, slot, rowg = _swa_probs(q[:, hk * 256:(hk + 1) * 256], kexp, n, sink_ref, hk)
            dy2 = dy_ref[:, hk * 256:(hk + 1) * 256]
            dos = jnp.concatenate([jnp.where(slot == g, dy2, 0.0) for g in range(4)], axis=0)
            dp = _mm_nt(dos, vexp)
            delta = jnp.sum(p * dp, axis=-1, keepdims=True)
            ds = p * (dp - delta) * 0.125
            dsr = -ps * delta
            for g in range(4):
                dsk = dsk + jnp.where(hrow == hk * 4 + g, jnp.sum(jnp.where(rowg == g, dsr, 0.0), axis=0, keepdims=True), 0.0)
            dq4 = _mm(ds, kexp)
            dq2 = jnp.zeros((SWA_BLK, 256), F32)
            for g in range(4):
                dq2 = dq2 + jnp.where(slot == g, dq4[g * SWA_BLK:(g + 1) * SWA_BLK], 0.0)
            dq_ref[:, hk * 256:(hk + 1) * 256] = dq2
            dk = dk + _swa_fold(_mm_tn(ds, qs), hk)
            dv = dv + _swa_fold(_mm_tn(p, dos), hk)
        dq_ref[...] = _rope_t(dq_ref[...], tq)
        dkacc[pl.ds(r0, 3 * SWA_BLK), :] += dk
        dvacc[pl.ds(r0, 3 * SWA_BLK), :] += dv
        dsink_ref[...] += dsk

        @pl.when(n == NBLK - 1)
        def _():
            dkv_ref[:, 0:128] = _rope_t(dkacc[SWA_BLK:SWA_BLK + L], tk_ref[...])
            dkv_ref[:, 128:256] = dvacc[SWA_BLK:SWA_BLK + L]

    blk = lambda col: pl.BlockSpec((SWA_BLK, 512), lambda s, n, sk: (s * NBLK + n, col))
    pad = pltpu.VMEM((L + 2 * SWA_BLK, 128), F32)
    return pl.pallas_call(
        body,
        grid_spec=pltpu.PrefetchScalarGridSpec(
            num_scalar_prefetch=1, grid=(NSEQ, NBLK),
            in_specs=[blk(2), pl.BlockSpec((L, 256), lambda s, n, sk: (s, 6)),
                      pl.BlockSpec((3, L, 128), lambda s, n, sk: (0, 0, 0)), blk(0)],
            out_specs=[blk(0), pl.BlockSpec((L, 256), lambda s, n, sk: (s, 0)),
                       pl.BlockSpec((8, 128), lambda s, n, sk: (0, 0))],
            scratch_shapes=[pad, pad, pad, pad]),
        out_shape=[_sds((N, 512)), _sds((N, 256)), _sds((8, 128))],
        name="swa_bwd", compiler_params=_cp(("arbitrary", "arbitrary")))(sink, h, h, tk, dyc)


def _outproj_fwd(ya, yb, yc, x, wo, g, b):
    tm = 512

    def body(ya_ref, yb_ref, yc_ref, x_ref, wo_ref, g_ref, b_ref, s_ref, x1_ref):
        mix = _mm(ya_ref[...], wo_ref[0:256]) + _mm(yb_ref[...], wo_ref[256:512]) + _mm(yc_ref[...], wo_ref[512:1024])
        s = ALPHA * x_ref[...] + mix
        s_ref[...] = s
        x1_ref[...] = _ln_fwd(s, g_ref[...], b_ref[...])

    row = lambda w_: pl.BlockSpec((tm, w_), lambda i: (i, 0))
    one = pl.BlockSpec((1, D), lambda i: (0, 0))
    return pl.pallas_call(
        body, grid=(N // tm,),
        in_specs=[row(256), row(256), row(512), row(D), pl.BlockSpec((D, D), lambda i: (0, 0)), one, one],
        out_specs=[row(D), row(D)], out_shape=[_sds((N, D)), _sds((N, D))],
        name="outproj_fwd", compiler_params=_cp(("parallel",)))(ya, yb, yc, x, wo, g, b)


def _outproj_bwd(dx1, s1, ya, yb, yc, wo, g):
    tm = 512
    nt = N // tm

    def body(dx1_ref, s_ref, ya_ref, yb_ref, yc_ref, wo_ref, g_ref,
             dya_ref, dyb_ref, dyc_ref, dxp_ref, dwo_ref, dg_ref, db_ref, acc):
        i = pl.program_id(0)

        @pl.when(i == 0)
        def _():
            acc[...] = jnp.zeros_like(acc)
            dg_ref[...] = jnp.zeros_like(dg_ref)
            db_ref[...] = jnp.zeros_like(db_ref)

        ds, dg, db = _ln_bwd(dx1_ref[...], s_ref[...], g_ref[...])
        dg_ref[...] += dg
        db_ref[...] += db
        dxp_ref[...] = ALPHA * ds
        dy = _mm_nt(ds, wo_ref[...])
        dya_ref[...] = dy[:, 0:256]
        dyb_ref[...] = dy[:, 256:512]
        dyc_ref[...] = dy[:, 512:1024]
        acc[0:256] += _mm_tn(ya_ref[...], ds)
        acc[256:512] += _mm_tn(yb_ref[...], ds)
        acc[512:1024] += _mm_tn(yc_ref[...], ds)

        @pl.when(i == nt - 1)
        def _():
            dwo_ref[...] = acc[...].astype(MX)

    row = lambda w_: pl.BlockSpec((tm, w_), lambda i: (i, 0))
    one = pl.BlockSpec((1, D), lambda i: (0, 0))
    full = pl.BlockSpec((D, D), lambda i: (0, 0))
    return pl.pallas_call(
        body, grid=(nt,),
        in_specs=[row(D), row(D), row(256), row(256), row(512), full, one],
        out_specs=[row(256), row(256), row(512), row(D), full, one, one],
        out_shape=[_sds((N, 256)), _sds((N, 256)), _sds((N, 512)), _sds((N, D)), _sds((D, D), MX), _sds((1, D)), _sds((1, D))],
        scratch_shapes=[pltpu.VMEM((D, D), F32)],
        name="outproj_bwd", compiler_params=_cp(("arbitrary",)))(dx1, s1, ya, yb, yc, wo, g)


def _ffn_fwd(x1, w1, w2, g, b):
    tm = FFN_TM

    def body(x_ref, w1_ref, w2_ref, g_ref, b_ref, a_ref, s_ref, x2_ref):
        j = pl.program_id(1)

        @pl.when(j == 0)
        def _():
            s_ref[...] = ALPHA * x_ref[...]

        a = _mm(x_ref[...], w1_ref[0])
        a_ref[...] = a.astype(MX)
        hid = jnp.square(jnp.maximum(a, 0.0))
        s_ref[...] += _mm(hid, w2_ref[0])

        @pl.when(j == NSHARD - 1)
        def _():
            x2_ref[...] = _ln_fwd(s_ref[...], g_ref[...], b_ref[...])

    row = pl.BlockSpec((tm, D), lambda i, j: (i, 0))
    wj = pl.BlockSpec((1, D, D), lambda i, j: (j, 0, 0))
    one = pl.BlockSpec((1, D), lambda i, j: (0, 0))
    return pl.pallas_call(
        body, grid=(N // tm, NSHARD),
        in_specs=[row, wj, wj, one, one],
        out_specs=[pl.BlockSpec((tm, D), lambda i, j: (i, j)), row, row],
        out_shape=[_sds((N, DFF), MX), _sds((N, D)), _sds((N, D))],
        name="ffn_fwd", compiler_params=_cp(("parallel", "arbitrary"), FFN_VMEM))(x1, w1, w2, g, b)


def _ffn_bwd_act(dy, s2, a, w1, w2, g):
    tm = FFN_TM

    def body(dy_ref, s_ref, a_ref, w1_ref, w2_ref, g_ref, da_ref, ds_ref, dx1_ref, dg_ref, db_ref):
        i = pl.program_id(0)
        j = pl.program_id(1)

        @pl.when((i == 0) & (j == 0))
        def _():
            dg_ref[...] = jnp.zeros_like(dg_ref)
            db_ref[...] = jnp.zeros_like(db_ref)

        @pl.when(j == 0)
        def _():
            ds, dg, db = _ln_bwd(dy_ref[...], s_ref[...], g_ref[...])
            ds_ref[...] = ds.astype(MX)
            dg_ref[...] += dg
            db_ref[...] += db
            dx1_ref[...] = ALPHA * ds

        dhid = _mm_nt(ds_ref[...], w2_ref[0])
        da = dhid * 2.0 * jnp.maximum(a_ref[...].astype(F32), 0.0)
        da_ref[...] = da.astype(MX)
        dx1_ref[...] += _mm_nt(da, w1_ref[0])

    row = pl.BlockSpec((tm, D), lambda i, j: (i, 0))
    col = pl.BlockSpec((tm, D), lambda i, j: (i, j))
    wj = pl.BlockSpec((1, D, D), lambda i, j: (j, 0, 0))
    one = pl.BlockSpec((1, D), lambda i, j: (0, 0))
    return pl.pallas_call(
        body, grid=(N // tm, NSHARD),
        in_specs=[row, row, col, wj, wj, one],
        out_specs=[col, row, row, one, one],
        out_shape=[_sds((N, DFF), MX), _sds((N, D), MX), _sds((N, D)), _sds((1, D)), _sds((1, D))],
        name="ffn_bwd_act", compiler_params=_cp(("arbitrary", "arbitrary"), FFN_VMEM))(dy, s2, a, w1, w2, g)


def _ffn_bwd_w(x1, da, a, ds):
    tm = FFN_TM_W
    nt = N // tm

    def body(x_ref, da_ref, a_ref, ds_ref, dw1_ref, dw2_ref, acc1, acc2):
        i = pl.program_id(1)

        @pl.when(i == 0)
        def _():
            acc1[...] = jnp.zeros_like(acc1)
            acc2[...] = jnp.zeros_like(acc2)

        acc1[...] += _mm_tn(x_ref[...], da_ref[...])
        hid = jnp.square(jnp.maximum(a_ref[...].astype(F32), 0.0))
        acc2[...] += _mm_tn(hid, ds_ref[...])

        @pl.when(i == nt - 1)
        def _():
            dw1_ref[0] = acc1[...].astype(MX)
            dw2_ref[0] = acc2[...].astype(MX)

    row = pl.BlockSpec((tm, D), lambda j, i: (i, 0))
    col = pl.BlockSpec((tm, D), lambda j, i: (i, j))
    wj = pl.BlockSpec((1, D, D), lambda j, i: (j, 0, 0))
    return pl.pallas_call(
        body, grid=(NSHARD, nt),
        in_specs=[row, col, col, row], out_specs=[wj, wj],
        out_shape=[_sds((NSHARD, D, D), MX), _sds((NSHARD, D, D), MX)],
        scratch_shapes=[pltpu.VMEM((D, D), F32), pltpu.VMEM((D, D), F32)],
        name="ffn_bwd_w", compiler_params=_cp(("parallel", "arbitrary")))(x1, da, a, ds)


def _loss_head(y, target):
    tm = 512

    def body(y_ref, t_ref, dy_ref, l_ref):
        @pl.when(pl.program_id(0) == 0)
        def _():
            l_ref[...] = jnp.zeros_like(l_ref)

        e = y_ref[...] - t_ref[...]
        dy_ref[...] = e * (1.0 / D)
        l_ref[...] += jnp.sum(jnp.sum(e * e, axis=1, keepdims=True), axis=0, keepdims=True) * (0.5 / D)

    row = pl.BlockSpec((tm, D), lambda i: (i, 0))
    return pl.pallas_call(
        body, grid=(N // tm,), in_specs=[row, row],
        out_specs=[row, pl.BlockSpec((8, 128), lambda i: (0, 0))],
        out_shape=[_sds((N, D)), _sds((8, 128))], name="loss_head", compiler_params=_cp(("arbitrary",)))(y, target)


def _s5_discretize(a_re, a_im, log_step, b_re, b_im):
    lam = lax.complex(a_re, a_im)
    lam_bar = jnp.exp(lam * jnp.exp(log_step))
    b_bar = ((lam_bar - 1.0) / lam)[..., None] * lax.complex(b_re, b_im)
    return jnp.real(lam_bar), jnp.imag(lam_bar), jnp.real(b_bar), jnp.imag(b_bar)


def _s5_in_blocks(b):
    e = jnp.eye(8, dtype=F32)
    return jnp.einsum('ij,zbjph->zbihjp', e, b.reshape(2, 2, 8, S5_P, S5_H)).reshape(2, 2, 128, SW)


def _s5_in_unblocks(d):
    return jnp.einsum('zbihip->zbiph', d.reshape(2, 2, 8, S5_H, 8, S5_P)).reshape(2, S5_G, S5_P, S5_H)


def _s5_out_blocks(c):
    e = jnp.eye(8, dtype=F32)
    return jnp.einsum('ij,zbjhp->zbjpih', e, c.reshape(2, 2, 8, S5_H, S5_P)).reshape(2, 2, SW, 128)


def _s5_out_unblocks(d):
    return jnp.einsum('zbipih->zbihp', d.reshape(2, 2, 8, S5_P, 8, S5_H)).reshape(2, S5_G, S5_H, S5_P)


def _gate_weight(w_a):
    z = jnp.zeros((16, 128), F32)
    top = jnp.concatenate([w_a[0], z], axis=1)
    bot = jnp.concatenate([z, w_a[1]], axis=1)
    return jnp.concatenate([top, bot, jnp.zeros((96, 256), F32)], axis=0)


def _layer_prep(p):
    lr, li, br, bi = _s5_discretize(p["s5_a_re"], p["s5_a_im"], p["s5_log_step"], p["s5_b_re"], p["s5_b_im"])
    q = dict(p)
    q["bre"] = _s5_in_blocks(br).astype(MX)
    q["bim"] = _s5_in_blocks(bi).astype(MX)
    q["cre"] = _s5_out_blocks(p["s5_c_re"]).astype(MX)
    q["cim"] = _s5_out_blocks(p["s5_c_im"]).astype(MX)
    mr, mi = lr.reshape(2, 1024), li.reshape(2, 1024)
    q["tab"] = jnp.stack([_scan_tables(mr[0], mi[0], False), _scan_tables(mr[1], mi[1], True)])
    q["tabc"] = jnp.stack([_scan_tables(mr[0], -mi[0], True), _scan_tables(mr[1], -mi[1], False)])
    q["dsk"] = p["s5_d"].reshape(1, 256)
    q["wa"] = _gate_weight(p["gla_w_a"]).astype(MX)
    q["ba"] = p["gla_b_a"].reshape(1, 256)
    q["lng"] = p["gla_ln_g"].reshape(1, 256)
    q["bv"] = p["s5_b_glu"][:256].reshape(1, 256)
    q["bg"] = p["s5_b_glu"][256:].reshape(1, 256)
    for k in ("ln1_g", "ln1_b", "ln2_g", "ln2_b"):
        q[k] = p[k].reshape(1, D)
    return q


def _layer_fwd(x, q, tk, fetch):
    q["w_in"] = fetch("w_in", x)
    h = _inproj_fwd(x, q["w_in"])
    hre, him, y2 = _s5_fwd(h, q["bre"], q["bim"], q["cre"], q["cim"], q["tab"])
    q["w4"] = fetch("s5_w_glu", y2)
    ya = _s5_glu_fwd(y2, h, q["dsk"], q["w4"], q["bv"], q["bg"])
    la2 = _gla_gate_fwd(h, q["wa"], q["ba"])
    of, ob, sf, sb = _gla_fwd(h, la2)
    yb = _gla_post_fwd(of, ob, h, q["lng"])
    yc = _swa_fwd(h, tk, q["swa_sink"])
    q["w_out"] = fetch("w_out", yc)
    s1, x1 = _outproj_fwd(ya, yb, yc, x, q["w_out"], q["ln1_g"], q["ln1_b"])
    q["w_ff1"] = fetch("w_ff1", x1)
    q["w_ff2"] = fetch("w_ff2", x1)
    a, s2, x2 = _ffn_fwd(x1, q["w_ff1"], q["w_ff2"], q["ln2_g"], q["ln2_b"])
    saved = dict(x=x, h=h, hre=hre, him=him, y2=y2, ya=ya, la2=la2, of=of, ob=ob, sf=sf, sb=sb, yb=yb, yc=yc,
                 s1=s1, x1=x1, a=a, s2=s2)
    return x2, saved


def _layer_bwd(dy, q, sv, tk, emit):
    g = {}
    da, ds2, dx1, g["dg2"], g["db2"] = _ffn_bwd_act(dy, sv["s2"], sv["a"], q["w_ff1"], q["w_ff2"], q["ln2_g"])
    dw1, dw2 = _ffn_bwd_w(sv["x1"], da, sv["a"], ds2)
    tie = emit(dict(w_ff1=dw1, w_ff2=dw2))
    dya, dyb, dyc, dxp, dwo, g["dg1"], g["db1"] = _outproj_bwd(dx1, sv["s1"], sv["ya"], sv["yb"], sv["yc"],
                                                               q["w_out"], q["ln1_g"] + tie)
    h = sv["h"]
    daq, dakv, g["dsink"] = _swa_bwd(h, tk, q["swa_sink"], dyc)
    do, gr, g["dlng"] = _gla_post_bwd(sv["of"], sv["ob"], h, q["lng"], dyb)
    gq_f, gk_f, gv_f, gl_f, gq_b, gk_b, gv_b, gl_b = _gla_bwd(h, sv["la2"], do, sv["sf"], sv["sb"])
    dhl, g["dwa"], g["dba"] = _gla_gate_bwd(h, q["wa"], q["ba"], gl_f, gl_b)
    dyp, dud, g["dd"], dw4, g["dbv"], g["dbg"] = _s5_glu_bwd(sv["y2"], h, q["dsk"], q["w4"], q["bv"], q["bg"], dya)
    tie = emit(dict(w_out=dwo.reshape(NSHARD, D // NSHARD, D), s5_w_glu=dw4))
    du2, g["dbre"], g["dbim"], g["dcre"], g["dcim"], g["dmu"] = _s5_bwd(
        h, dyp, sv["hre"], sv["him"], q["bre"], q["bim"], q["cre"], q["cim"], q["tabc"] + tie)
    dx, dwt = _inproj_bwd(sv["x"], q["w_in"], dxp, du2, dud, gq_f, gq_b, gk_f, gk_b, gv_f, gv_b, gr, daq, dakv, dhl)
    tie = emit(dict(w_in=_in_rows_back(dwt)))
    return dx, g, tie


NATIVE = ("dmu", "dbre", "dbim", "dcre", "dcim", "dd", "dbv", "dbg", "dwa", "dba", "dlng", "dsink",
          "dg1", "db1", "dg2", "db2")


def _finish_small(n, w):
    g = {}
    dmu = jnp.sum(n["dmu"], axis=4)
    dlr = dmu[:, :, :, 0].reshape(DEPTH, 2, S5_G, S5_P)
    dli = dmu[:, :, :, 1].reshape(DEPTH, 2, S5_G, S5_P)

    def unblock(c, perm, shape):
        return c.reshape(DEPTH, 2, 2, S5_H, 8, S5_P).transpose(perm).reshape(shape)

    b_shape, c_shape = (DEPTH, 2, S5_G, S5_P, S5_H), (DEPTH, 2, S5_G, S5_H, S5_P)
    _, vjp = jax.vjp(_s5_discretize, w["s5_a_re"], w["s5_a_im"], w["s5_log_step"], w["s5_b_re"], w["s5_b_im"])
    (g["s5_a_re"], g["s5_a_im"], g["s5_log_step"], g["s5_b_re"], g["s5_b_im"]) = vjp(
        (dlr, dli, unblock(n["dbre"], (0, 1, 2, 4, 5, 3), b_shape), unblock(n["dbim"], (0, 1, 2, 4, 5, 3), b_shape)))
    g["s5_c_re"] = unblock(n["dcre"], (0, 1, 2, 4, 3, 5), c_shape)
    g["s5_c_im"] = unblock(n["dcim"], (0, 1, 2, 4, 3, 5), c_shape)
    g["s5_d"] = n["dd"].reshape(DEPTH, S5_G, S5_H)
    g["s5_b_glu"] = jnp.concatenate([n["dbv"], n["dbg"]], axis=2).reshape(DEPTH, 512)
    g["gla_w_a"] = jnp.stack([n["dwa"][:, 0:16, 0:128], n["dwa"][:, 16:32, 128:256]], axis=1)
    g["gla_b_a"] = n["dba"].reshape(DEPTH, 2, 128)
    g["gla_ln_g"] = n["dlng"].reshape(DEPTH, 256)
    g["swa_sink"] = n["dsink"][:, :, 0]
    for k, s in (("ln1_g", "dg1"), ("ln1_b", "db1"), ("ln2_g", "dg2"), ("ln2_b", "db2")):
        g[k] = n[s].reshape(DEPTH, D)
    return g


def _local_step(x, target, qs, tk, fetch, emit):
    saved = []
    for l, q in enumerate(qs):
        x, sv = _layer_fwd(x, q, tk, functools.partial(fetch, l))
        saved.append(sv)
    dy, lacc = _loss_head(x, target)
    smalls = [None] * DEPTH
    tie = 0.0
    for l in reversed(range(DEPTH)):
        qs[l]["ln2_g"] = qs[l]["ln2_g"] + tie
        dy, smalls[l], tie = _layer_bwd(dy, qs[l], saved[l], tk, functools.partial(emit, l))
    smalls[0]["db2"] = smalls[0]["db2"] + tie
    return lacc[0, 0], dy, smalls


BIG = ("w_in", "s5_w_glu", "w_out", "w_ff1", "w_ff2")
SMALL = ("s5_a_re", "s5_a_im", "s5_log_step", "s5_b_re", "s5_b_im", "s5_c_re", "s5_c_im", "s5_d", "s5_b_glu",
         "gla_w_a", "gla_b_a", "gla_ln_g", "swa_sink", "ln1_g", "ln1_b", "ln2_g", "ln2_b")
ANY = pl.BlockSpec(memory_space=pl.ANY)


def _place():
    x, y, c = lax.axis_index("x"), lax.axis_index("y"), lax.axis_index("c")
    return x, y, c, [(1 - x, y), (x, 1 - y), (1 - x, 1 - y)]


HBM = pl.BlockSpec(memory_space=pltpu.HBM)
SEMS = pl.BlockSpec(memory_space=pltpu.SEMAPHORE)
EFFECT = pltpu.SideEffectType.DATAFLOW_SIDE_EFFECTING


def _push_copies(ins, lands, send, recv, gather, sending):
    x, y, c, chips = _place()
    me = 2 * x + y
    out = []
    for a in range(len(lands)):
        for j, (px, py) in enumerate(chips):
            peer = 2 * px + py
            src = lands[a].at[me] if gather else ins[a].at[peer if sending else me]
            dst = lands[a].at[me if sending else peer]
            out.append(pltpu.make_async_remote_copy(src_ref=src, dst_ref=dst, send_sem=send.at[3 * a + j],
                                                    recv_sem=recv.at[3 * a + j], device_id=(px, py, c),
                                                    device_id_type=MESH))
    return out


def _push_start(name, arrs, gather):
    n = len(arrs)
    ops = list(arrs) if gather else list(arrs) + [lax.empty(s.shape, s.dtype) for s in arrs]
    m = len(ops)

    def body(*refs):
        ins, lnd = (refs[:n], refs[:n]) if gather else (refs[:n], refs[n:m])
        for cp in _push_copies(ins, lnd, refs[m], refs[m + 1], gather, True):
            cp.start()
        refs[-1][...] = jnp.zeros((8, 128), F32)

    ops = [pltpu.with_memory_space_constraint(t, pltpu.HBM) for t in ops]
    res = pl.pallas_call(
        body, name=name,
        out_shape=(pltpu.SemaphoreType.DMA((3 * n,)), pltpu.SemaphoreType.DMA((3 * n,)),
                   *[pltpu.HBM(t.shape, t.dtype) for t in ops], _sds((8, 128))),
        in_specs=[HBM] * m,
        out_specs=(SEMS, SEMS, *[HBM] * m, pl.BlockSpec(memory_space=pltpu.VMEM)),
        input_output_aliases={i: 2 + i for i in range(m)},
        compiler_params=pltpu.CompilerParams(has_side_effects=EFFECT))(*ops)
    return res[0], res[1], list(res[2:2 + m]), res[-1]


def _push_wait(name, started, after, gather):
    send, recv, ops, _ = started
    m = len(ops)
    n = m if gather else m // 2

    def body(*refs):
        ins, lnd = (refs[:n], refs[:n]) if gather else (refs[:n], refs[n:m])
        for cp in _push_copies(ins, lnd, refs[m], refs[m + 1], gather, False):
            cp.wait_send()
            cp.wait_recv()

    res = pl.pallas_call(
        body, name=name,
        out_shape=[pltpu.HBM(t.shape, t.dtype) for t in ops],
        in_specs=[HBM] * m + [SEMS, SEMS, ANY], out_specs=[HBM] * m,
        input_output_aliases={i: i for i in range(m)},
        compiler_params=pltpu.CompilerParams(has_side_effects=EFFECT))(*ops, send, recv, after)
    return list(res)


def _row_tile(rows):
    return max(t for t in range(8, min(rows, 512) + 1, 8) if rows % t == 0)


def _cast_to_slot(me, w, l):
    _, rows, cols = w.shape
    tr = _row_tile(rows)

    def body(me_ref, w_ref, o_ref):
        o_ref[0] = w_ref[0].astype(MX)

    return pl.pallas_call(
        body,
        grid_spec=pltpu.PrefetchScalarGridSpec(
            num_scalar_prefetch=1, grid=(rows // tr,),
            in_specs=[pl.BlockSpec((1, tr, cols), lambda i, me_: (l, i, 0))],
            out_specs=pl.BlockSpec((1, tr, cols), lambda i, me_: (me_[0], i, 0))),
        out_shape=_sds((NSHARD, rows, cols), MX), name="cast_to_slot", compiler_params=_cp(("arbitrary",)))(me, w)


def _sum_sources(me, recv, own):
    _, rows, cols = recv[0].shape
    tr = min(_row_tile(rows), 256) if rows % 256 == 0 else _row_tile(rows)
    nt = rows // tr

    def body(me_ref, *refs):
        o_ref = refs[-1]
        for l in range(DEPTH):
            @pl.when(pl.program_id(0) == l)
            def _():
                r_ref, own_ref = refs[2 * l], refs[2 * l + 1]
                part = [jnp.where(me_ref[0] == s, own_ref[0], r_ref[s]).astype(F32) for s in range(NSHARD)]
                o_ref[...] = ((part[0] + part[1]) + part[2]) + part[3]

    in_specs = []
    for l in range(DEPTH):
        pick = lambda g, i, me_, l=l: jnp.where(g == l, i, jnp.where(g < l, 0, nt - 1))
        in_specs += [pl.BlockSpec((NSHARD, tr, cols), lambda g, i, me_, pick=pick: (0, pick(g, i, me_), 0)),
                     pl.BlockSpec((1, tr, cols), lambda g, i, me_, pick=pick: (me_[0], pick(g, i, me_), 0))]
    return pl.pallas_call(
        body,
        grid_spec=pltpu.PrefetchScalarGridSpec(
            num_scalar_prefetch=1, grid=(DEPTH, nt), in_specs=in_specs,
            out_specs=pl.BlockSpec((tr, cols), lambda g, i, me_: (g * nt + i, 0))),
        out_shape=_sds((DEPTH * rows, cols)), name="sum_sources",
        compiler_params=_cp(("arbitrary", "arbitrary")))(me, *[t for l in range(DEPTH) for t in (recv[l], own[l])])


def _swap_sibling(arrs):
    n = len(arrs)

    def body(*refs):
        ins, outs = refs[:n], refs[n:2 * n]
        send, recv = refs[2 * n:]
        x, y, c, _ = _place()
        cps = [pltpu.make_async_remote_copy(src_ref=ins[a], dst_ref=outs[a], send_sem=send.at[a], recv_sem=recv.at[a],
                                            device_id=(x, y, 1 - c), device_id_type=MESH) for a in range(n)]
        for cp in cps:
            cp.start()
        for cp in cps:
            cp.wait()

    return pl.pallas_call(
        body, in_specs=[ANY] * n, out_specs=[ANY] * n, out_shape=[_sds(a.shape, a.dtype) for a in arrs],
        scratch_shapes=[pltpu.SemaphoreType.DMA((n,)), pltpu.SemaphoreType.DMA((n,))],
        name="swap_sibling")(*arrs)


def _allreduce_small(per_layer):
    nk = len(per_layer[0])
    n = DEPTH * nk
    shapes = [a.shape for a in per_layer[0]]

    def body(*refs):
        ins, outs = refs[:n], refs[n:n + nk]
        sibs, slots = refs[n + nk:n + 2 * nk], refs[n + 2 * nk:n + 3 * nk]
        send, recv = refs[n + 3 * nk:]
        x, y, c, chips = _place()
        me = 2 * x + y
        d2d = [pltpu.make_async_remote_copy(src_ref=ins[l * nk + k], dst_ref=sibs[k].at[l], send_sem=send.at[l * nk + k],
                                            recv_sem=recv.at[l * nk + k], device_id=(x, y, 1 - c), device_id_type=MESH)
               for l in range(DEPTH) for k in range(nk)]
        for cp in d2d:
            cp.start()
        for cp in d2d:
            cp.wait()
        for l in range(DEPTH):
            for k in range(nk):
                slots[k][me, l] = ins[l * nk + k][...] + sibs[k][l]

        def remote(k, j, slot):
            px, py = chips[j]
            return pltpu.make_async_remote_copy(src_ref=slots[k].at[me], dst_ref=slots[k].at[slot],
                                                send_sem=send.at[n + 3 * k + j], recv_sem=recv.at[n + 3 * k + j],
                                                device_id=(px, py, c), device_id_type=MESH)

        sends = [remote(k, j, me) for k in range(nk) for j in range(3)]
        for cp in sends:
            cp.start()
        for k in range(nk):
            for j in range(3):
                remote(k, j, 2 * chips[j][0] + chips[j][1]).wait_recv()
        for cp in sends:
            cp.wait_send()
        for k in range(nk):
            outs[k][...] = ((slots[k][0] + slots[k][1]) + slots[k][2]) + slots[k][3]

    vm = pl.BlockSpec(memory_space=pltpu.VMEM)
    return pl.pallas_call(
        body, in_specs=[vm] * n, out_specs=[vm] * nk, out_shape=[_sds((DEPTH,) + s) for s in shapes],
        scratch_shapes=([pltpu.VMEM((DEPTH,) + s, F32) for s in shapes]
                        + [pltpu.VMEM((NSHARD, DEPTH) + s, F32) for s in shapes]
                        + [pltpu.SemaphoreType.DMA((n + 3 * nk,)), pltpu.SemaphoreType.DMA((n + 3 * nk,))]),
        name="allreduce_small", compiler_params=pltpu.CompilerParams(vmem_limit_bytes=VMEM_LIMIT))(
            *[a for layer in per_layer for a in layer])


def _adamw_math(w, g, m, v):
    m = ADAM_B1 * m + (1.0 - ADAM_B1) * g
    v = ADAM_B2 * v + (1.0 - ADAM_B2) * jnp.square(g)
    m_hat = m / (1.0 - ADAM_B1 ** ADAM_STEP)
    v_hat = v / (1.0 - ADAM_B2 ** ADAM_STEP)
    delta = -ADAM_LR * (m_hat / (jnp.sqrt(v_hat) + ADAM_EPS) + ADAM_WD * w)
    return delta, m, v


def _adamw(g_parts, w, m, v):
    rows, cols = w.shape
    tr = 256 if rows % 256 == 0 else _row_tile(rows)
    k = len(g_parts)

    def body(*refs):
        g = refs[0][...]
        for r in refs[1:k]:
            g = g + r[...]
        w_ref, m_ref, v_ref, go, do, mo, vo = refs[k:]
        d, mn, vn = _adamw_math(w_ref[...], g, m_ref[...], v_ref[...])
        go[...] = g
        do[...] = d
        mo[...] = mn
        vo[...] = vn

    spec = pl.BlockSpec((tr, cols), lambda i: (i, 0))
    return pl.pallas_call(
        body, grid=(rows // tr,), in_specs=[spec] * (k + 3), out_specs=[spec] * 4,
        out_shape=[_sds((rows, cols))] * 4, name="adamw", compiler_params=_cp(("parallel",)))(*g_parts, w, m, v)


def _adamw_small(gs, ws, ms, vs):
    n = len(gs)

    def body(*refs):
        for k in range(n):
            d, mn, vn = _adamw_math(refs[n + k][...], refs[k][...], refs[2 * n + k][...], refs[3 * n + k][...])
            refs[4 * n + k][...] = d
            refs[5 * n + k][...] = mn
            refs[6 * n + k][...] = vn

    vm = pl.BlockSpec(memory_space=pltpu.VMEM)
    shapes = [_sds(a.shape) for a in ws]
    res = pl.pallas_call(
        body, in_specs=[vm] * (4 * n), out_specs=[vm] * (3 * n), out_shape=shapes * 3, name="adamw_small",
        compiler_params=pltpu.CompilerParams(vmem_limit_bytes=VMEM_LIMIT))(*gs, *ws, *ms, *vs)
    return res[:n], res[n:2 * n], res[2 * n:]


_ARGS = ("x", "w_in", "s5_a_re", "s5_a_im", "s5_log_step", "s5_b_re", "s5_b_im", "s5_c_re", "s5_c_im", "s5_d",
         "s5_w_glu", "s5_b_glu", "gla_w_a", "gla_b_a", "gla_ln_g", "swa_sink", "w_out", "ln1_g", "ln1_b", "w_ff1",
         "w_ff2", "ln2_g", "ln2_b")
_WEIGHTS = _ARGS[1:]


def _in_rows(g4):
    t = g4.reshape(DIN, D)
    return jnp.concatenate([t[0:1024], t[1056:DIN], t[1024:1056], jnp.zeros((DINP - DIN, D), t.dtype)], axis=0)


def _in_rows_back(d):
    return jnp.concatenate([d[0:1024], d[1792:1824], d[1024:1792]], axis=0).reshape(NSHARD, DIN // NSHARD, D).astype(MX)


def _shard_cols(d):
    return d.reshape(d.shape[0], NSHARD, d.shape[1] // NSHARD).transpose(1, 0, 2)


def kernel(x, w_in, s5_a_re, s5_a_im, s5_log_step, s5_b_re, s5_b_im, s5_c_re, s5_c_im, s5_d, s5_w_glu, s5_b_glu, gla_w_a, gla_b_a, gla_ln_g, swa_sink, w_out, ln1_g, ln1_b, w_ff1, w_ff2, ln2_g, ln2_b, loss_target, m_w_in, m_s5_a_re, m_s5_a_im, m_s5_log_step, m_s5_b_re, m_s5_b_im, m_s5_c_re, m_s5_c_im, m_s5_d, m_s5_w_glu, m_s5_b_glu, m_gla_w_a, m_gla_b_a, m_gla_ln_g, m_swa_sink, m_w_out, m_ln1_g, m_ln1_b, m_w_ff1, m_w_ff2, m_ln2_g, m_ln2_b, v_w_in, v_s5_a_re, v_s5_a_im, v_s5_log_step, v_s5_b_re, v_s5_b_im, v_s5_c_re, v_s5_c_im, v_s5_d, v_s5_w_glu, v_s5_b_glu, v_gla_w_a, v_gla_b_a, v_gla_ln_g, v_swa_sink, v_w_out, v_ln1_g, v_ln1_b, v_w_ff1, v_w_ff2, v_ln2_g, v_ln2_b):
    given = dict(locals())
    w = {k: given[k] for k in _WEIGHTS}
    mom = {k: given["m_" + k] for k in _WEIGHTS}
    var = {k: given["v_" + k] for k in _WEIGHTS}

    me = (2 * lax.axis_index("x") + lax.axis_index("y")).astype(jnp.int32).reshape(1)
    tr = lambda t: t.transpose(0, 2, 1)
    shard = {k: (tr(w[k]) if k == "w_in" else w[k]) for k in BIG}
    qs = [_layer_prep({k: w[k][l] for k in SMALL}) for l in range(DEPTH)]

    first = ("w_in", "s5_w_glu", "w_out")
    follow = {(0, "w_in"): [(0, BIG[3:]), (1, first)], (0, "w_ff1"): [(1, BIG[3:])]}
    gathers = {}

    def start_gather(l, names, behind=None):
        lands = [_cast_to_slot(me, shard[k], l) for k in names]
        if behind is not None:
            lands, behind = lax.optimization_barrier((lands, behind))
        st = _push_start(f"gather_start_{l}_{names[0]}", lands, True)
        for k in names:
            gathers[l, k] = [names, st, None]
        return st[-1], behind

    token = start_gather(0, first[:1])[0] + start_gather(0, first[1:])[0]

    def fetch(l, name, after):
        names, st, got = gathers[l, name]
        tie = None
        if got is None:
            if l == 0 and name == "w_in":
                after = token
            lands = _push_wait(f"gather_wait_{l}_{names[0]}", st, after, True)
            for l2, names2 in follow.get((l, name), ()):
                tok, lands[0] = start_gather(l2, names2, lands[0])
                tie = tok[0, 0] if tie is None else tie + tok[0, 0]
            got = dict(zip(names, lands))
            for k in names:
                gathers[l, k][2] = got
        full = got[name]
        if name == "w_in":
            return _in_rows(full) if tie is None else _in_rows(full) + tie.astype(MX)
        if tie is not None:
            qs[l]["ln2_b"] = qs[l]["ln2_b"] + tie
        return full.reshape(D, D) if name == "w_out" else full

    scatters = []

    def emit(l, grads):
        names = tuple(grads)
        st = _push_start(f"scatter_start_{l}_{names[0]}", [grads[k] for k in names], False)
        scatters.append((l, names, st))
        return st[-1][0, 0]

    loss, dx, smalls = _local_step(x.reshape(N, D), loss_target.reshape(N, D), qs, _rope_tables(128), fetch, emit)
    loss = lax.psum(loss, ("x", "y", "c"))

    out = {}
    native = _allreduce_small([[smalls[l][k] for k in NATIVE] for l in range(DEPTH)])
    gsmall = _finish_small(dict(zip(NATIVE, native)), w)
    res = _adamw_small(*([t[k] for k in SMALL] for t in (gsmall, w, mom, var)))
    for i, k in enumerate(SMALL):
        out[k] = [gsmall[k], res[0][i], res[1][i], res[2][i]]

    recv, own = {}, {}

    def finish(keys, after):
        for l, names, st in scatters:
            if names[0] in keys:
                ops = _push_wait(f"scatter_wait_{l}_{names[0]}", st, after, False)
                for i, k in enumerate(names):
                    own[l, k], recv[l, k] = ops[i], ops[len(names) + i]
        sums = [_sum_sources(me, [recv[l, k] for l in range(DEPTH)], [own[l, k] for l in range(DEPTH)]) for k in keys]
        for k, mine, other in zip(keys, sums, _swap_sibling(sums)):
            shp = shard[k].shape
            r = _adamw([mine, other], *((tr(t[k]) if k == "w_in" else t[k]).reshape(-1, shp[-1]) for t in (w, mom, var)))
            r = [t.reshape(shp) for t in r]
            out[k] = [tr(t) for t in r] if k == "w_in" else r
        return out[keys[-1]][1]

    last = finish(("w_ff1", "w_ff2", "w_out", "s5_w_glu"), res[0][-1])
    finish(("w_in",), last)

    return (loss, dx.reshape(NSEQ, L, D), *[out[k][0] for k in _WEIGHTS], *[out[k][1] for k in _WEIGHTS],
            *[out[k][2] for k in _WEIGHTS], *[out[k][3] for k in _WEIGHTS])
```

```python
import functools
import math

import jax
import jax.numpy as jnp
from jax import lax
from jax.experimental import pallas as pl
from jax.experimental.pallas import tpu as pltpu

F32 = jnp.float32
MX = jnp.bfloat16
MESH = pl.DeviceIdType.MESH

DEPTH = 2
NSEQ = 2
L = 2048
N = NSEQ * L
D = 1024
DFF = 4096
NSHARD = 4
S5_G, S5_H, S5_P = 16, 16, 64
GLA_CHUNK = 64
NCHUNK = L // GLA_CHUNK
SWA_BLK = 128
NBLK = L // SWA_BLK
ROT = 16
ROPE_THETA = 500000.0
LN_EPS = 1e-5
ALPHA = (2 * DEPTH) ** 0.25
NEG_BIG = -1e30
DIN = 1824
DINP = 1920
ADAM_LR, ADAM_B1, ADAM_B2, ADAM_EPS, ADAM_WD, ADAM_STEP = 0.001, 0.9, 0.999, 1e-08, 0.01, 10
VMEM_LIMIT = 56 * 1024 * 1024
TT = 512
SW = 512
FFN_TM = 1024
FFN_TM_W = 1024
FFN_VMEM = 60 * 1024 * 1024


def _cp(sem, vmem=VMEM_LIMIT):
    return pltpu.CompilerParams(dimension_semantics=sem, vmem_limit_bytes=vmem)


def _mm(a, b):
    return jnp.dot(a.astype(MX), b.astype(MX), preferred_element_type=F32)


def _mm_nt(a, b):
    return lax.dot_general(a.astype(MX), b.astype(MX), (((1,), (1,)), ((), ())), preferred_element_type=F32)


def _mm_tn(a, b):
    return lax.dot_general(a.astype(MX), b.astype(MX), (((0,), (0,)), ((), ())), preferred_element_type=F32)


@jax.custom_vjp
def _dmm(a, b):
    return _mm(a, b)


_dmm.defvjp(lambda a, b: (_mm(a, b), (a, b)), lambda r, g: (_mm_nt(g, r[1]), _mm_tn(r[0], g)))


@jax.custom_vjp
def _dmm_nt(a, b):
    return _mm_nt(a, b)


_dmm_nt.defvjp(lambda a, b: (_mm_nt(a, b), (a, b)), lambda r, g: (_mm(g, r[1]), _mm_tn(g, r[0])))


@jax.custom_vjp
def _dmm_tn(a, b):
    return _mm_tn(a, b)


_dmm_tn.defvjp(lambda a, b: (_mm_tn(a, b), (a, b)), lambda r, g: (_mm_nt(r[1], g), _mm(r[0], g)))


def _split3(x):
    hi = x.astype(MX)
    r1 = x - hi.astype(F32)
    mid = r1.astype(MX)
    lo = (r1 - mid.astype(F32)).astype(MX)
    return hi, mid, lo


def _tri(rev):
    r = lax.broadcasted_iota(jnp.int32, (GLA_CHUNK, GLA_CHUNK), 0)
    c = lax.broadcasted_iota(jnp.int32, (GLA_CHUNK, GLA_CHUNK), 1)
    return jnp.where((c >= r) if rev else (c <= r), 1.0, 0.0).astype(MX)


def _cums_impl(x, rev):
    t = _tri(rev)
    return sum(jnp.dot(t, p, preferred_element_type=F32) for p in _split3(x))


@functools.partial(jax.custom_vjp, nondiff_argnums=(1,))
def _cums(x, rev):
    return _cums_impl(x, rev)


_cums.defvjp(lambda x, rev: (_cums_impl(x, rev), None), lambda rev, r, g: (_cums_impl(g, not rev),))


def _ln_fwd(s, g, b):
    mu = jnp.mean(s, axis=-1, keepdims=True)
    xc = s - mu
    var = jnp.mean(xc * xc, axis=-1, keepdims=True)
    return xc * lax.rsqrt(var + LN_EPS) * g + b


def _ln_bwd(dy, s, g):
    mu = jnp.mean(s, axis=-1, keepdims=True)
    xc = s - mu
    var = jnp.mean(xc * xc, axis=-1, keepdims=True)
    rstd = lax.rsqrt(var + LN_EPS)
    xhat = xc * rstd
    dxh = dy * g
    ds = rstd * (dxh - jnp.mean(dxh, axis=-1, keepdims=True) - xhat * jnp.mean(dxh * xhat, axis=-1, keepdims=True))
    return ds, jnp.sum(dy * xhat, axis=0, keepdims=True), jnp.sum(dy, axis=0, keepdims=True)


def _sds(shape, dtype=F32):
    return jax.ShapeDtypeStruct(shape, dtype)


def _inproj_fwd(x, wt):
    tm = 512

    def body(x_ref, w_ref, h_ref):
        h_ref[...] = _mm_nt(x_ref[...], w_ref[...])

    return pl.pallas_call(
        body, grid=(N // tm,),
        in_specs=[pl.BlockSpec((tm, D), lambda i: (i, 0)), pl.BlockSpec((DINP, D), lambda i: (0, 0))],
        out_specs=pl.BlockSpec((tm, DINP), lambda i: (i, 0)),
        out_shape=_sds((N, DINP)), name="inproj_fwd", compiler_params=_cp(("parallel",)))(x, wt)


def _inproj_bwd(x, w, dxp, du2, dud, gq_f, gq_b, gk_f, gk_b, gv_f, gv_b, gr, daq, dakv, dhl):
    tm = 256
    nt = N // tm

    def body(x_ref, w_ref, dxp_ref, du2_ref, dud_ref, gqf, gqb, gkf, gkb, gvf, gvb, gr_ref, daq_ref, dakv_ref, dhl_ref,
             dx_ref, dw_ref):
        i = pl.program_id(0)
        dh = jnp.concatenate([
            du2_ref[0] + du2_ref[1] + dud_ref[...], gqf[...] + gqb[...], gkf[...] + gkb[...], gvf[...] + gvb[...],
            gr_ref[...], daq_ref[...], dakv_ref[...], dhl_ref[...]], axis=1)
        dx_ref[...] = dxp_ref[...] + _mm(dh, w_ref[...])
        contrib = _mm_tn(dh, x_ref[...])

        @pl.when(i == 0)
        def _():
            dw_ref[...] = contrib

        @pl.when(i > 0)
        def _():
            dw_ref[...] += contrib

    row = lambda w_: pl.BlockSpec((tm, w_), lambda i: (i, 0))
    return pl.pallas_call(
        body, grid=(nt,),
        in_specs=[row(D), pl.BlockSpec((DINP, D), lambda i: (0, 0)), row(D),
                  pl.BlockSpec((2, tm, 256), lambda i: (0, i, 0)), row(256), row(128), row(128), row(128), row(128),
                  row(256), row(256), row(256), row(512), row(256), row(128)],
        out_specs=[row(D), pl.BlockSpec((DINP, D), lambda i: (0, 0))],
        out_shape=[_sds((N, D)), _sds((DINP, D))],
        name="inproj_bwd", compiler_params=_cp(("arbitrary",)))(
            x, w, dxp, du2, dud, gq_f, gq_b, gk_f, gk_b, gv_f, gv_b, gr, daq, dakv, dhl)


def _scan_tables(mr, mi, reverse):
    pw = [(mr, mi)]
    for _ in range(7):
        pr, pi = pw[-1]
        pw.append((pr * mr - pi * mi, pr * mi + pi * mr))
    rows = jnp.arange(8)[:, None]
    out = []
    for d in (1, 2, 4):
        keep = rows >= d
        out += [jnp.where(keep, pw[d - 1][0][None], 0.0), jnp.where(keep, pw[d - 1][1][None], 0.0)]
    out += [jnp.stack([p[0] for p in pw]), jnp.stack([p[1] for p in pw])]
    t = jnp.stack(out)
    if reverse:
        t = t[:, ::-1, :]
    return t.reshape(8, 8, 2, SW).transpose(2, 0, 1, 3)


def _tile_scan(xr, xi, a, cr, ci, reverse):
    for lvl, d in enumerate((1, 2, 4)):
        sh = 8 - d if reverse else d
        sr = pltpu.roll(xr, sh, 0)
        si = pltpu.roll(xi, sh, 0)
        ar, ai = a[2 * lvl], a[2 * lvl + 1]
        xr, xi = xr + ar * sr - ai * si, xi + ar * si + ai * sr
    pr, pi = a[6], a[7]
    return xr + pr * cr - pi * ci, xi + pr * ci + pi * cr


def _s5_time_block(z, s, t, adjoint):
    flip = (1 - z) if adjoint else z
    return s * (L // TT) + t + flip * (L // TT - 1 - 2 * t)


def _s5_fwd(h, bre, bim, cre, cim, tab):
    nt = L // TT

    def body(u_ref, bre_ref, bim_ref, cre_ref, cim_ref, tab_ref, hre_ref, him_ref, y_ref, car):
        z = pl.program_id(1)
        tc = pl.program_id(3)

        @pl.when(tc == 0)
        def _():
            car[...] = jnp.zeros_like(car)

        u = u_ref[...]
        hre_ref[0] = _mm(u, bre_ref[0, 0])
        him_ref[0] = _mm(u, bim_ref[0, 0])

        def run(reverse):
            a = [tab_ref[0, 0, k] for k in range(8)]

            def step(i, carry):
                cr, ci = carry
                r0 = pl.multiple_of((TT // 8 - 1 - i if reverse else i) * 8, 8)
                xr, xi = _tile_scan(hre_ref[0, pl.ds(r0, 8), :], him_ref[0, pl.ds(r0, 8), :], a, cr, ci, reverse)
                hre_ref[0, pl.ds(r0, 8), :] = xr
                him_ref[0, pl.ds(r0, 8), :] = xi
                row = 0 if reverse else 7
                return (jnp.broadcast_to(xr[row:row + 1, :], (8, SW)), jnp.broadcast_to(xi[row:row + 1, :], (8, SW)))

            cr, ci = lax.fori_loop(0, TT // 8, step, (car[0], car[1]), unroll=4)
            car[0] = cr
            car[1] = ci

        @pl.when(z == 0)
        def _():
            run(False)

        @pl.when(z == 1)
        def _():
            run(True)

        y_ref[0] = _mm(hre_ref[0], cre_ref[0, 0]) - _mm(him_ref[0], cim_ref[0, 0])

    tb = lambda b, z, s, t: _s5_time_block(z, s, t, False)
    wspec = lambda r, c: pl.BlockSpec((1, 1, r, c), lambda b, z, s, t: (z, b, 0, 0))
    return pl.pallas_call(
        body, grid=(2, 2, NSEQ, nt),
        in_specs=[pl.BlockSpec((TT, 128), lambda b, z, s, t: (tb(b, z, s, t), b)),
                  wspec(128, SW), wspec(128, SW), wspec(SW, 128), wspec(SW, 128),
                  pl.BlockSpec((1, 1, 8, 8, SW), lambda b, z, s, t: (z, b, 0, 0, 0))],
        out_specs=[pl.BlockSpec((1, TT, SW), lambda b, z, s, t: (z, tb(b, z, s, t), b)),
                   pl.BlockSpec((1, TT, SW), lambda b, z, s, t: (z, tb(b, z, s, t), b)),
                   pl.BlockSpec((1, TT, 128), lambda b, z, s, t: (z, tb(b, z, s, t), b))],
        out_shape=[_sds((2, N, 2 * SW)), _sds((2, N, 2 * SW)), _sds((2, N, 256))],
        scratch_shapes=[pltpu.VMEM((2, 8, SW), F32)],
        name="s5_fwd", compiler_params=_cp(("arbitrary",) * 4))(h, bre, bim, cre, cim, tab)


def _s5_bwd(h, dyp, hre, him, bre, bim, cre, cim, tabc):
    nt = L // TT

    def body(u_ref, dy_ref, hre_ref, him_ref, bre_ref, bim_ref, cre_ref, cim_ref, tab_ref,
             du_ref, dbre_ref, dbim_ref, dcre_ref, dcim_ref, dmu_ref, gre, gim, car, acc, macc):
        z = pl.program_id(1)
        s = pl.program_id(2)
        tc = pl.program_id(3)

        @pl.when(tc == 0)
        def _():
            car[...] = jnp.zeros_like(car)

        @pl.when((tc == 0) & (s == 0))
        def _():
            acc[...] = jnp.zeros_like(acc)
            macc[...] = jnp.zeros_like(macc)

        dy = dy_ref[...]
        gre[...] = _mm_nt(dy, cre_ref[0, 0])
        gim[...] = -_mm_nt(dy, cim_ref[0, 0])
        rowid = lax.broadcasted_iota(jnp.int32, (8, SW), 0)

        def run(reverse):
            a = [tab_ref[0, 0, k] for k in range(8)]
            first = 7 if reverse else 0

            def step(i, carry):
                cr, ci, dmr, dmi = carry
                r0 = pl.multiple_of((TT // 8 - 1 - i if reverse else i) * 8, 8)
                xr, xi = _tile_scan(gre[pl.ds(r0, 8), :], gim[pl.ds(r0, 8), :], a, cr, ci, reverse)
                gre[pl.ds(r0, 8), :] = xr
                gim[pl.ds(r0, 8), :] = xi
                sh = 7 if reverse else 1
                gpr = jnp.where(rowid == first, cr, pltpu.roll(xr, sh, 0))
                gpi = jnp.where(rowid == first, ci, pltpu.roll(xi, sh, 0))
                hr = hre_ref[0, pl.ds(r0, 8), :]
                hi = him_ref[0, pl.ds(r0, 8), :]
                dmr = dmr + gpr * hr + gpi * hi
                dmi = dmi + gpi * hr - gpr * hi
                row = 0 if reverse else 7
                return (jnp.broadcast_to(xr[row:row + 1, :], (8, SW)), jnp.broadcast_to(xi[row:row + 1, :], (8, SW)),
                        dmr, dmi)

            cr, ci, dmr, dmi = lax.fori_loop(0, TT // 8, step, (car[0], car[1], macc[0], macc[1]), unroll=4)
            car[0] = cr
            car[1] = ci
            macc[0] = dmr
            macc[1] = dmi

        @pl.when(z == 0)
        def _():
            run(True)

        @pl.when(z == 1)
        def _():
            run(False)

        gr = gre[...]
        gi = gim[...]
        u = u_ref[...]
        du_ref[0] = _mm_nt(gr, bre_ref[0, 0]) + _mm_nt(gi, bim_ref[0, 0])
        acc[0] += _mm_tn(u, gr)
        acc[1] += _mm_tn(u, gi)
        acc[2] += _mm_tn(dy, hre_ref[0])
        acc[3] -= _mm_tn(dy, him_ref[0])

        @pl.when((tc == nt - 1) & (s == NSEQ - 1))
        def _():
            grp = lax.broadcasted_iota(jnp.int32, (S5_H, SW), 1) // S5_P
            for k, out in enumerate((dbre_ref, dbim_ref, dcre_ref, dcim_ref)):
                c = jnp.zeros((S5_H, SW), F32)
                for i in range(8):
                    c = c + jnp.where(grp == i, acc[k, i * S5_H:(i + 1) * S5_H, :], 0.0)
                out[0, 0] = c
            dmu_ref[0, 0] = jnp.concatenate([jnp.sum(macc[0], axis=0, keepdims=True),
                                             jnp.sum(macc[1], axis=0, keepdims=True)], axis=0)

    tb = lambda b, z, s, t: _s5_time_block(z, s, t, True)
    wspec = lambda r, c: pl.BlockSpec((1, 1, r, c), lambda b, z, s, t: (z, b, 0, 0))
    tok = lambda w_: pl.BlockSpec((TT, w_), lambda b, z, s, t: (tb(b, z, s, t), b))
    st = pl.BlockSpec((1, TT, SW), lambda b, z, s, t: (z, tb(b, z, s, t), b))
    return pl.pallas_call(
        body, grid=(2, 2, NSEQ, nt),
        in_specs=[tok(128), tok(128), st, st, wspec(128, SW), wspec(128, SW), wspec(SW, 128), wspec(SW, 128),
                  pl.BlockSpec((1, 1, 8, 8, SW), lambda b, z, s, t: (z, b, 0, 0, 0))],
        out_specs=[pl.BlockSpec((1, TT, 128), lambda b, z, s, t: (z, tb(b, z, s, t), b)),
                   wspec(S5_H, SW), wspec(S5_H, SW), wspec(S5_H, SW), wspec(S5_H, SW),
                   wspec(2, SW)],
        out_shape=[_sds((2, N, 256))] + [_sds((2, 2, S5_H, SW))] * 4 + [_sds((2, 2, 2, SW))],
        scratch_shapes=[pltpu.VMEM((TT, SW), F32), pltpu.VMEM((TT, SW), F32), pltpu.VMEM((2, 8, SW), F32),
                        pltpu.VMEM((4, 128, SW), F32), pltpu.VMEM((2, 8, SW), F32)],
        name="s5_bwd", compiler_params=_cp(("arbitrary",) * 4))(h, dyp, hre, him, bre, bim, cre, cim, tabc)


_GELU_C = math.sqrt(2.0 / math.pi)


def _gelu(y):
    return 0.5 * y * (1.0 + jnp.tanh(_GELU_C * (y + 0.044715 * y * y * y)))


def _gelu_grad(y):
    t = jnp.tanh(_GELU_C * (y + 0.044715 * y * y * y))
    return 0.5 * (1.0 + t) + 0.5 * y * (1.0 - t * t) * _GELU_C * (1.0 + 3 * 0.044715 * y * y)


def _glu_halves(w4_ref):
    return (jnp.concatenate([w4_ref[0], w4_ref[1]], axis=1), jnp.concatenate([w4_ref[2], w4_ref[3]], axis=1))


def _s5_glu_fwd(y2, h, dsk, w4, bv, bg):
    tm = 512

    def body(y2_ref, u_ref, d_ref, w4_ref, bv_ref, bg_ref, ya_ref):
        wv, wg = _glu_halves(w4_ref)
        z = _gelu(y2_ref[0] + y2_ref[1] + d_ref[...] * u_ref[...])
        val = _mm(z, wv) + bv_ref[...]
        gate = _mm(z, wg) + bg_ref[...]
        ya_ref[...] = val * jax.nn.sigmoid(gate)

    full = lambda r, c: pl.BlockSpec((r, c), lambda i: (0, 0))
    return pl.pallas_call(
        body, grid=(N // tm,),
        in_specs=[pl.BlockSpec((2, tm, 256), lambda i: (0, i, 0)), pl.BlockSpec((tm, 256), lambda i: (i, 0)),
                  full(1, 256), pl.BlockSpec((NSHARD, 256, 128), lambda i: (0, 0, 0)), full(1, 256), full(1, 256)],
        out_specs=pl.BlockSpec((tm, 256), lambda i: (i, 0)),
        out_shape=_sds((N, 256)), name="s5_glu_fwd", compiler_params=_cp(("parallel",)))(y2, h, dsk, w4, bv, bg)


def _s5_glu_bwd(y2, h, dsk, w4, bv, bg, dya):
    tm = 512
    nt = N // tm

    def body(y2_ref, u_ref, d_ref, w4_ref, bv_ref, bg_ref, dya_ref,
             dyp_ref, dud_ref, dd_ref, dw4_ref, dbv_ref, dbg_ref, accv, accg):
        i = pl.program_id(0)

        @pl.when(i == 0)
        def _():
            for r in (dd_ref, accv, accg, dbv_ref, dbg_ref):
                r[...] = jnp.zeros_like(r)

        wv, wg = _glu_halves(w4_ref)
        u = u_ref[...]
        y = y2_ref[0] + y2_ref[1] + d_ref[...] * u
        z = _gelu(y)
        val = _mm(z, wv) + bv_ref[...]
        sig = jax.nn.sigmoid(_mm(z, wg) + bg_ref[...])
        dya = dya_ref[...]
        dval = dya * sig
        dgate = dya * val * sig * (1.0 - sig)
        dz = _mm_nt(dval, wv) + _mm_nt(dgate, wg)
        dy = dz * _gelu_grad(y)
        dyp_ref[...] = dy
        dud_ref[...] = dy * d_ref[...]
        dd_ref[...] += jnp.sum(dy * u, axis=0, keepdims=True)
        accv[...] += _mm_tn(z, dval)
        accg[...] += _mm_tn(z, dgate)
        dbv_ref[...] += jnp.sum(dval, axis=0, keepdims=True)
        dbg_ref[...] += jnp.sum(dgate, axis=0, keepdims=True)

        @pl.when(i == nt - 1)
        def _():
            dw4_ref[0] = accv[:, 0:128].astype(MX)
            dw4_ref[1] = accv[:, 128:256].astype(MX)
            dw4_ref[2] = accg[:, 0:128].astype(MX)
            dw4_ref[3] = accg[:, 128:256].astype(MX)

    full = lambda r, c: pl.BlockSpec((r, c), lambda i: (0, 0))
    row = pl.BlockSpec((tm, 256), lambda i: (i, 0))
    wspec = pl.BlockSpec((NSHARD, 256, 128), lambda i: (0, 0, 0))
    return pl.pallas_call(
        body, grid=(nt,),
        in_specs=[pl.BlockSpec((2, tm, 256), lambda i: (0, i, 0)), row, full(1, 256), wspec, full(1, 256), full(1, 256),
                  row],
        out_specs=[row, row, full(1, 256), wspec, full(1, 256), full(1, 256)],
        out_shape=[_sds((N, 256)), _sds((N, 256)), _sds((1, 256)), _sds((NSHARD, 256, 128), MX), _sds((1, 256)),
                   _sds((1, 256))],
        scratch_shapes=[pltpu.VMEM((256, 256), F32), pltpu.VMEM((256, 256), F32)],
        name="s5_glu_bwd", compiler_params=_cp(("arbitrary",)))(y2, h, dsk, w4, bv, bg, dya)


def _logsig(x):
    return jnp.minimum(x, 0.0) - jnp.log(1.0 + jnp.exp(-jnp.abs(x)))


def _gla_gate_fwd(h, wa, ba):
    tm = 512

    def body(hl_ref, wa_ref, ba_ref, la_ref):
        la_ref[...] = _logsig(_mm(hl_ref[...], wa_ref[...]) + ba_ref[...]) * (1.0 / 16.0)

    return pl.pallas_call(
        body, grid=(N // tm,),
        in_specs=[pl.BlockSpec((tm, 128), lambda i: (i, 14)), pl.BlockSpec((128, 256), lambda i: (0, 0)),
                  pl.BlockSpec((1, 256), lambda i: (0, 0))],
        out_specs=pl.BlockSpec((tm, 256), lambda i: (i, 0)),
        out_shape=_sds((N, 256)), name="gla_gate_fwd", compiler_params=_cp(("parallel",)))(h, wa, ba)


def _gla_gate_bwd(h, wa, ba, dla_f, dla_b):
    tm = 512

    def body(hl_ref, wa_ref, ba_ref, df_ref, db_ref, dhl_ref, dwa_ref, dba_ref):
        i = pl.program_id(0)

        @pl.when(i == 0)
        def _():
            dwa_ref[...] = jnp.zeros_like(dwa_ref)
            dba_ref[...] = jnp.zeros_like(dba_ref)

        hl = hl_ref[...]
        pre = _mm(hl, wa_ref[...]) + ba_ref[...]
        dpre = jnp.concatenate([df_ref[...], db_ref[...]], axis=1) * (1.0 / 16.0) * jax.nn.sigmoid(-pre)
        dhl_ref[...] = _mm_nt(dpre, wa_ref[...])
        dwa_ref[...] += _mm_tn(hl, dpre)[0:32]
        dba_ref[...] += jnp.sum(dpre, axis=0, keepdims=True)

    row = pl.BlockSpec((tm, 128), lambda i: (i, 0))
    return pl.pallas_call(
        body, grid=(N // tm,),
        in_specs=[pl.BlockSpec((tm, 128), lambda i: (i, 14)), pl.BlockSpec((128, 256), lambda i: (0, 0)),
                  pl.BlockSpec((1, 256), lambda i: (0, 0)), row, row],
        out_specs=[row, pl.BlockSpec((32, 256), lambda i: (0, 0)), pl.BlockSpec((1, 256), lambda i: (0, 0))],
        out_shape=[_sds((N, 128)), _sds((32, 256)), _sds((1, 256))],
        name="gla_gate_bwd", compiler_params=_cp(("arbitrary",)))(h, wa, ba, dla_f, dla_b)


def _gla_chunk(q, k, v, la, st, rev):
    c = GLA_CHUNK
    b = _cums(la, rev)
    bl = jnp.sum(la, axis=0, keepdims=True)
    q_in = q * (32.0 ** -0.5) * jnp.exp(b)
    k_in = k * jnp.exp(-b)
    k_st = k * jnp.exp(bl - b)
    lane_k = lax.broadcasted_iota(jnp.int32, (1, 128), 1) // 32
    lane_v = lax.broadcasted_iota(jnp.int32, (1, 256), 1) // 64
    r = lax.broadcasted_iota(jnp.int32, (c, c), 0)
    cc = lax.broadcasted_iota(jnp.int32, (c, c), 1)
    keep = (cc > r) if rev else (cc <= r)
    qs = jnp.concatenate([jnp.where(lane_k == hd, q_in, 0.0) for hd in range(4)], axis=0)
    a = _dmm_nt(qs, k_in)
    a = jnp.where(jnp.concatenate([keep] * 4, axis=0), a, 0.0)
    o4 = _dmm(a, v)
    o = _dmm_nt(q_in, st)
    for hd in range(4):
        o = o + jnp.where(lane_v == hd, o4[hd * c:(hd + 1) * c], 0.0)
    bd = (lax.broadcasted_iota(jnp.int32, (256, 128), 0) // 64) == (lax.broadcasted_iota(jnp.int32, (256, 128), 1) // 32)
    st_new = jnp.exp(bl) * st + jnp.where(bd, _dmm_tn(v, k_st), 0.0)
    return o, st_new


def _gla_chunk_of(c, rev):
    return NCHUNK - 1 - c if rev else c


def _gla_fwd(h, la2):
    c = GLA_CHUNK

    def body(qf, kf, vf, laf, qb, kb, vb, lab, of_ref, ob_ref, sf_ref, sb_ref, stf, stb):
        @pl.when(pl.program_id(0) == 0)
        def _():
            stf[...] = jnp.zeros_like(stf)
            stb[...] = jnp.zeros_like(stb)

        ins = [(qf[s], kf[s], vf[s], laf[s], stf[s], qb[s], kb[s], vb[s], lab[s], stb[s]) for s in range(NSEQ)]
        outs = [(_gla_chunk(*t[:5], False), _gla_chunk(*t[5:], True)) for t in ins]
        for s in range(NSEQ):
            sf_ref[s, 0] = ins[s][4]
            sb_ref[s, 0] = ins[s][9]
            (of_ref[s], stf[s]), (ob_ref[s], stb[s]) = outs[s]

    def specs(rev):
        ch = lambda i: _gla_chunk_of(i, rev)
        return [pl.BlockSpec((NSEQ, c, 128), lambda i: (0, ch(i), 2)), pl.BlockSpec((NSEQ, c, 128), lambda i: (0, ch(i), 3)),
                pl.BlockSpec((NSEQ, c, 256), lambda i: (0, ch(i), 2)),
                pl.BlockSpec((NSEQ, c, 128), lambda i: (0, ch(i), 1 if rev else 0))]

    orow = lambda rev: pl.BlockSpec((NSEQ, c, 256), lambda i: (0, _gla_chunk_of(i, rev), 0))
    srow = lambda rev: pl.BlockSpec((NSEQ, 1, 256, 128), lambda i: (0, _gla_chunk_of(i, rev), 0, 0))
    h3, la3 = h.reshape(NSEQ, L, DINP), la2.reshape(NSEQ, L, 256)
    of, ob, sf, sb = pl.pallas_call(
        body, grid=(NCHUNK,),
        in_specs=specs(False) + specs(True),
        out_specs=[orow(False), orow(True), srow(False), srow(True)],
        out_shape=[_sds((NSEQ, L, 256)), _sds((NSEQ, L, 256)), _sds((NSEQ, NCHUNK, 256, 128)),
                   _sds((NSEQ, NCHUNK, 256, 128))],
        scratch_shapes=[pltpu.VMEM((NSEQ, 256, 128), F32), pltpu.VMEM((NSEQ, 256, 128), F32)],
        name="gla_fwd", compiler_params=_cp(("arbitrary",)))(h3, h3, h3, la3, h3, h3, h3, la3)
    return of.reshape(N, 256), ob.reshape(N, 256), sf, sb


def _gla_bwd(h, la2, do, sf, sb):
    c = GLA_CHUNK

    def body(qf, kf, vf, laf, dof, sfr, qb, kb, vb, lab, dob, sbr,
             dqf, dkf, dvf, dlf, dqb, dkb, dvb, dlb, dstf, dstb):
        @pl.when(pl.program_id(0) == 0)
        def _():
            dstf[...] = jnp.zeros_like(dstf)
            dstb[...] = jnp.zeros_like(dstb)

        def one(s, q, k, v, la, do_, st, dst, rev):
            _, vjp = jax.vjp(functools.partial(_gla_chunk, rev=rev), q[s], k[s], v[s], la[s], st[s, 0])
            return vjp((do_[s], dst[s]))

        res = [(one(s, qf, kf, vf, laf, dof, sfr, dstf, False), one(s, qb, kb, vb, lab, dob, sbr, dstb, True))
               for s in range(NSEQ)]
        for s in range(NSEQ):
            dqf[s], dkf[s], dvf[s], dlf[s], dstf[s] = res[s][0]
            dqb[s], dkb[s], dvb[s], dlb[s], dstb[s] = res[s][1]

    def specs(rev):
        ch = lambda i: _gla_chunk_of(i, not rev)
        return [pl.BlockSpec((NSEQ, c, 128), lambda i: (0, ch(i), 2)), pl.BlockSpec((NSEQ, c, 128), lambda i: (0, ch(i), 3)),
                pl.BlockSpec((NSEQ, c, 256), lambda i: (0, ch(i), 2)),
                pl.BlockSpec((NSEQ, c, 128), lambda i: (0, ch(i), 1 if rev else 0)),
                pl.BlockSpec((NSEQ, c, 256), lambda i: (0, ch(i), 0)),
                pl.BlockSpec((NSEQ, 1, 256, 128), lambda i: (0, ch(i), 0, 0))]

    def ospecs(rev):
        ch = lambda i: _gla_chunk_of(i, not rev)
        n = pl.BlockSpec((NSEQ, c, 128), lambda i: (0, ch(i), 0))
        return [n, n, pl.BlockSpec((NSEQ, c, 256), lambda i: (0, ch(i), 0)), n]

    oshape = [_sds((NSEQ, L, 128)), _sds((NSEQ, L, 128)), _sds((NSEQ, L, 256)), _sds((NSEQ, L, 128))]
    h3, la3, do3 = h.reshape(NSEQ, L, DINP), la2.reshape(NSEQ, L, 256), do.reshape(NSEQ, L, 256)
    res = pl.pallas_call(
        body, grid=(NCHUNK,),
        in_specs=specs(False) + specs(True),
        out_specs=ospecs(False) + ospecs(True),
        out_shape=oshape + oshape,
        scratch_shapes=[pltpu.VMEM((NSEQ, 256, 128), F32), pltpu.VMEM((NSEQ, 256, 128), F32)],
        name="gla_bwd", compiler_params=_cp(("arbitrary",)))(h3, h3, h3, la3, do3, sf, h3, h3, h3, la3, do3, sb)
    return [r.reshape(N, r.shape[-1]) for r in res]


def _gla_post(of, ob, r, g):
    o = of + ob
    head = lax.broadcasted_iota(jnp.int32, (1, 256), 1) // 64
    mu = jnp.zeros_like(o)
    for hd in range(4):
        mu = mu + jnp.where(head == hd, jnp.sum(jnp.where(head == hd, o, 0.0), axis=-1, keepdims=True) * (1.0 / 64.0), 0.0)
    xc = o - mu
    var = jnp.zeros_like(o)
    for hd in range(4):
        var = var + jnp.where(head == hd, jnp.sum(jnp.where(head == hd, xc * xc, 0.0), axis=-1, keepdims=True) * (1.0 / 64.0), 0.0)
    return xc * lax.rsqrt(var + LN_EPS) * g * (r * jax.nn.sigmoid(r))


def _gla_post_fwd(of, ob, h, g):
    tm = 512

    def body(of_ref, ob_ref, r_ref, g_ref, y_ref):
        y_ref[...] = _gla_post(of_ref[...], ob_ref[...], r_ref[...], g_ref[...])

    row = pl.BlockSpec((tm, 256), lambda i: (i, 0))
    return pl.pallas_call(
        body, grid=(N // tm,),
        in_specs=[row, row, pl.BlockSpec((tm, 256), lambda i: (i, 3)), pl.BlockSpec((1, 256), lambda i: (0, 0))],
        out_specs=row, out_shape=_sds((N, 256)), name="gla_post_fwd", compiler_params=_cp(("parallel",)))(of, ob, h, g)


def _gla_post_bwd(of, ob, h, g, dyb):
    tm = 512

    def body(of_ref, ob_ref, r_ref, g_ref, dy_ref, do_ref, dr_ref, dg_ref):
        @pl.when(pl.program_id(0) == 0)
        def _():
            dg_ref[...] = jnp.zeros_like(dg_ref)

        _, vjp = jax.vjp(_gla_post, of_ref[...], ob_ref[...], r_ref[...], g_ref[...])
        go, _, gr, gg = vjp(dy_ref[...])
        do_ref[...] = go
        dr_ref[...] = gr
        dg_ref[...] += gg

    row = pl.BlockSpec((tm, 256), lambda i: (i, 0))
    one = pl.BlockSpec((1, 256), lambda i: (0, 0))
    return pl.pallas_call(
        body, grid=(N // tm,),
        in_specs=[row, row, pl.BlockSpec((tm, 256), lambda i: (i, 3)), one, row],
        out_specs=[row, row, one], out_shape=[_sds((N, 256)), _sds((N, 256)), _sds((1, 256))],
        name="gla_post_bwd", compiler_params=_cp(("arbitrary",)))(of, ob, h, g, dyb)


def _rope_tables(width):
    pos = jnp.arange(L, dtype=F32)
    inv_freq = ROPE_THETA ** (-jnp.arange(0, ROT, 2, dtype=F32) / ROT)
    ang = pos[:, None] * inv_freq[None, :]
    cos, sin = jnp.cos(ang), jnp.sin(ang)
    one = jnp.ones((L, 64 - ROT), F32)
    zero = jnp.zeros((L, 64 - ROT), F32)
    z8 = jnp.zeros((L, ROT // 2), F32)
    c = jnp.concatenate([cos, cos, one], axis=1)
    sa = jnp.concatenate([z8, sin, zero], axis=1)
    sb = jnp.concatenate([-sin, z8, zero], axis=1)
    rep = width // 64
    return jnp.stack([jnp.tile(c, (1, rep)), jnp.tile(sa, (1, rep)), jnp.tile(sb, (1, rep))])


def _rope(t, tab):
    w = t.shape[-1]
    return t * tab[0] + pltpu.roll(t, ROT // 2, 1) * tab[1] + pltpu.roll(t, w - ROT // 2, 1) * tab[2]


def _rope_t(g, tab):
    w = g.shape[-1]
    return g * tab[0] + pltpu.roll(g * tab[1], w - ROT // 2, 1) + pltpu.roll(g * tab[2], ROT // 2, 1)


def _swa_pad_kv(kv_ref, tk_ref, kpad, vpad):
    z = jnp.zeros((SWA_BLK, 128), F32)
    kpad[0:SWA_BLK] = z
    vpad[0:SWA_BLK] = z
    kpad[SWA_BLK + L:] = z
    vpad[SWA_BLK + L:] = z
    kpad[SWA_BLK:SWA_BLK + L] = _rope(kv_ref[:, 0:128], tk_ref[...])
    vpad[SWA_BLK:SWA_BLK + L] = kv_ref[:, 128:256]


def _swa_expand(x, hk):
    lane = lax.broadcasted_iota(jnp.int32, x.shape, 1)
    sw = pltpu.roll(x, 64, 1)
    pair = jnp.where(lane < 64, x, sw) if hk == 0 else jnp.where(lane < 64, sw, x)
    return jnp.concatenate([pair, pair], axis=1)


def _swa_fold(x, hk):
    a = x[:, 0:128] + x[:, 128:256]
    t = a + pltpu.roll(a, 64, 1)
    lane = lax.broadcasted_iota(jnp.int32, a.shape, 1)
    return jnp.where((lane < 64) if hk == 0 else (lane >= 64), t, 0.0)


def _swa_probs(q2, kexp, n, sink_ref, hk):
    slot = lax.broadcasted_iota(jnp.int32, (1, 256), 1) // 64
    qs = jnp.concatenate([jnp.where(slot == g, q2, 0.0) for g in range(4)], axis=0)
    s = _mm_nt(qs, kexp) * 0.125
    i = lax.broadcasted_iota(jnp.int32, (SWA_BLK, 3 * SWA_BLK), 0)
    j = lax.broadcasted_iota(jnp.int32, (SWA_BLK, 3 * SWA_BLK), 1)
    kpos = n * SWA_BLK - SWA_BLK + j
    ok = (j - i >= 0) & (j - i <= 2 * SWA_BLK) & (kpos >= 0) & (kpos < L)
    s = jnp.where(jnp.concatenate([ok] * 4, axis=0), s, NEG_BIG)
    rowg = lax.broadcasted_iota(jnp.int32, (4 * SWA_BLK, 1), 0) // SWA_BLK
    sink = jnp.zeros((4 * SWA_BLK, 1), F32)
    for g in range(4):
        sink = jnp.where(rowg == g, sink_ref[hk * 4 + g], sink)
    m = jnp.maximum(jnp.max(s, axis=-1, keepdims=True), sink)
    p = jnp.exp(s - m)
    ps = jnp.exp(sink - m)
    inv = 1.0 / (jnp.sum(p, axis=-1, keepdims=True) + ps)
    return qs, p * inv, ps * inv, slot, rowg


def _swa_qtab(tk_ref, r0):
    return [jnp.concatenate([tk_ref[i, pl.ds(r0, SWA_BLK), :]] * 4, axis=1) for i in range(3)]


def _swa_fwd(h, tk, sink):
    def body(sink_ref, q_ref, kv_ref, tk_ref, y_ref, kpad, vpad):
        n = pl.program_id(1)

        @pl.when(n == 0)
        def _():
            _swa_pad_kv(kv_ref, tk_ref, kpad, vpad)

        r0 = pl.multiple_of(n * SWA_BLK, SWA_BLK)
        q = _rope(q_ref[...], _swa_qtab(tk_ref, r0))
        kb = kpad[pl.ds(r0, 3 * SWA_BLK), :]
        vb = vpad[pl.ds(r0, 3 * SWA_BLK), :]
        for hk in range(2):
            _, p, _, slot, _ = _swa_probs(q[:, hk * 256:(hk + 1) * 256], _swa_expand(kb, hk), n, sink_ref, hk)
            o4 = _mm(p, _swa_expand(vb, hk))
            o = jnp.zeros((SWA_BLK, 256), F32)
            for g in range(4):
                o = o + jnp.where(slot == g, o4[g * SWA_BLK:(g + 1) * SWA_BLK], 0.0)
            y_ref[:, hk * 256:(hk + 1) * 256] = o

    return pl.pallas_call(
        body,
        grid_spec=pltpu.PrefetchScalarGridSpec(
            num_scalar_prefetch=1, grid=(NSEQ, NBLK),
            in_specs=[pl.BlockSpec((SWA_BLK, 512), lambda s, n, sk: (s * NBLK + n, 2)),
                      pl.BlockSpec((L, 256), lambda s, n, sk: (s, 6)),
                      pl.BlockSpec((3, L, 128), lambda s, n, sk: (0, 0, 0))],
            out_specs=pl.BlockSpec((SWA_BLK, 512), lambda s, n, sk: (s * NBLK + n, 0)),
            scratch_shapes=[pltpu.VMEM((L + 2 * SWA_BLK, 128), F32), pltpu.VMEM((L + 2 * SWA_BLK, 128), F32)]),
        out_shape=_sds((N, 512)), name="swa_fwd", compiler_params=_cp(("arbitrary", "arbitrary")))(sink, h, h, tk)


def _swa_bwd(h, tk, sink, dyc):
    def body(sink_ref, q_ref, kv_ref, tk_ref, dy_ref, dq_ref, dkv_ref, dsink_ref, kpad, vpad, dkacc, dvacc):
        sq = pl.program_id(0)
        n = pl.program_id(1)

        @pl.when(n == 0)
        def _():
            _swa_pad_kv(kv_ref, tk_ref, kpad, vpad)
            dkacc[...] = jnp.zeros_like(dkacc)
            dvacc[...] = jnp.zeros_like(dvacc)

        @pl.when((n == 0) & (sq == 0))
        def _():
            dsink_ref[...] = jnp.zeros_like(dsink_ref)

        r0 = pl.multiple_of(n * SWA_BLK, SWA_BLK)
        tq = _swa_qtab(tk_ref, r0)
        q = _rope(q_ref[...], tq)
        kb = kpad[pl.ds(r0, 3 * SWA_BLK), :]
        vb = vpad[pl.ds(r0, 3 * SWA_BLK), :]
        dk = jnp.zeros((3 * SWA_BLK, 128), F32)
        dv = jnp.zeros((3 * SWA_BLK, 128), F32)
        hrow = lax.broadcasted_iota(jnp.int32, (8, 128), 0)
        dsk = jnp.zeros((8, 128), F32)
        for hk in range(2):
            kexp = _swa_expand(kb, hk)
            vexp = _swa_expand(vb, hk)
            qs, p, ps, slot, rowg = _swa_probs(q[:, hk * 256:(hk + 1) * 256], kexp, n, sink_ref, hk)
            dy2 = dy_ref[:, hk * 256:(hk + 1) * 256]
            dos = jnp.concatenate([jnp.where(slot == g, dy2, 0.0) for g in range(4)], axis=0)
            dp = _mm_nt(dos, vexp)
            delta = jnp.sum(p * dp, axis=-1, keepdims=True)
            ds = p * (dp - delta) * 0.125
            dsr = -ps * delta
            for g in range(4):
                dsk = dsk + jnp.where(hrow == hk * 4 + g, jnp.sum(jnp.where(rowg == g, dsr, 0.0), axis=0, keepdims=True), 0.0)
            dq4 = _mm(ds, kexp)
            dq2 = jnp.zeros((SWA_BLK, 256), F32)
            for g in range(4):
                dq2 = dq2 + jnp.where(slot == g, dq4[g * SWA_BLK:(g + 1) * SWA_BLK], 0.0)
            dq_ref[:, hk * 256:(hk + 1) * 256] = dq2
            dk = dk + _swa_fold(_mm_tn(ds, qs), hk)
            dv = dv + _swa_fold(_mm_tn(p, dos), hk)
        dq_ref[...] = _rope_t(dq_ref[...], tq)
        dkacc[pl.ds(r0, 3 * SWA_BLK), :] += dk
        dvacc[pl.ds(r0, 3 * SWA_BLK), :] += dv
        dsink_ref[...] += dsk

        @pl.when(n == NBLK - 1)
        def _():
            dkv_ref[:, 0:128] = _rope_t(dkacc[SWA_BLK:SWA_BLK + L], tk_ref[...])
            dkv_ref[:, 128:256] = dvacc[SWA_BLK:SWA_BLK + L]

    blk = lambda col: pl.BlockSpec((SWA_BLK, 512), lambda s, n, sk: (s * NBLK + n, col))
    pad = pltpu.VMEM((L + 2 * SWA_BLK, 128), F32)
    return pl.pallas_call(
        body,
        grid_spec=pltpu.PrefetchScalarGridSpec(
            num_scalar_prefetch=1, grid=(NSEQ, NBLK),
            in_specs=[blk(2), pl.BlockSpec((L, 256), lambda s, n, sk: (s, 6)),
                      pl.BlockSpec((3, L, 128), lambda s, n, sk: (0, 0, 0)), blk(0)],
            out_specs=[blk(0), pl.BlockSpec((L, 256), lambda s, n, sk: (s, 0)),
                       pl.BlockSpec((8, 128), lambda s, n, sk: (0, 0))],
            scratch_shapes=[pad, pad, pad, pad]),
        out_shape=[_sds((N, 512)), _sds((N, 256)), _sds((8, 128))],
        name="swa_bwd", compiler_params=_cp(("arbitrary", "arbitrary")))(sink, h, h, tk, dyc)


def _outproj_fwd(ya, yb, yc, x, wo, g, b):
    tm = 512

    def body(ya_ref, yb_ref, yc_ref, x_ref, wo_ref, g_ref, b_ref, s_ref, x1_ref):
        mix = _mm(ya_ref[...], wo_ref[0:256]) + _mm(yb_ref[...], wo_ref[256:512]) + _mm(yc_ref[...], wo_ref[512:1024])
        s = ALPHA * x_ref[...] + mix
        s_ref[...] = s
        x1_ref[...] = _ln_fwd(s, g_ref[...], b_ref[...])

    row = lambda w_: pl.BlockSpec((tm, w_), lambda i: (i, 0))
    one = pl.BlockSpec((1, D), lambda i: (0, 0))
    return pl.pallas_call(
        body, grid=(N // tm,),
        in_specs=[row(256), row(256), row(512), row(D), pl.BlockSpec((D, D), lambda i: (0, 0)), one, one],
        out_specs=[row(D), row(D)], out_shape=[_sds((N, D)), _sds((N, D))],
        name="outproj_fwd", compiler_params=_cp(("parallel",)))(ya, yb, yc, x, wo, g, b)


def _outproj_bwd(dx1, s1, ya, yb, yc, wo, g):
    tm = 512
    nt = N // tm

    def body(dx1_ref, s_ref, ya_ref, yb_ref, yc_ref, wo_ref, g_ref,
             dya_ref, dyb_ref, dyc_ref, dxp_ref, dwo_ref, dg_ref, db_ref, acc):
        i = pl.program_id(0)

        @pl.when(i == 0)
        def _():
            acc[...] = jnp.zeros_like(acc)
            dg_ref[...] = jnp.zeros_like(dg_ref)
            db_ref[...] = jnp.zeros_like(db_ref)

        ds, dg, db = _ln_bwd(dx1_ref[...], s_ref[...], g_ref[...])
        dg_ref[...] += dg
        db_ref[...] += db
        dxp_ref[...] = ALPHA * ds
        dy = _mm_nt(ds, wo_ref[...])
        dya_ref[...] = dy[:, 0:256]
        dyb_ref[...] = dy[:, 256:512]
        dyc_ref[...] = dy[:, 512:1024]
        acc[0:256] += _mm_tn(ya_ref[...], ds)
        acc[256:512] += _mm_tn(yb_ref[...], ds)
        acc[512:1024] += _mm_tn(yc_ref[...], ds)

        @pl.when(i == nt - 1)
        def _():
            dwo_ref[...] = acc[...].astype(MX)

    row = lambda w_: pl.BlockSpec((tm, w_), lambda i: (i, 0))
    one = pl.BlockSpec((1, D), lambda i: (0, 0))
    full = pl.BlockSpec((D, D), lambda i: (0, 0))
    return pl.pallas_call(
        body, grid=(nt,),
        in_specs=[row(D), row(D), row(256), row(256), row(512), full, one],
        out_specs=[row(256), row(256), row(512), row(D), full, one, one],
        out_shape=[_sds((N, 256)), _sds((N, 256)), _sds((N, 512)), _sds((N, D)), _sds((D, D), MX), _sds((1, D)), _sds((1, D))],
        scratch_shapes=[pltpu.VMEM((D, D), F32)],
        name="outproj_bwd", compiler_params=_cp(("arbitrary",)))(dx1, s1, ya, yb, yc, wo, g)


def _ffn_fwd(x1, w1, w2, g, b):
    tm = FFN_TM

    def body(x_ref, w1_ref, w2_ref, g_ref, b_ref, a_ref, s_ref, x2_ref):
        j = pl.program_id(1)

        @pl.when(j == 0)
        def _():
            s_ref[...] = ALPHA * x_ref[...]

        a = _mm(x_ref[...], w1_ref[0])
        a_ref[...] = a.astype(MX)
        hid = jnp.square(jnp.maximum(a, 0.0))
        s_ref[...] += _mm(hid, w2_ref[0])

        @pl.when(j == NSHARD - 1)
        def _():
            x2_ref[...] = _ln_fwd(s_ref[...], g_ref[...], b_ref[...])

    row = pl.BlockSpec((tm, D), lambda i, j: (i, 0))
    wj = pl.BlockSpec((1, D, D), lambda i, j: (j, 0, 0))
    one = pl.BlockSpec((1, D), lambda i, j: (0, 0))
    return pl.pallas_call(
        body, grid=(N // tm, NSHARD),
        in_specs=[row, wj, wj, one, one],
        out_specs=[pl.BlockSpec((tm, D), lambda i, j: (i, j)), row, row],
        out_shape=[_sds((N, DFF), MX), _sds((N, D)), _sds((N, D))],
        name="ffn_fwd", compiler_params=_cp(("parallel", "arbitrary"), FFN_VMEM))(x1, w1, w2, g, b)


def _ffn_bwd_act(dy, s2, a, w1, w2, g):
    tm = FFN_TM

    def body(dy_ref, s_ref, a_ref, w1_ref, w2_ref, g_ref, da_ref, ds_ref, dx1_ref, dg_ref, db_ref):
        i = pl.program_id(0)
        j = pl.program_id(1)

        @pl.when((i == 0) & (j == 0))
        def _():
            dg_ref[...] = jnp.zeros_like(dg_ref)
            db_ref[...] = jnp.zeros_like(db_ref)

        @pl.when(j == 0)
        def _():
            ds, dg, db = _ln_bwd(dy_ref[...], s_ref[...], g_ref[...])
            ds_ref[...] = ds.astype(MX)
            dg_ref[...] += dg
            db_ref[...] += db
            dx1_ref[...] = ALPHA * ds

        dhid = _mm_nt(ds_ref[...], w2_ref[0])
        da = dhid * 2.0 * jnp.maximum(a_ref[...].astype(F32), 0.0)
        da_ref[...] = da.astype(MX)
        dx1_ref[...] += _mm_nt(da, w1_ref[0])

    row = pl.BlockSpec((tm, D), lambda i, j: (i, 0))
    col = pl.BlockSpec((tm, D), lambda i, j: (i, j))
    wj = pl.BlockSpec((1, D, D), lambda i, j: (j, 0, 0))
    one = pl.BlockSpec((1, D), lambda i, j: (0, 0))
    return pl.pallas_call(
        body, grid=(N // tm, NSHARD),
        in_specs=[row, row, col, wj, wj, one],
        out_specs=[col, row, row, one, one],
        out_shape=[_sds((N, DFF), MX), _sds((N, D), MX), _sds((N, D)), _sds((1, D)), _sds((1, D))],
        name="ffn_bwd_act", compiler_params=_cp(("arbitrary", "arbitrary"), FFN_VMEM))(dy, s2, a, w1, w2, g)


def _ffn_bwd_w(x1, da, a, ds):
    tm = FFN_TM_W
    nt = N // tm

    def body(x_ref, da_ref, a_ref, ds_ref, dw1_ref, dw2_ref, acc1, acc2):
        i = pl.program_id(1)

        @pl.when(i == 0)
        def _():
            acc1[...] = jnp.zeros_like(acc1)
            acc2[...] = jnp.zeros_like(acc2)

        acc1[...] += _mm_tn(x_ref[...], da_ref[...])
        hid = jnp.square(jnp.maximum(a_ref[...].astype(F32), 0.0))
        acc2[...] += _mm_tn(hid, ds_ref[...])

        @pl.when(i == nt - 1)
        def _():
            dw1_ref[0] = acc1[...].astype(MX)
            dw2_ref[0] = acc2[...].astype(MX)

    row = pl.BlockSpec((tm, D), lambda j, i: (i, 0))
    col = pl.BlockSpec((tm, D), lambda j, i: (i, j))
    wj = pl.BlockSpec((1, D, D), lambda j, i: (j, 0, 0))
    return pl.pallas_call(
        body, grid=(NSHARD, nt),
        in_specs=[row, col, col, row], out_specs=[wj, wj],
        out_shape=[_sds((NSHARD, D, D), MX), _sds((NSHARD, D, D), MX)],
        scratch_shapes=[pltpu.VMEM((D, D), F32), pltpu.VMEM((D, D), F32)],
        name="ffn_bwd_w", compiler_params=_cp(("parallel", "arbitrary")))(x1, da, a, ds)


def _loss_head(y, target):
    tm = 512

    def body(y_ref, t_ref, dy_ref, l_ref):
        @pl.when(pl.program_id(0) == 0)
        def _():
            l_ref[...] = jnp.zeros_like(l_ref)

        e = y_ref[...] - t_ref[...]
        dy_ref[...] = e * (1.0 / D)
        l_ref[...] += jnp.sum(jnp.sum(e * e, axis=1, keepdims=True), axis=0, keepdims=True) * (0.5 / D)

    row = pl.BlockSpec((tm, D), lambda i: (i, 0))
    return pl.pallas_call(
        body, grid=(N // tm,), in_specs=[row, row],
        out_specs=[row, pl.BlockSpec((8, 128), lambda i: (0, 0))],
        out_shape=[_sds((N, D)), _sds((8, 128))], name="loss_head", compiler_params=_cp(("arbitrary",)))(y, target)


def _s5_discretize(a_re, a_im, log_step, b_re, b_im):
    lam = lax.complex(a_re, a_im)
    lam_bar = jnp.exp(lam * jnp.exp(log_step))
    b_bar = ((lam_bar - 1.0) / lam)[..., None] * lax.complex(b_re, b_im)
    return jnp.real(lam_bar), jnp.imag(lam_bar), jnp.real(b_bar), jnp.imag(b_bar)


def _s5_in_blocks(b):
    e = jnp.eye(8, dtype=F32)
    return jnp.einsum('ij,zbjph->zbihjp', e, b.reshape(2, 2, 8, S5_P, S5_H)).reshape(2, 2, 128, SW)


def _s5_in_unblocks(d):
    return jnp.einsum('zbihip->zbiph', d.reshape(2, 2, 8, S5_H, 8, S5_P)).reshape(2, S5_G, S5_P, S5_H)


def _s5_out_blocks(c):
    e = jnp.eye(8, dtype=F32)
    return jnp.einsum('ij,zbjhp->zbjpih', e, c.reshape(2, 2, 8, S5_H, S5_P)).reshape(2, 2, SW, 128)


def _s5_out_unblocks(d):
    return jnp.einsum('zbipih->zbihp', d.reshape(2, 2, 8, S5_P, 8, S5_H)).reshape(2, S5_G, S5_H, S5_P)


def _gate_weight(w_a):
    z = jnp.zeros((16, 128), F32)
    top = jnp.concatenate([w_a[0], z], axis=1)
    bot = jnp.concatenate([z, w_a[1]], axis=1)
    return jnp.concatenate([top, bot, jnp.zeros((96, 256), F32)], axis=0)


def _layer_prep(p):
    lr, li, br, bi = _s5_discretize(p["s5_a_re"], p["s5_a_im"], p["s5_log_step"], p["s5_b_re"], p["s5_b_im"])
    q = dict(p)
    q["bre"] = _s5_in_blocks(br).astype(MX)
    q["bim"] = _s5_in_blocks(bi).astype(MX)
    q["cre"] = _s5_out_blocks(p["s5_c_re"]).astype(MX)
    q["cim"] = _s5_out_blocks(p["s5_c_im"]).astype(MX)
    mr, mi = lr.reshape(2, 1024), li.reshape(2, 1024)
    q["tab"] = jnp.stack([_scan_tables(mr[0], mi[0], False), _scan_tables(mr[1], mi[1], True)])
    q["tabc"] = jnp.stack([_scan_tables(mr[0], -mi[0], True), _scan_tables(mr[1], -mi[1], False)])
    q["dsk"] = p["s5_d"].reshape(1, 256)
    q["wa"] = _gate_weight(p["gla_w_a"]).astype(MX)
    q["ba"] = p["gla_b_a"].reshape(1, 256)
    q["lng"] = p["gla_ln_g"].reshape(1, 256)
    q["bv"] = p["s5_b_glu"][:256].reshape(1, 256)
    q["bg"] = p["s5_b_glu"][256:].reshape(1, 256)
    for k in ("ln1_g", "ln1_b", "ln2_g", "ln2_b"):
        q[k] = p[k].reshape(1, D)
    return q


def _layer_fwd(x, q, tk, fetch):
    q["w_in"] = fetch("w_in", x)
    h = _inproj_fwd(x, q["w_in"])
    hre, him, y2 = _s5_fwd(h, q["bre"], q["bim"], q["cre"], q["cim"], q["tab"])
    q["w4"] = fetch("s5_w_glu", y2)
    ya = _s5_glu_fwd(y2, h, q["dsk"], q["w4"], q["bv"], q["bg"])
    la2 = _gla_gate_fwd(h, q["wa"], q["ba"])
    of, ob, sf, sb = _gla_fwd(h, la2)
    yb = _gla_post_fwd(of, ob, h, q["lng"])
    yc = _swa_fwd(h, tk, q["swa_sink"])
    q["w_out"] = fetch("w_out", yc)
    s1, x1 = _outproj_fwd(ya, yb, yc, x, q["w_out"], q["ln1_g"], q["ln1_b"])
    q["w_ff1"] = fetch("w_ff1", x1)
    q["w_ff2"] = fetch("w_ff2", x1)
    a, s2, x2 = _ffn_fwd(x1, q["w_ff1"], q["w_ff2"], q["ln2_g"], q["ln2_b"])
    saved = dict(x=x, h=h, hre=hre, him=him, y2=y2, ya=ya, la2=la2, of=of, ob=ob, sf=sf, sb=sb, yb=yb, yc=yc,
                 s1=s1, x1=x1, a=a, s2=s2)
    return x2, saved


def _layer_bwd(dy, q, sv, tk, emit):
    g = {}
    da, ds2, dx1, g["dg2"], g["db2"] = _ffn_bwd_act(dy, sv["s2"], sv["a"], q["w_ff1"], q["w_ff2"], q["ln2_g"])
    dw1, dw2 = _ffn_bwd_w(sv["x1"], da, sv["a"], ds2)
    tie = emit(dict(w_ff1=dw1, w_ff2=dw2))
    dya, dyb, dyc, dxp, dwo, g["dg1"], g["db1"] = _outproj_bwd(dx1, sv["s1"], sv["ya"], sv["yb"], sv["yc"],
                                                               q["w_out"], q["ln1_g"] + tie)
    h = sv["h"]
    daq, dakv, g["dsink"] = _swa_bwd(h, tk, q["swa_sink"], dyc)
    do, gr, g["dlng"] = _gla_post_bwd(sv["of"], sv["ob"], h, q["lng"], dyb)
    gq_f, gk_f, gv_f, gl_f, gq_b, gk_b, gv_b, gl_b = _gla_bwd(h, sv["la2"], do, sv["sf"], sv["sb"])
    dhl, g["dwa"], g["dba"] = _gla_gate_bwd(h, q["wa"], q["ba"], gl_f, gl_b)
    dyp, dud, g["dd"], dw4, g["dbv"], g["dbg"] = _s5_glu_bwd(sv["y2"], h, q["dsk"], q["w4"], q["bv"], q["bg"], dya)
    tie = emit(dict(w_out=dwo.reshape(NSHARD, D // NSHARD, D), s5_w_glu=dw4))
    du2, g["dbre"], g["dbim"], g["dcre"], g["dcim"], g["dmu"] = _s5_bwd(
        h, dyp, sv["hre"], sv["him"], q["bre"], q["bim"], q["cre"], q["cim"], q["tabc"] + tie)
    dx, dwt = _inproj_bwd(sv["x"], q["w_in"], dxp, du2, dud, gq_f, gq_b, gk_f, gk_b, gv_f, gv_b, gr, daq, dakv, dhl)
    tie = emit(dict(w_in=_in_rows_back(dwt)))
    return dx, g, tie


NATIVE = ("dmu", "dbre", "dbim", "dcre", "dcim", "dd", "dbv", "dbg", "dwa", "dba", "dlng", "dsink",
          "dg1", "db1", "dg2", "db2")
ICI_CORE = (0, 0, 0, 1, 1, 0, 0, 0, 1, 1, 1, 1, 0, 0, 1, 1)


def _finish_small(n, w):
    g = {}
    dmu = n["dmu"]
    dlr = dmu[:, :, :, 0].reshape(DEPTH, 2, S5_G, S5_P)
    dli = dmu[:, :, :, 1].reshape(DEPTH, 2, S5_G, S5_P)

    def unblock(c, perm, shape):
        return c.reshape(DEPTH, 2, 2, S5_H, 8, S5_P).transpose(perm).reshape(shape)

    b_shape, c_shape = (DEPTH, 2, S5_G, S5_P, S5_H), (DEPTH, 2, S5_G, S5_H, S5_P)
    _, vjp = jax.vjp(_s5_discretize, w["s5_a_re"], w["s5_a_im"], w["s5_log_step"], w["s5_b_re"], w["s5_b_im"])
    (g["s5_a_re"], g["s5_a_im"], g["s5_log_step"], g["s5_b_re"], g["s5_b_im"]) = vjp(
        (dlr, dli, unblock(n["dbre"], (0, 1, 2, 4, 5, 3), b_shape), unblock(n["dbim"], (0, 1, 2, 4, 5, 3), b_shape)))
    g["s5_c_re"] = unblock(n["dcre"], (0, 1, 2, 4, 3, 5), c_shape)
    g["s5_c_im"] = unblock(n["dcim"], (0, 1, 2, 4, 3, 5), c_shape)
    g["s5_d"] = n["dd"].reshape(DEPTH, S5_G, S5_H)
    g["s5_b_glu"] = jnp.concatenate([n["dbv"], n["dbg"]], axis=2).reshape(DEPTH, 512)
    g["gla_w_a"] = jnp.stack([n["dwa"][:, 0:16, 0:128], n["dwa"][:, 16:32, 128:256]], axis=1)
    g["gla_b_a"] = n["dba"].reshape(DEPTH, 2, 128)
    g["gla_ln_g"] = n["dlng"].reshape(DEPTH, 256)
    g["swa_sink"] = n["dsink"][:, :, 0]
    for k, s in (("ln1_g", "dg1"), ("ln1_b", "db1"), ("ln2_g", "dg2"), ("ln2_b", "db2")):
        g[k] = n[s].reshape(DEPTH, D)
    return g


def _local_step(x, target, qs, tk, fetch, emit):
    saved = []
    for l, q in enumerate(qs):
        x, sv = _layer_fwd(x, q, tk, functools.partial(fetch, l))
        saved.append(sv)
    dy, lacc = _loss_head(x, target)
    smalls = [None] * DEPTH
    tie = 0.0
    for l in reversed(range(DEPTH)):
        qs[l]["ln2_g"] = qs[l]["ln2_g"] + tie
        dy, smalls[l], tie = _layer_bwd(dy, qs[l], saved[l], tk, functools.partial(emit, l))
    smalls[0]["db2"] = smalls[0]["db2"] + tie
    return lacc[0, 0], dy, smalls


BIG = ("w_in", "s5_w_glu", "w_out", "w_ff1", "w_ff2")
SMALL = ("s5_a_re", "s5_a_im", "s5_log_step", "s5_b_re", "s5_b_im", "s5_c_re", "s5_c_im", "s5_d", "s5_b_glu",
         "gla_w_a", "gla_b_a", "gla_ln_g", "swa_sink", "ln1_g", "ln1_b", "ln2_g", "ln2_b")
ANY = pl.BlockSpec(memory_space=pl.ANY)


def _place():
    x, y, c = lax.axis_index("x"), lax.axis_index("y"), lax.axis_index("c")
    return x, y, c, [(1 - x, y), (x, 1 - y), (1 - x, 1 - y)]


HBM = pl.BlockSpec(memory_space=pltpu.HBM)
SEMS = pl.BlockSpec(memory_space=pltpu.SEMAPHORE)
EFFECT = pltpu.SideEffectType.DATAFLOW_SIDE_EFFECTING


def _push_copies(ins, lands, send, recv, gather, sending):
    x, y, c, chips = _place()
    me = 2 * x + y
    out = []
    for a in range(len(lands)):
        for j, (px, py) in enumerate(chips):
            peer = 2 * px + py
            src = lands[a].at[me] if gather else ins[a].at[peer if sending else me]
            dst = lands[a].at[me if sending else peer]
            out.append(pltpu.make_async_remote_copy(src_ref=src, dst_ref=dst, send_sem=send.at[3 * a + j],
                                                    recv_sem=recv.at[3 * a + j], device_id=(px, py, c),
                                                    device_id_type=MESH))
    return out


def _push_start(name, arrs, gather):
    n = len(arrs)
    ops = list(arrs) if gather else list(arrs) + [lax.empty(s.shape, s.dtype) for s in arrs]
    m = len(ops)

    def body(*refs):
        ins, lnd = (refs[:n], refs[:n]) if gather else (refs[:n], refs[n:m])
        for cp in _push_copies(ins, lnd, refs[m], refs[m + 1], gather, True):
            cp.start()
        refs[-1][...] = jnp.zeros((8, 128), F32)

    ops = [pltpu.with_memory_space_constraint(t, pltpu.HBM) for t in ops]
    res = pl.pallas_call(
        body, name=name,
        out_shape=(pltpu.SemaphoreType.DMA((3 * n,)), pltpu.SemaphoreType.DMA((3 * n,)),
                   *[pltpu.HBM(t.shape, t.dtype) for t in ops], _sds((8, 128))),
        in_specs=[HBM] * m,
        out_specs=(SEMS, SEMS, *[HBM] * m, pl.BlockSpec(memory_space=pltpu.VMEM)),
        input_output_aliases={i: 2 + i for i in range(m)},
        compiler_params=pltpu.CompilerParams(has_side_effects=EFFECT))(*ops)
    return res[0], res[1], list(res[2:2 + m]), res[-1]


def _push_wait(name, started, after, gather):
    send, recv, ops, _ = started
    m = len(ops)
    n = m if gather else m // 2

    def body(*refs):
        ins, lnd = (refs[:n], refs[:n]) if gather else (refs[:n], refs[n:m])
        for cp in _push_copies(ins, lnd, refs[m], refs[m + 1], gather, False):
            cp.wait_send()
            cp.wait_recv()

    res = pl.pallas_call(
        body, name=name,
        out_shape=[pltpu.HBM(t.shape, t.dtype) for t in ops],
        in_specs=[HBM] * m + [SEMS, SEMS, ANY], out_specs=[HBM] * m,
        input_output_aliases={i: i for i in range(m)},
        compiler_params=pltpu.CompilerParams(has_side_effects=EFFECT))(*ops, send, recv, after)
    return list(res)


def _row_tile(rows):
    return max(t for t in range(8, min(rows, 512) + 1, 8) if rows % t == 0)


def _cast_to_slot(me, w, l):
    _, rows, cols = w.shape
    tr = _row_tile(rows)

    def body(me_ref, w_ref, o_ref):
        o_ref[0] = w_ref[0].astype(MX)

    return pl.pallas_call(
        body,
        grid_spec=pltpu.PrefetchScalarGridSpec(
            num_scalar_prefetch=1, grid=(rows // tr,),
            in_specs=[pl.BlockSpec((1, tr, cols), lambda i, me_: (l, i, 0))],
            out_specs=pl.BlockSpec((1, tr, cols), lambda i, me_: (me_[0], i, 0))),
        out_shape=_sds((NSHARD, rows, cols), MX), name="cast_to_slot", compiler_params=_cp(("arbitrary",)))(me, w)


def _sum_sources(me, recv, own):
    _, rows, cols = recv[0].shape
    tr = min(_row_tile(rows), 256) if rows % 256 == 0 else _row_tile(rows)
    nt = rows // tr

    def body(me_ref, *refs):
        o_ref = refs[-1]
        for l in range(DEPTH):
            @pl.when(pl.program_id(0) == l)
            def _():
                r_ref, own_ref = refs[2 * l], refs[2 * l + 1]
                part = [jnp.where(me_ref[0] == s, own_ref[0], r_ref[s]).astype(F32) for s in range(NSHARD)]
                o_ref[...] = ((part[0] + part[1]) + part[2]) + part[3]

    in_specs = []
    for l in range(DEPTH):
        pick = lambda g, i, me_, l=l: jnp.where(g == l, i, jnp.where(g < l, 0, nt - 1))
        in_specs += [pl.BlockSpec((NSHARD, tr, cols), lambda g, i, me_, pick=pick: (0, pick(g, i, me_), 0)),
                     pl.BlockSpec((1, tr, cols), lambda g, i, me_, pick=pick: (me_[0], pick(g, i, me_), 0))]
    return pl.pallas_call(
        body,
        grid_spec=pltpu.PrefetchScalarGridSpec(
            num_scalar_prefetch=1, grid=(DEPTH, nt), in_specs=in_specs,
            out_specs=pl.BlockSpec((tr, cols), lambda g, i, me_: (g * nt + i, 0))),
        out_shape=_sds((DEPTH * rows, cols)), name="sum_sources",
        compiler_params=_cp(("arbitrary", "arbitrary")))(me, *[t for l in range(DEPTH) for t in (recv[l], own[l])])


def _swap_sibling(arrs):
    n = len(arrs)

    def body(*refs):
        ins, outs = refs[:n], refs[n:2 * n]
        send, recv = refs[2 * n:]
        x, y, c, _ = _place()
        cps = [pltpu.make_async_remote_copy(src_ref=ins[a], dst_ref=outs[a], send_sem=send.at[a], recv_sem=recv.at[a],
                                            device_id=(x, y, 1 - c), device_id_type=MESH) for a in range(n)]
        for cp in cps:
            cp.start()
        for cp in cps:
            cp.wait()

    return pl.pallas_call(
        body, in_specs=[ANY] * n, out_specs=[ANY] * n, out_shape=[_sds(a.shape, a.dtype) for a in arrs],
        scratch_shapes=[pltpu.SemaphoreType.DMA((n,)), pltpu.SemaphoreType.DMA((n,))],
        name="swap_sibling")(*arrs)


def _allreduce_small(per_layer):
    nk = len(per_layer[0])
    n = DEPTH * nk
    shapes = [a.shape for a in per_layer[0]]

    def body(*refs):
        ins, outs = refs[:n], refs[n:n + nk]
        sibs, slots = refs[n + nk:n + 2 * nk], refs[n + 2 * nk:n + 3 * nk]
        send, recv = refs[n + 3 * nk:]
        x, y, c, chips = _place()
        me = 2 * x + y
        d2d = [pltpu.make_async_remote_copy(src_ref=ins[l * nk + k], dst_ref=sibs[k].at[l], send_sem=send.at[l * nk + k],
                                            recv_sem=recv.at[l * nk + k], device_id=(x, y, 1 - c), device_id_type=MESH)
               for l in range(DEPTH) for k in range(nk)]
        for cp in d2d:
            cp.start()
        for cp in d2d:
            cp.wait()
        for l in range(DEPTH):
            for k in range(nk):
                slots[k][me, l] = ins[l * nk + k][...] + sibs[k][l]

        def remote(k, j, slot):
            px, py = chips[j]
            return pltpu.make_async_remote_copy(src_ref=slots[k].at[me], dst_ref=slots[k].at[slot],
                                                send_sem=send.at[n + 3 * k + j], recv_sem=recv.at[n + 3 * k + j],
                                                device_id=(px, py, c), device_id_type=MESH)

        def handover(k):
            return pltpu.make_async_remote_copy(src_ref=outs[k], dst_ref=outs[k], send_sem=send.at[n + 3 * nk + k],
                                                recv_sem=recv.at[n + 3 * nk + k], device_id=(x, y, 1 - c),
                                                device_id_type=MESH)

        halves = (tuple(k for k in range(nk) if ICI_CORE[k] == 0), tuple(k for k in range(nk) if ICI_CORE[k] == 1))
        for cc in range(2):
            @pl.when(c == cc)
            def _():
                mine, theirs = halves[cc], halves[1 - cc]
                sends = [remote(k, j, me) for k in mine for j in range(3)]
                for cp in sends:
                    cp.start()
                for k in mine:
                    for j in range(3):
                        remote(k, j, 2 * chips[j][0] + chips[j][1]).wait_recv()
                for cp in sends:
                    cp.wait_send()
                for k in mine:
                    outs[k][...] = ((slots[k][0] + slots[k][1]) + slots[k][2]) + slots[k][3]
                over = [handover(k) for k in mine]
                for cp in over:
                    cp.start()
                for k in theirs:
                    handover(k).wait_recv()
                for cp in over:
                    cp.wait_send()

    vm = pl.BlockSpec(memory_space=pltpu.VMEM)
    return pl.pallas_call(
        body, in_specs=[vm] * n, out_specs=[vm] * nk, out_shape=[_sds((DEPTH,) + s) for s in shapes],
        scratch_shapes=([pltpu.VMEM((DEPTH,) + s, F32) for s in shapes]
                        + [pltpu.VMEM((NSHARD, DEPTH) + s, F32) for s in shapes]
                        + [pltpu.SemaphoreType.DMA((n + 4 * nk,)), pltpu.SemaphoreType.DMA((n + 4 * nk,))]),
        name="allreduce_small", compiler_params=pltpu.CompilerParams(vmem_limit_bytes=VMEM_LIMIT))(
            *[a for layer in per_layer for a in layer])


def _adamw_math(w, g, m, v):
    m = ADAM_B1 * m + (1.0 - ADAM_B1) * g
    v = ADAM_B2 * v + (1.0 - ADAM_B2) * jnp.square(g)
    m_hat = m / (1.0 - ADAM_B1 ** ADAM_STEP)
    v_hat = v / (1.0 - ADAM_B2 ** ADAM_STEP)
    delta = -ADAM_LR * (m_hat / (jnp.sqrt(v_hat) + ADAM_EPS) + ADAM_WD * w)
    return delta, m, v


def _adamw(g_parts, w, m, v):
    rows, cols = w.shape
    tr = 256 if rows % 256 == 0 else _row_tile(rows)
    k = len(g_parts)

    def body(*refs):
        g = refs[0][...]
        for r in refs[1:k]:
            g = g + r[...]
        w_ref, m_ref, v_ref, go, do, mo, vo = refs[k:]
        d, mn, vn = _adamw_math(w_ref[...], g, m_ref[...], v_ref[...])
        go[...] = g
        do[...] = d
        mo[...] = mn
        vo[...] = vn

    spec = pl.BlockSpec((tr, cols), lambda i: (i, 0))
    return pl.pallas_call(
        body, grid=(rows // tr,), in_specs=[spec] * (k + 3), out_specs=[spec] * 4,
        out_shape=[_sds((rows, cols))] * 4, name="adamw", compiler_params=_cp(("parallel",)))(*g_parts, w, m, v)


def _adamw_small(gs, ws, ms, vs):
    n = len(gs)

    def body(*refs):
        for k in range(n):
            d, mn, vn = _adamw_math(refs[n + k][...], refs[k][...], refs[2 * n + k][...], refs[3 * n + k][...])
            refs[4 * n + k][...] = d
            refs[5 * n + k][...] = mn
            refs[6 * n + k][...] = vn

    vm = pl.BlockSpec(memory_space=pltpu.VMEM)
    shapes = [_sds(a.shape) for a in ws]
    res = pl.pallas_call(
        body, in_specs=[vm] * (4 * n), out_specs=[vm] * (3 * n), out_shape=shapes * 3, name="adamw_small",
        compiler_params=pltpu.CompilerParams(vmem_limit_bytes=VMEM_LIMIT))(*gs, *ws, *ms, *vs)
    return res[:n], res[n:2 * n], res[2 * n:]


_ARGS = ("x", "w_in", "s5_a_re", "s5_a_im", "s5_log_step", "s5_b_re", "s5_b_im", "s5_c_re", "s5_c_im", "s5_d",
         "s5_w_glu", "s5_b_glu", "gla_w_a", "gla_b_a", "gla_ln_g", "swa_sink", "w_out", "ln1_g", "ln1_b", "w_ff1",
         "w_ff2", "ln2_g", "ln2_b")
_WEIGHTS = _ARGS[1:]


def _in_rows(g4):
    t = g4.reshape(DIN, D)
    return jnp.concatenate([t[0:1024], t[1056:DIN], t[1024:1056], jnp.zeros((DINP - DIN, D), t.dtype)], axis=0)


def _in_rows_back(d):
    return jnp.concatenate([d[0:1024], d[1792:1824], d[1024:1792]], axis=0).reshape(NSHARD, DIN // NSHARD, D).astype(MX)


def _shard_cols(d):
    return d.reshape(d.shape[0], NSHARD, d.shape[1] // NSHARD).transpose(1, 0, 2)


def kernel(x, w_in, s5_a_re, s5_a_im, s5_log_step, s5_b_re, s5_b_im, s5_c_re, s5_c_im, s5_d, s5_w_glu, s5_b_glu, gla_w_a, gla_b_a, gla_ln_g, swa_sink, w_out, ln1_g, ln1_b, w_ff1, w_ff2, ln2_g, ln2_b, loss_target, m_w_in, m_s5_a_re, m_s5_a_im, m_s5_log_step, m_s5_b_re, m_s5_b_im, m_s5_c_re, m_s5_c_im, m_s5_d, m_s5_w_glu, m_s5_b_glu, m_gla_w_a, m_gla_b_a, m_gla_ln_g, m_swa_sink, m_w_out, m_ln1_g, m_ln1_b, m_w_ff1, m_w_ff2, m_ln2_g, m_ln2_b, v_w_in, v_s5_a_re, v_s5_a_im, v_s5_log_step, v_s5_b_re, v_s5_b_im, v_s5_c_re, v_s5_c_im, v_s5_d, v_s5_w_glu, v_s5_b_glu, v_gla_w_a, v_gla_b_a, v_gla_ln_g, v_swa_sink, v_w_out, v_ln1_g, v_ln1_b, v_w_ff1, v_w_ff2, v_ln2_g, v_ln2_b):
    given = dict(locals())
    w = {k: given[k] for k in _WEIGHTS}
    mom = {k: given["m_" + k] for k in _WEIGHTS}
    var = {k: given["v_" + k] for k in _WEIGHTS}

    me = (2 * lax.axis_index("x") + lax.axis_index("y")).astype(jnp.int32).reshape(1)
    tr = lambda t: t.transpose(0, 2, 1)
    shard = {k: (tr(w[k]) if k == "w_in" else w[k]) for k in BIG}
    qs = [_layer_prep({k: w[k][l] for k in SMALL}) for l in range(DEPTH)]

    first = ("w_in", "s5_w_glu", "w_out")
    follow = {(0, "w_in"): [(0, BIG[3:]), (1, first)], (0, "w_ff1"): [(1, BIG[3:])]}
    gathers = {}

    def start_gather(l, names, behind=None):
        lands = [_cast_to_slot(me, shard[k], l) for k in names]
        if behind is not None:
            lands, behind = lax.optimization_barrier((lands, behind))
        st = _push_start(f"gather_start_{l}_{names[0]}", lands, True)
        for k in names:
            gathers[l, k] = [names, st, None]
        return st[-1], behind

    token = start_gather(0, first[:1])[0] + start_gather(0, first[1:])[0]

    def fetch(l, name, after):
        names, st, got = gathers[l, name]
        tie = None
        if got is None:
            if l == 0 and name == "w_in":
                after = token
            lands = _push_wait(f"gather_wait_{l}_{names[0]}", st, after, True)
            for l2, names2 in follow.get((l, name), ()):
                tok, lands[0] = start_gather(l2, names2, lands[0])
                tie = tok[0, 0] if tie is None else tie + tok[0, 0]
            got = dict(zip(names, lands))
            for k in names:
                gathers[l, k][2] = got
        full = got[name]
        if name == "w_in":
            return _in_rows(full) if tie is None else _in_rows(full) + tie.astype(MX)
        if tie is not None:
            qs[l]["ln2_b"] = qs[l]["ln2_b"] + tie
        return full.reshape(D, D) if name == "w_out" else full

    scatters = []

    def emit(l, grads):
        names = tuple(grads)
        st = _push_start(f"scatter_start_{l}_{names[0]}", [grads[k] for k in names], False)
        scatters.append((l, names, st))
        return st[-1][0, 0]

    loss, dx, smalls = _local_step(x.reshape(N, D), loss_target.reshape(N, D), qs, _rope_tables(128), fetch, emit)
    loss = lax.psum(loss, ("x", "y", "c"))

    out = {}
    native = _allreduce_small([[smalls[l][k] for k in NATIVE] for l in range(DEPTH)])
    gsmall = _finish_small(dict(zip(NATIVE, native)), w)
    res = _adamw_small(*([t[k] for k in SMALL] for t in (gsmall, w, mom, var)))
    for i, k in enumerate(SMALL):
        out[k] = [gsmall[k], res[0][i], res[1][i], res[2][i]]

    recv, own = {}, {}

    def finish(keys, after):
        for l, names, st in scatters:
            if names[0] in keys:
                ops = _push_wait(f"scatter_wait_{l}_{names[0]}", st, after, False)
                for i, k in enumerate(names):
                    own[l, k], recv[l, k] = ops[i], ops[len(names) + i]
        sums = [_sum_sources(me, [recv[l, k] for l in range(DEPTH)], [own[l, k] for l in range(DEPTH)]) for k in keys]
        for k, mine, other in zip(keys, sums, _swap_sibling(sums)):
            shp = shard[k].shape
            r = _adamw([mine, other], *((tr(t[k]) if k == "w_in" else t[k]).reshape(-1, shp[-1]) for t in (w, mom, var)))
            r = [t.reshape(shp) for t in r]
            out[k] = [tr(t) for t in r] if k == "w_in" else r
        return out[keys[-1]][1]

    last = finish(("w_ff1", "w_ff2", "w_out", "s5_w_glu"), res[0][-1])
    finish(("w_in",), last)

    return (loss, dx.reshape(NSEQ, L, D), *[out[k][0] for k in _WEIGHTS], *[out[k][1] for k in _WEIGHTS],
            *[out[k][2] for k in _WEIGHTS], *[out[k][3] for k in _WEIGHTS])
```

```python
import functools
import math

import jax
import jax.numpy as jnp
from jax import lax
from jax.experimental import pallas as pl
from jax.experimental.pallas import tpu as pltpu

F32 = jnp.float32
MX = jnp.bfloat16
MESH = pl.DeviceIdType.MESH

DEPTH = 2
NSEQ = 2
L = 2048
N = NSEQ * L
D = 1024
DFF = 4096
NSHARD = 4
S5_G, S5_H, S5_P = 16, 16, 64
GLA_CHUNK = 64
NCHUNK = L // GLA_CHUNK
SWA_BLK = 128
NBLK = L // SWA_BLK
ROT = 16
ROPE_THETA = 500000.0
LN_EPS = 1e-5
ALPHA = (2 * DEPTH) ** 0.25
NEG_BIG = -1e30
DIN = 1824
DINP = 1920
ADAM_LR, ADAM_B1, ADAM_B2, ADAM_EPS, ADAM_WD, ADAM_STEP = 0.001, 0.9, 0.999, 1e-08, 0.01, 10
VMEM_LIMIT = 56 * 1024 * 1024
TT = 512
SW = 512
FFN_TM = 1024
FFN_TM_W = 1024
FFN_VMEM = 60 * 1024 * 1024


def _cp(sem, vmem=VMEM_LIMIT):
    return pltpu.CompilerParams(dimension_semantics=sem, vmem_limit_bytes=vmem)


def _mm(a, b):
    return jnp.dot(a.astype(MX), b.astype(MX), preferred_element_type=F32)


def _mm_nt(a, b):
    return lax.dot_general(a.astype(MX), b.astype(MX), (((1,), (1,)), ((), ())), preferred_element_type=F32)


def _mm_tn(a, b):
    return lax.dot_general(a.astype(MX), b.astype(MX), (((0,), (0,)), ((), ())), preferred_element_type=F32)


@jax.custom_vjp
def _dmm(a, b):
    return _mm(a, b)


_dmm.defvjp(lambda a, b: (_mm(a, b), (a, b)), lambda r, g: (_mm_nt(g, r[1]), _mm_tn(r[0], g)))


@jax.custom_vjp
def _dmm_nt(a, b):
    return _mm_nt(a, b)


_dmm_nt.defvjp(lambda a, b: (_mm_nt(a, b), (a, b)), lambda r, g: (_mm(g, r[1]), _mm_tn(g, r[0])))


@jax.custom_vjp
def _dmm_tn(a, b):
    return _mm_tn(a, b)


_dmm_tn.defvjp(lambda a, b: (_mm_tn(a, b), (a, b)), lambda r, g: (_mm_nt(r[1], g), _mm(r[0], g)))


def _split3(x):
    hi = x.astype(MX)
    r1 = x - hi.astype(F32)
    mid = r1.astype(MX)
    lo = (r1 - mid.astype(F32)).astype(MX)
    return hi, mid, lo


def _tri(rev):
    r = lax.broadcasted_iota(jnp.int32, (GLA_CHUNK, GLA_CHUNK), 0)
    c = lax.broadcasted_iota(jnp.int32, (GLA_CHUNK, GLA_CHUNK), 1)
    return jnp.where((c >= r) if rev else (c <= r), 1.0, 0.0).astype(MX)


def _cums_impl(x, rev):
    t = _tri(rev)
    return sum(jnp.dot(t, p, preferred_element_type=F32) for p in _split3(x))


@functools.partial(jax.custom_vjp, nondiff_argnums=(1,))
def _cums(x, rev):
    return _cums_impl(x, rev)


_cums.defvjp(lambda x, rev: (_cums_impl(x, rev), None), lambda rev, r, g: (_cums_impl(g, not rev),))


def _ln_fwd(s, g, b):
    mu = jnp.mean(s, axis=-1, keepdims=True)
    xc = s - mu
    var = jnp.mean(xc * xc, axis=-1, keepdims=True)
    return xc * lax.rsqrt(var + LN_EPS) * g + b


def _ln_bwd(dy, s, g):
    mu = jnp.mean(s, axis=-1, keepdims=True)
    xc = s - mu
    var = jnp.mean(xc * xc, axis=-1, keepdims=True)
    rstd = lax.rsqrt(var + LN_EPS)
    xhat = xc * rstd
    dxh = dy * g
    ds = rstd * (dxh - jnp.mean(dxh, axis=-1, keepdims=True) - xhat * jnp.mean(dxh * xhat, axis=-1, keepdims=True))
    return ds, jnp.sum(dy * xhat, axis=0, keepdims=True), jnp.sum(dy, axis=0, keepdims=True)


def _sds(shape, dtype=F32):
    return jax.ShapeDtypeStruct(shape, dtype)


def _inproj_fwd(x, wt):
    tm = 512

    def body(x_ref, w_ref, h_ref):
        h_ref[...] = _mm_nt(x_ref[...], w_ref[...])

    return pl.pallas_call(
        body, grid=(N // tm,),
        in_specs=[pl.BlockSpec((tm, D), lambda i: (i, 0)), pl.BlockSpec((DINP, D), lambda i: (0, 0))],
        out_specs=pl.BlockSpec((tm, DINP), lambda i: (i, 0)),
        out_shape=_sds((N, DINP)), name="inproj_fwd", compiler_params=_cp(("parallel",)))(x, wt)


def _inproj_bwd(x, w, dxp, du2, dud, gq_f, gq_b, gk_f, gk_b, gv_f, gv_b, gr, daq, dakv, dhl):
    tm = 256
    nt = N // tm

    def body(x_ref, w_ref, dxp_ref, du2_ref, dud_ref, gqf, gqb, gkf, gkb, gvf, gvb, gr_ref, daq_ref, dakv_ref, dhl_ref,
             dx_ref, dw_ref):
        i = pl.program_id(0)
        dh = jnp.concatenate([
            du2_ref[0] + du2_ref[1] + dud_ref[...], gqf[...] + gqb[...], gkf[...] + gkb[...], gvf[...] + gvb[...],
            gr_ref[...], daq_ref[...], dakv_ref[...], dhl_ref[...]], axis=1)
        dx_ref[...] = dxp_ref[...] + _mm(dh, w_ref[...])
        contrib = _mm_tn(dh, x_ref[...])

        @pl.when(i == 0)
        def _():
            dw_ref[...] = contrib

        @pl.when(i > 0)
        def _():
            dw_ref[...] += contrib

    row = lambda w_: pl.BlockSpec((tm, w_), lambda i: (i, 0))
    return pl.pallas_call(
        body, grid=(nt,),
        in_specs=[row(D), pl.BlockSpec((DINP, D), lambda i: (0, 0)), row(D),
                  pl.BlockSpec((2, tm, 256), lambda i: (0, i, 0)), row(256), row(128), row(128), row(128), row(128),
                  row(256), row(256), row(256), row(512), row(256), row(128)],
        out_specs=[row(D), pl.BlockSpec((DINP, D), lambda i: (0, 0))],
        out_shape=[_sds((N, D)), _sds((DINP, D))],
        name="inproj_bwd", compiler_params=_cp(("arbitrary",)))(
            x, w, dxp, du2, dud, gq_f, gq_b, gk_f, gk_b, gv_f, gv_b, gr, daq, dakv, dhl)


def _scan_tables(mr, mi, reverse):
    pw = [(mr, mi)]
    for _ in range(7):
        pr, pi = pw[-1]
        pw.append((pr * mr - pi * mi, pr * mi + pi * mr))
    rows = jnp.arange(8)[:, None]
    out = []
    for d in (1, 2, 4):
        keep = rows >= d
        out += [jnp.where(keep, pw[d - 1][0][None], 0.0), jnp.where(keep, pw[d - 1][1][None], 0.0)]
    out += [jnp.stack([p[0] for p in pw]), jnp.stack([p[1] for p in pw])]
    t = jnp.stack(out)
    if reverse:
        t = t[:, ::-1, :]
    return t.reshape(8, 8, 2, SW).transpose(2, 0, 1, 3)


def _tile_scan(xr, xi, a, cr, ci, reverse):
    for lvl, d in enumerate((1, 2, 4)):
        sh = 8 - d if reverse else d
        sr = pltpu.roll(xr, sh, 0)
        si = pltpu.roll(xi, sh, 0)
        ar, ai = a[2 * lvl], a[2 * lvl + 1]
        xr, xi = xr + ar * sr - ai * si, xi + ar * si + ai * sr
    pr, pi = a[6], a[7]
    return xr + pr * cr - pi * ci, xi + pr * ci + pi * cr


NJ = TT // 8


def _lockstep_tables(mr, mi, reverse):
    nr, ni = mr, mi
    for _ in range(NJ.bit_length() - 1):
        nr, ni = nr * nr - ni * ni, 2.0 * nr * ni
    pr, pi = mr[None], mi[None]
    while pr.shape[0] < NJ:
        k = pr.shape[0]
        tr, ti = pr[k - 1], pi[k - 1]
        pr, pi = (jnp.concatenate([pr, pr * tr - pi * ti]), jnp.concatenate([pi, pr * ti + pi * tr]))
    if reverse:
        pr, pi = pr[::-1], pi[::-1]
    rows = jnp.broadcast_to(jnp.stack([mr, mi])[:, None, :], (2, 8, 2 * SW))
    link = _scan_tables(nr, ni, reverse)
    a = jnp.concatenate([rows.reshape(2, 8, 2, SW).transpose(2, 0, 1, 3), link], axis=1)
    return a, jnp.stack([pr, pi]).reshape(2, NJ, 2, SW).transpose(2, 0, 1, 3)


def _to_lockstep(ref, *lead):
    return jnp.concatenate([ref[(*lead, pl.ds(j, 8, stride=NJ), slice(None))] for j in range(NJ)], axis=0)


def _from_lockstep(val, ref, *lead):
    for j in range(NJ):
        ref[(*lead, pl.ds(j, 8, stride=NJ), slice(None))] = val[8 * j:8 * j + 8]


def _expand_powers(p_ref, pexp):
    for c in range(2):
        for j in range(NJ):
            pexp[c, j] = jnp.broadcast_to(p_ref[0, 0, c, j:j + 1, :], (8, SW))


def _lockstep_scan(xre, xim, a_ref, pexp, car, reverse, extra=None):
    a = [a_ref[0, 0, k] for k in range(10)]
    mr, mi = a[0], a[1]
    order = (lambda i: NJ - 1 - i) if reverse else (lambda i: i)

    def local(i, hcar):
        hr, hi = hcar
        r0 = pl.multiple_of(order(i) * 8, 8)
        hr, hi = mr * hr - mi * hi + xre[pl.ds(r0, 8), :], mr * hi + mi * hr + xim[pl.ds(r0, 8), :]
        xre[pl.ds(r0, 8), :] = hr
        xim[pl.ds(r0, 8), :] = hi
        return hr, hi

    z8 = jnp.zeros((8, SW), F32)
    er, ei = lax.fori_loop(0, NJ, local, (z8, z8), unroll=4)
    c0r, c0i = car[0], car[1]
    er, ei = _tile_scan(er, ei, a[2:], c0r, c0i, reverse)
    rowid = lax.broadcasted_iota(jnp.int32, (8, SW), 0)
    first, sh, last = (7, 7, 0) if reverse else (0, 1, 7)
    cvr = jnp.where(rowid == first, c0r, pltpu.roll(er, sh, 0))
    cvi = jnp.where(rowid == first, c0i, pltpu.roll(ei, sh, 0))
    car[0] = jnp.broadcast_to(er[last:last + 1, :], (8, SW))
    car[1] = jnp.broadcast_to(ei[last:last + 1, :], (8, SW))

    def fix(i, carry):
        j = order(i)
        r0 = pl.multiple_of(j * 8, 8)
        pr, pi = pexp[0, j], pexp[1, j]
        sr = xre[pl.ds(r0, 8), :] + pr * cvr - pi * cvi
        si = xim[pl.ds(r0, 8), :] + pr * cvi + pi * cvr
        xre[pl.ds(r0, 8), :] = sr
        xim[pl.ds(r0, 8), :] = si
        if extra is None:
            return carry
        return (sr, si, extra(r0, sr, si, carry[0], carry[1], carry[2]))

    init = (cvr, cvi, extra(None, None, None, None, None, None)) if extra is not None else 0
    return lax.fori_loop(0, NJ, fix, init, unroll=4)


def _s5_time_block(z, s, t, adjoint):
    flip = (1 - z) if adjoint else z
    return s * (L // TT) + t + flip * (L // TT - 1 - 2 * t)


def _s5_fwd(h, bre, bim, cre, cim, tab):
    nt = L // TT
    taba, tabp = tab

    def body(u_ref, bre_ref, bim_ref, cre_ref, cim_ref, a_ref, p_ref, hre_ref, him_ref, y_ref, car, pexp):
        z = pl.program_id(1)
        s = pl.program_id(2)
        tc = pl.program_id(3)

        @pl.when(tc == 0)
        def _():
            car[...] = jnp.zeros_like(car)

        @pl.when((tc == 0) & (s == 0))
        def _():
            _expand_powers(p_ref, pexp)

        u = _to_lockstep(u_ref)
        hre_ref[0] = _mm(u, bre_ref[0, 0])
        him_ref[0] = _mm(u, bim_ref[0, 0])

        @pl.when(z == 0)
        def _():
            _lockstep_scan(hre_ref.at[0], him_ref.at[0], a_ref, pexp, car, False)

        @pl.when(z == 1)
        def _():
            _lockstep_scan(hre_ref.at[0], him_ref.at[0], a_ref, pexp, car, True)

        _from_lockstep(_mm(hre_ref[0], cre_ref[0, 0]) - _mm(him_ref[0], cim_ref[0, 0]), y_ref, 0)

    tb = lambda b, z, s, t: _s5_time_block(z, s, t, False)
    wspec = lambda r, c: pl.BlockSpec((1, 1, r, c), lambda b, z, s, t: (z, b, 0, 0))
    return pl.pallas_call(
        body, grid=(2, 2, NSEQ, nt),
        in_specs=[pl.BlockSpec((TT, 128), lambda b, z, s, t: (tb(b, z, s, t), b)),
                  wspec(128, SW), wspec(128, SW), wspec(SW, 128), wspec(SW, 128),
                  pl.BlockSpec((1, 1, 10, 8, SW), lambda b, z, s, t: (z, b, 0, 0, 0)),
                  pl.BlockSpec((1, 1, 2, NJ, SW), lambda b, z, s, t: (z, b, 0, 0, 0))],
        out_specs=[pl.BlockSpec((1, TT, SW), lambda b, z, s, t: (z, tb(b, z, s, t), b)),
                   pl.BlockSpec((1, TT, SW), lambda b, z, s, t: (z, tb(b, z, s, t), b)),
                   pl.BlockSpec((1, TT, 128), lambda b, z, s, t: (z, tb(b, z, s, t), b))],
        out_shape=[_sds((2, N, 2 * SW)), _sds((2, N, 2 * SW)), _sds((2, N, 256))],
        scratch_shapes=[pltpu.VMEM((2, 8, SW), F32), pltpu.VMEM((2, NJ, 8, SW), F32)],
        name="s5_fwd", compiler_params=_cp(("arbitrary",) * 4))(h, bre, bim, cre, cim, taba, tabp)


def _s5_bwd(h, dyp, hre, him, bre, bim, cre, cim, tabc):
    nt = L // TT
    taba, tabp = tabc

    def body(u_ref, dy_ref, hre_ref, him_ref, bre_ref, bim_ref, cre_ref, cim_ref, a_ref, p_ref,
             du_ref, dbre_ref, dbim_ref, dcre_ref, dcim_ref, dmu_ref, gre, gim, car, acc, macc, pexp):
        z = pl.program_id(1)
        s = pl.program_id(2)
        tc = pl.program_id(3)

        @pl.when(tc == 0)
        def _():
            car[...] = jnp.zeros_like(car)

        @pl.when((tc == 0) & (s == 0))
        def _():
            acc[...] = jnp.zeros_like(acc)
            macc[...] = jnp.zeros_like(macc)
            _expand_powers(p_ref, pexp)

        dy = _to_lockstep(dy_ref)
        gre[...] = _mm_nt(dy, cre_ref[0, 0])
        gim[...] = -_mm_nt(dy, cim_ref[0, 0])

        def run(reverse):
            def pair(r0, gr_, gi_, pvr, pvi, m):
                if r0 is None:
                    return (macc[0], macc[1])
                hr = hre_ref[0, pl.ds(r0, 8), :]
                hi = him_ref[0, pl.ds(r0, 8), :]
                return (m[0] + pvr * hr + pvi * hi, m[1] + pvi * hr - pvr * hi)

            _, _, (dmr, dmi) = _lockstep_scan(gre, gim, a_ref, pexp, car, reverse, pair)
            macc[0] = dmr
            macc[1] = dmi

        @pl.when(z == 0)
        def _():
            run(True)

        @pl.when(z == 1)
        def _():
            run(False)

        gr = gre[...]
        gi = gim[...]
        u = _to_lockstep(u_ref)
        _from_lockstep(_mm_nt(gr, bre_ref[0, 0]) + _mm_nt(gi, bim_ref[0, 0]), du_ref, 0)
        acc[0] += _mm_tn(u, gr)
        acc[1] += _mm_tn(u, gi)
        acc[2] += _mm_tn(dy, hre_ref[0])
        acc[3] -= _mm_tn(dy, him_ref[0])

        @pl.when((tc == nt - 1) & (s == NSEQ - 1))
        def _():
            grp = lax.broadcasted_iota(jnp.int32, (S5_H, SW), 1) // S5_P
            for k, out in enumerate((dbre_ref, dbim_ref, dcre_ref, dcim_ref)):
                c = jnp.zeros((S5_H, SW), F32)
                for i in range(8):
                    c = c + jnp.where(grp == i, acc[k, i * S5_H:(i + 1) * S5_H, :], 0.0)
                out[0, 0] = c
            dmu_ref[0, 0] = jnp.concatenate([jnp.sum(macc[0], axis=0, keepdims=True),
                                             jnp.sum(macc[1], axis=0, keepdims=True)], axis=0)

    tb = lambda b, z, s, t: _s5_time_block(z, s, t, True)
    wspec = lambda r, c: pl.BlockSpec((1, 1, r, c), lambda b, z, s, t: (z, b, 0, 0))
    tok = lambda w_: pl.BlockSpec((TT, w_), lambda b, z, s, t: (tb(b, z, s, t), b))
    st = pl.BlockSpec((1, TT, SW), lambda b, z, s, t: (z, tb(b, z, s, t), b))
    return pl.pallas_call(
        body, grid=(2, 2, NSEQ, nt),
        in_specs=[tok(128), tok(128), st, st, wspec(128, SW), wspec(128, SW), wspec(SW, 128), wspec(SW, 128),
                  pl.BlockSpec((1, 1, 10, 8, SW), lambda b, z, s, t: (z, b, 0, 0, 0)),
                  pl.BlockSpec((1, 1, 2, NJ, SW), lambda b, z, s, t: (z, b, 0, 0, 0))],
        out_specs=[pl.BlockSpec((1, TT, 128), lambda b, z, s, t: (z, tb(b, z, s, t), b)),
                   wspec(S5_H, SW), wspec(S5_H, SW), wspec(S5_H, SW), wspec(S5_H, SW),
                   wspec(2, SW)],
        out_shape=[_sds((2, N, 256))] + [_sds((2, 2, S5_H, SW))] * 4 + [_sds((2, 2, 2, SW))],
        scratch_shapes=[pltpu.VMEM((TT, SW), F32), pltpu.VMEM((TT, SW), F32), pltpu.VMEM((2, 8, SW), F32),
                        pltpu.VMEM((4, 128, SW), F32), pltpu.VMEM((2, 8, SW), F32), pltpu.VMEM((2, NJ, 8, SW), F32)],
        name="s5_bwd", compiler_params=_cp(("arbitrary",) * 4))(h, dyp, hre, him, bre, bim, cre, cim, taba, tabp)


_GELU_C = math.sqrt(2.0 / math.pi)


def _gelu(y):
    return 0.5 * y * (1.0 + jnp.tanh(_GELU_C * (y + 0.044715 * y * y * y)))


def _gelu_grad(y):
    t = jnp.tanh(_GELU_C * (y + 0.044715 * y * y * y))
    return 0.5 * (1.0 + t) + 0.5 * y * (1.0 - t * t) * _GELU_C * (1.0 + 3 * 0.044715 * y * y)


def _glu_halves(w4_ref):
    return (jnp.concatenate([w4_ref[0], w4_ref[1]], axis=1), jnp.concatenate([w4_ref[2], w4_ref[3]], axis=1))


def _s5_glu_fwd(y2, h, dsk, w4, bv, bg):
    tm = 512

    def body(y2_ref, u_ref, d_ref, w4_ref, bv_ref, bg_ref, ya_ref):
        wv, wg = _glu_halves(w4_ref)
        z = _gelu(y2_ref[0] + y2_ref[1] + d_ref[...] * u_ref[...])
        val = _mm(z, wv) + bv_ref[...]
        gate = _mm(z, wg) + bg_ref[...]
        ya_ref[...] = val * jax.nn.sigmoid(gate)

    full = lambda r, c: pl.BlockSpec((r, c), lambda i: (0, 0))
    return pl.pallas_call(
        body, grid=(N // tm,),
        in_specs=[pl.BlockSpec((2, tm, 256), lambda i: (0, i, 0)), pl.BlockSpec((tm, 256), lambda i: (i, 0)),
                  full(1, 256), pl.BlockSpec((NSHARD, 256, 128), lambda i: (0, 0, 0)), full(1, 256), full(1, 256)],
        out_specs=pl.BlockSpec((tm, 256), lambda i: (i, 0)),
        out_shape=_sds((N, 256)), name="s5_glu_fwd", compiler_params=_cp(("parallel",)))(y2, h, dsk, w4, bv, bg)


def _s5_glu_bwd(y2, h, dsk, w4, bv, bg, dya):
    tm = 512
    nt = N // tm

    def body(y2_ref, u_ref, d_ref, w4_ref, bv_ref, bg_ref, dya_ref,
             dyp_ref, dud_ref, dd_ref, dw4_ref, dbv_ref, dbg_ref, accv, accg):
        i = pl.program_id(0)

        @pl.when(i == 0)
        def _():
            for r in (dd_ref, accv, accg, dbv_ref, dbg_ref):
                r[...] = jnp.zeros_like(r)

        wv, wg = _glu_halves(w4_ref)
        u = u_ref[...]
        y = y2_ref[0] + y2_ref[1] + d_ref[...] * u
        z = _gelu(y)
        val = _mm(z, wv) + bv_ref[...]
        sig = jax.nn.sigmoid(_mm(z, wg) + bg_ref[...])
        dya = dya_ref[...]
        dval = dya * sig
        dgate = dya * val * sig * (1.0 - sig)
        dz = _mm_nt(dval, wv) + _mm_nt(dgate, wg)
        dy = dz * _gelu_grad(y)
        dyp_ref[...] = dy
        dud_ref[...] = dy * d_ref[...]
        dd_ref[...] += jnp.sum(dy * u, axis=0, keepdims=True)
        accv[...] += _mm_tn(z, dval)
        accg[...] += _mm_tn(z, dgate)
        dbv_ref[...] += jnp.sum(dval, axis=0, keepdims=True)
        dbg_ref[...] += jnp.sum(dgate, axis=0, keepdims=True)

        @pl.when(i == nt - 1)
        def _():
            dw4_ref[0] = accv[:, 0:128].astype(MX)
            dw4_ref[1] = accv[:, 128:256].astype(MX)
            dw4_ref[2] = accg[:, 0:128].astype(MX)
            dw4_ref[3] = accg[:, 128:256].astype(MX)

    full = lambda r, c: pl.BlockSpec((r, c), lambda i: (0, 0))
    row = pl.BlockSpec((tm, 256), lambda i: (i, 0))
    wspec = pl.BlockSpec((NSHARD, 256, 128), lambda i: (0, 0, 0))
    return pl.pallas_call(
        body, grid=(nt,),
        in_specs=[pl.BlockSpec((2, tm, 256), lambda i: (0, i, 0)), row, full(1, 256), wspec, full(1, 256), full(1, 256),
                  row],
        out_specs=[row, row, full(1, 256), wspec, full(1, 256), full(1, 256)],
        out_shape=[_sds((N, 256)), _sds((N, 256)), _sds((1, 256)), _sds((NSHARD, 256, 128), MX), _sds((1, 256)),
                   _sds((1, 256))],
        scratch_shapes=[pltpu.VMEM((256, 256), F32), pltpu.VMEM((256, 256), F32)],
        name="s5_glu_bwd", compiler_params=_cp(("arbitrary",)))(y2, h, dsk, w4, bv, bg, dya)


def _logsig(x):
    return jnp.minimum(x, 0.0) - jnp.log(1.0 + jnp.exp(-jnp.abs(x)))


def _gla_gate_fwd(h, wa, ba):
    tm = 512

    def body(hl_ref, wa_ref, ba_ref, la_ref):
        la_ref[...] = _logsig(_mm(hl_ref[...], wa_ref[...]) + ba_ref[...]) * (1.0 / 16.0)

    return pl.pallas_call(
        body, grid=(N // tm,),
        in_specs=[pl.BlockSpec((tm, 128), lambda i: (i, 14)), pl.BlockSpec((128, 256), lambda i: (0, 0)),
                  pl.BlockSpec((1, 256), lambda i: (0, 0))],
        out_specs=pl.BlockSpec((tm, 256), lambda i: (i, 0)),
        out_shape=_sds((N, 256)), name="gla_gate_fwd", compiler_params=_cp(("parallel",)))(h, wa, ba)


def _gla_gate_bwd(h, wa, ba, dla_f, dla_b):
    tm = 512

    def body(hl_ref, wa_ref, ba_ref, df_ref, db_ref, dhl_ref, dwa_ref, dba_ref):
        i = pl.program_id(0)

        @pl.when(i == 0)
        def _():
            dwa_ref[...] = jnp.zeros_like(dwa_ref)
            dba_ref[...] = jnp.zeros_like(dba_ref)

        hl = hl_ref[...]
        pre = _mm(hl, wa_ref[...]) + ba_ref[...]
        dpre = jnp.concatenate([df_ref[...], db_ref[...]], axis=1) * (1.0 / 16.0) * jax.nn.sigmoid(-pre)
        dhl_ref[...] = _mm_nt(dpre, wa_ref[...])
        dwa_ref[...] += _mm_tn(hl, dpre)[0:32]
        dba_ref[...] += jnp.sum(dpre, axis=0, keepdims=True)

    row = pl.BlockSpec((tm, 128), lambda i: (i, 0))
    return pl.pallas_call(
        body, grid=(N // tm,),
        in_specs=[pl.BlockSpec((tm, 128), lambda i: (i, 14)), pl.BlockSpec((128, 256), lambda i: (0, 0)),
                  pl.BlockSpec((1, 256), lambda i: (0, 0)), row, row],
        out_specs=[row, pl.BlockSpec((32, 256), lambda i: (0, 0)), pl.BlockSpec((1, 256), lambda i: (0, 0))],
        out_shape=[_sds((N, 128)), _sds((32, 256)), _sds((1, 256))],
        name="gla_gate_bwd", compiler_params=_cp(("arbitrary",)))(h, wa, ba, dla_f, dla_b)


def _gla_chunk(q, k, v, la, st, rev):
    c = GLA_CHUNK
    b = _cums(la, rev)
    bl = jnp.sum(la, axis=0, keepdims=True)
    q_in = q * (32.0 ** -0.5) * jnp.exp(b)
    k_in = k * jnp.exp(-b)
    k_st = k * jnp.exp(bl - b)
    lane_k = lax.broadcasted_iota(jnp.int32, (1, 128), 1) // 32
    lane_v = lax.broadcasted_iota(jnp.int32, (1, 256), 1) // 64
    r = lax.broadcasted_iota(jnp.int32, (c, c), 0)
    cc = lax.broadcasted_iota(jnp.int32, (c, c), 1)
    keep = (cc > r) if rev else (cc <= r)
    qs = jnp.concatenate([jnp.where(lane_k == hd, q_in, 0.0) for hd in range(4)], axis=0)
    a = _dmm_nt(qs, k_in)
    a = jnp.where(jnp.concatenate([keep] * 4, axis=0), a, 0.0)
    o4 = _dmm(a, v)
    o = _dmm_nt(q_in, st)
    for hd in range(4):
        o = o + jnp.where(lane_v == hd, o4[hd * c:(hd + 1) * c], 0.0)
    bd = (lax.broadcasted_iota(jnp.int32, (256, 128), 0) // 64) == (lax.broadcasted_iota(jnp.int32, (256, 128), 1) // 32)
    st_new = jnp.exp(bl) * st + jnp.where(bd, _dmm_tn(v, k_st), 0.0)
    return o, st_new


def _gla_chunk_of(c, rev):
    return NCHUNK - 1 - c if rev else c


def _gla_fwd(h, la2):
    c = GLA_CHUNK

    def body(qf, kf, vf, laf, qb, kb, vb, lab, of_ref, ob_ref, sf_ref, sb_ref, stf, stb):
        @pl.when(pl.program_id(0) == 0)
        def _():
            stf[...] = jnp.zeros_like(stf)
            stb[...] = jnp.zeros_like(stb)

        ins = [(qf[s], kf[s], vf[s], laf[s], stf[s], qb[s], kb[s], vb[s], lab[s], stb[s]) for s in range(NSEQ)]
        outs = [(_gla_chunk(*t[:5], False), _gla_chunk(*t[5:], True)) for t in ins]
        for s in range(NSEQ):
            sf_ref[s, 0] = ins[s][4]
            sb_ref[s, 0] = ins[s][9]
            (of_ref[s], stf[s]), (ob_ref[s], stb[s]) = outs[s]

    def specs(rev):
        ch = lambda i: _gla_chunk_of(i, rev)
        return [pl.BlockSpec((NSEQ, c, 128), lambda i: (0, ch(i), 2)), pl.BlockSpec((NSEQ, c, 128), lambda i: (0, ch(i), 3)),
                pl.BlockSpec((NSEQ, c, 256), lambda i: (0, ch(i), 2)),
                pl.BlockSpec((NSEQ, c, 128), lambda i: (0, ch(i), 1 if rev else 0))]

    orow = lambda rev: pl.BlockSpec((NSEQ, c, 256), lambda i: (0, _gla_chunk_of(i, rev), 0))
    srow = lambda rev: pl.BlockSpec((NSEQ, 1, 256, 128), lambda i: (0, _gla_chunk_of(i, rev), 0, 0))
    h3, la3 = h.reshape(NSEQ, L, DINP), la2.reshape(NSEQ, L, 256)
    of, ob, sf, sb = pl.pallas_call(
        body, grid=(NCHUNK,),
        in_specs=specs(False) + specs(True),
        out_specs=[orow(False), orow(True), srow(False), srow(True)],
        out_shape=[_sds((NSEQ, L, 256)), _sds((NSEQ, L, 256)), _sds((NSEQ, NCHUNK, 256, 128)),
                   _sds((NSEQ, NCHUNK, 256, 128))],
        scratch_shapes=[pltpu.VMEM((NSEQ, 256, 128), F32), pltpu.VMEM((NSEQ, 256, 128), F32)],
        name="gla_fwd", compiler_params=_cp(("arbitrary",)))(h3, h3, h3, la3, h3, h3, h3, la3)
    return of.reshape(N, 256), ob.reshape(N, 256), sf, sb


def _gla_bwd(h, la2, do, sf, sb):
    c = GLA_CHUNK

    def body(qf, kf, vf, laf, dof, sfr, qb, kb, vb, lab, dob, sbr,
             dqf, dkf, dvf, dlf, dqb, dkb, dvb, dlb, dstf, dstb):
        @pl.when(pl.program_id(0) == 0)
        def _():
            dstf[...] = jnp.zeros_like(dstf)
            dstb[...] = jnp.zeros_like(dstb)

        def one(s, q, k, v, la, do_, st, dst, rev):
            _, vjp = jax.vjp(functools.partial(_gla_chunk, rev=rev), q[s], k[s], v[s], la[s], st[s, 0])
            return vjp((do_[s], dst[s]))

        res = [(one(s, qf, kf, vf, laf, dof, sfr, dstf, False), one(s, qb, kb, vb, lab, dob, sbr, dstb, True))
               for s in range(NSEQ)]
        for s in range(NSEQ):
            dqf[s], dkf[s], dvf[s], dlf[s], dstf[s] = res[s][0]
            dqb[s], dkb[s], dvb[s], dlb[s], dstb[s] = res[s][1]

    def specs(rev):
        ch = lambda i: _gla_chunk_of(i, not rev)
        return [pl.BlockSpec((NSEQ, c, 128), lambda i: (0, ch(i), 2)), pl.BlockSpec((NSEQ, c, 128), lambda i: (0, ch(i), 3)),
                pl.BlockSpec((NSEQ, c, 256), lambda i: (0, ch(i), 2)),
                pl.BlockSpec((NSEQ, c, 128), lambda i: (0, ch(i), 1 if rev else 0)),
                pl.BlockSpec((NSEQ, c, 256), lambda i: (0, ch(i), 0)),
                pl.BlockSpec((NSEQ, 1, 256, 128), lambda i: (0, ch(i), 0, 0))]

    def ospecs(rev):
        ch = lambda i: _gla_chunk_of(i, not rev)
        n = pl.BlockSpec((NSEQ, c, 128), lambda i: (0, ch(i), 0))
        return [n, n, pl.BlockSpec((NSEQ, c, 256), lambda i: (0, ch(i), 0)), n]

    oshape = [_sds((NSEQ, L, 128)), _sds((NSEQ, L, 128)), _sds((NSEQ, L, 256)), _sds((NSEQ, L, 128))]
    h3, la3, do3 = h.reshape(NSEQ, L, DINP), la2.reshape(NSEQ, L, 256), do.reshape(NSEQ, L, 256)
    res = pl.pallas_call(
        body, grid=(NCHUNK,),
        in_specs=specs(False) + specs(True),
        out_specs=ospecs(False) + ospecs(True),
        out_shape=oshape + oshape,
        scratch_shapes=[pltpu.VMEM((NSEQ, 256, 128), F32), pltpu.VMEM((NSEQ, 256, 128), F32)],
        name="gla_bwd", compiler_params=_cp(("arbitrary",)))(h3, h3, h3, la3, do3, sf, h3, h3, h3, la3, do3, sb)
    return [r.reshape(N, r.shape[-1]) for r in res]


def _gla_post(of, ob, r, g):
    o = of + ob
    head = lax.broadcasted_iota(jnp.int32, (1, 256), 1) // 64
    mu = jnp.zeros_like(o)
    for hd in range(4):
        mu = mu + jnp.where(head == hd, jnp.sum(jnp.where(head == hd, o, 0.0), axis=-1, keepdims=True) * (1.0 / 64.0), 0.0)
    xc = o - mu
    var = jnp.zeros_like(o)
    for hd in range(4):
        var = var + jnp.where(head == hd, jnp.sum(jnp.where(head == hd, xc * xc, 0.0), axis=-1, keepdims=True) * (1.0 / 64.0), 0.0)
    return xc * lax.rsqrt(var + LN_EPS) * g * (r * jax.nn.sigmoid(r))


def _gla_post_fwd(of, ob, h, g):
    tm = 512

    def body(of_ref, ob_ref, r_ref, g_ref, y_ref):
        y_ref[...] = _gla_post(of_ref[...], ob_ref[...], r_ref[...], g_ref[...])

    row = pl.BlockSpec((tm, 256), lambda i: (i, 0))
    return pl.pallas_call(
        body, grid=(N // tm,),
        in_specs=[row, row, pl.BlockSpec((tm, 256), lambda i: (i, 3)), pl.BlockSpec((1, 256), lambda i: (0, 0))],
        out_specs=row, out_shape=_sds((N, 256)), name="gla_post_fwd", compiler_params=_cp(("parallel",)))(of, ob, h, g)


def _gla_post_bwd(of, ob, h, g, dyb):
    tm = 512

    def body(of_ref, ob_ref, r_ref, g_ref, dy_ref, do_ref, dr_ref, dg_ref):
        @pl.when(pl.program_id(0) == 0)
        def _():
            dg_ref[...] = jnp.zeros_like(dg_ref)

        _, vjp = jax.vjp(_gla_post, of_ref[...], ob_ref[...], r_ref[...], g_ref[...])
        go, _, gr, gg = vjp(dy_ref[...])
        do_ref[...] = go
        dr_ref[...] = gr
        dg_ref[...] += gg

    row = pl.BlockSpec((tm, 256), lambda i: (i, 0))
    one = pl.BlockSpec((1, 256), lambda i: (0, 0))
    return pl.pallas_call(
        body, grid=(N // tm,),
        in_specs=[row, row, pl.BlockSpec((tm, 256), lambda i: (i, 3)), one, row],
        out_specs=[row, row, one], out_shape=[_sds((N, 256)), _sds((N, 256)), _sds((1, 256))],
        name="gla_post_bwd", compiler_params=_cp(("arbitrary",)))(of, ob, h, g, dyb)


def _rope_tables(width):
    pos = jnp.arange(L, dtype=F32)
    inv_freq = ROPE_THETA ** (-jnp.arange(0, ROT, 2, dtype=F32) / ROT)
    ang = pos[:, None] * inv_freq[None, :]
    cos, sin = jnp.cos(ang), jnp.sin(ang)
    one = jnp.ones((L, 64 - ROT), F32)
    zero = jnp.zeros((L, 64 - ROT), F32)
    z8 = jnp.zeros((L, ROT // 2), F32)
    c = jnp.concatenate([cos, cos, one], axis=1)
    sa = jnp.concatenate([z8, sin, zero], axis=1)
    sb = jnp.concatenate([-sin, z8, zero], axis=1)
    rep = width // 64
    return jnp.stack([jnp.tile(c, (1, rep)), jnp.tile(sa, (1, rep)), jnp.tile(sb, (1, rep))])


def _rope(t, tab):
    w = t.shape[-1]
    return t * tab[0] + pltpu.roll(t, ROT // 2, 1) * tab[1] + pltpu.roll(t, w - ROT // 2, 1) * tab[2]


def _rope_t(g, tab):
    w = g.shape[-1]
    return g * tab[0] + pltpu.roll(g * tab[1], w - ROT // 2, 1) + pltpu.roll(g * tab[2], ROT // 2, 1)


def _swa_pad_kv(kv_ref, tk_ref, kpad, vpad):
    z = jnp.zeros((SWA_BLK, 128), F32)
    kpad[0:SWA_BLK] = z
    vpad[0:SWA_BLK] = z
    kpad[SWA_BLK + L:] = z
    vpad[SWA_BLK + L:] = z
    kpad[SWA_BLK:SWA_BLK + L] = _rope(kv_ref[:, 0:128], tk_ref[...])
    vpad[SWA_BLK:SWA_BLK + L] = kv_ref[:, 128:256]


def _swa_expand(x, hk):
    lane = lax.broadcasted_iota(jnp.int32, x.shape, 1)
    sw = pltpu.roll(x, 64, 1)
    pair = jnp.where(lane < 64, x, sw) if hk == 0 else jnp.where(lane < 64, sw, x)
    return jnp.concatenate([pair, pair], axis=1)


def _swa_fold(x, hk):
    a = x[:, 0:128] + x[:, 128:256]
    t = a + pltpu.roll(a, 64, 1)
    lane = lax.broadcasted_iota(jnp.int32, a.shape, 1)
    return jnp.where((lane < 64) if hk == 0 else (lane >= 64), t, 0.0)


def _swa_probs(q2, kexp, n, sink_ref, hk):
    slot = lax.broadcasted_iota(jnp.int32, (1, 256), 1) // 64
    qs = jnp.concatenate([jnp.where(slot == g, q2, 0.0) for g in range(4)], axis=0)
    s = _mm_nt(qs, kexp) * 0.125
    i = lax.broadcasted_iota(jnp.int32, (SWA_BLK, 3 * SWA_BLK), 0)
    j = lax.broadcasted_iota(jnp.int32, (SWA_BLK, 3 * SWA_BLK), 1)
    kpos = n * SWA_BLK - SWA_BLK + j
    ok = (j - i >= 0) & (j - i <= 2 * SWA_BLK) & (kpos >= 0) & (kpos < L)
    s = jnp.where(jnp.concatenate([ok] * 4, axis=0), s, NEG_BIG)
    rowg = lax.broadcasted_iota(jnp.int32, (4 * SWA_BLK, 1), 0) // SWA_BLK
    sink = jnp.zeros((4 * SWA_BLK, 1), F32)
    for g in range(4):
        sink = jnp.where(rowg == g, sink_ref[hk * 4 + g], sink)
    m = jnp.maximum(jnp.max(s, axis=-1, keepdims=True), sink)
    p = jnp.exp(s - m)
    ps = jnp.exp(sink - m)
    inv = 1.0 / (jnp.sum(p, axis=-1, keepdims=True) + ps)
    return qs, p * inv, ps * inv, slot, rowg


def _swa_qtab(tk_ref, r0):
    return [jnp.concatenate([tk_ref[i, pl.ds(r0, SWA_BLK), :]] * 4, axis=1) for i in range(3)]


def _swa_fwd(h, tk, sink):
    def body(sink_ref, q_ref, kv_ref, tk_ref, y_ref, kpad, vpad):
        n = pl.program_id(1)

        @pl.when(n == 0)
        def _():
            _swa_pad_kv(kv_ref, tk_ref, kpad, vpad)

        r0 = pl.multiple_of(n * SWA_BLK, SWA_BLK)
        q = _rope(q_ref[...], _swa_qtab(tk_ref, r0))
        kb = kpad[pl.ds(r0, 3 * SWA_BLK), :]
        vb = vpad[pl.ds(r0, 3 * SWA_BLK), :]
        for hk in range(2):
            _, p, _, slot, _ = _swa_probs(q[:, hk * 256:(hk + 1) * 256], _swa_expand(kb, hk), n, sink_ref, hk)
            o4 = _mm(p, _swa_expand(vb, hk))
            o = jnp.zeros((SWA_BLK, 256), F32)
            for g in range(4):
                o = o + jnp.where(slot == g, o4[g * SWA_BLK:(g + 1) * SWA_BLK], 0.0)
            y_ref[:, hk * 256:(hk + 1) * 256] = o

    return pl.pallas_call(
        body,
        grid_spec=pltpu.PrefetchScalarGridSpec(
            num_scalar_prefetch=1, grid=(NSEQ, NBLK),
            in_specs=[pl.BlockSpec((SWA_BLK, 512), lambda s, n, sk: (s * NBLK + n, 2)),
                      pl.BlockSpec((L, 256), lambda s, n, sk: (s, 6)),
                      pl.BlockSpec((3, L, 128), lambda s, n, sk: (0, 0, 0))],
            out_specs=pl.BlockSpec((SWA_BLK, 512), lambda s, n, sk: (s * NBLK + n, 0)),
            scratch_shapes=[pltpu.VMEM((L + 2 * SWA_BLK, 128), F32), pltpu.VMEM((L + 2 * SWA_BLK, 128), F32)]),
        out_shape=_sds((N, 512)), name="swa_fwd", compiler_params=_cp(("arbitrary", "arbitrary")))(sink, h, h, tk)


def _swa_bwd(h, tk, sink, dyc):
    def body(sink_ref, q_ref, kv_ref, tk_ref, dy_ref, dq_ref, dkv_ref, dsink_ref, kpad, vpad, dkacc, dvacc):
        sq = pl.program_id(0)
        n = pl.program_id(1)

        @pl.when(n == 0)
        def _():
            _swa_pad_kv(kv_ref, tk_ref, kpad, vpad)
            dkacc[...] = jnp.zeros_like(dkacc)
            dvacc[...] = jnp.zeros_like(dvacc)

        @pl.when((n == 0) & (sq == 0))
        def _():
            dsink_ref[...] = jnp.zeros_like(dsink_ref)

        r0 = pl.multiple_of(n * SWA_BLK, SWA_BLK)
        tq = _swa_qtab(tk_ref, r0)
        q = _rope(q_ref[...], tq)
        kb = kpad[pl.ds(r0, 3 * SWA_BLK), :]
        vb = vpad[pl.ds(r0, 3 * SWA_BLK), :]
        dk = jnp.zeros((3 * SWA_BLK, 128), F32)
        dv = jnp.zeros((3 * SWA_BLK, 128), F32)
        hrow = lax.broadcasted_iota(jnp.int32, (8, 128), 0)
        dsk = jnp.zeros((8, 128), F32)
        for hk in range(2):
            kexp = _swa_expand(kb, hk)
            vexp = _swa_expand(vb, hk)
            qs, p, ps, slot, rowg = _swa_probs(q[:, hk * 256:(hk + 1) * 256], kexp, n, sink_ref, hk)
            dy2 = dy_ref[:, hk * 256:(hk + 1) * 256]
            dos = jnp.concatenate([jnp.where(slot == g, dy2, 0.0) for g in range(4)], axis=0)
            dp = _mm_nt(dos, vexp)
            delta = jnp.sum(p * dp, axis=-1, keepdims=True)
            ds = p * (dp - delta) * 0.125
            dsr = -ps * delta
            for g in range(4):
                dsk = dsk + jnp.where(hrow == hk * 4 + g, jnp.sum(jnp.where(rowg == g, dsr, 0.0), axis=0, keepdims=True), 0.0)
            dq4 = _mm(ds, kexp)
            dq2 = jnp.zeros((SWA_BLK, 256), F32)
            for g in range(4):
                dq2 = dq2 + jnp.where(slot == g, dq4[g * SWA_BLK:(g + 1) * SWA_BLK], 0.0)
            dq_ref[:, hk * 256:(hk + 1) * 256] = dq2
            dk = dk + _swa_fold(_mm_tn(ds, qs), hk)
            dv = dv + _swa_fold(_mm_tn(p, dos), hk)
        dq_ref[...] = _rope_t(dq_ref[...], tq)
        dkacc[pl.ds(r0, 3 * SWA_BLK), :] += dk
        dvacc[pl.ds(r0, 3 * SWA_BLK), :] += dv
        dsink_ref[...] += dsk

        @pl.when(n == NBLK - 1)
        def _():
            dkv_ref[:, 0:128] = _rope_t(dkacc[SWA_BLK:SWA_BLK + L], tk_ref[...])
            dkv_ref[:, 128:256] = dvacc[SWA_BLK:SWA_BLK + L]

    blk = lambda col: pl.BlockSpec((SWA_BLK, 512), lambda s, n, sk: (s * NBLK + n, col))
    pad = pltpu.VMEM((L + 2 * SWA_BLK, 128), F32)
    return pl.pallas_call(
        body,
        grid_spec=pltpu.PrefetchScalarGridSpec(
            num_scalar_prefetch=1, grid=(NSEQ, NBLK),
            in_specs=[blk(2), pl.BlockSpec((L, 256), lambda s, n, sk: (s, 6)),
                      pl.BlockSpec((3, L, 128), lambda s, n, sk: (0, 0, 0)), blk(0)],
            out_specs=[blk(0), pl.BlockSpec((L, 256), lambda s, n, sk: (s, 0)),
                       pl.BlockSpec((8, 128), lambda s, n, sk: (0, 0))],
            scratch_shapes=[pad, pad, pad, pad]),
        out_shape=[_sds((N, 512)), _sds((N, 256)), _sds((8, 128))],
        name="swa_bwd", compiler_params=_cp(("arbitrary", "arbitrary")))(sink, h, h, tk, dyc)


def _outproj_fwd(ya, yb, yc, x, wo, g, b):
    tm = 512

    def body(ya_ref, yb_ref, yc_ref, x_ref, wo_ref, g_ref, b_ref, s_ref, x1_ref):
        mix = _mm(ya_ref[...], wo_ref[0:256]) + _mm(yb_ref[...], wo_ref[256:512]) + _mm(yc_ref[...], wo_ref[512:1024])
        s = ALPHA * x_ref[...] + mix
        s_ref[...] = s
        x1_ref[...] = _ln_fwd(s, g_ref[...], b_ref[...])

    row = lambda w_: pl.BlockSpec((tm, w_), lambda i: (i, 0))
    one = pl.BlockSpec((1, D), lambda i: (0, 0))
    return pl.pallas_call(
        body, grid=(N // tm,),
        in_specs=[row(256), row(256), row(512), row(D), pl.BlockSpec((D, D), lambda i: (0, 0)), one, one],
        out_specs=[row(D), row(D)], out_shape=[_sds((N, D)), _sds((N, D))],
        name="outproj_fwd", compiler_params=_cp(("parallel",)))(ya, yb, yc, x, wo, g, b)


def _outproj_bwd(dx1, s1, ya, yb, yc, wo, g):
    tm = 512
    nt = N // tm

    def body(dx1_ref, s_ref, ya_ref, yb_ref, yc_ref, wo_ref, g_ref,
             dya_ref, dyb_ref, dyc_ref, dxp_ref, dwo_ref, dg_ref, db_ref, acc):
        i = pl.program_id(0)

        @pl.when(i == 0)
        def _():
            acc[...] = jnp.zeros_like(acc)
            dg_ref[...] = jnp.zeros_like(dg_ref)
            db_ref[...] = jnp.zeros_like(db_ref)

        ds, dg, db = _ln_bwd(dx1_ref[...], s_ref[...], g_ref[...])
        dg_ref[...] += dg
        db_ref[...] += db
        dxp_ref[...] = ALPHA * ds
        dy = _mm_nt(ds, wo_ref[...])
        dya_ref[...] = dy[:, 0:256]
        dyb_ref[...] = dy[:, 256:512]
        dyc_ref[...] = dy[:, 512:1024]
        acc[0:256] += _mm_tn(ya_ref[...], ds)
        acc[256:512] += _mm_tn(yb_ref[...], ds)
        acc[512:1024] += _mm_tn(yc_ref[...], ds)

        @pl.when(i == nt - 1)
        def _():
            dwo_ref[...] = acc[...].astype(MX)

    row = lambda w_: pl.BlockSpec((tm, w_), lambda i: (i, 0))
    one = pl.BlockSpec((1, D), lambda i: (0, 0))
    full = pl.BlockSpec((D, D), lambda i: (0, 0))
    return pl.pallas_call(
        body, grid=(nt,),
        in_specs=[row(D), row(D), row(256), row(256), row(512), full, one],
        out_specs=[row(256), row(256), row(512), row(D), full, one, one],
        out_shape=[_sds((N, 256)), _sds((N, 256)), _sds((N, 512)), _sds((N, D)), _sds((D, D), MX), _sds((1, D)), _sds((1, D))],
        scratch_shapes=[pltpu.VMEM((D, D), F32)],
        name="outproj_bwd", compiler_params=_cp(("arbitrary",)))(dx1, s1, ya, yb, yc, wo, g)


def _ffn_fwd(x1, w1, w2, g, b):
    tm = FFN_TM

    def body(x_ref, w1_ref, w2_ref, g_ref, b_ref, a_ref, s_ref, x2_ref):
        j = pl.program_id(1)

        @pl.when(j == 0)
        def _():
            s_ref[...] = ALPHA * x_ref[...]

        a = _mm(x_ref[...], w1_ref[0])
        a_ref[...] = a.astype(MX)
        hid = jnp.square(jnp.maximum(a, 0.0))
        s_ref[...] += _mm(hid, w2_ref[0])

        @pl.when(j == NSHARD - 1)
        def _():
            x2_ref[...] = _ln_fwd(s_ref[...], g_ref[...], b_ref[...])

    row = pl.BlockSpec((tm, D), lambda i, j: (i, 0))
    wj = pl.BlockSpec((1, D, D), lambda i, j: (j, 0, 0))
    one = pl.BlockSpec((1, D), lambda i, j: (0, 0))
    return pl.pallas_call(
        body, grid=(N // tm, NSHARD),
        in_specs=[row, wj, wj, one, one],
        out_specs=[pl.BlockSpec((tm, D), lambda i, j: (i, j)), row, row],
        out_shape=[_sds((N, DFF), MX), _sds((N, D)), _sds((N, D))],
        name="ffn_fwd", compiler_params=_cp(("parallel", "arbitrary"), FFN_VMEM))(x1, w1, w2, g, b)


def _ffn_bwd_act(dy, s2, a, w1, w2, g):
    tm = FFN_TM

    def body(dy_ref, s_ref, a_ref, w1_ref, w2_ref, g_ref, da_ref, ds_ref, dx1_ref, dg_ref, db_ref):
        i = pl.program_id(0)
        j = pl.program_id(1)

        @pl.when((i == 0) & (j == 0))
        def _():
            dg_ref[...] = jnp.zeros_like(dg_ref)
            db_ref[...] = jnp.zeros_like(db_ref)

        @pl.when(j == 0)
        def _():
            ds, dg, db = _ln_bwd(dy_ref[...], s_ref[...], g_ref[...])
            ds_ref[...] = ds.astype(MX)
            dg_ref[...] += dg
            db_ref[...] += db
            dx1_ref[...] = ALPHA * ds

        dhid = _mm_nt(ds_ref[...], w2_ref[0])
        da = dhid * 2.0 * jnp.maximum(a_ref[...].astype(F32), 0.0)
        da_ref[...] = da.astype(MX)
        dx1_ref[...] += _mm_nt(da, w1_ref[0])

    row = pl.BlockSpec((tm, D), lambda i, j: (i, 0))
    col = pl.BlockSpec((tm, D), lambda i, j: (i, j))
    wj = pl.BlockSpec((1, D, D), lambda i, j: (j, 0, 0))
    one = pl.BlockSpec((1, D), lambda i, j: (0, 0))
    return pl.pallas_call(
        body, grid=(N // tm, NSHARD),
        in_specs=[row, row, col, wj, wj, one],
        out_specs=[col, row, row, one, one],
        out_shape=[_sds((N, DFF), MX), _sds((N, D), MX), _sds((N, D)), _sds((1, D)), _sds((1, D))],
        name="ffn_bwd_act", compiler_params=_cp(("arbitrary", "arbitrary"), FFN_VMEM))(dy, s2, a, w1, w2, g)


def _ffn_bwd_w(x1, da, a, ds):
    tm = FFN_TM_W
    nt = N // tm

    def body(x_ref, da_ref, a_ref, ds_ref, dw1_ref, dw2_ref, acc1, acc2):
        i = pl.program_id(1)

        @pl.when(i == 0)
        def _():
            acc1[...] = jnp.zeros_like(acc1)
            acc2[...] = jnp.zeros_like(acc2)

        acc1[...] += _mm_tn(x_ref[...], da_ref[...])
        hid = jnp.square(jnp.maximum(a_ref[...].astype(F32), 0.0))
        acc2[...] += _mm_tn(hid, ds_ref[...])

        @pl.when(i == nt - 1)
        def _():
            dw1_ref[0] = acc1[...].astype(MX)
            dw2_ref[0] = acc2[...].astype(MX)

    row = pl.BlockSpec((tm, D), lambda j, i: (i, 0))
    col = pl.BlockSpec((tm, D), lambda j, i: (i, j))
    wj = pl.BlockSpec((1, D, D), lambda j, i: (j, 0, 0))
    return pl.pallas_call(
        body, grid=(NSHARD, nt),
        in_specs=[row, col, col, row], out_specs=[wj, wj],
        out_shape=[_sds((NSHARD, D, D), MX), _sds((NSHARD, D, D), MX)],
        scratch_shapes=[pltpu.VMEM((D, D), F32), pltpu.VMEM((D, D), F32)],
        name="ffn_bwd_w", compiler_params=_cp(("parallel", "arbitrary")))(x1, da, a, ds)


def _loss_head(y, target):
    tm = 512

    def body(y_ref, t_ref, dy_ref, l_ref):
        @pl.when(pl.program_id(0) == 0)
        def _():
            l_ref[...] = jnp.zeros_like(l_ref)

        e = y_ref[...] - t_ref[...]
        dy_ref[...] = e * (1.0 / D)
        l_ref[...] += jnp.sum(jnp.sum(e * e, axis=1, keepdims=True), axis=0, keepdims=True) * (0.5 / D)

    row = pl.BlockSpec((tm, D), lambda i: (i, 0))
    return pl.pallas_call(
        body, grid=(N // tm,), in_specs=[row, row],
        out_specs=[row, pl.BlockSpec((8, 128), lambda i: (0, 0))],
        out_shape=[_sds((N, D)), _sds((8, 128))], name="loss_head", compiler_params=_cp(("arbitrary",)))(y, target)


def _s5_discretize(a_re, a_im, log_step, b_re, b_im):
    lam = lax.complex(a_re, a_im)
    lam_bar = jnp.exp(lam * jnp.exp(log_step))
    b_bar = ((lam_bar - 1.0) / lam)[..., None] * lax.complex(b_re, b_im)
    return jnp.real(lam_bar), jnp.imag(lam_bar), jnp.real(b_bar), jnp.imag(b_bar)


def _s5_in_blocks(b):
    e = jnp.eye(8, dtype=F32)
    return jnp.einsum('ij,zbjph->zbihjp', e, b.reshape(2, 2, 8, S5_P, S5_H)).reshape(2, 2, 128, SW)


def _s5_in_unblocks(d):
    return jnp.einsum('zbihip->zbiph', d.reshape(2, 2, 8, S5_H, 8, S5_P)).reshape(2, S5_G, S5_P, S5_H)


def _s5_out_blocks(c):
    e = jnp.eye(8, dtype=F32)
    return jnp.einsum('ij,zbjhp->zbjpih', e, c.reshape(2, 2, 8, S5_H, S5_P)).reshape(2, 2, SW, 128)


def _s5_out_unblocks(d):
    return jnp.einsum('zbipih->zbihp', d.reshape(2, 2, 8, S5_P, 8, S5_H)).reshape(2, S5_G, S5_H, S5_P)


def _gate_weight(w_a):
    z = jnp.zeros((16, 128), F32)
    top = jnp.concatenate([w_a[0], z], axis=1)
    bot = jnp.concatenate([z, w_a[1]], axis=1)
    return jnp.concatenate([top, bot, jnp.zeros((96, 256), F32)], axis=0)


def _layer_prep(p):
    lr, li, br, bi = _s5_discretize(p["s5_a_re"], p["s5_a_im"], p["s5_log_step"], p["s5_b_re"], p["s5_b_im"])
    q = dict(p)
    q["bre"] = _s5_in_blocks(br).astype(MX)
    q["bim"] = _s5_in_blocks(bi).astype(MX)
    q["cre"] = _s5_out_blocks(p["s5_c_re"]).astype(MX)
    q["cim"] = _s5_out_blocks(p["s5_c_im"]).astype(MX)
    mr, mi = lr.reshape(2, 1024), li.reshape(2, 1024)
    both = lambda t0, t1: tuple(jnp.stack(p) for p in zip(t0, t1))
    q["tab"] = both(_lockstep_tables(mr[0], mi[0], False), _lockstep_tables(mr[1], mi[1], True))
    q["tabc"] = both(_lockstep_tables(mr[0], -mi[0], True), _lockstep_tables(mr[1], -mi[1], False))
    q["dsk"] = p["s5_d"].reshape(1, 256)
    q["wa"] = _gate_weight(p["gla_w_a"]).astype(MX)
    q["ba"] = p["gla_b_a"].reshape(1, 256)
    q["lng"] = p["gla_ln_g"].reshape(1, 256)
    q["bv"] = p["s5_b_glu"][:256].reshape(1, 256)
    q["bg"] = p["s5_b_glu"][256:].reshape(1, 256)
    for k in ("ln1_g", "ln1_b", "ln2_g", "ln2_b"):
        q[k] = p[k].reshape(1, D)
    return q


def _layer_fwd(x, q, tk, fetch):
    q["w_in"] = fetch("w_in", x)
    h = _inproj_fwd(x, q["w_in"])
    hre, him, y2 = _s5_fwd(h, q["bre"], q["bim"], q["cre"], q["cim"], q["tab"])
    q["w4"] = fetch("s5_w_glu", y2)
    ya = _s5_glu_fwd(y2, h, q["dsk"], q["w4"], q["bv"], q["bg"])
    la2 = _gla_gate_fwd(h, q["wa"], q["ba"])
    of, ob, sf, sb = _gla_fwd(h, la2)
    yb = _gla_post_fwd(of, ob, h, q["lng"])
    yc = _swa_fwd(h, tk, q["swa_sink"])
    q["w_out"] = fetch("w_out", yc)
    s1, x1 = _outproj_fwd(ya, yb, yc, x, q["w_out"], q["ln1_g"], q["ln1_b"])
    q["w_ff1"] = fetch("w_ff1", x1)
    q["w_ff2"] = fetch("w_ff2", x1)
    a, s2, x2 = _ffn_fwd(x1, q["w_ff1"], q["w_ff2"], q["ln2_g"], q["ln2_b"])
    saved = dict(x=x, h=h, hre=hre, him=him, y2=y2, ya=ya, la2=la2, of=of, ob=ob, sf=sf, sb=sb, yb=yb, yc=yc,
                 s1=s1, x1=x1, a=a, s2=s2)
    return x2, saved


def _layer_bwd(dy, q, sv, tk, emit):
    g = {}
    da, ds2, dx1, g["dg2"], g["db2"] = _ffn_bwd_act(dy, sv["s2"], sv["a"], q["w_ff1"], q["w_ff2"], q["ln2_g"])
    dw1, dw2 = _ffn_bwd_w(sv["x1"], da, sv["a"], ds2)
    tie = emit(dict(w_ff1=dw1, w_ff2=dw2))
    dya, dyb, dyc, dxp, dwo, g["dg1"], g["db1"] = _outproj_bwd(dx1, sv["s1"], sv["ya"], sv["yb"], sv["yc"],
                                                               q["w_out"], q["ln1_g"] + tie)
    h = sv["h"]
    daq, dakv, g["dsink"] = _swa_bwd(h, tk, q["swa_sink"], dyc)
    do, gr, g["dlng"] = _gla_post_bwd(sv["of"], sv["ob"], h, q["lng"], dyb)
    gq_f, gk_f, gv_f, gl_f, gq_b, gk_b, gv_b, gl_b = _gla_bwd(h, sv["la2"], do, sv["sf"], sv["sb"])
    dhl, g["dwa"], g["dba"] = _gla_gate_bwd(h, q["wa"], q["ba"], gl_f, gl_b)
    dyp, dud, g["dd"], dw4, g["dbv"], g["dbg"] = _s5_glu_bwd(sv["y2"], h, q["dsk"], q["w4"], q["bv"], q["bg"], dya)
    tie = emit(dict(w_out=dwo.reshape(NSHARD, D // NSHARD, D), s5_w_glu=dw4))
    du2, g["dbre"], g["dbim"], g["dcre"], g["dcim"], g["dmu"] = _s5_bwd(
        h, dyp, sv["hre"], sv["him"], q["bre"], q["bim"], q["cre"], q["cim"], (q["tabc"][0], q["tabc"][1] + tie))
    dx, dwt = _inproj_bwd(sv["x"], q["w_in"], dxp, du2, dud, gq_f, gq_b, gk_f, gk_b, gv_f, gv_b, gr, daq, dakv, dhl)
    tie = emit(dict(w_in=_in_rows_back(dwt)))
    return dx, g, tie


NATIVE = ("dmu", "dbre", "dbim", "dcre", "dcim", "dd", "dbv", "dbg", "dwa", "dba", "dlng", "dsink",
          "dg1", "db1", "dg2", "db2")
ICI_CORE = (0, 0, 0, 1, 1, 0, 0, 0, 1, 1, 1, 1, 0, 0, 1, 1)


def _finish_small(n, w):
    g = {}
    dmu = n["dmu"]
    dlr = dmu[:, :, :, 0].reshape(DEPTH, 2, S5_G, S5_P)
    dli = dmu[:, :, :, 1].reshape(DEPTH, 2, S5_G, S5_P)

    def unblock(c, perm, shape):
        return c.reshape(DEPTH, 2, 2, S5_H, 8, S5_P).transpose(perm).reshape(shape)

    b_shape, c_shape = (DEPTH, 2, S5_G, S5_P, S5_H), (DEPTH, 2, S5_G, S5_H, S5_P)
    _, vjp = jax.vjp(_s5_discretize, w["s5_a_re"], w["s5_a_im"], w["s5_log_step"], w["s5_b_re"], w["s5_b_im"])
    (g["s5_a_re"], g["s5_a_im"], g["s5_log_step"], g["s5_b_re"], g["s5_b_im"]) = vjp(
        (dlr, dli, unblock(n["dbre"], (0, 1, 2, 4, 5, 3), b_shape), unblock(n["dbim"], (0, 1, 2, 4, 5, 3), b_shape)))
    g["s5_c_re"] = unblock(n["dcre"], (0, 1, 2, 4, 3, 5), c_shape)
    g["s5_c_im"] = unblock(n["dcim"], (0, 1, 2, 4, 3, 5), c_shape)
    g["s5_d"] = n["dd"].reshape(DEPTH, S5_G, S5_H)
    g["s5_b_glu"] = jnp.concatenate([n["dbv"], n["dbg"]], axis=2).reshape(DEPTH, 512)
    g["gla_w_a"] = jnp.stack([n["dwa"][:, 0:16, 0:128], n["dwa"][:, 16:32, 128:256]], axis=1)
    g["gla_b_a"] = n["dba"].reshape(DEPTH, 2, 128)
    g["gla_ln_g"] = n["dlng"].reshape(DEPTH, 256)
    g["swa_sink"] = n["dsink"][:, :, 0]
    for k, s in (("ln1_g", "dg1"), ("ln1_b", "db1"), ("ln2_g", "dg2"), ("ln2_b", "db2")):
        g[k] = n[s].reshape(DEPTH, D)
    return g


def _local_step(x, target, qs, tk, fetch, emit):
    saved = []
    for l, q in enumerate(qs):
        x, sv = _layer_fwd(x, q, tk, functools.partial(fetch, l))
        saved.append(sv)
    dy, lacc = _loss_head(x, target)
    smalls = [None] * DEPTH
    tie = 0.0
    for l in reversed(range(DEPTH)):
        qs[l]["ln2_g"] = qs[l]["ln2_g"] + tie
        dy, smalls[l], tie = _layer_bwd(dy, qs[l], saved[l], tk, functools.partial(emit, l))
    smalls[0]["db2"] = smalls[0]["db2"] + tie
    return lacc[0, 0], dy, smalls


BIG = ("w_in", "s5_w_glu", "w_out", "w_ff1", "w_ff2")
SMALL = ("s5_a_re", "s5_a_im", "s5_log_step", "s5_b_re", "s5_b_im", "s5_c_re", "s5_c_im", "s5_d", "s5_b_glu",
         "gla_w_a", "gla_b_a", "gla_ln_g", "swa_sink", "ln1_g", "ln1_b", "ln2_g", "ln2_b")
ANY = pl.BlockSpec(memory_space=pl.ANY)


def _place():
    x, y, c = lax.axis_index("x"), lax.axis_index("y"), lax.axis_index("c")
    return x, y, c, [(1 - x, y), (x, 1 - y), (1 - x, 1 - y)]


HBM = pl.BlockSpec(memory_space=pltpu.HBM)
SEMS = pl.BlockSpec(memory_space=pltpu.SEMAPHORE)
EFFECT = pltpu.SideEffectType.DATAFLOW_SIDE_EFFECTING


def _push_copies(ins, lands, send, recv, gather, sending):
    x, y, c, chips = _place()
    me = 2 * x + y
    out = []
    for a in range(len(lands)):
        for j, (px, py) in enumerate(chips):
            peer = 2 * px + py
            src = lands[a].at[me] if gather else ins[a].at[peer if sending else me]
            dst = lands[a].at[me if sending else peer]
            out.append(pltpu.make_async_remote_copy(src_ref=src, dst_ref=dst, send_sem=send.at[3 * a + j],
                                                    recv_sem=recv.at[3 * a + j], device_id=(px, py, c),
                                                    device_id_type=MESH))
    return out


def _push_start(name, arrs, gather):
    n = len(arrs)
    ops = list(arrs) if gather else list(arrs) + [lax.empty(s.shape, s.dtype) for s in arrs]
    m = len(ops)

    def body(*refs):
        ins, lnd = (refs[:n], refs[:n]) if gather else (refs[:n], refs[n:m])
        for cp in _push_copies(ins, lnd, refs[m], refs[m + 1], gather, True):
            cp.start()
        refs[-1][...] = jnp.zeros((8, 128), F32)

    ops = [pltpu.with_memory_space_constraint(t, pltpu.HBM) for t in ops]
    res = pl.pallas_call(
        body, name=name,
        out_shape=(pltpu.SemaphoreType.DMA((3 * n,)), pltpu.SemaphoreType.DMA((3 * n,)),
                   *[pltpu.HBM(t.shape, t.dtype) for t in ops], _sds((8, 128))),
        in_specs=[HBM] * m,
        out_specs=(SEMS, SEMS, *[HBM] * m, pl.BlockSpec(memory_space=pltpu.VMEM)),
        input_output_aliases={i: 2 + i for i in range(m)},
        compiler_params=pltpu.CompilerParams(has_side_effects=EFFECT))(*ops)
    return res[0], res[1], list(res[2:2 + m]), res[-1]


def _push_wait(name, started, after, gather):
    send, recv, ops, _ = started
    m = len(ops)
    n = m if gather else m // 2

    def body(*refs):
        ins, lnd = (refs[:n], refs[:n]) if gather else (refs[:n], refs[n:m])
        for cp in _push_copies(ins, lnd, refs[m], refs[m + 1], gather, False):
            cp.wait_send()
            cp.wait_recv()

    res = pl.pallas_call(
        body, name=name,
        out_shape=[pltpu.HBM(t.shape, t.dtype) for t in ops],
        in_specs=[HBM] * m + [SEMS, SEMS, ANY], out_specs=[HBM] * m,
        input_output_aliases={i: i for i in range(m)},
        compiler_params=pltpu.CompilerParams(has_side_effects=EFFECT))(*ops, send, recv, after)
    return list(res)


def _row_tile(rows):
    return max(t for t in range(8, min(rows, 512) + 1, 8) if rows % t == 0)


def _cast_to_slot(me, w, l):
    _, rows, cols = w.shape
    tr = _row_tile(rows)

    def body(me_ref, w_ref, o_ref):
        o_ref[0] = w_ref[0].astype(MX)

    return pl.pallas_call(
        body,
        grid_spec=pltpu.PrefetchScalarGridSpec(
            num_scalar_prefetch=1, grid=(rows // tr,),
            in_specs=[pl.BlockSpec((1, tr, cols), lambda i, me_: (l, i, 0))],
            out_specs=pl.BlockSpec((1, tr, cols), lambda i, me_: (me_[0], i, 0))),
        out_shape=_sds((NSHARD, rows, cols), MX), name="cast_to_slot", compiler_params=_cp(("arbitrary",)))(me, w)


def _sum_sources(me, recv, own):
    _, rows, cols = recv[0].shape
    tr = min(_row_tile(rows), 256) if rows % 256 == 0 else _row_tile(rows)
    nt = rows // tr

    def body(me_ref, *refs):
        o_ref = refs[-1]
        for l in range(DEPTH):
            @pl.when(pl.program_id(0) == l)
            def _():
                r_ref, own_ref = refs[2 * l], refs[2 * l + 1]
                part = [jnp.where(me_ref[0] == s, own_ref[0], r_ref[s]).astype(F32) for s in range(NSHARD)]
                o_ref[...] = ((part[0] + part[1]) + part[2]) + part[3]

    in_specs = []
    for l in range(DEPTH):
        pick = lambda g, i, me_, l=l: jnp.where(g == l, i, jnp.where(g < l, 0, nt - 1))
        in_specs += [pl.BlockSpec((NSHARD, tr, cols), lambda g, i, me_, pick=pick: (0, pick(g, i, me_), 0)),
                     pl.BlockSpec((1, tr, cols), lambda g, i, me_, pick=pick: (me_[0], pick(g, i, me_), 0))]
    return pl.pallas_call(
        body,
        grid_spec=pltpu.PrefetchScalarGridSpec(
            num_scalar_prefetch=1, grid=(DEPTH, nt), in_specs=in_specs,
            out_specs=pl.BlockSpec((tr, cols), lambda g, i, me_: (g * nt + i, 0))),
        out_shape=_sds((DEPTH * rows, cols)), name="sum_sources",
        compiler_params=_cp(("arbitrary", "arbitrary")))(me, *[t for l in range(DEPTH) for t in (recv[l], own[l])])


def _swap_sibling(arrs):
    n = len(arrs)

    def body(*refs):
        ins, outs = refs[:n], refs[n:2 * n]
        send, recv = refs[2 * n:]
        x, y, c, _ = _place()
        cps = [pltpu.make_async_remote_copy(src_ref=ins[a], dst_ref=outs[a], send_sem=send.at[a], recv_sem=recv.at[a],
                                            device_id=(x, y, 1 - c), device_id_type=MESH) for a in range(n)]
        for cp in cps:
            cp.start()
        for cp in cps:
            cp.wait()

    return pl.pallas_call(
        body, in_specs=[ANY] * n, out_specs=[ANY] * n, out_shape=[_sds(a.shape, a.dtype) for a in arrs],
        scratch_shapes=[pltpu.SemaphoreType.DMA((n,)), pltpu.SemaphoreType.DMA((n,))],
        name="swap_sibling")(*arrs)


def _allreduce_small(per_layer):
    nk = len(per_layer[0])
    n = DEPTH * nk
    shapes = [a.shape for a in per_layer[0]]

    def body(*refs):
        ins, outs = refs[:n], refs[n:n + nk]
        sibs, slots = refs[n + nk:n + 2 * nk], refs[n + 2 * nk:n + 3 * nk]
        send, recv = refs[n + 3 * nk:]
        x, y, c, chips = _place()
        me = 2 * x + y
        d2d = [pltpu.make_async_remote_copy(src_ref=ins[l * nk + k], dst_ref=sibs[k].at[l], send_sem=send.at[l * nk + k],
                                            recv_sem=recv.at[l * nk + k], device_id=(x, y, 1 - c), device_id_type=MESH)
               for l in range(DEPTH) for k in range(nk)]
        for cp in d2d:
            cp.start()
        for cp in d2d:
            cp.wait()
        for l in range(DEPTH):
            for k in range(nk):
                slots[k][me, l] = ins[l * nk + k][...] + sibs[k][l]

        def remote(k, j, slot):
            px, py = chips[j]
            return pltpu.make_async_remote_copy(src_ref=slots[k].at[me], dst_ref=slots[k].at[slot],
                                                send_sem=send.at[n + 3 * k + j], recv_sem=recv.at[n + 3 * k + j],
                                                device_id=(px, py, c), device_id_type=MESH)

        def handover(k):
            return pltpu.make_async_remote_copy(src_ref=outs[k], dst_ref=outs[k], send_sem=send.at[n + 3 * nk + k],
                                                recv_sem=recv.at[n + 3 * nk + k], device_id=(x, y, 1 - c),
                                                device_id_type=MESH)

        halves = (tuple(k for k in range(nk) if ICI_CORE[k] == 0), tuple(k for k in range(nk) if ICI_CORE[k] == 1))
        for cc in range(2):
            @pl.when(c == cc)
            def _():
                mine, theirs = halves[cc], halves[1 - cc]
                sends = [remote(k, j, me) for k in mine for j in range(3)]
                for cp in sends:
                    cp.start()
                for k in mine:
                    for j in range(3):
                        remote(k, j, 2 * chips[j][0] + chips[j][1]).wait_recv()
                for cp in sends:
                    cp.wait_send()
                for k in mine:
                    outs[k][...] = ((slots[k][0] + slots[k][1]) + slots[k][2]) + slots[k][3]
                over = [handover(k) for k in mine]
                for cp in over:
                    cp.start()
                for k in theirs:
                    handover(k).wait_recv()
                for cp in over:
                    cp.wait_send()

    vm = pl.BlockSpec(memory_space=pltpu.VMEM)
    return pl.pallas_call(
        body, in_specs=[vm] * n, out_specs=[vm] * nk, out_shape=[_sds((DEPTH,) + s) for s in shapes],
        scratch_shapes=([pltpu.VMEM((DEPTH,) + s, F32) for s in shapes]
                        + [pltpu.VMEM((NSHARD, DEPTH) + s, F32) for s in shapes]
                        + [pltpu.SemaphoreType.DMA((n + 4 * nk,)), pltpu.SemaphoreType.DMA((n + 4 * nk,))]),
        name="allreduce_small", compiler_params=pltpu.CompilerParams(vmem_limit_bytes=VMEM_LIMIT))(
            *[a for layer in per_layer for a in layer])


def _adamw_math(w, g, m, v):
    m = ADAM_B1 * m + (1.0 - ADAM_B1) * g
    v = ADAM_B2 * v + (1.0 - ADAM_B2) * jnp.square(g)
    m_hat = m / (1.0 - ADAM_B1 ** ADAM_STEP)
    v_hat = v / (1.0 - ADAM_B2 ** ADAM_STEP)
    delta = -ADAM_LR * (m_hat / (jnp.sqrt(v_hat) + ADAM_EPS) + ADAM_WD * w)
    return delta, m, v


def _adamw(g_parts, w, m, v):
    rows, cols = w.shape
    tr = 256 if rows % 256 == 0 else _row_tile(rows)
    k = len(g_parts)

    def body(*refs):
        g = refs[0][...]
        for r in refs[1:k]:
            g = g + r[...]
        w_ref, m_ref, v_ref, go, do, mo, vo = refs[k:]
        d, mn, vn = _adamw_math(w_ref[...], g, m_ref[...], v_ref[...])
        go[...] = g
        do[...] = d
        mo[...] = mn
        vo[...] = vn

    spec = pl.BlockSpec((tr, cols), lambda i: (i, 0))
    return pl.pallas_call(
        body, grid=(rows // tr,), in_specs=[spec] * (k + 3), out_specs=[spec] * 4,
        out_shape=[_sds((rows, cols))] * 4, name="adamw", compiler_params=_cp(("parallel",)))(*g_parts, w, m, v)


def _adamw_small(gs, ws, ms, vs):
    n = len(gs)

    def body(*refs):
        for k in range(n):
            d, mn, vn = _adamw_math(refs[n + k][...], refs[k][...], refs[2 * n + k][...], refs[3 * n + k][...])
            refs[4 * n + k][...] = d
            refs[5 * n + k][...] = mn
            refs[6 * n + k][...] = vn

    vm = pl.BlockSpec(memory_space=pltpu.VMEM)
    shapes = [_sds(a.shape) for a in ws]
    res = pl.pallas_call(
        body, in_specs=[vm] * (4 * n), out_specs=[vm] * (3 * n), out_shape=shapes * 3, name="adamw_small",
        compiler_params=pltpu.CompilerParams(vmem_limit_bytes=VMEM_LIMIT))(*gs, *ws, *ms, *vs)
    return res[:n], res[n:2 * n], res[2 * n:]


_ARGS = ("x", "w_in", "s5_a_re", "s5_a_im", "s5_log_step", "s5_b_re", "s5_b_im", "s5_c_re", "s5_c_im", "s5_d",
         "s5_w_glu", "s5_b_glu", "gla_w_a", "gla_b_a", "gla_ln_g", "swa_sink", "w_out", "ln1_g", "ln1_b", "w_ff1",
         "w_ff2", "ln2_g", "ln2_b")
_WEIGHTS = _ARGS[1:]


def _in_rows(g4):
    t = g4.reshape(DIN, D)
    return jnp.concatenate([t[0:1024], t[1056:DIN], t[1024:1056], jnp.zeros((DINP - DIN, D), t.dtype)], axis=0)


def _in_rows_back(d):
    return jnp.concatenate([d[0:1024], d[1792:1824], d[1024:1792]], axis=0).reshape(NSHARD, DIN // NSHARD, D).astype(MX)


def _shard_cols(d):
    return d.reshape(d.shape[0], NSHARD, d.shape[1] // NSHARD).transpose(1, 0, 2)


def kernel(x, w_in, s5_a_re, s5_a_im, s5_log_step, s5_b_re, s5_b_im, s5_c_re, s5_c_im, s5_d, s5_w_glu, s5_b_glu, gla_w_a, gla_b_a, gla_ln_g, swa_sink, w_out, ln1_g, ln1_b, w_ff1, w_ff2, ln2_g, ln2_b, loss_target, m_w_in, m_s5_a_re, m_s5_a_im, m_s5_log_step, m_s5_b_re, m_s5_b_im, m_s5_c_re, m_s5_c_im, m_s5_d, m_s5_w_glu, m_s5_b_glu, m_gla_w_a, m_gla_b_a, m_gla_ln_g, m_swa_sink, m_w_out, m_ln1_g, m_ln1_b, m_w_ff1, m_w_ff2, m_ln2_g, m_ln2_b, v_w_in, v_s5_a_re, v_s5_a_im, v_s5_log_step, v_s5_b_re, v_s5_b_im, v_s5_c_re, v_s5_c_im, v_s5_d, v_s5_w_glu, v_s5_b_glu, v_gla_w_a, v_gla_b_a, v_gla_ln_g, v_swa_sink, v_w_out, v_ln1_g, v_ln1_b, v_w_ff1, v_w_ff2, v_ln2_g, v_ln2_b):
    given = dict(locals())
    w = {k: given[k] for k in _WEIGHTS}
    mom = {k: given["m_" + k] for k in _WEIGHTS}
    var = {k: given["v_" + k] for k in _WEIGHTS}

    me = (2 * lax.axis_index("x") + lax.axis_index("y")).astype(jnp.int32).reshape(1)
    tr = lambda t: t.transpose(0, 2, 1)
    shard = {k: (tr(w[k]) if k == "w_in" else w[k]) for k in BIG}
    qs = [_layer_prep({k: w[k][l] for k in SMALL}) for l in range(DEPTH)]

    first = ("w_in", "s5_w_glu", "w_out")
    follow = {(0, "w_in"): [(0, BIG[3:]), (1, first)], (0, "w_ff1"): [(1, BIG[3:])]}
    gathers = {}

    def start_gather(l, names, behind=None):
        lands = [_cast_to_slot(me, shard[k], l) for k in names]
        if behind is not None:
            lands, behind = lax.optimization_barrier((lands, behind))
        st = _push_start(f"gather_start_{l}_{names[0]}", lands, True)
        for k in names:
            gathers[l, k] = [names, st, None]
        return st[-1], behind

    token = start_gather(0, first[:1])[0] + start_gather(0, first[1:])[0]

    def fetch(l, name, after):
        names, st, got = gathers[l, name]
        tie = None
        if got is None:
            if l == 0 and name == "w_in":
                after = token
            lands = _push_wait(f"gather_wait_{l}_{names[0]}", st, after, True)
            for l2, names2 in follow.get((l, name), ()):
                tok, lands[0] = start_gather(l2, names2, lands[0])
                tie = tok[0, 0] if tie is None else tie + tok[0, 0]
            got = dict(zip(names, lands))
            for k in names:
                gathers[l, k][2] = got
        full = got[name]
        if name == "w_in":
            return _in_rows(full) if tie is None else _in_rows(full) + tie.astype(MX)
        if tie is not None:
            qs[l]["ln2_b"] = qs[l]["ln2_b"] + tie
        return full.reshape(D, D) if name == "w_out" else full

    scatters = []

    def emit(l, grads):
        names = tuple(grads)
        st = _push_start(f"scatter_start_{l}_{names[0]}", [grads[k] for k in names], False)
        scatters.append((l, names, st))
        return st[-1][0, 0]

    loss, dx, smalls = _local_step(x.reshape(N, D), loss_target.reshape(N, D), qs, _rope_tables(128), fetch, emit)
    loss = lax.psum(loss, ("x", "y", "c"))

    out = {}
    native = _allreduce_small([[smalls[l][k] for k in NATIVE] for l in range(DEPTH)])
    gsmall = _finish_small(dict(zip(NATIVE, native)), w)
    res = _adamw_small(*([t[k] for k in SMALL] for t in (gsmall, w, mom, var)))
    for i, k in enumerate(SMALL):
        out[k] = [gsmall[k], res[0][i], res[1][i], res[2][i]]

    recv, own = {}, {}

    def finish(keys, after):
        for l, names, st in scatters:
            if names[0] in keys:
                ops = _push_wait(f"scatter_wait_{l}_{names[0]}", st, after, False)
                for i, k in enumerate(names):
                    own[l, k], recv[l, k] = ops[i], ops[len(names) + i]
        sums = [_sum_sources(me, [recv[l, k] for l in range(DEPTH)], [own[l, k] for l in range(DEPTH)]) for k in keys]
        for k, mine, other in zip(keys, sums, _swap_sibling(sums)):
            shp = shard[k].shape
            r = _adamw([mine, other], *((tr(t[k]) if k == "w_in" else t[k]).reshape(-1, shp[-1]) for t in (w, mom, var)))
            r = [t.reshape(shp) for t in r]
            out[k] = [tr(t) for t in r] if k == "w_in" else r
        return out[keys[-1]][1]

    last = finish(("w_ff1", "w_ff2", "w_out", "s5_w_glu"), res[0][-1])
    finish(("w_in",), last)

    return (loss, dx.reshape(NSEQ, L, D), *[out[k][0] for k in _WEIGHTS], *[out[k][1] for k in _WEIGHTS],
            *[out[k][2] for k in _WEIGHTS], *[out[k][3] for k in _WEIGHTS])
```

```python
import functools
import math

import jax
import jax.numpy as jnp
from jax import lax
from jax.experimental import pallas as pl
from jax.experimental.pallas import tpu as pltpu

F32 = jnp.float32
MX = jnp.bfloat16
MESH = pl.DeviceIdType.MESH

DEPTH = 2
NSEQ = 2
L = 2048
N = NSEQ * L
D = 1024
DFF = 4096
NSHARD = 4
S5_G, S5_H, S5_P = 16, 16, 64
GLA_CHUNK = 64
NCHUNK = L // GLA_CHUNK
SWA_BLK = 128
NBLK = L // SWA_BLK
ROT = 16
ROPE_THETA = 500000.0
LN_EPS = 1e-5
ALPHA = (2 * DEPTH) ** 0.25
NEG_BIG = -1e30
DIN = 1824
DINP = 1920
ADAM_LR, ADAM_B1, ADAM_B2, ADAM_EPS, ADAM_WD, ADAM_STEP = 0.001, 0.9, 0.999, 1e-08, 0.01, 10
VMEM_LIMIT = 56 * 1024 * 1024
TT = 512
SW = 512
FFN_TM = 1024
FFN_TM_W = 1024
FFN_VMEM = 60 * 1024 * 1024


def _cp(sem, vmem=VMEM_LIMIT):
    return pltpu.CompilerParams(dimension_semantics=sem, vmem_limit_bytes=vmem)


def _mm(a, b):
    return jnp.dot(a.astype(MX), b.astype(MX), preferred_element_type=F32)


def _mm_nt(a, b):
    return lax.dot_general(a.astype(MX), b.astype(MX), (((1,), (1,)), ((), ())), preferred_element_type=F32)


def _mm_tn(a, b):
    return lax.dot_general(a.astype(MX), b.astype(MX), (((0,), (0,)), ((), ())), preferred_element_type=F32)


@jax.custom_vjp
def _dmm(a, b):
    return _mm(a, b)


_dmm.defvjp(lambda a, b: (_mm(a, b), (a, b)), lambda r, g: (_mm_nt(g, r[1]), _mm_tn(r[0], g)))


@jax.custom_vjp
def _dmm_nt(a, b):
    return _mm_nt(a, b)


_dmm_nt.defvjp(lambda a, b: (_mm_nt(a, b), (a, b)), lambda r, g: (_mm(g, r[1]), _mm_tn(g, r[0])))


@jax.custom_vjp
def _dmm_tn(a, b):
    return _mm_tn(a, b)


_dmm_tn.defvjp(lambda a, b: (_mm_tn(a, b), (a, b)), lambda r, g: (_mm_nt(r[1], g), _mm(r[0], g)))


def _split3(x):
    hi = x.astype(MX)
    r1 = x - hi.astype(F32)
    mid = r1.astype(MX)
    lo = (r1 - mid.astype(F32)).astype(MX)
    return hi, mid, lo


def _tri(rev):
    r = lax.broadcasted_iota(jnp.int32, (GLA_CHUNK, GLA_CHUNK), 0)
    c = lax.broadcasted_iota(jnp.int32, (GLA_CHUNK, GLA_CHUNK), 1)
    return jnp.where((c >= r) if rev else (c <= r), 1.0, 0.0).astype(MX)


def _cums_impl(x, rev):
    t = _tri(rev)
    return sum(jnp.dot(t, p, preferred_element_type=F32) for p in _split3(x))


@functools.partial(jax.custom_vjp, nondiff_argnums=(1,))
def _cums(x, rev):
    return _cums_impl(x, rev)


_cums.defvjp(lambda x, rev: (_cums_impl(x, rev), None), lambda rev, r, g: (_cums_impl(g, not rev),))


def _ln_fwd(s, g, b):
    mu = jnp.mean(s, axis=-1, keepdims=True)
    xc = s - mu
    var = jnp.mean(xc * xc, axis=-1, keepdims=True)
    return xc * lax.rsqrt(var + LN_EPS) * g + b


def _ln_bwd(dy, s, g):
    mu = jnp.mean(s, axis=-1, keepdims=True)
    xc = s - mu
    var = jnp.mean(xc * xc, axis=-1, keepdims=True)
    rstd = lax.rsqrt(var + LN_EPS)
    xhat = xc * rstd
    dxh = dy * g
    ds = rstd * (dxh - jnp.mean(dxh, axis=-1, keepdims=True) - xhat * jnp.mean(dxh * xhat, axis=-1, keepdims=True))
    return ds, jnp.sum(dy * xhat, axis=0, keepdims=True), jnp.sum(dy, axis=0, keepdims=True)


def _sds(shape, dtype=F32):
    return jax.ShapeDtypeStruct(shape, dtype)


_IN_ROW_PIECES = (((0, 0), (0, 456)), ((1, 0), (456, 456)), ((2, 0), (912, 112)), ((2, 112), (1792, 32)),
                  ((2, 144), (1024, 312)), ((3, 0), (1336, 456)))


def _in_rows(g4, behind):
    def body(g_ref, behind_ref, o_ref, tmp):
        tmp[DIN:DINP] = jnp.zeros((DINP - DIN, D), F32)
        for (j, s0), (d0, n_) in _IN_ROW_PIECES:
            tmp[d0:d0 + n_] = g_ref[j, s0:s0 + n_].astype(F32)
        o_ref[...] = tmp[...].astype(MX)

    vm = pl.BlockSpec(memory_space=pltpu.VMEM)
    return pl.pallas_call(body, in_specs=[vm, pl.BlockSpec(memory_space=pl.ANY)], out_specs=vm,
                          out_shape=_sds((DINP, D), MX), scratch_shapes=[pltpu.VMEM((DINP, D), F32)], name="in_rows",
                          compiler_params=pltpu.CompilerParams(vmem_limit_bytes=VMEM_LIMIT))(g4, behind)


def _inproj_fwd(x, wt):
    tm = 512

    def body(x_ref, w_ref, h_ref):
        h_ref[...] = _mm_nt(x_ref[...], w_ref[...])

    return pl.pallas_call(
        body, grid=(N // tm,),
        in_specs=[pl.BlockSpec((tm, D), lambda i: (i, 0)), pl.BlockSpec((DINP, D), lambda i: (0, 0))],
        out_specs=pl.BlockSpec((tm, DINP), lambda i: (i, 0)),
        out_shape=_sds((N, DINP)), name="inproj_fwd", compiler_params=_cp(("parallel",)))(x, wt)


def _inproj_bwd(x, w, dxp, du2, dud, gq_f, gq_b, gk_f, gk_b, gv_f, gv_b, gr, daq, dakv, dhl):
    tm = 256
    nt = N // tm

    def body(x_ref, w_ref, dxp_ref, du2_ref, dud_ref, gqf, gqb, gkf, gkb, gvf, gvb, gr_ref, daq_ref, dakv_ref, dhl_ref,
             dx_ref, dw_ref, acc):
        i = pl.program_id(0)
        dh = jnp.concatenate([
            du2_ref[0] + du2_ref[1] + dud_ref[...], gqf[...] + gqb[...], gkf[...] + gkb[...], gvf[...] + gvb[...],
            gr_ref[...], daq_ref[...], dakv_ref[...], dhl_ref[...]], axis=1)
        dx_ref[...] = dxp_ref[...] + _mm(dh, w_ref[...])
        contrib = _mm_tn(dh, x_ref[...])

        @pl.when(i == 0)
        def _():
            acc[...] = contrib

        @pl.when(i > 0)
        def _():
            acc[...] += contrib

        @pl.when(i == nt - 1)
        def _():
            for (j, d0), (s0, n_) in _IN_ROW_PIECES:
                dw_ref[j, d0:d0 + n_] = acc[s0:s0 + n_].astype(MX)

    row = lambda w_: pl.BlockSpec((tm, w_), lambda i: (i, 0))
    return pl.pallas_call(
        body, grid=(nt,),
        in_specs=[row(D), pl.BlockSpec((DINP, D), lambda i: (0, 0)), row(D),
                  pl.BlockSpec((2, tm, 256), lambda i: (0, i, 0)), row(256), row(128), row(128), row(128), row(128),
                  row(256), row(256), row(256), row(512), row(256), row(128)],
        out_specs=[row(D), pl.BlockSpec((NSHARD, DIN // NSHARD, D), lambda i: (0, 0, 0))],
        out_shape=[_sds((N, D)), _sds((NSHARD, DIN // NSHARD, D), MX)],
        scratch_shapes=[pltpu.VMEM((DINP, D), F32)],
        name="inproj_bwd", compiler_params=_cp(("arbitrary",)))(
            x, w, dxp, du2, dud, gq_f, gq_b, gk_f, gk_b, gv_f, gv_b, gr, daq, dakv, dhl)


def _scan_tables(mr, mi, reverse):
    pw = [(mr, mi)]
    for _ in range(7):
        pr, pi = pw[-1]
        pw.append((pr * mr - pi * mi, pr * mi + pi * mr))
    rows = jnp.arange(8)[:, None]
    out = []
    for d in (1, 2, 4):
        keep = rows >= d
        out += [jnp.where(keep, pw[d - 1][0][None], 0.0), jnp.where(keep, pw[d - 1][1][None], 0.0)]
    out += [jnp.stack([p[0] for p in pw]), jnp.stack([p[1] for p in pw])]
    t = jnp.stack(out)
    if reverse:
        t = t[:, ::-1, :]
    return t.reshape(8, 8, 2, SW).transpose(2, 0, 1, 3)


def _tile_scan(xr, xi, a, cr, ci, reverse):
    for lvl, d in enumerate((1, 2, 4)):
        sh = 8 - d if reverse else d
        sr = pltpu.roll(xr, sh, 0)
        si = pltpu.roll(xi, sh, 0)
        ar, ai = a[2 * lvl], a[2 * lvl + 1]
        xr, xi = xr + ar * sr - ai * si, xi + ar * si + ai * sr
    pr, pi = a[6], a[7]
    return xr + pr * cr - pi * ci, xi + pr * ci + pi * cr


NJ = TT // 8


def _lockstep_tables(mr, mi, reverse):
    nr, ni = mr, mi
    for _ in range(NJ.bit_length() - 1):
        nr, ni = nr * nr - ni * ni, 2.0 * nr * ni
    pr, pi = mr[None], mi[None]
    while pr.shape[0] < NJ:
        k = pr.shape[0]
        tr, ti = pr[k - 1], pi[k - 1]
        pr, pi = (jnp.concatenate([pr, pr * tr - pi * ti]), jnp.concatenate([pi, pr * ti + pi * tr]))
    if reverse:
        pr, pi = pr[::-1], pi[::-1]
    rows = jnp.broadcast_to(jnp.stack([mr, mi])[:, None, :], (2, 8, 2 * SW))
    link = _scan_tables(nr, ni, reverse)
    a = jnp.concatenate([rows.reshape(2, 8, 2, SW).transpose(2, 0, 1, 3), link], axis=1)
    return a, jnp.stack([pr, pi]).reshape(2, NJ, 2, SW).transpose(2, 0, 1, 3)


def _to_lockstep(ref, *lead):
    return jnp.concatenate([ref[(*lead, pl.ds(j, 8, stride=NJ), slice(None))] for j in range(NJ)], axis=0)


def _from_lockstep(val, ref, *lead):
    for j in range(NJ):
        ref[(*lead, pl.ds(j, 8, stride=NJ), slice(None))] = val[8 * j:8 * j + 8]


def _expand_powers(p_ref, pexp):
    for c in range(2):
        for j in range(NJ):
            pexp[c, j] = jnp.broadcast_to(p_ref[0, 0, c, j:j + 1, :], (8, SW))


def _lockstep_scan(xre, xim, a_ref, pexp, car, reverse, extra=None):
    a = [a_ref[0, 0, k] for k in range(10)]
    mr, mi = a[0], a[1]
    order = (lambda i: NJ - 1 - i) if reverse else (lambda i: i)

    def local(i, hcar):
        hr, hi = hcar
        r0 = pl.multiple_of(order(i) * 8, 8)
        hr, hi = mr * hr - mi * hi + xre[pl.ds(r0, 8), :], mr * hi + mi * hr + xim[pl.ds(r0, 8), :]
        xre[pl.ds(r0, 8), :] = hr
        xim[pl.ds(r0, 8), :] = hi
        return hr, hi

    z8 = jnp.zeros((8, SW), F32)
    er, ei = lax.fori_loop(0, NJ, local, (z8, z8), unroll=4)
    c0r, c0i = car[0], car[1]
    er, ei = _tile_scan(er, ei, a[2:], c0r, c0i, reverse)
    rowid = lax.broadcasted_iota(jnp.int32, (8, SW), 0)
    first, sh, last = (7, 7, 0) if reverse else (0, 1, 7)
    cvr = jnp.where(rowid == first, c0r, pltpu.roll(er, sh, 0))
    cvi = jnp.where(rowid == first, c0i, pltpu.roll(ei, sh, 0))
    car[0] = jnp.broadcast_to(er[last:last + 1, :], (8, SW))
    car[1] = jnp.broadcast_to(ei[last:last + 1, :], (8, SW))

    def fix(i, carry):
        j = order(i)
        r0 = pl.multiple_of(j * 8, 8)
        pr, pi = pexp[0, j], pexp[1, j]
        sr = xre[pl.ds(r0, 8), :] + pr * cvr - pi * cvi
        si = xim[pl.ds(r0, 8), :] + pr * cvi + pi * cvr
        xre[pl.ds(r0, 8), :] = sr
        xim[pl.ds(r0, 8), :] = si
        if extra is None:
            return carry
        return (sr, si, extra(r0, sr, si, carry[0], carry[1], carry[2]))

    init = (cvr, cvi, extra(None, None, None, None, None, None)) if extra is not None else 0
    return lax.fori_loop(0, NJ, fix, init, unroll=4)


def _s5_time_block(z, s, t, adjoint):
    flip = (1 - z) if adjoint else z
    return s * (L // TT) + t + flip * (L // TT - 1 - 2 * t)


def _s5_fwd(h, bre, bim, cre, cim, tab):
    nt = L // TT
    taba, tabp = tab

    def body(u_ref, bre_ref, bim_ref, cre_ref, cim_ref, a_ref, p_ref, hre_ref, him_ref, y_ref, car, pexp):
        z = pl.program_id(1)
        s = pl.program_id(2)
        tc = pl.program_id(3)

        @pl.when(tc == 0)
        def _():
            car[...] = jnp.zeros_like(car)

        @pl.when((tc == 0) & (s == 0))
        def _():
            _expand_powers(p_ref, pexp)

        u = _to_lockstep(u_ref)
        hre_ref[0] = _mm(u, bre_ref[0, 0])
        him_ref[0] = _mm(u, bim_ref[0, 0])

        @pl.when(z == 0)
        def _():
            _lockstep_scan(hre_ref.at[0], him_ref.at[0], a_ref, pexp, car, False)

        @pl.when(z == 1)
        def _():
            _lockstep_scan(hre_ref.at[0], him_ref.at[0], a_ref, pexp, car, True)

        _from_lockstep(_mm(hre_ref[0], cre_ref[0, 0]) - _mm(him_ref[0], cim_ref[0, 0]), y_ref, 0)

    tb = lambda b, z, s, t: _s5_time_block(z, s, t, False)
    wspec = lambda r, c: pl.BlockSpec((1, 1, r, c), lambda b, z, s, t: (z, b, 0, 0))
    return pl.pallas_call(
        body, grid=(2, 2, NSEQ, nt),
        in_specs=[pl.BlockSpec((TT, 128), lambda b, z, s, t: (tb(b, z, s, t), b)),
                  wspec(128, SW), wspec(128, SW), wspec(SW, 128), wspec(SW, 128),
                  pl.BlockSpec((1, 1, 10, 8, SW), lambda b, z, s, t: (z, b, 0, 0, 0)),
                  pl.BlockSpec((1, 1, 2, NJ, SW), lambda b, z, s, t: (z, b, 0, 0, 0))],
        out_specs=[pl.BlockSpec((1, TT, SW), lambda b, z, s, t: (z, tb(b, z, s, t), b)),
                   pl.BlockSpec((1, TT, SW), lambda b, z, s, t: (z, tb(b, z, s, t), b)),
                   pl.BlockSpec((1, TT, 128), lambda b, z, s, t: (z, tb(b, z, s, t), b))],
        out_shape=[_sds((2, N, 2 * SW)), _sds((2, N, 2 * SW)), _sds((2, N, 256))],
        scratch_shapes=[pltpu.VMEM((2, 8, SW), F32), pltpu.VMEM((2, NJ, 8, SW), F32)],
        name="s5_fwd", compiler_params=_cp(("arbitrary",) * 4))(h, bre, bim, cre, cim, taba, tabp)


def _s5_bwd(h, dyp, hre, him, bre, bim, cre, cim, tabc):
    nt = L // TT
    taba, tabp = tabc

    def body(u_ref, dy_ref, hre_ref, him_ref, bre_ref, bim_ref, cre_ref, cim_ref, a_ref, p_ref,
             du_ref, dbre_ref, dbim_ref, dcre_ref, dcim_ref, dmu_ref, gre, gim, car, acc, macc, pexp):
        z = pl.program_id(1)
        s = pl.program_id(2)
        tc = pl.program_id(3)

        @pl.when(tc == 0)
        def _():
            car[...] = jnp.zeros_like(car)

        @pl.when((tc == 0) & (s == 0))
        def _():
            acc[...] = jnp.zeros_like(acc)
            macc[...] = jnp.zeros_like(macc)
            _expand_powers(p_ref, pexp)

        dy = _to_lockstep(dy_ref)
        gre[...] = _mm_nt(dy, cre_ref[0, 0])
        gim[...] = -_mm_nt(dy, cim_ref[0, 0])

        def run(reverse):
            def pair(r0, gr_, gi_, pvr, pvi, m):
                if r0 is None:
                    return (macc[0], macc[1])
                hr = hre_ref[0, pl.ds(r0, 8), :]
                hi = him_ref[0, pl.ds(r0, 8), :]
                return (m[0] + pvr * hr + pvi * hi, m[1] + pvi * hr - pvr * hi)

            _, _, (dmr, dmi) = _lockstep_scan(gre, gim, a_ref, pexp, car, reverse, pair)
            macc[0] = dmr
            macc[1] = dmi

        @pl.when(z == 0)
        def _():
            run(True)

        @pl.when(z == 1)
        def _():
            run(False)

        gr = gre[...]
        gi = gim[...]
        u = _to_lockstep(u_ref)
        _from_lockstep(_mm_nt(gr, bre_ref[0, 0]) + _mm_nt(gi, bim_ref[0, 0]), du_ref, 0)
        acc[0] += _mm_tn(u, gr)
        acc[1] += _mm_tn(u, gi)
        acc[2] += _mm_tn(dy, hre_ref[0])
        acc[3] -= _mm_tn(dy, him_ref[0])

        @pl.when((tc == nt - 1) & (s == NSEQ - 1))
        def _():
            grp = lax.broadcasted_iota(jnp.int32, (S5_H, SW), 1) // S5_P
            for k, out in enumerate((dbre_ref, dbim_ref, dcre_ref, dcim_ref)):
                c = jnp.zeros((S5_H, SW), F32)
                for i in range(8):
                    c = c + jnp.where(grp == i, acc[k, i * S5_H:(i + 1) * S5_H, :], 0.0)
                out[0, 0] = c
            dmu_ref[0, 0] = jnp.concatenate([jnp.sum(macc[0], axis=0, keepdims=True),
                                             jnp.sum(macc[1], axis=0, keepdims=True)], axis=0)

    tb = lambda b, z, s, t: _s5_time_block(z, s, t, True)
    wspec = lambda r, c: pl.BlockSpec((1, 1, r, c), lambda b, z, s, t: (z, b, 0, 0))
    tok = lambda w_: pl.BlockSpec((TT, w_), lambda b, z, s, t: (tb(b, z, s, t), b))
    st = pl.BlockSpec((1, TT, SW), lambda b, z, s, t: (z, tb(b, z, s, t), b))
    return pl.pallas_call(
        body, grid=(2, 2, NSEQ, nt),
        in_specs=[tok(128), tok(128), st, st, wspec(128, SW), wspec(128, SW), wspec(SW, 128), wspec(SW, 128),
                  pl.BlockSpec((1, 1, 10, 8, SW), lambda b, z, s, t: (z, b, 0, 0, 0)),
                  pl.BlockSpec((1, 1, 2, NJ, SW), lambda b, z, s, t: (z, b, 0, 0, 0))],
        out_specs=[pl.BlockSpec((1, TT, 128), lambda b, z, s, t: (z, tb(b, z, s, t), b)),
                   wspec(S5_H, SW), wspec(S5_H, SW), wspec(S5_H, SW), wspec(S5_H, SW),
                   wspec(2, SW)],
        out_shape=[_sds((2, N, 256))] + [_sds((2, 2, S5_H, SW))] * 4 + [_sds((2, 2, 2, SW))],
        scratch_shapes=[pltpu.VMEM((TT, SW), F32), pltpu.VMEM((TT, SW), F32), pltpu.VMEM((2, 8, SW), F32),
                        pltpu.VMEM((4, 128, SW), F32), pltpu.VMEM((2, 8, SW), F32), pltpu.VMEM((2, NJ, 8, SW), F32)],
        name="s5_bwd", compiler_params=_cp(("arbitrary",) * 4))(h, dyp, hre, him, bre, bim, cre, cim, taba, tabp)


_GELU_C = math.sqrt(2.0 / math.pi)


def _gelu(y):
    return 0.5 * y * (1.0 + jnp.tanh(_GELU_C * (y + 0.044715 * y * y * y)))


def _gelu_grad(y):
    t = jnp.tanh(_GELU_C * (y + 0.044715 * y * y * y))
    return 0.5 * (1.0 + t) + 0.5 * y * (1.0 - t * t) * _GELU_C * (1.0 + 3 * 0.044715 * y * y)


def _glu_halves(w4_ref):
    return (jnp.concatenate([w4_ref[0], w4_ref[1]], axis=1), jnp.concatenate([w4_ref[2], w4_ref[3]], axis=1))


def _s5_glu_fwd(y2, h, dsk, w4, bv, bg):
    tm = 512

    def body(y2_ref, u_ref, d_ref, w4_ref, bv_ref, bg_ref, ya_ref):
        wv, wg = _glu_halves(w4_ref)
        z = _gelu(y2_ref[0] + y2_ref[1] + d_ref[...] * u_ref[...])
        val = _mm(z, wv) + bv_ref[...]
        gate = _mm(z, wg) + bg_ref[...]
        ya_ref[...] = val * jax.nn.sigmoid(gate)

    full = lambda r, c: pl.BlockSpec((r, c), lambda i: (0, 0))
    return pl.pallas_call(
        body, grid=(N // tm,),
        in_specs=[pl.BlockSpec((2, tm, 256), lambda i: (0, i, 0)), pl.BlockSpec((tm, 256), lambda i: (i, 0)),
                  full(1, 256), pl.BlockSpec((NSHARD, 256, 128), lambda i: (0, 0, 0)), full(1, 256), full(1, 256)],
        out_specs=pl.BlockSpec((tm, 256), lambda i: (i, 0)),
        out_shape=_sds((N, 256)), name="s5_glu_fwd", compiler_params=_cp(("parallel",)))(y2, h, dsk, w4, bv, bg)


def _s5_glu_bwd(y2, h, dsk, w4, bv, bg, dya):
    tm = 512
    nt = N // tm

    def body(y2_ref, u_ref, d_ref, w4_ref, bv_ref, bg_ref, dya_ref,
             dyp_ref, dud_ref, dd_ref, dw4_ref, dbv_ref, dbg_ref, accv, accg):
        i = pl.program_id(0)

        @pl.when(i == 0)
        def _():
            for r in (dd_ref, accv, accg, dbv_ref, dbg_ref):
                r[...] = jnp.zeros_like(r)

        wv, wg = _glu_halves(w4_ref)
        u = u_ref[...]
        y = y2_ref[0] + y2_ref[1] + d_ref[...] * u
        z = _gelu(y)
        val = _mm(z, wv) + bv_ref[...]
        sig = jax.nn.sigmoid(_mm(z, wg) + bg_ref[...])
        dya = dya_ref[...]
        dval = dya * sig
        dgate = dya * val * sig * (1.0 - sig)
        dz = _mm_nt(dval, wv) + _mm_nt(dgate, wg)
        dy = dz * _gelu_grad(y)
        dyp_ref[...] = dy
        dud_ref[...] = dy * d_ref[...]
        dd_ref[...] += jnp.sum(dy * u, axis=0, keepdims=True)
        accv[...] += _mm_tn(z, dval)
        accg[...] += _mm_tn(z, dgate)
        dbv_ref[...] += jnp.sum(dval, axis=0, keepdims=True)
        dbg_ref[...] += jnp.sum(dgate, axis=0, keepdims=True)

        @pl.when(i == nt - 1)
        def _():
            dw4_ref[0] = accv[:, 0:128].astype(MX)
            dw4_ref[1] = accv[:, 128:256].astype(MX)
            dw4_ref[2] = accg[:, 0:128].astype(MX)
            dw4_ref[3] = accg[:, 128:256].astype(MX)

    full = lambda r, c: pl.BlockSpec((r, c), lambda i: (0, 0))
    row = pl.BlockSpec((tm, 256), lambda i: (i, 0))
    wspec = pl.BlockSpec((NSHARD, 256, 128), lambda i: (0, 0, 0))
    return pl.pallas_call(
        body, grid=(nt,),
        in_specs=[pl.BlockSpec((2, tm, 256), lambda i: (0, i, 0)), row, full(1, 256), wspec, full(1, 256), full(1, 256),
                  row],
        out_specs=[row, row, full(1, 256), wspec, full(1, 256), full(1, 256)],
        out_shape=[_sds((N, 256)), _sds((N, 256)), _sds((1, 256)), _sds((NSHARD, 256, 128), MX), _sds((1, 256)),
                   _sds((1, 256))],
        scratch_shapes=[pltpu.VMEM((256, 256), F32), pltpu.VMEM((256, 256), F32)],
        name="s5_glu_bwd", compiler_params=_cp(("arbitrary",)))(y2, h, dsk, w4, bv, bg, dya)


def _logsig(x):
    return jnp.minimum(x, 0.0) - jnp.log(1.0 + jnp.exp(-jnp.abs(x)))


def _gla_gate_fwd(h, wa, ba):
    tm = 512

    def body(hl_ref, wa_ref, ba_ref, la_ref):
        la_ref[...] = _logsig(_mm(hl_ref[...], wa_ref[...]) + ba_ref[...]) * (1.0 / 16.0)

    return pl.pallas_call(
        body, grid=(N // tm,),
        in_specs=[pl.BlockSpec((tm, 128), lambda i: (i, 14)), pl.BlockSpec((128, 256), lambda i: (0, 0)),
                  pl.BlockSpec((1, 256), lambda i: (0, 0))],
        out_specs=pl.BlockSpec((tm, 256), lambda i: (i, 0)),
        out_shape=_sds((N, 256)), name="gla_gate_fwd", compiler_params=_cp(("parallel",)))(h, wa, ba)


def _gla_gate_bwd(h, wa, ba, dla_f, dla_b):
    tm = 512

    def body(hl_ref, wa_ref, ba_ref, df_ref, db_ref, dhl_ref, dwa_ref, dba_ref):
        i = pl.program_id(0)

        @pl.when(i == 0)
        def _():
            dwa_ref[...] = jnp.zeros_like(dwa_ref)
            dba_ref[...] = jnp.zeros_like(dba_ref)

        hl = hl_ref[...]
        pre = _mm(hl, wa_ref[...]) + ba_ref[...]
        dpre = jnp.concatenate([df_ref[...], db_ref[...]], axis=1) * (1.0 / 16.0) * jax.nn.sigmoid(-pre)
        dhl_ref[...] = _mm_nt(dpre, wa_ref[...])
        dwa_ref[...] += _mm_tn(hl, dpre)[0:32]
        dba_ref[...] += jnp.sum(dpre, axis=0, keepdims=True)

    row = pl.BlockSpec((tm, 128), lambda i: (i, 0))
    return pl.pallas_call(
        body, grid=(N // tm,),
        in_specs=[pl.BlockSpec((tm, 128), lambda i: (i, 14)), pl.BlockSpec((128, 256), lambda i: (0, 0)),
                  pl.BlockSpec((1, 256), lambda i: (0, 0)), row, row],
        out_specs=[row, pl.BlockSpec((32, 256), lambda i: (0, 0)), pl.BlockSpec((1, 256), lambda i: (0, 0))],
        out_shape=[_sds((N, 128)), _sds((32, 256)), _sds((1, 256))],
        name="gla_gate_bwd", compiler_params=_cp(("arbitrary",)))(h, wa, ba, dla_f, dla_b)


def _gla_chunk(q, k, v, la, st, rev):
    c = GLA_CHUNK
    b = _cums(la, rev)
    bl = jnp.sum(la, axis=0, keepdims=True)
    q_in = q * (32.0 ** -0.5) * jnp.exp(b)
    k_in = k * jnp.exp(-b)
    k_st = k * jnp.exp(bl - b)
    lane_k = lax.broadcasted_iota(jnp.int32, (1, 128), 1) // 32
    lane_v = lax.broadcasted_iota(jnp.int32, (1, 256), 1) // 64
    r = lax.broadcasted_iota(jnp.int32, (c, c), 0)
    cc = lax.broadcasted_iota(jnp.int32, (c, c), 1)
    keep = (cc > r) if rev else (cc <= r)
    qs = jnp.concatenate([jnp.where(lane_k == hd, q_in, 0.0) for hd in range(4)], axis=0)
    a = _dmm_nt(qs, k_in)
    a = jnp.where(jnp.concatenate([keep] * 4, axis=0), a, 0.0)
    o4 = _dmm(a, v)
    o = _dmm_nt(q_in, st)
    for hd in range(4):
        o = o + jnp.where(lane_v == hd, o4[hd * c:(hd + 1) * c], 0.0)
    bd = (lax.broadcasted_iota(jnp.int32, (256, 128), 0) // 64) == (lax.broadcasted_iota(jnp.int32, (256, 128), 1) // 32)
    st_new = jnp.exp(bl) * st + jnp.where(bd, _dmm_tn(v, k_st), 0.0)
    return o, st_new


def _gla_chunk_of(c, rev):
    return NCHUNK - 1 - c if rev else c


def _gla_fwd(h, la2):
    c = GLA_CHUNK

    def body(qf, kf, vf, laf, qb, kb, vb, lab, of_ref, ob_ref, sf_ref, sb_ref, stf, stb):
        @pl.when(pl.program_id(0) == 0)
        def _():
            stf[...] = jnp.zeros_like(stf)
            stb[...] = jnp.zeros_like(stb)

        ins = [(qf[s], kf[s], vf[s], laf[s], stf[s], qb[s], kb[s], vb[s], lab[s], stb[s]) for s in range(NSEQ)]
        outs = [(_gla_chunk(*t[:5], False), _gla_chunk(*t[5:], True)) for t in ins]
        for s in range(NSEQ):
            sf_ref[s, 0] = ins[s][4]
            sb_ref[s, 0] = ins[s][9]
            (of_ref[s], stf[s]), (ob_ref[s], stb[s]) = outs[s]

    def specs(rev):
        ch = lambda i: _gla_chunk_of(i, rev)
        return [pl.BlockSpec((NSEQ, c, 128), lambda i: (0, ch(i), 2)), pl.BlockSpec((NSEQ, c, 128), lambda i: (0, ch(i), 3)),
                pl.BlockSpec((NSEQ, c, 256), lambda i: (0, ch(i), 2)),
                pl.BlockSpec((NSEQ, c, 128), lambda i: (0, ch(i), 1 if rev else 0))]

    orow = lambda rev: pl.BlockSpec((NSEQ, c, 256), lambda i: (0, _gla_chunk_of(i, rev), 0))
    srow = lambda rev: pl.BlockSpec((NSEQ, 1, 256, 128), lambda i: (0, _gla_chunk_of(i, rev), 0, 0))
    h3, la3 = h.reshape(NSEQ, L, DINP), la2.reshape(NSEQ, L, 256)
    of, ob, sf, sb = pl.pallas_call(
        body, grid=(NCHUNK,),
        in_specs=specs(False) + specs(True),
        out_specs=[orow(False), orow(True), srow(False), srow(True)],
        out_shape=[_sds((NSEQ, L, 256)), _sds((NSEQ, L, 256)), _sds((NSEQ, NCHUNK, 256, 128)),
                   _sds((NSEQ, NCHUNK, 256, 128))],
        scratch_shapes=[pltpu.VMEM((NSEQ, 256, 128), F32), pltpu.VMEM((NSEQ, 256, 128), F32)],
        name="gla_fwd", compiler_params=_cp(("arbitrary",)))(h3, h3, h3, la3, h3, h3, h3, la3)
    return of.reshape(N, 256), ob.reshape(N, 256), sf, sb


def _gla_bwd(h, la2, do, sf, sb):
    c = GLA_CHUNK

    def body(qf, kf, vf, laf, dof, sfr, qb, kb, vb, lab, dob, sbr,
             dqf, dkf, dvf, dlf, dqb, dkb, dvb, dlb, dstf, dstb):
        @pl.when(pl.program_id(0) == 0)
        def _():
            dstf[...] = jnp.zeros_like(dstf)
            dstb[...] = jnp.zeros_like(dstb)

        def one(s, q, k, v, la, do_, st, dst, rev):
            _, vjp = jax.vjp(functools.partial(_gla_chunk, rev=rev), q[s], k[s], v[s], la[s], st[s, 0])
            return vjp((do_[s], dst[s]))

        res = [(one(s, qf, kf, vf, laf, dof, sfr, dstf, False), one(s, qb, kb, vb, lab, dob, sbr, dstb, True))
               for s in range(NSEQ)]
        for s in range(NSEQ):
            dqf[s], dkf[s], dvf[s], dlf[s], dstf[s] = res[s][0]
            dqb[s], dkb[s], dvb[s], dlb[s], dstb[s] = res[s][1]

    def specs(rev):
        ch = lambda i: _gla_chunk_of(i, not rev)
        return [pl.BlockSpec((NSEQ, c, 128), lambda i: (0, ch(i), 2)), pl.BlockSpec((NSEQ, c, 128), lambda i: (0, ch(i), 3)),
                pl.BlockSpec((NSEQ, c, 256), lambda i: (0, ch(i), 2)),
                pl.BlockSpec((NSEQ, c, 128), lambda i: (0, ch(i), 1 if rev else 0)),
                pl.BlockSpec((NSEQ, c, 256), lambda i: (0, ch(i), 0)),
                pl.BlockSpec((NSEQ, 1, 256, 128), lambda i: (0, ch(i), 0, 0))]

    def ospecs(rev):
        ch = lambda i: _gla_chunk_of(i, not rev)
        n = pl.BlockSpec((NSEQ, c, 128), lambda i: (0, ch(i), 0))
        return [n, n, pl.BlockSpec((NSEQ, c, 256), lambda i: (0, ch(i), 0)), n]

    oshape = [_sds((NSEQ, L, 128)), _sds((NSEQ, L, 128)), _sds((NSEQ, L, 256)), _sds((NSEQ, L, 128))]
    h3, la3, do3 = h.reshape(NSEQ, L, DINP), la2.reshape(NSEQ, L, 256), do.reshape(NSEQ, L, 256)
    res = pl.pallas_call(
        body, grid=(NCHUNK,),
        in_specs=specs(False) + specs(True),
        out_specs=ospecs(False) + ospecs(True),
        out_shape=oshape + oshape,
        scratch_shapes=[pltpu.VMEM((NSEQ, 256, 128), F32), pltpu.VMEM((NSEQ, 256, 128), F32)],
        name="gla_bwd", compiler_params=_cp(("arbitrary",)))(h3, h3, h3, la3, do3, sf, h3, h3, h3, la3, do3, sb)
    return [r.reshape(N, r.shape[-1]) for r in res]


def _gla_post(of, ob, r, g):
    o = of + ob
    head = lax.broadcasted_iota(jnp.int32, (1, 256), 1) // 64
    mu = jnp.zeros_like(o)
    for hd in range(4):
        mu = mu + jnp.where(head == hd, jnp.sum(jnp.where(head == hd, o, 0.0), axis=-1, keepdims=True) * (1.0 / 64.0), 0.0)
    xc = o - mu
    var = jnp.zeros_like(o)
    for hd in range(4):
        var = var + jnp.where(head == hd, jnp.sum(jnp.where(head == hd, xc * xc, 0.0), axis=-1, keepdims=True) * (1.0 / 64.0), 0.0)
    return xc * lax.rsqrt(var + LN_EPS) * g * (r * jax.nn.sigmoid(r))


def _gla_post_fwd(of, ob, h, g):
    tm = 512

    def body(of_ref, ob_ref, r_ref, g_ref, y_ref):
        y_ref[...] = _gla_post(of_ref[...], ob_ref[...], r_ref[...], g_ref[...])

    row = pl.BlockSpec((tm, 256), lambda i: (i, 0))
    return pl.pallas_call(
        body, grid=(N // tm,),
        in_specs=[row, row, pl.BlockSpec((tm, 256), lambda i: (i, 3)), pl.BlockSpec((1, 256), lambda i: (0, 0))],
        out_specs=row, out_shape=_sds((N, 256)), name="gla_post_fwd", compiler_params=_cp(("parallel",)))(of, ob, h, g)


def _gla_post_bwd(of, ob, h, g, dyb):
    tm = 512

    def body(of_ref, ob_ref, r_ref, g_ref, dy_ref, do_ref, dr_ref, dg_ref):
        @pl.when(pl.program_id(0) == 0)
        def _():
            dg_ref[...] = jnp.zeros_like(dg_ref)

        _, vjp = jax.vjp(_gla_post, of_ref[...], ob_ref[...], r_ref[...], g_ref[...])
        go, _, gr, gg = vjp(dy_ref[...])
        do_ref[...] = go
        dr_ref[...] = gr
        dg_ref[...] += gg

    row = pl.BlockSpec((tm, 256), lambda i: (i, 0))
    one = pl.BlockSpec((1, 256), lambda i: (0, 0))
    return pl.pallas_call(
        body, grid=(N // tm,),
        in_specs=[row, row, pl.BlockSpec((tm, 256), lambda i: (i, 3)), one, row],
        out_specs=[row, row, one], out_shape=[_sds((N, 256)), _sds((N, 256)), _sds((1, 256))],
        name="gla_post_bwd", compiler_params=_cp(("arbitrary",)))(of, ob, h, g, dyb)


def _rope_tables(width):
    pos = jnp.arange(L, dtype=F32)
    inv_freq = ROPE_THETA ** (-jnp.arange(0, ROT, 2, dtype=F32) / ROT)
    ang = pos[:, None] * inv_freq[None, :]
    cos, sin = jnp.cos(ang), jnp.sin(ang)
    one = jnp.ones((L, 64 - ROT), F32)
    zero = jnp.zeros((L, 64 - ROT), F32)
    z8 = jnp.zeros((L, ROT // 2), F32)
    c = jnp.concatenate([cos, cos, one], axis=1)
    sa = jnp.concatenate([z8, sin, zero], axis=1)
    sb = jnp.concatenate([-sin, z8, zero], axis=1)
    rep = width // 64
    return jnp.stack([jnp.tile(c, (1, rep)), jnp.tile(sa, (1, rep)), jnp.tile(sb, (1, rep))])


def _pieces(t, f):
    out = [f(t[:, c * 128:(c + 1) * 128]) for c in range(t.shape[-1] // 128)]
    return out[0] if len(out) == 1 else jnp.concatenate(out, axis=1)


def _rope(t, tab):
    return _pieces(t, lambda x: x * tab[0] + pltpu.roll(x, ROT // 2, 1) * tab[1] + pltpu.roll(x, 128 - ROT // 2, 1) * tab[2])


def _rope_t(g, tab):
    return _pieces(g, lambda x: x * tab[0] + pltpu.roll(x * tab[1], 128 - ROT // 2, 1) + pltpu.roll(x * tab[2], ROT // 2, 1))


def _swa_pad_kv(kv_ref, tk_ref, kexp, vexp):
    z = jnp.zeros((SWA_BLK, 256), F32)
    kr = _rope(kv_ref[:, 0:128], tk_ref[...])
    for hk in range(2):
        for pad in (kexp, vexp):
            pad[hk, 0:SWA_BLK] = z
            pad[hk, SWA_BLK + L:] = z
        kexp[hk, SWA_BLK:SWA_BLK + L] = _swa_expand(kr, hk)
        vexp[hk, SWA_BLK:SWA_BLK + L] = _swa_expand(kv_ref[:, 128:256], hk)


def _swa_expand(x, hk):
    lane = lax.broadcasted_iota(jnp.int32, x.shape, 1)
    sw = pltpu.roll(x, 64, 1)
    pair = jnp.where(lane < 64, x, sw) if hk == 0 else jnp.where(lane < 64, sw, x)
    return jnp.concatenate([pair, pair], axis=1)


def _swa_fold(x, hk):
    a = x[:, 0:128] + x[:, 128:256]
    t = a + pltpu.roll(a, 64, 1)
    lane = lax.broadcasted_iota(jnp.int32, a.shape, 1)
    return jnp.where((lane < 64) if hk == 0 else (lane >= 64), t, 0.0)


def _swa_probs(q2, kexp, n, sink_ref, hk):
    slot = lax.broadcasted_iota(jnp.int32, (1, 256), 1) // 64
    qs = jnp.concatenate([jnp.where(slot == g, q2, 0.0) for g in range(4)], axis=0)
    s = _mm_nt(qs, kexp) * 0.125
    i = lax.broadcasted_iota(jnp.int32, (SWA_BLK, 3 * SWA_BLK), 0)
    j = lax.broadcasted_iota(jnp.int32, (SWA_BLK, 3 * SWA_BLK), 1)
    kpos = n * SWA_BLK - SWA_BLK + j
    ok = (j - i >= 0) & (j - i <= 2 * SWA_BLK) & (kpos >= 0) & (kpos < L)
    s = jnp.where(jnp.concatenate([ok] * 4, axis=0), s, NEG_BIG)
    rowg = lax.broadcasted_iota(jnp.int32, (4 * SWA_BLK, 1), 0) // SWA_BLK
    sink = jnp.zeros((4 * SWA_BLK, 1), F32)
    for g in range(4):
        sink = jnp.where(rowg == g, sink_ref[hk * 4 + g], sink)
    m = jnp.maximum(jnp.max(s, axis=-1, keepdims=True), sink)
    p = jnp.exp(s - m)
    ps = jnp.exp(sink - m)
    inv = 1.0 / (jnp.sum(p, axis=-1, keepdims=True) + ps)
    return qs, p * inv, ps * inv, slot, rowg


def _swa_qtab(tk_ref, r0):
    return [tk_ref[i, pl.ds(r0, SWA_BLK), :] for i in range(3)]


def _swa_fwd(h, tk, sink):
    def body(sink_ref, q_ref, kv_ref, tk_ref, y_ref, kexp, vexp):
        n = pl.program_id(1)

        @pl.when(n == 0)
        def _():
            _swa_pad_kv(kv_ref, tk_ref, kexp, vexp)

        r0 = pl.multiple_of(n * SWA_BLK, SWA_BLK)
        q = _rope(q_ref[...], _swa_qtab(tk_ref, r0))
        for hk in range(2):
            _, p, _, slot, _ = _swa_probs(q[:, hk * 256:(hk + 1) * 256], kexp[hk, pl.ds(r0, 3 * SWA_BLK), :], n,
                                          sink_ref, hk)
            o4 = _mm(p, vexp[hk, pl.ds(r0, 3 * SWA_BLK), :])
            o = jnp.zeros((SWA_BLK, 256), F32)
            for g in range(4):
                o = o + jnp.where(slot == g, o4[g * SWA_BLK:(g + 1) * SWA_BLK], 0.0)
            y_ref[:, hk * 256:(hk + 1) * 256] = o

    return pl.pallas_call(
        body,
        grid_spec=pltpu.PrefetchScalarGridSpec(
            num_scalar_prefetch=1, grid=(NSEQ, NBLK),
            in_specs=[pl.BlockSpec((SWA_BLK, 512), lambda s, n, sk: (s * NBLK + n, 2)),
                      pl.BlockSpec((L, 256), lambda s, n, sk: (s, 6)),
                      pl.BlockSpec((3, L, 128), lambda s, n, sk: (0, 0, 0))],
            out_specs=pl.BlockSpec((SWA_BLK, 512), lambda s, n, sk: (s * NBLK + n, 0)),
            scratch_shapes=[pltpu.VMEM((2, L + 2 * SWA_BLK, 256), F32), pltpu.VMEM((2, L + 2 * SWA_BLK, 256), F32)]),
        out_shape=_sds((N, 512)), name="swa_fwd", compiler_params=_cp(("arbitrary", "arbitrary")))(sink, h, h, tk)


def _swa_bwd(h, tk, sink, dyc):
    def body(sink_ref, q_ref, kv_ref, tk_ref, dy_ref, dq_ref, dkv_ref, dsink_ref, kexp_all, vexp_all, dkacc, dvacc):
        sq = pl.program_id(0)
        n = pl.program_id(1)

        @pl.when(n == 0)
        def _():
            _swa_pad_kv(kv_ref, tk_ref, kexp_all, vexp_all)
            dkacc[...] = jnp.zeros_like(dkacc)
            dvacc[...] = jnp.zeros_like(dvacc)

        @pl.when((n == 0) & (sq == 0))
        def _():
            dsink_ref[...] = jnp.zeros_like(dsink_ref)

        r0 = pl.multiple_of(n * SWA_BLK, SWA_BLK)
        tq = _swa_qtab(tk_ref, r0)
        q = _rope(q_ref[...], tq)
        hrow = lax.broadcasted_iota(jnp.int32, (8, 128), 0)
        dsk = jnp.zeros((8, 128), F32)
        for hk in range(2):
            kexp = kexp_all[hk, pl.ds(r0, 3 * SWA_BLK), :]
            vexp = vexp_all[hk, pl.ds(r0, 3 * SWA_BLK), :]
            qs, p, ps, slot, rowg = _swa_probs(q[:, hk * 256:(hk + 1) * 256], kexp, n, sink_ref, hk)
            dy2 = dy_ref[:, hk * 256:(hk + 1) * 256]
            dos = jnp.concatenate([jnp.where(slot == g, dy2, 0.0) for g in range(4)], axis=0)
            dp = _mm_nt(dos, vexp)
            delta = jnp.sum(p * dp, axis=-1, keepdims=True)
            ds = p * (dp - delta) * 0.125
            dsr = -ps * delta
            for g in range(4):
                dsk = dsk + jnp.where(hrow == hk * 4 + g, jnp.sum(jnp.where(rowg == g, dsr, 0.0), axis=0, keepdims=True), 0.0)
            dq4 = _mm(ds, kexp)
            dq2 = jnp.zeros((SWA_BLK, 256), F32)
            for g in range(4):
                dq2 = dq2 + jnp.where(slot == g, dq4[g * SWA_BLK:(g + 1) * SWA_BLK], 0.0)
            dq_ref[:, hk * 256:(hk + 1) * 256] = dq2
            dkacc[hk, pl.ds(r0, 3 * SWA_BLK), :] += _mm_tn(ds, qs)
            dvacc[hk, pl.ds(r0, 3 * SWA_BLK), :] += _mm_tn(p, dos)
        dq_ref[...] = _rope_t(dq_ref[...], tq)
        dsink_ref[...] += dsk

        @pl.when(n == NBLK - 1)
        def _():
            seq = slice(SWA_BLK, SWA_BLK + L)
            dkv_ref[:, 0:128] = _rope_t(_swa_fold(dkacc[0, seq], 0) + _swa_fold(dkacc[1, seq], 1), tk_ref[...])
            dkv_ref[:, 128:256] = _swa_fold(dvacc[0, seq], 0) + _swa_fold(dvacc[1, seq], 1)

    blk = lambda col: pl.BlockSpec((SWA_BLK, 512), lambda s, n, sk: (s * NBLK + n, col))
    pad = pltpu.VMEM((2, L + 2 * SWA_BLK, 256), F32)
    return pl.pallas_call(
        body,
        grid_spec=pltpu.PrefetchScalarGridSpec(
            num_scalar_prefetch=1, grid=(NSEQ, NBLK),
            in_specs=[blk(2), pl.BlockSpec((L, 256), lambda s, n, sk: (s, 6)),
                      pl.BlockSpec((3, L, 128), lambda s, n, sk: (0, 0, 0)), blk(0)],
            out_specs=[blk(0), pl.BlockSpec((L, 256), lambda s, n, sk: (s, 0)),
                       pl.BlockSpec((8, 128), lambda s, n, sk: (0, 0))],
            scratch_shapes=[pad, pad, pad, pad]),
        out_shape=[_sds((N, 512)), _sds((N, 256)), _sds((8, 128))],
        name="swa_bwd", compiler_params=_cp(("arbitrary", "arbitrary")))(sink, h, h, tk, dyc)


def _outproj_fwd(ya, yb, yc, x, wo, g, b):
    tm = 512

    def body(ya_ref, yb_ref, yc_ref, x_ref, wo_ref, g_ref, b_ref, s_ref, x1_ref):
        mix = _mm(ya_ref[...], wo_ref[0:256]) + _mm(yb_ref[...], wo_ref[256:512]) + _mm(yc_ref[...], wo_ref[512:1024])
        s = ALPHA * x_ref[...] + mix
        s_ref[...] = s
        x1_ref[...] = _ln_fwd(s, g_ref[...], b_ref[...])

    row = lambda w_: pl.BlockSpec((tm, w_), lambda i: (i, 0))
    one = pl.BlockSpec((1, D), lambda i: (0, 0))
    return pl.pallas_call(
        body, grid=(N // tm,),
        in_specs=[row(256), row(256), row(512), row(D), pl.BlockSpec((D, D), lambda i: (0, 0)), one, one],
        out_specs=[row(D), row(D)], out_shape=[_sds((N, D)), _sds((N, D))],
        name="outproj_fwd", compiler_params=_cp(("parallel",)))(ya, yb, yc, x, wo, g, b)


def _outproj_bwd(dx1, s1, ya, yb, yc, wo, g):
    tm = 512
    nt = N // tm

    def body(dx1_ref, s_ref, ya_ref, yb_ref, yc_ref, wo_ref, g_ref,
             dya_ref, dyb_ref, dyc_ref, dxp_ref, dwo_ref, dg_ref, db_ref, acc):
        i = pl.program_id(0)

        @pl.when(i == 0)
        def _():
            acc[...] = jnp.zeros_like(acc)
            dg_ref[...] = jnp.zeros_like(dg_ref)
            db_ref[...] = jnp.zeros_like(db_ref)

        ds, dg, db = _ln_bwd(dx1_ref[...], s_ref[...], g_ref[...])
        dg_ref[...] += dg
        db_ref[...] += db
        dxp_ref[...] = ALPHA * ds
        dy = _mm_nt(ds, wo_ref[...])
        dya_ref[...] = dy[:, 0:256]
        dyb_ref[...] = dy[:, 256:512]
        dyc_ref[...] = dy[:, 512:1024]
        acc[0:256] += _mm_tn(ya_ref[...], ds)
        acc[256:512] += _mm_tn(yb_ref[...], ds)
        acc[512:1024] += _mm_tn(yc_ref[...], ds)

        @pl.when(i == nt - 1)
        def _():
            dwo_ref[...] = acc[...].astype(MX)

    row = lambda w_: pl.BlockSpec((tm, w_), lambda i: (i, 0))
    one = pl.BlockSpec((1, D), lambda i: (0, 0))
    full = pl.BlockSpec((D, D), lambda i: (0, 0))
    return pl.pallas_call(
        body, grid=(nt,),
        in_specs=[row(D), row(D), row(256), row(256), row(512), full, one],
        out_specs=[row(256), row(256), row(512), row(D), full, one, one],
        out_shape=[_sds((N, 256)), _sds((N, 256)), _sds((N, 512)), _sds((N, D)), _sds((D, D), MX), _sds((1, D)), _sds((1, D))],
        scratch_shapes=[pltpu.VMEM((D, D), F32)],
        name="outproj_bwd", compiler_params=_cp(("arbitrary",)))(dx1, s1, ya, yb, yc, wo, g)


def _ffn_fwd(x1, w1, w2, g, b):
    tm = FFN_TM

    def body(x_ref, w1_ref, w2_ref, g_ref, b_ref, a_ref, s_ref, x2_ref):
        j = pl.program_id(1)

        @pl.when(j == 0)
        def _():
            s_ref[...] = ALPHA * x_ref[...]

        a = _mm(x_ref[...], w1_ref[0])
        a_ref[...] = a.astype(MX)
        hid = jnp.square(jnp.maximum(a, 0.0))
        s_ref[...] += _mm(hid, w2_ref[0])

        @pl.when(j == NSHARD - 1)
        def _():
            x2_ref[...] = _ln_fwd(s_ref[...], g_ref[...], b_ref[...])

    row = pl.BlockSpec((tm, D), lambda i, j: (i, 0))
    wj = pl.BlockSpec((1, D, D), lambda i, j: (j, 0, 0))
    one = pl.BlockSpec((1, D), lambda i, j: (0, 0))
    return pl.pallas_call(
        body, grid=(N // tm, NSHARD),
        in_specs=[row, wj, wj, one, one],
        out_specs=[pl.BlockSpec((tm, D), lambda i, j: (i, j)), row, row],
        out_shape=[_sds((N, DFF), MX), _sds((N, D)), _sds((N, D))],
        name="ffn_fwd", compiler_params=_cp(("parallel", "arbitrary"), FFN_VMEM))(x1, w1, w2, g, b)


def _ffn_bwd_act(dy, s2, a, w1, w2, g):
    tm = FFN_TM

    def body(dy_ref, s_ref, a_ref, w1_ref, w2_ref, g_ref, da_ref, ds_ref, dx1_ref, dg_ref, db_ref):
        i = pl.program_id(0)
        j = pl.program_id(1)

        @pl.when((i == 0) & (j == 0))
        def _():
            dg_ref[...] = jnp.zeros_like(dg_ref)
            db_ref[...] = jnp.zeros_like(db_ref)

        @pl.when(j == 0)
        def _():
            ds, dg, db = _ln_bwd(dy_ref[...], s_ref[...], g_ref[...])
            ds_ref[...] = ds.astype(MX)
            dg_ref[...] += dg
            db_ref[...] += db
            dx1_ref[...] = ALPHA * ds

        dhid = _mm_nt(ds_ref[...], w2_ref[0])
        da = dhid * 2.0 * jnp.maximum(a_ref[...].astype(F32), 0.0)
        da_ref[...] = da.astype(MX)
        dx1_ref[...] += _mm_nt(da, w1_ref[0])

    row = pl.BlockSpec((tm, D), lambda i, j: (i, 0))
    col = pl.BlockSpec((tm, D), lambda i, j: (i, j))
    wj = pl.BlockSpec((1, D, D), lambda i, j: (j, 0, 0))
    one = pl.BlockSpec((1, D), lambda i, j: (0, 0))
    return pl.pallas_call(
        body, grid=(N // tm, NSHARD),
        in_specs=[row, row, col, wj, wj, one],
        out_specs=[col, row, row, one, one],
        out_shape=[_sds((N, DFF), MX), _sds((N, D), MX), _sds((N, D)), _sds((1, D)), _sds((1, D))],
        name="ffn_bwd_act", compiler_params=_cp(("arbitrary", "arbitrary"), FFN_VMEM))(dy, s2, a, w1, w2, g)


def _ffn_bwd_w(x1, da, a, ds):
    tm = FFN_TM_W
    nt = N // tm

    def body(x_ref, da_ref, a_ref, ds_ref, dw1_ref, dw2_ref, acc1, acc2):
        i = pl.program_id(1)

        @pl.when(i == 0)
        def _():
            acc1[...] = jnp.zeros_like(acc1)
            acc2[...] = jnp.zeros_like(acc2)

        acc1[...] += _mm_tn(x_ref[...], da_ref[...])
        hid = jnp.square(jnp.maximum(a_ref[...].astype(F32), 0.0))
        acc2[...] += _mm_tn(hid, ds_ref[...])

        @pl.when(i == nt - 1)
        def _():
            dw1_ref[0] = acc1[...].astype(MX)
            dw2_ref[0] = acc2[...].astype(MX)

    row = pl.BlockSpec((tm, D), lambda j, i: (i, 0))
    col = pl.BlockSpec((tm, D), lambda j, i: (i, j))
    wj = pl.BlockSpec((1, D, D), lambda j, i: (j, 0, 0))
    return pl.pallas_call(
        body, grid=(NSHARD, nt),
        in_specs=[row, col, col, row], out_specs=[wj, wj],
        out_shape=[_sds((NSHARD, D, D), MX), _sds((NSHARD, D, D), MX)],
        scratch_shapes=[pltpu.VMEM((D, D), F32), pltpu.VMEM((D, D), F32)],
        name="ffn_bwd_w", compiler_params=_cp(("parallel", "arbitrary")))(x1, da, a, ds)


def _loss_head(y, target):
    tm = 512

    def body(y_ref, t_ref, dy_ref, l_ref):
        @pl.when(pl.program_id(0) == 0)
        def _():
            l_ref[...] = jnp.zeros_like(l_ref)

        e = y_ref[...] - t_ref[...]
        dy_ref[...] = e * (1.0 / D)
        l_ref[...] += jnp.sum(jnp.sum(e * e, axis=1, keepdims=True), axis=0, keepdims=True) * (0.5 / D)

    row = pl.BlockSpec((tm, D), lambda i: (i, 0))
    return pl.pallas_call(
        body, grid=(N // tm,), in_specs=[row, row],
        out_specs=[row, pl.BlockSpec((8, 128), lambda i: (0, 0))],
        out_shape=[_sds((N, D)), _sds((8, 128))], name="loss_head", compiler_params=_cp(("arbitrary",)))(y, target)


def _s5_discretize(a_re, a_im, log_step, b_re, b_im):
    lam = lax.complex(a_re, a_im)
    lam_bar = jnp.exp(lam * jnp.exp(log_step))
    b_bar = ((lam_bar - 1.0) / lam)[..., None] * lax.complex(b_re, b_im)
    return jnp.real(lam_bar), jnp.imag(lam_bar), jnp.real(b_bar), jnp.imag(b_bar)


def _s5_in_blocks(b):
    e = jnp.eye(8, dtype=F32)
    return jnp.einsum('ij,zbjph->zbihjp', e, b.reshape(2, 2, 8, S5_P, S5_H)).reshape(2, 2, 128, SW)


def _s5_in_unblocks(d):
    return jnp.einsum('zbihip->zbiph', d.reshape(2, 2, 8, S5_H, 8, S5_P)).reshape(2, S5_G, S5_P, S5_H)


def _s5_out_blocks(c):
    e = jnp.eye(8, dtype=F32)
    return jnp.einsum('ij,zbjhp->zbjpih', e, c.reshape(2, 2, 8, S5_H, S5_P)).reshape(2, 2, SW, 128)


def _s5_out_unblocks(d):
    return jnp.einsum('zbipih->zbihp', d.reshape(2, 2, 8, S5_P, 8, S5_H)).reshape(2, S5_G, S5_H, S5_P)


def _gate_weight(w_a):
    z = jnp.zeros((16, 128), F32)
    top = jnp.concatenate([w_a[0], z], axis=1)
    bot = jnp.concatenate([z, w_a[1]], axis=1)
    return jnp.concatenate([top, bot, jnp.zeros((96, 256), F32)], axis=0)


def _layer_prep(p):
    lr, li, br, bi = _s5_discretize(p["s5_a_re"], p["s5_a_im"], p["s5_log_step"], p["s5_b_re"], p["s5_b_im"])
    q = dict(p)
    q["bre"] = _s5_in_blocks(br).astype(MX)
    q["bim"] = _s5_in_blocks(bi).astype(MX)
    q["cre"] = _s5_out_blocks(p["s5_c_re"]).astype(MX)
    q["cim"] = _s5_out_blocks(p["s5_c_im"]).astype(MX)
    mr, mi = lr.reshape(2, 1024), li.reshape(2, 1024)
    both = lambda t0, t1: tuple(jnp.stack(p) for p in zip(t0, t1))
    q["tab"] = both(_lockstep_tables(mr[0], mi[0], False), _lockstep_tables(mr[1], mi[1], True))
    q["tabc"] = both(_lockstep_tables(mr[0], -mi[0], True), _lockstep_tables(mr[1], -mi[1], False))
    q["dsk"] = p["s5_d"].reshape(1, 256)
    q["wa"] = _gate_weight(p["gla_w_a"]).astype(MX)
    q["ba"] = p["gla_b_a"].reshape(1, 256)
    q["lng"] = p["gla_ln_g"].reshape(1, 256)
    q["bv"] = p["s5_b_glu"][:256].reshape(1, 256)
    q["bg"] = p["s5_b_glu"][256:].reshape(1, 256)
    for k in ("ln1_g", "ln1_b", "ln2_g", "ln2_b"):
        q[k] = p[k].reshape(1, D)
    return q


def _layer_fwd(x, q, tk, fetch):
    q["w_in"] = fetch("w_in", x)
    h = _inproj_fwd(x, q["w_in"])
    hre, him, y2 = _s5_fwd(h, q["bre"], q["bim"], q["cre"], q["cim"], q["tab"])
    q["w4"] = fetch("s5_w_glu", y2)
    ya = _s5_glu_fwd(y2, h, q["dsk"], q["w4"], q["bv"], q["bg"])
    la2 = _gla_gate_fwd(h, q["wa"], q["ba"])
    of, ob, sf, sb = _gla_fwd(h, la2)
    yb = _gla_post_fwd(of, ob, h, q["lng"])
    yc = _swa_fwd(h, tk, q["swa_sink"])
    q["w_out"] = fetch("w_out", yc)
    s1, x1 = _outproj_fwd(ya, yb, yc, x, q["w_out"], q["ln1_g"], q["ln1_b"])
    q["w_ff1"] = fetch("w_ff1", x1)
    q["w_ff2"] = fetch("w_ff2", x1)
    a, s2, x2 = _ffn_fwd(x1, q["w_ff1"], q["w_ff2"], q["ln2_g"], q["ln2_b"])
    saved = dict(x=x, h=h, hre=hre, him=him, y2=y2, ya=ya, la2=la2, of=of, ob=ob, sf=sf, sb=sb, yb=yb, yc=yc,
                 s1=s1, x1=x1, a=a, s2=s2)
    return x2, saved


def _layer_bwd(dy, q, sv, tk, emit):
    g = {}
    da, ds2, dx1, g["dg2"], g["db2"] = _ffn_bwd_act(dy, sv["s2"], sv["a"], q["w_ff1"], q["w_ff2"], q["ln2_g"])
    dw1, dw2 = _ffn_bwd_w(sv["x1"], da, sv["a"], ds2)
    tie = emit(dict(w_ff1=dw1, w_ff2=dw2))
    dya, dyb, dyc, dxp, dwo, g["dg1"], g["db1"] = _outproj_bwd(dx1, sv["s1"], sv["ya"], sv["yb"], sv["yc"],
                                                               q["w_out"], q["ln1_g"] + tie)
    h = sv["h"]
    daq, dakv, g["dsink"] = _swa_bwd(h, tk, q["swa_sink"], dyc)
    do, gr, g["dlng"] = _gla_post_bwd(sv["of"], sv["ob"], h, q["lng"], dyb)
    gq_f, gk_f, gv_f, gl_f, gq_b, gk_b, gv_b, gl_b = _gla_bwd(h, sv["la2"], do, sv["sf"], sv["sb"])
    dhl, g["dwa"], g["dba"] = _gla_gate_bwd(h, q["wa"], q["ba"], gl_f, gl_b)
    dyp, dud, g["dd"], dw4, g["dbv"], g["dbg"] = _s5_glu_bwd(sv["y2"], h, q["dsk"], q["w4"], q["bv"], q["bg"], dya)
    tie = emit(dict(w_out=dwo.reshape(NSHARD, D // NSHARD, D), s5_w_glu=dw4))
    du2, g["dbre"], g["dbim"], g["dcre"], g["dcim"], g["dmu"] = _s5_bwd(
        h, dyp, sv["hre"], sv["him"], q["bre"], q["bim"], q["cre"], q["cim"], (q["tabc"][0], q["tabc"][1] + tie))
    dx, dwt = _inproj_bwd(sv["x"], q["w_in"], dxp, du2, dud, gq_f, gq_b, gk_f, gk_b, gv_f, gv_b, gr, daq, dakv, dhl)
    tie = emit(dict(w_in=dwt))
    return dx, g, tie


NATIVE = ("dmu", "dbre", "dbim", "dcre", "dcim", "dd", "dbv", "dbg", "dwa", "dba", "dlng", "dsink",
          "dg1", "db1", "dg2", "db2", "loss")
ICI_CORE = (0, 0, 0, 1, 1, 0, 0, 0, 1, 1, 1, 1, 0, 0, 1, 1, 0)


def _finish_small(n, w):
    g = {}
    dmu = n["dmu"]
    dlr = dmu[:, :, :, 0].reshape(DEPTH, 2, S5_G, S5_P)
    dli = dmu[:, :, :, 1].reshape(DEPTH, 2, S5_G, S5_P)

    def unblock(c, perm, shape):
        return c.reshape(DEPTH, 2, 2, S5_H, 8, S5_P).transpose(perm).reshape(shape)

    b_shape, c_shape = (DEPTH, 2, S5_G, S5_P, S5_H), (DEPTH, 2, S5_G, S5_H, S5_P)
    _, vjp = jax.vjp(_s5_discretize, w["s5_a_re"], w["s5_a_im"], w["s5_log_step"], w["s5_b_re"], w["s5_b_im"])
    (g["s5_a_re"], g["s5_a_im"], g["s5_log_step"], g["s5_b_re"], g["s5_b_im"]) = vjp(
        (dlr, dli, unblock(n["dbre"], (0, 1, 2, 4, 5, 3), b_shape), unblock(n["dbim"], (0, 1, 2, 4, 5, 3), b_shape)))
    g["s5_c_re"] = unblock(n["dcre"], (0, 1, 2, 4, 3, 5), c_shape)
    g["s5_c_im"] = unblock(n["dcim"], (0, 1, 2, 4, 3, 5), c_shape)
    g["s5_d"] = n["dd"].reshape(DEPTH, S5_G, S5_H)
    g["s5_b_glu"] = jnp.concatenate([n["dbv"], n["dbg"]], axis=2).reshape(DEPTH, 512)
    g["gla_w_a"] = jnp.stack([n["dwa"][:, 0:16, 0:128], n["dwa"][:, 16:32, 128:256]], axis=1)
    g["gla_b_a"] = n["dba"].reshape(DEPTH, 2, 128)
    g["gla_ln_g"] = n["dlng"].reshape(DEPTH, 256)
    g["swa_sink"] = n["dsink"][:, :, 0]
    for k, s in (("ln1_g", "dg1"), ("ln1_b", "db1"), ("ln2_g", "dg2"), ("ln2_b", "db2")):
        g[k] = n[s].reshape(DEPTH, D)
    return g


def _local_step(x, target, qs, tk, fetch, emit):
    saved = []
    for l, q in enumerate(qs):
        x, sv = _layer_fwd(x, q, tk, functools.partial(fetch, l))
        saved.append(sv)
    dy, lacc = _loss_head(x, target)
    smalls = [None] * DEPTH
    tie = 0.0
    for l in reversed(range(DEPTH)):
        qs[l]["ln2_g"] = qs[l]["ln2_g"] + tie
        dy, smalls[l], tie = _layer_bwd(dy, qs[l], saved[l], tk, functools.partial(emit, l))
    smalls[0]["db2"] = smalls[0]["db2"] + tie
    for l in range(DEPTH):
        smalls[l]["loss"] = lacc if l == 0 else jnp.zeros_like(lacc)
    return lacc[0, 0], dy, smalls


BIG = ("w_in", "s5_w_glu", "w_out", "w_ff1", "w_ff2")
SMALL = ("s5_a_re", "s5_a_im", "s5_log_step", "s5_b_re", "s5_b_im", "s5_c_re", "s5_c_im", "s5_d", "s5_b_glu",
         "gla_w_a", "gla_b_a", "gla_ln_g", "swa_sink", "ln1_g", "ln1_b", "ln2_g", "ln2_b")
ANY = pl.BlockSpec(memory_space=pl.ANY)


def _place():
    x, y, c = lax.axis_index("x"), lax.axis_index("y"), lax.axis_index("c")
    return x, y, c, [(1 - x, y), (x, 1 - y), (1 - x, 1 - y)]


HBM = pl.BlockSpec(memory_space=pltpu.HBM)
SEMS = pl.BlockSpec(memory_space=pltpu.SEMAPHORE)
EFFECT = pltpu.SideEffectType.DATAFLOW_SIDE_EFFECTING


def _push_copies(ins, lands, send, recv, gather, sending):
    x, y, c, chips = _place()
    me = 2 * x + y
    out = []
    for a in range(len(lands)):
        for j, (px, py) in enumerate(chips):
            peer = 2 * px + py
            src = lands[a].at[me] if gather else ins[a].at[peer if sending else me]
            dst = lands[a].at[me if sending else peer]
            out.append(pltpu.make_async_remote_copy(src_ref=src, dst_ref=dst, send_sem=send.at[3 * a + j],
                                                    recv_sem=recv.at[3 * a + j], device_id=(px, py, c),
                                                    device_id_type=MESH))
    return out


def _push_start(name, arrs, gather):
    n = len(arrs)
    ops = list(arrs) if gather else list(arrs) + [lax.empty(s.shape, s.dtype) for s in arrs]
    m = len(ops)

    def body(*refs):
        ins, lnd = (refs[:n], refs[:n]) if gather else (refs[:n], refs[n:m])
        for cp in _push_copies(ins, lnd, refs[m], refs[m + 1], gather, True):
            cp.start()
        refs[-1][...] = jnp.zeros((8, 128), F32)

    ops = [pltpu.with_memory_space_constraint(t, pltpu.HBM) for t in ops]
    res = pl.pallas_call(
        body, name=name,
        out_shape=(pltpu.SemaphoreType.DMA((3 * n,)), pltpu.SemaphoreType.DMA((3 * n,)),
                   *[pltpu.HBM(t.shape, t.dtype) for t in ops], _sds((8, 128))),
        in_specs=[HBM] * m,
        out_specs=(SEMS, SEMS, *[HBM] * m, pl.BlockSpec(memory_space=pltpu.VMEM)),
        input_output_aliases={i: 2 + i for i in range(m)},
        compiler_params=pltpu.CompilerParams(has_side_effects=EFFECT))(*ops)
    return res[0], res[1], list(res[2:2 + m]), res[-1]


def _push_wait(name, started, after, gather):
    send, recv, ops, _ = started
    m = len(ops)
    n = m if gather else m // 2

    def body(*refs):
        ins, lnd = (refs[:n], refs[:n]) if gather else (refs[:n], refs[n:m])
        for cp in _push_copies(ins, lnd, refs[m], refs[m + 1], gather, False):
            cp.wait_send()
            cp.wait_recv()

    res = pl.pallas_call(
        body, name=name,
        out_shape=[pltpu.HBM(t.shape, t.dtype) for t in ops],
        in_specs=[HBM] * m + [SEMS, SEMS, ANY], out_specs=[HBM] * m,
        input_output_aliases={i: i for i in range(m)},
        compiler_params=pltpu.CompilerParams(has_side_effects=EFFECT))(*ops, send, recv, after)
    return list(res)


def _row_tile(rows):
    return max(t for t in range(8, min(rows, 512) + 1, 8) if rows % t == 0)


def _cast_to_slot(me, w, l):
    _, rows, cols = w.shape
    tr = _row_tile(rows)

    def body(me_ref, w_ref, o_ref):
        o_ref[0] = w_ref[0].astype(MX)

    return pl.pallas_call(
        body,
        grid_spec=pltpu.PrefetchScalarGridSpec(
            num_scalar_prefetch=1, grid=(rows // tr,),
            in_specs=[pl.BlockSpec((1, tr, cols), lambda i, me_: (l, i, 0))],
            out_specs=pl.BlockSpec((1, tr, cols), lambda i, me_: (me_[0], i, 0))),
        out_shape=_sds((NSHARD, rows, cols), MX), name="cast_to_slot", compiler_params=_cp(("arbitrary",)))(me, w)


def _sum_sources(me, recv, own):
    _, rows, cols = recv[0].shape
    tr = min(_row_tile(rows), 256) if rows % 256 == 0 else _row_tile(rows)
    nt = rows // tr

    def body(me_ref, *refs):
        o_ref = refs[-1]
        for l in range(DEPTH):
            @pl.when(pl.program_id(0) == l)
            def _():
                r_ref, own_ref = refs[2 * l], refs[2 * l + 1]
                part = [jnp.where(me_ref[0] == s, own_ref[0], r_ref[s]).astype(F32) for s in range(NSHARD)]
                o_ref[...] = ((part[0] + part[1]) + part[2]) + part[3]

    in_specs = []
    for l in range(DEPTH):
        pick = lambda g, i, me_, l=l: jnp.where(g == l, i, jnp.where(g < l, 0, nt - 1))
        in_specs += [pl.BlockSpec((NSHARD, tr, cols), lambda g, i, me_, pick=pick: (0, pick(g, i, me_), 0)),
                     pl.BlockSpec((1, tr, cols), lambda g, i, me_, pick=pick: (me_[0], pick(g, i, me_), 0))]
    return pl.pallas_call(
        body,
        grid_spec=pltpu.PrefetchScalarGridSpec(
            num_scalar_prefetch=1, grid=(DEPTH, nt), in_specs=in_specs,
            out_specs=pl.BlockSpec((tr, cols), lambda g, i, me_: (g * nt + i, 0))),
        out_shape=_sds((DEPTH * rows, cols)), name="sum_sources",
        compiler_params=_cp(("arbitrary", "arbitrary")))(me, *[t for l in range(DEPTH) for t in (recv[l], own[l])])


def _swap_sibling(arrs):
    n = len(arrs)

    def body(*refs):
        ins, outs = refs[:n], refs[n:2 * n]
        send, recv = refs[2 * n:]
        x, y, c, _ = _place()
        cps = [pltpu.make_async_remote_copy(src_ref=ins[a], dst_ref=outs[a], send_sem=send.at[a], recv_sem=recv.at[a],
                                            device_id=(x, y, 1 - c), device_id_type=MESH) for a in range(n)]
        for cp in cps:
            cp.start()
        for cp in cps:
            cp.wait()

    return pl.pallas_call(
        body, in_specs=[ANY] * n, out_specs=[ANY] * n, out_shape=[_sds(a.shape, a.dtype) for a in arrs],
        scratch_shapes=[pltpu.SemaphoreType.DMA((n,)), pltpu.SemaphoreType.DMA((n,))],
        name="swap_sibling")(*arrs)


def _allreduce_small(per_layer):
    nk = len(per_layer[0])
    n = DEPTH * nk
    shapes = [a.shape for a in per_layer[0]]

    def body(*refs):
        ins, outs = refs[:n], refs[n:n + nk]
        sibs, slots = refs[n + nk:n + 2 * nk], refs[n + 2 * nk:n + 3 * nk]
        send, recv = refs[n + 3 * nk:]
        x, y, c, chips = _place()
        me = 2 * x + y
        d2d = [pltpu.make_async_remote_copy(src_ref=ins[l * nk + k], dst_ref=sibs[k].at[l], send_sem=send.at[l * nk + k],
                                            recv_sem=recv.at[l * nk + k], device_id=(x, y, 1 - c), device_id_type=MESH)
               for l in range(DEPTH) for k in range(nk)]
        for cp in d2d:
            cp.start()
        for cp in d2d:
            cp.wait()
        for l in range(DEPTH):
            for k in range(nk):
                slots[k][me, l] = ins[l * nk + k][...] + sibs[k][l]

        def remote(k, j, slot):
            px, py = chips[j]
            return pltpu.make_async_remote_copy(src_ref=slots[k].at[me], dst_ref=slots[k].at[slot],
                                                send_sem=send.at[n + 3 * k + j], recv_sem=recv.at[n + 3 * k + j],
                                                device_id=(px, py, c), device_id_type=MESH)

        def handover(k):
            return pltpu.make_async_remote_copy(src_ref=outs[k], dst_ref=outs[k], send_sem=send.at[n + 3 * nk + k],
                                                recv_sem=recv.at[n + 3 * nk + k], device_id=(x, y, 1 - c),
                                                device_id_type=MESH)

        halves = (tuple(k for k in range(nk) if ICI_CORE[k] == 0), tuple(k for k in range(nk) if ICI_CORE[k] == 1))
        for cc in range(2):
            @pl.when(c == cc)
            def _():
                mine, theirs = halves[cc], halves[1 - cc]
                sends = [remote(k, j, me) for k in mine for j in range(3)]
                for cp in sends:
                    cp.start()
                for k in mine:
                    for j in range(3):
                        remote(k, j, 2 * chips[j][0] + chips[j][1]).wait_recv()
                for cp in sends:
                    cp.wait_send()
                for k in mine:
                    outs[k][...] = ((slots[k][0] + slots[k][1]) + slots[k][2]) + slots[k][3]
                over = [handover(k) for k in mine]
                for cp in over:
                    cp.start()
                for k in theirs:
                    handover(k).wait_recv()
                for cp in over:
                    cp.wait_send()

    vm = pl.BlockSpec(memory_space=pltpu.VMEM)
    return pl.pallas_call(
        body, in_specs=[vm] * n, out_specs=[vm] * nk, out_shape=[_sds((DEPTH,) + s) for s in shapes],
        scratch_shapes=([pltpu.VMEM((DEPTH,) + s, F32) for s in shapes]
                        + [pltpu.VMEM((NSHARD, DEPTH) + s, F32) for s in shapes]
                        + [pltpu.SemaphoreType.DMA((n + 4 * nk,)), pltpu.SemaphoreType.DMA((n + 4 * nk,))]),
        name="allreduce_small", compiler_params=pltpu.CompilerParams(vmem_limit_bytes=VMEM_LIMIT))(
            *[a for layer in per_layer for a in layer])


def _adamw_math(w, g, m, v):
    m = ADAM_B1 * m + (1.0 - ADAM_B1) * g
    v = ADAM_B2 * v + (1.0 - ADAM_B2) * jnp.square(g)
    m_hat = m / (1.0 - ADAM_B1 ** ADAM_STEP)
    v_hat = v / (1.0 - ADAM_B2 ** ADAM_STEP)
    delta = -ADAM_LR * (m_hat / (jnp.sqrt(v_hat) + ADAM_EPS) + ADAM_WD * w)
    return delta, m, v


def _adamw(g_parts, w, m, v):
    rows, cols = w.shape
    tr = 256 if rows % 256 == 0 else _row_tile(rows)
    k = len(g_parts)

    def body(*refs):
        g = refs[0][...]
        for r in refs[1:k]:
            g = g + r[...]
        w_ref, m_ref, v_ref, go, do, mo, vo = refs[k:]
        d, mn, vn = _adamw_math(w_ref[...], g, m_ref[...], v_ref[...])
        go[...] = g
        do[...] = d
        mo[...] = mn
        vo[...] = vn

    spec = pl.BlockSpec((tr, cols), lambda i: (i, 0))
    return pl.pallas_call(
        body, grid=(rows // tr,), in_specs=[spec] * (k + 3), out_specs=[spec] * 4,
        out_shape=[_sds((rows, cols))] * 4, name="adamw", compiler_params=_cp(("parallel",)))(*g_parts, w, m, v)


def _adamw_small(gs, ws, ms, vs):
    n = len(gs)

    def body(*refs):
        for k in range(n):
            d, mn, vn = _adamw_math(refs[n + k][...], refs[k][...], refs[2 * n + k][...], refs[3 * n + k][...])
            refs[4 * n + k][...] = d
            refs[5 * n + k][...] = mn
            refs[6 * n + k][...] = vn

    vm = pl.BlockSpec(memory_space=pltpu.VMEM)
    shapes = [_sds(a.shape) for a in ws]
    res = pl.pallas_call(
        body, in_specs=[vm] * (4 * n), out_specs=[vm] * (3 * n), out_shape=shapes * 3, name="adamw_small",
        compiler_params=pltpu.CompilerParams(vmem_limit_bytes=VMEM_LIMIT))(*gs, *ws, *ms, *vs)
    return res[:n], res[n:2 * n], res[2 * n:]


_ARGS = ("x", "w_in", "s5_a_re", "s5_a_im", "s5_log_step", "s5_b_re", "s5_b_im", "s5_c_re", "s5_c_im", "s5_d",
         "s5_w_glu", "s5_b_glu", "gla_w_a", "gla_b_a", "gla_ln_g", "swa_sink", "w_out", "ln1_g", "ln1_b", "w_ff1",
         "w_ff2", "ln2_g", "ln2_b")
_WEIGHTS = _ARGS[1:]


def _shard_cols(d):
    return d.reshape(d.shape[0], NSHARD, d.shape[1] // NSHARD).transpose(1, 0, 2)


def kernel(x, w_in, s5_a_re, s5_a_im, s5_log_step, s5_b_re, s5_b_im, s5_c_re, s5_c_im, s5_d, s5_w_glu, s5_b_glu, gla_w_a, gla_b_a, gla_ln_g, swa_sink, w_out, ln1_g, ln1_b, w_ff1, w_ff2, ln2_g, ln2_b, loss_target, m_w_in, m_s5_a_re, m_s5_a_im, m_s5_log_step, m_s5_b_re, m_s5_b_im, m_s5_c_re, m_s5_c_im, m_s5_d, m_s5_w_glu, m_s5_b_glu, m_gla_w_a, m_gla_b_a, m_gla_ln_g, m_swa_sink, m_w_out, m_ln1_g, m_ln1_b, m_w_ff1, m_w_ff2, m_ln2_g, m_ln2_b, v_w_in, v_s5_a_re, v_s5_a_im, v_s5_log_step, v_s5_b_re, v_s5_b_im, v_s5_c_re, v_s5_c_im, v_s5_d, v_s5_w_glu, v_s5_b_glu, v_gla_w_a, v_gla_b_a, v_gla_ln_g, v_swa_sink, v_w_out, v_ln1_g, v_ln1_b, v_w_ff1, v_w_ff2, v_ln2_g, v_ln2_b):
    given = dict(locals())
    w = {k: given[k] for k in _WEIGHTS}
    mom = {k: given["m_" + k] for k in _WEIGHTS}
    var = {k: given["v_" + k] for k in _WEIGHTS}

    me = (2 * lax.axis_index("x") + lax.axis_index("y")).astype(jnp.int32).reshape(1)
    tr = lambda t: t.transpose(0, 2, 1)
    shard = {k: (tr(w[k]) if k == "w_in" else w[k]) for k in BIG}
    qs = [_layer_prep({k: w[k][l] for k in SMALL}) for l in range(DEPTH)]

    first = ("w_in", "s5_w_glu", "w_out")
    follow = {(0, "w_in"): [(0, BIG[3:]), (1, first)], (0, "w_ff1"): [(1, BIG[3:])]}
    gathers = {}

    def start_gather(l, names, behind=None):
        lands = [_cast_to_slot(me, shard[k], l) for k in names]
        if behind is not None:
            lands, behind = lax.optimization_barrier((lands, behind))
        st = _push_start(f"gather_start_{l}_{names[0]}", lands, True)
        for k in names:
            gathers[l, k] = [names, st, None]
        return st[-1], behind

    token = start_gather(0, first[:1])[0] + start_gather(0, first[1:])[0]

    def fetch(l, name, after):
        names, st, got = gathers[l, name]
        tie = None
        if got is None:
            if l == 0 and name == "w_in":
                after = token
            lands = _push_wait(f"gather_wait_{l}_{names[0]}", st, after, True)
            for l2, names2 in follow.get((l, name), ()):
                tok, lands[0] = start_gather(l2, names2, lands[0])
                tie = tok if tie is None else tie + tok
            got = dict(zip(names, lands))
            for k in names:
                gathers[l, k][2] = got
        full = got[name]
        if name == "w_in":
            return _in_rows(full, token if tie is None else tie)
        if tie is not None:
            qs[l]["ln2_b"] = qs[l]["ln2_b"] + tie[0, 0]
        return full.reshape(D, D) if name == "w_out" else full

    scatters = []

    def emit(l, grads):
        names = tuple(grads)
        st = _push_start(f"scatter_start_{l}_{names[0]}", [grads[k] for k in names], False)
        scatters.append((l, names, st))
        return st[-1][0, 0]

    loss, dx, smalls = _local_step(x.reshape(N, D), loss_target.reshape(N, D), qs, _rope_tables(128), fetch, emit)

    out = {}
    native = _allreduce_small([[smalls[l][k] for k in NATIVE] for l in range(DEPTH)])
    native = dict(zip(NATIVE, native))
    loss = native["loss"][0, 0, 0] + native["loss"][1, 0, 0]
    gsmall = _finish_small(native, w)
    res = _adamw_small(*([t[k] for k in SMALL] for t in (gsmall, w, mom, var)))
    for i, k in enumerate(SMALL):
        out[k] = [gsmall[k], res[0][i], res[1][i], res[2][i]]

    recv, own = {}, {}

    def finish(keys, after):
        for l, names, st in scatters:
            if names[0] in keys:
                ops = _push_wait(f"scatter_wait_{l}_{names[0]}", st, after, False)
                for i, k in enumerate(names):
                    own[l, k], recv[l, k] = ops[i], ops[len(names) + i]
        sums = [_sum_sources(me, [recv[l, k] for l in range(DEPTH)], [own[l, k] for l in range(DEPTH)]) for k in keys]
        for k, mine, other in zip(keys, sums, _swap_sibling(sums)):
            shp = shard[k].shape
            r = _adamw([mine, other], *((tr(t[k]) if k == "w_in" else t[k]).reshape(-1, shp[-1]) for t in (w, mom, var)))
            r = [t.reshape(shp) for t in r]
            out[k] = [tr(t) for t in r] if k == "w_in" else r
        return out[keys[-1]][1]

    last = finish(("w_ff1", "w_ff2", "w_out", "s5_w_glu"), res[0][-1])
    finish(("w_in",), last)

    return (loss, dx.reshape(NSEQ, L, D), *[out[k][0] for k in _WEIGHTS], *[out[k][1] for k in _WEIGHTS],
            *[out[k][2] for k in _WEIGHTS], *[out[k][3] for k in _WEIGHTS])
```

```python
import functools
import math

import jax
import jax.numpy as jnp
from jax import lax
from jax.experimental import pallas as pl
from jax.experimental.pallas import tpu as pltpu

F32 = jnp.float32
MX = jnp.bfloat16
MESH = pl.DeviceIdType.MESH

DEPTH = 2
NSEQ = 2
L = 2048
N = NSEQ * L
D = 1024
DFF = 4096
NSHARD = 4
S5_G, S5_H, S5_P = 16, 16, 64
GLA_CHUNK = 64
NCHUNK = L // GLA_CHUNK
GLA_GROUP = 4
NGROUP = NCHUNK // GLA_GROUP
SWA_BLK = 128
NBLK = L // SWA_BLK
ROT = 16
ROPE_THETA = 500000.0
LN_EPS = 1e-5
ALPHA = (2 * DEPTH) ** 0.25
NEG_BIG = -1e30
DIN = 1824
DINP = 1920
ADAM_LR, ADAM_B1, ADAM_B2, ADAM_EPS, ADAM_WD, ADAM_STEP = 0.001, 0.9, 0.999, 1e-08, 0.01, 10
VMEM_LIMIT = 56 * 1024 * 1024
TT = 512
SW = 512
FFN_TM = 1024
FFN_TM_W = 1024
FFN_VMEM = 60 * 1024 * 1024
INPROJ_BWD_TM = 512


def _cp(sem, vmem=VMEM_LIMIT):
    return pltpu.CompilerParams(dimension_semantics=sem, vmem_limit_bytes=vmem)


def _mm(a, b):
    return jnp.dot(a.astype(MX), b.astype(MX), preferred_element_type=F32)


def _mm_nt(a, b):
    return lax.dot_general(a.astype(MX), b.astype(MX), (((1,), (1,)), ((), ())), preferred_element_type=F32)


def _mm_tn(a, b):
    return lax.dot_general(a.astype(MX), b.astype(MX), (((0,), (0,)), ((), ())), preferred_element_type=F32)


@jax.custom_vjp
def _dmm(a, b):
    return _mm(a, b)


_dmm.defvjp(lambda a, b: (_mm(a, b), (a, b)), lambda r, g: (_mm_nt(g, r[1]), _mm_tn(r[0], g)))


@jax.custom_vjp
def _dmm_nt(a, b):
    return _mm_nt(a, b)


_dmm_nt.defvjp(lambda a, b: (_mm_nt(a, b), (a, b)), lambda r, g: (_mm(g, r[1]), _mm_tn(g, r[0])))


@jax.custom_vjp
def _dmm_tn(a, b):
    return _mm_tn(a, b)


_dmm_tn.defvjp(lambda a, b: (_mm_tn(a, b), (a, b)), lambda r, g: (_mm_nt(r[1], g), _mm(r[0], g)))


def _split3(x):
    hi = x.astype(MX)
    r1 = x - hi.astype(F32)
    mid = r1.astype(MX)
    lo = (r1 - mid.astype(F32)).astype(MX)
    return hi, mid, lo


def _chunk_pairs(rows, rev, strict):
    r = lax.broadcasted_iota(jnp.int32, (rows, rows), 0)
    c = lax.broadcasted_iota(jnp.int32, (rows, rows), 1)
    order = ((c > r) if strict else (c >= r)) if rev else ((c < r) if strict else (c <= r))
    return (r // GLA_CHUNK == c // GLA_CHUNK) & order


def _cums_impl(x, rev):
    rows, w = x.shape
    t = jnp.where(_chunk_pairs(rows, rev, False), 1.0, 0.0).astype(MX)
    s = jnp.dot(t, jnp.concatenate(_split3(x), axis=1), preferred_element_type=F32)
    return s[:, 0:w] + s[:, w:2 * w] + s[:, 2 * w:3 * w]


@functools.partial(jax.custom_vjp, nondiff_argnums=(1,))
def _cums(x, rev):
    return _cums_impl(x, rev)


_cums.defvjp(lambda x, rev: (_cums_impl(x, rev), None), lambda rev, r, g: (_cums_impl(g, not rev),))


def _ln_fwd(s, g, b):
    mu = jnp.mean(s, axis=-1, keepdims=True)
    xc = s - mu
    var = jnp.mean(xc * xc, axis=-1, keepdims=True)
    return xc * lax.rsqrt(var + LN_EPS) * g + b


def _ln_bwd(dy, s, g):
    mu = jnp.mean(s, axis=-1, keepdims=True)
    xc = s - mu
    var = jnp.mean(xc * xc, axis=-1, keepdims=True)
    rstd = lax.rsqrt(var + LN_EPS)
    xhat = xc * rstd
    dxh = dy * g
    ds = rstd * (dxh - jnp.mean(dxh, axis=-1, keepdims=True) - xhat * jnp.mean(dxh * xhat, axis=-1, keepdims=True))
    return ds, jnp.sum(dy * xhat, axis=0, keepdims=True), jnp.sum(dy, axis=0, keepdims=True)


def _sds(shape, dtype=F32):
    return jax.ShapeDtypeStruct(shape, dtype)


_IN_ROW_PIECES = (((0, 0), (0, 456)), ((1, 0), (456, 456)), ((2, 0), (912, 112)), ((2, 112), (1792, 32)),
                  ((2, 144), (1024, 312)), ((3, 0), (1336, 456)))


def _in_rows(g4, behind):
    def body(g_ref, behind_ref, o_ref, tmp):
        tmp[DIN:DINP] = jnp.zeros((DINP - DIN, D), F32)
        for (j, s0), (d0, n_) in _IN_ROW_PIECES:
            tmp[d0:d0 + n_] = g_ref[j, s0:s0 + n_].astype(F32)
        o_ref[...] = tmp[...].astype(MX)

    vm = pl.BlockSpec(memory_space=pltpu.VMEM)
    return pl.pallas_call(body, in_specs=[vm, pl.BlockSpec(memory_space=pl.ANY)], out_specs=vm,
                          out_shape=_sds((DINP, D), MX), scratch_shapes=[pltpu.VMEM((DINP, D), F32)], name="in_rows",
                          compiler_params=pltpu.CompilerParams(vmem_limit_bytes=VMEM_LIMIT))(g4, behind)


def _inproj_fwd(x, wt):
    tm = 512

    def body(x_ref, w_ref, h_ref):
        h_ref[...] = _mm_nt(x_ref[...], w_ref[...])

    return pl.pallas_call(
        body, grid=(N // tm,),
        in_specs=[pl.BlockSpec((tm, D), lambda i: (i, 0)), pl.BlockSpec((DINP, D), lambda i: (0, 0))],
        out_specs=pl.BlockSpec((tm, DINP), lambda i: (i, 0)),
        out_shape=_sds((N, DINP)), name="inproj_fwd", compiler_params=_cp(("parallel",)))(x, wt)


def _inproj_bwd(x, w, dxp, du2, dud, gq_f, gq_b, gk_f, gk_b, gv_f, gv_b, gr, daq, dakv, dhl):
    tm = INPROJ_BWD_TM
    nt = N // tm

    def body(x_ref, w_ref, dxp_ref, du2_ref, dud_ref, gqf, gqb, gkf, gkb, gvf, gvb, gr_ref, daq_ref, dakv_ref, dhl_ref,
             dx_ref, dw_ref, acc):
        i = pl.program_id(0)
        dh = jnp.concatenate([
            du2_ref[0] + du2_ref[1] + dud_ref[...], gqf[...] + gqb[...], gkf[...] + gkb[...], gvf[...] + gvb[...],
            gr_ref[...], daq_ref[...], dakv_ref[...], dhl_ref[...]], axis=1)
        dx_ref[...] = dxp_ref[...] + _mm(dh, w_ref[...])
        contrib = _mm_tn(dh, x_ref[...])

        @pl.when(i == 0)
        def _():
            acc[...] = contrib

        @pl.when(i > 0)
        def _():
            acc[...] += contrib

        @pl.when(i == nt - 1)
        def _():
            for (j, d0), (s0, n_) in _IN_ROW_PIECES:
                dw_ref[j, d0:d0 + n_] = acc[s0:s0 + n_].astype(MX)

    row = lambda w_: pl.BlockSpec((tm, w_), lambda i: (i, 0))
    return pl.pallas_call(
        body, grid=(nt,),
        in_specs=[row(D), pl.BlockSpec((DINP, D), lambda i: (0, 0)), row(D),
                  pl.BlockSpec((2, tm, 256), lambda i: (0, i, 0)), row(256), row(128), row(128), row(128), row(128),
                  row(256), row(256), row(256), row(512), row(256), row(128)],
        out_specs=[row(D), pl.BlockSpec((NSHARD, DIN // NSHARD, D), lambda i: (0, 0, 0))],
        out_shape=[_sds((N, D)), _sds((NSHARD, DIN // NSHARD, D), MX)],
        scratch_shapes=[pltpu.VMEM((DINP, D), F32)],
        name="inproj_bwd", compiler_params=_cp(("arbitrary",)))(
            x, w, dxp, du2, dud, gq_f, gq_b, gk_f, gk_b, gv_f, gv_b, gr, daq, dakv, dhl)


def _scan_tables(mr, mi, reverse):
    pw = [(mr, mi)]
    for _ in range(7):
        pr, pi = pw[-1]
        pw.append((pr * mr - pi * mi, pr * mi + pi * mr))
    rows = jnp.arange(8)[:, None]
    out = []
    for d in (1, 2, 4):
        keep = rows >= d
        out += [jnp.where(keep, pw[d - 1][0][None], 0.0), jnp.where(keep, pw[d - 1][1][None], 0.0)]
    out += [jnp.stack([p[0] for p in pw]), jnp.stack([p[1] for p in pw])]
    t = jnp.stack(out)
    if reverse:
        t = t[:, ::-1, :]
    return t.reshape(8, 8, 2, SW).transpose(2, 0, 1, 3)


def _tile_scan(xr, xi, a, cr, ci, reverse):
    for lvl, d in enumerate((1, 2, 4)):
        sh = 8 - d if reverse else d
        sr = pltpu.roll(xr, sh, 0)
        si = pltpu.roll(xi, sh, 0)
        ar, ai = a[2 * lvl], a[2 * lvl + 1]
        xr, xi = xr + ar * sr - ai * si, xi + ar * si + ai * sr
    pr, pi = a[6], a[7]
    return xr + pr * cr - pi * ci, xi + pr * ci + pi * cr


NJ = TT // 8


def _lockstep_tables(mr, mi, reverse):
    nr, ni = mr, mi
    for _ in range(NJ.bit_length() - 1):
        nr, ni = nr * nr - ni * ni, 2.0 * nr * ni
    pr, pi = mr[None], mi[None]
    while pr.shape[0] < NJ:
        k = pr.shape[0]
        tr, ti = pr[k - 1], pi[k - 1]
        pr, pi = (jnp.concatenate([pr, pr * tr - pi * ti]), jnp.concatenate([pi, pr * ti + pi * tr]))
    if reverse:
        pr, pi = pr[::-1], pi[::-1]
    rows = jnp.broadcast_to(jnp.stack([mr, mi])[:, None, :], (2, 8, 2 * SW))
    link = _scan_tables(nr, ni, reverse)
    a = jnp.concatenate([rows.reshape(2, 8, 2, SW).transpose(2, 0, 1, 3), link], axis=1)
    return a, jnp.stack([pr, pi]).reshape(2, NJ, 2, SW).transpose(2, 0, 1, 3)


def _to_lockstep(ref, *lead):
    return jnp.concatenate([ref[(*lead, pl.ds(j, 8, stride=NJ), slice(None))] for j in range(NJ)], axis=0)


def _from_lockstep(val, ref, *lead):
    for j in range(NJ):
        ref[(*lead, pl.ds(j, 8, stride=NJ), slice(None))] = val[8 * j:8 * j + 8]


def _expand_powers(p_ref, pexp):
    for c in range(2):
        for j in range(NJ):
            pexp[c, j] = jnp.broadcast_to(p_ref[0, 0, c, j:j + 1, :], (8, SW))


def _lockstep_scan(xre, xim, a_ref, pexp, car, reverse, extra=None):
    a = [a_ref[0, 0, k] for k in range(10)]
    mr, mi = a[0], a[1]
    order = (lambda i: NJ - 1 - i) if reverse else (lambda i: i)

    def local(i, hcar):
        hr, hi = hcar
        r0 = pl.multiple_of(order(i) * 8, 8)
        hr, hi = mr * hr - mi * hi + xre[pl.ds(r0, 8), :], mr * hi + mi * hr + xim[pl.ds(r0, 8), :]
        xre[pl.ds(r0, 8), :] = hr
        xim[pl.ds(r0, 8), :] = hi
        return hr, hi

    z8 = jnp.zeros((8, SW), F32)
    er, ei = lax.fori_loop(0, NJ, local, (z8, z8), unroll=4)
    c0r, c0i = car[0], car[1]
    er, ei = _tile_scan(er, ei, a[2:], c0r, c0i, reverse)
    rowid = lax.broadcasted_iota(jnp.int32, (8, SW), 0)
    first, sh, last = (7, 7, 0) if reverse else (0, 1, 7)
    cvr = jnp.where(rowid == first, c0r, pltpu.roll(er, sh, 0))
    cvi = jnp.where(rowid == first, c0i, pltpu.roll(ei, sh, 0))
    car[0] = jnp.broadcast_to(er[last:last + 1, :], (8, SW))
    car[1] = jnp.broadcast_to(ei[last:last + 1, :], (8, SW))

    def fix(i, carry):
        j = order(i)
        r0 = pl.multiple_of(j * 8, 8)
        pr, pi = pexp[0, j], pexp[1, j]
        sr = xre[pl.ds(r0, 8), :] + pr * cvr - pi * cvi
        si = xim[pl.ds(r0, 8), :] + pr * cvi + pi * cvr
        xre[pl.ds(r0, 8), :] = sr
        xim[pl.ds(r0, 8), :] = si
        if extra is None:
            return carry
        return (sr, si, extra(r0, sr, si, carry[0], carry[1], carry[2]))

    init = (cvr, cvi, extra(None, None, None, None, None, None)) if extra is not None else 0
    return lax.fori_loop(0, NJ, fix, init, unroll=4)


def _s5_time_block(z, s, t, adjoint):
    flip = (1 - z) if adjoint else z
    return s * (L // TT) + t + flip * (L // TT - 1 - 2 * t)


def _s5_fwd(h, bre, bim, cre, cim, tab):
    nt = L // TT
    taba, tabp = tab

    def body(u_ref, bre_ref, bim_ref, cre_ref, cim_ref, a_ref, p_ref, hre_ref, him_ref, y_ref, car, pexp):
        z = pl.program_id(1)
        s = pl.program_id(2)
        tc = pl.program_id(3)

        @pl.when(tc == 0)
        def _():
            car[...] = jnp.zeros_like(car)

        @pl.when((tc == 0) & (s == 0))
        def _():
            _expand_powers(p_ref, pexp)

        u = _to_lockstep(u_ref)
        hre_ref[0] = _mm(u, bre_ref[0, 0])
        him_ref[0] = _mm(u, bim_ref[0, 0])

        @pl.when(z == 0)
        def _():
            _lockstep_scan(hre_ref.at[0], him_ref.at[0], a_ref, pexp, car, False)

        @pl.when(z == 1)
        def _():
            _lockstep_scan(hre_ref.at[0], him_ref.at[0], a_ref, pexp, car, True)

        _from_lockstep(_mm(hre_ref[0], cre_ref[0, 0]) - _mm(him_ref[0], cim_ref[0, 0]), y_ref, 0)

    tb = lambda b, z, s, t: _s5_time_block(z, s, t, False)
    wspec = lambda r, c: pl.BlockSpec((1, 1, r, c), lambda b, z, s, t: (z, b, 0, 0))
    return pl.pallas_call(
        body, grid=(2, 2, NSEQ, nt),
        in_specs=[pl.BlockSpec((TT, 128), lambda b, z, s, t: (tb(b, z, s, t), b)),
                  wspec(128, SW), wspec(128, SW), wspec(SW, 128), wspec(SW, 128),
                  pl.BlockSpec((1, 1, 10, 8, SW), lambda b, z, s, t: (z, b, 0, 0, 0)),
                  pl.BlockSpec((1, 1, 2, NJ, SW), lambda b, z, s, t: (z, b, 0, 0, 0))],
        out_specs=[pl.BlockSpec((1, TT, SW), lambda b, z, s, t: (z, tb(b, z, s, t), b)),
                   pl.BlockSpec((1, TT, SW), lambda b, z, s, t: (z, tb(b, z, s, t), b)),
                   pl.BlockSpec((1, TT, 128), lambda b, z, s, t: (z, tb(b, z, s, t), b))],
        out_shape=[_sds((2, N, 2 * SW)), _sds((2, N, 2 * SW)), _sds((2, N, 256))],
        scratch_shapes=[pltpu.VMEM((2, 8, SW), F32), pltpu.VMEM((2, NJ, 8, SW), F32)],
        name="s5_fwd", compiler_params=_cp(("arbitrary",) * 4))(h, bre, bim, cre, cim, taba, tabp)


def _s5_bwd(h, dyp, hre, him, bre, bim, cre, cim, tabc):
    nt = L // TT
    taba, tabp = tabc

    def body(u_ref, dy_ref, hre_ref, him_ref, bre_ref, bim_ref, cre_ref, cim_ref, a_ref, p_ref,
             du_ref, dbre_ref, dbim_ref, dcre_ref, dcim_ref, dmu_ref, gre, gim, car, acc, macc, pexp):
        z = pl.program_id(1)
        s = pl.program_id(2)
        tc = pl.program_id(3)

        @pl.when(tc == 0)
        def _():
            car[...] = jnp.zeros_like(car)

        @pl.when((tc == 0) & (s == 0))
        def _():
            acc[...] = jnp.zeros_like(acc)
            macc[...] = jnp.zeros_like(macc)
            _expand_powers(p_ref, pexp)

        dy = _to_lockstep(dy_ref)
        gre[...] = _mm_nt(dy, cre_ref[0, 0])
        gim[...] = -_mm_nt(dy, cim_ref[0, 0])

        def run(reverse):
            def pair(r0, gr_, gi_, pvr, pvi, m):
                if r0 is None:
                    return (macc[0], macc[1])
                hr = hre_ref[0, pl.ds(r0, 8), :]
                hi = him_ref[0, pl.ds(r0, 8), :]
                return (m[0] + pvr * hr + pvi * hi, m[1] + pvi * hr - pvr * hi)

            _, _, (dmr, dmi) = _lockstep_scan(gre, gim, a_ref, pexp, car, reverse, pair)
            macc[0] = dmr
            macc[1] = dmi

        @pl.when(z == 0)
        def _():
            run(True)

        @pl.when(z == 1)
        def _():
            run(False)

        gr = gre[...]
        gi = gim[...]
        u = _to_lockstep(u_ref)
        _from_lockstep(_mm_nt(gr, bre_ref[0, 0]) + _mm_nt(gi, bim_ref[0, 0]), du_ref, 0)
        acc[0] += _mm_tn(u, gr)
        acc[1] += _mm_tn(u, gi)
        acc[2] += _mm_tn(dy, hre_ref[0])
        acc[3] -= _mm_tn(dy, him_ref[0])

        @pl.when((tc == nt - 1) & (s == NSEQ - 1))
        def _():
            grp = lax.broadcasted_iota(jnp.int32, (S5_H, SW), 1) // S5_P
            for k, out in enumerate((dbre_ref, dbim_ref, dcre_ref, dcim_ref)):
                c = jnp.zeros((S5_H, SW), F32)
                for i in range(8):
                    c = c + jnp.where(grp == i, acc[k, i * S5_H:(i + 1) * S5_H, :], 0.0)
                out[0, 0] = c
            dmu_ref[0, 0] = jnp.concatenate([jnp.sum(macc[0], axis=0, keepdims=True),
                                             jnp.sum(macc[1], axis=0, keepdims=True)], axis=0)

    tb = lambda b, z, s, t: _s5_time_block(z, s, t, True)
    wspec = lambda r, c: pl.BlockSpec((1, 1, r, c), lambda b, z, s, t: (z, b, 0, 0))
    tok = lambda w_: pl.BlockSpec((TT, w_), lambda b, z, s, t: (tb(b, z, s, t), b))
    st = pl.BlockSpec((1, TT, SW), lambda b, z, s, t: (z, tb(b, z, s, t), b))
    return pl.pallas_call(
        body, grid=(2, 2, NSEQ, nt),
        in_specs=[tok(128), tok(128), st, st, wspec(128, SW), wspec(128, SW), wspec(SW, 128), wspec(SW, 128),
                  pl.BlockSpec((1, 1, 10, 8, SW), lambda b, z, s, t: (z, b, 0, 0, 0)),
                  pl.BlockSpec((1, 1, 2, NJ, SW), lambda b, z, s, t: (z, b, 0, 0, 0))],
        out_specs=[pl.BlockSpec((1, TT, 128), lambda b, z, s, t: (z, tb(b, z, s, t), b)),
                   wspec(S5_H, SW), wspec(S5_H, SW), wspec(S5_H, SW), wspec(S5_H, SW),
                   wspec(2, SW)],
        out_shape=[_sds((2, N, 256))] + [_sds((2, 2, S5_H, SW))] * 4 + [_sds((2, 2, 2, SW))],
        scratch_shapes=[pltpu.VMEM((TT, SW), F32), pltpu.VMEM((TT, SW), F32), pltpu.VMEM((2, 8, SW), F32),
                        pltpu.VMEM((4, 128, SW), F32), pltpu.VMEM((2, 8, SW), F32), pltpu.VMEM((2, NJ, 8, SW), F32)],
        name="s5_bwd", compiler_params=_cp(("arbitrary",) * 4))(h, dyp, hre, him, bre, bim, cre, cim, taba, tabp)


_GELU_C = math.sqrt(2.0 / math.pi)


def _gelu(y):
    return 0.5 * y * (1.0 + jnp.tanh(_GELU_C * (y + 0.044715 * y * y * y)))


def _gelu_grad(y):
    t = jnp.tanh(_GELU_C * (y + 0.044715 * y * y * y))
    return 0.5 * (1.0 + t) + 0.5 * y * (1.0 - t * t) * _GELU_C * (1.0 + 3 * 0.044715 * y * y)


def _glu_halves(w4_ref):
    return (jnp.concatenate([w4_ref[0], w4_ref[1]], axis=1), jnp.concatenate([w4_ref[2], w4_ref[3]], axis=1))


def _s5_glu_fwd(y2, h, dsk, w4, bv, bg):
    tm = 512

    def body(y2_ref, u_ref, d_ref, w4_ref, bv_ref, bg_ref, ya_ref):
        wv, wg = _glu_halves(w4_ref)
        z = _gelu(y2_ref[0] + y2_ref[1] + d_ref[...] * u_ref[...])
        val = _mm(z, wv) + bv_ref[...]
        gate = _mm(z, wg) + bg_ref[...]
        ya_ref[...] = (val * jax.nn.sigmoid(gate)).astype(MX)

    full = lambda r, c: pl.BlockSpec((r, c), lambda i: (0, 0))
    return pl.pallas_call(
        body, grid=(N // tm,),
        in_specs=[pl.BlockSpec((2, tm, 256), lambda i: (0, i, 0)), pl.BlockSpec((tm, 256), lambda i: (i, 0)),
                  full(1, 256), pl.BlockSpec((NSHARD, 256, 128), lambda i: (0, 0, 0)), full(1, 256), full(1, 256)],
        out_specs=pl.BlockSpec((tm, 256), lambda i: (i, 0)),
        out_shape=_sds((N, 256), MX), name="s5_glu_fwd", compiler_params=_cp(("parallel",)))(y2, h, dsk, w4, bv, bg)


def _s5_glu_bwd(y2, h, dsk, w4, bv, bg, dya):
    tm = 512
    nt = N // tm

    def body(y2_ref, u_ref, d_ref, w4_ref, bv_ref, bg_ref, dya_ref,
             dyp_ref, dud_ref, dd_ref, dw4_ref, dbv_ref, dbg_ref, accv, accg):
        i = pl.program_id(0)

        @pl.when(i == 0)
        def _():
            for r in (dd_ref, accv, accg, dbv_ref, dbg_ref):
                r[...] = jnp.zeros_like(r)

        wv, wg = _glu_halves(w4_ref)
        u = u_ref[...]
        y = y2_ref[0] + y2_ref[1] + d_ref[...] * u
        z = _gelu(y)
        val = _mm(z, wv) + bv_ref[...]
        sig = jax.nn.sigmoid(_mm(z, wg) + bg_ref[...])
        dya = dya_ref[...]
        dval = dya * sig
        dgate = dya * val * sig * (1.0 - sig)
        dz = _mm_nt(dval, wv) + _mm_nt(dgate, wg)
        dy = dz * _gelu_grad(y)
        dyp_ref[...] = dy
        dud_ref[...] = dy * d_ref[...]
        dd_ref[...] += jnp.sum(dy * u, axis=0, keepdims=True)
        accv[...] += _mm_tn(z, dval)
        accg[...] += _mm_tn(z, dgate)
        dbv_ref[...] += jnp.sum(dval, axis=0, keepdims=True)
        dbg_ref[...] += jnp.sum(dgate, axis=0, keepdims=True)

        @pl.when(i == nt - 1)
        def _():
            dw4_ref[0] = accv[:, 0:128].astype(MX)
            dw4_ref[1] = accv[:, 128:256].astype(MX)
            dw4_ref[2] = accg[:, 0:128].astype(MX)
            dw4_ref[3] = accg[:, 128:256].astype(MX)

    full = lambda r, c: pl.BlockSpec((r, c), lambda i: (0, 0))
    row = pl.BlockSpec((tm, 256), lambda i: (i, 0))
    wspec = pl.BlockSpec((NSHARD, 256, 128), lambda i: (0, 0, 0))
    return pl.pallas_call(
        body, grid=(nt,),
        in_specs=[pl.BlockSpec((2, tm, 256), lambda i: (0, i, 0)), row, full(1, 256), wspec, full(1, 256), full(1, 256),
                  row],
        out_specs=[row, row, full(1, 256), wspec, full(1, 256), full(1, 256)],
        out_shape=[_sds((N, 256)), _sds((N, 256)), _sds((1, 256)), _sds((NSHARD, 256, 128), MX), _sds((1, 256)),
                   _sds((1, 256))],
        scratch_shapes=[pltpu.VMEM((256, 256), F32), pltpu.VMEM((256, 256), F32)],
        name="s5_glu_bwd", compiler_params=_cp(("arbitrary",)))(y2, h, dsk, w4, bv, bg, dya)


def _logsig(x):
    return jnp.minimum(x, 0.0) - jnp.log(1.0 + jnp.exp(-jnp.abs(x)))


def _gla_gate_fwd(h, wa, ba):
    tm = 512

    def body(hl_ref, wa_ref, ba_ref, la_ref):
        la_ref[...] = _logsig(_mm(hl_ref[...], wa_ref[...]) + ba_ref[...]) * (1.0 / 16.0)

    return pl.pallas_call(
        body, grid=(N // tm,),
        in_specs=[pl.BlockSpec((tm, 128), lambda i: (i, 14)), pl.BlockSpec((128, 256), lambda i: (0, 0)),
                  pl.BlockSpec((1, 256), lambda i: (0, 0))],
        out_specs=pl.BlockSpec((tm, 256), lambda i: (i, 0)),
        out_shape=_sds((N, 256)), name="gla_gate_fwd", compiler_params=_cp(("parallel",)))(h, wa, ba)


def _gla_gate_bwd(h, wa, ba, dla_f, dla_b):
    tm = 512

    def body(hl_ref, wa_ref, ba_ref, df_ref, db_ref, dhl_ref, dwa_ref, dba_ref):
        i = pl.program_id(0)

        @pl.when(i == 0)
        def _():
            dwa_ref[...] = jnp.zeros_like(dwa_ref)
            dba_ref[...] = jnp.zeros_like(dba_ref)

        hl = hl_ref[...]
        pre = _mm(hl, wa_ref[...]) + ba_ref[...]
        dpre = jnp.concatenate([df_ref[...], db_ref[...]], axis=1) * (1.0 / 16.0) * jax.nn.sigmoid(-pre)
        dhl_ref[...] = _mm_nt(dpre, wa_ref[...])
        dwa_ref[...] += _mm_tn(hl, dpre)[0:32]
        dba_ref[...] += jnp.sum(dpre, axis=0, keepdims=True)

    row = pl.BlockSpec((tm, 128), lambda i: (i, 0))
    return pl.pallas_call(
        body, grid=(N // tm,),
        in_specs=[pl.BlockSpec((tm, 128), lambda i: (i, 14)), pl.BlockSpec((128, 256), lambda i: (0, 0)),
                  pl.BlockSpec((1, 256), lambda i: (0, 0)), row, row],
        out_specs=[row, pl.BlockSpec((32, 256), lambda i: (0, 0)), pl.BlockSpec((1, 256), lambda i: (0, 0))],
        out_shape=[_sds((N, 128)), _sds((32, 256)), _sds((1, 256))],
        name="gla_gate_bwd", compiler_params=_cp(("arbitrary",)))(h, wa, ba, dla_f, dla_b)


def _gla_chunk(q, k, v, la, st, rev):
    c = GLA_CHUNK
    rows = q.shape[0]
    nch = rows // c
    b = _cums(la, rev)
    blc = [jnp.sum(la[i * c:(i + 1) * c], axis=0, keepdims=True) for i in range(nch)]
    bl = jnp.concatenate([jnp.broadcast_to(t, (c, 128)) for t in blc], axis=0)
    q_in = q * (32.0 ** -0.5) * jnp.exp(b)
    k_in = k * jnp.exp(-b)
    k_st = k * jnp.exp(bl - b)
    lane_k = lax.broadcasted_iota(jnp.int32, (1, 128), 1) // 32
    lane_v = lax.broadcasted_iota(jnp.int32, (1, 256), 1) // 64
    qs = jnp.concatenate([jnp.where(lane_k == hd, q_in, 0.0) for hd in range(4)], axis=0)
    a = _dmm_nt(qs, k_in)
    a = jnp.where(jnp.concatenate([_chunk_pairs(rows, rev, rev)] * 4, axis=0), a, 0.0)
    o4 = _dmm(a, v)
    o = jnp.zeros((rows, 256), F32)
    for hd in range(4):
        o = o + jnp.where(lane_v == hd, o4[hd * rows:(hd + 1) * rows], 0.0)
    bd = (lax.broadcasted_iota(jnp.int32, (256, 128), 0) // 64) == (lax.broadcasted_iota(jnp.int32, (256, 128), 1) // 32)
    inter = [None] * nch
    for i in (reversed(range(nch)) if rev else range(nch)):
        sl = slice(i * c, (i + 1) * c)
        inter[i] = _dmm_nt(q_in[sl], st)
        st = jnp.exp(blc[i]) * st + jnp.where(bd, _dmm_tn(v[sl], k_st[sl]), 0.0)
    return o + jnp.concatenate(inter, axis=0), st


def _gla_chunk_of(c, rev):
    return NGROUP - 1 - c if rev else c


def _gla_fwd(h, la2):
    c = GLA_GROUP * GLA_CHUNK

    def body(qf, kf, vf, laf, qb, kb, vb, lab, of_ref, ob_ref, sf_ref, sb_ref, stf, stb):
        @pl.when(pl.program_id(0) == 0)
        def _():
            stf[...] = jnp.zeros_like(stf)
            stb[...] = jnp.zeros_like(stb)

        ins = [(qf[s], kf[s], vf[s], laf[s], stf[s], qb[s], kb[s], vb[s], lab[s], stb[s]) for s in range(NSEQ)]
        outs = [(_gla_chunk(*t[:5], False), _gla_chunk(*t[5:], True)) for t in ins]
        for s in range(NSEQ):
            sf_ref[s, 0] = ins[s][4]
            sb_ref[s, 0] = ins[s][9]
            (of_ref[s], stf[s]), (ob_ref[s], stb[s]) = outs[s]

    def specs(rev):
        ch = lambda i: _gla_chunk_of(i, rev)
        return [pl.BlockSpec((NSEQ, c, 128), lambda i: (0, ch(i), 2)), pl.BlockSpec((NSEQ, c, 128), lambda i: (0, ch(i), 3)),
                pl.BlockSpec((NSEQ, c, 256), lambda i: (0, ch(i), 2)),
                pl.BlockSpec((NSEQ, c, 128), lambda i: (0, ch(i), 1 if rev else 0))]

    orow = lambda rev: pl.BlockSpec((NSEQ, c, 256), lambda i: (0, _gla_chunk_of(i, rev), 0))
    srow = lambda rev: pl.BlockSpec((NSEQ, 1, 256, 128), lambda i: (0, _gla_chunk_of(i, rev), 0, 0))
    h3, la3 = h.reshape(NSEQ, L, DINP), la2.reshape(NSEQ, L, 256)
    of, ob, sf, sb = pl.pallas_call(
        body, grid=(NGROUP,),
        in_specs=specs(False) + specs(True),
        out_specs=[orow(False), orow(True), srow(False), srow(True)],
        out_shape=[_sds((NSEQ, L, 256)), _sds((NSEQ, L, 256)), _sds((NSEQ, NGROUP, 256, 128)),
                   _sds((NSEQ, NGROUP, 256, 128))],
        scratch_shapes=[pltpu.VMEM((NSEQ, 256, 128), F32), pltpu.VMEM((NSEQ, 256, 128), F32)],
        name="gla_fwd", compiler_params=_cp(("arbitrary",)))(h3, h3, h3, la3, h3, h3, h3, la3)
    return of.reshape(N, 256), ob.reshape(N, 256), sf, sb


def _gla_bwd(h, la2, do, sf, sb):
    c = GLA_GROUP * GLA_CHUNK

    def body(qf, kf, vf, laf, dof, sfr, qb, kb, vb, lab, dob, sbr,
             dqf, dkf, dvf, dlf, dqb, dkb, dvb, dlb, dstf, dstb):
        @pl.when(pl.program_id(0) == 0)
        def _():
            dstf[...] = jnp.zeros_like(dstf)
            dstb[...] = jnp.zeros_like(dstb)

        def one(s, q, k, v, la, do_, st, dst, rev):
            _, vjp = jax.vjp(functools.partial(_gla_chunk, rev=rev), q[s], k[s], v[s], la[s], st[s, 0])
            return vjp((do_[s], dst[s]))

        res = [(one(s, qf, kf, vf, laf, dof, sfr, dstf, False), one(s, qb, kb, vb, lab, dob, sbr, dstb, True))
               for s in range(NSEQ)]
        for s in range(NSEQ):
            dqf[s], dkf[s], dvf[s], dlf[s], dstf[s] = res[s][0]
            dqb[s], dkb[s], dvb[s], dlb[s], dstb[s] = res[s][1]

    def specs(rev):
        ch = lambda i: _gla_chunk_of(i, not rev)
        return [pl.BlockSpec((NSEQ, c, 128), lambda i: (0, ch(i), 2)), pl.BlockSpec((NSEQ, c, 128), lambda i: (0, ch(i), 3)),
                pl.BlockSpec((NSEQ, c, 256), lambda i: (0, ch(i), 2)),
                pl.BlockSpec((NSEQ, c, 128), lambda i: (0, ch(i), 1 if rev else 0)),
                pl.BlockSpec((NSEQ, c, 256), lambda i: (0, ch(i), 0)),
                pl.BlockSpec((NSEQ, 1, 256, 128), lambda i: (0, ch(i), 0, 0))]

    def ospecs(rev):
        ch = lambda i: _gla_chunk_of(i, not rev)
        n = pl.BlockSpec((NSEQ, c, 128), lambda i: (0, ch(i), 0))
        return [n, n, pl.BlockSpec((NSEQ, c, 256), lambda i: (0, ch(i), 0)), n]

    oshape = [_sds((NSEQ, L, 128)), _sds((NSEQ, L, 128)), _sds((NSEQ, L, 256)), _sds((NSEQ, L, 128))]
    h3, la3, do3 = h.reshape(NSEQ, L, DINP), la2.reshape(NSEQ, L, 256), do.reshape(NSEQ, L, 256)
    res = pl.pallas_call(
        body, grid=(NGROUP,),
        in_specs=specs(False) + specs(True),
        out_specs=ospecs(False) + ospecs(True),
        out_shape=oshape + oshape,
        scratch_shapes=[pltpu.VMEM((NSEQ, 256, 128), F32), pltpu.VMEM((NSEQ, 256, 128), F32)],
        name="gla_bwd", compiler_params=_cp(("arbitrary",)))(h3, h3, h3, la3, do3, sf, h3, h3, h3, la3, do3, sb)
    return [r.reshape(N, r.shape[-1]) for r in res]


def _gla_post(of, ob, r, g):
    o = of + ob
    head = lax.broadcasted_iota(jnp.int32, (1, 256), 1) // 64
    mu = jnp.zeros_like(o)
    for hd in range(4):
        mu = mu + jnp.where(head == hd, jnp.sum(jnp.where(head == hd, o, 0.0), axis=-1, keepdims=True) * (1.0 / 64.0), 0.0)
    xc = o - mu
    var = jnp.zeros_like(o)
    for hd in range(4):
        var = var + jnp.where(head == hd, jnp.sum(jnp.where(head == hd, xc * xc, 0.0), axis=-1, keepdims=True) * (1.0 / 64.0), 0.0)
    return xc * lax.rsqrt(var + LN_EPS) * g * (r * jax.nn.sigmoid(r))


def _gla_post_fwd(of, ob, h, g):
    tm = 512

    def body(of_ref, ob_ref, r_ref, g_ref, y_ref):
        y_ref[...] = _gla_post(of_ref[...], ob_ref[...], r_ref[...], g_ref[...]).astype(MX)

    row = pl.BlockSpec((tm, 256), lambda i: (i, 0))
    return pl.pallas_call(
        body, grid=(N // tm,),
        in_specs=[row, row, pl.BlockSpec((tm, 256), lambda i: (i, 3)), pl.BlockSpec((1, 256), lambda i: (0, 0))],
        out_specs=row, out_shape=_sds((N, 256), MX), name="gla_post_fwd", compiler_params=_cp(("parallel",)))(of, ob, h, g)


def _gla_post_bwd(of, ob, h, g, dyb):
    tm = 512

    def body(of_ref, ob_ref, r_ref, g_ref, dy_ref, do_ref, dr_ref, dg_ref):
        @pl.when(pl.program_id(0) == 0)
        def _():
            dg_ref[...] = jnp.zeros_like(dg_ref)

        _, vjp = jax.vjp(_gla_post, of_ref[...], ob_ref[...], r_ref[...], g_ref[...])
        go, _, gr, gg = vjp(dy_ref[...])
        do_ref[...] = go
        dr_ref[...] = gr
        dg_ref[...] += gg

    row = pl.BlockSpec((tm, 256), lambda i: (i, 0))
    one = pl.BlockSpec((1, 256), lambda i: (0, 0))
    return pl.pallas_call(
        body, grid=(N // tm,),
        in_specs=[row, row, pl.BlockSpec((tm, 256), lambda i: (i, 3)), one, row],
        out_specs=[row, row, one], out_shape=[_sds((N, 256)), _sds((N, 256)), _sds((1, 256))],
        name="gla_post_bwd", compiler_params=_cp(("arbitrary",)))(of, ob, h, g, dyb)


def _rope_tables(width):
    pos = jnp.arange(L, dtype=F32)
    inv_freq = ROPE_THETA ** (-jnp.arange(0, ROT, 2, dtype=F32) / ROT)
    ang = pos[:, None] * inv_freq[None, :]
    cos, sin = jnp.cos(ang), jnp.sin(ang)
    one = jnp.ones((L, 64 - ROT), F32)
    zero = jnp.zeros((L, 64 - ROT), F32)
    z8 = jnp.zeros((L, ROT // 2), F32)
    c = jnp.concatenate([cos, cos, one], axis=1)
    sa = jnp.concatenate([z8, sin, zero], axis=1)
    sb = jnp.concatenate([-sin, z8, zero], axis=1)
    rep = width // 64
    return jnp.stack([jnp.tile(c, (1, rep)), jnp.tile(sa, (1, rep)), jnp.tile(sb, (1, rep))])


def _pieces(t, f):
    out = [f(t[:, c * 128:(c + 1) * 128]) for c in range(t.shape[-1] // 128)]
    return out[0] if len(out) == 1 else jnp.concatenate(out, axis=1)


def _rope(t, tab):
    return _pieces(t, lambda x: x * tab[0] + pltpu.roll(x, ROT // 2, 1) * tab[1] + pltpu.roll(x, 128 - ROT // 2, 1) * tab[2])


def _rope_t(g, tab):
    return _pieces(g, lambda x: x * tab[0] + pltpu.roll(x * tab[1], 128 - ROT // 2, 1) + pltpu.roll(x * tab[2], ROT // 2, 1))


def _swa_pad_kv(kv_ref, tk_ref, kexp, vexp):
    z = jnp.zeros((SWA_BLK, 256), F32)
    kr = _rope(kv_ref[:, 0:128], tk_ref[...])
    for hk in range(2):
        for pad in (kexp, vexp):
            pad[hk, 0:SWA_BLK] = z
            pad[hk, SWA_BLK + L:] = z
        kexp[hk, SWA_BLK:SWA_BLK + L] = _swa_expand(kr, hk)
        vexp[hk, SWA_BLK:SWA_BLK + L] = _swa_expand(kv_ref[:, 128:256], hk)


def _swa_expand(x, hk):
    lane = lax.broadcasted_iota(jnp.int32, x.shape, 1)
    sw = pltpu.roll(x, 64, 1)
    pair = jnp.where(lane < 64, x, sw) if hk == 0 else jnp.where(lane < 64, sw, x)
    return jnp.concatenate([pair, pair], axis=1)


def _swa_fold(x, hk):
    a = x[:, 0:128] + x[:, 128:256]
    t = a + pltpu.roll(a, 64, 1)
    lane = lax.broadcasted_iota(jnp.int32, a.shape, 1)
    return jnp.where((lane < 64) if hk == 0 else (lane >= 64), t, 0.0)


def _swa_probs(q2, kexp, n, sink_ref, hk):
    slot = lax.broadcasted_iota(jnp.int32, (1, 256), 1) // 64
    qs = jnp.concatenate([jnp.where(slot == g, q2, 0.0) for g in range(4)], axis=0)
    s = _mm_nt(qs, kexp) * 0.125
    i = lax.broadcasted_iota(jnp.int32, (SWA_BLK, 3 * SWA_BLK), 0)
    j = lax.broadcasted_iota(jnp.int32, (SWA_BLK, 3 * SWA_BLK), 1)
    kpos = n * SWA_BLK - SWA_BLK + j
    ok = (j - i >= 0) & (j - i <= 2 * SWA_BLK) & (kpos >= 0) & (kpos < L)
    s = jnp.where(jnp.concatenate([ok] * 4, axis=0), s, NEG_BIG)
    rowg = lax.broadcasted_iota(jnp.int32, (4 * SWA_BLK, 1), 0) // SWA_BLK
    sink = jnp.zeros((4 * SWA_BLK, 1), F32)
    for g in range(4):
        sink = jnp.where(rowg == g, sink_ref[hk * 4 + g], sink)
    m = jnp.maximum(jnp.max(s, axis=-1, keepdims=True), sink)
    p = jnp.exp(s - m)
    ps = jnp.exp(sink - m)
    inv = 1.0 / (jnp.sum(p, axis=-1, keepdims=True) + ps)
    return qs, p * inv, ps * inv, slot, rowg


def _swa_qtab(tk_ref, r0):
    return [tk_ref[i, pl.ds(r0, SWA_BLK), :] for i in range(3)]


def _swa_fwd(h, tk, sink):
    def body(sink_ref, q_ref, kv_ref, tk_ref, y_ref, kexp, vexp):
        n = pl.program_id(1)

        @pl.when(n == 0)
        def _():
            _swa_pad_kv(kv_ref, tk_ref, kexp, vexp)

        r0 = pl.multiple_of(n * SWA_BLK, SWA_BLK)
        q = _rope(q_ref[...], _swa_qtab(tk_ref, r0))
        for hk in range(2):
            _, p, _, slot, _ = _swa_probs(q[:, hk * 256:(hk + 1) * 256], kexp[hk, pl.ds(r0, 3 * SWA_BLK), :], n,
                                          sink_ref, hk)
            o4 = _mm(p, vexp[hk, pl.ds(r0, 3 * SWA_BLK), :])
            o = jnp.zeros((SWA_BLK, 256), F32)
            for g in range(4):
                o = o + jnp.where(slot == g, o4[g * SWA_BLK:(g + 1) * SWA_BLK], 0.0)
            y_ref[:, hk * 256:(hk + 1) * 256] = o.astype(MX)

    return pl.pallas_call(
        body,
        grid_spec=pltpu.PrefetchScalarGridSpec(
            num_scalar_prefetch=1, grid=(NSEQ, NBLK),
            in_specs=[pl.BlockSpec((SWA_BLK, 512), lambda s, n, sk: (s * NBLK + n, 2)),
                      pl.BlockSpec((L, 256), lambda s, n, sk: (s, 6)),
                      pl.BlockSpec((3, L, 128), lambda s, n, sk: (0, 0, 0))],
            out_specs=pl.BlockSpec((SWA_BLK, 512), lambda s, n, sk: (s * NBLK + n, 0)),
            scratch_shapes=[pltpu.VMEM((2, L + 2 * SWA_BLK, 256), F32), pltpu.VMEM((2, L + 2 * SWA_BLK, 256), F32)]),
        out_shape=_sds((N, 512), MX), name="swa_fwd", compiler_params=_cp(("arbitrary", "arbitrary")))(sink, h, h, tk)


def _swa_bwd(h, tk, sink, dyc):
    def body(sink_ref, q_ref, kv_ref, tk_ref, dy_ref, dq_ref, dkv_ref, dsink_ref, kexp_all, vexp_all, dkacc, dvacc):
        sq = pl.program_id(0)
        n = pl.program_id(1)

        @pl.when(n == 0)
        def _():
            _swa_pad_kv(kv_ref, tk_ref, kexp_all, vexp_all)
            dkacc[...] = jnp.zeros_like(dkacc)
            dvacc[...] = jnp.zeros_like(dvacc)

        @pl.when((n == 0) & (sq == 0))
        def _():
            dsink_ref[...] = jnp.zeros_like(dsink_ref)

        r0 = pl.multiple_of(n * SWA_BLK, SWA_BLK)
        tq = _swa_qtab(tk_ref, r0)
        q = _rope(q_ref[...], tq)
        hrow = lax.broadcasted_iota(jnp.int32, (8, 128), 0)
        dsk = jnp.zeros((8, 128), F32)
        for hk in range(2):
            kexp = kexp_all[hk, pl.ds(r0, 3 * SWA_BLK), :]
            vexp = vexp_all[hk, pl.ds(r0, 3 * SWA_BLK), :]
            qs, p, ps, slot, rowg = _swa_probs(q[:, hk * 256:(hk + 1) * 256], kexp, n, sink_ref, hk)
            dy2 = dy_ref[:, hk * 256:(hk + 1) * 256]
            dos = jnp.concatenate([jnp.where(slot == g, dy2, 0.0) for g in range(4)], axis=0)
            dp = _mm_nt(dos, vexp)
            delta = jnp.sum(p * dp, axis=-1, keepdims=True)
            ds = p * (dp - delta) * 0.125
            dsr = -ps * delta
            for g in range(4):
                dsk = dsk + jnp.where(hrow == hk * 4 + g, jnp.sum(jnp.where(rowg == g, dsr, 0.0), axis=0, keepdims=True), 0.0)
            dq4 = _mm(ds, kexp)
            dq2 = jnp.zeros((SWA_BLK, 256), F32)
            for g in range(4):
                dq2 = dq2 + jnp.where(slot == g, dq4[g * SWA_BLK:(g + 1) * SWA_BLK], 0.0)
            dq_ref[:, hk * 256:(hk + 1) * 256] = dq2
            dkacc[hk, pl.ds(r0, 3 * SWA_BLK), :] += _mm_tn(ds, qs)
            dvacc[hk, pl.ds(r0, 3 * SWA_BLK), :] += _mm_tn(p, dos)
        dq_ref[...] = _rope_t(dq_ref[...], tq)
        dsink_ref[...] += dsk

        @pl.when(n == NBLK - 1)
        def _():
            seq = slice(SWA_BLK, SWA_BLK + L)
            dkv_ref[:, 0:128] = _rope_t(_swa_fold(dkacc[0, seq], 0) + _swa_fold(dkacc[1, seq], 1), tk_ref[...])
            dkv_ref[:, 128:256] = _swa_fold(dvacc[0, seq], 0) + _swa_fold(dvacc[1, seq], 1)

    blk = lambda col: pl.BlockSpec((SWA_BLK, 512), lambda s, n, sk: (s * NBLK + n, col))
    pad = pltpu.VMEM((2, L + 2 * SWA_BLK, 256), F32)
    return pl.pallas_call(
        body,
        grid_spec=pltpu.PrefetchScalarGridSpec(
            num_scalar_prefetch=1, grid=(NSEQ, NBLK),
            in_specs=[blk(2), pl.BlockSpec((L, 256), lambda s, n, sk: (s, 6)),
                      pl.BlockSpec((3, L, 128), lambda s, n, sk: (0, 0, 0)), blk(0)],
            out_specs=[blk(0), pl.BlockSpec((L, 256), lambda s, n, sk: (s, 0)),
                       pl.BlockSpec((8, 128), lambda s, n, sk: (0, 0))],
            scratch_shapes=[pad, pad, pad, pad]),
        out_shape=[_sds((N, 512)), _sds((N, 256)), _sds((8, 128))],
        name="swa_bwd", compiler_params=_cp(("arbitrary", "arbitrary")))(sink, h, h, tk, dyc)


def _outproj_fwd(ya, yb, yc, x, wo, g, b):
    tm = 512

    def body(ya_ref, yb_ref, yc_ref, x_ref, wo_ref, g_ref, b_ref, s_ref, x1_ref):
        mix = _mm(ya_ref[...], wo_ref[0:256]) + _mm(yb_ref[...], wo_ref[256:512]) + _mm(yc_ref[...], wo_ref[512:1024])
        s = ALPHA * x_ref[...] + mix
        s_ref[...] = s
        x1_ref[...] = _ln_fwd(s, g_ref[...], b_ref[...])

    row = lambda w_: pl.BlockSpec((tm, w_), lambda i: (i, 0))
    one = pl.BlockSpec((1, D), lambda i: (0, 0))
    return pl.pallas_call(
        body, grid=(N // tm,),
        in_specs=[row(256), row(256), row(512), row(D), pl.BlockSpec((D, D), lambda i: (0, 0)), one, one],
        out_specs=[row(D), row(D)], out_shape=[_sds((N, D)), _sds((N, D))],
        name="outproj_fwd", compiler_params=_cp(("parallel",)))(ya, yb, yc, x, wo, g, b)


def _outproj_bwd(dx1, s1, ya, yb, yc, wo, g):
    tm = 512
    nt = N // tm

    def body(dx1_ref, s_ref, ya_ref, yb_ref, yc_ref, wo_ref, g_ref,
             dya_ref, dyb_ref, dyc_ref, dxp_ref, dwo_ref, dg_ref, db_ref, acc):
        i = pl.program_id(0)

        @pl.when(i == 0)
        def _():
            acc[...] = jnp.zeros_like(acc)
            dg_ref[...] = jnp.zeros_like(dg_ref)
            db_ref[...] = jnp.zeros_like(db_ref)

        ds, dg, db = _ln_bwd(dx1_ref[...], s_ref[...], g_ref[...])
        dg_ref[...] += dg
        db_ref[...] += db
        dxp_ref[...] = ALPHA * ds
        dy = _mm_nt(ds, wo_ref[...])
        dya_ref[...] = dy[:, 0:256]
        dyb_ref[...] = dy[:, 256:512]
        dyc_ref[...] = dy[:, 512:1024]
        acc[0:256] += _mm_tn(ya_ref[...], ds)
        acc[256:512] += _mm_tn(yb_ref[...], ds)
        acc[512:1024] += _mm_tn(yc_ref[...], ds)

        @pl.when(i == nt - 1)
        def _():
            dwo_ref[...] = acc[...].astype(MX)

    row = lambda w_: pl.BlockSpec((tm, w_), lambda i: (i, 0))
    one = pl.BlockSpec((1, D), lambda i: (0, 0))
    full = pl.BlockSpec((D, D), lambda i: (0, 0))
    return pl.pallas_call(
        body, grid=(nt,),
        in_specs=[row(D), row(D), row(256), row(256), row(512), full, one],
        out_specs=[row(256), row(256), row(512), row(D), full, one, one],
        out_shape=[_sds((N, 256)), _sds((N, 256)), _sds((N, 512)), _sds((N, D)), _sds((D, D), MX), _sds((1, D)), _sds((1, D))],
        scratch_shapes=[pltpu.VMEM((D, D), F32)],
        name="outproj_bwd", compiler_params=_cp(("arbitrary",)))(dx1, s1, ya, yb, yc, wo, g)


def _ffn_fwd(x1, w1, w2, g, b):
    tm = FFN_TM

    def body(x_ref, w1_ref, w2_ref, g_ref, b_ref, a_ref, s_ref, x2_ref):
        j = pl.program_id(1)

        @pl.when(j == 0)
        def _():
            s_ref[...] = ALPHA * x_ref[...]

        a = _mm(x_ref[...], w1_ref[0])
        a_ref[...] = a.astype(MX)
        hid = jnp.square(jnp.maximum(a, 0.0))
        s_ref[...] += _mm(hid, w2_ref[0])

        @pl.when(j == NSHARD - 1)
        def _():
            x2_ref[...] = _ln_fwd(s_ref[...], g_ref[...], b_ref[...])

    row = pl.BlockSpec((tm, D), lambda i, j: (i, 0))
    wj = pl.BlockSpec((1, D, D), lambda i, j: (j, 0, 0))
    one = pl.BlockSpec((1, D), lambda i, j: (0, 0))
    return pl.pallas_call(
        body, grid=(N // tm, NSHARD),
        in_specs=[row, wj, wj, one, one],
        out_specs=[pl.BlockSpec((tm, D), lambda i, j: (i, j)), row, row],
        out_shape=[_sds((N, DFF), MX), _sds((N, D)), _sds((N, D))],
        name="ffn_fwd", compiler_params=_cp(("parallel", "arbitrary"), FFN_VMEM))(x1, w1, w2, g, b)


def _ffn_bwd_act(dy, s2, a, w1, w2, g):
    tm = FFN_TM

    def body(dy_ref, s_ref, a_ref, w1_ref, w2_ref, g_ref, da_ref, ds_ref, dx1_ref, dg_ref, db_ref):
        i = pl.program_id(0)
        j = pl.program_id(1)

        @pl.when((i == 0) & (j == 0))
        def _():
            dg_ref[...] = jnp.zeros_like(dg_ref)
            db_ref[...] = jnp.zeros_like(db_ref)

        @pl.when(j == 0)
        def _():
            ds, dg, db = _ln_bwd(dy_ref[...], s_ref[...], g_ref[...])
            ds_ref[...] = ds.astype(MX)
            dg_ref[...] += dg
            db_ref[...] += db
            dx1_ref[...] = ALPHA * ds

        dhid = _mm_nt(ds_ref[...], w2_ref[0])
        da = dhid * 2.0 * jnp.maximum(a_ref[...].astype(F32), 0.0)
        da_ref[...] = da.astype(MX)
        dx1_ref[...] += _mm_nt(da, w1_ref[0])

    row = pl.BlockSpec((tm, D), lambda i, j: (i, 0))
    col = pl.BlockSpec((tm, D), lambda i, j: (i, j))
    wj = pl.BlockSpec((1, D, D), lambda i, j: (j, 0, 0))
    one = pl.BlockSpec((1, D), lambda i, j: (0, 0))
    return pl.pallas_call(
        body, grid=(N // tm, NSHARD),
        in_specs=[row, row, col, wj, wj, one],
        out_specs=[col, row, row, one, one],
        out_shape=[_sds((N, DFF), MX), _sds((N, D), MX), _sds((N, D)), _sds((1, D)), _sds((1, D))],
        name="ffn_bwd_act", compiler_params=_cp(("arbitrary", "arbitrary"), FFN_VMEM))(dy, s2, a, w1, w2, g)


def _ffn_bwd_w(x1, da, a, ds):
    tm = FFN_TM_W
    nt = N // tm

    def body(x_ref, da_ref, a_ref, ds_ref, dw1_ref, dw2_ref, acc1, acc2):
        i = pl.program_id(1)

        @pl.when(i == 0)
        def _():
            acc1[...] = jnp.zeros_like(acc1)
            acc2[...] = jnp.zeros_like(acc2)

        acc1[...] += _mm_tn(x_ref[...], da_ref[...])
        hid = jnp.square(jnp.maximum(a_ref[...].astype(F32), 0.0))
        acc2[...] += _mm_tn(hid, ds_ref[...])

        @pl.when(i == nt - 1)
        def _():
            dw1_ref[0] = acc1[...].astype(MX)
            dw2_ref[0] = acc2[...].astype(MX)

    row = pl.BlockSpec((tm, D), lambda j, i: (i, 0))
    col = pl.BlockSpec((tm, D), lambda j, i: (i, j))
    wj = pl.BlockSpec((1, D, D), lambda j, i: (j, 0, 0))
    return pl.pallas_call(
        body, grid=(NSHARD, nt),
        in_specs=[row, col, col, row], out_specs=[wj, wj],
        out_shape=[_sds((NSHARD, D, D), MX), _sds((NSHARD, D, D), MX)],
        scratch_shapes=[pltpu.VMEM((D, D), F32), pltpu.VMEM((D, D), F32)],
        name="ffn_bwd_w", compiler_params=_cp(("parallel", "arbitrary")))(x1, da, a, ds)


def _loss_head(y, target):
    tm = 512

    def body(y_ref, t_ref, dy_ref, l_ref):
        @pl.when(pl.program_id(0) == 0)
        def _():
            l_ref[...] = jnp.zeros_like(l_ref)

        e = y_ref[...] - t_ref[...]
        dy_ref[...] = e * (1.0 / D)
        l_ref[...] += jnp.sum(jnp.sum(e * e, axis=1, keepdims=True), axis=0, keepdims=True) * (0.5 / D)

    row = pl.BlockSpec((tm, D), lambda i: (i, 0))
    return pl.pallas_call(
        body, grid=(N // tm,), in_specs=[row, row],
        out_specs=[row, pl.BlockSpec((8, 128), lambda i: (0, 0))],
        out_shape=[_sds((N, D)), _sds((8, 128))], name="loss_head", compiler_params=_cp(("arbitrary",)))(y, target)


def _s5_discretize(a_re, a_im, log_step, b_re, b_im):
    lam = lax.complex(a_re, a_im)
    lam_bar = jnp.exp(lam * jnp.exp(log_step))
    b_bar = ((lam_bar - 1.0) / lam)[..., None] * lax.complex(b_re, b_im)
    return jnp.real(lam_bar), jnp.imag(lam_bar), jnp.real(b_bar), jnp.imag(b_bar)


def _s5_in_blocks(b):
    e = jnp.eye(8, dtype=F32)
    return jnp.einsum('ij,zbjph->zbihjp', e, b.reshape(2, 2, 8, S5_P, S5_H)).reshape(2, 2, 128, SW)


def _s5_in_unblocks(d):
    return jnp.einsum('zbihip->zbiph', d.reshape(2, 2, 8, S5_H, 8, S5_P)).reshape(2, S5_G, S5_P, S5_H)


def _s5_out_blocks(c):
    e = jnp.eye(8, dtype=F32)
    return jnp.einsum('ij,zbjhp->zbjpih', e, c.reshape(2, 2, 8, S5_H, S5_P)).reshape(2, 2, SW, 128)


def _s5_out_unblocks(d):
    return jnp.einsum('zbipih->zbihp', d.reshape(2, 2, 8, S5_P, 8, S5_H)).reshape(2, S5_G, S5_H, S5_P)


def _gate_weight(w_a):
    z = jnp.zeros((16, 128), F32)
    top = jnp.concatenate([w_a[0], z], axis=1)
    bot = jnp.concatenate([z, w_a[1]], axis=1)
    return jnp.concatenate([top, bot, jnp.zeros((96, 256), F32)], axis=0)


def _layer_prep(p):
    lr, li, br, bi = _s5_discretize(p["s5_a_re"], p["s5_a_im"], p["s5_log_step"], p["s5_b_re"], p["s5_b_im"])
    q = dict(p)
    q["bre"] = _s5_in_blocks(br).astype(MX)
    q["bim"] = _s5_in_blocks(bi).astype(MX)
    q["cre"] = _s5_out_blocks(p["s5_c_re"]).astype(MX)
    q["cim"] = _s5_out_blocks(p["s5_c_im"]).astype(MX)
    mr, mi = lr.reshape(2, 1024), li.reshape(2, 1024)
    both = lambda t0, t1: tuple(jnp.stack(p) for p in zip(t0, t1))
    q["tab"] = both(_lockstep_tables(mr[0], mi[0], False), _lockstep_tables(mr[1], mi[1], True))
    q["tabc"] = both(_lockstep_tables(mr[0], -mi[0], True), _lockstep_tables(mr[1], -mi[1], False))
    q["dsk"] = p["s5_d"].reshape(1, 256)
    q["wa"] = _gate_weight(p["gla_w_a"]).astype(MX)
    q["ba"] = p["gla_b_a"].reshape(1, 256)
    q["lng"] = p["gla_ln_g"].reshape(1, 256)
    q["bv"] = p["s5_b_glu"][:256].reshape(1, 256)
    q["bg"] = p["s5_b_glu"][256:].reshape(1, 256)
    for k in ("ln1_g", "ln1_b", "ln2_g", "ln2_b"):
        q[k] = p[k].reshape(1, D)
    return q


def _layer_fwd(x, q, tk, fetch):
    q["w_in"] = fetch("w_in", x)
    h = _inproj_fwd(x, q["w_in"])
    hre, him, y2 = _s5_fwd(h, q["bre"], q["bim"], q["cre"], q["cim"], q["tab"])
    q["w4"] = fetch("s5_w_glu", y2)
    ya = _s5_glu_fwd(y2, h, q["dsk"], q["w4"], q["bv"], q["bg"])
    la2 = _gla_gate_fwd(h, q["wa"], q["ba"])
    of, ob, sf, sb = _gla_fwd(h, la2)
    yb = _gla_post_fwd(of, ob, h, q["lng"])
    yc = _swa_fwd(h, tk, q["swa_sink"])
    q["w_out"] = fetch("w_out", yc)
    s1, x1 = _outproj_fwd(ya, yb, yc, x, q["w_out"], q["ln1_g"], q["ln1_b"])
    q["w_ff1"] = fetch("w_ff1", x1)
    q["w_ff2"] = fetch("w_ff2", x1)
    a, s2, x2 = _ffn_fwd(x1, q["w_ff1"], q["w_ff2"], q["ln2_g"], q["ln2_b"])
    saved = dict(x=x, h=h, hre=hre, him=him, y2=y2, ya=ya, la2=la2, of=of, ob=ob, sf=sf, sb=sb, yb=yb, yc=yc,
                 s1=s1, x1=x1, a=a, s2=s2)
    return x2, saved


def _layer_bwd(dy, q, sv, tk, emit):
    g = {}
    da, ds2, dx1, g["dg2"], g["db2"] = _ffn_bwd_act(dy, sv["s2"], sv["a"], q["w_ff1"], q["w_ff2"], q["ln2_g"])
    dw1, dw2 = _ffn_bwd_w(sv["x1"], da, sv["a"], ds2)
    tie = emit(dict(w_ff1=dw1, w_ff2=dw2))
    dya, dyb, dyc, dxp, dwo, g["dg1"], g["db1"] = _outproj_bwd(dx1, sv["s1"], sv["ya"], sv["yb"], sv["yc"],
                                                               q["w_out"], q["ln1_g"] + tie)
    h = sv["h"]
    daq, dakv, g["dsink"] = _swa_bwd(h, tk, q["swa_sink"], dyc)
    do, gr, g["dlng"] = _gla_post_bwd(sv["of"], sv["ob"], h, q["lng"], dyb)
    gq_f, gk_f, gv_f, gl_f, gq_b, gk_b, gv_b, gl_b = _gla_bwd(h, sv["la2"], do, sv["sf"], sv["sb"])
    dhl, g["dwa"], g["dba"] = _gla_gate_bwd(h, q["wa"], q["ba"], gl_f, gl_b)
    dyp, dud, g["dd"], dw4, g["dbv"], g["dbg"] = _s5_glu_bwd(sv["y2"], h, q["dsk"], q["w4"], q["bv"], q["bg"], dya)
    tie = emit(dict(w_out=dwo.reshape(NSHARD, D // NSHARD, D), s5_w_glu=dw4))
    du2, g["dbre"], g["dbim"], g["dcre"], g["dcim"], g["dmu"] = _s5_bwd(
        h, dyp, sv["hre"], sv["him"], q["bre"], q["bim"], q["cre"], q["cim"], (q["tabc"][0], q["tabc"][1] + tie))
    dx, dwt = _inproj_bwd(sv["x"], q["w_in"], dxp, du2, dud, gq_f, gq_b, gk_f, gk_b, gv_f, gv_b, gr, daq, dakv, dhl)
    tie = emit(dict(w_in=dwt))
    return dx, g, tie


NATIVE = ("dmu", "dbre", "dbim", "dcre", "dcim", "dd", "dbv", "dbg", "dwa", "dba", "dlng", "dsink",
          "dg1", "db1", "dg2", "db2", "loss")
ICI_CORE = (0, 0, 0, 1, 1, 0, 0, 0, 1, 1, 1, 1, 0, 0, 1, 1, 0)


def _finish_small(n, w):
    g = {}
    dmu = n["dmu"]
    dlr = dmu[:, :, :, 0].reshape(DEPTH, 2, S5_G, S5_P)
    dli = dmu[:, :, :, 1].reshape(DEPTH, 2, S5_G, S5_P)

    def unblock(c, perm, shape):
        return c.reshape(DEPTH, 2, 2, S5_H, 8, S5_P).transpose(perm).reshape(shape)

    b_shape, c_shape = (DEPTH, 2, S5_G, S5_P, S5_H), (DEPTH, 2, S5_G, S5_H, S5_P)
    _, vjp = jax.vjp(_s5_discretize, w["s5_a_re"], w["s5_a_im"], w["s5_log_step"], w["s5_b_re"], w["s5_b_im"])
    (g["s5_a_re"], g["s5_a_im"], g["s5_log_step"], g["s5_b_re"], g["s5_b_im"]) = vjp(
        (dlr, dli, unblock(n["dbre"], (0, 1, 2, 4, 5, 3), b_shape), unblock(n["dbim"], (0, 1, 2, 4, 5, 3), b_shape)))
    g["s5_c_re"] = unblock(n["dcre"], (0, 1, 2, 4, 3, 5), c_shape)
    g["s5_c_im"] = unblock(n["dcim"], (0, 1, 2, 4, 3, 5), c_shape)
    g["s5_d"] = n["dd"].reshape(DEPTH, S5_G, S5_H)
    g["s5_b_glu"] = jnp.concatenate([n["dbv"], n["dbg"]], axis=2).reshape(DEPTH, 512)
    g["gla_w_a"] = jnp.stack([n["dwa"][:, 0:16, 0:128], n["dwa"][:, 16:32, 128:256]], axis=1)
    g["gla_b_a"] = n["dba"].reshape(DEPTH, 2, 128)
    g["gla_ln_g"] = n["dlng"].reshape(DEPTH, 256)
    g["swa_sink"] = n["dsink"][:, :, 0]
    for k, s in (("ln1_g", "dg1"), ("ln1_b", "db1"), ("ln2_g", "dg2"), ("ln2_b", "db2")):
        g[k] = n[s].reshape(DEPTH, D)
    return g


def _local_step(x, target, qs, tk, fetch, emit):
    saved = []
    for l, q in enumerate(qs):
        x, sv = _layer_fwd(x, q, tk, functools.partial(fetch, l))
        saved.append(sv)
    dy, lacc = _loss_head(x, target)
    smalls = [None] * DEPTH
    tie = 0.0
    for l in reversed(range(DEPTH)):
        qs[l]["ln2_g"] = qs[l]["ln2_g"] + tie
        dy, smalls[l], tie = _layer_bwd(dy, qs[l], saved[l], tk, functools.partial(emit, l))
    smalls[0]["db2"] = smalls[0]["db2"] + tie
    for l in range(DEPTH):
        smalls[l]["loss"] = lacc if l == 0 else jnp.zeros_like(lacc)
    return lacc[0, 0], dy, smalls


BIG = ("w_in", "s5_w_glu", "w_out", "w_ff1", "w_ff2")
SMALL = ("s5_a_re", "s5_a_im", "s5_log_step", "s5_b_re", "s5_b_im", "s5_c_re", "s5_c_im", "s5_d", "s5_b_glu",
         "gla_w_a", "gla_b_a", "gla_ln_g", "swa_sink", "ln1_g", "ln1_b", "ln2_g", "ln2_b")
ANY = pl.BlockSpec(memory_space=pl.ANY)


def _place():
    x, y, c = lax.axis_index("x"), lax.axis_index("y"), lax.axis_index("c")
    return x, y, c, [(1 - x, y), (x, 1 - y), (1 - x, 1 - y)]


HBM = pl.BlockSpec(memory_space=pltpu.HBM)
SEMS = pl.BlockSpec(memory_space=pltpu.SEMAPHORE)
EFFECT = pltpu.SideEffectType.DATAFLOW_SIDE_EFFECTING


def _push_copies(ins, lands, send, recv, gather, sending):
    x, y, c, chips = _place()
    me = 2 * x + y
    out = []
    for a in range(len(lands)):
        for j, (px, py) in enumerate(chips):
            peer = 2 * px + py
            src = lands[a].at[me] if gather else ins[a].at[peer if sending else me]
            dst = lands[a].at[me if sending else peer]
            out.append(pltpu.make_async_remote_copy(src_ref=src, dst_ref=dst, send_sem=send.at[3 * a + j],
                                                    recv_sem=recv.at[3 * a + j], device_id=(px, py, c),
                                                    device_id_type=MESH))
    return out


def _push_start(name, arrs, gather):
    n = len(arrs)
    ops = list(arrs) if gather else list(arrs) + [lax.empty(s.shape, s.dtype) for s in arrs]
    m = len(ops)

    def body(*refs):
        ins, lnd = (refs[:n], refs[:n]) if gather else (refs[:n], refs[n:m])
        for cp in _push_copies(ins, lnd, refs[m], refs[m + 1], gather, True):
            cp.start()
        refs[-1][...] = jnp.zeros((8, 128), F32)

    ops = [pltpu.with_memory_space_constraint(t, pltpu.HBM) for t in ops]
    res = pl.pallas_call(
        body, name=name,
        out_shape=(pltpu.SemaphoreType.DMA((3 * n,)), pltpu.SemaphoreType.DMA((3 * n,)),
                   *[pltpu.HBM(t.shape, t.dtype) for t in ops], _sds((8, 128))),
        in_specs=[HBM] * m,
        out_specs=(SEMS, SEMS, *[HBM] * m, pl.BlockSpec(memory_space=pltpu.VMEM)),
        input_output_aliases={i: 2 + i for i in range(m)},
        compiler_params=pltpu.CompilerParams(has_side_effects=EFFECT))(*ops)
    return res[0], res[1], list(res[2:2 + m]), res[-1]


def _push_wait(name, started, after, gather):
    send, recv, ops, _ = started
    m = len(ops)
    n = m if gather else m // 2

    def body(*refs):
        ins, lnd = (refs[:n], refs[:n]) if gather else (refs[:n], refs[n:m])
        for cp in _push_copies(ins, lnd, refs[m], refs[m + 1], gather, False):
            cp.wait_send()
            cp.wait_recv()

    res = pl.pallas_call(
        body, name=name,
        out_shape=[pltpu.HBM(t.shape, t.dtype) for t in ops],
        in_specs=[HBM] * m + [SEMS, SEMS, ANY], out_specs=[HBM] * m,
        input_output_aliases={i: i for i in range(m)},
        compiler_params=pltpu.CompilerParams(has_side_effects=EFFECT))(*ops, send, recv, after)
    return list(res)


def _row_tile(rows):
    return max(t for t in range(8, min(rows, 512) + 1, 8) if rows % t == 0)


def _cast_to_slot(me, w, l):
    _, rows, cols = w.shape
    tr = _row_tile(rows)

    def body(me_ref, w_ref, o_ref):
        o_ref[0] = w_ref[0].astype(MX)

    return pl.pallas_call(
        body,
        grid_spec=pltpu.PrefetchScalarGridSpec(
            num_scalar_prefetch=1, grid=(rows // tr,),
            in_specs=[pl.BlockSpec((1, tr, cols), lambda i, me_: (l, i, 0))],
            out_specs=pl.BlockSpec((1, tr, cols), lambda i, me_: (me_[0], i, 0))),
        out_shape=_sds((NSHARD, rows, cols), MX), name="cast_to_slot", compiler_params=_cp(("arbitrary",)))(me, w)


def _sum_sources(me, recv, own):
    _, rows, cols = recv[0].shape
    tr = min(_row_tile(rows), 256) if rows % 256 == 0 else _row_tile(rows)
    nt = rows // tr

    def body(me_ref, *refs):
        o_ref = refs[-1]
        for l in range(DEPTH):
            @pl.when(pl.program_id(0) == l)
            def _():
                r_ref, own_ref = refs[2 * l], refs[2 * l + 1]
                part = [jnp.where(me_ref[0] == s, own_ref[0], r_ref[s]).astype(F32) for s in range(NSHARD)]
                o_ref[...] = ((part[0] + part[1]) + part[2]) + part[3]

    in_specs = []
    for l in range(DEPTH):
        pick = lambda g, i, me_, l=l: jnp.where(g == l, i, jnp.where(g < l, 0, nt - 1))
        in_specs += [pl.BlockSpec((NSHARD, tr, cols), lambda g, i, me_, pick=pick: (0, pick(g, i, me_), 0)),
                     pl.BlockSpec((1, tr, cols), lambda g, i, me_, pick=pick: (me_[0], pick(g, i, me_), 0))]
    return pl.pallas_call(
        body,
        grid_spec=pltpu.PrefetchScalarGridSpec(
            num_scalar_prefetch=1, grid=(DEPTH, nt), in_specs=in_specs,
            out_specs=pl.BlockSpec((tr, cols), lambda g, i, me_: (g * nt + i, 0))),
        out_shape=_sds((DEPTH * rows, cols)), name="sum_sources",
        compiler_params=_cp(("arbitrary", "arbitrary")))(me, *[t for l in range(DEPTH) for t in (recv[l], own[l])])


def _swap_sibling(arrs):
    n = len(arrs)

    def body(*refs):
        ins, outs = refs[:n], refs[n:2 * n]
        send, recv = refs[2 * n:]
        x, y, c, _ = _place()
        cps = [pltpu.make_async_remote_copy(src_ref=ins[a], dst_ref=outs[a], send_sem=send.at[a], recv_sem=recv.at[a],
                                            device_id=(x, y, 1 - c), device_id_type=MESH) for a in range(n)]
        for cp in cps:
            cp.start()
        for cp in cps:
            cp.wait()

    return pl.pallas_call(
        body, in_specs=[ANY] * n, out_specs=[ANY] * n, out_shape=[_sds(a.shape, a.dtype) for a in arrs],
        scratch_shapes=[pltpu.SemaphoreType.DMA((n,)), pltpu.SemaphoreType.DMA((n,))],
        name="swap_sibling")(*arrs)


def _allreduce_small(per_layer):
    nk = len(per_layer[0])
    n = DEPTH * nk
    shapes = [a.shape for a in per_layer[0]]

    def body(*refs):
        ins, outs = refs[:n], refs[n:n + nk]
        sibs, slots = refs[n + nk:n + 2 * nk], refs[n + 2 * nk:n + 3 * nk]
        send, recv = refs[n + 3 * nk:]
        x, y, c, chips = _place()
        me = 2 * x + y
        d2d = [pltpu.make_async_remote_copy(src_ref=ins[l * nk + k], dst_ref=sibs[k].at[l], send_sem=send.at[l * nk + k],
                                            recv_sem=recv.at[l * nk + k], device_id=(x, y, 1 - c), device_id_type=MESH)
               for l in range(DEPTH) for k in range(nk)]
        for cp in d2d:
            cp.start()
        for cp in d2d:
            cp.wait()
        for l in range(DEPTH):
            for k in range(nk):
                slots[k][me, l] = ins[l * nk + k][...] + sibs[k][l]

        def remote(k, j, slot):
            px, py = chips[j]
            return pltpu.make_async_remote_copy(src_ref=slots[k].at[me], dst_ref=slots[k].at[slot],
                                                send_sem=send.at[n + 3 * k + j], recv_sem=recv.at[n + 3 * k + j],
                                                device_id=(px, py, c), device_id_type=MESH)

        def handover(k):
            return pltpu.make_async_remote_copy(src_ref=outs[k], dst_ref=outs[k], send_sem=send.at[n + 3 * nk + k],
                                                recv_sem=recv.at[n + 3 * nk + k], device_id=(x, y, 1 - c),
                                                device_id_type=MESH)

        halves = (tuple(k for k in range(nk) if ICI_CORE[k] == 0), tuple(k for k in range(nk) if ICI_CORE[k] == 1))
        for cc in range(2):
            @pl.when(c == cc)
            def _():
                mine, theirs = halves[cc], halves[1 - cc]
                sends = [remote(k, j, me) for k in mine for j in range(3)]
                for cp in sends:
                    cp.start()
                for k in mine:
                    for j in range(3):
                        remote(k, j, 2 * chips[j][0] + chips[j][1]).wait_recv()
                for cp in sends:
                    cp.wait_send()
                for k in mine:
                    outs[k][...] = ((slots[k][0] + slots[k][1]) + slots[k][2]) + slots[k][3]
                over = [handover(k) for k in mine]
                for cp in over:
                    cp.start()
                for k in theirs:
                    handover(k).wait_recv()
                for cp in over:
                    cp.wait_send()

    vm = pl.BlockSpec(memory_space=pltpu.VMEM)
    return pl.pallas_call(
        body, in_specs=[vm] * n, out_specs=[vm] * nk, out_shape=[_sds((DEPTH,) + s) for s in shapes],
        scratch_shapes=([pltpu.VMEM((DEPTH,) + s, F32) for s in shapes]
                        + [pltpu.VMEM((NSHARD, DEPTH) + s, F32) for s in shapes]
                        + [pltpu.SemaphoreType.DMA((n + 4 * nk,)), pltpu.SemaphoreType.DMA((n + 4 * nk,))]),
        name="allreduce_small", compiler_params=pltpu.CompilerParams(vmem_limit_bytes=VMEM_LIMIT))(
            *[a for layer in per_layer for a in layer])


def _adamw_math(w, g, m, v):
    m = ADAM_B1 * m + (1.0 - ADAM_B1) * g
    v = ADAM_B2 * v + (1.0 - ADAM_B2) * jnp.square(g)
    m_hat = m / (1.0 - ADAM_B1 ** ADAM_STEP)
    v_hat = v / (1.0 - ADAM_B2 ** ADAM_STEP)
    delta = -ADAM_LR * (m_hat / (jnp.sqrt(v_hat) + ADAM_EPS) + ADAM_WD * w)
    return delta, m, v


def _adamw(g_parts, w, m, v):
    rows, cols = w.shape
    tr = 256 if rows % 256 == 0 else _row_tile(rows)
    k = len(g_parts)

    def body(*refs):
        g = refs[0][...]
        for r in refs[1:k]:
            g = g + r[...]
        w_ref, m_ref, v_ref, go, do, mo, vo = refs[k:]
        d, mn, vn = _adamw_math(w_ref[...], g, m_ref[...], v_ref[...])
        go[...] = g
        do[...] = d
        mo[...] = mn
        vo[...] = vn

    spec = pl.BlockSpec((tr, cols), lambda i: (i, 0))
    return pl.pallas_call(
        body, grid=(rows // tr,), in_specs=[spec] * (k + 3), out_specs=[spec] * 4,
        out_shape=[_sds((rows, cols))] * 4, name="adamw", compiler_params=_cp(("parallel",)))(*g_parts, w, m, v)


def _adamw_small(gs, ws, ms, vs):
    n = len(gs)

    def body(*refs):
        for k in range(n):
            d, mn, vn = _adamw_math(refs[n + k][...], refs[k][...], refs[2 * n + k][...], refs[3 * n + k][...])
            refs[4 * n + k][...] = d
            refs[5 * n + k][...] = mn
            refs[6 * n + k][...] = vn

    vm = pl.BlockSpec(memory_space=pltpu.VMEM)
    shapes = [_sds(a.shape) for a in ws]
    res = pl.pallas_call(
        body, in_specs=[vm] * (4 * n), out_specs=[vm] * (3 * n), out_shape=shapes * 3, name="adamw_small",
        compiler_params=pltpu.CompilerParams(vmem_limit_bytes=VMEM_LIMIT))(*gs, *ws, *ms, *vs)
    return res[:n], res[n:2 * n], res[2 * n:]


_ARGS = ("x", "w_in", "s5_a_re", "s5_a_im", "s5_log_step", "s5_b_re", "s5_b_im", "s5_c_re", "s5_c_im", "s5_d",
         "s5_w_glu", "s5_b_glu", "gla_w_a", "gla_b_a", "gla_ln_g", "swa_sink", "w_out", "ln1_g", "ln1_b", "w_ff1",
         "w_ff2", "ln2_g", "ln2_b")
_WEIGHTS = _ARGS[1:]


def _shard_cols(d):
    return d.reshape(d.shape[0], NSHARD, d.shape[1] // NSHARD).transpose(1, 0, 2)


def kernel(x, w_in, s5_a_re, s5_a_im, s5_log_step, s5_b_re, s5_b_im, s5_c_re, s5_c_im, s5_d, s5_w_glu, s5_b_glu, gla_w_a, gla_b_a, gla_ln_g, swa_sink, w_out, ln1_g, ln1_b, w_ff1, w_ff2, ln2_g, ln2_b, loss_target, m_w_in, m_s5_a_re, m_s5_a_im, m_s5_log_step, m_s5_b_re, m_s5_b_im, m_s5_c_re, m_s5_c_im, m_s5_d, m_s5_w_glu, m_s5_b_glu, m_gla_w_a, m_gla_b_a, m_gla_ln_g, m_swa_sink, m_w_out, m_ln1_g, m_ln1_b, m_w_ff1, m_w_ff2, m_ln2_g, m_ln2_b, v_w_in, v_s5_a_re, v_s5_a_im, v_s5_log_step, v_s5_b_re, v_s5_b_im, v_s5_c_re, v_s5_c_im, v_s5_d, v_s5_w_glu, v_s5_b_glu, v_gla_w_a, v_gla_b_a, v_gla_ln_g, v_swa_sink, v_w_out, v_ln1_g, v_ln1_b, v_w_ff1, v_w_ff2, v_ln2_g, v_ln2_b):
    given = dict(locals())
    w = {k: given[k] for k in _WEIGHTS}
    mom = {k: given["m_" + k] for k in _WEIGHTS}
    var = {k: given["v_" + k] for k in _WEIGHTS}

    me = (2 * lax.axis_index("x") + lax.axis_index("y")).astype(jnp.int32).reshape(1)
    tr = lambda t: t.transpose(0, 2, 1)
    shard = {k: (tr(w[k]) if k == "w_in" else w[k]) for k in BIG}
    qs = [_layer_prep({k: w[k][l] for k in SMALL}) for l in range(DEPTH)]

    first = ("w_in", "s5_w_glu", "w_out")
    follow = {(0, "w_in"): [(0, BIG[3:]), (1, first)], (0, "w_ff1"): [(1, BIG[3:])]}
    gathers = {}

    def start_gather(l, names, behind=None):
        lands = [_cast_to_slot(me, shard[k], l) for k in names]
        if behind is not None:
            lands, behind = lax.optimization_barrier((lands, behind))
        st = _push_start(f"gather_start_{l}_{names[0]}", lands, True)
        for k in names:
            gathers[l, k] = [names, st, None]
        return st[-1], behind

    token = start_gather(0, first[:1])[0] + start_gather(0, first[1:])[0]

    def fetch(l, name, after):
        names, st, got = gathers[l, name]
        tie = None
        if got is None:
            if l == 0 and name == "w_in":
                after = token
            lands = _push_wait(f"gather_wait_{l}_{names[0]}", st, after, True)
            for l2, names2 in follow.get((l, name), ()):
                tok, lands[0] = start_gather(l2, names2, lands[0])
                tie = tok if tie is None else tie + tok
            got = dict(zip(names, lands))
            for k in names:
                gathers[l, k][2] = got
        full = got[name]
        if name == "w_in":
            return _in_rows(full, token if tie is None else tie)
        if tie is not None:
            qs[l]["ln2_b"] = qs[l]["ln2_b"] + tie[0, 0]
        return full.reshape(D, D) if name == "w_out" else full

    scatters = []

    def emit(l, grads):
        names = tuple(grads)
        st = _push_start(f"scatter_start_{l}_{names[0]}", [grads[k] for k in names], False)
        scatters.append((l, names, st))
        return st[-1][0, 0]

    loss, dx, smalls = _local_step(x.reshape(N, D), loss_target.reshape(N, D), qs, _rope_tables(128), fetch, emit)

    out = {}
    native = _allreduce_small([[smalls[l][k] for k in NATIVE] for l in range(DEPTH)])
    native = dict(zip(NATIVE, native))
    loss = native["loss"][0, 0, 0] + native["loss"][1, 0, 0]
    gsmall = _finish_small(native, w)
    res = _adamw_small(*([t[k] for k in SMALL] for t in (gsmall, w, mom, var)))
    for i, k in enumerate(SMALL):
        out[k] = [gsmall[k], res[0][i], res[1][i], res[2][i]]

    recv, own = {}, {}

    def finish(keys, after):
        for l, names, st in scatters:
            if names[0] in keys:
                ops = _push_wait(f"scatter_wait_{l}_{names[0]}", st, after, False)
                for i, k in enumerate(names):
                    own[l, k], recv[l, k] = ops[i], ops[len(names) + i]
        sums = [_sum_sources(me, [recv[l, k] for l in range(DEPTH)], [own[l, k] for l in range(DEPTH)]) for k in keys]
        for k, mine, other in zip(keys, sums, _swap_sibling(sums)):
            shp = shard[k].shape
            r = _adamw([mine, other], *((tr(t[k]) if k == "w_in" else t[k]).reshape(-1, shp[-1]) for t in (w, mom, var)))
            r = [t.reshape(shp) for t in r]
            out[k] = [tr(t) for t in r] if k == "w_in" else r
        return out[keys[-1]][1]

    last = finish(("w_ff1", "w_ff2", "w_out", "s5_w_glu"), res[0][-1])
    finish(("w_in",), last)

    return (loss, dx.reshape(NSEQ, L, D), *[out[k][0] for k in _WEIGHTS], *[out[k][1] for k in _WEIGHTS],
            *[out[k][2] for k in _WEIGHTS], *[out[k][3] for k in _WEIGHTS])
```

```python
import functools
import math

import jax
import jax.numpy as jnp
from jax import lax
from jax.experimental import pallas as pl
from jax.experimental.pallas import tpu as pltpu

F32 = jnp.float32
MX = jnp.bfloat16
MESH = pl.DeviceIdType.MESH

DEPTH = 2
NSEQ = 2
L = 2048
N = NSEQ * L
D = 1024
DFF = 4096
NSHARD = 4
S5_G, S5_H, S5_P = 16, 16, 64
GLA_CHUNK = 64
NCHUNK = L // GLA_CHUNK
GLA_GROUP = 4
NGROUP = NCHUNK // GLA_GROUP
SWA_BLK = 128
NBLK = L // SWA_BLK
ROT = 16
ROPE_THETA = 500000.0
LN_EPS = 1e-5
ALPHA = (2 * DEPTH) ** 0.25
NEG_BIG = -1e30
DIN = 1824
DINP = 1920
ADAM_LR, ADAM_B1, ADAM_B2, ADAM_EPS, ADAM_WD, ADAM_STEP = 0.001, 0.9, 0.999, 1e-08, 0.01, 10
VMEM_LIMIT = 56 * 1024 * 1024
TT = 512
SW = 512
FFN_TM = 512
FFN_TM_W = 1024
FFN_WB = 1
FFN_VMEM = 60 * 1024 * 1024
INPROJ_BWD_TM = 512


def _cp(sem, vmem=VMEM_LIMIT):
    return pltpu.CompilerParams(dimension_semantics=sem, vmem_limit_bytes=vmem)


def _mm(a, b):
    return jnp.dot(a.astype(MX), b.astype(MX), preferred_element_type=F32)


def _mm_nt(a, b):
    return lax.dot_general(a.astype(MX), b.astype(MX), (((1,), (1,)), ((), ())), preferred_element_type=F32)


def _mm_tn(a, b):
    return lax.dot_general(a.astype(MX), b.astype(MX), (((0,), (0,)), ((), ())), preferred_element_type=F32)


@jax.custom_vjp
def _dmm(a, b):
    return _mm(a, b)


_dmm.defvjp(lambda a, b: (_mm(a, b), (a, b)), lambda r, g: (_mm_nt(g, r[1]), _mm_tn(r[0], g)))


@jax.custom_vjp
def _dmm_nt(a, b):
    return _mm_nt(a, b)


_dmm_nt.defvjp(lambda a, b: (_mm_nt(a, b), (a, b)), lambda r, g: (_mm(g, r[1]), _mm_tn(g, r[0])))


@jax.custom_vjp
def _dmm_tn(a, b):
    return _mm_tn(a, b)


_dmm_tn.defvjp(lambda a, b: (_mm_tn(a, b), (a, b)), lambda r, g: (_mm_nt(r[1], g), _mm(r[0], g)))


def _split3(x):
    hi = x.astype(MX)
    r1 = x - hi.astype(F32)
    mid = r1.astype(MX)
    lo = (r1 - mid.astype(F32)).astype(MX)
    return hi, mid, lo


def _chunk_pairs(rows, rev, strict):
    r = lax.broadcasted_iota(jnp.int32, (rows, rows), 0)
    c = lax.broadcasted_iota(jnp.int32, (rows, rows), 1)
    order = ((c > r) if strict else (c >= r)) if rev else ((c < r) if strict else (c <= r))
    return (r // GLA_CHUNK == c // GLA_CHUNK) & order


def _cums_impl(x, rev):
    rows, w = x.shape
    t = jnp.where(_chunk_pairs(rows, rev, False), 1.0, 0.0).astype(MX)
    s = jnp.dot(t, jnp.concatenate(_split3(x), axis=1), preferred_element_type=F32)
    return s[:, 0:w] + s[:, w:2 * w] + s[:, 2 * w:3 * w]


@functools.partial(jax.custom_vjp, nondiff_argnums=(1,))
def _cums(x, rev):
    return _cums_impl(x, rev)


_cums.defvjp(lambda x, rev: (_cums_impl(x, rev), None), lambda rev, r, g: (_cums_impl(g, not rev),))


def _ln_fwd(s, g, b):
    mu = jnp.mean(s, axis=-1, keepdims=True)
    xc = s - mu
    var = jnp.mean(xc * xc, axis=-1, keepdims=True)
    return xc * lax.rsqrt(var + LN_EPS) * g + b


def _ln_bwd(dy, s, g):
    mu = jnp.mean(s, axis=-1, keepdims=True)
    xc = s - mu
    var = jnp.mean(xc * xc, axis=-1, keepdims=True)
    rstd = lax.rsqrt(var + LN_EPS)
    xhat = xc * rstd
    dxh = dy * g
    ds = rstd * (dxh - jnp.mean(dxh, axis=-1, keepdims=True) - xhat * jnp.mean(dxh * xhat, axis=-1, keepdims=True))
    return ds, jnp.sum(dy * xhat, axis=0, keepdims=True), jnp.sum(dy, axis=0, keepdims=True)


def _sds(shape, dtype=F32):
    return jax.ShapeDtypeStruct(shape, dtype)


_IN_ROW_PIECES = (((0, 0), (0, 456)), ((1, 0), (456, 456)), ((2, 0), (912, 112)), ((2, 112), (1792, 32)),
                  ((2, 144), (1024, 312)), ((3, 0), (1336, 456)))


def _in_rows(g4, behind):
    def body(g_ref, behind_ref, o_ref, tmp):
        tmp[DIN:DINP] = jnp.zeros((DINP - DIN, D), F32)
        for (j, s0), (d0, n_) in _IN_ROW_PIECES:
            tmp[d0:d0 + n_] = g_ref[j, s0:s0 + n_].astype(F32)
        o_ref[...] = tmp[...].astype(MX)

    vm = pl.BlockSpec(memory_space=pltpu.VMEM)
    return pl.pallas_call(body, in_specs=[vm, pl.BlockSpec(memory_space=pl.ANY)], out_specs=vm,
                          out_shape=_sds((DINP, D), MX), scratch_shapes=[pltpu.VMEM((DINP, D), F32)], name="in_rows",
                          compiler_params=pltpu.CompilerParams(vmem_limit_bytes=VMEM_LIMIT))(g4, behind)


def _inproj_fwd(x, wt):
    tm = 512

    def body(x_ref, w_ref, h_ref):
        h_ref[...] = _mm_nt(x_ref[...], w_ref[...])

    return pl.pallas_call(
        body, grid=(N // tm,),
        in_specs=[pl.BlockSpec((tm, D), lambda i: (i, 0)), pl.BlockSpec((DINP, D), lambda i: (0, 0))],
        out_specs=pl.BlockSpec((tm, DINP), lambda i: (i, 0)),
        out_shape=_sds((N, DINP)), name="inproj_fwd", compiler_params=_cp(("parallel",)))(x, wt)


def _inproj_bwd(x, w, dxp, du2, dud, gq_f, gq_b, gk_f, gk_b, gv_f, gv_b, gr, daq, dakv, dhl):
    tm = INPROJ_BWD_TM
    nt = N // tm

    def body(x_ref, w_ref, dxp_ref, du2_ref, dud_ref, gqf, gqb, gkf, gkb, gvf, gvb, gr_ref, daq_ref, dakv_ref, dhl_ref,
             dx_ref, dw_ref, acc):
        i = pl.program_id(0)
        f = lambda r: r[...].astype(F32)
        dh = jnp.concatenate([
            du2_ref[0] + du2_ref[1] + f(dud_ref), f(gqf) + f(gqb), f(gkf) + f(gkb), f(gvf) + f(gvb),
            f(gr_ref), f(daq_ref), f(dakv_ref), f(dhl_ref)], axis=1)
        dx_ref[...] = dxp_ref[...] + _mm(dh, w_ref[...])
        contrib = _mm_tn(dh, x_ref[...])

        @pl.when(i == 0)
        def _():
            acc[...] = contrib

        @pl.when(i > 0)
        def _():
            acc[...] += contrib

        @pl.when(i == nt - 1)
        def _():
            for (j, d0), (s0, n_) in _IN_ROW_PIECES:
                dw_ref[j, d0:d0 + n_] = acc[s0:s0 + n_].astype(MX)

    row = lambda w_: pl.BlockSpec((tm, w_), lambda i: (i, 0))
    return pl.pallas_call(
        body, grid=(nt,),
        in_specs=[row(D), pl.BlockSpec((DINP, D), lambda i: (0, 0)), row(D),
                  pl.BlockSpec((2, tm, 256), lambda i: (0, i, 0)), row(256), row(128), row(128), row(128), row(128),
                  row(256), row(256), row(256), row(512), row(256), row(128)],
        out_specs=[row(D), pl.BlockSpec((NSHARD, DIN // NSHARD, D), lambda i: (0, 0, 0))],
        out_shape=[_sds((N, D)), _sds((NSHARD, DIN // NSHARD, D), MX)],
        scratch_shapes=[pltpu.VMEM((DINP, D), F32)],
        name="inproj_bwd", compiler_params=_cp(("arbitrary",)))(
            x, w, dxp, du2, dud, gq_f, gq_b, gk_f, gk_b, gv_f, gv_b, gr, daq, dakv, dhl)


def _scan_tables(mr, mi, reverse):
    pw = [(mr, mi)]
    for _ in range(7):
        pr, pi = pw[-1]
        pw.append((pr * mr - pi * mi, pr * mi + pi * mr))
    rows = jnp.arange(8)[:, None]
    out = []
    for d in (1, 2, 4):
        keep = rows >= d
        out += [jnp.where(keep, pw[d - 1][0][None], 0.0), jnp.where(keep, pw[d - 1][1][None], 0.0)]
    out += [jnp.stack([p[0] for p in pw]), jnp.stack([p[1] for p in pw])]
    t = jnp.stack(out)
    if reverse:
        t = t[:, ::-1, :]
    return t.reshape(8, 8, 2, SW).transpose(2, 0, 1, 3)


def _tile_scan(xr, xi, a, cr, ci, reverse):
    for lvl, d in enumerate((1, 2, 4)):
        sh = 8 - d if reverse else d
        sr = pltpu.roll(xr, sh, 0)
        si = pltpu.roll(xi, sh, 0)
        ar, ai = a[2 * lvl], a[2 * lvl + 1]
        xr, xi = xr + ar * sr - ai * si, xi + ar * si + ai * sr
    pr, pi = a[6], a[7]
    return xr + pr * cr - pi * ci, xi + pr * ci + pi * cr


NJ = TT // 8


def _lockstep_tables(mr, mi, reverse):
    nr, ni = mr, mi
    for _ in range(NJ.bit_length() - 1):
        nr, ni = nr * nr - ni * ni, 2.0 * nr * ni
    pr, pi = mr[None], mi[None]
    while pr.shape[0] < NJ:
        k = pr.shape[0]
        tr, ti = pr[k - 1], pi[k - 1]
        pr, pi = (jnp.concatenate([pr, pr * tr - pi * ti]), jnp.concatenate([pi, pr * ti + pi * tr]))
    if reverse:
        pr, pi = pr[::-1], pi[::-1]
    rows = jnp.broadcast_to(jnp.stack([mr, mi])[:, None, :], (2, 8, 2 * SW))
    link = _scan_tables(nr, ni, reverse)
    a = jnp.concatenate([rows.reshape(2, 8, 2, SW).transpose(2, 0, 1, 3), link], axis=1)
    return a, jnp.stack([pr, pi]).reshape(2, NJ, 2, SW).transpose(2, 0, 1, 3)


def _to_lockstep(ref, *lead):
    return jnp.concatenate([ref[(*lead, pl.ds(j, 8, stride=NJ), slice(None))] for j in range(NJ)], axis=0)


def _from_lockstep(val, ref, *lead):
    for j in range(NJ):
        ref[(*lead, pl.ds(j, 8, stride=NJ), slice(None))] = val[8 * j:8 * j + 8]


def _expand_powers(p_ref, pexp):
    for c in range(2):
        for j in range(NJ):
            pexp[c, j] = jnp.broadcast_to(p_ref[0, 0, c, j:j + 1, :], (8, SW))


def _lockstep_scan(xre, xim, a_ref, pexp, car, reverse, extra=None):
    a = [a_ref[0, 0, k] for k in range(10)]
    mr, mi = a[0], a[1]
    order = (lambda i: NJ - 1 - i) if reverse else (lambda i: i)

    def local(i, hcar):
        hr, hi = hcar
        r0 = pl.multiple_of(order(i) * 8, 8)
        hr, hi = mr * hr - mi * hi + xre[pl.ds(r0, 8), :], mr * hi + mi * hr + xim[pl.ds(r0, 8), :]
        xre[pl.ds(r0, 8), :] = hr
        xim[pl.ds(r0, 8), :] = hi
        return hr, hi

    z8 = jnp.zeros((8, SW), F32)
    er, ei = lax.fori_loop(0, NJ, local, (z8, z8), unroll=4)
    c0r, c0i = car[0], car[1]
    er, ei = _tile_scan(er, ei, a[2:], c0r, c0i, reverse)
    rowid = lax.broadcasted_iota(jnp.int32, (8, SW), 0)
    first, sh, last = (7, 7, 0) if reverse else (0, 1, 7)
    cvr = jnp.where(rowid == first, c0r, pltpu.roll(er, sh, 0))
    cvi = jnp.where(rowid == first, c0i, pltpu.roll(ei, sh, 0))
    car[0] = jnp.broadcast_to(er[last:last + 1, :], (8, SW))
    car[1] = jnp.broadcast_to(ei[last:last + 1, :], (8, SW))

    def fix(i, carry):
        j = order(i)
        r0 = pl.multiple_of(j * 8, 8)
        pr, pi = pexp[0, j], pexp[1, j]
        sr = xre[pl.ds(r0, 8), :] + pr * cvr - pi * cvi
        si = xim[pl.ds(r0, 8), :] + pr * cvi + pi * cvr
        xre[pl.ds(r0, 8), :] = sr
        xim[pl.ds(r0, 8), :] = si
        if extra is None:
            return carry
        return (sr, si, extra(r0, sr, si, carry[0], carry[1], carry[2]))

    init = (cvr, cvi, extra(None, None, None, None, None, None)) if extra is not None else 0
    return lax.fori_loop(0, NJ, fix, init, unroll=4)


def _s5_time_block(z, s, t, adjoint):
    flip = (1 - z) if adjoint else z
    return s * (L // TT) + t + flip * (L // TT - 1 - 2 * t)


def _s5_fwd(h, bre, bim, cre, cim, tab):
    nt = L // TT
    taba, tabp = tab

    def body(u_ref, bre_ref, bim_ref, cre_ref, cim_ref, a_ref, p_ref, hre_ref, him_ref, y_ref, car, pexp):
        z = pl.program_id(1)
        s = pl.program_id(2)
        tc = pl.program_id(3)

        @pl.when(tc == 0)
        def _():
            car[...] = jnp.zeros_like(car)

        @pl.when((tc == 0) & (s == 0))
        def _():
            _expand_powers(p_ref, pexp)

        u = _to_lockstep(u_ref)
        hre_ref[0] = _mm(u, bre_ref[0, 0])
        him_ref[0] = _mm(u, bim_ref[0, 0])

        @pl.when(z == 0)
        def _():
            _lockstep_scan(hre_ref.at[0], him_ref.at[0], a_ref, pexp, car, False)

        @pl.when(z == 1)
        def _():
            _lockstep_scan(hre_ref.at[0], him_ref.at[0], a_ref, pexp, car, True)

        _from_lockstep(_mm(hre_ref[0], cre_ref[0, 0]) - _mm(him_ref[0], cim_ref[0, 0]), y_ref, 0)

    tb = lambda b, z, s, t: _s5_time_block(z, s, t, False)
    wspec = lambda r, c: pl.BlockSpec((1, 1, r, c), lambda b, z, s, t: (z, b, 0, 0))
    return pl.pallas_call(
        body, grid=(2, 2, NSEQ, nt),
        in_specs=[pl.BlockSpec((TT, 128), lambda b, z, s, t: (tb(b, z, s, t), b)),
                  wspec(128, SW), wspec(128, SW), wspec(SW, 128), wspec(SW, 128),
                  pl.BlockSpec((1, 1, 10, 8, SW), lambda b, z, s, t: (z, b, 0, 0, 0)),
                  pl.BlockSpec((1, 1, 2, NJ, SW), lambda b, z, s, t: (z, b, 0, 0, 0))],
        out_specs=[pl.BlockSpec((1, TT, SW), lambda b, z, s, t: (z, tb(b, z, s, t), b)),
                   pl.BlockSpec((1, TT, SW), lambda b, z, s, t: (z, tb(b, z, s, t), b)),
                   pl.BlockSpec((1, TT, 128), lambda b, z, s, t: (z, tb(b, z, s, t), b))],
        out_shape=[_sds((2, N, 2 * SW)), _sds((2, N, 2 * SW)), _sds((2, N, 256))],
        scratch_shapes=[pltpu.VMEM((2, 8, SW), F32), pltpu.VMEM((2, NJ, 8, SW), F32)],
        name="s5_fwd", compiler_params=_cp(("arbitrary",) * 4))(h, bre, bim, cre, cim, taba, tabp)


def _s5_bwd(h, dyp, hre, him, bre, bim, cre, cim, tabc):
    nt = L // TT
    taba, tabp = tabc

    def body(u_ref, dy_ref, hre_ref, him_ref, bre_ref, bim_ref, cre_ref, cim_ref, a_ref, p_ref,
             du_ref, dbre_ref, dbim_ref, dcre_ref, dcim_ref, dmu_ref, gre, gim, car, acc, macc, pexp):
        z = pl.program_id(1)
        s = pl.program_id(2)
        tc = pl.program_id(3)

        @pl.when(tc == 0)
        def _():
            car[...] = jnp.zeros_like(car)

        @pl.when((tc == 0) & (s == 0))
        def _():
            acc[...] = jnp.zeros_like(acc)
            macc[...] = jnp.zeros_like(macc)
            _expand_powers(p_ref, pexp)

        dy = _to_lockstep(dy_ref)
        gre[...] = _mm_nt(dy, cre_ref[0, 0])
        gim[...] = -_mm_nt(dy, cim_ref[0, 0])

        def run(reverse):
            def pair(r0, gr_, gi_, pvr, pvi, m):
                if r0 is None:
                    return (macc[0], macc[1])
                hr = hre_ref[0, pl.ds(r0, 8), :]
                hi = him_ref[0, pl.ds(r0, 8), :]
                return (m[0] + pvr * hr + pvi * hi, m[1] + pvi * hr - pvr * hi)

            _, _, (dmr, dmi) = _lockstep_scan(gre, gim, a_ref, pexp, car, reverse, pair)
            macc[0] = dmr
            macc[1] = dmi

        @pl.when(z == 0)
        def _():
            run(True)

        @pl.when(z == 1)
        def _():
            run(False)

        gr = gre[...]
        gi = gim[...]
        u = _to_lockstep(u_ref)
        _from_lockstep(_mm_nt(gr, bre_ref[0, 0]) + _mm_nt(gi, bim_ref[0, 0]), du_ref, 0)
        acc[0] += _mm_tn(u, gr)
        acc[1] += _mm_tn(u, gi)
        acc[2] += _mm_tn(dy, hre_ref[0])
        acc[3] -= _mm_tn(dy, him_ref[0])

        @pl.when((tc == nt - 1) & (s == NSEQ - 1))
        def _():
            grp = lax.broadcasted_iota(jnp.int32, (S5_H, SW), 1) // S5_P
            for k, out in enumerate((dbre_ref, dbim_ref, dcre_ref, dcim_ref)):
                c = jnp.zeros((S5_H, SW), F32)
                for i in range(8):
                    c = c + jnp.where(grp == i, acc[k, i * S5_H:(i + 1) * S5_H, :], 0.0)
                out[0, 0] = c
            dmu_ref[0, 0] = jnp.concatenate([jnp.sum(macc[0], axis=0, keepdims=True),
                                             jnp.sum(macc[1], axis=0, keepdims=True)], axis=0)

    tb = lambda b, z, s, t: _s5_time_block(z, s, t, True)
    wspec = lambda r, c: pl.BlockSpec((1, 1, r, c), lambda b, z, s, t: (z, b, 0, 0))
    tok = lambda w_: pl.BlockSpec((TT, w_), lambda b, z, s, t: (tb(b, z, s, t), b))
    st = pl.BlockSpec((1, TT, SW), lambda b, z, s, t: (z, tb(b, z, s, t), b))
    return pl.pallas_call(
        body, grid=(2, 2, NSEQ, nt),
        in_specs=[tok(128), tok(128), st, st, wspec(128, SW), wspec(128, SW), wspec(SW, 128), wspec(SW, 128),
                  pl.BlockSpec((1, 1, 10, 8, SW), lambda b, z, s, t: (z, b, 0, 0, 0)),
                  pl.BlockSpec((1, 1, 2, NJ, SW), lambda b, z, s, t: (z, b, 0, 0, 0))],
        out_specs=[pl.BlockSpec((1, TT, 128), lambda b, z, s, t: (z, tb(b, z, s, t), b)),
                   wspec(S5_H, SW), wspec(S5_H, SW), wspec(S5_H, SW), wspec(S5_H, SW),
                   wspec(2, SW)],
        out_shape=[_sds((2, N, 256))] + [_sds((2, 2, S5_H, SW))] * 4 + [_sds((2, 2, 2, SW))],
        scratch_shapes=[pltpu.VMEM((TT, SW), F32), pltpu.VMEM((TT, SW), F32), pltpu.VMEM((2, 8, SW), F32),
                        pltpu.VMEM((4, 128, SW), F32), pltpu.VMEM((2, 8, SW), F32), pltpu.VMEM((2, NJ, 8, SW), F32)],
        name="s5_bwd", compiler_params=_cp(("arbitrary",) * 4))(h, dyp, hre, him, bre, bim, cre, cim, taba, tabp)


_GELU_C = math.sqrt(2.0 / math.pi)


def _gelu(y):
    return 0.5 * y * (1.0 + jnp.tanh(_GELU_C * (y + 0.044715 * y * y * y)))


def _gelu_grad(y):
    t = jnp.tanh(_GELU_C * (y + 0.044715 * y * y * y))
    return 0.5 * (1.0 + t) + 0.5 * y * (1.0 - t * t) * _GELU_C * (1.0 + 3 * 0.044715 * y * y)


def _glu_halves(w4_ref):
    return (jnp.concatenate([w4_ref[0], w4_ref[1]], axis=1), jnp.concatenate([w4_ref[2], w4_ref[3]], axis=1))


def _s5_glu_fwd(y2, h, dsk, w4, bv, bg):
    tm = 512

    def body(y2_ref, u_ref, d_ref, w4_ref, bv_ref, bg_ref, ya_ref):
        wv, wg = _glu_halves(w4_ref)
        z = _gelu(y2_ref[0] + y2_ref[1] + d_ref[...] * u_ref[...])
        val = _mm(z, wv) + bv_ref[...]
        gate = _mm(z, wg) + bg_ref[...]
        ya_ref[...] = (val * jax.nn.sigmoid(gate)).astype(MX)

    full = lambda r, c: pl.BlockSpec((r, c), lambda i: (0, 0))
    return pl.pallas_call(
        body, grid=(N // tm,),
        in_specs=[pl.BlockSpec((2, tm, 256), lambda i: (0, i, 0)), pl.BlockSpec((tm, 256), lambda i: (i, 0)),
                  full(1, 256), pl.BlockSpec((NSHARD, 256, 128), lambda i: (0, 0, 0)), full(1, 256), full(1, 256)],
        out_specs=pl.BlockSpec((tm, 256), lambda i: (i, 0)),
        out_shape=_sds((N, 256), MX), name="s5_glu_fwd", compiler_params=_cp(("parallel",)))(y2, h, dsk, w4, bv, bg)


def _s5_glu_bwd(y2, h, dsk, w4, bv, bg, dya):
    tm = 512
    nt = N // tm

    def body(y2_ref, u_ref, d_ref, w4_ref, bv_ref, bg_ref, dya_ref,
             dyp_ref, dud_ref, dd_ref, dw4_ref, dbv_ref, dbg_ref, accv, accg):
        i = pl.program_id(0)

        @pl.when(i == 0)
        def _():
            for r in (dd_ref, accv, accg, dbv_ref, dbg_ref):
                r[...] = jnp.zeros_like(r)

        wv, wg = _glu_halves(w4_ref)
        u = u_ref[...]
        y = y2_ref[0] + y2_ref[1] + d_ref[...] * u
        z = _gelu(y)
        val = _mm(z, wv) + bv_ref[...]
        sig = jax.nn.sigmoid(_mm(z, wg) + bg_ref[...])
        dya = dya_ref[...]
        dval = dya * sig
        dgate = dya * val * sig * (1.0 - sig)
        dz = _mm_nt(dval, wv) + _mm_nt(dgate, wg)
        dy = dz * _gelu_grad(y)
        dyp_ref[...] = dy
        dud_ref[...] = (dy * d_ref[...]).astype(MX)
        dd_ref[...] += jnp.sum(dy * u, axis=0, keepdims=True)
        accv[...] += _mm_tn(z, dval)
        accg[...] += _mm_tn(z, dgate)
        dbv_ref[...] += jnp.sum(dval, axis=0, keepdims=True)
        dbg_ref[...] += jnp.sum(dgate, axis=0, keepdims=True)

        @pl.when(i == nt - 1)
        def _():
            dw4_ref[0] = accv[:, 0:128].astype(MX)
            dw4_ref[1] = accv[:, 128:256].astype(MX)
            dw4_ref[2] = accg[:, 0:128].astype(MX)
            dw4_ref[3] = accg[:, 128:256].astype(MX)

    full = lambda r, c: pl.BlockSpec((r, c), lambda i: (0, 0))
    row = pl.BlockSpec((tm, 256), lambda i: (i, 0))
    wspec = pl.BlockSpec((NSHARD, 256, 128), lambda i: (0, 0, 0))
    return pl.pallas_call(
        body, grid=(nt,),
        in_specs=[pl.BlockSpec((2, tm, 256), lambda i: (0, i, 0)), row, full(1, 256), wspec, full(1, 256), full(1, 256),
                  row],
        out_specs=[row, row, full(1, 256), wspec, full(1, 256), full(1, 256)],
        out_shape=[_sds((N, 256)), _sds((N, 256), MX), _sds((1, 256)), _sds((NSHARD, 256, 128), MX), _sds((1, 256)),
                   _sds((1, 256))],
        scratch_shapes=[pltpu.VMEM((256, 256), F32), pltpu.VMEM((256, 256), F32)],
        name="s5_glu_bwd", compiler_params=_cp(("arbitrary",)))(y2, h, dsk, w4, bv, bg, dya)


def _logsig(x):
    return jnp.minimum(x, 0.0) - jnp.log(1.0 + jnp.exp(-jnp.abs(x)))


def _gla_gate_fwd(h, wa, ba):
    tm = 512

    def body(hl_ref, wa_ref, ba_ref, la_ref):
        la_ref[...] = _logsig(_mm(hl_ref[...], wa_ref[...]) + ba_ref[...]) * (1.0 / 16.0)

    return pl.pallas_call(
        body, grid=(N // tm,),
        in_specs=[pl.BlockSpec((tm, 128), lambda i: (i, 14)), pl.BlockSpec((128, 256), lambda i: (0, 0)),
                  pl.BlockSpec((1, 256), lambda i: (0, 0))],
        out_specs=pl.BlockSpec((tm, 256), lambda i: (i, 0)),
        out_shape=_sds((N, 256)), name="gla_gate_fwd", compiler_params=_cp(("parallel",)))(h, wa, ba)


def _gla_gate_bwd(h, wa, ba, dla_f, dla_b):
    tm = 512

    def body(hl_ref, wa_ref, ba_ref, df_ref, db_ref, dhl_ref, dwa_ref, dba_ref):
        i = pl.program_id(0)

        @pl.when(i == 0)
        def _():
            dwa_ref[...] = jnp.zeros_like(dwa_ref)
            dba_ref[...] = jnp.zeros_like(dba_ref)

        hl = hl_ref[...]
        pre = _mm(hl, wa_ref[...]) + ba_ref[...]
        dpre = jnp.concatenate([df_ref[...], db_ref[...]], axis=1) * (1.0 / 16.0) * jax.nn.sigmoid(-pre)
        dhl_ref[...] = _mm_nt(dpre, wa_ref[...]).astype(MX)
        dwa_ref[...] += _mm_tn(hl, dpre)[0:32]
        dba_ref[...] += jnp.sum(dpre, axis=0, keepdims=True)

    row = pl.BlockSpec((tm, 128), lambda i: (i, 0))
    return pl.pallas_call(
        body, grid=(N // tm,),
        in_specs=[pl.BlockSpec((tm, 128), lambda i: (i, 14)), pl.BlockSpec((128, 256), lambda i: (0, 0)),
                  pl.BlockSpec((1, 256), lambda i: (0, 0)), row, row],
        out_specs=[row, pl.BlockSpec((32, 256), lambda i: (0, 0)), pl.BlockSpec((1, 256), lambda i: (0, 0))],
        out_shape=[_sds((N, 128), MX), _sds((32, 256)), _sds((1, 256))],
        name="gla_gate_bwd", compiler_params=_cp(("arbitrary",)))(h, wa, ba, dla_f, dla_b)


def _gla_chunk(q, k, v, la, st, rev):
    c = GLA_CHUNK
    rows = q.shape[0]
    nch = rows // c
    b = _cums(la, rev)
    blc = [jnp.sum(la[i * c:(i + 1) * c], axis=0, keepdims=True) for i in range(nch)]
    bl = jnp.concatenate([jnp.broadcast_to(t, (c, 128)) for t in blc], axis=0)
    q_in = q * (32.0 ** -0.5) * jnp.exp(b)
    k_in = k * jnp.exp(-b)
    k_st = k * jnp.exp(bl - b)
    lane_k = lax.broadcasted_iota(jnp.int32, (1, 128), 1) // 32
    lane_v = lax.broadcasted_iota(jnp.int32, (1, 256), 1) // 64
    qs = jnp.concatenate([jnp.where(lane_k == hd, q_in, 0.0) for hd in range(4)], axis=0)
    a = _dmm_nt(qs, k_in)
    a = jnp.where(jnp.concatenate([_chunk_pairs(rows, rev, rev)] * 4, axis=0), a, 0.0)
    o4 = _dmm(a, v)
    o = jnp.zeros((rows, 256), F32)
    for hd in range(4):
        o = o + jnp.where(lane_v == hd, o4[hd * rows:(hd + 1) * rows], 0.0)
    bd = (lax.broadcasted_iota(jnp.int32, (256, 128), 0) // 64) == (lax.broadcasted_iota(jnp.int32, (256, 128), 1) // 32)
    inter = [None] * nch
    for i in (reversed(range(nch)) if rev else range(nch)):
        sl = slice(i * c, (i + 1) * c)
        inter[i] = _dmm_nt(q_in[sl], st)
        st = jnp.exp(blc[i]) * st + jnp.where(bd, _dmm_tn(v[sl], k_st[sl]), 0.0)
    return o + jnp.concatenate(inter, axis=0), st


def _gla_chunk_of(c, rev):
    return NGROUP - 1 - c if rev else c


def _gla_fwd(h, la2):
    c = GLA_GROUP * GLA_CHUNK

    def body(qf, kf, vf, laf, qb, kb, vb, lab, of_ref, ob_ref, sf_ref, sb_ref, stf, stb):
        @pl.when(pl.program_id(0) == 0)
        def _():
            stf[...] = jnp.zeros_like(stf)
            stb[...] = jnp.zeros_like(stb)

        ins = [(qf[s], kf[s], vf[s], laf[s], stf[s], qb[s], kb[s], vb[s], lab[s], stb[s]) for s in range(NSEQ)]
        outs = [(_gla_chunk(*t[:5], False), _gla_chunk(*t[5:], True)) for t in ins]
        for s in range(NSEQ):
            sf_ref[s, 0] = ins[s][4]
            sb_ref[s, 0] = ins[s][9]
            (of_ref[s], stf[s]), (ob_ref[s], stb[s]) = outs[s]

    def specs(rev):
        ch = lambda i: _gla_chunk_of(i, rev)
        return [pl.BlockSpec((NSEQ, c, 128), lambda i: (0, ch(i), 2)), pl.BlockSpec((NSEQ, c, 128), lambda i: (0, ch(i), 3)),
                pl.BlockSpec((NSEQ, c, 256), lambda i: (0, ch(i), 2)),
                pl.BlockSpec((NSEQ, c, 128), lambda i: (0, ch(i), 1 if rev else 0))]

    orow = lambda rev: pl.BlockSpec((NSEQ, c, 256), lambda i: (0, _gla_chunk_of(i, rev), 0))
    srow = lambda rev: pl.BlockSpec((NSEQ, 1, 256, 128), lambda i: (0, _gla_chunk_of(i, rev), 0, 0))
    h3, la3 = h.reshape(NSEQ, L, DINP), la2.reshape(NSEQ, L, 256)
    of, ob, sf, sb = pl.pallas_call(
        body, grid=(NGROUP,),
        in_specs=specs(False) + specs(True),
        out_specs=[orow(False), orow(True), srow(False), srow(True)],
        out_shape=[_sds((NSEQ, L, 256)), _sds((NSEQ, L, 256)), _sds((NSEQ, NGROUP, 256, 128)),
                   _sds((NSEQ, NGROUP, 256, 128))],
        scratch_shapes=[pltpu.VMEM((NSEQ, 256, 128), F32), pltpu.VMEM((NSEQ, 256, 128), F32)],
        name="gla_fwd", compiler_params=_cp(("arbitrary",)))(h3, h3, h3, la3, h3, h3, h3, la3)
    return of.reshape(N, 256), ob.reshape(N, 256), sf, sb


def _gla_bwd(h, la2, do, sf, sb):
    c = GLA_GROUP * GLA_CHUNK

    def body(qf, kf, vf, laf, dof, sfr, qb, kb, vb, lab, dob, sbr,
             dqf, dkf, dvf, dlf, dqb, dkb, dvb, dlb, dstf, dstb):
        @pl.when(pl.program_id(0) == 0)
        def _():
            dstf[...] = jnp.zeros_like(dstf)
            dstb[...] = jnp.zeros_like(dstb)

        def one(s, q, k, v, la, do_, st, dst, rev):
            _, vjp = jax.vjp(functools.partial(_gla_chunk, rev=rev), q[s], k[s], v[s], la[s], st[s, 0])
            return vjp((do_[s], dst[s]))

        res = [(one(s, qf, kf, vf, laf, dof, sfr, dstf, False), one(s, qb, kb, vb, lab, dob, sbr, dstb, True))
               for s in range(NSEQ)]
        for s in range(NSEQ):
            for (gq, gk, gv, gl, gs), (dq, dk, dv, dl, dst) in ((res[s][0], (dqf, dkf, dvf, dlf, dstf)),
                                                                  (res[s][1], (dqb, dkb, dvb, dlb, dstb))):
                dq[s], dk[s], dv[s] = gq.astype(MX), gk.astype(MX), gv.astype(MX)
                dl[s], dst[s] = gl, gs

    def specs(rev):
        ch = lambda i: _gla_chunk_of(i, not rev)
        return [pl.BlockSpec((NSEQ, c, 128), lambda i: (0, ch(i), 2)), pl.BlockSpec((NSEQ, c, 128), lambda i: (0, ch(i), 3)),
                pl.BlockSpec((NSEQ, c, 256), lambda i: (0, ch(i), 2)),
                pl.BlockSpec((NSEQ, c, 128), lambda i: (0, ch(i), 1 if rev else 0)),
                pl.BlockSpec((NSEQ, c, 256), lambda i: (0, ch(i), 0)),
                pl.BlockSpec((NSEQ, 1, 256, 128), lambda i: (0, ch(i), 0, 0))]

    def ospecs(rev):
        ch = lambda i: _gla_chunk_of(i, not rev)
        n = pl.BlockSpec((NSEQ, c, 128), lambda i: (0, ch(i), 0))
        return [n, n, pl.BlockSpec((NSEQ, c, 256), lambda i: (0, ch(i), 0)), n]

    oshape = [_sds((NSEQ, L, 128), MX), _sds((NSEQ, L, 128), MX), _sds((NSEQ, L, 256), MX), _sds((NSEQ, L, 128))]
    h3, la3, do3 = h.reshape(NSEQ, L, DINP), la2.reshape(NSEQ, L, 256), do.reshape(NSEQ, L, 256)
    res = pl.pallas_call(
        body, grid=(NGROUP,),
        in_specs=specs(False) + specs(True),
        out_specs=ospecs(False) + ospecs(True),
        out_shape=oshape + oshape,
        scratch_shapes=[pltpu.VMEM((NSEQ, 256, 128), F32), pltpu.VMEM((NSEQ, 256, 128), F32)],
        name="gla_bwd", compiler_params=_cp(("arbitrary",)))(h3, h3, h3, la3, do3, sf, h3, h3, h3, la3, do3, sb)
    return [r.reshape(N, r.shape[-1]) for r in res]


def _gla_post(of, ob, r, g):
    o = of + ob
    head = lax.broadcasted_iota(jnp.int32, (1, 256), 1) // 64
    mu = jnp.zeros_like(o)
    for hd in range(4):
        mu = mu + jnp.where(head == hd, jnp.sum(jnp.where(head == hd, o, 0.0), axis=-1, keepdims=True) * (1.0 / 64.0), 0.0)
    xc = o - mu
    var = jnp.zeros_like(o)
    for hd in range(4):
        var = var + jnp.where(head == hd, jnp.sum(jnp.where(head == hd, xc * xc, 0.0), axis=-1, keepdims=True) * (1.0 / 64.0), 0.0)
    return xc * lax.rsqrt(var + LN_EPS) * g * (r * jax.nn.sigmoid(r))


def _gla_post_fwd(of, ob, h, g):
    tm = 512

    def body(of_ref, ob_ref, r_ref, g_ref, y_ref):
        y_ref[...] = _gla_post(of_ref[...], ob_ref[...], r_ref[...], g_ref[...]).astype(MX)

    row = pl.BlockSpec((tm, 256), lambda i: (i, 0))
    return pl.pallas_call(
        body, grid=(N // tm,),
        in_specs=[row, row, pl.BlockSpec((tm, 256), lambda i: (i, 3)), pl.BlockSpec((1, 256), lambda i: (0, 0))],
        out_specs=row, out_shape=_sds((N, 256), MX), name="gla_post_fwd", compiler_params=_cp(("parallel",)))(of, ob, h, g)


def _gla_post_bwd(of, ob, h, g, dyb):
    tm = 512

    def body(of_ref, ob_ref, r_ref, g_ref, dy_ref, do_ref, dr_ref, dg_ref):
        @pl.when(pl.program_id(0) == 0)
        def _():
            dg_ref[...] = jnp.zeros_like(dg_ref)

        _, vjp = jax.vjp(_gla_post, of_ref[...], ob_ref[...], r_ref[...], g_ref[...])
        go, _, gr, gg = vjp(dy_ref[...])
        do_ref[...] = go
        dr_ref[...] = gr.astype(MX)
        dg_ref[...] += gg

    row = pl.BlockSpec((tm, 256), lambda i: (i, 0))
    one = pl.BlockSpec((1, 256), lambda i: (0, 0))
    return pl.pallas_call(
        body, grid=(N // tm,),
        in_specs=[row, row, pl.BlockSpec((tm, 256), lambda i: (i, 3)), one, row],
        out_specs=[row, row, one], out_shape=[_sds((N, 256)), _sds((N, 256), MX), _sds((1, 256))],
        name="gla_post_bwd", compiler_params=_cp(("arbitrary",)))(of, ob, h, g, dyb)


def _rope_tables(width):
    pos = jnp.arange(L, dtype=F32)
    inv_freq = ROPE_THETA ** (-jnp.arange(0, ROT, 2, dtype=F32) / ROT)
    ang = pos[:, None] * inv_freq[None, :]
    cos, sin = jnp.cos(ang), jnp.sin(ang)
    one = jnp.ones((L, 64 - ROT), F32)
    zero = jnp.zeros((L, 64 - ROT), F32)
    z8 = jnp.zeros((L, ROT // 2), F32)
    c = jnp.concatenate([cos, cos, one], axis=1)
    sa = jnp.concatenate([z8, sin, zero], axis=1)
    sb = jnp.concatenate([-sin, z8, zero], axis=1)
    rep = width // 64
    return jnp.stack([jnp.tile(c, (1, rep)), jnp.tile(sa, (1, rep)), jnp.tile(sb, (1, rep))])


def _pieces(t, f):
    out = [f(t[:, c * 128:(c + 1) * 128]) for c in range(t.shape[-1] // 128)]
    return out[0] if len(out) == 1 else jnp.concatenate(out, axis=1)


def _rope(t, tab):
    return _pieces(t, lambda x: x * tab[0] + pltpu.roll(x, ROT // 2, 1) * tab[1] + pltpu.roll(x, 128 - ROT // 2, 1) * tab[2])


def _rope_t(g, tab):
    return _pieces(g, lambda x: x * tab[0] + pltpu.roll(x * tab[1], 128 - ROT // 2, 1) + pltpu.roll(x * tab[2], ROT // 2, 1))


def _swa_pad_kv(kv_ref, tk_ref, kexp, vexp):
    z = jnp.zeros((SWA_BLK, 256), F32)
    kr = _rope(kv_ref[:, 0:128], tk_ref[...])
    for hk in range(2):
        for pad in (kexp, vexp):
            pad[hk, 0:SWA_BLK] = z
            pad[hk, SWA_BLK + L:] = z
        kexp[hk, SWA_BLK:SWA_BLK + L] = _swa_expand(kr, hk)
        vexp[hk, SWA_BLK:SWA_BLK + L] = _swa_expand(kv_ref[:, 128:256], hk)


def _swa_expand(x, hk):
    lane = lax.broadcasted_iota(jnp.int32, x.shape, 1)
    sw = pltpu.roll(x, 64, 1)
    pair = jnp.where(lane < 64, x, sw) if hk == 0 else jnp.where(lane < 64, sw, x)
    return jnp.concatenate([pair, pair], axis=1)


def _swa_fold(x, hk):
    a = x[:, 0:128] + x[:, 128:256]
    t = a + pltpu.roll(a, 64, 1)
    lane = lax.broadcasted_iota(jnp.int32, a.shape, 1)
    return jnp.where((lane < 64) if hk == 0 else (lane >= 64), t, 0.0)


def _swa_probs(q2, kexp, n, sink_ref, hk):
    slot = lax.broadcasted_iota(jnp.int32, (1, 256), 1) // 64
    qs = jnp.concatenate([jnp.where(slot == g, q2, 0.0) for g in range(4)], axis=0)
    s = _mm_nt(qs, kexp) * 0.125
    i = lax.broadcasted_iota(jnp.int32, (SWA_BLK, 3 * SWA_BLK), 0)
    j = lax.broadcasted_iota(jnp.int32, (SWA_BLK, 3 * SWA_BLK), 1)
    kpos = n * SWA_BLK - SWA_BLK + j
    ok = (j - i >= 0) & (j - i <= 2 * SWA_BLK) & (kpos >= 0) & (kpos < L)
    s = jnp.where(jnp.concatenate([ok] * 4, axis=0), s, NEG_BIG)
    rowg = lax.broadcasted_iota(jnp.int32, (4 * SWA_BLK, 1), 0) // SWA_BLK
    sink = jnp.zeros((4 * SWA_BLK, 1), F32)
    for g in range(4):
        sink = jnp.where(rowg == g, sink_ref[hk * 4 + g], sink)
    m = jnp.maximum(jnp.max(s, axis=-1, keepdims=True), sink)
    p = jnp.exp(s - m)
    ps = jnp.exp(sink - m)
    inv = 1.0 / (jnp.sum(p, axis=-1, keepdims=True) + ps)
    return qs, p * inv, ps * inv, slot, rowg


def _swa_qtab(tk_ref, r0):
    return [tk_ref[i, pl.ds(r0, SWA_BLK), :] for i in range(3)]


def _swa_fwd(h, tk, sink):
    def body(sink_ref, q_ref, kv_ref, tk_ref, y_ref, kexp, vexp):
        n = pl.program_id(1)

        @pl.when(n == 0)
        def _():
            _swa_pad_kv(kv_ref, tk_ref, kexp, vexp)

        r0 = pl.multiple_of(n * SWA_BLK, SWA_BLK)
        q = _rope(q_ref[...], _swa_qtab(tk_ref, r0))
        for hk in range(2):
            _, p, _, slot, _ = _swa_probs(q[:, hk * 256:(hk + 1) * 256], kexp[hk, pl.ds(r0, 3 * SWA_BLK), :], n,
                                          sink_ref, hk)
            o4 = _mm(p, vexp[hk, pl.ds(r0, 3 * SWA_BLK), :])
            o = jnp.zeros((SWA_BLK, 256), F32)
            for g in range(4):
                o = o + jnp.where(slot == g, o4[g * SWA_BLK:(g + 1) * SWA_BLK], 0.0)
            y_ref[:, hk * 256:(hk + 1) * 256] = o.astype(MX)

    return pl.pallas_call(
        body,
        grid_spec=pltpu.PrefetchScalarGridSpec(
            num_scalar_prefetch=1, grid=(NSEQ, NBLK),
            in_specs=[pl.BlockSpec((SWA_BLK, 512), lambda s, n, sk: (s * NBLK + n, 2)),
                      pl.BlockSpec((L, 256), lambda s, n, sk: (s, 6)),
                      pl.BlockSpec((3, L, 128), lambda s, n, sk: (0, 0, 0))],
            out_specs=pl.BlockSpec((SWA_BLK, 512), lambda s, n, sk: (s * NBLK + n, 0)),
            scratch_shapes=[pltpu.VMEM((2, L + 2 * SWA_BLK, 256), F32), pltpu.VMEM((2, L + 2 * SWA_BLK, 256), F32)]),
        out_shape=_sds((N, 512), MX), name="swa_fwd", compiler_params=_cp(("arbitrary", "arbitrary")))(sink, h, h, tk)


def _swa_bwd(h, tk, sink, dyc):
    def body(sink_ref, q_ref, kv_ref, tk_ref, dy_ref, dq_ref, dkv_ref, dsink_ref, kexp_all, vexp_all, dkacc, dvacc):
        sq = pl.program_id(0)
        n = pl.program_id(1)

        @pl.when(n == 0)
        def _():
            _swa_pad_kv(kv_ref, tk_ref, kexp_all, vexp_all)
            dkacc[...] = jnp.zeros_like(dkacc)
            dvacc[...] = jnp.zeros_like(dvacc)

        @pl.when((n == 0) & (sq == 0))
        def _():
            dsink_ref[...] = jnp.zeros_like(dsink_ref)

        r0 = pl.multiple_of(n * SWA_BLK, SWA_BLK)
        tq = _swa_qtab(tk_ref, r0)
        q = _rope(q_ref[...], tq)
        hrow = lax.broadcasted_iota(jnp.int32, (8, 128), 0)
        dsk = jnp.zeros((8, 128), F32)
        for hk in range(2):
            kexp = kexp_all[hk, pl.ds(r0, 3 * SWA_BLK), :]
            vexp = vexp_all[hk, pl.ds(r0, 3 * SWA_BLK), :]
            qs, p, ps, slot, rowg = _swa_probs(q[:, hk * 256:(hk + 1) * 256], kexp, n, sink_ref, hk)
            dy2 = dy_ref[:, hk * 256:(hk + 1) * 256]
            dos = jnp.concatenate([jnp.where(slot == g, dy2, 0.0) for g in range(4)], axis=0)
            dp = _mm_nt(dos, vexp)
            delta = jnp.sum(p * dp, axis=-1, keepdims=True)
            ds = p * (dp - delta) * 0.125
            dsr = -ps * delta
            for g in range(4):
                dsk = dsk + jnp.where(hrow == hk * 4 + g, jnp.sum(jnp.where(rowg == g, dsr, 0.0), axis=0, keepdims=True), 0.0)
            dq4 = _mm(ds, kexp)
            dq2 = jnp.zeros((SWA_BLK, 256), F32)
            for g in range(4):
                dq2 = dq2 + jnp.where(slot == g, dq4[g * SWA_BLK:(g + 1) * SWA_BLK], 0.0)
            dq_ref[:, hk * 256:(hk + 1) * 256] = _rope_t(dq2, tq).astype(MX)
            dkacc[hk, pl.ds(r0, 3 * SWA_BLK), :] += _mm_tn(ds, qs)
            dvacc[hk, pl.ds(r0, 3 * SWA_BLK), :] += _mm_tn(p, dos)
        dsink_ref[...] += dsk

        @pl.when(n == NBLK - 1)
        def _():
            seq = slice(SWA_BLK, SWA_BLK + L)
            dk = _rope_t(_swa_fold(dkacc[0, seq], 0) + _swa_fold(dkacc[1, seq], 1), tk_ref[...])
            dkv_ref[:, 0:128] = dk.astype(MX)
            dkv_ref[:, 128:256] = (_swa_fold(dvacc[0, seq], 0) + _swa_fold(dvacc[1, seq], 1)).astype(MX)

    blk = lambda col: pl.BlockSpec((SWA_BLK, 512), lambda s, n, sk: (s * NBLK + n, col))
    pad = pltpu.VMEM((2, L + 2 * SWA_BLK, 256), F32)
    return pl.pallas_call(
        body,
        grid_spec=pltpu.PrefetchScalarGridSpec(
            num_scalar_prefetch=1, grid=(NSEQ, NBLK),
            in_specs=[blk(2), pl.BlockSpec((L, 256), lambda s, n, sk: (s, 6)),
                      pl.BlockSpec((3, L, 128), lambda s, n, sk: (0, 0, 0)), blk(0)],
            out_specs=[blk(0), pl.BlockSpec((L, 256), lambda s, n, sk: (s, 0)),
                       pl.BlockSpec((8, 128), lambda s, n, sk: (0, 0))],
            scratch_shapes=[pad, pad, pad, pad]),
        out_shape=[_sds((N, 512), MX), _sds((N, 256), MX), _sds((8, 128))],
        name="swa_bwd", compiler_params=_cp(("arbitrary", "arbitrary")))(sink, h, h, tk, dyc)


def _outproj_fwd(ya, yb, yc, x, wo, g, b):
    tm = 512

    def body(ya_ref, yb_ref, yc_ref, x_ref, wo_ref, g_ref, b_ref, s_ref, x1_ref):
        mix = _mm(ya_ref[...], wo_ref[0:256]) + _mm(yb_ref[...], wo_ref[256:512]) + _mm(yc_ref[...], wo_ref[512:1024])
        s = ALPHA * x_ref[...] + mix
        s_ref[...] = s
        x1_ref[...] = _ln_fwd(s, g_ref[...], b_ref[...])

    row = lambda w_: pl.BlockSpec((tm, w_), lambda i: (i, 0))
    one = pl.BlockSpec((1, D), lambda i: (0, 0))
    return pl.pallas_call(
        body, grid=(N // tm,),
        in_specs=[row(256), row(256), row(512), row(D), pl.BlockSpec((D, D), lambda i: (0, 0)), one, one],
        out_specs=[row(D), row(D)], out_shape=[_sds((N, D)), _sds((N, D))],
        name="outproj_fwd", compiler_params=_cp(("parallel",)))(ya, yb, yc, x, wo, g, b)


def _outproj_bwd(dx1, s1, ya, yb, yc, wo, g):
    tm = 512
    nt = N // tm

    def body(dx1_ref, s_ref, ya_ref, yb_ref, yc_ref, wo_ref, g_ref,
             dya_ref, dyb_ref, dyc_ref, dxp_ref, dwo_ref, dg_ref, db_ref, acc):
        i = pl.program_id(0)

        @pl.when(i == 0)
        def _():
            acc[...] = jnp.zeros_like(acc)
            dg_ref[...] = jnp.zeros_like(dg_ref)
            db_ref[...] = jnp.zeros_like(db_ref)

        ds, dg, db = _ln_bwd(dx1_ref[...], s_ref[...], g_ref[...])
        dg_ref[...] += dg
        db_ref[...] += db
        dxp_ref[...] = ALPHA * ds
        dy = _mm_nt(ds, wo_ref[...])
        dya_ref[...] = dy[:, 0:256]
        dyb_ref[...] = dy[:, 256:512]
        dyc_ref[...] = dy[:, 512:1024]
        acc[0:256] += _mm_tn(ya_ref[...], ds)
        acc[256:512] += _mm_tn(yb_ref[...], ds)
        acc[512:1024] += _mm_tn(yc_ref[...], ds)

        @pl.when(i == nt - 1)
        def _():
            dwo_ref[...] = acc[...].astype(MX)

    row = lambda w_: pl.BlockSpec((tm, w_), lambda i: (i, 0))
    one = pl.BlockSpec((1, D), lambda i: (0, 0))
    full = pl.BlockSpec((D, D), lambda i: (0, 0))
    return pl.pallas_call(
        body, grid=(nt,),
        in_specs=[row(D), row(D), row(256), row(256), row(512), full, one],
        out_specs=[row(256), row(256), row(512), row(D), full, one, one],
        out_shape=[_sds((N, 256)), _sds((N, 256)), _sds((N, 512)), _sds((N, D)), _sds((D, D), MX), _sds((1, D)), _sds((1, D))],
        scratch_shapes=[pltpu.VMEM((D, D), F32)],
        name="outproj_bwd", compiler_params=_cp(("arbitrary",)))(dx1, s1, ya, yb, yc, wo, g)


def _ffn_fwd(x1, w1, w2, g, b):
    tm = FFN_TM

    def body(x_ref, w1_ref, w2_ref, g_ref, b_ref, a_ref, s_ref, x2_ref):
        x = x_ref[...]
        xb = x.astype(MX)
        s = ALPHA * x
        for j in range(NSHARD):
            a = _mm(xb, w1_ref[j])
            a_ref[:, j * D:(j + 1) * D] = a.astype(MX)
            s = s + _mm(jnp.square(jnp.maximum(a, 0.0)), w2_ref[j])
        s_ref[...] = s
        x2_ref[...] = _ln_fwd(s, g_ref[...], b_ref[...])

    row = pl.BlockSpec((tm, D), lambda i: (i, 0))
    wall = pl.BlockSpec((NSHARD, D, D), lambda i: (0, 0, 0))
    one = pl.BlockSpec((1, D), lambda i: (0, 0))
    return pl.pallas_call(
        body, grid=(N // tm,),
        in_specs=[row, wall, wall, one, one],
        out_specs=[pl.BlockSpec((tm, DFF), lambda i: (i, 0)), row, row],
        out_shape=[_sds((N, DFF), MX), _sds((N, D)), _sds((N, D))],
        name="ffn_fwd", compiler_params=_cp(("parallel",), FFN_VMEM))(x1, w1, w2, g, b)


def _ffn_bwd_act(dy, s2, a, w1, w2, g):
    tm = FFN_TM

    def body(dy_ref, s_ref, a_ref, w1_ref, w2_ref, g_ref, da_ref, ds_ref, dx1_ref, dg_ref, db_ref):
        @pl.when(pl.program_id(0) == 0)
        def _():
            dg_ref[...] = jnp.zeros_like(dg_ref)
            db_ref[...] = jnp.zeros_like(db_ref)

        ds, dg, db = _ln_bwd(dy_ref[...], s_ref[...], g_ref[...])
        dsb = ds.astype(MX)
        ds_ref[...] = dsb
        dg_ref[...] += dg
        db_ref[...] += db
        dx1 = ALPHA * ds
        for j in range(NSHARD):
            da = (_mm_nt(dsb, w2_ref[j]) * 2.0 * jnp.maximum(a_ref[:, j * D:(j + 1) * D].astype(F32), 0.0)).astype(MX)
            da_ref[:, j * D:(j + 1) * D] = da
            dx1 = dx1 + _mm_nt(da, w1_ref[j])
        dx1_ref[...] = dx1

    row = pl.BlockSpec((tm, D), lambda i: (i, 0))
    wide = pl.BlockSpec((tm, DFF), lambda i: (i, 0))
    wall = pl.BlockSpec((NSHARD, D, D), lambda i: (0, 0, 0))
    one = pl.BlockSpec((1, D), lambda i: (0, 0))
    return pl.pallas_call(
        body, grid=(N // tm,),
        in_specs=[row, row, wide, wall, wall, one],
        out_specs=[wide, row, row, one, one],
        out_shape=[_sds((N, DFF), MX), _sds((N, D), MX), _sds((N, D)), _sds((1, D)), _sds((1, D))],
        name="ffn_bwd_act", compiler_params=_cp(("arbitrary",), FFN_VMEM))(dy, s2, a, w1, w2, g)


def _ffn_bwd_w(x1, da, a, ds):
    tm, nb = FFN_TM_W, FFN_WB
    nt = N // tm

    def body(x_ref, da_ref, a_ref, ds_ref, dw1_ref, dw2_ref, acc1, acc2):
        i = pl.program_id(1)

        @pl.when(i == 0)
        def _():
            acc1[...] = jnp.zeros_like(acc1)
            acc2[...] = jnp.zeros_like(acc2)

        x, ds_ = x_ref[...], ds_ref[...]
        for k in range(nb):
            cols = slice(k * D, (k + 1) * D)
            acc1[k] += _mm_tn(x, da_ref[:, cols])
            acc2[k] += _mm_tn(jnp.square(jnp.maximum(a_ref[:, cols].astype(F32), 0.0)), ds_)

        @pl.when(i == nt - 1)
        def _():
            dw1_ref[...] = acc1[...].astype(MX)
            dw2_ref[...] = acc2[...].astype(MX)

    row = pl.BlockSpec((tm, D), lambda j, i: (i, 0))
    col = pl.BlockSpec((tm, nb * D), lambda j, i: (i, j))
    wj = pl.BlockSpec((nb, D, D), lambda j, i: (j, 0, 0))
    return pl.pallas_call(
        body, grid=(NSHARD // nb, nt),
        in_specs=[row, col, col, row], out_specs=[wj, wj],
        out_shape=[_sds((NSHARD, D, D), MX), _sds((NSHARD, D, D), MX)],
        scratch_shapes=[pltpu.VMEM((nb, D, D), F32), pltpu.VMEM((nb, D, D), F32)],
        name="ffn_bwd_w", compiler_params=_cp(("parallel", "arbitrary"), FFN_VMEM))(x1, da, a, ds)


def _loss_head(y, target):
    tm = 512

    def body(y_ref, t_ref, dy_ref, l_ref):
        @pl.when(pl.program_id(0) == 0)
        def _():
            l_ref[...] = jnp.zeros_like(l_ref)

        e = y_ref[...] - t_ref[...]
        dy_ref[...] = e * (1.0 / D)
        l_ref[...] += jnp.sum(jnp.sum(e * e, axis=1, keepdims=True), axis=0, keepdims=True) * (0.5 / D)

    row = pl.BlockSpec((tm, D), lambda i: (i, 0))
    return pl.pallas_call(
        body, grid=(N // tm,), in_specs=[row, row],
        out_specs=[row, pl.BlockSpec((8, 128), lambda i: (0, 0))],
        out_shape=[_sds((N, D)), _sds((8, 128))], name="loss_head", compiler_params=_cp(("arbitrary",)))(y, target)


def _s5_discretize(a_re, a_im, log_step, b_re, b_im):
    lam = lax.complex(a_re, a_im)
    lam_bar = jnp.exp(lam * jnp.exp(log_step))
    b_bar = ((lam_bar - 1.0) / lam)[..., None] * lax.complex(b_re, b_im)
    return jnp.real(lam_bar), jnp.imag(lam_bar), jnp.real(b_bar), jnp.imag(b_bar)


def _s5_in_blocks(b):
    e = jnp.eye(8, dtype=F32)
    return jnp.einsum('ij,zbjph->zbihjp', e, b.reshape(2, 2, 8, S5_P, S5_H)).reshape(2, 2, 128, SW)


def _s5_in_unblocks(d):
    return jnp.einsum('zbihip->zbiph', d.reshape(2, 2, 8, S5_H, 8, S5_P)).reshape(2, S5_G, S5_P, S5_H)


def _s5_out_blocks(c):
    e = jnp.eye(8, dtype=F32)
    return jnp.einsum('ij,zbjhp->zbjpih', e, c.reshape(2, 2, 8, S5_H, S5_P)).reshape(2, 2, SW, 128)


def _s5_out_unblocks(d):
    return jnp.einsum('zbipih->zbihp', d.reshape(2, 2, 8, S5_P, 8, S5_H)).reshape(2, S5_G, S5_H, S5_P)


def _gate_weight(w_a):
    z = jnp.zeros((16, 128), F32)
    top = jnp.concatenate([w_a[0], z], axis=1)
    bot = jnp.concatenate([z, w_a[1]], axis=1)
    return jnp.concatenate([top, bot, jnp.zeros((96, 256), F32)], axis=0)


def _layer_prep(p):
    lr, li, br, bi = _s5_discretize(p["s5_a_re"], p["s5_a_im"], p["s5_log_step"], p["s5_b_re"], p["s5_b_im"])
    q = dict(p)
    q["bre"] = _s5_in_blocks(br).astype(MX)
    q["bim"] = _s5_in_blocks(bi).astype(MX)
    q["cre"] = _s5_out_blocks(p["s5_c_re"]).astype(MX)
    q["cim"] = _s5_out_blocks(p["s5_c_im"]).astype(MX)
    mr, mi = lr.reshape(2, 1024), li.reshape(2, 1024)
    both = lambda t0, t1: tuple(jnp.stack(p) for p in zip(t0, t1))
    q["tab"] = both(_lockstep_tables(mr[0], mi[0], False), _lockstep_tables(mr[1], mi[1], True))
    q["tabc"] = both(_lockstep_tables(mr[0], -mi[0], True), _lockstep_tables(mr[1], -mi[1], False))
    q["dsk"] = p["s5_d"].reshape(1, 256)
    q["wa"] = _gate_weight(p["gla_w_a"]).astype(MX)
    q["ba"] = p["gla_b_a"].reshape(1, 256)
    q["lng"] = p["gla_ln_g"].reshape(1, 256)
    q["bv"] = p["s5_b_glu"][:256].reshape(1, 256)
    q["bg"] = p["s5_b_glu"][256:].reshape(1, 256)
    for k in ("ln1_g", "ln1_b", "ln2_g", "ln2_b"):
        q[k] = p[k].reshape(1, D)
    return q


def _layer_fwd(x, q, tk, fetch):
    q["w_in"] = fetch("w_in", x)
    h = _inproj_fwd(x, q["w_in"])
    hre, him, y2 = _s5_fwd(h, q["bre"], q["bim"], q["cre"], q["cim"], q["tab"])
    q["w4"] = fetch("s5_w_glu", y2)
    ya = _s5_glu_fwd(y2, h, q["dsk"], q["w4"], q["bv"], q["bg"])
    la2 = _gla_gate_fwd(h, q["wa"], q["ba"])
    of, ob, sf, sb = _gla_fwd(h, la2)
    yb = _gla_post_fwd(of, ob, h, q["lng"])
    yc = _swa_fwd(h, tk, q["swa_sink"])
    q["w_out"] = fetch("w_out", yc)
    s1, x1 = _outproj_fwd(ya, yb, yc, x, q["w_out"], q["ln1_g"], q["ln1_b"])
    q["w_ff1"] = fetch("w_ff1", x1)
    q["w_ff2"] = fetch("w_ff2", x1)
    a, s2, x2 = _ffn_fwd(x1, q["w_ff1"], q["w_ff2"], q["ln2_g"], q["ln2_b"])
    saved = dict(x=x, h=h, hre=hre, him=him, y2=y2, ya=ya, la2=la2, of=of, ob=ob, sf=sf, sb=sb, yb=yb, yc=yc,
                 s1=s1, x1=x1, a=a, s2=s2)
    return x2, saved


def _layer_bwd(dy, q, sv, tk, emit):
    g = {}
    da, ds2, dx1, g["dg2"], g["db2"] = _ffn_bwd_act(dy, sv["s2"], sv["a"], q["w_ff1"], q["w_ff2"], q["ln2_g"])
    dw1, dw2 = _ffn_bwd_w(sv["x1"], da, sv["a"], ds2)
    tie = emit(dict(w_ff1=dw1, w_ff2=dw2))
    dya, dyb, dyc, dxp, dwo, g["dg1"], g["db1"] = _outproj_bwd(dx1, sv["s1"], sv["ya"], sv["yb"], sv["yc"],
                                                               q["w_out"], q["ln1_g"] + tie)
    h = sv["h"]
    daq, dakv, g["dsink"] = _swa_bwd(h, tk, q["swa_sink"], dyc)
    do, gr, g["dlng"] = _gla_post_bwd(sv["of"], sv["ob"], h, q["lng"], dyb)
    gq_f, gk_f, gv_f, gl_f, gq_b, gk_b, gv_b, gl_b = _gla_bwd(h, sv["la2"], do, sv["sf"], sv["sb"])
    dhl, g["dwa"], g["dba"] = _gla_gate_bwd(h, q["wa"], q["ba"], gl_f, gl_b)
    dyp, dud, g["dd"], dw4, g["dbv"], g["dbg"] = _s5_glu_bwd(sv["y2"], h, q["dsk"], q["w4"], q["bv"], q["bg"], dya)
    tie = emit(dict(w_out=dwo.reshape(NSHARD, D // NSHARD, D), s5_w_glu=dw4))
    du2, g["dbre"], g["dbim"], g["dcre"], g["dcim"], g["dmu"] = _s5_bwd(
        h, dyp, sv["hre"], sv["him"], q["bre"], q["bim"], q["cre"], q["cim"], (q["tabc"][0], q["tabc"][1] + tie))
    dx, dwt = _inproj_bwd(sv["x"], q["w_in"], dxp, du2, dud, gq_f, gq_b, gk_f, gk_b, gv_f, gv_b, gr, daq, dakv, dhl)
    tie = emit(dict(w_in=dwt))
    return dx, g, tie


NATIVE = ("dmu", "dbre", "dbim", "dcre", "dcim", "dd", "dbv", "dbg", "dwa", "dba", "dlng", "dsink",
          "dg1", "db1", "dg2", "db2", "loss")
ICI_CORE = (0, 0, 0, 1, 1, 0, 0, 0, 1, 1, 1, 1, 0, 0, 1, 1, 0)


def _finish_small(n, w):
    g = {}
    dmu = n["dmu"]
    dlr = dmu[:, :, :, 0].reshape(DEPTH, 2, S5_G, S5_P)
    dli = dmu[:, :, :, 1].reshape(DEPTH, 2, S5_G, S5_P)

    def unblock(c, perm, shape):
        return c.reshape(DEPTH, 2, 2, S5_H, 8, S5_P).transpose(perm).reshape(shape)

    b_shape, c_shape = (DEPTH, 2, S5_G, S5_P, S5_H), (DEPTH, 2, S5_G, S5_H, S5_P)
    _, vjp = jax.vjp(_s5_discretize, w["s5_a_re"], w["s5_a_im"], w["s5_log_step"], w["s5_b_re"], w["s5_b_im"])
    (g["s5_a_re"], g["s5_a_im"], g["s5_log_step"], g["s5_b_re"], g["s5_b_im"]) = vjp(
        (dlr, dli, unblock(n["dbre"], (0, 1, 2, 4, 5, 3), b_shape), unblock(n["dbim"], (0, 1, 2, 4, 5, 3), b_shape)))
    g["s5_c_re"] = unblock(n["dcre"], (0, 1, 2, 4, 3, 5), c_shape)
    g["s5_c_im"] = unblock(n["dcim"], (0, 1, 2, 4, 3, 5), c_shape)
    g["s5_d"] = n["dd"].reshape(DEPTH, S5_G, S5_H)
    g["s5_b_glu"] = jnp.concatenate([n["dbv"], n["dbg"]], axis=2).reshape(DEPTH, 512)
    g["gla_w_a"] = jnp.stack([n["dwa"][:, 0:16, 0:128], n["dwa"][:, 16:32, 128:256]], axis=1)
    g["gla_b_a"] = n["dba"].reshape(DEPTH, 2, 128)
    g["gla_ln_g"] = n["dlng"].reshape(DEPTH, 256)
    g["swa_sink"] = n["dsink"][:, :, 0]
    for k, s in (("ln1_g", "dg1"), ("ln1_b", "db1"), ("ln2_g", "dg2"), ("ln2_b", "db2")):
        g[k] = n[s].reshape(DEPTH, D)
    return g


def _local_step(x, target, qs, tk, fetch, emit):
    saved = []
    for l, q in enumerate(qs):
        x, sv = _layer_fwd(x, q, tk, functools.partial(fetch, l))
        saved.append(sv)
    dy, lacc = _loss_head(x, target)
    smalls = [None] * DEPTH
    tie = 0.0
    for l in reversed(range(DEPTH)):
        qs[l]["ln2_g"] = qs[l]["ln2_g"] + tie
        dy, smalls[l], tie = _layer_bwd(dy, qs[l], saved[l], tk, functools.partial(emit, l))
    smalls[0]["db2"] = smalls[0]["db2"] + tie
    for l in range(DEPTH):
        smalls[l]["loss"] = lacc if l == 0 else jnp.zeros_like(lacc)
    return lacc[0, 0], dy, smalls


BIG = ("w_in", "s5_w_glu", "w_out", "w_ff1", "w_ff2")
SMALL = ("s5_a_re", "s5_a_im", "s5_log_step", "s5_b_re", "s5_b_im", "s5_c_re", "s5_c_im", "s5_d", "s5_b_glu",
         "gla_w_a", "gla_b_a", "gla_ln_g", "swa_sink", "ln1_g", "ln1_b", "ln2_g", "ln2_b")
ANY = pl.BlockSpec(memory_space=pl.ANY)


def _place():
    x, y, c = lax.axis_index("x"), lax.axis_index("y"), lax.axis_index("c")
    return x, y, c, [(1 - x, y), (x, 1 - y), (1 - x, 1 - y)]


HBM = pl.BlockSpec(memory_space=pltpu.HBM)
SEMS = pl.BlockSpec(memory_space=pltpu.SEMAPHORE)
EFFECT = pltpu.SideEffectType.DATAFLOW_SIDE_EFFECTING


def _push_copies(ins, lands, send, recv, gather, sending):
    x, y, c, chips = _place()
    me = 2 * x + y
    out = []
    for a in range(len(lands)):
        for j, (px, py) in enumerate(chips):
            peer = 2 * px + py
            src = lands[a].at[me] if gather else ins[a].at[peer if sending else me]
            dst = lands[a].at[me if sending else peer]
            out.append(pltpu.make_async_remote_copy(src_ref=src, dst_ref=dst, send_sem=send.at[3 * a + j],
                                                    recv_sem=recv.at[3 * a + j], device_id=(px, py, c),
                                                    device_id_type=MESH))
    return out


def _push_start(name, arrs, gather):
    n = len(arrs)
    ops = list(arrs) if gather else list(arrs) + [lax.empty(s.shape, s.dtype) for s in arrs]
    m = len(ops)

    def body(*refs):
        ins, lnd = (refs[:n], refs[:n]) if gather else (refs[:n], refs[n:m])
        for cp in _push_copies(ins, lnd, refs[m], refs[m + 1], gather, True):
            cp.start()
        refs[-1][...] = jnp.zeros((8, 128), F32)

    ops = [pltpu.with_memory_space_constraint(t, pltpu.HBM) for t in ops]
    res = pl.pallas_call(
        body, name=name,
        out_shape=(pltpu.SemaphoreType.DMA((3 * n,)), pltpu.SemaphoreType.DMA((3 * n,)),
                   *[pltpu.HBM(t.shape, t.dtype) for t in ops], _sds((8, 128))),
        in_specs=[HBM] * m,
        out_specs=(SEMS, SEMS, *[HBM] * m, pl.BlockSpec(memory_space=pltpu.VMEM)),
        input_output_aliases={i: 2 + i for i in range(m)},
        compiler_params=pltpu.CompilerParams(has_side_effects=EFFECT))(*ops)
    return res[0], res[1], list(res[2:2 + m]), res[-1]


def _push_wait(name, started, after, gather):
    send, recv, ops, _ = started
    m = len(ops)
    n = m if gather else m // 2

    def body(*refs):
        ins, lnd = (refs[:n], refs[:n]) if gather else (refs[:n], refs[n:m])
        for cp in _push_copies(ins, lnd, refs[m], refs[m + 1], gather, False):
            cp.wait_send()
            cp.wait_recv()

    res = pl.pallas_call(
        body, name=name,
        out_shape=[pltpu.HBM(t.shape, t.dtype) for t in ops],
        in_specs=[HBM] * m + [SEMS, SEMS, ANY], out_specs=[HBM] * m,
        input_output_aliases={i: i for i in range(m)},
        compiler_params=pltpu.CompilerParams(has_side_effects=EFFECT))(*ops, send, recv, after)
    return list(res)


def _row_tile(rows):
    return max(t for t in range(8, min(rows, 512) + 1, 8) if rows % t == 0)


def _cast_to_slot(me, w, l):
    _, rows, cols = w.shape
    tr = _row_tile(rows)

    def body(me_ref, w_ref, o_ref):
        o_ref[0] = w_ref[0].astype(MX)

    return pl.pallas_call(
        body,
        grid_spec=pltpu.PrefetchScalarGridSpec(
            num_scalar_prefetch=1, grid=(rows // tr,),
            in_specs=[pl.BlockSpec((1, tr, cols), lambda i, me_: (l, i, 0))],
            out_specs=pl.BlockSpec((1, tr, cols), lambda i, me_: (me_[0], i, 0))),
        out_shape=_sds((NSHARD, rows, cols), MX), name="cast_to_slot", compiler_params=_cp(("arbitrary",)))(me, w)


def _sum_sources(me, recv, own):
    _, rows, cols = recv[0].shape
    tr = min(_row_tile(rows), 256) if rows % 256 == 0 else _row_tile(rows)
    nt = rows // tr

    def body(me_ref, *refs):
        o_ref = refs[-1]
        for l in range(DEPTH):
            @pl.when(pl.program_id(0) == l)
            def _():
                r_ref, own_ref = refs[2 * l], refs[2 * l + 1]
                part = [jnp.where(me_ref[0] == s, own_ref[0], r_ref[s]).astype(F32) for s in range(NSHARD)]
                o_ref[...] = ((part[0] + part[1]) + part[2]) + part[3]

    in_specs = []
    for l in range(DEPTH):
        pick = lambda g, i, me_, l=l: jnp.where(g == l, i, jnp.where(g < l, 0, nt - 1))
        in_specs += [pl.BlockSpec((NSHARD, tr, cols), lambda g, i, me_, pick=pick: (0, pick(g, i, me_), 0)),
                     pl.BlockSpec((1, tr, cols), lambda g, i, me_, pick=pick: (me_[0], pick(g, i, me_), 0))]
    return pl.pallas_call(
        body,
        grid_spec=pltpu.PrefetchScalarGridSpec(
            num_scalar_prefetch=1, grid=(DEPTH, nt), in_specs=in_specs,
            out_specs=pl.BlockSpec((tr, cols), lambda g, i, me_: (g * nt + i, 0))),
        out_shape=_sds((DEPTH * rows, cols)), name="sum_sources",
        compiler_params=_cp(("arbitrary", "arbitrary")))(me, *[t for l in range(DEPTH) for t in (recv[l], own[l])])


def _swap_sibling(arrs):
    n = len(arrs)

    def body(*refs):
        ins, outs = refs[:n], refs[n:2 * n]
        send, recv = refs[2 * n:]
        x, y, c, _ = _place()
        cps = [pltpu.make_async_remote_copy(src_ref=ins[a], dst_ref=outs[a], send_sem=send.at[a], recv_sem=recv.at[a],
                                            device_id=(x, y, 1 - c), device_id_type=MESH) for a in range(n)]
        for cp in cps:
            cp.start()
        for cp in cps:
            cp.wait()

    return pl.pallas_call(
        body, in_specs=[ANY] * n, out_specs=[ANY] * n, out_shape=[_sds(a.shape, a.dtype) for a in arrs],
        scratch_shapes=[pltpu.SemaphoreType.DMA((n,)), pltpu.SemaphoreType.DMA((n,))],
        name="swap_sibling")(*arrs)


def _allreduce_small(per_layer):
    nk = len(per_layer[0])
    n = DEPTH * nk
    shapes = [a.shape for a in per_layer[0]]

    def body(*refs):
        ins, outs = refs[:n], refs[n:n + nk]
        sibs, slots = refs[n + nk:n + 2 * nk], refs[n + 2 * nk:n + 3 * nk]
        send, recv = refs[n + 3 * nk:]
        x, y, c, chips = _place()
        me = 2 * x + y
        d2d = [pltpu.make_async_remote_copy(src_ref=ins[l * nk + k], dst_ref=sibs[k].at[l], send_sem=send.at[l * nk + k],
                                            recv_sem=recv.at[l * nk + k], device_id=(x, y, 1 - c), device_id_type=MESH)
               for l in range(DEPTH) for k in range(nk)]
        for cp in d2d:
            cp.start()
        for cp in d2d:
            cp.wait()
        for l in range(DEPTH):
            for k in range(nk):
                slots[k][me, l] = ins[l * nk + k][...] + sibs[k][l]

        def remote(k, j, slot):
            px, py = chips[j]
            return pltpu.make_async_remote_copy(src_ref=slots[k].at[me], dst_ref=slots[k].at[slot],
                                                send_sem=send.at[n + 3 * k + j], recv_sem=recv.at[n + 3 * k + j],
                                                device_id=(px, py, c), device_id_type=MESH)

        def handover(k):
            return pltpu.make_async_remote_copy(src_ref=outs[k], dst_ref=outs[k], send_sem=send.at[n + 3 * nk + k],
                                                recv_sem=recv.at[n + 3 * nk + k], device_id=(x, y, 1 - c),
                                                device_id_type=MESH)

        halves = (tuple(k for k in range(nk) if ICI_CORE[k] == 0), tuple(k for k in range(nk) if ICI_CORE[k] == 1))
        for cc in range(2):
            @pl.when(c == cc)
            def _():
                mine, theirs = halves[cc], halves[1 - cc]
                sends = [remote(k, j, me) for k in mine for j in range(3)]
                for cp in sends:
                    cp.start()
                for k in mine:
                    for j in range(3):
                        remote(k, j, 2 * chips[j][0] + chips[j][1]).wait_recv()
                for cp in sends:
                    cp.wait_send()
                for k in mine:
                    outs[k][...] = ((slots[k][0] + slots[k][1]) + slots[k][2]) + slots[k][3]
                over = [handover(k) for k in mine]
                for cp in over:
                    cp.start()
                for k in theirs:
                    handover(k).wait_recv()
                for cp in over:
                    cp.wait_send()

    vm = pl.BlockSpec(memory_space=pltpu.VMEM)
    return pl.pallas_call(
        body, in_specs=[vm] * n, out_specs=[vm] * nk, out_shape=[_sds((DEPTH,) + s) for s in shapes],
        scratch_shapes=([pltpu.VMEM((DEPTH,) + s, F32) for s in shapes]
                        + [pltpu.VMEM((NSHARD, DEPTH) + s, F32) for s in shapes]
                        + [pltpu.SemaphoreType.DMA((n + 4 * nk,)), pltpu.SemaphoreType.DMA((n + 4 * nk,))]),
        name="allreduce_small", compiler_params=pltpu.CompilerParams(vmem_limit_bytes=VMEM_LIMIT))(
            *[a for layer in per_layer for a in layer])


def _adamw_math(w, g, m, v):
    m = ADAM_B1 * m + (1.0 - ADAM_B1) * g
    v = ADAM_B2 * v + (1.0 - ADAM_B2) * jnp.square(g)
    m_hat = m / (1.0 - ADAM_B1 ** ADAM_STEP)
    v_hat = v / (1.0 - ADAM_B2 ** ADAM_STEP)
    delta = -ADAM_LR * (m_hat / (jnp.sqrt(v_hat) + ADAM_EPS) + ADAM_WD * w)
    return delta, m, v


def _adamw(g_parts, w, m, v):
    rows, cols = w.shape
    tr = 256 if rows % 256 == 0 else _row_tile(rows)
    k = len(g_parts)

    def body(*refs):
        g = refs[0][...]
        for r in refs[1:k]:
            g = g + r[...]
        w_ref, m_ref, v_ref, go, do, mo, vo = refs[k:]
        d, mn, vn = _adamw_math(w_ref[...], g, m_ref[...], v_ref[...])
        go[...] = g
        do[...] = d
        mo[...] = mn
        vo[...] = vn

    spec = pl.BlockSpec((tr, cols), lambda i: (i, 0))
    return pl.pallas_call(
        body, grid=(rows // tr,), in_specs=[spec] * (k + 3), out_specs=[spec] * 4,
        out_shape=[_sds((rows, cols))] * 4, name="adamw", compiler_params=_cp(("parallel",)))(*g_parts, w, m, v)


def _adamw_small(gs, ws, ms, vs):
    n = len(gs)

    def body(*refs):
        for k in range(n):
            d, mn, vn = _adamw_math(refs[n + k][...], refs[k][...], refs[2 * n + k][...], refs[3 * n + k][...])
            refs[4 * n + k][...] = d
            refs[5 * n + k][...] = mn
            refs[6 * n + k][...] = vn

    vm = pl.BlockSpec(memory_space=pltpu.VMEM)
    shapes = [_sds(a.shape) for a in ws]
    res = pl.pallas_call(
        body, in_specs=[vm] * (4 * n), out_specs=[vm] * (3 * n), out_shape=shapes * 3, name="adamw_small",
        compiler_params=pltpu.CompilerParams(vmem_limit_bytes=VMEM_LIMIT))(*gs, *ws, *ms, *vs)
    return res[:n], res[n:2 * n], res[2 * n:]


_ARGS = ("x", "w_in", "s5_a_re", "s5_a_im", "s5_log_step", "s5_b_re", "s5_b_im", "s5_c_re", "s5_c_im", "s5_d",
         "s5_w_glu", "s5_b_glu", "gla_w_a", "gla_b_a", "gla_ln_g", "swa_sink", "w_out", "ln1_g", "ln1_b", "w_ff1",
         "w_ff2", "ln2_g", "ln2_b")
_WEIGHTS = _ARGS[1:]


def _shard_cols(d):
    return d.reshape(d.shape[0], NSHARD, d.shape[1] // NSHARD).transpose(1, 0, 2)


def kernel(x, w_in, s5_a_re, s5_a_im, s5_log_step, s5_b_re, s5_b_im, s5_c_re, s5_c_im, s5_d, s5_w_glu, s5_b_glu, gla_w_a, gla_b_a, gla_ln_g, swa_sink, w_out, ln1_g, ln1_b, w_ff1, w_ff2, ln2_g, ln2_b, loss_target, m_w_in, m_s5_a_re, m_s5_a_im, m_s5_log_step, m_s5_b_re, m_s5_b_im, m_s5_c_re, m_s5_c_im, m_s5_d, m_s5_w_glu, m_s5_b_glu, m_gla_w_a, m_gla_b_a, m_gla_ln_g, m_swa_sink, m_w_out, m_ln1_g, m_ln1_b, m_w_ff1, m_w_ff2, m_ln2_g, m_ln2_b, v_w_in, v_s5_a_re, v_s5_a_im, v_s5_log_step, v_s5_b_re, v_s5_b_im, v_s5_c_re, v_s5_c_im, v_s5_d, v_s5_w_glu, v_s5_b_glu, v_gla_w_a, v_gla_b_a, v_gla_ln_g, v_swa_sink, v_w_out, v_ln1_g, v_ln1_b, v_w_ff1, v_w_ff2, v_ln2_g, v_ln2_b):
    given = dict(locals())
    w = {k: given[k] for k in _WEIGHTS}
    mom = {k: given["m_" + k] for k in _WEIGHTS}
    var = {k: given["v_" + k] for k in _WEIGHTS}

    me = (2 * lax.axis_index("x") + lax.axis_index("y")).astype(jnp.int32).reshape(1)
    tr = lambda t: t.transpose(0, 2, 1)
    shard = {k: (tr(w[k]) if k == "w_in" else w[k]) for k in BIG}
    qs = [_layer_prep({k: w[k][l] for k in SMALL}) for l in range(DEPTH)]

    first = ("w_in", "s5_w_glu", "w_out")
    follow = {(0, "w_in"): [(0, BIG[3:]), (1, first)], (0, "w_ff1"): [(1, BIG[3:])]}
    gathers = {}

    def start_gather(l, names, behind=None):
        lands = [_cast_to_slot(me, shard[k], l) for k in names]
        if behind is not None:
            lands, behind = lax.optimization_barrier((lands, behind))
        st = _push_start(f"gather_start_{l}_{names[0]}", lands, True)
        for k in names:
            gathers[l, k] = [names, st, None]
        return st[-1], behind

    token = start_gather(0, first[:1])[0] + start_gather(0, first[1:])[0]

    def fetch(l, name, after):
        names, st, got = gathers[l, name]
        tie = None
        if got is None:
            if l == 0 and name == "w_in":
                after = token
            lands = _push_wait(f"gather_wait_{l}_{names[0]}", st, after, True)
            for l2, names2 in follow.get((l, name), ()):
                tok, lands[0] = start_gather(l2, names2, lands[0])
                tie = tok if tie is None else tie + tok
            got = dict(zip(names, lands))
            for k in names:
                gathers[l, k][2] = got
        full = got[name]
        if name == "w_in":
            return _in_rows(full, token if tie is None else tie)
        if tie is not None:
            qs[l]["ln2_b"] = qs[l]["ln2_b"] + tie[0, 0]
        return full.reshape(D, D) if name == "w_out" else full

    scatters = []

    def emit(l, grads):
        names = tuple(grads)
        st = _push_start(f"scatter_start_{l}_{names[0]}", [grads[k] for k in names], False)
        scatters.append((l, names, st))
        return st[-1][0, 0]

    loss, dx, smalls = _local_step(x.reshape(N, D), loss_target.reshape(N, D), qs, _rope_tables(128), fetch, emit)

    out = {}
    native = _allreduce_small([[smalls[l][k] for k in NATIVE] for l in range(DEPTH)])
    native = dict(zip(NATIVE, native))
    loss = native["loss"][0, 0, 0] + native["loss"][1, 0, 0]
    gsmall = _finish_small(native, w)
    res = _adamw_small(*([t[k] for k in SMALL] for t in (gsmall, w, mom, var)))
    for i, k in enumerate(SMALL):
        out[k] = [gsmall[k], res[0][i], res[1][i], res[2][i]]

    recv, own = {}, {}

    def finish(keys, after):
        for l, names, st in scatters:
            if names[0] in keys:
                ops = _push_wait(f"scatter_wait_{l}_{names[0]}", st, after, False)
                for i, k in enumerate(names):
                    own[l, k], recv[l, k] = ops[i], ops[len(names) + i]
        sums = [_sum_sources(me, [recv[l, k] for l in range(DEPTH)], [own[l, k] for l in range(DEPTH)]) for k in keys]
        for k, mine, other in zip(keys, sums, _swap_sibling(sums)):
            shp = shard[k].shape
            r = _adamw([mine, other], *((tr(t[k]) if k == "w_in" else t[k]).reshape(-1, shp[-1]) for t in (w, mom, var)))
            r = [t.reshape(shp) for t in r]
            out[k] = [tr(t) for t in r] if k == "w_in" else r
        return out[keys[-1]][1]

    last = finish(("w_ff1", "w_ff2", "w_out", "s5_w_glu"), res[0][-1])
    finish(("w_in",), last)

    return (loss, dx.reshape(NSEQ, L, D), *[out[k][0] for k in _WEIGHTS], *[out[k][1] for k in _WEIGHTS],
            *[out[k][2] for k in _WEIGHTS], *[out[k][3] for k in _WEIGHTS])
```

```python
import functools
import math

import jax
import jax.numpy as jnp
from jax import lax
from jax.experimental import pallas as pl
from jax.experimental.pallas import tpu as pltpu

F32 = jnp.float32
MX = jnp.bfloat16
MESH = pl.DeviceIdType.MESH

DEPTH = 2
NSEQ = 2
L = 2048
N = NSEQ * L
D = 1024
DFF = 4096
NSHARD = 4
S5_G, S5_H, S5_P = 16, 16, 64
GLA_CHUNK = 64
NCHUNK = L // GLA_CHUNK
GLA_GROUP = 4
NGROUP = NCHUNK // GLA_GROUP
SWA_BLK = 128
NBLK = L // SWA_BLK
ROT = 16
ROPE_THETA = 500000.0
LN_EPS = 1e-5
ALPHA = (2 * DEPTH) ** 0.25
NEG_BIG = -1e30
DIN = 1824
DINP = 1920
ADAM_LR, ADAM_B1, ADAM_B2, ADAM_EPS, ADAM_WD, ADAM_STEP = 0.001, 0.9, 0.999, 1e-08, 0.01, 10
VMEM_LIMIT = 56 * 1024 * 1024
TT = 512
SW = 512
FFN_TM = 512
FFN_TM_W = 1024
FFN_WB = 1
FFN_VMEM = 60 * 1024 * 1024
INPROJ_BWD_TM = 512


def _cp(sem, vmem=VMEM_LIMIT):
    return pltpu.CompilerParams(dimension_semantics=sem, vmem_limit_bytes=vmem)


def _mm(a, b):
    return jnp.dot(a.astype(MX), b.astype(MX), preferred_element_type=F32)


def _mm_nt(a, b):
    return lax.dot_general(a.astype(MX), b.astype(MX), (((1,), (1,)), ((), ())), preferred_element_type=F32)


def _mm_tn(a, b):
    return lax.dot_general(a.astype(MX), b.astype(MX), (((0,), (0,)), ((), ())), preferred_element_type=F32)


@jax.custom_vjp
def _dmm(a, b):
    return _mm(a, b)


_dmm.defvjp(lambda a, b: (_mm(a, b), (a, b)), lambda r, g: (_mm_nt(g, r[1]), _mm_tn(r[0], g)))


@jax.custom_vjp
def _dmm_nt(a, b):
    return _mm_nt(a, b)


_dmm_nt.defvjp(lambda a, b: (_mm_nt(a, b), (a, b)), lambda r, g: (_mm(g, r[1]), _mm_tn(g, r[0])))


@jax.custom_vjp
def _dmm_tn(a, b):
    return _mm_tn(a, b)


_dmm_tn.defvjp(lambda a, b: (_mm_tn(a, b), (a, b)), lambda r, g: (_mm_nt(r[1], g), _mm(r[0], g)))


def _split3(x):
    hi = x.astype(MX)
    r1 = x - hi.astype(F32)
    mid = r1.astype(MX)
    lo = (r1 - mid.astype(F32)).astype(MX)
    return hi, mid, lo


def _chunk_pairs(rows, rev, strict):
    r = lax.broadcasted_iota(jnp.int32, (rows, rows), 0)
    c = lax.broadcasted_iota(jnp.int32, (rows, rows), 1)
    order = ((c > r) if strict else (c >= r)) if rev else ((c < r) if strict else (c <= r))
    return (r // GLA_CHUNK == c // GLA_CHUNK) & order


def _cums_impl(x, rev):
    rows, w = x.shape
    t = jnp.where(_chunk_pairs(rows, rev, False), 1.0, 0.0).astype(MX)
    s = jnp.dot(t, jnp.concatenate(_split3(x), axis=1), preferred_element_type=F32)
    return s[:, 0:w] + s[:, w:2 * w] + s[:, 2 * w:3 * w]


@functools.partial(jax.custom_vjp, nondiff_argnums=(1,))
def _cums(x, rev):
    return _cums_impl(x, rev)


_cums.defvjp(lambda x, rev: (_cums_impl(x, rev), None), lambda rev, r, g: (_cums_impl(g, not rev),))


def _ln_fwd(s, g, b):
    mu = jnp.mean(s, axis=-1, keepdims=True)
    xc = s - mu
    var = jnp.mean(xc * xc, axis=-1, keepdims=True)
    return xc * lax.rsqrt(var + LN_EPS) * g + b


def _ln_bwd(dy, s, g):
    mu = jnp.mean(s, axis=-1, keepdims=True)
    xc = s - mu
    var = jnp.mean(xc * xc, axis=-1, keepdims=True)
    rstd = lax.rsqrt(var + LN_EPS)
    xhat = xc * rstd
    dxh = dy * g
    ds = rstd * (dxh - jnp.mean(dxh, axis=-1, keepdims=True) - xhat * jnp.mean(dxh * xhat, axis=-1, keepdims=True))
    return ds, jnp.sum(dy * xhat, axis=0, keepdims=True), jnp.sum(dy, axis=0, keepdims=True)


def _sds(shape, dtype=F32):
    return jax.ShapeDtypeStruct(shape, dtype)


_IN_ROW_PIECES = (((0, 0), (0, 456)), ((1, 0), (456, 456)), ((2, 0), (912, 112)), ((2, 112), (1792, 32)),
                  ((2, 144), (1024, 312)), ((3, 0), (1336, 456)))


def _in_rows(g4, behind):
    def body(g_ref, behind_ref, o_ref, tmp):
        tmp[DIN:DINP] = jnp.zeros((DINP - DIN, D), F32)
        for (j, s0), (d0, n_) in _IN_ROW_PIECES:
            tmp[d0:d0 + n_] = g_ref[j, s0:s0 + n_].astype(F32)
        o_ref[...] = tmp[...].astype(MX)

    vm = pl.BlockSpec(memory_space=pltpu.VMEM)
    return pl.pallas_call(body, in_specs=[vm, pl.BlockSpec(memory_space=pl.ANY)], out_specs=vm,
                          out_shape=_sds((DINP, D), MX), scratch_shapes=[pltpu.VMEM((DINP, D), F32)], name="in_rows",
                          compiler_params=pltpu.CompilerParams(vmem_limit_bytes=VMEM_LIMIT))(g4, behind)


def _inproj_fwd(x, wt):
    tm = 512

    def body(x_ref, w_ref, h_ref):
        h_ref[...] = _mm_nt(x_ref[...], w_ref[...])

    return pl.pallas_call(
        body, grid=(N // tm,),
        in_specs=[pl.BlockSpec((tm, D), lambda i: (i, 0)), pl.BlockSpec((DINP, D), lambda i: (0, 0))],
        out_specs=pl.BlockSpec((tm, DINP), lambda i: (i, 0)),
        out_shape=_sds((N, DINP)), name="inproj_fwd", compiler_params=_cp(("parallel",)))(x, wt)


def _inproj_bwd(x, w, dxp, du2, dud, gq_f, gq_b, gk_f, gk_b, gv_f, gv_b, gr, daq, dakv, dhl):
    tm = INPROJ_BWD_TM
    nt = N // tm

    def body(x_ref, w_ref, dxp_ref, du2_ref, dud_ref, gqf, gqb, gkf, gkb, gvf, gvb, gr_ref, daq_ref, dakv_ref, dhl_ref,
             dx_ref, dw_ref, acc):
        i = pl.program_id(0)
        f = lambda r: r[...].astype(F32)
        dh = jnp.concatenate([
            du2_ref[0] + du2_ref[1] + f(dud_ref), f(gqf) + f(gqb), f(gkf) + f(gkb), f(gvf) + f(gvb),
            f(gr_ref), f(daq_ref), f(dakv_ref), f(dhl_ref)], axis=1)
        dx_ref[...] = dxp_ref[...] + _mm(dh, w_ref[...])
        contrib = _mm_tn(dh, x_ref[...])

        @pl.when(i == 0)
        def _():
            acc[...] = contrib

        @pl.when(i > 0)
        def _():
            acc[...] += contrib

        @pl.when(i == nt - 1)
        def _():
            for (j, d0), (s0, n_) in _IN_ROW_PIECES:
                dw_ref[j, d0:d0 + n_] = acc[s0:s0 + n_].astype(MX)

    row = lambda w_: pl.BlockSpec((tm, w_), lambda i: (i, 0))
    return pl.pallas_call(
        body, grid=(nt,),
        in_specs=[row(D), pl.BlockSpec((DINP, D), lambda i: (0, 0)), row(D),
                  pl.BlockSpec((2, tm, 256), lambda i: (0, i, 0)), row(256), row(128), row(128), row(128), row(128),
                  row(256), row(256), row(256), row(512), row(256), row(128)],
        out_specs=[row(D), pl.BlockSpec((NSHARD, DIN // NSHARD, D), lambda i: (0, 0, 0))],
        out_shape=[_sds((N, D)), _sds((NSHARD, DIN // NSHARD, D), MX)],
        scratch_shapes=[pltpu.VMEM((DINP, D), F32)],
        name="inproj_bwd", compiler_params=_cp(("arbitrary",)))(
            x, w, dxp, du2, dud, gq_f, gq_b, gk_f, gk_b, gv_f, gv_b, gr, daq, dakv, dhl)


def _scan_tables(mr, mi, reverse):
    pw = [(mr, mi)]
    for _ in range(7):
        pr, pi = pw[-1]
        pw.append((pr * mr - pi * mi, pr * mi + pi * mr))
    rows = jnp.arange(8)[:, None]
    out = []
    for d in (1, 2, 4):
        keep = rows >= d
        out += [jnp.where(keep, pw[d - 1][0][None], 0.0), jnp.where(keep, pw[d - 1][1][None], 0.0)]
    out += [jnp.stack([p[0] for p in pw]), jnp.stack([p[1] for p in pw])]
    t = jnp.stack(out)
    if reverse:
        t = t[:, ::-1, :]
    return t.reshape(8, 8, 2, SW).transpose(2, 0, 1, 3)


def _tile_scan(xr, xi, a, cr, ci, reverse):
    for lvl, d in enumerate((1, 2, 4)):
        sh = 8 - d if reverse else d
        sr = pltpu.roll(xr, sh, 0)
        si = pltpu.roll(xi, sh, 0)
        ar, ai = a[2 * lvl], a[2 * lvl + 1]
        xr, xi = xr + ar * sr - ai * si, xi + ar * si + ai * sr
    pr, pi = a[6], a[7]
    return xr + pr * cr - pi * ci, xi + pr * ci + pi * cr


NJ = TT // 8


def _lockstep_tables(mr, mi, reverse):
    nr, ni = mr, mi
    for _ in range(NJ.bit_length() - 1):
        nr, ni = nr * nr - ni * ni, 2.0 * nr * ni
    pr, pi = mr[None], mi[None]
    while pr.shape[0] < NJ:
        k = pr.shape[0]
        tr, ti = pr[k - 1], pi[k - 1]
        pr, pi = (jnp.concatenate([pr, pr * tr - pi * ti]), jnp.concatenate([pi, pr * ti + pi * tr]))
    if reverse:
        pr, pi = pr[::-1], pi[::-1]
    rows = jnp.broadcast_to(jnp.stack([mr, mi])[:, None, :], (2, 8, 2 * SW))
    link = _scan_tables(nr, ni, reverse)
    a = jnp.concatenate([rows.reshape(2, 8, 2, SW).transpose(2, 0, 1, 3), link], axis=1)
    return a, jnp.stack([pr, pi]).reshape(2, NJ, 2, SW).transpose(2, 0, 1, 3)


def _to_lockstep(ref, *lead):
    return jnp.concatenate([ref[(*lead, pl.ds(j, 8, stride=NJ), slice(None))] for j in range(NJ)], axis=0)


def _from_lockstep(val, ref, *lead):
    for j in range(NJ):
        ref[(*lead, pl.ds(j, 8, stride=NJ), slice(None))] = val[8 * j:8 * j + 8]


def _expand_powers(p_ref, pexp):
    for c in range(2):
        for j in range(NJ):
            pexp[c, j] = jnp.broadcast_to(p_ref[0, 0, c, j:j + 1, :], (8, SW))


def _lockstep_scan(xre, xim, a_ref, pexp, car, reverse, extra=None):
    a = [a_ref[0, 0, k] for k in range(10)]
    mr, mi = a[0], a[1]
    order = (lambda i: NJ - 1 - i) if reverse else (lambda i: i)

    def local(i, hcar):
        hr, hi = hcar
        r0 = pl.multiple_of(order(i) * 8, 8)
        hr, hi = mr * hr - mi * hi + xre[pl.ds(r0, 8), :], mr * hi + mi * hr + xim[pl.ds(r0, 8), :]
        xre[pl.ds(r0, 8), :] = hr
        xim[pl.ds(r0, 8), :] = hi
        return hr, hi

    z8 = jnp.zeros((8, SW), F32)
    er, ei = lax.fori_loop(0, NJ, local, (z8, z8), unroll=4)
    c0r, c0i = car[0], car[1]
    er, ei = _tile_scan(er, ei, a[2:], c0r, c0i, reverse)
    rowid = lax.broadcasted_iota(jnp.int32, (8, SW), 0)
    first, sh, last = (7, 7, 0) if reverse else (0, 1, 7)
    cvr = jnp.where(rowid == first, c0r, pltpu.roll(er, sh, 0))
    cvi = jnp.where(rowid == first, c0i, pltpu.roll(ei, sh, 0))
    car[0] = jnp.broadcast_to(er[last:last + 1, :], (8, SW))
    car[1] = jnp.broadcast_to(ei[last:last + 1, :], (8, SW))

    def fix(i, carry):
        j = order(i)
        r0 = pl.multiple_of(j * 8, 8)
        pr, pi = pexp[0, j], pexp[1, j]
        sr = xre[pl.ds(r0, 8), :] + pr * cvr - pi * cvi
        si = xim[pl.ds(r0, 8), :] + pr * cvi + pi * cvr
        xre[pl.ds(r0, 8), :] = sr
        xim[pl.ds(r0, 8), :] = si
        if extra is None:
            return carry
        return (sr, si, extra(r0, sr, si, carry[0], carry[1], carry[2]))

    init = (cvr, cvi, extra(None, None, None, None, None, None)) if extra is not None else 0
    return lax.fori_loop(0, NJ, fix, init, unroll=4)


def _s5_time_block(z, s, t, adjoint):
    flip = (1 - z) if adjoint else z
    return s * (L // TT) + t + flip * (L // TT - 1 - 2 * t)


def _s5_fwd(h, bre, bim, cre, cim, tab):
    nt = L // TT
    taba, tabp = tab

    def body(u_ref, bre_ref, bim_ref, cre_ref, cim_ref, a_ref, p_ref, hre_ref, him_ref, y_ref, car, pexp):
        z = pl.program_id(1)
        s = pl.program_id(2)
        tc = pl.program_id(3)

        @pl.when(tc == 0)
        def _():
            car[...] = jnp.zeros_like(car)

        @pl.when((tc == 0) & (s == 0))
        def _():
            _expand_powers(p_ref, pexp)

        u = _to_lockstep(u_ref)
        hre_ref[0] = _mm(u, bre_ref[0, 0])
        him_ref[0] = _mm(u, bim_ref[0, 0])

        @pl.when(z == 0)
        def _():
            _lockstep_scan(hre_ref.at[0], him_ref.at[0], a_ref, pexp, car, False)

        @pl.when(z == 1)
        def _():
            _lockstep_scan(hre_ref.at[0], him_ref.at[0], a_ref, pexp, car, True)

        _from_lockstep(_mm(hre_ref[0], cre_ref[0, 0]) - _mm(him_ref[0], cim_ref[0, 0]), y_ref, 0)

    tb = lambda b, z, s, t: _s5_time_block(z, s, t, False)
    wspec = lambda r, c: pl.BlockSpec((1, 1, r, c), lambda b, z, s, t: (z, b, 0, 0))
    return pl.pallas_call(
        body, grid=(2, 2, NSEQ, nt),
        in_specs=[pl.BlockSpec((TT, 128), lambda b, z, s, t: (tb(b, z, s, t), b)),
                  wspec(128, SW), wspec(128, SW), wspec(SW, 128), wspec(SW, 128),
                  pl.BlockSpec((1, 1, 10, 8, SW), lambda b, z, s, t: (z, b, 0, 0, 0)),
                  pl.BlockSpec((1, 1, 2, NJ, SW), lambda b, z, s, t: (z, b, 0, 0, 0))],
        out_specs=[pl.BlockSpec((1, TT, SW), lambda b, z, s, t: (z, tb(b, z, s, t), b)),
                   pl.BlockSpec((1, TT, SW), lambda b, z, s, t: (z, tb(b, z, s, t), b)),
                   pl.BlockSpec((1, TT, 128), lambda b, z, s, t: (z, tb(b, z, s, t), b))],
        out_shape=[_sds((2, N, 2 * SW)), _sds((2, N, 2 * SW)), _sds((2, N, 256))],
        scratch_shapes=[pltpu.VMEM((2, 8, SW), F32), pltpu.VMEM((2, NJ, 8, SW), F32)],
        name="s5_fwd", compiler_params=_cp(("arbitrary",) * 4))(h, bre, bim, cre, cim, taba, tabp)


def _s5_bwd(h, dyp, hre, him, bre, bim, cre, cim, tabc):
    nt = L // TT
    taba, tabp = tabc

    def body(u_ref, dy_ref, hre_ref, him_ref, bre_ref, bim_ref, cre_ref, cim_ref, a_ref, p_ref,
             du_ref, dbre_ref, dbim_ref, dcre_ref, dcim_ref, dmu_ref, gre, gim, car, acc, macc, pexp):
        z = pl.program_id(1)
        s = pl.program_id(2)
        tc = pl.program_id(3)

        @pl.when(tc == 0)
        def _():
            car[...] = jnp.zeros_like(car)

        @pl.when((tc == 0) & (s == 0))
        def _():
            acc[...] = jnp.zeros_like(acc)
            macc[...] = jnp.zeros_like(macc)
            _expand_powers(p_ref, pexp)

        dy = _to_lockstep(dy_ref)
        gre[...] = _mm_nt(dy, cre_ref[0, 0])
        gim[...] = -_mm_nt(dy, cim_ref[0, 0])

        def run(reverse):
            def pair(r0, gr_, gi_, pvr, pvi, m):
                if r0 is None:
                    return (macc[0], macc[1])
                hr = hre_ref[0, pl.ds(r0, 8), :]
                hi = him_ref[0, pl.ds(r0, 8), :]
                return (m[0] + pvr * hr + pvi * hi, m[1] + pvi * hr - pvr * hi)

            _, _, (dmr, dmi) = _lockstep_scan(gre, gim, a_ref, pexp, car, reverse, pair)
            macc[0] = dmr
            macc[1] = dmi

        @pl.when(z == 0)
        def _():
            run(True)

        @pl.when(z == 1)
        def _():
            run(False)

        gr = gre[...]
        gi = gim[...]
        u = _to_lockstep(u_ref)
        _from_lockstep(_mm_nt(gr, bre_ref[0, 0]) + _mm_nt(gi, bim_ref[0, 0]), du_ref, 0)
        acc[0] += _mm_tn(u, gr)
        acc[1] += _mm_tn(u, gi)
        acc[2] += _mm_tn(dy, hre_ref[0])
        acc[3] -= _mm_tn(dy, him_ref[0])

        @pl.when((tc == nt - 1) & (s == NSEQ - 1))
        def _():
            grp = lax.broadcasted_iota(jnp.int32, (S5_H, SW), 1) // S5_P
            for k, out in enumerate((dbre_ref, dbim_ref, dcre_ref, dcim_ref)):
                c = jnp.zeros((S5_H, SW), F32)
                for i in range(8):
                    c = c + jnp.where(grp == i, acc[k, i * S5_H:(i + 1) * S5_H, :], 0.0)
                out[0, 0] = c
            dmu_ref[0, 0] = jnp.concatenate([jnp.sum(macc[0], axis=0, keepdims=True),
                                             jnp.sum(macc[1], axis=0, keepdims=True)], axis=0)

    tb = lambda b, z, s, t: _s5_time_block(z, s, t, True)
    wspec = lambda r, c: pl.BlockSpec((1, 1, r, c), lambda b, z, s, t: (z, b, 0, 0))
    tok = lambda w_: pl.BlockSpec((TT, w_), lambda b, z, s, t: (tb(b, z, s, t), b))
    st = pl.BlockSpec((1, TT, SW), lambda b, z, s, t: (z, tb(b, z, s, t), b))
    return pl.pallas_call(
        body, grid=(2, 2, NSEQ, nt),
        in_specs=[tok(128), tok(128), st, st, wspec(128, SW), wspec(128, SW), wspec(SW, 128), wspec(SW, 128),
                  pl.BlockSpec((1, 1, 10, 8, SW), lambda b, z, s, t: (z, b, 0, 0, 0)),
                  pl.BlockSpec((1, 1, 2, NJ, SW), lambda b, z, s, t: (z, b, 0, 0, 0))],
        out_specs=[pl.BlockSpec((1, TT, 128), lambda b, z, s, t: (z, tb(b, z, s, t), b)),
                   wspec(S5_H, SW), wspec(S5_H, SW), wspec(S5_H, SW), wspec(S5_H, SW),
                   wspec(2, SW)],
        out_shape=[_sds((2, N, 256))] + [_sds((2, 2, S5_H, SW))] * 4 + [_sds((2, 2, 2, SW))],
        scratch_shapes=[pltpu.VMEM((TT, SW), F32), pltpu.VMEM((TT, SW), F32), pltpu.VMEM((2, 8, SW), F32),
                        pltpu.VMEM((4, 128, SW), F32), pltpu.VMEM((2, 8, SW), F32), pltpu.VMEM((2, NJ, 8, SW), F32)],
        name="s5_bwd", compiler_params=_cp(("arbitrary",) * 4))(h, dyp, hre, him, bre, bim, cre, cim, taba, tabp)


_GELU_C = math.sqrt(2.0 / math.pi)


def _gelu(y):
    return 0.5 * y * (1.0 + jnp.tanh(_GELU_C * (y + 0.044715 * y * y * y)))


def _gelu_grad(y):
    t = jnp.tanh(_GELU_C * (y + 0.044715 * y * y * y))
    return 0.5 * (1.0 + t) + 0.5 * y * (1.0 - t * t) * _GELU_C * (1.0 + 3 * 0.044715 * y * y)


def _glu_halves(w4_ref):
    return (jnp.concatenate([w4_ref[0], w4_ref[1]], axis=1), jnp.concatenate([w4_ref[2], w4_ref[3]], axis=1))


def _s5_glu_fwd(y2, h, dsk, w4, bv, bg):
    tm = 512

    def body(y2_ref, u_ref, d_ref, w4_ref, bv_ref, bg_ref, ya_ref):
        wv, wg = _glu_halves(w4_ref)
        z = _gelu(y2_ref[0] + y2_ref[1] + d_ref[...] * u_ref[...])
        val = _mm(z, wv) + bv_ref[...]
        gate = _mm(z, wg) + bg_ref[...]
        ya_ref[...] = (val * jax.nn.sigmoid(gate)).astype(MX)

    full = lambda r, c: pl.BlockSpec((r, c), lambda i: (0, 0))
    return pl.pallas_call(
        body, grid=(N // tm,),
        in_specs=[pl.BlockSpec((2, tm, 256), lambda i: (0, i, 0)), pl.BlockSpec((tm, 256), lambda i: (i, 0)),
                  full(1, 256), pl.BlockSpec((NSHARD, 256, 128), lambda i: (0, 0, 0)), full(1, 256), full(1, 256)],
        out_specs=pl.BlockSpec((tm, 256), lambda i: (i, 0)),
        out_shape=_sds((N, 256), MX), name="s5_glu_fwd", compiler_params=_cp(("parallel",)))(y2, h, dsk, w4, bv, bg)


def _s5_glu_bwd(y2, h, dsk, w4, bv, bg, dya):
    tm = 512
    nt = N // tm

    def body(y2_ref, u_ref, d_ref, w4_ref, bv_ref, bg_ref, dya_ref,
             dyp_ref, dud_ref, dd_ref, dw4_ref, dbv_ref, dbg_ref, accv, accg):
        i = pl.program_id(0)

        @pl.when(i == 0)
        def _():
            for r in (dd_ref, accv, accg, dbv_ref, dbg_ref):
                r[...] = jnp.zeros_like(r)

        wv, wg = _glu_halves(w4_ref)
        u = u_ref[...]
        y = y2_ref[0] + y2_ref[1] + d_ref[...] * u
        z = _gelu(y)
        val = _mm(z, wv) + bv_ref[...]
        sig = jax.nn.sigmoid(_mm(z, wg) + bg_ref[...])
        dya = dya_ref[...]
        dval = dya * sig
        dgate = dya * val * sig * (1.0 - sig)
        dz = _mm_nt(dval, wv) + _mm_nt(dgate, wg)
        dy = dz * _gelu_grad(y)
        dyp_ref[...] = dy
        dud_ref[...] = (dy * d_ref[...]).astype(MX)
        dd_ref[...] += jnp.sum(dy * u, axis=0, keepdims=True)
        accv[...] += _mm_tn(z, dval)
        accg[...] += _mm_tn(z, dgate)
        dbv_ref[...] += jnp.sum(dval, axis=0, keepdims=True)
        dbg_ref[...] += jnp.sum(dgate, axis=0, keepdims=True)

        @pl.when(i == nt - 1)
        def _():
            dw4_ref[0] = accv[:, 0:128].astype(MX)
            dw4_ref[1] = accv[:, 128:256].astype(MX)
            dw4_ref[2] = accg[:, 0:128].astype(MX)
            dw4_ref[3] = accg[:, 128:256].astype(MX)

    full = lambda r, c: pl.BlockSpec((r, c), lambda i: (0, 0))
    row = pl.BlockSpec((tm, 256), lambda i: (i, 0))
    wspec = pl.BlockSpec((NSHARD, 256, 128), lambda i: (0, 0, 0))
    return pl.pallas_call(
        body, grid=(nt,),
        in_specs=[pl.BlockSpec((2, tm, 256), lambda i: (0, i, 0)), row, full(1, 256), wspec, full(1, 256), full(1, 256),
                  row],
        out_specs=[row, row, full(1, 256), wspec, full(1, 256), full(1, 256)],
        out_shape=[_sds((N, 256)), _sds((N, 256), MX), _sds((1, 256)), _sds((NSHARD, 256, 128), MX), _sds((1, 256)),
                   _sds((1, 256))],
        scratch_shapes=[pltpu.VMEM((256, 256), F32), pltpu.VMEM((256, 256), F32)],
        name="s5_glu_bwd", compiler_params=_cp(("arbitrary",)))(y2, h, dsk, w4, bv, bg, dya)


def _logsig(x):
    return jnp.minimum(x, 0.0) - jnp.log(1.0 + jnp.exp(-jnp.abs(x)))


def _gla_gate_fwd(h, wa, ba):
    tm = 512

    def body(hl_ref, wa_ref, ba_ref, la_ref):
        la_ref[...] = _logsig(_mm(hl_ref[...], wa_ref[...]) + ba_ref[...]) * (1.0 / 16.0)

    return pl.pallas_call(
        body, grid=(N // tm,),
        in_specs=[pl.BlockSpec((tm, 128), lambda i: (i, 14)), pl.BlockSpec((128, 256), lambda i: (0, 0)),
                  pl.BlockSpec((1, 256), lambda i: (0, 0))],
        out_specs=pl.BlockSpec((tm, 256), lambda i: (i, 0)),
        out_shape=_sds((N, 256)), name="gla_gate_fwd", compiler_params=_cp(("parallel",)))(h, wa, ba)


def _gla_gate_bwd(h, wa, ba, dla_f, dla_b):
    tm = 512

    def body(hl_ref, wa_ref, ba_ref, df_ref, db_ref, dhl_ref, dwa_ref, dba_ref):
        i = pl.program_id(0)

        @pl.when(i == 0)
        def _():
            dwa_ref[...] = jnp.zeros_like(dwa_ref)
            dba_ref[...] = jnp.zeros_like(dba_ref)

        hl = hl_ref[...]
        pre = _mm(hl, wa_ref[...]) + ba_ref[...]
        dpre = jnp.concatenate([df_ref[...], db_ref[...]], axis=1) * (1.0 / 16.0) * jax.nn.sigmoid(-pre)
        dhl_ref[...] = _mm_nt(dpre, wa_ref[...]).astype(MX)
        dwa_ref[...] += _mm_tn(hl, dpre)[0:32]
        dba_ref[...] += jnp.sum(dpre, axis=0, keepdims=True)

    row = pl.BlockSpec((tm, 128), lambda i: (i, 0))
    return pl.pallas_call(
        body, grid=(N // tm,),
        in_specs=[pl.BlockSpec((tm, 128), lambda i: (i, 14)), pl.BlockSpec((128, 256), lambda i: (0, 0)),
                  pl.BlockSpec((1, 256), lambda i: (0, 0)), row, row],
        out_specs=[row, pl.BlockSpec((32, 256), lambda i: (0, 0)), pl.BlockSpec((1, 256), lambda i: (0, 0))],
        out_shape=[_sds((N, 128), MX), _sds((32, 256)), _sds((1, 256))],
        name="gla_gate_bwd", compiler_params=_cp(("arbitrary",)))(h, wa, ba, dla_f, dla_b)


def _gla_chunk(q, k, v, la, st, rev):
    c = GLA_CHUNK
    rows = q.shape[0]
    nch = rows // c
    b = _cums(la, rev)
    blc = [jnp.sum(la[i * c:(i + 1) * c], axis=0, keepdims=True) for i in range(nch)]
    bl = jnp.concatenate([jnp.broadcast_to(t, (c, 128)) for t in blc], axis=0)
    q_in = q * (32.0 ** -0.5) * jnp.exp(b)
    k_in = k * jnp.exp(-b)
    k_st = k * jnp.exp(bl - b)
    lane_k = lax.broadcasted_iota(jnp.int32, (1, 128), 1) // 32
    lane_v = lax.broadcasted_iota(jnp.int32, (1, 256), 1) // 64
    qs = jnp.concatenate([jnp.where(lane_k == hd, q_in, 0.0) for hd in range(4)], axis=0)
    a = _dmm_nt(qs, k_in)
    a = jnp.where(jnp.concatenate([_chunk_pairs(rows, rev, rev)] * 4, axis=0), a, 0.0)
    o4 = _dmm(a, v)
    o = jnp.zeros((rows, 256), F32)
    for hd in range(4):
        o = o + jnp.where(lane_v == hd, o4[hd * rows:(hd + 1) * rows], 0.0)
    bd = (lax.broadcasted_iota(jnp.int32, (256, 128), 0) // 64) == (lax.broadcasted_iota(jnp.int32, (256, 128), 1) // 32)
    inter = [None] * nch
    for i in (reversed(range(nch)) if rev else range(nch)):
        sl = slice(i * c, (i + 1) * c)
        inter[i] = _dmm_nt(q_in[sl], st)
        st = jnp.exp(blc[i]) * st + jnp.where(bd, _dmm_tn(v[sl], k_st[sl]), 0.0)
    return o + jnp.concatenate(inter, axis=0), st


def _gla_chunk_of(c, rev):
    return NGROUP - 1 - c if rev else c


def _gla_fwd(h, la2):
    c = GLA_GROUP * GLA_CHUNK

    def body(qf, kf, vf, laf, qb, kb, vb, lab, of_ref, ob_ref, sf_ref, sb_ref, stf, stb):
        @pl.when(pl.program_id(0) == 0)
        def _():
            stf[...] = jnp.zeros_like(stf)
            stb[...] = jnp.zeros_like(stb)

        ins = [(qf[s], kf[s], vf[s], laf[s], stf[s], qb[s], kb[s], vb[s], lab[s], stb[s]) for s in range(NSEQ)]
        outs = [(_gla_chunk(*t[:5], False), _gla_chunk(*t[5:], True)) for t in ins]
        for s in range(NSEQ):
            sf_ref[s, 0] = ins[s][4]
            sb_ref[s, 0] = ins[s][9]
            (of_ref[s], stf[s]), (ob_ref[s], stb[s]) = outs[s]

    def specs(rev):
        ch = lambda i: _gla_chunk_of(i, rev)
        return [pl.BlockSpec((NSEQ, c, 128), lambda i: (0, ch(i), 2)), pl.BlockSpec((NSEQ, c, 128), lambda i: (0, ch(i), 3)),
                pl.BlockSpec((NSEQ, c, 256), lambda i: (0, ch(i), 2)),
                pl.BlockSpec((NSEQ, c, 128), lambda i: (0, ch(i), 1 if rev else 0))]

    orow = lambda rev: pl.BlockSpec((NSEQ, c, 256), lambda i: (0, _gla_chunk_of(i, rev), 0))
    srow = lambda rev: pl.BlockSpec((NSEQ, 1, 256, 128), lambda i: (0, _gla_chunk_of(i, rev), 0, 0))
    h3, la3 = h.reshape(NSEQ, L, DINP), la2.reshape(NSEQ, L, 256)
    of, ob, sf, sb = pl.pallas_call(
        body, grid=(NGROUP,),
        in_specs=specs(False) + specs(True),
        out_specs=[orow(False), orow(True), srow(False), srow(True)],
        out_shape=[_sds((NSEQ, L, 256)), _sds((NSEQ, L, 256)), _sds((NSEQ, NGROUP, 256, 128)),
                   _sds((NSEQ, NGROUP, 256, 128))],
        scratch_shapes=[pltpu.VMEM((NSEQ, 256, 128), F32), pltpu.VMEM((NSEQ, 256, 128), F32)],
        name="gla_fwd", compiler_params=_cp(("arbitrary",)))(h3, h3, h3, la3, h3, h3, h3, la3)
    return of.reshape(N, 256), ob.reshape(N, 256), sf, sb


def _gla_bwd(h, la2, do, sf, sb):
    c = GLA_GROUP * GLA_CHUNK

    def body(qf, kf, vf, laf, dof, sfr, qb, kb, vb, lab, dob, sbr,
             dqf, dkf, dvf, dlf, dqb, dkb, dvb, dlb, dstf, dstb):
        @pl.when(pl.program_id(0) == 0)
        def _():
            dstf[...] = jnp.zeros_like(dstf)
            dstb[...] = jnp.zeros_like(dstb)

        def one(s, q, k, v, la, do_, st, dst, rev):
            _, vjp = jax.vjp(functools.partial(_gla_chunk, rev=rev), q[s], k[s], v[s], la[s], st[s, 0])
            return vjp((do_[s], dst[s]))

        res = [(one(s, qf, kf, vf, laf, dof, sfr, dstf, False), one(s, qb, kb, vb, lab, dob, sbr, dstb, True))
               for s in range(NSEQ)]
        for s in range(NSEQ):
            for (gq, gk, gv, gl, gs), (dq, dk, dv, dl, dst) in ((res[s][0], (dqf, dkf, dvf, dlf, dstf)),
                                                                  (res[s][1], (dqb, dkb, dvb, dlb, dstb))):
                dq[s], dk[s], dv[s] = gq.astype(MX), gk.astype(MX), gv.astype(MX)
                dl[s], dst[s] = gl, gs

    def specs(rev):
        ch = lambda i: _gla_chunk_of(i, not rev)
        return [pl.BlockSpec((NSEQ, c, 128), lambda i: (0, ch(i), 2)), pl.BlockSpec((NSEQ, c, 128), lambda i: (0, ch(i), 3)),
                pl.BlockSpec((NSEQ, c, 256), lambda i: (0, ch(i), 2)),
                pl.BlockSpec((NSEQ, c, 128), lambda i: (0, ch(i), 1 if rev else 0)),
                pl.BlockSpec((NSEQ, c, 256), lambda i: (0, ch(i), 0)),
                pl.BlockSpec((NSEQ, 1, 256, 128), lambda i: (0, ch(i), 0, 0))]

    def ospecs(rev):
        ch = lambda i: _gla_chunk_of(i, not rev)
        n = pl.BlockSpec((NSEQ, c, 128), lambda i: (0, ch(i), 0))
        return [n, n, pl.BlockSpec((NSEQ, c, 256), lambda i: (0, ch(i), 0)), n]

    oshape = [_sds((NSEQ, L, 128), MX), _sds((NSEQ, L, 128), MX), _sds((NSEQ, L, 256), MX), _sds((NSEQ, L, 128))]
    h3, la3, do3 = h.reshape(NSEQ, L, DINP), la2.reshape(NSEQ, L, 256), do.reshape(NSEQ, L, 256)
    res = pl.pallas_call(
        body, grid=(NGROUP,),
        in_specs=specs(False) + specs(True),
        out_specs=ospecs(False) + ospecs(True),
        out_shape=oshape + oshape,
        scratch_shapes=[pltpu.VMEM((NSEQ, 256, 128), F32), pltpu.VMEM((NSEQ, 256, 128), F32)],
        name="gla_bwd", compiler_params=_cp(("arbitrary",)))(h3, h3, h3, la3, do3, sf, h3, h3, h3, la3, do3, sb)
    return [r.reshape(N, r.shape[-1]) for r in res]


def _gla_post(of, ob, r, g):
    o = of + ob
    head = lax.broadcasted_iota(jnp.int32, (1, 256), 1) // 64
    mu = jnp.zeros_like(o)
    for hd in range(4):
        mu = mu + jnp.where(head == hd, jnp.sum(jnp.where(head == hd, o, 0.0), axis=-1, keepdims=True) * (1.0 / 64.0), 0.0)
    xc = o - mu
    var = jnp.zeros_like(o)
    for hd in range(4):
        var = var + jnp.where(head == hd, jnp.sum(jnp.where(head == hd, xc * xc, 0.0), axis=-1, keepdims=True) * (1.0 / 64.0), 0.0)
    return xc * lax.rsqrt(var + LN_EPS) * g * (r * jax.nn.sigmoid(r))


def _gla_post_fwd(of, ob, h, g):
    tm = 512

    def body(of_ref, ob_ref, r_ref, g_ref, y_ref):
        y_ref[...] = _gla_post(of_ref[...], ob_ref[...], r_ref[...], g_ref[...]).astype(MX)

    row = pl.BlockSpec((tm, 256), lambda i: (i, 0))
    return pl.pallas_call(
        body, grid=(N // tm,),
        in_specs=[row, row, pl.BlockSpec((tm, 256), lambda i: (i, 3)), pl.BlockSpec((1, 256), lambda i: (0, 0))],
        out_specs=row, out_shape=_sds((N, 256), MX), name="gla_post_fwd", compiler_params=_cp(("parallel",)))(of, ob, h, g)


def _gla_post_bwd(of, ob, h, g, dyb):
    tm = 512

    def body(of_ref, ob_ref, r_ref, g_ref, dy_ref, do_ref, dr_ref, dg_ref):
        @pl.when(pl.program_id(0) == 0)
        def _():
            dg_ref[...] = jnp.zeros_like(dg_ref)

        _, vjp = jax.vjp(_gla_post, of_ref[...], ob_ref[...], r_ref[...], g_ref[...])
        go, _, gr, gg = vjp(dy_ref[...])
        do_ref[...] = go
        dr_ref[...] = gr.astype(MX)
        dg_ref[...] += gg

    row = pl.BlockSpec((tm, 256), lambda i: (i, 0))
    one = pl.BlockSpec((1, 256), lambda i: (0, 0))
    return pl.pallas_call(
        body, grid=(N // tm,),
        in_specs=[row, row, pl.BlockSpec((tm, 256), lambda i: (i, 3)), one, row],
        out_specs=[row, row, one], out_shape=[_sds((N, 256)), _sds((N, 256), MX), _sds((1, 256))],
        name="gla_post_bwd", compiler_params=_cp(("arbitrary",)))(of, ob, h, g, dyb)


def _rope_tables(width):
    pos = jnp.arange(L, dtype=F32)
    inv_freq = ROPE_THETA ** (-jnp.arange(0, ROT, 2, dtype=F32) / ROT)
    ang = pos[:, None] * inv_freq[None, :]
    cos, sin = jnp.cos(ang), jnp.sin(ang)
    one = jnp.ones((L, 64 - ROT), F32)
    zero = jnp.zeros((L, 64 - ROT), F32)
    z8 = jnp.zeros((L, ROT // 2), F32)
    c = jnp.concatenate([cos, cos, one], axis=1)
    sa = jnp.concatenate([z8, sin, zero], axis=1)
    sb = jnp.concatenate([-sin, z8, zero], axis=1)
    rep = width // 64
    return jnp.stack([jnp.tile(c, (1, rep)), jnp.tile(sa, (1, rep)), jnp.tile(sb, (1, rep))])


def _pieces(t, f):
    out = [f(t[:, c * 128:(c + 1) * 128]) for c in range(t.shape[-1] // 128)]
    return out[0] if len(out) == 1 else jnp.concatenate(out, axis=1)


def _rope(t, tab):
    return _pieces(t, lambda x: x * tab[0] + pltpu.roll(x, ROT // 2, 1) * tab[1] + pltpu.roll(x, 128 - ROT // 2, 1) * tab[2])


def _rope_t(g, tab):
    return _pieces(g, lambda x: x * tab[0] + pltpu.roll(x * tab[1], 128 - ROT // 2, 1) + pltpu.roll(x * tab[2], ROT // 2, 1))


def _swa_pad_kv(kv_ref, tk_ref, kexp, vexp):
    z = jnp.zeros((SWA_BLK, 256), F32)
    kr = _rope(kv_ref[:, 0:128], tk_ref[...])
    for hk in range(2):
        for pad in (kexp, vexp):
            pad[hk, 0:SWA_BLK] = z
            pad[hk, SWA_BLK + L:] = z
        kexp[hk, SWA_BLK:SWA_BLK + L] = _swa_expand(kr, hk)
        vexp[hk, SWA_BLK:SWA_BLK + L] = _swa_expand(kv_ref[:, 128:256], hk)


def _swa_expand(x, hk):
    lane = lax.broadcasted_iota(jnp.int32, x.shape, 1)
    sw = pltpu.roll(x, 64, 1)
    pair = jnp.where(lane < 64, x, sw) if hk == 0 else jnp.where(lane < 64, sw, x)
    return jnp.concatenate([pair, pair], axis=1)


def _swa_fold(x, hk):
    a = x[:, 0:128] + x[:, 128:256]
    t = a + pltpu.roll(a, 64, 1)
    lane = lax.broadcasted_iota(jnp.int32, a.shape, 1)
    return jnp.where((lane < 64) if hk == 0 else (lane >= 64), t, 0.0)


def _swa_probs(q2, kexp, n, sink_ref, hk):
    slot = lax.broadcasted_iota(jnp.int32, (1, 256), 1) // 64
    qs = jnp.concatenate([jnp.where(slot == g, q2, 0.0) for g in range(4)], axis=0)
    s = _mm_nt(qs, kexp) * 0.125
    i = lax.broadcasted_iota(jnp.int32, (SWA_BLK, 3 * SWA_BLK), 0)
    j = lax.broadcasted_iota(jnp.int32, (SWA_BLK, 3 * SWA_BLK), 1)
    kpos = n * SWA_BLK - SWA_BLK + j
    ok = (j - i >= 0) & (j - i <= 2 * SWA_BLK) & (kpos >= 0) & (kpos < L)
    s = jnp.where(jnp.concatenate([ok] * 4, axis=0), s, NEG_BIG)
    rowg = lax.broadcasted_iota(jnp.int32, (4 * SWA_BLK, 1), 0) // SWA_BLK
    sink = jnp.zeros((4 * SWA_BLK, 1), F32)
    for g in range(4):
        sink = jnp.where(rowg == g, sink_ref[hk * 4 + g], sink)
    m = jnp.maximum(jnp.max(s, axis=-1, keepdims=True), sink)
    p = jnp.exp(s - m)
    ps = jnp.exp(sink - m)
    inv = 1.0 / (jnp.sum(p, axis=-1, keepdims=True) + ps)
    return qs, p * inv, ps * inv, slot, rowg


def _swa_qtab(tk_ref, r0):
    return [tk_ref[i, pl.ds(r0, SWA_BLK), :] for i in range(3)]


def _swa_fwd(h, tk, sink):
    def body(sink_ref, q_ref, kv_ref, tk_ref, y_ref, kexp, vexp):
        n = pl.program_id(1)

        @pl.when(n == 0)
        def _():
            _swa_pad_kv(kv_ref, tk_ref, kexp, vexp)

        r0 = pl.multiple_of(n * SWA_BLK, SWA_BLK)
        q = _rope(q_ref[...], _swa_qtab(tk_ref, r0))
        for hk in range(2):
            _, p, _, slot, _ = _swa_probs(q[:, hk * 256:(hk + 1) * 256], kexp[hk, pl.ds(r0, 3 * SWA_BLK), :], n,
                                          sink_ref, hk)
            o4 = _mm(p, vexp[hk, pl.ds(r0, 3 * SWA_BLK), :])
            o = jnp.zeros((SWA_BLK, 256), F32)
            for g in range(4):
                o = o + jnp.where(slot == g, o4[g * SWA_BLK:(g + 1) * SWA_BLK], 0.0)
            y_ref[:, hk * 256:(hk + 1) * 256] = o.astype(MX)

    return pl.pallas_call(
        body,
        grid_spec=pltpu.PrefetchScalarGridSpec(
            num_scalar_prefetch=1, grid=(NSEQ, NBLK),
            in_specs=[pl.BlockSpec((SWA_BLK, 512), lambda s, n, sk: (s * NBLK + n, 2)),
                      pl.BlockSpec((L, 256), lambda s, n, sk: (s, 6)),
                      pl.BlockSpec((3, L, 128), lambda s, n, sk: (0, 0, 0))],
            out_specs=pl.BlockSpec((SWA_BLK, 512), lambda s, n, sk: (s * NBLK + n, 0)),
            scratch_shapes=[pltpu.VMEM((2, L + 2 * SWA_BLK, 256), F32), pltpu.VMEM((2, L + 2 * SWA_BLK, 256), F32)]),
        out_shape=_sds((N, 512), MX), name="swa_fwd", compiler_params=_cp(("arbitrary", "arbitrary")))(sink, h, h, tk)


def _swa_bwd(h, tk, sink, dyc):
    def body(sink_ref, q_ref, kv_ref, tk_ref, dy_ref, dq_ref, dkv_ref, dsink_ref, kexp_all, vexp_all, dkacc, dvacc):
        sq = pl.program_id(0)
        n = pl.program_id(1)

        @pl.when(n == 0)
        def _():
            _swa_pad_kv(kv_ref, tk_ref, kexp_all, vexp_all)
            dkacc[...] = jnp.zeros_like(dkacc)
            dvacc[...] = jnp.zeros_like(dvacc)

        @pl.when((n == 0) & (sq == 0))
        def _():
            dsink_ref[...] = jnp.zeros_like(dsink_ref)

        r0 = pl.multiple_of(n * SWA_BLK, SWA_BLK)
        tq = _swa_qtab(tk_ref, r0)
        q = _rope(q_ref[...], tq)
        hrow = lax.broadcasted_iota(jnp.int32, (8, 128), 0)
        dsk = jnp.zeros((8, 128), F32)
        for hk in range(2):
            kexp = kexp_all[hk, pl.ds(r0, 3 * SWA_BLK), :]
            vexp = vexp_all[hk, pl.ds(r0, 3 * SWA_BLK), :]
            qs, p, ps, slot, rowg = _swa_probs(q[:, hk * 256:(hk + 1) * 256], kexp, n, sink_ref, hk)
            dy2 = dy_ref[:, hk * 256:(hk + 1) * 256]
            dos = jnp.concatenate([jnp.where(slot == g, dy2, 0.0) for g in range(4)], axis=0)
            dp = _mm_nt(dos, vexp)
            delta = jnp.sum(p * dp, axis=-1, keepdims=True)
            ds = p * (dp - delta) * 0.125
            dsr = -ps * delta
            for g in range(4):
                dsk = dsk + jnp.where(hrow == hk * 4 + g, jnp.sum(jnp.where(rowg == g, dsr, 0.0), axis=0, keepdims=True), 0.0)
            dq4 = _mm(ds, kexp)
            dq2 = jnp.zeros((SWA_BLK, 256), F32)
            for g in range(4):
                dq2 = dq2 + jnp.where(slot == g, dq4[g * SWA_BLK:(g + 1) * SWA_BLK], 0.0)
            dq_ref[:, hk * 256:(hk + 1) * 256] = _rope_t(dq2, tq).astype(MX)
            dkacc[hk, pl.ds(r0, 3 * SWA_BLK), :] += _mm_tn(ds, qs)
            dvacc[hk, pl.ds(r0, 3 * SWA_BLK), :] += _mm_tn(p, dos)
        dsink_ref[...] += dsk

        @pl.when(n == NBLK - 1)
        def _():
            seq = slice(SWA_BLK, SWA_BLK + L)
            dk = _rope_t(_swa_fold(dkacc[0, seq], 0) + _swa_fold(dkacc[1, seq], 1), tk_ref[...])
            dkv_ref[:, 0:128] = dk.astype(MX)
            dkv_ref[:, 128:256] = (_swa_fold(dvacc[0, seq], 0) + _swa_fold(dvacc[1, seq], 1)).astype(MX)

    blk = lambda col: pl.BlockSpec((SWA_BLK, 512), lambda s, n, sk: (s * NBLK + n, col))
    pad = pltpu.VMEM((2, L + 2 * SWA_BLK, 256), F32)
    return pl.pallas_call(
        body,
        grid_spec=pltpu.PrefetchScalarGridSpec(
            num_scalar_prefetch=1, grid=(NSEQ, NBLK),
            in_specs=[blk(2), pl.BlockSpec((L, 256), lambda s, n, sk: (s, 6)),
                      pl.BlockSpec((3, L, 128), lambda s, n, sk: (0, 0, 0)), blk(0)],
            out_specs=[blk(0), pl.BlockSpec((L, 256), lambda s, n, sk: (s, 0)),
                       pl.BlockSpec((8, 128), lambda s, n, sk: (0, 0))],
            scratch_shapes=[pad, pad, pad, pad]),
        out_shape=[_sds((N, 512), MX), _sds((N, 256), MX), _sds((8, 128))],
        name="swa_bwd", compiler_params=_cp(("arbitrary", "arbitrary")))(sink, h, h, tk, dyc)


def _outproj_fwd(ya, yb, yc, x, wo, g, b):
    tm = 512

    def body(ya_ref, yb_ref, yc_ref, x_ref, wo_ref, g_ref, b_ref, s_ref, x1_ref):
        mix = _mm(ya_ref[...], wo_ref[0:256]) + _mm(yb_ref[...], wo_ref[256:512]) + _mm(yc_ref[...], wo_ref[512:1024])
        s = ALPHA * x_ref[...] + mix
        s_ref[...] = s
        x1_ref[...] = _ln_fwd(s, g_ref[...], b_ref[...])

    row = lambda w_: pl.BlockSpec((tm, w_), lambda i: (i, 0))
    one = pl.BlockSpec((1, D), lambda i: (0, 0))
    return pl.pallas_call(
        body, grid=(N // tm,),
        in_specs=[row(256), row(256), row(512), row(D), pl.BlockSpec((D, D), lambda i: (0, 0)), one, one],
        out_specs=[row(D), row(D)], out_shape=[_sds((N, D)), _sds((N, D))],
        name="outproj_fwd", compiler_params=_cp(("parallel",)))(ya, yb, yc, x, wo, g, b)


def _outproj_bwd(dx1, s1, ya, yb, yc, wo, g):
    tm = 512
    nt = N // tm

    def body(dx1_ref, s_ref, ya_ref, yb_ref, yc_ref, wo_ref, g_ref,
             dya_ref, dyb_ref, dyc_ref, dxp_ref, dwo_ref, dg_ref, db_ref, acc):
        i = pl.program_id(0)

        @pl.when(i == 0)
        def _():
            acc[...] = jnp.zeros_like(acc)
            dg_ref[...] = jnp.zeros_like(dg_ref)
            db_ref[...] = jnp.zeros_like(db_ref)

        ds, dg, db = _ln_bwd(dx1_ref[...], s_ref[...], g_ref[...])
        dg_ref[...] += dg
        db_ref[...] += db
        dxp_ref[...] = ALPHA * ds
        dy = _mm_nt(ds, wo_ref[...])
        dya_ref[...] = dy[:, 0:256]
        dyb_ref[...] = dy[:, 256:512]
        dyc_ref[...] = dy[:, 512:1024]
        acc[0:256] += _mm_tn(ya_ref[...], ds)
        acc[256:512] += _mm_tn(yb_ref[...], ds)
        acc[512:1024] += _mm_tn(yc_ref[...], ds)

        @pl.when(i == nt - 1)
        def _():
            dwo_ref[...] = acc[...].astype(MX)

    row = lambda w_: pl.BlockSpec((tm, w_), lambda i: (i, 0))
    one = pl.BlockSpec((1, D), lambda i: (0, 0))
    full = pl.BlockSpec((D, D), lambda i: (0, 0))
    return pl.pallas_call(
        body, grid=(nt,),
        in_specs=[row(D), row(D), row(256), row(256), row(512), full, one],
        out_specs=[row(256), row(256), row(512), row(D), full, one, one],
        out_shape=[_sds((N, 256)), _sds((N, 256)), _sds((N, 512)), _sds((N, D)), _sds((D, D), MX), _sds((1, D)), _sds((1, D))],
        scratch_shapes=[pltpu.VMEM((D, D), F32)],
        name="outproj_bwd", compiler_params=_cp(("arbitrary",)))(dx1, s1, ya, yb, yc, wo, g)


def _ffn_fwd(x1, w1, w2, g, b):
    tm = FFN_TM

    def body(x_ref, w1_ref, w2_ref, g_ref, b_ref, a_ref, s_ref, x2_ref):
        x = x_ref[...]
        xb = x.astype(MX)
        s = ALPHA * x
        for j in range(NSHARD):
            a = _mm(xb, w1_ref[j])
            a_ref[:, j * D:(j + 1) * D] = a.astype(MX)
            s = s + _mm(jnp.square(jnp.maximum(a, 0.0)), w2_ref[j])
        s_ref[...] = s
        x2_ref[...] = _ln_fwd(s, g_ref[...], b_ref[...])

    row = pl.BlockSpec((tm, D), lambda i: (i, 0))
    wall = pl.BlockSpec((NSHARD, D, D), lambda i: (0, 0, 0))
    one = pl.BlockSpec((1, D), lambda i: (0, 0))
    return pl.pallas_call(
        body, grid=(N // tm,),
        in_specs=[row, wall, wall, one, one],
        out_specs=[pl.BlockSpec((tm, DFF), lambda i: (i, 0)), row, row],
        out_shape=[_sds((N, DFF), MX), _sds((N, D)), _sds((N, D))],
        name="ffn_fwd", compiler_params=_cp(("parallel",), FFN_VMEM))(x1, w1, w2, g, b)


def _ffn_bwd_act(dy, s2, a, w1, w2, g):
    tm = FFN_TM

    def body(dy_ref, s_ref, a_ref, w1_ref, w2_ref, g_ref, da_ref, ds_ref, dx1_ref, dg_ref, db_ref):
        @pl.when(pl.program_id(0) == 0)
        def _():
            dg_ref[...] = jnp.zeros_like(dg_ref)
            db_ref[...] = jnp.zeros_like(db_ref)

        ds, dg, db = _ln_bwd(dy_ref[...], s_ref[...], g_ref[...])
        dsb = ds.astype(MX)
        ds_ref[...] = dsb
        dg_ref[...] += dg
        db_ref[...] += db
        dx1 = ALPHA * ds
        for j in range(NSHARD):
            da = (_mm_nt(dsb, w2_ref[j]) * 2.0 * jnp.maximum(a_ref[:, j * D:(j + 1) * D].astype(F32), 0.0)).astype(MX)
            da_ref[:, j * D:(j + 1) * D] = da
            dx1 = dx1 + _mm_nt(da, w1_ref[j])
        dx1_ref[...] = dx1

    row = pl.BlockSpec((tm, D), lambda i: (i, 0))
    wide = pl.BlockSpec((tm, DFF), lambda i: (i, 0))
    wall = pl.BlockSpec((NSHARD, D, D), lambda i: (0, 0, 0))
    one = pl.BlockSpec((1, D), lambda i: (0, 0))
    return pl.pallas_call(
        body, grid=(N // tm,),
        in_specs=[row, row, wide, wall, wall, one],
        out_specs=[wide, row, row, one, one],
        out_shape=[_sds((N, DFF), MX), _sds((N, D), MX), _sds((N, D)), _sds((1, D)), _sds((1, D))],
        name="ffn_bwd_act", compiler_params=_cp(("arbitrary",), FFN_VMEM))(dy, s2, a, w1, w2, g)


def _ffn_bwd_w(x1, da, a, ds):
    tm, nb = FFN_TM_W, FFN_WB
    nt = N // tm

    def body(x_ref, da_ref, a_ref, ds_ref, dw1_ref, dw2_ref, acc1, acc2):
        i = pl.program_id(1)

        @pl.when(i == 0)
        def _():
            acc1[...] = jnp.zeros_like(acc1)
            acc2[...] = jnp.zeros_like(acc2)

        x, ds_ = x_ref[...], ds_ref[...]
        for k in range(nb):
            cols = slice(k * D, (k + 1) * D)
            acc1[k] += _mm_tn(x, da_ref[:, cols])
            acc2[k] += _mm_tn(jnp.square(jnp.maximum(a_ref[:, cols].astype(F32), 0.0)), ds_)

        @pl.when(i == nt - 1)
        def _():
            dw1_ref[...] = acc1[...].astype(MX)
            dw2_ref[...] = acc2[...].astype(MX)

    row = pl.BlockSpec((tm, D), lambda j, i: (i, 0))
    col = pl.BlockSpec((tm, nb * D), lambda j, i: (i, j))
    wj = pl.BlockSpec((nb, D, D), lambda j, i: (j, 0, 0))
    return pl.pallas_call(
        body, grid=(NSHARD // nb, nt),
        in_specs=[row, col, col, row], out_specs=[wj, wj],
        out_shape=[_sds((NSHARD, D, D), MX), _sds((NSHARD, D, D), MX)],
        scratch_shapes=[pltpu.VMEM((nb, D, D), F32), pltpu.VMEM((nb, D, D), F32)],
        name="ffn_bwd_w", compiler_params=_cp(("parallel", "arbitrary"), FFN_VMEM))(x1, da, a, ds)


def _loss_head(y, target):
    tm = 512

    def body(y_ref, t_ref, dy_ref, l_ref):
        @pl.when(pl.program_id(0) == 0)
        def _():
            l_ref[...] = jnp.zeros_like(l_ref)

        e = y_ref[...] - t_ref[...]
        dy_ref[...] = e * (1.0 / D)
        l_ref[...] += jnp.sum(jnp.sum(e * e, axis=1, keepdims=True), axis=0, keepdims=True) * (0.5 / D)

    row = pl.BlockSpec((tm, D), lambda i: (i, 0))
    return pl.pallas_call(
        body, grid=(N // tm,), in_specs=[row, row],
        out_specs=[row, pl.BlockSpec((8, 128), lambda i: (0, 0))],
        out_shape=[_sds((N, D)), _sds((8, 128))], name="loss_head", compiler_params=_cp(("arbitrary",)))(y, target)


def _s5_discretize(a_re, a_im, log_step, b_re, b_im):
    lam = lax.complex(a_re, a_im)
    lam_bar = jnp.exp(lam * jnp.exp(log_step))
    b_bar = ((lam_bar - 1.0) / lam)[..., None] * lax.complex(b_re, b_im)
    return jnp.real(lam_bar), jnp.imag(lam_bar), jnp.real(b_bar), jnp.imag(b_bar)


def _s5_in_blocks(b):
    e = jnp.eye(8, dtype=F32)
    return jnp.einsum('ij,zbjph->zbihjp', e, b.reshape(2, 2, 8, S5_P, S5_H)).reshape(2, 2, 128, SW)


def _s5_in_unblocks(d):
    return jnp.einsum('zbihip->zbiph', d.reshape(2, 2, 8, S5_H, 8, S5_P)).reshape(2, S5_G, S5_P, S5_H)


def _s5_out_blocks(c):
    e = jnp.eye(8, dtype=F32)
    return jnp.einsum('ij,zbjhp->zbjpih', e, c.reshape(2, 2, 8, S5_H, S5_P)).reshape(2, 2, SW, 128)


def _s5_out_unblocks(d):
    return jnp.einsum('zbipih->zbihp', d.reshape(2, 2, 8, S5_P, 8, S5_H)).reshape(2, S5_G, S5_H, S5_P)


def _gate_weight(w_a):
    z = jnp.zeros((16, 128), F32)
    top = jnp.concatenate([w_a[0], z], axis=1)
    bot = jnp.concatenate([z, w_a[1]], axis=1)
    return jnp.concatenate([top, bot, jnp.zeros((96, 256), F32)], axis=0)


def _layer_prep(p):
    lr, li, br, bi = _s5_discretize(p["s5_a_re"], p["s5_a_im"], p["s5_log_step"], p["s5_b_re"], p["s5_b_im"])
    q = dict(p)
    q["bre"] = _s5_in_blocks(br).astype(MX)
    q["bim"] = _s5_in_blocks(bi).astype(MX)
    q["cre"] = _s5_out_blocks(p["s5_c_re"]).astype(MX)
    q["cim"] = _s5_out_blocks(p["s5_c_im"]).astype(MX)
    mr, mi = lr.reshape(2, 1024), li.reshape(2, 1024)
    both = lambda t0, t1: tuple(jnp.stack(p) for p in zip(t0, t1))
    q["tab"] = both(_lockstep_tables(mr[0], mi[0], False), _lockstep_tables(mr[1], mi[1], True))
    q["tabc"] = both(_lockstep_tables(mr[0], -mi[0], True), _lockstep_tables(mr[1], -mi[1], False))
    q["dsk"] = p["s5_d"].reshape(1, 256)
    q["wa"] = _gate_weight(p["gla_w_a"]).astype(MX)
    q["ba"] = p["gla_b_a"].reshape(1, 256)
    q["lng"] = p["gla_ln_g"].reshape(1, 256)
    q["bv"] = p["s5_b_glu"][:256].reshape(1, 256)
    q["bg"] = p["s5_b_glu"][256:].reshape(1, 256)
    for k in ("ln1_g", "ln1_b", "ln2_g", "ln2_b"):
        q[k] = p[k].reshape(1, D)
    return q


def _layer_fwd(x, q, tk, fetch):
    q["w_in"] = fetch("w_in", x)
    h = _inproj_fwd(x, q["w_in"])
    hre, him, y2 = _s5_fwd(h, q["bre"], q["bim"], q["cre"], q["cim"], q["tab"])
    q["w4"] = fetch("s5_w_glu", y2)
    ya = _s5_glu_fwd(y2, h, q["dsk"], q["w4"], q["bv"], q["bg"])
    la2 = _gla_gate_fwd(h, q["wa"], q["ba"])
    of, ob, sf, sb = _gla_fwd(h, la2)
    yb = _gla_post_fwd(of, ob, h, q["lng"])
    yc = _swa_fwd(h, tk, q["swa_sink"])
    q["w_out"] = fetch("w_out", yc)
    s1, x1 = _outproj_fwd(ya, yb, yc, x, q["w_out"], q["ln1_g"], q["ln1_b"])
    q["w_ff1"] = fetch("w_ff1", x1)
    q["w_ff2"] = fetch("w_ff2", x1)
    a, s2, x2 = _ffn_fwd(x1, q["w_ff1"], q["w_ff2"], q["ln2_g"], q["ln2_b"])
    saved = dict(x=x, h=h, hre=hre, him=him, y2=y2, ya=ya, la2=la2, of=of, ob=ob, sf=sf, sb=sb, yb=yb, yc=yc,
                 s1=s1, x1=x1, a=a, s2=s2)
    return x2, saved


def _layer_bwd(dy, q, sv, tk, emit):
    g = {}
    da, ds2, dx1, g["dg2"], g["db2"] = _ffn_bwd_act(dy, sv["s2"], sv["a"], q["w_ff1"], q["w_ff2"], q["ln2_g"])
    dw1, dw2 = _ffn_bwd_w(sv["x1"], da, sv["a"], ds2)
    tie = emit(dict(w_ff1=dw1, w_ff2=dw2))
    dya, dyb, dyc, dxp, dwo, g["dg1"], g["db1"] = _outproj_bwd(dx1, sv["s1"], sv["ya"], sv["yb"], sv["yc"],
                                                               q["w_out"], q["ln1_g"] + tie)
    h = sv["h"]
    daq, dakv, g["dsink"] = _swa_bwd(h, tk, q["swa_sink"], dyc)
    do, gr, g["dlng"] = _gla_post_bwd(sv["of"], sv["ob"], h, q["lng"], dyb)
    gq_f, gk_f, gv_f, gl_f, gq_b, gk_b, gv_b, gl_b = _gla_bwd(h, sv["la2"], do, sv["sf"], sv["sb"])
    dhl, g["dwa"], g["dba"] = _gla_gate_bwd(h, q["wa"], q["ba"], gl_f, gl_b)
    dyp, dud, g["dd"], dw4, g["dbv"], g["dbg"] = _s5_glu_bwd(sv["y2"], h, q["dsk"], q["w4"], q["bv"], q["bg"], dya)
    tie = emit(dict(w_out=dwo.reshape(NSHARD, D // NSHARD, D), s5_w_glu=dw4))
    du2, g["dbre"], g["dbim"], g["dcre"], g["dcim"], g["dmu"] = _s5_bwd(
        h, dyp, sv["hre"], sv["him"], q["bre"], q["bim"], q["cre"], q["cim"], (q["tabc"][0], q["tabc"][1] + tie))
    dx, dwt = _inproj_bwd(sv["x"], q["w_in"], dxp, du2, dud, gq_f, gq_b, gk_f, gk_b, gv_f, gv_b, gr, daq, dakv, dhl)
    tie = emit(dict(w_in=dwt))
    return dx, g, tie


NATIVE = ("dmu", "dbre", "dbim", "dcre", "dcim", "dd", "dbv", "dbg", "dwa", "dba", "dlng", "dsink",
          "dg1", "db1", "dg2", "db2", "loss")
ICI_CORE = (0, 0, 0, 1, 1, 0, 0, 0, 1, 1, 1, 1, 0, 0, 1, 1, 0)


def _finish_small(n, w):
    g = {}
    dmu = n["dmu"]
    dlr = dmu[:, :, :, 0].reshape(DEPTH, 2, S5_G, S5_P)
    dli = dmu[:, :, :, 1].reshape(DEPTH, 2, S5_G, S5_P)

    def unblock(c, perm, shape):
        return c.reshape(DEPTH, 2, 2, S5_H, 8, S5_P).transpose(perm).reshape(shape)

    b_shape, c_shape = (DEPTH, 2, S5_G, S5_P, S5_H), (DEPTH, 2, S5_G, S5_H, S5_P)
    _, vjp = jax.vjp(_s5_discretize, w["s5_a_re"], w["s5_a_im"], w["s5_log_step"], w["s5_b_re"], w["s5_b_im"])
    (g["s5_a_re"], g["s5_a_im"], g["s5_log_step"], g["s5_b_re"], g["s5_b_im"]) = vjp(
        (dlr, dli, unblock(n["dbre"], (0, 1, 2, 4, 5, 3), b_shape), unblock(n["dbim"], (0, 1, 2, 4, 5, 3), b_shape)))
    g["s5_c_re"] = unblock(n["dcre"], (0, 1, 2, 4, 3, 5), c_shape)
    g["s5_c_im"] = unblock(n["dcim"], (0, 1, 2, 4, 3, 5), c_shape)
    g["s5_d"] = n["dd"].reshape(DEPTH, S5_G, S5_H)
    g["s5_b_glu"] = jnp.concatenate([n["dbv"], n["dbg"]], axis=2).reshape(DEPTH, 512)
    g["gla_w_a"] = jnp.stack([n["dwa"][:, 0:16, 0:128], n["dwa"][:, 16:32, 128:256]], axis=1)
    g["gla_b_a"] = n["dba"].reshape(DEPTH, 2, 128)
    g["gla_ln_g"] = n["dlng"].reshape(DEPTH, 256)
    g["swa_sink"] = n["dsink"][:, :, 0]
    for k, s in (("ln1_g", "dg1"), ("ln1_b", "db1"), ("ln2_g", "dg2"), ("ln2_b", "db2")):
        g[k] = n[s].reshape(DEPTH, D)
    return g


def _local_step(x, target, qs, tk, fetch, emit):
    saved = []
    for l, q in enumerate(qs):
        x, sv = _layer_fwd(x, q, tk, functools.partial(fetch, l))
        saved.append(sv)
    dy, lacc = _loss_head(x, target)
    smalls = [None] * DEPTH
    tie = 0.0
    for l in reversed(range(DEPTH)):
        qs[l]["ln2_g"] = qs[l]["ln2_g"] + tie
        dy, smalls[l], tie = _layer_bwd(dy, qs[l], saved[l], tk, functools.partial(emit, l))
    smalls[0]["db2"] = smalls[0]["db2"] + tie
    for l in range(DEPTH):
        smalls[l]["loss"] = lacc if l == 0 else jnp.zeros_like(lacc)
    return lacc[0, 0], dy, smalls


BIG = ("w_in", "s5_w_glu", "w_out", "w_ff1", "w_ff2")
SMALL = ("s5_a_re", "s5_a_im", "s5_log_step", "s5_b_re", "s5_b_im", "s5_c_re", "s5_c_im", "s5_d", "s5_b_glu",
         "gla_w_a", "gla_b_a", "gla_ln_g", "swa_sink", "ln1_g", "ln1_b", "ln2_g", "ln2_b")
ANY = pl.BlockSpec(memory_space=pl.ANY)


def _place():
    x, y, c = lax.axis_index("x"), lax.axis_index("y"), lax.axis_index("c")
    return x, y, c, [(1 - x, y), (x, 1 - y), (1 - x, 1 - y)]


HBM = pl.BlockSpec(memory_space=pltpu.HBM)
SEMS = pl.BlockSpec(memory_space=pltpu.SEMAPHORE)
EFFECT = pltpu.SideEffectType.DATAFLOW_SIDE_EFFECTING


def _push_copies(ins, lands, send, recv, gather, sending):
    x, y, c, chips = _place()
    me = 2 * x + y
    out = []
    for a in range(len(lands)):
        for j, (px, py) in enumerate(chips):
            peer = 2 * px + py
            src = lands[a].at[me] if gather else ins[a].at[peer if sending else me]
            dst = lands[a].at[me if sending else peer]
            out.append(pltpu.make_async_remote_copy(src_ref=src, dst_ref=dst, send_sem=send.at[3 * a + j],
                                                    recv_sem=recv.at[3 * a + j], device_id=(px, py, c),
                                                    device_id_type=MESH))
    return out


def _push_start(name, arrs, gather):
    n = len(arrs)
    ops = list(arrs) if gather else list(arrs) + [lax.empty(s.shape, s.dtype) for s in arrs]
    m = len(ops)

    def body(*refs):
        ins, lnd = (refs[:n], refs[:n]) if gather else (refs[:n], refs[n:m])
        for cp in _push_copies(ins, lnd, refs[m], refs[m + 1], gather, True):
            cp.start()
        refs[-1][...] = jnp.zeros((8, 128), F32)

    ops = [pltpu.with_memory_space_constraint(t, pltpu.HBM) for t in ops]
    res = pl.pallas_call(
        body, name=name,
        out_shape=(pltpu.SemaphoreType.DMA((3 * n,)), pltpu.SemaphoreType.DMA((3 * n,)),
                   *[pltpu.HBM(t.shape, t.dtype) for t in ops], _sds((8, 128))),
        in_specs=[HBM] * m,
        out_specs=(SEMS, SEMS, *[HBM] * m, pl.BlockSpec(memory_space=pltpu.VMEM)),
        input_output_aliases={i: 2 + i for i in range(m)},
        compiler_params=pltpu.CompilerParams(has_side_effects=EFFECT))(*ops)
    return res[0], res[1], list(res[2:2 + m]), res[-1]


def _push_wait(name, started, after, gather):
    send, recv, ops, _ = started
    m = len(ops)
    n = m if gather else m // 2

    def body(*refs):
        ins, lnd = (refs[:n], refs[:n]) if gather else (refs[:n], refs[n:m])
        for cp in _push_copies(ins, lnd, refs[m], refs[m + 1], gather, False):
            cp.wait_send()
            cp.wait_recv()

    res = pl.pallas_call(
        body, name=name,
        out_shape=[pltpu.HBM(t.shape, t.dtype) for t in ops],
        in_specs=[HBM] * m + [SEMS, SEMS, ANY], out_specs=[HBM] * m,
        input_output_aliases={i: i for i in range(m)},
        compiler_params=pltpu.CompilerParams(has_side_effects=EFFECT))(*ops, send, recv, after)
    return list(res)


def _row_tile(rows):
    return max(t for t in range(8, min(rows, 512) + 1, 8) if rows % t == 0)


def _cast_to_slot(me, w, l):
    _, rows, cols = w.shape
    tr = _row_tile(rows)

    def body(me_ref, w_ref, o_ref):
        o_ref[0] = w_ref[0].astype(MX)

    return pl.pallas_call(
        body,
        grid_spec=pltpu.PrefetchScalarGridSpec(
            num_scalar_prefetch=1, grid=(rows // tr,),
            in_specs=[pl.BlockSpec((1, tr, cols), lambda i, me_: (l, i, 0))],
            out_specs=pl.BlockSpec((1, tr, cols), lambda i, me_: (me_[0], i, 0))),
        out_shape=_sds((NSHARD, rows, cols), MX), name="cast_to_slot", compiler_params=_cp(("arbitrary",)))(me, w)


def _sum_sources(me, recv, own):
    _, rows, cols = recv[0].shape
    tr = min(_row_tile(rows), 256) if rows % 256 == 0 else _row_tile(rows)
    nt = rows // tr

    def body(me_ref, *refs):
        o_ref = refs[-1]
        for l in range(DEPTH):
            @pl.when(pl.program_id(0) == l)
            def _():
                r_ref, own_ref = refs[2 * l], refs[2 * l + 1]
                part = [jnp.where(me_ref[0] == s, own_ref[0], r_ref[s]).astype(F32) for s in range(NSHARD)]
                o_ref[...] = ((part[0] + part[1]) + part[2]) + part[3]

    in_specs = []
    for l in range(DEPTH):
        pick = lambda g, i, me_, l=l: jnp.where(g == l, i, jnp.where(g < l, 0, nt - 1))
        in_specs += [pl.BlockSpec((NSHARD, tr, cols), lambda g, i, me_, pick=pick: (0, pick(g, i, me_), 0)),
                     pl.BlockSpec((1, tr, cols), lambda g, i, me_, pick=pick: (me_[0], pick(g, i, me_), 0))]
    return pl.pallas_call(
        body,
        grid_spec=pltpu.PrefetchScalarGridSpec(
            num_scalar_prefetch=1, grid=(DEPTH, nt), in_specs=in_specs,
            out_specs=pl.BlockSpec((tr, cols), lambda g, i, me_: (g * nt + i, 0))),
        out_shape=_sds((DEPTH * rows, cols)), name="sum_sources",
        compiler_params=_cp(("arbitrary", "arbitrary")))(me, *[t for l in range(DEPTH) for t in (recv[l], own[l])])


def _swap_sibling(arrs):
    n = len(arrs)

    def body(*refs):
        ins, outs = refs[:n], refs[n:2 * n]
        send, recv = refs[2 * n:]
        x, y, c, _ = _place()
        cps = [pltpu.make_async_remote_copy(src_ref=ins[a], dst_ref=outs[a], send_sem=send.at[a], recv_sem=recv.at[a],
                                            device_id=(x, y, 1 - c), device_id_type=MESH) for a in range(n)]
        for cp in cps:
            cp.start()
        for cp in cps:
            cp.wait()

    return pl.pallas_call(
        body, in_specs=[ANY] * n, out_specs=[ANY] * n, out_shape=[_sds(a.shape, a.dtype) for a in arrs],
        scratch_shapes=[pltpu.SemaphoreType.DMA((n,)), pltpu.SemaphoreType.DMA((n,))],
        name="swap_sibling")(*arrs)


def _allreduce_small(per_layer):
    nk = len(per_layer[0])
    n = DEPTH * nk
    shapes = [a.shape for a in per_layer[0]]

    def body(*refs):
        ins, outs = refs[:n], refs[n:n + nk]
        sibs, slots = refs[n + nk:n + 2 * nk], refs[n + 2 * nk:n + 3 * nk]
        send, recv = refs[n + 3 * nk:]
        x, y, c, chips = _place()
        me = 2 * x + y
        d2d = [pltpu.make_async_remote_copy(src_ref=ins[l * nk + k], dst_ref=sibs[k].at[l], send_sem=send.at[l * nk + k],
                                            recv_sem=recv.at[l * nk + k], device_id=(x, y, 1 - c), device_id_type=MESH)
               for l in range(DEPTH) for k in range(nk)]
        for cp in d2d:
            cp.start()
        for cp in d2d:
            cp.wait()
        for l in range(DEPTH):
            for k in range(nk):
                slots[k][0, l] = ins[l * nk + k][...] + sibs[k][l]

        def swap(k, stage):
            peer = (1 - x, y, c) if stage == 0 else (x, 1 - y, c)
            return pltpu.make_async_remote_copy(src_ref=slots[k].at[2 * stage], dst_ref=slots[k].at[2 * stage + 1],
                                                send_sem=send.at[n + 3 * k + stage], recv_sem=recv.at[n + 3 * k + stage],
                                                device_id=peer, device_id_type=MESH)

        def handover(k):
            return pltpu.make_async_remote_copy(src_ref=outs[k], dst_ref=outs[k], send_sem=send.at[n + 3 * nk + k],
                                                recv_sem=recv.at[n + 3 * nk + k], device_id=(x, y, 1 - c),
                                                device_id_type=MESH)

        halves = (tuple(k for k in range(nk) if ICI_CORE[k] == 0), tuple(k for k in range(nk) if ICI_CORE[k] == 1))
        for cc in range(2):
            @pl.when(c == cc)
            def _():
                mine, theirs = halves[cc], halves[1 - cc]
                for stage in range(2):
                    cps = [swap(k, stage) for k in mine]
                    for cp in cps:
                        cp.start()
                    for cp in cps:
                        cp.wait()
                    for k in mine:
                        if stage == 0:
                            slots[k][2] = slots[k][0] + slots[k][1]
                        else:
                            outs[k][...] = slots[k][2] + slots[k][3]
                over = [handover(k) for k in mine]
                for cp in over:
                    cp.start()
                for k in theirs:
                    handover(k).wait_recv()
                for cp in over:
                    cp.wait_send()

    vm = pl.BlockSpec(memory_space=pltpu.VMEM)
    return pl.pallas_call(
        body, in_specs=[vm] * n, out_specs=[vm] * nk, out_shape=[_sds((DEPTH,) + s) for s in shapes],
        scratch_shapes=([pltpu.VMEM((DEPTH,) + s, F32) for s in shapes]
                        + [pltpu.VMEM((NSHARD, DEPTH) + s, F32) for s in shapes]
                        + [pltpu.SemaphoreType.DMA((n + 4 * nk,)), pltpu.SemaphoreType.DMA((n + 4 * nk,))]),
        name="allreduce_small", compiler_params=pltpu.CompilerParams(vmem_limit_bytes=VMEM_LIMIT))(
            *[a for layer in per_layer for a in layer])


def _adamw_math(w, g, m, v):
    m = ADAM_B1 * m + (1.0 - ADAM_B1) * g
    v = ADAM_B2 * v + (1.0 - ADAM_B2) * jnp.square(g)
    m_hat = m / (1.0 - ADAM_B1 ** ADAM_STEP)
    v_hat = v / (1.0 - ADAM_B2 ** ADAM_STEP)
    delta = -ADAM_LR * (m_hat / (jnp.sqrt(v_hat) + ADAM_EPS) + ADAM_WD * w)
    return delta, m, v


def _adamw(g_parts, w, m, v):
    rows, cols = w.shape
    tr = 256 if rows % 256 == 0 else _row_tile(rows)
    k = len(g_parts)

    def body(*refs):
        g = refs[0][...]
        for r in refs[1:k]:
            g = g + r[...]
        w_ref, m_ref, v_ref, go, do, mo, vo = refs[k:]
        d, mn, vn = _adamw_math(w_ref[...], g, m_ref[...], v_ref[...])
        go[...] = g
        do[...] = d
        mo[...] = mn
        vo[...] = vn

    spec = pl.BlockSpec((tr, cols), lambda i: (i, 0))
    return pl.pallas_call(
        body, grid=(rows // tr,), in_specs=[spec] * (k + 3), out_specs=[spec] * 4,
        out_shape=[_sds((rows, cols))] * 4, name="adamw", compiler_params=_cp(("parallel",)))(*g_parts, w, m, v)


def _adamw_small(gs, ws, ms, vs):
    n = len(gs)

    def body(*refs):
        for k in range(n):
            d, mn, vn = _adamw_math(refs[n + k][...], refs[k][...], refs[2 * n + k][...], refs[3 * n + k][...])
            refs[4 * n + k][...] = d
            refs[5 * n + k][...] = mn
            refs[6 * n + k][...] = vn

    vm = pl.BlockSpec(memory_space=pltpu.VMEM)
    shapes = [_sds(a.shape) for a in ws]
    res = pl.pallas_call(
        body, in_specs=[vm] * (4 * n), out_specs=[vm] * (3 * n), out_shape=shapes * 3, name="adamw_small",
        compiler_params=pltpu.CompilerParams(vmem_limit_bytes=VMEM_LIMIT))(*gs, *ws, *ms, *vs)
    return res[:n], res[n:2 * n], res[2 * n:]


_ARGS = ("x", "w_in", "s5_a_re", "s5_a_im", "s5_log_step", "s5_b_re", "s5_b_im", "s5_c_re", "s5_c_im", "s5_d",
         "s5_w_glu", "s5_b_glu", "gla_w_a", "gla_b_a", "gla_ln_g", "swa_sink", "w_out", "ln1_g", "ln1_b", "w_ff1",
         "w_ff2", "ln2_g", "ln2_b")
_WEIGHTS = _ARGS[1:]


def _shard_cols(d):
    return d.reshape(d.shape[0], NSHARD, d.shape[1] // NSHARD).transpose(1, 0, 2)


def kernel(x, w_in, s5_a_re, s5_a_im, s5_log_step, s5_b_re, s5_b_im, s5_c_re, s5_c_im, s5_d, s5_w_glu, s5_b_glu, gla_w_a, gla_b_a, gla_ln_g, swa_sink, w_out, ln1_g, ln1_b, w_ff1, w_ff2, ln2_g, ln2_b, loss_target, m_w_in, m_s5_a_re, m_s5_a_im, m_s5_log_step, m_s5_b_re, m_s5_b_im, m_s5_c_re, m_s5_c_im, m_s5_d, m_s5_w_glu, m_s5_b_glu, m_gla_w_a, m_gla_b_a, m_gla_ln_g, m_swa_sink, m_w_out, m_ln1_g, m_ln1_b, m_w_ff1, m_w_ff2, m_ln2_g, m_ln2_b, v_w_in, v_s5_a_re, v_s5_a_im, v_s5_log_step, v_s5_b_re, v_s5_b_im, v_s5_c_re, v_s5_c_im, v_s5_d, v_s5_w_glu, v_s5_b_glu, v_gla_w_a, v_gla_b_a, v_gla_ln_g, v_swa_sink, v_w_out, v_ln1_g, v_ln1_b, v_w_ff1, v_w_ff2, v_ln2_g, v_ln2_b):
    given = dict(locals())
    w = {k: given[k] for k in _WEIGHTS}
    mom = {k: given["m_" + k] for k in _WEIGHTS}
    var = {k: given["v_" + k] for k in _WEIGHTS}

    me = (2 * lax.axis_index("x") + lax.axis_index("y")).astype(jnp.int32).reshape(1)
    tr = lambda t: t.transpose(0, 2, 1)
    shard = {k: (tr(w[k]) if k == "w_in" else w[k]) for k in BIG}
    qs = [None] * DEPTH

    first = ("w_in", "s5_w_glu", "w_out")
    follow = {(0, "w_in"): [(0, BIG[3:]), (1, first)], (0, "w_ff1"): [(1, BIG[3:])]}
    gathers = {}

    def start_gather(l, names, behind=None):
        lands = [_cast_to_slot(me, shard[k], l) for k in names]
        if behind is not None:
            lands, behind = lax.optimization_barrier((lands, behind))
        st = _push_start(f"gather_start_{l}_{names[0]}", lands, True)
        for k in names:
            gathers[l, k] = [names, st, None]
        return st[-1], behind

    token = start_gather(0, first[:1])[0] + start_gather(0, first[1:])[0]
    zero = token[0, 0]
    for l in range(DEPTH):
        qs[l] = _layer_prep({k: (w[k][l] + zero if k == "s5_a_re" else w[k][l]) for k in SMALL})
        token = token + qs[l]["tabc"][1][0, 0, 0, :8, :128] + qs[l]["bre"][0, 0, :8, :128].astype(F32)

    def fetch(l, name, after):
        names, st, got = gathers[l, name]
        tie = None
        if got is None:
            if l == 0 and name == "w_in":
                after = token
            lands = _push_wait(f"gather_wait_{l}_{names[0]}", st, after, True)
            for l2, names2 in follow.get((l, name), ()):
                tok, lands[0] = start_gather(l2, names2, lands[0])
                tie = tok if tie is None else tie + tok
            got = dict(zip(names, lands))
            for k in names:
                gathers[l, k][2] = got
        full = got[name]
        if name == "w_in":
            return _in_rows(full, token if tie is None else tie)
        if tie is not None:
            qs[l]["ln2_b"] = qs[l]["ln2_b"] + tie[0, 0]
        return full.reshape(D, D) if name == "w_out" else full

    scatters, held = [], {}

    def emit(l, grads):
        if l > 0:
            held.update(grads)
            if "w_in" not in grads:
                return 0.0
            grads = dict(held)
            held.clear()
        names = tuple(grads)
        st = _push_start(f"scatter_start_{l}_{names[0]}", [grads[k] for k in names], False)
        scatters.append((l, names, st))
        return st[-1][0, 0]

    loss, dx, smalls = _local_step(x.reshape(N, D), loss_target.reshape(N, D), qs, _rope_tables(128), fetch, emit)

    out = {}
    native = _allreduce_small([[smalls[l][k] for k in NATIVE] for l in range(DEPTH)])
    native = dict(zip(NATIVE, native))
    loss = native["loss"][0, 0, 0] + native["loss"][1, 0, 0]
    gsmall = _finish_small(native, w)
    res = _adamw_small(*([t[k] for k in SMALL] for t in (gsmall, w, mom, var)))
    for i, k in enumerate(SMALL):
        out[k] = [gsmall[k], res[0][i], res[1][i], res[2][i]]

    recv, own = {}, {}

    def finish(keys, after):
        for l, names, st in scatters:
            if names[0] in keys:
                ops = _push_wait(f"scatter_wait_{l}_{names[0]}", st, after, False)
                for i, k in enumerate(names):
                    own[l, k], recv[l, k] = ops[i], ops[len(names) + i]
        sums = [_sum_sources(me, [recv[l, k] for l in range(DEPTH)], [own[l, k] for l in range(DEPTH)]) for k in keys]
        for k, mine, other in zip(keys, sums, _swap_sibling(sums)):
            shp = shard[k].shape
            r = _adamw([mine, other], *((tr(t[k]) if k == "w_in" else t[k]).reshape(-1, shp[-1]) for t in (w, mom, var)))
            r = [t.reshape(shp) for t in r]
            out[k] = [tr(t) for t in r] if k == "w_in" else r
        return out[keys[-1]][1]

    last = finish(("w_ff1", "w_ff2", "w_out", "s5_w_glu"), res[0][-1])
    finish(("w_in",), last)

    return (loss, dx.reshape(NSEQ, L, D), *[out[k][0] for k in _WEIGHTS], *[out[k][1] for k in _WEIGHTS],
            *[out[k][2] for k in _WEIGHTS], *[out[k][3] for k in _WEIGHTS])
```

```python
import functools
import math

import jax
import jax.numpy as jnp
from jax import lax
from jax.experimental import pallas as pl
from jax.experimental.pallas import tpu as pltpu

F32 = jnp.float32
MX = jnp.bfloat16
MESH = pl.DeviceIdType.MESH

DEPTH = 2
NSEQ = 2
L = 2048
N = NSEQ * L
D = 1024
DFF = 4096
NSHARD = 4
S5_G, S5_H, S5_P = 16, 16, 64
GLA_CHUNK = 64
NCHUNK = L // GLA_CHUNK
GLA_GROUP = 4
NGROUP = NCHUNK // GLA_GROUP
SWA_BLK = 128
NBLK = L // SWA_BLK
ROT = 16
ROPE_THETA = 500000.0
LN_EPS = 1e-5
ALPHA = (2 * DEPTH) ** 0.25
NEG_BIG = -1e30
DIN = 1824
DINP = 1920
ADAM_LR, ADAM_B1, ADAM_B2, ADAM_EPS, ADAM_WD, ADAM_STEP = 0.001, 0.9, 0.999, 1e-08, 0.01, 10
VMEM_LIMIT = 56 * 1024 * 1024
TT = 512
SW = 512
FFN_TM = 512
FFN_TM_W = 1024
FFN_WB = 1
FFN_VMEM = 60 * 1024 * 1024
INPROJ_BWD_TM = 512


def _cp(sem, vmem=VMEM_LIMIT):
    return pltpu.CompilerParams(dimension_semantics=sem, vmem_limit_bytes=vmem)


def _mm(a, b):
    return jnp.dot(a.astype(MX), b.astype(MX), preferred_element_type=F32)


def _mm_nt(a, b):
    return lax.dot_general(a.astype(MX), b.astype(MX), (((1,), (1,)), ((), ())), preferred_element_type=F32)


def _mm_tn(a, b):
    return lax.dot_general(a.astype(MX), b.astype(MX), (((0,), (0,)), ((), ())), preferred_element_type=F32)


@jax.custom_vjp
def _dmm(a, b):
    return _mm(a, b)


_dmm.defvjp(lambda a, b: (_mm(a, b), (a, b)), lambda r, g: (_mm_nt(g, r[1]), _mm_tn(r[0], g)))


@jax.custom_vjp
def _dmm_nt(a, b):
    return _mm_nt(a, b)


_dmm_nt.defvjp(lambda a, b: (_mm_nt(a, b), (a, b)), lambda r, g: (_mm(g, r[1]), _mm_tn(g, r[0])))


@jax.custom_vjp
def _dmm_tn(a, b):
    return _mm_tn(a, b)


_dmm_tn.defvjp(lambda a, b: (_mm_tn(a, b), (a, b)), lambda r, g: (_mm_nt(r[1], g), _mm(r[0], g)))


def _split3(x):
    hi = x.astype(MX)
    r1 = x - hi.astype(F32)
    mid = r1.astype(MX)
    lo = (r1 - mid.astype(F32)).astype(MX)
    return hi, mid, lo


def _chunk_pairs(rows, rev, strict):
    r = lax.broadcasted_iota(jnp.int32, (rows, rows), 0)
    c = lax.broadcasted_iota(jnp.int32, (rows, rows), 1)
    order = ((c > r) if strict else (c >= r)) if rev else ((c < r) if strict else (c <= r))
    return (r // GLA_CHUNK == c // GLA_CHUNK) & order


def _cums_impl(x, rev):
    rows, w = x.shape
    t = jnp.where(_chunk_pairs(rows, rev, False), 1.0, 0.0).astype(MX)
    s = jnp.dot(t, jnp.concatenate(_split3(x), axis=1), preferred_element_type=F32)
    return s[:, 0:w] + s[:, w:2 * w] + s[:, 2 * w:3 * w]


@functools.partial(jax.custom_vjp, nondiff_argnums=(1,))
def _cums(x, rev):
    return _cums_impl(x, rev)


_cums.defvjp(lambda x, rev: (_cums_impl(x, rev), None), lambda rev, r, g: (_cums_impl(g, not rev),))


def _ln_fwd(s, g, b):
    mu = jnp.mean(s, axis=-1, keepdims=True)
    xc = s - mu
    var = jnp.mean(xc * xc, axis=-1, keepdims=True)
    return xc * lax.rsqrt(var + LN_EPS) * g + b


def _ln_bwd(dy, s, g):
    mu = jnp.mean(s, axis=-1, keepdims=True)
    xc = s - mu
    var = jnp.mean(xc * xc, axis=-1, keepdims=True)
    rstd = lax.rsqrt(var + LN_EPS)
    xhat = xc * rstd
    dxh = dy * g
    ds = rstd * (dxh - jnp.mean(dxh, axis=-1, keepdims=True) - xhat * jnp.mean(dxh * xhat, axis=-1, keepdims=True))
    return ds, jnp.sum(dy * xhat, axis=0, keepdims=True), jnp.sum(dy, axis=0, keepdims=True)


def _sds(shape, dtype=F32):
    return jax.ShapeDtypeStruct(shape, dtype)


_IN_ROW_PIECES = (((0, 0), (0, 456)), ((1, 0), (456, 456)), ((2, 0), (912, 112)), ((2, 112), (1792, 32)),
                  ((2, 144), (1024, 312)), ((3, 0), (1336, 456)))


def _in_rows(g4, behind):
    def body(g_ref, behind_ref, o_ref, tmp):
        tmp[DIN:DINP] = jnp.zeros((DINP - DIN, D), F32)
        for (j, s0), (d0, n_) in _IN_ROW_PIECES:
            tmp[d0:d0 + n_] = g_ref[j, s0:s0 + n_].astype(F32)
        o_ref[...] = tmp[...].astype(MX)

    vm = pl.BlockSpec(memory_space=pltpu.VMEM)
    return pl.pallas_call(body, in_specs=[vm, pl.BlockSpec(memory_space=pl.ANY)], out_specs=vm,
                          out_shape=_sds((DINP, D), MX), scratch_shapes=[pltpu.VMEM((DINP, D), F32)], name="in_rows",
                          compiler_params=pltpu.CompilerParams(vmem_limit_bytes=VMEM_LIMIT))(g4, behind)


def _inproj_fwd(x, wt):
    tm = 512

    def body(x_ref, w_ref, h_ref):
        h_ref[...] = _mm_nt(x_ref[...], w_ref[...])

    return pl.pallas_call(
        body, grid=(N // tm,),
        in_specs=[pl.BlockSpec((tm, D), lambda i: (i, 0)), pl.BlockSpec((DINP, D), lambda i: (0, 0))],
        out_specs=pl.BlockSpec((tm, DINP), lambda i: (i, 0)),
        out_shape=_sds((N, DINP)), name="inproj_fwd", compiler_params=_cp(("parallel",)))(x, wt)


def _inproj_bwd(x, w, dxp, du2, dud, gq_f, gq_b, gk_f, gk_b, gv_f, gv_b, gr, daq, dakv, dhl):
    tm = INPROJ_BWD_TM
    nt = N // tm

    def body(x_ref, w_ref, dxp_ref, du2_ref, dud_ref, gqf, gqb, gkf, gkb, gvf, gvb, gr_ref, daq_ref, dakv_ref, dhl_ref,
             dx_ref, dw_ref, acc):
        i = pl.program_id(0)
        f = lambda r: r[...].astype(F32)
        dh = jnp.concatenate([
            du2_ref[0] + du2_ref[1] + f(dud_ref), f(gqf) + f(gqb), f(gkf) + f(gkb), f(gvf) + f(gvb),
            f(gr_ref), f(daq_ref), f(dakv_ref), f(dhl_ref)], axis=1)
        dx_ref[...] = dxp_ref[...] + _mm(dh, w_ref[...])
        contrib = _mm_tn(dh, x_ref[...])

        @pl.when(i == 0)
        def _():
            acc[...] = contrib

        @pl.when(i > 0)
        def _():
            acc[...] += contrib

        @pl.when(i == nt - 1)
        def _():
            for (j, d0), (s0, n_) in _IN_ROW_PIECES:
                dw_ref[j, d0:d0 + n_] = acc[s0:s0 + n_].astype(MX)

    row = lambda w_: pl.BlockSpec((tm, w_), lambda i: (i, 0))
    return pl.pallas_call(
        body, grid=(nt,),
        in_specs=[row(D), pl.BlockSpec((DINP, D), lambda i: (0, 0)), row(D),
                  pl.BlockSpec((2, tm, 256), lambda i: (0, i, 0)), row(256), row(128), row(128), row(128), row(128),
                  row(256), row(256), row(256), row(512), row(256), row(128)],
        out_specs=[row(D), pl.BlockSpec((NSHARD, DIN // NSHARD, D), lambda i: (0, 0, 0))],
        out_shape=[_sds((N, D)), _sds((NSHARD, DIN // NSHARD, D), MX)],
        scratch_shapes=[pltpu.VMEM((DINP, D), F32)],
        name="inproj_bwd", compiler_params=_cp(("arbitrary",)))(
            x, w, dxp, du2, dud, gq_f, gq_b, gk_f, gk_b, gv_f, gv_b, gr, daq, dakv, dhl)


def _scan_tables(mr, mi, reverse):
    pw = [(mr, mi)]
    for _ in range(7):
        pr, pi = pw[-1]
        pw.append((pr * mr - pi * mi, pr * mi + pi * mr))
    rows = jnp.arange(8)[:, None]
    out = []
    for d in (1, 2, 4):
        keep = rows >= d
        out += [jnp.where(keep, pw[d - 1][0][None], 0.0), jnp.where(keep, pw[d - 1][1][None], 0.0)]
    out += [jnp.stack([p[0] for p in pw]), jnp.stack([p[1] for p in pw])]
    t = jnp.stack(out)
    if reverse:
        t = t[:, ::-1, :]
    return t.reshape(8, 8, 2, SW).transpose(2, 0, 1, 3)


def _tile_scan(xr, xi, a, cr, ci, reverse):
    for lvl, d in enumerate((1, 2, 4)):
        sh = 8 - d if reverse else d
        sr = pltpu.roll(xr, sh, 0)
        si = pltpu.roll(xi, sh, 0)
        ar, ai = a[2 * lvl], a[2 * lvl + 1]
        xr, xi = xr + ar * sr - ai * si, xi + ar * si + ai * sr
    pr, pi = a[6], a[7]
    return xr + pr * cr - pi * ci, xi + pr * ci + pi * cr


NJ = TT // 8


def _lockstep_tables(mr, mi, reverse):
    nr, ni = mr, mi
    for _ in range(NJ.bit_length() - 1):
        nr, ni = nr * nr - ni * ni, 2.0 * nr * ni
    pr, pi = mr[None], mi[None]
    while pr.shape[0] < NJ:
        k = pr.shape[0]
        tr, ti = pr[k - 1], pi[k - 1]
        pr, pi = (jnp.concatenate([pr, pr * tr - pi * ti]), jnp.concatenate([pi, pr * ti + pi * tr]))
    if reverse:
        pr, pi = pr[::-1], pi[::-1]
    rows = jnp.broadcast_to(jnp.stack([mr, mi])[:, None, :], (2, 8, 2 * SW))
    link = _scan_tables(nr, ni, reverse)
    a = jnp.concatenate([rows.reshape(2, 8, 2, SW).transpose(2, 0, 1, 3), link], axis=1)
    return a, jnp.stack([pr, pi]).reshape(2, NJ, 2, SW).transpose(2, 0, 1, 3)


def _to_lockstep(ref, *lead):
    return jnp.concatenate([ref[(*lead, pl.ds(j, 8, stride=NJ), slice(None))] for j in range(NJ)], axis=0)


def _from_lockstep(val, ref, *lead):
    for j in range(NJ):
        ref[(*lead, pl.ds(j, 8, stride=NJ), slice(None))] = val[8 * j:8 * j + 8]


def _expand_powers(p_ref, pexp):
    for c in range(2):
        for j in range(NJ):
            pexp[c, j] = jnp.broadcast_to(p_ref[0, 0, c, j:j + 1, :], (8, SW))


def _lockstep_scan(xre, xim, a_ref, pexp, car, reverse, extra=None):
    a = [a_ref[0, 0, k] for k in range(10)]
    mr, mi = a[0], a[1]
    order = (lambda i: NJ - 1 - i) if reverse else (lambda i: i)

    def local(i, hcar):
        hr, hi = hcar
        r0 = pl.multiple_of(order(i) * 8, 8)
        hr, hi = mr * hr - mi * hi + xre[pl.ds(r0, 8), :], mr * hi + mi * hr + xim[pl.ds(r0, 8), :]
        xre[pl.ds(r0, 8), :] = hr
        xim[pl.ds(r0, 8), :] = hi
        return hr, hi

    z8 = jnp.zeros((8, SW), F32)
    er, ei = lax.fori_loop(0, NJ, local, (z8, z8), unroll=4)
    c0r, c0i = car[0], car[1]
    er, ei = _tile_scan(er, ei, a[2:], c0r, c0i, reverse)
    rowid = lax.broadcasted_iota(jnp.int32, (8, SW), 0)
    first, sh, last = (7, 7, 0) if reverse else (0, 1, 7)
    cvr = jnp.where(rowid == first, c0r, pltpu.roll(er, sh, 0))
    cvi = jnp.where(rowid == first, c0i, pltpu.roll(ei, sh, 0))
    car[0] = jnp.broadcast_to(er[last:last + 1, :], (8, SW))
    car[1] = jnp.broadcast_to(ei[last:last + 1, :], (8, SW))

    def fix(i, carry):
        j = order(i)
        r0 = pl.multiple_of(j * 8, 8)
        pr, pi = pexp[0, j], pexp[1, j]
        sr = xre[pl.ds(r0, 8), :] + pr * cvr - pi * cvi
        si = xim[pl.ds(r0, 8), :] + pr * cvi + pi * cvr
        xre[pl.ds(r0, 8), :] = sr
        xim[pl.ds(r0, 8), :] = si
        if extra is None:
            return carry
        return (sr, si, extra(r0, sr, si, carry[0], carry[1], carry[2]))

    init = (cvr, cvi, extra(None, None, None, None, None, None)) if extra is not None else 0
    return lax.fori_loop(0, NJ, fix, init, unroll=4)


def _s5_time_block(z, s, t, adjoint):
    flip = (1 - z) if adjoint else z
    return s * (L // TT) + t + flip * (L // TT - 1 - 2 * t)


def _s5_fwd(h, bre, bim, cre, cim, tab):
    nt = L // TT
    taba, tabp = tab

    def body(u_ref, bre_ref, bim_ref, cre_ref, cim_ref, a_ref, p_ref, hre_ref, him_ref, y_ref, car, pexp):
        z = pl.program_id(1)
        s = pl.program_id(2)
        tc = pl.program_id(3)

        @pl.when(tc == 0)
        def _():
            car[...] = jnp.zeros_like(car)

        @pl.when((tc == 0) & (s == 0))
        def _():
            _expand_powers(p_ref, pexp)

        u = _to_lockstep(u_ref)
        hre_ref[0] = _mm(u, bre_ref[0, 0])
        him_ref[0] = _mm(u, bim_ref[0, 0])

        @pl.when(z == 0)
        def _():
            _lockstep_scan(hre_ref.at[0], him_ref.at[0], a_ref, pexp, car, False)

        @pl.when(z == 1)
        def _():
            _lockstep_scan(hre_ref.at[0], him_ref.at[0], a_ref, pexp, car, True)

        _from_lockstep(_mm(hre_ref[0], cre_ref[0, 0]) - _mm(him_ref[0], cim_ref[0, 0]), y_ref, 0)

    tb = lambda b, z, s, t: _s5_time_block(z, s, t, False)
    wspec = lambda r, c: pl.BlockSpec((1, 1, r, c), lambda b, z, s, t: (z, b, 0, 0))
    return pl.pallas_call(
        body, grid=(2, 2, NSEQ, nt),
        in_specs=[pl.BlockSpec((TT, 128), lambda b, z, s, t: (tb(b, z, s, t), b)),
                  wspec(128, SW), wspec(128, SW), wspec(SW, 128), wspec(SW, 128),
                  pl.BlockSpec((1, 1, 10, 8, SW), lambda b, z, s, t: (z, b, 0, 0, 0)),
                  pl.BlockSpec((1, 1, 2, NJ, SW), lambda b, z, s, t: (z, b, 0, 0, 0))],
        out_specs=[pl.BlockSpec((1, TT, SW), lambda b, z, s, t: (z, tb(b, z, s, t), b)),
                   pl.BlockSpec((1, TT, SW), lambda b, z, s, t: (z, tb(b, z, s, t), b)),
                   pl.BlockSpec((1, TT, 128), lambda b, z, s, t: (z, tb(b, z, s, t), b))],
        out_shape=[_sds((2, N, 2 * SW)), _sds((2, N, 2 * SW)), _sds((2, N, 256))],
        scratch_shapes=[pltpu.VMEM((2, 8, SW), F32), pltpu.VMEM((2, NJ, 8, SW), F32)],
        name="s5_fwd", compiler_params=_cp(("arbitrary",) * 4))(h, bre, bim, cre, cim, taba, tabp)


def _s5_bwd(h, dyp, hre, him, bre, bim, cre, cim, tabc):
    nt = L // TT
    taba, tabp = tabc

    def body(u_ref, dy_ref, hre_ref, him_ref, bre_ref, bim_ref, cre_ref, cim_ref, a_ref, p_ref,
             du_ref, dbre_ref, dbim_ref, dcre_ref, dcim_ref, dmu_ref, gre, gim, car, acc, macc, pexp):
        z = pl.program_id(1)
        s = pl.program_id(2)
        tc = pl.program_id(3)

        @pl.when(tc == 0)
        def _():
            car[...] = jnp.zeros_like(car)

        @pl.when((tc == 0) & (s == 0))
        def _():
            acc[...] = jnp.zeros_like(acc)
            macc[...] = jnp.zeros_like(macc)
            _expand_powers(p_ref, pexp)

        dy = _to_lockstep(dy_ref)
        gre[...] = _mm_nt(dy, cre_ref[0, 0])
        gim[...] = -_mm_nt(dy, cim_ref[0, 0])

        def run(reverse):
            def pair(r0, gr_, gi_, pvr, pvi, m):
                if r0 is None:
                    return (macc[0], macc[1])
                hr = hre_ref[0, pl.ds(r0, 8), :]
                hi = him_ref[0, pl.ds(r0, 8), :]
                return (m[0] + pvr * hr + pvi * hi, m[1] + pvi * hr - pvr * hi)

            _, _, (dmr, dmi) = _lockstep_scan(gre, gim, a_ref, pexp, car, reverse, pair)
            macc[0] = dmr
            macc[1] = dmi

        @pl.when(z == 0)
        def _():
            run(True)

        @pl.when(z == 1)
        def _():
            run(False)

        gr = gre[...]
        gi = gim[...]
        u = _to_lockstep(u_ref)
        _from_lockstep(_mm_nt(gr, bre_ref[0, 0]) + _mm_nt(gi, bim_ref[0, 0]), du_ref, 0)
        acc[0] += _mm_tn(u, gr)
        acc[1] += _mm_tn(u, gi)
        acc[2] += _mm_tn(dy, hre_ref[0])
        acc[3] -= _mm_tn(dy, him_ref[0])

        @pl.when((tc == nt - 1) & (s == NSEQ - 1))
        def _():
            grp = lax.broadcasted_iota(jnp.int32, (S5_H, SW), 1) // S5_P
            for k, out in enumerate((dbre_ref, dbim_ref, dcre_ref, dcim_ref)):
                c = jnp.zeros((S5_H, SW), F32)
                for i in range(8):
                    c = c + jnp.where(grp == i, acc[k, i * S5_H:(i + 1) * S5_H, :], 0.0)
                out[0, 0] = c
            dmu_ref[0, 0] = jnp.concatenate([jnp.sum(macc[0], axis=0, keepdims=True),
                                             jnp.sum(macc[1], axis=0, keepdims=True)], axis=0)

    tb = lambda b, z, s, t: _s5_time_block(z, s, t, True)
    wspec = lambda r, c: pl.BlockSpec((1, 1, r, c), lambda b, z, s, t: (z, b, 0, 0))
    tok = lambda w_: pl.BlockSpec((TT, w_), lambda b, z, s, t: (tb(b, z, s, t), b))
    st = pl.BlockSpec((1, TT, SW), lambda b, z, s, t: (z, tb(b, z, s, t), b))
    return pl.pallas_call(
        body, grid=(2, 2, NSEQ, nt),
        in_specs=[tok(128), tok(128), st, st, wspec(128, SW), wspec(128, SW), wspec(SW, 128), wspec(SW, 128),
                  pl.BlockSpec((1, 1, 10, 8, SW), lambda b, z, s, t: (z, b, 0, 0, 0)),
                  pl.BlockSpec((1, 1, 2, NJ, SW), lambda b, z, s, t: (z, b, 0, 0, 0))],
        out_specs=[pl.BlockSpec((1, TT, 128), lambda b, z, s, t: (z, tb(b, z, s, t), b)),
                   wspec(S5_H, SW), wspec(S5_H, SW), wspec(S5_H, SW), wspec(S5_H, SW),
                   wspec(2, SW)],
        out_shape=[_sds((2, N, 256))] + [_sds((2, 2, S5_H, SW))] * 4 + [_sds((2, 2, 2, SW))],
        scratch_shapes=[pltpu.VMEM((TT, SW), F32), pltpu.VMEM((TT, SW), F32), pltpu.VMEM((2, 8, SW), F32),
                        pltpu.VMEM((4, 128, SW), F32), pltpu.VMEM((2, 8, SW), F32), pltpu.VMEM((2, NJ, 8, SW), F32)],
        name="s5_bwd", compiler_params=_cp(("arbitrary",) * 4))(h, dyp, hre, him, bre, bim, cre, cim, taba, tabp)


_GELU_C = math.sqrt(2.0 / math.pi)


def _gelu(y):
    return 0.5 * y * (1.0 + jnp.tanh(_GELU_C * (y + 0.044715 * y * y * y)))


def _gelu_grad(y):
    t = jnp.tanh(_GELU_C * (y + 0.044715 * y * y * y))
    return 0.5 * (1.0 + t) + 0.5 * y * (1.0 - t * t) * _GELU_C * (1.0 + 3 * 0.044715 * y * y)


def _glu_halves(w4_ref):
    return (jnp.concatenate([w4_ref[0], w4_ref[1]], axis=1), jnp.concatenate([w4_ref[2], w4_ref[3]], axis=1))


def _s5_glu_fwd(y2, h, dsk, w4, bv, bg):
    tm = 512

    def body(y2_ref, u_ref, d_ref, w4_ref, bv_ref, bg_ref, ya_ref):
        wv, wg = _glu_halves(w4_ref)
        z = _gelu(y2_ref[0] + y2_ref[1] + d_ref[...] * u_ref[...])
        val = _mm(z, wv) + bv_ref[...]
        gate = _mm(z, wg) + bg_ref[...]
        ya_ref[...] = (val * jax.nn.sigmoid(gate)).astype(MX)

    full = lambda r, c: pl.BlockSpec((r, c), lambda i: (0, 0))
    return pl.pallas_call(
        body, grid=(N // tm,),
        in_specs=[pl.BlockSpec((2, tm, 256), lambda i: (0, i, 0)), pl.BlockSpec((tm, 256), lambda i: (i, 0)),
                  full(1, 256), pl.BlockSpec((NSHARD, 256, 128), lambda i: (0, 0, 0)), full(1, 256), full(1, 256)],
        out_specs=pl.BlockSpec((tm, 256), lambda i: (i, 0)),
        out_shape=_sds((N, 256), MX), name="s5_glu_fwd", compiler_params=_cp(("parallel",)))(y2, h, dsk, w4, bv, bg)


def _s5_glu_bwd(y2, h, dsk, w4, bv, bg, dya):
    tm = 512
    nt = N // tm

    def body(y2_ref, u_ref, d_ref, w4_ref, bv_ref, bg_ref, dya_ref,
             dyp_ref, dud_ref, dd_ref, dw4_ref, dbv_ref, dbg_ref, accv, accg):
        i = pl.program_id(0)

        @pl.when(i == 0)
        def _():
            for r in (dd_ref, accv, accg, dbv_ref, dbg_ref):
                r[...] = jnp.zeros_like(r)

        wv, wg = _glu_halves(w4_ref)
        u = u_ref[...]
        y = y2_ref[0] + y2_ref[1] + d_ref[...] * u
        z = _gelu(y)
        val = _mm(z, wv) + bv_ref[...]
        sig = jax.nn.sigmoid(_mm(z, wg) + bg_ref[...])
        dya = dya_ref[...]
        dval = dya * sig
        dgate = dya * val * sig * (1.0 - sig)
        dz = _mm_nt(dval, wv) + _mm_nt(dgate, wg)
        dy = dz * _gelu_grad(y)
        dyp_ref[...] = dy
        dud_ref[...] = (dy * d_ref[...]).astype(MX)
        dd_ref[...] += jnp.sum(dy * u, axis=0, keepdims=True)
        accv[...] += _mm_tn(z, dval)
        accg[...] += _mm_tn(z, dgate)
        dbv_ref[...] += jnp.sum(dval, axis=0, keepdims=True)
        dbg_ref[...] += jnp.sum(dgate, axis=0, keepdims=True)

        @pl.when(i == nt - 1)
        def _():
            dw4_ref[0] = accv[:, 0:128].astype(MX)
            dw4_ref[1] = accv[:, 128:256].astype(MX)
            dw4_ref[2] = accg[:, 0:128].astype(MX)
            dw4_ref[3] = accg[:, 128:256].astype(MX)

    full = lambda r, c: pl.BlockSpec((r, c), lambda i: (0, 0))
    row = pl.BlockSpec((tm, 256), lambda i: (i, 0))
    wspec = pl.BlockSpec((NSHARD, 256, 128), lambda i: (0, 0, 0))
    return pl.pallas_call(
        body, grid=(nt,),
        in_specs=[pl.BlockSpec((2, tm, 256), lambda i: (0, i, 0)), row, full(1, 256), wspec, full(1, 256), full(1, 256),
                  row],
        out_specs=[row, row, full(1, 256), wspec, full(1, 256), full(1, 256)],
        out_shape=[_sds((N, 256)), _sds((N, 256), MX), _sds((1, 256)), _sds((NSHARD, 256, 128), MX), _sds((1, 256)),
                   _sds((1, 256))],
        scratch_shapes=[pltpu.VMEM((256, 256), F32), pltpu.VMEM((256, 256), F32)],
        name="s5_glu_bwd", compiler_params=_cp(("arbitrary",)))(y2, h, dsk, w4, bv, bg, dya)


def _logsig(x):
    return jnp.minimum(x, 0.0) - jnp.log(1.0 + jnp.exp(-jnp.abs(x)))


def _gla_gate_fwd(h, wa, ba):
    tm = 512

    def body(hl_ref, wa_ref, ba_ref, la_ref):
        la_ref[...] = _logsig(_mm(hl_ref[...], wa_ref[...]) + ba_ref[...]) * (1.0 / 16.0)

    return pl.pallas_call(
        body, grid=(N // tm,),
        in_specs=[pl.BlockSpec((tm, 128), lambda i: (i, 14)), pl.BlockSpec((128, 256), lambda i: (0, 0)),
                  pl.BlockSpec((1, 256), lambda i: (0, 0))],
        out_specs=pl.BlockSpec((tm, 256), lambda i: (i, 0)),
        out_shape=_sds((N, 256)), name="gla_gate_fwd", compiler_params=_cp(("parallel",)))(h, wa, ba)


def _gla_gate_bwd(h, wa, ba, dla_f, dla_b):
    tm = 512

    def body(hl_ref, wa_ref, ba_ref, df_ref, db_ref, dhl_ref, dwa_ref, dba_ref):
        i = pl.program_id(0)

        @pl.when(i == 0)
        def _():
            dwa_ref[...] = jnp.zeros_like(dwa_ref)
            dba_ref[...] = jnp.zeros_like(dba_ref)

        hl = hl_ref[...]
        pre = _mm(hl, wa_ref[...]) + ba_ref[...]
        dpre = jnp.concatenate([df_ref[...], db_ref[...]], axis=1) * (1.0 / 16.0) * jax.nn.sigmoid(-pre)
        dhl_ref[...] = _mm_nt(dpre, wa_ref[...]).astype(MX)
        dwa_ref[...] += _mm_tn(hl, dpre)[0:32]
        dba_ref[...] += jnp.sum(dpre, axis=0, keepdims=True)

    row = pl.BlockSpec((tm, 128), lambda i: (i, 0))
    return pl.pallas_call(
        body, grid=(N // tm,),
        in_specs=[pl.BlockSpec((tm, 128), lambda i: (i, 14)), pl.BlockSpec((128, 256), lambda i: (0, 0)),
                  pl.BlockSpec((1, 256), lambda i: (0, 0)), row, row],
        out_specs=[row, pl.BlockSpec((32, 256), lambda i: (0, 0)), pl.BlockSpec((1, 256), lambda i: (0, 0))],
        out_shape=[_sds((N, 128), MX), _sds((32, 256)), _sds((1, 256))],
        name="gla_gate_bwd", compiler_params=_cp(("arbitrary",)))(h, wa, ba, dla_f, dla_b)


def _gla_chunk(q, k, v, la, st, rev):
    c = GLA_CHUNK
    rows = q.shape[0]
    nch = rows // c
    b = _cums(la, rev)
    blc = [jnp.sum(la[i * c:(i + 1) * c], axis=0, keepdims=True) for i in range(nch)]
    bl = jnp.concatenate([jnp.broadcast_to(t, (c, 128)) for t in blc], axis=0)
    q_in = q * (32.0 ** -0.5) * jnp.exp(b)
    k_in = k * jnp.exp(-b)
    k_st = k * jnp.exp(bl - b)
    lane_k = lax.broadcasted_iota(jnp.int32, (1, 128), 1) // 32
    lane_v = lax.broadcasted_iota(jnp.int32, (1, 256), 1) // 64
    qs = jnp.concatenate([jnp.where(lane_k == hd, q_in, 0.0) for hd in range(4)], axis=0)
    a = _dmm_nt(qs, k_in)
    a = jnp.where(jnp.concatenate([_chunk_pairs(rows, rev, rev)] * 4, axis=0), a, 0.0)
    o4 = _dmm(a, v)
    o = jnp.zeros((rows, 256), F32)
    for hd in range(4):
        o = o + jnp.where(lane_v == hd, o4[hd * rows:(hd + 1) * rows], 0.0)
    bd = (lax.broadcasted_iota(jnp.int32, (256, 128), 0) // 64) == (lax.broadcasted_iota(jnp.int32, (256, 128), 1) // 32)
    inter = [None] * nch
    for i in (reversed(range(nch)) if rev else range(nch)):
        sl = slice(i * c, (i + 1) * c)
        inter[i] = _dmm_nt(q_in[sl], st)
        st = jnp.exp(blc[i]) * st + jnp.where(bd, _dmm_tn(v[sl], k_st[sl]), 0.0)
    return o + jnp.concatenate(inter, axis=0), st


def _gla_chunk_of(c, rev):
    return NGROUP - 1 - c if rev else c


def _gla_fwd(h, la2):
    c = GLA_GROUP * GLA_CHUNK

    def body(qf, kf, vf, laf, qb, kb, vb, lab, of_ref, ob_ref, sf_ref, sb_ref, stf, stb):
        @pl.when(pl.program_id(0) == 0)
        def _():
            stf[...] = jnp.zeros_like(stf)
            stb[...] = jnp.zeros_like(stb)

        ins = [(qf[s], kf[s], vf[s], laf[s], stf[s], qb[s], kb[s], vb[s], lab[s], stb[s]) for s in range(NSEQ)]
        outs = [(_gla_chunk(*t[:5], False), _gla_chunk(*t[5:], True)) for t in ins]
        for s in range(NSEQ):
            sf_ref[s, 0] = ins[s][4]
            sb_ref[s, 0] = ins[s][9]
            (of_ref[s], stf[s]), (ob_ref[s], stb[s]) = outs[s]

    def specs(rev):
        ch = lambda i: _gla_chunk_of(i, rev)
        return [pl.BlockSpec((NSEQ, c, 128), lambda i: (0, ch(i), 2)), pl.BlockSpec((NSEQ, c, 128), lambda i: (0, ch(i), 3)),
                pl.BlockSpec((NSEQ, c, 256), lambda i: (0, ch(i), 2)),
                pl.BlockSpec((NSEQ, c, 128), lambda i: (0, ch(i), 1 if rev else 0))]

    orow = lambda rev: pl.BlockSpec((NSEQ, c, 256), lambda i: (0, _gla_chunk_of(i, rev), 0))
    srow = lambda rev: pl.BlockSpec((NSEQ, 1, 256, 128), lambda i: (0, _gla_chunk_of(i, rev), 0, 0))
    h3, la3 = h.reshape(NSEQ, L, DINP), la2.reshape(NSEQ, L, 256)
    of, ob, sf, sb = pl.pallas_call(
        body, grid=(NGROUP,),
        in_specs=specs(False) + specs(True),
        out_specs=[orow(False), orow(True), srow(False), srow(True)],
        out_shape=[_sds((NSEQ, L, 256)), _sds((NSEQ, L, 256)), _sds((NSEQ, NGROUP, 256, 128)),
                   _sds((NSEQ, NGROUP, 256, 128))],
        scratch_shapes=[pltpu.VMEM((NSEQ, 256, 128), F32), pltpu.VMEM((NSEQ, 256, 128), F32)],
        name="gla_fwd", compiler_params=_cp(("arbitrary",)))(h3, h3, h3, la3, h3, h3, h3, la3)
    return of.reshape(N, 256), ob.reshape(N, 256), sf, sb


def _gla_bwd(h, la2, do, sf, sb):
    c = GLA_GROUP * GLA_CHUNK

    def body(qf, kf, vf, laf, dof, sfr, qb, kb, vb, lab, dob, sbr,
             dqf, dkf, dvf, dlf, dqb, dkb, dvb, dlb, dstf, dstb):
        @pl.when(pl.program_id(0) == 0)
        def _():
            dstf[...] = jnp.zeros_like(dstf)
            dstb[...] = jnp.zeros_like(dstb)

        def one(s, q, k, v, la, do_, st, dst, rev):
            _, vjp = jax.vjp(functools.partial(_gla_chunk, rev=rev), q[s], k[s], v[s], la[s], st[s, 0])
            return vjp((do_[s], dst[s]))

        res = [(one(s, qf, kf, vf, laf, dof, sfr, dstf, False), one(s, qb, kb, vb, lab, dob, sbr, dstb, True))
               for s in range(NSEQ)]
        for s in range(NSEQ):
            for (gq, gk, gv, gl, gs), (dq, dk, dv, dl, dst) in ((res[s][0], (dqf, dkf, dvf, dlf, dstf)),
                                                                  (res[s][1], (dqb, dkb, dvb, dlb, dstb))):
                dq[s], dk[s], dv[s] = gq.astype(MX), gk.astype(MX), gv.astype(MX)
                dl[s], dst[s] = gl, gs

    def specs(rev):
        ch = lambda i: _gla_chunk_of(i, not rev)
        return [pl.BlockSpec((NSEQ, c, 128), lambda i: (0, ch(i), 2)), pl.BlockSpec((NSEQ, c, 128), lambda i: (0, ch(i), 3)),
                pl.BlockSpec((NSEQ, c, 256), lambda i: (0, ch(i), 2)),
                pl.BlockSpec((NSEQ, c, 128), lambda i: (0, ch(i), 1 if rev else 0)),
                pl.BlockSpec((NSEQ, c, 256), lambda i: (0, ch(i), 0)),
                pl.BlockSpec((NSEQ, 1, 256, 128), lambda i: (0, ch(i), 0, 0))]

    def ospecs(rev):
        ch = lambda i: _gla_chunk_of(i, not rev)
        n = pl.BlockSpec((NSEQ, c, 128), lambda i: (0, ch(i), 0))
        return [n, n, pl.BlockSpec((NSEQ, c, 256), lambda i: (0, ch(i), 0)), n]

    oshape = [_sds((NSEQ, L, 128), MX), _sds((NSEQ, L, 128), MX), _sds((NSEQ, L, 256), MX), _sds((NSEQ, L, 128))]
    h3, la3, do3 = h.reshape(NSEQ, L, DINP), la2.reshape(NSEQ, L, 256), do.reshape(NSEQ, L, 256)
    res = pl.pallas_call(
        body, grid=(NGROUP,),
        in_specs=specs(False) + specs(True),
        out_specs=ospecs(False) + ospecs(True),
        out_shape=oshape + oshape,
        scratch_shapes=[pltpu.VMEM((NSEQ, 256, 128), F32), pltpu.VMEM((NSEQ, 256, 128), F32)],
        name="gla_bwd", compiler_params=_cp(("arbitrary",)))(h3, h3, h3, la3, do3, sf, h3, h3, h3, la3, do3, sb)
    return [r.reshape(N, r.shape[-1]) for r in res]


def _gla_post(of, ob, r, g):
    o = of + ob
    head = lax.broadcasted_iota(jnp.int32, (1, 256), 1) // 64
    mu = jnp.zeros_like(o)
    for hd in range(4):
        mu = mu + jnp.where(head == hd, jnp.sum(jnp.where(head == hd, o, 0.0), axis=-1, keepdims=True) * (1.0 / 64.0), 0.0)
    xc = o - mu
    var = jnp.zeros_like(o)
    for hd in range(4):
        var = var + jnp.where(head == hd, jnp.sum(jnp.where(head == hd, xc * xc, 0.0), axis=-1, keepdims=True) * (1.0 / 64.0), 0.0)
    return xc * lax.rsqrt(var + LN_EPS) * g * (r * jax.nn.sigmoid(r))


def _gla_post_fwd(of, ob, h, g):
    tm = 512

    def body(of_ref, ob_ref, r_ref, g_ref, y_ref):
        y_ref[...] = _gla_post(of_ref[...], ob_ref[...], r_ref[...], g_ref[...]).astype(MX)

    row = pl.BlockSpec((tm, 256), lambda i: (i, 0))
    return pl.pallas_call(
        body, grid=(N // tm,),
        in_specs=[row, row, pl.BlockSpec((tm, 256), lambda i: (i, 3)), pl.BlockSpec((1, 256), lambda i: (0, 0))],
        out_specs=row, out_shape=_sds((N, 256), MX), name="gla_post_fwd", compiler_params=_cp(("parallel",)))(of, ob, h, g)


def _gla_post_bwd(of, ob, h, g, dyb):
    tm = 512

    def body(of_ref, ob_ref, r_ref, g_ref, dy_ref, do_ref, dr_ref, dg_ref):
        @pl.when(pl.program_id(0) == 0)
        def _():
            dg_ref[...] = jnp.zeros_like(dg_ref)

        _, vjp = jax.vjp(_gla_post, of_ref[...], ob_ref[...], r_ref[...], g_ref[...])
        go, _, gr, gg = vjp(dy_ref[...])
        do_ref[...] = go
        dr_ref[...] = gr.astype(MX)
        dg_ref[...] += gg

    row = pl.BlockSpec((tm, 256), lambda i: (i, 0))
    one = pl.BlockSpec((1, 256), lambda i: (0, 0))
    return pl.pallas_call(
        body, grid=(N // tm,),
        in_specs=[row, row, pl.BlockSpec((tm, 256), lambda i: (i, 3)), one, row],
        out_specs=[row, row, one], out_shape=[_sds((N, 256)), _sds((N, 256), MX), _sds((1, 256))],
        name="gla_post_bwd", compiler_params=_cp(("arbitrary",)))(of, ob, h, g, dyb)


def _rope_tables(width):
    pos = jnp.arange(L, dtype=F32)
    inv_freq = ROPE_THETA ** (-jnp.arange(0, ROT, 2, dtype=F32) / ROT)
    ang = pos[:, None] * inv_freq[None, :]
    cos, sin = jnp.cos(ang), jnp.sin(ang)
    one = jnp.ones((L, 64 - ROT), F32)
    zero = jnp.zeros((L, 64 - ROT), F32)
    z8 = jnp.zeros((L, ROT // 2), F32)
    c = jnp.concatenate([cos, cos, one], axis=1)
    sa = jnp.concatenate([z8, sin, zero], axis=1)
    sb = jnp.concatenate([-sin, z8, zero], axis=1)
    rep = width // 64
    return jnp.stack([jnp.tile(c, (1, rep)), jnp.tile(sa, (1, rep)), jnp.tile(sb, (1, rep))])


def _pieces(t, f):
    out = [f(t[:, c * 128:(c + 1) * 128]) for c in range(t.shape[-1] // 128)]
    return out[0] if len(out) == 1 else jnp.concatenate(out, axis=1)


def _rope(t, tab):
    return _pieces(t, lambda x: x * tab[0] + pltpu.roll(x, ROT // 2, 1) * tab[1] + pltpu.roll(x, 128 - ROT // 2, 1) * tab[2])


def _rope_t(g, tab):
    return _pieces(g, lambda x: x * tab[0] + pltpu.roll(x * tab[1], 128 - ROT // 2, 1) + pltpu.roll(x * tab[2], ROT // 2, 1))


def _swa_pad_kv(kv_ref, tk_ref, kexp, vexp):
    z = jnp.zeros((SWA_BLK, 256), F32)
    kr = _rope(kv_ref[:, 0:128], tk_ref[...])
    for hk in range(2):
        for pad in (kexp, vexp):
            pad[hk, 0:SWA_BLK] = z
            pad[hk, SWA_BLK + L:] = z
        kexp[hk, SWA_BLK:SWA_BLK + L] = _swa_expand(kr, hk)
        vexp[hk, SWA_BLK:SWA_BLK + L] = _swa_expand(kv_ref[:, 128:256], hk)


def _swa_expand(x, hk):
    lane = lax.broadcasted_iota(jnp.int32, x.shape, 1)
    sw = pltpu.roll(x, 64, 1)
    pair = jnp.where(lane < 64, x, sw) if hk == 0 else jnp.where(lane < 64, sw, x)
    return jnp.concatenate([pair, pair], axis=1)


def _swa_fold(x, hk):
    a = x[:, 0:128] + x[:, 128:256]
    t = a + pltpu.roll(a, 64, 1)
    lane = lax.broadcasted_iota(jnp.int32, a.shape, 1)
    return jnp.where((lane < 64) if hk == 0 else (lane >= 64), t, 0.0)


def _swa_probs(q2, kexp, n, sink_ref, hk):
    slot = lax.broadcasted_iota(jnp.int32, (1, 256), 1) // 64
    qs = jnp.concatenate([jnp.where(slot == g, q2, 0.0) for g in range(4)], axis=0)
    s = _mm_nt(qs, kexp) * 0.125
    i = lax.broadcasted_iota(jnp.int32, (SWA_BLK, 3 * SWA_BLK), 0)
    j = lax.broadcasted_iota(jnp.int32, (SWA_BLK, 3 * SWA_BLK), 1)
    kpos = n * SWA_BLK - SWA_BLK + j
    ok = (j - i >= 0) & (j - i <= 2 * SWA_BLK) & (kpos >= 0) & (kpos < L)
    s = jnp.where(jnp.concatenate([ok] * 4, axis=0), s, NEG_BIG)
    rowg = lax.broadcasted_iota(jnp.int32, (4 * SWA_BLK, 1), 0) // SWA_BLK
    sink = jnp.zeros((4 * SWA_BLK, 1), F32)
    for g in range(4):
        sink = jnp.where(rowg == g, sink_ref[hk * 4 + g], sink)
    m = jnp.maximum(jnp.max(s, axis=-1, keepdims=True), sink)
    p = jnp.exp(s - m)
    ps = jnp.exp(sink - m)
    inv = 1.0 / (jnp.sum(p, axis=-1, keepdims=True) + ps)
    return qs, p * inv, ps * inv, slot, rowg


def _swa_qtab(tk_ref, r0):
    return [tk_ref[i, pl.ds(r0, SWA_BLK), :] for i in range(3)]


def _swa_fwd(h, tk, sink):
    def body(sink_ref, q_ref, kv_ref, tk_ref, y_ref, kexp, vexp):
        n = pl.program_id(1)

        @pl.when(n == 0)
        def _():
            _swa_pad_kv(kv_ref, tk_ref, kexp, vexp)

        r0 = pl.multiple_of(n * SWA_BLK, SWA_BLK)
        q = _rope(q_ref[...], _swa_qtab(tk_ref, r0))
        for hk in range(2):
            _, p, _, slot, _ = _swa_probs(q[:, hk * 256:(hk + 1) * 256], kexp[hk, pl.ds(r0, 3 * SWA_BLK), :], n,
                                          sink_ref, hk)
            o4 = _mm(p, vexp[hk, pl.ds(r0, 3 * SWA_BLK), :])
            o = jnp.zeros((SWA_BLK, 256), F32)
            for g in range(4):
                o = o + jnp.where(slot == g, o4[g * SWA_BLK:(g + 1) * SWA_BLK], 0.0)
            y_ref[:, hk * 256:(hk + 1) * 256] = o.astype(MX)

    return pl.pallas_call(
        body,
        grid_spec=pltpu.PrefetchScalarGridSpec(
            num_scalar_prefetch=1, grid=(NSEQ, NBLK),
            in_specs=[pl.BlockSpec((SWA_BLK, 512), lambda s, n, sk: (s * NBLK + n, 2)),
                      pl.BlockSpec((L, 256), lambda s, n, sk: (s, 6)),
                      pl.BlockSpec((3, L, 128), lambda s, n, sk: (0, 0, 0))],
            out_specs=pl.BlockSpec((SWA_BLK, 512), lambda s, n, sk: (s * NBLK + n, 0)),
            scratch_shapes=[pltpu.VMEM((2, L + 2 * SWA_BLK, 256), F32), pltpu.VMEM((2, L + 2 * SWA_BLK, 256), F32)]),
        out_shape=_sds((N, 512), MX), name="swa_fwd", compiler_params=_cp(("arbitrary", "arbitrary")))(sink, h, h, tk)


def _swa_bwd(h, tk, sink, dyc):
    def body(sink_ref, q_ref, kv_ref, tk_ref, dy_ref, dq_ref, dkv_ref, dsink_ref, kexp_all, vexp_all, dkacc, dvacc):
        sq = pl.program_id(0)
        n = pl.program_id(1)

        @pl.when(n == 0)
        def _():
            _swa_pad_kv(kv_ref, tk_ref, kexp_all, vexp_all)
            dkacc[...] = jnp.zeros_like(dkacc)
            dvacc[...] = jnp.zeros_like(dvacc)

        @pl.when((n == 0) & (sq == 0))
        def _():
            dsink_ref[...] = jnp.zeros_like(dsink_ref)

        r0 = pl.multiple_of(n * SWA_BLK, SWA_BLK)
        tq = _swa_qtab(tk_ref, r0)
        q = _rope(q_ref[...], tq)
        hrow = lax.broadcasted_iota(jnp.int32, (8, 128), 0)
        dsk = jnp.zeros((8, 128), F32)
        for hk in range(2):
            kexp = kexp_all[hk, pl.ds(r0, 3 * SWA_BLK), :]
            vexp = vexp_all[hk, pl.ds(r0, 3 * SWA_BLK), :]
            qs, p, ps, slot, rowg = _swa_probs(q[:, hk * 256:(hk + 1) * 256], kexp, n, sink_ref, hk)
            dy2 = dy_ref[:, hk * 256:(hk + 1) * 256]
            dos = jnp.concatenate([jnp.where(slot == g, dy2, 0.0) for g in range(4)], axis=0)
            dp = _mm_nt(dos, vexp)
            delta = jnp.sum(p * dp, axis=-1, keepdims=True)
            ds = p * (dp - delta) * 0.125
            dsr = -ps * delta
            for g in range(4):
                dsk = dsk + jnp.where(hrow == hk * 4 + g, jnp.sum(jnp.where(rowg == g, dsr, 0.0), axis=0, keepdims=True), 0.0)
            dq4 = _mm(ds, kexp)
            dq2 = jnp.zeros((SWA_BLK, 256), F32)
            for g in range(4):
                dq2 = dq2 + jnp.where(slot == g, dq4[g * SWA_BLK:(g + 1) * SWA_BLK], 0.0)
            dq_ref[:, hk * 256:(hk + 1) * 256] = _rope_t(dq2, tq).astype(MX)
            dkacc[hk, pl.ds(r0, 3 * SWA_BLK), :] += _mm_tn(ds, qs)
            dvacc[hk, pl.ds(r0, 3 * SWA_BLK), :] += _mm_tn(p, dos)
        dsink_ref[...] += dsk

        @pl.when(n == NBLK - 1)
        def _():
            seq = slice(SWA_BLK, SWA_BLK + L)
            dk = _rope_t(_swa_fold(dkacc[0, seq], 0) + _swa_fold(dkacc[1, seq], 1), tk_ref[...])
            dkv_ref[:, 0:128] = dk.astype(MX)
            dkv_ref[:, 128:256] = (_swa_fold(dvacc[0, seq], 0) + _swa_fold(dvacc[1, seq], 1)).astype(MX)

    blk = lambda col: pl.BlockSpec((SWA_BLK, 512), lambda s, n, sk: (s * NBLK + n, col))
    pad = pltpu.VMEM((2, L + 2 * SWA_BLK, 256), F32)
    return pl.pallas_call(
        body,
        grid_spec=pltpu.PrefetchScalarGridSpec(
            num_scalar_prefetch=1, grid=(NSEQ, NBLK),
            in_specs=[blk(2), pl.BlockSpec((L, 256), lambda s, n, sk: (s, 6)),
                      pl.BlockSpec((3, L, 128), lambda s, n, sk: (0, 0, 0)), blk(0)],
            out_specs=[blk(0), pl.BlockSpec((L, 256), lambda s, n, sk: (s, 0)),
                       pl.BlockSpec((8, 128), lambda s, n, sk: (0, 0))],
            scratch_shapes=[pad, pad, pad, pad]),
        out_shape=[_sds((N, 512), MX), _sds((N, 256), MX), _sds((8, 128))],
        name="swa_bwd", compiler_params=_cp(("arbitrary", "arbitrary")))(sink, h, h, tk, dyc)


def _outproj_fwd(ya, yb, yc, x, wo, g, b):
    tm = 512

    def body(ya_ref, yb_ref, yc_ref, x_ref, wo_ref, g_ref, b_ref, s_ref, x1_ref):
        mix = _mm(ya_ref[...], wo_ref[0:256]) + _mm(yb_ref[...], wo_ref[256:512]) + _mm(yc_ref[...], wo_ref[512:1024])
        s = ALPHA * x_ref[...] + mix
        s_ref[...] = s
        x1_ref[...] = _ln_fwd(s, g_ref[...], b_ref[...])

    row = lambda w_: pl.BlockSpec((tm, w_), lambda i: (i, 0))
    one = pl.BlockSpec((1, D), lambda i: (0, 0))
    return pl.pallas_call(
        body, grid=(N // tm,),
        in_specs=[row(256), row(256), row(512), row(D), pl.BlockSpec((D, D), lambda i: (0, 0)), one, one],
        out_specs=[row(D), row(D)], out_shape=[_sds((N, D)), _sds((N, D))],
        name="outproj_fwd", compiler_params=_cp(("parallel",)))(ya, yb, yc, x, wo, g, b)


def _outproj_bwd(dx1, s1, ya, yb, yc, wo, g):
    tm = 512
    nt = N // tm

    def body(dx1_ref, s_ref, ya_ref, yb_ref, yc_ref, wo_ref, g_ref,
             dya_ref, dyb_ref, dyc_ref, dxp_ref, dwo_ref, dg_ref, db_ref, acc):
        i = pl.program_id(0)

        @pl.when(i == 0)
        def _():
            acc[...] = jnp.zeros_like(acc)
            dg_ref[...] = jnp.zeros_like(dg_ref)
            db_ref[...] = jnp.zeros_like(db_ref)

        ds, dg, db = _ln_bwd(dx1_ref[...], s_ref[...], g_ref[...])
        dg_ref[...] += dg
        db_ref[...] += db
        dxp_ref[...] = ALPHA * ds
        dy = _mm_nt(ds, wo_ref[...])
        dya_ref[...] = dy[:, 0:256]
        dyb_ref[...] = dy[:, 256:512]
        dyc_ref[...] = dy[:, 512:1024]
        acc[0:256] += _mm_tn(ya_ref[...], ds)
        acc[256:512] += _mm_tn(yb_ref[...], ds)
        acc[512:1024] += _mm_tn(yc_ref[...], ds)

        @pl.when(i == nt - 1)
        def _():
            dwo_ref[...] = acc[...].astype(MX)

    row = lambda w_: pl.BlockSpec((tm, w_), lambda i: (i, 0))
    one = pl.BlockSpec((1, D), lambda i: (0, 0))
    full = pl.BlockSpec((D, D), lambda i: (0, 0))
    return pl.pallas_call(
        body, grid=(nt,),
        in_specs=[row(D), row(D), row(256), row(256), row(512), full, one],
        out_specs=[row(256), row(256), row(512), row(D), full, one, one],
        out_shape=[_sds((N, 256)), _sds((N, 256)), _sds((N, 512)), _sds((N, D)), _sds((D, D), MX), _sds((1, D)), _sds((1, D))],
        scratch_shapes=[pltpu.VMEM((D, D), F32)],
        name="outproj_bwd", compiler_params=_cp(("arbitrary",)))(dx1, s1, ya, yb, yc, wo, g)


def _ffn_fwd(x1, w1, w2, g, b, target=None):
    tm = FFN_TM
    head = target is not None

    def body(*refs):
        x_ref, w1_ref, w2_ref, g_ref, b_ref = refs[:5]
        a_ref, s_ref, y_ref = refs[5 + head:8 + head]
        x = x_ref[...]
        xb = x.astype(MX)
        s = ALPHA * x
        for j in range(NSHARD):
            a = _mm(xb, w1_ref[j])
            a_ref[:, j * D:(j + 1) * D] = a.astype(MX)
            s = s + _mm(jnp.square(jnp.maximum(a, 0.0)), w2_ref[j])
        s_ref[...] = s
        x2 = _ln_fwd(s, g_ref[...], b_ref[...])
        if not head:
            y_ref[...] = x2
            return
        l_ref = refs[-1]

        @pl.when(pl.program_id(0) == 0)
        def _():
            l_ref[...] = jnp.zeros_like(l_ref)

        e = x2 - refs[5][...]
        y_ref[...] = e * (1.0 / D)
        l_ref[...] += jnp.sum(jnp.sum(e * e, axis=1, keepdims=True), axis=0, keepdims=True) * (0.5 / D)

    row = pl.BlockSpec((tm, D), lambda i: (i, 0))
    wall = pl.BlockSpec((NSHARD, D, D), lambda i: (0, 0, 0))
    one = pl.BlockSpec((1, D), lambda i: (0, 0))
    acc = pl.BlockSpec((8, 128), lambda i: (0, 0))
    return pl.pallas_call(
        body, grid=(N // tm,),
        in_specs=[row, wall, wall, one, one] + [row] * head,
        out_specs=[pl.BlockSpec((tm, DFF), lambda i: (i, 0)), row, row] + [acc] * head,
        out_shape=[_sds((N, DFF), MX), _sds((N, D)), _sds((N, D))] + [_sds((8, 128))] * head,
        name="ffn_fwd", compiler_params=_cp(("arbitrary",), FFN_VMEM))(x1, w1, w2, g, b, *([target] * head))


def _ffn_bwd_act(dy, s2, a, w1, w2, g):
    tm = FFN_TM

    def body(dy_ref, s_ref, a_ref, w1_ref, w2_ref, g_ref, da_ref, ds_ref, dx1_ref, dg_ref, db_ref):
        @pl.when(pl.program_id(0) == 0)
        def _():
            dg_ref[...] = jnp.zeros_like(dg_ref)
            db_ref[...] = jnp.zeros_like(db_ref)

        ds, dg, db = _ln_bwd(dy_ref[...], s_ref[...], g_ref[...])
        dsb = ds.astype(MX)
        ds_ref[...] = dsb
        dg_ref[...] += dg
        db_ref[...] += db
        dx1 = ALPHA * ds
        for j in range(NSHARD):
            da = (_mm_nt(dsb, w2_ref[j]) * 2.0 * jnp.maximum(a_ref[:, j * D:(j + 1) * D].astype(F32), 0.0)).astype(MX)
            da_ref[:, j * D:(j + 1) * D] = da
            dx1 = dx1 + _mm_nt(da, w1_ref[j])
        dx1_ref[...] = dx1

    row = pl.BlockSpec((tm, D), lambda i: (i, 0))
    wide = pl.BlockSpec((tm, DFF), lambda i: (i, 0))
    wall = pl.BlockSpec((NSHARD, D, D), lambda i: (0, 0, 0))
    one = pl.BlockSpec((1, D), lambda i: (0, 0))
    return pl.pallas_call(
        body, grid=(N // tm,),
        in_specs=[row, row, wide, wall, wall, one],
        out_specs=[wide, row, row, one, one],
        out_shape=[_sds((N, DFF), MX), _sds((N, D), MX), _sds((N, D)), _sds((1, D)), _sds((1, D))],
        name="ffn_bwd_act", compiler_params=_cp(("arbitrary",), FFN_VMEM))(dy, s2, a, w1, w2, g)


def _ffn_bwd_w(x1, da, a, ds):
    tm, nb = FFN_TM_W, FFN_WB
    nt = N // tm

    def body(x_ref, da_ref, a_ref, ds_ref, dw1_ref, dw2_ref, acc1, acc2):
        i = pl.program_id(1)

        @pl.when(i == 0)
        def _():
            acc1[...] = jnp.zeros_like(acc1)
            acc2[...] = jnp.zeros_like(acc2)

        x, ds_ = x_ref[...], ds_ref[...]
        for k in range(nb):
            cols = slice(k * D, (k + 1) * D)
            acc1[k] += _mm_tn(x, da_ref[:, cols])
            acc2[k] += _mm_tn(jnp.square(jnp.maximum(a_ref[:, cols].astype(F32), 0.0)), ds_)

        @pl.when(i == nt - 1)
        def _():
            dw1_ref[...] = acc1[...].astype(MX)
            dw2_ref[...] = acc2[...].astype(MX)

    row = pl.BlockSpec((tm, D), lambda j, i: (i, 0))
    col = pl.BlockSpec((tm, nb * D), lambda j, i: (i, j))
    wj = pl.BlockSpec((nb, D, D), lambda j, i: (j, 0, 0))
    return pl.pallas_call(
        body, grid=(NSHARD // nb, nt),
        in_specs=[row, col, col, row], out_specs=[wj, wj],
        out_shape=[_sds((NSHARD, D, D), MX), _sds((NSHARD, D, D), MX)],
        scratch_shapes=[pltpu.VMEM((nb, D, D), F32), pltpu.VMEM((nb, D, D), F32)],
        name="ffn_bwd_w", compiler_params=_cp(("parallel", "arbitrary"), FFN_VMEM))(x1, da, a, ds)


def _loss_head(y, target):
    tm = 512

    def body(y_ref, t_ref, dy_ref, l_ref):
        @pl.when(pl.program_id(0) == 0)
        def _():
            l_ref[...] = jnp.zeros_like(l_ref)

        e = y_ref[...] - t_ref[...]
        dy_ref[...] = e * (1.0 / D)
        l_ref[...] += jnp.sum(jnp.sum(e * e, axis=1, keepdims=True), axis=0, keepdims=True) * (0.5 / D)

    row = pl.BlockSpec((tm, D), lambda i: (i, 0))
    return pl.pallas_call(
        body, grid=(N // tm,), in_specs=[row, row],
        out_specs=[row, pl.BlockSpec((8, 128), lambda i: (0, 0))],
        out_shape=[_sds((N, D)), _sds((8, 128))], name="loss_head", compiler_params=_cp(("arbitrary",)))(y, target)


def _s5_discretize(a_re, a_im, log_step, b_re, b_im):
    lam = lax.complex(a_re, a_im)
    lam_bar = jnp.exp(lam * jnp.exp(log_step))
    b_bar = ((lam_bar - 1.0) / lam)[..., None] * lax.complex(b_re, b_im)
    return jnp.real(lam_bar), jnp.imag(lam_bar), jnp.real(b_bar), jnp.imag(b_bar)


def _s5_in_blocks(b):
    e = jnp.eye(8, dtype=F32)
    return jnp.einsum('ij,zbjph->zbihjp', e, b.reshape(2, 2, 8, S5_P, S5_H)).reshape(2, 2, 128, SW)


def _s5_in_unblocks(d):
    return jnp.einsum('zbihip->zbiph', d.reshape(2, 2, 8, S5_H, 8, S5_P)).reshape(2, S5_G, S5_P, S5_H)


def _s5_out_blocks(c):
    e = jnp.eye(8, dtype=F32)
    return jnp.einsum('ij,zbjhp->zbjpih', e, c.reshape(2, 2, 8, S5_H, S5_P)).reshape(2, 2, SW, 128)


def _s5_out_unblocks(d):
    return jnp.einsum('zbipih->zbihp', d.reshape(2, 2, 8, S5_P, 8, S5_H)).reshape(2, S5_G, S5_H, S5_P)


def _gate_weight(w_a):
    z = jnp.zeros((16, 128), F32)
    top = jnp.concatenate([w_a[0], z], axis=1)
    bot = jnp.concatenate([z, w_a[1]], axis=1)
    return jnp.concatenate([top, bot, jnp.zeros((96, 256), F32)], axis=0)


def _layer_prep(p):
    lr, li, br, bi = _s5_discretize(p["s5_a_re"], p["s5_a_im"], p["s5_log_step"], p["s5_b_re"], p["s5_b_im"])
    q = dict(p)
    q["bre"] = _s5_in_blocks(br).astype(MX)
    q["bim"] = _s5_in_blocks(bi).astype(MX)
    q["cre"] = _s5_out_blocks(p["s5_c_re"]).astype(MX)
    q["cim"] = _s5_out_blocks(p["s5_c_im"]).astype(MX)
    mr, mi = lr.reshape(2, 1024), li.reshape(2, 1024)
    both = lambda t0, t1: tuple(jnp.stack(p) for p in zip(t0, t1))
    q["tab"] = both(_lockstep_tables(mr[0], mi[0], False), _lockstep_tables(mr[1], mi[1], True))
    q["tabc"] = both(_lockstep_tables(mr[0], -mi[0], True), _lockstep_tables(mr[1], -mi[1], False))
    q["dsk"] = p["s5_d"].reshape(1, 256)
    q["wa"] = _gate_weight(p["gla_w_a"]).astype(MX)
    q["ba"] = p["gla_b_a"].reshape(1, 256)
    q["lng"] = p["gla_ln_g"].reshape(1, 256)
    q["bv"] = p["s5_b_glu"][:256].reshape(1, 256)
    q["bg"] = p["s5_b_glu"][256:].reshape(1, 256)
    for k in ("ln1_g", "ln1_b", "ln2_g", "ln2_b"):
        q[k] = p[k].reshape(1, D)
    return q


def _layer_fwd(x, q, tk, fetch, target=None):
    q["w_in"] = fetch("w_in", x)
    h = _inproj_fwd(x, q["w_in"])
    hre, him, y2 = _s5_fwd(h, q["bre"], q["bim"], q["cre"], q["cim"], q["tab"])
    q["w4"] = fetch("s5_w_glu", y2)
    ya = _s5_glu_fwd(y2, h, q["dsk"], q["w4"], q["bv"], q["bg"])
    la2 = _gla_gate_fwd(h, q["wa"], q["ba"])
    of, ob, sf, sb = _gla_fwd(h, la2)
    yb = _gla_post_fwd(of, ob, h, q["lng"])
    yc = _swa_fwd(h, tk, q["swa_sink"])
    q["w_out"] = fetch("w_out", yc)
    s1, x1 = _outproj_fwd(ya, yb, yc, x, q["w_out"], q["ln1_g"], q["ln1_b"])
    q["w_ff1"] = fetch("w_ff1", x1)
    q["w_ff2"] = fetch("w_ff2", x1)
    a, s2, *out = _ffn_fwd(x1, q["w_ff1"], q["w_ff2"], q["ln2_g"], q["ln2_b"], target)
    saved = dict(x=x, h=h, hre=hre, him=him, y2=y2, ya=ya, la2=la2, of=of, ob=ob, sf=sf, sb=sb, yb=yb, yc=yc,
                 s1=s1, x1=x1, a=a, s2=s2)
    return (out[0] if target is None else tuple(out)), saved


def _layer_bwd(dy, q, sv, tk, emit):
    g = {}
    da, ds2, dx1, g["dg2"], g["db2"] = _ffn_bwd_act(dy, sv["s2"], sv["a"], q["w_ff1"], q["w_ff2"], q["ln2_g"])
    dw1, dw2 = _ffn_bwd_w(sv["x1"], da, sv["a"], ds2)
    tie = emit(dict(w_ff1=dw1, w_ff2=dw2))
    dya, dyb, dyc, dxp, dwo, g["dg1"], g["db1"] = _outproj_bwd(dx1, sv["s1"], sv["ya"], sv["yb"], sv["yc"],
                                                               q["w_out"], q["ln1_g"] + tie)
    h = sv["h"]
    daq, dakv, g["dsink"] = _swa_bwd(h, tk, q["swa_sink"], dyc)
    do, gr, g["dlng"] = _gla_post_bwd(sv["of"], sv["ob"], h, q["lng"], dyb)
    gq_f, gk_f, gv_f, gl_f, gq_b, gk_b, gv_b, gl_b = _gla_bwd(h, sv["la2"], do, sv["sf"], sv["sb"])
    dhl, g["dwa"], g["dba"] = _gla_gate_bwd(h, q["wa"], q["ba"], gl_f, gl_b)
    dyp, dud, g["dd"], dw4, g["dbv"], g["dbg"] = _s5_glu_bwd(sv["y2"], h, q["dsk"], q["w4"], q["bv"], q["bg"], dya)
    tie = emit(dict(w_out=dwo.reshape(NSHARD, D // NSHARD, D), s5_w_glu=dw4))
    du2, g["dbre"], g["dbim"], g["dcre"], g["dcim"], g["dmu"] = _s5_bwd(
        h, dyp, sv["hre"], sv["him"], q["bre"], q["bim"], q["cre"], q["cim"], (q["tabc"][0], q["tabc"][1] + tie))
    dx, dwt = _inproj_bwd(sv["x"], q["w_in"], dxp, du2, dud, gq_f, gq_b, gk_f, gk_b, gv_f, gv_b, gr, daq, dakv, dhl)
    tie = emit(dict(w_in=dwt))
    return dx, g, tie


NATIVE = ("dmu", "dbre", "dbim", "dcre", "dcim", "dd", "dbv", "dbg", "dwa", "dba", "dlng", "dsink",
          "dg1", "db1", "dg2", "db2", "loss")
ICI_CORE = (0, 0, 0, 1, 1, 0, 0, 0, 1, 1, 1, 1, 0, 0, 1, 1, 0)


def _finish_small(n, w):
    g = {}
    dmu = n["dmu"]
    dlr = dmu[:, :, :, 0].reshape(DEPTH, 2, S5_G, S5_P)
    dli = dmu[:, :, :, 1].reshape(DEPTH, 2, S5_G, S5_P)

    def unblock(c, perm, shape):
        return c.reshape(DEPTH, 2, 2, S5_H, 8, S5_P).transpose(perm).reshape(shape)

    b_shape, c_shape = (DEPTH, 2, S5_G, S5_P, S5_H), (DEPTH, 2, S5_G, S5_H, S5_P)
    _, vjp = jax.vjp(_s5_discretize, w["s5_a_re"], w["s5_a_im"], w["s5_log_step"], w["s5_b_re"], w["s5_b_im"])
    (g["s5_a_re"], g["s5_a_im"], g["s5_log_step"], g["s5_b_re"], g["s5_b_im"]) = vjp(
        (dlr, dli, unblock(n["dbre"], (0, 1, 2, 4, 5, 3), b_shape), unblock(n["dbim"], (0, 1, 2, 4, 5, 3), b_shape)))
    g["s5_c_re"] = unblock(n["dcre"], (0, 1, 2, 4, 3, 5), c_shape)
    g["s5_c_im"] = unblock(n["dcim"], (0, 1, 2, 4, 3, 5), c_shape)
    g["s5_d"] = n["dd"].reshape(DEPTH, S5_G, S5_H)
    g["s5_b_glu"] = jnp.concatenate([n["dbv"], n["dbg"]], axis=2).reshape(DEPTH, 512)
    g["gla_w_a"] = jnp.stack([n["dwa"][:, 0:16, 0:128], n["dwa"][:, 16:32, 128:256]], axis=1)
    g["gla_b_a"] = n["dba"].reshape(DEPTH, 2, 128)
    g["gla_ln_g"] = n["dlng"].reshape(DEPTH, 256)
    g["swa_sink"] = n["dsink"][:, :, 0]
    for k, s in (("ln1_g", "dg1"), ("ln1_b", "db1"), ("ln2_g", "dg2"), ("ln2_b", "db2")):
        g[k] = n[s].reshape(DEPTH, D)
    return g


def _local_step(x, target, qs, tk, fetch, emit):
    saved = []
    for l, q in enumerate(qs):
        x, sv = _layer_fwd(x, q, tk, functools.partial(fetch, l), target if l == DEPTH - 1 else None)
        saved.append(sv)
    dy, lacc = x
    smalls = [None] * DEPTH
    tie = 0.0
    for l in reversed(range(DEPTH)):
        qs[l]["ln2_g"] = qs[l]["ln2_g"] + tie
        dy, smalls[l], tie = _layer_bwd(dy, qs[l], saved[l], tk, functools.partial(emit, l))
    smalls[0]["db2"] = smalls[0]["db2"] + tie
    for l in range(DEPTH):
        smalls[l]["loss"] = lacc if l == 0 else jnp.zeros_like(lacc)
    return lacc[0, 0], dy, smalls


BIG = ("w_in", "s5_w_glu", "w_out", "w_ff1", "w_ff2")
SMALL = ("s5_a_re", "s5_a_im", "s5_log_step", "s5_b_re", "s5_b_im", "s5_c_re", "s5_c_im", "s5_d", "s5_b_glu",
         "gla_w_a", "gla_b_a", "gla_ln_g", "swa_sink", "ln1_g", "ln1_b", "ln2_g", "ln2_b")
ANY = pl.BlockSpec(memory_space=pl.ANY)


def _place():
    x, y, c = lax.axis_index("x"), lax.axis_index("y"), lax.axis_index("c")
    return x, y, c, [(1 - x, y), (x, 1 - y), (1 - x, 1 - y)]


HBM = pl.BlockSpec(memory_space=pltpu.HBM)
SEMS = pl.BlockSpec(memory_space=pltpu.SEMAPHORE)
EFFECT = pltpu.SideEffectType.DATAFLOW_SIDE_EFFECTING


def _push_copies(ins, lands, send, recv, gather, sending):
    x, y, c, chips = _place()
    me = 2 * x + y
    if gather == "sibling":
        return [pltpu.make_async_remote_copy(src_ref=ins[a], dst_ref=lands[a], send_sem=send.at[a], recv_sem=recv.at[a],
                                             device_id=(x, y, 1 - c), device_id_type=MESH) for a in range(len(lands))]
    out = []
    for a in range(len(lands)):
        for j, (px, py) in enumerate(chips):
            peer = 2 * px + py
            src = lands[a].at[me] if gather else ins[a].at[peer if sending else me]
            dst = lands[a].at[me if sending else peer]
            out.append(pltpu.make_async_remote_copy(src_ref=src, dst_ref=dst, send_sem=send.at[3 * a + j],
                                                    recv_sem=recv.at[3 * a + j], device_id=(px, py, c),
                                                    device_id_type=MESH))
    return out


def _push_start(name, arrs, gather):
    n = len(arrs)
    ops = list(arrs) if gather is True else list(arrs) + [lax.empty(s.shape, s.dtype) for s in arrs]
    m = len(ops)

    def body(*refs):
        ins, lnd = (refs[:n], refs[:n]) if gather is True else (refs[:n], refs[n:m])
        for cp in _push_copies(ins, lnd, refs[m], refs[m + 1], gather, True):
            cp.start()
        refs[-1][...] = jnp.zeros((8, 128), F32)

    ops = [pltpu.with_memory_space_constraint(t, pltpu.HBM) for t in ops]
    res = pl.pallas_call(
        body, name=name,
        out_shape=(pltpu.SemaphoreType.DMA((3 * n,)), pltpu.SemaphoreType.DMA((3 * n,)),
                   *[pltpu.HBM(t.shape, t.dtype) for t in ops], _sds((8, 128))),
        in_specs=[HBM] * m,
        out_specs=(SEMS, SEMS, *[HBM] * m, pl.BlockSpec(memory_space=pltpu.VMEM)),
        input_output_aliases={i: 2 + i for i in range(m)},
        compiler_params=pltpu.CompilerParams(has_side_effects=EFFECT))(*ops)
    return res[0], res[1], list(res[2:2 + m]), res[-1]


def _push_wait(name, started, after, gather):
    send, recv, ops, _ = started
    m = len(ops)
    n = m if gather is True else m // 2

    def body(*refs):
        ins, lnd = (refs[:n], refs[:n]) if gather is True else (refs[:n], refs[n:m])
        for cp in _push_copies(ins, lnd, refs[m], refs[m + 1], gather, False):
            cp.wait_send()
            cp.wait_recv()

    res = pl.pallas_call(
        body, name=name,
        out_shape=[pltpu.HBM(t.shape, t.dtype) for t in ops],
        in_specs=[HBM] * m + [SEMS, SEMS, ANY], out_specs=[HBM] * m,
        input_output_aliases={i: i for i in range(m)},
        compiler_params=pltpu.CompilerParams(has_side_effects=EFFECT))(*ops, send, recv, after)
    return list(res)


def _row_tile(rows):
    return max(t for t in range(8, min(rows, 512) + 1, 8) if rows % t == 0)


def _cast_to_slot(me, w, l):
    _, rows, cols = w.shape
    tr = _row_tile(rows)

    def body(me_ref, w_ref, o_ref):
        o_ref[0] = w_ref[0].astype(MX)

    return pl.pallas_call(
        body,
        grid_spec=pltpu.PrefetchScalarGridSpec(
            num_scalar_prefetch=1, grid=(rows // tr,),
            in_specs=[pl.BlockSpec((1, tr, cols), lambda i, me_: (l, i, 0))],
            out_specs=pl.BlockSpec((1, tr, cols), lambda i, me_: (me_[0], i, 0))),
        out_shape=_sds((NSHARD, rows, cols), MX), name="cast_to_slot", compiler_params=_cp(("arbitrary",)))(me, w)


def _sum_sources(me, recv, own):
    _, rows, cols = recv[0].shape
    tr = min(_row_tile(rows), 256) if rows % 256 == 0 else _row_tile(rows)
    nt = rows // tr

    def body(me_ref, *refs):
        o_ref = refs[-1]
        for l in range(DEPTH):
            @pl.when(pl.program_id(0) == l)
            def _():
                r_ref, own_ref = refs[2 * l], refs[2 * l + 1]
                part = [jnp.where(me_ref[0] == s, own_ref[0], r_ref[s]).astype(F32) for s in range(NSHARD)]
                o_ref[...] = ((part[0] + part[1]) + part[2]) + part[3]

    in_specs = []
    for l in range(DEPTH):
        pick = lambda g, i, me_, l=l: jnp.where(g == l, i, jnp.where(g < l, 0, nt - 1))
        in_specs += [pl.BlockSpec((NSHARD, tr, cols), lambda g, i, me_, pick=pick: (0, pick(g, i, me_), 0)),
                     pl.BlockSpec((1, tr, cols), lambda g, i, me_, pick=pick: (me_[0], pick(g, i, me_), 0))]
    return pl.pallas_call(
        body,
        grid_spec=pltpu.PrefetchScalarGridSpec(
            num_scalar_prefetch=1, grid=(DEPTH, nt), in_specs=in_specs,
            out_specs=pl.BlockSpec((tr, cols), lambda g, i, me_: (g * nt + i, 0))),
        out_shape=_sds((DEPTH * rows, cols)), name="sum_sources",
        compiler_params=_cp(("arbitrary", "arbitrary")))(me, *[t for l in range(DEPTH) for t in (recv[l], own[l])])


def _swap_sibling(arrs):
    n = len(arrs)

    def body(*refs):
        ins, outs = refs[:n], refs[n:2 * n]
        send, recv = refs[2 * n:]
        x, y, c, _ = _place()
        cps = [pltpu.make_async_remote_copy(src_ref=ins[a], dst_ref=outs[a], send_sem=send.at[a], recv_sem=recv.at[a],
                                            device_id=(x, y, 1 - c), device_id_type=MESH) for a in range(n)]
        for cp in cps:
            cp.start()
        for cp in cps:
            cp.wait()

    return pl.pallas_call(
        body, in_specs=[ANY] * n, out_specs=[ANY] * n, out_shape=[_sds(a.shape, a.dtype) for a in arrs],
        scratch_shapes=[pltpu.SemaphoreType.DMA((n,)), pltpu.SemaphoreType.DMA((n,))],
        name="swap_sibling")(*arrs)


def _allreduce_small(per_layer):
    nk = len(per_layer[0])
    n = DEPTH * nk
    shapes = [a.shape for a in per_layer[0]]

    def body(*refs):
        ins, outs = refs[:n], refs[n:n + nk]
        sibs, slots = refs[n + nk:n + 2 * nk], refs[n + 2 * nk:n + 3 * nk]
        send, recv = refs[n + 3 * nk:]
        x, y, c, chips = _place()
        me = 2 * x + y
        d2d = [pltpu.make_async_remote_copy(src_ref=ins[l * nk + k], dst_ref=sibs[k].at[l], send_sem=send.at[l * nk + k],
                                            recv_sem=recv.at[l * nk + k], device_id=(x, y, 1 - c), device_id_type=MESH)
               for l in range(DEPTH) for k in range(nk)]
        for cp in d2d:
            cp.start()
        for cp in d2d:
            cp.wait()
        for l in range(DEPTH):
            for k in range(nk):
                slots[k][0, l] = ins[l * nk + k][...] + sibs[k][l]

        def swap(k, stage):
            peer = (1 - x, y, c) if stage == 0 else (x, 1 - y, c)
            return pltpu.make_async_remote_copy(src_ref=slots[k].at[2 * stage], dst_ref=slots[k].at[2 * stage + 1],
                                                send_sem=send.at[n + 3 * k + stage], recv_sem=recv.at[n + 3 * k + stage],
                                                device_id=peer, device_id_type=MESH)

        def handover(k):
            return pltpu.make_async_remote_copy(src_ref=outs[k], dst_ref=outs[k], send_sem=send.at[n + 3 * nk + k],
                                                recv_sem=recv.at[n + 3 * nk + k], device_id=(x, y, 1 - c),
                                                device_id_type=MESH)

        halves = (tuple(k for k in range(nk) if ICI_CORE[k] == 0), tuple(k for k in range(nk) if ICI_CORE[k] == 1))
        for cc in range(2):
            @pl.when(c == cc)
            def _():
                mine, theirs = halves[cc], halves[1 - cc]
                for stage in range(2):
                    cps = [swap(k, stage) for k in mine]
                    for cp in cps:
                        cp.start()
                    for cp in cps:
                        cp.wait()
                    for k in mine:
                        if stage == 0:
                            slots[k][2] = slots[k][0] + slots[k][1]
                        else:
                            outs[k][...] = slots[k][2] + slots[k][3]
                over = [handover(k) for k in mine]
                for cp in over:
                    cp.start()
                for k in theirs:
                    handover(k).wait_recv()
                for cp in over:
                    cp.wait_send()

    vm = pl.BlockSpec(memory_space=pltpu.VMEM)
    return pl.pallas_call(
        body, in_specs=[vm] * n, out_specs=[vm] * nk, out_shape=[_sds((DEPTH,) + s) for s in shapes],
        scratch_shapes=([pltpu.VMEM((DEPTH,) + s, F32) for s in shapes]
                        + [pltpu.VMEM((NSHARD, DEPTH) + s, F32) for s in shapes]
                        + [pltpu.SemaphoreType.DMA((n + 4 * nk,)), pltpu.SemaphoreType.DMA((n + 4 * nk,))]),
        name="allreduce_small", compiler_params=pltpu.CompilerParams(vmem_limit_bytes=VMEM_LIMIT))(
            *[a for layer in per_layer for a in layer])


def _adamw_math(w, g, m, v):
    m = ADAM_B1 * m + (1.0 - ADAM_B1) * g
    v = ADAM_B2 * v + (1.0 - ADAM_B2) * jnp.square(g)
    m_hat = m / (1.0 - ADAM_B1 ** ADAM_STEP)
    v_hat = v / (1.0 - ADAM_B2 ** ADAM_STEP)
    delta = -ADAM_LR * (m_hat / (jnp.sqrt(v_hat) + ADAM_EPS) + ADAM_WD * w)
    return delta, m, v


def _adamw(g_parts, w, m, v):
    rows, cols = w.shape
    tr = 256 if rows % 256 == 0 else _row_tile(rows)
    k = len(g_parts)

    def body(*refs):
        g = refs[0][...]
        for r in refs[1:k]:
            g = g + r[...]
        w_ref, m_ref, v_ref, go, do, mo, vo = refs[k:]
        d, mn, vn = _adamw_math(w_ref[...], g, m_ref[...], v_ref[...])
        go[...] = g
        do[...] = d
        mo[...] = mn
        vo[...] = vn

    spec = pl.BlockSpec((tr, cols), lambda i: (i, 0))
    return pl.pallas_call(
        body, grid=(rows // tr,), in_specs=[spec] * (k + 3), out_specs=[spec] * 4,
        out_shape=[_sds((rows, cols))] * 4, name="adamw", compiler_params=_cp(("parallel",)))(*g_parts, w, m, v)


def _adamw_small(gs, ws, ms, vs):
    n = len(gs)

    def body(*refs):
        for k in range(n):
            d, mn, vn = _adamw_math(refs[n + k][...], refs[k][...], refs[2 * n + k][...], refs[3 * n + k][...])
            refs[4 * n + k][...] = d
            refs[5 * n + k][...] = mn
            refs[6 * n + k][...] = vn

    vm = pl.BlockSpec(memory_space=pltpu.VMEM)
    shapes = [_sds(a.shape) for a in ws]
    res = pl.pallas_call(
        body, in_specs=[vm] * (4 * n), out_specs=[vm] * (3 * n), out_shape=shapes * 3, name="adamw_small",
        compiler_params=pltpu.CompilerParams(vmem_limit_bytes=VMEM_LIMIT))(*gs, *ws, *ms, *vs)
    return res[:n], res[n:2 * n], res[2 * n:]


_ARGS = ("x", "w_in", "s5_a_re", "s5_a_im", "s5_log_step", "s5_b_re", "s5_b_im", "s5_c_re", "s5_c_im", "s5_d",
         "s5_w_glu", "s5_b_glu", "gla_w_a", "gla_b_a", "gla_ln_g", "swa_sink", "w_out", "ln1_g", "ln1_b", "w_ff1",
         "w_ff2", "ln2_g", "ln2_b")
_WEIGHTS = _ARGS[1:]


def _shard_cols(d):
    return d.reshape(d.shape[0], NSHARD, d.shape[1] // NSHARD).transpose(1, 0, 2)


def kernel(x, w_in, s5_a_re, s5_a_im, s5_log_step, s5_b_re, s5_b_im, s5_c_re, s5_c_im, s5_d, s5_w_glu, s5_b_glu, gla_w_a, gla_b_a, gla_ln_g, swa_sink, w_out, ln1_g, ln1_b, w_ff1, w_ff2, ln2_g, ln2_b, loss_target, m_w_in, m_s5_a_re, m_s5_a_im, m_s5_log_step, m_s5_b_re, m_s5_b_im, m_s5_c_re, m_s5_c_im, m_s5_d, m_s5_w_glu, m_s5_b_glu, m_gla_w_a, m_gla_b_a, m_gla_ln_g, m_swa_sink, m_w_out, m_ln1_g, m_ln1_b, m_w_ff1, m_w_ff2, m_ln2_g, m_ln2_b, v_w_in, v_s5_a_re, v_s5_a_im, v_s5_log_step, v_s5_b_re, v_s5_b_im, v_s5_c_re, v_s5_c_im, v_s5_d, v_s5_w_glu, v_s5_b_glu, v_gla_w_a, v_gla_b_a, v_gla_ln_g, v_swa_sink, v_w_out, v_ln1_g, v_ln1_b, v_w_ff1, v_w_ff2, v_ln2_g, v_ln2_b):
    given = dict(locals())
    w = {k: given[k] for k in _WEIGHTS}
    mom = {k: given["m_" + k] for k in _WEIGHTS}
    var = {k: given["v_" + k] for k in _WEIGHTS}

    me = (2 * lax.axis_index("x") + lax.axis_index("y")).astype(jnp.int32).reshape(1)
    tr = lambda t: t.transpose(0, 2, 1)
    shard = {k: (tr(w[k]) if k == "w_in" else w[k]) for k in BIG}
    qs = [None] * DEPTH

    first = ("w_in", "s5_w_glu", "w_out")
    follow = {(0, "w_in"): [(0, BIG[3:]), (1, first)], (0, "w_ff1"): [(1, BIG[3:])]}
    gathers = {}

    def start_gather(l, names, behind=None):
        lands = [_cast_to_slot(me, shard[k], l) for k in names]
        if behind is not None:
            lands, behind = lax.optimization_barrier((lands, behind))
        st = _push_start(f"gather_start_{l}_{names[0]}", lands, True)
        for k in names:
            gathers[l, k] = [names, st, None]
        return st[-1], behind

    token = start_gather(0, first[:1])[0] + start_gather(0, first[1:])[0]
    zero = token[0, 0]
    for l in range(DEPTH):
        qs[l] = _layer_prep({k: (w[k][l] + zero if k == "s5_a_re" else w[k][l]) for k in SMALL})
        token = token + qs[l]["tabc"][1][0, 0, 0, :8, :128] + qs[l]["bre"][0, 0, :8, :128].astype(F32)

    def fetch(l, name, after):
        names, st, got = gathers[l, name]
        tie = None
        if got is None:
            if l == 0 and name == "w_in":
                after = token
            lands = _push_wait(f"gather_wait_{l}_{names[0]}", st, after, True)
            for l2, names2 in follow.get((l, name), ()):
                tok, lands[0] = start_gather(l2, names2, lands[0])
                tie = tok if tie is None else tie + tok
            got = dict(zip(names, lands))
            for k in names:
                gathers[l, k][2] = got
        full = got[name]
        if name == "w_in":
            return _in_rows(full, token if tie is None else tie)
        if tie is not None:
            qs[l]["ln2_b"] = qs[l]["ln2_b"] + tie[0, 0]
        return full.reshape(D, D) if name == "w_out" else full

    scatters, held = [], {}

    def emit(l, grads):
        if l > 0:
            held.update(grads)
            if "w_in" not in grads:
                return 0.0
            grads = dict(held)
            held.clear()
        names = tuple(grads)
        st = _push_start(f"scatter_start_{l}_{names[0]}", [grads[k] for k in names], False)
        scatters.append((l, names, st))
        return st[-1][0, 0]

    loss, dx, smalls = _local_step(x.reshape(N, D), loss_target.reshape(N, D), qs, _rope_tables(128), fetch, emit)

    out = {}
    native = _allreduce_small([[smalls[l][k] for k in NATIVE] for l in range(DEPTH)])
    native = dict(zip(NATIVE, native))
    loss = native["loss"][0, 0, 0] + native["loss"][1, 0, 0]
    gsmall = _finish_small(native, w)
    res = _adamw_small(*([t[k] for k in SMALL] for t in (gsmall, w, mom, var)))
    for i, k in enumerate(SMALL):
        out[k] = [gsmall[k], res[0][i], res[1][i], res[2][i]]

    recv, own = {}, {}

    def collect(keys, after):
        for l, names, st in scatters:
            if names[0] in keys:
                ops = _push_wait(f"scatter_wait_{l}_{names[0]}", st, after, False)
                for i, k in enumerate(names):
                    own[l, k], recv[l, k] = ops[i], ops[len(names) + i]

    def to_sibling(keys):
        sums = [_sum_sources(me, [recv[l, k] for l in range(DEPTH)], [own[l, k] for l in range(DEPTH)]) for k in keys]
        return _push_start(f"swap_start_{keys[0]}", sums, "sibling")

    def apply(keys, started, after):
        ops = _push_wait(f"swap_wait_{keys[0]}", started, after, "sibling")
        for i, k in enumerate(keys):
            mine, other = ops[i], ops[len(keys) + i]
            shp = shard[k].shape
            r = _adamw([mine, other], *((tr(t[k]) if k == "w_in" else t[k]).reshape(-1, shp[-1]) for t in (w, mom, var)))
            r = [t.reshape(shp) for t in r]
            out[k] = [tr(t) for t in r] if k == "w_in" else r
        return out[keys[-1]][1]

    collect(("w_ff1", "w_ff2", "w_out", "s5_w_glu"), res[0][-1])
    ff = to_sibling(("w_ff1", "w_ff2"))
    mix = to_sibling(("w_out", "s5_w_glu"))
    last = apply(("w_ff1", "w_ff2"), ff, mix[-1])
    collect(("w_in",), last)
    win = to_sibling(("w_in",))
    last = apply(("w_out", "s5_w_glu"), mix, win[-1])
    apply(("w_in",), win, last)

    return (loss, dx.reshape(NSEQ, L, D), *[out[k][0] for k in _WEIGHTS], *[out[k][1] for k in _WEIGHTS],
            *[out[k][2] for k in _WEIGHTS], *[out[k][3] for k in _WEIGHTS])
```

```python
import functools
import math

import jax
import jax.numpy as jnp
from jax import lax
from jax.experimental import pallas as pl
from jax.experimental.pallas import tpu as pltpu

F32 = jnp.float32
MX = jnp.bfloat16
MESH = pl.DeviceIdType.MESH

DEPTH = 2
NSEQ = 2
L = 2048
N = NSEQ * L
D = 1024
DFF = 4096
NSHARD = 4
S5_G, S5_H, S5_P = 16, 16, 64
GLA_CHUNK = 64
NCHUNK = L // GLA_CHUNK
GLA_GROUP = 4
NGROUP = NCHUNK // GLA_GROUP
SWA_BLK = 128
NBLK = L // SWA_BLK
ROT = 16
ROPE_THETA = 500000.0
LN_EPS = 1e-5
ALPHA = (2 * DEPTH) ** 0.25
NEG_BIG = -1e30
DIN = 1824
DINP = 1920
ADAM_LR, ADAM_B1, ADAM_B2, ADAM_EPS, ADAM_WD, ADAM_STEP = 0.001, 0.9, 0.999, 1e-08, 0.01, 10
VMEM_LIMIT = 56 * 1024 * 1024
TT = 512
SW = 512
FFN_TM = 512
FFN_TM_W = 1024
FFN_WB = 1
FFN_VMEM = 60 * 1024 * 1024
INPROJ_BWD_TM = 512


def _cp(sem, vmem=VMEM_LIMIT):
    return pltpu.CompilerParams(dimension_semantics=sem, vmem_limit_bytes=vmem)


def _mm(a, b):
    return jnp.dot(a.astype(MX), b.astype(MX), preferred_element_type=F32)


def _mm_nt(a, b):
    return lax.dot_general(a.astype(MX), b.astype(MX), (((1,), (1,)), ((), ())), preferred_element_type=F32)


def _mm_tn(a, b):
    return lax.dot_general(a.astype(MX), b.astype(MX), (((0,), (0,)), ((), ())), preferred_element_type=F32)


@jax.custom_vjp
def _dmm(a, b):
    return _mm(a, b)


_dmm.defvjp(lambda a, b: (_mm(a, b), (a, b)), lambda r, g: (_mm_nt(g, r[1]), _mm_tn(r[0], g)))


@jax.custom_vjp
def _dmm_nt(a, b):
    return _mm_nt(a, b)


_dmm_nt.defvjp(lambda a, b: (_mm_nt(a, b), (a, b)), lambda r, g: (_mm(g, r[1]), _mm_tn(g, r[0])))


@jax.custom_vjp
def _dmm_tn(a, b):
    return _mm_tn(a, b)


_dmm_tn.defvjp(lambda a, b: (_mm_tn(a, b), (a, b)), lambda r, g: (_mm_nt(r[1], g), _mm(r[0], g)))


def _split3(x):
    hi = x.astype(MX)
    r1 = x - hi.astype(F32)
    mid = r1.astype(MX)
    lo = (r1 - mid.astype(F32)).astype(MX)
    return hi, mid, lo


def _chunk_pairs(rows, rev, strict):
    r = lax.broadcasted_iota(jnp.int32, (rows, rows), 0)
    c = lax.broadcasted_iota(jnp.int32, (rows, rows), 1)
    order = ((c > r) if strict else (c >= r)) if rev else ((c < r) if strict else (c <= r))
    return (r // GLA_CHUNK == c // GLA_CHUNK) & order


def _cums_impl(x, rev):
    rows, w = x.shape
    t = jnp.where(_chunk_pairs(rows, rev, False), 1.0, 0.0).astype(MX)
    s = jnp.dot(t, jnp.concatenate(_split3(x), axis=1), preferred_element_type=F32)
    return s[:, 0:w] + s[:, w:2 * w] + s[:, 2 * w:3 * w]


@functools.partial(jax.custom_vjp, nondiff_argnums=(1,))
def _cums(x, rev):
    return _cums_impl(x, rev)


_cums.defvjp(lambda x, rev: (_cums_impl(x, rev), None), lambda rev, r, g: (_cums_impl(g, not rev),))


def _ln_fwd(s, g, b):
    mu = jnp.mean(s, axis=-1, keepdims=True)
    xc = s - mu
    var = jnp.mean(xc * xc, axis=-1, keepdims=True)
    return xc * lax.rsqrt(var + LN_EPS) * g + b


def _ln_bwd(dy, s, g):
    mu = jnp.mean(s, axis=-1, keepdims=True)
    xc = s - mu
    var = jnp.mean(xc * xc, axis=-1, keepdims=True)
    rstd = lax.rsqrt(var + LN_EPS)
    xhat = xc * rstd
    dxh = dy * g
    ds = rstd * (dxh - jnp.mean(dxh, axis=-1, keepdims=True) - xhat * jnp.mean(dxh * xhat, axis=-1, keepdims=True))
    return ds, jnp.sum(dy * xhat, axis=0, keepdims=True), jnp.sum(dy, axis=0, keepdims=True)


def _sds(shape, dtype=F32):
    return jax.ShapeDtypeStruct(shape, dtype)


_IN_ROW_PIECES = (((0, 0), (0, 456)), ((1, 0), (456, 456)), ((2, 0), (912, 112)), ((2, 112), (1792, 32)),
                  ((2, 144), (1024, 312)), ((3, 0), (1336, 456)))


def _in_rows(g4, behind):
    def body(g_ref, behind_ref, o_ref, tmp):
        tmp[DIN:DINP] = jnp.zeros((DINP - DIN, D), F32)
        for (j, s0), (d0, n_) in _IN_ROW_PIECES:
            tmp[d0:d0 + n_] = g_ref[j, s0:s0 + n_].astype(F32)
        o_ref[...] = tmp[...].astype(MX)

    vm = pl.BlockSpec(memory_space=pltpu.VMEM)
    return pl.pallas_call(body, in_specs=[vm, pl.BlockSpec(memory_space=pl.ANY)], out_specs=vm,
                          out_shape=_sds((DINP, D), MX), scratch_shapes=[pltpu.VMEM((DINP, D), F32)], name="in_rows",
                          compiler_params=pltpu.CompilerParams(vmem_limit_bytes=VMEM_LIMIT))(g4, behind)


def _inproj_fwd(x, wt):
    tm = 512

    def body(x_ref, w_ref, h_ref):
        h_ref[...] = _mm_nt(x_ref[...], w_ref[...])

    return pl.pallas_call(
        body, grid=(N // tm,),
        in_specs=[pl.BlockSpec((tm, D), lambda i: (i, 0)), pl.BlockSpec((DINP, D), lambda i: (0, 0))],
        out_specs=pl.BlockSpec((tm, DINP), lambda i: (i, 0)),
        out_shape=_sds((N, DINP)), name="inproj_fwd", compiler_params=_cp(("parallel",)))(x, wt)


def _inproj_bwd(x, w, dxp, du2, dud, gq_f, gq_b, gk_f, gk_b, gv_f, gv_b, gr, daq, dakv, dhl):
    tm = INPROJ_BWD_TM
    nt = N // tm

    def body(x_ref, w_ref, dxp_ref, du2_ref, dud_ref, gqf, gqb, gkf, gkb, gvf, gvb, gr_ref, daq_ref, dakv_ref, dhl_ref,
             dx_ref, dw_ref, acc):
        i = pl.program_id(0)
        f = lambda r: r[...].astype(F32)
        dh = jnp.concatenate([
            du2_ref[0] + du2_ref[1] + f(dud_ref), f(gqf) + f(gqb), f(gkf) + f(gkb), f(gvf) + f(gvb),
            f(gr_ref), f(daq_ref), f(dakv_ref), f(dhl_ref)], axis=1)
        dx_ref[...] = dxp_ref[...] + _mm(dh, w_ref[...])
        contrib = _mm_tn(dh, x_ref[...])

        @pl.when(i == 0)
        def _():
            acc[...] = contrib

        @pl.when(i > 0)
        def _():
            acc[...] += contrib

        @pl.when(i == nt - 1)
        def _():
            for (j, d0), (s0, n_) in _IN_ROW_PIECES:
                dw_ref[j, d0:d0 + n_] = acc[s0:s0 + n_].astype(MX)

    row = lambda w_: pl.BlockSpec((tm, w_), lambda i: (i, 0))
    return pl.pallas_call(
        body, grid=(nt,),
        in_specs=[row(D), pl.BlockSpec((DINP, D), lambda i: (0, 0)), row(D),
                  pl.BlockSpec((2, tm, 256), lambda i: (0, i, 0)), row(256), row(128), row(128), row(128), row(128),
                  row(256), row(256), row(256), row(512), row(256), row(128)],
        out_specs=[row(D), pl.BlockSpec((NSHARD, DIN // NSHARD, D), lambda i: (0, 0, 0))],
        out_shape=[_sds((N, D)), _sds((NSHARD, DIN // NSHARD, D), MX)],
        scratch_shapes=[pltpu.VMEM((DINP, D), F32)],
        name="inproj_bwd", compiler_params=_cp(("arbitrary",)))(
            x, w, dxp, du2, dud, gq_f, gq_b, gk_f, gk_b, gv_f, gv_b, gr, daq, dakv, dhl)


def _scan_tables(mr, mi, reverse):
    pw = [(mr, mi)]
    for _ in range(7):
        pr, pi = pw[-1]
        pw.append((pr * mr - pi * mi, pr * mi + pi * mr))
    rows = jnp.arange(8)[:, None]
    out = []
    for d in (1, 2, 4):
        keep = rows >= d
        out += [jnp.where(keep, pw[d - 1][0][None], 0.0), jnp.where(keep, pw[d - 1][1][None], 0.0)]
    out += [jnp.stack([p[0] for p in pw]), jnp.stack([p[1] for p in pw])]
    t = jnp.stack(out)
    if reverse:
        t = t[:, ::-1, :]
    return t.reshape(8, 8, 2, SW).transpose(2, 0, 1, 3)


def _tile_scan(xr, xi, a, cr, ci, reverse):
    for lvl, d in enumerate((1, 2, 4)):
        sh = 8 - d if reverse else d
        sr = pltpu.roll(xr, sh, 0)
        si = pltpu.roll(xi, sh, 0)
        ar, ai = a[2 * lvl], a[2 * lvl + 1]
        xr, xi = xr + ar * sr - ai * si, xi + ar * si + ai * sr
    pr, pi = a[6], a[7]
    return xr + pr * cr - pi * ci, xi + pr * ci + pi * cr


NJ = TT // 8


def _lockstep_tables(mr, mi, reverse):
    nr, ni = mr, mi
    for _ in range(NJ.bit_length() - 1):
        nr, ni = nr * nr - ni * ni, 2.0 * nr * ni
    pr, pi = mr[None], mi[None]
    while pr.shape[0] < NJ:
        k = pr.shape[0]
        tr, ti = pr[k - 1], pi[k - 1]
        pr, pi = (jnp.concatenate([pr, pr * tr - pi * ti]), jnp.concatenate([pi, pr * ti + pi * tr]))
    if reverse:
        pr, pi = pr[::-1], pi[::-1]
    rows = jnp.broadcast_to(jnp.stack([mr, mi])[:, None, :], (2, 8, 2 * SW))
    link = _scan_tables(nr, ni, reverse)
    a = jnp.concatenate([rows.reshape(2, 8, 2, SW).transpose(2, 0, 1, 3), link], axis=1)
    return a, jnp.stack([pr, pi]).reshape(2, NJ, 2, SW).transpose(2, 0, 1, 3)


def _to_lockstep(ref, *lead):
    return jnp.concatenate([ref[(*lead, pl.ds(j, 8, stride=NJ), slice(None))] for j in range(NJ)], axis=0)


def _from_lockstep(val, ref, *lead):
    for j in range(NJ):
        ref[(*lead, pl.ds(j, 8, stride=NJ), slice(None))] = val[8 * j:8 * j + 8]


def _expand_powers(p_ref, pexp):
    for c in range(2):
        for j in range(NJ):
            pexp[c, j] = jnp.broadcast_to(p_ref[0, 0, c, j:j + 1, :], (8, SW))


def _lockstep_scan(xre, xim, a_ref, pexp, car, reverse, extra=None):
    a = [a_ref[0, 0, k] for k in range(10)]
    mr, mi = a[0], a[1]
    order = (lambda i: NJ - 1 - i) if reverse else (lambda i: i)

    def local(i, hcar):
        hr, hi = hcar
        r0 = pl.multiple_of(order(i) * 8, 8)
        hr, hi = mr * hr - mi * hi + xre[pl.ds(r0, 8), :], mr * hi + mi * hr + xim[pl.ds(r0, 8), :]
        xre[pl.ds(r0, 8), :] = hr
        xim[pl.ds(r0, 8), :] = hi
        return hr, hi

    z8 = jnp.zeros((8, SW), F32)
    er, ei = lax.fori_loop(0, NJ, local, (z8, z8), unroll=4)
    c0r, c0i = car[0], car[1]
    er, ei = _tile_scan(er, ei, a[2:], c0r, c0i, reverse)
    rowid = lax.broadcasted_iota(jnp.int32, (8, SW), 0)
    first, sh, last = (7, 7, 0) if reverse else (0, 1, 7)
    cvr = jnp.where(rowid == first, c0r, pltpu.roll(er, sh, 0))
    cvi = jnp.where(rowid == first, c0i, pltpu.roll(ei, sh, 0))
    car[0] = jnp.broadcast_to(er[last:last + 1, :], (8, SW))
    car[1] = jnp.broadcast_to(ei[last:last + 1, :], (8, SW))

    def fix(i, carry):
        j = order(i)
        r0 = pl.multiple_of(j * 8, 8)
        pr, pi = pexp[0, j], pexp[1, j]
        sr = xre[pl.ds(r0, 8), :] + pr * cvr - pi * cvi
        si = xim[pl.ds(r0, 8), :] + pr * cvi + pi * cvr
        xre[pl.ds(r0, 8), :] = sr
        xim[pl.ds(r0, 8), :] = si
        if extra is None:
            return carry
        return (sr, si, extra(r0, sr, si, carry[0], carry[1], carry[2]))

    init = (cvr, cvi, extra(None, None, None, None, None, None)) if extra is not None else 0
    return lax.fori_loop(0, NJ, fix, init, unroll=4)


def _s5_time_block(z, s, t, adjoint):
    flip = (1 - z) if adjoint else z
    return s * (L // TT) + t + flip * (L // TT - 1 - 2 * t)


def _s5_fwd(h, bre, bim, cre, cim, tab):
    nt = L // TT
    taba, tabp = tab

    def body(u_ref, bre_ref, bim_ref, cre_ref, cim_ref, a_ref, p_ref, hre_ref, him_ref, y_ref, car, pexp):
        z = pl.program_id(1)
        s = pl.program_id(2)
        tc = pl.program_id(3)

        @pl.when(tc == 0)
        def _():
            car[...] = jnp.zeros_like(car)

        @pl.when((tc == 0) & (s == 0))
        def _():
            _expand_powers(p_ref, pexp)

        u = _to_lockstep(u_ref)
        hre_ref[0] = _mm(u, bre_ref[0, 0])
        him_ref[0] = _mm(u, bim_ref[0, 0])

        @pl.when(z == 0)
        def _():
            _lockstep_scan(hre_ref.at[0], him_ref.at[0], a_ref, pexp, car, False)

        @pl.when(z == 1)
        def _():
            _lockstep_scan(hre_ref.at[0], him_ref.at[0], a_ref, pexp, car, True)

        _from_lockstep(_mm(hre_ref[0], cre_ref[0, 0]) - _mm(him_ref[0], cim_ref[0, 0]), y_ref, 0)

    tb = lambda b, z, s, t: _s5_time_block(z, s, t, False)
    wspec = lambda r, c: pl.BlockSpec((1, 1, r, c), lambda b, z, s, t: (z, b, 0, 0))
    return pl.pallas_call(
        body, grid=(2, 2, NSEQ, nt),
        in_specs=[pl.BlockSpec((TT, 128), lambda b, z, s, t: (tb(b, z, s, t), b)),
                  wspec(128, SW), wspec(128, SW), wspec(SW, 128), wspec(SW, 128),
                  pl.BlockSpec((1, 1, 10, 8, SW), lambda b, z, s, t: (z, b, 0, 0, 0)),
                  pl.BlockSpec((1, 1, 2, NJ, SW), lambda b, z, s, t: (z, b, 0, 0, 0))],
        out_specs=[pl.BlockSpec((1, TT, SW), lambda b, z, s, t: (z, tb(b, z, s, t), b)),
                   pl.BlockSpec((1, TT, SW), lambda b, z, s, t: (z, tb(b, z, s, t), b)),
                   pl.BlockSpec((1, TT, 128), lambda b, z, s, t: (z, tb(b, z, s, t), b))],
        out_shape=[_sds((2, N, 2 * SW)), _sds((2, N, 2 * SW)), _sds((2, N, 256))],
        scratch_shapes=[pltpu.VMEM((2, 8, SW), F32), pltpu.VMEM((2, NJ, 8, SW), F32)],
        name="s5_fwd", compiler_params=_cp(("arbitrary",) * 4))(h, bre, bim, cre, cim, taba, tabp)


def _s5_bwd(h, dyp, hre, him, bre, bim, cre, cim, tabc):
    nt = L // TT
    taba, tabp = tabc

    def body(u_ref, dy_ref, hre_ref, him_ref, bre_ref, bim_ref, cre_ref, cim_ref, a_ref, p_ref,
             du_ref, dbre_ref, dbim_ref, dcre_ref, dcim_ref, dmu_ref, gre, gim, car, acc, macc, pexp):
        z = pl.program_id(1)
        s = pl.program_id(2)
        tc = pl.program_id(3)

        @pl.when(tc == 0)
        def _():
            car[...] = jnp.zeros_like(car)

        @pl.when((tc == 0) & (s == 0))
        def _():
            acc[...] = jnp.zeros_like(acc)
            macc[...] = jnp.zeros_like(macc)
            _expand_powers(p_ref, pexp)

        dy = _to_lockstep(dy_ref)
        gre[...] = _mm_nt(dy, cre_ref[0, 0])
        gim[...] = -_mm_nt(dy, cim_ref[0, 0])

        def run(reverse):
            def pair(r0, gr_, gi_, pvr, pvi, m):
                if r0 is None:
                    return (macc[0], macc[1])
                hr = hre_ref[0, pl.ds(r0, 8), :]
                hi = him_ref[0, pl.ds(r0, 8), :]
                return (m[0] + pvr * hr + pvi * hi, m[1] + pvi * hr - pvr * hi)

            _, _, (dmr, dmi) = _lockstep_scan(gre, gim, a_ref, pexp, car, reverse, pair)
            macc[0] = dmr
            macc[1] = dmi

        @pl.when(z == 0)
        def _():
            run(True)

        @pl.when(z == 1)
        def _():
            run(False)

        gr = gre[...]
        gi = gim[...]
        u = _to_lockstep(u_ref)
        _from_lockstep(_mm_nt(gr, bre_ref[0, 0]) + _mm_nt(gi, bim_ref[0, 0]), du_ref, 0)
        acc[0] += _mm_tn(u, gr)
        acc[1] += _mm_tn(u, gi)
        acc[2] += _mm_tn(dy, hre_ref[0])
        acc[3] -= _mm_tn(dy, him_ref[0])

        @pl.when((tc == nt - 1) & (s == NSEQ - 1))
        def _():
            grp = lax.broadcasted_iota(jnp.int32, (S5_H, SW), 1) // S5_P
            for k, out in enumerate((dbre_ref, dbim_ref, dcre_ref, dcim_ref)):
                c = jnp.zeros((S5_H, SW), F32)
                for i in range(8):
                    c = c + jnp.where(grp == i, acc[k, i * S5_H:(i + 1) * S5_H, :], 0.0)
                out[0, 0] = c
            dmu_ref[0, 0] = jnp.concatenate([jnp.sum(macc[0], axis=0, keepdims=True),
                                             jnp.sum(macc[1], axis=0, keepdims=True)], axis=0)

    tb = lambda b, z, s, t: _s5_time_block(z, s, t, True)
    wspec = lambda r, c: pl.BlockSpec((1, 1, r, c), lambda b, z, s, t: (z, b, 0, 0))
    tok = lambda w_: pl.BlockSpec((TT, w_), lambda b, z, s, t: (tb(b, z, s, t), b))
    st = pl.BlockSpec((1, TT, SW), lambda b, z, s, t: (z, tb(b, z, s, t), b))
    return pl.pallas_call(
        body, grid=(2, 2, NSEQ, nt),
        in_specs=[tok(128), tok(128), st, st, wspec(128, SW), wspec(128, SW), wspec(SW, 128), wspec(SW, 128),
                  pl.BlockSpec((1, 1, 10, 8, SW), lambda b, z, s, t: (z, b, 0, 0, 0)),
                  pl.BlockSpec((1, 1, 2, NJ, SW), lambda b, z, s, t: (z, b, 0, 0, 0))],
        out_specs=[pl.BlockSpec((1, TT, 128), lambda b, z, s, t: (z, tb(b, z, s, t), b)),
                   wspec(S5_H, SW), wspec(S5_H, SW), wspec(S5_H, SW), wspec(S5_H, SW),
                   wspec(2, SW)],
        out_shape=[_sds((2, N, 256))] + [_sds((2, 2, S5_H, SW))] * 4 + [_sds((2, 2, 2, SW))],
        scratch_shapes=[pltpu.VMEM((TT, SW), F32), pltpu.VMEM((TT, SW), F32), pltpu.VMEM((2, 8, SW), F32),
                        pltpu.VMEM((4, 128, SW), F32), pltpu.VMEM((2, 8, SW), F32), pltpu.VMEM((2, NJ, 8, SW), F32)],
        name="s5_bwd", compiler_params=_cp(("arbitrary",) * 4))(h, dyp, hre, him, bre, bim, cre, cim, taba, tabp)


_GELU_C = math.sqrt(2.0 / math.pi)


def _gelu(y):
    return 0.5 * y * (1.0 + jnp.tanh(_GELU_C * (y + 0.044715 * y * y * y)))


def _gelu_grad(y):
    t = jnp.tanh(_GELU_C * (y + 0.044715 * y * y * y))
    return 0.5 * (1.0 + t) + 0.5 * y * (1.0 - t * t) * _GELU_C * (1.0 + 3 * 0.044715 * y * y)


def _glu_halves(w4_ref):
    return (jnp.concatenate([w4_ref[0], w4_ref[1]], axis=1), jnp.concatenate([w4_ref[2], w4_ref[3]], axis=1))


def _s5_glu_fwd(y2, h, dsk, w4, bv, bg):
    tm = 512

    def body(y2_ref, u_ref, d_ref, w4_ref, bv_ref, bg_ref, ya_ref):
        wv, wg = _glu_halves(w4_ref)
        z = _gelu(y2_ref[0] + y2_ref[1] + d_ref[...] * u_ref[...])
        val = _mm(z, wv) + bv_ref[...]
        gate = _mm(z, wg) + bg_ref[...]
        ya_ref[...] = (val * jax.nn.sigmoid(gate)).astype(MX)

    full = lambda r, c: pl.BlockSpec((r, c), lambda i: (0, 0))
    return pl.pallas_call(
        body, grid=(N // tm,),
        in_specs=[pl.BlockSpec((2, tm, 256), lambda i: (0, i, 0)), pl.BlockSpec((tm, 256), lambda i: (i, 0)),
                  full(1, 256), pl.BlockSpec((NSHARD, 256, 128), lambda i: (0, 0, 0)), full(1, 256), full(1, 256)],
        out_specs=pl.BlockSpec((tm, 256), lambda i: (i, 0)),
        out_shape=_sds((N, 256), MX), name="s5_glu_fwd", compiler_params=_cp(("parallel",)))(y2, h, dsk, w4, bv, bg)


def _s5_glu_bwd(y2, h, dsk, w4, bv, bg, dya):
    tm = 512
    nt = N // tm

    def body(y2_ref, u_ref, d_ref, w4_ref, bv_ref, bg_ref, dya_ref,
             dyp_ref, dud_ref, dd_ref, dw4_ref, dbv_ref, dbg_ref, accv, accg):
        i = pl.program_id(0)

        @pl.when(i == 0)
        def _():
            for r in (dd_ref, accv, accg, dbv_ref, dbg_ref):
                r[...] = jnp.zeros_like(r)

        wv, wg = _glu_halves(w4_ref)
        u = u_ref[...]
        y = y2_ref[0] + y2_ref[1] + d_ref[...] * u
        z = _gelu(y)
        val = _mm(z, wv) + bv_ref[...]
        sig = jax.nn.sigmoid(_mm(z, wg) + bg_ref[...])
        dya = dya_ref[...]
        dval = dya * sig
        dgate = dya * val * sig * (1.0 - sig)
        dz = _mm_nt(dval, wv) + _mm_nt(dgate, wg)
        dy = dz * _gelu_grad(y)
        dyp_ref[...] = dy
        dud_ref[...] = (dy * d_ref[...]).astype(MX)
        dd_ref[...] += jnp.sum(dy * u, axis=0, keepdims=True)
        accv[...] += _mm_tn(z, dval)
        accg[...] += _mm_tn(z, dgate)
        dbv_ref[...] += jnp.sum(dval, axis=0, keepdims=True)
        dbg_ref[...] += jnp.sum(dgate, axis=0, keepdims=True)

        @pl.when(i == nt - 1)
        def _():
            dw4_ref[0] = accv[:, 0:128].astype(MX)
            dw4_ref[1] = accv[:, 128:256].astype(MX)
            dw4_ref[2] = accg[:, 0:128].astype(MX)
            dw4_ref[3] = accg[:, 128:256].astype(MX)

    full = lambda r, c: pl.BlockSpec((r, c), lambda i: (0, 0))
    row = pl.BlockSpec((tm, 256), lambda i: (i, 0))
    wspec = pl.BlockSpec((NSHARD, 256, 128), lambda i: (0, 0, 0))
    return pl.pallas_call(
        body, grid=(nt,),
        in_specs=[pl.BlockSpec((2, tm, 256), lambda i: (0, i, 0)), row, full(1, 256), wspec, full(1, 256), full(1, 256),
                  row],
        out_specs=[row, row, full(1, 256), wspec, full(1, 256), full(1, 256)],
        out_shape=[_sds((N, 256)), _sds((N, 256), MX), _sds((1, 256)), _sds((NSHARD, 256, 128), MX), _sds((1, 256)),
                   _sds((1, 256))],
        scratch_shapes=[pltpu.VMEM((256, 256), F32), pltpu.VMEM((256, 256), F32)],
        name="s5_glu_bwd", compiler_params=_cp(("arbitrary",)))(y2, h, dsk, w4, bv, bg, dya)


def _logsig(x):
    return jnp.minimum(x, 0.0) - jnp.log(1.0 + jnp.exp(-jnp.abs(x)))


def _gla_gate_fwd(h, wa, ba):
    tm = 512

    def body(hl_ref, wa_ref, ba_ref, la_ref):
        la_ref[...] = _logsig(_mm(hl_ref[...], wa_ref[...]) + ba_ref[...]) * (1.0 / 16.0)

    return pl.pallas_call(
        body, grid=(N // tm,),
        in_specs=[pl.BlockSpec((tm, 128), lambda i: (i, 14)), pl.BlockSpec((128, 256), lambda i: (0, 0)),
                  pl.BlockSpec((1, 256), lambda i: (0, 0))],
        out_specs=pl.BlockSpec((tm, 256), lambda i: (i, 0)),
        out_shape=_sds((N, 256)), name="gla_gate_fwd", compiler_params=_cp(("parallel",)))(h, wa, ba)


def _gla_gate_bwd(h, wa, ba, dla_f, dla_b):
    tm = 512

    def body(hl_ref, wa_ref, ba_ref, df_ref, db_ref, dhl_ref, dwa_ref, dba_ref):
        i = pl.program_id(0)

        @pl.when(i == 0)
        def _():
            dwa_ref[...] = jnp.zeros_like(dwa_ref)
            dba_ref[...] = jnp.zeros_like(dba_ref)

        hl = hl_ref[...]
        pre = _mm(hl, wa_ref[...]) + ba_ref[...]
        dpre = jnp.concatenate([df_ref[...], db_ref[...]], axis=1) * (1.0 / 16.0) * jax.nn.sigmoid(-pre)
        dhl_ref[...] = _mm_nt(dpre, wa_ref[...]).astype(MX)
        dwa_ref[...] += _mm_tn(hl, dpre)[0:32]
        dba_ref[...] += jnp.sum(dpre, axis=0, keepdims=True)

    row = pl.BlockSpec((tm, 128), lambda i: (i, 0))
    return pl.pallas_call(
        body, grid=(N // tm,),
        in_specs=[pl.BlockSpec((tm, 128), lambda i: (i, 14)), pl.BlockSpec((128, 256), lambda i: (0, 0)),
                  pl.BlockSpec((1, 256), lambda i: (0, 0)), row, row],
        out_specs=[row, pl.BlockSpec((32, 256), lambda i: (0, 0)), pl.BlockSpec((1, 256), lambda i: (0, 0))],
        out_shape=[_sds((N, 128), MX), _sds((32, 256)), _sds((1, 256))],
        name="gla_gate_bwd", compiler_params=_cp(("arbitrary",)))(h, wa, ba, dla_f, dla_b)


def _gla_chunk(q, k, v, la, st, rev):
    c = GLA_CHUNK
    rows = q.shape[0]
    nch = rows // c
    b = _cums(la, rev)
    blc = [jnp.sum(la[i * c:(i + 1) * c], axis=0, keepdims=True) for i in range(nch)]
    bl = jnp.concatenate([jnp.broadcast_to(t, (c, 128)) for t in blc], axis=0)
    q_in = q * (32.0 ** -0.5) * jnp.exp(b)
    k_in = k * jnp.exp(-b)
    k_st = k * jnp.exp(bl - b)
    lane_k = lax.broadcasted_iota(jnp.int32, (1, 128), 1) // 32
    lane_v = lax.broadcasted_iota(jnp.int32, (1, 256), 1) // 64
    qs = jnp.concatenate([jnp.where(lane_k == hd, q_in, 0.0) for hd in range(4)], axis=0)
    a = _dmm_nt(qs, k_in)
    a = jnp.where(jnp.concatenate([_chunk_pairs(rows, rev, rev)] * 4, axis=0), a, 0.0)
    o4 = _dmm(a, v)
    o = jnp.zeros((rows, 256), F32)
    for hd in range(4):
        o = o + jnp.where(lane_v == hd, o4[hd * rows:(hd + 1) * rows], 0.0)
    bd = (lax.broadcasted_iota(jnp.int32, (256, 128), 0) // 64) == (lax.broadcasted_iota(jnp.int32, (256, 128), 1) // 32)
    inter = [None] * nch
    for i in (reversed(range(nch)) if rev else range(nch)):
        sl = slice(i * c, (i + 1) * c)
        inter[i] = _dmm_nt(q_in[sl], st)
        st = jnp.exp(blc[i]) * st + jnp.where(bd, _dmm_tn(v[sl], k_st[sl]), 0.0)
    return o + jnp.concatenate(inter, axis=0), st


def _gla_chunk_of(c, rev):
    return NGROUP - 1 - c if rev else c


def _gla_fwd(h, la2):
    c = GLA_GROUP * GLA_CHUNK

    def body(qf, kf, vf, laf, qb, kb, vb, lab, of_ref, ob_ref, sf_ref, sb_ref, stf, stb):
        @pl.when(pl.program_id(0) == 0)
        def _():
            stf[...] = jnp.zeros_like(stf)
            stb[...] = jnp.zeros_like(stb)

        ins = [(qf[s], kf[s], vf[s], laf[s], stf[s], qb[s], kb[s], vb[s], lab[s], stb[s]) for s in range(NSEQ)]
        outs = [(_gla_chunk(*t[:5], False), _gla_chunk(*t[5:], True)) for t in ins]
        for s in range(NSEQ):
            sf_ref[s, 0] = ins[s][4]
            sb_ref[s, 0] = ins[s][9]
            (of_ref[s], stf[s]), (ob_ref[s], stb[s]) = outs[s]

    def specs(rev):
        ch = lambda i: _gla_chunk_of(i, rev)
        return [pl.BlockSpec((NSEQ, c, 128), lambda i: (0, ch(i), 2)), pl.BlockSpec((NSEQ, c, 128), lambda i: (0, ch(i), 3)),
                pl.BlockSpec((NSEQ, c, 256), lambda i: (0, ch(i), 2)),
                pl.BlockSpec((NSEQ, c, 128), lambda i: (0, ch(i), 1 if rev else 0))]

    orow = lambda rev: pl.BlockSpec((NSEQ, c, 256), lambda i: (0, _gla_chunk_of(i, rev), 0))
    srow = lambda rev: pl.BlockSpec((NSEQ, 1, 256, 128), lambda i: (0, _gla_chunk_of(i, rev), 0, 0))
    h3, la3 = h.reshape(NSEQ, L, DINP), la2.reshape(NSEQ, L, 256)
    of, ob, sf, sb = pl.pallas_call(
        body, grid=(NGROUP,),
        in_specs=specs(False) + specs(True),
        out_specs=[orow(False), orow(True), srow(False), srow(True)],
        out_shape=[_sds((NSEQ, L, 256)), _sds((NSEQ, L, 256)), _sds((NSEQ, NGROUP, 256, 128)),
                   _sds((NSEQ, NGROUP, 256, 128))],
        scratch_shapes=[pltpu.VMEM((NSEQ, 256, 128), F32), pltpu.VMEM((NSEQ, 256, 128), F32)],
        name="gla_fwd", compiler_params=_cp(("arbitrary",)))(h3, h3, h3, la3, h3, h3, h3, la3)
    return of.reshape(N, 256), ob.reshape(N, 256), sf, sb


def _gla_bwd(h, la2, do, sf, sb):
    c = GLA_GROUP * GLA_CHUNK

    def body(qf, kf, vf, laf, dof, sfr, qb, kb, vb, lab, dob, sbr,
             dqf, dkf, dvf, dlf, dqb, dkb, dvb, dlb, dstf, dstb):
        @pl.when(pl.program_id(0) == 0)
        def _():
            dstf[...] = jnp.zeros_like(dstf)
            dstb[...] = jnp.zeros_like(dstb)

        def one(s, q, k, v, la, do_, st, dst, rev):
            _, vjp = jax.vjp(functools.partial(_gla_chunk, rev=rev), q[s], k[s], v[s], la[s], st[s, 0])
            return vjp((do_[s], dst[s]))

        res = [(one(s, qf, kf, vf, laf, dof, sfr, dstf, False), one(s, qb, kb, vb, lab, dob, sbr, dstb, True))
               for s in range(NSEQ)]
        for s in range(NSEQ):
            for (gq, gk, gv, gl, gs), (dq, dk, dv, dl, dst) in ((res[s][0], (dqf, dkf, dvf, dlf, dstf)),
                                                                  (res[s][1], (dqb, dkb, dvb, dlb, dstb))):
                dq[s], dk[s], dv[s] = gq.astype(MX), gk.astype(MX), gv.astype(MX)
                dl[s], dst[s] = gl, gs

    def specs(rev):
        ch = lambda i: _gla_chunk_of(i, not rev)
        return [pl.BlockSpec((NSEQ, c, 128), lambda i: (0, ch(i), 2)), pl.BlockSpec((NSEQ, c, 128), lambda i: (0, ch(i), 3)),
                pl.BlockSpec((NSEQ, c, 256), lambda i: (0, ch(i), 2)),
                pl.BlockSpec((NSEQ, c, 128), lambda i: (0, ch(i), 1 if rev else 0)),
                pl.BlockSpec((NSEQ, c, 256), lambda i: (0, ch(i), 0)),
                pl.BlockSpec((NSEQ, 1, 256, 128), lambda i: (0, ch(i), 0, 0))]

    def ospecs(rev):
        ch = lambda i: _gla_chunk_of(i, not rev)
        n = pl.BlockSpec((NSEQ, c, 128), lambda i: (0, ch(i), 0))
        return [n, n, pl.BlockSpec((NSEQ, c, 256), lambda i: (0, ch(i), 0)), n]

    oshape = [_sds((NSEQ, L, 128), MX), _sds((NSEQ, L, 128), MX), _sds((NSEQ, L, 256), MX), _sds((NSEQ, L, 128))]
    h3, la3, do3 = h.reshape(NSEQ, L, DINP), la2.reshape(NSEQ, L, 256), do.reshape(NSEQ, L, 256)
    res = pl.pallas_call(
        body, grid=(NGROUP,),
        in_specs=specs(False) + specs(True),
        out_specs=ospecs(False) + ospecs(True),
        out_shape=oshape + oshape,
        scratch_shapes=[pltpu.VMEM((NSEQ, 256, 128), F32), pltpu.VMEM((NSEQ, 256, 128), F32)],
        name="gla_bwd", compiler_params=_cp(("arbitrary",)))(h3, h3, h3, la3, do3, sf, h3, h3, h3, la3, do3, sb)
    return [r.reshape(N, r.shape[-1]) for r in res]


def _gla_post(of, ob, r, g):
    o = of + ob
    head = lax.broadcasted_iota(jnp.int32, (1, 256), 1) // 64
    mu = jnp.zeros_like(o)
    for hd in range(4):
        mu = mu + jnp.where(head == hd, jnp.sum(jnp.where(head == hd, o, 0.0), axis=-1, keepdims=True) * (1.0 / 64.0), 0.0)
    xc = o - mu
    var = jnp.zeros_like(o)
    for hd in range(4):
        var = var + jnp.where(head == hd, jnp.sum(jnp.where(head == hd, xc * xc, 0.0), axis=-1, keepdims=True) * (1.0 / 64.0), 0.0)
    return xc * lax.rsqrt(var + LN_EPS) * g * (r * jax.nn.sigmoid(r))


def _gla_post_fwd(of, ob, h, g):
    tm = 512

    def body(of_ref, ob_ref, r_ref, g_ref, y_ref):
        y_ref[...] = _gla_post(of_ref[...], ob_ref[...], r_ref[...], g_ref[...]).astype(MX)

    row = pl.BlockSpec((tm, 256), lambda i: (i, 0))
    return pl.pallas_call(
        body, grid=(N // tm,),
        in_specs=[row, row, pl.BlockSpec((tm, 256), lambda i: (i, 3)), pl.BlockSpec((1, 256), lambda i: (0, 0))],
        out_specs=row, out_shape=_sds((N, 256), MX), name="gla_post_fwd", compiler_params=_cp(("parallel",)))(of, ob, h, g)


def _gla_post_bwd(of, ob, h, g, dyb):
    tm = 512

    def body(of_ref, ob_ref, r_ref, g_ref, dy_ref, do_ref, dr_ref, dg_ref):
        @pl.when(pl.program_id(0) == 0)
        def _():
            dg_ref[...] = jnp.zeros_like(dg_ref)

        _, vjp = jax.vjp(_gla_post, of_ref[...], ob_ref[...], r_ref[...], g_ref[...])
        go, _, gr, gg = vjp(dy_ref[...])
        do_ref[...] = go
        dr_ref[...] = gr.astype(MX)
        dg_ref[...] += gg

    row = pl.BlockSpec((tm, 256), lambda i: (i, 0))
    one = pl.BlockSpec((1, 256), lambda i: (0, 0))
    return pl.pallas_call(
        body, grid=(N // tm,),
        in_specs=[row, row, pl.BlockSpec((tm, 256), lambda i: (i, 3)), one, row],
        out_specs=[row, row, one], out_shape=[_sds((N, 256)), _sds((N, 256), MX), _sds((1, 256))],
        name="gla_post_bwd", compiler_params=_cp(("arbitrary",)))(of, ob, h, g, dyb)


def _rope_tables(width):
    pos = jnp.arange(L, dtype=F32)
    inv_freq = ROPE_THETA ** (-jnp.arange(0, ROT, 2, dtype=F32) / ROT)
    ang = pos[:, None] * inv_freq[None, :]
    cos, sin = jnp.cos(ang), jnp.sin(ang)
    one = jnp.ones((L, 64 - ROT), F32)
    zero = jnp.zeros((L, 64 - ROT), F32)
    z8 = jnp.zeros((L, ROT // 2), F32)
    c = jnp.concatenate([cos, cos, one], axis=1)
    sa = jnp.concatenate([z8, sin, zero], axis=1)
    sb = jnp.concatenate([-sin, z8, zero], axis=1)
    rep = width // 64
    return jnp.stack([jnp.tile(c, (1, rep)), jnp.tile(sa, (1, rep)), jnp.tile(sb, (1, rep))])


def _pieces(t, f):
    out = [f(t[:, c * 128:(c + 1) * 128]) for c in range(t.shape[-1] // 128)]
    return out[0] if len(out) == 1 else jnp.concatenate(out, axis=1)


def _rope(t, tab):
    return _pieces(t, lambda x: x * tab[0] + pltpu.roll(x, ROT // 2, 1) * tab[1] + pltpu.roll(x, 128 - ROT // 2, 1) * tab[2])


def _rope_t(g, tab):
    return _pieces(g, lambda x: x * tab[0] + pltpu.roll(x * tab[1], 128 - ROT // 2, 1) + pltpu.roll(x * tab[2], ROT // 2, 1))


def _swa_pad_kv(kv_ref, tk_ref, kexp, vexp):
    z = jnp.zeros((SWA_BLK, 256), F32)
    kr = _rope(kv_ref[:, 0:128], tk_ref[...])
    for hk in range(2):
        for pad in (kexp, vexp):
            pad[hk, 0:SWA_BLK] = z
            pad[hk, SWA_BLK + L:] = z
        kexp[hk, SWA_BLK:SWA_BLK + L] = _swa_expand(kr, hk)
        vexp[hk, SWA_BLK:SWA_BLK + L] = _swa_expand(kv_ref[:, 128:256], hk)


def _swa_expand(x, hk):
    lane = lax.broadcasted_iota(jnp.int32, x.shape, 1)
    sw = pltpu.roll(x, 64, 1)
    pair = jnp.where(lane < 64, x, sw) if hk == 0 else jnp.where(lane < 64, sw, x)
    return jnp.concatenate([pair, pair], axis=1)


def _swa_fold(x, hk):
    a = x[:, 0:128] + x[:, 128:256]
    t = a + pltpu.roll(a, 64, 1)
    lane = lax.broadcasted_iota(jnp.int32, a.shape, 1)
    return jnp.where((lane < 64) if hk == 0 else (lane >= 64), t, 0.0)


def _swa_probs(q2, kexp, n, sink_ref, hk):
    slot = lax.broadcasted_iota(jnp.int32, (1, 256), 1) // 64
    qs = jnp.concatenate([jnp.where(slot == g, q2, 0.0) for g in range(4)], axis=0)
    s = _mm_nt(qs, kexp) * 0.125
    i = lax.broadcasted_iota(jnp.int32, (SWA_BLK, 3 * SWA_BLK), 0)
    j = lax.broadcasted_iota(jnp.int32, (SWA_BLK, 3 * SWA_BLK), 1)
    kpos = n * SWA_BLK - SWA_BLK + j
    ok = (j - i >= 0) & (j - i <= 2 * SWA_BLK) & (kpos >= 0) & (kpos < L)
    s = jnp.where(jnp.concatenate([ok] * 4, axis=0), s, NEG_BIG)
    rowg = lax.broadcasted_iota(jnp.int32, (4 * SWA_BLK, 1), 0) // SWA_BLK
    sink = jnp.zeros((4 * SWA_BLK, 1), F32)
    for g in range(4):
        sink = jnp.where(rowg == g, sink_ref[hk * 4 + g], sink)
    m = jnp.maximum(jnp.max(s, axis=-1, keepdims=True), sink)
    p = jnp.exp(s - m)
    ps = jnp.exp(sink - m)
    inv = 1.0 / (jnp.sum(p, axis=-1, keepdims=True) + ps)
    return qs, p * inv, ps * inv, slot, rowg


def _swa_qtab(tk_ref, r0):
    return [tk_ref[i, pl.ds(r0, SWA_BLK), :] for i in range(3)]


def _swa_fwd(h, tk, sink):
    def body(sink_ref, q_ref, kv_ref, tk_ref, y_ref, kexp, vexp):
        n = pl.program_id(1)

        @pl.when(n == 0)
        def _():
            _swa_pad_kv(kv_ref, tk_ref, kexp, vexp)

        r0 = pl.multiple_of(n * SWA_BLK, SWA_BLK)
        q = _rope(q_ref[...], _swa_qtab(tk_ref, r0))
        for hk in range(2):
            _, p, _, slot, _ = _swa_probs(q[:, hk * 256:(hk + 1) * 256], kexp[hk, pl.ds(r0, 3 * SWA_BLK), :], n,
                                          sink_ref, hk)
            o4 = _mm(p, vexp[hk, pl.ds(r0, 3 * SWA_BLK), :])
            o = jnp.zeros((SWA_BLK, 256), F32)
            for g in range(4):
                o = o + jnp.where(slot == g, o4[g * SWA_BLK:(g + 1) * SWA_BLK], 0.0)
            y_ref[:, hk * 256:(hk + 1) * 256] = o.astype(MX)

    return pl.pallas_call(
        body,
        grid_spec=pltpu.PrefetchScalarGridSpec(
            num_scalar_prefetch=1, grid=(NSEQ, NBLK),
            in_specs=[pl.BlockSpec((SWA_BLK, 512), lambda s, n, sk: (s * NBLK + n, 2)),
                      pl.BlockSpec((L, 256), lambda s, n, sk: (s, 6)),
                      pl.BlockSpec((3, L, 128), lambda s, n, sk: (0, 0, 0))],
            out_specs=pl.BlockSpec((SWA_BLK, 512), lambda s, n, sk: (s * NBLK + n, 0)),
            scratch_shapes=[pltpu.VMEM((2, L + 2 * SWA_BLK, 256), F32), pltpu.VMEM((2, L + 2 * SWA_BLK, 256), F32)]),
        out_shape=_sds((N, 512), MX), name="swa_fwd", compiler_params=_cp(("arbitrary", "arbitrary")))(sink, h, h, tk)


def _swa_bwd(h, tk, sink, dyc):
    def body(sink_ref, q_ref, kv_ref, tk_ref, dy_ref, dq_ref, dkv_ref, dsink_ref, kexp_all, vexp_all, dkacc, dvacc):
        sq = pl.program_id(0)
        n = pl.program_id(1)

        @pl.when(n == 0)
        def _():
            _swa_pad_kv(kv_ref, tk_ref, kexp_all, vexp_all)
            dkacc[...] = jnp.zeros_like(dkacc)
            dvacc[...] = jnp.zeros_like(dvacc)

        @pl.when((n == 0) & (sq == 0))
        def _():
            dsink_ref[...] = jnp.zeros_like(dsink_ref)

        r0 = pl.multiple_of(n * SWA_BLK, SWA_BLK)
        tq = _swa_qtab(tk_ref, r0)
        q = _rope(q_ref[...], tq)
        hrow = lax.broadcasted_iota(jnp.int32, (8, 128), 0)
        dsk = jnp.zeros((8, 128), F32)
        for hk in range(2):
            kexp = kexp_all[hk, pl.ds(r0, 3 * SWA_BLK), :]
            vexp = vexp_all[hk, pl.ds(r0, 3 * SWA_BLK), :]
            qs, p, ps, slot, rowg = _swa_probs(q[:, hk * 256:(hk + 1) * 256], kexp, n, sink_ref, hk)
            dy2 = dy_ref[:, hk * 256:(hk + 1) * 256]
            dos = jnp.concatenate([jnp.where(slot == g, dy2, 0.0) for g in range(4)], axis=0)
            dp = _mm_nt(dos, vexp)
            delta = jnp.sum(p * dp, axis=-1, keepdims=True)
            ds = p * (dp - delta) * 0.125
            dsr = -ps * delta
            for g in range(4):
                dsk = dsk + jnp.where(hrow == hk * 4 + g, jnp.sum(jnp.where(rowg == g, dsr, 0.0), axis=0, keepdims=True), 0.0)
            dq4 = _mm(ds, kexp)
            dq2 = jnp.zeros((SWA_BLK, 256), F32)
            for g in range(4):
                dq2 = dq2 + jnp.where(slot == g, dq4[g * SWA_BLK:(g + 1) * SWA_BLK], 0.0)
            dq_ref[:, hk * 256:(hk + 1) * 256] = _rope_t(dq2, tq).astype(MX)
            dkacc[hk, pl.ds(r0, 3 * SWA_BLK), :] += _mm_tn(ds, qs)
            dvacc[hk, pl.ds(r0, 3 * SWA_BLK), :] += _mm_tn(p, dos)
        dsink_ref[...] += dsk

        @pl.when(n == NBLK - 1)
        def _():
            seq = slice(SWA_BLK, SWA_BLK + L)
            dk = _rope_t(_swa_fold(dkacc[0, seq], 0) + _swa_fold(dkacc[1, seq], 1), tk_ref[...])
            dkv_ref[:, 0:128] = dk.astype(MX)
            dkv_ref[:, 128:256] = (_swa_fold(dvacc[0, seq], 0) + _swa_fold(dvacc[1, seq], 1)).astype(MX)

    blk = lambda col: pl.BlockSpec((SWA_BLK, 512), lambda s, n, sk: (s * NBLK + n, col))
    pad = pltpu.VMEM((2, L + 2 * SWA_BLK, 256), F32)
    return pl.pallas_call(
        body,
        grid_spec=pltpu.PrefetchScalarGridSpec(
            num_scalar_prefetch=1, grid=(NSEQ, NBLK),
            in_specs=[blk(2), pl.BlockSpec((L, 256), lambda s, n, sk: (s, 6)),
                      pl.BlockSpec((3, L, 128), lambda s, n, sk: (0, 0, 0)), blk(0)],
            out_specs=[blk(0), pl.BlockSpec((L, 256), lambda s, n, sk: (s, 0)),
                       pl.BlockSpec((8, 128), lambda s, n, sk: (0, 0))],
            scratch_shapes=[pad, pad, pad, pad]),
        out_shape=[_sds((N, 512), MX), _sds((N, 256), MX), _sds((8, 128))],
        name="swa_bwd", compiler_params=_cp(("arbitrary", "arbitrary")))(sink, h, h, tk, dyc)


def _outproj_fwd(ya, yb, yc, x, wo, g, b):
    tm = 512

    def body(ya_ref, yb_ref, yc_ref, x_ref, wo_ref, g_ref, b_ref, s_ref, x1_ref):
        mix = _mm(ya_ref[...], wo_ref[0:256]) + _mm(yb_ref[...], wo_ref[256:512]) + _mm(yc_ref[...], wo_ref[512:1024])
        s = ALPHA * x_ref[...] + mix
        s_ref[...] = s
        x1_ref[...] = _ln_fwd(s, g_ref[...], b_ref[...])

    row = lambda w_: pl.BlockSpec((tm, w_), lambda i: (i, 0))
    one = pl.BlockSpec((1, D), lambda i: (0, 0))
    return pl.pallas_call(
        body, grid=(N // tm,),
        in_specs=[row(256), row(256), row(512), row(D), pl.BlockSpec((D, D), lambda i: (0, 0)), one, one],
        out_specs=[row(D), row(D)], out_shape=[_sds((N, D)), _sds((N, D))],
        name="outproj_fwd", compiler_params=_cp(("parallel",)))(ya, yb, yc, x, wo, g, b)


def _outproj_bwd(dx1, s1, ya, yb, yc, wo, g):
    tm = 512
    nt = N // tm

    def body(dx1_ref, s_ref, ya_ref, yb_ref, yc_ref, wo_ref, g_ref,
             dya_ref, dyb_ref, dyc_ref, dxp_ref, dwo_ref, dg_ref, db_ref, acc):
        i = pl.program_id(0)

        @pl.when(i == 0)
        def _():
            acc[...] = jnp.zeros_like(acc)
            dg_ref[...] = jnp.zeros_like(dg_ref)
            db_ref[...] = jnp.zeros_like(db_ref)

        ds, dg, db = _ln_bwd(dx1_ref[...], s_ref[...], g_ref[...])
        dg_ref[...] += dg
        db_ref[...] += db
        dxp_ref[...] = ALPHA * ds
        dy = _mm_nt(ds, wo_ref[...])
        dya_ref[...] = dy[:, 0:256]
        dyb_ref[...] = dy[:, 256:512]
        dyc_ref[...] = dy[:, 512:1024]
        acc[0:256] += _mm_tn(ya_ref[...], ds)
        acc[256:512] += _mm_tn(yb_ref[...], ds)
        acc[512:1024] += _mm_tn(yc_ref[...], ds)

        @pl.when(i == nt - 1)
        def _():
            dwo_ref[...] = acc[...].astype(MX)

    row = lambda w_: pl.BlockSpec((tm, w_), lambda i: (i, 0))
    one = pl.BlockSpec((1, D), lambda i: (0, 0))
    full = pl.BlockSpec((D, D), lambda i: (0, 0))
    return pl.pallas_call(
        body, grid=(nt,),
        in_specs=[row(D), row(D), row(256), row(256), row(512), full, one],
        out_specs=[row(256), row(256), row(512), row(D), full, one, one],
        out_shape=[_sds((N, 256)), _sds((N, 256)), _sds((N, 512)), _sds((N, D)), _sds((D, D), MX), _sds((1, D)), _sds((1, D))],
        scratch_shapes=[pltpu.VMEM((D, D), F32)],
        name="outproj_bwd", compiler_params=_cp(("arbitrary",)))(dx1, s1, ya, yb, yc, wo, g)


def _ffn_fwd(x1, w1, w2, g, b, target=None):
    tm = FFN_TM
    head = target is not None

    def body(*refs):
        x_ref, w1_ref, w2_ref, g_ref, b_ref = refs[:5]
        a_ref, s_ref, y_ref = refs[5 + head:8 + head]
        x = x_ref[...]
        xb = x.astype(MX)
        s = ALPHA * x
        for j in range(NSHARD):
            a = _mm(xb, w1_ref[j])
            a_ref[:, j * D:(j + 1) * D] = a.astype(MX)
            s = s + _mm(jnp.square(jnp.maximum(a, 0.0)), w2_ref[j])
        s_ref[...] = s
        x2 = _ln_fwd(s, g_ref[...], b_ref[...])
        if not head:
            y_ref[...] = x2
            return
        l_ref = refs[-1]

        @pl.when(pl.program_id(0) == 0)
        def _():
            l_ref[...] = jnp.zeros_like(l_ref)

        e = x2 - refs[5][...]
        y_ref[...] = e * (1.0 / D)
        l_ref[...] += jnp.sum(jnp.sum(e * e, axis=1, keepdims=True), axis=0, keepdims=True) * (0.5 / D)

    row = pl.BlockSpec((tm, D), lambda i: (i, 0))
    wall = pl.BlockSpec((NSHARD, D, D), lambda i: (0, 0, 0))
    one = pl.BlockSpec((1, D), lambda i: (0, 0))
    acc = pl.BlockSpec((8, 128), lambda i: (0, 0))
    return pl.pallas_call(
        body, grid=(N // tm,),
        in_specs=[row, wall, wall, one, one] + [row] * head,
        out_specs=[pl.BlockSpec((tm, DFF), lambda i: (i, 0)), row, row] + [acc] * head,
        out_shape=[_sds((N, DFF), MX), _sds((N, D)), _sds((N, D))] + [_sds((8, 128))] * head,
        name="ffn_fwd", compiler_params=_cp(("arbitrary",), FFN_VMEM))(x1, w1, w2, g, b, *([target] * head))


def _ffn_bwd_act(dy, s2, a, w1, w2, g):
    tm = FFN_TM

    def body(dy_ref, s_ref, a_ref, w1_ref, w2_ref, g_ref, da_ref, ds_ref, dx1_ref, dg_ref, db_ref):
        @pl.when(pl.program_id(0) == 0)
        def _():
            dg_ref[...] = jnp.zeros_like(dg_ref)
            db_ref[...] = jnp.zeros_like(db_ref)

        ds, dg, db = _ln_bwd(dy_ref[...], s_ref[...], g_ref[...])
        dsb = ds.astype(MX)
        ds_ref[...] = dsb
        dg_ref[...] += dg
        db_ref[...] += db
        dx1 = ALPHA * ds
        for j in range(NSHARD):
            da = (_mm_nt(dsb, w2_ref[j]) * 2.0 * jnp.maximum(a_ref[:, j * D:(j + 1) * D].astype(F32), 0.0)).astype(MX)
            da_ref[:, j * D:(j + 1) * D] = da
            dx1 = dx1 + _mm_nt(da, w1_ref[j])
        dx1_ref[...] = dx1

    row = pl.BlockSpec((tm, D), lambda i: (i, 0))
    wide = pl.BlockSpec((tm, DFF), lambda i: (i, 0))
    wall = pl.BlockSpec((NSHARD, D, D), lambda i: (0, 0, 0))
    one = pl.BlockSpec((1, D), lambda i: (0, 0))
    return pl.pallas_call(
        body, grid=(N // tm,),
        in_specs=[row, row, wide, wall, wall, one],
        out_specs=[wide, row, row, one, one],
        out_shape=[_sds((N, DFF), MX), _sds((N, D), MX), _sds((N, D)), _sds((1, D)), _sds((1, D))],
        name="ffn_bwd_act", compiler_params=_cp(("arbitrary",), FFN_VMEM))(dy, s2, a, w1, w2, g)


def _ffn_bwd_w(x1, da, a, ds):
    tm, nb = FFN_TM_W, FFN_WB
    nt = N // tm

    def body(x_ref, da_ref, a_ref, ds_ref, dw1_ref, dw2_ref, acc1, acc2):
        i = pl.program_id(1)

        @pl.when(i == 0)
        def _():
            acc1[...] = jnp.zeros_like(acc1)
            acc2[...] = jnp.zeros_like(acc2)

        x, ds_ = x_ref[...], ds_ref[...]
        for k in range(nb):
            cols = slice(k * D, (k + 1) * D)
            acc1[k] += _mm_tn(x, da_ref[:, cols])
            acc2[k] += _mm_tn(jnp.square(jnp.maximum(a_ref[:, cols].astype(F32), 0.0)), ds_)

        @pl.when(i == nt - 1)
        def _():
            dw1_ref[...] = acc1[...].astype(MX)
            dw2_ref[...] = acc2[...].astype(MX)

    row = pl.BlockSpec((tm, D), lambda j, i: (i, 0))
    col = pl.BlockSpec((tm, nb * D), lambda j, i: (i, j))
    wj = pl.BlockSpec((nb, D, D), lambda j, i: (j, 0, 0))
    return pl.pallas_call(
        body, grid=(NSHARD // nb, nt),
        in_specs=[row, col, col, row], out_specs=[wj, wj],
        out_shape=[_sds((NSHARD, D, D), MX), _sds((NSHARD, D, D), MX)],
        scratch_shapes=[pltpu.VMEM((nb, D, D), F32), pltpu.VMEM((nb, D, D), F32)],
        name="ffn_bwd_w", compiler_params=_cp(("parallel", "arbitrary"), FFN_VMEM))(x1, da, a, ds)


def _loss_head(y, target):
    tm = 512

    def body(y_ref, t_ref, dy_ref, l_ref):
        @pl.when(pl.program_id(0) == 0)
        def _():
            l_ref[...] = jnp.zeros_like(l_ref)

        e = y_ref[...] - t_ref[...]
        dy_ref[...] = e * (1.0 / D)
        l_ref[...] += jnp.sum(jnp.sum(e * e, axis=1, keepdims=True), axis=0, keepdims=True) * (0.5 / D)

    row = pl.BlockSpec((tm, D), lambda i: (i, 0))
    return pl.pallas_call(
        body, grid=(N // tm,), in_specs=[row, row],
        out_specs=[row, pl.BlockSpec((8, 128), lambda i: (0, 0))],
        out_shape=[_sds((N, D)), _sds((8, 128))], name="loss_head", compiler_params=_cp(("arbitrary",)))(y, target)


def _s5_discretize(a_re, a_im, log_step, b_re, b_im):
    lam = lax.complex(a_re, a_im)
    lam_bar = jnp.exp(lam * jnp.exp(log_step))
    b_bar = ((lam_bar - 1.0) / lam)[..., None] * lax.complex(b_re, b_im)
    return jnp.real(lam_bar), jnp.imag(lam_bar), jnp.real(b_bar), jnp.imag(b_bar)


def _s5_in_blocks(b):
    e = jnp.eye(8, dtype=F32)
    return jnp.einsum('ij,zbjph->zbihjp', e, b.reshape(2, 2, 8, S5_P, S5_H)).reshape(2, 2, 128, SW)


def _s5_in_unblocks(d):
    return jnp.einsum('zbihip->zbiph', d.reshape(2, 2, 8, S5_H, 8, S5_P)).reshape(2, S5_G, S5_P, S5_H)


def _s5_out_blocks(c):
    e = jnp.eye(8, dtype=F32)
    return jnp.einsum('ij,zbjhp->zbjpih', e, c.reshape(2, 2, 8, S5_H, S5_P)).reshape(2, 2, SW, 128)


def _s5_out_unblocks(d):
    return jnp.einsum('zbipih->zbihp', d.reshape(2, 2, 8, S5_P, 8, S5_H)).reshape(2, S5_G, S5_H, S5_P)


def _gate_weight(w_a):
    z = jnp.zeros((16, 128), F32)
    top = jnp.concatenate([w_a[0], z], axis=1)
    bot = jnp.concatenate([z, w_a[1]], axis=1)
    return jnp.concatenate([top, bot, jnp.zeros((96, 256), F32)], axis=0)


def _layer_prep(p):
    lr, li, br, bi = _s5_discretize(p["s5_a_re"], p["s5_a_im"], p["s5_log_step"], p["s5_b_re"], p["s5_b_im"])
    q = dict(p)
    q["bre"] = _s5_in_blocks(br).astype(MX)
    q["bim"] = _s5_in_blocks(bi).astype(MX)
    q["cre"] = _s5_out_blocks(p["s5_c_re"]).astype(MX)
    q["cim"] = _s5_out_blocks(p["s5_c_im"]).astype(MX)
    mr, mi = lr.reshape(2, 1024), li.reshape(2, 1024)
    both = lambda t0, t1: tuple(jnp.stack(p) for p in zip(t0, t1))
    q["tab"] = both(_lockstep_tables(mr[0], mi[0], False), _lockstep_tables(mr[1], mi[1], True))
    q["tabc"] = both(_lockstep_tables(mr[0], -mi[0], True), _lockstep_tables(mr[1], -mi[1], False))
    q["dsk"] = p["s5_d"].reshape(1, 256)
    q["wa"] = _gate_weight(p["gla_w_a"]).astype(MX)
    q["ba"] = p["gla_b_a"].reshape(1, 256)
    q["lng"] = p["gla_ln_g"].reshape(1, 256)
    q["bv"] = p["s5_b_glu"][:256].reshape(1, 256)
    q["bg"] = p["s5_b_glu"][256:].reshape(1, 256)
    for k in ("ln1_g", "ln1_b", "ln2_g", "ln2_b"):
        q[k] = p[k].reshape(1, D)
    return q


def _layer_fwd(x, q, tk, fetch, target=None):
    q["w_in"] = fetch("w_in", x)
    h = _inproj_fwd(x, q["w_in"])
    hre, him, y2 = _s5_fwd(h, q["bre"], q["bim"], q["cre"], q["cim"], q["tab"])
    q["w4"] = fetch("s5_w_glu", y2)
    ya = _s5_glu_fwd(y2, h, q["dsk"], q["w4"], q["bv"], q["bg"])
    la2 = _gla_gate_fwd(h, q["wa"], q["ba"])
    of, ob, sf, sb = _gla_fwd(h, la2)
    yb = _gla_post_fwd(of, ob, h, q["lng"])
    yc = _swa_fwd(h, tk, q["swa_sink"])
    q["w_out"] = fetch("w_out", yc)
    s1, x1 = _outproj_fwd(ya, yb, yc, x, q["w_out"], q["ln1_g"], q["ln1_b"])
    q["w_ff1"] = fetch("w_ff1", x1)
    q["w_ff2"] = fetch("w_ff2", x1)
    a, s2, *out = _ffn_fwd(x1, q["w_ff1"], q["w_ff2"], q["ln2_g"], q["ln2_b"], target)
    saved = dict(x=x, h=h, hre=hre, him=him, y2=y2, ya=ya, la2=la2, of=of, ob=ob, sf=sf, sb=sb, yb=yb, yc=yc,
                 s1=s1, x1=x1, a=a, s2=s2)
    return (out[0] if target is None else tuple(out)), saved


def _layer_bwd(dy, q, sv, tk, emit):
    g = {}
    da, ds2, dx1, g["dg2"], g["db2"] = _ffn_bwd_act(dy, sv["s2"], sv["a"], q["w_ff1"], q["w_ff2"], q["ln2_g"])
    dw1, dw2 = _ffn_bwd_w(sv["x1"], da, sv["a"], ds2)
    tie = emit(dict(w_ff1=dw1, w_ff2=dw2))
    dya, dyb, dyc, dxp, dwo, g["dg1"], g["db1"] = _outproj_bwd(dx1, sv["s1"], sv["ya"], sv["yb"], sv["yc"],
                                                               q["w_out"], q["ln1_g"] + tie)
    h = sv["h"]
    daq, dakv, g["dsink"] = _swa_bwd(h, tk, q["swa_sink"], dyc)
    do, gr, g["dlng"] = _gla_post_bwd(sv["of"], sv["ob"], h, q["lng"], dyb)
    gq_f, gk_f, gv_f, gl_f, gq_b, gk_b, gv_b, gl_b = _gla_bwd(h, sv["la2"], do, sv["sf"], sv["sb"])
    dhl, g["dwa"], g["dba"] = _gla_gate_bwd(h, q["wa"], q["ba"], gl_f, gl_b)
    dyp, dud, g["dd"], dw4, g["dbv"], g["dbg"] = _s5_glu_bwd(sv["y2"], h, q["dsk"], q["w4"], q["bv"], q["bg"], dya)
    tie = emit(dict(w_out=dwo.reshape(NSHARD, D // NSHARD, D), s5_w_glu=dw4))
    du2, g["dbre"], g["dbim"], g["dcre"], g["dcim"], g["dmu"] = _s5_bwd(
        h, dyp, sv["hre"], sv["him"], q["bre"], q["bim"], q["cre"], q["cim"], (q["tabc"][0], q["tabc"][1] + tie))
    dx, dwt = _inproj_bwd(sv["x"], q["w_in"], dxp, du2, dud, gq_f, gq_b, gk_f, gk_b, gv_f, gv_b, gr, daq, dakv, dhl)
    tie = emit(dict(w_in=dwt))
    return dx, g, tie


NATIVE = ("dmu", "dbre", "dbim", "dcre", "dcim", "dd", "dbv", "dbg", "dwa", "dba", "dlng", "dsink",
          "dg1", "db1", "dg2", "db2", "loss")
ICI_CORE = (0, 0, 0, 1, 1, 0, 0, 0, 1, 1, 1, 1, 0, 0, 1, 1, 0)


def _finish_small(n, w):
    g = {}
    dmu = n["dmu"]
    dlr = dmu[:, :, :, 0].reshape(DEPTH, 2, S5_G, S5_P)
    dli = dmu[:, :, :, 1].reshape(DEPTH, 2, S5_G, S5_P)

    def unblock(c, perm, shape):
        return c.reshape(DEPTH, 2, 2, S5_H, 8, S5_P).transpose(perm).reshape(shape)

    b_shape, c_shape = (DEPTH, 2, S5_G, S5_P, S5_H), (DEPTH, 2, S5_G, S5_H, S5_P)
    _, vjp = jax.vjp(_s5_discretize, w["s5_a_re"], w["s5_a_im"], w["s5_log_step"], w["s5_b_re"], w["s5_b_im"])
    (g["s5_a_re"], g["s5_a_im"], g["s5_log_step"], g["s5_b_re"], g["s5_b_im"]) = vjp(
        (dlr, dli, unblock(n["dbre"], (0, 1, 2, 4, 5, 3), b_shape), unblock(n["dbim"], (0, 1, 2, 4, 5, 3), b_shape)))
    g["s5_c_re"] = unblock(n["dcre"], (0, 1, 2, 4, 3, 5), c_shape)
    g["s5_c_im"] = unblock(n["dcim"], (0, 1, 2, 4, 3, 5), c_shape)
    g["s5_d"] = n["dd"].reshape(DEPTH, S5_G, S5_H)
    g["s5_b_glu"] = jnp.concatenate([n["dbv"], n["dbg"]], axis=2).reshape(DEPTH, 512)
    g["gla_w_a"] = jnp.stack([n["dwa"][:, 0:16, 0:128], n["dwa"][:, 16:32, 128:256]], axis=1)
    g["gla_b_a"] = n["dba"].reshape(DEPTH, 2, 128)
    g["gla_ln_g"] = n["dlng"].reshape(DEPTH, 256)
    g["swa_sink"] = n["dsink"][:, :, 0]
    for k, s in (("ln1_g", "dg1"), ("ln1_b", "db1"), ("ln2_g", "dg2"), ("ln2_b", "db2")):
        g[k] = n[s].reshape(DEPTH, D)
    return g


def _local_step(x, target, qs, tk, fetch, emit):
    saved = []
    for l, q in enumerate(qs):
        x, sv = _layer_fwd(x, q, tk, functools.partial(fetch, l), target if l == DEPTH - 1 else None)
        saved.append(sv)
    dy, lacc = x
    smalls = [None] * DEPTH
    tie = 0.0
    for l in reversed(range(DEPTH)):
        qs[l]["ln2_g"] = qs[l]["ln2_g"] + tie
        dy, smalls[l], tie = _layer_bwd(dy, qs[l], saved[l], tk, functools.partial(emit, l))
    smalls[0]["db2"] = smalls[0]["db2"] + tie
    for l in range(DEPTH):
        smalls[l]["loss"] = lacc if l == 0 else jnp.zeros_like(lacc)
    return lacc[0, 0], dy, smalls


BIG = ("w_in", "s5_w_glu", "w_out", "w_ff1", "w_ff2")
SMALL = ("s5_a_re", "s5_a_im", "s5_log_step", "s5_b_re", "s5_b_im", "s5_c_re", "s5_c_im", "s5_d", "s5_b_glu",
         "gla_w_a", "gla_b_a", "gla_ln_g", "swa_sink", "ln1_g", "ln1_b", "ln2_g", "ln2_b")
ANY = pl.BlockSpec(memory_space=pl.ANY)


def _place():
    x, y, c = lax.axis_index("x"), lax.axis_index("y"), lax.axis_index("c")
    return x, y, c, [(1 - x, y), (x, 1 - y), (1 - x, 1 - y)]


HBM = pl.BlockSpec(memory_space=pltpu.HBM)
SEMS = pl.BlockSpec(memory_space=pltpu.SEMAPHORE)
EFFECT = pltpu.SideEffectType.DATAFLOW_SIDE_EFFECTING


def _push_copies(ins, lands, send, recv, gather, sending):
    x, y, c, chips = _place()
    me = 2 * x + y
    if gather == "sibling":
        return [pltpu.make_async_remote_copy(src_ref=ins[a], dst_ref=lands[a], send_sem=send.at[a], recv_sem=recv.at[a],
                                             device_id=(x, y, 1 - c), device_id_type=MESH) for a in range(len(lands))]
    out = []
    for a in range(len(lands)):
        for j, (px, py) in enumerate(chips):
            peer = 2 * px + py
            src = lands[a].at[me] if gather else ins[a].at[peer if sending else me]
            dst = lands[a].at[me if sending else peer]
            out.append(pltpu.make_async_remote_copy(src_ref=src, dst_ref=dst, send_sem=send.at[3 * a + j],
                                                    recv_sem=recv.at[3 * a + j], device_id=(px, py, c),
                                                    device_id_type=MESH))
    return out


def _push_start(name, arrs, gather):
    n = len(arrs)
    ops = list(arrs) if gather is True else list(arrs) + [lax.empty(s.shape, s.dtype) for s in arrs]
    m = len(ops)

    def body(*refs):
        ins, lnd = (refs[:n], refs[:n]) if gather is True else (refs[:n], refs[n:m])
        for cp in _push_copies(ins, lnd, refs[m], refs[m + 1], gather, True):
            cp.start()
        refs[-1][...] = jnp.zeros((8, 128), F32)

    ops = [pltpu.with_memory_space_constraint(t, pltpu.HBM) for t in ops]
    res = pl.pallas_call(
        body, name=name,
        out_shape=(pltpu.SemaphoreType.DMA((3 * n,)), pltpu.SemaphoreType.DMA((3 * n,)),
                   *[pltpu.HBM(t.shape, t.dtype) for t in ops], _sds((8, 128))),
        in_specs=[HBM] * m,
        out_specs=(SEMS, SEMS, *[HBM] * m, pl.BlockSpec(memory_space=pltpu.VMEM)),
        input_output_aliases={i: 2 + i for i in range(m)},
        compiler_params=pltpu.CompilerParams(has_side_effects=EFFECT))(*ops)
    return res[0], res[1], list(res[2:2 + m]), res[-1]


def _push_wait(name, started, after, gather):
    send, recv, ops, _ = started
    m = len(ops)
    n = m if gather is True else m // 2

    def body(*refs):
        ins, lnd = (refs[:n], refs[:n]) if gather is True else (refs[:n], refs[n:m])
        for cp in _push_copies(ins, lnd, refs[m], refs[m + 1], gather, False):
            cp.wait_send()
            cp.wait_recv()

    res = pl.pallas_call(
        body, name=name,
        out_shape=[pltpu.HBM(t.shape, t.dtype) for t in ops],
        in_specs=[HBM] * m + [SEMS, SEMS, ANY], out_specs=[HBM] * m,
        input_output_aliases={i: i for i in range(m)},
        compiler_params=pltpu.CompilerParams(has_side_effects=EFFECT))(*ops, send, recv, after)
    return list(res)


def _row_tile(rows):
    return max(t for t in range(8, min(rows, 512) + 1, 8) if rows % t == 0)


def _cast_to_slot(me, w, l):
    _, rows, cols = w.shape
    tr = _row_tile(rows)

    def body(me_ref, w_ref, o_ref):
        o_ref[0] = w_ref[0].astype(MX)

    return pl.pallas_call(
        body,
        grid_spec=pltpu.PrefetchScalarGridSpec(
            num_scalar_prefetch=1, grid=(rows // tr,),
            in_specs=[pl.BlockSpec((1, tr, cols), lambda i, me_: (l, i, 0))],
            out_specs=pl.BlockSpec((1, tr, cols), lambda i, me_: (me_[0], i, 0))),
        out_shape=_sds((NSHARD, rows, cols), MX), name="cast_to_slot", compiler_params=_cp(("arbitrary",)))(me, w)


def _sum_sources(me, recv, own):
    _, rows, cols = recv[0].shape
    tr = min(_row_tile(rows), 256) if rows % 256 == 0 else _row_tile(rows)
    nt = rows // tr

    def body(me_ref, *refs):
        o_ref = refs[-1]
        for l in range(DEPTH):
            @pl.when(pl.program_id(0) == l)
            def _():
                r_ref, own_ref = refs[2 * l], refs[2 * l + 1]
                part = [jnp.where(me_ref[0] == s, own_ref[0], r_ref[s]).astype(F32) for s in range(NSHARD)]
                o_ref[...] = ((part[0] + part[1]) + part[2]) + part[3]

    in_specs = []
    for l in range(DEPTH):
        pick = lambda g, i, me_, l=l: jnp.where(g == l, i, jnp.where(g < l, 0, nt - 1))
        in_specs += [pl.BlockSpec((NSHARD, tr, cols), lambda g, i, me_, pick=pick: (0, pick(g, i, me_), 0)),
                     pl.BlockSpec((1, tr, cols), lambda g, i, me_, pick=pick: (me_[0], pick(g, i, me_), 0))]
    return pl.pallas_call(
        body,
        grid_spec=pltpu.PrefetchScalarGridSpec(
            num_scalar_prefetch=1, grid=(DEPTH, nt), in_specs=in_specs,
            out_specs=pl.BlockSpec((tr, cols), lambda g, i, me_: (g * nt + i, 0))),
        out_shape=_sds((DEPTH * rows, cols)), name="sum_sources",
        compiler_params=_cp(("arbitrary", "arbitrary")))(me, *[t for l in range(DEPTH) for t in (recv[l], own[l])])


def _swap_sibling(arrs):
    n = len(arrs)

    def body(*refs):
        ins, outs = refs[:n], refs[n:2 * n]
        send, recv = refs[2 * n:]
        x, y, c, _ = _place()
        cps = [pltpu.make_async_remote_copy(src_ref=ins[a], dst_ref=outs[a], send_sem=send.at[a], recv_sem=recv.at[a],
                                            device_id=(x, y, 1 - c), device_id_type=MESH) for a in range(n)]
        for cp in cps:
            cp.start()
        for cp in cps:
            cp.wait()

    return pl.pallas_call(
        body, in_specs=[ANY] * n, out_specs=[ANY] * n, out_shape=[_sds(a.shape, a.dtype) for a in arrs],
        scratch_shapes=[pltpu.SemaphoreType.DMA((n,)), pltpu.SemaphoreType.DMA((n,))],
        name="swap_sibling")(*arrs)


def _allreduce_small(per_layer):
    nk = len(per_layer[0])
    n = DEPTH * nk
    shapes = [a.shape for a in per_layer[0]]

    def body(*refs):
        ins, outs = refs[:n], refs[n:n + nk]
        sibs, slots = refs[n + nk:n + 2 * nk], refs[n + 2 * nk:n + 3 * nk]
        send, recv = refs[n + 3 * nk:]
        x, y, c, chips = _place()
        me = 2 * x + y
        d2d = [pltpu.make_async_remote_copy(src_ref=ins[l * nk + k], dst_ref=sibs[k].at[l], send_sem=send.at[l * nk + k],
                                            recv_sem=recv.at[l * nk + k], device_id=(x, y, 1 - c), device_id_type=MESH)
               for l in range(DEPTH) for k in range(nk)]
        for cp in d2d:
            cp.start()
        for cp in d2d:
            cp.wait()
        for l in range(DEPTH):
            for k in range(nk):
                slots[k][0, l] = ins[l * nk + k][...] + sibs[k][l]

        def swap(k, stage):
            peer = (1 - x, y, c) if stage == 0 else (x, 1 - y, c)
            return pltpu.make_async_remote_copy(src_ref=slots[k].at[2 * stage], dst_ref=slots[k].at[2 * stage + 1],
                                                send_sem=send.at[n + 3 * k + stage], recv_sem=recv.at[n + 3 * k + stage],
                                                device_id=peer, device_id_type=MESH)

        def handover(k):
            return pltpu.make_async_remote_copy(src_ref=outs[k], dst_ref=outs[k], send_sem=send.at[n + 3 * nk + k],
                                                recv_sem=recv.at[n + 3 * nk + k], device_id=(x, y, 1 - c),
                                                device_id_type=MESH)

        halves = (tuple(k for k in range(nk) if ICI_CORE[k] == 0), tuple(k for k in range(nk) if ICI_CORE[k] == 1))
        for cc in range(2):
            @pl.when(c == cc)
            def _():
                mine, theirs = halves[cc], halves[1 - cc]
                for stage in range(2):
                    cps = [swap(k, stage) for k in mine]
                    for cp in cps:
                        cp.start()
                    for cp in cps:
                        cp.wait()
                    for k in mine:
                        if stage == 0:
                            slots[k][2] = slots[k][0] + slots[k][1]
                        else:
                            outs[k][...] = slots[k][2] + slots[k][3]
                over = [handover(k) for k in mine]
                for cp in over:
                    cp.start()
                for k in theirs:
                    handover(k).wait_recv()
                for cp in over:
                    cp.wait_send()

    vm = pl.BlockSpec(memory_space=pltpu.VMEM)
    return pl.pallas_call(
        body, in_specs=[vm] * n, out_specs=[vm] * nk, out_shape=[_sds((DEPTH,) + s) for s in shapes],
        scratch_shapes=([pltpu.VMEM((DEPTH,) + s, F32) for s in shapes]
                        + [pltpu.VMEM((NSHARD, DEPTH) + s, F32) for s in shapes]
                        + [pltpu.SemaphoreType.DMA((n + 4 * nk,)), pltpu.SemaphoreType.DMA((n + 4 * nk,))]),
        name="allreduce_small", compiler_params=pltpu.CompilerParams(vmem_limit_bytes=VMEM_LIMIT))(
            *[a for layer in per_layer for a in layer])


def _adamw_math(w, g, m, v):
    m = ADAM_B1 * m + (1.0 - ADAM_B1) * g
    v = ADAM_B2 * v + (1.0 - ADAM_B2) * jnp.square(g)
    m_hat = m / (1.0 - ADAM_B1 ** ADAM_STEP)
    v_hat = v / (1.0 - ADAM_B2 ** ADAM_STEP)
    delta = -ADAM_LR * (m_hat / (jnp.sqrt(v_hat) + ADAM_EPS) + ADAM_WD * w)
    return delta, m, v


def _adamw(g_parts, w, m, v):
    rows, cols = w.shape
    tr = 256 if rows % 256 == 0 else _row_tile(rows)
    k = len(g_parts)

    def body(*refs):
        g = refs[0][...]
        for r in refs[1:k]:
            g = g + r[...]
        w_ref, m_ref, v_ref, go, do, mo, vo = refs[k:]
        d, mn, vn = _adamw_math(w_ref[...], g, m_ref[...], v_ref[...])
        go[...] = g
        do[...] = d
        mo[...] = mn
        vo[...] = vn

    spec = pl.BlockSpec((tr, cols), lambda i: (i, 0))
    return pl.pallas_call(
        body, grid=(rows // tr,), in_specs=[spec] * (k + 3), out_specs=[spec] * 4,
        out_shape=[_sds((rows, cols))] * 4, name="adamw", compiler_params=_cp(("parallel",)))(*g_parts, w, m, v)


def _adamw_small(gs, ws, ms, vs):
    n = len(gs)

    def body(*refs):
        for k in range(n):
            d, mn, vn = _adamw_math(refs[n + k][...], refs[k][...], refs[2 * n + k][...], refs[3 * n + k][...])
            refs[4 * n + k][...] = d
            refs[5 * n + k][...] = mn
            refs[6 * n + k][...] = vn

    vm = pl.BlockSpec(memory_space=pltpu.VMEM)
    shapes = [_sds(a.shape) for a in ws]
    res = pl.pallas_call(
        body, in_specs=[vm] * (4 * n), out_specs=[vm] * (3 * n), out_shape=shapes * 3, name="adamw_small",
        compiler_params=pltpu.CompilerParams(vmem_limit_bytes=VMEM_LIMIT))(*gs, *ws, *ms, *vs)
    return res[:n], res[n:2 * n], res[2 * n:]


_ARGS = ("x", "w_in", "s5_a_re", "s5_a_im", "s5_log_step", "s5_b_re", "s5_b_im", "s5_c_re", "s5_c_im", "s5_d",
         "s5_w_glu", "s5_b_glu", "gla_w_a", "gla_b_a", "gla_ln_g", "swa_sink", "w_out", "ln1_g", "ln1_b", "w_ff1",
         "w_ff2", "ln2_g", "ln2_b")
_WEIGHTS = _ARGS[1:]


def _shard_cols(d):
    return d.reshape(d.shape[0], NSHARD, d.shape[1] // NSHARD).transpose(1, 0, 2)


def kernel(x, w_in, s5_a_re, s5_a_im, s5_log_step, s5_b_re, s5_b_im, s5_c_re, s5_c_im, s5_d, s5_w_glu, s5_b_glu, gla_w_a, gla_b_a, gla_ln_g, swa_sink, w_out, ln1_g, ln1_b, w_ff1, w_ff2, ln2_g, ln2_b, loss_target, m_w_in, m_s5_a_re, m_s5_a_im, m_s5_log_step, m_s5_b_re, m_s5_b_im, m_s5_c_re, m_s5_c_im, m_s5_d, m_s5_w_glu, m_s5_b_glu, m_gla_w_a, m_gla_b_a, m_gla_ln_g, m_swa_sink, m_w_out, m_ln1_g, m_ln1_b, m_w_ff1, m_w_ff2, m_ln2_g, m_ln2_b, v_w_in, v_s5_a_re, v_s5_a_im, v_s5_log_step, v_s5_b_re, v_s5_b_im, v_s5_c_re, v_s5_c_im, v_s5_d, v_s5_w_glu, v_s5_b_glu, v_gla_w_a, v_gla_b_a, v_gla_ln_g, v_swa_sink, v_w_out, v_ln1_g, v_ln1_b, v_w_ff1, v_w_ff2, v_ln2_g, v_ln2_b):
    given = dict(locals())
    w = {k: given[k] for k in _WEIGHTS}
    mom = {k: given["m_" + k] for k in _WEIGHTS}
    var = {k: given["v_" + k] for k in _WEIGHTS}

    me = (2 * lax.axis_index("x") + lax.axis_index("y")).astype(jnp.int32).reshape(1)
    tr = lambda t: t.transpose(0, 2, 1)
    shard = {k: (tr(w[k]) if k == "w_in" else w[k]) for k in BIG}
    qs = [None] * DEPTH

    first = ("w_in", "s5_w_glu", "w_out")
    follow = {(0, "w_in"): [(0, BIG[3:]), (1, first)], (0, "w_ff1"): [(1, BIG[3:])]}
    gathers = {}

    def start_gather(l, names, behind=None):
        lands = [_cast_to_slot(me, shard[k], l) for k in names]
        if behind is not None:
            lands, behind = lax.optimization_barrier((lands, behind))
        st = _push_start(f"gather_start_{l}_{names[0]}", lands, True)
        for k in names:
            gathers[l, k] = [names, st, None]
        return st[-1], behind

    token = start_gather(0, first[:1])[0] + start_gather(0, first[1:])[0]
    zero = token[0, 0]
    for l in range(DEPTH):
        qs[l] = _layer_prep({k: (w[k][l] + zero if k == "s5_a_re" else w[k][l]) for k in SMALL})
        token = token + qs[l]["tabc"][1][0, 0, 0, :8, :128] + qs[l]["bre"][0, 0, :8, :128].astype(F32)

    def fetch(l, name, after):
        names, st, got = gathers[l, name]
        tie = None
        if got is None:
            if l == 0 and name == "w_in":
                after = token
            lands = _push_wait(f"gather_wait_{l}_{names[0]}", st, after, True)
            for l2, names2 in follow.get((l, name), ()):
                tok, lands[0] = start_gather(l2, names2, lands[0])
                tie = tok if tie is None else tie + tok
            got = dict(zip(names, lands))
            for k in names:
                gathers[l, k][2] = got
        full = got[name]
        if name == "w_in":
            return _in_rows(full, token if tie is None else tie)
        if tie is not None:
            qs[l]["ln2_b"] = qs[l]["ln2_b"] + tie[0, 0]
        return full.reshape(D, D) if name == "w_out" else full

    scatters, held = [], {}

    def emit(l, grads):
        if l > 0:
            held.update(grads)
            if "w_in" not in grads:
                return 0.0
            grads = dict(held)
            held.clear()
        names = tuple(grads)
        st = _push_start(f"scatter_start_{l}_{names[0]}", [grads[k] for k in names], False)
        scatters.append((l, names, st))
        return st[-1][0, 0]

    loss, dx, smalls = _local_step(x.reshape(N, D), loss_target.reshape(N, D), qs, _rope_tables(128), fetch, emit)

    out, recv, own = {}, {}, {}

    def collect(keys, after):
        for l, names, st in scatters:
            if names[0] in keys:
                ops = _push_wait(f"scatter_wait_{l}_{names[0]}", st, after, False)
                for i, k in enumerate(names):
                    own[l, k], recv[l, k] = ops[i], ops[len(names) + i]

    def to_sibling(keys):
        sums = [_sum_sources(me, [recv[l, k] for l in range(DEPTH)], [own[l, k] for l in range(DEPTH)]) for k in keys]
        return _push_start(f"swap_start_{keys[0]}", sums, "sibling")

    def apply(keys, started, after):
        ops = _push_wait(f"swap_wait_{keys[0]}", started, after, "sibling")
        for i, k in enumerate(keys):
            mine, other = ops[i], ops[len(keys) + i]
            shp = shard[k].shape
            r = _adamw([mine, other], *((tr(t[k]) if k == "w_in" else t[k]).reshape(-1, shp[-1]) for t in (w, mom, var)))
            r = [t.reshape(shp) for t in r]
            out[k] = [tr(t) for t in r] if k == "w_in" else r
        return out[keys[-1]][1]

    collect(("w_ff1", "w_ff2", "w_out", "s5_w_glu"), dx)
    ff = to_sibling(("w_ff1", "w_ff2"))
    mix = to_sibling(("w_out", "s5_w_glu"))
    smalls[0]["db1"] = smalls[0]["db1"] + mix[-1][0, 0]
    native = _allreduce_small([[smalls[l][k] for k in NATIVE] for l in range(DEPTH)])
    native = dict(zip(NATIVE, native))
    loss = native["loss"][0, 0, 0] + native["loss"][1, 0, 0]
    gsmall = _finish_small(native, w)
    res = _adamw_small(*([t[k] for k in SMALL] for t in (gsmall, w, mom, var)))
    for i, k in enumerate(SMALL):
        out[k] = [gsmall[k], res[0][i], res[1][i], res[2][i]]
    last = apply(("w_ff1", "w_ff2"), ff, res[0][-1])
    collect(("w_in",), last)
    win = to_sibling(("w_in",))
    last = apply(("w_out", "s5_w_glu"), mix, win[-1])
    apply(("w_in",), win, last)

    return (loss, dx.reshape(NSEQ, L, D), *[out[k][0] for k in _WEIGHTS], *[out[k][1] for k in _WEIGHTS],
            *[out[k][2] for k in _WEIGHTS], *[out[k][3] for k in _WEIGHTS])
```

```python
import functools
import math

import jax
import jax.numpy as jnp
from jax import lax
from jax.experimental import pallas as pl
from jax.experimental.pallas import tpu as pltpu

F32 = jnp.float32
MX = jnp.bfloat16
MESH = pl.DeviceIdType.MESH

DEPTH = 2
NSEQ = 2
L = 2048
N = NSEQ * L
D = 1024
DFF = 4096
NSHARD = 4
S5_G, S5_H, S5_P = 16, 16, 64
GLA_CHUNK = 64
NCHUNK = L // GLA_CHUNK
GLA_GROUP = 4
NGROUP = NCHUNK // GLA_GROUP
SWA_BLK = 128
NBLK = L // SWA_BLK
ROT = 16
ROPE_THETA = 500000.0
LN_EPS = 1e-5
ALPHA = (2 * DEPTH) ** 0.25
NEG_BIG = -1e30
DIN = 1824
DINP = 1920
ADAM_LR, ADAM_B1, ADAM_B2, ADAM_EPS, ADAM_WD, ADAM_STEP = 0.001, 0.9, 0.999, 1e-08, 0.01, 10
VMEM_LIMIT = 56 * 1024 * 1024
TT = 512
SW = 512
FFN_TM = 512
FFN_TM_W = 1024
FFN_WB = 1
FFN_VMEM = 60 * 1024 * 1024
INPROJ_BWD_TM = 512


def _cp(sem, vmem=VMEM_LIMIT):
    return pltpu.CompilerParams(dimension_semantics=sem, vmem_limit_bytes=vmem)


def _mm(a, b):
    return jnp.dot(a.astype(MX), b.astype(MX), preferred_element_type=F32)


def _mm_nt(a, b):
    return lax.dot_general(a.astype(MX), b.astype(MX), (((1,), (1,)), ((), ())), preferred_element_type=F32)


def _mm_tn(a, b):
    return lax.dot_general(a.astype(MX), b.astype(MX), (((0,), (0,)), ((), ())), preferred_element_type=F32)


@jax.custom_vjp
def _dmm(a, b):
    return _mm(a, b)


_dmm.defvjp(lambda a, b: (_mm(a, b), (a, b)), lambda r, g: (_mm_nt(g, r[1]), _mm_tn(r[0], g)))


@jax.custom_vjp
def _dmm_nt(a, b):
    return _mm_nt(a, b)


_dmm_nt.defvjp(lambda a, b: (_mm_nt(a, b), (a, b)), lambda r, g: (_mm(g, r[1]), _mm_tn(g, r[0])))


@jax.custom_vjp
def _dmm_tn(a, b):
    return _mm_tn(a, b)


_dmm_tn.defvjp(lambda a, b: (_mm_tn(a, b), (a, b)), lambda r, g: (_mm_nt(r[1], g), _mm(r[0], g)))


def _split3(x):
    hi = x.astype(MX)
    r1 = x - hi.astype(F32)
    mid = r1.astype(MX)
    lo = (r1 - mid.astype(F32)).astype(MX)
    return hi, mid, lo


def _chunk_pairs(rows, rev, strict):
    r = lax.broadcasted_iota(jnp.int32, (rows, rows), 0)
    c = lax.broadcasted_iota(jnp.int32, (rows, rows), 1)
    order = ((c > r) if strict else (c >= r)) if rev else ((c < r) if strict else (c <= r))
    return (r // GLA_CHUNK == c // GLA_CHUNK) & order


def _cums_impl(x, rev):
    rows, w = x.shape
    t = jnp.where(_chunk_pairs(rows, rev, False), 1.0, 0.0).astype(MX)
    s = jnp.dot(t, jnp.concatenate(_split3(x), axis=1), preferred_element_type=F32)
    return s[:, 0:w] + s[:, w:2 * w] + s[:, 2 * w:3 * w]


@functools.partial(jax.custom_vjp, nondiff_argnums=(1,))
def _cums(x, rev):
    return _cums_impl(x, rev)


_cums.defvjp(lambda x, rev: (_cums_impl(x, rev), None), lambda rev, r, g: (_cums_impl(g, not rev),))


def _ln_fwd(s, g, b):
    mu = jnp.mean(s, axis=-1, keepdims=True)
    xc = s - mu
    var = jnp.mean(xc * xc, axis=-1, keepdims=True)
    return xc * lax.rsqrt(var + LN_EPS) * g + b


def _ln_bwd(dy, s, g):
    mu = jnp.mean(s, axis=-1, keepdims=True)
    xc = s - mu
    var = jnp.mean(xc * xc, axis=-1, keepdims=True)
    rstd = lax.rsqrt(var + LN_EPS)
    xhat = xc * rstd
    dxh = dy * g
    ds = rstd * (dxh - jnp.mean(dxh, axis=-1, keepdims=True) - xhat * jnp.mean(dxh * xhat, axis=-1, keepdims=True))
    return ds, jnp.sum(dy * xhat, axis=0, keepdims=True), jnp.sum(dy, axis=0, keepdims=True)


def _sds(shape, dtype=F32):
    return jax.ShapeDtypeStruct(shape, dtype)


_IN_ROW_PIECES = (((0, 0), (0, 456)), ((1, 0), (456, 456)), ((2, 0), (912, 112)), ((2, 112), (1792, 32)),
                  ((2, 144), (1024, 312)), ((3, 0), (1336, 456)))


def _in_rows(g4, behind):
    def body(g_ref, behind_ref, o_ref, tmp):
        tmp[DIN:DINP] = jnp.zeros((DINP - DIN, D), F32)
        for (j, s0), (d0, n_) in _IN_ROW_PIECES:
            tmp[d0:d0 + n_] = g_ref[j, s0:s0 + n_].astype(F32)
        o_ref[...] = tmp[...].astype(MX)

    vm = pl.BlockSpec(memory_space=pltpu.VMEM)
    return pl.pallas_call(body, in_specs=[vm, pl.BlockSpec(memory_space=pl.ANY)], out_specs=vm,
                          out_shape=_sds((DINP, D), MX), scratch_shapes=[pltpu.VMEM((DINP, D), F32)], name="in_rows",
                          compiler_params=pltpu.CompilerParams(vmem_limit_bytes=VMEM_LIMIT))(g4, behind)


def _inproj_fwd(x, wt):
    tm = 512

    def body(x_ref, w_ref, h_ref):
        h_ref[...] = _mm_nt(x_ref[...], w_ref[...])

    return pl.pallas_call(
        body, grid=(N // tm,),
        in_specs=[pl.BlockSpec((tm, D), lambda i: (i, 0)), pl.BlockSpec((DINP, D), lambda i: (0, 0))],
        out_specs=pl.BlockSpec((tm, DINP), lambda i: (i, 0)),
        out_shape=_sds((N, DINP)), name="inproj_fwd", compiler_params=_cp(("parallel",)))(x, wt)


def _inproj_bwd(x, w, dxp, du2, dud, gq_f, gq_b, gk_f, gk_b, gv_f, gv_b, gr, daq, dakv, dhl):
    tm = INPROJ_BWD_TM
    nt = N // tm

    def body(x_ref, w_ref, dxp_ref, du2_ref, dud_ref, gqf, gqb, gkf, gkb, gvf, gvb, gr_ref, daq_ref, dakv_ref, dhl_ref,
             dx_ref, dw_ref, acc):
        i = pl.program_id(0)
        f = lambda r: r[...].astype(F32)
        dh = jnp.concatenate([
            du2_ref[0] + du2_ref[1] + f(dud_ref), f(gqf) + f(gqb), f(gkf) + f(gkb), f(gvf) + f(gvb),
            f(gr_ref), f(daq_ref), f(dakv_ref), f(dhl_ref)], axis=1)
        dx_ref[...] = dxp_ref[...] + _mm(dh, w_ref[...])
        contrib = _mm_tn(dh, x_ref[...])

        @pl.when(i == 0)
        def _():
            acc[...] = contrib

        @pl.when(i > 0)
        def _():
            acc[...] += contrib

        @pl.when(i == nt - 1)
        def _():
            for (j, d0), (s0, n_) in _IN_ROW_PIECES:
                dw_ref[j, d0:d0 + n_] = acc[s0:s0 + n_].astype(MX)

    row = lambda w_: pl.BlockSpec((tm, w_), lambda i: (i, 0))
    return pl.pallas_call(
        body, grid=(nt,),
        in_specs=[row(D), pl.BlockSpec((DINP, D), lambda i: (0, 0)), row(D),
                  pl.BlockSpec((2, tm, 256), lambda i: (0, i, 0)), row(256), row(128), row(128), row(128), row(128),
                  row(256), row(256), row(256), row(512), row(256), row(128)],
        out_specs=[row(D), pl.BlockSpec((NSHARD, DIN // NSHARD, D), lambda i: (0, 0, 0))],
        out_shape=[_sds((N, D)), _sds((NSHARD, DIN // NSHARD, D), MX)],
        scratch_shapes=[pltpu.VMEM((DINP, D), F32)],
        name="inproj_bwd", compiler_params=_cp(("arbitrary",)))(
            x, w, dxp, du2, dud, gq_f, gq_b, gk_f, gk_b, gv_f, gv_b, gr, daq, dakv, dhl)


def _scan_tables(mr, mi, reverse):
    pw = [(mr, mi)]
    for _ in range(7):
        pr, pi = pw[-1]
        pw.append((pr * mr - pi * mi, pr * mi + pi * mr))
    rows = jnp.arange(8)[:, None]
    out = []
    for d in (1, 2, 4):
        keep = rows >= d
        out += [jnp.where(keep, pw[d - 1][0][None], 0.0), jnp.where(keep, pw[d - 1][1][None], 0.0)]
    out += [jnp.stack([p[0] for p in pw]), jnp.stack([p[1] for p in pw])]
    t = jnp.stack(out)
    if reverse:
        t = t[:, ::-1, :]
    return t.reshape(8, 8, 2, SW).transpose(2, 0, 1, 3)


def _tile_scan(xr, xi, a, cr, ci, reverse):
    for lvl, d in enumerate((1, 2, 4)):
        sh = 8 - d if reverse else d
        sr = pltpu.roll(xr, sh, 0)
        si = pltpu.roll(xi, sh, 0)
        ar, ai = a[2 * lvl], a[2 * lvl + 1]
        xr, xi = xr + ar * sr - ai * si, xi + ar * si + ai * sr
    pr, pi = a[6], a[7]
    return xr + pr * cr - pi * ci, xi + pr * ci + pi * cr


NJ = TT // 8


def _lockstep_tables(mr, mi, reverse):
    nr, ni = mr, mi
    for _ in range(NJ.bit_length() - 1):
        nr, ni = nr * nr - ni * ni, 2.0 * nr * ni
    pr, pi = mr[None], mi[None]
    while pr.shape[0] < NJ:
        k = pr.shape[0]
        tr, ti = pr[k - 1], pi[k - 1]
        pr, pi = (jnp.concatenate([pr, pr * tr - pi * ti]), jnp.concatenate([pi, pr * ti + pi * tr]))
    if reverse:
        pr, pi = pr[::-1], pi[::-1]
    rows = jnp.broadcast_to(jnp.stack([mr, mi])[:, None, :], (2, 8, 2 * SW))
    link = _scan_tables(nr, ni, reverse)
    a = jnp.concatenate([rows.reshape(2, 8, 2, SW).transpose(2, 0, 1, 3), link], axis=1)
    return a, jnp.stack([pr, pi]).reshape(2, NJ, 2, SW).transpose(2, 0, 1, 3)


def _to_lockstep(ref, *lead):
    return jnp.concatenate([ref[(*lead, pl.ds(j, 8, stride=NJ), slice(None))] for j in range(NJ)], axis=0)


def _from_lockstep(val, ref, *lead):
    for j in range(NJ):
        ref[(*lead, pl.ds(j, 8, stride=NJ), slice(None))] = val[8 * j:8 * j + 8]


def _expand_powers(p_ref, pexp):
    for c in range(2):
        for j in range(NJ):
            pexp[c, j] = jnp.broadcast_to(p_ref[0, 0, c, j:j + 1, :], (8, SW))


def _lockstep_scan(xre, xim, a_ref, pexp, car, reverse, extra=None):
    a = [a_ref[0, 0, k] for k in range(10)]
    mr, mi = a[0], a[1]
    order = (lambda i: NJ - 1 - i) if reverse else (lambda i: i)

    def local(i, hcar):
        hr, hi = hcar
        r0 = pl.multiple_of(order(i) * 8, 8)
        hr, hi = mr * hr - mi * hi + xre[pl.ds(r0, 8), :], mr * hi + mi * hr + xim[pl.ds(r0, 8), :]
        xre[pl.ds(r0, 8), :] = hr
        xim[pl.ds(r0, 8), :] = hi
        return hr, hi

    z8 = jnp.zeros((8, SW), F32)
    er, ei = lax.fori_loop(0, NJ, local, (z8, z8), unroll=4)
    c0r, c0i = car[0], car[1]
    er, ei = _tile_scan(er, ei, a[2:], c0r, c0i, reverse)
    rowid = lax.broadcasted_iota(jnp.int32, (8, SW), 0)
    first, sh, last = (7, 7, 0) if reverse else (0, 1, 7)
    cvr = jnp.where(rowid == first, c0r, pltpu.roll(er, sh, 0))
    cvi = jnp.where(rowid == first, c0i, pltpu.roll(ei, sh, 0))
    car[0] = jnp.broadcast_to(er[last:last + 1, :], (8, SW))
    car[1] = jnp.broadcast_to(ei[last:last + 1, :], (8, SW))

    def fix(i, carry):
        j = order(i)
        r0 = pl.multiple_of(j * 8, 8)
        pr, pi = pexp[0, j], pexp[1, j]
        sr = xre[pl.ds(r0, 8), :] + pr * cvr - pi * cvi
        si = xim[pl.ds(r0, 8), :] + pr * cvi + pi * cvr
        xre[pl.ds(r0, 8), :] = sr
        xim[pl.ds(r0, 8), :] = si
        if extra is None:
            return carry
        return (sr, si, extra(r0, sr, si, carry[0], carry[1], carry[2]))

    init = (cvr, cvi, extra(None, None, None, None, None, None)) if extra is not None else 0
    return lax.fori_loop(0, NJ, fix, init, unroll=4)


def _s5_time_block(z, s, t, adjoint):
    flip = (1 - z) if adjoint else z
    return s * (L // TT) + t + flip * (L // TT - 1 - 2 * t)


def _s5_fwd(h, bre, bim, cre, cim, tab):
    nt = L // TT
    taba, tabp = tab

    def body(u_ref, bre_ref, bim_ref, cre_ref, cim_ref, a_ref, p_ref, hre_ref, him_ref, y_ref, car, pexp):
        z = pl.program_id(1)
        s = pl.program_id(2)
        tc = pl.program_id(3)

        @pl.when(tc == 0)
        def _():
            car[...] = jnp.zeros_like(car)

        @pl.when((tc == 0) & (s == 0))
        def _():
            _expand_powers(p_ref, pexp)

        u = _to_lockstep(u_ref)
        hre_ref[0] = _mm(u, bre_ref[0, 0])
        him_ref[0] = _mm(u, bim_ref[0, 0])

        @pl.when(z == 0)
        def _():
            _lockstep_scan(hre_ref.at[0], him_ref.at[0], a_ref, pexp, car, False)

        @pl.when(z == 1)
        def _():
            _lockstep_scan(hre_ref.at[0], him_ref.at[0], a_ref, pexp, car, True)

        _from_lockstep(_mm(hre_ref[0], cre_ref[0, 0]) - _mm(him_ref[0], cim_ref[0, 0]), y_ref, 0)

    tb = lambda b, z, s, t: _s5_time_block(z, s, t, False)
    wspec = lambda r, c: pl.BlockSpec((1, 1, r, c), lambda b, z, s, t: (z, b, 0, 0))
    return pl.pallas_call(
        body, grid=(2, 2, NSEQ, nt),
        in_specs=[pl.BlockSpec((TT, 128), lambda b, z, s, t: (tb(b, z, s, t), b)),
                  wspec(128, SW), wspec(128, SW), wspec(SW, 128), wspec(SW, 128),
                  pl.BlockSpec((1, 1, 10, 8, SW), lambda b, z, s, t: (z, b, 0, 0, 0)),
                  pl.BlockSpec((1, 1, 2, NJ, SW), lambda b, z, s, t: (z, b, 0, 0, 0))],
        out_specs=[pl.BlockSpec((1, TT, SW), lambda b, z, s, t: (z, tb(b, z, s, t), b)),
                   pl.BlockSpec((1, TT, SW), lambda b, z, s, t: (z, tb(b, z, s, t), b)),
                   pl.BlockSpec((1, TT, 128), lambda b, z, s, t: (z, tb(b, z, s, t), b))],
        out_shape=[_sds((2, N, 2 * SW)), _sds((2, N, 2 * SW)), _sds((2, N, 256))],
        scratch_shapes=[pltpu.VMEM((2, 8, SW), F32), pltpu.VMEM((2, NJ, 8, SW), F32)],
        name="s5_fwd", compiler_params=_cp(("arbitrary",) * 4))(h, bre, bim, cre, cim, taba, tabp)


def _s5_bwd(h, dyp, hre, him, bre, bim, cre, cim, tabc):
    nt = L // TT
    taba, tabp = tabc

    def body(u_ref, dy_ref, hre_ref, him_ref, bre_ref, bim_ref, cre_ref, cim_ref, a_ref, p_ref,
             du_ref, dbre_ref, dbim_ref, dcre_ref, dcim_ref, dmu_ref, gre, gim, car, acc, macc, pexp):
        z = pl.program_id(1)
        s = pl.program_id(2)
        tc = pl.program_id(3)

        @pl.when(tc == 0)
        def _():
            car[...] = jnp.zeros_like(car)

        @pl.when((tc == 0) & (s == 0))
        def _():
            acc[...] = jnp.zeros_like(acc)
            macc[...] = jnp.zeros_like(macc)
            _expand_powers(p_ref, pexp)

        dy = _to_lockstep(dy_ref)
        gre[...] = _mm_nt(dy, cre_ref[0, 0])
        gim[...] = -_mm_nt(dy, cim_ref[0, 0])

        def run(reverse):
            def pair(r0, gr_, gi_, pvr, pvi, m):
                if r0 is None:
                    return (macc[0], macc[1])
                hr = hre_ref[0, pl.ds(r0, 8), :]
                hi = him_ref[0, pl.ds(r0, 8), :]
                return (m[0] + pvr * hr + pvi * hi, m[1] + pvi * hr - pvr * hi)

            _, _, (dmr, dmi) = _lockstep_scan(gre, gim, a_ref, pexp, car, reverse, pair)
            macc[0] = dmr
            macc[1] = dmi

        @pl.when(z == 0)
        def _():
            run(True)

        @pl.when(z == 1)
        def _():
            run(False)

        gr = gre[...]
        gi = gim[...]
        u = _to_lockstep(u_ref)
        _from_lockstep(_mm_nt(gr, bre_ref[0, 0]) + _mm_nt(gi, bim_ref[0, 0]), du_ref, 0)
        acc[0] += _mm_tn(u, gr)
        acc[1] += _mm_tn(u, gi)
        acc[2] += _mm_tn(dy, hre_ref[0])
        acc[3] -= _mm_tn(dy, him_ref[0])

        @pl.when((tc == nt - 1) & (s == NSEQ - 1))
        def _():
            grp = lax.broadcasted_iota(jnp.int32, (S5_H, SW), 1) // S5_P
            for k, out in enumerate((dbre_ref, dbim_ref, dcre_ref, dcim_ref)):
                c = jnp.zeros((S5_H, SW), F32)
                for i in range(8):
                    c = c + jnp.where(grp == i, acc[k, i * S5_H:(i + 1) * S5_H, :], 0.0)
                out[0, 0] = c
            dmu_ref[0, 0] = jnp.concatenate([jnp.sum(macc[0], axis=0, keepdims=True),
                                             jnp.sum(macc[1], axis=0, keepdims=True)], axis=0)

    tb = lambda b, z, s, t: _s5_time_block(z, s, t, True)
    wspec = lambda r, c: pl.BlockSpec((1, 1, r, c), lambda b, z, s, t: (z, b, 0, 0))
    tok = lambda w_: pl.BlockSpec((TT, w_), lambda b, z, s, t: (tb(b, z, s, t), b))
    st = pl.BlockSpec((1, TT, SW), lambda b, z, s, t: (z, tb(b, z, s, t), b))
    return pl.pallas_call(
        body, grid=(2, 2, NSEQ, nt),
        in_specs=[tok(128), tok(128), st, st, wspec(128, SW), wspec(128, SW), wspec(SW, 128), wspec(SW, 128),
                  pl.BlockSpec((1, 1, 10, 8, SW), lambda b, z, s, t: (z, b, 0, 0, 0)),
                  pl.BlockSpec((1, 1, 2, NJ, SW), lambda b, z, s, t: (z, b, 0, 0, 0))],
        out_specs=[pl.BlockSpec((1, TT, 128), lambda b, z, s, t: (z, tb(b, z, s, t), b)),
                   wspec(S5_H, SW), wspec(S5_H, SW), wspec(S5_H, SW), wspec(S5_H, SW),
                   wspec(2, SW)],
        out_shape=[_sds((2, N, 256))] + [_sds((2, 2, S5_H, SW))] * 4 + [_sds((2, 2, 2, SW))],
        scratch_shapes=[pltpu.VMEM((TT, SW), F32), pltpu.VMEM((TT, SW), F32), pltpu.VMEM((2, 8, SW), F32),
                        pltpu.VMEM((4, 128, SW), F32), pltpu.VMEM((2, 8, SW), F32), pltpu.VMEM((2, NJ, 8, SW), F32)],
        name="s5_bwd", compiler_params=_cp(("arbitrary",) * 4))(h, dyp, hre, him, bre, bim, cre, cim, taba, tabp)


_GELU_C = math.sqrt(2.0 / math.pi)


def _gelu(y):
    return 0.5 * y * (1.0 + jnp.tanh(_GELU_C * (y + 0.044715 * y * y * y)))


def _gelu_grad(y):
    t = jnp.tanh(_GELU_C * (y + 0.044715 * y * y * y))
    return 0.5 * (1.0 + t) + 0.5 * y * (1.0 - t * t) * _GELU_C * (1.0 + 3 * 0.044715 * y * y)


def _glu_halves(w4_ref):
    return (jnp.concatenate([w4_ref[0], w4_ref[1]], axis=1), jnp.concatenate([w4_ref[2], w4_ref[3]], axis=1))


def _s5_glu_fwd(y2, h, dsk, w4, bv, bg):
    tm = 512

    def body(y2_ref, u_ref, d_ref, w4_ref, bv_ref, bg_ref, ya_ref):
        wv, wg = _glu_halves(w4_ref)
        z = _gelu(y2_ref[0] + y2_ref[1] + d_ref[...] * u_ref[...])
        val = _mm(z, wv) + bv_ref[...]
        gate = _mm(z, wg) + bg_ref[...]
        ya_ref[...] = (val * jax.nn.sigmoid(gate)).astype(MX)

    full = lambda r, c: pl.BlockSpec((r, c), lambda i: (0, 0))
    return pl.pallas_call(
        body, grid=(N // tm,),
        in_specs=[pl.BlockSpec((2, tm, 256), lambda i: (0, i, 0)), pl.BlockSpec((tm, 256), lambda i: (i, 0)),
                  full(1, 256), pl.BlockSpec((NSHARD, 256, 128), lambda i: (0, 0, 0)), full(1, 256), full(1, 256)],
        out_specs=pl.BlockSpec((tm, 256), lambda i: (i, 0)),
        out_shape=_sds((N, 256), MX), name="s5_glu_fwd", compiler_params=_cp(("parallel",)))(y2, h, dsk, w4, bv, bg)


def _s5_glu_bwd(y2, h, dsk, w4, bv, bg, dya):
    tm = 512
    nt = N // tm

    def body(y2_ref, u_ref, d_ref, w4_ref, bv_ref, bg_ref, dya_ref,
             dyp_ref, dud_ref, dd_ref, dw4_ref, dbv_ref, dbg_ref, accv, accg):
        i = pl.program_id(0)

        @pl.when(i == 0)
        def _():
            for r in (dd_ref, accv, accg, dbv_ref, dbg_ref):
                r[...] = jnp.zeros_like(r)

        wv, wg = _glu_halves(w4_ref)
        u = u_ref[...]
        y = y2_ref[0] + y2_ref[1] + d_ref[...] * u
        z = _gelu(y)
        val = _mm(z, wv) + bv_ref[...]
        sig = jax.nn.sigmoid(_mm(z, wg) + bg_ref[...])
        dya = dya_ref[...]
        dval = dya * sig
        dgate = dya * val * sig * (1.0 - sig)
        dz = _mm_nt(dval, wv) + _mm_nt(dgate, wg)
        dy = dz * _gelu_grad(y)
        dyp_ref[...] = dy
        dud_ref[...] = (dy * d_ref[...]).astype(MX)
        dd_ref[...] += jnp.sum(dy * u, axis=0, keepdims=True)
        accv[...] += _mm_tn(z, dval)
        accg[...] += _mm_tn(z, dgate)
        dbv_ref[...] += jnp.sum(dval, axis=0, keepdims=True)
        dbg_ref[...] += jnp.sum(dgate, axis=0, keepdims=True)

        @pl.when(i == nt - 1)
        def _():
            dw4_ref[0] = accv[:, 0:128].astype(MX)
            dw4_ref[1] = accv[:, 128:256].astype(MX)
            dw4_ref[2] = accg[:, 0:128].astype(MX)
            dw4_ref[3] = accg[:, 128:256].astype(MX)

    full = lambda r, c: pl.BlockSpec((r, c), lambda i: (0, 0))
    row = pl.BlockSpec((tm, 256), lambda i: (i, 0))
    wspec = pl.BlockSpec((NSHARD, 256, 128), lambda i: (0, 0, 0))
    return pl.pallas_call(
        body, grid=(nt,),
        in_specs=[pl.BlockSpec((2, tm, 256), lambda i: (0, i, 0)), row, full(1, 256), wspec, full(1, 256), full(1, 256),
                  row],
        out_specs=[row, row, full(1, 256), wspec, full(1, 256), full(1, 256)],
        out_shape=[_sds((N, 256)), _sds((N, 256), MX), _sds((1, 256)), _sds((NSHARD, 256, 128), MX), _sds((1, 256)),
                   _sds((1, 256))],
        scratch_shapes=[pltpu.VMEM((256, 256), F32), pltpu.VMEM((256, 256), F32)],
        name="s5_glu_bwd", compiler_params=_cp(("arbitrary",)))(y2, h, dsk, w4, bv, bg, dya)


def _logsig(x):
    return jnp.minimum(x, 0.0) - jnp.log(1.0 + jnp.exp(-jnp.abs(x)))


def _gla_gate_fwd(h, wa, ba):
    tm = 512

    def body(hl_ref, wa_ref, ba_ref, la_ref):
        la_ref[...] = _logsig(_mm(hl_ref[...], wa_ref[...]) + ba_ref[...]) * (1.0 / 16.0)

    return pl.pallas_call(
        body, grid=(N // tm,),
        in_specs=[pl.BlockSpec((tm, 128), lambda i: (i, 14)), pl.BlockSpec((128, 256), lambda i: (0, 0)),
                  pl.BlockSpec((1, 256), lambda i: (0, 0))],
        out_specs=pl.BlockSpec((tm, 256), lambda i: (i, 0)),
        out_shape=_sds((N, 256)), name="gla_gate_fwd", compiler_params=_cp(("parallel",)))(h, wa, ba)


def _gla_gate_bwd(h, wa, ba, dla_f, dla_b):
    tm = 512

    def body(hl_ref, wa_ref, ba_ref, df_ref, db_ref, dhl_ref, dwa_ref, dba_ref):
        i = pl.program_id(0)

        @pl.when(i == 0)
        def _():
            dwa_ref[...] = jnp.zeros_like(dwa_ref)
            dba_ref[...] = jnp.zeros_like(dba_ref)

        hl = hl_ref[...]
        pre = _mm(hl, wa_ref[...]) + ba_ref[...]
        dpre = jnp.concatenate([df_ref[...], db_ref[...]], axis=1) * (1.0 / 16.0) * jax.nn.sigmoid(-pre)
        dhl_ref[...] = _mm_nt(dpre, wa_ref[...]).astype(MX)
        dwa_ref[...] += _mm_tn(hl, dpre)[0:32]
        dba_ref[...] += jnp.sum(dpre, axis=0, keepdims=True)

    row = pl.BlockSpec((tm, 128), lambda i: (i, 0))
    return pl.pallas_call(
        body, grid=(N // tm,),
        in_specs=[pl.BlockSpec((tm, 128), lambda i: (i, 14)), pl.BlockSpec((128, 256), lambda i: (0, 0)),
                  pl.BlockSpec((1, 256), lambda i: (0, 0)), row, row],
        out_specs=[row, pl.BlockSpec((32, 256), lambda i: (0, 0)), pl.BlockSpec((1, 256), lambda i: (0, 0))],
        out_shape=[_sds((N, 128), MX), _sds((32, 256)), _sds((1, 256))],
        name="gla_gate_bwd", compiler_params=_cp(("arbitrary",)))(h, wa, ba, dla_f, dla_b)


def _gla_chunk(q, k, v, la, st, rev):
    c = GLA_CHUNK
    rows = q.shape[0]
    nch = rows // c
    b = _cums(la, rev)
    blc = [jnp.sum(la[i * c:(i + 1) * c], axis=0, keepdims=True) for i in range(nch)]
    bl = jnp.concatenate([jnp.broadcast_to(t, (c, 128)) for t in blc], axis=0)
    q_in = q * (32.0 ** -0.5) * jnp.exp(b)
    k_in = k * jnp.exp(-b)
    k_st = k * jnp.exp(bl - b)
    lane_k = lax.broadcasted_iota(jnp.int32, (1, 128), 1) // 32
    lane_v = lax.broadcasted_iota(jnp.int32, (1, 256), 1) // 64
    qs = jnp.concatenate([jnp.where(lane_k == hd, q_in, 0.0) for hd in range(4)], axis=0)
    a = _dmm_nt(qs, k_in)
    a = jnp.where(jnp.concatenate([_chunk_pairs(rows, rev, rev)] * 4, axis=0), a, 0.0)
    o4 = _dmm(a, v)
    o = jnp.zeros((rows, 256), F32)
    for hd in range(4):
        o = o + jnp.where(lane_v == hd, o4[hd * rows:(hd + 1) * rows], 0.0)
    bd = (lax.broadcasted_iota(jnp.int32, (256, 128), 0) // 64) == (lax.broadcasted_iota(jnp.int32, (256, 128), 1) // 32)
    inter = [None] * nch
    for i in (reversed(range(nch)) if rev else range(nch)):
        sl = slice(i * c, (i + 1) * c)
        inter[i] = _dmm_nt(q_in[sl], st)
        st = jnp.exp(blc[i]) * st + jnp.where(bd, _dmm_tn(v[sl], k_st[sl]), 0.0)
    return o + jnp.concatenate(inter, axis=0), st


def _gla_chunk_of(c, rev):
    return NGROUP - 1 - c if rev else c


def _gla_fwd(h, la2):
    c = GLA_GROUP * GLA_CHUNK

    def body(qf, kf, vf, laf, qb, kb, vb, lab, of_ref, ob_ref, sf_ref, sb_ref, stf, stb):
        @pl.when(pl.program_id(0) == 0)
        def _():
            stf[...] = jnp.zeros_like(stf)
            stb[...] = jnp.zeros_like(stb)

        ins = [(qf[s], kf[s], vf[s], laf[s], stf[s], qb[s], kb[s], vb[s], lab[s], stb[s]) for s in range(NSEQ)]
        outs = [(_gla_chunk(*t[:5], False), _gla_chunk(*t[5:], True)) for t in ins]
        for s in range(NSEQ):
            sf_ref[s, 0] = ins[s][4]
            sb_ref[s, 0] = ins[s][9]
            (of_ref[s], stf[s]), (ob_ref[s], stb[s]) = outs[s]

    def specs(rev):
        ch = lambda i: _gla_chunk_of(i, rev)
        return [pl.BlockSpec((NSEQ, c, 128), lambda i: (0, ch(i), 2)), pl.BlockSpec((NSEQ, c, 128), lambda i: (0, ch(i), 3)),
                pl.BlockSpec((NSEQ, c, 256), lambda i: (0, ch(i), 2)),
                pl.BlockSpec((NSEQ, c, 128), lambda i: (0, ch(i), 1 if rev else 0))]

    orow = lambda rev: pl.BlockSpec((NSEQ, c, 256), lambda i: (0, _gla_chunk_of(i, rev), 0))
    srow = lambda rev: pl.BlockSpec((NSEQ, 1, 256, 128), lambda i: (0, _gla_chunk_of(i, rev), 0, 0))
    h3, la3 = h.reshape(NSEQ, L, DINP), la2.reshape(NSEQ, L, 256)
    of, ob, sf, sb = pl.pallas_call(
        body, grid=(NGROUP,),
        in_specs=specs(False) + specs(True),
        out_specs=[orow(False), orow(True), srow(False), srow(True)],
        out_shape=[_sds((NSEQ, L, 256)), _sds((NSEQ, L, 256)), _sds((NSEQ, NGROUP, 256, 128)),
                   _sds((NSEQ, NGROUP, 256, 128))],
        scratch_shapes=[pltpu.VMEM((NSEQ, 256, 128), F32), pltpu.VMEM((NSEQ, 256, 128), F32)],
        name="gla_fwd", compiler_params=_cp(("arbitrary",)))(h3, h3, h3, la3, h3, h3, h3, la3)
    return of.reshape(N, 256), ob.reshape(N, 256), sf, sb


def _gla_bwd(h, la2, do, sf, sb):
    c = GLA_GROUP * GLA_CHUNK

    def body(qf, kf, vf, laf, dof, sfr, qb, kb, vb, lab, dob, sbr,
             dqf, dkf, dvf, dlf, dqb, dkb, dvb, dlb, dstf, dstb):
        @pl.when(pl.program_id(0) == 0)
        def _():
            dstf[...] = jnp.zeros_like(dstf)
            dstb[...] = jnp.zeros_like(dstb)

        def one(s, q, k, v, la, do_, st, dst, rev):
            _, vjp = jax.vjp(functools.partial(_gla_chunk, rev=rev), q[s], k[s], v[s], la[s], st[s, 0])
            return vjp((do_[s], dst[s]))

        res = [(one(s, qf, kf, vf, laf, dof, sfr, dstf, False), one(s, qb, kb, vb, lab, dob, sbr, dstb, True))
               for s in range(NSEQ)]
        for s in range(NSEQ):
            for (gq, gk, gv, gl, gs), (dq, dk, dv, dl, dst) in ((res[s][0], (dqf, dkf, dvf, dlf, dstf)),
                                                                  (res[s][1], (dqb, dkb, dvb, dlb, dstb))):
                dq[s], dk[s], dv[s] = gq.astype(MX), gk.astype(MX), gv.astype(MX)
                dl[s], dst[s] = gl, gs

    def specs(rev):
        ch = lambda i: _gla_chunk_of(i, not rev)
        return [pl.BlockSpec((NSEQ, c, 128), lambda i: (0, ch(i), 2)), pl.BlockSpec((NSEQ, c, 128), lambda i: (0, ch(i), 3)),
                pl.BlockSpec((NSEQ, c, 256), lambda i: (0, ch(i), 2)),
                pl.BlockSpec((NSEQ, c, 128), lambda i: (0, ch(i), 1 if rev else 0)),
                pl.BlockSpec((NSEQ, c, 256), lambda i: (0, ch(i), 0)),
                pl.BlockSpec((NSEQ, 1, 256, 128), lambda i: (0, ch(i), 0, 0))]

    def ospecs(rev):
        ch = lambda i: _gla_chunk_of(i, not rev)
        n = pl.BlockSpec((NSEQ, c, 128), lambda i: (0, ch(i), 0))
        return [n, n, pl.BlockSpec((NSEQ, c, 256), lambda i: (0, ch(i), 0)), n]

    oshape = [_sds((NSEQ, L, 128), MX), _sds((NSEQ, L, 128), MX), _sds((NSEQ, L, 256), MX), _sds((NSEQ, L, 128))]
    h3, la3, do3 = h.reshape(NSEQ, L, DINP), la2.reshape(NSEQ, L, 256), do.reshape(NSEQ, L, 256)
    res = pl.pallas_call(
        body, grid=(NGROUP,),
        in_specs=specs(False) + specs(True),
        out_specs=ospecs(False) + ospecs(True),
        out_shape=oshape + oshape,
        scratch_shapes=[pltpu.VMEM((NSEQ, 256, 128), F32), pltpu.VMEM((NSEQ, 256, 128), F32)],
        name="gla_bwd", compiler_params=_cp(("arbitrary",)))(h3, h3, h3, la3, do3, sf, h3, h3, h3, la3, do3, sb)
    return [r.reshape(N, r.shape[-1]) for r in res]


def _gla_post(of, ob, r, g):
    o = of + ob
    head = lax.broadcasted_iota(jnp.int32, (1, 256), 1) // 64
    mu = jnp.zeros_like(o)
    for hd in range(4):
        mu = mu + jnp.where(head == hd, jnp.sum(jnp.where(head == hd, o, 0.0), axis=-1, keepdims=True) * (1.0 / 64.0), 0.0)
    xc = o - mu
    var = jnp.zeros_like(o)
    for hd in range(4):
        var = var + jnp.where(head == hd, jnp.sum(jnp.where(head == hd, xc * xc, 0.0), axis=-1, keepdims=True) * (1.0 / 64.0), 0.0)
    return xc * lax.rsqrt(var + LN_EPS) * g * (r * jax.nn.sigmoid(r))


def _gla_post_fwd(of, ob, h, g):
    tm = 512

    def body(of_ref, ob_ref, r_ref, g_ref, y_ref):
        y_ref[...] = _gla_post(of_ref[...], ob_ref[...], r_ref[...], g_ref[...]).astype(MX)

    row = pl.BlockSpec((tm, 256), lambda i: (i, 0))
    return pl.pallas_call(
        body, grid=(N // tm,),
        in_specs=[row, row, pl.BlockSpec((tm, 256), lambda i: (i, 3)), pl.BlockSpec((1, 256), lambda i: (0, 0))],
        out_specs=row, out_shape=_sds((N, 256), MX), name="gla_post_fwd", compiler_params=_cp(("parallel",)))(of, ob, h, g)


def _gla_post_bwd(of, ob, h, g, dyb):
    tm = 512

    def body(of_ref, ob_ref, r_ref, g_ref, dy_ref, do_ref, dr_ref, dg_ref):
        @pl.when(pl.program_id(0) == 0)
        def _():
            dg_ref[...] = jnp.zeros_like(dg_ref)

        _, vjp = jax.vjp(_gla_post, of_ref[...], ob_ref[...], r_ref[...], g_ref[...])
        go, _, gr, gg = vjp(dy_ref[...])
        do_ref[...] = go
        dr_ref[...] = gr.astype(MX)
        dg_ref[...] += gg

    row = pl.BlockSpec((tm, 256), lambda i: (i, 0))
    one = pl.BlockSpec((1, 256), lambda i: (0, 0))
    return pl.pallas_call(
        body, grid=(N // tm,),
        in_specs=[row, row, pl.BlockSpec((tm, 256), lambda i: (i, 3)), one, row],
        out_specs=[row, row, one], out_shape=[_sds((N, 256)), _sds((N, 256), MX), _sds((1, 256))],
        name="gla_post_bwd", compiler_params=_cp(("arbitrary",)))(of, ob, h, g, dyb)


def _rope_tables(width):
    pos = jnp.arange(L, dtype=F32)
    inv_freq = ROPE_THETA ** (-jnp.arange(0, ROT, 2, dtype=F32) / ROT)
    ang = pos[:, None] * inv_freq[None, :]
    cos, sin = jnp.cos(ang), jnp.sin(ang)
    one = jnp.ones((L, 64 - ROT), F32)
    zero = jnp.zeros((L, 64 - ROT), F32)
    z8 = jnp.zeros((L, ROT // 2), F32)
    c = jnp.concatenate([cos, cos, one], axis=1)
    sa = jnp.concatenate([z8, sin, zero], axis=1)
    sb = jnp.concatenate([-sin, z8, zero], axis=1)
    rep = width // 64
    return jnp.stack([jnp.tile(c, (1, rep)), jnp.tile(sa, (1, rep)), jnp.tile(sb, (1, rep))])


def _pieces(t, f):
    out = [f(t[:, c * 128:(c + 1) * 128]) for c in range(t.shape[-1] // 128)]
    return out[0] if len(out) == 1 else jnp.concatenate(out, axis=1)


def _rope(t, tab):
    return _pieces(t, lambda x: x * tab[0] + pltpu.roll(x, ROT // 2, 1) * tab[1] + pltpu.roll(x, 128 - ROT // 2, 1) * tab[2])


def _rope_t(g, tab):
    return _pieces(g, lambda x: x * tab[0] + pltpu.roll(x * tab[1], 128 - ROT // 2, 1) + pltpu.roll(x * tab[2], ROT // 2, 1))


def _swa_pad_kv(kv_ref, tk_ref, kexp, vexp):
    z = jnp.zeros((SWA_BLK, 256), F32)
    kr = _rope(kv_ref[:, 0:128], tk_ref[...])
    for hk in range(2):
        for pad in (kexp, vexp):
            pad[hk, 0:SWA_BLK] = z
            pad[hk, SWA_BLK + L:] = z
        kexp[hk, SWA_BLK:SWA_BLK + L] = _swa_expand(kr, hk)
        vexp[hk, SWA_BLK:SWA_BLK + L] = _swa_expand(kv_ref[:, 128:256], hk)


def _swa_expand(x, hk):
    lane = lax.broadcasted_iota(jnp.int32, x.shape, 1)
    sw = pltpu.roll(x, 64, 1)
    pair = jnp.where(lane < 64, x, sw) if hk == 0 else jnp.where(lane < 64, sw, x)
    return jnp.concatenate([pair, pair], axis=1)


def _swa_fold(x, hk):
    a = x[:, 0:128] + x[:, 128:256]
    t = a + pltpu.roll(a, 64, 1)
    lane = lax.broadcasted_iota(jnp.int32, a.shape, 1)
    return jnp.where((lane < 64) if hk == 0 else (lane >= 64), t, 0.0)


def _swa_probs(q2, kexp, n, sink_ref, hk):
    slot = lax.broadcasted_iota(jnp.int32, (1, 256), 1) // 64
    qs = jnp.concatenate([jnp.where(slot == g, q2, 0.0) for g in range(4)], axis=0)
    s = _mm_nt(qs, kexp) * 0.125
    i = lax.broadcasted_iota(jnp.int32, (SWA_BLK, 3 * SWA_BLK), 0)
    j = lax.broadcasted_iota(jnp.int32, (SWA_BLK, 3 * SWA_BLK), 1)
    kpos = n * SWA_BLK - SWA_BLK + j
    ok = (j - i >= 0) & (j - i <= 2 * SWA_BLK) & (kpos >= 0) & (kpos < L)
    s = jnp.where(jnp.concatenate([ok] * 4, axis=0), s, NEG_BIG)
    rowg = lax.broadcasted_iota(jnp.int32, (4 * SWA_BLK, 1), 0) // SWA_BLK
    sink = jnp.zeros((4 * SWA_BLK, 1), F32)
    for g in range(4):
        sink = jnp.where(rowg == g, sink_ref[hk * 4 + g], sink)
    m = jnp.maximum(jnp.max(s, axis=-1, keepdims=True), sink)
    p = jnp.exp(s - m)
    ps = jnp.exp(sink - m)
    inv = 1.0 / (jnp.sum(p, axis=-1, keepdims=True) + ps)
    return qs, p * inv, ps * inv, slot, rowg


def _swa_qtab(tk_ref, r0):
    return [tk_ref[i, pl.ds(r0, SWA_BLK), :] for i in range(3)]


def _swa_fwd(h, tk, sink):
    def body(sink_ref, q_ref, kv_ref, tk_ref, y_ref, kexp, vexp):
        n = pl.program_id(1)

        @pl.when(n == 0)
        def _():
            _swa_pad_kv(kv_ref, tk_ref, kexp, vexp)

        r0 = pl.multiple_of(n * SWA_BLK, SWA_BLK)
        q = _rope(q_ref[...], _swa_qtab(tk_ref, r0))
        for hk in range(2):
            _, p, _, slot, _ = _swa_probs(q[:, hk * 256:(hk + 1) * 256], kexp[hk, pl.ds(r0, 3 * SWA_BLK), :], n,
                                          sink_ref, hk)
            o4 = _mm(p, vexp[hk, pl.ds(r0, 3 * SWA_BLK), :])
            o = jnp.zeros((SWA_BLK, 256), F32)
            for g in range(4):
                o = o + jnp.where(slot == g, o4[g * SWA_BLK:(g + 1) * SWA_BLK], 0.0)
            y_ref[:, hk * 256:(hk + 1) * 256] = o.astype(MX)

    return pl.pallas_call(
        body,
        grid_spec=pltpu.PrefetchScalarGridSpec(
            num_scalar_prefetch=1, grid=(NSEQ, NBLK),
            in_specs=[pl.BlockSpec((SWA_BLK, 512), lambda s, n, sk: (s * NBLK + n, 2)),
                      pl.BlockSpec((L, 256), lambda s, n, sk: (s, 6)),
                      pl.BlockSpec((3, L, 128), lambda s, n, sk: (0, 0, 0))],
            out_specs=pl.BlockSpec((SWA_BLK, 512), lambda s, n, sk: (s * NBLK + n, 0)),
            scratch_shapes=[pltpu.VMEM((2, L + 2 * SWA_BLK, 256), F32), pltpu.VMEM((2, L + 2 * SWA_BLK, 256), F32)]),
        out_shape=_sds((N, 512), MX), name="swa_fwd", compiler_params=_cp(("arbitrary", "arbitrary")))(sink, h, h, tk)


def _swa_bwd(h, tk, sink, dyc):
    def body(sink_ref, q_ref, kv_ref, tk_ref, dy_ref, dq_ref, dkv_ref, dsink_ref, kexp_all, vexp_all, dkacc, dvacc):
        sq = pl.program_id(0)
        n = pl.program_id(1)

        @pl.when(n == 0)
        def _():
            _swa_pad_kv(kv_ref, tk_ref, kexp_all, vexp_all)
            dkacc[...] = jnp.zeros_like(dkacc)
            dvacc[...] = jnp.zeros_like(dvacc)

        @pl.when((n == 0) & (sq == 0))
        def _():
            dsink_ref[...] = jnp.zeros_like(dsink_ref)

        r0 = pl.multiple_of(n * SWA_BLK, SWA_BLK)
        tq = _swa_qtab(tk_ref, r0)
        q = _rope(q_ref[...], tq)
        hrow = lax.broadcasted_iota(jnp.int32, (8, 128), 0)
        dsk = jnp.zeros((8, 128), F32)
        for hk in range(2):
            kexp = kexp_all[hk, pl.ds(r0, 3 * SWA_BLK), :]
            vexp = vexp_all[hk, pl.ds(r0, 3 * SWA_BLK), :]
            qs, p, ps, slot, rowg = _swa_probs(q[:, hk * 256:(hk + 1) * 256], kexp, n, sink_ref, hk)
            dy2 = dy_ref[:, hk * 256:(hk + 1) * 256]
            dos = jnp.concatenate([jnp.where(slot == g, dy2, 0.0) for g in range(4)], axis=0)
            dp = _mm_nt(dos, vexp)
            delta = jnp.sum(p * dp, axis=-1, keepdims=True)
            ds = p * (dp - delta) * 0.125
            dsr = -ps * delta
            for g in range(4):
                dsk = dsk + jnp.where(hrow == hk * 4 + g, jnp.sum(jnp.where(rowg == g, dsr, 0.0), axis=0, keepdims=True), 0.0)
            dq4 = _mm(ds, kexp)
            dq2 = jnp.zeros((SWA_BLK, 256), F32)
            for g in range(4):
                dq2 = dq2 + jnp.where(slot == g, dq4[g * SWA_BLK:(g + 1) * SWA_BLK], 0.0)
            dq_ref[:, hk * 256:(hk + 1) * 256] = _rope_t(dq2, tq).astype(MX)
            dkacc[hk, pl.ds(r0, 3 * SWA_BLK), :] += _mm_tn(ds, qs)
            dvacc[hk, pl.ds(r0, 3 * SWA_BLK), :] += _mm_tn(p, dos)
        dsink_ref[...] += dsk

        @pl.when(n == NBLK - 1)
        def _():
            seq = slice(SWA_BLK, SWA_BLK + L)
            dk = _rope_t(_swa_fold(dkacc[0, seq], 0) + _swa_fold(dkacc[1, seq], 1), tk_ref[...])
            dkv_ref[:, 0:128] = dk.astype(MX)
            dkv_ref[:, 128:256] = (_swa_fold(dvacc[0, seq], 0) + _swa_fold(dvacc[1, seq], 1)).astype(MX)

    blk = lambda col: pl.BlockSpec((SWA_BLK, 512), lambda s, n, sk: (s * NBLK + n, col))
    pad = pltpu.VMEM((2, L + 2 * SWA_BLK, 256), F32)
    return pl.pallas_call(
        body,
        grid_spec=pltpu.PrefetchScalarGridSpec(
            num_scalar_prefetch=1, grid=(NSEQ, NBLK),
            in_specs=[blk(2), pl.BlockSpec((L, 256), lambda s, n, sk: (s, 6)),
                      pl.BlockSpec((3, L, 128), lambda s, n, sk: (0, 0, 0)), blk(0)],
            out_specs=[blk(0), pl.BlockSpec((L, 256), lambda s, n, sk: (s, 0)),
                       pl.BlockSpec((8, 128), lambda s, n, sk: (0, 0))],
            scratch_shapes=[pad, pad, pad, pad]),
        out_shape=[_sds((N, 512), MX), _sds((N, 256), MX), _sds((8, 128))],
        name="swa_bwd", compiler_params=_cp(("arbitrary", "arbitrary")))(sink, h, h, tk, dyc)


def _outproj_fwd(ya, yb, yc, x, wo, g, b):
    tm = 512

    def body(ya_ref, yb_ref, yc_ref, x_ref, wo_ref, g_ref, b_ref, s_ref, x1_ref):
        mix = _mm(ya_ref[...], wo_ref[0:256]) + _mm(yb_ref[...], wo_ref[256:512]) + _mm(yc_ref[...], wo_ref[512:1024])
        s = ALPHA * x_ref[...] + mix
        s_ref[...] = s
        x1_ref[...] = _ln_fwd(s, g_ref[...], b_ref[...])

    row = lambda w_: pl.BlockSpec((tm, w_), lambda i: (i, 0))
    one = pl.BlockSpec((1, D), lambda i: (0, 0))
    return pl.pallas_call(
        body, grid=(N // tm,),
        in_specs=[row(256), row(256), row(512), row(D), pl.BlockSpec((D, D), lambda i: (0, 0)), one, one],
        out_specs=[row(D), row(D)], out_shape=[_sds((N, D)), _sds((N, D))],
        name="outproj_fwd", compiler_params=_cp(("parallel",)))(ya, yb, yc, x, wo, g, b)


def _outproj_bwd(dx1, s1, ya, yb, yc, wo, g):
    tm = 512
    nt = N // tm

    def body(dx1_ref, s_ref, ya_ref, yb_ref, yc_ref, wo_ref, g_ref,
             dya_ref, dyb_ref, dyc_ref, dxp_ref, dwo_ref, dg_ref, db_ref, acc):
        i = pl.program_id(0)

        @pl.when(i == 0)
        def _():
            acc[...] = jnp.zeros_like(acc)
            dg_ref[...] = jnp.zeros_like(dg_ref)
            db_ref[...] = jnp.zeros_like(db_ref)

        ds, dg, db = _ln_bwd(dx1_ref[...], s_ref[...], g_ref[...])
        dg_ref[...] += dg
        db_ref[...] += db
        dxp_ref[...] = ALPHA * ds
        dy = _mm_nt(ds, wo_ref[...])
        dya_ref[...] = dy[:, 0:256]
        dyb_ref[...] = dy[:, 256:512]
        dyc_ref[...] = dy[:, 512:1024]
        acc[0:256] += _mm_tn(ya_ref[...], ds)
        acc[256:512] += _mm_tn(yb_ref[...], ds)
        acc[512:1024] += _mm_tn(yc_ref[...], ds)

        @pl.when(i == nt - 1)
        def _():
            dwo_ref[...] = acc[...].astype(MX)

    row = lambda w_: pl.BlockSpec((tm, w_), lambda i: (i, 0))
    one = pl.BlockSpec((1, D), lambda i: (0, 0))
    full = pl.BlockSpec((D, D), lambda i: (0, 0))
    return pl.pallas_call(
        body, grid=(nt,),
        in_specs=[row(D), row(D), row(256), row(256), row(512), full, one],
        out_specs=[row(256), row(256), row(512), row(D), full, one, one],
        out_shape=[_sds((N, 256)), _sds((N, 256)), _sds((N, 512)), _sds((N, D)), _sds((D, D), MX), _sds((1, D)), _sds((1, D))],
        scratch_shapes=[pltpu.VMEM((D, D), F32)],
        name="outproj_bwd", compiler_params=_cp(("arbitrary",)))(dx1, s1, ya, yb, yc, wo, g)


def _ffn_fwd(x1, w1, w2, g, b, target=None):
    tm = FFN_TM
    head = target is not None

    def body(*refs):
        x_ref, w1_ref, w2_ref, g_ref, b_ref = refs[:5]
        a_ref, s_ref, y_ref = refs[5 + head:8 + head]
        x = x_ref[...]
        xb = x.astype(MX)
        s = ALPHA * x
        for j in range(NSHARD):
            a = _mm(xb, w1_ref[j])
            a_ref[:, j * D:(j + 1) * D] = a.astype(MX)
            s = s + _mm(jnp.square(jnp.maximum(a, 0.0)), w2_ref[j])
        s_ref[...] = s
        x2 = _ln_fwd(s, g_ref[...], b_ref[...])
        if not head:
            y_ref[...] = x2
            return
        l_ref = refs[-1]

        @pl.when(pl.program_id(0) == 0)
        def _():
            l_ref[...] = jnp.zeros_like(l_ref)

        e = x2 - refs[5][...]
        y_ref[...] = e * (1.0 / D)
        l_ref[...] += jnp.sum(jnp.sum(e * e, axis=1, keepdims=True), axis=0, keepdims=True) * (0.5 / D)

    row = pl.BlockSpec((tm, D), lambda i: (i, 0))
    wall = pl.BlockSpec((NSHARD, D, D), lambda i: (0, 0, 0))
    one = pl.BlockSpec((1, D), lambda i: (0, 0))
    acc = pl.BlockSpec((8, 128), lambda i: (0, 0))
    return pl.pallas_call(
        body, grid=(N // tm,),
        in_specs=[row, wall, wall, one, one] + [row] * head,
        out_specs=[pl.BlockSpec((tm, DFF), lambda i: (i, 0)), row, row] + [acc] * head,
        out_shape=[_sds((N, DFF), MX), _sds((N, D)), _sds((N, D))] + [_sds((8, 128))] * head,
        name="ffn_fwd", compiler_params=_cp(("arbitrary",), FFN_VMEM))(x1, w1, w2, g, b, *([target] * head))


def _ffn_bwd_act(dy, s2, a, w1, w2, g):
    tm = FFN_TM

    def body(dy_ref, s_ref, a_ref, w1_ref, w2_ref, g_ref, da_ref, ds_ref, dx1_ref, dg_ref, db_ref):
        @pl.when(pl.program_id(0) == 0)
        def _():
            dg_ref[...] = jnp.zeros_like(dg_ref)
            db_ref[...] = jnp.zeros_like(db_ref)

        ds, dg, db = _ln_bwd(dy_ref[...], s_ref[...], g_ref[...])
        dsb = ds.astype(MX)
        ds_ref[...] = dsb
        dg_ref[...] += dg
        db_ref[...] += db
        dx1 = ALPHA * ds
        for j in range(NSHARD):
            da = (_mm_nt(dsb, w2_ref[j]) * 2.0 * jnp.maximum(a_ref[:, j * D:(j + 1) * D].astype(F32), 0.0)).astype(MX)
            da_ref[:, j * D:(j + 1) * D] = da
            dx1 = dx1 + _mm_nt(da, w1_ref[j])
        dx1_ref[...] = dx1

    row = pl.BlockSpec((tm, D), lambda i: (i, 0))
    wide = pl.BlockSpec((tm, DFF), lambda i: (i, 0))
    wall = pl.BlockSpec((NSHARD, D, D), lambda i: (0, 0, 0))
    one = pl.BlockSpec((1, D), lambda i: (0, 0))
    return pl.pallas_call(
        body, grid=(N // tm,),
        in_specs=[row, row, wide, wall, wall, one],
        out_specs=[wide, row, row, one, one],
        out_shape=[_sds((N, DFF), MX), _sds((N, D), MX), _sds((N, D)), _sds((1, D)), _sds((1, D))],
        name="ffn_bwd_act", compiler_params=_cp(("arbitrary",), FFN_VMEM))(dy, s2, a, w1, w2, g)


def _ffn_bwd_w(x1, da, a, ds):
    tm, nb = FFN_TM_W, FFN_WB
    nt = N // tm

    def body(x_ref, da_ref, a_ref, ds_ref, dw1_ref, dw2_ref, acc1, acc2):
        i = pl.program_id(1)

        @pl.when(i == 0)
        def _():
            acc1[...] = jnp.zeros_like(acc1)
            acc2[...] = jnp.zeros_like(acc2)

        x, ds_ = x_ref[...], ds_ref[...]
        for k in range(nb):
            cols = slice(k * D, (k + 1) * D)
            acc1[k] += _mm_tn(x, da_ref[:, cols])
            acc2[k] += _mm_tn(jnp.square(jnp.maximum(a_ref[:, cols].astype(F32), 0.0)), ds_)

        @pl.when(i == nt - 1)
        def _():
            dw1_ref[...] = acc1[...].astype(MX)
            dw2_ref[...] = acc2[...].astype(MX)

    row = pl.BlockSpec((tm, D), lambda j, i: (i, 0))
    col = pl.BlockSpec((tm, nb * D), lambda j, i: (i, j))
    wj = pl.BlockSpec((nb, D, D), lambda j, i: (j, 0, 0))
    return pl.pallas_call(
        body, grid=(NSHARD // nb, nt),
        in_specs=[row, col, col, row], out_specs=[wj, wj],
        out_shape=[_sds((NSHARD, D, D), MX), _sds((NSHARD, D, D), MX)],
        scratch_shapes=[pltpu.VMEM((nb, D, D), F32), pltpu.VMEM((nb, D, D), F32)],
        name="ffn_bwd_w", compiler_params=_cp(("parallel", "arbitrary"), FFN_VMEM))(x1, da, a, ds)


def _loss_head(y, target):
    tm = 512

    def body(y_ref, t_ref, dy_ref, l_ref):
        @pl.when(pl.program_id(0) == 0)
        def _():
            l_ref[...] = jnp.zeros_like(l_ref)

        e = y_ref[...] - t_ref[...]
        dy_ref[...] = e * (1.0 / D)
        l_ref[...] += jnp.sum(jnp.sum(e * e, axis=1, keepdims=True), axis=0, keepdims=True) * (0.5 / D)

    row = pl.BlockSpec((tm, D), lambda i: (i, 0))
    return pl.pallas_call(
        body, grid=(N // tm,), in_specs=[row, row],
        out_specs=[row, pl.BlockSpec((8, 128), lambda i: (0, 0))],
        out_shape=[_sds((N, D)), _sds((8, 128))], name="loss_head", compiler_params=_cp(("arbitrary",)))(y, target)


def _s5_discretize(a_re, a_im, log_step, b_re, b_im):
    lam = lax.complex(a_re, a_im)
    lam_bar = jnp.exp(lam * jnp.exp(log_step))
    b_bar = ((lam_bar - 1.0) / lam)[..., None] * lax.complex(b_re, b_im)
    return jnp.real(lam_bar), jnp.imag(lam_bar), jnp.real(b_bar), jnp.imag(b_bar)


def _s5_in_blocks(b):
    e = jnp.eye(8, dtype=F32)
    return jnp.einsum('ij,zbjph->zbihjp', e, b.reshape(2, 2, 8, S5_P, S5_H)).reshape(2, 2, 128, SW)


def _s5_in_unblocks(d):
    return jnp.einsum('zbihip->zbiph', d.reshape(2, 2, 8, S5_H, 8, S5_P)).reshape(2, S5_G, S5_P, S5_H)


def _s5_out_blocks(c):
    e = jnp.eye(8, dtype=F32)
    return jnp.einsum('ij,zbjhp->zbjpih', e, c.reshape(2, 2, 8, S5_H, S5_P)).reshape(2, 2, SW, 128)


def _s5_out_unblocks(d):
    return jnp.einsum('zbipih->zbihp', d.reshape(2, 2, 8, S5_P, 8, S5_H)).reshape(2, S5_G, S5_H, S5_P)


def _gate_weight(w_a):
    z = jnp.zeros((16, 128), F32)
    top = jnp.concatenate([w_a[0], z], axis=1)
    bot = jnp.concatenate([z, w_a[1]], axis=1)
    return jnp.concatenate([top, bot, jnp.zeros((96, 256), F32)], axis=0)


def _layer_prep(p):
    lr, li, br, bi = _s5_discretize(p["s5_a_re"], p["s5_a_im"], p["s5_log_step"], p["s5_b_re"], p["s5_b_im"])
    q = dict(p)
    q["bre"] = _s5_in_blocks(br).astype(MX)
    q["bim"] = _s5_in_blocks(bi).astype(MX)
    q["cre"] = _s5_out_blocks(p["s5_c_re"]).astype(MX)
    q["cim"] = _s5_out_blocks(p["s5_c_im"]).astype(MX)
    mr, mi = lr.reshape(2, 1024), li.reshape(2, 1024)
    both = lambda t0, t1: tuple(jnp.stack(p) for p in zip(t0, t1))
    q["tab"] = both(_lockstep_tables(mr[0], mi[0], False), _lockstep_tables(mr[1], mi[1], True))
    q["tabc"] = both(_lockstep_tables(mr[0], -mi[0], True), _lockstep_tables(mr[1], -mi[1], False))
    q["dsk"] = p["s5_d"].reshape(1, 256)
    q["wa"] = _gate_weight(p["gla_w_a"]).astype(MX)
    q["ba"] = p["gla_b_a"].reshape(1, 256)
    q["lng"] = p["gla_ln_g"].reshape(1, 256)
    q["bv"] = p["s5_b_glu"][:256].reshape(1, 256)
    q["bg"] = p["s5_b_glu"][256:].reshape(1, 256)
    for k in ("ln1_g", "ln1_b", "ln2_g", "ln2_b"):
        q[k] = p[k].reshape(1, D)
    return q


def _layer_fwd(x, q, tk, fetch, target=None):
    q["w_in"] = fetch("w_in", x)
    h = _inproj_fwd(x, q["w_in"])
    hre, him, y2 = _s5_fwd(h, q["bre"], q["bim"], q["cre"], q["cim"], q["tab"])
    q["w4"] = fetch("s5_w_glu", y2)
    ya = _s5_glu_fwd(y2, h, q["dsk"], q["w4"], q["bv"], q["bg"])
    la2 = _gla_gate_fwd(h, q["wa"], q["ba"])
    of, ob, sf, sb = _gla_fwd(h, la2)
    yb = _gla_post_fwd(of, ob, h, q["lng"])
    yc = _swa_fwd(h, tk, q["swa_sink"])
    q["w_out"] = fetch("w_out", yc)
    s1, x1 = _outproj_fwd(ya, yb, yc, x, q["w_out"], q["ln1_g"], q["ln1_b"])
    q["w_ff1"] = fetch("w_ff1", x1)
    q["w_ff2"] = fetch("w_ff2", x1)
    a, s2, *out = _ffn_fwd(x1, q["w_ff1"], q["w_ff2"], q["ln2_g"], q["ln2_b"], target)
    saved = dict(x=x, h=h, hre=hre, him=him, y2=y2, ya=ya, la2=la2, of=of, ob=ob, sf=sf, sb=sb, yb=yb, yc=yc,
                 s1=s1, x1=x1, a=a, s2=s2)
    return (out[0] if target is None else tuple(out)), saved


def _layer_bwd(dy, q, sv, tk, emit):
    g = {}
    da, ds2, dx1, g["dg2"], g["db2"] = _ffn_bwd_act(dy, sv["s2"], sv["a"], q["w_ff1"], q["w_ff2"], q["ln2_g"])
    dw1, dw2 = _ffn_bwd_w(sv["x1"], da, sv["a"], ds2)
    tie = emit(dict(w_ff1=dw1, w_ff2=dw2))
    dya, dyb, dyc, dxp, dwo, g["dg1"], g["db1"] = _outproj_bwd(dx1, sv["s1"], sv["ya"], sv["yb"], sv["yc"],
                                                               q["w_out"], q["ln1_g"] + tie)
    h = sv["h"]
    daq, dakv, g["dsink"] = _swa_bwd(h, tk, q["swa_sink"], dyc)
    do, gr, g["dlng"] = _gla_post_bwd(sv["of"], sv["ob"], h, q["lng"], dyb)
    gq_f, gk_f, gv_f, gl_f, gq_b, gk_b, gv_b, gl_b = _gla_bwd(h, sv["la2"], do, sv["sf"], sv["sb"])
    dhl, g["dwa"], g["dba"] = _gla_gate_bwd(h, q["wa"], q["ba"], gl_f, gl_b)
    dyp, dud, g["dd"], dw4, g["dbv"], g["dbg"] = _s5_glu_bwd(sv["y2"], h, q["dsk"], q["w4"], q["bv"], q["bg"], dya)
    tie = emit(dict(w_out=dwo.reshape(NSHARD, D // NSHARD, D), s5_w_glu=dw4))
    du2, g["dbre"], g["dbim"], g["dcre"], g["dcim"], g["dmu"] = _s5_bwd(
        h, dyp, sv["hre"], sv["him"], q["bre"], q["bim"], q["cre"], q["cim"], (q["tabc"][0], q["tabc"][1] + tie))
    dx, dwt = _inproj_bwd(sv["x"], q["w_in"], dxp, du2, dud, gq_f, gq_b, gk_f, gk_b, gv_f, gv_b, gr, daq, dakv, dhl)
    tie = emit(dict(w_in=dwt))
    return dx, g, tie


NATIVE = ("dmu", "dbre", "dbim", "dcre", "dcim", "dd", "dbv", "dbg", "dwa", "dba", "dlng", "dsink",
          "dg1", "db1", "dg2", "db2", "loss")
ICI_CORE = (0, 0, 0, 1, 1, 0, 0, 0, 1, 1, 1, 1, 0, 0, 1, 1, 0)


def _finish_small(n, w):
    g = {}
    dmu = n["dmu"]
    dlr = dmu[:, :, :, 0].reshape(DEPTH, 2, S5_G, S5_P)
    dli = dmu[:, :, :, 1].reshape(DEPTH, 2, S5_G, S5_P)

    def unblock(c, perm, shape):
        return c.reshape(DEPTH, 2, 2, S5_H, 8, S5_P).transpose(perm).reshape(shape)

    b_shape, c_shape = (DEPTH, 2, S5_G, S5_P, S5_H), (DEPTH, 2, S5_G, S5_H, S5_P)
    _, vjp = jax.vjp(_s5_discretize, w["s5_a_re"], w["s5_a_im"], w["s5_log_step"], w["s5_b_re"], w["s5_b_im"])
    (g["s5_a_re"], g["s5_a_im"], g["s5_log_step"], g["s5_b_re"], g["s5_b_im"]) = vjp(
        (dlr, dli, unblock(n["dbre"], (0, 1, 2, 4, 5, 3), b_shape), unblock(n["dbim"], (0, 1, 2, 4, 5, 3), b_shape)))
    g["s5_c_re"] = unblock(n["dcre"], (0, 1, 2, 4, 3, 5), c_shape)
    g["s5_c_im"] = unblock(n["dcim"], (0, 1, 2, 4, 3, 5), c_shape)
    g["s5_d"] = n["dd"].reshape(DEPTH, S5_G, S5_H)
    g["s5_b_glu"] = jnp.concatenate([n["dbv"], n["dbg"]], axis=2).reshape(DEPTH, 512)
    g["gla_w_a"] = jnp.stack([n["dwa"][:, 0:16, 0:128], n["dwa"][:, 16:32, 128:256]], axis=1)
    g["gla_b_a"] = n["dba"].reshape(DEPTH, 2, 128)
    g["gla_ln_g"] = n["dlng"].reshape(DEPTH, 256)
    g["swa_sink"] = n["dsink"][:, :, 0]
    for k, s in (("ln1_g", "dg1"), ("ln1_b", "db1"), ("ln2_g", "dg2"), ("ln2_b", "db2")):
        g[k] = n[s].reshape(DEPTH, D)
    return g


def _local_step(x, target, qs, tk, fetch, emit):
    saved = []
    for l, q in enumerate(qs):
        x, sv = _layer_fwd(x, q, tk, functools.partial(fetch, l), target if l == DEPTH - 1 else None)
        saved.append(sv)
    dy, lacc = x
    smalls = [None] * DEPTH
    tie = 0.0
    for l in reversed(range(DEPTH)):
        qs[l]["ln2_g"] = qs[l]["ln2_g"] + tie
        dy, smalls[l], tie = _layer_bwd(dy, qs[l], saved[l], tk, functools.partial(emit, l))
    smalls[0]["db2"] = smalls[0]["db2"] + tie
    for l in range(DEPTH):
        smalls[l]["loss"] = lacc if l == 0 else jnp.zeros_like(lacc)
    return lacc[0, 0], dy, smalls


BIG = ("w_in", "s5_w_glu", "w_out", "w_ff1", "w_ff2")
SMALL = ("s5_a_re", "s5_a_im", "s5_log_step", "s5_b_re", "s5_b_im", "s5_c_re", "s5_c_im", "s5_d", "s5_b_glu",
         "gla_w_a", "gla_b_a", "gla_ln_g", "swa_sink", "ln1_g", "ln1_b", "ln2_g", "ln2_b")
ANY = pl.BlockSpec(memory_space=pl.ANY)


def _place():
    x, y, c = lax.axis_index("x"), lax.axis_index("y"), lax.axis_index("c")
    return x, y, c, [(1 - x, y), (x, 1 - y), (1 - x, 1 - y)]


HBM = pl.BlockSpec(memory_space=pltpu.HBM)
SEMS = pl.BlockSpec(memory_space=pltpu.SEMAPHORE)
EFFECT = pltpu.SideEffectType.DATAFLOW_SIDE_EFFECTING


def _push_copies(ins, lands, send, recv, gather, sending):
    x, y, c, chips = _place()
    me = 2 * x + y
    if gather == "sibling":
        return [pltpu.make_async_remote_copy(src_ref=ins[a], dst_ref=lands[a], send_sem=send.at[a], recv_sem=recv.at[a],
                                             device_id=(x, y, 1 - c), device_id_type=MESH) for a in range(len(lands))]
    out = []
    for a in range(len(lands)):
        for j, (px, py) in enumerate(chips):
            peer = 2 * px + py
            src = lands[a].at[me] if gather else ins[a].at[peer if sending else me]
            dst = lands[a].at[me if sending else peer]
            out.append(pltpu.make_async_remote_copy(src_ref=src, dst_ref=dst, send_sem=send.at[3 * a + j],
                                                    recv_sem=recv.at[3 * a + j], device_id=(px, py, c),
                                                    device_id_type=MESH))
    return out


def _push_start(name, arrs, gather):
    n = len(arrs)
    ops = list(arrs) if gather is True else list(arrs) + [lax.empty(s.shape, s.dtype) for s in arrs]
    m = len(ops)

    def body(*refs):
        ins, lnd = (refs[:n], refs[:n]) if gather is True else (refs[:n], refs[n:m])
        for cp in _push_copies(ins, lnd, refs[m], refs[m + 1], gather, True):
            cp.start()
        refs[-1][...] = jnp.zeros((8, 128), F32)

    ops = [pltpu.with_memory_space_constraint(t, pltpu.HBM) for t in ops]
    res = pl.pallas_call(
        body, name=name,
        out_shape=(pltpu.SemaphoreType.DMA((3 * n,)), pltpu.SemaphoreType.DMA((3 * n,)),
                   *[pltpu.HBM(t.shape, t.dtype) for t in ops], _sds((8, 128))),
        in_specs=[HBM] * m,
        out_specs=(SEMS, SEMS, *[HBM] * m, pl.BlockSpec(memory_space=pltpu.VMEM)),
        input_output_aliases={i: 2 + i for i in range(m)},
        compiler_params=pltpu.CompilerParams(has_side_effects=EFFECT))(*ops)
    return res[0], res[1], list(res[2:2 + m]), res[-1]


def _push_wait(name, started, after, gather):
    send, recv, ops, _ = started
    m = len(ops)
    n = m if gather is True else m // 2

    def body(*refs):
        ins, lnd = (refs[:n], refs[:n]) if gather is True else (refs[:n], refs[n:m])
        for cp in _push_copies(ins, lnd, refs[m], refs[m + 1], gather, False):
            cp.wait_send()
            cp.wait_recv()

    res = pl.pallas_call(
        body, name=name,
        out_shape=[pltpu.HBM(t.shape, t.dtype) for t in ops],
        in_specs=[HBM] * m + [SEMS, SEMS, ANY], out_specs=[HBM] * m,
        input_output_aliases={i: i for i in range(m)},
        compiler_params=pltpu.CompilerParams(has_side_effects=EFFECT))(*ops, send, recv, after)
    return list(res)


def _row_tile(rows):
    return max(t for t in range(8, min(rows, 512) + 1, 8) if rows % t == 0)


def _cast_to_slot(me, w, l):
    _, rows, cols = w.shape
    tr = _row_tile(rows)

    def body(me_ref, w_ref, o_ref):
        o_ref[0] = w_ref[0].astype(MX)

    return pl.pallas_call(
        body,
        grid_spec=pltpu.PrefetchScalarGridSpec(
            num_scalar_prefetch=1, grid=(rows // tr,),
            in_specs=[pl.BlockSpec((1, tr, cols), lambda i, me_: (l, i, 0))],
            out_specs=pl.BlockSpec((1, tr, cols), lambda i, me_: (me_[0], i, 0))),
        out_shape=_sds((NSHARD, rows, cols), MX), name="cast_to_slot", compiler_params=_cp(("arbitrary",)))(me, w)


def _sum_sources(me, recv, own):
    _, rows, cols = recv[0].shape
    tr = min(_row_tile(rows), 256) if rows % 256 == 0 else _row_tile(rows)
    nt = rows // tr

    def body(me_ref, *refs):
        o_ref = refs[-1]
        for l in range(DEPTH):
            @pl.when(pl.program_id(0) == l)
            def _():
                r_ref, own_ref = refs[2 * l], refs[2 * l + 1]
                part = [jnp.where(me_ref[0] == s, own_ref[0], r_ref[s]).astype(F32) for s in range(NSHARD)]
                o_ref[...] = ((part[0] + part[1]) + part[2]) + part[3]

    in_specs = []
    for l in range(DEPTH):
        pick = lambda g, i, me_, l=l: jnp.where(g == l, i, jnp.where(g < l, 0, nt - 1))
        in_specs += [pl.BlockSpec((NSHARD, tr, cols), lambda g, i, me_, pick=pick: (0, pick(g, i, me_), 0)),
                     pl.BlockSpec((1, tr, cols), lambda g, i, me_, pick=pick: (me_[0], pick(g, i, me_), 0))]
    return pl.pallas_call(
        body,
        grid_spec=pltpu.PrefetchScalarGridSpec(
            num_scalar_prefetch=1, grid=(DEPTH, nt), in_specs=in_specs,
            out_specs=pl.BlockSpec((tr, cols), lambda g, i, me_: (g * nt + i, 0))),
        out_shape=_sds((DEPTH * rows, cols)), name="sum_sources",
        compiler_params=_cp(("arbitrary", "arbitrary")))(me, *[t for l in range(DEPTH) for t in (recv[l], own[l])])


def _swap_sibling(arrs):
    n = len(arrs)

    def body(*refs):
        ins, outs = refs[:n], refs[n:2 * n]
        send, recv = refs[2 * n:]
        x, y, c, _ = _place()
        cps = [pltpu.make_async_remote_copy(src_ref=ins[a], dst_ref=outs[a], send_sem=send.at[a], recv_sem=recv.at[a],
                                            device_id=(x, y, 1 - c), device_id_type=MESH) for a in range(n)]
        for cp in cps:
            cp.start()
        for cp in cps:
            cp.wait()

    return pl.pallas_call(
        body, in_specs=[ANY] * n, out_specs=[ANY] * n, out_shape=[_sds(a.shape, a.dtype) for a in arrs],
        scratch_shapes=[pltpu.SemaphoreType.DMA((n,)), pltpu.SemaphoreType.DMA((n,))],
        name="swap_sibling")(*arrs)


def _allreduce_small(per_layer):
    nk = len(per_layer[0])
    n = DEPTH * nk
    shapes = [a.shape for a in per_layer[0]]

    def body(*refs):
        ins, outs = refs[:n], refs[n:n + nk]
        sibs, slots = refs[n + nk:n + 2 * nk], refs[n + 2 * nk:n + 3 * nk]
        send, recv = refs[n + 3 * nk:]
        x, y, c, chips = _place()
        me = 2 * x + y
        d2d = [pltpu.make_async_remote_copy(src_ref=ins[l * nk + k], dst_ref=sibs[k].at[l], send_sem=send.at[l * nk + k],
                                            recv_sem=recv.at[l * nk + k], device_id=(x, y, 1 - c), device_id_type=MESH)
               for l in range(DEPTH) for k in range(nk)]
        for cp in d2d:
            cp.start()
        for cp in d2d:
            cp.wait()
        for l in range(DEPTH):
            for k in range(nk):
                slots[k][0, l] = ins[l * nk + k][...] + sibs[k][l]

        def swap(k, stage):
            peer = (1 - x, y, c) if stage == 0 else (x, 1 - y, c)
            return pltpu.make_async_remote_copy(src_ref=slots[k].at[2 * stage], dst_ref=slots[k].at[2 * stage + 1],
                                                send_sem=send.at[n + 3 * k + stage], recv_sem=recv.at[n + 3 * k + stage],
                                                device_id=peer, device_id_type=MESH)

        def handover(k):
            return pltpu.make_async_remote_copy(src_ref=outs[k], dst_ref=outs[k], send_sem=send.at[n + 3 * nk + k],
                                                recv_sem=recv.at[n + 3 * nk + k], device_id=(x, y, 1 - c),
                                                device_id_type=MESH)

        halves = (tuple(k for k in range(nk) if ICI_CORE[k] == 0), tuple(k for k in range(nk) if ICI_CORE[k] == 1))
        for cc in range(2):
            @pl.when(c == cc)
            def _():
                mine, theirs = halves[cc], halves[1 - cc]
                for stage in range(2):
                    cps = [swap(k, stage) for k in mine]
                    for cp in cps:
                        cp.start()
                    for cp in cps:
                        cp.wait()
                    for k in mine:
                        if stage == 0:
                            slots[k][2] = slots[k][0] + slots[k][1]
                        else:
                            outs[k][...] = slots[k][2] + slots[k][3]
                over = [handover(k) for k in mine]
                for cp in over:
                    cp.start()
                for k in theirs:
                    handover(k).wait_recv()
                for cp in over:
                    cp.wait_send()

    vm = pl.BlockSpec(memory_space=pltpu.VMEM)
    return pl.pallas_call(
        body, in_specs=[vm] * n, out_specs=[vm] * nk, out_shape=[_sds((DEPTH,) + s) for s in shapes],
        scratch_shapes=([pltpu.VMEM((DEPTH,) + s, F32) for s in shapes]
                        + [pltpu.VMEM((NSHARD, DEPTH) + s, F32) for s in shapes]
                        + [pltpu.SemaphoreType.DMA((n + 4 * nk,)), pltpu.SemaphoreType.DMA((n + 4 * nk,))]),
        name="allreduce_small", compiler_params=pltpu.CompilerParams(vmem_limit_bytes=VMEM_LIMIT))(
            *[a for layer in per_layer for a in layer])


def _adamw_math(w, g, m, v):
    m = ADAM_B1 * m + (1.0 - ADAM_B1) * g
    v = ADAM_B2 * v + (1.0 - ADAM_B2) * jnp.square(g)
    m_hat = m / (1.0 - ADAM_B1 ** ADAM_STEP)
    v_hat = v / (1.0 - ADAM_B2 ** ADAM_STEP)
    delta = -ADAM_LR * (m_hat / (jnp.sqrt(v_hat) + ADAM_EPS) + ADAM_WD * w)
    return delta, m, v


def _adamw(g_parts, w, m, v):
    rows, cols = w.shape
    tr = 256 if rows % 256 == 0 else _row_tile(rows)
    k = len(g_parts)

    def body(*refs):
        g = refs[0][...]
        for r in refs[1:k]:
            g = g + r[...]
        w_ref, m_ref, v_ref, go, do, mo, vo = refs[k:]
        d, mn, vn = _adamw_math(w_ref[...], g, m_ref[...], v_ref[...])
        go[...] = g
        do[...] = d
        mo[...] = mn
        vo[...] = vn

    spec = pl.BlockSpec((tr, cols), lambda i: (i, 0))
    return pl.pallas_call(
        body, grid=(rows // tr,), in_specs=[spec] * (k + 3), out_specs=[spec] * 4,
        out_shape=[_sds((rows, cols))] * 4, name="adamw", compiler_params=_cp(("parallel",)))(*g_parts, w, m, v)


def _adamw_small(gs, ws, ms, vs):
    n = len(gs)

    def body(*refs):
        for k in range(n):
            d, mn, vn = _adamw_math(refs[n + k][...], refs[k][...], refs[2 * n + k][...], refs[3 * n + k][...])
            refs[4 * n + k][...] = d
            refs[5 * n + k][...] = mn
            refs[6 * n + k][...] = vn

    vm = pl.BlockSpec(memory_space=pltpu.VMEM)
    shapes = [_sds(a.shape) for a in ws]
    res = pl.pallas_call(
        body, in_specs=[vm] * (4 * n), out_specs=[vm] * (3 * n), out_shape=shapes * 3, name="adamw_small",
        compiler_params=pltpu.CompilerParams(vmem_limit_bytes=VMEM_LIMIT))(*gs, *ws, *ms, *vs)
    return res[:n], res[n:2 * n], res[2 * n:]


_ARGS = ("x", "w_in", "s5_a_re", "s5_a_im", "s5_log_step", "s5_b_re", "s5_b_im", "s5_c_re", "s5_c_im", "s5_d",
         "s5_w_glu", "s5_b_glu", "gla_w_a", "gla_b_a", "gla_ln_g", "swa_sink", "w_out", "ln1_g", "ln1_b", "w_ff1",
         "w_ff2", "ln2_g", "ln2_b")
_WEIGHTS = _ARGS[1:]


def _shard_cols(d):
    return d.reshape(d.shape[0], NSHARD, d.shape[1] // NSHARD).transpose(1, 0, 2)


def kernel(x, w_in, s5_a_re, s5_a_im, s5_log_step, s5_b_re, s5_b_im, s5_c_re, s5_c_im, s5_d, s5_w_glu, s5_b_glu, gla_w_a, gla_b_a, gla_ln_g, swa_sink, w_out, ln1_g, ln1_b, w_ff1, w_ff2, ln2_g, ln2_b, loss_target, m_w_in, m_s5_a_re, m_s5_a_im, m_s5_log_step, m_s5_b_re, m_s5_b_im, m_s5_c_re, m_s5_c_im, m_s5_d, m_s5_w_glu, m_s5_b_glu, m_gla_w_a, m_gla_b_a, m_gla_ln_g, m_swa_sink, m_w_out, m_ln1_g, m_ln1_b, m_w_ff1, m_w_ff2, m_ln2_g, m_ln2_b, v_w_in, v_s5_a_re, v_s5_a_im, v_s5_log_step, v_s5_b_re, v_s5_b_im, v_s5_c_re, v_s5_c_im, v_s5_d, v_s5_w_glu, v_s5_b_glu, v_gla_w_a, v_gla_b_a, v_gla_ln_g, v_swa_sink, v_w_out, v_ln1_g, v_ln1_b, v_w_ff1, v_w_ff2, v_ln2_g, v_ln2_b):
    given = dict(locals())
    w = {k: given[k] for k in _WEIGHTS}
    mom = {k: given["m_" + k] for k in _WEIGHTS}
    var = {k: given["v_" + k] for k in _WEIGHTS}

    me = (2 * lax.axis_index("x") + lax.axis_index("y")).astype(jnp.int32).reshape(1)
    tr = lambda t: t.transpose(0, 2, 1)
    shard = {k: (tr(w[k]) if k == "w_in" else w[k]) for k in BIG}
    qs = [None] * DEPTH

    first = ("w_in", "s5_w_glu", "w_out")
    follow = {(0, "w_in"): [(0, BIG[3:]), (1, first)], (0, "w_ff1"): [(1, BIG[3:])]}
    gathers = {}

    def start_gather(l, names, behind=None):
        lands = [_cast_to_slot(me, shard[k], l) for k in names]
        if behind is not None:
            lands, behind = lax.optimization_barrier((lands, behind))
        st = _push_start(f"gather_start_{l}_{names[0]}", lands, True)
        for k in names:
            gathers[l, k] = [names, st, None]
        return st[-1], behind

    token = start_gather(0, first[:1])[0] + start_gather(0, first[1:])[0]
    zero = token[0, 0]
    for l in range(DEPTH):
        qs[l] = _layer_prep({k: (w[k][l] + zero if k == "s5_a_re" else w[k][l]) for k in SMALL})
        token = token + qs[l]["tabc"][1][0, 0, 0, :8, :128] + qs[l]["bre"][0, 0, :8, :128].astype(F32)

    def fetch(l, name, after):
        names, st, got = gathers[l, name]
        tie = None
        if got is None:
            if l == 0 and name == "w_in":
                after = token
            lands = _push_wait(f"gather_wait_{l}_{names[0]}", st, after, True)
            for l2, names2 in follow.get((l, name), ()):
                tok, lands[0] = start_gather(l2, names2, lands[0])
                tie = tok if tie is None else tie + tok
            got = dict(zip(names, lands))
            for k in names:
                gathers[l, k][2] = got
        full = got[name]
        if name == "w_in":
            return _in_rows(full, token if tie is None else tie)
        if tie is not None:
            qs[l]["ln2_b"] = qs[l]["ln2_b"] + tie[0, 0]
        return full.reshape(D, D) if name == "w_out" else full

    scatters, held = [], {}

    def emit(l, grads):
        if l > 0:
            held.update(grads)
            if "w_in" not in grads:
                return 0.0
            grads = dict(held)
            held.clear()
        names = tuple(grads)
        st = _push_start(f"scatter_start_{l}_{names[0]}", [grads[k] for k in names], False)
        scatters.append((l, names, st))
        return st[-1][0, 0]

    loss, dx, smalls = _local_step(x.reshape(N, D), loss_target.reshape(N, D), qs, _rope_tables(128), fetch, emit)

    out, recv, own = {}, {}, {}

    def collect(keys, after):
        for l, names, st in scatters:
            if names[0] in keys:
                ops = _push_wait(f"scatter_wait_{l}_{names[0]}", st, after, False)
                for i, k in enumerate(names):
                    own[l, k], recv[l, k] = ops[i], ops[len(names) + i]

    def to_sibling(keys):
        sums = [_sum_sources(me, [recv[l, k] for l in range(DEPTH)], [own[l, k] for l in range(DEPTH)]) for k in keys]
        return _push_start(f"swap_start_{keys[0]}", sums, "sibling")

    def apply(keys, started, after):
        ops = _push_wait(f"swap_wait_{keys[0]}", started, after, "sibling")
        for i, k in enumerate(keys):
            mine, other = ops[i], ops[len(keys) + i]
            shp = shard[k].shape
            r = _adamw([mine, other], *((tr(t[k]) if k == "w_in" else t[k]).reshape(-1, shp[-1]) for t in (w, mom, var)))
            r = [t.reshape(shp) for t in r]
            out[k] = [tr(t) for t in r] if k == "w_in" else r
        return out[keys[-1]][1]

    collect(("w_ff1", "w_ff2", "w_out", "s5_w_glu"), dx)
    ff = to_sibling(("w_ff1", "w_ff2"))
    mix = to_sibling(("w_out", "s5_w_glu"))
    smalls[0]["db1"] = smalls[0]["db1"] + (ff[-1][0, 0] + mix[-1][0, 0])
    native = _allreduce_small([[smalls[l][k] for k in NATIVE] for l in range(DEPTH)])
    native = dict(zip(NATIVE, native))
    loss = native["loss"][0, 0, 0] + native["loss"][1, 0, 0]
    gsmall = _finish_small(native, w)
    view = lambda k, t: t.transpose(0, 1, 2, 4, 3) if k in ("s5_b_re", "s5_b_im") else t
    res = _adamw_small(*([view(k, t[k]) for k in SMALL] for t in (gsmall, w, mom, var)))
    for i, k in enumerate(SMALL):
        out[k] = [gsmall[k]] + [view(k, r[i]) for r in res]
    last = apply(("w_ff1", "w_ff2"), ff, res[0][-1])
    collect(("w_in",), last)
    win = to_sibling(("w_in",))
    last = apply(("w_out", "s5_w_glu"), mix, win[-1])
    apply(("w_in",), win, last)

    return (loss, dx.reshape(NSEQ, L, D), *[out[k][0] for k in _WEIGHTS], *[out[k][1] for k in _WEIGHTS],
            *[out[k][2] for k in _WEIGHTS], *[out[k][3] for k in _WEIGHTS])
```

```python
import functools
import math

import jax
import jax.numpy as jnp
from jax import lax
from jax.experimental import pallas as pl
from jax.experimental.pallas import tpu as pltpu

F32 = jnp.float32
MX = jnp.bfloat16
MESH = pl.DeviceIdType.MESH

DEPTH = 2
NSEQ = 2
L = 2048
N = NSEQ * L
D = 1024
DFF = 4096
NSHARD = 4
S5_G, S5_H, S5_P = 16, 16, 64
GLA_CHUNK = 64
NCHUNK = L // GLA_CHUNK
GLA_GROUP = 4
NGROUP = NCHUNK // GLA_GROUP
SWA_BLK = 128
NBLK = L // SWA_BLK
ROT = 16
ROPE_THETA = 500000.0
LN_EPS = 1e-5
ALPHA = (2 * DEPTH) ** 0.25
NEG_BIG = -1e30
DIN = 1824
DINP = 1920
ADAM_LR, ADAM_B1, ADAM_B2, ADAM_EPS, ADAM_WD, ADAM_STEP = 0.001, 0.9, 0.999, 1e-08, 0.01, 10
VMEM_LIMIT = 56 * 1024 * 1024
TT = 512
SW = 512
FFN_TM = 512
FFN_TM_W = 1024
FFN_WB = 1
FFN_VMEM = 60 * 1024 * 1024
INPROJ_BWD_TM = 512


def _cp(sem, vmem=VMEM_LIMIT):
    return pltpu.CompilerParams(dimension_semantics=sem, vmem_limit_bytes=vmem)


def _mm(a, b):
    return jnp.dot(a.astype(MX), b.astype(MX), preferred_element_type=F32)


def _mm_nt(a, b):
    return lax.dot_general(a.astype(MX), b.astype(MX), (((1,), (1,)), ((), ())), preferred_element_type=F32)


def _mm_tn(a, b):
    return lax.dot_general(a.astype(MX), b.astype(MX), (((0,), (0,)), ((), ())), preferred_element_type=F32)


@jax.custom_vjp
def _dmm(a, b):
    return _mm(a, b)


_dmm.defvjp(lambda a, b: (_mm(a, b), (a, b)), lambda r, g: (_mm_nt(g, r[1]), _mm_tn(r[0], g)))


@jax.custom_vjp
def _dmm_nt(a, b):
    return _mm_nt(a, b)


_dmm_nt.defvjp(lambda a, b: (_mm_nt(a, b), (a, b)), lambda r, g: (_mm(g, r[1]), _mm_tn(g, r[0])))


@jax.custom_vjp
def _dmm_tn(a, b):
    return _mm_tn(a, b)


_dmm_tn.defvjp(lambda a, b: (_mm_tn(a, b), (a, b)), lambda r, g: (_mm_nt(r[1], g), _mm(r[0], g)))


def _split3(x):
    hi = x.astype(MX)
    r1 = x - hi.astype(F32)
    mid = r1.astype(MX)
    lo = (r1 - mid.astype(F32)).astype(MX)
    return hi, mid, lo


def _chunk_pairs(rows, rev, strict):
    r = lax.broadcasted_iota(jnp.int32, (rows, rows), 0)
    c = lax.broadcasted_iota(jnp.int32, (rows, rows), 1)
    order = ((c > r) if strict else (c >= r)) if rev else ((c < r) if strict else (c <= r))
    return (r // GLA_CHUNK == c // GLA_CHUNK) & order


def _cums_impl(x, rev):
    rows, w = x.shape
    t = jnp.where(_chunk_pairs(rows, rev, False), 1.0, 0.0).astype(MX)
    s = jnp.dot(t, jnp.concatenate(_split3(x), axis=1), preferred_element_type=F32)
    return s[:, 0:w] + s[:, w:2 * w] + s[:, 2 * w:3 * w]


@functools.partial(jax.custom_vjp, nondiff_argnums=(1,))
def _cums(x, rev):
    return _cums_impl(x, rev)


_cums.defvjp(lambda x, rev: (_cums_impl(x, rev), None), lambda rev, r, g: (_cums_impl(g, not rev),))


def _ln_fwd(s, g, b):
    mu = jnp.mean(s, axis=-1, keepdims=True)
    xc = s - mu
    var = jnp.mean(xc * xc, axis=-1, keepdims=True)
    return xc * lax.rsqrt(var + LN_EPS) * g + b


def _ln_bwd(dy, s, g):
    mu = jnp.mean(s, axis=-1, keepdims=True)
    xc = s - mu
    var = jnp.mean(xc * xc, axis=-1, keepdims=True)
    rstd = lax.rsqrt(var + LN_EPS)
    xhat = xc * rstd
    dxh = dy * g
    ds = rstd * (dxh - jnp.mean(dxh, axis=-1, keepdims=True) - xhat * jnp.mean(dxh * xhat, axis=-1, keepdims=True))
    return ds, jnp.sum(dy * xhat, axis=0, keepdims=True), jnp.sum(dy, axis=0, keepdims=True)


def _sds(shape, dtype=F32):
    return jax.ShapeDtypeStruct(shape, dtype)


_IN_ROW_PIECES = (((0, 0), (0, 456)), ((1, 0), (456, 456)), ((2, 0), (912, 112)), ((2, 112), (1792, 32)),
                  ((2, 144), (1024, 312)), ((3, 0), (1336, 456)))


def _in_rows(g4, behind):
    def body(g_ref, behind_ref, o_ref, tmp):
        tmp[DIN:DINP] = jnp.zeros((DINP - DIN, D), F32)
        for (j, s0), (d0, n_) in _IN_ROW_PIECES:
            tmp[d0:d0 + n_] = g_ref[j, s0:s0 + n_].astype(F32)
        o_ref[...] = tmp[...].astype(MX)

    vm = pl.BlockSpec(memory_space=pltpu.VMEM)
    return pl.pallas_call(body, in_specs=[vm, pl.BlockSpec(memory_space=pl.ANY)], out_specs=vm,
                          out_shape=_sds((DINP, D), MX), scratch_shapes=[pltpu.VMEM((DINP, D), F32)], name="in_rows",
                          compiler_params=pltpu.CompilerParams(vmem_limit_bytes=VMEM_LIMIT))(g4, behind)


def _inproj_fwd(x, wt):
    tm = 512

    def body(x_ref, w_ref, h_ref):
        h_ref[...] = _mm_nt(x_ref[...], w_ref[...])

    return pl.pallas_call(
        body, grid=(N // tm,),
        in_specs=[pl.BlockSpec((tm, D), lambda i: (i, 0)), pl.BlockSpec((DINP, D), lambda i: (0, 0))],
        out_specs=pl.BlockSpec((tm, DINP), lambda i: (i, 0)),
        out_shape=_sds((N, DINP)), name="inproj_fwd", compiler_params=_cp(("parallel",)))(x, wt)


def _inproj_bwd(x, w, dxp, du2, dud, gq_f, gq_b, gk_f, gk_b, gv_f, gv_b, gr, daq, dakv, dhl):
    tm = INPROJ_BWD_TM
    nt = N // tm

    def body(x_ref, w_ref, dxp_ref, du2_ref, dud_ref, gqf, gqb, gkf, gkb, gvf, gvb, gr_ref, daq_ref, dakv_ref, dhl_ref,
             dx_ref, dw_ref, acc):
        i = pl.program_id(0)
        f = lambda r: r[...].astype(F32)
        dh = jnp.concatenate([
            du2_ref[0] + du2_ref[1] + f(dud_ref), f(gqf) + f(gqb), f(gkf) + f(gkb), f(gvf) + f(gvb),
            f(gr_ref), f(daq_ref), f(dakv_ref), f(dhl_ref)], axis=1)
        dx_ref[...] = dxp_ref[...] + _mm(dh, w_ref[...])
        contrib = _mm_tn(dh, x_ref[...])

        @pl.when(i == 0)
        def _():
            acc[...] = contrib

        @pl.when(i > 0)
        def _():
            acc[...] += contrib

        @pl.when(i == nt - 1)
        def _():
            for (j, d0), (s0, n_) in _IN_ROW_PIECES:
                dw_ref[j, d0:d0 + n_] = acc[s0:s0 + n_].astype(MX)

    row = lambda w_: pl.BlockSpec((tm, w_), lambda i: (i, 0))
    return pl.pallas_call(
        body, grid=(nt,),
        in_specs=[row(D), pl.BlockSpec((DINP, D), lambda i: (0, 0)), row(D),
                  pl.BlockSpec((2, tm, 256), lambda i: (0, i, 0)), row(256), row(128), row(128), row(128), row(128),
                  row(256), row(256), row(256), row(512), row(256), row(128)],
        out_specs=[row(D), pl.BlockSpec((NSHARD, DIN // NSHARD, D), lambda i: (0, 0, 0))],
        out_shape=[_sds((N, D)), _sds((NSHARD, DIN // NSHARD, D), MX)],
        scratch_shapes=[pltpu.VMEM((DINP, D), F32)],
        name="inproj_bwd", compiler_params=_cp(("arbitrary",)))(
            x, w, dxp, du2, dud, gq_f, gq_b, gk_f, gk_b, gv_f, gv_b, gr, daq, dakv, dhl)


def _scan_tables(mr, mi, reverse):
    pw = [(mr, mi)]
    for _ in range(7):
        pr, pi = pw[-1]
        pw.append((pr * mr - pi * mi, pr * mi + pi * mr))
    rows = jnp.arange(8)[:, None]
    out = []
    for d in (1, 2, 4):
        keep = rows >= d
        out += [jnp.where(keep, pw[d - 1][0][None], 0.0), jnp.where(keep, pw[d - 1][1][None], 0.0)]
    out += [jnp.stack([p[0] for p in pw]), jnp.stack([p[1] for p in pw])]
    t = jnp.stack(out)
    if reverse:
        t = t[:, ::-1, :]
    return t.reshape(8, 8, 2, SW).transpose(2, 0, 1, 3)


def _tile_scan(xr, xi, a, cr, ci, reverse):
    for lvl, d in enumerate((1, 2, 4)):
        sh = 8 - d if reverse else d
        sr = pltpu.roll(xr, sh, 0)
        si = pltpu.roll(xi, sh, 0)
        ar, ai = a[2 * lvl], a[2 * lvl + 1]
        xr, xi = xr + ar * sr - ai * si, xi + ar * si + ai * sr
    pr, pi = a[6], a[7]
    return xr + pr * cr - pi * ci, xi + pr * ci + pi * cr


NJ = TT // 8


def _lockstep_tables(mr, mi, reverse):
    nr, ni = mr, mi
    for _ in range(NJ.bit_length() - 1):
        nr, ni = nr * nr - ni * ni, 2.0 * nr * ni
    pr, pi = mr[None], mi[None]
    while pr.shape[0] < NJ:
        k = pr.shape[0]
        tr, ti = pr[k - 1], pi[k - 1]
        pr, pi = (jnp.concatenate([pr, pr * tr - pi * ti]), jnp.concatenate([pi, pr * ti + pi * tr]))
    if reverse:
        pr, pi = pr[::-1], pi[::-1]
    rows = jnp.broadcast_to(jnp.stack([mr, mi])[:, None, :], (2, 8, 2 * SW))
    link = _scan_tables(nr, ni, reverse)
    a = jnp.concatenate([rows.reshape(2, 8, 2, SW).transpose(2, 0, 1, 3), link], axis=1)
    return a, jnp.stack([pr, pi]).reshape(2, NJ, 2, SW).transpose(2, 0, 1, 3)


def _to_lockstep(ref, *lead):
    return jnp.concatenate([ref[(*lead, pl.ds(j, 8, stride=NJ), slice(None))] for j in range(NJ)], axis=0)


def _from_lockstep(val, ref, *lead):
    for j in range(NJ):
        ref[(*lead, pl.ds(j, 8, stride=NJ), slice(None))] = val[8 * j:8 * j + 8]


def _expand_powers(p_ref, pexp):
    for c in range(2):
        for j in range(NJ):
            pexp[c, j] = jnp.broadcast_to(p_ref[0, 0, c, j:j + 1, :], (8, SW))


def _lockstep_scan(xre, xim, a_ref, pexp, car, reverse, extra=None):
    a = [a_ref[0, 0, k] for k in range(10)]
    mr, mi = a[0], a[1]
    order = (lambda i: NJ - 1 - i) if reverse else (lambda i: i)

    def local(i, hcar):
        hr, hi = hcar
        r0 = pl.multiple_of(order(i) * 8, 8)
        hr, hi = mr * hr - mi * hi + xre[pl.ds(r0, 8), :], mr * hi + mi * hr + xim[pl.ds(r0, 8), :]
        xre[pl.ds(r0, 8), :] = hr
        xim[pl.ds(r0, 8), :] = hi
        return hr, hi

    z8 = jnp.zeros((8, SW), F32)
    er, ei = lax.fori_loop(0, NJ, local, (z8, z8), unroll=4)
    c0r, c0i = car[0], car[1]
    er, ei = _tile_scan(er, ei, a[2:], c0r, c0i, reverse)
    rowid = lax.broadcasted_iota(jnp.int32, (8, SW), 0)
    first, sh, last = (7, 7, 0) if reverse else (0, 1, 7)
    cvr = jnp.where(rowid == first, c0r, pltpu.roll(er, sh, 0))
    cvi = jnp.where(rowid == first, c0i, pltpu.roll(ei, sh, 0))
    car[0] = jnp.broadcast_to(er[last:last + 1, :], (8, SW))
    car[1] = jnp.broadcast_to(ei[last:last + 1, :], (8, SW))

    def fix(i, carry):
        j = order(i)
        r0 = pl.multiple_of(j * 8, 8)
        pr, pi = pexp[0, j], pexp[1, j]
        sr = xre[pl.ds(r0, 8), :] + pr * cvr - pi * cvi
        si = xim[pl.ds(r0, 8), :] + pr * cvi + pi * cvr
        xre[pl.ds(r0, 8), :] = sr
        xim[pl.ds(r0, 8), :] = si
        if extra is None:
            return carry
        return (sr, si, extra(r0, sr, si, carry[0], carry[1], carry[2]))

    init = (cvr, cvi, extra(None, None, None, None, None, None)) if extra is not None else 0
    return lax.fori_loop(0, NJ, fix, init, unroll=4)


def _s5_time_block(z, s, t, adjoint):
    flip = (1 - z) if adjoint else z
    return s * (L // TT) + t + flip * (L // TT - 1 - 2 * t)


def _s5_fwd(h, bre, bim, cre, cim, tab):
    nt = L // TT
    taba, tabp = tab

    def body(u_ref, bre_ref, bim_ref, cre_ref, cim_ref, a_ref, p_ref, hre_ref, him_ref, y_ref, car, pexp):
        z = pl.program_id(1)
        s = pl.program_id(2)
        tc = pl.program_id(3)

        @pl.when(tc == 0)
        def _():
            car[...] = jnp.zeros_like(car)

        @pl.when((tc == 0) & (s == 0))
        def _():
            _expand_powers(p_ref, pexp)

        u = _to_lockstep(u_ref)
        hre_ref[0] = _mm(u, bre_ref[0, 0])
        him_ref[0] = _mm(u, bim_ref[0, 0])

        @pl.when(z == 0)
        def _():
            _lockstep_scan(hre_ref.at[0], him_ref.at[0], a_ref, pexp, car, False)

        @pl.when(z == 1)
        def _():
            _lockstep_scan(hre_ref.at[0], him_ref.at[0], a_ref, pexp, car, True)

        _from_lockstep(_mm(hre_ref[0], cre_ref[0, 0]) - _mm(him_ref[0], cim_ref[0, 0]), y_ref, 0)

    tb = lambda b, z, s, t: _s5_time_block(z, s, t, False)
    wspec = lambda r, c: pl.BlockSpec((1, 1, r, c), lambda b, z, s, t: (z, b, 0, 0))
    return pl.pallas_call(
        body, grid=(2, 2, NSEQ, nt),
        in_specs=[pl.BlockSpec((TT, 128), lambda b, z, s, t: (tb(b, z, s, t), b)),
                  wspec(128, SW), wspec(128, SW), wspec(SW, 128), wspec(SW, 128),
                  pl.BlockSpec((1, 1, 10, 8, SW), lambda b, z, s, t: (z, b, 0, 0, 0)),
                  pl.BlockSpec((1, 1, 2, NJ, SW), lambda b, z, s, t: (z, b, 0, 0, 0))],
        out_specs=[pl.BlockSpec((1, TT, SW), lambda b, z, s, t: (z, tb(b, z, s, t), b)),
                   pl.BlockSpec((1, TT, SW), lambda b, z, s, t: (z, tb(b, z, s, t), b)),
                   pl.BlockSpec((1, TT, 128), lambda b, z, s, t: (z, tb(b, z, s, t), b))],
        out_shape=[_sds((2, N, 2 * SW)), _sds((2, N, 2 * SW)), _sds((2, N, 256))],
        scratch_shapes=[pltpu.VMEM((2, 8, SW), F32), pltpu.VMEM((2, NJ, 8, SW), F32)],
        name="s5_fwd", compiler_params=_cp(("arbitrary",) * 4))(h, bre, bim, cre, cim, taba, tabp)


def _s5_bwd(h, dyp, hre, him, bre, bim, cre, cim, tabc):
    nt = L // TT
    taba, tabp = tabc

    def body(u_ref, dy_ref, hre_ref, him_ref, bre_ref, bim_ref, cre_ref, cim_ref, a_ref, p_ref,
             du_ref, dbre_ref, dbim_ref, dcre_ref, dcim_ref, dmu_ref, gre, gim, car, acc, macc, pexp):
        z = pl.program_id(1)
        s = pl.program_id(2)
        tc = pl.program_id(3)

        @pl.when(tc == 0)
        def _():
            car[...] = jnp.zeros_like(car)

        @pl.when((tc == 0) & (s == 0))
        def _():
            acc[...] = jnp.zeros_like(acc)
            macc[...] = jnp.zeros_like(macc)
            _expand_powers(p_ref, pexp)

        dy = _to_lockstep(dy_ref)
        gre[...] = _mm_nt(dy, cre_ref[0, 0])
        gim[...] = -_mm_nt(dy, cim_ref[0, 0])

        def run(reverse):
            def pair(r0, gr_, gi_, pvr, pvi, m):
                if r0 is None:
                    return (macc[0], macc[1])
                hr = hre_ref[0, pl.ds(r0, 8), :]
                hi = him_ref[0, pl.ds(r0, 8), :]
                return (m[0] + pvr * hr + pvi * hi, m[1] + pvi * hr - pvr * hi)

            _, _, (dmr, dmi) = _lockstep_scan(gre, gim, a_ref, pexp, car, reverse, pair)
            macc[0] = dmr
            macc[1] = dmi

        @pl.when(z == 0)
        def _():
            run(True)

        @pl.when(z == 1)
        def _():
            run(False)

        gr = gre[...]
        gi = gim[...]
        u = _to_lockstep(u_ref)
        _from_lockstep(_mm_nt(gr, bre_ref[0, 0]) + _mm_nt(gi, bim_ref[0, 0]), du_ref, 0)
        acc[0] += _mm_tn(u, gr)
        acc[1] += _mm_tn(u, gi)
        acc[2] += _mm_tn(dy, hre_ref[0])
        acc[3] -= _mm_tn(dy, him_ref[0])

        @pl.when((tc == nt - 1) & (s == NSEQ - 1))
        def _():
            grp = lax.broadcasted_iota(jnp.int32, (S5_H, SW), 1) // S5_P
            for k, out in enumerate((dbre_ref, dbim_ref, dcre_ref, dcim_ref)):
                c = jnp.zeros((S5_H, SW), F32)
                for i in range(8):
                    c = c + jnp.where(grp == i, acc[k, i * S5_H:(i + 1) * S5_H, :], 0.0)
                out[0, 0] = c
            dmu_ref[0, 0] = jnp.concatenate([jnp.sum(macc[0], axis=0, keepdims=True),
                                             jnp.sum(macc[1], axis=0, keepdims=True)], axis=0)

    tb = lambda b, z, s, t: _s5_time_block(z, s, t, True)
    wspec = lambda r, c: pl.BlockSpec((1, 1, r, c), lambda b, z, s, t: (z, b, 0, 0))
    tok = lambda w_: pl.BlockSpec((TT, w_), lambda b, z, s, t: (tb(b, z, s, t), b))
    st = pl.BlockSpec((1, TT, SW), lambda b, z, s, t: (z, tb(b, z, s, t), b))
    return pl.pallas_call(
        body, grid=(2, 2, NSEQ, nt),
        in_specs=[tok(128), tok(128), st, st, wspec(128, SW), wspec(128, SW), wspec(SW, 128), wspec(SW, 128),
                  pl.BlockSpec((1, 1, 10, 8, SW), lambda b, z, s, t: (z, b, 0, 0, 0)),
                  pl.BlockSpec((1, 1, 2, NJ, SW), lambda b, z, s, t: (z, b, 0, 0, 0))],
        out_specs=[pl.BlockSpec((1, TT, 128), lambda b, z, s, t: (z, tb(b, z, s, t), b)),
                   wspec(S5_H, SW), wspec(S5_H, SW), wspec(S5_H, SW), wspec(S5_H, SW),
                   wspec(2, SW)],
        out_shape=[_sds((2, N, 256))] + [_sds((2, 2, S5_H, SW))] * 4 + [_sds((2, 2, 2, SW))],
        scratch_shapes=[pltpu.VMEM((TT, SW), F32), pltpu.VMEM((TT, SW), F32), pltpu.VMEM((2, 8, SW), F32),
                        pltpu.VMEM((4, 128, SW), F32), pltpu.VMEM((2, 8, SW), F32), pltpu.VMEM((2, NJ, 8, SW), F32)],
        name="s5_bwd", compiler_params=_cp(("arbitrary",) * 4))(h, dyp, hre, him, bre, bim, cre, cim, taba, tabp)


_GELU_C = math.sqrt(2.0 / math.pi)


def _gelu(y):
    return 0.5 * y * (1.0 + jnp.tanh(_GELU_C * (y + 0.044715 * y * y * y)))


def _gelu_grad(y):
    t = jnp.tanh(_GELU_C * (y + 0.044715 * y * y * y))
    return 0.5 * (1.0 + t) + 0.5 * y * (1.0 - t * t) * _GELU_C * (1.0 + 3 * 0.044715 * y * y)


def _glu_halves(w4_ref):
    return (jnp.concatenate([w4_ref[0], w4_ref[1]], axis=1), jnp.concatenate([w4_ref[2], w4_ref[3]], axis=1))


def _s5_glu_fwd(y2, h, dsk, w4, bv, bg):
    tm = 512

    def body(y2_ref, u_ref, d_ref, w4_ref, bv_ref, bg_ref, ya_ref):
        wv, wg = _glu_halves(w4_ref)
        z = _gelu(y2_ref[0] + y2_ref[1] + d_ref[...] * u_ref[...])
        val = _mm(z, wv) + bv_ref[...]
        gate = _mm(z, wg) + bg_ref[...]
        ya_ref[...] = (val * jax.nn.sigmoid(gate)).astype(MX)

    full = lambda r, c: pl.BlockSpec((r, c), lambda i: (0, 0))
    return pl.pallas_call(
        body, grid=(N // tm,),
        in_specs=[pl.BlockSpec((2, tm, 256), lambda i: (0, i, 0)), pl.BlockSpec((tm, 256), lambda i: (i, 0)),
                  full(1, 256), pl.BlockSpec((NSHARD, 256, 128), lambda i: (0, 0, 0)), full(1, 256), full(1, 256)],
        out_specs=pl.BlockSpec((tm, 256), lambda i: (i, 0)),
        out_shape=_sds((N, 256), MX), name="s5_glu_fwd", compiler_params=_cp(("parallel",)))(y2, h, dsk, w4, bv, bg)


def _s5_glu_bwd(y2, h, dsk, w4, bv, bg, dya):
    tm = 512
    nt = N // tm

    def body(y2_ref, u_ref, d_ref, w4_ref, bv_ref, bg_ref, dya_ref,
             dyp_ref, dud_ref, dd_ref, dw4_ref, dbv_ref, dbg_ref, accv, accg):
        i = pl.program_id(0)

        @pl.when(i == 0)
        def _():
            for r in (dd_ref, accv, accg, dbv_ref, dbg_ref):
                r[...] = jnp.zeros_like(r)

        wv, wg = _glu_halves(w4_ref)
        u = u_ref[...]
        y = y2_ref[0] + y2_ref[1] + d_ref[...] * u
        z = _gelu(y)
        val = _mm(z, wv) + bv_ref[...]
        sig = jax.nn.sigmoid(_mm(z, wg) + bg_ref[...])
        dya = dya_ref[...]
        dval = dya * sig
        dgate = dya * val * sig * (1.0 - sig)
        dz = _mm_nt(dval, wv) + _mm_nt(dgate, wg)
        dy = dz * _gelu_grad(y)
        dyp_ref[...] = dy
        dud_ref[...] = (dy * d_ref[...]).astype(MX)
        dd_ref[...] += jnp.sum(dy * u, axis=0, keepdims=True)
        accv[...] += _mm_tn(z, dval)
        accg[...] += _mm_tn(z, dgate)
        dbv_ref[...] += jnp.sum(dval, axis=0, keepdims=True)
        dbg_ref[...] += jnp.sum(dgate, axis=0, keepdims=True)

        @pl.when(i == nt - 1)
        def _():
            dw4_ref[0] = accv[:, 0:128].astype(MX)
            dw4_ref[1] = accv[:, 128:256].astype(MX)
            dw4_ref[2] = accg[:, 0:128].astype(MX)
            dw4_ref[3] = accg[:, 128:256].astype(MX)

    full = lambda r, c: pl.BlockSpec((r, c), lambda i: (0, 0))
    row = pl.BlockSpec((tm, 256), lambda i: (i, 0))
    wspec = pl.BlockSpec((NSHARD, 256, 128), lambda i: (0, 0, 0))
    return pl.pallas_call(
        body, grid=(nt,),
        in_specs=[pl.BlockSpec((2, tm, 256), lambda i: (0, i, 0)), row, full(1, 256), wspec, full(1, 256), full(1, 256),
                  row],
        out_specs=[row, row, full(1, 256), wspec, full(1, 256), full(1, 256)],
        out_shape=[_sds((N, 256)), _sds((N, 256), MX), _sds((1, 256)), _sds((NSHARD, 256, 128), MX), _sds((1, 256)),
                   _sds((1, 256))],
        scratch_shapes=[pltpu.VMEM((256, 256), F32), pltpu.VMEM((256, 256), F32)],
        name="s5_glu_bwd", compiler_params=_cp(("arbitrary",)))(y2, h, dsk, w4, bv, bg, dya)


def _logsig(x):
    return jnp.minimum(x, 0.0) - jnp.log(1.0 + jnp.exp(-jnp.abs(x)))


def _gla_gate_fwd(h, wa, ba):
    tm = 512

    def body(hl_ref, wa_ref, ba_ref, la_ref):
        la_ref[...] = _logsig(_mm(hl_ref[...], wa_ref[...]) + ba_ref[...]) * (1.0 / 16.0)

    return pl.pallas_call(
        body, grid=(N // tm,),
        in_specs=[pl.BlockSpec((tm, 128), lambda i: (i, 14)), pl.BlockSpec((128, 256), lambda i: (0, 0)),
                  pl.BlockSpec((1, 256), lambda i: (0, 0))],
        out_specs=pl.BlockSpec((tm, 256), lambda i: (i, 0)),
        out_shape=_sds((N, 256)), name="gla_gate_fwd", compiler_params=_cp(("parallel",)))(h, wa, ba)


def _gla_gate_bwd(h, wa, ba, dla_f, dla_b):
    tm = 512

    def body(hl_ref, wa_ref, ba_ref, df_ref, db_ref, dhl_ref, dwa_ref, dba_ref):
        i = pl.program_id(0)

        @pl.when(i == 0)
        def _():
            dwa_ref[...] = jnp.zeros_like(dwa_ref)
            dba_ref[...] = jnp.zeros_like(dba_ref)

        hl = hl_ref[...]
        pre = _mm(hl, wa_ref[...]) + ba_ref[...]
        dpre = jnp.concatenate([df_ref[...], db_ref[...]], axis=1) * (1.0 / 16.0) * jax.nn.sigmoid(-pre)
        dhl_ref[...] = _mm_nt(dpre, wa_ref[...]).astype(MX)
        dwa_ref[...] += _mm_tn(hl, dpre)[0:32]
        dba_ref[...] += jnp.sum(dpre, axis=0, keepdims=True)

    row = pl.BlockSpec((tm, 128), lambda i: (i, 0))
    return pl.pallas_call(
        body, grid=(N // tm,),
        in_specs=[pl.BlockSpec((tm, 128), lambda i: (i, 14)), pl.BlockSpec((128, 256), lambda i: (0, 0)),
                  pl.BlockSpec((1, 256), lambda i: (0, 0)), row, row],
        out_specs=[row, pl.BlockSpec((32, 256), lambda i: (0, 0)), pl.BlockSpec((1, 256), lambda i: (0, 0))],
        out_shape=[_sds((N, 128), MX), _sds((32, 256)), _sds((1, 256))],
        name="gla_gate_bwd", compiler_params=_cp(("arbitrary",)))(h, wa, ba, dla_f, dla_b)


def _gla_chunk(q, k, v, la, st, rev):
    c = GLA_CHUNK
    rows = q.shape[0]
    nch = rows // c
    b = _cums(la, rev)
    blc = [jnp.sum(la[i * c:(i + 1) * c], axis=0, keepdims=True) for i in range(nch)]
    bl = jnp.concatenate([jnp.broadcast_to(t, (c, 128)) for t in blc], axis=0)
    q_in = q * (32.0 ** -0.5) * jnp.exp(b)
    k_in = k * jnp.exp(-b)
    k_st = k * jnp.exp(bl - b)
    lane_k = lax.broadcasted_iota(jnp.int32, (1, 128), 1) // 32
    lane_v = lax.broadcasted_iota(jnp.int32, (1, 256), 1) // 64
    qs = jnp.concatenate([jnp.where(lane_k == hd, q_in, 0.0) for hd in range(4)], axis=0)
    a = _dmm_nt(qs, k_in)
    a = jnp.where(jnp.concatenate([_chunk_pairs(rows, rev, rev)] * 4, axis=0), a, 0.0)
    o4 = _dmm(a, v)
    o = jnp.zeros((rows, 256), F32)
    for hd in range(4):
        o = o + jnp.where(lane_v == hd, o4[hd * rows:(hd + 1) * rows], 0.0)
    bd = (lax.broadcasted_iota(jnp.int32, (256, 128), 0) // 64) == (lax.broadcasted_iota(jnp.int32, (256, 128), 1) // 32)
    inter = [None] * nch
    for i in (reversed(range(nch)) if rev else range(nch)):
        sl = slice(i * c, (i + 1) * c)
        inter[i] = _dmm_nt(q_in[sl], st)
        st = jnp.exp(blc[i]) * st + jnp.where(bd, _dmm_tn(v[sl], k_st[sl]), 0.0)
    return o + jnp.concatenate(inter, axis=0), st


def _gla_chunk_of(c, rev):
    return NGROUP - 1 - c if rev else c


def _gla_fwd(h, la2):
    c = GLA_GROUP * GLA_CHUNK

    def body(qf, kf, vf, laf, qb, kb, vb, lab, of_ref, ob_ref, sf_ref, sb_ref, stf, stb):
        @pl.when(pl.program_id(0) == 0)
        def _():
            stf[...] = jnp.zeros_like(stf)
            stb[...] = jnp.zeros_like(stb)

        ins = [(qf[s], kf[s], vf[s], laf[s], stf[s], qb[s], kb[s], vb[s], lab[s], stb[s]) for s in range(NSEQ)]
        outs = [(_gla_chunk(*t[:5], False), _gla_chunk(*t[5:], True)) for t in ins]
        for s in range(NSEQ):
            sf_ref[s, 0] = ins[s][4]
            sb_ref[s, 0] = ins[s][9]
            (of_ref[s], stf[s]), (ob_ref[s], stb[s]) = outs[s]

    def specs(rev):
        ch = lambda i: _gla_chunk_of(i, rev)
        return [pl.BlockSpec((NSEQ, c, 128), lambda i: (0, ch(i), 2)), pl.BlockSpec((NSEQ, c, 128), lambda i: (0, ch(i), 3)),
                pl.BlockSpec((NSEQ, c, 256), lambda i: (0, ch(i), 2)),
                pl.BlockSpec((NSEQ, c, 128), lambda i: (0, ch(i), 1 if rev else 0))]

    orow = lambda rev: pl.BlockSpec((NSEQ, c, 256), lambda i: (0, _gla_chunk_of(i, rev), 0))
    srow = lambda rev: pl.BlockSpec((NSEQ, 1, 256, 128), lambda i: (0, _gla_chunk_of(i, rev), 0, 0))
    h3, la3 = h.reshape(NSEQ, L, DINP), la2.reshape(NSEQ, L, 256)
    of, ob, sf, sb = pl.pallas_call(
        body, grid=(NGROUP,),
        in_specs=specs(False) + specs(True),
        out_specs=[orow(False), orow(True), srow(False), srow(True)],
        out_shape=[_sds((NSEQ, L, 256)), _sds((NSEQ, L, 256)), _sds((NSEQ, NGROUP, 256, 128)),
                   _sds((NSEQ, NGROUP, 256, 128))],
        scratch_shapes=[pltpu.VMEM((NSEQ, 256, 128), F32), pltpu.VMEM((NSEQ, 256, 128), F32)],
        name="gla_fwd", compiler_params=_cp(("arbitrary",)))(h3, h3, h3, la3, h3, h3, h3, la3)
    return of.reshape(N, 256), ob.reshape(N, 256), sf, sb


def _gla_bwd(h, la2, do, sf, sb):
    c = GLA_GROUP * GLA_CHUNK

    def body(qf, kf, vf, laf, dof, sfr, qb, kb, vb, lab, dob, sbr,
             dqf, dkf, dvf, dlf, dqb, dkb, dvb, dlb, dstf, dstb):
        @pl.when(pl.program_id(0) == 0)
        def _():
            dstf[...] = jnp.zeros_like(dstf)
            dstb[...] = jnp.zeros_like(dstb)

        def one(s, q, k, v, la, do_, st, dst, rev):
            _, vjp = jax.vjp(functools.partial(_gla_chunk, rev=rev), q[s], k[s], v[s], la[s], st[s, 0])
            return vjp((do_[s], dst[s]))

        res = [(one(s, qf, kf, vf, laf, dof, sfr, dstf, False), one(s, qb, kb, vb, lab, dob, sbr, dstb, True))
               for s in range(NSEQ)]
        for s in range(NSEQ):
            for (gq, gk, gv, gl, gs), (dq, dk, dv, dl, dst) in ((res[s][0], (dqf, dkf, dvf, dlf, dstf)),
                                                                  (res[s][1], (dqb, dkb, dvb, dlb, dstb))):
                dq[s], dk[s], dv[s] = gq.astype(MX), gk.astype(MX), gv.astype(MX)
                dl[s], dst[s] = gl, gs

    def specs(rev):
        ch = lambda i: _gla_chunk_of(i, not rev)
        return [pl.BlockSpec((NSEQ, c, 128), lambda i: (0, ch(i), 2)), pl.BlockSpec((NSEQ, c, 128), lambda i: (0, ch(i), 3)),
                pl.BlockSpec((NSEQ, c, 256), lambda i: (0, ch(i), 2)),
                pl.BlockSpec((NSEQ, c, 128), lambda i: (0, ch(i), 1 if rev else 0)),
                pl.BlockSpec((NSEQ, c, 256), lambda i: (0, ch(i), 0)),
                pl.BlockSpec((NSEQ, 1, 256, 128), lambda i: (0, ch(i), 0, 0))]

    def ospecs(rev):
        ch = lambda i: _gla_chunk_of(i, not rev)
        n = pl.BlockSpec((NSEQ, c, 128), lambda i: (0, ch(i), 0))
        return [n, n, pl.BlockSpec((NSEQ, c, 256), lambda i: (0, ch(i), 0)), n]

    oshape = [_sds((NSEQ, L, 128), MX), _sds((NSEQ, L, 128), MX), _sds((NSEQ, L, 256), MX), _sds((NSEQ, L, 128))]
    h3, la3, do3 = h.reshape(NSEQ, L, DINP), la2.reshape(NSEQ, L, 256), do.reshape(NSEQ, L, 256)
    res = pl.pallas_call(
        body, grid=(NGROUP,),
        in_specs=specs(False) + specs(True),
        out_specs=ospecs(False) + ospecs(True),
        out_shape=oshape + oshape,
        scratch_shapes=[pltpu.VMEM((NSEQ, 256, 128), F32), pltpu.VMEM((NSEQ, 256, 128), F32)],
        name="gla_bwd", compiler_params=_cp(("arbitrary",)))(h3, h3, h3, la3, do3, sf, h3, h3, h3, la3, do3, sb)
    return [r.reshape(N, r.shape[-1]) for r in res]


def _gla_post(of, ob, r, g):
    o = of + ob
    head = lax.broadcasted_iota(jnp.int32, (1, 256), 1) // 64
    mu = jnp.zeros_like(o)
    for hd in range(4):
        mu = mu + jnp.where(head == hd, jnp.sum(jnp.where(head == hd, o, 0.0), axis=-1, keepdims=True) * (1.0 / 64.0), 0.0)
    xc = o - mu
    var = jnp.zeros_like(o)
    for hd in range(4):
        var = var + jnp.where(head == hd, jnp.sum(jnp.where(head == hd, xc * xc, 0.0), axis=-1, keepdims=True) * (1.0 / 64.0), 0.0)
    return xc * lax.rsqrt(var + LN_EPS) * g * (r * jax.nn.sigmoid(r))


def _gla_post_fwd(of, ob, h, g):
    tm = 512

    def body(of_ref, ob_ref, r_ref, g_ref, y_ref):
        y_ref[...] = _gla_post(of_ref[...], ob_ref[...], r_ref[...], g_ref[...]).astype(MX)

    row = pl.BlockSpec((tm, 256), lambda i: (i, 0))
    return pl.pallas_call(
        body, grid=(N // tm,),
        in_specs=[row, row, pl.BlockSpec((tm, 256), lambda i: (i, 3)), pl.BlockSpec((1, 256), lambda i: (0, 0))],
        out_specs=row, out_shape=_sds((N, 256), MX), name="gla_post_fwd", compiler_params=_cp(("parallel",)))(of, ob, h, g)


def _gla_post_bwd(of, ob, h, g, dyb):
    tm = 512

    def body(of_ref, ob_ref, r_ref, g_ref, dy_ref, do_ref, dr_ref, dg_ref):
        @pl.when(pl.program_id(0) == 0)
        def _():
            dg_ref[...] = jnp.zeros_like(dg_ref)

        _, vjp = jax.vjp(_gla_post, of_ref[...], ob_ref[...], r_ref[...], g_ref[...])
        go, _, gr, gg = vjp(dy_ref[...])
        do_ref[...] = go
        dr_ref[...] = gr.astype(MX)
        dg_ref[...] += gg

    row = pl.BlockSpec((tm, 256), lambda i: (i, 0))
    one = pl.BlockSpec((1, 256), lambda i: (0, 0))
    return pl.pallas_call(
        body, grid=(N // tm,),
        in_specs=[row, row, pl.BlockSpec((tm, 256), lambda i: (i, 3)), one, row],
        out_specs=[row, row, one], out_shape=[_sds((N, 256)), _sds((N, 256), MX), _sds((1, 256))],
        name="gla_post_bwd", compiler_params=_cp(("arbitrary",)))(of, ob, h, g, dyb)


def _rope_tables(width):
    pos = jnp.arange(L, dtype=F32)
    inv_freq = ROPE_THETA ** (-jnp.arange(0, ROT, 2, dtype=F32) / ROT)
    ang = pos[:, None] * inv_freq[None, :]
    cos, sin = jnp.cos(ang), jnp.sin(ang)
    one = jnp.ones((L, 64 - ROT), F32)
    zero = jnp.zeros((L, 64 - ROT), F32)
    z8 = jnp.zeros((L, ROT // 2), F32)
    c = jnp.concatenate([cos, cos, one], axis=1)
    sa = jnp.concatenate([z8, sin, zero], axis=1)
    sb = jnp.concatenate([-sin, z8, zero], axis=1)
    rep = width // 64
    return jnp.stack([jnp.tile(c, (1, rep)), jnp.tile(sa, (1, rep)), jnp.tile(sb, (1, rep))])


def _pieces(t, f):
    out = [f(t[:, c * 128:(c + 1) * 128]) for c in range(t.shape[-1] // 128)]
    return out[0] if len(out) == 1 else jnp.concatenate(out, axis=1)


def _rope(t, tab):
    return _pieces(t, lambda x: x * tab[0] + pltpu.roll(x, ROT // 2, 1) * tab[1] + pltpu.roll(x, 128 - ROT // 2, 1) * tab[2])


def _rope_t(g, tab):
    return _pieces(g, lambda x: x * tab[0] + pltpu.roll(x * tab[1], 128 - ROT // 2, 1) + pltpu.roll(x * tab[2], ROT // 2, 1))


def _swa_pad_kv(kv_ref, tk_ref, kexp, vexp):
    z = jnp.zeros((SWA_BLK, 256), F32)
    kr = _rope(kv_ref[:, 0:128], tk_ref[...])
    for hk in range(2):
        for pad in (kexp, vexp):
            pad[hk, 0:SWA_BLK] = z
            pad[hk, SWA_BLK + L:] = z
        kexp[hk, SWA_BLK:SWA_BLK + L] = _swa_expand(kr, hk)
        vexp[hk, SWA_BLK:SWA_BLK + L] = _swa_expand(kv_ref[:, 128:256], hk)


def _swa_expand(x, hk):
    lane = lax.broadcasted_iota(jnp.int32, x.shape, 1)
    sw = pltpu.roll(x, 64, 1)
    pair = jnp.where(lane < 64, x, sw) if hk == 0 else jnp.where(lane < 64, sw, x)
    return jnp.concatenate([pair, pair], axis=1)


def _swa_fold(x, hk):
    a = x[:, 0:128] + x[:, 128:256]
    t = a + pltpu.roll(a, 64, 1)
    lane = lax.broadcasted_iota(jnp.int32, a.shape, 1)
    return jnp.where((lane < 64) if hk == 0 else (lane >= 64), t, 0.0)


def _swa_probs(q2, kexp, n, sink_ref, hk):
    slot = lax.broadcasted_iota(jnp.int32, (1, 256), 1) // 64
    qs = jnp.concatenate([jnp.where(slot == g, q2, 0.0) for g in range(4)], axis=0)
    s = _mm_nt(qs, kexp) * 0.125
    i = lax.broadcasted_iota(jnp.int32, (SWA_BLK, 3 * SWA_BLK), 0)
    j = lax.broadcasted_iota(jnp.int32, (SWA_BLK, 3 * SWA_BLK), 1)
    kpos = n * SWA_BLK - SWA_BLK + j
    ok = (j - i >= 0) & (j - i <= 2 * SWA_BLK) & (kpos >= 0) & (kpos < L)
    s = jnp.where(jnp.concatenate([ok] * 4, axis=0), s, NEG_BIG)
    rowg = lax.broadcasted_iota(jnp.int32, (4 * SWA_BLK, 1), 0) // SWA_BLK
    sink = jnp.zeros((4 * SWA_BLK, 1), F32)
    for g in range(4):
        sink = jnp.where(rowg == g, sink_ref[hk * 4 + g], sink)
    m = jnp.maximum(jnp.max(s, axis=-1, keepdims=True), sink)
    p = jnp.exp(s - m)
    ps = jnp.exp(sink - m)
    inv = 1.0 / (jnp.sum(p, axis=-1, keepdims=True) + ps)
    return qs, p * inv, ps * inv, slot, rowg


def _swa_qtab(tk_ref, r0):
    return [tk_ref[i, pl.ds(r0, SWA_BLK), :] for i in range(3)]


def _swa_fwd(h, tk, sink):
    def body(sink_ref, q_ref, kv_ref, tk_ref, y_ref, kexp, vexp):
        n = pl.program_id(1)

        @pl.when(n == 0)
        def _():
            _swa_pad_kv(kv_ref, tk_ref, kexp, vexp)

        r0 = pl.multiple_of(n * SWA_BLK, SWA_BLK)
        q = _rope(q_ref[...], _swa_qtab(tk_ref, r0))
        for hk in range(2):
            _, p, _, slot, _ = _swa_probs(q[:, hk * 256:(hk + 1) * 256], kexp[hk, pl.ds(r0, 3 * SWA_BLK), :], n,
                                          sink_ref, hk)
            o4 = _mm(p, vexp[hk, pl.ds(r0, 3 * SWA_BLK), :])
            o = jnp.zeros((SWA_BLK, 256), F32)
            for g in range(4):
                o = o + jnp.where(slot == g, o4[g * SWA_BLK:(g + 1) * SWA_BLK], 0.0)
            y_ref[:, hk * 256:(hk + 1) * 256] = o.astype(MX)

    return pl.pallas_call(
        body,
        grid_spec=pltpu.PrefetchScalarGridSpec(
            num_scalar_prefetch=1, grid=(NSEQ, NBLK),
            in_specs=[pl.BlockSpec((SWA_BLK, 512), lambda s, n, sk: (s * NBLK + n, 2)),
                      pl.BlockSpec((L, 256), lambda s, n, sk: (s, 6)),
                      pl.BlockSpec((3, L, 128), lambda s, n, sk: (0, 0, 0))],
            out_specs=pl.BlockSpec((SWA_BLK, 512), lambda s, n, sk: (s * NBLK + n, 0)),
            scratch_shapes=[pltpu.VMEM((2, L + 2 * SWA_BLK, 256), F32), pltpu.VMEM((2, L + 2 * SWA_BLK, 256), F32)]),
        out_shape=_sds((N, 512), MX), name="swa_fwd", compiler_params=_cp(("arbitrary", "arbitrary")))(sink, h, h, tk)


def _swa_bwd(h, tk, sink, dyc):
    def body(sink_ref, q_ref, kv_ref, tk_ref, dy_ref, dq_ref, dkv_ref, dsink_ref, kexp_all, vexp_all, dkacc, dvacc):
        sq = pl.program_id(0)
        n = pl.program_id(1)

        @pl.when(n == 0)
        def _():
            _swa_pad_kv(kv_ref, tk_ref, kexp_all, vexp_all)
            dkacc[...] = jnp.zeros_like(dkacc)
            dvacc[...] = jnp.zeros_like(dvacc)

        @pl.when((n == 0) & (sq == 0))
        def _():
            dsink_ref[...] = jnp.zeros_like(dsink_ref)

        r0 = pl.multiple_of(n * SWA_BLK, SWA_BLK)
        tq = _swa_qtab(tk_ref, r0)
        q = _rope(q_ref[...], tq)
        hrow = lax.broadcasted_iota(jnp.int32, (8, 128), 0)
        dsk = jnp.zeros((8, 128), F32)
        for hk in range(2):
            kexp = kexp_all[hk, pl.ds(r0, 3 * SWA_BLK), :]
            vexp = vexp_all[hk, pl.ds(r0, 3 * SWA_BLK), :]
            qs, p, ps, slot, rowg = _swa_probs(q[:, hk * 256:(hk + 1) * 256], kexp, n, sink_ref, hk)
            dy2 = dy_ref[:, hk * 256:(hk + 1) * 256]
            dos = jnp.concatenate([jnp.where(slot == g, dy2, 0.0) for g in range(4)], axis=0)
            dp = _mm_nt(dos, vexp)
            delta = jnp.sum(p * dp, axis=-1, keepdims=True)
            ds = p * (dp - delta) * 0.125
            dsr = -ps * delta
            for g in range(4):
                dsk = dsk + jnp.where(hrow == hk * 4 + g, jnp.sum(jnp.where(rowg == g, dsr, 0.0), axis=0, keepdims=True), 0.0)
            dq4 = _mm(ds, kexp)
            dq2 = jnp.zeros((SWA_BLK, 256), F32)
            for g in range(4):
                dq2 = dq2 + jnp.where(slot == g, dq4[g * SWA_BLK:(g + 1) * SWA_BLK], 0.0)
            dq_ref[:, hk * 256:(hk + 1) * 256] = _rope_t(dq2, tq).astype(MX)
            dkacc[hk, pl.ds(r0, 3 * SWA_BLK), :] += _mm_tn(ds, qs)
            dvacc[hk, pl.ds(r0, 3 * SWA_BLK), :] += _mm_tn(p, dos)
        dsink_ref[...] += dsk

        @pl.when(n == NBLK - 1)
        def _():
            seq = slice(SWA_BLK, SWA_BLK + L)
            dk = _rope_t(_swa_fold(dkacc[0, seq], 0) + _swa_fold(dkacc[1, seq], 1), tk_ref[...])
            dkv_ref[:, 0:128] = dk.astype(MX)
            dkv_ref[:, 128:256] = (_swa_fold(dvacc[0, seq], 0) + _swa_fold(dvacc[1, seq], 1)).astype(MX)

    blk = lambda col: pl.BlockSpec((SWA_BLK, 512), lambda s, n, sk: (s * NBLK + n, col))
    pad = pltpu.VMEM((2, L + 2 * SWA_BLK, 256), F32)
    return pl.pallas_call(
        body,
        grid_spec=pltpu.PrefetchScalarGridSpec(
            num_scalar_prefetch=1, grid=(NSEQ, NBLK),
            in_specs=[blk(2), pl.BlockSpec((L, 256), lambda s, n, sk: (s, 6)),
                      pl.BlockSpec((3, L, 128), lambda s, n, sk: (0, 0, 0)), blk(0)],
            out_specs=[blk(0), pl.BlockSpec((L, 256), lambda s, n, sk: (s, 0)),
                       pl.BlockSpec((8, 128), lambda s, n, sk: (0, 0))],
            scratch_shapes=[pad, pad, pad, pad]),
        out_shape=[_sds((N, 512), MX), _sds((N, 256), MX), _sds((8, 128))],
        name="swa_bwd", compiler_params=_cp(("arbitrary", "arbitrary")))(sink, h, h, tk, dyc)


def _outproj_fwd(ya, yb, yc, x, wo, g, b):
    tm = 512

    def body(ya_ref, yb_ref, yc_ref, x_ref, wo_ref, g_ref, b_ref, s_ref, x1_ref):
        mix = _mm(ya_ref[...], wo_ref[0:256]) + _mm(yb_ref[...], wo_ref[256:512]) + _mm(yc_ref[...], wo_ref[512:1024])
        s = ALPHA * x_ref[...] + mix
        s_ref[...] = s
        x1_ref[...] = _ln_fwd(s, g_ref[...], b_ref[...])

    row = lambda w_: pl.BlockSpec((tm, w_), lambda i: (i, 0))
    one = pl.BlockSpec((1, D), lambda i: (0, 0))
    return pl.pallas_call(
        body, grid=(N // tm,),
        in_specs=[row(256), row(256), row(512), row(D), pl.BlockSpec((D, D), lambda i: (0, 0)), one, one],
        out_specs=[row(D), row(D)], out_shape=[_sds((N, D)), _sds((N, D))],
        name="outproj_fwd", compiler_params=_cp(("parallel",)))(ya, yb, yc, x, wo, g, b)


def _outproj_bwd(dx1, s1, ya, yb, yc, wo, g):
    tm = 512
    nt = N // tm

    def body(dx1_ref, s_ref, ya_ref, yb_ref, yc_ref, wo_ref, g_ref,
             dya_ref, dyb_ref, dyc_ref, dxp_ref, dwo_ref, dg_ref, db_ref, acc):
        i = pl.program_id(0)

        @pl.when(i == 0)
        def _():
            acc[...] = jnp.zeros_like(acc)
            dg_ref[...] = jnp.zeros_like(dg_ref)
            db_ref[...] = jnp.zeros_like(db_ref)

        ds, dg, db = _ln_bwd(dx1_ref[...], s_ref[...], g_ref[...])
        dg_ref[...] += dg
        db_ref[...] += db
        dxp_ref[...] = ALPHA * ds
        dy = _mm_nt(ds, wo_ref[...])
        dya_ref[...] = dy[:, 0:256]
        dyb_ref[...] = dy[:, 256:512]
        dyc_ref[...] = dy[:, 512:1024]
        acc[0:256] += _mm_tn(ya_ref[...], ds)
        acc[256:512] += _mm_tn(yb_ref[...], ds)
        acc[512:1024] += _mm_tn(yc_ref[...], ds)

        @pl.when(i == nt - 1)
        def _():
            dwo_ref[...] = acc[...].astype(MX)

    row = lambda w_: pl.BlockSpec((tm, w_), lambda i: (i, 0))
    one = pl.BlockSpec((1, D), lambda i: (0, 0))
    full = pl.BlockSpec((D, D), lambda i: (0, 0))
    return pl.pallas_call(
        body, grid=(nt,),
        in_specs=[row(D), row(D), row(256), row(256), row(512), full, one],
        out_specs=[row(256), row(256), row(512), row(D), full, one, one],
        out_shape=[_sds((N, 256)), _sds((N, 256)), _sds((N, 512)), _sds((N, D)), _sds((D, D), MX), _sds((1, D)), _sds((1, D))],
        scratch_shapes=[pltpu.VMEM((D, D), F32)],
        name="outproj_bwd", compiler_params=_cp(("arbitrary",)))(dx1, s1, ya, yb, yc, wo, g)


def _ffn_fwd(x1, w1, w2, g, b, target=None):
    tm = FFN_TM
    head = target is not None

    def body(*refs):
        x_ref, w1_ref, w2_ref, g_ref, b_ref = refs[:5]
        a_ref, s_ref, y_ref = refs[5 + head:8 + head]
        x = x_ref[...]
        xb = x.astype(MX)
        s = ALPHA * x
        for j in range(NSHARD):
            a = _mm(xb, w1_ref[j])
            a_ref[:, j * D:(j + 1) * D] = a.astype(MX)
            s = s + _mm(jnp.square(jnp.maximum(a, 0.0)), w2_ref[j])
        s_ref[...] = s
        x2 = _ln_fwd(s, g_ref[...], b_ref[...])
        if not head:
            y_ref[...] = x2
            return
        l_ref = refs[-1]

        @pl.when(pl.program_id(0) == 0)
        def _():
            l_ref[...] = jnp.zeros_like(l_ref)

        e = x2 - refs[5][...]
        y_ref[...] = e * (1.0 / D)
        l_ref[...] += jnp.sum(jnp.sum(e * e, axis=1, keepdims=True), axis=0, keepdims=True) * (0.5 / D)

    row = pl.BlockSpec((tm, D), lambda i: (i, 0))
    wall = pl.BlockSpec((NSHARD, D, D), lambda i: (0, 0, 0))
    one = pl.BlockSpec((1, D), lambda i: (0, 0))
    acc = pl.BlockSpec((8, 128), lambda i: (0, 0))
    return pl.pallas_call(
        body, grid=(N // tm,),
        in_specs=[row, wall, wall, one, one] + [row] * head,
        out_specs=[pl.BlockSpec((tm, DFF), lambda i: (i, 0)), row, row] + [acc] * head,
        out_shape=[_sds((N, DFF), MX), _sds((N, D)), _sds((N, D))] + [_sds((8, 128))] * head,
        name="ffn_fwd", compiler_params=_cp(("arbitrary",), FFN_VMEM))(x1, w1, w2, g, b, *([target] * head))


def _ffn_bwd_act(dy, s2, a, w1, w2, g):
    tm = FFN_TM

    def body(dy_ref, s_ref, a_ref, w1_ref, w2_ref, g_ref, da_ref, ds_ref, dx1_ref, dg_ref, db_ref):
        @pl.when(pl.program_id(0) == 0)
        def _():
            dg_ref[...] = jnp.zeros_like(dg_ref)
            db_ref[...] = jnp.zeros_like(db_ref)

        ds, dg, db = _ln_bwd(dy_ref[...], s_ref[...], g_ref[...])
        dsb = ds.astype(MX)
        ds_ref[...] = dsb
        dg_ref[...] += dg
        db_ref[...] += db
        dx1 = ALPHA * ds
        for j in range(NSHARD):
            da = (_mm_nt(dsb, w2_ref[j]) * 2.0 * jnp.maximum(a_ref[:, j * D:(j + 1) * D].astype(F32), 0.0)).astype(MX)
            da_ref[:, j * D:(j + 1) * D] = da
            dx1 = dx1 + _mm_nt(da, w1_ref[j])
        dx1_ref[...] = dx1

    row = pl.BlockSpec((tm, D), lambda i: (i, 0))
    wide = pl.BlockSpec((tm, DFF), lambda i: (i, 0))
    wall = pl.BlockSpec((NSHARD, D, D), lambda i: (0, 0, 0))
    one = pl.BlockSpec((1, D), lambda i: (0, 0))
    return pl.pallas_call(
        body, grid=(N // tm,),
        in_specs=[row, row, wide, wall, wall, one],
        out_specs=[wide, row, row, one, one],
        out_shape=[_sds((N, DFF), MX), _sds((N, D), MX), _sds((N, D)), _sds((1, D)), _sds((1, D))],
        name="ffn_bwd_act", compiler_params=_cp(("arbitrary",), FFN_VMEM))(dy, s2, a, w1, w2, g)


def _ffn_bwd_w(x1, da, a, ds):
    tm, nb = FFN_TM_W, FFN_WB
    nt = N // tm

    def body(x_ref, da_ref, a_ref, ds_ref, dw1_ref, dw2_ref, acc1, acc2):
        i = pl.program_id(1)

        @pl.when(i == 0)
        def _():
            acc1[...] = jnp.zeros_like(acc1)
            acc2[...] = jnp.zeros_like(acc2)

        x, ds_ = x_ref[...], ds_ref[...]
        for k in range(nb):
            cols = slice(k * D, (k + 1) * D)
            acc1[k] += _mm_tn(x, da_ref[:, cols])
            acc2[k] += _mm_tn(jnp.square(jnp.maximum(a_ref[:, cols].astype(F32), 0.0)), ds_)

        @pl.when(i == nt - 1)
        def _():
            dw1_ref[...] = acc1[...].astype(MX)
            dw2_ref[...] = acc2[...].astype(MX)

    row = pl.BlockSpec((tm, D), lambda j, i: (i, 0))
    col = pl.BlockSpec((tm, nb * D), lambda j, i: (i, j))
    wj = pl.BlockSpec((nb, D, D), lambda j, i: (j, 0, 0))
    return pl.pallas_call(
        body, grid=(NSHARD // nb, nt),
        in_specs=[row, col, col, row], out_specs=[wj, wj],
        out_shape=[_sds((NSHARD, D, D), MX), _sds((NSHARD, D, D), MX)],
        scratch_shapes=[pltpu.VMEM((nb, D, D), F32), pltpu.VMEM((nb, D, D), F32)],
        name="ffn_bwd_w", compiler_params=_cp(("parallel", "arbitrary"), FFN_VMEM))(x1, da, a, ds)


def _loss_head(y, target):
    tm = 512

    def body(y_ref, t_ref, dy_ref, l_ref):
        @pl.when(pl.program_id(0) == 0)
        def _():
            l_ref[...] = jnp.zeros_like(l_ref)

        e = y_ref[...] - t_ref[...]
        dy_ref[...] = e * (1.0 / D)
        l_ref[...] += jnp.sum(jnp.sum(e * e, axis=1, keepdims=True), axis=0, keepdims=True) * (0.5 / D)

    row = pl.BlockSpec((tm, D), lambda i: (i, 0))
    return pl.pallas_call(
        body, grid=(N // tm,), in_specs=[row, row],
        out_specs=[row, pl.BlockSpec((8, 128), lambda i: (0, 0))],
        out_shape=[_sds((N, D)), _sds((8, 128))], name="loss_head", compiler_params=_cp(("arbitrary",)))(y, target)


def _s5_discretize(a_re, a_im, log_step, b_re, b_im):
    lam = lax.complex(a_re, a_im)
    lam_bar = jnp.exp(lam * jnp.exp(log_step))
    b_bar = ((lam_bar - 1.0) / lam)[..., None] * lax.complex(b_re, b_im)
    return jnp.real(lam_bar), jnp.imag(lam_bar), jnp.real(b_bar), jnp.imag(b_bar)


def _s5_in_blocks(b):
    e = jnp.eye(8, dtype=F32)
    return jnp.einsum('ij,zbjph->zbihjp', e, b.reshape(2, 2, 8, S5_P, S5_H)).reshape(2, 2, 128, SW)


def _s5_in_unblocks(d):
    return jnp.einsum('zbihip->zbiph', d.reshape(2, 2, 8, S5_H, 8, S5_P)).reshape(2, S5_G, S5_P, S5_H)


def _s5_out_blocks(c):
    e = jnp.eye(8, dtype=F32)
    return jnp.einsum('ij,zbjhp->zbjpih', e, c.reshape(2, 2, 8, S5_H, S5_P)).reshape(2, 2, SW, 128)


def _s5_out_unblocks(d):
    return jnp.einsum('zbipih->zbihp', d.reshape(2, 2, 8, S5_P, 8, S5_H)).reshape(2, S5_G, S5_H, S5_P)


def _gate_weight(w_a):
    z = jnp.zeros((16, 128), F32)
    top = jnp.concatenate([w_a[0], z], axis=1)
    bot = jnp.concatenate([z, w_a[1]], axis=1)
    return jnp.concatenate([top, bot, jnp.zeros((96, 256), F32)], axis=0)


def _layer_prep(p):
    lr, li, br, bi = _s5_discretize(p["s5_a_re"], p["s5_a_im"], p["s5_log_step"], p["s5_b_re"], p["s5_b_im"])
    q = dict(p)
    q["bre"] = _s5_in_blocks(br).astype(MX)
    q["bim"] = _s5_in_blocks(bi).astype(MX)
    q["cre"] = _s5_out_blocks(p["s5_c_re"]).astype(MX)
    q["cim"] = _s5_out_blocks(p["s5_c_im"]).astype(MX)
    mr, mi = lr.reshape(2, 1024), li.reshape(2, 1024)
    both = lambda t0, t1: tuple(jnp.stack(p) for p in zip(t0, t1))
    q["tab"] = both(_lockstep_tables(mr[0], mi[0], False), _lockstep_tables(mr[1], mi[1], True))
    q["tabc"] = both(_lockstep_tables(mr[0], -mi[0], True), _lockstep_tables(mr[1], -mi[1], False))
    q["dsk"] = p["s5_d"].reshape(1, 256)
    q["wa"] = _gate_weight(p["gla_w_a"]).astype(MX)
    q["ba"] = p["gla_b_a"].reshape(1, 256)
    q["lng"] = p["gla_ln_g"].reshape(1, 256)
    q["bv"] = p["s5_b_glu"][:256].reshape(1, 256)
    q["bg"] = p["s5_b_glu"][256:].reshape(1, 256)
    for k in ("ln1_g", "ln1_b", "ln2_g", "ln2_b"):
        q[k] = p[k].reshape(1, D)
    return q


def _layer_fwd(x, q, tk, fetch, target=None):
    q["w_in"] = fetch("w_in", x)
    h = _inproj_fwd(x, q["w_in"])
    hre, him, y2 = _s5_fwd(h, q["bre"], q["bim"], q["cre"], q["cim"], q["tab"])
    q["w4"] = fetch("s5_w_glu", y2)
    ya = _s5_glu_fwd(y2, h, q["dsk"], q["w4"], q["bv"], q["bg"])
    la2 = _gla_gate_fwd(h, q["wa"], q["ba"])
    of, ob, sf, sb = _gla_fwd(h, la2)
    yb = _gla_post_fwd(of, ob, h, q["lng"])
    yc = _swa_fwd(h, tk, q["swa_sink"])
    q["w_out"] = fetch("w_out", yc)
    s1, x1 = _outproj_fwd(ya, yb, yc, x, q["w_out"], q["ln1_g"], q["ln1_b"])
    q["w_ff1"] = fetch("w_ff1", x1)
    q["w_ff2"] = fetch("w_ff2", x1)
    a, s2, *out = _ffn_fwd(x1, q["w_ff1"], q["w_ff2"], q["ln2_g"], q["ln2_b"], target)
    saved = dict(x=x, h=h, hre=hre, him=him, y2=y2, ya=ya, la2=la2, of=of, ob=ob, sf=sf, sb=sb, yb=yb, yc=yc,
                 s1=s1, x1=x1, a=a, s2=s2)
    return (out[0] if target is None else tuple(out)), saved


def _layer_bwd(dy, q, sv, tk, emit):
    g = {}
    da, ds2, dx1, g["dg2"], g["db2"] = _ffn_bwd_act(dy, sv["s2"], sv["a"], q["w_ff1"], q["w_ff2"], q["ln2_g"])
    dw1, dw2 = _ffn_bwd_w(sv["x1"], da, sv["a"], ds2)
    tie = emit(dict(w_ff1=dw1, w_ff2=dw2))
    dya, dyb, dyc, dxp, dwo, g["dg1"], g["db1"] = _outproj_bwd(dx1, sv["s1"], sv["ya"], sv["yb"], sv["yc"],
                                                               q["w_out"], q["ln1_g"] + tie)
    h = sv["h"]
    daq, dakv, g["dsink"] = _swa_bwd(h, tk, q["swa_sink"], dyc)
    do, gr, g["dlng"] = _gla_post_bwd(sv["of"], sv["ob"], h, q["lng"], dyb)
    gq_f, gk_f, gv_f, gl_f, gq_b, gk_b, gv_b, gl_b = _gla_bwd(h, sv["la2"], do, sv["sf"], sv["sb"])
    dhl, g["dwa"], g["dba"] = _gla_gate_bwd(h, q["wa"], q["ba"], gl_f, gl_b)
    dyp, dud, g["dd"], dw4, g["dbv"], g["dbg"] = _s5_glu_bwd(sv["y2"], h, q["dsk"], q["w4"], q["bv"], q["bg"], dya)
    tie = emit(dict(w_out=dwo.reshape(NSHARD, D // NSHARD, D), s5_w_glu=dw4))
    du2, g["dbre"], g["dbim"], g["dcre"], g["dcim"], g["dmu"] = _s5_bwd(
        h, dyp, sv["hre"], sv["him"], q["bre"], q["bim"], q["cre"], q["cim"], (q["tabc"][0], q["tabc"][1] + tie))
    dx, dwt = _inproj_bwd(sv["x"], q["w_in"], dxp, du2, dud, gq_f, gq_b, gk_f, gk_b, gv_f, gv_b, gr, daq, dakv, dhl)
    tie = emit(dict(w_in=dwt))
    return dx, g, tie


NATIVE = ("dmu", "dbre", "dbim", "dcre", "dcim", "dd", "dbv", "dbg", "dwa", "dba", "dlng", "dsink",
          "dg1", "db1", "dg2", "db2", "loss")
ICI_CORE = (0, 0, 0, 1, 1, 0, 0, 0, 1, 1, 1, 1, 0, 0, 1, 1, 0)


def _finish_small(n, w):
    g = {}
    dmu = n["dmu"]
    dlr = dmu[:, :, :, 0].reshape(DEPTH, 2, S5_G, S5_P)
    dli = dmu[:, :, :, 1].reshape(DEPTH, 2, S5_G, S5_P)

    def unblock(c, perm, shape):
        return c.reshape(DEPTH, 2, 2, S5_H, 8, S5_P).transpose(perm).reshape(shape)

    b_shape, c_shape = (DEPTH, 2, S5_G, S5_P, S5_H), (DEPTH, 2, S5_G, S5_H, S5_P)
    _, vjp = jax.vjp(_s5_discretize, w["s5_a_re"], w["s5_a_im"], w["s5_log_step"], w["s5_b_re"], w["s5_b_im"])
    (g["s5_a_re"], g["s5_a_im"], g["s5_log_step"], g["s5_b_re"], g["s5_b_im"]) = vjp(
        (dlr, dli, unblock(n["dbre"], (0, 1, 2, 4, 5, 3), b_shape), unblock(n["dbim"], (0, 1, 2, 4, 5, 3), b_shape)))
    g["s5_c_re"] = unblock(n["dcre"], (0, 1, 2, 4, 3, 5), c_shape)
    g["s5_c_im"] = unblock(n["dcim"], (0, 1, 2, 4, 3, 5), c_shape)
    g["s5_d"] = n["dd"].reshape(DEPTH, S5_G, S5_H)
    g["s5_b_glu"] = jnp.concatenate([n["dbv"], n["dbg"]], axis=2).reshape(DEPTH, 512)
    g["gla_w_a"] = jnp.stack([n["dwa"][:, 0:16, 0:128], n["dwa"][:, 16:32, 128:256]], axis=1)
    g["gla_b_a"] = n["dba"].reshape(DEPTH, 2, 128)
    g["gla_ln_g"] = n["dlng"].reshape(DEPTH, 256)
    g["swa_sink"] = n["dsink"][:, :, 0]
    for k, s in (("ln1_g", "dg1"), ("ln1_b", "db1"), ("ln2_g", "dg2"), ("ln2_b", "db2")):
        g[k] = n[s].reshape(DEPTH, D)
    return g


def _local_step(x, target, qs, tk, fetch, emit):
    saved = []
    for l, q in enumerate(qs):
        x, sv = _layer_fwd(x, q, tk, functools.partial(fetch, l), target if l == DEPTH - 1 else None)
        saved.append(sv)
    dy, lacc = x
    smalls = [None] * DEPTH
    tie = 0.0
    for l in reversed(range(DEPTH)):
        qs[l]["ln2_g"] = qs[l]["ln2_g"] + tie
        dy, smalls[l], tie = _layer_bwd(dy, qs[l], saved[l], tk, functools.partial(emit, l))
    smalls[0]["db2"] = smalls[0]["db2"] + tie
    for l in range(DEPTH):
        smalls[l]["loss"] = lacc if l == 0 else jnp.zeros_like(lacc)
    return lacc[0, 0], dy, smalls


BIG = ("w_in", "s5_w_glu", "w_out", "w_ff1", "w_ff2")
SMALL = ("s5_a_re", "s5_a_im", "s5_log_step", "s5_b_re", "s5_b_im", "s5_c_re", "s5_c_im", "s5_d", "s5_b_glu",
         "gla_w_a", "gla_b_a", "gla_ln_g", "swa_sink", "ln1_g", "ln1_b", "ln2_g", "ln2_b")
ANY = pl.BlockSpec(memory_space=pl.ANY)


def _place():
    x, y, c = lax.axis_index("x"), lax.axis_index("y"), lax.axis_index("c")
    return x, y, c, [(1 - x, y), (x, 1 - y), (1 - x, 1 - y)]


HBM = pl.BlockSpec(memory_space=pltpu.HBM)
SEMS = pl.BlockSpec(memory_space=pltpu.SEMAPHORE)
EFFECT = pltpu.SideEffectType.DATAFLOW_SIDE_EFFECTING


def _push_copies(ins, lands, send, recv, gather, sending):
    x, y, c, chips = _place()
    me = 2 * x + y
    if gather == "sibling":
        return [pltpu.make_async_remote_copy(src_ref=ins[a], dst_ref=lands[a], send_sem=send.at[a], recv_sem=recv.at[a],
                                             device_id=(x, y, 1 - c), device_id_type=MESH) for a in range(len(lands))]
    out = []
    for a in range(len(lands)):
        for j, (px, py) in enumerate(chips):
            peer = 2 * px + py
            src = lands[a].at[me] if gather else ins[a].at[peer if sending else me]
            dst = lands[a].at[me if sending else peer]
            out.append(pltpu.make_async_remote_copy(src_ref=src, dst_ref=dst, send_sem=send.at[3 * a + j],
                                                    recv_sem=recv.at[3 * a + j], device_id=(px, py, c),
                                                    device_id_type=MESH))
    return out


def _push_start(name, arrs, gather):
    n = len(arrs)
    ops = list(arrs) if gather is True else list(arrs) + [lax.empty(s.shape, s.dtype) for s in arrs]
    m = len(ops)

    def body(*refs):
        ins, lnd = (refs[:n], refs[:n]) if gather is True else (refs[:n], refs[n:m])
        for cp in _push_copies(ins, lnd, refs[m], refs[m + 1], gather, True):
            cp.start()
        refs[-1][...] = jnp.zeros((8, 128), F32)

    ops = [pltpu.with_memory_space_constraint(t, pltpu.HBM) for t in ops]
    res = pl.pallas_call(
        body, name=name,
        out_shape=(pltpu.SemaphoreType.DMA((3 * n,)), pltpu.SemaphoreType.DMA((3 * n,)),
                   *[pltpu.HBM(t.shape, t.dtype) for t in ops], _sds((8, 128))),
        in_specs=[HBM] * m,
        out_specs=(SEMS, SEMS, *[HBM] * m, pl.BlockSpec(memory_space=pltpu.VMEM)),
        input_output_aliases={i: 2 + i for i in range(m)},
        compiler_params=pltpu.CompilerParams(has_side_effects=EFFECT))(*ops)
    return res[0], res[1], list(res[2:2 + m]), res[-1]


def _push_wait(name, started, after, gather):
    send, recv, ops, _ = started
    m = len(ops)
    n = m if gather is True else m // 2

    def body(*refs):
        ins, lnd = (refs[:n], refs[:n]) if gather is True else (refs[:n], refs[n:m])
        for cp in _push_copies(ins, lnd, refs[m], refs[m + 1], gather, False):
            cp.wait_send()
            cp.wait_recv()

    res = pl.pallas_call(
        body, name=name,
        out_shape=[pltpu.HBM(t.shape, t.dtype) for t in ops],
        in_specs=[HBM] * m + [SEMS, SEMS, ANY], out_specs=[HBM] * m,
        input_output_aliases={i: i for i in range(m)},
        compiler_params=pltpu.CompilerParams(has_side_effects=EFFECT))(*ops, send, recv, after)
    return list(res)


def _row_tile(rows):
    return max(t for t in range(8, min(rows, 512) + 1, 8) if rows % t == 0)


def _cast_to_slot(me, w, l):
    _, rows, cols = w.shape
    tr = _row_tile(rows)

    def body(me_ref, w_ref, o_ref):
        o_ref[0] = w_ref[0].astype(MX)

    return pl.pallas_call(
        body,
        grid_spec=pltpu.PrefetchScalarGridSpec(
            num_scalar_prefetch=1, grid=(rows // tr,),
            in_specs=[pl.BlockSpec((1, tr, cols), lambda i, me_: (l, i, 0))],
            out_specs=pl.BlockSpec((1, tr, cols), lambda i, me_: (me_[0], i, 0))),
        out_shape=_sds((NSHARD, rows, cols), MX), name="cast_to_slot", compiler_params=_cp(("arbitrary",)))(me, w)


def _sum_sources(me, recv, own):
    _, rows, cols = recv[0].shape
    tr = min(_row_tile(rows), 256) if rows % 256 == 0 else _row_tile(rows)
    nt = rows // tr

    def body(me_ref, *refs):
        o_ref = refs[-1]
        for l in range(DEPTH):
            @pl.when(pl.program_id(0) == l)
            def _():
                r_ref, own_ref = refs[2 * l], refs[2 * l + 1]
                part = [jnp.where(me_ref[0] == s, own_ref[0], r_ref[s]).astype(F32) for s in range(NSHARD)]
                o_ref[...] = ((part[0] + part[1]) + part[2]) + part[3]

    in_specs = []
    for l in range(DEPTH):
        pick = lambda g, i, me_, l=l: jnp.where(g == l, i, jnp.where(g < l, 0, nt - 1))
        in_specs += [pl.BlockSpec((NSHARD, tr, cols), lambda g, i, me_, pick=pick: (0, pick(g, i, me_), 0)),
                     pl.BlockSpec((1, tr, cols), lambda g, i, me_, pick=pick: (me_[0], pick(g, i, me_), 0))]
    return pl.pallas_call(
        body,
        grid_spec=pltpu.PrefetchScalarGridSpec(
            num_scalar_prefetch=1, grid=(DEPTH, nt), in_specs=in_specs,
            out_specs=pl.BlockSpec((tr, cols), lambda g, i, me_: (g * nt + i, 0))),
        out_shape=_sds((DEPTH * rows, cols)), name="sum_sources",
        compiler_params=_cp(("arbitrary", "arbitrary")))(me, *[t for l in range(DEPTH) for t in (recv[l], own[l])])


def _swap_sibling(arrs):
    n = len(arrs)

    def body(*refs):
        ins, outs = refs[:n], refs[n:2 * n]
        send, recv = refs[2 * n:]
        x, y, c, _ = _place()
        cps = [pltpu.make_async_remote_copy(src_ref=ins[a], dst_ref=outs[a], send_sem=send.at[a], recv_sem=recv.at[a],
                                            device_id=(x, y, 1 - c), device_id_type=MESH) for a in range(n)]
        for cp in cps:
            cp.start()
        for cp in cps:
            cp.wait()

    return pl.pallas_call(
        body, in_specs=[ANY] * n, out_specs=[ANY] * n, out_shape=[_sds(a.shape, a.dtype) for a in arrs],
        scratch_shapes=[pltpu.SemaphoreType.DMA((n,)), pltpu.SemaphoreType.DMA((n,))],
        name="swap_sibling")(*arrs)


def _allreduce_small(per_layer):
    nk = len(per_layer[0])
    n = DEPTH * nk
    shapes = [a.shape for a in per_layer[0]]

    def body(*refs):
        ins, outs = refs[:n], refs[n:n + nk]
        sibs, slots = refs[n + nk:n + 2 * nk], refs[n + 2 * nk:n + 3 * nk]
        send, recv = refs[n + 3 * nk:]
        x, y, c, chips = _place()
        me = 2 * x + y
        d2d = [pltpu.make_async_remote_copy(src_ref=ins[l * nk + k], dst_ref=sibs[k].at[l], send_sem=send.at[l * nk + k],
                                            recv_sem=recv.at[l * nk + k], device_id=(x, y, 1 - c), device_id_type=MESH)
               for l in range(DEPTH) for k in range(nk)]
        for cp in d2d:
            cp.start()
        for cp in d2d:
            cp.wait()
        for l in range(DEPTH):
            for k in range(nk):
                slots[k][0, l] = ins[l * nk + k][...] + sibs[k][l]

        def swap(k, stage):
            peer = (1 - x, y, c) if stage == 0 else (x, 1 - y, c)
            return pltpu.make_async_remote_copy(src_ref=slots[k].at[2 * stage], dst_ref=slots[k].at[2 * stage + 1],
                                                send_sem=send.at[n + 3 * k + stage], recv_sem=recv.at[n + 3 * k + stage],
                                                device_id=peer, device_id_type=MESH)

        def handover(k):
            return pltpu.make_async_remote_copy(src_ref=outs[k], dst_ref=outs[k], send_sem=send.at[n + 3 * nk + k],
                                                recv_sem=recv.at[n + 3 * nk + k], device_id=(x, y, 1 - c),
                                                device_id_type=MESH)

        halves = (tuple(k for k in range(nk) if ICI_CORE[k] == 0), tuple(k for k in range(nk) if ICI_CORE[k] == 1))
        for cc in range(2):
            @pl.when(c == cc)
            def _():
                mine, theirs = halves[cc], halves[1 - cc]
                for stage in range(2):
                    cps = [swap(k, stage) for k in mine]
                    for cp in cps:
                        cp.start()
                    for cp in cps:
                        cp.wait()
                    for k in mine:
                        if stage == 0:
                            slots[k][2] = slots[k][0] + slots[k][1]
                        else:
                            outs[k][...] = slots[k][2] + slots[k][3]
                over = [handover(k) for k in mine]
                for cp in over:
                    cp.start()
                for k in theirs:
                    handover(k).wait_recv()
                for cp in over:
                    cp.wait_send()

    vm = pl.BlockSpec(memory_space=pltpu.VMEM)
    return pl.pallas_call(
        body, in_specs=[vm] * n, out_specs=[vm] * nk, out_shape=[_sds((DEPTH,) + s) for s in shapes],
        scratch_shapes=([pltpu.VMEM((DEPTH,) + s, F32) for s in shapes]
                        + [pltpu.VMEM((NSHARD, DEPTH) + s, F32) for s in shapes]
                        + [pltpu.SemaphoreType.DMA((n + 4 * nk,)), pltpu.SemaphoreType.DMA((n + 4 * nk,))]),
        name="allreduce_small", compiler_params=pltpu.CompilerParams(vmem_limit_bytes=VMEM_LIMIT))(
            *[a for layer in per_layer for a in layer])


def _adamw_math(w, g, m, v):
    m = ADAM_B1 * m + (1.0 - ADAM_B1) * g
    v = ADAM_B2 * v + (1.0 - ADAM_B2) * jnp.square(g)
    m_hat = m / (1.0 - ADAM_B1 ** ADAM_STEP)
    v_hat = v / (1.0 - ADAM_B2 ** ADAM_STEP)
    delta = -ADAM_LR * (m_hat / (jnp.sqrt(v_hat) + ADAM_EPS) + ADAM_WD * w)
    return delta, m, v


def _adamw(g_parts, w, m, v):
    rows, cols = w.shape
    tr = 256 if rows % 256 == 0 else _row_tile(rows)
    k = len(g_parts)

    def body(*refs):
        g = refs[0][...]
        for r in refs[1:k]:
            g = g + r[...]
        w_ref, m_ref, v_ref, go, do, mo, vo = refs[k:]
        d, mn, vn = _adamw_math(w_ref[...], g, m_ref[...], v_ref[...])
        go[...] = g
        do[...] = d
        mo[...] = mn
        vo[...] = vn

    spec = pl.BlockSpec((tr, cols), lambda i: (i, 0))
    return pl.pallas_call(
        body, grid=(rows // tr,), in_specs=[spec] * (k + 3), out_specs=[spec] * 4,
        out_shape=[_sds((rows, cols))] * 4, name="adamw", compiler_params=_cp(("parallel",)))(*g_parts, w, m, v)


def _adamw_small(gs, ws, ms, vs):
    n = len(gs)

    def body(*refs):
        for k in range(n):
            d, mn, vn = _adamw_math(refs[n + k][...], refs[k][...], refs[2 * n + k][...], refs[3 * n + k][...])
            refs[4 * n + k][...] = d
            refs[5 * n + k][...] = mn
            refs[6 * n + k][...] = vn

    vm = pl.BlockSpec(memory_space=pltpu.VMEM)
    shapes = [_sds(a.shape) for a in ws]
    res = pl.pallas_call(
        body, in_specs=[vm] * (4 * n), out_specs=[vm] * (3 * n), out_shape=shapes * 3, name="adamw_small",
        compiler_params=pltpu.CompilerParams(vmem_limit_bytes=VMEM_LIMIT))(*gs, *ws, *ms, *vs)
    return res[:n], res[n:2 * n], res[2 * n:]


_ARGS = ("x", "w_in", "s5_a_re", "s5_a_im", "s5_log_step", "s5_b_re", "s5_b_im", "s5_c_re", "s5_c_im", "s5_d",
         "s5_w_glu", "s5_b_glu", "gla_w_a", "gla_b_a", "gla_ln_g", "swa_sink", "w_out", "ln1_g", "ln1_b", "w_ff1",
         "w_ff2", "ln2_g", "ln2_b")
_WEIGHTS = _ARGS[1:]


def _shard_cols(d):
    return d.reshape(d.shape[0], NSHARD, d.shape[1] // NSHARD).transpose(1, 0, 2)


def kernel(x, w_in, s5_a_re, s5_a_im, s5_log_step, s5_b_re, s5_b_im, s5_c_re, s5_c_im, s5_d, s5_w_glu, s5_b_glu, gla_w_a, gla_b_a, gla_ln_g, swa_sink, w_out, ln1_g, ln1_b, w_ff1, w_ff2, ln2_g, ln2_b, loss_target, m_w_in, m_s5_a_re, m_s5_a_im, m_s5_log_step, m_s5_b_re, m_s5_b_im, m_s5_c_re, m_s5_c_im, m_s5_d, m_s5_w_glu, m_s5_b_glu, m_gla_w_a, m_gla_b_a, m_gla_ln_g, m_swa_sink, m_w_out, m_ln1_g, m_ln1_b, m_w_ff1, m_w_ff2, m_ln2_g, m_ln2_b, v_w_in, v_s5_a_re, v_s5_a_im, v_s5_log_step, v_s5_b_re, v_s5_b_im, v_s5_c_re, v_s5_c_im, v_s5_d, v_s5_w_glu, v_s5_b_glu, v_gla_w_a, v_gla_b_a, v_gla_ln_g, v_swa_sink, v_w_out, v_ln1_g, v_ln1_b, v_w_ff1, v_w_ff2, v_ln2_g, v_ln2_b):
    given = dict(locals())
    w = {k: given[k] for k in _WEIGHTS}
    mom = {k: given["m_" + k] for k in _WEIGHTS}
    var = {k: given["v_" + k] for k in _WEIGHTS}

    me = (2 * lax.axis_index("x") + lax.axis_index("y")).astype(jnp.int32).reshape(1)
    tr = lambda t: t.transpose(0, 2, 1)
    shard = {k: (tr(w[k]) if k == "w_in" else w[k]) for k in BIG}
    qs = [None] * DEPTH

    first = ("w_in", "s5_w_glu", "w_out")
    follow = {(0, "w_in"): [(0, BIG[3:]), (1, first)], (0, "w_ff1"): [(1, BIG[3:])]}
    gathers = {}

    casts = {}

    def start_gather(l, names, behind=None):
        lands = [casts.pop((l, k)) if (l, k) in casts else _cast_to_slot(me, shard[k], l) for k in names]
        if behind is not None:
            lands, behind = lax.optimization_barrier((lands, behind))
        st = _push_start(f"gather_start_{l}_{names[0]}", lands, True)
        for k in names:
            gathers[l, k] = [names, st, None]
        return st[-1], behind

    token = start_gather(0, first[:1])[0] + start_gather(0, first[1:])[0]
    zero = token[0, 0]
    for l in range(DEPTH):
        for k in BIG:
            if (l, k) not in gathers:
                casts[l, k] = _cast_to_slot(me, lax.optimization_barrier((shard[k], token))[0], l)
        qs[l] = _layer_prep({k: (w[k][l] + zero if k == "s5_a_re" else w[k][l]) for k in SMALL})
    token, casts, qs = lax.optimization_barrier((token, casts, qs))

    def fetch(l, name, after):
        names, st, got = gathers[l, name]
        tie = None
        if got is None:
            if l == 0 and name == "w_in":
                after = token
            lands = _push_wait(f"gather_wait_{l}_{names[0]}", st, after, True)
            for l2, names2 in follow.get((l, name), ()):
                tok, lands[0] = start_gather(l2, names2, lands[0])
                tie = tok if tie is None else tie + tok
            got = dict(zip(names, lands))
            for k in names:
                gathers[l, k][2] = got
        full = got[name]
        if name == "w_in":
            return _in_rows(full, token if tie is None else tie)
        if tie is not None:
            qs[l]["ln2_b"] = qs[l]["ln2_b"] + tie[0, 0]
        return full.reshape(D, D) if name == "w_out" else full

    scatters, held = [], {}

    def emit(l, grads):
        if l > 0:
            held.update(grads)
            if "w_in" not in grads:
                return 0.0
            grads = dict(held)
            held.clear()
        names = tuple(grads)
        st = _push_start(f"scatter_start_{l}_{names[0]}", [grads[k] for k in names], False)
        scatters.append((l, names, st))
        return st[-1][0, 0]

    loss, dx, smalls = _local_step(x.reshape(N, D), loss_target.reshape(N, D), qs, _rope_tables(128), fetch, emit)

    out, recv, own = {}, {}, {}

    def collect(keys, after):
        for l, names, st in scatters:
            if names[0] in keys:
                ops = _push_wait(f"scatter_wait_{l}_{names[0]}", st, after, False)
                for i, k in enumerate(names):
                    own[l, k], recv[l, k] = ops[i], ops[len(names) + i]

    def to_sibling(keys):
        sums = [_sum_sources(me, [recv[l, k] for l in range(DEPTH)], [own[l, k] for l in range(DEPTH)]) for k in keys]
        return _push_start(f"swap_start_{keys[0]}", sums, "sibling")

    def apply(keys, started, after):
        ops = _push_wait(f"swap_wait_{keys[0]}", started, after, "sibling")
        for i, k in enumerate(keys):
            mine, other = ops[i], ops[len(keys) + i]
            shp = shard[k].shape
            r = _adamw([mine, other], *((tr(t[k]) if k == "w_in" else t[k]).reshape(-1, shp[-1]) for t in (w, mom, var)))
            r = [t.reshape(shp) for t in r]
            out[k] = [tr(t) for t in r] if k == "w_in" else r
        return out[keys[-1]][1]

    collect(("w_ff1", "w_ff2", "w_out", "s5_w_glu"), dx)
    ff = to_sibling(("w_ff1", "w_ff2"))
    mix = to_sibling(("w_out", "s5_w_glu"))
    smalls[0]["db1"] = smalls[0]["db1"] + (ff[-1][0, 0] + mix[-1][0, 0])
    native = _allreduce_small([[smalls[l][k] for k in NATIVE] for l in range(DEPTH)])
    native = dict(zip(NATIVE, native))
    loss = native["loss"][0, 0, 0] + native["loss"][1, 0, 0]
    gsmall = _finish_small(native, w)
    view = lambda k, t: t.transpose(0, 1, 2, 4, 3) if k in ("s5_b_re", "s5_b_im") else t
    res = _adamw_small(*([view(k, t[k]) for k in SMALL] for t in (gsmall, w, mom, var)))
    for i, k in enumerate(SMALL):
        out[k] = [gsmall[k]] + [view(k, r[i]) for r in res]
    last = apply(("w_ff1", "w_ff2"), ff, res[0][-1])
    collect(("w_in",), last)
    win = to_sibling(("w_in",))
    last = apply(("w_out", "s5_w_glu"), mix, win[-1])
    apply(("w_in",), win, last)

    return (loss, dx.reshape(NSEQ, L, D), *[out[k][0] for k in _WEIGHTS], *[out[k][1] for k in _WEIGHTS],
            *[out[k][2] for k in _WEIGHTS], *[out[k][3] for k in _WEIGHTS])
```

```python
import functools
import math

import jax
import jax.numpy as jnp
from jax import lax
from jax.experimental import pallas as pl
from jax.experimental.pallas import tpu as pltpu

F32 = jnp.float32
MX = jnp.bfloat16
MESH = pl.DeviceIdType.MESH

DEPTH = 2
NSEQ = 2
L = 2048
N = NSEQ * L
D = 1024
DFF = 4096
NSHARD = 4
S5_G, S5_H, S5_P = 16, 16, 64
GLA_CHUNK = 64
NCHUNK = L // GLA_CHUNK
GLA_GROUP = 4
NGROUP = NCHUNK // GLA_GROUP
SWA_BLK = 128
NBLK = L // SWA_BLK
ROT = 16
ROPE_THETA = 500000.0
LN_EPS = 1e-5
ALPHA = (2 * DEPTH) ** 0.25
NEG_BIG = -1e30
DIN = 1824
DINP = 1920
ADAM_LR, ADAM_B1, ADAM_B2, ADAM_EPS, ADAM_WD, ADAM_STEP = 0.001, 0.9, 0.999, 1e-08, 0.01, 10
VMEM_LIMIT = 56 * 1024 * 1024
TT = 512
SW = 512
FFN_TM = 512
FFN_TM_W = 1024
FFN_WB = 1
FFN_VMEM = 60 * 1024 * 1024
INPROJ_BWD_TM = 512


def _cp(sem, vmem=VMEM_LIMIT):
    return pltpu.CompilerParams(dimension_semantics=sem, vmem_limit_bytes=vmem)


def _mm(a, b):
    return jnp.dot(a.astype(MX), b.astype(MX), preferred_element_type=F32)


def _mm_nt(a, b):
    return lax.dot_general(a.astype(MX), b.astype(MX), (((1,), (1,)), ((), ())), preferred_element_type=F32)


def _mm_tn(a, b):
    return lax.dot_general(a.astype(MX), b.astype(MX), (((0,), (0,)), ((), ())), preferred_element_type=F32)


@jax.custom_vjp
def _dmm(a, b):
    return _mm(a, b)


_dmm.defvjp(lambda a, b: (_mm(a, b), (a, b)), lambda r, g: (_mm_nt(g, r[1]), _mm_tn(r[0], g)))


@jax.custom_vjp
def _dmm_nt(a, b):
    return _mm_nt(a, b)


_dmm_nt.defvjp(lambda a, b: (_mm_nt(a, b), (a, b)), lambda r, g: (_mm(g, r[1]), _mm_tn(g, r[0])))


@jax.custom_vjp
def _dmm_tn(a, b):
    return _mm_tn(a, b)


_dmm_tn.defvjp(lambda a, b: (_mm_tn(a, b), (a, b)), lambda r, g: (_mm_nt(r[1], g), _mm(r[0], g)))


def _split3(x):
    hi = x.astype(MX)
    r1 = x - hi.astype(F32)
    mid = r1.astype(MX)
    lo = (r1 - mid.astype(F32)).astype(MX)
    return hi, mid, lo


def _chunk_pairs(rows, rev, strict):
    r = lax.broadcasted_iota(jnp.int32, (rows, rows), 0)
    c = lax.broadcasted_iota(jnp.int32, (rows, rows), 1)
    order = ((c > r) if strict else (c >= r)) if rev else ((c < r) if strict else (c <= r))
    return (r // GLA_CHUNK == c // GLA_CHUNK) & order


def _cums_impl(x, rev):
    rows, w = x.shape
    t = jnp.where(_chunk_pairs(rows, rev, False), 1.0, 0.0).astype(MX)
    s = jnp.dot(t, jnp.concatenate(_split3(x), axis=1), preferred_element_type=F32)
    return s[:, 0:w] + s[:, w:2 * w] + s[:, 2 * w:3 * w]


@functools.partial(jax.custom_vjp, nondiff_argnums=(1,))
def _cums(x, rev):
    return _cums_impl(x, rev)


_cums.defvjp(lambda x, rev: (_cums_impl(x, rev), None), lambda rev, r, g: (_cums_impl(g, not rev),))


def _ln_fwd(s, g, b):
    mu = jnp.mean(s, axis=-1, keepdims=True)
    xc = s - mu
    var = jnp.mean(xc * xc, axis=-1, keepdims=True)
    return xc * lax.rsqrt(var + LN_EPS) * g + b


def _ln_bwd(dy, s, g):
    mu = jnp.mean(s, axis=-1, keepdims=True)
    xc = s - mu
    var = jnp.mean(xc * xc, axis=-1, keepdims=True)
    rstd = lax.rsqrt(var + LN_EPS)
    xhat = xc * rstd
    dxh = dy * g
    ds = rstd * (dxh - jnp.mean(dxh, axis=-1, keepdims=True) - xhat * jnp.mean(dxh * xhat, axis=-1, keepdims=True))
    return ds, jnp.sum(dy * xhat, axis=0, keepdims=True), jnp.sum(dy, axis=0, keepdims=True)


def _sds(shape, dtype=F32):
    return jax.ShapeDtypeStruct(shape, dtype)


_IN_ROW_PIECES = (((0, 0), (0, 456)), ((1, 0), (456, 456)), ((2, 0), (912, 112)), ((2, 112), (1792, 32)),
                  ((2, 144), (1024, 312)), ((3, 0), (1336, 456)))


def _in_rows(g4, behind):
    def body(g_ref, behind_ref, o_ref, tmp):
        tmp[DIN:DINP] = jnp.zeros((DINP - DIN, D), F32)
        for (j, s0), (d0, n_) in _IN_ROW_PIECES:
            tmp[d0:d0 + n_] = g_ref[j, s0:s0 + n_].astype(F32)
        o_ref[...] = tmp[...].astype(MX)

    vm = pl.BlockSpec(memory_space=pltpu.VMEM)
    return pl.pallas_call(body, in_specs=[vm, pl.BlockSpec(memory_space=pl.ANY)], out_specs=vm,
                          out_shape=_sds((DINP, D), MX), scratch_shapes=[pltpu.VMEM((DINP, D), F32)], name="in_rows",
                          compiler_params=pltpu.CompilerParams(vmem_limit_bytes=VMEM_LIMIT))(g4, behind)


def _inproj_fwd(x, wt):
    tm = 512

    def body(x_ref, w_ref, h_ref):
        h_ref[...] = _mm_nt(x_ref[...], w_ref[...])

    return pl.pallas_call(
        body, grid=(N // tm,),
        in_specs=[pl.BlockSpec((tm, D), lambda i: (i, 0)), pl.BlockSpec((DINP, D), lambda i: (0, 0))],
        out_specs=pl.BlockSpec((tm, DINP), lambda i: (i, 0)),
        out_shape=_sds((N, DINP)), name="inproj_fwd", compiler_params=_cp(("parallel",)))(x, wt)


def _inproj_bwd(x, w, dxp, du2, dud, gq_f, gq_b, gk_f, gk_b, gv_f, gv_b, gr, daq, dakv, dhl):
    tm = INPROJ_BWD_TM
    nt = N // tm

    def body(x_ref, w_ref, dxp_ref, du2_ref, dud_ref, gqf, gqb, gkf, gkb, gvf, gvb, gr_ref, daq_ref, dakv_ref, dhl_ref,
             dx_ref, dw_ref, acc):
        i = pl.program_id(0)
        f = lambda r: r[...].astype(F32)
        dh = jnp.concatenate([
            du2_ref[0] + du2_ref[1] + f(dud_ref), f(gqf) + f(gqb), f(gkf) + f(gkb), f(gvf) + f(gvb),
            f(gr_ref), f(daq_ref), f(dakv_ref), f(dhl_ref)], axis=1)
        dx_ref[...] = dxp_ref[...] + _mm(dh, w_ref[...])
        contrib = _mm_tn(dh, x_ref[...])

        @pl.when(i == 0)
        def _():
            acc[...] = contrib

        @pl.when(i > 0)
        def _():
            acc[...] += contrib

        @pl.when(i == nt - 1)
        def _():
            for (j, d0), (s0, n_) in _IN_ROW_PIECES:
                dw_ref[j, d0:d0 + n_] = acc[s0:s0 + n_].astype(MX)

    row = lambda w_: pl.BlockSpec((tm, w_), lambda i: (i, 0))
    return pl.pallas_call(
        body, grid=(nt,),
        in_specs=[row(D), pl.BlockSpec((DINP, D), lambda i: (0, 0)), row(D),
                  pl.BlockSpec((2, tm, 256), lambda i: (0, i, 0)), row(256), row(128), row(128), row(128), row(128),
                  row(256), row(256), row(256), row(512), row(256), row(128)],
        out_specs=[row(D), pl.BlockSpec((NSHARD, DIN // NSHARD, D), lambda i: (0, 0, 0))],
        out_shape=[_sds((N, D)), _sds((NSHARD, DIN // NSHARD, D), MX)],
        scratch_shapes=[pltpu.VMEM((DINP, D), F32)],
        name="inproj_bwd", compiler_params=_cp(("arbitrary",)))(
            x, w, dxp, du2, dud, gq_f, gq_b, gk_f, gk_b, gv_f, gv_b, gr, daq, dakv, dhl)


def _scan_tables(mr, mi, reverse):
    pw = [(mr, mi)]
    for _ in range(7):
        pr, pi = pw[-1]
        pw.append((pr * mr - pi * mi, pr * mi + pi * mr))
    rows = jnp.arange(8)[:, None]
    out = []
    for d in (1, 2, 4):
        keep = rows >= d
        out += [jnp.where(keep, pw[d - 1][0][None], 0.0), jnp.where(keep, pw[d - 1][1][None], 0.0)]
    out += [jnp.stack([p[0] for p in pw]), jnp.stack([p[1] for p in pw])]
    t = jnp.stack(out)
    if reverse:
        t = t[:, ::-1, :]
    return t.reshape(8, 8, 2, SW).transpose(2, 0, 1, 3)


def _tile_scan(xr, xi, a, cr, ci, reverse):
    for lvl, d in enumerate((1, 2, 4)):
        sh = 8 - d if reverse else d
        sr = pltpu.roll(xr, sh, 0)
        si = pltpu.roll(xi, sh, 0)
        ar, ai = a[2 * lvl], a[2 * lvl + 1]
        xr, xi = xr + ar * sr - ai * si, xi + ar * si + ai * sr
    pr, pi = a[6], a[7]
    return xr + pr * cr - pi * ci, xi + pr * ci + pi * cr


NJ = TT // 8


def _lockstep_tables(mr, mi, reverse):
    nr, ni = mr, mi
    for _ in range(NJ.bit_length() - 1):
        nr, ni = nr * nr - ni * ni, 2.0 * nr * ni
    pr, pi = mr[None], mi[None]
    while pr.shape[0] < NJ:
        k = pr.shape[0]
        tr, ti = pr[k - 1], pi[k - 1]
        pr, pi = (jnp.concatenate([pr, pr * tr - pi * ti]), jnp.concatenate([pi, pr * ti + pi * tr]))
    if reverse:
        pr, pi = pr[::-1], pi[::-1]
    rows = jnp.broadcast_to(jnp.stack([mr, mi])[:, None, :], (2, 8, 2 * SW))
    link = _scan_tables(nr, ni, reverse)
    a = jnp.concatenate([rows.reshape(2, 8, 2, SW).transpose(2, 0, 1, 3), link], axis=1)
    return a, jnp.stack([pr, pi]).reshape(2, NJ, 2, SW).transpose(2, 0, 1, 3)


def _to_lockstep(ref, *lead):
    return jnp.concatenate([ref[(*lead, pl.ds(j, 8, stride=NJ), slice(None))] for j in range(NJ)], axis=0)


def _from_lockstep(val, ref, *lead):
    for j in range(NJ):
        ref[(*lead, pl.ds(j, 8, stride=NJ), slice(None))] = val[8 * j:8 * j + 8]


def _expand_powers(p_ref, pexp):
    for c in range(2):
        for j in range(NJ):
            pexp[c, j] = jnp.broadcast_to(p_ref[0, 0, c, j:j + 1, :], (8, SW))


def _lockstep_scan(xre, xim, a_ref, pexp, car, reverse, extra=None):
    a = [a_ref[0, 0, k] for k in range(10)]
    mr, mi = a[0], a[1]
    order = (lambda i: NJ - 1 - i) if reverse else (lambda i: i)

    def local(i, hcar):
        hr, hi = hcar
        r0 = pl.multiple_of(order(i) * 8, 8)
        hr, hi = mr * hr - mi * hi + xre[pl.ds(r0, 8), :], mr * hi + mi * hr + xim[pl.ds(r0, 8), :]
        xre[pl.ds(r0, 8), :] = hr
        xim[pl.ds(r0, 8), :] = hi
        return hr, hi

    z8 = jnp.zeros((8, SW), F32)
    er, ei = lax.fori_loop(0, NJ, local, (z8, z8), unroll=4)
    c0r, c0i = car[0], car[1]
    er, ei = _tile_scan(er, ei, a[2:], c0r, c0i, reverse)
    rowid = lax.broadcasted_iota(jnp.int32, (8, SW), 0)
    first, sh, last = (7, 7, 0) if reverse else (0, 1, 7)
    cvr = jnp.where(rowid == first, c0r, pltpu.roll(er, sh, 0))
    cvi = jnp.where(rowid == first, c0i, pltpu.roll(ei, sh, 0))
    car[0] = jnp.broadcast_to(er[last:last + 1, :], (8, SW))
    car[1] = jnp.broadcast_to(ei[last:last + 1, :], (8, SW))

    def fix(i, carry):
        j = order(i)
        r0 = pl.multiple_of(j * 8, 8)
        pr, pi = pexp[0, j], pexp[1, j]
        sr = xre[pl.ds(r0, 8), :] + pr * cvr - pi * cvi
        si = xim[pl.ds(r0, 8), :] + pr * cvi + pi * cvr
        xre[pl.ds(r0, 8), :] = sr
        xim[pl.ds(r0, 8), :] = si
        if extra is None:
            return carry
        return (sr, si, extra(r0, sr, si, carry[0], carry[1], carry[2]))

    init = (cvr, cvi, extra(None, None, None, None, None, None)) if extra is not None else 0
    return lax.fori_loop(0, NJ, fix, init, unroll=4)


def _s5_time_block(z, s, t, adjoint):
    flip = (1 - z) if adjoint else z
    return s * (L // TT) + t + flip * (L // TT - 1 - 2 * t)


def _s5_fwd(h, bre, bim, cre, cim, tab):
    nt = L // TT
    taba, tabp = tab

    def body(u_ref, bre_ref, bim_ref, cre_ref, cim_ref, a_ref, p_ref, hre_ref, him_ref, y_ref, car, pexp):
        z = pl.program_id(1)
        s = pl.program_id(2)
        tc = pl.program_id(3)

        @pl.when(tc == 0)
        def _():
            car[...] = jnp.zeros_like(car)

        @pl.when((tc == 0) & (s == 0))
        def _():
            _expand_powers(p_ref, pexp)

        u = _to_lockstep(u_ref)
        hre_ref[0] = _mm(u, bre_ref[0, 0])
        him_ref[0] = _mm(u, bim_ref[0, 0])

        @pl.when(z == 0)
        def _():
            _lockstep_scan(hre_ref.at[0], him_ref.at[0], a_ref, pexp, car, False)

        @pl.when(z == 1)
        def _():
            _lockstep_scan(hre_ref.at[0], him_ref.at[0], a_ref, pexp, car, True)

        _from_lockstep(_mm(hre_ref[0], cre_ref[0, 0]) - _mm(him_ref[0], cim_ref[0, 0]), y_ref, 0)

    tb = lambda b, z, s, t: _s5_time_block(z, s, t, False)
    wspec = lambda r, c: pl.BlockSpec((1, 1, r, c), lambda b, z, s, t: (z, b, 0, 0))
    return pl.pallas_call(
        body, grid=(2, 2, NSEQ, nt),
        in_specs=[pl.BlockSpec((TT, 128), lambda b, z, s, t: (tb(b, z, s, t), b)),
                  wspec(128, SW), wspec(128, SW), wspec(SW, 128), wspec(SW, 128),
                  pl.BlockSpec((1, 1, 10, 8, SW), lambda b, z, s, t: (z, b, 0, 0, 0)),
                  pl.BlockSpec((1, 1, 2, NJ, SW), lambda b, z, s, t: (z, b, 0, 0, 0))],
        out_specs=[pl.BlockSpec((1, TT, SW), lambda b, z, s, t: (z, tb(b, z, s, t), b)),
                   pl.BlockSpec((1, TT, SW), lambda b, z, s, t: (z, tb(b, z, s, t), b)),
                   pl.BlockSpec((1, TT, 128), lambda b, z, s, t: (z, tb(b, z, s, t), b))],
        out_shape=[_sds((2, N, 2 * SW)), _sds((2, N, 2 * SW)), _sds((2, N, 256))],
        scratch_shapes=[pltpu.VMEM((2, 8, SW), F32), pltpu.VMEM((2, NJ, 8, SW), F32)],
        name="s5_fwd", compiler_params=_cp(("arbitrary",) * 4))(h, bre, bim, cre, cim, taba, tabp)


def _s5_bwd(h, dyp, hre, him, bre, bim, cre, cim, tabc):
    nt = L // TT
    taba, tabp = tabc

    def body(u_ref, dy_ref, hre_ref, him_ref, bre_ref, bim_ref, cre_ref, cim_ref, a_ref, p_ref,
             du_ref, dbre_ref, dbim_ref, dcre_ref, dcim_ref, dmu_ref, gre, gim, car, acc, macc, pexp):
        z = pl.program_id(1)
        s = pl.program_id(2)
        tc = pl.program_id(3)

        @pl.when(tc == 0)
        def _():
            car[...] = jnp.zeros_like(car)

        @pl.when((tc == 0) & (s == 0))
        def _():
            acc[...] = jnp.zeros_like(acc)
            macc[...] = jnp.zeros_like(macc)
            _expand_powers(p_ref, pexp)

        dy = _to_lockstep(dy_ref)
        gre[...] = _mm_nt(dy, cre_ref[0, 0])
        gim[...] = -_mm_nt(dy, cim_ref[0, 0])

        def run(reverse):
            def pair(r0, gr_, gi_, pvr, pvi, m):
                if r0 is None:
                    return (macc[0], macc[1])
                hr = hre_ref[0, pl.ds(r0, 8), :]
                hi = him_ref[0, pl.ds(r0, 8), :]
                return (m[0] + pvr * hr + pvi * hi, m[1] + pvi * hr - pvr * hi)

            _, _, (dmr, dmi) = _lockstep_scan(gre, gim, a_ref, pexp, car, reverse, pair)
            macc[0] = dmr
            macc[1] = dmi

        @pl.when(z == 0)
        def _():
            run(True)

        @pl.when(z == 1)
        def _():
            run(False)

        gr = gre[...]
        gi = gim[...]
        u = _to_lockstep(u_ref)
        _from_lockstep(_mm_nt(gr, bre_ref[0, 0]) + _mm_nt(gi, bim_ref[0, 0]), du_ref, 0)
        acc[0] += _mm_tn(u, gr)
        acc[1] += _mm_tn(u, gi)
        acc[2] += _mm_tn(dy, hre_ref[0])
        acc[3] -= _mm_tn(dy, him_ref[0])

        @pl.when((tc == nt - 1) & (s == NSEQ - 1))
        def _():
            grp = lax.broadcasted_iota(jnp.int32, (S5_H, SW), 1) // S5_P
            for k, out in enumerate((dbre_ref, dbim_ref, dcre_ref, dcim_ref)):
                c = jnp.zeros((S5_H, SW), F32)
                for i in range(8):
                    c = c + jnp.where(grp == i, acc[k, i * S5_H:(i + 1) * S5_H, :], 0.0)
                out[0, 0] = c
            dmu_ref[0, 0] = jnp.concatenate([jnp.sum(macc[0], axis=0, keepdims=True),
                                             jnp.sum(macc[1], axis=0, keepdims=True)], axis=0)

    tb = lambda b, z, s, t: _s5_time_block(z, s, t, True)
    wspec = lambda r, c: pl.BlockSpec((1, 1, r, c), lambda b, z, s, t: (z, b, 0, 0))
    tok = lambda w_: pl.BlockSpec((TT, w_), lambda b, z, s, t: (tb(b, z, s, t), b))
    st = pl.BlockSpec((1, TT, SW), lambda b, z, s, t: (z, tb(b, z, s, t), b))
    return pl.pallas_call(
        body, grid=(2, 2, NSEQ, nt),
        in_specs=[tok(128), tok(128), st, st, wspec(128, SW), wspec(128, SW), wspec(SW, 128), wspec(SW, 128),
                  pl.BlockSpec((1, 1, 10, 8, SW), lambda b, z, s, t: (z, b, 0, 0, 0)),
                  pl.BlockSpec((1, 1, 2, NJ, SW), lambda b, z, s, t: (z, b, 0, 0, 0))],
        out_specs=[pl.BlockSpec((1, TT, 128), lambda b, z, s, t: (z, tb(b, z, s, t), b)),
                   wspec(S5_H, SW), wspec(S5_H, SW), wspec(S5_H, SW), wspec(S5_H, SW),
                   wspec(2, SW)],
        out_shape=[_sds((2, N, 256))] + [_sds((2, 2, S5_H, SW))] * 4 + [_sds((2, 2, 2, SW))],
        scratch_shapes=[pltpu.VMEM((TT, SW), F32), pltpu.VMEM((TT, SW), F32), pltpu.VMEM((2, 8, SW), F32),
                        pltpu.VMEM((4, 128, SW), F32), pltpu.VMEM((2, 8, SW), F32), pltpu.VMEM((2, NJ, 8, SW), F32)],
        name="s5_bwd", compiler_params=_cp(("arbitrary",) * 4))(h, dyp, hre, him, bre, bim, cre, cim, taba, tabp)


_GELU_C = math.sqrt(2.0 / math.pi)


def _gelu(y):
    return 0.5 * y * (1.0 + jnp.tanh(_GELU_C * (y + 0.044715 * y * y * y)))


def _gelu_grad(y):
    t = jnp.tanh(_GELU_C * (y + 0.044715 * y * y * y))
    return 0.5 * (1.0 + t) + 0.5 * y * (1.0 - t * t) * _GELU_C * (1.0 + 3 * 0.044715 * y * y)


def _glu_halves(w4_ref):
    return (jnp.concatenate([w4_ref[0], w4_ref[1]], axis=1), jnp.concatenate([w4_ref[2], w4_ref[3]], axis=1))


def _s5_glu_fwd(y2, h, dsk, w4, bv, bg):
    tm = 512

    def body(y2_ref, u_ref, d_ref, w4_ref, bv_ref, bg_ref, ya_ref):
        wv, wg = _glu_halves(w4_ref)
        z = _gelu(y2_ref[0] + y2_ref[1] + d_ref[...] * u_ref[...])
        val = _mm(z, wv) + bv_ref[...]
        gate = _mm(z, wg) + bg_ref[...]
        ya_ref[...] = (val * jax.nn.sigmoid(gate)).astype(MX)

    full = lambda r, c: pl.BlockSpec((r, c), lambda i: (0, 0))
    return pl.pallas_call(
        body, grid=(N // tm,),
        in_specs=[pl.BlockSpec((2, tm, 256), lambda i: (0, i, 0)), pl.BlockSpec((tm, 256), lambda i: (i, 0)),
                  full(1, 256), pl.BlockSpec((NSHARD, 256, 128), lambda i: (0, 0, 0)), full(1, 256), full(1, 256)],
        out_specs=pl.BlockSpec((tm, 256), lambda i: (i, 0)),
        out_shape=_sds((N, 256), MX), name="s5_glu_fwd", compiler_params=_cp(("parallel",)))(y2, h, dsk, w4, bv, bg)


def _s5_glu_bwd(y2, h, dsk, w4, bv, bg, dya):
    tm = 512
    nt = N // tm

    def body(y2_ref, u_ref, d_ref, w4_ref, bv_ref, bg_ref, dya_ref,
             dyp_ref, dud_ref, dd_ref, dw4_ref, dbv_ref, dbg_ref, accv, accg):
        i = pl.program_id(0)

        @pl.when(i == 0)
        def _():
            for r in (dd_ref, accv, accg, dbv_ref, dbg_ref):
                r[...] = jnp.zeros_like(r)

        wv, wg = _glu_halves(w4_ref)
        u = u_ref[...]
        y = y2_ref[0] + y2_ref[1] + d_ref[...] * u
        z = _gelu(y)
        val = _mm(z, wv) + bv_ref[...]
        sig = jax.nn.sigmoid(_mm(z, wg) + bg_ref[...])
        dya = dya_ref[...]
        dval = dya * sig
        dgate = dya * val * sig * (1.0 - sig)
        dz = _mm_nt(dval, wv) + _mm_nt(dgate, wg)
        dy = dz * _gelu_grad(y)
        dyp_ref[...] = dy
        dud_ref[...] = (dy * d_ref[...]).astype(MX)
        dd_ref[...] += jnp.sum(dy * u, axis=0, keepdims=True)
        accv[...] += _mm_tn(z, dval)
        accg[...] += _mm_tn(z, dgate)
        dbv_ref[...] += jnp.sum(dval, axis=0, keepdims=True)
        dbg_ref[...] += jnp.sum(dgate, axis=0, keepdims=True)

        @pl.when(i == nt - 1)
        def _():
            dw4_ref[0] = accv[:, 0:128].astype(MX)
            dw4_ref[1] = accv[:, 128:256].astype(MX)
            dw4_ref[2] = accg[:, 0:128].astype(MX)
            dw4_ref[3] = accg[:, 128:256].astype(MX)

    full = lambda r, c: pl.BlockSpec((r, c), lambda i: (0, 0))
    row = pl.BlockSpec((tm, 256), lambda i: (i, 0))
    wspec = pl.BlockSpec((NSHARD, 256, 128), lambda i: (0, 0, 0))
    return pl.pallas_call(
        body, grid=(nt,),
        in_specs=[pl.BlockSpec((2, tm, 256), lambda i: (0, i, 0)), row, full(1, 256), wspec, full(1, 256), full(1, 256),
                  row],
        out_specs=[row, row, full(1, 256), wspec, full(1, 256), full(1, 256)],
        out_shape=[_sds((N, 256)), _sds((N, 256), MX), _sds((1, 256)), _sds((NSHARD, 256, 128), MX), _sds((1, 256)),
                   _sds((1, 256))],
        scratch_shapes=[pltpu.VMEM((256, 256), F32), pltpu.VMEM((256, 256), F32)],
        name="s5_glu_bwd", compiler_params=_cp(("arbitrary",)))(y2, h, dsk, w4, bv, bg, dya)


def _logsig(x):
    return jnp.minimum(x, 0.0) - jnp.log(1.0 + jnp.exp(-jnp.abs(x)))


def _gla_gate_fwd(h, wa, ba):
    tm = 512

    def body(hl_ref, wa_ref, ba_ref, la_ref):
        la_ref[...] = _logsig(_mm(hl_ref[...], wa_ref[...]) + ba_ref[...]) * (1.0 / 16.0)

    return pl.pallas_call(
        body, grid=(N // tm,),
        in_specs=[pl.BlockSpec((tm, 128), lambda i: (i, 14)), pl.BlockSpec((128, 256), lambda i: (0, 0)),
                  pl.BlockSpec((1, 256), lambda i: (0, 0))],
        out_specs=pl.BlockSpec((tm, 256), lambda i: (i, 0)),
        out_shape=_sds((N, 256)), name="gla_gate_fwd", compiler_params=_cp(("parallel",)))(h, wa, ba)


def _gla_gate_bwd(h, wa, ba, dla_f, dla_b):
    tm = 512

    def body(hl_ref, wa_ref, ba_ref, df_ref, db_ref, dhl_ref, dwa_ref, dba_ref):
        i = pl.program_id(0)

        @pl.when(i == 0)
        def _():
            dwa_ref[...] = jnp.zeros_like(dwa_ref)
            dba_ref[...] = jnp.zeros_like(dba_ref)

        hl = hl_ref[...]
        pre = _mm(hl, wa_ref[...]) + ba_ref[...]
        dpre = jnp.concatenate([df_ref[...], db_ref[...]], axis=1) * (1.0 / 16.0) * jax.nn.sigmoid(-pre)
        dhl_ref[...] = _mm_nt(dpre, wa_ref[...]).astype(MX)
        dwa_ref[...] += _mm_tn(hl, dpre)[0:32]
        dba_ref[...] += jnp.sum(dpre, axis=0, keepdims=True)

    row = pl.BlockSpec((tm, 128), lambda i: (i, 0))
    return pl.pallas_call(
        body, grid=(N // tm,),
        in_specs=[pl.BlockSpec((tm, 128), lambda i: (i, 14)), pl.BlockSpec((128, 256), lambda i: (0, 0)),
                  pl.BlockSpec((1, 256), lambda i: (0, 0)), row, row],
        out_specs=[row, pl.BlockSpec((32, 256), lambda i: (0, 0)), pl.BlockSpec((1, 256), lambda i: (0, 0))],
        out_shape=[_sds((N, 128), MX), _sds((32, 256)), _sds((1, 256))],
        name="gla_gate_bwd", compiler_params=_cp(("arbitrary",)))(h, wa, ba, dla_f, dla_b)


def _gla_chunk(q, k, v, la, st, rev):
    c = GLA_CHUNK
    rows = q.shape[0]
    nch = rows // c
    b = _cums(la, rev)
    blc = [jnp.sum(la[i * c:(i + 1) * c], axis=0, keepdims=True) for i in range(nch)]
    bl = jnp.concatenate([jnp.broadcast_to(t, (c, 128)) for t in blc], axis=0)
    q_in = q * (32.0 ** -0.5) * jnp.exp(b)
    k_in = k * jnp.exp(-b)
    k_st = k * jnp.exp(bl - b)
    lane_k = lax.broadcasted_iota(jnp.int32, (1, 128), 1) // 32
    lane_v = lax.broadcasted_iota(jnp.int32, (1, 256), 1) // 64
    qs = jnp.concatenate([jnp.where(lane_k == hd, q_in, 0.0) for hd in range(4)], axis=0)
    a = _dmm_nt(qs, k_in)
    a = jnp.where(jnp.concatenate([_chunk_pairs(rows, rev, rev)] * 4, axis=0), a, 0.0)
    o4 = _dmm(a, v)
    o = jnp.zeros((rows, 256), F32)
    for hd in range(4):
        o = o + jnp.where(lane_v == hd, o4[hd * rows:(hd + 1) * rows], 0.0)
    bd = (lax.broadcasted_iota(jnp.int32, (256, 128), 0) // 64) == (lax.broadcasted_iota(jnp.int32, (256, 128), 1) // 32)
    inter = [None] * nch
    for i in (reversed(range(nch)) if rev else range(nch)):
        sl = slice(i * c, (i + 1) * c)
        inter[i] = _dmm_nt(q_in[sl], st)
        st = jnp.exp(blc[i]) * st + jnp.where(bd, _dmm_tn(v[sl], k_st[sl]), 0.0)
    return o + jnp.concatenate(inter, axis=0), st


def _gla_chunk_of(c, rev):
    return NGROUP - 1 - c if rev else c


def _gla_fwd(h, la2):
    c = GLA_GROUP * GLA_CHUNK

    def body(qf, kf, vf, laf, qb, kb, vb, lab, of_ref, ob_ref, sf_ref, sb_ref, stf, stb):
        @pl.when(pl.program_id(0) == 0)
        def _():
            stf[...] = jnp.zeros_like(stf)
            stb[...] = jnp.zeros_like(stb)

        ins = [(qf[s], kf[s], vf[s], laf[s], stf[s], qb[s], kb[s], vb[s], lab[s], stb[s]) for s in range(NSEQ)]
        outs = [(_gla_chunk(*t[:5], False), _gla_chunk(*t[5:], True)) for t in ins]
        for s in range(NSEQ):
            sf_ref[s, 0] = ins[s][4]
            sb_ref[s, 0] = ins[s][9]
            (of_ref[s], stf[s]), (ob_ref[s], stb[s]) = outs[s]

    def specs(rev):
        ch = lambda i: _gla_chunk_of(i, rev)
        return [pl.BlockSpec((NSEQ, c, 128), lambda i: (0, ch(i), 2)), pl.BlockSpec((NSEQ, c, 128), lambda i: (0, ch(i), 3)),
                pl.BlockSpec((NSEQ, c, 256), lambda i: (0, ch(i), 2)),
                pl.BlockSpec((NSEQ, c, 128), lambda i: (0, ch(i), 1 if rev else 0))]

    orow = lambda rev: pl.BlockSpec((NSEQ, c, 256), lambda i: (0, _gla_chunk_of(i, rev), 0))
    srow = lambda rev: pl.BlockSpec((NSEQ, 1, 256, 128), lambda i: (0, _gla_chunk_of(i, rev), 0, 0))
    h3, la3 = h.reshape(NSEQ, L, DINP), la2.reshape(NSEQ, L, 256)
    of, ob, sf, sb = pl.pallas_call(
        body, grid=(NGROUP,),
        in_specs=specs(False) + specs(True),
        out_specs=[orow(False), orow(True), srow(False), srow(True)],
        out_shape=[_sds((NSEQ, L, 256)), _sds((NSEQ, L, 256)), _sds((NSEQ, NGROUP, 256, 128)),
                   _sds((NSEQ, NGROUP, 256, 128))],
        scratch_shapes=[pltpu.VMEM((NSEQ, 256, 128), F32), pltpu.VMEM((NSEQ, 256, 128), F32)],
        name="gla_fwd", compiler_params=_cp(("arbitrary",)))(h3, h3, h3, la3, h3, h3, h3, la3)
    return of.reshape(N, 256), ob.reshape(N, 256), sf, sb


def _gla_bwd(h, la2, do, sf, sb):
    c = GLA_GROUP * GLA_CHUNK

    def body(qf, kf, vf, laf, dof, sfr, qb, kb, vb, lab, dob, sbr,
             dqf, dkf, dvf, dlf, dqb, dkb, dvb, dlb, dstf, dstb):
        @pl.when(pl.program_id(0) == 0)
        def _():
            dstf[...] = jnp.zeros_like(dstf)
            dstb[...] = jnp.zeros_like(dstb)

        def one(s, q, k, v, la, do_, st, dst, rev):
            _, vjp = jax.vjp(functools.partial(_gla_chunk, rev=rev), q[s], k[s], v[s], la[s], st[s, 0])
            return vjp((do_[s], dst[s]))

        res = [(one(s, qf, kf, vf, laf, dof, sfr, dstf, False), one(s, qb, kb, vb, lab, dob, sbr, dstb, True))
               for s in range(NSEQ)]
        for s in range(NSEQ):
            for (gq, gk, gv, gl, gs), (dq, dk, dv, dl, dst) in ((res[s][0], (dqf, dkf, dvf, dlf, dstf)),
                                                                  (res[s][1], (dqb, dkb, dvb, dlb, dstb))):
                dq[s], dk[s], dv[s] = gq.astype(MX), gk.astype(MX), gv.astype(MX)
                dl[s], dst[s] = gl, gs

    def specs(rev):
        ch = lambda i: _gla_chunk_of(i, not rev)
        return [pl.BlockSpec((NSEQ, c, 128), lambda i: (0, ch(i), 2)), pl.BlockSpec((NSEQ, c, 128), lambda i: (0, ch(i), 3)),
                pl.BlockSpec((NSEQ, c, 256), lambda i: (0, ch(i), 2)),
                pl.BlockSpec((NSEQ, c, 128), lambda i: (0, ch(i), 1 if rev else 0)),
                pl.BlockSpec((NSEQ, c, 256), lambda i: (0, ch(i), 0)),
                pl.BlockSpec((NSEQ, 1, 256, 128), lambda i: (0, ch(i), 0, 0))]

    def ospecs(rev):
        ch = lambda i: _gla_chunk_of(i, not rev)
        n = pl.BlockSpec((NSEQ, c, 128), lambda i: (0, ch(i), 0))
        return [n, n, pl.BlockSpec((NSEQ, c, 256), lambda i: (0, ch(i), 0)), n]

    oshape = [_sds((NSEQ, L, 128), MX), _sds((NSEQ, L, 128), MX), _sds((NSEQ, L, 256), MX), _sds((NSEQ, L, 128))]
    h3, la3, do3 = h.reshape(NSEQ, L, DINP), la2.reshape(NSEQ, L, 256), do.reshape(NSEQ, L, 256)
    res = pl.pallas_call(
        body, grid=(NGROUP,),
        in_specs=specs(False) + specs(True),
        out_specs=ospecs(False) + ospecs(True),
        out_shape=oshape + oshape,
        scratch_shapes=[pltpu.VMEM((NSEQ, 256, 128), F32), pltpu.VMEM((NSEQ, 256, 128), F32)],
        name="gla_bwd", compiler_params=_cp(("arbitrary",)))(h3, h3, h3, la3, do3, sf, h3, h3, h3, la3, do3, sb)
    return [r.reshape(N, r.shape[-1]) for r in res]


def _gla_post(of, ob, r, g):
    o = of + ob
    head = lax.broadcasted_iota(jnp.int32, (1, 256), 1) // 64
    mu = jnp.zeros_like(o)
    for hd in range(4):
        mu = mu + jnp.where(head == hd, jnp.sum(jnp.where(head == hd, o, 0.0), axis=-1, keepdims=True) * (1.0 / 64.0), 0.0)
    xc = o - mu
    var = jnp.zeros_like(o)
    for hd in range(4):
        var = var + jnp.where(head == hd, jnp.sum(jnp.where(head == hd, xc * xc, 0.0), axis=-1, keepdims=True) * (1.0 / 64.0), 0.0)
    return xc * lax.rsqrt(var + LN_EPS) * g * (r * jax.nn.sigmoid(r))


def _gla_post_fwd(of, ob, h, g):
    tm = 512

    def body(of_ref, ob_ref, r_ref, g_ref, y_ref):
        y_ref[...] = _gla_post(of_ref[...], ob_ref[...], r_ref[...], g_ref[...]).astype(MX)

    row = pl.BlockSpec((tm, 256), lambda i: (i, 0))
    return pl.pallas_call(
        body, grid=(N // tm,),
        in_specs=[row, row, pl.BlockSpec((tm, 256), lambda i: (i, 3)), pl.BlockSpec((1, 256), lambda i: (0, 0))],
        out_specs=row, out_shape=_sds((N, 256), MX), name="gla_post_fwd", compiler_params=_cp(("parallel",)))(of, ob, h, g)


def _gla_post_bwd(of, ob, h, g, dyb):
    tm = 512

    def body(of_ref, ob_ref, r_ref, g_ref, dy_ref, do_ref, dr_ref, dg_ref):
        @pl.when(pl.program_id(0) == 0)
        def _():
            dg_ref[...] = jnp.zeros_like(dg_ref)

        _, vjp = jax.vjp(_gla_post, of_ref[...], ob_ref[...], r_ref[...], g_ref[...])
        go, _, gr, gg = vjp(dy_ref[...])
        do_ref[...] = go
        dr_ref[...] = gr.astype(MX)
        dg_ref[...] += gg

    row = pl.BlockSpec((tm, 256), lambda i: (i, 0))
    one = pl.BlockSpec((1, 256), lambda i: (0, 0))
    return pl.pallas_call(
        body, grid=(N // tm,),
        in_specs=[row, row, pl.BlockSpec((tm, 256), lambda i: (i, 3)), one, row],
        out_specs=[row, row, one], out_shape=[_sds((N, 256)), _sds((N, 256), MX), _sds((1, 256))],
        name="gla_post_bwd", compiler_params=_cp(("arbitrary",)))(of, ob, h, g, dyb)


def _rope_tables(width):
    pos = jnp.arange(L, dtype=F32)
    inv_freq = ROPE_THETA ** (-jnp.arange(0, ROT, 2, dtype=F32) / ROT)
    ang = pos[:, None] * inv_freq[None, :]
    cos, sin = jnp.cos(ang), jnp.sin(ang)
    one = jnp.ones((L, 64 - ROT), F32)
    zero = jnp.zeros((L, 64 - ROT), F32)
    z8 = jnp.zeros((L, ROT // 2), F32)
    c = jnp.concatenate([cos, cos, one], axis=1)
    sa = jnp.concatenate([z8, sin, zero], axis=1)
    sb = jnp.concatenate([-sin, z8, zero], axis=1)
    rep = width // 64
    return jnp.stack([jnp.tile(c, (1, rep)), jnp.tile(sa, (1, rep)), jnp.tile(sb, (1, rep))])


def _pieces(t, f):
    out = [f(t[:, c * 128:(c + 1) * 128]) for c in range(t.shape[-1] // 128)]
    return out[0] if len(out) == 1 else jnp.concatenate(out, axis=1)


def _rope(t, tab):
    return _pieces(t, lambda x: x * tab[0] + pltpu.roll(x, ROT // 2, 1) * tab[1] + pltpu.roll(x, 128 - ROT // 2, 1) * tab[2])


def _rope_t(g, tab):
    return _pieces(g, lambda x: x * tab[0] + pltpu.roll(x * tab[1], 128 - ROT // 2, 1) + pltpu.roll(x * tab[2], ROT // 2, 1))


def _swa_pad_kv(kv_ref, tk_ref, kexp, vexp):
    z = jnp.zeros((SWA_BLK, 256), F32)
    kr = _rope(kv_ref[:, 0:128], tk_ref[...])
    for hk in range(2):
        for pad in (kexp, vexp):
            pad[hk, 0:SWA_BLK] = z
            pad[hk, SWA_BLK + L:] = z
        kexp[hk, SWA_BLK:SWA_BLK + L] = _swa_expand(kr, hk)
        vexp[hk, SWA_BLK:SWA_BLK + L] = _swa_expand(kv_ref[:, 128:256], hk)


def _swa_expand(x, hk):
    lane = lax.broadcasted_iota(jnp.int32, x.shape, 1)
    sw = pltpu.roll(x, 64, 1)
    pair = jnp.where(lane < 64, x, sw) if hk == 0 else jnp.where(lane < 64, sw, x)
    return jnp.concatenate([pair, pair], axis=1)


def _swa_fold(x, hk):
    a = x[:, 0:128] + x[:, 128:256]
    t = a + pltpu.roll(a, 64, 1)
    lane = lax.broadcasted_iota(jnp.int32, a.shape, 1)
    return jnp.where((lane < 64) if hk == 0 else (lane >= 64), t, 0.0)


def _swa_probs(q2, kexp, n, sink_ref, hk):
    slot = lax.broadcasted_iota(jnp.int32, (1, 256), 1) // 64
    qs = jnp.concatenate([jnp.where(slot == g, q2, 0.0) for g in range(4)], axis=0)
    s = _mm_nt(qs, kexp) * 0.125
    i = lax.broadcasted_iota(jnp.int32, (SWA_BLK, 3 * SWA_BLK), 0)
    j = lax.broadcasted_iota(jnp.int32, (SWA_BLK, 3 * SWA_BLK), 1)
    kpos = n * SWA_BLK - SWA_BLK + j
    ok = (j - i >= 0) & (j - i <= 2 * SWA_BLK) & (kpos >= 0) & (kpos < L)
    s = jnp.where(jnp.concatenate([ok] * 4, axis=0), s, NEG_BIG)
    rowg = lax.broadcasted_iota(jnp.int32, (4 * SWA_BLK, 1), 0) // SWA_BLK
    sink = jnp.zeros((4 * SWA_BLK, 1), F32)
    for g in range(4):
        sink = jnp.where(rowg == g, sink_ref[hk * 4 + g], sink)
    m = jnp.maximum(jnp.max(s, axis=-1, keepdims=True), sink)
    p = jnp.exp(s - m)
    ps = jnp.exp(sink - m)
    inv = 1.0 / (jnp.sum(p, axis=-1, keepdims=True) + ps)
    return qs, p * inv, ps * inv, slot, rowg


def _swa_qtab(tk_ref, r0):
    return [tk_ref[i, pl.ds(r0, SWA_BLK), :] for i in range(3)]


def _swa_fwd(h, tk, sink):
    def body(sink_ref, q_ref, kv_ref, tk_ref, y_ref, kexp, vexp):
        n = pl.program_id(1)

        @pl.when(n == 0)
        def _():
            _swa_pad_kv(kv_ref, tk_ref, kexp, vexp)

        r0 = pl.multiple_of(n * SWA_BLK, SWA_BLK)
        q = _rope(q_ref[...], _swa_qtab(tk_ref, r0))
        for hk in range(2):
            _, p, _, slot, _ = _swa_probs(q[:, hk * 256:(hk + 1) * 256], kexp[hk, pl.ds(r0, 3 * SWA_BLK), :], n,
                                          sink_ref, hk)
            o4 = _mm(p, vexp[hk, pl.ds(r0, 3 * SWA_BLK), :])
            o = jnp.zeros((SWA_BLK, 256), F32)
            for g in range(4):
                o = o + jnp.where(slot == g, o4[g * SWA_BLK:(g + 1) * SWA_BLK], 0.0)
            y_ref[:, hk * 256:(hk + 1) * 256] = o.astype(MX)

    return pl.pallas_call(
        body,
        grid_spec=pltpu.PrefetchScalarGridSpec(
            num_scalar_prefetch=1, grid=(NSEQ, NBLK),
            in_specs=[pl.BlockSpec((SWA_BLK, 512), lambda s, n, sk: (s * NBLK + n, 2)),
                      pl.BlockSpec((L, 256), lambda s, n, sk: (s, 6)),
                      pl.BlockSpec((3, L, 128), lambda s, n, sk: (0, 0, 0))],
            out_specs=pl.BlockSpec((SWA_BLK, 512), lambda s, n, sk: (s * NBLK + n, 0)),
            scratch_shapes=[pltpu.VMEM((2, L + 2 * SWA_BLK, 256), F32), pltpu.VMEM((2, L + 2 * SWA_BLK, 256), F32)]),
        out_shape=_sds((N, 512), MX), name="swa_fwd", compiler_params=_cp(("arbitrary", "arbitrary")))(sink, h, h, tk)


def _swa_bwd(h, tk, sink, dyc):
    def body(sink_ref, q_ref, kv_ref, tk_ref, dy_ref, dq_ref, dkv_ref, dsink_ref, kexp_all, vexp_all, dkacc, dvacc):
        sq = pl.program_id(0)
        n = pl.program_id(1)

        @pl.when(n == 0)
        def _():
            _swa_pad_kv(kv_ref, tk_ref, kexp_all, vexp_all)
            dkacc[...] = jnp.zeros_like(dkacc)
            dvacc[...] = jnp.zeros_like(dvacc)

        @pl.when((n == 0) & (sq == 0))
        def _():
            dsink_ref[...] = jnp.zeros_like(dsink_ref)

        r0 = pl.multiple_of(n * SWA_BLK, SWA_BLK)
        tq = _swa_qtab(tk_ref, r0)
        q = _rope(q_ref[...], tq)
        hrow = lax.broadcasted_iota(jnp.int32, (8, 128), 0)
        dsk = jnp.zeros((8, 128), F32)
        for hk in range(2):
            kexp = kexp_all[hk, pl.ds(r0, 3 * SWA_BLK), :]
            vexp = vexp_all[hk, pl.ds(r0, 3 * SWA_BLK), :]
            qs, p, ps, slot, rowg = _swa_probs(q[:, hk * 256:(hk + 1) * 256], kexp, n, sink_ref, hk)
            dy2 = dy_ref[:, hk * 256:(hk + 1) * 256]
            dos = jnp.concatenate([jnp.where(slot == g, dy2, 0.0) for g in range(4)], axis=0)
            dp = _mm_nt(dos, vexp)
            delta = jnp.sum(p * dp, axis=-1, keepdims=True)
            ds = p * (dp - delta) * 0.125
            dsr = -ps * delta
            for g in range(4):
                dsk = dsk + jnp.where(hrow == hk * 4 + g, jnp.sum(jnp.where(rowg == g, dsr, 0.0), axis=0, keepdims=True), 0.0)
            dq4 = _mm(ds, kexp)
            dq2 = jnp.zeros((SWA_BLK, 256), F32)
            for g in range(4):
                dq2 = dq2 + jnp.where(slot == g, dq4[g * SWA_BLK:(g + 1) * SWA_BLK], 0.0)
            dq_ref[:, hk * 256:(hk + 1) * 256] = _rope_t(dq2, tq).astype(MX)
            dkacc[hk, pl.ds(r0, 3 * SWA_BLK), :] += _mm_tn(ds, qs)
            dvacc[hk, pl.ds(r0, 3 * SWA_BLK), :] += _mm_tn(p, dos)
        dsink_ref[...] += dsk

        @pl.when(n == NBLK - 1)
        def _():
            seq = slice(SWA_BLK, SWA_BLK + L)
            dk = _rope_t(_swa_fold(dkacc[0, seq], 0) + _swa_fold(dkacc[1, seq], 1), tk_ref[...])
            dkv_ref[:, 0:128] = dk.astype(MX)
            dkv_ref[:, 128:256] = (_swa_fold(dvacc[0, seq], 0) + _swa_fold(dvacc[1, seq], 1)).astype(MX)

    blk = lambda col: pl.BlockSpec((SWA_BLK, 512), lambda s, n, sk: (s * NBLK + n, col))
    pad = pltpu.VMEM((2, L + 2 * SWA_BLK, 256), F32)
    return pl.pallas_call(
        body,
        grid_spec=pltpu.PrefetchScalarGridSpec(
            num_scalar_prefetch=1, grid=(NSEQ, NBLK),
            in_specs=[blk(2), pl.BlockSpec((L, 256), lambda s, n, sk: (s, 6)),
                      pl.BlockSpec((3, L, 128), lambda s, n, sk: (0, 0, 0)), blk(0)],
            out_specs=[blk(0), pl.BlockSpec((L, 256), lambda s, n, sk: (s, 0)),
                       pl.BlockSpec((8, 128), lambda s, n, sk: (0, 0))],
            scratch_shapes=[pad, pad, pad, pad]),
        out_shape=[_sds((N, 512), MX), _sds((N, 256), MX), _sds((8, 128))],
        name="swa_bwd", compiler_params=_cp(("arbitrary", "arbitrary")))(sink, h, h, tk, dyc)


def _outproj_fwd(ya, yb, yc, x, wo, g, b):
    tm = 512

    def body(ya_ref, yb_ref, yc_ref, x_ref, wo_ref, g_ref, b_ref, s_ref, x1_ref):
        mix = _mm(ya_ref[...], wo_ref[0:256]) + _mm(yb_ref[...], wo_ref[256:512]) + _mm(yc_ref[...], wo_ref[512:1024])
        s = ALPHA * x_ref[...] + mix
        s_ref[...] = s
        x1_ref[...] = _ln_fwd(s, g_ref[...], b_ref[...])

    row = lambda w_: pl.BlockSpec((tm, w_), lambda i: (i, 0))
    one = pl.BlockSpec((1, D), lambda i: (0, 0))
    return pl.pallas_call(
        body, grid=(N // tm,),
        in_specs=[row(256), row(256), row(512), row(D), pl.BlockSpec((D, D), lambda i: (0, 0)), one, one],
        out_specs=[row(D), row(D)], out_shape=[_sds((N, D)), _sds((N, D))],
        name="outproj_fwd", compiler_params=_cp(("parallel",)))(ya, yb, yc, x, wo, g, b)


def _outproj_bwd(dx1, s1, ya, yb, yc, wo, g):
    tm = 512
    nt = N // tm

    def body(dx1_ref, s_ref, ya_ref, yb_ref, yc_ref, wo_ref, g_ref,
             dya_ref, dyb_ref, dyc_ref, dxp_ref, dwo_ref, dg_ref, db_ref, acc):
        i = pl.program_id(0)

        @pl.when(i == 0)
        def _():
            acc[...] = jnp.zeros_like(acc)
            dg_ref[...] = jnp.zeros_like(dg_ref)
            db_ref[...] = jnp.zeros_like(db_ref)

        ds, dg, db = _ln_bwd(dx1_ref[...], s_ref[...], g_ref[...])
        dg_ref[...] += dg
        db_ref[...] += db
        dxp_ref[...] = ALPHA * ds
        dy = _mm_nt(ds, wo_ref[...])
        dya_ref[...] = dy[:, 0:256]
        dyb_ref[...] = dy[:, 256:512]
        dyc_ref[...] = dy[:, 512:1024]
        acc[0:256] += _mm_tn(ya_ref[...], ds)
        acc[256:512] += _mm_tn(yb_ref[...], ds)
        acc[512:1024] += _mm_tn(yc_ref[...], ds)

        @pl.when(i == nt - 1)
        def _():
            dwo_ref[...] = acc[...].astype(MX)

    row = lambda w_: pl.BlockSpec((tm, w_), lambda i: (i, 0))
    one = pl.BlockSpec((1, D), lambda i: (0, 0))
    full = pl.BlockSpec((D, D), lambda i: (0, 0))
    return pl.pallas_call(
        body, grid=(nt,),
        in_specs=[row(D), row(D), row(256), row(256), row(512), full, one],
        out_specs=[row(256), row(256), row(512), row(D), full, one, one],
        out_shape=[_sds((N, 256)), _sds((N, 256)), _sds((N, 512)), _sds((N, D)), _sds((D, D), MX), _sds((1, D)), _sds((1, D))],
        scratch_shapes=[pltpu.VMEM((D, D), F32)],
        name="outproj_bwd", compiler_params=_cp(("arbitrary",)))(dx1, s1, ya, yb, yc, wo, g)


def _mix_ffn_fwd(ya, yb, yc, x, wo, g1, b1, w1, w2, g, b, target=None):
    tm = FFN_TM
    head = target is not None

    def body(*refs):
        ya_ref, yb_ref, yc_ref, xin_ref, wo_ref, g1_ref, b1_ref, w1_ref, w2_ref, g_ref, b_ref = refs[:11]
        s1_ref, x1_ref, a_ref, s_ref, y_ref = refs[11 + head:16 + head]
        mix = _mm(ya_ref[...], wo_ref[0:256]) + _mm(yb_ref[...], wo_ref[256:512]) + _mm(yc_ref[...], wo_ref[512:1024])
        s1 = ALPHA * xin_ref[...] + mix
        s1_ref[...] = s1
        x = _ln_fwd(s1, g1_ref[...], b1_ref[...])
        x1_ref[...] = x
        xb = x.astype(MX)
        s = ALPHA * x
        for j in range(NSHARD):
            a = _mm(xb, w1_ref[j])
            a_ref[:, j * D:(j + 1) * D] = a.astype(MX)
            s = s + _mm(jnp.square(jnp.maximum(a, 0.0)), w2_ref[j])
        s_ref[...] = s
        x2 = _ln_fwd(s, g_ref[...], b_ref[...])
        if not head:
            y_ref[...] = x2
            return
        l_ref = refs[-1]

        @pl.when(pl.program_id(0) == 0)
        def _():
            l_ref[...] = jnp.zeros_like(l_ref)

        e = x2 - refs[11][...]
        y_ref[...] = e * (1.0 / D)
        l_ref[...] += jnp.sum(jnp.sum(e * e, axis=1, keepdims=True), axis=0, keepdims=True) * (0.5 / D)

    rw = lambda w_: pl.BlockSpec((tm, w_), lambda i: (i, 0))
    row = rw(D)
    once = dict(pipeline_mode=pl.Buffered(1))
    wall = pl.BlockSpec((NSHARD, D, D), lambda i: (0, 0, 0), **once)
    one = pl.BlockSpec((1, D), lambda i: (0, 0))
    acc = pl.BlockSpec((8, 128), lambda i: (0, 0))
    return pl.pallas_call(
        body, grid=(N // tm,),
        in_specs=[rw(256), rw(256), rw(512), row, pl.BlockSpec((D, D), lambda i: (0, 0), **once), one, one,
                  wall, wall, one, one] + [row] * head,
        out_specs=[row, row, pl.BlockSpec((tm, DFF), lambda i: (i, 0)), row, row] + [acc] * head,
        out_shape=[_sds((N, D)), _sds((N, D)), _sds((N, DFF), MX), _sds((N, D)), _sds((N, D))] + [_sds((8, 128))] * head,
        name="mix_ffn_fwd", compiler_params=_cp(("arbitrary",), FFN_VMEM))(
            ya, yb, yc, x, wo, g1, b1, w1, w2, g, b, *([target] * head))


def _ffn_bwd_act(dy, s2, a, w1, w2, g):
    tm = FFN_TM

    def body(dy_ref, s_ref, a_ref, w1_ref, w2_ref, g_ref, da_ref, ds_ref, dx1_ref, dg_ref, db_ref):
        @pl.when(pl.program_id(0) == 0)
        def _():
            dg_ref[...] = jnp.zeros_like(dg_ref)
            db_ref[...] = jnp.zeros_like(db_ref)

        ds, dg, db = _ln_bwd(dy_ref[...], s_ref[...], g_ref[...])
        dsb = ds.astype(MX)
        ds_ref[...] = dsb
        dg_ref[...] += dg
        db_ref[...] += db
        dx1 = ALPHA * ds
        for j in range(NSHARD):
            da = (_mm_nt(dsb, w2_ref[j]) * 2.0 * jnp.maximum(a_ref[:, j * D:(j + 1) * D].astype(F32), 0.0)).astype(MX)
            da_ref[:, j * D:(j + 1) * D] = da
            dx1 = dx1 + _mm_nt(da, w1_ref[j])
        dx1_ref[...] = dx1

    row = pl.BlockSpec((tm, D), lambda i: (i, 0))
    wide = pl.BlockSpec((tm, DFF), lambda i: (i, 0))
    wall = pl.BlockSpec((NSHARD, D, D), lambda i: (0, 0, 0))
    one = pl.BlockSpec((1, D), lambda i: (0, 0))
    return pl.pallas_call(
        body, grid=(N // tm,),
        in_specs=[row, row, wide, wall, wall, one],
        out_specs=[wide, row, row, one, one],
        out_shape=[_sds((N, DFF), MX), _sds((N, D), MX), _sds((N, D)), _sds((1, D)), _sds((1, D))],
        name="ffn_bwd_act", compiler_params=_cp(("arbitrary",), FFN_VMEM))(dy, s2, a, w1, w2, g)


def _ffn_bwd_w(x1, da, a, ds):
    tm, nb = FFN_TM_W, FFN_WB
    nt = N // tm

    def body(x_ref, da_ref, a_ref, ds_ref, dw1_ref, dw2_ref, acc1, acc2):
        i = pl.program_id(1)

        @pl.when(i == 0)
        def _():
            acc1[...] = jnp.zeros_like(acc1)
            acc2[...] = jnp.zeros_like(acc2)

        x, ds_ = x_ref[...], ds_ref[...]
        for k in range(nb):
            cols = slice(k * D, (k + 1) * D)
            acc1[k] += _mm_tn(x, da_ref[:, cols])
            acc2[k] += _mm_tn(jnp.square(jnp.maximum(a_ref[:, cols].astype(F32), 0.0)), ds_)

        @pl.when(i == nt - 1)
        def _():
            dw1_ref[...] = acc1[...].astype(MX)
            dw2_ref[...] = acc2[...].astype(MX)

    row = pl.BlockSpec((tm, D), lambda j, i: (i, 0))
    col = pl.BlockSpec((tm, nb * D), lambda j, i: (i, j))
    wj = pl.BlockSpec((nb, D, D), lambda j, i: (j, 0, 0))
    return pl.pallas_call(
        body, grid=(NSHARD // nb, nt),
        in_specs=[row, col, col, row], out_specs=[wj, wj],
        out_shape=[_sds((NSHARD, D, D), MX), _sds((NSHARD, D, D), MX)],
        scratch_shapes=[pltpu.VMEM((nb, D, D), F32), pltpu.VMEM((nb, D, D), F32)],
        name="ffn_bwd_w", compiler_params=_cp(("parallel", "arbitrary"), FFN_VMEM))(x1, da, a, ds)


def _loss_head(y, target):
    tm = 512

    def body(y_ref, t_ref, dy_ref, l_ref):
        @pl.when(pl.program_id(0) == 0)
        def _():
            l_ref[...] = jnp.zeros_like(l_ref)

        e = y_ref[...] - t_ref[...]
        dy_ref[...] = e * (1.0 / D)
        l_ref[...] += jnp.sum(jnp.sum(e * e, axis=1, keepdims=True), axis=0, keepdims=True) * (0.5 / D)

    row = pl.BlockSpec((tm, D), lambda i: (i, 0))
    return pl.pallas_call(
        body, grid=(N // tm,), in_specs=[row, row],
        out_specs=[row, pl.BlockSpec((8, 128), lambda i: (0, 0))],
        out_shape=[_sds((N, D)), _sds((8, 128))], name="loss_head", compiler_params=_cp(("arbitrary",)))(y, target)


def _s5_discretize(a_re, a_im, log_step, b_re, b_im):
    lam = lax.complex(a_re, a_im)
    lam_bar = jnp.exp(lam * jnp.exp(log_step))
    b_bar = ((lam_bar - 1.0) / lam)[..., None] * lax.complex(b_re, b_im)
    return jnp.real(lam_bar), jnp.imag(lam_bar), jnp.real(b_bar), jnp.imag(b_bar)


def _s5_in_blocks(b):
    e = jnp.eye(8, dtype=F32)
    return jnp.einsum('ij,zbjph->zbihjp', e, b.reshape(2, 2, 8, S5_P, S5_H)).reshape(2, 2, 128, SW)


def _s5_in_unblocks(d):
    return jnp.einsum('zbihip->zbiph', d.reshape(2, 2, 8, S5_H, 8, S5_P)).reshape(2, S5_G, S5_P, S5_H)


def _s5_out_blocks(c):
    e = jnp.eye(8, dtype=F32)
    return jnp.einsum('ij,zbjhp->zbjpih', e, c.reshape(2, 2, 8, S5_H, S5_P)).reshape(2, 2, SW, 128)


def _s5_out_unblocks(d):
    return jnp.einsum('zbipih->zbihp', d.reshape(2, 2, 8, S5_P, 8, S5_H)).reshape(2, S5_G, S5_H, S5_P)


def _gate_weight(w_a):
    z = jnp.zeros((16, 128), F32)
    top = jnp.concatenate([w_a[0], z], axis=1)
    bot = jnp.concatenate([z, w_a[1]], axis=1)
    return jnp.concatenate([top, bot, jnp.zeros((96, 256), F32)], axis=0)


def _layer_prep(p):
    lr, li, br, bi = _s5_discretize(p["s5_a_re"], p["s5_a_im"], p["s5_log_step"], p["s5_b_re"], p["s5_b_im"])
    q = dict(p)
    q["bre"] = _s5_in_blocks(br).astype(MX)
    q["bim"] = _s5_in_blocks(bi).astype(MX)
    q["cre"] = _s5_out_blocks(p["s5_c_re"]).astype(MX)
    q["cim"] = _s5_out_blocks(p["s5_c_im"]).astype(MX)
    mr, mi = lr.reshape(2, 1024), li.reshape(2, 1024)
    both = lambda t0, t1: tuple(jnp.stack(p) for p in zip(t0, t1))
    q["tab"] = both(_lockstep_tables(mr[0], mi[0], False), _lockstep_tables(mr[1], mi[1], True))
    q["tabc"] = both(_lockstep_tables(mr[0], -mi[0], True), _lockstep_tables(mr[1], -mi[1], False))
    q["dsk"] = p["s5_d"].reshape(1, 256)
    q["wa"] = _gate_weight(p["gla_w_a"]).astype(MX)
    q["ba"] = p["gla_b_a"].reshape(1, 256)
    q["lng"] = p["gla_ln_g"].reshape(1, 256)
    q["bv"] = p["s5_b_glu"][:256].reshape(1, 256)
    q["bg"] = p["s5_b_glu"][256:].reshape(1, 256)
    for k in ("ln1_g", "ln1_b", "ln2_g", "ln2_b"):
        q[k] = p[k].reshape(1, D)
    return q


def _layer_fwd(x, q, tk, fetch, target=None):
    q["w_in"] = fetch("w_in", x)
    h = _inproj_fwd(x, q["w_in"])
    hre, him, y2 = _s5_fwd(h, q["bre"], q["bim"], q["cre"], q["cim"], q["tab"])
    q["w4"] = fetch("s5_w_glu", y2)
    ya = _s5_glu_fwd(y2, h, q["dsk"], q["w4"], q["bv"], q["bg"])
    la2 = _gla_gate_fwd(h, q["wa"], q["ba"])
    of, ob, sf, sb = _gla_fwd(h, la2)
    yb = _gla_post_fwd(of, ob, h, q["lng"])
    yc = _swa_fwd(h, tk, q["swa_sink"])
    q["w_out"] = fetch("w_out", yc)
    q["w_ff1"] = fetch("w_ff1", yc)
    q["w_ff2"] = fetch("w_ff2", yc)
    s1, x1, a, s2, *out = _mix_ffn_fwd(ya, yb, yc, x, q["w_out"], q["ln1_g"], q["ln1_b"], q["w_ff1"], q["w_ff2"],
                                       q["ln2_g"], q["ln2_b"], target)
    saved = dict(x=x, h=h, hre=hre, him=him, y2=y2, ya=ya, la2=la2, of=of, ob=ob, sf=sf, sb=sb, yb=yb, yc=yc,
                 s1=s1, x1=x1, a=a, s2=s2)
    return (out[0] if target is None else tuple(out)), saved


def _layer_bwd(dy, q, sv, tk, emit):
    g = {}
    da, ds2, dx1, g["dg2"], g["db2"] = _ffn_bwd_act(dy, sv["s2"], sv["a"], q["w_ff1"], q["w_ff2"], q["ln2_g"])
    dw1, dw2 = _ffn_bwd_w(sv["x1"], da, sv["a"], ds2)
    tie = emit(dict(w_ff1=dw1, w_ff2=dw2))
    dya, dyb, dyc, dxp, dwo, g["dg1"], g["db1"] = _outproj_bwd(dx1, sv["s1"], sv["ya"], sv["yb"], sv["yc"],
                                                               q["w_out"], q["ln1_g"] + tie)
    h = sv["h"]
    daq, dakv, g["dsink"] = _swa_bwd(h, tk, q["swa_sink"], dyc)
    do, gr, g["dlng"] = _gla_post_bwd(sv["of"], sv["ob"], h, q["lng"], dyb)
    gq_f, gk_f, gv_f, gl_f, gq_b, gk_b, gv_b, gl_b = _gla_bwd(h, sv["la2"], do, sv["sf"], sv["sb"])
    dhl, g["dwa"], g["dba"] = _gla_gate_bwd(h, q["wa"], q["ba"], gl_f, gl_b)
    dyp, dud, g["dd"], dw4, g["dbv"], g["dbg"] = _s5_glu_bwd(sv["y2"], h, q["dsk"], q["w4"], q["bv"], q["bg"], dya)
    tie = emit(dict(w_out=dwo.reshape(NSHARD, D // NSHARD, D), s5_w_glu=dw4))
    du2, g["dbre"], g["dbim"], g["dcre"], g["dcim"], g["dmu"] = _s5_bwd(
        h, dyp, sv["hre"], sv["him"], q["bre"], q["bim"], q["cre"], q["cim"], (q["tabc"][0], q["tabc"][1] + tie))
    dx, dwt = _inproj_bwd(sv["x"], q["w_in"], dxp, du2, dud, gq_f, gq_b, gk_f, gk_b, gv_f, gv_b, gr, daq, dakv, dhl)
    tie = emit(dict(w_in=dwt))
    return dx, g, tie


NATIVE = ("dmu", "dbre", "dbim", "dcre", "dcim", "dd", "dbv", "dbg", "dwa", "dba", "dlng", "dsink",
          "dg1", "db1", "dg2", "db2", "loss")
ICI_CORE = (0, 0, 0, 1, 1, 0, 0, 0, 1, 1, 1, 1, 0, 0, 1, 1, 0)


def _finish_small(n, w):
    g = {}
    dmu = n["dmu"]
    dlr = dmu[:, :, :, 0].reshape(DEPTH, 2, S5_G, S5_P)
    dli = dmu[:, :, :, 1].reshape(DEPTH, 2, S5_G, S5_P)

    def unblock(c, perm, shape):
        return c.reshape(DEPTH, 2, 2, S5_H, 8, S5_P).transpose(perm).reshape(shape)

    b_shape, c_shape = (DEPTH, 2, S5_G, S5_P, S5_H), (DEPTH, 2, S5_G, S5_H, S5_P)
    _, vjp = jax.vjp(_s5_discretize, w["s5_a_re"], w["s5_a_im"], w["s5_log_step"], w["s5_b_re"], w["s5_b_im"])
    (g["s5_a_re"], g["s5_a_im"], g["s5_log_step"], g["s5_b_re"], g["s5_b_im"]) = vjp(
        (dlr, dli, unblock(n["dbre"], (0, 1, 2, 4, 5, 3), b_shape), unblock(n["dbim"], (0, 1, 2, 4, 5, 3), b_shape)))
    g["s5_c_re"] = unblock(n["dcre"], (0, 1, 2, 4, 3, 5), c_shape)
    g["s5_c_im"] = unblock(n["dcim"], (0, 1, 2, 4, 3, 5), c_shape)
    g["s5_d"] = n["dd"].reshape(DEPTH, S5_G, S5_H)
    g["s5_b_glu"] = jnp.concatenate([n["dbv"], n["dbg"]], axis=2).reshape(DEPTH, 512)
    g["gla_w_a"] = jnp.stack([n["dwa"][:, 0:16, 0:128], n["dwa"][:, 16:32, 128:256]], axis=1)
    g["gla_b_a"] = n["dba"].reshape(DEPTH, 2, 128)
    g["gla_ln_g"] = n["dlng"].reshape(DEPTH, 256)
    g["swa_sink"] = n["dsink"][:, :, 0]
    for k, s in (("ln1_g", "dg1"), ("ln1_b", "db1"), ("ln2_g", "dg2"), ("ln2_b", "db2")):
        g[k] = n[s].reshape(DEPTH, D)
    return g


def _local_step(x, target, qs, tk, fetch, emit):
    saved = []
    for l, q in enumerate(qs):
        x, sv = _layer_fwd(x, q, tk, functools.partial(fetch, l), target if l == DEPTH - 1 else None)
        saved.append(sv)
    dy, lacc = x
    smalls = [None] * DEPTH
    tie = 0.0
    for l in reversed(range(DEPTH)):
        qs[l]["ln2_g"] = qs[l]["ln2_g"] + tie
        dy, smalls[l], tie = _layer_bwd(dy, qs[l], saved[l], tk, functools.partial(emit, l))
    smalls[0]["db2"] = smalls[0]["db2"] + tie
    for l in range(DEPTH):
        smalls[l]["loss"] = lacc if l == 0 else jnp.zeros_like(lacc)
    return lacc[0, 0], dy, smalls


BIG = ("w_in", "s5_w_glu", "w_out", "w_ff1", "w_ff2")
SMALL = ("s5_a_re", "s5_a_im", "s5_log_step", "s5_b_re", "s5_b_im", "s5_c_re", "s5_c_im", "s5_d", "s5_b_glu",
         "gla_w_a", "gla_b_a", "gla_ln_g", "swa_sink", "ln1_g", "ln1_b", "ln2_g", "ln2_b")
ANY = pl.BlockSpec(memory_space=pl.ANY)


def _place():
    x, y, c = lax.axis_index("x"), lax.axis_index("y"), lax.axis_index("c")
    return x, y, c, [(1 - x, y), (x, 1 - y), (1 - x, 1 - y)]


HBM = pl.BlockSpec(memory_space=pltpu.HBM)
SEMS = pl.BlockSpec(memory_space=pltpu.SEMAPHORE)
EFFECT = pltpu.SideEffectType.DATAFLOW_SIDE_EFFECTING


def _push_copies(ins, lands, send, recv, gather, sending):
    x, y, c, chips = _place()
    me = 2 * x + y
    if gather == "sibling":
        return [pltpu.make_async_remote_copy(src_ref=ins[a], dst_ref=lands[a], send_sem=send.at[a], recv_sem=recv.at[a],
                                             device_id=(x, y, 1 - c), device_id_type=MESH) for a in range(len(lands))]
    out = []
    for a in range(len(lands)):
        for j, (px, py) in enumerate(chips):
            peer = 2 * px + py
            src = lands[a].at[me] if gather else ins[a].at[peer if sending else me]
            dst = lands[a].at[me if sending else peer]
            out.append(pltpu.make_async_remote_copy(src_ref=src, dst_ref=dst, send_sem=send.at[3 * a + j],
                                                    recv_sem=recv.at[3 * a + j], device_id=(px, py, c),
                                                    device_id_type=MESH))
    return out


def _push_start(name, arrs, gather):
    n = len(arrs)
    ops = list(arrs) if gather is True else list(arrs) + [lax.empty(s.shape, s.dtype) for s in arrs]
    m = len(ops)

    def body(*refs):
        ins, lnd = (refs[:n], refs[:n]) if gather is True else (refs[:n], refs[n:m])
        for cp in _push_copies(ins, lnd, refs[m], refs[m + 1], gather, True):
            cp.start()
        refs[-1][...] = jnp.zeros((8, 128), F32)

    ops = [pltpu.with_memory_space_constraint(t, pltpu.HBM) for t in ops]
    res = pl.pallas_call(
        body, name=name,
        out_shape=(pltpu.SemaphoreType.DMA((3 * n,)), pltpu.SemaphoreType.DMA((3 * n,)),
                   *[pltpu.HBM(t.shape, t.dtype) for t in ops], _sds((8, 128))),
        in_specs=[HBM] * m,
        out_specs=(SEMS, SEMS, *[HBM] * m, pl.BlockSpec(memory_space=pltpu.VMEM)),
        input_output_aliases={i: 2 + i for i in range(m)},
        compiler_params=pltpu.CompilerParams(has_side_effects=EFFECT))(*ops)
    return res[0], res[1], list(res[2:2 + m]), res[-1]


def _push_wait(name, started, after, gather):
    send, recv, ops, _ = started
    m = len(ops)
    n = m if gather is True else m // 2

    def body(*refs):
        ins, lnd = (refs[:n], refs[:n]) if gather is True else (refs[:n], refs[n:m])
        for cp in _push_copies(ins, lnd, refs[m], refs[m + 1], gather, False):
            cp.wait_send()
            cp.wait_recv()

    res = pl.pallas_call(
        body, name=name,
        out_shape=[pltpu.HBM(t.shape, t.dtype) for t in ops],
        in_specs=[HBM] * m + [SEMS, SEMS, ANY], out_specs=[HBM] * m,
        input_output_aliases={i: i for i in range(m)},
        compiler_params=pltpu.CompilerParams(has_side_effects=EFFECT))(*ops, send, recv, after)
    return list(res)


def _row_tile(rows):
    return max(t for t in range(8, min(rows, 512) + 1, 8) if rows % t == 0)


def _cast_to_slot(me, w, l):
    _, rows, cols = w.shape
    tr = _row_tile(rows)

    def body(me_ref, w_ref, o_ref):
        o_ref[0] = w_ref[0].astype(MX)

    return pl.pallas_call(
        body,
        grid_spec=pltpu.PrefetchScalarGridSpec(
            num_scalar_prefetch=1, grid=(rows // tr,),
            in_specs=[pl.BlockSpec((1, tr, cols), lambda i, me_: (l, i, 0))],
            out_specs=pl.BlockSpec((1, tr, cols), lambda i, me_: (me_[0], i, 0))),
        out_shape=_sds((NSHARD, rows, cols), MX), name="cast_to_slot", compiler_params=_cp(("arbitrary",)))(me, w)


def _sum_sources(me, recv, own):
    _, rows, cols = recv[0].shape
    tr = min(_row_tile(rows), 256) if rows % 256 == 0 else _row_tile(rows)
    nt = rows // tr

    def body(me_ref, *refs):
        o_ref = refs[-1]
        for l in range(DEPTH):
            @pl.when(pl.program_id(0) == l)
            def _():
                r_ref, own_ref = refs[2 * l], refs[2 * l + 1]
                part = [jnp.where(me_ref[0] == s, own_ref[0], r_ref[s]).astype(F32) for s in range(NSHARD)]
                o_ref[...] = ((part[0] + part[1]) + part[2]) + part[3]

    in_specs = []
    for l in range(DEPTH):
        pick = lambda g, i, me_, l=l: jnp.where(g == l, i, jnp.where(g < l, 0, nt - 1))
        in_specs += [pl.BlockSpec((NSHARD, tr, cols), lambda g, i, me_, pick=pick: (0, pick(g, i, me_), 0)),
                     pl.BlockSpec((1, tr, cols), lambda g, i, me_, pick=pick: (me_[0], pick(g, i, me_), 0))]
    return pl.pallas_call(
        body,
        grid_spec=pltpu.PrefetchScalarGridSpec(
            num_scalar_prefetch=1, grid=(DEPTH, nt), in_specs=in_specs,
            out_specs=pl.BlockSpec((tr, cols), lambda g, i, me_: (g * nt + i, 0))),
        out_shape=_sds((DEPTH * rows, cols)), name="sum_sources",
        compiler_params=_cp(("arbitrary", "arbitrary")))(me, *[t for l in range(DEPTH) for t in (recv[l], own[l])])


def _swap_sibling(arrs):
    n = len(arrs)

    def body(*refs):
        ins, outs = refs[:n], refs[n:2 * n]
        send, recv = refs[2 * n:]
        x, y, c, _ = _place()
        cps = [pltpu.make_async_remote_copy(src_ref=ins[a], dst_ref=outs[a], send_sem=send.at[a], recv_sem=recv.at[a],
                                            device_id=(x, y, 1 - c), device_id_type=MESH) for a in range(n)]
        for cp in cps:
            cp.start()
        for cp in cps:
            cp.wait()

    return pl.pallas_call(
        body, in_specs=[ANY] * n, out_specs=[ANY] * n, out_shape=[_sds(a.shape, a.dtype) for a in arrs],
        scratch_shapes=[pltpu.SemaphoreType.DMA((n,)), pltpu.SemaphoreType.DMA((n,))],
        name="swap_sibling")(*arrs)


def _allreduce_small(per_layer):
    nk = len(per_layer[0])
    n = DEPTH * nk
    shapes = [a.shape for a in per_layer[0]]

    def body(*refs):
        ins, outs = refs[:n], refs[n:n + nk]
        sibs, slots = refs[n + nk:n + 2 * nk], refs[n + 2 * nk:n + 3 * nk]
        send, recv = refs[n + 3 * nk:]
        x, y, c, chips = _place()
        me = 2 * x + y
        d2d = [pltpu.make_async_remote_copy(src_ref=ins[l * nk + k], dst_ref=sibs[k].at[l], send_sem=send.at[l * nk + k],
                                            recv_sem=recv.at[l * nk + k], device_id=(x, y, 1 - c), device_id_type=MESH)
               for l in range(DEPTH) for k in range(nk)]
        for cp in d2d:
            cp.start()
        for cp in d2d:
            cp.wait()
        for l in range(DEPTH):
            for k in range(nk):
                slots[k][0, l] = ins[l * nk + k][...] + sibs[k][l]

        def swap(k, stage):
            peer = (1 - x, y, c) if stage == 0 else (x, 1 - y, c)
            return pltpu.make_async_remote_copy(src_ref=slots[k].at[2 * stage], dst_ref=slots[k].at[2 * stage + 1],
                                                send_sem=send.at[n + 3 * k + stage], recv_sem=recv.at[n + 3 * k + stage],
                                                device_id=peer, device_id_type=MESH)

        def handover(k):
            return pltpu.make_async_remote_copy(src_ref=outs[k], dst_ref=outs[k], send_sem=send.at[n + 3 * nk + k],
                                                recv_sem=recv.at[n + 3 * nk + k], device_id=(x, y, 1 - c),
                                                device_id_type=MESH)

        halves = (tuple(k for k in range(nk) if ICI_CORE[k] == 0), tuple(k for k in range(nk) if ICI_CORE[k] == 1))
        for cc in range(2):
            @pl.when(c == cc)
            def _():
                mine, theirs = halves[cc], halves[1 - cc]
                for stage in range(2):
                    cps = [swap(k, stage) for k in mine]
                    for cp in cps:
                        cp.start()
                    for cp in cps:
                        cp.wait()
                    for k in mine:
                        if stage == 0:
                            slots[k][2] = slots[k][0] + slots[k][1]
                        else:
                            outs[k][...] = slots[k][2] + slots[k][3]
                over = [handover(k) for k in mine]
                for cp in over:
                    cp.start()
                for k in theirs:
                    handover(k).wait_recv()
                for cp in over:
                    cp.wait_send()

    vm = pl.BlockSpec(memory_space=pltpu.VMEM)
    return pl.pallas_call(
        body, in_specs=[vm] * n, out_specs=[vm] * nk, out_shape=[_sds((DEPTH,) + s) for s in shapes],
        scratch_shapes=([pltpu.VMEM((DEPTH,) + s, F32) for s in shapes]
                        + [pltpu.VMEM((NSHARD, DEPTH) + s, F32) for s in shapes]
                        + [pltpu.SemaphoreType.DMA((n + 4 * nk,)), pltpu.SemaphoreType.DMA((n + 4 * nk,))]),
        name="allreduce_small", compiler_params=pltpu.CompilerParams(vmem_limit_bytes=VMEM_LIMIT))(
            *[a for layer in per_layer for a in layer])


def _adamw_math(w, g, m, v):
    m = ADAM_B1 * m + (1.0 - ADAM_B1) * g
    v = ADAM_B2 * v + (1.0 - ADAM_B2) * jnp.square(g)
    m_hat = m / (1.0 - ADAM_B1 ** ADAM_STEP)
    v_hat = v / (1.0 - ADAM_B2 ** ADAM_STEP)
    delta = -ADAM_LR * (m_hat / (jnp.sqrt(v_hat) + ADAM_EPS) + ADAM_WD * w)
    return delta, m, v


def _adamw(g_parts, w, m, v):
    rows, cols = w.shape
    tr = 256 if rows % 256 == 0 else _row_tile(rows)
    k = len(g_parts)

    def body(*refs):
        g = refs[0][...]
        for r in refs[1:k]:
            g = g + r[...]
        w_ref, m_ref, v_ref, go, do, mo, vo = refs[k:]
        d, mn, vn = _adamw_math(w_ref[...], g, m_ref[...], v_ref[...])
        go[...] = g
        do[...] = d
        mo[...] = mn
        vo[...] = vn

    spec = pl.BlockSpec((tr, cols), lambda i: (i, 0))
    return pl.pallas_call(
        body, grid=(rows // tr,), in_specs=[spec] * (k + 3), out_specs=[spec] * 4,
        out_shape=[_sds((rows, cols))] * 4, name="adamw", compiler_params=_cp(("parallel",)))(*g_parts, w, m, v)


def _adamw_small(gs, ws, ms, vs):
    n = len(gs)

    def body(*refs):
        for k in range(n):
            d, mn, vn = _adamw_math(refs[n + k][...], refs[k][...], refs[2 * n + k][...], refs[3 * n + k][...])
            refs[4 * n + k][...] = d
            refs[5 * n + k][...] = mn
            refs[6 * n + k][...] = vn

    vm = pl.BlockSpec(memory_space=pltpu.VMEM)
    shapes = [_sds(a.shape) for a in ws]
    res = pl.pallas_call(
        body, in_specs=[vm] * (4 * n), out_specs=[vm] * (3 * n), out_shape=shapes * 3, name="adamw_small",
        compiler_params=pltpu.CompilerParams(vmem_limit_bytes=VMEM_LIMIT))(*gs, *ws, *ms, *vs)
    return res[:n], res[n:2 * n], res[2 * n:]


_ARGS = ("x", "w_in", "s5_a_re", "s5_a_im", "s5_log_step", "s5_b_re", "s5_b_im", "s5_c_re", "s5_c_im", "s5_d",
         "s5_w_glu", "s5_b_glu", "gla_w_a", "gla_b_a", "gla_ln_g", "swa_sink", "w_out", "ln1_g", "ln1_b", "w_ff1",
         "w_ff2", "ln2_g", "ln2_b")
_WEIGHTS = _ARGS[1:]


def _shard_cols(d):
    return d.reshape(d.shape[0], NSHARD, d.shape[1] // NSHARD).transpose(1, 0, 2)


def kernel(x, w_in, s5_a_re, s5_a_im, s5_log_step, s5_b_re, s5_b_im, s5_c_re, s5_c_im, s5_d, s5_w_glu, s5_b_glu, gla_w_a, gla_b_a, gla_ln_g, swa_sink, w_out, ln1_g, ln1_b, w_ff1, w_ff2, ln2_g, ln2_b, loss_target, m_w_in, m_s5_a_re, m_s5_a_im, m_s5_log_step, m_s5_b_re, m_s5_b_im, m_s5_c_re, m_s5_c_im, m_s5_d, m_s5_w_glu, m_s5_b_glu, m_gla_w_a, m_gla_b_a, m_gla_ln_g, m_swa_sink, m_w_out, m_ln1_g, m_ln1_b, m_w_ff1, m_w_ff2, m_ln2_g, m_ln2_b, v_w_in, v_s5_a_re, v_s5_a_im, v_s5_log_step, v_s5_b_re, v_s5_b_im, v_s5_c_re, v_s5_c_im, v_s5_d, v_s5_w_glu, v_s5_b_glu, v_gla_w_a, v_gla_b_a, v_gla_ln_g, v_swa_sink, v_w_out, v_ln1_g, v_ln1_b, v_w_ff1, v_w_ff2, v_ln2_g, v_ln2_b):
    given = dict(locals())
    w = {k: given[k] for k in _WEIGHTS}
    mom = {k: given["m_" + k] for k in _WEIGHTS}
    var = {k: given["v_" + k] for k in _WEIGHTS}

    me = (2 * lax.axis_index("x") + lax.axis_index("y")).astype(jnp.int32).reshape(1)
    tr = lambda t: t.transpose(0, 2, 1)
    shard = {k: (tr(w[k]) if k == "w_in" else w[k]) for k in BIG}
    qs = [None] * DEPTH

    first = ("w_in", "s5_w_glu", "w_out")
    follow = {(0, "w_in"): [(0, BIG[3:]), (1, first)], (0, "w_ff1"): [(1, BIG[3:])]}
    gathers = {}

    casts = {}

    def start_gather(l, names, behind=None):
        lands = [casts.pop((l, k)) if (l, k) in casts else _cast_to_slot(me, shard[k], l) for k in names]
        if behind is not None:
            lands, behind = lax.optimization_barrier((lands, behind))
        st = _push_start(f"gather_start_{l}_{names[0]}", lands, True)
        for k in names:
            gathers[l, k] = [names, st, None]
        return st[-1], behind

    token = start_gather(0, first[:1])[0] + start_gather(0, first[1:])[0]
    zero = token[0, 0]
    for l in range(DEPTH):
        for k in BIG:
            if (l, k) not in gathers:
                casts[l, k] = _cast_to_slot(me, lax.optimization_barrier((shard[k], token))[0], l)
        qs[l] = _layer_prep({k: (w[k][l] + zero if k == "s5_a_re" else w[k][l]) for k in SMALL})
    token, casts, qs = lax.optimization_barrier((token, casts, qs))

    def fetch(l, name, after):
        names, st, got = gathers[l, name]
        tie = None
        if got is None:
            if l == 0 and name == "w_in":
                after = token
            lands = _push_wait(f"gather_wait_{l}_{names[0]}", st, after, True)
            for l2, names2 in follow.get((l, name), ()):
                tok, lands[0] = start_gather(l2, names2, lands[0])
                tie = tok if tie is None else tie + tok
            got = dict(zip(names, lands))
            for k in names:
                gathers[l, k][2] = got
        full = got[name]
        if name == "w_in":
            return _in_rows(full, token if tie is None else tie)
        if tie is not None:
            qs[l]["ln2_b"] = qs[l]["ln2_b"] + tie[0, 0]
        return full.reshape(D, D) if name == "w_out" else full

    scatters, held = [], {}

    def emit(l, grads):
        if l > 0:
            held.update(grads)
            if "w_in" not in grads:
                return 0.0
            grads = dict(held)
            held.clear()
        names = tuple(grads)
        st = _push_start(f"scatter_start_{l}_{names[0]}", [grads[k] for k in names], False)
        scatters.append((l, names, st))
        return st[-1][0, 0]

    loss, dx, smalls = _local_step(x.reshape(N, D), loss_target.reshape(N, D), qs, _rope_tables(128), fetch, emit)

    out, recv, own = {}, {}, {}

    def collect(keys, after):
        for l, names, st in scatters:
            if names[0] in keys:
                ops = _push_wait(f"scatter_wait_{l}_{names[0]}", st, after, False)
                for i, k in enumerate(names):
                    own[l, k], recv[l, k] = ops[i], ops[len(names) + i]

    def to_sibling(keys):
        sums = [_sum_sources(me, [recv[l, k] for l in range(DEPTH)], [own[l, k] for l in range(DEPTH)]) for k in keys]
        return _push_start(f"swap_start_{keys[0]}", sums, "sibling")

    def apply(keys, started, after):
        ops = _push_wait(f"swap_wait_{keys[0]}", started, after, "sibling")
        for i, k in enumerate(keys):
            mine, other = ops[i], ops[len(keys) + i]
            shp = shard[k].shape
            r = _adamw([mine, other], *((tr(t[k]) if k == "w_in" else t[k]).reshape(-1, shp[-1]) for t in (w, mom, var)))
            r = [t.reshape(shp) for t in r]
            out[k] = [tr(t) for t in r] if k == "w_in" else r
        return out[keys[-1]][1]

    collect(("w_ff1", "w_ff2", "w_out", "s5_w_glu"), dx)
    ff = to_sibling(("w_ff1", "w_ff2"))
    mix = to_sibling(("w_out", "s5_w_glu"))
    smalls[0]["db1"] = smalls[0]["db1"] + (ff[-1][0, 0] + mix[-1][0, 0])
    native = _allreduce_small([[smalls[l][k] for k in NATIVE] for l in range(DEPTH)])
    native = dict(zip(NATIVE, native))
    loss = native["loss"][0, 0, 0] + native["loss"][1, 0, 0]
    gsmall = _finish_small(native, w)
    view = lambda k, t: t.transpose(0, 1, 2, 4, 3) if k in ("s5_b_re", "s5_b_im") else t
    res = _adamw_small(*([view(k, t[k]) for k in SMALL] for t in (gsmall, w, mom, var)))
    for i, k in enumerate(SMALL):
        out[k] = [gsmall[k]] + [view(k, r[i]) for r in res]
    last = apply(("w_ff1", "w_ff2"), ff, res[0][-1])
    collect(("w_in",), last)
    win = to_sibling(("w_in",))
    last = apply(("w_out", "s5_w_glu"), mix, win[-1])
    apply(("w_in",), win, last)

    return (loss, dx.reshape(NSEQ, L, D), *[out[k][0] for k in _WEIGHTS], *[out[k][1] for k in _WEIGHTS],
            *[out[k][2] for k in _WEIGHTS], *[out[k][3] for k in _WEIGHTS])
```

```python
import functools
import math

import jax
import jax.numpy as jnp
from jax import lax
from jax.experimental import pallas as pl
from jax.experimental.pallas import tpu as pltpu

F32 = jnp.float32
MX = jnp.bfloat16
MESH = pl.DeviceIdType.MESH

DEPTH = 2
NSEQ = 2
L = 2048
N = NSEQ * L
D = 1024
DFF = 4096
NSHARD = 4
S5_G, S5_H, S5_P = 16, 16, 64
GLA_CHUNK = 64
NCHUNK = L // GLA_CHUNK
GLA_GROUP = 4
NGROUP = NCHUNK // GLA_GROUP
SWA_BLK = 128
NBLK = L // SWA_BLK
ROT = 16
ROPE_THETA = 500000.0
LN_EPS = 1e-5
ALPHA = (2 * DEPTH) ** 0.25
NEG_BIG = -1e30
DIN = 1824
DINP = 1920
ADAM_LR, ADAM_B1, ADAM_B2, ADAM_EPS, ADAM_WD, ADAM_STEP = 0.001, 0.9, 0.999, 1e-08, 0.01, 10
VMEM_LIMIT = 56 * 1024 * 1024
TT = 512
SW = 512
FFN_TM = 512
FFN_TM_W = 1024
FFN_WB = 1
FFN_VMEM = 60 * 1024 * 1024
INPROJ_BWD_TM = 512


def _cp(sem, vmem=VMEM_LIMIT):
    return pltpu.CompilerParams(dimension_semantics=sem, vmem_limit_bytes=vmem)


def _mm(a, b):
    return jnp.dot(a.astype(MX), b.astype(MX), preferred_element_type=F32)


def _mm_nt(a, b):
    return lax.dot_general(a.astype(MX), b.astype(MX), (((1,), (1,)), ((), ())), preferred_element_type=F32)


def _mm_tn(a, b):
    return lax.dot_general(a.astype(MX), b.astype(MX), (((0,), (0,)), ((), ())), preferred_element_type=F32)


@jax.custom_vjp
def _dmm(a, b):
    return _mm(a, b)


_dmm.defvjp(lambda a, b: (_mm(a, b), (a, b)), lambda r, g: (_mm_nt(g, r[1]), _mm_tn(r[0], g)))


@jax.custom_vjp
def _dmm_nt(a, b):
    return _mm_nt(a, b)


_dmm_nt.defvjp(lambda a, b: (_mm_nt(a, b), (a, b)), lambda r, g: (_mm(g, r[1]), _mm_tn(g, r[0])))


@jax.custom_vjp
def _dmm_tn(a, b):
    return _mm_tn(a, b)


_dmm_tn.defvjp(lambda a, b: (_mm_tn(a, b), (a, b)), lambda r, g: (_mm_nt(r[1], g), _mm(r[0], g)))


def _split3(x):
    hi = x.astype(MX)
    r1 = x - hi.astype(F32)
    mid = r1.astype(MX)
    lo = (r1 - mid.astype(F32)).astype(MX)
    return hi, mid, lo


def _chunk_pairs(rows, rev, strict):
    r = lax.broadcasted_iota(jnp.int32, (rows, rows), 0)
    c = lax.broadcasted_iota(jnp.int32, (rows, rows), 1)
    order = ((c > r) if strict else (c >= r)) if rev else ((c < r) if strict else (c <= r))
    return (r // GLA_CHUNK == c // GLA_CHUNK) & order


def _cums_impl(x, rev):
    rows, w = x.shape
    t = jnp.where(_chunk_pairs(rows, rev, False), 1.0, 0.0).astype(MX)
    s = jnp.dot(t, jnp.concatenate(_split3(x), axis=1), preferred_element_type=F32)
    return s[:, 0:w] + s[:, w:2 * w] + s[:, 2 * w:3 * w]


@functools.partial(jax.custom_vjp, nondiff_argnums=(1,))
def _cums(x, rev):
    return _cums_impl(x, rev)


_cums.defvjp(lambda x, rev: (_cums_impl(x, rev), None), lambda rev, r, g: (_cums_impl(g, not rev),))


def _ln_fwd(s, g, b):
    mu = jnp.mean(s, axis=-1, keepdims=True)
    xc = s - mu
    var = jnp.mean(xc * xc, axis=-1, keepdims=True)
    return xc * lax.rsqrt(var + LN_EPS) * g + b


def _ln_bwd(dy, s, g):
    mu = jnp.mean(s, axis=-1, keepdims=True)
    xc = s - mu
    var = jnp.mean(xc * xc, axis=-1, keepdims=True)
    rstd = lax.rsqrt(var + LN_EPS)
    xhat = xc * rstd
    dxh = dy * g
    ds = rstd * (dxh - jnp.mean(dxh, axis=-1, keepdims=True) - xhat * jnp.mean(dxh * xhat, axis=-1, keepdims=True))
    return ds, jnp.sum(dy * xhat, axis=0, keepdims=True), jnp.sum(dy, axis=0, keepdims=True)


def _sds(shape, dtype=F32):
    return jax.ShapeDtypeStruct(shape, dtype)


_IN_ROW_PIECES = (((0, 0), (0, 456)), ((1, 0), (456, 456)), ((2, 0), (912, 112)), ((2, 112), (1792, 32)),
                  ((2, 144), (1024, 312)), ((3, 0), (1336, 456)))


def _in_rows(g4, behind):
    def body(g_ref, behind_ref, o_ref, tmp):
        tmp[DIN:DINP] = jnp.zeros((DINP - DIN, D), F32)
        for (j, s0), (d0, n_) in _IN_ROW_PIECES:
            tmp[d0:d0 + n_] = g_ref[j, s0:s0 + n_].astype(F32)
        o_ref[...] = tmp[...].astype(MX)

    vm = pl.BlockSpec(memory_space=pltpu.VMEM)
    return pl.pallas_call(body, in_specs=[vm, pl.BlockSpec(memory_space=pl.ANY)], out_specs=vm,
                          out_shape=_sds((DINP, D), MX), scratch_shapes=[pltpu.VMEM((DINP, D), F32)], name="in_rows",
                          compiler_params=pltpu.CompilerParams(vmem_limit_bytes=VMEM_LIMIT))(g4, behind)


def _inproj_fwd(x, wt):
    tm = 512

    def body(x_ref, w_ref, h_ref):
        h_ref[...] = _mm_nt(x_ref[...], w_ref[...])

    return pl.pallas_call(
        body, grid=(N // tm,),
        in_specs=[pl.BlockSpec((tm, D), lambda i: (i, 0)), pl.BlockSpec((DINP, D), lambda i: (0, 0))],
        out_specs=pl.BlockSpec((tm, DINP), lambda i: (i, 0)),
        out_shape=_sds((N, DINP)), name="inproj_fwd", compiler_params=_cp(("parallel",)))(x, wt)


def _inproj_bwd(x, w, dxp, du2, dud, gq_f, gq_b, gk_f, gk_b, gv_f, gv_b, gr, daq, dakv, dhl):
    tm = INPROJ_BWD_TM
    nt = N // tm

    def body(x_ref, w_ref, dxp_ref, du2_ref, dud_ref, gqf, gqb, gkf, gkb, gvf, gvb, gr_ref, daq_ref, dakv_ref, dhl_ref,
             dx_ref, dw_ref, acc):
        i = pl.program_id(0)
        f = lambda r: r[...].astype(F32)
        dh = jnp.concatenate([
            du2_ref[0] + du2_ref[1] + f(dud_ref), f(gqf) + f(gqb), f(gkf) + f(gkb), f(gvf) + f(gvb),
            f(gr_ref), f(daq_ref), f(dakv_ref), f(dhl_ref)], axis=1)
        dx_ref[...] = dxp_ref[...] + _mm(dh, w_ref[...])
        contrib = _mm_tn(dh, x_ref[...])

        @pl.when(i == 0)
        def _():
            acc[...] = contrib

        @pl.when(i > 0)
        def _():
            acc[...] += contrib

        @pl.when(i == nt - 1)
        def _():
            for (j, d0), (s0, n_) in _IN_ROW_PIECES:
                dw_ref[j, d0:d0 + n_] = acc[s0:s0 + n_].astype(MX)

    row = lambda w_: pl.BlockSpec((tm, w_), lambda i: (i, 0))
    return pl.pallas_call(
        body, grid=(nt,),
        in_specs=[row(D), pl.BlockSpec((DINP, D), lambda i: (0, 0)), row(D),
                  pl.BlockSpec((2, tm, 256), lambda i: (0, i, 0)), row(256), row(128), row(128), row(128), row(128),
                  row(256), row(256), row(256), row(512), row(256), row(128)],
        out_specs=[row(D), pl.BlockSpec((NSHARD, DIN // NSHARD, D), lambda i: (0, 0, 0))],
        out_shape=[_sds((N, D)), _sds((NSHARD, DIN // NSHARD, D), MX)],
        scratch_shapes=[pltpu.VMEM((DINP, D), F32)],
        name="inproj_bwd", compiler_params=_cp(("arbitrary",)))(
            x, w, dxp, du2, dud, gq_f, gq_b, gk_f, gk_b, gv_f, gv_b, gr, daq, dakv, dhl)


def _scan_tables(mr, mi, reverse):
    pw = [(mr, mi)]
    for _ in range(7):
        pr, pi = pw[-1]
        pw.append((pr * mr - pi * mi, pr * mi + pi * mr))
    rows = jnp.arange(8)[:, None]
    out = []
    for d in (1, 2, 4):
        keep = rows >= d
        out += [jnp.where(keep, pw[d - 1][0][None], 0.0), jnp.where(keep, pw[d - 1][1][None], 0.0)]
    out += [jnp.stack([p[0] for p in pw]), jnp.stack([p[1] for p in pw])]
    t = jnp.stack(out)
    if reverse:
        t = t[:, ::-1, :]
    return t.reshape(8, 8, 2, SW).transpose(2, 0, 1, 3)


def _tile_scan(xr, xi, a, cr, ci, reverse):
    for lvl, d in enumerate((1, 2, 4)):
        sh = 8 - d if reverse else d
        sr = pltpu.roll(xr, sh, 0)
        si = pltpu.roll(xi, sh, 0)
        ar, ai = a[2 * lvl], a[2 * lvl + 1]
        xr, xi = xr + ar * sr - ai * si, xi + ar * si + ai * sr
    pr, pi = a[6], a[7]
    return xr + pr * cr - pi * ci, xi + pr * ci + pi * cr


NJ = TT // 8


def _lockstep_tables(mr, mi, reverse):
    nr, ni = mr, mi
    for _ in range(NJ.bit_length() - 1):
        nr, ni = nr * nr - ni * ni, 2.0 * nr * ni
    pr, pi = mr[None], mi[None]
    while pr.shape[0] < NJ:
        k = pr.shape[0]
        tr, ti = pr[k - 1], pi[k - 1]
        pr, pi = (jnp.concatenate([pr, pr * tr - pi * ti]), jnp.concatenate([pi, pr * ti + pi * tr]))
    if reverse:
        pr, pi = pr[::-1], pi[::-1]
    rows = jnp.broadcast_to(jnp.stack([mr, mi])[:, None, :], (2, 8, 2 * SW))
    link = _scan_tables(nr, ni, reverse)
    a = jnp.concatenate([rows.reshape(2, 8, 2, SW).transpose(2, 0, 1, 3), link], axis=1)
    return a, jnp.stack([pr, pi]).reshape(2, NJ, 2, SW).transpose(2, 0, 1, 3)


def _to_lockstep(ref, *lead):
    return jnp.concatenate([ref[(*lead, pl.ds(j, 8, stride=NJ), slice(None))] for j in range(NJ)], axis=0)


def _from_lockstep(val, ref, *lead):
    for j in range(NJ):
        ref[(*lead, pl.ds(j, 8, stride=NJ), slice(None))] = val[8 * j:8 * j + 8]


def _expand_powers(p_ref, pexp):
    for c in range(2):
        for j in range(NJ):
            pexp[c, j] = jnp.broadcast_to(p_ref[0, 0, c, j:j + 1, :], (8, SW))


def _lockstep_scan(xre, xim, a_ref, pexp, car, reverse, extra=None):
    a = [a_ref[0, 0, k] for k in range(10)]
    mr, mi = a[0], a[1]
    order = (lambda i: NJ - 1 - i) if reverse else (lambda i: i)

    def local(i, hcar):
        hr, hi = hcar
        r0 = pl.multiple_of(order(i) * 8, 8)
        hr, hi = mr * hr - mi * hi + xre[pl.ds(r0, 8), :], mr * hi + mi * hr + xim[pl.ds(r0, 8), :]
        xre[pl.ds(r0, 8), :] = hr
        xim[pl.ds(r0, 8), :] = hi
        return hr, hi

    z8 = jnp.zeros((8, SW), F32)
    er, ei = lax.fori_loop(0, NJ, local, (z8, z8), unroll=4)
    c0r, c0i = car[0], car[1]
    er, ei = _tile_scan(er, ei, a[2:], c0r, c0i, reverse)
    rowid = lax.broadcasted_iota(jnp.int32, (8, SW), 0)
    first, sh, last = (7, 7, 0) if reverse else (0, 1, 7)
    cvr = jnp.where(rowid == first, c0r, pltpu.roll(er, sh, 0))
    cvi = jnp.where(rowid == first, c0i, pltpu.roll(ei, sh, 0))
    car[0] = jnp.broadcast_to(er[last:last + 1, :], (8, SW))
    car[1] = jnp.broadcast_to(ei[last:last + 1, :], (8, SW))

    def fix(i, carry):
        j = order(i)
        r0 = pl.multiple_of(j * 8, 8)
        pr, pi = pexp[0, j], pexp[1, j]
        sr = xre[pl.ds(r0, 8), :] + pr * cvr - pi * cvi
        si = xim[pl.ds(r0, 8), :] + pr * cvi + pi * cvr
        xre[pl.ds(r0, 8), :] = sr
        xim[pl.ds(r0, 8), :] = si
        if extra is None:
            return carry
        return (sr, si, extra(r0, sr, si, carry[0], carry[1], carry[2]))

    init = (cvr, cvi, extra(None, None, None, None, None, None)) if extra is not None else 0
    return lax.fori_loop(0, NJ, fix, init, unroll=4)


def _s5_time_block(z, s, t, adjoint):
    flip = (1 - z) if adjoint else z
    return s * (L // TT) + t + flip * (L // TT - 1 - 2 * t)


def _s5_fwd(h, bre, bim, cre, cim, tab):
    nt = L // TT
    taba, tabp = tab

    def body(u_ref, bre_ref, bim_ref, cre_ref, cim_ref, a_ref, p_ref, hre_ref, him_ref, y_ref, car, pexp):
        z = pl.program_id(1)
        s = pl.program_id(2)
        tc = pl.program_id(3)

        @pl.when(tc == 0)
        def _():
            car[...] = jnp.zeros_like(car)

        @pl.when((tc == 0) & (s == 0))
        def _():
            _expand_powers(p_ref, pexp)

        u = _to_lockstep(u_ref)
        hre_ref[0] = _mm(u, bre_ref[0, 0])
        him_ref[0] = _mm(u, bim_ref[0, 0])

        @pl.when(z == 0)
        def _():
            _lockstep_scan(hre_ref.at[0], him_ref.at[0], a_ref, pexp, car, False)

        @pl.when(z == 1)
        def _():
            _lockstep_scan(hre_ref.at[0], him_ref.at[0], a_ref, pexp, car, True)

        _from_lockstep(_mm(hre_ref[0], cre_ref[0, 0]) - _mm(him_ref[0], cim_ref[0, 0]), y_ref, 0)

    tb = lambda b, z, s, t: _s5_time_block(z, s, t, False)
    wspec = lambda r, c: pl.BlockSpec((1, 1, r, c), lambda b, z, s, t: (z, b, 0, 0))
    return pl.pallas_call(
        body, grid=(2, 2, NSEQ, nt),
        in_specs=[pl.BlockSpec((TT, 128), lambda b, z, s, t: (tb(b, z, s, t), b)),
                  wspec(128, SW), wspec(128, SW), wspec(SW, 128), wspec(SW, 128),
                  pl.BlockSpec((1, 1, 10, 8, SW), lambda b, z, s, t: (z, b, 0, 0, 0)),
                  pl.BlockSpec((1, 1, 2, NJ, SW), lambda b, z, s, t: (z, b, 0, 0, 0))],
        out_specs=[pl.BlockSpec((1, TT, SW), lambda b, z, s, t: (z, tb(b, z, s, t), b)),
                   pl.BlockSpec((1, TT, SW), lambda b, z, s, t: (z, tb(b, z, s, t), b)),
                   pl.BlockSpec((1, TT, 128), lambda b, z, s, t: (z, tb(b, z, s, t), b))],
        out_shape=[_sds((2, N, 2 * SW)), _sds((2, N, 2 * SW)), _sds((2, N, 256))],
        scratch_shapes=[pltpu.VMEM((2, 8, SW), F32), pltpu.VMEM((2, NJ, 8, SW), F32)],
        name="s5_fwd", compiler_params=_cp(("arbitrary",) * 4))(h, bre, bim, cre, cim, taba, tabp)


def _s5_bwd(h, dyp, hre, him, bre, bim, cre, cim, tabc):
    nt = L // TT
    taba, tabp = tabc

    def body(u_ref, dy_ref, hre_ref, him_ref, bre_ref, bim_ref, cre_ref, cim_ref, a_ref, p_ref,
             du_ref, dbre_ref, dbim_ref, dcre_ref, dcim_ref, dmu_ref, gre, gim, car, acc, macc, pexp):
        z = pl.program_id(1)
        s = pl.program_id(2)
        tc = pl.program_id(3)

        @pl.when(tc == 0)
        def _():
            car[...] = jnp.zeros_like(car)

        @pl.when((tc == 0) & (s == 0))
        def _():
            acc[...] = jnp.zeros_like(acc)
            macc[...] = jnp.zeros_like(macc)
            _expand_powers(p_ref, pexp)

        dy = _to_lockstep(dy_ref)
        gre[...] = _mm_nt(dy, cre_ref[0, 0])
        gim[...] = -_mm_nt(dy, cim_ref[0, 0])

        def run(reverse):
            def pair(r0, gr_, gi_, pvr, pvi, m):
                if r0 is None:
                    return (macc[0], macc[1])
                hr = hre_ref[0, pl.ds(r0, 8), :]
                hi = him_ref[0, pl.ds(r0, 8), :]
                return (m[0] + pvr * hr + pvi * hi, m[1] + pvi * hr - pvr * hi)

            _, _, (dmr, dmi) = _lockstep_scan(gre, gim, a_ref, pexp, car, reverse, pair)
            macc[0] = dmr
            macc[1] = dmi

        @pl.when(z == 0)
        def _():
            run(True)

        @pl.when(z == 1)
        def _():
            run(False)

        gr = gre[...]
        gi = gim[...]
        u = _to_lockstep(u_ref)
        _from_lockstep(_mm_nt(gr, bre_ref[0, 0]) + _mm_nt(gi, bim_ref[0, 0]), du_ref, 0)
        acc[0] += _mm_tn(u, gr)
        acc[1] += _mm_tn(u, gi)
        acc[2] += _mm_tn(dy, hre_ref[0])
        acc[3] -= _mm_tn(dy, him_ref[0])

        @pl.when((tc == nt - 1) & (s == NSEQ - 1))
        def _():
            grp = lax.broadcasted_iota(jnp.int32, (S5_H, SW), 1) // S5_P
            for k, out in enumerate((dbre_ref, dbim_ref, dcre_ref, dcim_ref)):
                c = jnp.zeros((S5_H, SW), F32)
                for i in range(8):
                    c = c + jnp.where(grp == i, acc[k, i * S5_H:(i + 1) * S5_H, :], 0.0)
                out[0, 0] = c
            dmu_ref[0, 0] = jnp.concatenate([jnp.sum(macc[0], axis=0, keepdims=True),
                                             jnp.sum(macc[1], axis=0, keepdims=True)], axis=0)

    tb = lambda b, z, s, t: _s5_time_block(z, s, t, True)
    wspec = lambda r, c: pl.BlockSpec((1, 1, r, c), lambda b, z, s, t: (z, b, 0, 0))
    tok = lambda w_: pl.BlockSpec((TT, w_), lambda b, z, s, t: (tb(b, z, s, t), b))
    st = pl.BlockSpec((1, TT, SW), lambda b, z, s, t: (z, tb(b, z, s, t), b))
    return pl.pallas_call(
        body, grid=(2, 2, NSEQ, nt),
        in_specs=[tok(128), tok(128), st, st, wspec(128, SW), wspec(128, SW), wspec(SW, 128), wspec(SW, 128),
                  pl.BlockSpec((1, 1, 10, 8, SW), lambda b, z, s, t: (z, b, 0, 0, 0)),
                  pl.BlockSpec((1, 1, 2, NJ, SW), lambda b, z, s, t: (z, b, 0, 0, 0))],
        out_specs=[pl.BlockSpec((1, TT, 128), lambda b, z, s, t: (z, tb(b, z, s, t), b)),
                   wspec(S5_H, SW), wspec(S5_H, SW), wspec(S5_H, SW), wspec(S5_H, SW),
                   wspec(2, SW)],
        out_shape=[_sds((2, N, 256))] + [_sds((2, 2, S5_H, SW))] * 4 + [_sds((2, 2, 2, SW))],
        scratch_shapes=[pltpu.VMEM((TT, SW), F32), pltpu.VMEM((TT, SW), F32), pltpu.VMEM((2, 8, SW), F32),
                        pltpu.VMEM((4, 128, SW), F32), pltpu.VMEM((2, 8, SW), F32), pltpu.VMEM((2, NJ, 8, SW), F32)],
        name="s5_bwd", compiler_params=_cp(("arbitrary",) * 4))(h, dyp, hre, him, bre, bim, cre, cim, taba, tabp)


_GELU_C = math.sqrt(2.0 / math.pi)


def _gelu(y):
    return 0.5 * y * (1.0 + jnp.tanh(_GELU_C * (y + 0.044715 * y * y * y)))


def _gelu_grad(y):
    t = jnp.tanh(_GELU_C * (y + 0.044715 * y * y * y))
    return 0.5 * (1.0 + t) + 0.5 * y * (1.0 - t * t) * _GELU_C * (1.0 + 3 * 0.044715 * y * y)


def _glu_halves(w4_ref):
    return (jnp.concatenate([w4_ref[0], w4_ref[1]], axis=1), jnp.concatenate([w4_ref[2], w4_ref[3]], axis=1))


def _s5_glu_fwd(y2, h, dsk, w4, bv, bg):
    tm = 512

    def body(y2_ref, u_ref, d_ref, w4_ref, bv_ref, bg_ref, ya_ref):
        wv, wg = _glu_halves(w4_ref)
        z = _gelu(y2_ref[0] + y2_ref[1] + d_ref[...] * u_ref[...])
        val = _mm(z, wv) + bv_ref[...]
        gate = _mm(z, wg) + bg_ref[...]
        ya_ref[...] = (val * jax.nn.sigmoid(gate)).astype(MX)

    full = lambda r, c: pl.BlockSpec((r, c), lambda i: (0, 0))
    return pl.pallas_call(
        body, grid=(N // tm,),
        in_specs=[pl.BlockSpec((2, tm, 256), lambda i: (0, i, 0)), pl.BlockSpec((tm, 256), lambda i: (i, 0)),
                  full(1, 256), pl.BlockSpec((NSHARD, 256, 128), lambda i: (0, 0, 0)), full(1, 256), full(1, 256)],
        out_specs=pl.BlockSpec((tm, 256), lambda i: (i, 0)),
        out_shape=_sds((N, 256), MX), name="s5_glu_fwd", compiler_params=_cp(("parallel",)))(y2, h, dsk, w4, bv, bg)


def _s5_glu_bwd(y2, h, dsk, w4, bv, bg, dya):
    tm = 512
    nt = N // tm

    def body(y2_ref, u_ref, d_ref, w4_ref, bv_ref, bg_ref, dya_ref,
             dyp_ref, dud_ref, dd_ref, dw4_ref, dbv_ref, dbg_ref, accv, accg):
        i = pl.program_id(0)

        @pl.when(i == 0)
        def _():
            for r in (dd_ref, accv, accg, dbv_ref, dbg_ref):
                r[...] = jnp.zeros_like(r)

        wv, wg = _glu_halves(w4_ref)
        u = u_ref[...]
        y = y2_ref[0] + y2_ref[1] + d_ref[...] * u
        z = _gelu(y)
        val = _mm(z, wv) + bv_ref[...]
        sig = jax.nn.sigmoid(_mm(z, wg) + bg_ref[...])
        dya = dya_ref[...]
        dval = dya * sig
        dgate = dya * val * sig * (1.0 - sig)
        dz = _mm_nt(dval, wv) + _mm_nt(dgate, wg)
        dy = dz * _gelu_grad(y)
        dyp_ref[...] = dy
        dud_ref[...] = (dy * d_ref[...]).astype(MX)
        dd_ref[...] += jnp.sum(dy * u, axis=0, keepdims=True)
        accv[...] += _mm_tn(z, dval)
        accg[...] += _mm_tn(z, dgate)
        dbv_ref[...] += jnp.sum(dval, axis=0, keepdims=True)
        dbg_ref[...] += jnp.sum(dgate, axis=0, keepdims=True)

        @pl.when(i == nt - 1)
        def _():
            dw4_ref[0] = accv[:, 0:128].astype(MX)
            dw4_ref[1] = accv[:, 128:256].astype(MX)
            dw4_ref[2] = accg[:, 0:128].astype(MX)
            dw4_ref[3] = accg[:, 128:256].astype(MX)

    full = lambda r, c: pl.BlockSpec((r, c), lambda i: (0, 0))
    row = pl.BlockSpec((tm, 256), lambda i: (i, 0))
    wspec = pl.BlockSpec((NSHARD, 256, 128), lambda i: (0, 0, 0))
    return pl.pallas_call(
        body, grid=(nt,),
        in_specs=[pl.BlockSpec((2, tm, 256), lambda i: (0, i, 0)), row, full(1, 256), wspec, full(1, 256), full(1, 256),
                  row],
        out_specs=[row, row, full(1, 256), wspec, full(1, 256), full(1, 256)],
        out_shape=[_sds((N, 256)), _sds((N, 256), MX), _sds((1, 256)), _sds((NSHARD, 256, 128), MX), _sds((1, 256)),
                   _sds((1, 256))],
        scratch_shapes=[pltpu.VMEM((256, 256), F32), pltpu.VMEM((256, 256), F32)],
        name="s5_glu_bwd", compiler_params=_cp(("arbitrary",)))(y2, h, dsk, w4, bv, bg, dya)


def _logsig(x):
    return jnp.minimum(x, 0.0) - jnp.log(1.0 + jnp.exp(-jnp.abs(x)))


def _gla_gate_fwd(h, wa, ba):
    tm = 512

    def body(hl_ref, wa_ref, ba_ref, la_ref):
        la_ref[...] = _logsig(_mm(hl_ref[...], wa_ref[...]) + ba_ref[...]) * (1.0 / 16.0)

    return pl.pallas_call(
        body, grid=(N // tm,),
        in_specs=[pl.BlockSpec((tm, 128), lambda i: (i, 14)), pl.BlockSpec((128, 256), lambda i: (0, 0)),
                  pl.BlockSpec((1, 256), lambda i: (0, 0))],
        out_specs=pl.BlockSpec((tm, 256), lambda i: (i, 0)),
        out_shape=_sds((N, 256)), name="gla_gate_fwd", compiler_params=_cp(("parallel",)))(h, wa, ba)


def _gla_gate_bwd(h, wa, ba, dla_f, dla_b):
    tm = 512

    def body(hl_ref, wa_ref, ba_ref, df_ref, db_ref, dhl_ref, dwa_ref, dba_ref):
        i = pl.program_id(0)

        @pl.when(i == 0)
        def _():
            dwa_ref[...] = jnp.zeros_like(dwa_ref)
            dba_ref[...] = jnp.zeros_like(dba_ref)

        hl = hl_ref[...]
        pre = _mm(hl, wa_ref[...]) + ba_ref[...]
        dpre = jnp.concatenate([df_ref[...], db_ref[...]], axis=1) * (1.0 / 16.0) * jax.nn.sigmoid(-pre)
        dhl_ref[...] = _mm_nt(dpre, wa_ref[...]).astype(MX)
        dwa_ref[...] += _mm_tn(hl, dpre)[0:32]
        dba_ref[...] += jnp.sum(dpre, axis=0, keepdims=True)

    row = pl.BlockSpec((tm, 128), lambda i: (i, 0))
    return pl.pallas_call(
        body, grid=(N // tm,),
        in_specs=[pl.BlockSpec((tm, 128), lambda i: (i, 14)), pl.BlockSpec((128, 256), lambda i: (0, 0)),
                  pl.BlockSpec((1, 256), lambda i: (0, 0)), row, row],
        out_specs=[row, pl.BlockSpec((32, 256), lambda i: (0, 0)), pl.BlockSpec((1, 256), lambda i: (0, 0))],
        out_shape=[_sds((N, 128), MX), _sds((32, 256)), _sds((1, 256))],
        name="gla_gate_bwd", compiler_params=_cp(("arbitrary",)))(h, wa, ba, dla_f, dla_b)


def _gla_chunk(q, k, v, la, st, rev):
    c = GLA_CHUNK
    rows = q.shape[0]
    nch = rows // c
    b = _cums(la, rev)
    blc = [jnp.sum(la[i * c:(i + 1) * c], axis=0, keepdims=True) for i in range(nch)]
    bl = jnp.concatenate([jnp.broadcast_to(t, (c, 128)) for t in blc], axis=0)
    q_in = q * (32.0 ** -0.5) * jnp.exp(b)
    k_in = k * jnp.exp(-b)
    k_st = k * jnp.exp(bl - b)
    lane_k = lax.broadcasted_iota(jnp.int32, (1, 128), 1) // 32
    lane_v = lax.broadcasted_iota(jnp.int32, (1, 256), 1) // 64
    qs = jnp.concatenate([jnp.where(lane_k == hd, q_in, 0.0) for hd in range(4)], axis=0)
    a = _dmm_nt(qs, k_in)
    a = jnp.where(jnp.concatenate([_chunk_pairs(rows, rev, rev)] * 4, axis=0), a, 0.0)
    o4 = _dmm(a, v)
    o = jnp.zeros((rows, 256), F32)
    for hd in range(4):
        o = o + jnp.where(lane_v == hd, o4[hd * rows:(hd + 1) * rows], 0.0)
    bd = (lax.broadcasted_iota(jnp.int32, (256, 128), 0) // 64) == (lax.broadcasted_iota(jnp.int32, (256, 128), 1) // 32)
    inter = [None] * nch
    for i in (reversed(range(nch)) if rev else range(nch)):
        sl = slice(i * c, (i + 1) * c)
        inter[i] = _dmm_nt(q_in[sl], st)
        st = jnp.exp(blc[i]) * st + jnp.where(bd, _dmm_tn(v[sl], k_st[sl]), 0.0)
    return o + jnp.concatenate(inter, axis=0), st


def _gla_chunk_of(c, rev):
    return NGROUP - 1 - c if rev else c


def _gla_fwd(h, la2):
    c = GLA_GROUP * GLA_CHUNK

    def body(qf, kf, vf, laf, qb, kb, vb, lab, of_ref, ob_ref, sf_ref, sb_ref, stf, stb):
        @pl.when(pl.program_id(0) == 0)
        def _():
            stf[...] = jnp.zeros_like(stf)
            stb[...] = jnp.zeros_like(stb)

        ins = [(qf[s], kf[s], vf[s], laf[s], stf[s], qb[s], kb[s], vb[s], lab[s], stb[s]) for s in range(NSEQ)]
        outs = [(_gla_chunk(*t[:5], False), _gla_chunk(*t[5:], True)) for t in ins]
        for s in range(NSEQ):
            sf_ref[s, 0] = ins[s][4]
            sb_ref[s, 0] = ins[s][9]
            (of_ref[s], stf[s]), (ob_ref[s], stb[s]) = outs[s]

    def specs(rev):
        ch = lambda i: _gla_chunk_of(i, rev)
        return [pl.BlockSpec((NSEQ, c, 128), lambda i: (0, ch(i), 2)), pl.BlockSpec((NSEQ, c, 128), lambda i: (0, ch(i), 3)),
                pl.BlockSpec((NSEQ, c, 256), lambda i: (0, ch(i), 2)),
                pl.BlockSpec((NSEQ, c, 128), lambda i: (0, ch(i), 1 if rev else 0))]

    orow = lambda rev: pl.BlockSpec((NSEQ, c, 256), lambda i: (0, _gla_chunk_of(i, rev), 0))
    srow = lambda rev: pl.BlockSpec((NSEQ, 1, 256, 128), lambda i: (0, _gla_chunk_of(i, rev), 0, 0))
    h3, la3 = h.reshape(NSEQ, L, DINP), la2.reshape(NSEQ, L, 256)
    of, ob, sf, sb = pl.pallas_call(
        body, grid=(NGROUP,),
        in_specs=specs(False) + specs(True),
        out_specs=[orow(False), orow(True), srow(False), srow(True)],
        out_shape=[_sds((NSEQ, L, 256)), _sds((NSEQ, L, 256)), _sds((NSEQ, NGROUP, 256, 128)),
                   _sds((NSEQ, NGROUP, 256, 128))],
        scratch_shapes=[pltpu.VMEM((NSEQ, 256, 128), F32), pltpu.VMEM((NSEQ, 256, 128), F32)],
        name="gla_fwd", compiler_params=_cp(("arbitrary",)))(h3, h3, h3, la3, h3, h3, h3, la3)
    return of.reshape(N, 256), ob.reshape(N, 256), sf, sb


def _gla_bwd(h, la2, do, sf, sb):
    c = GLA_GROUP * GLA_CHUNK

    def body(qf, kf, vf, laf, dof, sfr, qb, kb, vb, lab, dob, sbr,
             dqf, dkf, dvf, dlf, dqb, dkb, dvb, dlb, dstf, dstb):
        @pl.when(pl.program_id(0) == 0)
        def _():
            dstf[...] = jnp.zeros_like(dstf)
            dstb[...] = jnp.zeros_like(dstb)

        def one(s, q, k, v, la, do_, st, dst, rev):
            _, vjp = jax.vjp(functools.partial(_gla_chunk, rev=rev), q[s], k[s], v[s], la[s], st[s, 0])
            return vjp((do_[s], dst[s]))

        res = [(one(s, qf, kf, vf, laf, dof, sfr, dstf, False), one(s, qb, kb, vb, lab, dob, sbr, dstb, True))
               for s in range(NSEQ)]
        for s in range(NSEQ):
            for (gq, gk, gv, gl, gs), (dq, dk, dv, dl, dst) in ((res[s][0], (dqf, dkf, dvf, dlf, dstf)),
                                                                  (res[s][1], (dqb, dkb, dvb, dlb, dstb))):
                dq[s], dk[s], dv[s] = gq.astype(MX), gk.astype(MX), gv.astype(MX)
                dl[s], dst[s] = gl, gs

    def specs(rev):
        ch = lambda i: _gla_chunk_of(i, not rev)
        return [pl.BlockSpec((NSEQ, c, 128), lambda i: (0, ch(i), 2)), pl.BlockSpec((NSEQ, c, 128), lambda i: (0, ch(i), 3)),
                pl.BlockSpec((NSEQ, c, 256), lambda i: (0, ch(i), 2)),
                pl.BlockSpec((NSEQ, c, 128), lambda i: (0, ch(i), 1 if rev else 0)),
                pl.BlockSpec((NSEQ, c, 256), lambda i: (0, ch(i), 0)),
                pl.BlockSpec((NSEQ, 1, 256, 128), lambda i: (0, ch(i), 0, 0))]

    def ospecs(rev):
        ch = lambda i: _gla_chunk_of(i, not rev)
        n = pl.BlockSpec((NSEQ, c, 128), lambda i: (0, ch(i), 0))
        return [n, n, pl.BlockSpec((NSEQ, c, 256), lambda i: (0, ch(i), 0)), n]

    oshape = [_sds((NSEQ, L, 128), MX), _sds((NSEQ, L, 128), MX), _sds((NSEQ, L, 256), MX), _sds((NSEQ, L, 128))]
    h3, la3, do3 = h.reshape(NSEQ, L, DINP), la2.reshape(NSEQ, L, 256), do.reshape(NSEQ, L, 256)
    res = pl.pallas_call(
        body, grid=(NGROUP,),
        in_specs=specs(False) + specs(True),
        out_specs=ospecs(False) + ospecs(True),
        out_shape=oshape + oshape,
        scratch_shapes=[pltpu.VMEM((NSEQ, 256, 128), F32), pltpu.VMEM((NSEQ, 256, 128), F32)],
        name="gla_bwd", compiler_params=_cp(("arbitrary",)))(h3, h3, h3, la3, do3, sf, h3, h3, h3, la3, do3, sb)
    return [r.reshape(N, r.shape[-1]) for r in res]


def _gla_post(of, ob, r, g):
    o = of + ob
    head = lax.broadcasted_iota(jnp.int32, (1, 256), 1) // 64
    mu = jnp.zeros_like(o)
    for hd in range(4):
        mu = mu + jnp.where(head == hd, jnp.sum(jnp.where(head == hd, o, 0.0), axis=-1, keepdims=True) * (1.0 / 64.0), 0.0)
    xc = o - mu
    var = jnp.zeros_like(o)
    for hd in range(4):
        var = var + jnp.where(head == hd, jnp.sum(jnp.where(head == hd, xc * xc, 0.0), axis=-1, keepdims=True) * (1.0 / 64.0), 0.0)
    return xc * lax.rsqrt(var + LN_EPS) * g * (r * jax.nn.sigmoid(r))


def _gla_post_fwd(of, ob, h, g):
    tm = 512

    def body(of_ref, ob_ref, r_ref, g_ref, y_ref):
        y_ref[...] = _gla_post(of_ref[...], ob_ref[...], r_ref[...], g_ref[...]).astype(MX)

    row = pl.BlockSpec((tm, 256), lambda i: (i, 0))
    return pl.pallas_call(
        body, grid=(N // tm,),
        in_specs=[row, row, pl.BlockSpec((tm, 256), lambda i: (i, 3)), pl.BlockSpec((1, 256), lambda i: (0, 0))],
        out_specs=row, out_shape=_sds((N, 256), MX), name="gla_post_fwd", compiler_params=_cp(("parallel",)))(of, ob, h, g)


def _gla_post_bwd(of, ob, h, g, dyb):
    tm = 512

    def body(of_ref, ob_ref, r_ref, g_ref, dy_ref, do_ref, dr_ref, dg_ref):
        @pl.when(pl.program_id(0) == 0)
        def _():
            dg_ref[...] = jnp.zeros_like(dg_ref)

        _, vjp = jax.vjp(_gla_post, of_ref[...], ob_ref[...], r_ref[...], g_ref[...])
        go, _, gr, gg = vjp(dy_ref[...])
        do_ref[...] = go
        dr_ref[...] = gr.astype(MX)
        dg_ref[...] += gg

    row = pl.BlockSpec((tm, 256), lambda i: (i, 0))
    one = pl.BlockSpec((1, 256), lambda i: (0, 0))
    return pl.pallas_call(
        body, grid=(N // tm,),
        in_specs=[row, row, pl.BlockSpec((tm, 256), lambda i: (i, 3)), one, row],
        out_specs=[row, row, one], out_shape=[_sds((N, 256)), _sds((N, 256), MX), _sds((1, 256))],
        name="gla_post_bwd", compiler_params=_cp(("arbitrary",)))(of, ob, h, g, dyb)


def _rope_tables(width):
    pos = jnp.arange(L, dtype=F32)
    inv_freq = ROPE_THETA ** (-jnp.arange(0, ROT, 2, dtype=F32) / ROT)
    ang = pos[:, None] * inv_freq[None, :]
    cos, sin = jnp.cos(ang), jnp.sin(ang)
    one = jnp.ones((L, 64 - ROT), F32)
    zero = jnp.zeros((L, 64 - ROT), F32)
    z8 = jnp.zeros((L, ROT // 2), F32)
    c = jnp.concatenate([cos, cos, one], axis=1)
    sa = jnp.concatenate([z8, sin, zero], axis=1)
    sb = jnp.concatenate([-sin, z8, zero], axis=1)
    rep = width // 64
    return jnp.stack([jnp.tile(c, (1, rep)), jnp.tile(sa, (1, rep)), jnp.tile(sb, (1, rep))])


def _pieces(t, f):
    out = [f(t[:, c * 128:(c + 1) * 128]) for c in range(t.shape[-1] // 128)]
    return out[0] if len(out) == 1 else jnp.concatenate(out, axis=1)


def _rope(t, tab):
    return _pieces(t, lambda x: x * tab[0] + pltpu.roll(x, ROT // 2, 1) * tab[1] + pltpu.roll(x, 128 - ROT // 2, 1) * tab[2])


def _rope_t(g, tab):
    return _pieces(g, lambda x: x * tab[0] + pltpu.roll(x * tab[1], 128 - ROT // 2, 1) + pltpu.roll(x * tab[2], ROT // 2, 1))


def _swa_pad_kv(kv_ref, tk_ref, kexp, vexp):
    z = jnp.zeros((SWA_BLK, 256), F32)
    kr = _rope(kv_ref[:, 0:128], tk_ref[...])
    for hk in range(2):
        for pad in (kexp, vexp):
            pad[hk, 0:SWA_BLK] = z
            pad[hk, SWA_BLK + L:] = z
        kexp[hk, SWA_BLK:SWA_BLK + L] = _swa_expand(kr, hk)
        vexp[hk, SWA_BLK:SWA_BLK + L] = _swa_expand(kv_ref[:, 128:256], hk)


def _swa_expand(x, hk):
    lane = lax.broadcasted_iota(jnp.int32, x.shape, 1)
    sw = pltpu.roll(x, 64, 1)
    pair = jnp.where(lane < 64, x, sw) if hk == 0 else jnp.where(lane < 64, sw, x)
    return jnp.concatenate([pair, pair], axis=1)


def _swa_fold(x, hk):
    a = x[:, 0:128] + x[:, 128:256]
    t = a + pltpu.roll(a, 64, 1)
    lane = lax.broadcasted_iota(jnp.int32, a.shape, 1)
    return jnp.where((lane < 64) if hk == 0 else (lane >= 64), t, 0.0)


def _swa_probs(q2, kexp, n, sink_ref, hk):
    slot = lax.broadcasted_iota(jnp.int32, (1, 256), 1) // 64
    qs = jnp.concatenate([jnp.where(slot == g, q2, 0.0) for g in range(4)], axis=0)
    s = _mm_nt(qs, kexp) * 0.125
    i = lax.broadcasted_iota(jnp.int32, (SWA_BLK, 3 * SWA_BLK), 0)
    j = lax.broadcasted_iota(jnp.int32, (SWA_BLK, 3 * SWA_BLK), 1)
    kpos = n * SWA_BLK - SWA_BLK + j
    ok = (j - i >= 0) & (j - i <= 2 * SWA_BLK) & (kpos >= 0) & (kpos < L)
    s = jnp.where(jnp.concatenate([ok] * 4, axis=0), s, NEG_BIG)
    rowg = lax.broadcasted_iota(jnp.int32, (4 * SWA_BLK, 1), 0) // SWA_BLK
    sink = jnp.zeros((4 * SWA_BLK, 1), F32)
    for g in range(4):
        sink = jnp.where(rowg == g, sink_ref[hk * 4 + g], sink)
    m = jnp.maximum(jnp.max(s, axis=-1, keepdims=True), sink)
    p = jnp.exp(s - m)
    ps = jnp.exp(sink - m)
    inv = 1.0 / (jnp.sum(p, axis=-1, keepdims=True) + ps)
    return qs, p * inv, ps * inv, slot, rowg


def _swa_qtab(tk_ref, r0):
    return [tk_ref[i, pl.ds(r0, SWA_BLK), :] for i in range(3)]


def _swa_fwd(h, tk, sink):
    def body(sink_ref, q_ref, kv_ref, tk_ref, y_ref, kexp, vexp):
        n = pl.program_id(1)

        @pl.when(n == 0)
        def _():
            _swa_pad_kv(kv_ref, tk_ref, kexp, vexp)

        r0 = pl.multiple_of(n * SWA_BLK, SWA_BLK)
        q = _rope(q_ref[...], _swa_qtab(tk_ref, r0))
        for hk in range(2):
            _, p, _, slot, _ = _swa_probs(q[:, hk * 256:(hk + 1) * 256], kexp[hk, pl.ds(r0, 3 * SWA_BLK), :], n,
                                          sink_ref, hk)
            o4 = _mm(p, vexp[hk, pl.ds(r0, 3 * SWA_BLK), :])
            o = jnp.zeros((SWA_BLK, 256), F32)
            for g in range(4):
                o = o + jnp.where(slot == g, o4[g * SWA_BLK:(g + 1) * SWA_BLK], 0.0)
            y_ref[:, hk * 256:(hk + 1) * 256] = o.astype(MX)

    return pl.pallas_call(
        body,
        grid_spec=pltpu.PrefetchScalarGridSpec(
            num_scalar_prefetch=1, grid=(NSEQ, NBLK),
            in_specs=[pl.BlockSpec((SWA_BLK, 512), lambda s, n, sk: (s * NBLK + n, 2)),
                      pl.BlockSpec((L, 256), lambda s, n, sk: (s, 6)),
                      pl.BlockSpec((3, L, 128), lambda s, n, sk: (0, 0, 0))],
            out_specs=pl.BlockSpec((SWA_BLK, 512), lambda s, n, sk: (s * NBLK + n, 0)),
            scratch_shapes=[pltpu.VMEM((2, L + 2 * SWA_BLK, 256), F32), pltpu.VMEM((2, L + 2 * SWA_BLK, 256), F32)]),
        out_shape=_sds((N, 512), MX), name="swa_fwd", compiler_params=_cp(("arbitrary", "arbitrary")))(sink, h, h, tk)


def _swa_bwd(h, tk, sink, dyc):
    def body(sink_ref, q_ref, kv_ref, tk_ref, dy_ref, dq_ref, dkv_ref, dsink_ref, kexp_all, vexp_all, dkacc, dvacc):
        sq = pl.program_id(0)
        n = pl.program_id(1)

        @pl.when(n == 0)
        def _():
            _swa_pad_kv(kv_ref, tk_ref, kexp_all, vexp_all)
            dkacc[...] = jnp.zeros_like(dkacc)
            dvacc[...] = jnp.zeros_like(dvacc)

        @pl.when((n == 0) & (sq == 0))
        def _():
            dsink_ref[...] = jnp.zeros_like(dsink_ref)

        r0 = pl.multiple_of(n * SWA_BLK, SWA_BLK)
        tq = _swa_qtab(tk_ref, r0)
        q = _rope(q_ref[...], tq)
        hrow = lax.broadcasted_iota(jnp.int32, (8, 128), 0)
        dsk = jnp.zeros((8, 128), F32)
        for hk in range(2):
            kexp = kexp_all[hk, pl.ds(r0, 3 * SWA_BLK), :]
            vexp = vexp_all[hk, pl.ds(r0, 3 * SWA_BLK), :]
            qs, p, ps, slot, rowg = _swa_probs(q[:, hk * 256:(hk + 1) * 256], kexp, n, sink_ref, hk)
            dy2 = dy_ref[:, hk * 256:(hk + 1) * 256]
            dos = jnp.concatenate([jnp.where(slot == g, dy2, 0.0) for g in range(4)], axis=0)
            dp = _mm_nt(dos, vexp)
            delta = jnp.sum(p * dp, axis=-1, keepdims=True)
            ds = p * (dp - delta) * 0.125
            dsr = -ps * delta
            for g in range(4):
                dsk = dsk + jnp.where(hrow == hk * 4 + g, jnp.sum(jnp.where(rowg == g, dsr, 0.0), axis=0, keepdims=True), 0.0)
            dq4 = _mm(ds, kexp)
            dq2 = jnp.zeros((SWA_BLK, 256), F32)
            for g in range(4):
                dq2 = dq2 + jnp.where(slot == g, dq4[g * SWA_BLK:(g + 1) * SWA_BLK], 0.0)
            dq_ref[:, hk * 256:(hk + 1) * 256] = _rope_t(dq2, tq).astype(MX)
            dkacc[hk, pl.ds(r0, 3 * SWA_BLK), :] += _mm_tn(ds, qs)
            dvacc[hk, pl.ds(r0, 3 * SWA_BLK), :] += _mm_tn(p, dos)
        dsink_ref[...] += dsk

        @pl.when(n == NBLK - 1)
        def _():
            seq = slice(SWA_BLK, SWA_BLK + L)
            dk = _rope_t(_swa_fold(dkacc[0, seq], 0) + _swa_fold(dkacc[1, seq], 1), tk_ref[...])
            dkv_ref[:, 0:128] = dk.astype(MX)
            dkv_ref[:, 128:256] = (_swa_fold(dvacc[0, seq], 0) + _swa_fold(dvacc[1, seq], 1)).astype(MX)

    blk = lambda col: pl.BlockSpec((SWA_BLK, 512), lambda s, n, sk: (s * NBLK + n, col))
    pad = pltpu.VMEM((2, L + 2 * SWA_BLK, 256), F32)
    return pl.pallas_call(
        body,
        grid_spec=pltpu.PrefetchScalarGridSpec(
            num_scalar_prefetch=1, grid=(NSEQ, NBLK),
            in_specs=[blk(2), pl.BlockSpec((L, 256), lambda s, n, sk: (s, 6)),
                      pl.BlockSpec((3, L, 128), lambda s, n, sk: (0, 0, 0)), blk(0)],
            out_specs=[blk(0), pl.BlockSpec((L, 256), lambda s, n, sk: (s, 0)),
                       pl.BlockSpec((8, 128), lambda s, n, sk: (0, 0))],
            scratch_shapes=[pad, pad, pad, pad]),
        out_shape=[_sds((N, 512), MX), _sds((N, 256), MX), _sds((8, 128))],
        name="swa_bwd", compiler_params=_cp(("arbitrary", "arbitrary")))(sink, h, h, tk, dyc)


def _outproj_fwd(ya, yb, yc, x, wo, g, b):
    tm = 512

    def body(ya_ref, yb_ref, yc_ref, x_ref, wo_ref, g_ref, b_ref, s_ref, x1_ref):
        mix = _mm(ya_ref[...], wo_ref[0:256]) + _mm(yb_ref[...], wo_ref[256:512]) + _mm(yc_ref[...], wo_ref[512:1024])
        s = ALPHA * x_ref[...] + mix
        s_ref[...] = s
        x1_ref[...] = _ln_fwd(s, g_ref[...], b_ref[...])

    row = lambda w_: pl.BlockSpec((tm, w_), lambda i: (i, 0))
    one = pl.BlockSpec((1, D), lambda i: (0, 0))
    return pl.pallas_call(
        body, grid=(N // tm,),
        in_specs=[row(256), row(256), row(512), row(D), pl.BlockSpec((D, D), lambda i: (0, 0)), one, one],
        out_specs=[row(D), row(D)], out_shape=[_sds((N, D)), _sds((N, D))],
        name="outproj_fwd", compiler_params=_cp(("parallel",)))(ya, yb, yc, x, wo, g, b)


def _outproj_bwd(dx1, s1, ya, yb, yc, wo, g):
    tm = 512
    nt = N // tm

    def body(dx1_ref, s_ref, ya_ref, yb_ref, yc_ref, wo_ref, g_ref,
             dya_ref, dyb_ref, dyc_ref, dxp_ref, dwo_ref, dg_ref, db_ref, acc):
        i = pl.program_id(0)

        @pl.when(i == 0)
        def _():
            acc[...] = jnp.zeros_like(acc)
            dg_ref[...] = jnp.zeros_like(dg_ref)
            db_ref[...] = jnp.zeros_like(db_ref)

        ds, dg, db = _ln_bwd(dx1_ref[...], s_ref[...], g_ref[...])
        dg_ref[...] += dg
        db_ref[...] += db
        dxp_ref[...] = ALPHA * ds
        dy = _mm_nt(ds, wo_ref[...])
        dya_ref[...] = dy[:, 0:256]
        dyb_ref[...] = dy[:, 256:512]
        dyc_ref[...] = dy[:, 512:1024]
        acc[0:256] += _mm_tn(ya_ref[...], ds)
        acc[256:512] += _mm_tn(yb_ref[...], ds)
        acc[512:1024] += _mm_tn(yc_ref[...], ds)

        @pl.when(i == nt - 1)
        def _():
            dwo_ref[...] = acc[...].astype(MX)

    row = lambda w_: pl.BlockSpec((tm, w_), lambda i: (i, 0))
    one = pl.BlockSpec((1, D), lambda i: (0, 0))
    full = pl.BlockSpec((D, D), lambda i: (0, 0))
    return pl.pallas_call(
        body, grid=(nt,),
        in_specs=[row(D), row(D), row(256), row(256), row(512), full, one],
        out_specs=[row(256), row(256), row(512), row(D), full, one, one],
        out_shape=[_sds((N, 256)), _sds((N, 256)), _sds((N, 512)), _sds((N, D)), _sds((D, D), MX), _sds((1, D)), _sds((1, D))],
        scratch_shapes=[pltpu.VMEM((D, D), F32)],
        name="outproj_bwd", compiler_params=_cp(("arbitrary",)))(dx1, s1, ya, yb, yc, wo, g)


def _mix_ffn_fwd(ya, yb, yc, x, wo, g1, b1, w1, w2, g, b, target=None):
    tm = FFN_TM
    head = target is not None

    def body(*refs):
        ya_ref, yb_ref, yc_ref, xin_ref, wo_ref, g1_ref, b1_ref, w1_ref, w2_ref, g_ref, b_ref = refs[:11]
        s1_ref, x1_ref, a_ref, s_ref, y_ref = refs[11 + head:16 + head]
        mix = _mm(ya_ref[...], wo_ref[0:256]) + _mm(yb_ref[...], wo_ref[256:512]) + _mm(yc_ref[...], wo_ref[512:1024])
        s1 = ALPHA * xin_ref[...] + mix
        s1_ref[...] = s1
        x = _ln_fwd(s1, g1_ref[...], b1_ref[...])
        x1_ref[...] = x
        xb = x.astype(MX)
        s = ALPHA * x
        for j in range(NSHARD):
            a = _mm(xb, w1_ref[j])
            a_ref[:, j * D:(j + 1) * D] = a.astype(MX)
            s = s + _mm(jnp.square(jnp.maximum(a, 0.0)), w2_ref[j])
        s_ref[...] = s
        x2 = _ln_fwd(s, g_ref[...], b_ref[...])
        if not head:
            y_ref[...] = x2
            return
        l_ref = refs[-1]

        @pl.when(pl.program_id(0) == 0)
        def _():
            l_ref[...] = jnp.zeros_like(l_ref)

        e = x2 - refs[11][...]
        y_ref[...] = e * (1.0 / D)
        l_ref[...] += jnp.sum(jnp.sum(e * e, axis=1, keepdims=True), axis=0, keepdims=True) * (0.5 / D)

    rw = lambda w_: pl.BlockSpec((tm, w_), lambda i: (i, 0))
    row = rw(D)
    once = dict(pipeline_mode=pl.Buffered(1))
    wall = pl.BlockSpec((NSHARD, D, D), lambda i: (0, 0, 0), **once)
    one = pl.BlockSpec((1, D), lambda i: (0, 0))
    acc = pl.BlockSpec((8, 128), lambda i: (0, 0))
    return pl.pallas_call(
        body, grid=(N // tm,),
        in_specs=[rw(256), rw(256), rw(512), row, pl.BlockSpec((D, D), lambda i: (0, 0), **once), one, one,
                  wall, wall, one, one] + [row] * head,
        out_specs=[row, row, pl.BlockSpec((tm, DFF), lambda i: (i, 0)), row, row] + [acc] * head,
        out_shape=[_sds((N, D)), _sds((N, D)), _sds((N, DFF), MX), _sds((N, D)), _sds((N, D))] + [_sds((8, 128))] * head,
        name="mix_ffn_fwd", compiler_params=_cp(("arbitrary",), FFN_VMEM))(
            ya, yb, yc, x, wo, g1, b1, w1, w2, g, b, *([target] * head))


def _ffn_bwd_act(dy, s2, a, w1, w2, g):
    tm = FFN_TM

    def body(dy_ref, s_ref, a_ref, w1_ref, w2_ref, g_ref, da_ref, ds_ref, dx1_ref, dg_ref, db_ref):
        @pl.when(pl.program_id(0) == 0)
        def _():
            dg_ref[...] = jnp.zeros_like(dg_ref)
            db_ref[...] = jnp.zeros_like(db_ref)

        ds, dg, db = _ln_bwd(dy_ref[...], s_ref[...], g_ref[...])
        dsb = ds.astype(MX)
        ds_ref[...] = dsb
        dg_ref[...] += dg
        db_ref[...] += db
        dx1 = ALPHA * ds
        for j in range(NSHARD):
            da = (_mm_nt(dsb, w2_ref[j]) * 2.0 * jnp.maximum(a_ref[:, j * D:(j + 1) * D].astype(F32), 0.0)).astype(MX)
            da_ref[:, j * D:(j + 1) * D] = da
            dx1 = dx1 + _mm_nt(da, w1_ref[j])
        dx1_ref[...] = dx1

    row = pl.BlockSpec((tm, D), lambda i: (i, 0))
    wide = pl.BlockSpec((tm, DFF), lambda i: (i, 0))
    wall = pl.BlockSpec((NSHARD, D, D), lambda i: (0, 0, 0))
    one = pl.BlockSpec((1, D), lambda i: (0, 0))
    return pl.pallas_call(
        body, grid=(N // tm,),
        in_specs=[row, row, wide, wall, wall, one],
        out_specs=[wide, row, row, one, one],
        out_shape=[_sds((N, DFF), MX), _sds((N, D), MX), _sds((N, D)), _sds((1, D)), _sds((1, D))],
        name="ffn_bwd_act", compiler_params=_cp(("arbitrary",), FFN_VMEM))(dy, s2, a, w1, w2, g)


def _ffn_bwd_w(x1, da, a, ds):
    tm, nb = FFN_TM_W, FFN_WB
    nt = N // tm

    def body(x_ref, da_ref, a_ref, ds_ref, dw1_ref, dw2_ref, acc1, acc2):
        i = pl.program_id(1)

        @pl.when(i == 0)
        def _():
            acc1[...] = jnp.zeros_like(acc1)
            acc2[...] = jnp.zeros_like(acc2)

        x, ds_ = x_ref[...], ds_ref[...]
        for k in range(nb):
            cols = slice(k * D, (k + 1) * D)
            acc1[k] += _mm_tn(x, da_ref[:, cols])
            acc2[k] += _mm_tn(jnp.square(jnp.maximum(a_ref[:, cols].astype(F32), 0.0)), ds_)

        @pl.when(i == nt - 1)
        def _():
            dw1_ref[...] = acc1[...].astype(MX)
            dw2_ref[...] = acc2[...].astype(MX)

    row = pl.BlockSpec((tm, D), lambda j, i: (i, 0))
    col = pl.BlockSpec((tm, nb * D), lambda j, i: (i, j))
    wj = pl.BlockSpec((nb, D, D), lambda j, i: (j, 0, 0))
    return pl.pallas_call(
        body, grid=(NSHARD // nb, nt),
        in_specs=[row, col, col, row], out_specs=[wj, wj],
        out_shape=[_sds((NSHARD, D, D), MX), _sds((NSHARD, D, D), MX)],
        scratch_shapes=[pltpu.VMEM((nb, D, D), F32), pltpu.VMEM((nb, D, D), F32)],
        name="ffn_bwd_w", compiler_params=_cp(("parallel", "arbitrary"), FFN_VMEM))(x1, da, a, ds)


def _loss_head(y, target):
    tm = 512

    def body(y_ref, t_ref, dy_ref, l_ref):
        @pl.when(pl.program_id(0) == 0)
        def _():
            l_ref[...] = jnp.zeros_like(l_ref)

        e = y_ref[...] - t_ref[...]
        dy_ref[...] = e * (1.0 / D)
        l_ref[...] += jnp.sum(jnp.sum(e * e, axis=1, keepdims=True), axis=0, keepdims=True) * (0.5 / D)

    row = pl.BlockSpec((tm, D), lambda i: (i, 0))
    return pl.pallas_call(
        body, grid=(N // tm,), in_specs=[row, row],
        out_specs=[row, pl.BlockSpec((8, 128), lambda i: (0, 0))],
        out_shape=[_sds((N, D)), _sds((8, 128))], name="loss_head", compiler_params=_cp(("arbitrary",)))(y, target)


def _s5_discretize(a_re, a_im, log_step, b_re, b_im):
    lam = lax.complex(a_re, a_im)
    lam_bar = jnp.exp(lam * jnp.exp(log_step))
    b_bar = ((lam_bar - 1.0) / lam)[..., None] * lax.complex(b_re, b_im)
    return jnp.real(lam_bar), jnp.imag(lam_bar), jnp.real(b_bar), jnp.imag(b_bar)


def _s5_in_blocks(b):
    e = jnp.eye(8, dtype=F32)
    return jnp.einsum('ij,zbjph->zbihjp', e, b.reshape(2, 2, 8, S5_P, S5_H)).reshape(2, 2, 128, SW)


def _s5_in_unblocks(d):
    return jnp.einsum('zbihip->zbiph', d.reshape(2, 2, 8, S5_H, 8, S5_P)).reshape(2, S5_G, S5_P, S5_H)


def _s5_out_blocks(c):
    e = jnp.eye(8, dtype=F32)
    return jnp.einsum('ij,zbjhp->zbjpih', e, c.reshape(2, 2, 8, S5_H, S5_P)).reshape(2, 2, SW, 128)


def _s5_out_unblocks(d):
    return jnp.einsum('zbipih->zbihp', d.reshape(2, 2, 8, S5_P, 8, S5_H)).reshape(2, S5_G, S5_H, S5_P)


def _gate_weight(w_a):
    z = jnp.zeros((16, 128), F32)
    top = jnp.concatenate([w_a[0], z], axis=1)
    bot = jnp.concatenate([z, w_a[1]], axis=1)
    return jnp.concatenate([top, bot, jnp.zeros((96, 256), F32)], axis=0)


def _layer_prep(p):
    lr, li, br, bi = _s5_discretize(p["s5_a_re"], p["s5_a_im"], p["s5_log_step"], p["s5_b_re"], p["s5_b_im"])
    q = dict(p)
    q["bre"] = _s5_in_blocks(br).astype(MX)
    q["bim"] = _s5_in_blocks(bi).astype(MX)
    q["cre"] = _s5_out_blocks(p["s5_c_re"]).astype(MX)
    q["cim"] = _s5_out_blocks(p["s5_c_im"]).astype(MX)
    mr, mi = lr.reshape(2, 1024), li.reshape(2, 1024)
    both = lambda t0, t1: tuple(jnp.stack(p) for p in zip(t0, t1))
    q["tab"] = both(_lockstep_tables(mr[0], mi[0], False), _lockstep_tables(mr[1], mi[1], True))
    q["tabc"] = both(_lockstep_tables(mr[0], -mi[0], True), _lockstep_tables(mr[1], -mi[1], False))
    q["dsk"] = p["s5_d"].reshape(1, 256)
    q["wa"] = _gate_weight(p["gla_w_a"]).astype(MX)
    q["ba"] = p["gla_b_a"].reshape(1, 256)
    q["lng"] = p["gla_ln_g"].reshape(1, 256)
    q["bv"] = p["s5_b_glu"][:256].reshape(1, 256)
    q["bg"] = p["s5_b_glu"][256:].reshape(1, 256)
    for k in ("ln1_g", "ln1_b", "ln2_g", "ln2_b"):
        q[k] = p[k].reshape(1, D)
    return q


def _layer_fwd(x, q, tk, fetch, target=None):
    q["w_in"] = fetch("w_in", x)
    h = _inproj_fwd(x, q["w_in"])
    hre, him, y2 = _s5_fwd(h, q["bre"], q["bim"], q["cre"], q["cim"], q["tab"])
    q["w4"] = fetch("s5_w_glu", y2)
    ya = _s5_glu_fwd(y2, h, q["dsk"], q["w4"], q["bv"], q["bg"])
    la2 = _gla_gate_fwd(h, q["wa"], q["ba"])
    of, ob, sf, sb = _gla_fwd(h, la2)
    yb = _gla_post_fwd(of, ob, h, q["lng"])
    yc = _swa_fwd(h, tk, q["swa_sink"])
    mixed = ya[:8, :128] + yb[:8, :128] + yc[:8, :128]
    q["w_out"] = fetch("w_out", mixed)
    q["w_ff1"] = fetch("w_ff1", mixed)
    q["w_ff2"] = fetch("w_ff2", mixed)
    s1, x1, a, s2, *out = _mix_ffn_fwd(ya, yb, yc, x, q["w_out"], q["ln1_g"], q["ln1_b"], q["w_ff1"], q["w_ff2"],
                                       q["ln2_g"], q["ln2_b"], target)
    saved = dict(x=x, h=h, hre=hre, him=him, y2=y2, ya=ya, la2=la2, of=of, ob=ob, sf=sf, sb=sb, yb=yb, yc=yc,
                 s1=s1, x1=x1, a=a, s2=s2)
    return (out[0] if target is None else tuple(out)), saved


def _layer_bwd(dy, q, sv, tk, emit):
    g = {}
    da, ds2, dx1, g["dg2"], g["db2"] = _ffn_bwd_act(dy, sv["s2"], sv["a"], q["w_ff1"], q["w_ff2"], q["ln2_g"])
    dw1, dw2 = _ffn_bwd_w(sv["x1"], da, sv["a"], ds2)
    tie = emit(dict(w_ff1=dw1, w_ff2=dw2))
    dya, dyb, dyc, dxp, dwo, g["dg1"], g["db1"] = _outproj_bwd(dx1, sv["s1"], sv["ya"], sv["yb"], sv["yc"],
                                                               q["w_out"], q["ln1_g"] + tie)
    h = sv["h"]
    daq, dakv, g["dsink"] = _swa_bwd(h, tk, q["swa_sink"], dyc)
    do, gr, g["dlng"] = _gla_post_bwd(sv["of"], sv["ob"], h, q["lng"], dyb)
    gq_f, gk_f, gv_f, gl_f, gq_b, gk_b, gv_b, gl_b = _gla_bwd(h, sv["la2"], do, sv["sf"], sv["sb"])
    dhl, g["dwa"], g["dba"] = _gla_gate_bwd(h, q["wa"], q["ba"], gl_f, gl_b)
    dyp, dud, g["dd"], dw4, g["dbv"], g["dbg"] = _s5_glu_bwd(sv["y2"], h, q["dsk"], q["w4"], q["bv"], q["bg"], dya)
    tie = emit(dict(w_out=dwo.reshape(NSHARD, D // NSHARD, D), s5_w_glu=dw4))
    du2, g["dbre"], g["dbim"], g["dcre"], g["dcim"], g["dmu"] = _s5_bwd(
        h, dyp, sv["hre"], sv["him"], q["bre"], q["bim"], q["cre"], q["cim"], (q["tabc"][0], q["tabc"][1] + tie))
    dx, dwt = _inproj_bwd(sv["x"], q["w_in"], dxp, du2, dud, gq_f, gq_b, gk_f, gk_b, gv_f, gv_b, gr, daq, dakv, dhl)
    tie = emit(dict(w_in=dwt))
    return dx, g, tie


NATIVE = ("dmu", "dbre", "dbim", "dcre", "dcim", "dd", "dbv", "dbg", "dwa", "dba", "dlng", "dsink",
          "dg1", "db1", "dg2", "db2", "loss")
ICI_CORE = (0, 0, 0, 1, 1, 0, 0, 0, 1, 1, 1, 1, 0, 0, 1, 1, 0)


def _finish_small(n, w):
    g = {}
    dmu = n["dmu"]
    dlr = dmu[:, :, :, 0].reshape(DEPTH, 2, S5_G, S5_P)
    dli = dmu[:, :, :, 1].reshape(DEPTH, 2, S5_G, S5_P)

    def unblock(c, perm, shape):
        return c.reshape(DEPTH, 2, 2, S5_H, 8, S5_P).transpose(perm).reshape(shape)

    b_shape, c_shape = (DEPTH, 2, S5_G, S5_P, S5_H), (DEPTH, 2, S5_G, S5_H, S5_P)
    _, vjp = jax.vjp(_s5_discretize, w["s5_a_re"], w["s5_a_im"], w["s5_log_step"], w["s5_b_re"], w["s5_b_im"])
    (g["s5_a_re"], g["s5_a_im"], g["s5_log_step"], g["s5_b_re"], g["s5_b_im"]) = vjp(
        (dlr, dli, unblock(n["dbre"], (0, 1, 2, 4, 5, 3), b_shape), unblock(n["dbim"], (0, 1, 2, 4, 5, 3), b_shape)))
    g["s5_c_re"] = unblock(n["dcre"], (0, 1, 2, 4, 3, 5), c_shape)
    g["s5_c_im"] = unblock(n["dcim"], (0, 1, 2, 4, 3, 5), c_shape)
    g["s5_d"] = n["dd"].reshape(DEPTH, S5_G, S5_H)
    g["s5_b_glu"] = jnp.concatenate([n["dbv"], n["dbg"]], axis=2).reshape(DEPTH, 512)
    g["gla_w_a"] = jnp.stack([n["dwa"][:, 0:16, 0:128], n["dwa"][:, 16:32, 128:256]], axis=1)
    g["gla_b_a"] = n["dba"].reshape(DEPTH, 2, 128)
    g["gla_ln_g"] = n["dlng"].reshape(DEPTH, 256)
    g["swa_sink"] = n["dsink"][:, :, 0]
    for k, s in (("ln1_g", "dg1"), ("ln1_b", "db1"), ("ln2_g", "dg2"), ("ln2_b", "db2")):
        g[k] = n[s].reshape(DEPTH, D)
    return g


def _local_step(x, target, qs, tk, fetch, emit):
    saved = []
    for l, q in enumerate(qs):
        x, sv = _layer_fwd(x, q, tk, functools.partial(fetch, l), target if l == DEPTH - 1 else None)
        saved.append(sv)
    dy, lacc = x
    smalls = [None] * DEPTH
    tie = 0.0
    for l in reversed(range(DEPTH)):
        qs[l]["ln2_g"] = qs[l]["ln2_g"] + tie
        dy, smalls[l], tie = _layer_bwd(dy, qs[l], saved[l], tk, functools.partial(emit, l))
    smalls[0]["db2"] = smalls[0]["db2"] + tie
    for l in range(DEPTH):
        smalls[l]["loss"] = lacc if l == 0 else jnp.zeros_like(lacc)
    return lacc[0, 0], dy, smalls


BIG = ("w_in", "s5_w_glu", "w_out", "w_ff1", "w_ff2")
SMALL = ("s5_a_re", "s5_a_im", "s5_log_step", "s5_b_re", "s5_b_im", "s5_c_re", "s5_c_im", "s5_d", "s5_b_glu",
         "gla_w_a", "gla_b_a", "gla_ln_g", "swa_sink", "ln1_g", "ln1_b", "ln2_g", "ln2_b")
ANY = pl.BlockSpec(memory_space=pl.ANY)


def _place():
    x, y, c = lax.axis_index("x"), lax.axis_index("y"), lax.axis_index("c")
    return x, y, c, [(1 - x, y), (x, 1 - y), (1 - x, 1 - y)]


HBM = pl.BlockSpec(memory_space=pltpu.HBM)
SEMS = pl.BlockSpec(memory_space=pltpu.SEMAPHORE)
EFFECT = pltpu.SideEffectType.DATAFLOW_SIDE_EFFECTING


def _push_copies(ins, lands, send, recv, gather, sending):
    x, y, c, chips = _place()
    me = 2 * x + y
    if gather == "sibling":
        return [pltpu.make_async_remote_copy(src_ref=ins[a], dst_ref=lands[a], send_sem=send.at[a], recv_sem=recv.at[a],
                                             device_id=(x, y, 1 - c), device_id_type=MESH) for a in range(len(lands))]
    out = []
    for a in range(len(lands)):
        for j, (px, py) in enumerate(chips):
            peer = 2 * px + py
            src = lands[a].at[me] if gather else ins[a].at[peer if sending else me]
            dst = lands[a].at[me if sending else peer]
            out.append(pltpu.make_async_remote_copy(src_ref=src, dst_ref=dst, send_sem=send.at[3 * a + j],
                                                    recv_sem=recv.at[3 * a + j], device_id=(px, py, c),
                                                    device_id_type=MESH))
    return out


def _push_start(name, arrs, gather):
    n = len(arrs)
    ops = list(arrs) if gather is True else list(arrs) + [lax.empty(s.shape, s.dtype) for s in arrs]
    m = len(ops)

    def body(*refs):
        ins, lnd = (refs[:n], refs[:n]) if gather is True else (refs[:n], refs[n:m])
        for cp in _push_copies(ins, lnd, refs[m], refs[m + 1], gather, True):
            cp.start()
        refs[-1][...] = jnp.zeros((8, 128), F32)

    ops = [pltpu.with_memory_space_constraint(t, pltpu.HBM) for t in ops]
    res = pl.pallas_call(
        body, name=name,
        out_shape=(pltpu.SemaphoreType.DMA((3 * n,)), pltpu.SemaphoreType.DMA((3 * n,)),
                   *[pltpu.HBM(t.shape, t.dtype) for t in ops], _sds((8, 128))),
        in_specs=[HBM] * m,
        out_specs=(SEMS, SEMS, *[HBM] * m, pl.BlockSpec(memory_space=pltpu.VMEM)),
        input_output_aliases={i: 2 + i for i in range(m)},
        compiler_params=pltpu.CompilerParams(has_side_effects=EFFECT))(*ops)
    return res[0], res[1], list(res[2:2 + m]), res[-1]


def _push_wait(name, started, after, gather):
    send, recv, ops, _ = started
    m = len(ops)
    n = m if gather is True else m // 2

    def body(*refs):
        ins, lnd = (refs[:n], refs[:n]) if gather is True else (refs[:n], refs[n:m])
        for cp in _push_copies(ins, lnd, refs[m], refs[m + 1], gather, False):
            cp.wait_send()
            cp.wait_recv()

    res = pl.pallas_call(
        body, name=name,
        out_shape=[pltpu.HBM(t.shape, t.dtype) for t in ops],
        in_specs=[HBM] * m + [SEMS, SEMS, ANY], out_specs=[HBM] * m,
        input_output_aliases={i: i for i in range(m)},
        compiler_params=pltpu.CompilerParams(has_side_effects=EFFECT))(*ops, send, recv, after)
    return list(res)


def _row_tile(rows):
    return max(t for t in range(8, min(rows, 512) + 1, 8) if rows % t == 0)


def _cast_to_slot(me, w, l):
    _, rows, cols = w.shape
    tr = _row_tile(rows)

    def body(me_ref, w_ref, o_ref):
        o_ref[0] = w_ref[0].astype(MX)

    return pl.pallas_call(
        body,
        grid_spec=pltpu.PrefetchScalarGridSpec(
            num_scalar_prefetch=1, grid=(rows // tr,),
            in_specs=[pl.BlockSpec((1, tr, cols), lambda i, me_: (l, i, 0))],
            out_specs=pl.BlockSpec((1, tr, cols), lambda i, me_: (me_[0], i, 0))),
        out_shape=_sds((NSHARD, rows, cols), MX), name="cast_to_slot", compiler_params=_cp(("arbitrary",)))(me, w)


def _sum_sources(me, recv, own):
    _, rows, cols = recv[0].shape
    tr = min(_row_tile(rows), 256) if rows % 256 == 0 else _row_tile(rows)
    nt = rows // tr

    def body(me_ref, *refs):
        o_ref = refs[-1]
        for l in range(DEPTH):
            @pl.when(pl.program_id(0) == l)
            def _():
                r_ref, own_ref = refs[2 * l], refs[2 * l + 1]
                part = [jnp.where(me_ref[0] == s, own_ref[0], r_ref[s]).astype(F32) for s in range(NSHARD)]
                o_ref[...] = ((part[0] + part[1]) + part[2]) + part[3]

    in_specs = []
    for l in range(DEPTH):
        pick = lambda g, i, me_, l=l: jnp.where(g == l, i, jnp.where(g < l, 0, nt - 1))
        in_specs += [pl.BlockSpec((NSHARD, tr, cols), lambda g, i, me_, pick=pick: (0, pick(g, i, me_), 0)),
                     pl.BlockSpec((1, tr, cols), lambda g, i, me_, pick=pick: (me_[0], pick(g, i, me_), 0))]
    return pl.pallas_call(
        body,
        grid_spec=pltpu.PrefetchScalarGridSpec(
            num_scalar_prefetch=1, grid=(DEPTH, nt), in_specs=in_specs,
            out_specs=pl.BlockSpec((tr, cols), lambda g, i, me_: (g * nt + i, 0))),
        out_shape=_sds((DEPTH * rows, cols)), name="sum_sources",
        compiler_params=_cp(("arbitrary", "arbitrary")))(me, *[t for l in range(DEPTH) for t in (recv[l], own[l])])


def _swap_sibling(arrs):
    n = len(arrs)

    def body(*refs):
        ins, outs = refs[:n], refs[n:2 * n]
        send, recv = refs[2 * n:]
        x, y, c, _ = _place()
        cps = [pltpu.make_async_remote_copy(src_ref=ins[a], dst_ref=outs[a], send_sem=send.at[a], recv_sem=recv.at[a],
                                            device_id=(x, y, 1 - c), device_id_type=MESH) for a in range(n)]
        for cp in cps:
            cp.start()
        for cp in cps:
            cp.wait()

    return pl.pallas_call(
        body, in_specs=[ANY] * n, out_specs=[ANY] * n, out_shape=[_sds(a.shape, a.dtype) for a in arrs],
        scratch_shapes=[pltpu.SemaphoreType.DMA((n,)), pltpu.SemaphoreType.DMA((n,))],
        name="swap_sibling")(*arrs)


def _allreduce_small(per_layer):
    nk = len(per_layer[0])
    n = DEPTH * nk
    shapes = [a.shape for a in per_layer[0]]

    def body(*refs):
        ins, outs = refs[:n], refs[n:n + nk]
        sibs, slots = refs[n + nk:n + 2 * nk], refs[n + 2 * nk:n + 3 * nk]
        send, recv = refs[n + 3 * nk:]
        x, y, c, chips = _place()
        me = 2 * x + y
        d2d = [pltpu.make_async_remote_copy(src_ref=ins[l * nk + k], dst_ref=sibs[k].at[l], send_sem=send.at[l * nk + k],
                                            recv_sem=recv.at[l * nk + k], device_id=(x, y, 1 - c), device_id_type=MESH)
               for l in range(DEPTH) for k in range(nk)]
        for cp in d2d:
            cp.start()
        for cp in d2d:
            cp.wait()
        for l in range(DEPTH):
            for k in range(nk):
                slots[k][0, l] = ins[l * nk + k][...] + sibs[k][l]

        def swap(k, stage):
            peer = (1 - x, y, c) if stage == 0 else (x, 1 - y, c)
            return pltpu.make_async_remote_copy(src_ref=slots[k].at[2 * stage], dst_ref=slots[k].at[2 * stage + 1],
                                                send_sem=send.at[n + 3 * k + stage], recv_sem=recv.at[n + 3 * k + stage],
                                                device_id=peer, device_id_type=MESH)

        def handover(k):
            return pltpu.make_async_remote_copy(src_ref=outs[k], dst_ref=outs[k], send_sem=send.at[n + 3 * nk + k],
                                                recv_sem=recv.at[n + 3 * nk + k], device_id=(x, y, 1 - c),
                                                device_id_type=MESH)

        halves = (tuple(k for k in range(nk) if ICI_CORE[k] == 0), tuple(k for k in range(nk) if ICI_CORE[k] == 1))
        for cc in range(2):
            @pl.when(c == cc)
            def _():
                mine, theirs = halves[cc], halves[1 - cc]
                for stage in range(2):
                    cps = [swap(k, stage) for k in mine]
                    for cp in cps:
                        cp.start()
                    for cp in cps:
                        cp.wait()
                    for k in mine:
                        if stage == 0:
                            slots[k][2] = slots[k][0] + slots[k][1]
                        else:
                            outs[k][...] = slots[k][2] + slots[k][3]
                over = [handover(k) for k in mine]
                for cp in over:
                    cp.start()
                for k in theirs:
                    handover(k).wait_recv()
                for cp in over:
                    cp.wait_send()

    vm = pl.BlockSpec(memory_space=pltpu.VMEM)
    return pl.pallas_call(
        body, in_specs=[vm] * n, out_specs=[vm] * nk, out_shape=[_sds((DEPTH,) + s) for s in shapes],
        scratch_shapes=([pltpu.VMEM((DEPTH,) + s, F32) for s in shapes]
                        + [pltpu.VMEM((NSHARD, DEPTH) + s, F32) for s in shapes]
                        + [pltpu.SemaphoreType.DMA((n + 4 * nk,)), pltpu.SemaphoreType.DMA((n + 4 * nk,))]),
        name="allreduce_small", compiler_params=pltpu.CompilerParams(vmem_limit_bytes=VMEM_LIMIT))(
            *[a for layer in per_layer for a in layer])


def _adamw_math(w, g, m, v):
    m = ADAM_B1 * m + (1.0 - ADAM_B1) * g
    v = ADAM_B2 * v + (1.0 - ADAM_B2) * jnp.square(g)
    m_hat = m / (1.0 - ADAM_B1 ** ADAM_STEP)
    v_hat = v / (1.0 - ADAM_B2 ** ADAM_STEP)
    delta = -ADAM_LR * (m_hat / (jnp.sqrt(v_hat) + ADAM_EPS) + ADAM_WD * w)
    return delta, m, v


def _adamw(g_parts, w, m, v):
    rows, cols = w.shape
    tr = 256 if rows % 256 == 0 else _row_tile(rows)
    k = len(g_parts)

    def body(*refs):
        g = refs[0][...]
        for r in refs[1:k]:
            g = g + r[...]
        w_ref, m_ref, v_ref, go, do, mo, vo = refs[k:]
        d, mn, vn = _adamw_math(w_ref[...], g, m_ref[...], v_ref[...])
        go[...] = g
        do[...] = d
        mo[...] = mn
        vo[...] = vn

    spec = pl.BlockSpec((tr, cols), lambda i: (i, 0))
    return pl.pallas_call(
        body, grid=(rows // tr,), in_specs=[spec] * (k + 3), out_specs=[spec] * 4,
        out_shape=[_sds((rows, cols))] * 4, name="adamw", compiler_params=_cp(("parallel",)))(*g_parts, w, m, v)


def _adamw_small(gs, ws, ms, vs):
    n = len(gs)

    def body(*refs):
        for k in range(n):
            d, mn, vn = _adamw_math(refs[n + k][...], refs[k][...], refs[2 * n + k][...], refs[3 * n + k][...])
            refs[4 * n + k][...] = d
            refs[5 * n + k][...] = mn
            refs[6 * n + k][...] = vn

    vm = pl.BlockSpec(memory_space=pltpu.VMEM)
    shapes = [_sds(a.shape) for a in ws]
    res = pl.pallas_call(
        body, in_specs=[vm] * (4 * n), out_specs=[vm] * (3 * n), out_shape=shapes * 3, name="adamw_small",
        compiler_params=pltpu.CompilerParams(vmem_limit_bytes=VMEM_LIMIT))(*gs, *ws, *ms, *vs)
    return res[:n], res[n:2 * n], res[2 * n:]


_ARGS = ("x", "w_in", "s5_a_re", "s5_a_im", "s5_log_step", "s5_b_re", "s5_b_im", "s5_c_re", "s5_c_im", "s5_d",
         "s5_w_glu", "s5_b_glu", "gla_w_a", "gla_b_a", "gla_ln_g", "swa_sink", "w_out", "ln1_g", "ln1_b", "w_ff1",
         "w_ff2", "ln2_g", "ln2_b")
_WEIGHTS = _ARGS[1:]


def _shard_cols(d):
    return d.reshape(d.shape[0], NSHARD, d.shape[1] // NSHARD).transpose(1, 0, 2)


def kernel(x, w_in, s5_a_re, s5_a_im, s5_log_step, s5_b_re, s5_b_im, s5_c_re, s5_c_im, s5_d, s5_w_glu, s5_b_glu, gla_w_a, gla_b_a, gla_ln_g, swa_sink, w_out, ln1_g, ln1_b, w_ff1, w_ff2, ln2_g, ln2_b, loss_target, m_w_in, m_s5_a_re, m_s5_a_im, m_s5_log_step, m_s5_b_re, m_s5_b_im, m_s5_c_re, m_s5_c_im, m_s5_d, m_s5_w_glu, m_s5_b_glu, m_gla_w_a, m_gla_b_a, m_gla_ln_g, m_swa_sink, m_w_out, m_ln1_g, m_ln1_b, m_w_ff1, m_w_ff2, m_ln2_g, m_ln2_b, v_w_in, v_s5_a_re, v_s5_a_im, v_s5_log_step, v_s5_b_re, v_s5_b_im, v_s5_c_re, v_s5_c_im, v_s5_d, v_s5_w_glu, v_s5_b_glu, v_gla_w_a, v_gla_b_a, v_gla_ln_g, v_swa_sink, v_w_out, v_ln1_g, v_ln1_b, v_w_ff1, v_w_ff2, v_ln2_g, v_ln2_b):
    given = dict(locals())
    w = {k: given[k] for k in _WEIGHTS}
    mom = {k: given["m_" + k] for k in _WEIGHTS}
    var = {k: given["v_" + k] for k in _WEIGHTS}

    me = (2 * lax.axis_index("x") + lax.axis_index("y")).astype(jnp.int32).reshape(1)
    tr = lambda t: t.transpose(0, 2, 1)
    shard = {k: (tr(w[k]) if k == "w_in" else w[k]) for k in BIG}
    qs = [None] * DEPTH

    first = ("w_in", "s5_w_glu", "w_out")
    follow = {(0, "w_in"): [(0, BIG[3:])], (0, "s5_w_glu"): [(1, first)], (0, "w_ff1"): [(1, BIG[3:])]}
    gathers = {}

    casts = {}

    def start_gather(l, names, behind=None):
        lands = [casts.pop((l, k)) if (l, k) in casts else _cast_to_slot(me, shard[k], l) for k in names]
        if behind is not None:
            lands, behind = lax.optimization_barrier((lands, behind))
        st = _push_start(f"gather_start_{l}_{names[0]}", lands, True)
        for k in names:
            gathers[l, k] = [names, st, None]
        return st[-1], behind

    token = start_gather(0, first[:1])[0] + start_gather(0, first[1:])[0]
    zero = token[0, 0]
    for l in range(DEPTH):
        for k in BIG:
            if (l, k) not in gathers:
                casts[l, k] = _cast_to_slot(me, lax.optimization_barrier((shard[k], token))[0], l)
        qs[l] = _layer_prep({k: (w[k][l] + zero if k == "s5_a_re" else w[k][l]) for k in SMALL})
    token, casts, qs = lax.optimization_barrier((token, casts, qs))

    def fetch(l, name, after):
        names, st, got = gathers[l, name]
        tie = None
        if got is None:
            if l == 0 and name == "w_in":
                after = token
            lands = _push_wait(f"gather_wait_{l}_{names[0]}", st, after, True)
            for l2, names2 in follow.get((l, name), ()):
                tok, lands[0] = start_gather(l2, names2, lands[0])
                tie = tok if tie is None else tie + tok
            got = dict(zip(names, lands))
            for k in names:
                gathers[l, k][2] = got
        full = got[name]
        if name == "w_in":
            return _in_rows(full, token if tie is None else tie)
        if tie is not None:
            near = "bv" if name == "s5_w_glu" else "ln2_b"
            qs[l][near] = qs[l][near] + tie[0, 0]
        return full.reshape(D, D) if name == "w_out" else full

    scatters, held = [], {}

    def emit(l, grads):
        if l > 0:
            held.update(grads)
            if "w_in" not in grads:
                return 0.0
            grads = dict(held)
            held.clear()
        names = tuple(grads)
        st = _push_start(f"scatter_start_{l}_{names[0]}", [grads[k] for k in names], False)
        scatters.append((l, names, st))
        return st[-1][0, 0]

    loss, dx, smalls = _local_step(x.reshape(N, D), loss_target.reshape(N, D), qs, _rope_tables(128), fetch, emit)

    out, recv, own = {}, {}, {}

    def collect(keys, after):
        for l, names, st in scatters:
            if names[0] in keys:
                ops = _push_wait(f"scatter_wait_{l}_{names[0]}", st, after, False)
                for i, k in enumerate(names):
                    own[l, k], recv[l, k] = ops[i], ops[len(names) + i]

    def to_sibling(keys):
        sums = [_sum_sources(me, [recv[l, k] for l in range(DEPTH)], [own[l, k] for l in range(DEPTH)]) for k in keys]
        return _push_start(f"swap_start_{keys[0]}", sums, "sibling")

    def apply(keys, started, after):
        ops = _push_wait(f"swap_wait_{keys[0]}", started, after, "sibling")
        for i, k in enumerate(keys):
            mine, other = ops[i], ops[len(keys) + i]
            shp = shard[k].shape
            r = _adamw([mine, other], *((tr(t[k]) if k == "w_in" else t[k]).reshape(-1, shp[-1]) for t in (w, mom, var)))
            r = [t.reshape(shp) for t in r]
            out[k] = [tr(t) for t in r] if k == "w_in" else r
        return out[keys[-1]][1]

    collect(("w_ff1", "w_ff2", "w_out", "s5_w_glu"), dx)
    ff = to_sibling(("w_ff1", "w_ff2"))
    mix = to_sibling(("w_out", "s5_w_glu"))
    smalls[0]["db1"] = smalls[0]["db1"] + (ff[-1][0, 0] + mix[-1][0, 0])
    native = _allreduce_small([[smalls[l][k] for k in NATIVE] for l in range(DEPTH)])
    native = dict(zip(NATIVE, native))
    loss = native["loss"][0, 0, 0] + native["loss"][1, 0, 0]
    gsmall = _finish_small(native, w)
    view = lambda k, t: t.transpose(0, 1, 2, 4, 3) if k in ("s5_b_re", "s5_b_im") else t
    res = _adamw_small(*([view(k, t[k]) for k in SMALL] for t in (gsmall, w, mom, var)))
    for i, k in enumerate(SMALL):
        out[k] = [gsmall[k]] + [view(k, r[i]) for r in res]
    last = apply(("w_ff1", "w_ff2"), ff, res[0][-1])
    collect(("w_in",), last)
    win = to_sibling(("w_in",))
    last = apply(("w_out", "s5_w_glu"), mix, win[-1])
    apply(("w_in",), win, last)

    return (loss, dx.reshape(NSEQ, L, D), *[out[k][0] for k in _WEIGHTS], *[out[k][1] for k in _WEIGHTS],
            *[out[k][2] for k in _WEIGHTS], *[out[k][3] for k in _WEIGHTS])
```

```python
import functools
import math

import jax
import jax.numpy as jnp
from jax import lax
from jax.experimental import pallas as pl
from jax.experimental.pallas import tpu as pltpu

F32 = jnp.float32
MX = jnp.bfloat16
MESH = pl.DeviceIdType.MESH

DEPTH = 2
NSEQ = 2
L = 2048
N = NSEQ * L
D = 1024
DFF = 4096
NSHARD = 4
S5_G, S5_H, S5_P = 16, 16, 64
GLA_CHUNK = 64
NCHUNK = L // GLA_CHUNK
GLA_GROUP = 4
NGROUP = NCHUNK // GLA_GROUP
SWA_BLK = 128
NBLK = L // SWA_BLK
ROT = 16
ROPE_THETA = 500000.0
LN_EPS = 1e-5
ALPHA = (2 * DEPTH) ** 0.25
NEG_BIG = -1e30
DIN = 1824
DINP = 1920
ADAM_LR, ADAM_B1, ADAM_B2, ADAM_EPS, ADAM_WD, ADAM_STEP = 0.001, 0.9, 0.999, 1e-08, 0.01, 10
VMEM_LIMIT = 56 * 1024 * 1024
TT = 512
SW = 512
FFN_TM = 512
FFN_TM_W = 1024
FFN_WB = 1
FFN_VMEM = 60 * 1024 * 1024
INPROJ_BWD_TM = 512


def _cp(sem, vmem=VMEM_LIMIT):
    return pltpu.CompilerParams(dimension_semantics=sem, vmem_limit_bytes=vmem)


def _mm(a, b):
    return jnp.dot(a.astype(MX), b.astype(MX), preferred_element_type=F32)


def _mm_nt(a, b):
    return lax.dot_general(a.astype(MX), b.astype(MX), (((1,), (1,)), ((), ())), preferred_element_type=F32)


def _mm_tn(a, b):
    return lax.dot_general(a.astype(MX), b.astype(MX), (((0,), (0,)), ((), ())), preferred_element_type=F32)


@jax.custom_vjp
def _dmm(a, b):
    return _mm(a, b)


_dmm.defvjp(lambda a, b: (_mm(a, b), (a, b)), lambda r, g: (_mm_nt(g, r[1]), _mm_tn(r[0], g)))


@jax.custom_vjp
def _dmm_nt(a, b):
    return _mm_nt(a, b)


_dmm_nt.defvjp(lambda a, b: (_mm_nt(a, b), (a, b)), lambda r, g: (_mm(g, r[1]), _mm_tn(g, r[0])))


@jax.custom_vjp
def _dmm_tn(a, b):
    return _mm_tn(a, b)


_dmm_tn.defvjp(lambda a, b: (_mm_tn(a, b), (a, b)), lambda r, g: (_mm_nt(r[1], g), _mm(r[0], g)))


def _split3(x):
    hi = x.astype(MX)
    r1 = x - hi.astype(F32)
    mid = r1.astype(MX)
    lo = (r1 - mid.astype(F32)).astype(MX)
    return hi, mid, lo


def _chunk_pairs(rows, rev, strict):
    r = lax.broadcasted_iota(jnp.int32, (rows, rows), 0)
    c = lax.broadcasted_iota(jnp.int32, (rows, rows), 1)
    order = ((c > r) if strict else (c >= r)) if rev else ((c < r) if strict else (c <= r))
    return (r // GLA_CHUNK == c // GLA_CHUNK) & order


def _cums_impl(x, rev):
    rows, w = x.shape
    t = jnp.where(_chunk_pairs(rows, rev, False), 1.0, 0.0).astype(MX)
    s = jnp.dot(t, jnp.concatenate(_split3(x), axis=1), preferred_element_type=F32)
    return s[:, 0:w] + s[:, w:2 * w] + s[:, 2 * w:3 * w]


@functools.partial(jax.custom_vjp, nondiff_argnums=(1,))
def _cums(x, rev):
    return _cums_impl(x, rev)


_cums.defvjp(lambda x, rev: (_cums_impl(x, rev), None), lambda rev, r, g: (_cums_impl(g, not rev),))


def _ln_fwd(s, g, b):
    mu = jnp.mean(s, axis=-1, keepdims=True)
    xc = s - mu
    var = jnp.mean(xc * xc, axis=-1, keepdims=True)
    return xc * lax.rsqrt(var + LN_EPS) * g + b


def _ln_bwd(dy, s, g):
    mu = jnp.mean(s, axis=-1, keepdims=True)
    xc = s - mu
    var = jnp.mean(xc * xc, axis=-1, keepdims=True)
    rstd = lax.rsqrt(var + LN_EPS)
    xhat = xc * rstd
    dxh = dy * g
    ds = rstd * (dxh - jnp.mean(dxh, axis=-1, keepdims=True) - xhat * jnp.mean(dxh * xhat, axis=-1, keepdims=True))
    return ds, jnp.sum(dy * xhat, axis=0, keepdims=True), jnp.sum(dy, axis=0, keepdims=True)


def _sds(shape, dtype=F32):
    return jax.ShapeDtypeStruct(shape, dtype)


_IN_ROW_PIECES = (((0, 0), (0, 456)), ((1, 0), (456, 456)), ((2, 0), (912, 112)), ((2, 112), (1792, 32)),
                  ((2, 144), (1024, 312)), ((3, 0), (1336, 456)))


def _in_rows(g4, behind):
    def body(g_ref, behind_ref, o_ref, tmp):
        tmp[DIN:DINP] = jnp.zeros((DINP - DIN, D), F32)
        for (j, s0), (d0, n_) in _IN_ROW_PIECES:
            tmp[d0:d0 + n_] = g_ref[j, s0:s0 + n_].astype(F32)
        o_ref[...] = tmp[...].astype(MX)

    vm = pl.BlockSpec(memory_space=pltpu.VMEM)
    return pl.pallas_call(body, in_specs=[vm, pl.BlockSpec(memory_space=pl.ANY)], out_specs=vm,
                          out_shape=_sds((DINP, D), MX), scratch_shapes=[pltpu.VMEM((DINP, D), F32)], name="in_rows",
                          compiler_params=pltpu.CompilerParams(vmem_limit_bytes=VMEM_LIMIT))(g4, behind)


def _inproj_fwd(x, wt):
    tm = 512

    def body(x_ref, w_ref, h_ref):
        h_ref[...] = _mm_nt(x_ref[...], w_ref[...])

    return pl.pallas_call(
        body, grid=(N // tm,),
        in_specs=[pl.BlockSpec((tm, D), lambda i: (i, 0)), pl.BlockSpec((DINP, D), lambda i: (0, 0))],
        out_specs=pl.BlockSpec((tm, DINP), lambda i: (i, 0)),
        out_shape=_sds((N, DINP)), name="inproj_fwd", compiler_params=_cp(("parallel",)))(x, wt)


def _inproj_bwd(x, w, dxp, du2, dud, gq_f, gq_b, gk_f, gk_b, gv_f, gv_b, gr, daq, dakv, dhl):
    tm = INPROJ_BWD_TM
    nt = N // tm

    def body(x_ref, w_ref, dxp_ref, du2_ref, dud_ref, gqf, gqb, gkf, gkb, gvf, gvb, gr_ref, daq_ref, dakv_ref, dhl_ref,
             dx_ref, dw_ref, acc):
        i = pl.program_id(0)
        f = lambda r: r[...].astype(F32)
        dh = jnp.concatenate([
            du2_ref[0] + du2_ref[1] + f(dud_ref), f(gqf) + f(gqb), f(gkf) + f(gkb), f(gvf) + f(gvb),
            f(gr_ref), f(daq_ref), f(dakv_ref), f(dhl_ref)], axis=1)
        dx_ref[...] = dxp_ref[...] + _mm(dh, w_ref[...])
        contrib = _mm_tn(dh, x_ref[...])

        @pl.when(i == 0)
        def _():
            acc[...] = contrib

        @pl.when(i > 0)
        def _():
            acc[...] += contrib

        @pl.when(i == nt - 1)
        def _():
            for (j, d0), (s0, n_) in _IN_ROW_PIECES:
                dw_ref[j, d0:d0 + n_] = acc[s0:s0 + n_].astype(MX)

    row = lambda w_: pl.BlockSpec((tm, w_), lambda i: (i, 0))
    return pl.pallas_call(
        body, grid=(nt,),
        in_specs=[row(D), pl.BlockSpec((DINP, D), lambda i: (0, 0)), row(D),
                  pl.BlockSpec((2, tm, 256), lambda i: (0, i, 0)), row(256), row(128), row(128), row(128), row(128),
                  row(256), row(256), row(256), row(512), row(256), row(128)],
        out_specs=[row(D), pl.BlockSpec((NSHARD, DIN // NSHARD, D), lambda i: (0, 0, 0))],
        out_shape=[_sds((N, D)), _sds((NSHARD, DIN // NSHARD, D), MX)],
        scratch_shapes=[pltpu.VMEM((DINP, D), F32)],
        name="inproj_bwd", compiler_params=_cp(("arbitrary",)))(
            x, w, dxp, du2, dud, gq_f, gq_b, gk_f, gk_b, gv_f, gv_b, gr, daq, dakv, dhl)


def _tile_scan(xr, xi, a, cr, ci, reverse):
    for lvl, d in enumerate((1, 2, 4)):
        sh = 8 - d if reverse else d
        sr = pltpu.roll(xr, sh, 0)
        si = pltpu.roll(xi, sh, 0)
        ar, ai = a[2 * lvl], a[2 * lvl + 1]
        xr, xi = xr + ar * sr - ai * si, xi + ar * si + ai * sr
    pr, pi = a[6], a[7]
    return xr + pr * cr - pi * ci, xi + pr * ci + pi * cr


NJ = TT // 8


def _lockstep_tables(mr, mi):
    def body(mr_ref, mi_ref, a_ref, p_ref, ac_ref, pc_ref):
        rowid = lax.broadcasted_iota(jnp.int32, (8, 2 * SW), 0)

        def mul(a, b):
            return a[0] * b[0] - a[1] * b[1], a[0] * b[1] + a[1] * b[0]

        for z in range(2):
            for sign, reverse, a_out, p_out in ((1.0, z == 1, a_ref, p_ref), (-1.0, z == 0, ac_ref, pc_ref)):
                m = (mr_ref[z:z + 1, :], sign * mi_ref[z:z + 1, :])
                pw = [m]
                for _ in range(NJ - 1):
                    pw.append(mul(pw[-1], m))
                n = pw[-1]
                link = [n]
                for _ in range(7):
                    link.append(mul(link[-1], n))
                tiles = [jnp.broadcast_to(m[0], (8, 2 * SW)), jnp.broadcast_to(m[1], (8, 2 * SW))]
                for d in (1, 2, 4):
                    keep = (rowid <= 7 - d) if reverse else (rowid >= d)
                    tiles += [jnp.where(keep, link[d - 1][c], 0.0) for c in range(2)]
                for c in range(2):
                    t = jnp.zeros((8, 2 * SW), F32)
                    for i in range(8):
                        t = jnp.where(rowid == (7 - i if reverse else i), link[i][c], t)
                    tiles.append(t)
                for blk in range(2):
                    lanes = slice(blk * SW, (blk + 1) * SW)
                    for k, t in enumerate(tiles):
                        a_out[z, blk, k] = t[:, lanes]
                    for j in range(NJ):
                        src = pw[NJ - 1 - j] if reverse else pw[j]
                        for c in range(2):
                            p_out[z, blk, c, j:j + 1, :] = src[c][:, lanes]

    vm = pl.BlockSpec(memory_space=pltpu.VMEM)
    a_shape, p_shape = _sds((2, 2, 10, 8, SW)), _sds((2, 2, 2, NJ, SW))
    a, p, ac, pc = pl.pallas_call(body, in_specs=[vm, vm], out_specs=[vm] * 4, out_shape=[a_shape, p_shape] * 2,
                                  name="s5_tables")(mr, mi)
    return (a, p), (ac, pc)


def _to_lockstep(ref, *lead):
    return jnp.concatenate([ref[(*lead, pl.ds(j, 8, stride=NJ), slice(None))] for j in range(NJ)], axis=0)


def _from_lockstep(val, ref, *lead):
    for j in range(NJ):
        ref[(*lead, pl.ds(j, 8, stride=NJ), slice(None))] = val[8 * j:8 * j + 8]


def _expand_powers(p_ref, pexp):
    for c in range(2):
        for j in range(NJ):
            pexp[c, j] = jnp.broadcast_to(p_ref[0, 0, c, j:j + 1, :], (8, SW))


def _lockstep_scan(xre, xim, a_ref, pexp, car, reverse, extra=None):
    a = [a_ref[0, 0, k] for k in range(10)]
    mr, mi = a[0], a[1]
    order = (lambda i: NJ - 1 - i) if reverse else (lambda i: i)

    def local(i, hcar):
        hr, hi = hcar
        r0 = pl.multiple_of(order(i) * 8, 8)
        hr, hi = mr * hr - mi * hi + xre[pl.ds(r0, 8), :], mr * hi + mi * hr + xim[pl.ds(r0, 8), :]
        xre[pl.ds(r0, 8), :] = hr
        xim[pl.ds(r0, 8), :] = hi
        return hr, hi

    z8 = jnp.zeros((8, SW), F32)
    er, ei = lax.fori_loop(0, NJ, local, (z8, z8), unroll=4)
    c0r, c0i = car[0], car[1]
    er, ei = _tile_scan(er, ei, a[2:], c0r, c0i, reverse)
    rowid = lax.broadcasted_iota(jnp.int32, (8, SW), 0)
    first, sh, last = (7, 7, 0) if reverse else (0, 1, 7)
    cvr = jnp.where(rowid == first, c0r, pltpu.roll(er, sh, 0))
    cvi = jnp.where(rowid == first, c0i, pltpu.roll(ei, sh, 0))
    car[0] = jnp.broadcast_to(er[last:last + 1, :], (8, SW))
    car[1] = jnp.broadcast_to(ei[last:last + 1, :], (8, SW))

    def fix(i, carry):
        j = order(i)
        r0 = pl.multiple_of(j * 8, 8)
        pr, pi = pexp[0, j], pexp[1, j]
        sr = xre[pl.ds(r0, 8), :] + pr * cvr - pi * cvi
        si = xim[pl.ds(r0, 8), :] + pr * cvi + pi * cvr
        xre[pl.ds(r0, 8), :] = sr
        xim[pl.ds(r0, 8), :] = si
        if extra is None:
            return carry
        return (sr, si, extra(r0, sr, si, carry[0], carry[1], carry[2]))

    init = (cvr, cvi, extra(None, None, None, None, None, None)) if extra is not None else 0
    return lax.fori_loop(0, NJ, fix, init, unroll=4)


def _s5_time_block(z, s, t, adjoint):
    flip = (1 - z) if adjoint else z
    return s * (L // TT) + t + flip * (L // TT - 1 - 2 * t)


def _s5_fwd(h, bre, bim, cre, cim, tab):
    nt = L // TT
    taba, tabp = tab

    def body(u_ref, bre_ref, bim_ref, cre_ref, cim_ref, a_ref, p_ref, hre_ref, him_ref, y_ref, car, pexp):
        z = pl.program_id(1)
        s = pl.program_id(2)
        tc = pl.program_id(3)

        @pl.when(tc == 0)
        def _():
            car[...] = jnp.zeros_like(car)

        @pl.when((tc == 0) & (s == 0))
        def _():
            _expand_powers(p_ref, pexp)

        u = _to_lockstep(u_ref)
        hre_ref[0] = _mm(u, bre_ref[0, 0])
        him_ref[0] = _mm(u, bim_ref[0, 0])

        @pl.when(z == 0)
        def _():
            _lockstep_scan(hre_ref.at[0], him_ref.at[0], a_ref, pexp, car, False)

        @pl.when(z == 1)
        def _():
            _lockstep_scan(hre_ref.at[0], him_ref.at[0], a_ref, pexp, car, True)

        _from_lockstep(_mm(hre_ref[0], cre_ref[0, 0]) - _mm(him_ref[0], cim_ref[0, 0]), y_ref, 0)

    tb = lambda b, z, s, t: _s5_time_block(z, s, t, False)
    wspec = lambda r, c: pl.BlockSpec((1, 1, r, c), lambda b, z, s, t: (z, b, 0, 0))
    return pl.pallas_call(
        body, grid=(2, 2, NSEQ, nt),
        in_specs=[pl.BlockSpec((TT, 128), lambda b, z, s, t: (tb(b, z, s, t), b)),
                  wspec(128, SW), wspec(128, SW), wspec(SW, 128), wspec(SW, 128),
                  pl.BlockSpec((1, 1, 10, 8, SW), lambda b, z, s, t: (z, b, 0, 0, 0)),
                  pl.BlockSpec((1, 1, 2, NJ, SW), lambda b, z, s, t: (z, b, 0, 0, 0))],
        out_specs=[pl.BlockSpec((1, TT, SW), lambda b, z, s, t: (z, tb(b, z, s, t), b)),
                   pl.BlockSpec((1, TT, SW), lambda b, z, s, t: (z, tb(b, z, s, t), b)),
                   pl.BlockSpec((1, TT, 128), lambda b, z, s, t: (z, tb(b, z, s, t), b))],
        out_shape=[_sds((2, N, 2 * SW)), _sds((2, N, 2 * SW)), _sds((2, N, 256))],
        scratch_shapes=[pltpu.VMEM((2, 8, SW), F32), pltpu.VMEM((2, NJ, 8, SW), F32)],
        name="s5_fwd", compiler_params=_cp(("arbitrary",) * 4))(h, bre, bim, cre, cim, taba, tabp)


def _s5_bwd(h, dyp, hre, him, bre, bim, cre, cim, tabc):
    nt = L // TT
    taba, tabp = tabc

    def body(u_ref, dy_ref, hre_ref, him_ref, bre_ref, bim_ref, cre_ref, cim_ref, a_ref, p_ref,
             du_ref, dbre_ref, dbim_ref, dcre_ref, dcim_ref, dmu_ref, gre, gim, car, acc, macc, pexp):
        z = pl.program_id(1)
        s = pl.program_id(2)
        tc = pl.program_id(3)

        @pl.when(tc == 0)
        def _():
            car[...] = jnp.zeros_like(car)

        @pl.when((tc == 0) & (s == 0))
        def _():
            acc[...] = jnp.zeros_like(acc)
            macc[...] = jnp.zeros_like(macc)
            _expand_powers(p_ref, pexp)

        dy = _to_lockstep(dy_ref)
        gre[...] = _mm_nt(dy, cre_ref[0, 0])
        gim[...] = -_mm_nt(dy, cim_ref[0, 0])

        def run(reverse):
            def pair(r0, gr_, gi_, pvr, pvi, m):
                if r0 is None:
                    return (macc[0], macc[1])
                hr = hre_ref[0, pl.ds(r0, 8), :]
                hi = him_ref[0, pl.ds(r0, 8), :]
                return (m[0] + pvr * hr + pvi * hi, m[1] + pvi * hr - pvr * hi)

            _, _, (dmr, dmi) = _lockstep_scan(gre, gim, a_ref, pexp, car, reverse, pair)
            macc[0] = dmr
            macc[1] = dmi

        @pl.when(z == 0)
        def _():
            run(True)

        @pl.when(z == 1)
        def _():
            run(False)

        gr = gre[...]
        gi = gim[...]
        u = _to_lockstep(u_ref)
        _from_lockstep(_mm_nt(gr, bre_ref[0, 0]) + _mm_nt(gi, bim_ref[0, 0]), du_ref, 0)
        acc[0] += _mm_tn(u, gr)
        acc[1] += _mm_tn(u, gi)
        acc[2] += _mm_tn(dy, hre_ref[0])
        acc[3] -= _mm_tn(dy, him_ref[0])

        @pl.when((tc == nt - 1) & (s == NSEQ - 1))
        def _():
            grp = lax.broadcasted_iota(jnp.int32, (S5_H, SW), 1) // S5_P
            for k, out in enumerate((dbre_ref, dbim_ref, dcre_ref, dcim_ref)):
                c = jnp.zeros((S5_H, SW), F32)
                for i in range(8):
                    c = c + jnp.where(grp == i, acc[k, i * S5_H:(i + 1) * S5_H, :], 0.0)
                out[0, 0] = c
            dmu_ref[0, 0] = jnp.concatenate([jnp.sum(macc[0], axis=0, keepdims=True),
                                             jnp.sum(macc[1], axis=0, keepdims=True)], axis=0)

    tb = lambda b, z, s, t: _s5_time_block(z, s, t, True)
    wspec = lambda r, c: pl.BlockSpec((1, 1, r, c), lambda b, z, s, t: (z, b, 0, 0))
    tok = lambda w_: pl.BlockSpec((TT, w_), lambda b, z, s, t: (tb(b, z, s, t), b))
    st = pl.BlockSpec((1, TT, SW), lambda b, z, s, t: (z, tb(b, z, s, t), b))
    return pl.pallas_call(
        body, grid=(2, 2, NSEQ, nt),
        in_specs=[tok(128), tok(128), st, st, wspec(128, SW), wspec(128, SW), wspec(SW, 128), wspec(SW, 128),
                  pl.BlockSpec((1, 1, 10, 8, SW), lambda b, z, s, t: (z, b, 0, 0, 0)),
                  pl.BlockSpec((1, 1, 2, NJ, SW), lambda b, z, s, t: (z, b, 0, 0, 0))],
        out_specs=[pl.BlockSpec((1, TT, 128), lambda b, z, s, t: (z, tb(b, z, s, t), b)),
                   wspec(S5_H, SW), wspec(S5_H, SW), wspec(S5_H, SW), wspec(S5_H, SW),
                   wspec(2, SW)],
        out_shape=[_sds((2, N, 256))] + [_sds((2, 2, S5_H, SW))] * 4 + [_sds((2, 2, 2, SW))],
        scratch_shapes=[pltpu.VMEM((TT, SW), F32), pltpu.VMEM((TT, SW), F32), pltpu.VMEM((2, 8, SW), F32),
                        pltpu.VMEM((4, 128, SW), F32), pltpu.VMEM((2, 8, SW), F32), pltpu.VMEM((2, NJ, 8, SW), F32)],
        name="s5_bwd", compiler_params=_cp(("arbitrary",) * 4))(h, dyp, hre, him, bre, bim, cre, cim, taba, tabp)


_GELU_C = math.sqrt(2.0 / math.pi)


def _gelu(y):
    return 0.5 * y * (1.0 + jnp.tanh(_GELU_C * (y + 0.044715 * y * y * y)))


def _gelu_grad(y):
    t = jnp.tanh(_GELU_C * (y + 0.044715 * y * y * y))
    return 0.5 * (1.0 + t) + 0.5 * y * (1.0 - t * t) * _GELU_C * (1.0 + 3 * 0.044715 * y * y)


def _glu_halves(w4_ref):
    return (jnp.concatenate([w4_ref[0], w4_ref[1]], axis=1), jnp.concatenate([w4_ref[2], w4_ref[3]], axis=1))


def _s5_glu_fwd(y2, h, dsk, w4, bv, bg):
    tm = 512

    def body(y2_ref, u_ref, d_ref, w4_ref, bv_ref, bg_ref, ya_ref):
        wv, wg = _glu_halves(w4_ref)
        z = _gelu(y2_ref[0] + y2_ref[1] + d_ref[...] * u_ref[...])
        val = _mm(z, wv) + bv_ref[...]
        gate = _mm(z, wg) + bg_ref[...]
        ya_ref[...] = (val * jax.nn.sigmoid(gate)).astype(MX)

    full = lambda r, c: pl.BlockSpec((r, c), lambda i: (0, 0))
    return pl.pallas_call(
        body, grid=(N // tm,),
        in_specs=[pl.BlockSpec((2, tm, 256), lambda i: (0, i, 0)), pl.BlockSpec((tm, 256), lambda i: (i, 0)),
                  full(1, 256), pl.BlockSpec((NSHARD, 256, 128), lambda i: (0, 0, 0)), full(1, 256), full(1, 256)],
        out_specs=pl.BlockSpec((tm, 256), lambda i: (i, 0)),
        out_shape=_sds((N, 256), MX), name="s5_glu_fwd", compiler_params=_cp(("parallel",)))(y2, h, dsk, w4, bv, bg)


def _s5_glu_bwd(y2, h, dsk, w4, bv, bg, dya):
    tm = 512
    nt = N // tm

    def body(y2_ref, u_ref, d_ref, w4_ref, bv_ref, bg_ref, dya_ref,
             dyp_ref, dud_ref, dd_ref, dw4_ref, dbv_ref, dbg_ref, accv, accg):
        i = pl.program_id(0)

        @pl.when(i == 0)
        def _():
            for r in (dd_ref, accv, accg, dbv_ref, dbg_ref):
                r[...] = jnp.zeros_like(r)

        wv, wg = _glu_halves(w4_ref)
        u = u_ref[...]
        y = y2_ref[0] + y2_ref[1] + d_ref[...] * u
        z = _gelu(y)
        val = _mm(z, wv) + bv_ref[...]
        sig = jax.nn.sigmoid(_mm(z, wg) + bg_ref[...])
        dya = dya_ref[...]
        dval = dya * sig
        dgate = dya * val * sig * (1.0 - sig)
        dz = _mm_nt(dval, wv) + _mm_nt(dgate, wg)
        dy = dz * _gelu_grad(y)
        dyp_ref[...] = dy
        dud_ref[...] = (dy * d_ref[...]).astype(MX)
        dd_ref[...] += jnp.sum(dy * u, axis=0, keepdims=True)
        accv[...] += _mm_tn(z, dval)
        accg[...] += _mm_tn(z, dgate)
        dbv_ref[...] += jnp.sum(dval, axis=0, keepdims=True)
        dbg_ref[...] += jnp.sum(dgate, axis=0, keepdims=True)

        @pl.when(i == nt - 1)
        def _():
            dw4_ref[0] = accv[:, 0:128].astype(MX)
            dw4_ref[1] = accv[:, 128:256].astype(MX)
            dw4_ref[2] = accg[:, 0:128].astype(MX)
            dw4_ref[3] = accg[:, 128:256].astype(MX)

    full = lambda r, c: pl.BlockSpec((r, c), lambda i: (0, 0))
    row = pl.BlockSpec((tm, 256), lambda i: (i, 0))
    wspec = pl.BlockSpec((NSHARD, 256, 128), lambda i: (0, 0, 0))
    return pl.pallas_call(
        body, grid=(nt,),
        in_specs=[pl.BlockSpec((2, tm, 256), lambda i: (0, i, 0)), row, full(1, 256), wspec, full(1, 256), full(1, 256),
                  row],
        out_specs=[row, row, full(1, 256), wspec, full(1, 256), full(1, 256)],
        out_shape=[_sds((N, 256)), _sds((N, 256), MX), _sds((1, 256)), _sds((NSHARD, 256, 128), MX), _sds((1, 256)),
                   _sds((1, 256))],
        scratch_shapes=[pltpu.VMEM((256, 256), F32), pltpu.VMEM((256, 256), F32)],
        name="s5_glu_bwd", compiler_params=_cp(("arbitrary",)))(y2, h, dsk, w4, bv, bg, dya)


def _logsig(x):
    return jnp.minimum(x, 0.0) - jnp.log(1.0 + jnp.exp(-jnp.abs(x)))


def _gla_gate_fwd(h, wa, ba):
    tm = 512

    def body(hl_ref, wa_ref, ba_ref, la_ref):
        la_ref[...] = _logsig(_mm(hl_ref[...], wa_ref[...]) + ba_ref[...]) * (1.0 / 16.0)

    return pl.pallas_call(
        body, grid=(N // tm,),
        in_specs=[pl.BlockSpec((tm, 128), lambda i: (i, 14)), pl.BlockSpec((128, 256), lambda i: (0, 0)),
                  pl.BlockSpec((1, 256), lambda i: (0, 0))],
        out_specs=pl.BlockSpec((tm, 256), lambda i: (i, 0)),
        out_shape=_sds((N, 256)), name="gla_gate_fwd", compiler_params=_cp(("parallel",)))(h, wa, ba)


def _gla_gate_bwd(h, wa, ba, dla_f, dla_b):
    tm = 512

    def body(hl_ref, wa_ref, ba_ref, df_ref, db_ref, dhl_ref, dwa_ref, dba_ref):
        i = pl.program_id(0)

        @pl.when(i == 0)
        def _():
            dwa_ref[...] = jnp.zeros_like(dwa_ref)
            dba_ref[...] = jnp.zeros_like(dba_ref)

        hl = hl_ref[...]
        pre = _mm(hl, wa_ref[...]) + ba_ref[...]
        dpre = jnp.concatenate([df_ref[...], db_ref[...]], axis=1) * (1.0 / 16.0) * jax.nn.sigmoid(-pre)
        dhl_ref[...] = _mm_nt(dpre, wa_ref[...]).astype(MX)
        dwa_ref[...] += _mm_tn(hl, dpre)[0:32]
        dba_ref[...] += jnp.sum(dpre, axis=0, keepdims=True)

    row = pl.BlockSpec((tm, 128), lambda i: (i, 0))
    return pl.pallas_call(
        body, grid=(N // tm,),
        in_specs=[pl.BlockSpec((tm, 128), lambda i: (i, 14)), pl.BlockSpec((128, 256), lambda i: (0, 0)),
                  pl.BlockSpec((1, 256), lambda i: (0, 0)), row, row],
        out_specs=[row, pl.BlockSpec((32, 256), lambda i: (0, 0)), pl.BlockSpec((1, 256), lambda i: (0, 0))],
        out_shape=[_sds((N, 128), MX), _sds((32, 256)), _sds((1, 256))],
        name="gla_gate_bwd", compiler_params=_cp(("arbitrary",)))(h, wa, ba, dla_f, dla_b)


def _gla_chunk(q, k, v, la, st, rev):
    c = GLA_CHUNK
    rows = q.shape[0]
    nch = rows // c
    b = _cums(la, rev)
    blc = [jnp.sum(la[i * c:(i + 1) * c], axis=0, keepdims=True) for i in range(nch)]
    bl = jnp.concatenate([jnp.broadcast_to(t, (c, 128)) for t in blc], axis=0)
    q_in = q * (32.0 ** -0.5) * jnp.exp(b)
    k_in = k * jnp.exp(-b)
    k_st = k * jnp.exp(bl - b)
    lane_k = lax.broadcasted_iota(jnp.int32, (1, 128), 1) // 32
    lane_v = lax.broadcasted_iota(jnp.int32, (1, 256), 1) // 64
    qs = jnp.concatenate([jnp.where(lane_k == hd, q_in, 0.0) for hd in range(4)], axis=0)
    a = _dmm_nt(qs, k_in)
    a = jnp.where(jnp.concatenate([_chunk_pairs(rows, rev, rev)] * 4, axis=0), a, 0.0)
    o4 = _dmm(a, v)
    o = jnp.zeros((rows, 256), F32)
    for hd in range(4):
        o = o + jnp.where(lane_v == hd, o4[hd * rows:(hd + 1) * rows], 0.0)
    bd = (lax.broadcasted_iota(jnp.int32, (256, 128), 0) // 64) == (lax.broadcasted_iota(jnp.int32, (256, 128), 1) // 32)
    inter = [None] * nch
    for i in (reversed(range(nch)) if rev else range(nch)):
        sl = slice(i * c, (i + 1) * c)
        inter[i] = _dmm_nt(q_in[sl], st)
        st = jnp.exp(blc[i]) * st + jnp.where(bd, _dmm_tn(v[sl], k_st[sl]), 0.0)
    return o + jnp.concatenate(inter, axis=0), st


def _gla_chunk_of(c, rev):
    return NGROUP - 1 - c if rev else c


def _gla_fwd(h, la2):
    c = GLA_GROUP * GLA_CHUNK

    def body(qf, kf, vf, laf, qb, kb, vb, lab, of_ref, ob_ref, sf_ref, sb_ref, stf, stb):
        @pl.when(pl.program_id(0) == 0)
        def _():
            stf[...] = jnp.zeros_like(stf)
            stb[...] = jnp.zeros_like(stb)

        ins = [(qf[s], kf[s], vf[s], laf[s], stf[s], qb[s], kb[s], vb[s], lab[s], stb[s]) for s in range(NSEQ)]
        outs = [(_gla_chunk(*t[:5], False), _gla_chunk(*t[5:], True)) for t in ins]
        for s in range(NSEQ):
            sf_ref[s, 0] = ins[s][4]
            sb_ref[s, 0] = ins[s][9]
            (of_ref[s], stf[s]), (ob_ref[s], stb[s]) = outs[s]

    def specs(rev):
        ch = lambda i: _gla_chunk_of(i, rev)
        return [pl.BlockSpec((NSEQ, c, 128), lambda i: (0, ch(i), 2)), pl.BlockSpec((NSEQ, c, 128), lambda i: (0, ch(i), 3)),
                pl.BlockSpec((NSEQ, c, 256), lambda i: (0, ch(i), 2)),
                pl.BlockSpec((NSEQ, c, 128), lambda i: (0, ch(i), 1 if rev else 0))]

    orow = lambda rev: pl.BlockSpec((NSEQ, c, 256), lambda i: (0, _gla_chunk_of(i, rev), 0))
    srow = lambda rev: pl.BlockSpec((NSEQ, 1, 256, 128), lambda i: (0, _gla_chunk_of(i, rev), 0, 0))
    h3, la3 = h.reshape(NSEQ, L, DINP), la2.reshape(NSEQ, L, 256)
    of, ob, sf, sb = pl.pallas_call(
        body, grid=(NGROUP,),
        in_specs=specs(False) + specs(True),
        out_specs=[orow(False), orow(True), srow(False), srow(True)],
        out_shape=[_sds((NSEQ, L, 256)), _sds((NSEQ, L, 256)), _sds((NSEQ, NGROUP, 256, 128)),
                   _sds((NSEQ, NGROUP, 256, 128))],
        scratch_shapes=[pltpu.VMEM((NSEQ, 256, 128), F32), pltpu.VMEM((NSEQ, 256, 128), F32)],
        name="gla_fwd", compiler_params=_cp(("arbitrary",)))(h3, h3, h3, la3, h3, h3, h3, la3)
    return of.reshape(N, 256), ob.reshape(N, 256), sf, sb


def _gla_bwd(h, la2, do, sf, sb):
    c = GLA_GROUP * GLA_CHUNK

    def body(qf, kf, vf, laf, dof, sfr, qb, kb, vb, lab, dob, sbr,
             dqf, dkf, dvf, dlf, dqb, dkb, dvb, dlb, dstf, dstb):
        @pl.when(pl.program_id(0) == 0)
        def _():
            dstf[...] = jnp.zeros_like(dstf)
            dstb[...] = jnp.zeros_like(dstb)

        def one(s, q, k, v, la, do_, st, dst, rev):
            _, vjp = jax.vjp(functools.partial(_gla_chunk, rev=rev), q[s], k[s], v[s], la[s], st[s, 0])
            return vjp((do_[s], dst[s]))

        res = [(one(s, qf, kf, vf, laf, dof, sfr, dstf, False), one(s, qb, kb, vb, lab, dob, sbr, dstb, True))
               for s in range(NSEQ)]
        for s in range(NSEQ):
            for (gq, gk, gv, gl, gs), (dq, dk, dv, dl, dst) in ((res[s][0], (dqf, dkf, dvf, dlf, dstf)),
                                                                  (res[s][1], (dqb, dkb, dvb, dlb, dstb))):
                dq[s], dk[s], dv[s] = gq.astype(MX), gk.astype(MX), gv.astype(MX)
                dl[s], dst[s] = gl, gs

    def specs(rev):
        ch = lambda i: _gla_chunk_of(i, not rev)
        return [pl.BlockSpec((NSEQ, c, 128), lambda i: (0, ch(i), 2)), pl.BlockSpec((NSEQ, c, 128), lambda i: (0, ch(i), 3)),
                pl.BlockSpec((NSEQ, c, 256), lambda i: (0, ch(i), 2)),
                pl.BlockSpec((NSEQ, c, 128), lambda i: (0, ch(i), 1 if rev else 0)),
                pl.BlockSpec((NSEQ, c, 256), lambda i: (0, ch(i), 0)),
                pl.BlockSpec((NSEQ, 1, 256, 128), lambda i: (0, ch(i), 0, 0))]

    def ospecs(rev):
        ch = lambda i: _gla_chunk_of(i, not rev)
        n = pl.BlockSpec((NSEQ, c, 128), lambda i: (0, ch(i), 0))
        return [n, n, pl.BlockSpec((NSEQ, c, 256), lambda i: (0, ch(i), 0)), n]

    oshape = [_sds((NSEQ, L, 128), MX), _sds((NSEQ, L, 128), MX), _sds((NSEQ, L, 256), MX), _sds((NSEQ, L, 128))]
    h3, la3, do3 = h.reshape(NSEQ, L, DINP), la2.reshape(NSEQ, L, 256), do.reshape(NSEQ, L, 256)
    res = pl.pallas_call(
        body, grid=(NGROUP,),
        in_specs=specs(False) + specs(True),
        out_specs=ospecs(False) + ospecs(True),
        out_shape=oshape + oshape,
        scratch_shapes=[pltpu.VMEM((NSEQ, 256, 128), F32), pltpu.VMEM((NSEQ, 256, 128), F32)],
        name="gla_bwd", compiler_params=_cp(("arbitrary",)))(h3, h3, h3, la3, do3, sf, h3, h3, h3, la3, do3, sb)
    return [r.reshape(N, r.shape[-1]) for r in res]


def _gla_post(of, ob, r, g):
    o = of + ob
    head = lax.broadcasted_iota(jnp.int32, (1, 256), 1) // 64
    mu = jnp.zeros_like(o)
    for hd in range(4):
        mu = mu + jnp.where(head == hd, jnp.sum(jnp.where(head == hd, o, 0.0), axis=-1, keepdims=True) * (1.0 / 64.0), 0.0)
    xc = o - mu
    var = jnp.zeros_like(o)
    for hd in range(4):
        var = var + jnp.where(head == hd, jnp.sum(jnp.where(head == hd, xc * xc, 0.0), axis=-1, keepdims=True) * (1.0 / 64.0), 0.0)
    return xc * lax.rsqrt(var + LN_EPS) * g * (r * jax.nn.sigmoid(r))


def _gla_post_fwd(of, ob, h, g):
    tm = 512

    def body(of_ref, ob_ref, r_ref, g_ref, y_ref):
        y_ref[...] = _gla_post(of_ref[...], ob_ref[...], r_ref[...], g_ref[...]).astype(MX)

    row = pl.BlockSpec((tm, 256), lambda i: (i, 0))
    return pl.pallas_call(
        body, grid=(N // tm,),
        in_specs=[row, row, pl.BlockSpec((tm, 256), lambda i: (i, 3)), pl.BlockSpec((1, 256), lambda i: (0, 0))],
        out_specs=row, out_shape=_sds((N, 256), MX), name="gla_post_fwd", compiler_params=_cp(("parallel",)))(of, ob, h, g)


def _gla_post_bwd(of, ob, h, g, dyb):
    tm = 512

    def body(of_ref, ob_ref, r_ref, g_ref, dy_ref, do_ref, dr_ref, dg_ref):
        @pl.when(pl.program_id(0) == 0)
        def _():
            dg_ref[...] = jnp.zeros_like(dg_ref)

        _, vjp = jax.vjp(_gla_post, of_ref[...], ob_ref[...], r_ref[...], g_ref[...])
        go, _, gr, gg = vjp(dy_ref[...])
        do_ref[...] = go
        dr_ref[...] = gr.astype(MX)
        dg_ref[...] += gg

    row = pl.BlockSpec((tm, 256), lambda i: (i, 0))
    one = pl.BlockSpec((1, 256), lambda i: (0, 0))
    return pl.pallas_call(
        body, grid=(N // tm,),
        in_specs=[row, row, pl.BlockSpec((tm, 256), lambda i: (i, 3)), one, row],
        out_specs=[row, row, one], out_shape=[_sds((N, 256)), _sds((N, 256), MX), _sds((1, 256))],
        name="gla_post_bwd", compiler_params=_cp(("arbitrary",)))(of, ob, h, g, dyb)


def _rope_tables(width):
    pos = jnp.arange(L, dtype=F32)
    inv_freq = ROPE_THETA ** (-jnp.arange(0, ROT, 2, dtype=F32) / ROT)
    ang = pos[:, None] * inv_freq[None, :]
    cos, sin = jnp.cos(ang), jnp.sin(ang)
    one = jnp.ones((L, 64 - ROT), F32)
    zero = jnp.zeros((L, 64 - ROT), F32)
    z8 = jnp.zeros((L, ROT // 2), F32)
    c = jnp.concatenate([cos, cos, one], axis=1)
    sa = jnp.concatenate([z8, sin, zero], axis=1)
    sb = jnp.concatenate([-sin, z8, zero], axis=1)
    rep = width // 64
    return jnp.stack([jnp.tile(c, (1, rep)), jnp.tile(sa, (1, rep)), jnp.tile(sb, (1, rep))])


def _pieces(t, f):
    out = [f(t[:, c * 128:(c + 1) * 128]) for c in range(t.shape[-1] // 128)]
    return out[0] if len(out) == 1 else jnp.concatenate(out, axis=1)


def _rope(t, tab):
    return _pieces(t, lambda x: x * tab[0] + pltpu.roll(x, ROT // 2, 1) * tab[1] + pltpu.roll(x, 128 - ROT // 2, 1) * tab[2])


def _rope_t(g, tab):
    return _pieces(g, lambda x: x * tab[0] + pltpu.roll(x * tab[1], 128 - ROT // 2, 1) + pltpu.roll(x * tab[2], ROT // 2, 1))


def _swa_pad_kv(kv_ref, tk_ref, kexp, vexp):
    z = jnp.zeros((SWA_BLK, 256), F32)
    kr = _rope(kv_ref[:, 0:128], tk_ref[...])
    for hk in range(2):
        for pad in (kexp, vexp):
            pad[hk, 0:SWA_BLK] = z
            pad[hk, SWA_BLK + L:] = z
        kexp[hk, SWA_BLK:SWA_BLK + L] = _swa_expand(kr, hk)
        vexp[hk, SWA_BLK:SWA_BLK + L] = _swa_expand(kv_ref[:, 128:256], hk)


def _swa_expand(x, hk):
    lane = lax.broadcasted_iota(jnp.int32, x.shape, 1)
    sw = pltpu.roll(x, 64, 1)
    pair = jnp.where(lane < 64, x, sw) if hk == 0 else jnp.where(lane < 64, sw, x)
    return jnp.concatenate([pair, pair], axis=1)


def _swa_fold(x, hk):
    a = x[:, 0:128] + x[:, 128:256]
    t = a + pltpu.roll(a, 64, 1)
    lane = lax.broadcasted_iota(jnp.int32, a.shape, 1)
    return jnp.where((lane < 64) if hk == 0 else (lane >= 64), t, 0.0)


def _swa_probs(q2, kexp, n, sink_ref, hk):
    slot = lax.broadcasted_iota(jnp.int32, (1, 256), 1) // 64
    qs = jnp.concatenate([jnp.where(slot == g, q2, 0.0) for g in range(4)], axis=0)
    s = _mm_nt(qs, kexp) * 0.125
    i = lax.broadcasted_iota(jnp.int32, (SWA_BLK, 3 * SWA_BLK), 0)
    j = lax.broadcasted_iota(jnp.int32, (SWA_BLK, 3 * SWA_BLK), 1)
    kpos = n * SWA_BLK - SWA_BLK + j
    ok = (j - i >= 0) & (j - i <= 2 * SWA_BLK) & (kpos >= 0) & (kpos < L)
    s = jnp.where(jnp.concatenate([ok] * 4, axis=0), s, NEG_BIG)
    rowg = lax.broadcasted_iota(jnp.int32, (4 * SWA_BLK, 1), 0) // SWA_BLK
    sink = jnp.zeros((4 * SWA_BLK, 1), F32)
    for g in range(4):
        sink = jnp.where(rowg == g, sink_ref[hk * 4 + g], sink)
    m = jnp.maximum(jnp.max(s, axis=-1, keepdims=True), sink)
    p = jnp.exp(s - m)
    ps = jnp.exp(sink - m)
    inv = 1.0 / (jnp.sum(p, axis=-1, keepdims=True) + ps)
    return qs, p * inv, ps * inv, slot, rowg


def _swa_qtab(tk_ref, r0):
    return [tk_ref[i, pl.ds(r0, SWA_BLK), :] for i in range(3)]


def _swa_fwd(h, tk, sink):
    def body(sink_ref, q_ref, kv_ref, tk_ref, y_ref, kexp, vexp):
        n = pl.program_id(1)

        @pl.when(n == 0)
        def _():
            _swa_pad_kv(kv_ref, tk_ref, kexp, vexp)

        r0 = pl.multiple_of(n * SWA_BLK, SWA_BLK)
        q = _rope(q_ref[...], _swa_qtab(tk_ref, r0))
        for hk in range(2):
            _, p, _, slot, _ = _swa_probs(q[:, hk * 256:(hk + 1) * 256], kexp[hk, pl.ds(r0, 3 * SWA_BLK), :], n,
                                          sink_ref, hk)
            o4 = _mm(p, vexp[hk, pl.ds(r0, 3 * SWA_BLK), :])
            o = jnp.zeros((SWA_BLK, 256), F32)
            for g in range(4):
                o = o + jnp.where(slot == g, o4[g * SWA_BLK:(g + 1) * SWA_BLK], 0.0)
            y_ref[:, hk * 256:(hk + 1) * 256] = o.astype(MX)

    return pl.pallas_call(
        body,
        grid_spec=pltpu.PrefetchScalarGridSpec(
            num_scalar_prefetch=1, grid=(NSEQ, NBLK),
            in_specs=[pl.BlockSpec((SWA_BLK, 512), lambda s, n, sk: (s * NBLK + n, 2)),
                      pl.BlockSpec((L, 256), lambda s, n, sk: (s, 6)),
                      pl.BlockSpec((3, L, 128), lambda s, n, sk: (0, 0, 0))],
            out_specs=pl.BlockSpec((SWA_BLK, 512), lambda s, n, sk: (s * NBLK + n, 0)),
            scratch_shapes=[pltpu.VMEM((2, L + 2 * SWA_BLK, 256), F32), pltpu.VMEM((2, L + 2 * SWA_BLK, 256), F32)]),
        out_shape=_sds((N, 512), MX), name="swa_fwd", compiler_params=_cp(("arbitrary", "arbitrary")))(sink, h, h, tk)


def _swa_bwd(h, tk, sink, dyc):
    def body(sink_ref, q_ref, kv_ref, tk_ref, dy_ref, dq_ref, dkv_ref, dsink_ref, kexp_all, vexp_all, dkacc, dvacc):
        sq = pl.program_id(0)
        n = pl.program_id(1)

        @pl.when(n == 0)
        def _():
            _swa_pad_kv(kv_ref, tk_ref, kexp_all, vexp_all)
            dkacc[...] = jnp.zeros_like(dkacc)
            dvacc[...] = jnp.zeros_like(dvacc)

        @pl.when((n == 0) & (sq == 0))
        def _():
            dsink_ref[...] = jnp.zeros_like(dsink_ref)

        r0 = pl.multiple_of(n * SWA_BLK, SWA_BLK)
        tq = _swa_qtab(tk_ref, r0)
        q = _rope(q_ref[...], tq)
        hrow = lax.broadcasted_iota(jnp.int32, (8, 128), 0)
        dsk = jnp.zeros((8, 128), F32)
        for hk in range(2):
            kexp = kexp_all[hk, pl.ds(r0, 3 * SWA_BLK), :]
            vexp = vexp_all[hk, pl.ds(r0, 3 * SWA_BLK), :]
            qs, p, ps, slot, rowg = _swa_probs(q[:, hk * 256:(hk + 1) * 256], kexp, n, sink_ref, hk)
            dy2 = dy_ref[:, hk * 256:(hk + 1) * 256]
            dos = jnp.concatenate([jnp.where(slot == g, dy2, 0.0) for g in range(4)], axis=0)
            dp = _mm_nt(dos, vexp)
            delta = jnp.sum(p * dp, axis=-1, keepdims=True)
            ds = p * (dp - delta) * 0.125
            dsr = -ps * delta
            for g in range(4):
                dsk = dsk + jnp.where(hrow == hk * 4 + g, jnp.sum(jnp.where(rowg == g, dsr, 0.0), axis=0, keepdims=True), 0.0)
            dq4 = _mm(ds, kexp)
            dq2 = jnp.zeros((SWA_BLK, 256), F32)
            for g in range(4):
                dq2 = dq2 + jnp.where(slot == g, dq4[g * SWA_BLK:(g + 1) * SWA_BLK], 0.0)
            dq_ref[:, hk * 256:(hk + 1) * 256] = _rope_t(dq2, tq).astype(MX)
            dkacc[hk, pl.ds(r0, 3 * SWA_BLK), :] += _mm_tn(ds, qs)
            dvacc[hk, pl.ds(r0, 3 * SWA_BLK), :] += _mm_tn(p, dos)
        dsink_ref[...] += dsk

        @pl.when(n == NBLK - 1)
        def _():
            seq = slice(SWA_BLK, SWA_BLK + L)
            dk = _rope_t(_swa_fold(dkacc[0, seq], 0) + _swa_fold(dkacc[1, seq], 1), tk_ref[...])
            dkv_ref[:, 0:128] = dk.astype(MX)
            dkv_ref[:, 128:256] = (_swa_fold(dvacc[0, seq], 0) + _swa_fold(dvacc[1, seq], 1)).astype(MX)

    blk = lambda col: pl.BlockSpec((SWA_BLK, 512), lambda s, n, sk: (s * NBLK + n, col))
    pad = pltpu.VMEM((2, L + 2 * SWA_BLK, 256), F32)
    return pl.pallas_call(
        body,
        grid_spec=pltpu.PrefetchScalarGridSpec(
            num_scalar_prefetch=1, grid=(NSEQ, NBLK),
            in_specs=[blk(2), pl.BlockSpec((L, 256), lambda s, n, sk: (s, 6)),
                      pl.BlockSpec((3, L, 128), lambda s, n, sk: (0, 0, 0)), blk(0)],
            out_specs=[blk(0), pl.BlockSpec((L, 256), lambda s, n, sk: (s, 0)),
                       pl.BlockSpec((8, 128), lambda s, n, sk: (0, 0))],
            scratch_shapes=[pad, pad, pad, pad]),
        out_shape=[_sds((N, 512), MX), _sds((N, 256), MX), _sds((8, 128))],
        name="swa_bwd", compiler_params=_cp(("arbitrary", "arbitrary")))(sink, h, h, tk, dyc)


def _outproj_fwd(ya, yb, yc, x, wo, g, b):
    tm = 512

    def body(ya_ref, yb_ref, yc_ref, x_ref, wo_ref, g_ref, b_ref, s_ref, x1_ref):
        mix = _mm(ya_ref[...], wo_ref[0:256]) + _mm(yb_ref[...], wo_ref[256:512]) + _mm(yc_ref[...], wo_ref[512:1024])
        s = ALPHA * x_ref[...] + mix
        s_ref[...] = s
        x1_ref[...] = _ln_fwd(s, g_ref[...], b_ref[...])

    row = lambda w_: pl.BlockSpec((tm, w_), lambda i: (i, 0))
    one = pl.BlockSpec((1, D), lambda i: (0, 0))
    return pl.pallas_call(
        body, grid=(N // tm,),
        in_specs=[row(256), row(256), row(512), row(D), pl.BlockSpec((D, D), lambda i: (0, 0)), one, one],
        out_specs=[row(D), row(D)], out_shape=[_sds((N, D)), _sds((N, D))],
        name="outproj_fwd", compiler_params=_cp(("parallel",)))(ya, yb, yc, x, wo, g, b)


def _outproj_bwd(dx1, s1, ya, yb, yc, wo, g):
    tm = 512
    nt = N // tm

    def body(dx1_ref, s_ref, ya_ref, yb_ref, yc_ref, wo_ref, g_ref,
             dya_ref, dyb_ref, dyc_ref, dxp_ref, dwo_ref, dg_ref, db_ref, acc):
        i = pl.program_id(0)

        @pl.when(i == 0)
        def _():
            acc[...] = jnp.zeros_like(acc)
            dg_ref[...] = jnp.zeros_like(dg_ref)
            db_ref[...] = jnp.zeros_like(db_ref)

        ds, dg, db = _ln_bwd(dx1_ref[...], s_ref[...], g_ref[...])
        dg_ref[...] += dg
        db_ref[...] += db
        dxp_ref[...] = ALPHA * ds
        dy = _mm_nt(ds, wo_ref[...])
        dya_ref[...] = dy[:, 0:256]
        dyb_ref[...] = dy[:, 256:512]
        dyc_ref[...] = dy[:, 512:1024]
        acc[0:256] += _mm_tn(ya_ref[...], ds)
        acc[256:512] += _mm_tn(yb_ref[...], ds)
        acc[512:1024] += _mm_tn(yc_ref[...], ds)

        @pl.when(i == nt - 1)
        def _():
            dwo_ref[...] = acc[...].astype(MX)

    row = lambda w_: pl.BlockSpec((tm, w_), lambda i: (i, 0))
    one = pl.BlockSpec((1, D), lambda i: (0, 0))
    full = pl.BlockSpec((D, D), lambda i: (0, 0))
    return pl.pallas_call(
        body, grid=(nt,),
        in_specs=[row(D), row(D), row(256), row(256), row(512), full, one],
        out_specs=[row(256), row(256), row(512), row(D), full, one, one],
        out_shape=[_sds((N, 256)), _sds((N, 256)), _sds((N, 512)), _sds((N, D)), _sds((D, D), MX), _sds((1, D)), _sds((1, D))],
        scratch_shapes=[pltpu.VMEM((D, D), F32)],
        name="outproj_bwd", compiler_params=_cp(("arbitrary",)))(dx1, s1, ya, yb, yc, wo, g)


def _mix_ffn_fwd(ya, yb, yc, x, wo, g1, b1, w1, w2, g, b, target=None):
    tm = FFN_TM
    head = target is not None

    def body(*refs):
        ya_ref, yb_ref, yc_ref, xin_ref, wo_ref, g1_ref, b1_ref, w1_ref, w2_ref, g_ref, b_ref = refs[:11]
        s1_ref, x1_ref, a_ref, s_ref, y_ref = refs[11 + head:16 + head]
        mix = _mm(ya_ref[...], wo_ref[0:256]) + _mm(yb_ref[...], wo_ref[256:512]) + _mm(yc_ref[...], wo_ref[512:1024])
        s1 = ALPHA * xin_ref[...] + mix
        s1_ref[...] = s1
        x = _ln_fwd(s1, g1_ref[...], b1_ref[...])
        x1_ref[...] = x
        xb = x.astype(MX)
        s = ALPHA * x
        for j in range(NSHARD):
            a = _mm(xb, w1_ref[j])
            a_ref[:, j * D:(j + 1) * D] = a.astype(MX)
            s = s + _mm(jnp.square(jnp.maximum(a, 0.0)), w2_ref[j])
        s_ref[...] = s
        x2 = _ln_fwd(s, g_ref[...], b_ref[...])
        if not head:
            y_ref[...] = x2
            return
        l_ref = refs[16 + head]

        @pl.when(pl.program_id(0) == 0)
        def _():
            l_ref[...] = jnp.zeros_like(l_ref)

        e = x2 - refs[11][...]
        y_ref[...] = e * (1.0 / D)
        l_ref[...] += jnp.sum(jnp.sum(e * e, axis=1, keepdims=True), axis=0, keepdims=True) * (0.5 / D)

    rw = lambda w_: pl.BlockSpec((tm, w_), lambda i: (i, 0))
    row = rw(D)
    once = dict(pipeline_mode=pl.Buffered(1))
    wall = pl.BlockSpec((NSHARD, D, D), lambda i: (0, 0, 0), **once)
    one = pl.BlockSpec((1, D), lambda i: (0, 0))
    acc = pl.BlockSpec((8, 128), lambda i: (0, 0))
    return pl.pallas_call(
        body, grid=(N // tm,),
        in_specs=[rw(256), rw(256), rw(512), row, pl.BlockSpec((D, D), lambda i: (0, 0), **once), one, one,
                  wall, wall, one, one] + [row] * head,
        out_specs=[row, row, pl.BlockSpec((tm, DFF), lambda i: (i, 0)), row, row] + [acc] * head,
        out_shape=[_sds((N, D)), _sds((N, D)), _sds((N, DFF), MX), _sds((N, D)), _sds((N, D))] + [_sds((8, 128))] * head,
        name="mix_ffn_fwd", compiler_params=_cp(("arbitrary",), FFN_VMEM))(
            ya, yb, yc, x, wo, g1, b1, w1, w2, g, b, *([target] * head))


def _ffn_bwd_act(dy, s2, a, w1, w2, g):
    tm = FFN_TM

    def body(dy_ref, s_ref, a_ref, w1_ref, w2_ref, g_ref, da_ref, ds_ref, dx1_ref, dg_ref, db_ref):
        @pl.when(pl.program_id(0) == 0)
        def _():
            dg_ref[...] = jnp.zeros_like(dg_ref)
            db_ref[...] = jnp.zeros_like(db_ref)

        ds, dg, db = _ln_bwd(dy_ref[...], s_ref[...], g_ref[...])
        dsb = ds.astype(MX)
        ds_ref[...] = dsb
        dg_ref[...] += dg
        db_ref[...] += db
        dx1 = ALPHA * ds
        for j in range(NSHARD):
            da = (_mm_nt(dsb, w2_ref[j]) * 2.0 * jnp.maximum(a_ref[:, j * D:(j + 1) * D].astype(F32), 0.0)).astype(MX)
            da_ref[:, j * D:(j + 1) * D] = da
            dx1 = dx1 + _mm_nt(da, w1_ref[j])
        dx1_ref[...] = dx1

    row = pl.BlockSpec((tm, D), lambda i: (i, 0))
    wide = pl.BlockSpec((tm, DFF), lambda i: (i, 0))
    wall = pl.BlockSpec((NSHARD, D, D), lambda i: (0, 0, 0))
    one = pl.BlockSpec((1, D), lambda i: (0, 0))
    return pl.pallas_call(
        body, grid=(N // tm,),
        in_specs=[row, row, wide, wall, wall, one],
        out_specs=[wide, row, row, one, one],
        out_shape=[_sds((N, DFF), MX), _sds((N, D), MX), _sds((N, D)), _sds((1, D)), _sds((1, D))],
        name="ffn_bwd_act", compiler_params=_cp(("arbitrary",), FFN_VMEM))(dy, s2, a, w1, w2, g)


def _ffn_bwd_w(x1, da, a, ds):
    tm, nb = FFN_TM_W, FFN_WB
    nt = N // tm

    def body(x_ref, da_ref, a_ref, ds_ref, dw1_ref, dw2_ref, acc1, acc2):
        i = pl.program_id(1)

        @pl.when(i == 0)
        def _():
            acc1[...] = jnp.zeros_like(acc1)
            acc2[...] = jnp.zeros_like(acc2)

        x, ds_ = x_ref[...], ds_ref[...]
        for k in range(nb):
            cols = slice(k * D, (k + 1) * D)
            acc1[k] += _mm_tn(x, da_ref[:, cols])
            acc2[k] += _mm_tn(jnp.square(jnp.maximum(a_ref[:, cols].astype(F32), 0.0)), ds_)

        @pl.when(i == nt - 1)
        def _():
            dw1_ref[...] = acc1[...].astype(MX)
            dw2_ref[...] = acc2[...].astype(MX)

    row = pl.BlockSpec((tm, D), lambda j, i: (i, 0))
    col = pl.BlockSpec((tm, nb * D), lambda j, i: (i, j))
    wj = pl.BlockSpec((nb, D, D), lambda j, i: (j, 0, 0))
    return pl.pallas_call(
        body, grid=(NSHARD // nb, nt),
        in_specs=[row, col, col, row], out_specs=[wj, wj],
        out_shape=[_sds((NSHARD, D, D), MX), _sds((NSHARD, D, D), MX)],
        scratch_shapes=[pltpu.VMEM((nb, D, D), F32), pltpu.VMEM((nb, D, D), F32)],
        name="ffn_bwd_w", compiler_params=_cp(("parallel", "arbitrary"), FFN_VMEM))(x1, da, a, ds)


def _loss_head(y, target):
    tm = 512

    def body(y_ref, t_ref, dy_ref, l_ref):
        @pl.when(pl.program_id(0) == 0)
        def _():
            l_ref[...] = jnp.zeros_like(l_ref)

        e = y_ref[...] - t_ref[...]
        dy_ref[...] = e * (1.0 / D)
        l_ref[...] += jnp.sum(jnp.sum(e * e, axis=1, keepdims=True), axis=0, keepdims=True) * (0.5 / D)

    row = pl.BlockSpec((tm, D), lambda i: (i, 0))
    return pl.pallas_call(
        body, grid=(N // tm,), in_specs=[row, row],
        out_specs=[row, pl.BlockSpec((8, 128), lambda i: (0, 0))],
        out_shape=[_sds((N, D)), _sds((8, 128))], name="loss_head", compiler_params=_cp(("arbitrary",)))(y, target)


def _s5_discretize(a_re, a_im, log_step, b_re, b_im):
    lam = lax.complex(a_re, a_im)
    lam_bar = jnp.exp(lam * jnp.exp(log_step))
    b_bar = ((lam_bar - 1.0) / lam)[..., None] * lax.complex(b_re, b_im)
    return jnp.real(lam_bar), jnp.imag(lam_bar), jnp.real(b_bar), jnp.imag(b_bar)


def _s5_in_blocks(b):
    e = jnp.eye(8, dtype=F32)
    return jnp.einsum('ij,zbjph->zbihjp', e, b.reshape(2, 2, 8, S5_P, S5_H)).reshape(2, 2, 128, SW)


def _s5_in_unblocks(d):
    return jnp.einsum('zbihip->zbiph', d.reshape(2, 2, 8, S5_H, 8, S5_P)).reshape(2, S5_G, S5_P, S5_H)


def _s5_out_blocks(c):
    e = jnp.eye(8, dtype=F32)
    return jnp.einsum('ij,zbjhp->zbjpih', e, c.reshape(2, 2, 8, S5_H, S5_P)).reshape(2, 2, SW, 128)


def _s5_out_unblocks(d):
    return jnp.einsum('zbipih->zbihp', d.reshape(2, 2, 8, S5_P, 8, S5_H)).reshape(2, S5_G, S5_H, S5_P)


def _gate_weight(w_a):
    z = jnp.zeros((16, 128), F32)
    top = jnp.concatenate([w_a[0], z], axis=1)
    bot = jnp.concatenate([z, w_a[1]], axis=1)
    return jnp.concatenate([top, bot, jnp.zeros((96, 256), F32)], axis=0)


def _layer_prep(p):
    lr, li, br, bi = _s5_discretize(p["s5_a_re"], p["s5_a_im"], p["s5_log_step"], p["s5_b_re"], p["s5_b_im"])
    q = dict(p)
    q["bre"] = _s5_in_blocks(br).astype(MX)
    q["bim"] = _s5_in_blocks(bi).astype(MX)
    q["cre"] = _s5_out_blocks(p["s5_c_re"]).astype(MX)
    q["cim"] = _s5_out_blocks(p["s5_c_im"]).astype(MX)
    mr, mi = lr.reshape(2, 1024), li.reshape(2, 1024)
    q["tab"], q["tabc"] = _lockstep_tables(mr, mi)
    q["dsk"] = p["s5_d"].reshape(1, 256)
    q["wa"] = _gate_weight(p["gla_w_a"]).astype(MX)
    q["ba"] = p["gla_b_a"].reshape(1, 256)
    q["lng"] = p["gla_ln_g"].reshape(1, 256)
    q["bv"] = p["s5_b_glu"][:256].reshape(1, 256)
    q["bg"] = p["s5_b_glu"][256:].reshape(1, 256)
    for k in ("ln1_g", "ln1_b", "ln2_g", "ln2_b"):
        q[k] = p[k].reshape(1, D)
    return q


def _layer_fwd(x, q, tk, fetch, target=None):
    q["w_in"] = fetch("w_in", x)
    h = _inproj_fwd(x, q["w_in"])
    hre, him, y2 = _s5_fwd(h, q["bre"], q["bim"], q["cre"], q["cim"], q["tab"])
    q["w4"] = fetch("s5_w_glu", y2)
    ya = _s5_glu_fwd(y2, h, q["dsk"], q["w4"], q["bv"], q["bg"])
    la2 = _gla_gate_fwd(h, q["wa"], q["ba"])
    of, ob, sf, sb = _gla_fwd(h, la2)
    yb = _gla_post_fwd(of, ob, h, q["lng"])
    yc = _swa_fwd(h, tk, q["swa_sink"])
    mixed = ya[:8, :128] + yb[:8, :128] + yc[:8, :128]
    q["w_out"] = fetch("w_out", mixed)
    q["w_ff1"] = fetch("w_ff1", mixed)
    q["w_ff2"] = fetch("w_ff2", mixed)
    s1, x1, a, s2, *out = _mix_ffn_fwd(ya, yb, yc, x, q["w_out"], q["ln1_g"], q["ln1_b"], q["w_ff1"], q["w_ff2"],
                                       q["ln2_g"], q["ln2_b"], target)
    saved = dict(x=x, h=h, hre=hre, him=him, y2=y2, ya=ya, la2=la2, of=of, ob=ob, sf=sf, sb=sb, yb=yb, yc=yc,
                 s1=s1, x1=x1, a=a, s2=s2)
    return (out[0] if target is None else tuple(out)), saved


def _layer_bwd(dy, q, sv, tk, emit):
    g = {}
    da, ds2, dx1, g["dg2"], g["db2"] = _ffn_bwd_act(dy, sv["s2"], sv["a"], q["w_ff1"], q["w_ff2"], q["ln2_g"])
    dw1, dw2 = _ffn_bwd_w(sv["x1"], da, sv["a"], ds2)
    tie = emit(dict(w_ff1=dw1, w_ff2=dw2))
    dya, dyb, dyc, dxp, dwo, g["dg1"], g["db1"] = _outproj_bwd(dx1, sv["s1"], sv["ya"], sv["yb"], sv["yc"],
                                                               q["w_out"], q["ln1_g"] + tie)
    h = sv["h"]
    daq, dakv, g["dsink"] = _swa_bwd(h, tk, q["swa_sink"], dyc)
    do, gr, g["dlng"] = _gla_post_bwd(sv["of"], sv["ob"], h, q["lng"], dyb)
    gq_f, gk_f, gv_f, gl_f, gq_b, gk_b, gv_b, gl_b = _gla_bwd(h, sv["la2"], do, sv["sf"], sv["sb"])
    dhl, g["dwa"], g["dba"] = _gla_gate_bwd(h, q["wa"], q["ba"], gl_f, gl_b)
    dyp, dud, g["dd"], dw4, g["dbv"], g["dbg"] = _s5_glu_bwd(sv["y2"], h, q["dsk"], q["w4"], q["bv"], q["bg"], dya)
    tie = emit(dict(w_out=dwo.reshape(NSHARD, D // NSHARD, D), s5_w_glu=dw4))
    du2, g["dbre"], g["dbim"], g["dcre"], g["dcim"], g["dmu"] = _s5_bwd(
        h, dyp, sv["hre"], sv["him"], q["bre"], q["bim"], q["cre"], q["cim"], (q["tabc"][0], q["tabc"][1] + tie))
    dx, dwt = _inproj_bwd(sv["x"], q["w_in"], dxp, du2, dud, gq_f, gq_b, gk_f, gk_b, gv_f, gv_b, gr, daq, dakv, dhl)
    tie = emit(dict(w_in=dwt))
    return dx, g, tie


NATIVE = ("dmu", "dbre", "dbim", "dcre", "dcim", "dd", "dbv", "dbg", "dwa", "dba", "dlng", "dsink",
          "dg1", "db1", "dg2", "db2", "loss")
ICI_CORE = (0, 0, 0, 1, 1, 0, 0, 0, 1, 1, 1, 1, 0, 0, 1, 1, 0)


def _finish_small(n, w):
    g = {}
    dmu = n["dmu"]
    dlr = dmu[:, :, :, 0].reshape(DEPTH, 2, S5_G, S5_P)
    dli = dmu[:, :, :, 1].reshape(DEPTH, 2, S5_G, S5_P)

    def unblock(c, perm, shape):
        return c.reshape(DEPTH, 2, 2, S5_H, 8, S5_P).transpose(perm).reshape(shape)

    b_shape, c_shape = (DEPTH, 2, S5_G, S5_P, S5_H), (DEPTH, 2, S5_G, S5_H, S5_P)
    _, vjp = jax.vjp(_s5_discretize, w["s5_a_re"], w["s5_a_im"], w["s5_log_step"], w["s5_b_re"], w["s5_b_im"])
    (g["s5_a_re"], g["s5_a_im"], g["s5_log_step"], g["s5_b_re"], g["s5_b_im"]) = vjp(
        (dlr, dli, unblock(n["dbre"], (0, 1, 2, 4, 5, 3), b_shape), unblock(n["dbim"], (0, 1, 2, 4, 5, 3), b_shape)))
    g["s5_c_re"] = unblock(n["dcre"], (0, 1, 2, 4, 3, 5), c_shape)
    g["s5_c_im"] = unblock(n["dcim"], (0, 1, 2, 4, 3, 5), c_shape)
    g["s5_d"] = n["dd"].reshape(DEPTH, S5_G, S5_H)
    g["s5_b_glu"] = jnp.concatenate([n["dbv"], n["dbg"]], axis=2).reshape(DEPTH, 512)
    g["gla_w_a"] = jnp.stack([n["dwa"][:, 0:16, 0:128], n["dwa"][:, 16:32, 128:256]], axis=1)
    g["gla_b_a"] = n["dba"].reshape(DEPTH, 2, 128)
    g["gla_ln_g"] = n["dlng"].reshape(DEPTH, 256)
    g["swa_sink"] = n["dsink"][:, :, 0]
    for k, s in (("ln1_g", "dg1"), ("ln1_b", "db1"), ("ln2_g", "dg2"), ("ln2_b", "db2")):
        g[k] = n[s].reshape(DEPTH, D)
    return g


def _local_step(x, target, qs, tk, fetch, emit):
    saved = []
    for l, q in enumerate(qs):
        x, sv = _layer_fwd(x, q, tk, functools.partial(fetch, l), target if l == DEPTH - 1 else None)
        saved.append(sv)
    dy, lacc = x
    smalls = [None] * DEPTH
    tie = 0.0
    for l in reversed(range(DEPTH)):
        qs[l]["ln2_g"] = qs[l]["ln2_g"] + tie
        dy, smalls[l], tie = _layer_bwd(dy, qs[l], saved[l], tk, functools.partial(emit, l))
    smalls[0]["db2"] = smalls[0]["db2"] + tie
    for l in range(DEPTH):
        smalls[l]["loss"] = lacc if l == 0 else jnp.zeros_like(lacc)
    return lacc[0, 0], dy, smalls


BIG = ("w_in", "s5_w_glu", "w_out", "w_ff1", "w_ff2")
SMALL = ("s5_a_re", "s5_a_im", "s5_log_step", "s5_b_re", "s5_b_im", "s5_c_re", "s5_c_im", "s5_d", "s5_b_glu",
         "gla_w_a", "gla_b_a", "gla_ln_g", "swa_sink", "ln1_g", "ln1_b", "ln2_g", "ln2_b")
ANY = pl.BlockSpec(memory_space=pl.ANY)


def _place():
    x, y, c = lax.axis_index("x"), lax.axis_index("y"), lax.axis_index("c")
    return x, y, c, [(1 - x, y), (x, 1 - y), (1 - x, 1 - y)]


HBM = pl.BlockSpec(memory_space=pltpu.HBM)
SEMS = pl.BlockSpec(memory_space=pltpu.SEMAPHORE)
EFFECT = pltpu.SideEffectType.DATAFLOW_SIDE_EFFECTING


def _push_copies(ins, lands, send, recv, gather, sending):
    x, y, c, chips = _place()
    me = 2 * x + y
    if gather == "sibling":
        return [pltpu.make_async_remote_copy(src_ref=ins[a], dst_ref=lands[a], send_sem=send.at[a], recv_sem=recv.at[a],
                                             device_id=(x, y, 1 - c), device_id_type=MESH) for a in range(len(lands))]
    out = []
    for a in range(len(lands)):
        for j, (px, py) in enumerate(chips):
            peer = 2 * px + py
            src = lands[a].at[me] if gather else ins[a].at[peer if sending else me]
            dst = lands[a].at[me if sending else peer]
            out.append(pltpu.make_async_remote_copy(src_ref=src, dst_ref=dst, send_sem=send.at[3 * a + j],
                                                    recv_sem=recv.at[3 * a + j], device_id=(px, py, c),
                                                    device_id_type=MESH))
    return out


def _push_start(name, arrs, gather):
    n = len(arrs)
    ops = list(arrs) if gather is True else list(arrs) + [lax.empty(s.shape, s.dtype) for s in arrs]
    m = len(ops)

    def body(*refs):
        ins, lnd = (refs[:n], refs[:n]) if gather is True else (refs[:n], refs[n:m])
        for cp in _push_copies(ins, lnd, refs[m], refs[m + 1], gather, True):
            cp.start()
        refs[-1][...] = jnp.zeros((8, 128), F32)

    ops = [pltpu.with_memory_space_constraint(t, pltpu.HBM) for t in ops]
    res = pl.pallas_call(
        body, name=name,
        out_shape=(pltpu.SemaphoreType.DMA((3 * n,)), pltpu.SemaphoreType.DMA((3 * n,)),
                   *[pltpu.HBM(t.shape, t.dtype) for t in ops], _sds((8, 128))),
        in_specs=[HBM] * m,
        out_specs=(SEMS, SEMS, *[HBM] * m, pl.BlockSpec(memory_space=pltpu.VMEM)),
        input_output_aliases={i: 2 + i for i in range(m)},
        compiler_params=pltpu.CompilerParams(has_side_effects=EFFECT))(*ops)
    return res[0], res[1], list(res[2:2 + m]), res[-1]


def _push_wait(name, started, after, gather):
    send, recv, ops, _ = started
    m = len(ops)
    n = m if gather is True else m // 2

    def body(*refs):
        ins, lnd = (refs[:n], refs[:n]) if gather is True else (refs[:n], refs[n:m])
        for cp in _push_copies(ins, lnd, refs[m], refs[m + 1], gather, False):
            cp.wait_send()
            cp.wait_recv()

    res = pl.pallas_call(
        body, name=name,
        out_shape=[pltpu.HBM(t.shape, t.dtype) for t in ops],
        in_specs=[HBM] * m + [SEMS, SEMS, ANY], out_specs=[HBM] * m,
        input_output_aliases={i: i for i in range(m)},
        compiler_params=pltpu.CompilerParams(has_side_effects=EFFECT))(*ops, send, recv, after)
    return list(res)


def _row_tile(rows):
    return max(t for t in range(8, min(rows, 512) + 1, 8) if rows % t == 0)


def _cast_to_slot(me, w, l):
    _, rows, cols = w.shape
    tr = _row_tile(rows)

    def body(me_ref, w_ref, o_ref):
        o_ref[0] = w_ref[0].astype(MX)

    return pl.pallas_call(
        body,
        grid_spec=pltpu.PrefetchScalarGridSpec(
            num_scalar_prefetch=1, grid=(rows // tr,),
            in_specs=[pl.BlockSpec((1, tr, cols), lambda i, me_: (l, i, 0))],
            out_specs=pl.BlockSpec((1, tr, cols), lambda i, me_: (me_[0], i, 0))),
        out_shape=_sds((NSHARD, rows, cols), MX), name="cast_to_slot", compiler_params=_cp(("arbitrary",)))(me, w)


def _sum_sources(me, recv, own):
    _, rows, cols = recv[0].shape
    tr = min(_row_tile(rows), 256) if rows % 256 == 0 else _row_tile(rows)
    nt = rows // tr

    def body(me_ref, *refs):
        o_ref = refs[-1]
        for l in range(DEPTH):
            @pl.when(pl.program_id(0) == l)
            def _():
                r_ref, own_ref = refs[2 * l], refs[2 * l + 1]
                part = [jnp.where(me_ref[0] == s, own_ref[0], r_ref[s]).astype(F32) for s in range(NSHARD)]
                o_ref[...] = ((part[0] + part[1]) + part[2]) + part[3]

    in_specs = []
    for l in range(DEPTH):
        pick = lambda g, i, me_, l=l: jnp.where(g == l, i, jnp.where(g < l, 0, nt - 1))
        in_specs += [pl.BlockSpec((NSHARD, tr, cols), lambda g, i, me_, pick=pick: (0, pick(g, i, me_), 0)),
                     pl.BlockSpec((1, tr, cols), lambda g, i, me_, pick=pick: (me_[0], pick(g, i, me_), 0))]
    return pl.pallas_call(
        body,
        grid_spec=pltpu.PrefetchScalarGridSpec(
            num_scalar_prefetch=1, grid=(DEPTH, nt), in_specs=in_specs,
            out_specs=pl.BlockSpec((tr, cols), lambda g, i, me_: (g * nt + i, 0))),
        out_shape=_sds((DEPTH * rows, cols)), name="sum_sources",
        compiler_params=_cp(("arbitrary", "arbitrary")))(me, *[t for l in range(DEPTH) for t in (recv[l], own[l])])


def _swap_sibling(arrs):
    n = len(arrs)

    def body(*refs):
        ins, outs = refs[:n], refs[n:2 * n]
        send, recv = refs[2 * n:]
        x, y, c, _ = _place()
        cps = [pltpu.make_async_remote_copy(src_ref=ins[a], dst_ref=outs[a], send_sem=send.at[a], recv_sem=recv.at[a],
                                            device_id=(x, y, 1 - c), device_id_type=MESH) for a in range(n)]
        for cp in cps:
            cp.start()
        for cp in cps:
            cp.wait()

    return pl.pallas_call(
        body, in_specs=[ANY] * n, out_specs=[ANY] * n, out_shape=[_sds(a.shape, a.dtype) for a in arrs],
        scratch_shapes=[pltpu.SemaphoreType.DMA((n,)), pltpu.SemaphoreType.DMA((n,))],
        name="swap_sibling")(*arrs)


def _allreduce_small(per_layer):
    nk = len(per_layer[0])
    n = DEPTH * nk
    shapes = [a.shape for a in per_layer[0]]

    def body(*refs):
        ins, outs = refs[:n], refs[n:n + nk]
        sibs, slots = refs[n + nk:n + 2 * nk], refs[n + 2 * nk:n + 3 * nk]
        send, recv = refs[n + 3 * nk:]
        x, y, c, chips = _place()
        me = 2 * x + y
        d2d = [pltpu.make_async_remote_copy(src_ref=ins[l * nk + k], dst_ref=sibs[k].at[l], send_sem=send.at[l * nk + k],
                                            recv_sem=recv.at[l * nk + k], device_id=(x, y, 1 - c), device_id_type=MESH)
               for l in range(DEPTH) for k in range(nk)]
        for cp in d2d:
            cp.start()
        for cp in d2d:
            cp.wait()
        for l in range(DEPTH):
            for k in range(nk):
                slots[k][0, l] = ins[l * nk + k][...] + sibs[k][l]

        def swap(k, stage):
            peer = (1 - x, y, c) if stage == 0 else (x, 1 - y, c)
            return pltpu.make_async_remote_copy(src_ref=slots[k].at[2 * stage], dst_ref=slots[k].at[2 * stage + 1],
                                                send_sem=send.at[n + 3 * k + stage], recv_sem=recv.at[n + 3 * k + stage],
                                                device_id=peer, device_id_type=MESH)

        def handover(k):
            return pltpu.make_async_remote_copy(src_ref=outs[k], dst_ref=outs[k], send_sem=send.at[n + 3 * nk + k],
                                                recv_sem=recv.at[n + 3 * nk + k], device_id=(x, y, 1 - c),
                                                device_id_type=MESH)

        halves = (tuple(k for k in range(nk) if ICI_CORE[k] == 0), tuple(k for k in range(nk) if ICI_CORE[k] == 1))
        for cc in range(2):
            @pl.when(c == cc)
            def _():
                mine, theirs = halves[cc], halves[1 - cc]
                for stage in range(2):
                    cps = [swap(k, stage) for k in mine]
                    for cp in cps:
                        cp.start()
                    for cp in cps:
                        cp.wait()
                    for k in mine:
                        if stage == 0:
                            slots[k][2] = slots[k][0] + slots[k][1]
                        else:
                            outs[k][...] = slots[k][2] + slots[k][3]
                over = [handover(k) for k in mine]
                for cp in over:
                    cp.start()
                for k in theirs:
                    handover(k).wait_recv()
                for cp in over:
                    cp.wait_send()

    vm = pl.BlockSpec(memory_space=pltpu.VMEM)
    return pl.pallas_call(
        body, in_specs=[vm] * n, out_specs=[vm] * nk, out_shape=[_sds((DEPTH,) + s) for s in shapes],
        scratch_shapes=([pltpu.VMEM((DEPTH,) + s, F32) for s in shapes]
                        + [pltpu.VMEM((NSHARD, DEPTH) + s, F32) for s in shapes]
                        + [pltpu.SemaphoreType.DMA((n + 4 * nk,)), pltpu.SemaphoreType.DMA((n + 4 * nk,))]),
        name="allreduce_small", compiler_params=pltpu.CompilerParams(vmem_limit_bytes=VMEM_LIMIT))(
            *[a for layer in per_layer for a in layer])


def _adamw_math(w, g, m, v):
    m = ADAM_B1 * m + (1.0 - ADAM_B1) * g
    v = ADAM_B2 * v + (1.0 - ADAM_B2) * jnp.square(g)
    m_hat = m / (1.0 - ADAM_B1 ** ADAM_STEP)
    v_hat = v / (1.0 - ADAM_B2 ** ADAM_STEP)
    delta = -ADAM_LR * (m_hat / (jnp.sqrt(v_hat) + ADAM_EPS) + ADAM_WD * w)
    return delta, m, v


def _adamw(g_parts, w, m, v):
    rows, cols = w.shape
    tr = 256 if rows % 256 == 0 else _row_tile(rows)
    k = len(g_parts)

    def body(*refs):
        g = refs[0][...]
        for r in refs[1:k]:
            g = g + r[...]
        w_ref, m_ref, v_ref, go, do, mo, vo = refs[k:]
        d, mn, vn = _adamw_math(w_ref[...], g, m_ref[...], v_ref[...])
        go[...] = g
        do[...] = d
        mo[...] = mn
        vo[...] = vn

    spec = pl.BlockSpec((tr, cols), lambda i: (i, 0))
    return pl.pallas_call(
        body, grid=(rows // tr,), in_specs=[spec] * (k + 3), out_specs=[spec] * 4,
        out_shape=[_sds((rows, cols))] * 4, name="adamw", compiler_params=_cp(("parallel",)))(*g_parts, w, m, v)


def _adamw_small(gs, ws, ms, vs):
    n = len(gs)

    def body(*refs):
        for k in range(n):
            d, mn, vn = _adamw_math(refs[n + k][...], refs[k][...], refs[2 * n + k][...], refs[3 * n + k][...])
            refs[4 * n + k][...] = d
            refs[5 * n + k][...] = mn
            refs[6 * n + k][...] = vn

    vm = pl.BlockSpec(memory_space=pltpu.VMEM)
    shapes = [_sds(a.shape) for a in ws]
    res = pl.pallas_call(
        body, in_specs=[vm] * (4 * n), out_specs=[vm] * (3 * n), out_shape=shapes * 3, name="adamw_small",
        compiler_params=pltpu.CompilerParams(vmem_limit_bytes=VMEM_LIMIT))(*gs, *ws, *ms, *vs)
    return res[:n], res[n:2 * n], res[2 * n:]


_ARGS = ("x", "w_in", "s5_a_re", "s5_a_im", "s5_log_step", "s5_b_re", "s5_b_im", "s5_c_re", "s5_c_im", "s5_d",
         "s5_w_glu", "s5_b_glu", "gla_w_a", "gla_b_a", "gla_ln_g", "swa_sink", "w_out", "ln1_g", "ln1_b", "w_ff1",
         "w_ff2", "ln2_g", "ln2_b")
_WEIGHTS = _ARGS[1:]


def _shard_cols(d):
    return d.reshape(d.shape[0], NSHARD, d.shape[1] // NSHARD).transpose(1, 0, 2)


def kernel(x, w_in, s5_a_re, s5_a_im, s5_log_step, s5_b_re, s5_b_im, s5_c_re, s5_c_im, s5_d, s5_w_glu, s5_b_glu, gla_w_a, gla_b_a, gla_ln_g, swa_sink, w_out, ln1_g, ln1_b, w_ff1, w_ff2, ln2_g, ln2_b, loss_target, m_w_in, m_s5_a_re, m_s5_a_im, m_s5_log_step, m_s5_b_re, m_s5_b_im, m_s5_c_re, m_s5_c_im, m_s5_d, m_s5_w_glu, m_s5_b_glu, m_gla_w_a, m_gla_b_a, m_gla_ln_g, m_swa_sink, m_w_out, m_ln1_g, m_ln1_b, m_w_ff1, m_w_ff2, m_ln2_g, m_ln2_b, v_w_in, v_s5_a_re, v_s5_a_im, v_s5_log_step, v_s5_b_re, v_s5_b_im, v_s5_c_re, v_s5_c_im, v_s5_d, v_s5_w_glu, v_s5_b_glu, v_gla_w_a, v_gla_b_a, v_gla_ln_g, v_swa_sink, v_w_out, v_ln1_g, v_ln1_b, v_w_ff1, v_w_ff2, v_ln2_g, v_ln2_b):
    given = dict(locals())
    w = {k: given[k] for k in _WEIGHTS}
    mom = {k: given["m_" + k] for k in _WEIGHTS}
    var = {k: given["v_" + k] for k in _WEIGHTS}

    me = (2 * lax.axis_index("x") + lax.axis_index("y")).astype(jnp.int32).reshape(1)
    tr = lambda t: t.transpose(0, 2, 1)
    shard = {k: (tr(w[k]) if k == "w_in" else w[k]) for k in BIG}
    qs = [None] * DEPTH

    first = ("w_in", "s5_w_glu", "w_out")
    follow = {(0, "w_in"): [(0, BIG[3:])], (0, "s5_w_glu"): [(1, first)], (0, "w_ff1"): [(1, BIG[3:])]}
    gathers = {}

    casts = {}

    def start_gather(l, names, behind=None):
        lands = [casts.pop((l, k)) if (l, k) in casts else _cast_to_slot(me, shard[k], l) for k in names]
        if behind is not None:
            lands, behind = lax.optimization_barrier((lands, behind))
        st = _push_start(f"gather_start_{l}_{names[0]}", lands, True)
        for k in names:
            gathers[l, k] = [names, st, None]
        return st[-1], behind

    token = start_gather(0, first[:1])[0] + start_gather(0, first[1:])[0]
    zero = token[0, 0]
    for l in range(DEPTH):
        for k in BIG:
            if (l, k) not in gathers:
                casts[l, k] = _cast_to_slot(me, lax.optimization_barrier((shard[k], token))[0], l)
        qs[l] = _layer_prep({k: (w[k][l] + zero if k == "s5_a_re" else w[k][l]) for k in SMALL})
    token, casts, qs = lax.optimization_barrier((token, casts, qs))

    def fetch(l, name, after):
        names, st, got = gathers[l, name]
        tie = None
        if got is None:
            if l == 0 and name == "w_in":
                after = token
            lands = _push_wait(f"gather_wait_{l}_{names[0]}", st, after, True)
            for l2, names2 in follow.get((l, name), ()):
                tok, lands[0] = start_gather(l2, names2, lands[0])
                tie = tok if tie is None else tie + tok
            got = dict(zip(names, lands))
            for k in names:
                gathers[l, k][2] = got
        full = got[name]
        if name == "w_in":
            return _in_rows(full, token if tie is None else tie)
        if tie is not None:
            near = "bv" if name == "s5_w_glu" else "ln2_b"
            qs[l][near] = qs[l][near] + tie[0, 0]
        return full.reshape(D, D) if name == "w_out" else full

    scatters, held = [], {}

    def emit(l, grads):
        if l > 0:
            held.update(grads)
            if "w_in" not in grads:
                return 0.0
            grads = dict(held)
            held.clear()
        names = tuple(grads)
        st = _push_start(f"scatter_start_{l}_{names[0]}", [grads[k] for k in names], False)
        scatters.append((l, names, st))
        return st[-1][0, 0]

    loss, dx, smalls = _local_step(x.reshape(N, D), loss_target.reshape(N, D), qs, _rope_tables(128), fetch, emit)

    out, recv, own = {}, {}, {}

    def collect(keys, after):
        for l, names, st in scatters:
            if names[0] in keys:
                ops = _push_wait(f"scatter_wait_{l}_{names[0]}", st, after, False)
                for i, k in enumerate(names):
                    own[l, k], recv[l, k] = ops[i], ops[len(names) + i]

    def to_sibling(keys):
        sums = [_sum_sources(me, [recv[l, k] for l in range(DEPTH)], [own[l, k] for l in range(DEPTH)]) for k in keys]
        return _push_start(f"swap_start_{keys[0]}", sums, "sibling")

    def apply(keys, started, after):
        ops = _push_wait(f"swap_wait_{keys[0]}", started, after, "sibling")
        for i, k in enumerate(keys):
            mine, other = ops[i], ops[len(keys) + i]
            shp = shard[k].shape
            r = _adamw([mine, other], *((tr(t[k]) if k == "w_in" else t[k]).reshape(-1, shp[-1]) for t in (w, mom, var)))
            r = [t.reshape(shp) for t in r]
            out[k] = [tr(t) for t in r] if k == "w_in" else r
        return out[keys[-1]][1]

    collect(("w_ff1", "w_ff2", "w_out", "s5_w_glu"), dx)
    ff = to_sibling(("w_ff1", "w_ff2"))
    mix = to_sibling(("w_out", "s5_w_glu"))
    smalls[0]["db1"] = smalls[0]["db1"] + (ff[-1][0, 0] + mix[-1][0, 0])
    native = _allreduce_small([[smalls[l][k] for k in NATIVE] for l in range(DEPTH)])
    native = dict(zip(NATIVE, native))
    loss = native["loss"][0, 0, 0] + native["loss"][1, 0, 0]
    gsmall = _finish_small(native, w)
    view = lambda k, t: t.transpose(0, 1, 2, 4, 3) if k in ("s5_b_re", "s5_b_im") else t
    res = _adamw_small(*([view(k, t[k]) for k in SMALL] for t in (gsmall, w, mom, var)))
    for i, k in enumerate(SMALL):
        out[k] = [gsmall[k]] + [view(k, r[i]) for r in res]
    last = apply(("w_ff1", "w_ff2"), ff, res[0][-1])
    collect(("w_in",), last)
    win = to_sibling(("w_in",))
    last = apply(("w_out", "s5_w_glu"), mix, win[-1])
    apply(("w_in",), win, last)

    return (loss, dx.reshape(NSEQ, L, D), *[out[k][0] for k in _WEIGHTS], *[out[k][1] for k in _WEIGHTS],
            *[out[k][2] for k in _WEIGHTS], *[out[k][3] for k in _WEIGHTS])
```

```python
import functools
import math

import jax
import jax.numpy as jnp
from jax import lax
from jax.experimental import pallas as pl
from jax.experimental.pallas import tpu as pltpu

F32 = jnp.float32
MX = jnp.bfloat16
MESH = pl.DeviceIdType.MESH

DEPTH = 2
NSEQ = 2
L = 2048
N = NSEQ * L
D = 1024
DFF = 4096
NSHARD = 4
S5_G, S5_H, S5_P = 16, 16, 64
GLA_CHUNK = 64
NCHUNK = L // GLA_CHUNK
GLA_GROUP = 4
NGROUP = NCHUNK // GLA_GROUP
SWA_BLK = 128
NBLK = L // SWA_BLK
ROT = 16
ROPE_THETA = 500000.0
LN_EPS = 1e-5
ALPHA = (2 * DEPTH) ** 0.25
NEG_BIG = -1e30
DIN = 1824
DINP = 1920
ADAM_LR, ADAM_B1, ADAM_B2, ADAM_EPS, ADAM_WD, ADAM_STEP = 0.001, 0.9, 0.999, 1e-08, 0.01, 10
VMEM_LIMIT = 56 * 1024 * 1024
TT = 512
SW = 512
FFN_TM = 512
FFN_TM_W = 1024
FFN_WB = 1
FFN_VMEM = 60 * 1024 * 1024
INPROJ_BWD_TM = 512


def _cp(sem, vmem=VMEM_LIMIT):
    return pltpu.CompilerParams(dimension_semantics=sem, vmem_limit_bytes=vmem)


def _mm(a, b):
    return jnp.dot(a.astype(MX), b.astype(MX), preferred_element_type=F32)


def _mm_nt(a, b):
    return lax.dot_general(a.astype(MX), b.astype(MX), (((1,), (1,)), ((), ())), preferred_element_type=F32)


def _mm_tn(a, b):
    return lax.dot_general(a.astype(MX), b.astype(MX), (((0,), (0,)), ((), ())), preferred_element_type=F32)


@jax.custom_vjp
def _dmm(a, b):
    return _mm(a, b)


_dmm.defvjp(lambda a, b: (_mm(a, b), (a, b)), lambda r, g: (_mm_nt(g, r[1]), _mm_tn(r[0], g)))


@jax.custom_vjp
def _dmm_nt(a, b):
    return _mm_nt(a, b)


_dmm_nt.defvjp(lambda a, b: (_mm_nt(a, b), (a, b)), lambda r, g: (_mm(g, r[1]), _mm_tn(g, r[0])))


@jax.custom_vjp
def _dmm_tn(a, b):
    return _mm_tn(a, b)


_dmm_tn.defvjp(lambda a, b: (_mm_tn(a, b), (a, b)), lambda r, g: (_mm_nt(r[1], g), _mm(r[0], g)))


def _split3(x):
    hi = x.astype(MX)
    r1 = x - hi.astype(F32)
    mid = r1.astype(MX)
    lo = (r1 - mid.astype(F32)).astype(MX)
    return hi, mid, lo


def _chunk_pairs(rows, rev, strict):
    r = lax.broadcasted_iota(jnp.int32, (rows, rows), 0)
    c = lax.broadcasted_iota(jnp.int32, (rows, rows), 1)
    order = ((c > r) if strict else (c >= r)) if rev else ((c < r) if strict else (c <= r))
    return (r // GLA_CHUNK == c // GLA_CHUNK) & order


def _cums_impl(x, rev):
    rows, w = x.shape
    t = jnp.where(_chunk_pairs(rows, rev, False), 1.0, 0.0).astype(MX)
    s = jnp.dot(t, jnp.concatenate(_split3(x), axis=1), preferred_element_type=F32)
    return s[:, 0:w] + s[:, w:2 * w] + s[:, 2 * w:3 * w]


@functools.partial(jax.custom_vjp, nondiff_argnums=(1,))
def _cums(x, rev):
    return _cums_impl(x, rev)


_cums.defvjp(lambda x, rev: (_cums_impl(x, rev), None), lambda rev, r, g: (_cums_impl(g, not rev),))


def _ln_fwd(s, g, b):
    mu = jnp.mean(s, axis=-1, keepdims=True)
    xc = s - mu
    var = jnp.mean(xc * xc, axis=-1, keepdims=True)
    return xc * lax.rsqrt(var + LN_EPS) * g + b


def _ln_bwd(dy, s, g):
    mu = jnp.mean(s, axis=-1, keepdims=True)
    xc = s - mu
    var = jnp.mean(xc * xc, axis=-1, keepdims=True)
    rstd = lax.rsqrt(var + LN_EPS)
    xhat = xc * rstd
    dxh = dy * g
    ds = rstd * (dxh - jnp.mean(dxh, axis=-1, keepdims=True) - xhat * jnp.mean(dxh * xhat, axis=-1, keepdims=True))
    return ds, jnp.sum(dy * xhat, axis=0, keepdims=True), jnp.sum(dy, axis=0, keepdims=True)


def _sds(shape, dtype=F32):
    return jax.ShapeDtypeStruct(shape, dtype)


_IN_ROW_PIECES = (((0, 0), (0, 456)), ((1, 0), (456, 456)), ((2, 0), (912, 112)), ((2, 112), (1792, 32)),
                  ((2, 144), (1024, 312)), ((3, 0), (1336, 456)))


def _in_rows(g4, behind):
    def body(g_ref, behind_ref, o_ref, tmp):
        tmp[DIN:DINP] = jnp.zeros((DINP - DIN, D), F32)
        for (j, s0), (d0, n_) in _IN_ROW_PIECES:
            tmp[d0:d0 + n_] = g_ref[j, s0:s0 + n_].astype(F32)
        o_ref[...] = tmp[...].astype(MX)

    vm = pl.BlockSpec(memory_space=pltpu.VMEM)
    return pl.pallas_call(body, in_specs=[vm, pl.BlockSpec(memory_space=pl.ANY)], out_specs=vm,
                          out_shape=_sds((DINP, D), MX), scratch_shapes=[pltpu.VMEM((DINP, D), F32)], name="in_rows",
                          compiler_params=pltpu.CompilerParams(vmem_limit_bytes=VMEM_LIMIT))(g4, behind)


def _inproj_fwd(x, wt, wa, ba):
    tm = 512

    def body(x_ref, w_ref, wa_ref, ba_ref, h_ref, la_ref):
        h = _mm_nt(x_ref[...], w_ref[...])
        h_ref[...] = h
        la_ref[...] = _logsig(_mm(h[:, DINP - 128:], wa_ref[...]) + ba_ref[...]) * (1.0 / 16.0)

    return pl.pallas_call(
        body, grid=(N // tm,),
        in_specs=[pl.BlockSpec((tm, D), lambda i: (i, 0)), pl.BlockSpec((DINP, D), lambda i: (0, 0)),
                  pl.BlockSpec((128, 256), lambda i: (0, 0)), pl.BlockSpec((1, 256), lambda i: (0, 0))],
        out_specs=[pl.BlockSpec((tm, DINP), lambda i: (i, 0)), pl.BlockSpec((tm, 256), lambda i: (i, 0))],
        out_shape=[_sds((N, DINP)), _sds((N, 256))], name="inproj_fwd", compiler_params=_cp(("parallel",)))(x, wt, wa, ba)


def _inproj_bwd(x, w, dxp, du2, dud, gq_f, gq_b, gk_f, gk_b, gv_f, gv_b, gr, daq, dakv, dhl):
    tm = INPROJ_BWD_TM
    nt = N // tm

    def body(x_ref, w_ref, dxp_ref, du2_ref, dud_ref, gqf, gqb, gkf, gkb, gvf, gvb, gr_ref, daq_ref, dakv_ref, dhl_ref,
             dx_ref, dw_ref, acc):
        i = pl.program_id(0)
        f = lambda r: r[...].astype(F32)
        dh = jnp.concatenate([
            du2_ref[0] + du2_ref[1] + f(dud_ref), f(gqf) + f(gqb), f(gkf) + f(gkb), f(gvf) + f(gvb),
            f(gr_ref), f(daq_ref), f(dakv_ref), f(dhl_ref)], axis=1)
        dx_ref[...] = dxp_ref[...] + _mm(dh, w_ref[...])
        contrib = _mm_tn(dh, x_ref[...])

        @pl.when(i == 0)
        def _():
            acc[...] = contrib

        @pl.when(i > 0)
        def _():
            acc[...] += contrib

        @pl.when(i == nt - 1)
        def _():
            for (j, d0), (s0, n_) in _IN_ROW_PIECES:
                dw_ref[j, d0:d0 + n_] = acc[s0:s0 + n_].astype(MX)

    row = lambda w_: pl.BlockSpec((tm, w_), lambda i: (i, 0))
    return pl.pallas_call(
        body, grid=(nt,),
        in_specs=[row(D), pl.BlockSpec((DINP, D), lambda i: (0, 0)), row(D),
                  pl.BlockSpec((2, tm, 256), lambda i: (0, i, 0)), row(256), row(128), row(128), row(128), row(128),
                  row(256), row(256), row(256), row(512), row(256), row(128)],
        out_specs=[row(D), pl.BlockSpec((NSHARD, DIN // NSHARD, D), lambda i: (0, 0, 0))],
        out_shape=[_sds((N, D)), _sds((NSHARD, DIN // NSHARD, D), MX)],
        scratch_shapes=[pltpu.VMEM((DINP, D), F32)],
        name="inproj_bwd", compiler_params=_cp(("arbitrary",)))(
            x, w, dxp, du2, dud, gq_f, gq_b, gk_f, gk_b, gv_f, gv_b, gr, daq, dakv, dhl)


def _tile_scan(xr, xi, a, cr, ci, reverse):
    for lvl, d in enumerate((1, 2, 4)):
        sh = 8 - d if reverse else d
        sr = pltpu.roll(xr, sh, 0)
        si = pltpu.roll(xi, sh, 0)
        ar, ai = a[2 * lvl], a[2 * lvl + 1]
        xr, xi = xr + ar * sr - ai * si, xi + ar * si + ai * sr
    pr, pi = a[6], a[7]
    return xr + pr * cr - pi * ci, xi + pr * ci + pi * cr


NJ = TT // 8


def _lockstep_tables(mr, mi):
    def body(mr_ref, mi_ref, a_ref, p_ref, ac_ref, pc_ref):
        rowid = lax.broadcasted_iota(jnp.int32, (8, 2 * SW), 0)

        def mul(a, b):
            return a[0] * b[0] - a[1] * b[1], a[0] * b[1] + a[1] * b[0]

        for z in range(2):
            for sign, reverse, a_out, p_out in ((1.0, z == 1, a_ref, p_ref), (-1.0, z == 0, ac_ref, pc_ref)):
                m = (mr_ref[z:z + 1, :], sign * mi_ref[z:z + 1, :])
                pw = [m]
                for _ in range(NJ - 1):
                    pw.append(mul(pw[-1], m))
                n = pw[-1]
                link = [n]
                for _ in range(7):
                    link.append(mul(link[-1], n))
                tiles = [jnp.broadcast_to(m[0], (8, 2 * SW)), jnp.broadcast_to(m[1], (8, 2 * SW))]
                for d in (1, 2, 4):
                    keep = (rowid <= 7 - d) if reverse else (rowid >= d)
                    tiles += [jnp.where(keep, link[d - 1][c], 0.0) for c in range(2)]
                for c in range(2):
                    t = jnp.zeros((8, 2 * SW), F32)
                    for i in range(8):
                        t = jnp.where(rowid == (7 - i if reverse else i), link[i][c], t)
                    tiles.append(t)
                for blk in range(2):
                    lanes = slice(blk * SW, (blk + 1) * SW)
                    for k, t in enumerate(tiles):
                        a_out[z, blk, k] = t[:, lanes]
                    for j in range(NJ):
                        src = pw[NJ - 1 - j] if reverse else pw[j]
                        for c in range(2):
                            p_out[z, blk, c, j:j + 1, :] = src[c][:, lanes]

    vm = pl.BlockSpec(memory_space=pltpu.VMEM)
    a_shape, p_shape = _sds((2, 2, 10, 8, SW)), _sds((2, 2, 2, NJ, SW))
    a, p, ac, pc = pl.pallas_call(body, in_specs=[vm, vm], out_specs=[vm] * 4, out_shape=[a_shape, p_shape] * 2,
                                  name="s5_tables")(mr, mi)
    return (a, p), (ac, pc)


def _to_lockstep(ref, *lead):
    return jnp.concatenate([ref[(*lead, pl.ds(j, 8, stride=NJ), slice(None))] for j in range(NJ)], axis=0)


def _from_lockstep(val, ref, *lead):
    for j in range(NJ):
        ref[(*lead, pl.ds(j, 8, stride=NJ), slice(None))] = val[8 * j:8 * j + 8]


def _expand_powers(p_ref, pexp):
    for c in range(2):
        for j in range(NJ):
            pexp[c, j] = jnp.broadcast_to(p_ref[0, 0, c, j:j + 1, :], (8, SW))


def _lockstep_scan(xre, xim, a_ref, pexp, car, reverse, extra=None):
    a = [a_ref[0, 0, k] for k in range(10)]
    mr, mi = a[0], a[1]
    order = (lambda i: NJ - 1 - i) if reverse else (lambda i: i)

    def local(i, hcar):
        hr, hi = hcar
        r0 = pl.multiple_of(order(i) * 8, 8)
        hr, hi = mr * hr - mi * hi + xre[pl.ds(r0, 8), :], mr * hi + mi * hr + xim[pl.ds(r0, 8), :]
        xre[pl.ds(r0, 8), :] = hr
        xim[pl.ds(r0, 8), :] = hi
        return hr, hi

    z8 = jnp.zeros((8, SW), F32)
    er, ei = lax.fori_loop(0, NJ, local, (z8, z8), unroll=4)
    c0r, c0i = car[0], car[1]
    er, ei = _tile_scan(er, ei, a[2:], c0r, c0i, reverse)
    rowid = lax.broadcasted_iota(jnp.int32, (8, SW), 0)
    first, sh, last = (7, 7, 0) if reverse else (0, 1, 7)
    cvr = jnp.where(rowid == first, c0r, pltpu.roll(er, sh, 0))
    cvi = jnp.where(rowid == first, c0i, pltpu.roll(ei, sh, 0))
    car[0] = jnp.broadcast_to(er[last:last + 1, :], (8, SW))
    car[1] = jnp.broadcast_to(ei[last:last + 1, :], (8, SW))

    def fix(i, carry):
        j = order(i)
        r0 = pl.multiple_of(j * 8, 8)
        pr, pi = pexp[0, j], pexp[1, j]
        sr = xre[pl.ds(r0, 8), :] + pr * cvr - pi * cvi
        si = xim[pl.ds(r0, 8), :] + pr * cvi + pi * cvr
        xre[pl.ds(r0, 8), :] = sr
        xim[pl.ds(r0, 8), :] = si
        if extra is None:
            return carry
        return (sr, si, extra(r0, sr, si, carry[0], carry[1], carry[2]))

    init = (cvr, cvi, extra(None, None, None, None, None, None)) if extra is not None else 0
    return lax.fori_loop(0, NJ, fix, init, unroll=4)


def _s5_time_block(z, s, t, adjoint):
    flip = (1 - z) if adjoint else z
    return s * (L // TT) + t + flip * (L // TT - 1 - 2 * t)


def _s5_fwd(h, bre, bim, cre, cim, tab):
    nt = L // TT
    taba, tabp = tab

    def body(u_ref, bre_ref, bim_ref, cre_ref, cim_ref, a_ref, p_ref, hre_ref, him_ref, y_ref, car, pexp):
        z = pl.program_id(1)
        s = pl.program_id(2)
        tc = pl.program_id(3)

        @pl.when(tc == 0)
        def _():
            car[...] = jnp.zeros_like(car)

        @pl.when((tc == 0) & (s == 0))
        def _():
            _expand_powers(p_ref, pexp)

        u = _to_lockstep(u_ref)
        hre_ref[0] = _mm(u, bre_ref[0, 0])
        him_ref[0] = _mm(u, bim_ref[0, 0])

        @pl.when(z == 0)
        def _():
            _lockstep_scan(hre_ref.at[0], him_ref.at[0], a_ref, pexp, car, False)

        @pl.when(z == 1)
        def _():
            _lockstep_scan(hre_ref.at[0], him_ref.at[0], a_ref, pexp, car, True)

        _from_lockstep(_mm(hre_ref[0], cre_ref[0, 0]) - _mm(him_ref[0], cim_ref[0, 0]), y_ref, 0)

    tb = lambda b, z, s, t: _s5_time_block(z, s, t, False)
    wspec = lambda r, c: pl.BlockSpec((1, 1, r, c), lambda b, z, s, t: (z, b, 0, 0))
    return pl.pallas_call(
        body, grid=(2, 2, NSEQ, nt),
        in_specs=[pl.BlockSpec((TT, 128), lambda b, z, s, t: (tb(b, z, s, t), b)),
                  wspec(128, SW), wspec(128, SW), wspec(SW, 128), wspec(SW, 128),
                  pl.BlockSpec((1, 1, 10, 8, SW), lambda b, z, s, t: (z, b, 0, 0, 0)),
                  pl.BlockSpec((1, 1, 2, NJ, SW), lambda b, z, s, t: (z, b, 0, 0, 0))],
        out_specs=[pl.BlockSpec((1, TT, SW), lambda b, z, s, t: (z, tb(b, z, s, t), b)),
                   pl.BlockSpec((1, TT, SW), lambda b, z, s, t: (z, tb(b, z, s, t), b)),
                   pl.BlockSpec((1, TT, 128), lambda b, z, s, t: (z, tb(b, z, s, t), b))],
        out_shape=[_sds((2, N, 2 * SW)), _sds((2, N, 2 * SW)), _sds((2, N, 256))],
        scratch_shapes=[pltpu.VMEM((2, 8, SW), F32), pltpu.VMEM((2, NJ, 8, SW), F32)],
        name="s5_fwd", compiler_params=_cp(("arbitrary",) * 4))(h, bre, bim, cre, cim, taba, tabp)


def _s5_bwd(h, dyp, hre, him, bre, bim, cre, cim, tabc):
    nt = L // TT
    taba, tabp = tabc

    def body(u_ref, dy_ref, hre_ref, him_ref, bre_ref, bim_ref, cre_ref, cim_ref, a_ref, p_ref,
             du_ref, dbre_ref, dbim_ref, dcre_ref, dcim_ref, dmu_ref, gre, gim, car, acc, macc, pexp):
        z = pl.program_id(1)
        s = pl.program_id(2)
        tc = pl.program_id(3)

        @pl.when(tc == 0)
        def _():
            car[...] = jnp.zeros_like(car)

        @pl.when((tc == 0) & (s == 0))
        def _():
            acc[...] = jnp.zeros_like(acc)
            macc[...] = jnp.zeros_like(macc)
            _expand_powers(p_ref, pexp)

        dy = _to_lockstep(dy_ref)
        gre[...] = _mm_nt(dy, cre_ref[0, 0])
        gim[...] = -_mm_nt(dy, cim_ref[0, 0])

        def run(reverse):
            def pair(r0, gr_, gi_, pvr, pvi, m):
                if r0 is None:
                    return (macc[0], macc[1])
                hr = hre_ref[0, pl.ds(r0, 8), :]
                hi = him_ref[0, pl.ds(r0, 8), :]
                return (m[0] + pvr * hr + pvi * hi, m[1] + pvi * hr - pvr * hi)

            _, _, (dmr, dmi) = _lockstep_scan(gre, gim, a_ref, pexp, car, reverse, pair)
            macc[0] = dmr
            macc[1] = dmi

        @pl.when(z == 0)
        def _():
            run(True)

        @pl.when(z == 1)
        def _():
            run(False)

        gr = gre[...]
        gi = gim[...]
        u = _to_lockstep(u_ref)
        _from_lockstep(_mm_nt(gr, bre_ref[0, 0]) + _mm_nt(gi, bim_ref[0, 0]), du_ref, 0)
        acc[0] += _mm_tn(u, gr)
        acc[1] += _mm_tn(u, gi)
        acc[2] += _mm_tn(dy, hre_ref[0])
        acc[3] -= _mm_tn(dy, him_ref[0])

        @pl.when((tc == nt - 1) & (s == NSEQ - 1))
        def _():
            grp = lax.broadcasted_iota(jnp.int32, (S5_H, SW), 1) // S5_P
            for k, out in enumerate((dbre_ref, dbim_ref, dcre_ref, dcim_ref)):
                c = jnp.zeros((S5_H, SW), F32)
                for i in range(8):
                    c = c + jnp.where(grp == i, acc[k, i * S5_H:(i + 1) * S5_H, :], 0.0)
                out[0, 0] = c
            dmu_ref[0, 0] = jnp.concatenate([jnp.sum(macc[0], axis=0, keepdims=True),
                                             jnp.sum(macc[1], axis=0, keepdims=True)], axis=0)

    tb = lambda b, z, s, t: _s5_time_block(z, s, t, True)
    wspec = lambda r, c: pl.BlockSpec((1, 1, r, c), lambda b, z, s, t: (z, b, 0, 0))
    tok = lambda w_: pl.BlockSpec((TT, w_), lambda b, z, s, t: (tb(b, z, s, t), b))
    st = pl.BlockSpec((1, TT, SW), lambda b, z, s, t: (z, tb(b, z, s, t), b))
    return pl.pallas_call(
        body, grid=(2, 2, NSEQ, nt),
        in_specs=[tok(128), tok(128), st, st, wspec(128, SW), wspec(128, SW), wspec(SW, 128), wspec(SW, 128),
                  pl.BlockSpec((1, 1, 10, 8, SW), lambda b, z, s, t: (z, b, 0, 0, 0)),
                  pl.BlockSpec((1, 1, 2, NJ, SW), lambda b, z, s, t: (z, b, 0, 0, 0))],
        out_specs=[pl.BlockSpec((1, TT, 128), lambda b, z, s, t: (z, tb(b, z, s, t), b)),
                   wspec(S5_H, SW), wspec(S5_H, SW), wspec(S5_H, SW), wspec(S5_H, SW),
                   wspec(2, SW)],
        out_shape=[_sds((2, N, 256))] + [_sds((2, 2, S5_H, SW))] * 4 + [_sds((2, 2, 2, SW))],
        scratch_shapes=[pltpu.VMEM((TT, SW), F32), pltpu.VMEM((TT, SW), F32), pltpu.VMEM((2, 8, SW), F32),
                        pltpu.VMEM((4, 128, SW), F32), pltpu.VMEM((2, 8, SW), F32), pltpu.VMEM((2, NJ, 8, SW), F32)],
        name="s5_bwd", compiler_params=_cp(("arbitrary",) * 4))(h, dyp, hre, him, bre, bim, cre, cim, taba, tabp)


_GELU_C = math.sqrt(2.0 / math.pi)


def _gelu(y):
    return 0.5 * y * (1.0 + jnp.tanh(_GELU_C * (y + 0.044715 * y * y * y)))


def _gelu_grad(y):
    t = jnp.tanh(_GELU_C * (y + 0.044715 * y * y * y))
    return 0.5 * (1.0 + t) + 0.5 * y * (1.0 - t * t) * _GELU_C * (1.0 + 3 * 0.044715 * y * y)


def _glu_halves(w4_ref):
    return (jnp.concatenate([w4_ref[0], w4_ref[1]], axis=1), jnp.concatenate([w4_ref[2], w4_ref[3]], axis=1))


def _s5_glu_fwd(y2, h, dsk, w4, bv, bg):
    tm = 512

    def body(y2_ref, u_ref, d_ref, w4_ref, bv_ref, bg_ref, ya_ref):
        wv, wg = _glu_halves(w4_ref)
        z = _gelu(y2_ref[0] + y2_ref[1] + d_ref[...] * u_ref[...])
        val = _mm(z, wv) + bv_ref[...]
        gate = _mm(z, wg) + bg_ref[...]
        ya_ref[...] = (val * jax.nn.sigmoid(gate)).astype(MX)

    full = lambda r, c: pl.BlockSpec((r, c), lambda i: (0, 0))
    return pl.pallas_call(
        body, grid=(N // tm,),
        in_specs=[pl.BlockSpec((2, tm, 256), lambda i: (0, i, 0)), pl.BlockSpec((tm, 256), lambda i: (i, 0)),
                  full(1, 256), pl.BlockSpec((NSHARD, 256, 128), lambda i: (0, 0, 0)), full(1, 256), full(1, 256)],
        out_specs=pl.BlockSpec((tm, 256), lambda i: (i, 0)),
        out_shape=_sds((N, 256), MX), name="s5_glu_fwd", compiler_params=_cp(("parallel",)))(y2, h, dsk, w4, bv, bg)


def _s5_glu_bwd(y2, h, dsk, w4, bv, bg, dya):
    tm = 512
    nt = N // tm

    def body(y2_ref, u_ref, d_ref, w4_ref, bv_ref, bg_ref, dya_ref,
             dyp_ref, dud_ref, dd_ref, dw4_ref, dbv_ref, dbg_ref, accv, accg):
        i = pl.program_id(0)

        @pl.when(i == 0)
        def _():
            for r in (dd_ref, accv, accg, dbv_ref, dbg_ref):
                r[...] = jnp.zeros_like(r)

        wv, wg = _glu_halves(w4_ref)
        u = u_ref[...]
        y = y2_ref[0] + y2_ref[1] + d_ref[...] * u
        z = _gelu(y)
        val = _mm(z, wv) + bv_ref[...]
        sig = jax.nn.sigmoid(_mm(z, wg) + bg_ref[...])
        dya = dya_ref[...]
        dval = dya * sig
        dgate = dya * val * sig * (1.0 - sig)
        dz = _mm_nt(dval, wv) + _mm_nt(dgate, wg)
        dy = dz * _gelu_grad(y)
        dyp_ref[...] = dy
        dud_ref[...] = (dy * d_ref[...]).astype(MX)
        dd_ref[...] += jnp.sum(dy * u, axis=0, keepdims=True)
        accv[...] += _mm_tn(z, dval)
        accg[...] += _mm_tn(z, dgate)
        dbv_ref[...] += jnp.sum(dval, axis=0, keepdims=True)
        dbg_ref[...] += jnp.sum(dgate, axis=0, keepdims=True)

        @pl.when(i == nt - 1)
        def _():
            dw4_ref[0] = accv[:, 0:128].astype(MX)
            dw4_ref[1] = accv[:, 128:256].astype(MX)
            dw4_ref[2] = accg[:, 0:128].astype(MX)
            dw4_ref[3] = accg[:, 128:256].astype(MX)

    full = lambda r, c: pl.BlockSpec((r, c), lambda i: (0, 0))
    row = pl.BlockSpec((tm, 256), lambda i: (i, 0))
    wspec = pl.BlockSpec((NSHARD, 256, 128), lambda i: (0, 0, 0))
    return pl.pallas_call(
        body, grid=(nt,),
        in_specs=[pl.BlockSpec((2, tm, 256), lambda i: (0, i, 0)), row, full(1, 256), wspec, full(1, 256), full(1, 256),
                  row],
        out_specs=[row, row, full(1, 256), wspec, full(1, 256), full(1, 256)],
        out_shape=[_sds((N, 256)), _sds((N, 256), MX), _sds((1, 256)), _sds((NSHARD, 256, 128), MX), _sds((1, 256)),
                   _sds((1, 256))],
        scratch_shapes=[pltpu.VMEM((256, 256), F32), pltpu.VMEM((256, 256), F32)],
        name="s5_glu_bwd", compiler_params=_cp(("arbitrary",)))(y2, h, dsk, w4, bv, bg, dya)


def _logsig(x):
    return jnp.minimum(x, 0.0) - jnp.log(1.0 + jnp.exp(-jnp.abs(x)))


def _gla_gate_bwd(h, wa, ba, dla_f, dla_b):
    tm = 512

    def body(hl_ref, wa_ref, ba_ref, df_ref, db_ref, dhl_ref, dwa_ref, dba_ref):
        i = pl.program_id(0)

        @pl.when(i == 0)
        def _():
            dwa_ref[...] = jnp.zeros_like(dwa_ref)
            dba_ref[...] = jnp.zeros_like(dba_ref)

        hl = hl_ref[...]
        pre = _mm(hl, wa_ref[...]) + ba_ref[...]
        dpre = jnp.concatenate([df_ref[...], db_ref[...]], axis=1) * (1.0 / 16.0) * jax.nn.sigmoid(-pre)
        dhl_ref[...] = _mm_nt(dpre, wa_ref[...]).astype(MX)
        dwa_ref[...] += _mm_tn(hl, dpre)[0:32]
        dba_ref[...] += jnp.sum(dpre, axis=0, keepdims=True)

    row = pl.BlockSpec((tm, 128), lambda i: (i, 0))
    return pl.pallas_call(
        body, grid=(N // tm,),
        in_specs=[pl.BlockSpec((tm, 128), lambda i: (i, 14)), pl.BlockSpec((128, 256), lambda i: (0, 0)),
                  pl.BlockSpec((1, 256), lambda i: (0, 0)), row, row],
        out_specs=[row, pl.BlockSpec((32, 256), lambda i: (0, 0)), pl.BlockSpec((1, 256), lambda i: (0, 0))],
        out_shape=[_sds((N, 128), MX), _sds((32, 256)), _sds((1, 256))],
        name="gla_gate_bwd", compiler_params=_cp(("arbitrary",)))(h, wa, ba, dla_f, dla_b)


def _gla_chunk(q, k, v, la, st, rev):
    c = GLA_CHUNK
    rows = q.shape[0]
    nch = rows // c
    b = _cums(la, rev)
    blc = [jnp.sum(la[i * c:(i + 1) * c], axis=0, keepdims=True) for i in range(nch)]
    bl = jnp.concatenate([jnp.broadcast_to(t, (c, 128)) for t in blc], axis=0)
    q_in = q * (32.0 ** -0.5) * jnp.exp(b)
    k_in = k * jnp.exp(-b)
    k_st = k * jnp.exp(bl - b)
    lane_k = lax.broadcasted_iota(jnp.int32, (1, 128), 1) // 32
    lane_v = lax.broadcasted_iota(jnp.int32, (1, 256), 1) // 64
    qs = jnp.concatenate([jnp.where(lane_k == hd, q_in, 0.0) for hd in range(4)], axis=0)
    a = _dmm_nt(qs, k_in)
    a = jnp.where(jnp.concatenate([_chunk_pairs(rows, rev, rev)] * 4, axis=0), a, 0.0)
    o4 = _dmm(a, v)
    o = jnp.zeros((rows, 256), F32)
    for hd in range(4):
        o = o + jnp.where(lane_v == hd, o4[hd * rows:(hd + 1) * rows], 0.0)
    bd = (lax.broadcasted_iota(jnp.int32, (256, 128), 0) // 64) == (lax.broadcasted_iota(jnp.int32, (256, 128), 1) // 32)
    inter = [None] * nch
    for i in (reversed(range(nch)) if rev else range(nch)):
        sl = slice(i * c, (i + 1) * c)
        inter[i] = _dmm_nt(q_in[sl], st)
        st = jnp.exp(blc[i]) * st + jnp.where(bd, _dmm_tn(v[sl], k_st[sl]), 0.0)
    return o + jnp.concatenate(inter, axis=0), st


def _gla_chunk_of(c, rev):
    return NGROUP - 1 - c if rev else c


def _gla_fwd(h, la2):
    c = GLA_GROUP * GLA_CHUNK

    def body(qf, kf, vf, laf, qb, kb, vb, lab, of_ref, ob_ref, sf_ref, sb_ref, stf, stb):
        @pl.when(pl.program_id(0) == 0)
        def _():
            stf[...] = jnp.zeros_like(stf)
            stb[...] = jnp.zeros_like(stb)

        ins = [(qf[s], kf[s], vf[s], laf[s], stf[s], qb[s], kb[s], vb[s], lab[s], stb[s]) for s in range(NSEQ)]
        outs = [(_gla_chunk(*t[:5], False), _gla_chunk(*t[5:], True)) for t in ins]
        for s in range(NSEQ):
            sf_ref[s, 0] = ins[s][4]
            sb_ref[s, 0] = ins[s][9]
            (of_ref[s], stf[s]), (ob_ref[s], stb[s]) = outs[s]

    def specs(rev):
        ch = lambda i: _gla_chunk_of(i, rev)
        return [pl.BlockSpec((NSEQ, c, 128), lambda i: (0, ch(i), 2)), pl.BlockSpec((NSEQ, c, 128), lambda i: (0, ch(i), 3)),
                pl.BlockSpec((NSEQ, c, 256), lambda i: (0, ch(i), 2)),
                pl.BlockSpec((NSEQ, c, 128), lambda i: (0, ch(i), 1 if rev else 0))]

    orow = lambda rev: pl.BlockSpec((NSEQ, c, 256), lambda i: (0, _gla_chunk_of(i, rev), 0))
    srow = lambda rev: pl.BlockSpec((NSEQ, 1, 256, 128), lambda i: (0, _gla_chunk_of(i, rev), 0, 0))
    h3, la3 = h.reshape(NSEQ, L, DINP), la2.reshape(NSEQ, L, 256)
    of, ob, sf, sb = pl.pallas_call(
        body, grid=(NGROUP,),
        in_specs=specs(False) + specs(True),
        out_specs=[orow(False), orow(True), srow(False), srow(True)],
        out_shape=[_sds((NSEQ, L, 256)), _sds((NSEQ, L, 256)), _sds((NSEQ, NGROUP, 256, 128)),
                   _sds((NSEQ, NGROUP, 256, 128))],
        scratch_shapes=[pltpu.VMEM((NSEQ, 256, 128), F32), pltpu.VMEM((NSEQ, 256, 128), F32)],
        name="gla_fwd", compiler_params=_cp(("arbitrary",)))(h3, h3, h3, la3, h3, h3, h3, la3)
    return of.reshape(N, 256), ob.reshape(N, 256), sf, sb


def _gla_bwd(h, la2, do, sf, sb):
    c = GLA_GROUP * GLA_CHUNK

    def body(qf, kf, vf, laf, dof, sfr, qb, kb, vb, lab, dob, sbr,
             dqf, dkf, dvf, dlf, dqb, dkb, dvb, dlb, dstf, dstb):
        @pl.when(pl.program_id(0) == 0)
        def _():
            dstf[...] = jnp.zeros_like(dstf)
            dstb[...] = jnp.zeros_like(dstb)

        def one(s, q, k, v, la, do_, st, dst, rev):
            _, vjp = jax.vjp(functools.partial(_gla_chunk, rev=rev), q[s], k[s], v[s], la[s], st[s, 0])
            return vjp((do_[s], dst[s]))

        res = [(one(s, qf, kf, vf, laf, dof, sfr, dstf, False), one(s, qb, kb, vb, lab, dob, sbr, dstb, True))
               for s in range(NSEQ)]
        for s in range(NSEQ):
            for (gq, gk, gv, gl, gs), (dq, dk, dv, dl, dst) in ((res[s][0], (dqf, dkf, dvf, dlf, dstf)),
                                                                  (res[s][1], (dqb, dkb, dvb, dlb, dstb))):
                dq[s], dk[s], dv[s] = gq.astype(MX), gk.astype(MX), gv.astype(MX)
                dl[s], dst[s] = gl, gs

    def specs(rev):
        ch = lambda i: _gla_chunk_of(i, not rev)
        return [pl.BlockSpec((NSEQ, c, 128), lambda i: (0, ch(i), 2)), pl.BlockSpec((NSEQ, c, 128), lambda i: (0, ch(i), 3)),
                pl.BlockSpec((NSEQ, c, 256), lambda i: (0, ch(i), 2)),
                pl.BlockSpec((NSEQ, c, 128), lambda i: (0, ch(i), 1 if rev else 0)),
                pl.BlockSpec((NSEQ, c, 256), lambda i: (0, ch(i), 0)),
                pl.BlockSpec((NSEQ, 1, 256, 128), lambda i: (0, ch(i), 0, 0))]

    def ospecs(rev):
        ch = lambda i: _gla_chunk_of(i, not rev)
        n = pl.BlockSpec((NSEQ, c, 128), lambda i: (0, ch(i), 0))
        return [n, n, pl.BlockSpec((NSEQ, c, 256), lambda i: (0, ch(i), 0)), n]

    oshape = [_sds((NSEQ, L, 128), MX), _sds((NSEQ, L, 128), MX), _sds((NSEQ, L, 256), MX), _sds((NSEQ, L, 128))]
    h3, la3, do3 = h.reshape(NSEQ, L, DINP), la2.reshape(NSEQ, L, 256), do.reshape(NSEQ, L, 256)
    res = pl.pallas_call(
        body, grid=(NGROUP,),
        in_specs=specs(False) + specs(True),
        out_specs=ospecs(False) + ospecs(True),
        out_shape=oshape + oshape,
        scratch_shapes=[pltpu.VMEM((NSEQ, 256, 128), F32), pltpu.VMEM((NSEQ, 256, 128), F32)],
        name="gla_bwd", compiler_params=_cp(("arbitrary",)))(h3, h3, h3, la3, do3, sf, h3, h3, h3, la3, do3, sb)
    return [r.reshape(N, r.shape[-1]) for r in res]


def _gla_post(of, ob, r, g):
    o = of + ob
    head = lax.broadcasted_iota(jnp.int32, (1, 256), 1) // 64
    mu = jnp.zeros_like(o)
    for hd in range(4):
        mu = mu + jnp.where(head == hd, jnp.sum(jnp.where(head == hd, o, 0.0), axis=-1, keepdims=True) * (1.0 / 64.0), 0.0)
    xc = o - mu
    var = jnp.zeros_like(o)
    for hd in range(4):
        var = var + jnp.where(head == hd, jnp.sum(jnp.where(head == hd, xc * xc, 0.0), axis=-1, keepdims=True) * (1.0 / 64.0), 0.0)
    return xc * lax.rsqrt(var + LN_EPS) * g * (r * jax.nn.sigmoid(r))


def _gla_post_fwd(of, ob, h, g):
    tm = 512

    def body(of_ref, ob_ref, r_ref, g_ref, y_ref):
        y_ref[...] = _gla_post(of_ref[...], ob_ref[...], r_ref[...], g_ref[...]).astype(MX)

    row = pl.BlockSpec((tm, 256), lambda i: (i, 0))
    return pl.pallas_call(
        body, grid=(N // tm,),
        in_specs=[row, row, pl.BlockSpec((tm, 256), lambda i: (i, 3)), pl.BlockSpec((1, 256), lambda i: (0, 0))],
        out_specs=row, out_shape=_sds((N, 256), MX), name="gla_post_fwd", compiler_params=_cp(("parallel",)))(of, ob, h, g)


def _gla_post_bwd(of, ob, h, g, dyb):
    tm = 512

    def body(of_ref, ob_ref, r_ref, g_ref, dy_ref, do_ref, dr_ref, dg_ref):
        @pl.when(pl.program_id(0) == 0)
        def _():
            dg_ref[...] = jnp.zeros_like(dg_ref)

        _, vjp = jax.vjp(_gla_post, of_ref[...], ob_ref[...], r_ref[...], g_ref[...])
        go, _, gr, gg = vjp(dy_ref[...])
        do_ref[...] = go
        dr_ref[...] = gr.astype(MX)
        dg_ref[...] += gg

    row = pl.BlockSpec((tm, 256), lambda i: (i, 0))
    one = pl.BlockSpec((1, 256), lambda i: (0, 0))
    return pl.pallas_call(
        body, grid=(N // tm,),
        in_specs=[row, row, pl.BlockSpec((tm, 256), lambda i: (i, 3)), one, row],
        out_specs=[row, row, one], out_shape=[_sds((N, 256)), _sds((N, 256), MX), _sds((1, 256))],
        name="gla_post_bwd", compiler_params=_cp(("arbitrary",)))(of, ob, h, g, dyb)


def _rope_tables(width):
    pos = jnp.arange(L, dtype=F32)
    inv_freq = ROPE_THETA ** (-jnp.arange(0, ROT, 2, dtype=F32) / ROT)
    ang = pos[:, None] * inv_freq[None, :]
    cos, sin = jnp.cos(ang), jnp.sin(ang)
    one = jnp.ones((L, 64 - ROT), F32)
    zero = jnp.zeros((L, 64 - ROT), F32)
    z8 = jnp.zeros((L, ROT // 2), F32)
    c = jnp.concatenate([cos, cos, one], axis=1)
    sa = jnp.concatenate([z8, sin, zero], axis=1)
    sb = jnp.concatenate([-sin, z8, zero], axis=1)
    rep = width // 64
    return jnp.stack([jnp.tile(c, (1, rep)), jnp.tile(sa, (1, rep)), jnp.tile(sb, (1, rep))])


def _pieces(t, f):
    out = [f(t[:, c * 128:(c + 1) * 128]) for c in range(t.shape[-1] // 128)]
    return out[0] if len(out) == 1 else jnp.concatenate(out, axis=1)


def _rope(t, tab):
    return _pieces(t, lambda x: x * tab[0] + pltpu.roll(x, ROT // 2, 1) * tab[1] + pltpu.roll(x, 128 - ROT // 2, 1) * tab[2])


def _rope_t(g, tab):
    return _pieces(g, lambda x: x * tab[0] + pltpu.roll(x * tab[1], 128 - ROT // 2, 1) + pltpu.roll(x * tab[2], ROT // 2, 1))


def _swa_pad_kv(kv_ref, tk_ref, kexp, vexp):
    z = jnp.zeros((SWA_BLK, 256), F32)
    kr = _rope(kv_ref[:, 0:128], tk_ref[...])
    for hk in range(2):
        for pad in (kexp, vexp):
            pad[hk, 0:SWA_BLK] = z
            pad[hk, SWA_BLK + L:] = z
        kexp[hk, SWA_BLK:SWA_BLK + L] = _swa_expand(kr, hk)
        vexp[hk, SWA_BLK:SWA_BLK + L] = _swa_expand(kv_ref[:, 128:256], hk)


def _swa_expand(x, hk):
    lane = lax.broadcasted_iota(jnp.int32, x.shape, 1)
    sw = pltpu.roll(x, 64, 1)
    pair = jnp.where(lane < 64, x, sw) if hk == 0 else jnp.where(lane < 64, sw, x)
    return jnp.concatenate([pair, pair], axis=1)


def _swa_fold(x, hk):
    a = x[:, 0:128] + x[:, 128:256]
    t = a + pltpu.roll(a, 64, 1)
    lane = lax.broadcasted_iota(jnp.int32, a.shape, 1)
    return jnp.where((lane < 64) if hk == 0 else (lane >= 64), t, 0.0)


def _swa_probs(q2, kexp, n, sink_ref, hk):
    slot = lax.broadcasted_iota(jnp.int32, (1, 256), 1) // 64
    qs = jnp.concatenate([jnp.where(slot == g, q2, 0.0) for g in range(4)], axis=0)
    s = _mm_nt(qs, kexp) * 0.125
    i = lax.broadcasted_iota(jnp.int32, (SWA_BLK, 3 * SWA_BLK), 0)
    j = lax.broadcasted_iota(jnp.int32, (SWA_BLK, 3 * SWA_BLK), 1)
    kpos = n * SWA_BLK - SWA_BLK + j
    ok = (j - i >= 0) & (j - i <= 2 * SWA_BLK) & (kpos >= 0) & (kpos < L)
    s = jnp.where(jnp.concatenate([ok] * 4, axis=0), s, NEG_BIG)
    rowg = lax.broadcasted_iota(jnp.int32, (4 * SWA_BLK, 1), 0) // SWA_BLK
    sink = jnp.zeros((4 * SWA_BLK, 1), F32)
    for g in range(4):
        sink = jnp.where(rowg == g, sink_ref[hk * 4 + g], sink)
    m = jnp.maximum(jnp.max(s, axis=-1, keepdims=True), sink)
    p = jnp.exp(s - m)
    ps = jnp.exp(sink - m)
    inv = 1.0 / (jnp.sum(p, axis=-1, keepdims=True) + ps)
    return qs, p * inv, ps * inv, slot, rowg


def _swa_qtab(tk_ref, r0):
    return [tk_ref[i, pl.ds(r0, SWA_BLK), :] for i in range(3)]


def _swa_fwd(h, tk, sink):
    def body(sink_ref, q_ref, kv_ref, tk_ref, y_ref, kexp, vexp):
        n = pl.program_id(1)

        @pl.when(n == 0)
        def _():
            _swa_pad_kv(kv_ref, tk_ref, kexp, vexp)

        r0 = pl.multiple_of(n * SWA_BLK, SWA_BLK)
        q = _rope(q_ref[...], _swa_qtab(tk_ref, r0))
        for hk in range(2):
            _, p, _, slot, _ = _swa_probs(q[:, hk * 256:(hk + 1) * 256], kexp[hk, pl.ds(r0, 3 * SWA_BLK), :], n,
                                          sink_ref, hk)
            o4 = _mm(p, vexp[hk, pl.ds(r0, 3 * SWA_BLK), :])
            o = jnp.zeros((SWA_BLK, 256), F32)
            for g in range(4):
                o = o + jnp.where(slot == g, o4[g * SWA_BLK:(g + 1) * SWA_BLK], 0.0)
            y_ref[:, hk * 256:(hk + 1) * 256] = o.astype(MX)

    return pl.pallas_call(
        body,
        grid_spec=pltpu.PrefetchScalarGridSpec(
            num_scalar_prefetch=1, grid=(NSEQ, NBLK),
            in_specs=[pl.BlockSpec((SWA_BLK, 512), lambda s, n, sk: (s * NBLK + n, 2)),
                      pl.BlockSpec((L, 256), lambda s, n, sk: (s, 6)),
                      pl.BlockSpec((3, L, 128), lambda s, n, sk: (0, 0, 0))],
            out_specs=pl.BlockSpec((SWA_BLK, 512), lambda s, n, sk: (s * NBLK + n, 0)),
            scratch_shapes=[pltpu.VMEM((2, L + 2 * SWA_BLK, 256), F32), pltpu.VMEM((2, L + 2 * SWA_BLK, 256), F32)]),
        out_shape=_sds((N, 512), MX), name="swa_fwd", compiler_params=_cp(("arbitrary", "arbitrary")))(sink, h, h, tk)


def _swa_bwd(h, tk, sink, dyc):
    def body(sink_ref, q_ref, kv_ref, tk_ref, dy_ref, dq_ref, dkv_ref, dsink_ref, kexp_all, vexp_all, dkacc, dvacc):
        sq = pl.program_id(0)
        n = pl.program_id(1)

        @pl.when(n == 0)
        def _():
            _swa_pad_kv(kv_ref, tk_ref, kexp_all, vexp_all)
            dkacc[...] = jnp.zeros_like(dkacc)
            dvacc[...] = jnp.zeros_like(dvacc)

        @pl.when((n == 0) & (sq == 0))
        def _():
            dsink_ref[...] = jnp.zeros_like(dsink_ref)

        r0 = pl.multiple_of(n * SWA_BLK, SWA_BLK)
        tq = _swa_qtab(tk_ref, r0)
        q = _rope(q_ref[...], tq)
        hrow = lax.broadcasted_iota(jnp.int32, (8, 128), 0)
        dsk = jnp.zeros((8, 128), F32)
        for hk in range(2):
            kexp = kexp_all[hk, pl.ds(r0, 3 * SWA_BLK), :]
            vexp = vexp_all[hk, pl.ds(r0, 3 * SWA_BLK), :]
            qs, p, ps, slot, rowg = _swa_probs(q[:, hk * 256:(hk + 1) * 256], kexp, n, sink_ref, hk)
            dy2 = dy_ref[:, hk * 256:(hk + 1) * 256]
            dos = jnp.concatenate([jnp.where(slot == g, dy2, 0.0) for g in range(4)], axis=0)
            dp = _mm_nt(dos, vexp)
            delta = jnp.sum(p * dp, axis=-1, keepdims=True)
            ds = p * (dp - delta) * 0.125
            dsr = -ps * delta
            for g in range(4):
                dsk = dsk + jnp.where(hrow == hk * 4 + g, jnp.sum(jnp.where(rowg == g, dsr, 0.0), axis=0, keepdims=True), 0.0)
            dq4 = _mm(ds, kexp)
            dq2 = jnp.zeros((SWA_BLK, 256), F32)
            for g in range(4):
                dq2 = dq2 + jnp.where(slot == g, dq4[g * SWA_BLK:(g + 1) * SWA_BLK], 0.0)
            dq_ref[:, hk * 256:(hk + 1) * 256] = _rope_t(dq2, tq).astype(MX)
            dkacc[hk, pl.ds(r0, 3 * SWA_BLK), :] += _mm_tn(ds, qs)
            dvacc[hk, pl.ds(r0, 3 * SWA_BLK), :] += _mm_tn(p, dos)
        dsink_ref[...] += dsk

        @pl.when(n == NBLK - 1)
        def _():
            seq = slice(SWA_BLK, SWA_BLK + L)
            dk = _rope_t(_swa_fold(dkacc[0, seq], 0) + _swa_fold(dkacc[1, seq], 1), tk_ref[...])
            dkv_ref[:, 0:128] = dk.astype(MX)
            dkv_ref[:, 128:256] = (_swa_fold(dvacc[0, seq], 0) + _swa_fold(dvacc[1, seq], 1)).astype(MX)

    blk = lambda col: pl.BlockSpec((SWA_BLK, 512), lambda s, n, sk: (s * NBLK + n, col))
    pad = pltpu.VMEM((2, L + 2 * SWA_BLK, 256), F32)
    return pl.pallas_call(
        body,
        grid_spec=pltpu.PrefetchScalarGridSpec(
            num_scalar_prefetch=1, grid=(NSEQ, NBLK),
            in_specs=[blk(2), pl.BlockSpec((L, 256), lambda s, n, sk: (s, 6)),
                      pl.BlockSpec((3, L, 128), lambda s, n, sk: (0, 0, 0)), blk(0)],
            out_specs=[blk(0), pl.BlockSpec((L, 256), lambda s, n, sk: (s, 0)),
                       pl.BlockSpec((8, 128), lambda s, n, sk: (0, 0))],
            scratch_shapes=[pad, pad, pad, pad]),
        out_shape=[_sds((N, 512), MX), _sds((N, 256), MX), _sds((8, 128))],
        name="swa_bwd", compiler_params=_cp(("arbitrary", "arbitrary")))(sink, h, h, tk, dyc)


def _outproj_bwd(dx1, s1, ya, yb, yc, wo, g):
    tm = 512
    nt = N // tm

    def body(dx1_ref, s_ref, ya_ref, yb_ref, yc_ref, wo_ref, g_ref,
             dya_ref, dyb_ref, dyc_ref, dxp_ref, dwo_ref, dg_ref, db_ref, acc):
        i = pl.program_id(0)

        @pl.when(i == 0)
        def _():
            acc[...] = jnp.zeros_like(acc)
            dg_ref[...] = jnp.zeros_like(dg_ref)
            db_ref[...] = jnp.zeros_like(db_ref)

        ds, dg, db = _ln_bwd(dx1_ref[...], s_ref[...], g_ref[...])
        dg_ref[...] += dg
        db_ref[...] += db
        dxp_ref[...] = ALPHA * ds
        dy = _mm_nt(ds, wo_ref[...])
        dya_ref[...] = dy[:, 0:256]
        dyb_ref[...] = dy[:, 256:512]
        dyc_ref[...] = dy[:, 512:1024]
        acc[0:256] += _mm_tn(ya_ref[...], ds)
        acc[256:512] += _mm_tn(yb_ref[...], ds)
        acc[512:1024] += _mm_tn(yc_ref[...], ds)

        @pl.when(i == nt - 1)
        def _():
            dwo_ref[...] = acc[...].astype(MX)

    row = lambda w_: pl.BlockSpec((tm, w_), lambda i: (i, 0))
    one = pl.BlockSpec((1, D), lambda i: (0, 0))
    full = pl.BlockSpec((D, D), lambda i: (0, 0))
    return pl.pallas_call(
        body, grid=(nt,),
        in_specs=[row(D), row(D), row(256), row(256), row(512), full, one],
        out_specs=[row(256), row(256), row(512), row(D), full, one, one],
        out_shape=[_sds((N, 256)), _sds((N, 256)), _sds((N, 512)), _sds((N, D)), _sds((D, D), MX), _sds((1, D)), _sds((1, D))],
        scratch_shapes=[pltpu.VMEM((D, D), F32)],
        name="outproj_bwd", compiler_params=_cp(("arbitrary",)))(dx1, s1, ya, yb, yc, wo, g)


def _mix_ffn_fwd(ya, yb, yc, x, wo, g1, b1, w1, w2, g, b, target=None):
    tm = FFN_TM
    head = target is not None

    def body(*refs):
        ya_ref, yb_ref, yc_ref, xin_ref, wo_ref, g1_ref, b1_ref, w1_ref, w2_ref, g_ref, b_ref = refs[:11]
        s1_ref, x1_ref, a_ref, s_ref, y_ref = refs[11 + head:16 + head]
        mix = _mm(ya_ref[...], wo_ref[0:256]) + _mm(yb_ref[...], wo_ref[256:512]) + _mm(yc_ref[...], wo_ref[512:1024])
        s1 = ALPHA * xin_ref[...] + mix
        s1_ref[...] = s1
        x = _ln_fwd(s1, g1_ref[...], b1_ref[...])
        x1_ref[...] = x
        xb = x.astype(MX)
        s = ALPHA * x
        for j in range(NSHARD):
            a = _mm(xb, w1_ref[j])
            a_ref[:, j * D:(j + 1) * D] = a.astype(MX)
            s = s + _mm(jnp.square(jnp.maximum(a, 0.0)), w2_ref[j])
        s_ref[...] = s
        x2 = _ln_fwd(s, g_ref[...], b_ref[...])
        if not head:
            y_ref[...] = x2
            return
        l_ref = refs[16 + head]

        @pl.when(pl.program_id(0) == 0)
        def _():
            l_ref[...] = jnp.zeros_like(l_ref)

        e = x2 - refs[11][...]
        y_ref[...] = e * (1.0 / D)
        l_ref[...] += jnp.sum(jnp.sum(e * e, axis=1, keepdims=True), axis=0, keepdims=True) * (0.5 / D)

    rw = lambda w_: pl.BlockSpec((tm, w_), lambda i: (i, 0))
    row = rw(D)
    once = dict(pipeline_mode=pl.Buffered(1))
    wall = pl.BlockSpec((NSHARD, D, D), lambda i: (0, 0, 0), **once)
    one = pl.BlockSpec((1, D), lambda i: (0, 0))
    acc = pl.BlockSpec((8, 128), lambda i: (0, 0))
    return pl.pallas_call(
        body, grid=(N // tm,),
        in_specs=[rw(256), rw(256), rw(512), row, pl.BlockSpec((D, D), lambda i: (0, 0), **once), one, one,
                  wall, wall, one, one] + [row] * head,
        out_specs=[row, row, pl.BlockSpec((tm, DFF), lambda i: (i, 0)), row, row] + [acc] * head,
        out_shape=[_sds((N, D)), _sds((N, D)), _sds((N, DFF), MX), _sds((N, D)), _sds((N, D))] + [_sds((8, 128))] * head,
        name="mix_ffn_fwd", compiler_params=_cp(("arbitrary",), FFN_VMEM))(
            ya, yb, yc, x, wo, g1, b1, w1, w2, g, b, *([target] * head))


def _ffn_bwd_act(dy, s2, a, w1, w2, g):
    tm = FFN_TM

    def body(dy_ref, s_ref, a_ref, w1_ref, w2_ref, g_ref, da_ref, ds_ref, dx1_ref, dg_ref, db_ref):
        @pl.when(pl.program_id(0) == 0)
        def _():
            dg_ref[...] = jnp.zeros_like(dg_ref)
            db_ref[...] = jnp.zeros_like(db_ref)

        ds, dg, db = _ln_bwd(dy_ref[...], s_ref[...], g_ref[...])
        dsb = ds.astype(MX)
        ds_ref[...] = dsb
        dg_ref[...] += dg
        db_ref[...] += db
        dx1 = ALPHA * ds
        for j in range(NSHARD):
            da = (_mm_nt(dsb, w2_ref[j]) * 2.0 * jnp.maximum(a_ref[:, j * D:(j + 1) * D].astype(F32), 0.0)).astype(MX)
            da_ref[:, j * D:(j + 1) * D] = da
            dx1 = dx1 + _mm_nt(da, w1_ref[j])
        dx1_ref[...] = dx1

    row = pl.BlockSpec((tm, D), lambda i: (i, 0))
    wide = pl.BlockSpec((tm, DFF), lambda i: (i, 0))
    wall = pl.BlockSpec((NSHARD, D, D), lambda i: (0, 0, 0))
    one = pl.BlockSpec((1, D), lambda i: (0, 0))
    return pl.pallas_call(
        body, grid=(N // tm,),
        in_specs=[row, row, wide, wall, wall, one],
        out_specs=[wide, row, row, one, one],
        out_shape=[_sds((N, DFF), MX), _sds((N, D), MX), _sds((N, D)), _sds((1, D)), _sds((1, D))],
        name="ffn_bwd_act", compiler_params=_cp(("arbitrary",), FFN_VMEM))(dy, s2, a, w1, w2, g)


def _ffn_bwd_w(x1, da, a, ds):
    tm, nb = FFN_TM_W, FFN_WB
    nt = N // tm

    def body(x_ref, da_ref, a_ref, ds_ref, dw1_ref, dw2_ref, acc1, acc2):
        i = pl.program_id(1)

        @pl.when(i == 0)
        def _():
            acc1[...] = jnp.zeros_like(acc1)
            acc2[...] = jnp.zeros_like(acc2)

        x, ds_ = x_ref[...], ds_ref[...]
        for k in range(nb):
            cols = slice(k * D, (k + 1) * D)
            acc1[k] += _mm_tn(x, da_ref[:, cols])
            acc2[k] += _mm_tn(jnp.square(jnp.maximum(a_ref[:, cols].astype(F32), 0.0)), ds_)

        @pl.when(i == nt - 1)
        def _():
            dw1_ref[...] = acc1[...].astype(MX)
            dw2_ref[...] = acc2[...].astype(MX)

    row = pl.BlockSpec((tm, D), lambda j, i: (i, 0))
    col = pl.BlockSpec((tm, nb * D), lambda j, i: (i, j))
    wj = pl.BlockSpec((nb, D, D), lambda j, i: (j, 0, 0))
    return pl.pallas_call(
        body, grid=(NSHARD // nb, nt),
        in_specs=[row, col, col, row], out_specs=[wj, wj],
        out_shape=[_sds((NSHARD, D, D), MX), _sds((NSHARD, D, D), MX)],
        scratch_shapes=[pltpu.VMEM((nb, D, D), F32), pltpu.VMEM((nb, D, D), F32)],
        name="ffn_bwd_w", compiler_params=_cp(("parallel", "arbitrary"), FFN_VMEM))(x1, da, a, ds)


def _s5_discretize(a_re, a_im, log_step, b_re, b_im):
    lam = lax.complex(a_re, a_im)
    lam_bar = jnp.exp(lam * jnp.exp(log_step))
    b_bar = ((lam_bar - 1.0) / lam)[..., None] * lax.complex(b_re, b_im)
    return jnp.real(lam_bar), jnp.imag(lam_bar), jnp.real(b_bar), jnp.imag(b_bar)


def _s5_in_blocks(b):
    e = jnp.eye(8, dtype=F32)
    return jnp.einsum('ij,zbjph->zbihjp', e, b.reshape(2, 2, 8, S5_P, S5_H)).reshape(2, 2, 128, SW)


def _s5_out_blocks(c):
    e = jnp.eye(8, dtype=F32)
    return jnp.einsum('ij,zbjhp->zbjpih', e, c.reshape(2, 2, 8, S5_H, S5_P)).reshape(2, 2, SW, 128)


def _gate_weight(w_a):
    z = jnp.zeros((16, 128), F32)
    top = jnp.concatenate([w_a[0], z], axis=1)
    bot = jnp.concatenate([z, w_a[1]], axis=1)
    return jnp.concatenate([top, bot, jnp.zeros((96, 256), F32)], axis=0)


def _layer_prep(p):
    lr, li, br, bi = _s5_discretize(p["s5_a_re"], p["s5_a_im"], p["s5_log_step"], p["s5_b_re"], p["s5_b_im"])
    q = dict(p)
    q["bre"] = _s5_in_blocks(br).astype(MX)
    q["bim"] = _s5_in_blocks(bi).astype(MX)
    q["cre"] = _s5_out_blocks(p["s5_c_re"]).astype(MX)
    q["cim"] = _s5_out_blocks(p["s5_c_im"]).astype(MX)
    mr, mi = lr.reshape(2, 1024), li.reshape(2, 1024)
    q["tab"], q["tabc"] = _lockstep_tables(mr, mi)
    q["dsk"] = p["s5_d"].reshape(1, 256)
    q["wa"] = _gate_weight(p["gla_w_a"]).astype(MX)
    q["ba"] = p["gla_b_a"].reshape(1, 256)
    q["lng"] = p["gla_ln_g"].reshape(1, 256)
    q["bv"] = p["s5_b_glu"][:256].reshape(1, 256)
    q["bg"] = p["s5_b_glu"][256:].reshape(1, 256)
    for k in ("ln1_g", "ln1_b", "ln2_g", "ln2_b"):
        q[k] = p[k].reshape(1, D)
    return q


def _layer_fwd(x, q, tk, fetch, target=None):
    q["w_in"] = fetch("w_in", x)
    h, la2 = _inproj_fwd(x, q["w_in"], q["wa"], q["ba"])
    hre, him, y2 = _s5_fwd(h, q["bre"], q["bim"], q["cre"], q["cim"], q["tab"])
    q["w4"] = fetch("s5_w_glu", y2)
    ya = _s5_glu_fwd(y2, h, q["dsk"], q["w4"], q["bv"], q["bg"])
    of, ob, sf, sb = _gla_fwd(h, la2)
    yb = _gla_post_fwd(of, ob, h, q["lng"])
    yc = _swa_fwd(h, tk, q["swa_sink"])
    mixed = ya[:8, :128] + yb[:8, :128] + yc[:8, :128]
    q["w_out"] = fetch("w_out", mixed)
    q["w_ff1"] = fetch("w_ff1", mixed)
    q["w_ff2"] = fetch("w_ff2", mixed)
    s1, x1, a, s2, *out = _mix_ffn_fwd(ya, yb, yc, x, q["w_out"], q["ln1_g"], q["ln1_b"], q["w_ff1"], q["w_ff2"],
                                       q["ln2_g"], q["ln2_b"], target)
    saved = dict(x=x, h=h, hre=hre, him=him, y2=y2, ya=ya, la2=la2, of=of, ob=ob, sf=sf, sb=sb, yb=yb, yc=yc,
                 s1=s1, x1=x1, a=a, s2=s2)
    return (out[0] if target is None else tuple(out)), saved


def _layer_bwd(dy, q, sv, tk, emit):
    g = {}
    da, ds2, dx1, g["dg2"], g["db2"] = _ffn_bwd_act(dy, sv["s2"], sv["a"], q["w_ff1"], q["w_ff2"], q["ln2_g"])
    dw1, dw2 = _ffn_bwd_w(sv["x1"], da, sv["a"], ds2)
    tie = emit(dict(w_ff1=dw1, w_ff2=dw2))
    dya, dyb, dyc, dxp, dwo, g["dg1"], g["db1"] = _outproj_bwd(dx1, sv["s1"], sv["ya"], sv["yb"], sv["yc"],
                                                               q["w_out"], q["ln1_g"] + tie)
    h = sv["h"]
    daq, dakv, g["dsink"] = _swa_bwd(h, tk, q["swa_sink"], dyc)
    do, gr, g["dlng"] = _gla_post_bwd(sv["of"], sv["ob"], h, q["lng"], dyb)
    gq_f, gk_f, gv_f, gl_f, gq_b, gk_b, gv_b, gl_b = _gla_bwd(h, sv["la2"], do, sv["sf"], sv["sb"])
    dhl, g["dwa"], g["dba"] = _gla_gate_bwd(h, q["wa"], q["ba"], gl_f, gl_b)
    dyp, dud, g["dd"], dw4, g["dbv"], g["dbg"] = _s5_glu_bwd(sv["y2"], h, q["dsk"], q["w4"], q["bv"], q["bg"], dya)
    tie = emit(dict(w_out=dwo.reshape(NSHARD, D // NSHARD, D), s5_w_glu=dw4))
    du2, g["dbre"], g["dbim"], g["dcre"], g["dcim"], g["dmu"] = _s5_bwd(
        h, dyp, sv["hre"], sv["him"], q["bre"], q["bim"], q["cre"], q["cim"], (q["tabc"][0], q["tabc"][1] + tie))
    dx, dwt = _inproj_bwd(sv["x"], q["w_in"], dxp, du2, dud, gq_f, gq_b, gk_f, gk_b, gv_f, gv_b, gr, daq, dakv, dhl)
    tie = emit(dict(w_in=dwt))
    return dx, g, tie


NATIVE = ("dmu", "dbre", "dbim", "dcre", "dcim", "dd", "dbv", "dbg", "dwa", "dba", "dlng", "dsink",
          "dg1", "db1", "dg2", "db2", "loss")
ICI_CORE = (0, 0, 0, 1, 1, 0, 0, 0, 1, 1, 1, 1, 0, 0, 1, 1, 0)


def _finish_small(n, w):
    g = {}
    dmu = n["dmu"]
    dlr = dmu[:, :, :, 0].reshape(DEPTH, 2, S5_G, S5_P)
    dli = dmu[:, :, :, 1].reshape(DEPTH, 2, S5_G, S5_P)

    def unblock(c, perm, shape):
        return c.reshape(DEPTH, 2, 2, S5_H, 8, S5_P).transpose(perm).reshape(shape)

    b_shape, c_shape = (DEPTH, 2, S5_G, S5_P, S5_H), (DEPTH, 2, S5_G, S5_H, S5_P)
    _, vjp = jax.vjp(_s5_discretize, w["s5_a_re"], w["s5_a_im"], w["s5_log_step"], w["s5_b_re"], w["s5_b_im"])
    (g["s5_a_re"], g["s5_a_im"], g["s5_log_step"], g["s5_b_re"], g["s5_b_im"]) = vjp(
        (dlr, dli, unblock(n["dbre"], (0, 1, 2, 4, 5, 3), b_shape), unblock(n["dbim"], (0, 1, 2, 4, 5, 3), b_shape)))
    g["s5_c_re"] = unblock(n["dcre"], (0, 1, 2, 4, 3, 5), c_shape)
    g["s5_c_im"] = unblock(n["dcim"], (0, 1, 2, 4, 3, 5), c_shape)
    g["s5_d"] = n["dd"].reshape(DEPTH, S5_G, S5_H)
    g["s5_b_glu"] = jnp.concatenate([n["dbv"], n["dbg"]], axis=2).reshape(DEPTH, 512)
    g["gla_w_a"] = jnp.stack([n["dwa"][:, 0:16, 0:128], n["dwa"][:, 16:32, 128:256]], axis=1)
    g["gla_b_a"] = n["dba"].reshape(DEPTH, 2, 128)
    g["gla_ln_g"] = n["dlng"].reshape(DEPTH, 256)
    g["swa_sink"] = n["dsink"][:, :, 0]
    for k, s in (("ln1_g", "dg1"), ("ln1_b", "db1"), ("ln2_g", "dg2"), ("ln2_b", "db2")):
        g[k] = n[s].reshape(DEPTH, D)
    return g


def _local_step(x, target, qs, tk, fetch, emit):
    saved = []
    for l, q in enumerate(qs):
        x, sv = _layer_fwd(x, q, tk, functools.partial(fetch, l), target if l == DEPTH - 1 else None)
        saved.append(sv)
    dy, lacc = x
    smalls = [None] * DEPTH
    tie = 0.0
    for l in reversed(range(DEPTH)):
        qs[l]["ln2_g"] = qs[l]["ln2_g"] + tie
        dy, smalls[l], tie = _layer_bwd(dy, qs[l], saved[l], tk, functools.partial(emit, l))
    smalls[0]["db2"] = smalls[0]["db2"] + tie
    for l in range(DEPTH):
        smalls[l]["loss"] = lacc if l == 0 else jnp.zeros_like(lacc)
    return lacc[0, 0], dy, smalls


BIG = ("w_in", "s5_w_glu", "w_out", "w_ff1", "w_ff2")
SMALL = ("s5_a_re", "s5_a_im", "s5_log_step", "s5_b_re", "s5_b_im", "s5_c_re", "s5_c_im", "s5_d", "s5_b_glu",
         "gla_w_a", "gla_b_a", "gla_ln_g", "swa_sink", "ln1_g", "ln1_b", "ln2_g", "ln2_b")
ANY = pl.BlockSpec(memory_space=pl.ANY)


def _place():
    x, y, c = lax.axis_index("x"), lax.axis_index("y"), lax.axis_index("c")
    return x, y, c, [(1 - x, y), (x, 1 - y), (1 - x, 1 - y)]


HBM = pl.BlockSpec(memory_space=pltpu.HBM)
SEMS = pl.BlockSpec(memory_space=pltpu.SEMAPHORE)
EFFECT = pltpu.SideEffectType.DATAFLOW_SIDE_EFFECTING


def _push_copies(ins, lands, send, recv, gather, sending):
    x, y, c, chips = _place()
    me = 2 * x + y
    if gather == "sibling":
        return [pltpu.make_async_remote_copy(src_ref=ins[a], dst_ref=lands[a], send_sem=send.at[a], recv_sem=recv.at[a],
                                             device_id=(x, y, 1 - c), device_id_type=MESH) for a in range(len(lands))]
    out = []
    for a in range(len(lands)):
        for j, (px, py) in enumerate(chips):
            peer = 2 * px + py
            src = lands[a].at[me] if gather else ins[a].at[peer if sending else me]
            dst = lands[a].at[me if sending else peer]
            out.append(pltpu.make_async_remote_copy(src_ref=src, dst_ref=dst, send_sem=send.at[3 * a + j],
                                                    recv_sem=recv.at[3 * a + j], device_id=(px, py, c),
                                                    device_id_type=MESH))
    return out


def _push_start(name, arrs, gather):
    n = len(arrs)
    ops = list(arrs) if gather is True else list(arrs) + [lax.empty(s.shape, s.dtype) for s in arrs]
    m = len(ops)

    def body(*refs):
        ins, lnd = (refs[:n], refs[:n]) if gather is True else (refs[:n], refs[n:m])
        for cp in _push_copies(ins, lnd, refs[m], refs[m + 1], gather, True):
            cp.start()
        refs[-1][...] = jnp.zeros((8, 128), F32)

    ops = [pltpu.with_memory_space_constraint(t, pltpu.HBM) for t in ops]
    res = pl.pallas_call(
        body, name=name,
        out_shape=(pltpu.SemaphoreType.DMA((3 * n,)), pltpu.SemaphoreType.DMA((3 * n,)),
                   *[pltpu.HBM(t.shape, t.dtype) for t in ops], _sds((8, 128))),
        in_specs=[HBM] * m,
        out_specs=(SEMS, SEMS, *[HBM] * m, pl.BlockSpec(memory_space=pltpu.VMEM)),
        input_output_aliases={i: 2 + i for i in range(m)},
        compiler_params=pltpu.CompilerParams(has_side_effects=EFFECT))(*ops)
    return res[0], res[1], list(res[2:2 + m]), res[-1]


def _push_wait(name, started, after, gather):
    send, recv, ops, _ = started
    m = len(ops)
    n = m if gather is True else m // 2

    def body(*refs):
        ins, lnd = (refs[:n], refs[:n]) if gather is True else (refs[:n], refs[n:m])
        for cp in _push_copies(ins, lnd, refs[m], refs[m + 1], gather, False):
            cp.wait_send()
            cp.wait_recv()

    res = pl.pallas_call(
        body, name=name,
        out_shape=[pltpu.HBM(t.shape, t.dtype) for t in ops],
        in_specs=[HBM] * m + [SEMS, SEMS, ANY], out_specs=[HBM] * m,
        input_output_aliases={i: i for i in range(m)},
        compiler_params=pltpu.CompilerParams(has_side_effects=EFFECT))(*ops, send, recv, after)
    return list(res)


def _row_tile(rows):
    return max(t for t in range(8, min(rows, 512) + 1, 8) if rows % t == 0)


def _cast_to_slot(me, w, l):
    _, rows, cols = w.shape
    tr = _row_tile(rows)

    def body(me_ref, w_ref, o_ref):
        o_ref[0] = w_ref[0].astype(MX)

    return pl.pallas_call(
        body,
        grid_spec=pltpu.PrefetchScalarGridSpec(
            num_scalar_prefetch=1, grid=(rows // tr,),
            in_specs=[pl.BlockSpec((1, tr, cols), lambda i, me_: (l, i, 0))],
            out_specs=pl.BlockSpec((1, tr, cols), lambda i, me_: (me_[0], i, 0))),
        out_shape=_sds((NSHARD, rows, cols), MX), name="cast_to_slot", compiler_params=_cp(("arbitrary",)))(me, w)


def _sum_sources(me, recv, own):
    _, rows, cols = recv[0].shape
    tr = min(_row_tile(rows), 256) if rows % 256 == 0 else _row_tile(rows)
    nt = rows // tr

    def body(me_ref, *refs):
        o_ref = refs[-1]
        for l in range(DEPTH):
            @pl.when(pl.program_id(0) == l)
            def _():
                r_ref, own_ref = refs[2 * l], refs[2 * l + 1]
                part = [jnp.where(me_ref[0] == s, own_ref[0], r_ref[s]).astype(F32) for s in range(NSHARD)]
                o_ref[...] = ((part[0] + part[1]) + part[2]) + part[3]

    in_specs = []
    for l in range(DEPTH):
        pick = lambda g, i, me_, l=l: jnp.where(g == l, i, jnp.where(g < l, 0, nt - 1))
        in_specs += [pl.BlockSpec((NSHARD, tr, cols), lambda g, i, me_, pick=pick: (0, pick(g, i, me_), 0)),
                     pl.BlockSpec((1, tr, cols), lambda g, i, me_, pick=pick: (me_[0], pick(g, i, me_), 0))]
    return pl.pallas_call(
        body,
        grid_spec=pltpu.PrefetchScalarGridSpec(
            num_scalar_prefetch=1, grid=(DEPTH, nt), in_specs=in_specs,
            out_specs=pl.BlockSpec((tr, cols), lambda g, i, me_: (g * nt + i, 0))),
        out_shape=_sds((DEPTH * rows, cols)), name="sum_sources",
        compiler_params=_cp(("arbitrary", "arbitrary")))(me, *[t for l in range(DEPTH) for t in (recv[l], own[l])])


def _allreduce_small(per_layer):
    nk = len(per_layer[0])
    n = DEPTH * nk
    shapes = [a.shape for a in per_layer[0]]

    def body(*refs):
        ins, outs = refs[:n], refs[n:n + nk]
        sibs, slots = refs[n + nk:n + 2 * nk], refs[n + 2 * nk:n + 3 * nk]
        send, recv = refs[n + 3 * nk:]
        x, y, c, chips = _place()
        me = 2 * x + y
        d2d = [pltpu.make_async_remote_copy(src_ref=ins[l * nk + k], dst_ref=sibs[k].at[l], send_sem=send.at[l * nk + k],
                                            recv_sem=recv.at[l * nk + k], device_id=(x, y, 1 - c), device_id_type=MESH)
               for l in range(DEPTH) for k in range(nk)]
        for cp in d2d:
            cp.start()
        for cp in d2d:
            cp.wait()
        for l in range(DEPTH):
            for k in range(nk):
                slots[k][0, l] = ins[l * nk + k][...] + sibs[k][l]

        def swap(k, stage):
            peer = (1 - x, y, c) if stage == 0 else (x, 1 - y, c)
            return pltpu.make_async_remote_copy(src_ref=slots[k].at[2 * stage], dst_ref=slots[k].at[2 * stage + 1],
                                                send_sem=send.at[n + 3 * k + stage], recv_sem=recv.at[n + 3 * k + stage],
                                                device_id=peer, device_id_type=MESH)

        def handover(k):
            return pltpu.make_async_remote_copy(src_ref=outs[k], dst_ref=outs[k], send_sem=send.at[n + 3 * nk + k],
                                                recv_sem=recv.at[n + 3 * nk + k], device_id=(x, y, 1 - c),
                                                device_id_type=MESH)

        halves = (tuple(k for k in range(nk) if ICI_CORE[k] == 0), tuple(k for k in range(nk) if ICI_CORE[k] == 1))
        for cc in range(2):
            @pl.when(c == cc)
            def _():
                mine, theirs = halves[cc], halves[1 - cc]
                for stage in range(2):
                    cps = [swap(k, stage) for k in mine]
                    for cp in cps:
                        cp.start()
                    for cp in cps:
                        cp.wait()
                    for k in mine:
                        if stage == 0:
                            slots[k][2] = slots[k][0] + slots[k][1]
                        else:
                            outs[k][...] = slots[k][2] + slots[k][3]
                over = [handover(k) for k in mine]
                for cp in over:
                    cp.start()
                for k in theirs:
                    handover(k).wait_recv()
                for cp in over:
                    cp.wait_send()

    vm = pl.BlockSpec(memory_space=pltpu.VMEM)
    return pl.pallas_call(
        body, in_specs=[vm] * n, out_specs=[vm] * nk, out_shape=[_sds((DEPTH,) + s) for s in shapes],
        scratch_shapes=([pltpu.VMEM((DEPTH,) + s, F32) for s in shapes]
                        + [pltpu.VMEM((NSHARD, DEPTH) + s, F32) for s in shapes]
                        + [pltpu.SemaphoreType.DMA((n + 4 * nk,)), pltpu.SemaphoreType.DMA((n + 4 * nk,))]),
        name="allreduce_small", compiler_params=pltpu.CompilerParams(vmem_limit_bytes=VMEM_LIMIT))(
            *[a for layer in per_layer for a in layer])


def _adamw_math(w, g, m, v):
    m = ADAM_B1 * m + (1.0 - ADAM_B1) * g
    v = ADAM_B2 * v + (1.0 - ADAM_B2) * jnp.square(g)
    m_hat = m / (1.0 - ADAM_B1 ** ADAM_STEP)
    v_hat = v / (1.0 - ADAM_B2 ** ADAM_STEP)
    delta = -ADAM_LR * (m_hat / (jnp.sqrt(v_hat) + ADAM_EPS) + ADAM_WD * w)
    return delta, m, v


def _adamw(g_parts, w, m, v):
    rows, cols = w.shape
    tr = 256 if rows % 256 == 0 else _row_tile(rows)
    k = len(g_parts)

    def body(*refs):
        g = refs[0][...]
        for r in refs[1:k]:
            g = g + r[...]
        w_ref, m_ref, v_ref, go, do, mo, vo = refs[k:]
        d, mn, vn = _adamw_math(w_ref[...], g, m_ref[...], v_ref[...])
        go[...] = g
        do[...] = d
        mo[...] = mn
        vo[...] = vn

    spec = pl.BlockSpec((tr, cols), lambda i: (i, 0))
    return pl.pallas_call(
        body, grid=(rows // tr,), in_specs=[spec] * (k + 3), out_specs=[spec] * 4,
        out_shape=[_sds((rows, cols))] * 4, name="adamw", compiler_params=_cp(("parallel",)))(*g_parts, w, m, v)


def _adamw_small(gs, ws, ms, vs):
    n = len(gs)

    def body(*refs):
        for k in range(n):
            d, mn, vn = _adamw_math(refs[n + k][...], refs[k][...], refs[2 * n + k][...], refs[3 * n + k][...])
            refs[4 * n + k][...] = d
            refs[5 * n + k][...] = mn
            refs[6 * n + k][...] = vn

    vm = pl.BlockSpec(memory_space=pltpu.VMEM)
    shapes = [_sds(a.shape) for a in ws]
    res = pl.pallas_call(
        body, in_specs=[vm] * (4 * n), out_specs=[vm] * (3 * n), out_shape=shapes * 3, name="adamw_small",
        compiler_params=pltpu.CompilerParams(vmem_limit_bytes=VMEM_LIMIT))(*gs, *ws, *ms, *vs)
    return res[:n], res[n:2 * n], res[2 * n:]


_ARGS = ("x", "w_in", "s5_a_re", "s5_a_im", "s5_log_step", "s5_b_re", "s5_b_im", "s5_c_re", "s5_c_im", "s5_d",
         "s5_w_glu", "s5_b_glu", "gla_w_a", "gla_b_a", "gla_ln_g", "swa_sink", "w_out", "ln1_g", "ln1_b", "w_ff1",
         "w_ff2", "ln2_g", "ln2_b")
_WEIGHTS = _ARGS[1:]


def kernel(x, w_in, s5_a_re, s5_a_im, s5_log_step, s5_b_re, s5_b_im, s5_c_re, s5_c_im, s5_d, s5_w_glu, s5_b_glu, gla_w_a, gla_b_a, gla_ln_g, swa_sink, w_out, ln1_g, ln1_b, w_ff1, w_ff2, ln2_g, ln2_b, loss_target, m_w_in, m_s5_a_re, m_s5_a_im, m_s5_log_step, m_s5_b_re, m_s5_b_im, m_s5_c_re, m_s5_c_im, m_s5_d, m_s5_w_glu, m_s5_b_glu, m_gla_w_a, m_gla_b_a, m_gla_ln_g, m_swa_sink, m_w_out, m_ln1_g, m_ln1_b, m_w_ff1, m_w_ff2, m_ln2_g, m_ln2_b, v_w_in, v_s5_a_re, v_s5_a_im, v_s5_log_step, v_s5_b_re, v_s5_b_im, v_s5_c_re, v_s5_c_im, v_s5_d, v_s5_w_glu, v_s5_b_glu, v_gla_w_a, v_gla_b_a, v_gla_ln_g, v_swa_sink, v_w_out, v_ln1_g, v_ln1_b, v_w_ff1, v_w_ff2, v_ln2_g, v_ln2_b):
    given = dict(locals())
    w = {k: given[k] for k in _WEIGHTS}
    mom = {k: given["m_" + k] for k in _WEIGHTS}
    var = {k: given["v_" + k] for k in _WEIGHTS}

    me = (2 * lax.axis_index("x") + lax.axis_index("y")).astype(jnp.int32).reshape(1)
    tr = lambda t: t.transpose(0, 2, 1)
    shard = {k: (tr(w[k]) if k == "w_in" else w[k]) for k in BIG}
    qs = [None] * DEPTH

    first = ("w_in", "s5_w_glu", "w_out")
    follow = {(0, "w_in"): [(0, BIG[3:])], (0, "s5_w_glu"): [(1, first)], (0, "w_ff1"): [(1, BIG[3:])]}
    gathers = {}

    casts = {}

    def start_gather(l, names, behind=None):
        lands = [casts.pop((l, k)) if (l, k) in casts else _cast_to_slot(me, shard[k], l) for k in names]
        if behind is not None:
            lands, behind = lax.optimization_barrier((lands, behind))
        st = _push_start(f"gather_start_{l}_{names[0]}", lands, True)
        for k in names:
            gathers[l, k] = [names, st, None]
        return st[-1], behind

    token = start_gather(0, first[:1])[0] + start_gather(0, first[1:])[0]
    zero = token[0, 0]
    for l in range(DEPTH):
        for k in BIG:
            if (l, k) not in gathers:
                casts[l, k] = _cast_to_slot(me, lax.optimization_barrier((shard[k], token))[0], l)
        qs[l] = _layer_prep({k: (w[k][l] + zero if k == "s5_a_re" else w[k][l]) for k in SMALL})
    token, casts, qs = lax.optimization_barrier((token, casts, qs))

    def fetch(l, name, after):
        names, st, got = gathers[l, name]
        tie = None
        if got is None:
            if l == 0 and name == "w_in":
                after = token
            lands = _push_wait(f"gather_wait_{l}_{names[0]}", st, after, True)
            for l2, names2 in follow.get((l, name), ()):
                tok, lands[0] = start_gather(l2, names2, lands[0])
                tie = tok if tie is None else tie + tok
            got = dict(zip(names, lands))
            for k in names:
                gathers[l, k][2] = got
        full = got[name]
        if name == "w_in":
            return _in_rows(full, token if tie is None else tie)
        if tie is not None:
            near = "bv" if name == "s5_w_glu" else "ln2_b"
            qs[l][near] = qs[l][near] + tie[0, 0]
        return full.reshape(D, D) if name == "w_out" else full

    scatters, held = [], {}

    def emit(l, grads):
        if l > 0:
            held.update(grads)
            if "w_in" not in grads:
                return 0.0
            grads = dict(held)
            held.clear()
        names = tuple(grads)
        st = _push_start(f"scatter_start_{l}_{names[0]}", [grads[k] for k in names], False)
        scatters.append((l, names, st))
        return st[-1][0, 0]

    loss, dx, smalls = _local_step(x.reshape(N, D), loss_target.reshape(N, D), qs, _rope_tables(128), fetch, emit)

    out, recv, own = {}, {}, {}

    def collect(keys, after):
        for l, names, st in scatters:
            if names[0] in keys:
                ops = _push_wait(f"scatter_wait_{l}_{names[0]}", st, after, False)
                for i, k in enumerate(names):
                    own[l, k], recv[l, k] = ops[i], ops[len(names) + i]

    def to_sibling(keys):
        sums = [_sum_sources(me, [recv[l, k] for l in range(DEPTH)], [own[l, k] for l in range(DEPTH)]) for k in keys]
        return _push_start(f"swap_start_{keys[0]}", sums, "sibling")

    def apply(keys, started, after):
        ops = _push_wait(f"swap_wait_{keys[0]}", started, after, "sibling")
        for i, k in enumerate(keys):
            mine, other = ops[i], ops[len(keys) + i]
            shp = shard[k].shape
            r = _adamw([mine, other], *((tr(t[k]) if k == "w_in" else t[k]).reshape(-1, shp[-1]) for t in (w, mom, var)))
            r = [t.reshape(shp) for t in r]
            out[k] = [tr(t) for t in r] if k == "w_in" else r
        return out[keys[-1]][1]

    collect(("w_ff1", "w_ff2", "w_out", "s5_w_glu"), dx)
    ff = to_sibling(("w_ff1", "w_ff2"))
    mix = to_sibling(("w_out", "s5_w_glu"))
    smalls[0]["db1"] = smalls[0]["db1"] + (ff[-1][0, 0] + mix[-1][0, 0])
    native = _allreduce_small([[smalls[l][k] for k in NATIVE] for l in range(DEPTH)])
    native = dict(zip(NATIVE, native))
    loss = native["loss"][0, 0, 0] + native["loss"][1, 0, 0]
    gsmall = _finish_small(native, w)
    view = lambda k, t: t.transpose(0, 1, 2, 4, 3) if k in ("s5_b_re", "s5_b_im") else t
    res = _adamw_small(*([view(k, t[k]) for k in SMALL] for t in (gsmall, w, mom, var)))
    for i, k in enumerate(SMALL):
        out[k] = [gsmall[k]] + [view(k, r[i]) for r in res]
    last = apply(("w_ff1", "w_ff2"), ff, res[0][-1])
    collect(("w_in",), last)
    win = to_sibling(("w_in",))
    last = apply(("w_out", "s5_w_glu"), mix, win[-1])
    apply(("w_in",), win, last)

    return (loss, dx.reshape(NSEQ, L, D), *[out[k][0] for k in _WEIGHTS], *[out[k][1] for k in _WEIGHTS],
            *[out[k][2] for k in _WEIGHTS], *[out[k][3] for k in _WEIGHTS])
```

```python
import functools
import math

import jax
import jax.numpy as jnp
from jax import lax
from jax.experimental import pallas as pl
from jax.experimental.pallas import tpu as pltpu

F32 = jnp.float32
MX = jnp.bfloat16
MESH = pl.DeviceIdType.MESH

DEPTH = 2
NSEQ = 2
L = 2048
N = NSEQ * L
D = 1024
DFF = 4096
NSHARD = 4
S5_G, S5_H, S5_P = 16, 16, 64
GLA_CHUNK = 64
NCHUNK = L // GLA_CHUNK
GLA_GROUP = 4
NGROUP = NCHUNK // GLA_GROUP
SWA_BLK = 128
NBLK = L // SWA_BLK
SWA_PER = 2
ROT = 16
ROPE_THETA = 500000.0
LN_EPS = 1e-5
ALPHA = (2 * DEPTH) ** 0.25
NEG_BIG = -1e30
DIN = 1824
DINP = 1920
ADAM_LR, ADAM_B1, ADAM_B2, ADAM_EPS, ADAM_WD, ADAM_STEP = 0.001, 0.9, 0.999, 1e-08, 0.01, 10
VMEM_LIMIT = 56 * 1024 * 1024
TT = 512
SW = 512
FFN_TM = 512
FFN_TM_W = 1024
FFN_WB = 1
FFN_VMEM = 60 * 1024 * 1024
INPROJ_BWD_TM = 512


def _cp(sem, vmem=VMEM_LIMIT):
    return pltpu.CompilerParams(dimension_semantics=sem, vmem_limit_bytes=vmem)


def _mm(a, b):
    return jnp.dot(a.astype(MX), b.astype(MX), preferred_element_type=F32)


def _mm_nt(a, b):
    return lax.dot_general(a.astype(MX), b.astype(MX), (((1,), (1,)), ((), ())), preferred_element_type=F32)


def _mm_tn(a, b):
    return lax.dot_general(a.astype(MX), b.astype(MX), (((0,), (0,)), ((), ())), preferred_element_type=F32)


@jax.custom_vjp
def _dmm(a, b):
    return _mm(a, b)


_dmm.defvjp(lambda a, b: (_mm(a, b), (a, b)), lambda r, g: (_mm_nt(g, r[1]), _mm_tn(r[0], g)))


@jax.custom_vjp
def _dmm_nt(a, b):
    return _mm_nt(a, b)


_dmm_nt.defvjp(lambda a, b: (_mm_nt(a, b), (a, b)), lambda r, g: (_mm(g, r[1]), _mm_tn(g, r[0])))


@jax.custom_vjp
def _dmm_tn(a, b):
    return _mm_tn(a, b)


_dmm_tn.defvjp(lambda a, b: (_mm_tn(a, b), (a, b)), lambda r, g: (_mm_nt(r[1], g), _mm(r[0], g)))


def _split3(x):
    hi = x.astype(MX)
    r1 = x - hi.astype(F32)
    mid = r1.astype(MX)
    lo = (r1 - mid.astype(F32)).astype(MX)
    return hi, mid, lo


def _chunk_pairs(rows, rev, strict):
    r = lax.broadcasted_iota(jnp.int32, (rows, rows), 0)
    c = lax.broadcasted_iota(jnp.int32, (rows, rows), 1)
    order = ((c > r) if strict else (c >= r)) if rev else ((c < r) if strict else (c <= r))
    return (r // GLA_CHUNK == c // GLA_CHUNK) & order


def _cums_impl(x, rev):
    rows, w = x.shape
    t = jnp.where(_chunk_pairs(rows, rev, False), 1.0, 0.0).astype(MX)
    s = jnp.dot(t, jnp.concatenate(_split3(x), axis=1), preferred_element_type=F32)
    return s[:, 0:w] + s[:, w:2 * w] + s[:, 2 * w:3 * w]


@functools.partial(jax.custom_vjp, nondiff_argnums=(1,))
def _cums(x, rev):
    return _cums_impl(x, rev)


_cums.defvjp(lambda x, rev: (_cums_impl(x, rev), None), lambda rev, r, g: (_cums_impl(g, not rev),))


def _ln_fwd(s, g, b):
    mu = jnp.mean(s, axis=-1, keepdims=True)
    xc = s - mu
    var = jnp.mean(xc * xc, axis=-1, keepdims=True)
    return xc * lax.rsqrt(var + LN_EPS) * g + b


def _ln_bwd(dy, s, g):
    mu = jnp.mean(s, axis=-1, keepdims=True)
    xc = s - mu
    var = jnp.mean(xc * xc, axis=-1, keepdims=True)
    rstd = lax.rsqrt(var + LN_EPS)
    xhat = xc * rstd
    dxh = dy * g
    ds = rstd * (dxh - jnp.mean(dxh, axis=-1, keepdims=True) - xhat * jnp.mean(dxh * xhat, axis=-1, keepdims=True))
    return ds, jnp.sum(dy * xhat, axis=0, keepdims=True), jnp.sum(dy, axis=0, keepdims=True)


def _sds(shape, dtype=F32):
    return jax.ShapeDtypeStruct(shape, dtype)


_IN_ROW_PIECES = (((0, 0), (0, 456)), ((1, 0), (456, 456)), ((2, 0), (912, 112)), ((2, 112), (1792, 32)),
                  ((2, 144), (1024, 312)), ((3, 0), (1336, 456)))


def _in_rows(g4, behind):
    def body(g_ref, behind_ref, o_ref, tmp):
        tmp[DIN:DINP] = jnp.zeros((DINP - DIN, D), F32)
        for (j, s0), (d0, n_) in _IN_ROW_PIECES:
            tmp[d0:d0 + n_] = g_ref[j, s0:s0 + n_].astype(F32)
        o_ref[...] = tmp[...].astype(MX)

    vm = pl.BlockSpec(memory_space=pltpu.VMEM)
    return pl.pallas_call(body, in_specs=[vm, pl.BlockSpec(memory_space=pl.ANY)], out_specs=vm,
                          out_shape=_sds((DINP, D), MX), scratch_shapes=[pltpu.VMEM((DINP, D), F32)], name="in_rows",
                          compiler_params=pltpu.CompilerParams(vmem_limit_bytes=VMEM_LIMIT))(g4, behind)


def _inproj_fwd(x, wt, wa, ba):
    tm = 512

    def body(x_ref, w_ref, wa_ref, ba_ref, h_ref, la_ref):
        h = _mm_nt(x_ref[...], w_ref[...])
        h_ref[...] = h
        la_ref[...] = _logsig(_mm(h[:, DINP - 128:], wa_ref[...]) + ba_ref[...]) * (1.0 / 16.0)

    return pl.pallas_call(
        body, grid=(N // tm,),
        in_specs=[pl.BlockSpec((tm, D), lambda i: (i, 0)), pl.BlockSpec((DINP, D), lambda i: (0, 0)),
                  pl.BlockSpec((128, 256), lambda i: (0, 0)), pl.BlockSpec((1, 256), lambda i: (0, 0))],
        out_specs=[pl.BlockSpec((tm, DINP), lambda i: (i, 0)), pl.BlockSpec((tm, 256), lambda i: (i, 0))],
        out_shape=[_sds((N, DINP)), _sds((N, 256))], name="inproj_fwd", compiler_params=_cp(("parallel",)))(x, wt, wa, ba)


def _inproj_bwd(x, w, dxp, du2, dud, gq_f, gq_b, gk_f, gk_b, gv_f, gv_b, gr, daq, dakv, dhl):
    tm = INPROJ_BWD_TM
    nt = N // tm

    def body(x_ref, w_ref, dxp_ref, du2_ref, dud_ref, gqf, gqb, gkf, gkb, gvf, gvb, gr_ref, daq_ref, dakv_ref, dhl_ref,
             dx_ref, dw_ref, acc):
        i = pl.program_id(0)
        f = lambda r: r[...].astype(F32)
        dh = jnp.concatenate([
            du2_ref[0] + du2_ref[1] + f(dud_ref), f(gqf) + f(gqb), f(gkf) + f(gkb), f(gvf) + f(gvb),
            f(gr_ref), f(daq_ref), f(dakv_ref), f(dhl_ref)], axis=1)
        dx_ref[...] = dxp_ref[...] + _mm(dh, w_ref[...])
        contrib = _mm_tn(dh, x_ref[...])

        @pl.when(i == 0)
        def _():
            acc[...] = contrib

        @pl.when(i > 0)
        def _():
            acc[...] += contrib

        @pl.when(i == nt - 1)
        def _():
            for (j, d0), (s0, n_) in _IN_ROW_PIECES:
                dw_ref[j, d0:d0 + n_] = acc[s0:s0 + n_].astype(MX)

    row = lambda w_: pl.BlockSpec((tm, w_), lambda i: (i, 0))
    return pl.pallas_call(
        body, grid=(nt,),
        in_specs=[row(D), pl.BlockSpec((DINP, D), lambda i: (0, 0)), row(D),
                  pl.BlockSpec((2, tm, 256), lambda i: (0, i, 0)), row(256), row(128), row(128), row(128), row(128),
                  row(256), row(256), row(256), row(512), row(256), row(128)],
        out_specs=[row(D), pl.BlockSpec((NSHARD, DIN // NSHARD, D), lambda i: (0, 0, 0))],
        out_shape=[_sds((N, D)), _sds((NSHARD, DIN // NSHARD, D), MX)],
        scratch_shapes=[pltpu.VMEM((DINP, D), F32)],
        name="inproj_bwd", compiler_params=_cp(("arbitrary",)))(
            x, w, dxp, du2, dud, gq_f, gq_b, gk_f, gk_b, gv_f, gv_b, gr, daq, dakv, dhl)


def _tile_scan(xr, xi, a, cr, ci, reverse):
    for lvl, d in enumerate((1, 2, 4)):
        sh = 8 - d if reverse else d
        sr = pltpu.roll(xr, sh, 0)
        si = pltpu.roll(xi, sh, 0)
        ar, ai = a[2 * lvl], a[2 * lvl + 1]
        xr, xi = xr + ar * sr - ai * si, xi + ar * si + ai * sr
    pr, pi = a[6], a[7]
    return xr + pr * cr - pi * ci, xi + pr * ci + pi * cr


NJ = TT // 8


def _lockstep_tables(mr, mi):
    def body(mr_ref, mi_ref, a_ref, p_ref, ac_ref, pc_ref):
        rowid = lax.broadcasted_iota(jnp.int32, (8, 2 * SW), 0)

        def mul(a, b):
            return a[0] * b[0] - a[1] * b[1], a[0] * b[1] + a[1] * b[0]

        for z in range(2):
            for sign, reverse, a_out, p_out in ((1.0, z == 1, a_ref, p_ref), (-1.0, z == 0, ac_ref, pc_ref)):
                m = (mr_ref[z:z + 1, :], sign * mi_ref[z:z + 1, :])
                pw = [m]
                for _ in range(NJ - 1):
                    pw.append(mul(pw[-1], m))
                n = pw[-1]
                link = [n]
                for _ in range(7):
                    link.append(mul(link[-1], n))
                tiles = [jnp.broadcast_to(m[0], (8, 2 * SW)), jnp.broadcast_to(m[1], (8, 2 * SW))]
                for d in (1, 2, 4):
                    keep = (rowid <= 7 - d) if reverse else (rowid >= d)
                    tiles += [jnp.where(keep, link[d - 1][c], 0.0) for c in range(2)]
                for c in range(2):
                    t = jnp.zeros((8, 2 * SW), F32)
                    for i in range(8):
                        t = jnp.where(rowid == (7 - i if reverse else i), link[i][c], t)
                    tiles.append(t)
                for blk in range(2):
                    lanes = slice(blk * SW, (blk + 1) * SW)
                    for k, t in enumerate(tiles):
                        a_out[z, blk, k] = t[:, lanes]
                    for j in range(NJ):
                        src = pw[NJ - 1 - j] if reverse else pw[j]
                        for c in range(2):
                            p_out[z, blk, c, j:j + 1, :] = src[c][:, lanes]

    vm = pl.BlockSpec(memory_space=pltpu.VMEM)
    a_shape, p_shape = _sds((2, 2, 10, 8, SW)), _sds((2, 2, 2, NJ, SW))
    a, p, ac, pc = pl.pallas_call(body, in_specs=[vm, vm], out_specs=[vm] * 4, out_shape=[a_shape, p_shape] * 2,
                                  name="s5_tables")(mr, mi)
    return (a, p), (ac, pc)


def _to_lockstep(ref, *lead):
    return jnp.concatenate([ref[(*lead, pl.ds(j, 8, stride=NJ), slice(None))] for j in range(NJ)], axis=0)


def _from_lockstep(val, ref, *lead):
    for j in range(NJ):
        ref[(*lead, pl.ds(j, 8, stride=NJ), slice(None))] = val[8 * j:8 * j + 8]


def _expand_powers(p_ref, pexp):
    for c in range(2):
        for j in range(NJ):
            pexp[c, j] = jnp.broadcast_to(p_ref[0, 0, c, j:j + 1, :], (8, SW))


def _lockstep_scan(xre, xim, a_ref, pexp, car, reverse, extra=None):
    a = [a_ref[0, 0, k] for k in range(10)]
    mr, mi = a[0], a[1]
    order = (lambda i: NJ - 1 - i) if reverse else (lambda i: i)

    def local(i, hcar):
        hr, hi = hcar
        r0 = pl.multiple_of(order(i) * 8, 8)
        hr, hi = mr * hr - mi * hi + xre[pl.ds(r0, 8), :], mr * hi + mi * hr + xim[pl.ds(r0, 8), :]
        xre[pl.ds(r0, 8), :] = hr
        xim[pl.ds(r0, 8), :] = hi
        return hr, hi

    z8 = jnp.zeros((8, SW), F32)
    er, ei = lax.fori_loop(0, NJ, local, (z8, z8), unroll=4)
    c0r, c0i = car[0], car[1]
    er, ei = _tile_scan(er, ei, a[2:], c0r, c0i, reverse)
    rowid = lax.broadcasted_iota(jnp.int32, (8, SW), 0)
    first, sh, last = (7, 7, 0) if reverse else (0, 1, 7)
    cvr = jnp.where(rowid == first, c0r, pltpu.roll(er, sh, 0))
    cvi = jnp.where(rowid == first, c0i, pltpu.roll(ei, sh, 0))
    car[0] = jnp.broadcast_to(er[last:last + 1, :], (8, SW))
    car[1] = jnp.broadcast_to(ei[last:last + 1, :], (8, SW))

    def fix(i, carry):
        j = order(i)
        r0 = pl.multiple_of(j * 8, 8)
        pr, pi = pexp[0, j], pexp[1, j]
        sr = xre[pl.ds(r0, 8), :] + pr * cvr - pi * cvi
        si = xim[pl.ds(r0, 8), :] + pr * cvi + pi * cvr
        xre[pl.ds(r0, 8), :] = sr
        xim[pl.ds(r0, 8), :] = si
        if extra is None:
            return carry
        return (sr, si, extra(r0, sr, si, carry[0], carry[1], carry[2]))

    init = (cvr, cvi, extra(None, None, None, None, None, None)) if extra is not None else 0
    return lax.fori_loop(0, NJ, fix, init, unroll=4)


def _s5_time_block(z, s, t, adjoint):
    flip = (1 - z) if adjoint else z
    return s * (L // TT) + t + flip * (L // TT - 1 - 2 * t)


def _s5_fwd(h, bre, bim, cre, cim, tab):
    nt = L // TT
    taba, tabp = tab

    def body(u_ref, bre_ref, bim_ref, cre_ref, cim_ref, a_ref, p_ref, hre_ref, him_ref, y_ref, car, pexp):
        z = pl.program_id(1)
        s = pl.program_id(2)
        tc = pl.program_id(3)

        @pl.when(tc == 0)
        def _():
            car[...] = jnp.zeros_like(car)

        @pl.when((tc == 0) & (s == 0))
        def _():
            _expand_powers(p_ref, pexp)

        u = _to_lockstep(u_ref)
        hre_ref[0] = _mm(u, bre_ref[0, 0])
        him_ref[0] = _mm(u, bim_ref[0, 0])

        @pl.when(z == 0)
        def _():
            _lockstep_scan(hre_ref.at[0], him_ref.at[0], a_ref, pexp, car, False)

        @pl.when(z == 1)
        def _():
            _lockstep_scan(hre_ref.at[0], him_ref.at[0], a_ref, pexp, car, True)

        _from_lockstep(_mm(hre_ref[0], cre_ref[0, 0]) - _mm(him_ref[0], cim_ref[0, 0]), y_ref, 0)

    tb = lambda b, z, s, t: _s5_time_block(z, s, t, False)
    wspec = lambda r, c: pl.BlockSpec((1, 1, r, c), lambda b, z, s, t: (z, b, 0, 0))
    return pl.pallas_call(
        body, grid=(2, 2, NSEQ, nt),
        in_specs=[pl.BlockSpec((TT, 128), lambda b, z, s, t: (tb(b, z, s, t), b)),
                  wspec(128, SW), wspec(128, SW), wspec(SW, 128), wspec(SW, 128),
                  pl.BlockSpec((1, 1, 10, 8, SW), lambda b, z, s, t: (z, b, 0, 0, 0)),
                  pl.BlockSpec((1, 1, 2, NJ, SW), lambda b, z, s, t: (z, b, 0, 0, 0))],
        out_specs=[pl.BlockSpec((1, TT, SW), lambda b, z, s, t: (z, tb(b, z, s, t), b)),
                   pl.BlockSpec((1, TT, SW), lambda b, z, s, t: (z, tb(b, z, s, t), b)),
                   pl.BlockSpec((1, TT, 128), lambda b, z, s, t: (z, tb(b, z, s, t), b))],
        out_shape=[_sds((2, N, 2 * SW)), _sds((2, N, 2 * SW)), _sds((2, N, 256))],
        scratch_shapes=[pltpu.VMEM((2, 8, SW), F32), pltpu.VMEM((2, NJ, 8, SW), F32)],
        name="s5_fwd", compiler_params=_cp(("arbitrary",) * 4))(h, bre, bim, cre, cim, taba, tabp)


def _s5_bwd(h, dyp, hre, him, bre, bim, cre, cim, tabc):
    nt = L // TT
    taba, tabp = tabc

    def body(u_ref, dy_ref, hre_ref, him_ref, bre_ref, bim_ref, cre_ref, cim_ref, a_ref, p_ref,
             du_ref, dbre_ref, dbim_ref, dcre_ref, dcim_ref, dmu_ref, gre, gim, car, acc, macc, pexp):
        z = pl.program_id(1)
        s = pl.program_id(2)
        tc = pl.program_id(3)

        @pl.when(tc == 0)
        def _():
            car[...] = jnp.zeros_like(car)

        @pl.when((tc == 0) & (s == 0))
        def _():
            acc[...] = jnp.zeros_like(acc)
            macc[...] = jnp.zeros_like(macc)
            _expand_powers(p_ref, pexp)

        dy = _to_lockstep(dy_ref)
        gre[...] = _mm_nt(dy, cre_ref[0, 0])
        gim[...] = -_mm_nt(dy, cim_ref[0, 0])

        def run(reverse):
            def pair(r0, gr_, gi_, pvr, pvi, m):
                if r0 is None:
                    return (macc[0], macc[1])
                hr = hre_ref[0, pl.ds(r0, 8), :]
                hi = him_ref[0, pl.ds(r0, 8), :]
                return (m[0] + pvr * hr + pvi * hi, m[1] + pvi * hr - pvr * hi)

            _, _, (dmr, dmi) = _lockstep_scan(gre, gim, a_ref, pexp, car, reverse, pair)
            macc[0] = dmr
            macc[1] = dmi

        @pl.when(z == 0)
        def _():
            run(True)

        @pl.when(z == 1)
        def _():
            run(False)

        gr = gre[...]
        gi = gim[...]
        u = _to_lockstep(u_ref)
        _from_lockstep(_mm_nt(gr, bre_ref[0, 0]) + _mm_nt(gi, bim_ref[0, 0]), du_ref, 0)
        acc[0] += _mm_tn(u, gr)
        acc[1] += _mm_tn(u, gi)
        acc[2] += _mm_tn(dy, hre_ref[0])
        acc[3] -= _mm_tn(dy, him_ref[0])

        @pl.when((tc == nt - 1) & (s == NSEQ - 1))
        def _():
            grp = lax.broadcasted_iota(jnp.int32, (S5_H, SW), 1) // S5_P
            for k, out in enumerate((dbre_ref, dbim_ref, dcre_ref, dcim_ref)):
                c = jnp.zeros((S5_H, SW), F32)
                for i in range(8):
                    c = c + jnp.where(grp == i, acc[k, i * S5_H:(i + 1) * S5_H, :], 0.0)
                out[0, 0] = c
            dmu_ref[0, 0] = jnp.concatenate([jnp.sum(macc[0], axis=0, keepdims=True),
                                             jnp.sum(macc[1], axis=0, keepdims=True)], axis=0)

    tb = lambda b, z, s, t: _s5_time_block(z, s, t, True)
    wspec = lambda r, c: pl.BlockSpec((1, 1, r, c), lambda b, z, s, t: (z, b, 0, 0))
    tok = lambda w_: pl.BlockSpec((TT, w_), lambda b, z, s, t: (tb(b, z, s, t), b))
    st = pl.BlockSpec((1, TT, SW), lambda b, z, s, t: (z, tb(b, z, s, t), b))
    return pl.pallas_call(
        body, grid=(2, 2, NSEQ, nt),
        in_specs=[tok(128), tok(128), st, st, wspec(128, SW), wspec(128, SW), wspec(SW, 128), wspec(SW, 128),
                  pl.BlockSpec((1, 1, 10, 8, SW), lambda b, z, s, t: (z, b, 0, 0, 0)),
                  pl.BlockSpec((1, 1, 2, NJ, SW), lambda b, z, s, t: (z, b, 0, 0, 0))],
        out_specs=[pl.BlockSpec((1, TT, 128), lambda b, z, s, t: (z, tb(b, z, s, t), b)),
                   wspec(S5_H, SW), wspec(S5_H, SW), wspec(S5_H, SW), wspec(S5_H, SW),
                   wspec(2, SW)],
        out_shape=[_sds((2, N, 256))] + [_sds((2, 2, S5_H, SW))] * 4 + [_sds((2, 2, 2, SW))],
        scratch_shapes=[pltpu.VMEM((TT, SW), F32), pltpu.VMEM((TT, SW), F32), pltpu.VMEM((2, 8, SW), F32),
                        pltpu.VMEM((4, 128, SW), F32), pltpu.VMEM((2, 8, SW), F32), pltpu.VMEM((2, NJ, 8, SW), F32)],
        name="s5_bwd", compiler_params=_cp(("arbitrary",) * 4))(h, dyp, hre, him, bre, bim, cre, cim, taba, tabp)


_GELU_C = math.sqrt(2.0 / math.pi)


def _gelu(y):
    return 0.5 * y * (1.0 + jnp.tanh(_GELU_C * (y + 0.044715 * y * y * y)))


def _gelu_grad(y):
    t = jnp.tanh(_GELU_C * (y + 0.044715 * y * y * y))
    return 0.5 * (1.0 + t) + 0.5 * y * (1.0 - t * t) * _GELU_C * (1.0 + 3 * 0.044715 * y * y)


def _glu_halves(w4_ref):
    return (jnp.concatenate([w4_ref[0], w4_ref[1]], axis=1), jnp.concatenate([w4_ref[2], w4_ref[3]], axis=1))


def _s5_glu_fwd(y2, h, dsk, w4, bv, bg):
    tm = 512

    def body(y2_ref, u_ref, d_ref, w4_ref, bv_ref, bg_ref, ya_ref):
        wv, wg = _glu_halves(w4_ref)
        z = _gelu(y2_ref[0] + y2_ref[1] + d_ref[...] * u_ref[...])
        val = _mm(z, wv) + bv_ref[...]
        gate = _mm(z, wg) + bg_ref[...]
        ya_ref[...] = (val * jax.nn.sigmoid(gate)).astype(MX)

    full = lambda r, c: pl.BlockSpec((r, c), lambda i: (0, 0))
    return pl.pallas_call(
        body, grid=(N // tm,),
        in_specs=[pl.BlockSpec((2, tm, 256), lambda i: (0, i, 0)), pl.BlockSpec((tm, 256), lambda i: (i, 0)),
                  full(1, 256), pl.BlockSpec((NSHARD, 256, 128), lambda i: (0, 0, 0)), full(1, 256), full(1, 256)],
        out_specs=pl.BlockSpec((tm, 256), lambda i: (i, 0)),
        out_shape=_sds((N, 256), MX), name="s5_glu_fwd", compiler_params=_cp(("parallel",)))(y2, h, dsk, w4, bv, bg)


def _s5_glu_bwd(y2, h, dsk, w4, bv, bg, dya):
    tm = 512
    nt = N // tm

    def body(y2_ref, u_ref, d_ref, w4_ref, bv_ref, bg_ref, dya_ref,
             dyp_ref, dud_ref, dd_ref, dw4_ref, dbv_ref, dbg_ref, accv, accg):
        i = pl.program_id(0)

        @pl.when(i == 0)
        def _():
            for r in (dd_ref, accv, accg, dbv_ref, dbg_ref):
                r[...] = jnp.zeros_like(r)

        wv, wg = _glu_halves(w4_ref)
        u = u_ref[...]
        y = y2_ref[0] + y2_ref[1] + d_ref[...] * u
        z = _gelu(y)
        val = _mm(z, wv) + bv_ref[...]
        sig = jax.nn.sigmoid(_mm(z, wg) + bg_ref[...])
        dya = dya_ref[...]
        dval = dya * sig
        dgate = dya * val * sig * (1.0 - sig)
        dz = _mm_nt(dval, wv) + _mm_nt(dgate, wg)
        dy = dz * _gelu_grad(y)
        dyp_ref[...] = dy
        dud_ref[...] = (dy * d_ref[...]).astype(MX)
        dd_ref[...] += jnp.sum(dy * u, axis=0, keepdims=True)
        accv[...] += _mm_tn(z, dval)
        accg[...] += _mm_tn(z, dgate)
        dbv_ref[...] += jnp.sum(dval, axis=0, keepdims=True)
        dbg_ref[...] += jnp.sum(dgate, axis=0, keepdims=True)

        @pl.when(i == nt - 1)
        def _():
            dw4_ref[0] = accv[:, 0:128].astype(MX)
            dw4_ref[1] = accv[:, 128:256].astype(MX)
            dw4_ref[2] = accg[:, 0:128].astype(MX)
            dw4_ref[3] = accg[:, 128:256].astype(MX)

    full = lambda r, c: pl.BlockSpec((r, c), lambda i: (0, 0))
    row = pl.BlockSpec((tm, 256), lambda i: (i, 0))
    wspec = pl.BlockSpec((NSHARD, 256, 128), lambda i: (0, 0, 0))
    return pl.pallas_call(
        body, grid=(nt,),
        in_specs=[pl.BlockSpec((2, tm, 256), lambda i: (0, i, 0)), row, full(1, 256), wspec, full(1, 256), full(1, 256),
                  row],
        out_specs=[row, row, full(1, 256), wspec, full(1, 256), full(1, 256)],
        out_shape=[_sds((N, 256)), _sds((N, 256), MX), _sds((1, 256)), _sds((NSHARD, 256, 128), MX), _sds((1, 256)),
                   _sds((1, 256))],
        scratch_shapes=[pltpu.VMEM((256, 256), F32), pltpu.VMEM((256, 256), F32)],
        name="s5_glu_bwd", compiler_params=_cp(("arbitrary",)))(y2, h, dsk, w4, bv, bg, dya)


def _logsig(x):
    return jnp.minimum(x, 0.0) - jnp.log(1.0 + jnp.exp(-jnp.abs(x)))


def _gla_gate_bwd(h, wa, ba, dla_f, dla_b):
    tm = 512

    def body(hl_ref, wa_ref, ba_ref, df_ref, db_ref, dhl_ref, dwa_ref, dba_ref):
        i = pl.program_id(0)

        @pl.when(i == 0)
        def _():
            dwa_ref[...] = jnp.zeros_like(dwa_ref)
            dba_ref[...] = jnp.zeros_like(dba_ref)

        hl = hl_ref[...]
        pre = _mm(hl, wa_ref[...]) + ba_ref[...]
        dpre = jnp.concatenate([df_ref[...], db_ref[...]], axis=1) * (1.0 / 16.0) * jax.nn.sigmoid(-pre)
        dhl_ref[...] = _mm_nt(dpre, wa_ref[...]).astype(MX)
        dwa_ref[...] += _mm_tn(hl, dpre)[0:32]
        dba_ref[...] += jnp.sum(dpre, axis=0, keepdims=True)

    row = pl.BlockSpec((tm, 128), lambda i: (i, 0))
    return pl.pallas_call(
        body, grid=(N // tm,),
        in_specs=[pl.BlockSpec((tm, 128), lambda i: (i, 14)), pl.BlockSpec((128, 256), lambda i: (0, 0)),
                  pl.BlockSpec((1, 256), lambda i: (0, 0)), row, row],
        out_specs=[row, pl.BlockSpec((32, 256), lambda i: (0, 0)), pl.BlockSpec((1, 256), lambda i: (0, 0))],
        out_shape=[_sds((N, 128), MX), _sds((32, 256)), _sds((1, 256))],
        name="gla_gate_bwd", compiler_params=_cp(("arbitrary",)))(h, wa, ba, dla_f, dla_b)


def _gla_chunk(q, k, v, la, st, rev):
    c = GLA_CHUNK
    rows = q.shape[0]
    nch = rows // c
    b = _cums(la, rev)
    blc = [jnp.sum(la[i * c:(i + 1) * c], axis=0, keepdims=True) for i in range(nch)]
    bl = jnp.concatenate([jnp.broadcast_to(t, (c, 128)) for t in blc], axis=0)
    q_in = q * (32.0 ** -0.5) * jnp.exp(b)
    k_in = k * jnp.exp(-b)
    k_st = k * jnp.exp(bl - b)
    lane_k = lax.broadcasted_iota(jnp.int32, (1, 128), 1) // 32
    lane_v = lax.broadcasted_iota(jnp.int32, (1, 256), 1) // 64
    qs = jnp.concatenate([jnp.where(lane_k == hd, q_in, 0.0) for hd in range(4)], axis=0)
    a = _dmm_nt(qs, k_in)
    a = jnp.where(jnp.concatenate([_chunk_pairs(rows, rev, rev)] * 4, axis=0), a, 0.0)
    o4 = _dmm(a, v)
    o = jnp.zeros((rows, 256), F32)
    for hd in range(4):
        o = o + jnp.where(lane_v == hd, o4[hd * rows:(hd + 1) * rows], 0.0)
    bd = (lax.broadcasted_iota(jnp.int32, (256, 128), 0) // 64) == (lax.broadcasted_iota(jnp.int32, (256, 128), 1) // 32)
    inter = [None] * nch
    for i in (reversed(range(nch)) if rev else range(nch)):
        sl = slice(i * c, (i + 1) * c)
        inter[i] = _dmm_nt(q_in[sl], st)
        st = jnp.exp(blc[i]) * st + jnp.where(bd, _dmm_tn(v[sl], k_st[sl]), 0.0)
    return o + jnp.concatenate(inter, axis=0), st


def _gla_chunk_of(c, rev):
    return NGROUP - 1 - c if rev else c


def _gla_fwd(h, la2):
    c = GLA_GROUP * GLA_CHUNK

    def body(qf, kf, vf, laf, qb, kb, vb, lab, of_ref, ob_ref, sf_ref, sb_ref, stf, stb):
        @pl.when(pl.program_id(0) == 0)
        def _():
            stf[...] = jnp.zeros_like(stf)
            stb[...] = jnp.zeros_like(stb)

        ins = [(qf[s], kf[s], vf[s], laf[s], stf[s], qb[s], kb[s], vb[s], lab[s], stb[s]) for s in range(NSEQ)]
        outs = [(_gla_chunk(*t[:5], False), _gla_chunk(*t[5:], True)) for t in ins]
        for s in range(NSEQ):
            sf_ref[s, 0] = ins[s][4]
            sb_ref[s, 0] = ins[s][9]
            (of_ref[s], stf[s]), (ob_ref[s], stb[s]) = outs[s]

    def specs(rev):
        ch = lambda i: _gla_chunk_of(i, rev)
        return [pl.BlockSpec((NSEQ, c, 128), lambda i: (0, ch(i), 2)), pl.BlockSpec((NSEQ, c, 128), lambda i: (0, ch(i), 3)),
                pl.BlockSpec((NSEQ, c, 256), lambda i: (0, ch(i), 2)),
                pl.BlockSpec((NSEQ, c, 128), lambda i: (0, ch(i), 1 if rev else 0))]

    orow = lambda rev: pl.BlockSpec((NSEQ, c, 256), lambda i: (0, _gla_chunk_of(i, rev), 0))
    srow = lambda rev: pl.BlockSpec((NSEQ, 1, 256, 128), lambda i: (0, _gla_chunk_of(i, rev), 0, 0))
    h3, la3 = h.reshape(NSEQ, L, DINP), la2.reshape(NSEQ, L, 256)
    of, ob, sf, sb = pl.pallas_call(
        body, grid=(NGROUP,),
        in_specs=specs(False) + specs(True),
        out_specs=[orow(False), orow(True), srow(False), srow(True)],
        out_shape=[_sds((NSEQ, L, 256)), _sds((NSEQ, L, 256)), _sds((NSEQ, NGROUP, 256, 128)),
                   _sds((NSEQ, NGROUP, 256, 128))],
        scratch_shapes=[pltpu.VMEM((NSEQ, 256, 128), F32), pltpu.VMEM((NSEQ, 256, 128), F32)],
        name="gla_fwd", compiler_params=_cp(("arbitrary",)))(h3, h3, h3, la3, h3, h3, h3, la3)
    return of.reshape(N, 256), ob.reshape(N, 256), sf, sb


def _gla_bwd(h, la2, do, sf, sb):
    c = GLA_GROUP * GLA_CHUNK

    def body(qf, kf, vf, laf, dof, sfr, qb, kb, vb, lab, dob, sbr,
             dqf, dkf, dvf, dlf, dqb, dkb, dvb, dlb, dstf, dstb):
        @pl.when(pl.program_id(0) == 0)
        def _():
            dstf[...] = jnp.zeros_like(dstf)
            dstb[...] = jnp.zeros_like(dstb)

        def one(s, q, k, v, la, do_, st, dst, rev):
            _, vjp = jax.vjp(functools.partial(_gla_chunk, rev=rev), q[s], k[s], v[s], la[s], st[s, 0])
            return vjp((do_[s], dst[s]))

        res = [(one(s, qf, kf, vf, laf, dof, sfr, dstf, False), one(s, qb, kb, vb, lab, dob, sbr, dstb, True))
               for s in range(NSEQ)]
        for s in range(NSEQ):
            for (gq, gk, gv, gl, gs), (dq, dk, dv, dl, dst) in ((res[s][0], (dqf, dkf, dvf, dlf, dstf)),
                                                                  (res[s][1], (dqb, dkb, dvb, dlb, dstb))):
                dq[s], dk[s], dv[s] = gq.astype(MX), gk.astype(MX), gv.astype(MX)
                dl[s], dst[s] = gl, gs

    def specs(rev):
        ch = lambda i: _gla_chunk_of(i, not rev)
        return [pl.BlockSpec((NSEQ, c, 128), lambda i: (0, ch(i), 2)), pl.BlockSpec((NSEQ, c, 128), lambda i: (0, ch(i), 3)),
                pl.BlockSpec((NSEQ, c, 256), lambda i: (0, ch(i), 2)),
                pl.BlockSpec((NSEQ, c, 128), lambda i: (0, ch(i), 1 if rev else 0)),
                pl.BlockSpec((NSEQ, c, 256), lambda i: (0, ch(i), 0)),
                pl.BlockSpec((NSEQ, 1, 256, 128), lambda i: (0, ch(i), 0, 0))]

    def ospecs(rev):
        ch = lambda i: _gla_chunk_of(i, not rev)
        n = pl.BlockSpec((NSEQ, c, 128), lambda i: (0, ch(i), 0))
        return [n, n, pl.BlockSpec((NSEQ, c, 256), lambda i: (0, ch(i), 0)), n]

    oshape = [_sds((NSEQ, L, 128), MX), _sds((NSEQ, L, 128), MX), _sds((NSEQ, L, 256), MX), _sds((NSEQ, L, 128))]
    h3, la3, do3 = h.reshape(NSEQ, L, DINP), la2.reshape(NSEQ, L, 256), do.reshape(NSEQ, L, 256)
    res = pl.pallas_call(
        body, grid=(NGROUP,),
        in_specs=specs(False) + specs(True),
        out_specs=ospecs(False) + ospecs(True),
        out_shape=oshape + oshape,
        scratch_shapes=[pltpu.VMEM((NSEQ, 256, 128), F32), pltpu.VMEM((NSEQ, 256, 128), F32)],
        name="gla_bwd", compiler_params=_cp(("arbitrary",)))(h3, h3, h3, la3, do3, sf, h3, h3, h3, la3, do3, sb)
    return [r.reshape(N, r.shape[-1]) for r in res]


def _gla_post(of, ob, r, g):
    o = of + ob
    head = lax.broadcasted_iota(jnp.int32, (1, 256), 1) // 64
    mu = jnp.zeros_like(o)
    for hd in range(4):
        mu = mu + jnp.where(head == hd, jnp.sum(jnp.where(head == hd, o, 0.0), axis=-1, keepdims=True) * (1.0 / 64.0), 0.0)
    xc = o - mu
    var = jnp.zeros_like(o)
    for hd in range(4):
        var = var + jnp.where(head == hd, jnp.sum(jnp.where(head == hd, xc * xc, 0.0), axis=-1, keepdims=True) * (1.0 / 64.0), 0.0)
    return xc * lax.rsqrt(var + LN_EPS) * g * (r * jax.nn.sigmoid(r))


def _gla_post_fwd(of, ob, h, g):
    tm = 512

    def body(of_ref, ob_ref, r_ref, g_ref, y_ref):
        y_ref[...] = _gla_post(of_ref[...], ob_ref[...], r_ref[...], g_ref[...]).astype(MX)

    row = pl.BlockSpec((tm, 256), lambda i: (i, 0))
    return pl.pallas_call(
        body, grid=(N // tm,),
        in_specs=[row, row, pl.BlockSpec((tm, 256), lambda i: (i, 3)), pl.BlockSpec((1, 256), lambda i: (0, 0))],
        out_specs=row, out_shape=_sds((N, 256), MX), name="gla_post_fwd", compiler_params=_cp(("parallel",)))(of, ob, h, g)


def _gla_post_bwd(of, ob, h, g, dyb):
    tm = 512

    def body(of_ref, ob_ref, r_ref, g_ref, dy_ref, do_ref, dr_ref, dg_ref):
        @pl.when(pl.program_id(0) == 0)
        def _():
            dg_ref[...] = jnp.zeros_like(dg_ref)

        _, vjp = jax.vjp(_gla_post, of_ref[...], ob_ref[...], r_ref[...], g_ref[...])
        go, _, gr, gg = vjp(dy_ref[...])
        do_ref[...] = go
        dr_ref[...] = gr.astype(MX)
        dg_ref[...] += gg

    row = pl.BlockSpec((tm, 256), lambda i: (i, 0))
    one = pl.BlockSpec((1, 256), lambda i: (0, 0))
    return pl.pallas_call(
        body, grid=(N // tm,),
        in_specs=[row, row, pl.BlockSpec((tm, 256), lambda i: (i, 3)), one, row],
        out_specs=[row, row, one], out_shape=[_sds((N, 256)), _sds((N, 256), MX), _sds((1, 256))],
        name="gla_post_bwd", compiler_params=_cp(("arbitrary",)))(of, ob, h, g, dyb)


def _rope_tables(width):
    pos = jnp.arange(L, dtype=F32)
    inv_freq = ROPE_THETA ** (-jnp.arange(0, ROT, 2, dtype=F32) / ROT)
    ang = pos[:, None] * inv_freq[None, :]
    cos, sin = jnp.cos(ang), jnp.sin(ang)
    one = jnp.ones((L, 64 - ROT), F32)
    zero = jnp.zeros((L, 64 - ROT), F32)
    z8 = jnp.zeros((L, ROT // 2), F32)
    c = jnp.concatenate([cos, cos, one], axis=1)
    sa = jnp.concatenate([z8, sin, zero], axis=1)
    sb = jnp.concatenate([-sin, z8, zero], axis=1)
    rep = width // 64
    return jnp.stack([jnp.tile(c, (1, rep)), jnp.tile(sa, (1, rep)), jnp.tile(sb, (1, rep))])


def _pieces(t, f):
    out = [f(t[:, c * 128:(c + 1) * 128]) for c in range(t.shape[-1] // 128)]
    return out[0] if len(out) == 1 else jnp.concatenate(out, axis=1)


def _rope(t, tab):
    return _pieces(t, lambda x: x * tab[0] + pltpu.roll(x, ROT // 2, 1) * tab[1] + pltpu.roll(x, 128 - ROT // 2, 1) * tab[2])


def _rope_t(g, tab):
    return _pieces(g, lambda x: x * tab[0] + pltpu.roll(x * tab[1], 128 - ROT // 2, 1) + pltpu.roll(x * tab[2], ROT // 2, 1))


def _swa_pad_kv(kv_ref, tk_ref, kexp, vexp):
    z = jnp.zeros((SWA_BLK, 256), F32)
    kr = _rope(kv_ref[:, 0:128], tk_ref[...])
    for hk in range(2):
        for pad in (kexp, vexp):
            pad[hk, 0:SWA_BLK] = z
            pad[hk, SWA_BLK + L:] = z
        kexp[hk, SWA_BLK:SWA_BLK + L] = _swa_expand(kr, hk)
        vexp[hk, SWA_BLK:SWA_BLK + L] = _swa_expand(kv_ref[:, 128:256], hk)


def _swa_expand(x, hk):
    lane = lax.broadcasted_iota(jnp.int32, x.shape, 1)
    sw = pltpu.roll(x, 64, 1)
    pair = jnp.where(lane < 64, x, sw) if hk == 0 else jnp.where(lane < 64, sw, x)
    return jnp.concatenate([pair, pair], axis=1)


def _swa_fold(x, hk):
    a = x[:, 0:128] + x[:, 128:256]
    t = a + pltpu.roll(a, 64, 1)
    lane = lax.broadcasted_iota(jnp.int32, a.shape, 1)
    return jnp.where((lane < 64) if hk == 0 else (lane >= 64), t, 0.0)


def _swa_probs(q2, kexp, n, sink_ref, hk):
    slot = lax.broadcasted_iota(jnp.int32, (1, 256), 1) // 64
    qs = jnp.concatenate([jnp.where(slot == g, q2, 0.0) for g in range(4)], axis=0)
    s = _mm_nt(qs, kexp) * 0.125
    i = lax.broadcasted_iota(jnp.int32, (SWA_BLK, 3 * SWA_BLK), 0)
    j = lax.broadcasted_iota(jnp.int32, (SWA_BLK, 3 * SWA_BLK), 1)
    kpos = n * SWA_BLK - SWA_BLK + j
    ok = (j - i >= 0) & (j - i <= 2 * SWA_BLK) & (kpos >= 0) & (kpos < L)
    s = jnp.where(jnp.concatenate([ok] * 4, axis=0), s, NEG_BIG)
    rowg = lax.broadcasted_iota(jnp.int32, (4 * SWA_BLK, 1), 0) // SWA_BLK
    sink = jnp.zeros((4 * SWA_BLK, 1), F32)
    for g in range(4):
        sink = jnp.where(rowg == g, sink_ref[hk * 4 + g], sink)
    m = jnp.maximum(jnp.max(s, axis=-1, keepdims=True), sink)
    p = jnp.exp(s - m)
    ps = jnp.exp(sink - m)
    inv = 1.0 / (jnp.sum(p, axis=-1, keepdims=True) + ps)
    return qs, p * inv, ps * inv, slot, rowg


def _swa_qtab(tk_ref, r0):
    return [tk_ref[i, pl.ds(r0, SWA_BLK), :] for i in range(3)]


def _swa_fwd(h, tk, sink):
    def body(sink_ref, q_ref, kv_ref, tk_ref, y_ref, kexp, vexp):
        n = pl.program_id(1)

        @pl.when(n == 0)
        def _():
            _swa_pad_kv(kv_ref, tk_ref, kexp, vexp)

        for t in range(SWA_PER):
            blk = n * SWA_PER + t
            rows = slice(t * SWA_BLK, (t + 1) * SWA_BLK)
            r0 = pl.multiple_of(blk * SWA_BLK, SWA_BLK)
            q = _rope(q_ref[rows, :], _swa_qtab(tk_ref, r0))
            for hk in range(2):
                _, p, _, slot, _ = _swa_probs(q[:, hk * 256:(hk + 1) * 256], kexp[hk, pl.ds(r0, 3 * SWA_BLK), :], blk,
                                              sink_ref, hk)
                o4 = _mm(p, vexp[hk, pl.ds(r0, 3 * SWA_BLK), :])
                o = jnp.zeros((SWA_BLK, 256), F32)
                for g in range(4):
                    o = o + jnp.where(slot == g, o4[g * SWA_BLK:(g + 1) * SWA_BLK], 0.0)
                y_ref[rows, hk * 256:(hk + 1) * 256] = o.astype(MX)

    tm = SWA_PER * SWA_BLK
    return pl.pallas_call(
        body,
        grid_spec=pltpu.PrefetchScalarGridSpec(
            num_scalar_prefetch=1, grid=(NSEQ, L // tm),
            in_specs=[pl.BlockSpec((tm, 512), lambda s, n, sk: (s * (L // tm) + n, 2)),
                      pl.BlockSpec((L, 256), lambda s, n, sk: (s, 6)),
                      pl.BlockSpec((3, L, 128), lambda s, n, sk: (0, 0, 0))],
            out_specs=pl.BlockSpec((tm, 512), lambda s, n, sk: (s * (L // tm) + n, 0)),
            scratch_shapes=[pltpu.VMEM((2, L + 2 * SWA_BLK, 256), F32), pltpu.VMEM((2, L + 2 * SWA_BLK, 256), F32)]),
        out_shape=_sds((N, 512), MX), name="swa_fwd", compiler_params=_cp(("arbitrary", "arbitrary")))(sink, h, h, tk)


def _swa_bwd(h, tk, sink, dyc):
    tm = SWA_PER * SWA_BLK

    def body(sink_ref, q_ref, kv_ref, tk_ref, dy_ref, dq_ref, dkv_ref, dsink_ref, kexp_all, vexp_all, dkacc, dvacc):
        sq = pl.program_id(0)
        n = pl.program_id(1)

        @pl.when(n == 0)
        def _():
            _swa_pad_kv(kv_ref, tk_ref, kexp_all, vexp_all)
            dkacc[...] = jnp.zeros_like(dkacc)
            dvacc[...] = jnp.zeros_like(dvacc)

        @pl.when((n == 0) & (sq == 0))
        def _():
            dsink_ref[...] = jnp.zeros_like(dsink_ref)

        hrow = lax.broadcasted_iota(jnp.int32, (8, 128), 0)
        dsk = jnp.zeros((8, 128), F32)
        for t in range(SWA_PER):
            blk = n * SWA_PER + t
            rows = slice(t * SWA_BLK, (t + 1) * SWA_BLK)
            r0 = pl.multiple_of(blk * SWA_BLK, SWA_BLK)
            tq = _swa_qtab(tk_ref, r0)
            q = _rope(q_ref[rows, :], tq)
            for hk in range(2):
                kexp = kexp_all[hk, pl.ds(r0, 3 * SWA_BLK), :]
                vexp = vexp_all[hk, pl.ds(r0, 3 * SWA_BLK), :]
                qs, p, ps, slot, rowg = _swa_probs(q[:, hk * 256:(hk + 1) * 256], kexp, blk, sink_ref, hk)
                dy2 = dy_ref[rows, hk * 256:(hk + 1) * 256]
                dos = jnp.concatenate([jnp.where(slot == g, dy2, 0.0) for g in range(4)], axis=0)
                dp = _mm_nt(dos, vexp)
                delta = jnp.sum(p * dp, axis=-1, keepdims=True)
                ds = p * (dp - delta) * 0.125
                dsr = -ps * delta
                for g in range(4):
                    dsk = dsk + jnp.where(hrow == hk * 4 + g,
                                          jnp.sum(jnp.where(rowg == g, dsr, 0.0), axis=0, keepdims=True), 0.0)
                dq4 = _mm(ds, kexp)
                dq2 = jnp.zeros((SWA_BLK, 256), F32)
                for g in range(4):
                    dq2 = dq2 + jnp.where(slot == g, dq4[g * SWA_BLK:(g + 1) * SWA_BLK], 0.0)
                dq_ref[rows, hk * 256:(hk + 1) * 256] = _rope_t(dq2, tq).astype(MX)
                dkacc[hk, pl.ds(r0, 3 * SWA_BLK), :] += _mm_tn(ds, qs)
                dvacc[hk, pl.ds(r0, 3 * SWA_BLK), :] += _mm_tn(p, dos)
        dsink_ref[...] += dsk

        @pl.when(n == L // tm - 1)
        def _():
            seq = slice(SWA_BLK, SWA_BLK + L)
            dk = _rope_t(_swa_fold(dkacc[0, seq], 0) + _swa_fold(dkacc[1, seq], 1), tk_ref[...])
            dkv_ref[:, 0:128] = dk.astype(MX)
            dkv_ref[:, 128:256] = (_swa_fold(dvacc[0, seq], 0) + _swa_fold(dvacc[1, seq], 1)).astype(MX)

    blk = lambda col: pl.BlockSpec((tm, 512), lambda s, n, sk: (s * (L // tm) + n, col))
    pad = pltpu.VMEM((2, L + 2 * SWA_BLK, 256), F32)
    return pl.pallas_call(
        body,
        grid_spec=pltpu.PrefetchScalarGridSpec(
            num_scalar_prefetch=1, grid=(NSEQ, L // tm),
            in_specs=[blk(2), pl.BlockSpec((L, 256), lambda s, n, sk: (s, 6)),
                      pl.BlockSpec((3, L, 128), lambda s, n, sk: (0, 0, 0)), blk(0)],
            out_specs=[blk(0), pl.BlockSpec((L, 256), lambda s, n, sk: (s, 0)),
                       pl.BlockSpec((8, 128), lambda s, n, sk: (0, 0))],
            scratch_shapes=[pad, pad, pad, pad]),
        out_shape=[_sds((N, 512), MX), _sds((N, 256), MX), _sds((8, 128))],
        name="swa_bwd", compiler_params=_cp(("arbitrary", "arbitrary")))(sink, h, h, tk, dyc)


def _outproj_bwd(dx1, s1, ya, yb, yc, wo, g):
    tm = 512
    nt = N // tm

    def body(dx1_ref, s_ref, ya_ref, yb_ref, yc_ref, wo_ref, g_ref,
             dya_ref, dyb_ref, dyc_ref, dxp_ref, dwo_ref, dg_ref, db_ref, acc):
        i = pl.program_id(0)

        @pl.when(i == 0)
        def _():
            acc[...] = jnp.zeros_like(acc)
            dg_ref[...] = jnp.zeros_like(dg_ref)
            db_ref[...] = jnp.zeros_like(db_ref)

        ds, dg, db = _ln_bwd(dx1_ref[...], s_ref[...], g_ref[...])
        dg_ref[...] += dg
        db_ref[...] += db
        dxp_ref[...] = ALPHA * ds
        dy = _mm_nt(ds, wo_ref[...])
        dya_ref[...] = dy[:, 0:256]
        dyb_ref[...] = dy[:, 256:512]
        dyc_ref[...] = dy[:, 512:1024]
        acc[0:256] += _mm_tn(ya_ref[...], ds)
        acc[256:512] += _mm_tn(yb_ref[...], ds)
        acc[512:1024] += _mm_tn(yc_ref[...], ds)

        @pl.when(i == nt - 1)
        def _():
            dwo_ref[...] = acc[...].astype(MX)

    row = lambda w_: pl.BlockSpec((tm, w_), lambda i: (i, 0))
    one = pl.BlockSpec((1, D), lambda i: (0, 0))
    full = pl.BlockSpec((D, D), lambda i: (0, 0))
    return pl.pallas_call(
        body, grid=(nt,),
        in_specs=[row(D), row(D), row(256), row(256), row(512), full, one],
        out_specs=[row(256), row(256), row(512), row(D), full, one, one],
        out_shape=[_sds((N, 256)), _sds((N, 256)), _sds((N, 512)), _sds((N, D)), _sds((D, D), MX), _sds((1, D)), _sds((1, D))],
        scratch_shapes=[pltpu.VMEM((D, D), F32)],
        name="outproj_bwd", compiler_params=_cp(("arbitrary",)))(dx1, s1, ya, yb, yc, wo, g)


def _mix_ffn_fwd(ya, yb, yc, x, wo, g1, b1, w1, w2, g, b, target=None):
    tm = FFN_TM
    head = target is not None

    def body(*refs):
        ya_ref, yb_ref, yc_ref, xin_ref, wo_ref, g1_ref, b1_ref, w1_ref, w2_ref, g_ref, b_ref = refs[:11]
        s1_ref, x1_ref, a_ref, s_ref, y_ref = refs[11 + head:16 + head]
        mix = _mm(ya_ref[...], wo_ref[0:256]) + _mm(yb_ref[...], wo_ref[256:512]) + _mm(yc_ref[...], wo_ref[512:1024])
        s1 = ALPHA * xin_ref[...] + mix
        s1_ref[...] = s1
        x = _ln_fwd(s1, g1_ref[...], b1_ref[...])
        x1_ref[...] = x
        xb = x.astype(MX)
        s = ALPHA * x
        for j in range(NSHARD):
            a = _mm(xb, w1_ref[j])
            a_ref[:, j * D:(j + 1) * D] = a.astype(MX)
            s = s + _mm(jnp.square(jnp.maximum(a, 0.0)), w2_ref[j])
        s_ref[...] = s
        x2 = _ln_fwd(s, g_ref[...], b_ref[...])
        if not head:
            y_ref[...] = x2
            return
        l_ref = refs[16 + head]

        @pl.when(pl.program_id(0) == 0)
        def _():
            l_ref[...] = jnp.zeros_like(l_ref)

        e = x2 - refs[11][...]
        y_ref[...] = e * (1.0 / D)
        l_ref[...] += jnp.sum(jnp.sum(e * e, axis=1, keepdims=True), axis=0, keepdims=True) * (0.5 / D)

    rw = lambda w_: pl.BlockSpec((tm, w_), lambda i: (i, 0))
    row = rw(D)
    once = dict(pipeline_mode=pl.Buffered(1))
    wall = pl.BlockSpec((NSHARD, D, D), lambda i: (0, 0, 0), **once)
    one = pl.BlockSpec((1, D), lambda i: (0, 0))
    acc = pl.BlockSpec((8, 128), lambda i: (0, 0))
    return pl.pallas_call(
        body, grid=(N // tm,),
        in_specs=[rw(256), rw(256), rw(512), row, pl.BlockSpec((D, D), lambda i: (0, 0), **once), one, one,
                  wall, wall, one, one] + [row] * head,
        out_specs=[row, row, pl.BlockSpec((tm, DFF), lambda i: (i, 0)), row, row] + [acc] * head,
        out_shape=[_sds((N, D)), _sds((N, D)), _sds((N, DFF), MX), _sds((N, D)), _sds((N, D))] + [_sds((8, 128))] * head,
        name="mix_ffn_fwd", compiler_params=_cp(("arbitrary",), FFN_VMEM))(
            ya, yb, yc, x, wo, g1, b1, w1, w2, g, b, *([target] * head))


def _ffn_bwd_act(dy, s2, a, w1, w2, g):
    tm = FFN_TM

    def body(dy_ref, s_ref, a_ref, w1_ref, w2_ref, g_ref, da_ref, ds_ref, dx1_ref, dg_ref, db_ref):
        @pl.when(pl.program_id(0) == 0)
        def _():
            dg_ref[...] = jnp.zeros_like(dg_ref)
            db_ref[...] = jnp.zeros_like(db_ref)

        ds, dg, db = _ln_bwd(dy_ref[...], s_ref[...], g_ref[...])
        dsb = ds.astype(MX)
        ds_ref[...] = dsb
        dg_ref[...] += dg
        db_ref[...] += db
        dx1 = ALPHA * ds
        for j in range(NSHARD):
            da = (_mm_nt(dsb, w2_ref[j]) * 2.0 * jnp.maximum(a_ref[:, j * D:(j + 1) * D].astype(F32), 0.0)).astype(MX)
            da_ref[:, j * D:(j + 1) * D] = da
            dx1 = dx1 + _mm_nt(da, w1_ref[j])
        dx1_ref[...] = dx1

    row = pl.BlockSpec((tm, D), lambda i: (i, 0))
    wide = pl.BlockSpec((tm, DFF), lambda i: (i, 0))
    wall = pl.BlockSpec((NSHARD, D, D), lambda i: (0, 0, 0))
    one = pl.BlockSpec((1, D), lambda i: (0, 0))
    return pl.pallas_call(
        body, grid=(N // tm,),
        in_specs=[row, row, wide, wall, wall, one],
        out_specs=[wide, row, row, one, one],
        out_shape=[_sds((N, DFF), MX), _sds((N, D), MX), _sds((N, D)), _sds((1, D)), _sds((1, D))],
        name="ffn_bwd_act", compiler_params=_cp(("arbitrary",), FFN_VMEM))(dy, s2, a, w1, w2, g)


def _ffn_bwd_w(x1, da, a, ds):
    tm, nb = FFN_TM_W, FFN_WB
    nt = N // tm

    def body(x_ref, da_ref, a_ref, ds_ref, dw1_ref, dw2_ref, acc1, acc2):
        i = pl.program_id(1)

        @pl.when(i == 0)
        def _():
            acc1[...] = jnp.zeros_like(acc1)
            acc2[...] = jnp.zeros_like(acc2)

        x, ds_ = x_ref[...], ds_ref[...]
        for k in range(nb):
            cols = slice(k * D, (k + 1) * D)
            acc1[k] += _mm_tn(x, da_ref[:, cols])
            acc2[k] += _mm_tn(jnp.square(jnp.maximum(a_ref[:, cols].astype(F32), 0.0)), ds_)

        @pl.when(i == nt - 1)
        def _():
            dw1_ref[...] = acc1[...].astype(MX)
            dw2_ref[...] = acc2[...].astype(MX)

    row = pl.BlockSpec((tm, D), lambda j, i: (i, 0))
    col = pl.BlockSpec((tm, nb * D), lambda j, i: (i, j))
    wj = pl.BlockSpec((nb, D, D), lambda j, i: (j, 0, 0))
    return pl.pallas_call(
        body, grid=(NSHARD // nb, nt),
        in_specs=[row, col, col, row], out_specs=[wj, wj],
        out_shape=[_sds((NSHARD, D, D), MX), _sds((NSHARD, D, D), MX)],
        scratch_shapes=[pltpu.VMEM((nb, D, D), F32), pltpu.VMEM((nb, D, D), F32)],
        name="ffn_bwd_w", compiler_params=_cp(("parallel", "arbitrary"), FFN_VMEM))(x1, da, a, ds)


def _s5_discretize(a_re, a_im, log_step, b_re, b_im):
    lam = lax.complex(a_re, a_im)
    lam_bar = jnp.exp(lam * jnp.exp(log_step))
    b_bar = ((lam_bar - 1.0) / lam)[..., None] * lax.complex(b_re, b_im)
    return jnp.real(lam_bar), jnp.imag(lam_bar), jnp.real(b_bar), jnp.imag(b_bar)


def _s5_in_blocks(b):
    e = jnp.eye(8, dtype=F32)
    return jnp.einsum('ij,zbjph->zbihjp', e, b.reshape(2, 2, 8, S5_P, S5_H)).reshape(2, 2, 128, SW)


def _s5_out_blocks(c):
    e = jnp.eye(8, dtype=F32)
    return jnp.einsum('ij,zbjhp->zbjpih', e, c.reshape(2, 2, 8, S5_H, S5_P)).reshape(2, 2, SW, 128)


def _gate_weight(w_a):
    z = jnp.zeros((16, 128), F32)
    top = jnp.concatenate([w_a[0], z], axis=1)
    bot = jnp.concatenate([z, w_a[1]], axis=1)
    return jnp.concatenate([top, bot, jnp.zeros((96, 256), F32)], axis=0)


def _layer_prep(p):
    lr, li, br, bi = _s5_discretize(p["s5_a_re"], p["s5_a_im"], p["s5_log_step"], p["s5_b_re"], p["s5_b_im"])
    q = dict(p)
    q["bre"] = _s5_in_blocks(br).astype(MX)
    q["bim"] = _s5_in_blocks(bi).astype(MX)
    q["cre"] = _s5_out_blocks(p["s5_c_re"]).astype(MX)
    q["cim"] = _s5_out_blocks(p["s5_c_im"]).astype(MX)
    mr, mi = lr.reshape(2, 1024), li.reshape(2, 1024)
    q["tab"], q["tabc"] = _lockstep_tables(mr, mi)
    q["dsk"] = p["s5_d"].reshape(1, 256)
    q["wa"] = _gate_weight(p["gla_w_a"]).astype(MX)
    q["ba"] = p["gla_b_a"].reshape(1, 256)
    q["lng"] = p["gla_ln_g"].reshape(1, 256)
    q["bv"] = p["s5_b_glu"][:256].reshape(1, 256)
    q["bg"] = p["s5_b_glu"][256:].reshape(1, 256)
    for k in ("ln1_g", "ln1_b", "ln2_g", "ln2_b"):
        q[k] = p[k].reshape(1, D)
    return q


def _layer_fwd(x, q, tk, fetch, target=None):
    q["w_in"] = fetch("w_in", x)
    h, la2 = _inproj_fwd(x, q["w_in"], q["wa"], q["ba"])
    hre, him, y2 = _s5_fwd(h, q["bre"], q["bim"], q["cre"], q["cim"], q["tab"])
    q["w4"] = fetch("s5_w_glu", y2)
    ya = _s5_glu_fwd(y2, h, q["dsk"], q["w4"], q["bv"], q["bg"])
    of, ob, sf, sb = _gla_fwd(h, la2)
    yb = _gla_post_fwd(of, ob, h, q["lng"])
    yc = _swa_fwd(h, tk, q["swa_sink"])
    mixed = ya[:8, :128] + yb[:8, :128] + yc[:8, :128]
    q["w_out"] = fetch("w_out", mixed)
    q["w_ff1"] = fetch("w_ff1", mixed)
    q["w_ff2"] = fetch("w_ff2", mixed)
    s1, x1, a, s2, *out = _mix_ffn_fwd(ya, yb, yc, x, q["w_out"], q["ln1_g"], q["ln1_b"], q["w_ff1"], q["w_ff2"],
                                       q["ln2_g"], q["ln2_b"], target)
    saved = dict(x=x, h=h, hre=hre, him=him, y2=y2, ya=ya, la2=la2, of=of, ob=ob, sf=sf, sb=sb, yb=yb, yc=yc,
                 s1=s1, x1=x1, a=a, s2=s2)
    return (out[0] if target is None else tuple(out)), saved


def _layer_bwd(dy, q, sv, tk, emit):
    g = {}
    da, ds2, dx1, g["dg2"], g["db2"] = _ffn_bwd_act(dy, sv["s2"], sv["a"], q["w_ff1"], q["w_ff2"], q["ln2_g"])
    dw1, dw2 = _ffn_bwd_w(sv["x1"], da, sv["a"], ds2)
    tie = emit(dict(w_ff1=dw1, w_ff2=dw2))
    dya, dyb, dyc, dxp, dwo, g["dg1"], g["db1"] = _outproj_bwd(dx1, sv["s1"], sv["ya"], sv["yb"], sv["yc"],
                                                               q["w_out"], q["ln1_g"] + tie)
    h = sv["h"]
    daq, dakv, g["dsink"] = _swa_bwd(h, tk, q["swa_sink"], dyc)
    do, gr, g["dlng"] = _gla_post_bwd(sv["of"], sv["ob"], h, q["lng"], dyb)
    gq_f, gk_f, gv_f, gl_f, gq_b, gk_b, gv_b, gl_b = _gla_bwd(h, sv["la2"], do, sv["sf"], sv["sb"])
    dhl, g["dwa"], g["dba"] = _gla_gate_bwd(h, q["wa"], q["ba"], gl_f, gl_b)
    dyp, dud, g["dd"], dw4, g["dbv"], g["dbg"] = _s5_glu_bwd(sv["y2"], h, q["dsk"], q["w4"], q["bv"], q["bg"], dya)
    tie = emit(dict(w_out=dwo.reshape(NSHARD, D // NSHARD, D), s5_w_glu=dw4))
    du2, g["dbre"], g["dbim"], g["dcre"], g["dcim"], g["dmu"] = _s5_bwd(
        h, dyp, sv["hre"], sv["him"], q["bre"], q["bim"], q["cre"], q["cim"], (q["tabc"][0], q["tabc"][1] + tie))
    dx, dwt = _inproj_bwd(sv["x"], q["w_in"], dxp, du2, dud, gq_f, gq_b, gk_f, gk_b, gv_f, gv_b, gr, daq, dakv, dhl)
    tie = emit(dict(w_in=dwt))
    return dx, g, tie


NATIVE = ("dmu", "dbre", "dbim", "dcre", "dcim", "dd", "dbv", "dbg", "dwa", "dba", "dlng", "dsink",
          "dg1", "db1", "dg2", "db2", "loss")
ICI_CORE = (0, 0, 0, 1, 1, 0, 0, 0, 1, 1, 1, 1, 0, 0, 1, 1, 0)


def _finish_small(n, w):
    g = {}
    dmu = n["dmu"]
    dlr = dmu[:, :, :, 0].reshape(DEPTH, 2, S5_G, S5_P)
    dli = dmu[:, :, :, 1].reshape(DEPTH, 2, S5_G, S5_P)

    def unblock(c, perm, shape):
        return c.reshape(DEPTH, 2, 2, S5_H, 8, S5_P).transpose(perm).reshape(shape)

    b_shape, c_shape = (DEPTH, 2, S5_G, S5_P, S5_H), (DEPTH, 2, S5_G, S5_H, S5_P)
    _, vjp = jax.vjp(_s5_discretize, w["s5_a_re"], w["s5_a_im"], w["s5_log_step"], w["s5_b_re"], w["s5_b_im"])
    (g["s5_a_re"], g["s5_a_im"], g["s5_log_step"], g["s5_b_re"], g["s5_b_im"]) = vjp(
        (dlr, dli, unblock(n["dbre"], (0, 1, 2, 4, 5, 3), b_shape), unblock(n["dbim"], (0, 1, 2, 4, 5, 3), b_shape)))
    g["s5_c_re"] = unblock(n["dcre"], (0, 1, 2, 4, 3, 5), c_shape)
    g["s5_c_im"] = unblock(n["dcim"], (0, 1, 2, 4, 3, 5), c_shape)
    g["s5_d"] = n["dd"].reshape(DEPTH, S5_G, S5_H)
    g["s5_b_glu"] = jnp.concatenate([n["dbv"], n["dbg"]], axis=2).reshape(DEPTH, 512)
    g["gla_w_a"] = jnp.stack([n["dwa"][:, 0:16, 0:128], n["dwa"][:, 16:32, 128:256]], axis=1)
    g["gla_b_a"] = n["dba"].reshape(DEPTH, 2, 128)
    g["gla_ln_g"] = n["dlng"].reshape(DEPTH, 256)
    g["swa_sink"] = n["dsink"][:, :, 0]
    for k, s in (("ln1_g", "dg1"), ("ln1_b", "db1"), ("ln2_g", "dg2"), ("ln2_b", "db2")):
        g[k] = n[s].reshape(DEPTH, D)
    return g


def _local_step(x, target, qs, tk, fetch, emit):
    saved = []
    for l, q in enumerate(qs):
        x, sv = _layer_fwd(x, q, tk, functools.partial(fetch, l), target if l == DEPTH - 1 else None)
        saved.append(sv)
    dy, lacc = x
    smalls = [None] * DEPTH
    tie = 0.0
    for l in reversed(range(DEPTH)):
        qs[l]["ln2_g"] = qs[l]["ln2_g"] + tie
        dy, smalls[l], tie = _layer_bwd(dy, qs[l], saved[l], tk, functools.partial(emit, l))
    smalls[0]["db2"] = smalls[0]["db2"] + tie
    for l in range(DEPTH):
        smalls[l]["loss"] = lacc if l == 0 else jnp.zeros_like(lacc)
    return lacc[0, 0], dy, smalls


BIG = ("w_in", "s5_w_glu", "w_out", "w_ff1", "w_ff2")
SMALL = ("s5_a_re", "s5_a_im", "s5_log_step", "s5_b_re", "s5_b_im", "s5_c_re", "s5_c_im", "s5_d", "s5_b_glu",
         "gla_w_a", "gla_b_a", "gla_ln_g", "swa_sink", "ln1_g", "ln1_b", "ln2_g", "ln2_b")
ANY = pl.BlockSpec(memory_space=pl.ANY)


def _place():
    x, y, c = lax.axis_index("x"), lax.axis_index("y"), lax.axis_index("c")
    return x, y, c, [(1 - x, y), (x, 1 - y), (1 - x, 1 - y)]


HBM = pl.BlockSpec(memory_space=pltpu.HBM)
SEMS = pl.BlockSpec(memory_space=pltpu.SEMAPHORE)
EFFECT = pltpu.SideEffectType.DATAFLOW_SIDE_EFFECTING


def _push_copies(ins, lands, send, recv, gather, sending):
    x, y, c, chips = _place()
    me = 2 * x + y
    if gather == "sibling":
        return [pltpu.make_async_remote_copy(src_ref=ins[a], dst_ref=lands[a], send_sem=send.at[a], recv_sem=recv.at[a],
                                             device_id=(x, y, 1 - c), device_id_type=MESH) for a in range(len(lands))]
    out = []
    for a in range(len(lands)):
        for j, (px, py) in enumerate(chips):
            peer = 2 * px + py
            src = lands[a].at[me] if gather else ins[a].at[peer if sending else me]
            dst = lands[a].at[me if sending else peer]
            out.append(pltpu.make_async_remote_copy(src_ref=src, dst_ref=dst, send_sem=send.at[3 * a + j],
                                                    recv_sem=recv.at[3 * a + j], device_id=(px, py, c),
                                                    device_id_type=MESH))
    return out


def _push_start(name, arrs, gather):
    n = len(arrs)
    ops = list(arrs) if gather is True else list(arrs) + [lax.empty(s.shape, s.dtype) for s in arrs]
    m = len(ops)

    def body(*refs):
        ins, lnd = (refs[:n], refs[:n]) if gather is True else (refs[:n], refs[n:m])
        for cp in _push_copies(ins, lnd, refs[m], refs[m + 1], gather, True):
            cp.start()
        refs[-1][...] = jnp.zeros((8, 128), F32)

    ops = [pltpu.with_memory_space_constraint(t, pltpu.HBM) for t in ops]
    res = pl.pallas_call(
        body, name=name,
        out_shape=(pltpu.SemaphoreType.DMA((3 * n,)), pltpu.SemaphoreType.DMA((3 * n,)),
                   *[pltpu.HBM(t.shape, t.dtype) for t in ops], _sds((8, 128))),
        in_specs=[HBM] * m,
        out_specs=(SEMS, SEMS, *[HBM] * m, pl.BlockSpec(memory_space=pltpu.VMEM)),
        input_output_aliases={i: 2 + i for i in range(m)},
        compiler_params=pltpu.CompilerParams(has_side_effects=EFFECT))(*ops)
    return res[0], res[1], list(res[2:2 + m]), res[-1]


def _push_wait(name, started, after, gather):
    send, recv, ops, _ = started
    m = len(ops)
    n = m if gather is True else m // 2

    def body(*refs):
        ins, lnd = (refs[:n], refs[:n]) if gather is True else (refs[:n], refs[n:m])
        for cp in _push_copies(ins, lnd, refs[m], refs[m + 1], gather, False):
            cp.wait_send()
            cp.wait_recv()

    res = pl.pallas_call(
        body, name=name,
        out_shape=[pltpu.HBM(t.shape, t.dtype) for t in ops],
        in_specs=[HBM] * m + [SEMS, SEMS, ANY], out_specs=[HBM] * m,
        input_output_aliases={i: i for i in range(m)},
        compiler_params=pltpu.CompilerParams(has_side_effects=EFFECT))(*ops, send, recv, after)
    return list(res)


def _row_tile(rows):
    return max(t for t in range(8, min(rows, 512) + 1, 8) if rows % t == 0)


def _cast_to_slot(me, w, l):
    _, rows, cols = w.shape
    tr = _row_tile(rows)

    def body(me_ref, w_ref, o_ref):
        o_ref[0] = w_ref[0].astype(MX)

    return pl.pallas_call(
        body,
        grid_spec=pltpu.PrefetchScalarGridSpec(
            num_scalar_prefetch=1, grid=(rows // tr,),
            in_specs=[pl.BlockSpec((1, tr, cols), lambda i, me_: (l, i, 0))],
            out_specs=pl.BlockSpec((1, tr, cols), lambda i, me_: (me_[0], i, 0))),
        out_shape=_sds((NSHARD, rows, cols), MX), name="cast_to_slot", compiler_params=_cp(("arbitrary",)))(me, w)


def _sum_sources(me, recv, own):
    _, rows, cols = recv[0].shape
    tr = min(_row_tile(rows), 256) if rows % 256 == 0 else _row_tile(rows)
    nt = rows // tr

    def body(me_ref, *refs):
        o_ref = refs[-1]
        for l in range(DEPTH):
            @pl.when(pl.program_id(0) == l)
            def _():
                r_ref, own_ref = refs[2 * l], refs[2 * l + 1]
                part = [jnp.where(me_ref[0] == s, own_ref[0], r_ref[s]).astype(F32) for s in range(NSHARD)]
                o_ref[...] = ((part[0] + part[1]) + part[2]) + part[3]

    in_specs = []
    for l in range(DEPTH):
        pick = lambda g, i, me_, l=l: jnp.where(g == l, i, jnp.where(g < l, 0, nt - 1))
        in_specs += [pl.BlockSpec((NSHARD, tr, cols), lambda g, i, me_, pick=pick: (0, pick(g, i, me_), 0)),
                     pl.BlockSpec((1, tr, cols), lambda g, i, me_, pick=pick: (me_[0], pick(g, i, me_), 0))]
    return pl.pallas_call(
        body,
        grid_spec=pltpu.PrefetchScalarGridSpec(
            num_scalar_prefetch=1, grid=(DEPTH, nt), in_specs=in_specs,
            out_specs=pl.BlockSpec((tr, cols), lambda g, i, me_: (g * nt + i, 0))),
        out_shape=_sds((DEPTH * rows, cols)), name="sum_sources",
        compiler_params=_cp(("arbitrary", "arbitrary")))(me, *[t for l in range(DEPTH) for t in (recv[l], own[l])])


def _allreduce_small(per_layer):
    nk = len(per_layer[0])
    n = DEPTH * nk
    shapes = [a.shape for a in per_layer[0]]

    def body(*refs):
        ins, outs = refs[:n], refs[n:n + nk]
        sibs, slots = refs[n + nk:n + 2 * nk], refs[n + 2 * nk:n + 3 * nk]
        send, recv = refs[n + 3 * nk:]
        x, y, c, chips = _place()
        me = 2 * x + y
        d2d = [pltpu.make_async_remote_copy(src_ref=ins[l * nk + k], dst_ref=sibs[k].at[l], send_sem=send.at[l * nk + k],
                                            recv_sem=recv.at[l * nk + k], device_id=(x, y, 1 - c), device_id_type=MESH)
               for l in range(DEPTH) for k in range(nk)]
        for cp in d2d:
            cp.start()
        for cp in d2d:
            cp.wait()
        for l in range(DEPTH):
            for k in range(nk):
                slots[k][0, l] = ins[l * nk + k][...] + sibs[k][l]

        def swap(k, stage):
            peer = (1 - x, y, c) if stage == 0 else (x, 1 - y, c)
            return pltpu.make_async_remote_copy(src_ref=slots[k].at[2 * stage], dst_ref=slots[k].at[2 * stage + 1],
                                                send_sem=send.at[n + 3 * k + stage], recv_sem=recv.at[n + 3 * k + stage],
                                                device_id=peer, device_id_type=MESH)

        def handover(k):
            return pltpu.make_async_remote_copy(src_ref=outs[k], dst_ref=outs[k], send_sem=send.at[n + 3 * nk + k],
                                                recv_sem=recv.at[n + 3 * nk + k], device_id=(x, y, 1 - c),
                                                device_id_type=MESH)

        halves = (tuple(k for k in range(nk) if ICI_CORE[k] == 0), tuple(k for k in range(nk) if ICI_CORE[k] == 1))
        for cc in range(2):
            @pl.when(c == cc)
            def _():
                mine, theirs = halves[cc], halves[1 - cc]
                for stage in range(2):
                    cps = [swap(k, stage) for k in mine]
                    for cp in cps:
                        cp.start()
                    for cp in cps:
                        cp.wait()
                    for k in mine:
                        if stage == 0:
                            slots[k][2] = slots[k][0] + slots[k][1]
                        else:
                            outs[k][...] = slots[k][2] + slots[k][3]
                over = [handover(k) for k in mine]
                for cp in over:
                    cp.start()
                for k in theirs:
                    handover(k).wait_recv()
                for cp in over:
                    cp.wait_send()

    vm = pl.BlockSpec(memory_space=pltpu.VMEM)
    return pl.pallas_call(
        body, in_specs=[vm] * n, out_specs=[vm] * nk, out_shape=[_sds((DEPTH,) + s) for s in shapes],
        scratch_shapes=([pltpu.VMEM((DEPTH,) + s, F32) for s in shapes]
                        + [pltpu.VMEM((NSHARD, DEPTH) + s, F32) for s in shapes]
                        + [pltpu.SemaphoreType.DMA((n + 4 * nk,)), pltpu.SemaphoreType.DMA((n + 4 * nk,))]),
        name="allreduce_small", compiler_params=pltpu.CompilerParams(vmem_limit_bytes=VMEM_LIMIT))(
            *[a for layer in per_layer for a in layer])


def _adamw_math(w, g, m, v):
    m = ADAM_B1 * m + (1.0 - ADAM_B1) * g
    v = ADAM_B2 * v + (1.0 - ADAM_B2) * jnp.square(g)
    m_hat = m / (1.0 - ADAM_B1 ** ADAM_STEP)
    v_hat = v / (1.0 - ADAM_B2 ** ADAM_STEP)
    delta = -ADAM_LR * (m_hat / (jnp.sqrt(v_hat) + ADAM_EPS) + ADAM_WD * w)
    return delta, m, v


def _adamw(g_parts, w, m, v):
    rows, cols = w.shape
    tr = 256 if rows % 256 == 0 else _row_tile(rows)
    k = len(g_parts)

    def body(*refs):
        g = refs[0][...]
        for r in refs[1:k]:
            g = g + r[...]
        w_ref, m_ref, v_ref, go, do, mo, vo = refs[k:]
        d, mn, vn = _adamw_math(w_ref[...], g, m_ref[...], v_ref[...])
        go[...] = g
        do[...] = d
        mo[...] = mn
        vo[...] = vn

    spec = pl.BlockSpec((tr, cols), lambda i: (i, 0))
    return pl.pallas_call(
        body, grid=(rows // tr,), in_specs=[spec] * (k + 3), out_specs=[spec] * 4,
        out_shape=[_sds((rows, cols))] * 4, name="adamw", compiler_params=_cp(("parallel",)))(*g_parts, w, m, v)


def _adamw_small(gs, ws, ms, vs):
    n = len(gs)

    def body(*refs):
        for k in range(n):
            d, mn, vn = _adamw_math(refs[n + k][...], refs[k][...], refs[2 * n + k][...], refs[3 * n + k][...])
            refs[4 * n + k][...] = d
            refs[5 * n + k][...] = mn
            refs[6 * n + k][...] = vn

    vm = pl.BlockSpec(memory_space=pltpu.VMEM)
    shapes = [_sds(a.shape) for a in ws]
    res = pl.pallas_call(
        body, in_specs=[vm] * (4 * n), out_specs=[vm] * (3 * n), out_shape=shapes * 3, name="adamw_small",
        compiler_params=pltpu.CompilerParams(vmem_limit_bytes=VMEM_LIMIT))(*gs, *ws, *ms, *vs)
    return res[:n], res[n:2 * n], res[2 * n:]


_ARGS = ("x", "w_in", "s5_a_re", "s5_a_im", "s5_log_step", "s5_b_re", "s5_b_im", "s5_c_re", "s5_c_im", "s5_d",
         "s5_w_glu", "s5_b_glu", "gla_w_a", "gla_b_a", "gla_ln_g", "swa_sink", "w_out", "ln1_g", "ln1_b", "w_ff1",
         "w_ff2", "ln2_g", "ln2_b")
_WEIGHTS = _ARGS[1:]


def kernel(x, w_in, s5_a_re, s5_a_im, s5_log_step, s5_b_re, s5_b_im, s5_c_re, s5_c_im, s5_d, s5_w_glu, s5_b_glu, gla_w_a, gla_b_a, gla_ln_g, swa_sink, w_out, ln1_g, ln1_b, w_ff1, w_ff2, ln2_g, ln2_b, loss_target, m_w_in, m_s5_a_re, m_s5_a_im, m_s5_log_step, m_s5_b_re, m_s5_b_im, m_s5_c_re, m_s5_c_im, m_s5_d, m_s5_w_glu, m_s5_b_glu, m_gla_w_a, m_gla_b_a, m_gla_ln_g, m_swa_sink, m_w_out, m_ln1_g, m_ln1_b, m_w_ff1, m_w_ff2, m_ln2_g, m_ln2_b, v_w_in, v_s5_a_re, v_s5_a_im, v_s5_log_step, v_s5_b_re, v_s5_b_im, v_s5_c_re, v_s5_c_im, v_s5_d, v_s5_w_glu, v_s5_b_glu, v_gla_w_a, v_gla_b_a, v_gla_ln_g, v_swa_sink, v_w_out, v_ln1_g, v_ln1_b, v_w_ff1, v_w_ff2, v_ln2_g, v_ln2_b):
    given = dict(locals())
    w = {k: given[k] for k in _WEIGHTS}
    mom = {k: given["m_" + k] for k in _WEIGHTS}
    var = {k: given["v_" + k] for k in _WEIGHTS}

    me = (2 * lax.axis_index("x") + lax.axis_index("y")).astype(jnp.int32).reshape(1)
    tr = lambda t: t.transpose(0, 2, 1)
    shard = {k: (tr(w[k]) if k == "w_in" else w[k]) for k in BIG}
    qs = [None] * DEPTH

    first = ("w_in", "s5_w_glu", "w_out")
    follow = {(0, "w_in"): [(0, BIG[3:])], (0, "s5_w_glu"): [(1, first)], (0, "w_ff1"): [(1, BIG[3:])]}
    gathers = {}

    casts = {}

    def start_gather(l, names, behind=None):
        lands = [casts.pop((l, k)) if (l, k) in casts else _cast_to_slot(me, shard[k], l) for k in names]
        if behind is not None:
            lands, behind = lax.optimization_barrier((lands, behind))
        st = _push_start(f"gather_start_{l}_{names[0]}", lands, True)
        for k in names:
            gathers[l, k] = [names, st, None]
        return st[-1], behind

    token = start_gather(0, first[:1])[0] + start_gather(0, first[1:])[0]
    zero = token[0, 0]
    for l in range(DEPTH):
        for k in BIG:
            if (l, k) not in gathers:
                casts[l, k] = _cast_to_slot(me, lax.optimization_barrier((shard[k], token))[0], l)
        qs[l] = _layer_prep({k: (w[k][l] + zero if k == "s5_a_re" else w[k][l]) for k in SMALL})
    token, casts, qs = lax.optimization_barrier((token, casts, qs))

    def fetch(l, name, after):
        names, st, got = gathers[l, name]
        tie = None
        if got is None:
            if l == 0 and name == "w_in":
                after = token
            lands = _push_wait(f"gather_wait_{l}_{names[0]}", st, after, True)
            for l2, names2 in follow.get((l, name), ()):
                tok, lands[0] = start_gather(l2, names2, lands[0])
                tie = tok if tie is None else tie + tok
            got = dict(zip(names, lands))
            for k in names:
                gathers[l, k][2] = got
        full = got[name]
        if name == "w_in":
            return _in_rows(full, token if tie is None else tie)
        if tie is not None:
            near = "bv" if name == "s5_w_glu" else "ln2_b"
            qs[l][near] = qs[l][near] + tie[0, 0]
        return full.reshape(D, D) if name == "w_out" else full

    scatters, held = [], {}

    def emit(l, grads):
        if l > 0:
            held.update(grads)
            if "w_in" not in grads:
                return 0.0
            grads = dict(held)
            held.clear()
        names = tuple(grads)
        st = _push_start(f"scatter_start_{l}_{names[0]}", [grads[k] for k in names], False)
        scatters.append((l, names, st))
        return st[-1][0, 0]

    loss, dx, smalls = _local_step(x.reshape(N, D), loss_target.reshape(N, D), qs, _rope_tables(128), fetch, emit)

    out, recv, own = {}, {}, {}

    def collect(keys, after):
        for l, names, st in scatters:
            if names[0] in keys:
                ops = _push_wait(f"scatter_wait_{l}_{names[0]}", st, after, False)
                for i, k in enumerate(names):
                    own[l, k], recv[l, k] = ops[i], ops[len(names) + i]

    def to_sibling(keys):
        sums = [_sum_sources(me, [recv[l, k] for l in range(DEPTH)], [own[l, k] for l in range(DEPTH)]) for k in keys]
        return _push_start(f"swap_start_{keys[0]}", sums, "sibling")

    def apply(keys, started, after):
        ops = _push_wait(f"swap_wait_{keys[0]}", started, after, "sibling")
        for i, k in enumerate(keys):
            mine, other = ops[i], ops[len(keys) + i]
            shp = shard[k].shape
            r = _adamw([mine, other], *((tr(t[k]) if k == "w_in" else t[k]).reshape(-1, shp[-1]) for t in (w, mom, var)))
            r = [t.reshape(shp) for t in r]
            out[k] = [tr(t) for t in r] if k == "w_in" else r
        return out[keys[-1]][1]

    collect(("w_ff1", "w_ff2", "w_out", "s5_w_glu"), dx)
    ff = to_sibling(("w_ff1", "w_ff2"))
    mix = to_sibling(("w_out", "s5_w_glu"))
    smalls[0]["db1"] = smalls[0]["db1"] + (ff[-1][0, 0] + mix[-1][0, 0])
    native = _allreduce_small([[smalls[l][k] for k in NATIVE] for l in range(DEPTH)])
    native = dict(zip(NATIVE, native))
    loss = native["loss"][0, 0, 0] + native["loss"][1, 0, 0]
    gsmall = _finish_small(native, w)
    view = lambda k, t: t.transpose(0, 1, 2, 4, 3) if k in ("s5_b_re", "s5_b_im") else t
    res = _adamw_small(*([view(k, t[k]) for k in SMALL] for t in (gsmall, w, mom, var)))
    for i, k in enumerate(SMALL):
        out[k] = [gsmall[k]] + [view(k, r[i]) for r in res]
    last = apply(("w_ff1", "w_ff2"), ff, res[0][-1])
    collect(("w_in",), last)
    win = to_sibling(("w_in",))
    last = apply(("w_out", "s5_w_glu"), mix, win[-1])
    apply(("w_in",), win, last)

    return (loss, dx.reshape(NSEQ, L, D), *[out[k][0] for k in _WEIGHTS], *[out[k][1] for k in _WEIGHTS],
            *[out[k][2] for k in _WEIGHTS], *[out[k][3] for k in _WEIGHTS])
```

```python
import functools
import math

import jax
import jax.numpy as jnp
from jax import lax
from jax.experimental import pallas as pl
from jax.experimental.pallas import tpu as pltpu

F32 = jnp.float32
MX = jnp.bfloat16
MESH = pl.DeviceIdType.MESH

DEPTH = 2
NSEQ = 2
L = 2048
N = NSEQ * L
D = 1024
DFF = 4096
NSHARD = 4
S5_G, S5_H, S5_P = 16, 16, 64
GLA_CHUNK = 64
NCHUNK = L // GLA_CHUNK
GLA_GROUP = 4
NGROUP = NCHUNK // GLA_GROUP
SWA_BLK = 128
NBLK = L // SWA_BLK
SWA_PER = 2
ROT = 16
ROPE_THETA = 500000.0
LN_EPS = 1e-5
ALPHA = (2 * DEPTH) ** 0.25
NEG_BIG = -1e30
DIN = 1824
DINP = 1920
ADAM_LR, ADAM_B1, ADAM_B2, ADAM_EPS, ADAM_WD, ADAM_STEP = 0.001, 0.9, 0.999, 1e-08, 0.01, 10
VMEM_LIMIT = 56 * 1024 * 1024
TT = 512
SW = 512
FFN_TM = 512
FFN_TM_W = 1024
FFN_WB = 1
FFN_VMEM = 60 * 1024 * 1024
INPROJ_BWD_TM = 512


def _cp(sem, vmem=VMEM_LIMIT):
    return pltpu.CompilerParams(dimension_semantics=sem, vmem_limit_bytes=vmem)


def _mm(a, b):
    return jnp.dot(a.astype(MX), b.astype(MX), preferred_element_type=F32)


def _mm_nt(a, b):
    return lax.dot_general(a.astype(MX), b.astype(MX), (((1,), (1,)), ((), ())), preferred_element_type=F32)


def _mm_tn(a, b):
    return lax.dot_general(a.astype(MX), b.astype(MX), (((0,), (0,)), ((), ())), preferred_element_type=F32)


@jax.custom_vjp
def _dmm(a, b):
    return _mm(a, b)


_dmm.defvjp(lambda a, b: (_mm(a, b), (a, b)), lambda r, g: (_mm_nt(g, r[1]), _mm_tn(r[0], g)))


@jax.custom_vjp
def _dmm_nt(a, b):
    return _mm_nt(a, b)


_dmm_nt.defvjp(lambda a, b: (_mm_nt(a, b), (a, b)), lambda r, g: (_mm(g, r[1]), _mm_tn(g, r[0])))


@jax.custom_vjp
def _dmm_tn(a, b):
    return _mm_tn(a, b)


_dmm_tn.defvjp(lambda a, b: (_mm_tn(a, b), (a, b)), lambda r, g: (_mm_nt(r[1], g), _mm(r[0], g)))


def _split3(x):
    hi = x.astype(MX)
    r1 = x - hi.astype(F32)
    mid = r1.astype(MX)
    lo = (r1 - mid.astype(F32)).astype(MX)
    return hi, mid, lo


def _chunk_pairs(rows, rev, strict):
    r = lax.broadcasted_iota(jnp.int32, (rows, rows), 0)
    c = lax.broadcasted_iota(jnp.int32, (rows, rows), 1)
    order = ((c > r) if strict else (c >= r)) if rev else ((c < r) if strict else (c <= r))
    return (r // GLA_CHUNK == c // GLA_CHUNK) & order


def _cums_impl(x, rev):
    rows, w = x.shape
    t = jnp.where(_chunk_pairs(rows, rev, False), 1.0, 0.0).astype(MX)
    s = jnp.dot(t, jnp.concatenate(_split3(x), axis=1), preferred_element_type=F32)
    return s[:, 0:w] + s[:, w:2 * w] + s[:, 2 * w:3 * w]


@functools.partial(jax.custom_vjp, nondiff_argnums=(1,))
def _cums(x, rev):
    return _cums_impl(x, rev)


_cums.defvjp(lambda x, rev: (_cums_impl(x, rev), None), lambda rev, r, g: (_cums_impl(g, not rev),))


def _ln_fwd(s, g, b):
    mu = jnp.mean(s, axis=-1, keepdims=True)
    xc = s - mu
    var = jnp.mean(xc * xc, axis=-1, keepdims=True)
    return xc * lax.rsqrt(var + LN_EPS) * g + b


def _ln_bwd(dy, s, g):
    mu = jnp.mean(s, axis=-1, keepdims=True)
    xc = s - mu
    var = jnp.mean(xc * xc, axis=-1, keepdims=True)
    rstd = lax.rsqrt(var + LN_EPS)
    xhat = xc * rstd
    dxh = dy * g
    ds = rstd * (dxh - jnp.mean(dxh, axis=-1, keepdims=True) - xhat * jnp.mean(dxh * xhat, axis=-1, keepdims=True))
    return ds, jnp.sum(dy * xhat, axis=0, keepdims=True), jnp.sum(dy, axis=0, keepdims=True)


def _sds(shape, dtype=F32):
    return jax.ShapeDtypeStruct(shape, dtype)


_IN_ROW_PIECES = (((0, 0), (0, 456)), ((1, 0), (456, 456)), ((2, 0), (912, 112)), ((2, 112), (1792, 32)),
                  ((2, 144), (1024, 312)), ((3, 0), (1336, 456)))


def _in_rows(g4, behind):
    def body(g_ref, behind_ref, o_ref, tmp):
        tmp[DIN:DINP] = jnp.zeros((DINP - DIN, D), F32)
        for (j, s0), (d0, n_) in _IN_ROW_PIECES:
            tmp[d0:d0 + n_] = g_ref[j, s0:s0 + n_].astype(F32)
        o_ref[...] = tmp[...].astype(MX)

    vm = pl.BlockSpec(memory_space=pltpu.VMEM)
    return pl.pallas_call(body, in_specs=[vm, pl.BlockSpec(memory_space=pl.ANY)], out_specs=vm,
                          out_shape=_sds((DINP, D), MX), scratch_shapes=[pltpu.VMEM((DINP, D), F32)], name="in_rows",
                          compiler_params=pltpu.CompilerParams(vmem_limit_bytes=VMEM_LIMIT))(g4, behind)


def _inproj_fwd(x, wt, wa, ba):
    tm = 512

    def body(x_ref, w_ref, wa_ref, ba_ref, h_ref, la_ref):
        h = _mm_nt(x_ref[...], w_ref[...])
        h_ref[...] = h
        la_ref[...] = _logsig(_mm(h[:, DINP - 128:], wa_ref[...]) + ba_ref[...]) * (1.0 / 16.0)

    return pl.pallas_call(
        body, grid=(N // tm,),
        in_specs=[pl.BlockSpec((tm, D), lambda i: (i, 0)), pl.BlockSpec((DINP, D), lambda i: (0, 0)),
                  pl.BlockSpec((128, 256), lambda i: (0, 0)), pl.BlockSpec((1, 256), lambda i: (0, 0))],
        out_specs=[pl.BlockSpec((tm, DINP), lambda i: (i, 0)), pl.BlockSpec((tm, 256), lambda i: (i, 0))],
        out_shape=[_sds((N, DINP)), _sds((N, 256))], name="inproj_fwd", compiler_params=_cp(("parallel",)))(x, wt, wa, ba)


def _inproj_bwd(x, w, dxp, du2, dud, gq_f, gq_b, gk_f, gk_b, gv_f, gv_b, gr, daq, dakv, dhl):
    tm = INPROJ_BWD_TM
    nt = N // tm

    def body(x_ref, w_ref, dxp_ref, du2_ref, dud_ref, gqf, gqb, gkf, gkb, gvf, gvb, gr_ref, daq_ref, dakv_ref, dhl_ref,
             dx_ref, dw_ref, acc):
        i = pl.program_id(0)
        f = lambda r: r[...].astype(F32)
        dh = jnp.concatenate([
            du2_ref[0] + du2_ref[1] + f(dud_ref), f(gqf) + f(gqb), f(gkf) + f(gkb), f(gvf) + f(gvb),
            f(gr_ref), f(daq_ref), f(dakv_ref), f(dhl_ref)], axis=1)
        dx_ref[...] = dxp_ref[...] + _mm(dh, w_ref[...])
        contrib = _mm_tn(dh, x_ref[...])

        @pl.when(i == 0)
        def _():
            acc[...] = contrib

        @pl.when(i > 0)
        def _():
            acc[...] += contrib

        @pl.when(i == nt - 1)
        def _():
            for (j, d0), (s0, n_) in _IN_ROW_PIECES:
                dw_ref[j, d0:d0 + n_] = acc[s0:s0 + n_].astype(MX)

    row = lambda w_: pl.BlockSpec((tm, w_), lambda i: (i, 0))
    return pl.pallas_call(
        body, grid=(nt,),
        in_specs=[row(D), pl.BlockSpec((DINP, D), lambda i: (0, 0)), row(D),
                  pl.BlockSpec((2, tm, 256), lambda i: (0, i, 0)), row(256), row(128), row(128), row(128), row(128),
                  row(256), row(256), row(256), row(512), row(256), row(128)],
        out_specs=[row(D), pl.BlockSpec((NSHARD, DIN // NSHARD, D), lambda i: (0, 0, 0))],
        out_shape=[_sds((N, D)), _sds((NSHARD, DIN // NSHARD, D), MX)],
        scratch_shapes=[pltpu.VMEM((DINP, D), F32)],
        name="inproj_bwd", compiler_params=_cp(("arbitrary",)))(
            x, w, dxp, du2, dud, gq_f, gq_b, gk_f, gk_b, gv_f, gv_b, gr, daq, dakv, dhl)


def _tile_scan(xr, xi, a, cr, ci, reverse):
    for lvl, d in enumerate((1, 2, 4)):
        sh = 8 - d if reverse else d
        sr = pltpu.roll(xr, sh, 0)
        si = pltpu.roll(xi, sh, 0)
        ar, ai = a[2 * lvl], a[2 * lvl + 1]
        xr, xi = xr + ar * sr - ai * si, xi + ar * si + ai * sr
    pr, pi = a[6], a[7]
    return xr + pr * cr - pi * ci, xi + pr * ci + pi * cr


NJ = TT // 8


def _lockstep_tables(mr, mi):
    def body(mr_ref, mi_ref, a_ref, p_ref, ac_ref, pc_ref):
        rowid = lax.broadcasted_iota(jnp.int32, (8, 2 * SW), 0)

        def mul(a, b):
            return a[0] * b[0] - a[1] * b[1], a[0] * b[1] + a[1] * b[0]

        for z in range(2):
            for sign, reverse, a_out, p_out in ((1.0, z == 1, a_ref, p_ref), (-1.0, z == 0, ac_ref, pc_ref)):
                m = (mr_ref[z:z + 1, :], sign * mi_ref[z:z + 1, :])
                pw = [m]
                for _ in range(NJ - 1):
                    pw.append(mul(pw[-1], m))
                n = pw[-1]
                link = [n]
                for _ in range(7):
                    link.append(mul(link[-1], n))
                tiles = [jnp.broadcast_to(m[0], (8, 2 * SW)), jnp.broadcast_to(m[1], (8, 2 * SW))]
                for d in (1, 2, 4):
                    keep = (rowid <= 7 - d) if reverse else (rowid >= d)
                    tiles += [jnp.where(keep, link[d - 1][c], 0.0) for c in range(2)]
                for c in range(2):
                    t = jnp.zeros((8, 2 * SW), F32)
                    for i in range(8):
                        t = jnp.where(rowid == (7 - i if reverse else i), link[i][c], t)
                    tiles.append(t)
                for blk in range(2):
                    lanes = slice(blk * SW, (blk + 1) * SW)
                    for k, t in enumerate(tiles):
                        a_out[z, blk, k] = t[:, lanes]
                    for j in range(NJ):
                        src = pw[NJ - 1 - j] if reverse else pw[j]
                        for c in range(2):
                            p_out[z, blk, c, j:j + 1, :] = src[c][:, lanes]

    vm = pl.BlockSpec(memory_space=pltpu.VMEM)
    a_shape, p_shape = _sds((2, 2, 10, 8, SW)), _sds((2, 2, 2, NJ, SW))
    a, p, ac, pc = pl.pallas_call(body, in_specs=[vm, vm], out_specs=[vm] * 4, out_shape=[a_shape, p_shape] * 2,
                                  name="s5_tables")(mr, mi)
    return (a, p), (ac, pc)


def _to_lockstep(ref, *lead):
    return jnp.concatenate([ref[(*lead, pl.ds(j, 8, stride=NJ), slice(None))] for j in range(NJ)], axis=0)


def _from_lockstep(val, ref, *lead):
    for j in range(NJ):
        ref[(*lead, pl.ds(j, 8, stride=NJ), slice(None))] = val[8 * j:8 * j + 8]


def _expand_powers(p_ref, pexp):
    for c in range(2):
        for j in range(NJ):
            pexp[c, j] = jnp.broadcast_to(p_ref[0, 0, c, j:j + 1, :], (8, SW))


def _lockstep_scan(xre, xim, a_ref, pexp, car, reverse, extra=None):
    a = [a_ref[0, 0, k] for k in range(10)]
    mr, mi = a[0], a[1]
    order = (lambda i: NJ - 1 - i) if reverse else (lambda i: i)

    def local(i, hcar):
        hr, hi = hcar
        r0 = pl.multiple_of(order(i) * 8, 8)
        hr, hi = mr * hr - mi * hi + xre[pl.ds(r0, 8), :], mr * hi + mi * hr + xim[pl.ds(r0, 8), :]
        xre[pl.ds(r0, 8), :] = hr
        xim[pl.ds(r0, 8), :] = hi
        return hr, hi

    z8 = jnp.zeros((8, SW), F32)
    er, ei = lax.fori_loop(0, NJ, local, (z8, z8), unroll=4)
    c0r, c0i = car[0], car[1]
    er, ei = _tile_scan(er, ei, a[2:], c0r, c0i, reverse)
    rowid = lax.broadcasted_iota(jnp.int32, (8, SW), 0)
    first, sh, last = (7, 7, 0) if reverse else (0, 1, 7)
    cvr = jnp.where(rowid == first, c0r, pltpu.roll(er, sh, 0))
    cvi = jnp.where(rowid == first, c0i, pltpu.roll(ei, sh, 0))
    car[0] = jnp.broadcast_to(er[last:last + 1, :], (8, SW))
    car[1] = jnp.broadcast_to(ei[last:last + 1, :], (8, SW))

    def fix(i, carry):
        j = order(i)
        r0 = pl.multiple_of(j * 8, 8)
        pr, pi = pexp[0, j], pexp[1, j]
        sr = xre[pl.ds(r0, 8), :] + pr * cvr - pi * cvi
        si = xim[pl.ds(r0, 8), :] + pr * cvi + pi * cvr
        xre[pl.ds(r0, 8), :] = sr
        xim[pl.ds(r0, 8), :] = si
        if extra is None:
            return carry
        return (sr, si, extra(r0, sr, si, carry[0], carry[1], carry[2]))

    init = (cvr, cvi, extra(None, None, None, None, None, None)) if extra is not None else 0
    return lax.fori_loop(0, NJ, fix, init, unroll=4)


def _s5_time_block(z, s, t, adjoint):
    flip = (1 - z) if adjoint else z
    return s * (L // TT) + t + flip * (L // TT - 1 - 2 * t)


def _s5_fwd(h, bre, bim, cre, cim, tab):
    nt = L // TT
    taba, tabp = tab

    def body(u_ref, bre_ref, bim_ref, cre_ref, cim_ref, a_ref, p_ref, hre_ref, him_ref, y_ref, car, pexp):
        z = pl.program_id(1)
        s = pl.program_id(2)
        tc = pl.program_id(3)

        @pl.when(tc == 0)
        def _():
            car[...] = jnp.zeros_like(car)

        @pl.when((tc == 0) & (s == 0))
        def _():
            _expand_powers(p_ref, pexp)

        u = _to_lockstep(u_ref)
        hre_ref[0] = _mm(u, bre_ref[0, 0])
        him_ref[0] = _mm(u, bim_ref[0, 0])

        @pl.when(z == 0)
        def _():
            _lockstep_scan(hre_ref.at[0], him_ref.at[0], a_ref, pexp, car, False)

        @pl.when(z == 1)
        def _():
            _lockstep_scan(hre_ref.at[0], him_ref.at[0], a_ref, pexp, car, True)

        _from_lockstep(_mm(hre_ref[0], cre_ref[0, 0]) - _mm(him_ref[0], cim_ref[0, 0]), y_ref, 0)

    tb = lambda b, z, s, t: _s5_time_block(z, s, t, False)
    wspec = lambda r, c: pl.BlockSpec((1, 1, r, c), lambda b, z, s, t: (z, b, 0, 0))
    return pl.pallas_call(
        body, grid=(2, 2, NSEQ, nt),
        in_specs=[pl.BlockSpec((TT, 128), lambda b, z, s, t: (tb(b, z, s, t), b)),
                  wspec(128, SW), wspec(128, SW), wspec(SW, 128), wspec(SW, 128),
                  pl.BlockSpec((1, 1, 10, 8, SW), lambda b, z, s, t: (z, b, 0, 0, 0)),
                  pl.BlockSpec((1, 1, 2, NJ, SW), lambda b, z, s, t: (z, b, 0, 0, 0))],
        out_specs=[pl.BlockSpec((1, TT, SW), lambda b, z, s, t: (z, tb(b, z, s, t), b)),
                   pl.BlockSpec((1, TT, SW), lambda b, z, s, t: (z, tb(b, z, s, t), b)),
                   pl.BlockSpec((1, TT, 128), lambda b, z, s, t: (z, tb(b, z, s, t), b))],
        out_shape=[_sds((2, N, 2 * SW)), _sds((2, N, 2 * SW)), _sds((2, N, 256))],
        scratch_shapes=[pltpu.VMEM((2, 8, SW), F32), pltpu.VMEM((2, NJ, 8, SW), F32)],
        name="s5_fwd", compiler_params=_cp(("arbitrary",) * 4))(h, bre, bim, cre, cim, taba, tabp)


def _s5_bwd(h, dyp, hre, him, bre, bim, cre, cim, tabc):
    nt = L // TT
    taba, tabp = tabc

    def body(u_ref, dy_ref, hre_ref, him_ref, bre_ref, bim_ref, cre_ref, cim_ref, a_ref, p_ref,
             du_ref, dbre_ref, dbim_ref, dcre_ref, dcim_ref, dmu_ref, gre, gim, car, acc, macc, pexp):
        z = pl.program_id(1)
        s = pl.program_id(2)
        tc = pl.program_id(3)

        @pl.when(tc == 0)
        def _():
            car[...] = jnp.zeros_like(car)

        @pl.when((tc == 0) & (s == 0))
        def _():
            acc[...] = jnp.zeros_like(acc)
            macc[...] = jnp.zeros_like(macc)
            _expand_powers(p_ref, pexp)

        dy = _to_lockstep(dy_ref)
        gre[...] = _mm_nt(dy, cre_ref[0, 0])
        gim[...] = -_mm_nt(dy, cim_ref[0, 0])

        def run(reverse):
            def pair(r0, gr_, gi_, pvr, pvi, m):
                if r0 is None:
                    return (macc[0], macc[1])
                hr = hre_ref[0, pl.ds(r0, 8), :]
                hi = him_ref[0, pl.ds(r0, 8), :]
                return (m[0] + pvr * hr + pvi * hi, m[1] + pvi * hr - pvr * hi)

            _, _, (dmr, dmi) = _lockstep_scan(gre, gim, a_ref, pexp, car, reverse, pair)
            macc[0] = dmr
            macc[1] = dmi

        @pl.when(z == 0)
        def _():
            run(True)

        @pl.when(z == 1)
        def _():
            run(False)

        gr = gre[...]
        gi = gim[...]
        u = _to_lockstep(u_ref)
        _from_lockstep(_mm_nt(gr, bre_ref[0, 0]) + _mm_nt(gi, bim_ref[0, 0]), du_ref, 0)
        acc[0] += _mm_tn(u, gr)
        acc[1] += _mm_tn(u, gi)
        acc[2] += _mm_tn(dy, hre_ref[0])
        acc[3] -= _mm_tn(dy, him_ref[0])

        @pl.when((tc == nt - 1) & (s == NSEQ - 1))
        def _():
            grp = lax.broadcasted_iota(jnp.int32, (S5_H, SW), 1) // S5_P
            for k, out in enumerate((dbre_ref, dbim_ref, dcre_ref, dcim_ref)):
                c = jnp.zeros((S5_H, SW), F32)
                for i in range(8):
                    c = c + jnp.where(grp == i, acc[k, i * S5_H:(i + 1) * S5_H, :], 0.0)
                out[0, 0] = c
            dmu_ref[0, 0] = jnp.concatenate([jnp.sum(macc[0], axis=0, keepdims=True),
                                             jnp.sum(macc[1], axis=0, keepdims=True)], axis=0)

    tb = lambda b, z, s, t: _s5_time_block(z, s, t, True)
    wspec = lambda r, c: pl.BlockSpec((1, 1, r, c), lambda b, z, s, t: (z, b, 0, 0))
    tok = lambda w_: pl.BlockSpec((TT, w_), lambda b, z, s, t: (tb(b, z, s, t), b))
    st = pl.BlockSpec((1, TT, SW), lambda b, z, s, t: (z, tb(b, z, s, t), b))
    return pl.pallas_call(
        body, grid=(2, 2, NSEQ, nt),
        in_specs=[tok(128), tok(128), st, st, wspec(128, SW), wspec(128, SW), wspec(SW, 128), wspec(SW, 128),
                  pl.BlockSpec((1, 1, 10, 8, SW), lambda b, z, s, t: (z, b, 0, 0, 0)),
                  pl.BlockSpec((1, 1, 2, NJ, SW), lambda b, z, s, t: (z, b, 0, 0, 0))],
        out_specs=[pl.BlockSpec((1, TT, 128), lambda b, z, s, t: (z, tb(b, z, s, t), b)),
                   wspec(S5_H, SW), wspec(S5_H, SW), wspec(S5_H, SW), wspec(S5_H, SW),
                   wspec(2, SW)],
        out_shape=[_sds((2, N, 256))] + [_sds((2, 2, S5_H, SW))] * 4 + [_sds((2, 2, 2, SW))],
        scratch_shapes=[pltpu.VMEM((TT, SW), F32), pltpu.VMEM((TT, SW), F32), pltpu.VMEM((2, 8, SW), F32),
                        pltpu.VMEM((4, 128, SW), F32), pltpu.VMEM((2, 8, SW), F32), pltpu.VMEM((2, NJ, 8, SW), F32)],
        name="s5_bwd", compiler_params=_cp(("arbitrary",) * 4))(h, dyp, hre, him, bre, bim, cre, cim, taba, tabp)


_GELU_C = math.sqrt(2.0 / math.pi)


def _gelu(y):
    return 0.5 * y * (1.0 + jnp.tanh(_GELU_C * (y + 0.044715 * y * y * y)))


def _gelu_grad(y):
    t = jnp.tanh(_GELU_C * (y + 0.044715 * y * y * y))
    return 0.5 * (1.0 + t) + 0.5 * y * (1.0 - t * t) * _GELU_C * (1.0 + 3 * 0.044715 * y * y)


def _glu_halves(w4_ref):
    return (jnp.concatenate([w4_ref[0], w4_ref[1]], axis=1), jnp.concatenate([w4_ref[2], w4_ref[3]], axis=1))


def _s5_glu_fwd(y2, h, dsk, w4, bv, bg):
    tm = 512

    def body(y2_ref, u_ref, d_ref, w4_ref, bv_ref, bg_ref, ya_ref):
        wv, wg = _glu_halves(w4_ref)
        z = _gelu(y2_ref[0] + y2_ref[1] + d_ref[...] * u_ref[...])
        val = _mm(z, wv) + bv_ref[...]
        gate = _mm(z, wg) + bg_ref[...]
        ya_ref[...] = (val * jax.nn.sigmoid(gate)).astype(MX)

    full = lambda r, c: pl.BlockSpec((r, c), lambda i: (0, 0))
    return pl.pallas_call(
        body, grid=(N // tm,),
        in_specs=[pl.BlockSpec((2, tm, 256), lambda i: (0, i, 0)), pl.BlockSpec((tm, 256), lambda i: (i, 0)),
                  full(1, 256), pl.BlockSpec((NSHARD, 256, 128), lambda i: (0, 0, 0)), full(1, 256), full(1, 256)],
        out_specs=pl.BlockSpec((tm, 256), lambda i: (i, 0)),
        out_shape=_sds((N, 256), MX), name="s5_glu_fwd", compiler_params=_cp(("parallel",)))(y2, h, dsk, w4, bv, bg)


def _s5_glu_bwd(y2, h, dsk, w4, bv, bg, dya):
    tm = 512
    nt = N // tm

    def body(y2_ref, u_ref, d_ref, w4_ref, bv_ref, bg_ref, dya_ref,
             dyp_ref, dud_ref, dd_ref, dw4_ref, dbv_ref, dbg_ref, accv, accg):
        i = pl.program_id(0)

        @pl.when(i == 0)
        def _():
            for r in (dd_ref, accv, accg, dbv_ref, dbg_ref):
                r[...] = jnp.zeros_like(r)

        wv, wg = _glu_halves(w4_ref)
        u = u_ref[...]
        y = y2_ref[0] + y2_ref[1] + d_ref[...] * u
        z = _gelu(y)
        val = _mm(z, wv) + bv_ref[...]
        sig = jax.nn.sigmoid(_mm(z, wg) + bg_ref[...])
        dya = dya_ref[...]
        dval = dya * sig
        dgate = dya * val * sig * (1.0 - sig)
        dz = _mm_nt(dval, wv) + _mm_nt(dgate, wg)
        dy = dz * _gelu_grad(y)
        dyp_ref[...] = dy
        dud_ref[...] = (dy * d_ref[...]).astype(MX)
        dd_ref[...] += jnp.sum(dy * u, axis=0, keepdims=True)
        accv[...] += _mm_tn(z, dval)
        accg[...] += _mm_tn(z, dgate)
        dbv_ref[...] += jnp.sum(dval, axis=0, keepdims=True)
        dbg_ref[...] += jnp.sum(dgate, axis=0, keepdims=True)

        @pl.when(i == nt - 1)
        def _():
            dw4_ref[0] = accv[:, 0:128].astype(MX)
            dw4_ref[1] = accv[:, 128:256].astype(MX)
            dw4_ref[2] = accg[:, 0:128].astype(MX)
            dw4_ref[3] = accg[:, 128:256].astype(MX)

    full = lambda r, c: pl.BlockSpec((r, c), lambda i: (0, 0))
    row = pl.BlockSpec((tm, 256), lambda i: (i, 0))
    wspec = pl.BlockSpec((NSHARD, 256, 128), lambda i: (0, 0, 0))
    return pl.pallas_call(
        body, grid=(nt,),
        in_specs=[pl.BlockSpec((2, tm, 256), lambda i: (0, i, 0)), row, full(1, 256), wspec, full(1, 256), full(1, 256),
                  row],
        out_specs=[row, row, full(1, 256), wspec, full(1, 256), full(1, 256)],
        out_shape=[_sds((N, 256)), _sds((N, 256), MX), _sds((1, 256)), _sds((NSHARD, 256, 128), MX), _sds((1, 256)),
                   _sds((1, 256))],
        scratch_shapes=[pltpu.VMEM((256, 256), F32), pltpu.VMEM((256, 256), F32)],
        name="s5_glu_bwd", compiler_params=_cp(("arbitrary",)))(y2, h, dsk, w4, bv, bg, dya)


def _logsig(x):
    return jnp.minimum(x, 0.0) - jnp.log(1.0 + jnp.exp(-jnp.abs(x)))


def _gla_gate_bwd(h, wa, ba, dla_f, dla_b):
    tm = 512

    def body(hl_ref, wa_ref, ba_ref, df_ref, db_ref, dhl_ref, dwa_ref, dba_ref):
        i = pl.program_id(0)

        @pl.when(i == 0)
        def _():
            dwa_ref[...] = jnp.zeros_like(dwa_ref)
            dba_ref[...] = jnp.zeros_like(dba_ref)

        hl = hl_ref[...]
        pre = _mm(hl, wa_ref[...]) + ba_ref[...]
        dpre = jnp.concatenate([df_ref[...], db_ref[...]], axis=1) * (1.0 / 16.0) * jax.nn.sigmoid(-pre)
        dhl_ref[...] = _mm_nt(dpre, wa_ref[...]).astype(MX)
        dwa_ref[...] += _mm_tn(hl, dpre)[0:32]
        dba_ref[...] += jnp.sum(dpre, axis=0, keepdims=True)

    row = pl.BlockSpec((tm, 128), lambda i: (i, 0))
    return pl.pallas_call(
        body, grid=(N // tm,),
        in_specs=[pl.BlockSpec((tm, 128), lambda i: (i, 14)), pl.BlockSpec((128, 256), lambda i: (0, 0)),
                  pl.BlockSpec((1, 256), lambda i: (0, 0)), row, row],
        out_specs=[row, pl.BlockSpec((32, 256), lambda i: (0, 0)), pl.BlockSpec((1, 256), lambda i: (0, 0))],
        out_shape=[_sds((N, 128), MX), _sds((32, 256)), _sds((1, 256))],
        name="gla_gate_bwd", compiler_params=_cp(("arbitrary",)))(h, wa, ba, dla_f, dla_b)


def _gla_chunk(q, k, v, la, st, rev):
    c = GLA_CHUNK
    rows = q.shape[0]
    nch = rows // c
    b = _cums(la, rev)
    blc = [jnp.sum(la[i * c:(i + 1) * c], axis=0, keepdims=True) for i in range(nch)]
    bl = jnp.concatenate([jnp.broadcast_to(t, (c, 128)) for t in blc], axis=0)
    q_in = q * (32.0 ** -0.5) * jnp.exp(b)
    k_in = k * jnp.exp(-b)
    k_st = k * jnp.exp(bl - b)
    lane_k = lax.broadcasted_iota(jnp.int32, (1, 128), 1) // 32
    lane_v = lax.broadcasted_iota(jnp.int32, (1, 256), 1) // 64
    qs = jnp.concatenate([jnp.where(lane_k == hd, q_in, 0.0) for hd in range(4)], axis=0)
    a = _dmm_nt(qs, k_in)
    a = jnp.where(jnp.concatenate([_chunk_pairs(rows, rev, rev)] * 4, axis=0), a, 0.0)
    o4 = _dmm(a, v)
    o = jnp.zeros((rows, 256), F32)
    for hd in range(4):
        o = o + jnp.where(lane_v == hd, o4[hd * rows:(hd + 1) * rows], 0.0)
    bd = (lax.broadcasted_iota(jnp.int32, (256, 128), 0) // 64) == (lax.broadcasted_iota(jnp.int32, (256, 128), 1) // 32)
    inter = [None] * nch
    for i in (reversed(range(nch)) if rev else range(nch)):
        sl = slice(i * c, (i + 1) * c)
        inter[i] = _dmm_nt(q_in[sl], st)
        st = jnp.exp(blc[i]) * st + jnp.where(bd, _dmm_tn(v[sl], k_st[sl]), 0.0)
    return o + jnp.concatenate(inter, axis=0), st


def _gla_chunk_of(c, rev):
    return NGROUP - 1 - c if rev else c


def _gla_fwd(h, la2):
    c = GLA_GROUP * GLA_CHUNK

    def body(qf, kf, vf, laf, qb, kb, vb, lab, of_ref, ob_ref, sf_ref, sb_ref, stf, stb):
        @pl.when(pl.program_id(0) == 0)
        def _():
            stf[...] = jnp.zeros_like(stf)
            stb[...] = jnp.zeros_like(stb)

        ins = [(qf[s], kf[s], vf[s], laf[s], stf[s], qb[s], kb[s], vb[s], lab[s], stb[s]) for s in range(NSEQ)]
        outs = [(_gla_chunk(*t[:5], False), _gla_chunk(*t[5:], True)) for t in ins]
        for s in range(NSEQ):
            sf_ref[s, 0] = ins[s][4]
            sb_ref[s, 0] = ins[s][9]
            (of_ref[s], stf[s]), (ob_ref[s], stb[s]) = outs[s]

    def specs(rev):
        ch = lambda i: _gla_chunk_of(i, rev)
        return [pl.BlockSpec((NSEQ, c, 128), lambda i: (0, ch(i), 2)), pl.BlockSpec((NSEQ, c, 128), lambda i: (0, ch(i), 3)),
                pl.BlockSpec((NSEQ, c, 256), lambda i: (0, ch(i), 2)),
                pl.BlockSpec((NSEQ, c, 128), lambda i: (0, ch(i), 1 if rev else 0))]

    orow = lambda rev: pl.BlockSpec((NSEQ, c, 256), lambda i: (0, _gla_chunk_of(i, rev), 0))
    srow = lambda rev: pl.BlockSpec((NSEQ, 1, 256, 128), lambda i: (0, _gla_chunk_of(i, rev), 0, 0))
    h3, la3 = h.reshape(NSEQ, L, DINP), la2.reshape(NSEQ, L, 256)
    of, ob, sf, sb = pl.pallas_call(
        body, grid=(NGROUP,),
        in_specs=specs(False) + specs(True),
        out_specs=[orow(False), orow(True), srow(False), srow(True)],
        out_shape=[_sds((NSEQ, L, 256)), _sds((NSEQ, L, 256)), _sds((NSEQ, NGROUP, 256, 128)),
                   _sds((NSEQ, NGROUP, 256, 128))],
        scratch_shapes=[pltpu.VMEM((NSEQ, 256, 128), F32), pltpu.VMEM((NSEQ, 256, 128), F32)],
        name="gla_fwd", compiler_params=_cp(("arbitrary",)))(h3, h3, h3, la3, h3, h3, h3, la3)
    return of.reshape(N, 256), ob.reshape(N, 256), sf, sb


def _gla_bwd(h, la2, do, sf, sb):
    c = GLA_GROUP * GLA_CHUNK

    def body(qf, kf, vf, laf, dof, sfr, qb, kb, vb, lab, dob, sbr,
             dqf, dkf, dvf, dlf, dqb, dkb, dvb, dlb, dstf, dstb):
        @pl.when(pl.program_id(0) == 0)
        def _():
            dstf[...] = jnp.zeros_like(dstf)
            dstb[...] = jnp.zeros_like(dstb)

        def one(s, q, k, v, la, do_, st, dst, rev):
            _, vjp = jax.vjp(functools.partial(_gla_chunk, rev=rev), q[s], k[s], v[s], la[s], st[s, 0])
            return vjp((do_[s], dst[s]))

        res = [(one(s, qf, kf, vf, laf, dof, sfr, dstf, False), one(s, qb, kb, vb, lab, dob, sbr, dstb, True))
               for s in range(NSEQ)]
        for s in range(NSEQ):
            for (gq, gk, gv, gl, gs), (dq, dk, dv, dl, dst) in ((res[s][0], (dqf, dkf, dvf, dlf, dstf)),
                                                                  (res[s][1], (dqb, dkb, dvb, dlb, dstb))):
                dq[s], dk[s], dv[s] = gq.astype(MX), gk.astype(MX), gv.astype(MX)
                dl[s], dst[s] = gl, gs

    def specs(rev):
        ch = lambda i: _gla_chunk_of(i, not rev)
        return [pl.BlockSpec((NSEQ, c, 128), lambda i: (0, ch(i), 2)), pl.BlockSpec((NSEQ, c, 128), lambda i: (0, ch(i), 3)),
                pl.BlockSpec((NSEQ, c, 256), lambda i: (0, ch(i), 2)),
                pl.BlockSpec((NSEQ, c, 128), lambda i: (0, ch(i), 1 if rev else 0)),
                pl.BlockSpec((NSEQ, c, 256), lambda i: (0, ch(i), 0)),
                pl.BlockSpec((NSEQ, 1, 256, 128), lambda i: (0, ch(i), 0, 0))]

    def ospecs(rev):
        ch = lambda i: _gla_chunk_of(i, not rev)
        n = pl.BlockSpec((NSEQ, c, 128), lambda i: (0, ch(i), 0))
        return [n, n, pl.BlockSpec((NSEQ, c, 256), lambda i: (0, ch(i), 0)), n]

    oshape = [_sds((NSEQ, L, 128), MX), _sds((NSEQ, L, 128), MX), _sds((NSEQ, L, 256), MX), _sds((NSEQ, L, 128))]
    h3, la3, do3 = h.reshape(NSEQ, L, DINP), la2.reshape(NSEQ, L, 256), do.reshape(NSEQ, L, 256)
    res = pl.pallas_call(
        body, grid=(NGROUP,),
        in_specs=specs(False) + specs(True),
        out_specs=ospecs(False) + ospecs(True),
        out_shape=oshape + oshape,
        scratch_shapes=[pltpu.VMEM((NSEQ, 256, 128), F32), pltpu.VMEM((NSEQ, 256, 128), F32)],
        name="gla_bwd", compiler_params=_cp(("arbitrary",)))(h3, h3, h3, la3, do3, sf, h3, h3, h3, la3, do3, sb)
    return [r.reshape(N, r.shape[-1]) for r in res]


def _gla_post(of, ob, r, g):
    o = of + ob
    head = lax.broadcasted_iota(jnp.int32, (1, 256), 1) // 64
    mu = jnp.zeros_like(o)
    for hd in range(4):
        mu = mu + jnp.where(head == hd, jnp.sum(jnp.where(head == hd, o, 0.0), axis=-1, keepdims=True) * (1.0 / 64.0), 0.0)
    xc = o - mu
    var = jnp.zeros_like(o)
    for hd in range(4):
        var = var + jnp.where(head == hd, jnp.sum(jnp.where(head == hd, xc * xc, 0.0), axis=-1, keepdims=True) * (1.0 / 64.0), 0.0)
    return xc * lax.rsqrt(var + LN_EPS) * g * (r * jax.nn.sigmoid(r))


def _gla_post_fwd(of, ob, h, g):
    tm = 512

    def body(of_ref, ob_ref, r_ref, g_ref, y_ref):
        y_ref[...] = _gla_post(of_ref[...], ob_ref[...], r_ref[...], g_ref[...]).astype(MX)

    row = pl.BlockSpec((tm, 256), lambda i: (i, 0))
    return pl.pallas_call(
        body, grid=(N // tm,),
        in_specs=[row, row, pl.BlockSpec((tm, 256), lambda i: (i, 3)), pl.BlockSpec((1, 256), lambda i: (0, 0))],
        out_specs=row, out_shape=_sds((N, 256), MX), name="gla_post_fwd", compiler_params=_cp(("parallel",)))(of, ob, h, g)


def _gla_post_bwd(of, ob, h, g, dyb):
    tm = 512

    def body(of_ref, ob_ref, r_ref, g_ref, dy_ref, do_ref, dr_ref, dg_ref):
        @pl.when(pl.program_id(0) == 0)
        def _():
            dg_ref[...] = jnp.zeros_like(dg_ref)

        _, vjp = jax.vjp(_gla_post, of_ref[...], ob_ref[...], r_ref[...], g_ref[...])
        go, _, gr, gg = vjp(dy_ref[...])
        do_ref[...] = go
        dr_ref[...] = gr.astype(MX)
        dg_ref[...] += gg

    row = pl.BlockSpec((tm, 256), lambda i: (i, 0))
    one = pl.BlockSpec((1, 256), lambda i: (0, 0))
    return pl.pallas_call(
        body, grid=(N // tm,),
        in_specs=[row, row, pl.BlockSpec((tm, 256), lambda i: (i, 3)), one, row],
        out_specs=[row, row, one], out_shape=[_sds((N, 256)), _sds((N, 256), MX), _sds((1, 256))],
        name="gla_post_bwd", compiler_params=_cp(("arbitrary",)))(of, ob, h, g, dyb)


def _rope_tables(width):
    pos = jnp.arange(L, dtype=F32)
    inv_freq = ROPE_THETA ** (-jnp.arange(0, ROT, 2, dtype=F32) / ROT)
    ang = pos[:, None] * inv_freq[None, :]
    cos, sin = jnp.cos(ang), jnp.sin(ang)
    one = jnp.ones((L, 64 - ROT), F32)
    zero = jnp.zeros((L, 64 - ROT), F32)
    z8 = jnp.zeros((L, ROT // 2), F32)
    c = jnp.concatenate([cos, cos, one], axis=1)
    sa = jnp.concatenate([z8, sin, zero], axis=1)
    sb = jnp.concatenate([-sin, z8, zero], axis=1)
    rep = width // 64
    return jnp.stack([jnp.tile(c, (1, rep)), jnp.tile(sa, (1, rep)), jnp.tile(sb, (1, rep))])


def _pieces(t, f):
    out = [f(t[:, c * 128:(c + 1) * 128]) for c in range(t.shape[-1] // 128)]
    return out[0] if len(out) == 1 else jnp.concatenate(out, axis=1)


def _rope(t, tab):
    return _pieces(t, lambda x: x * tab[0] + pltpu.roll(x, ROT // 2, 1) * tab[1] + pltpu.roll(x, 128 - ROT // 2, 1) * tab[2])


def _rope_t(g, tab):
    return _pieces(g, lambda x: x * tab[0] + pltpu.roll(x * tab[1], 128 - ROT // 2, 1) + pltpu.roll(x * tab[2], ROT // 2, 1))


def _swa_pad_kv(kv_ref, tk_ref, kexp, vexp):
    z = jnp.zeros((SWA_BLK, 256), F32)
    kr = _rope(kv_ref[:, 0:128], tk_ref[...])
    for hk in range(2):
        for pad in (kexp, vexp):
            pad[hk, 0:SWA_BLK] = z
            pad[hk, SWA_BLK + L:] = z
        kexp[hk, SWA_BLK:SWA_BLK + L] = _swa_expand(kr, hk)
        vexp[hk, SWA_BLK:SWA_BLK + L] = _swa_expand(kv_ref[:, 128:256], hk)


def _swa_expand(x, hk):
    lane = lax.broadcasted_iota(jnp.int32, x.shape, 1)
    sw = pltpu.roll(x, 64, 1)
    pair = jnp.where(lane < 64, x, sw) if hk == 0 else jnp.where(lane < 64, sw, x)
    return jnp.concatenate([pair, pair], axis=1)


def _swa_fold(x, hk):
    a = x[:, 0:128] + x[:, 128:256]
    t = a + pltpu.roll(a, 64, 1)
    lane = lax.broadcasted_iota(jnp.int32, a.shape, 1)
    return jnp.where((lane < 64) if hk == 0 else (lane >= 64), t, 0.0)


def _swa_probs(q2, kexp, n, sink_ref, hk):
    slot = lax.broadcasted_iota(jnp.int32, (1, 256), 1) // 64
    qs = jnp.concatenate([jnp.where(slot == g, q2, 0.0) for g in range(4)], axis=0)
    s = _mm_nt(qs, kexp) * 0.125
    i = lax.broadcasted_iota(jnp.int32, (SWA_BLK, 3 * SWA_BLK), 0)
    j = lax.broadcasted_iota(jnp.int32, (SWA_BLK, 3 * SWA_BLK), 1)
    kpos = n * SWA_BLK - SWA_BLK + j
    ok = (j - i >= 0) & (j - i <= 2 * SWA_BLK) & (kpos >= 0) & (kpos < L)
    s = jnp.where(jnp.concatenate([ok] * 4, axis=0), s, NEG_BIG)
    rowg = lax.broadcasted_iota(jnp.int32, (4 * SWA_BLK, 1), 0) // SWA_BLK
    sink = jnp.zeros((4 * SWA_BLK, 1), F32)
    for g in range(4):
        sink = jnp.where(rowg == g, sink_ref[hk * 4 + g], sink)
    m = jnp.maximum(jnp.max(s, axis=-1, keepdims=True), sink)
    p = jnp.exp(s - m)
    ps = jnp.exp(sink - m)
    inv = 1.0 / (jnp.sum(p, axis=-1, keepdims=True) + ps)
    return qs, p * inv, ps * inv, slot, rowg


def _swa_qtab(tk_ref, r0):
    return [tk_ref[i, pl.ds(r0, SWA_BLK), :] for i in range(3)]


def _swa_fwd(h, tk, sink):
    def body(sink_ref, q_ref, kv_ref, tk_ref, y_ref, kexp, vexp):
        n = pl.program_id(1)

        @pl.when(n == 0)
        def _():
            _swa_pad_kv(kv_ref, tk_ref, kexp, vexp)

        for t in range(SWA_PER):
            blk = n * SWA_PER + t
            rows = slice(t * SWA_BLK, (t + 1) * SWA_BLK)
            r0 = pl.multiple_of(blk * SWA_BLK, SWA_BLK)
            q = _rope(q_ref[rows, :], _swa_qtab(tk_ref, r0))
            for hk in range(2):
                _, p, _, slot, _ = _swa_probs(q[:, hk * 256:(hk + 1) * 256], kexp[hk, pl.ds(r0, 3 * SWA_BLK), :], blk,
                                              sink_ref, hk)
                o4 = _mm(p, vexp[hk, pl.ds(r0, 3 * SWA_BLK), :])
                o = jnp.zeros((SWA_BLK, 256), F32)
                for g in range(4):
                    o = o + jnp.where(slot == g, o4[g * SWA_BLK:(g + 1) * SWA_BLK], 0.0)
                y_ref[rows, hk * 256:(hk + 1) * 256] = o.astype(MX)

    tm = SWA_PER * SWA_BLK
    return pl.pallas_call(
        body,
        grid_spec=pltpu.PrefetchScalarGridSpec(
            num_scalar_prefetch=1, grid=(NSEQ, L // tm),
            in_specs=[pl.BlockSpec((tm, 512), lambda s, n, sk: (s * (L // tm) + n, 2)),
                      pl.BlockSpec((L, 256), lambda s, n, sk: (s, 6)),
                      pl.BlockSpec((3, L, 128), lambda s, n, sk: (0, 0, 0))],
            out_specs=pl.BlockSpec((tm, 512), lambda s, n, sk: (s * (L // tm) + n, 0)),
            scratch_shapes=[pltpu.VMEM((2, L + 2 * SWA_BLK, 256), F32), pltpu.VMEM((2, L + 2 * SWA_BLK, 256), F32)]),
        out_shape=_sds((N, 512), MX), name="swa_fwd", compiler_params=_cp(("arbitrary", "arbitrary")))(sink, h, h, tk)


def _swa_bwd(h, tk, sink, dyc):
    tm = SWA_PER * SWA_BLK

    def body(sink_ref, q_ref, kv_ref, tk_ref, dy_ref, dq_ref, dkv_ref, dsink_ref, kexp_all, vexp_all, dkacc, dvacc):
        sq = pl.program_id(0)
        n = pl.program_id(1)

        @pl.when(n == 0)
        def _():
            _swa_pad_kv(kv_ref, tk_ref, kexp_all, vexp_all)
            dkacc[...] = jnp.zeros_like(dkacc)
            dvacc[...] = jnp.zeros_like(dvacc)

        @pl.when((n == 0) & (sq == 0))
        def _():
            dsink_ref[...] = jnp.zeros_like(dsink_ref)

        hrow = lax.broadcasted_iota(jnp.int32, (8, 128), 0)
        dsk = jnp.zeros((8, 128), F32)
        for t in range(SWA_PER):
            blk = n * SWA_PER + t
            rows = slice(t * SWA_BLK, (t + 1) * SWA_BLK)
            r0 = pl.multiple_of(blk * SWA_BLK, SWA_BLK)
            tq = _swa_qtab(tk_ref, r0)
            q = _rope(q_ref[rows, :], tq)
            for hk in range(2):
                kexp = kexp_all[hk, pl.ds(r0, 3 * SWA_BLK), :]
                vexp = vexp_all[hk, pl.ds(r0, 3 * SWA_BLK), :]
                qs, p, ps, slot, rowg = _swa_probs(q[:, hk * 256:(hk + 1) * 256], kexp, blk, sink_ref, hk)
                dy2 = dy_ref[rows, hk * 256:(hk + 1) * 256]
                dos = jnp.concatenate([jnp.where(slot == g, dy2, 0.0) for g in range(4)], axis=0)
                dp = _mm_nt(dos, vexp)
                delta = jnp.sum(p * dp, axis=-1, keepdims=True)
                ds = p * (dp - delta) * 0.125
                dsr = -ps * delta
                for g in range(4):
                    dsk = dsk + jnp.where(hrow == hk * 4 + g,
                                          jnp.sum(jnp.where(rowg == g, dsr, 0.0), axis=0, keepdims=True), 0.0)
                dq4 = _mm(ds, kexp)
                dq2 = jnp.zeros((SWA_BLK, 256), F32)
                for g in range(4):
                    dq2 = dq2 + jnp.where(slot == g, dq4[g * SWA_BLK:(g + 1) * SWA_BLK], 0.0)
                dq_ref[rows, hk * 256:(hk + 1) * 256] = _rope_t(dq2, tq).astype(MX)
                dkacc[hk, pl.ds(r0, 3 * SWA_BLK), :] += _mm_tn(ds, qs)
                dvacc[hk, pl.ds(r0, 3 * SWA_BLK), :] += _mm_tn(p, dos)
        dsink_ref[...] += dsk

        @pl.when(n == L // tm - 1)
        def _():
            seq = slice(SWA_BLK, SWA_BLK + L)
            dk = _rope_t(_swa_fold(dkacc[0, seq], 0) + _swa_fold(dkacc[1, seq], 1), tk_ref[...])
            dkv_ref[:, 0:128] = dk.astype(MX)
            dkv_ref[:, 128:256] = (_swa_fold(dvacc[0, seq], 0) + _swa_fold(dvacc[1, seq], 1)).astype(MX)

    blk = lambda col: pl.BlockSpec((tm, 512), lambda s, n, sk: (s * (L // tm) + n, col))
    pad = pltpu.VMEM((2, L + 2 * SWA_BLK, 256), F32)
    return pl.pallas_call(
        body,
        grid_spec=pltpu.PrefetchScalarGridSpec(
            num_scalar_prefetch=1, grid=(NSEQ, L // tm),
            in_specs=[blk(2), pl.BlockSpec((L, 256), lambda s, n, sk: (s, 6)),
                      pl.BlockSpec((3, L, 128), lambda s, n, sk: (0, 0, 0)), blk(0)],
            out_specs=[blk(0), pl.BlockSpec((L, 256), lambda s, n, sk: (s, 0)),
                       pl.BlockSpec((8, 128), lambda s, n, sk: (0, 0))],
            scratch_shapes=[pad, pad, pad, pad]),
        out_shape=[_sds((N, 512), MX), _sds((N, 256), MX), _sds((8, 128))],
        name="swa_bwd", compiler_params=_cp(("arbitrary", "arbitrary")))(sink, h, h, tk, dyc)


def _outproj_bwd(dx1, s1, ya, yb, yc, wo, g):
    tm = 512
    nt = N // tm

    def body(dx1_ref, s_ref, ya_ref, yb_ref, yc_ref, wo_ref, g_ref,
             dya_ref, dyb_ref, dyc_ref, dxp_ref, dwo_ref, dg_ref, db_ref, acc):
        i = pl.program_id(0)

        @pl.when(i == 0)
        def _():
            acc[...] = jnp.zeros_like(acc)
            dg_ref[...] = jnp.zeros_like(dg_ref)
            db_ref[...] = jnp.zeros_like(db_ref)

        ds, dg, db = _ln_bwd(dx1_ref[...], s_ref[...], g_ref[...])
        dg_ref[...] += dg
        db_ref[...] += db
        dxp_ref[...] = ALPHA * ds
        dy = _mm_nt(ds, wo_ref[...])
        dya_ref[...] = dy[:, 0:256]
        dyb_ref[...] = dy[:, 256:512]
        dyc_ref[...] = dy[:, 512:1024]
        acc[0:256] += _mm_tn(ya_ref[...], ds)
        acc[256:512] += _mm_tn(yb_ref[...], ds)
        acc[512:1024] += _mm_tn(yc_ref[...], ds)

        @pl.when(i == nt - 1)
        def _():
            dwo_ref[...] = acc[...].astype(MX)

    row = lambda w_: pl.BlockSpec((tm, w_), lambda i: (i, 0))
    one = pl.BlockSpec((1, D), lambda i: (0, 0))
    full = pl.BlockSpec((D, D), lambda i: (0, 0))
    return pl.pallas_call(
        body, grid=(nt,),
        in_specs=[row(D), row(D), row(256), row(256), row(512), full, one],
        out_specs=[row(256), row(256), row(512), row(D), full, one, one],
        out_shape=[_sds((N, 256)), _sds((N, 256)), _sds((N, 512)), _sds((N, D)), _sds((D, D), MX), _sds((1, D)), _sds((1, D))],
        scratch_shapes=[pltpu.VMEM((D, D), F32)],
        name="outproj_bwd", compiler_params=_cp(("arbitrary",)))(dx1, s1, ya, yb, yc, wo, g)


def _mix_ffn_fwd(ya, yb, yc, x, wo, g1, b1, w1, w2, g, b, target=None):
    tm = FFN_TM
    head = target is not None

    def body(*refs):
        ya_ref, yb_ref, yc_ref, xin_ref, wo_ref, g1_ref, b1_ref, w1_ref, w2_ref, g_ref, b_ref = refs[:11]
        s1_ref, x1_ref, a_ref, s_ref, y_ref = refs[11 + head:16 + head]
        mix = _mm(ya_ref[...], wo_ref[0:256]) + _mm(yb_ref[...], wo_ref[256:512]) + _mm(yc_ref[...], wo_ref[512:1024])
        s1 = ALPHA * xin_ref[...] + mix
        s1_ref[...] = s1
        x = _ln_fwd(s1, g1_ref[...], b1_ref[...])
        x1_ref[...] = x
        xb = x.astype(MX)
        s = ALPHA * x
        for j in range(NSHARD):
            a = _mm(xb, w1_ref[j])
            a_ref[:, j * D:(j + 1) * D] = a.astype(MX)
            s = s + _mm(jnp.square(jnp.maximum(a, 0.0)), w2_ref[j])
        s_ref[...] = s
        x2 = _ln_fwd(s, g_ref[...], b_ref[...])
        if not head:
            y_ref[...] = x2
            return
        l_ref = refs[16 + head]

        @pl.when(pl.program_id(0) == 0)
        def _():
            l_ref[...] = jnp.zeros_like(l_ref)

        e = x2 - refs[11][...]
        y_ref[...] = e * (1.0 / D)
        l_ref[...] += jnp.sum(jnp.sum(e * e, axis=1, keepdims=True), axis=0, keepdims=True) * (0.5 / D)

    rw = lambda w_: pl.BlockSpec((tm, w_), lambda i: (i, 0))
    row = rw(D)
    once = dict(pipeline_mode=pl.Buffered(1))
    wall = pl.BlockSpec((NSHARD, D, D), lambda i: (0, 0, 0), **once)
    one = pl.BlockSpec((1, D), lambda i: (0, 0))
    acc = pl.BlockSpec((8, 128), lambda i: (0, 0))
    return pl.pallas_call(
        body, grid=(N // tm,),
        in_specs=[rw(256), rw(256), rw(512), row, pl.BlockSpec((D, D), lambda i: (0, 0), **once), one, one,
                  wall, wall, one, one] + [row] * head,
        out_specs=[row, row, pl.BlockSpec((tm, DFF), lambda i: (i, 0)), row, row] + [acc] * head,
        out_shape=[_sds((N, D)), _sds((N, D)), _sds((N, DFF), MX), _sds((N, D)), _sds((N, D))] + [_sds((8, 128))] * head,
        name="mix_ffn_fwd", compiler_params=_cp(("arbitrary",), FFN_VMEM))(
            ya, yb, yc, x, wo, g1, b1, w1, w2, g, b, *([target] * head))


def _ffn_bwd_act(dy, s2, a, w1, w2, g):
    tm = FFN_TM

    def body(dy_ref, s_ref, a_ref, w1_ref, w2_ref, g_ref, da_ref, ds_ref, dx1_ref, dg_ref, db_ref):
        @pl.when(pl.program_id(0) == 0)
        def _():
            dg_ref[...] = jnp.zeros_like(dg_ref)
            db_ref[...] = jnp.zeros_like(db_ref)

        ds, dg, db = _ln_bwd(dy_ref[...], s_ref[...], g_ref[...])
        dsb = ds.astype(MX)
        ds_ref[...] = dsb
        dg_ref[...] += dg
        db_ref[...] += db
        dx1 = ALPHA * ds
        for j in range(NSHARD):
            da = (_mm_nt(dsb, w2_ref[j]) * 2.0 * jnp.maximum(a_ref[:, j * D:(j + 1) * D].astype(F32), 0.0)).astype(MX)
            da_ref[:, j * D:(j + 1) * D] = da
            dx1 = dx1 + _mm_nt(da, w1_ref[j])
        dx1_ref[...] = dx1

    row = pl.BlockSpec((tm, D), lambda i: (i, 0))
    wide = pl.BlockSpec((tm, DFF), lambda i: (i, 0))
    wall = pl.BlockSpec((NSHARD, D, D), lambda i: (0, 0, 0))
    one = pl.BlockSpec((1, D), lambda i: (0, 0))
    return pl.pallas_call(
        body, grid=(N // tm,),
        in_specs=[row, row, wide, wall, wall, one],
        out_specs=[wide, row, row, one, one],
        out_shape=[_sds((N, DFF), MX), _sds((N, D), MX), _sds((N, D)), _sds((1, D)), _sds((1, D))],
        name="ffn_bwd_act", compiler_params=_cp(("arbitrary",), FFN_VMEM))(dy, s2, a, w1, w2, g)


def _ffn_bwd_w(x1, da, a, ds):
    tm, nb = FFN_TM_W, FFN_WB
    nt = N // tm

    def body(x_ref, da_ref, a_ref, ds_ref, dw1_ref, dw2_ref, acc1, acc2):
        i = pl.program_id(1)

        @pl.when(i == 0)
        def _():
            acc1[...] = jnp.zeros_like(acc1)
            acc2[...] = jnp.zeros_like(acc2)

        x, ds_ = x_ref[...], ds_ref[...]
        for k in range(nb):
            cols = slice(k * D, (k + 1) * D)
            acc1[k] += _mm_tn(x, da_ref[:, cols])
            acc2[k] += _mm_tn(jnp.square(jnp.maximum(a_ref[:, cols].astype(F32), 0.0)), ds_)

        @pl.when(i == nt - 1)
        def _():
            dw1_ref[...] = acc1[...].astype(MX)
            dw2_ref[...] = acc2[...].astype(MX)

    row = pl.BlockSpec((tm, D), lambda j, i: (i, 0))
    col = pl.BlockSpec((tm, nb * D), lambda j, i: (i, j))
    wj = pl.BlockSpec((nb, D, D), lambda j, i: (j, 0, 0))
    return pl.pallas_call(
        body, grid=(NSHARD // nb, nt),
        in_specs=[row, col, col, row], out_specs=[wj, wj],
        out_shape=[_sds((NSHARD, D, D), MX), _sds((NSHARD, D, D), MX)],
        scratch_shapes=[pltpu.VMEM((nb, D, D), F32), pltpu.VMEM((nb, D, D), F32)],
        name="ffn_bwd_w", compiler_params=_cp(("parallel", "arbitrary"), FFN_VMEM))(x1, da, a, ds)


def _s5_discretize(a_re, a_im, log_step, b_re, b_im):
    lam = lax.complex(a_re, a_im)
    lam_bar = jnp.exp(lam * jnp.exp(log_step))
    b_bar = ((lam_bar - 1.0) / lam)[..., None] * lax.complex(b_re, b_im)
    return jnp.real(lam_bar), jnp.imag(lam_bar), jnp.real(b_bar), jnp.imag(b_bar)


def _s5_in_blocks(b):
    e = jnp.eye(8, dtype=F32)
    return jnp.einsum('ij,zbjph->zbihjp', e, b.reshape(2, 2, 8, S5_P, S5_H)).reshape(2, 2, 128, SW)


def _s5_out_blocks(c):
    e = jnp.eye(8, dtype=F32)
    return jnp.einsum('ij,zbjhp->zbjpih', e, c.reshape(2, 2, 8, S5_H, S5_P)).reshape(2, 2, SW, 128)


def _gate_weight(w_a):
    z = jnp.zeros((16, 128), F32)
    top = jnp.concatenate([w_a[0], z], axis=1)
    bot = jnp.concatenate([z, w_a[1]], axis=1)
    return jnp.concatenate([top, bot, jnp.zeros((96, 256), F32)], axis=0)


def _layer_prep(p):
    lr, li, br, bi = _s5_discretize(p["s5_a_re"], p["s5_a_im"], p["s5_log_step"], p["s5_b_re"], p["s5_b_im"])
    q = dict(p)
    q["bre"] = _s5_in_blocks(br).astype(MX)
    q["bim"] = _s5_in_blocks(bi).astype(MX)
    q["cre"] = _s5_out_blocks(p["s5_c_re"]).astype(MX)
    q["cim"] = _s5_out_blocks(p["s5_c_im"]).astype(MX)
    mr, mi = lr.reshape(2, 1024), li.reshape(2, 1024)
    q["tab"], q["tabc"] = _lockstep_tables(mr, mi)
    q["dsk"] = p["s5_d"].reshape(1, 256)
    q["wa"] = _gate_weight(p["gla_w_a"]).astype(MX)
    q["ba"] = p["gla_b_a"].reshape(1, 256)
    q["lng"] = p["gla_ln_g"].reshape(1, 256)
    q["bv"] = p["s5_b_glu"][:256].reshape(1, 256)
    q["bg"] = p["s5_b_glu"][256:].reshape(1, 256)
    for k in ("ln1_g", "ln1_b", "ln2_g", "ln2_b"):
        q[k] = p[k].reshape(1, D)
    return q


def _layer_fwd(x, q, tk, fetch, target=None):
    q["w_in"] = fetch("w_in", x)
    h, la2 = _inproj_fwd(x, q["w_in"], q["wa"], q["ba"])
    hre, him, y2 = _s5_fwd(h, q["bre"], q["bim"], q["cre"], q["cim"], q["tab"])
    q["w4"] = fetch("s5_w_glu", y2)
    ya = _s5_glu_fwd(y2, h, q["dsk"], q["w4"], q["bv"], q["bg"])
    of, ob, sf, sb = _gla_fwd(h, la2)
    yb = _gla_post_fwd(of, ob, h, q["lng"])
    yc = _swa_fwd(h, tk, q["swa_sink"])
    mixed = ya[:8, :128] + yb[:8, :128] + yc[:8, :128]
    q["w_out"] = fetch("w_out", mixed)
    q["w_ff1"] = fetch("w_ff1", mixed)
    q["w_ff2"] = fetch("w_ff2", mixed)
    s1, x1, a, s2, *out = _mix_ffn_fwd(ya, yb, yc, x, q["w_out"], q["ln1_g"], q["ln1_b"], q["w_ff1"], q["w_ff2"],
                                       q["ln2_g"], q["ln2_b"], target)
    saved = dict(x=x, h=h, hre=hre, him=him, y2=y2, ya=ya, la2=la2, of=of, ob=ob, sf=sf, sb=sb, yb=yb, yc=yc,
                 s1=s1, x1=x1, a=a, s2=s2)
    return (out[0] if target is None else tuple(out)), saved


def _layer_bwd(dy, q, sv, tk, emit):
    g = {}
    da, ds2, dx1, g["dg2"], g["db2"] = _ffn_bwd_act(dy, sv["s2"], sv["a"], q["w_ff1"], q["w_ff2"], q["ln2_g"])
    dw1, dw2 = _ffn_bwd_w(sv["x1"], da, sv["a"], ds2)
    tie = emit(dict(w_ff1=dw1, w_ff2=dw2))
    dya, dyb, dyc, dxp, dwo, g["dg1"], g["db1"] = _outproj_bwd(dx1, sv["s1"], sv["ya"], sv["yb"], sv["yc"],
                                                               q["w_out"], q["ln1_g"] + tie)
    h = sv["h"]
    daq, dakv, g["dsink"] = _swa_bwd(h, tk, q["swa_sink"], dyc)
    do, gr, g["dlng"] = _gla_post_bwd(sv["of"], sv["ob"], h, q["lng"], dyb)
    gq_f, gk_f, gv_f, gl_f, gq_b, gk_b, gv_b, gl_b = _gla_bwd(h, sv["la2"], do, sv["sf"], sv["sb"])
    dhl, g["dwa"], g["dba"] = _gla_gate_bwd(h, q["wa"], q["ba"], gl_f, gl_b)
    dyp, dud, g["dd"], dw4, g["dbv"], g["dbg"] = _s5_glu_bwd(sv["y2"], h, q["dsk"], q["w4"], q["bv"], q["bg"], dya)
    tie = emit(dict(w_out=dwo.reshape(NSHARD, D // NSHARD, D), s5_w_glu=dw4))
    du2, g["dbre"], g["dbim"], g["dcre"], g["dcim"], g["dmu"] = _s5_bwd(
        h, dyp, sv["hre"], sv["him"], q["bre"], q["bim"], q["cre"], q["cim"], (q["tabc"][0], q["tabc"][1] + tie))
    dx, dwt = _inproj_bwd(sv["x"], q["w_in"], dxp, du2, dud, gq_f, gq_b, gk_f, gk_b, gv_f, gv_b, gr, daq, dakv, dhl)
    tie = emit(dict(w_in=dwt))
    return dx, g, tie


NATIVE = ("dmu", "dbre", "dbim", "dcre", "dcim", "dd", "dbv", "dbg", "dwa", "dba", "dlng", "dsink",
          "dg1", "db1", "dg2", "db2", "loss")
ICI_CORE = (0, 0, 0, 1, 1, 0, 0, 0, 1, 1, 1, 1, 0, 0, 1, 1, 0)
Y_FIRST = (0, 0, 1, 0, 1, 1, 0, 1, 0, 1, 0, 1, 0, 1, 0, 1, 0)


def _finish_small(n, w):
    g = {}
    dmu = n["dmu"]
    dlr = dmu[:, :, :, 0].reshape(DEPTH, 2, S5_G, S5_P)
    dli = dmu[:, :, :, 1].reshape(DEPTH, 2, S5_G, S5_P)

    def unblock(c, perm, shape):
        return c.reshape(DEPTH, 2, 2, S5_H, 8, S5_P).transpose(perm).reshape(shape)

    b_shape, c_shape = (DEPTH, 2, S5_G, S5_P, S5_H), (DEPTH, 2, S5_G, S5_H, S5_P)
    _, vjp = jax.vjp(_s5_discretize, w["s5_a_re"], w["s5_a_im"], w["s5_log_step"], w["s5_b_re"], w["s5_b_im"])
    (g["s5_a_re"], g["s5_a_im"], g["s5_log_step"], g["s5_b_re"], g["s5_b_im"]) = vjp(
        (dlr, dli, unblock(n["dbre"], (0, 1, 2, 4, 5, 3), b_shape), unblock(n["dbim"], (0, 1, 2, 4, 5, 3), b_shape)))
    g["s5_c_re"] = unblock(n["dcre"], (0, 1, 2, 4, 3, 5), c_shape)
    g["s5_c_im"] = unblock(n["dcim"], (0, 1, 2, 4, 3, 5), c_shape)
    g["s5_d"] = n["dd"].reshape(DEPTH, S5_G, S5_H)
    g["s5_b_glu"] = jnp.concatenate([n["dbv"], n["dbg"]], axis=2).reshape(DEPTH, 512)
    g["gla_w_a"] = jnp.stack([n["dwa"][:, 0:16, 0:128], n["dwa"][:, 16:32, 128:256]], axis=1)
    g["gla_b_a"] = n["dba"].reshape(DEPTH, 2, 128)
    g["gla_ln_g"] = n["dlng"].reshape(DEPTH, 256)
    g["swa_sink"] = n["dsink"][:, :, 0]
    for k, s in (("ln1_g", "dg1"), ("ln1_b", "db1"), ("ln2_g", "dg2"), ("ln2_b", "db2")):
        g[k] = n[s].reshape(DEPTH, D)
    return g


def _local_step(x, target, qs, tk, fetch, emit):
    saved = []
    for l, q in enumerate(qs):
        x, sv = _layer_fwd(x, q, tk, functools.partial(fetch, l), target if l == DEPTH - 1 else None)
        saved.append(sv)
    dy, lacc = x
    smalls = [None] * DEPTH
    tie = 0.0
    for l in reversed(range(DEPTH)):
        qs[l]["ln2_g"] = qs[l]["ln2_g"] + tie
        dy, smalls[l], tie = _layer_bwd(dy, qs[l], saved[l], tk, functools.partial(emit, l))
    smalls[0]["db2"] = smalls[0]["db2"] + tie
    for l in range(DEPTH):
        smalls[l]["loss"] = lacc if l == 0 else jnp.zeros_like(lacc)
    return lacc[0, 0], dy, smalls


BIG = ("w_in", "s5_w_glu", "w_out", "w_ff1", "w_ff2")
SMALL = ("s5_a_re", "s5_a_im", "s5_log_step", "s5_b_re", "s5_b_im", "s5_c_re", "s5_c_im", "s5_d", "s5_b_glu",
         "gla_w_a", "gla_b_a", "gla_ln_g", "swa_sink", "ln1_g", "ln1_b", "ln2_g", "ln2_b")
ANY = pl.BlockSpec(memory_space=pl.ANY)


def _place():
    x, y, c = lax.axis_index("x"), lax.axis_index("y"), lax.axis_index("c")
    return x, y, c, [(1 - x, y), (x, 1 - y), (1 - x, 1 - y)]


HBM = pl.BlockSpec(memory_space=pltpu.HBM)
SEMS = pl.BlockSpec(memory_space=pltpu.SEMAPHORE)
EFFECT = pltpu.SideEffectType.DATAFLOW_SIDE_EFFECTING


def _push_copies(ins, lands, send, recv, gather, sending):
    x, y, c, chips = _place()
    me = 2 * x + y
    if gather == "sibling":
        return [pltpu.make_async_remote_copy(src_ref=ins[a], dst_ref=lands[a], send_sem=send.at[a], recv_sem=recv.at[a],
                                             device_id=(x, y, 1 - c), device_id_type=MESH) for a in range(len(lands))]
    out = []
    for a in range(len(lands)):
        for j, (px, py) in enumerate(chips):
            peer = 2 * px + py
            src = lands[a].at[me] if gather else ins[a].at[peer if sending else me]
            dst = lands[a].at[me if sending else peer]
            out.append(pltpu.make_async_remote_copy(src_ref=src, dst_ref=dst, send_sem=send.at[3 * a + j],
                                                    recv_sem=recv.at[3 * a + j], device_id=(px, py, c),
                                                    device_id_type=MESH))
    return out


def _push_start(name, arrs, gather):
    n = len(arrs)
    ops = list(arrs) if gather is True else list(arrs) + [lax.empty(s.shape, s.dtype) for s in arrs]
    m = len(ops)

    def body(*refs):
        ins, lnd = (refs[:n], refs[:n]) if gather is True else (refs[:n], refs[n:m])
        for cp in _push_copies(ins, lnd, refs[m], refs[m + 1], gather, True):
            cp.start()
        refs[-1][...] = jnp.zeros((8, 128), F32)

    ops = [pltpu.with_memory_space_constraint(t, pltpu.HBM) for t in ops]
    res = pl.pallas_call(
        body, name=name,
        out_shape=(pltpu.SemaphoreType.DMA((3 * n,)), pltpu.SemaphoreType.DMA((3 * n,)),
                   *[pltpu.HBM(t.shape, t.dtype) for t in ops], _sds((8, 128))),
        in_specs=[HBM] * m,
        out_specs=(SEMS, SEMS, *[HBM] * m, pl.BlockSpec(memory_space=pltpu.VMEM)),
        input_output_aliases={i: 2 + i for i in range(m)},
        compiler_params=pltpu.CompilerParams(has_side_effects=EFFECT))(*ops)
    return res[0], res[1], list(res[2:2 + m]), res[-1]


def _push_wait(name, started, after, gather):
    send, recv, ops, _ = started
    m = len(ops)
    n = m if gather is True else m // 2

    def body(*refs):
        ins, lnd = (refs[:n], refs[:n]) if gather is True else (refs[:n], refs[n:m])
        for cp in _push_copies(ins, lnd, refs[m], refs[m + 1], gather, False):
            cp.wait_send()
            cp.wait_recv()

    res = pl.pallas_call(
        body, name=name,
        out_shape=[pltpu.HBM(t.shape, t.dtype) for t in ops],
        in_specs=[HBM] * m + [SEMS, SEMS, ANY], out_specs=[HBM] * m,
        input_output_aliases={i: i for i in range(m)},
        compiler_params=pltpu.CompilerParams(has_side_effects=EFFECT))(*ops, send, recv, after)
    return list(res)


def _row_tile(rows):
    return max(t for t in range(8, min(rows, 512) + 1, 8) if rows % t == 0)


def _cast_to_slot(me, w, l):
    _, rows, cols = w.shape
    tr = _row_tile(rows)

    def body(me_ref, w_ref, o_ref):
        o_ref[0] = w_ref[0].astype(MX)

    return pl.pallas_call(
        body,
        grid_spec=pltpu.PrefetchScalarGridSpec(
            num_scalar_prefetch=1, grid=(rows // tr,),
            in_specs=[pl.BlockSpec((1, tr, cols), lambda i, me_: (l, i, 0))],
            out_specs=pl.BlockSpec((1, tr, cols), lambda i, me_: (me_[0], i, 0))),
        out_shape=_sds((NSHARD, rows, cols), MX), name="cast_to_slot", compiler_params=_cp(("arbitrary",)))(me, w)


def _sum_sources(me, recv, own):
    _, rows, cols = recv[0].shape
    tr = min(_row_tile(rows), 256) if rows % 256 == 0 else _row_tile(rows)
    nt = rows // tr

    def body(me_ref, *refs):
        o_ref = refs[-1]
        for l in range(DEPTH):
            @pl.when(pl.program_id(0) == l)
            def _():
                r_ref, own_ref = refs[2 * l], refs[2 * l + 1]
                part = [jnp.where(me_ref[0] == s, own_ref[0], r_ref[s]).astype(F32) for s in range(NSHARD)]
                o_ref[...] = ((part[0] + part[1]) + part[2]) + part[3]

    in_specs = []
    for l in range(DEPTH):
        pick = lambda g, i, me_, l=l: jnp.where(g == l, i, jnp.where(g < l, 0, nt - 1))
        in_specs += [pl.BlockSpec((NSHARD, tr, cols), lambda g, i, me_, pick=pick: (0, pick(g, i, me_), 0)),
                     pl.BlockSpec((1, tr, cols), lambda g, i, me_, pick=pick: (me_[0], pick(g, i, me_), 0))]
    return pl.pallas_call(
        body,
        grid_spec=pltpu.PrefetchScalarGridSpec(
            num_scalar_prefetch=1, grid=(DEPTH, nt), in_specs=in_specs,
            out_specs=pl.BlockSpec((tr, cols), lambda g, i, me_: (g * nt + i, 0))),
        out_shape=_sds((DEPTH * rows, cols)), name="sum_sources",
        compiler_params=_cp(("arbitrary", "arbitrary")))(me, *[t for l in range(DEPTH) for t in (recv[l], own[l])])


def _allreduce_small(per_layer):
    nk = len(per_layer[0])
    n = DEPTH * nk
    shapes = [a.shape for a in per_layer[0]]

    def body(*refs):
        ins, outs = refs[:n], refs[n:n + nk]
        sibs, slots = refs[n + nk:n + 2 * nk], refs[n + 2 * nk:n + 3 * nk]
        send, recv = refs[n + 3 * nk:]
        x, y, c, chips = _place()
        me = 2 * x + y
        d2d = [pltpu.make_async_remote_copy(src_ref=ins[l * nk + k], dst_ref=sibs[k].at[l], send_sem=send.at[l * nk + k],
                                            recv_sem=recv.at[l * nk + k], device_id=(x, y, 1 - c), device_id_type=MESH)
               for l in range(DEPTH) for k in range(nk)]
        for cp in d2d:
            cp.start()
        for cp in d2d:
            cp.wait()
        for l in range(DEPTH):
            for k in range(nk):
                slots[k][0, l] = ins[l * nk + k][...] + sibs[k][l]

        def swap(k, stage):
            peer = (1 - x, y, c) if stage == Y_FIRST[k] else (x, 1 - y, c)
            return pltpu.make_async_remote_copy(src_ref=slots[k].at[2 * stage], dst_ref=slots[k].at[2 * stage + 1],
                                                send_sem=send.at[n + 3 * k + stage], recv_sem=recv.at[n + 3 * k + stage],
                                                device_id=peer, device_id_type=MESH)

        def handover(k):
            return pltpu.make_async_remote_copy(src_ref=outs[k], dst_ref=outs[k], send_sem=send.at[n + 3 * nk + k],
                                                recv_sem=recv.at[n + 3 * nk + k], device_id=(x, y, 1 - c),
                                                device_id_type=MESH)

        halves = (tuple(k for k in range(nk) if ICI_CORE[k] == 0), tuple(k for k in range(nk) if ICI_CORE[k] == 1))
        for cc in range(2):
            @pl.when(c == cc)
            def _():
                mine, theirs = halves[cc], halves[1 - cc]
                for stage in range(2):
                    cps = [swap(k, stage) for k in mine]
                    for cp in cps:
                        cp.start()
                    for cp in cps:
                        cp.wait()
                    for k in mine:
                        if stage == 0:
                            slots[k][2] = slots[k][0] + slots[k][1]
                        else:
                            outs[k][...] = slots[k][2] + slots[k][3]
                over = [handover(k) for k in mine]
                for cp in over:
                    cp.start()
                for k in theirs:
                    handover(k).wait_recv()
                for cp in over:
                    cp.wait_send()

    vm = pl.BlockSpec(memory_space=pltpu.VMEM)
    return pl.pallas_call(
        body, in_specs=[vm] * n, out_specs=[vm] * nk, out_shape=[_sds((DEPTH,) + s) for s in shapes],
        scratch_shapes=([pltpu.VMEM((DEPTH,) + s, F32) for s in shapes]
                        + [pltpu.VMEM((NSHARD, DEPTH) + s, F32) for s in shapes]
                        + [pltpu.SemaphoreType.DMA((n + 4 * nk,)), pltpu.SemaphoreType.DMA((n + 4 * nk,))]),
        name="allreduce_small", compiler_params=pltpu.CompilerParams(vmem_limit_bytes=VMEM_LIMIT))(
            *[a for layer in per_layer for a in layer])


def _adamw_math(w, g, m, v):
    m = ADAM_B1 * m + (1.0 - ADAM_B1) * g
    v = ADAM_B2 * v + (1.0 - ADAM_B2) * jnp.square(g)
    m_hat = m / (1.0 - ADAM_B1 ** ADAM_STEP)
    v_hat = v / (1.0 - ADAM_B2 ** ADAM_STEP)
    delta = -ADAM_LR * (m_hat / (jnp.sqrt(v_hat) + ADAM_EPS) + ADAM_WD * w)
    return delta, m, v


def _adamw(g_parts, w, m, v):
    rows, cols = w.shape
    tr = 256 if rows % 256 == 0 else _row_tile(rows)
    k = len(g_parts)

    def body(*refs):
        g = refs[0][...]
        for r in refs[1:k]:
            g = g + r[...]
        w_ref, m_ref, v_ref, go, do, mo, vo = refs[k:]
        d, mn, vn = _adamw_math(w_ref[...], g, m_ref[...], v_ref[...])
        go[...] = g
        do[...] = d
        mo[...] = mn
        vo[...] = vn

    spec = pl.BlockSpec((tr, cols), lambda i: (i, 0))
    return pl.pallas_call(
        body, grid=(rows // tr,), in_specs=[spec] * (k + 3), out_specs=[spec] * 4,
        out_shape=[_sds((rows, cols))] * 4, name="adamw", compiler_params=_cp(("parallel",)))(*g_parts, w, m, v)


def _adamw_small(gs, ws, ms, vs):
    n = len(gs)

    def body(*refs):
        for k in range(n):
            d, mn, vn = _adamw_math(refs[n + k][...], refs[k][...], refs[2 * n + k][...], refs[3 * n + k][...])
            refs[4 * n + k][...] = d
            refs[5 * n + k][...] = mn
            refs[6 * n + k][...] = vn

    vm = pl.BlockSpec(memory_space=pltpu.VMEM)
    shapes = [_sds(a.shape) for a in ws]
    res = pl.pallas_call(
        body, in_specs=[vm] * (4 * n), out_specs=[vm] * (3 * n), out_shape=shapes * 3, name="adamw_small",
        compiler_params=pltpu.CompilerParams(vmem_limit_bytes=VMEM_LIMIT))(*gs, *ws, *ms, *vs)
    return res[:n], res[n:2 * n], res[2 * n:]


_ARGS = ("x", "w_in", "s5_a_re", "s5_a_im", "s5_log_step", "s5_b_re", "s5_b_im", "s5_c_re", "s5_c_im", "s5_d",
         "s5_w_glu", "s5_b_glu", "gla_w_a", "gla_b_a", "gla_ln_g", "swa_sink", "w_out", "ln1_g", "ln1_b", "w_ff1",
         "w_ff2", "ln2_g", "ln2_b")
_WEIGHTS = _ARGS[1:]


def kernel(x, w_in, s5_a_re, s5_a_im, s5_log_step, s5_b_re, s5_b_im, s5_c_re, s5_c_im, s5_d, s5_w_glu, s5_b_glu, gla_w_a, gla_b_a, gla_ln_g, swa_sink, w_out, ln1_g, ln1_b, w_ff1, w_ff2, ln2_g, ln2_b, loss_target, m_w_in, m_s5_a_re, m_s5_a_im, m_s5_log_step, m_s5_b_re, m_s5_b_im, m_s5_c_re, m_s5_c_im, m_s5_d, m_s5_w_glu, m_s5_b_glu, m_gla_w_a, m_gla_b_a, m_gla_ln_g, m_swa_sink, m_w_out, m_ln1_g, m_ln1_b, m_w_ff1, m_w_ff2, m_ln2_g, m_ln2_b, v_w_in, v_s5_a_re, v_s5_a_im, v_s5_log_step, v_s5_b_re, v_s5_b_im, v_s5_c_re, v_s5_c_im, v_s5_d, v_s5_w_glu, v_s5_b_glu, v_gla_w_a, v_gla_b_a, v_gla_ln_g, v_swa_sink, v_w_out, v_ln1_g, v_ln1_b, v_w_ff1, v_w_ff2, v_ln2_g, v_ln2_b):
    given = dict(locals())
    w = {k: given[k] for k in _WEIGHTS}
    mom = {k: given["m_" + k] for k in _WEIGHTS}
    var = {k: given["v_" + k] for k in _WEIGHTS}

    me = (2 * lax.axis_index("x") + lax.axis_index("y")).astype(jnp.int32).reshape(1)
    tr = lambda t: t.transpose(0, 2, 1)
    shard = {k: (tr(w[k]) if k == "w_in" else w[k]) for k in BIG}
    qs = [None] * DEPTH

    first = ("w_in", "s5_w_glu", "w_out")
    follow = {(0, "w_in"): [(0, BIG[3:])], (0, "s5_w_glu"): [(1, first)], (0, "w_ff1"): [(1, BIG[3:])]}
    gathers = {}

    casts = {}

    def start_gather(l, names, behind=None):
        lands = [casts.pop((l, k)) if (l, k) in casts else _cast_to_slot(me, shard[k], l) for k in names]
        if behind is not None:
            lands, behind = lax.optimization_barrier((lands, behind))
        st = _push_start(f"gather_start_{l}_{names[0]}", lands, True)
        for k in names:
            gathers[l, k] = [names, st, None]
        return st[-1], behind

    token = start_gather(0, first[:1])[0] + start_gather(0, first[1:])[0]
    zero = token[0, 0]
    for l in range(DEPTH):
        for k in BIG:
            if (l, k) not in gathers:
                casts[l, k] = _cast_to_slot(me, lax.optimization_barrier((shard[k], token))[0], l)
        qs[l] = _layer_prep({k: (w[k][l] + zero if k == "s5_a_re" else w[k][l]) for k in SMALL})
    token, casts, qs = lax.optimization_barrier((token, casts, qs))

    def fetch(l, name, after):
        names, st, got = gathers[l, name]
        tie = None
        if got is None:
            if l == 0 and name == "w_in":
                after = token
            lands = _push_wait(f"gather_wait_{l}_{names[0]}", st, after, True)
            for l2, names2 in follow.get((l, name), ()):
                tok, lands[0] = start_gather(l2, names2, lands[0])
                tie = tok if tie is None else tie + tok
            got = dict(zip(names, lands))
            for k in names:
                gathers[l, k][2] = got
        full = got[name]
        if name == "w_in":
            return _in_rows(full, token if tie is None else tie)
        if tie is not None:
            near = "bv" if name == "s5_w_glu" else "ln2_b"
            qs[l][near] = qs[l][near] + tie[0, 0]
        return full.reshape(D, D) if name == "w_out" else full

    scatters, held = [], {}

    def emit(l, grads):
        if l > 0:
            held.update(grads)
            if "w_in" not in grads:
                return 0.0
            grads = dict(held)
            held.clear()
        names = tuple(grads)
        st = _push_start(f"scatter_start_{l}_{names[0]}", [grads[k] for k in names], False)
        scatters.append((l, names, st))
        return st[-1][0, 0]

    loss, dx, smalls = _local_step(x.reshape(N, D), loss_target.reshape(N, D), qs, _rope_tables(128), fetch, emit)

    out, recv, own = {}, {}, {}

    def collect(keys, after):
        for l, names, st in scatters:
            if names[0] in keys:
                ops = _push_wait(f"scatter_wait_{l}_{names[0]}", st, after, False)
                for i, k in enumerate(names):
                    own[l, k], recv[l, k] = ops[i], ops[len(names) + i]

    def to_sibling(keys):
        sums = [_sum_sources(me, [recv[l, k] for l in range(DEPTH)], [own[l, k] for l in range(DEPTH)]) for k in keys]
        return _push_start(f"swap_start_{keys[0]}", sums, "sibling")

    def apply(keys, started, after):
        ops = _push_wait(f"swap_wait_{keys[0]}", started, after, "sibling")
        for i, k in enumerate(keys):
            mine, other = ops[i], ops[len(keys) + i]
            shp = shard[k].shape
            r = _adamw([mine, other], *((tr(t[k]) if k == "w_in" else t[k]).reshape(-1, shp[-1]) for t in (w, mom, var)))
            r = [t.reshape(shp) for t in r]
            out[k] = [tr(t) for t in r] if k == "w_in" else r
        return out[keys[-1]][1]

    collect(("w_ff1", "w_ff2", "w_out", "s5_w_glu"), dx)
    ff = to_sibling(("w_ff1", "w_ff2"))
    mix = to_sibling(("w_out", "s5_w_glu"))
    smalls[0]["db1"] = smalls[0]["db1"] + (ff[-1][0, 0] + mix[-1][0, 0])
    native = _allreduce_small([[smalls[l][k] for k in NATIVE] for l in range(DEPTH)])
    native = dict(zip(NATIVE, native))
    loss = native["loss"][0, 0, 0] + native["loss"][1, 0, 0]
    gsmall = _finish_small(native, w)
    view = lambda k, t: t.transpose(0, 1, 2, 4, 3) if k in ("s5_b_re", "s5_b_im") else t
    res = _adamw_small(*([view(k, t[k]) for k in SMALL] for t in (gsmall, w, mom, var)))
    for i, k in enumerate(SMALL):
        out[k] = [gsmall[k]] + [view(k, r[i]) for r in res]
    last = apply(("w_ff1", "w_ff2"), ff, res[0][-1])
    collect(("w_in",), last)
    win = to_sibling(("w_in",))
    last = apply(("w_out", "s5_w_glu"), mix, win[-1])
    apply(("w_in",), win, last)

    return (loss, dx.reshape(NSEQ, L, D), *[out[k][0] for k in _WEIGHTS], *[out[k][1] for k in _WEIGHTS],
            *[out[k][2] for k in _WEIGHTS], *[out[k][3] for k in _WEIGHTS])
```

```python
import functools
import math

import jax
import jax.numpy as jnp
from jax import lax
from jax.experimental import pallas as pl
from jax.experimental.pallas import tpu as pltpu

F32 = jnp.float32
MX = jnp.bfloat16
MESH = pl.DeviceIdType.MESH

DEPTH = 2
NSEQ = 2
L = 2048
N = NSEQ * L
D = 1024
DFF = 4096
NSHARD = 4
S5_G, S5_H, S5_P = 16, 16, 64
GLA_CHUNK = 64
NCHUNK = L // GLA_CHUNK
GLA_GROUP = 4
NGROUP = NCHUNK // GLA_GROUP
SWA_BLK = 128
NBLK = L // SWA_BLK
SWA_PER = 2
ROT = 16
ROPE_THETA = 500000.0
LN_EPS = 1e-5
ALPHA = (2 * DEPTH) ** 0.25
NEG_BIG = -1e30
DIN = 1824
DINP = 1920
ADAM_LR, ADAM_B1, ADAM_B2, ADAM_EPS, ADAM_WD, ADAM_STEP = 0.001, 0.9, 0.999, 1e-08, 0.01, 10
VMEM_LIMIT = 56 * 1024 * 1024
TT = 512
SW = 512
FFN_TM = 512
FFN_TM_W = 1024
FFN_WB = 1
FFN_VMEM = 60 * 1024 * 1024
INPROJ_BWD_TM = 512


def _cp(sem, vmem=VMEM_LIMIT):
    return pltpu.CompilerParams(dimension_semantics=sem, vmem_limit_bytes=vmem)


def _mm(a, b):
    return jnp.dot(a.astype(MX), b.astype(MX), preferred_element_type=F32)


def _mm_nt(a, b):
    return lax.dot_general(a.astype(MX), b.astype(MX), (((1,), (1,)), ((), ())), preferred_element_type=F32)


def _mm_tn(a, b):
    return lax.dot_general(a.astype(MX), b.astype(MX), (((0,), (0,)), ((), ())), preferred_element_type=F32)


@jax.custom_vjp
def _dmm(a, b):
    return _mm(a, b)


_dmm.defvjp(lambda a, b: (_mm(a, b), (a, b)), lambda r, g: (_mm_nt(g, r[1]), _mm_tn(r[0], g)))


@jax.custom_vjp
def _dmm_nt(a, b):
    return _mm_nt(a, b)


_dmm_nt.defvjp(lambda a, b: (_mm_nt(a, b), (a, b)), lambda r, g: (_mm(g, r[1]), _mm_tn(g, r[0])))


@jax.custom_vjp
def _dmm_tn(a, b):
    return _mm_tn(a, b)


_dmm_tn.defvjp(lambda a, b: (_mm_tn(a, b), (a, b)), lambda r, g: (_mm_nt(r[1], g), _mm(r[0], g)))


def _split3(x):
    hi = x.astype(MX)
    r1 = x - hi.astype(F32)
    mid = r1.astype(MX)
    lo = (r1 - mid.astype(F32)).astype(MX)
    return hi, mid, lo


def _chunk_pairs(rows, rev, strict):
    r = lax.broadcasted_iota(jnp.int32, (rows, rows), 0)
    c = lax.broadcasted_iota(jnp.int32, (rows, rows), 1)
    order = ((c > r) if strict else (c >= r)) if rev else ((c < r) if strict else (c <= r))
    return (r // GLA_CHUNK == c // GLA_CHUNK) & order


def _cums_impl(x, rev):
    rows, w = x.shape
    t = jnp.where(_chunk_pairs(rows, rev, False), 1.0, 0.0).astype(MX)
    s = jnp.dot(t, jnp.concatenate(_split3(x), axis=1), preferred_element_type=F32)
    return s[:, 0:w] + s[:, w:2 * w] + s[:, 2 * w:3 * w]


@functools.partial(jax.custom_vjp, nondiff_argnums=(1,))
def _cums(x, rev):
    return _cums_impl(x, rev)


_cums.defvjp(lambda x, rev: (_cums_impl(x, rev), None), lambda rev, r, g: (_cums_impl(g, not rev),))


def _ln_fwd(s, g, b):
    mu = jnp.mean(s, axis=-1, keepdims=True)
    xc = s - mu
    var = jnp.mean(xc * xc, axis=-1, keepdims=True)
    return xc * lax.rsqrt(var + LN_EPS) * g + b


def _ln_bwd(dy, s, g):
    mu = jnp.mean(s, axis=-1, keepdims=True)
    xc = s - mu
    var = jnp.mean(xc * xc, axis=-1, keepdims=True)
    rstd = lax.rsqrt(var + LN_EPS)
    xhat = xc * rstd
    dxh = dy * g
    ds = rstd * (dxh - jnp.mean(dxh, axis=-1, keepdims=True) - xhat * jnp.mean(dxh * xhat, axis=-1, keepdims=True))
    return ds, jnp.sum(dy * xhat, axis=0, keepdims=True), jnp.sum(dy, axis=0, keepdims=True)


def _sds(shape, dtype=F32):
    return jax.ShapeDtypeStruct(shape, dtype)


_IN_ROW_PIECES = (((0, 0), (0, 456)), ((1, 0), (456, 456)), ((2, 0), (912, 112)), ((2, 112), (1792, 32)),
                  ((2, 144), (1024, 312)), ((3, 0), (1336, 456)))


def _in_rows(g4, behind):
    def body(g_ref, behind_ref, o_ref, tmp):
        tmp[DIN:DINP] = jnp.zeros((DINP - DIN, D), F32)
        for (j, s0), (d0, n_) in _IN_ROW_PIECES:
            tmp[d0:d0 + n_] = g_ref[j, s0:s0 + n_].astype(F32)
        o_ref[...] = tmp[...].astype(MX)

    vm = pl.BlockSpec(memory_space=pltpu.VMEM)
    return pl.pallas_call(body, in_specs=[vm, pl.BlockSpec(memory_space=pl.ANY)], out_specs=vm,
                          out_shape=_sds((DINP, D), MX), scratch_shapes=[pltpu.VMEM((DINP, D), F32)], name="in_rows",
                          compiler_params=pltpu.CompilerParams(vmem_limit_bytes=VMEM_LIMIT))(g4, behind)


def _inproj_fwd(x, wt, wa, ba):
    tm = 512

    def body(x_ref, w_ref, wa_ref, ba_ref, h_ref, la_ref):
        h = _mm_nt(x_ref[...], w_ref[...])
        h_ref[...] = h
        la_ref[...] = _logsig(_mm(h[:, DINP - 128:], wa_ref[...]) + ba_ref[...]) * (1.0 / 16.0)

    return pl.pallas_call(
        body, grid=(N // tm,),
        in_specs=[pl.BlockSpec((tm, D), lambda i: (i, 0)), pl.BlockSpec((DINP, D), lambda i: (0, 0)),
                  pl.BlockSpec((128, 256), lambda i: (0, 0)), pl.BlockSpec((1, 256), lambda i: (0, 0))],
        out_specs=[pl.BlockSpec((tm, DINP), lambda i: (i, 0)), pl.BlockSpec((tm, 256), lambda i: (i, 0))],
        out_shape=[_sds((N, DINP)), _sds((N, 256))], name="inproj_fwd", compiler_params=_cp(("parallel",)))(x, wt, wa, ba)


def _inproj_bwd(x, w, dxp, du2, dud, gq_f, gq_b, gk_f, gk_b, gv_f, gv_b, gr, daq, dakv, dhl):
    tm = INPROJ_BWD_TM
    nt = N // tm

    def body(x_ref, w_ref, dxp_ref, du2_ref, dud_ref, gqf, gqb, gkf, gkb, gvf, gvb, gr_ref, daq_ref, dakv_ref, dhl_ref,
             dx_ref, dw_ref, acc):
        i = pl.program_id(0)
        f = lambda r: r[...].astype(F32)
        dh = jnp.concatenate([
            du2_ref[0] + du2_ref[1] + f(dud_ref), f(gqf) + f(gqb), f(gkf) + f(gkb), f(gvf) + f(gvb),
            f(gr_ref), f(daq_ref), f(dakv_ref), f(dhl_ref)], axis=1)
        dx_ref[...] = dxp_ref[...] + _mm(dh, w_ref[...])
        contrib = _mm_tn(dh, x_ref[...])

        @pl.when(i == 0)
        def _():
            acc[...] = contrib

        @pl.when(i > 0)
        def _():
            acc[...] += contrib

        @pl.when(i == nt - 1)
        def _():
            for (j, d0), (s0, n_) in _IN_ROW_PIECES:
                dw_ref[j, d0:d0 + n_] = acc[s0:s0 + n_].astype(MX)

    row = lambda w_: pl.BlockSpec((tm, w_), lambda i: (i, 0))
    return pl.pallas_call(
        body, grid=(nt,),
        in_specs=[row(D), pl.BlockSpec((DINP, D), lambda i: (0, 0)), row(D),
                  pl.BlockSpec((2, tm, 256), lambda i: (0, i, 0)), row(256), row(128), row(128), row(128), row(128),
                  row(256), row(256), row(256), row(512), row(256), row(128)],
        out_specs=[row(D), pl.BlockSpec((NSHARD, DIN // NSHARD, D), lambda i: (0, 0, 0))],
        out_shape=[_sds((N, D)), _sds((NSHARD, DIN // NSHARD, D), MX)],
        scratch_shapes=[pltpu.VMEM((DINP, D), F32)],
        name="inproj_bwd", compiler_params=_cp(("arbitrary",)))(
            x, w, dxp, du2, dud, gq_f, gq_b, gk_f, gk_b, gv_f, gv_b, gr, daq, dakv, dhl)


def _tile_scan(xr, xi, a, cr, ci, reverse):
    for lvl, d in enumerate((1, 2, 4)):
        sh = 8 - d if reverse else d
        sr = pltpu.roll(xr, sh, 0)
        si = pltpu.roll(xi, sh, 0)
        ar, ai = a[2 * lvl], a[2 * lvl + 1]
        xr, xi = xr + ar * sr - ai * si, xi + ar * si + ai * sr
    pr, pi = a[6], a[7]
    return xr + pr * cr - pi * ci, xi + pr * ci + pi * cr


NJ = TT // 8


def _lockstep_tables(mr, mi):
    def body(mr_ref, mi_ref, a_ref, p_ref, ac_ref, pc_ref):
        rowid = lax.broadcasted_iota(jnp.int32, (8, 2 * SW), 0)

        def mul(a, b):
            return a[0] * b[0] - a[1] * b[1], a[0] * b[1] + a[1] * b[0]

        for z in range(2):
            for sign, reverse, a_out, p_out in ((1.0, z == 1, a_ref, p_ref), (-1.0, z == 0, ac_ref, pc_ref)):
                m = (mr_ref[z:z + 1, :], sign * mi_ref[z:z + 1, :])
                pw = [m]
                for _ in range(NJ - 1):
                    pw.append(mul(pw[-1], m))
                n = pw[-1]
                link = [n]
                for _ in range(7):
                    link.append(mul(link[-1], n))
                tiles = [jnp.broadcast_to(m[0], (8, 2 * SW)), jnp.broadcast_to(m[1], (8, 2 * SW))]
                for d in (1, 2, 4):
                    keep = (rowid <= 7 - d) if reverse else (rowid >= d)
                    tiles += [jnp.where(keep, link[d - 1][c], 0.0) for c in range(2)]
                for c in range(2):
                    t = jnp.zeros((8, 2 * SW), F32)
                    for i in range(8):
                        t = jnp.where(rowid == (7 - i if reverse else i), link[i][c], t)
                    tiles.append(t)
                for blk in range(2):
                    lanes = slice(blk * SW, (blk + 1) * SW)
                    for k, t in enumerate(tiles):
                        a_out[z, blk, k] = t[:, lanes]
                    for j in range(NJ):
                        src = pw[NJ - 1 - j] if reverse else pw[j]
                        for c in range(2):
                            p_out[z, blk, c, j:j + 1, :] = src[c][:, lanes]

    vm = pl.BlockSpec(memory_space=pltpu.VMEM)
    a_shape, p_shape = _sds((2, 2, 10, 8, SW)), _sds((2, 2, 2, NJ, SW))
    a, p, ac, pc = pl.pallas_call(body, in_specs=[vm, vm], out_specs=[vm] * 4, out_shape=[a_shape, p_shape] * 2,
                                  name="s5_tables")(mr, mi)
    return (a, p), (ac, pc)


def _to_lockstep(ref, *lead):
    return jnp.concatenate([ref[(*lead, pl.ds(j, 8, stride=NJ), slice(None))] for j in range(NJ)], axis=0)


def _from_lockstep(val, ref, *lead):
    for j in range(NJ):
        ref[(*lead, pl.ds(j, 8, stride=NJ), slice(None))] = val[8 * j:8 * j + 8]


def _expand_powers(p_ref, pexp):
    for c in range(2):
        for j in range(NJ):
            pexp[c, j] = jnp.broadcast_to(p_ref[0, 0, c, j:j + 1, :], (8, SW))


def _lockstep_scan(xre, xim, a_ref, pexp, car, reverse, extra=None):
    a = [a_ref[0, 0, k] for k in range(10)]
    mr, mi = a[0], a[1]
    order = (lambda i: NJ - 1 - i) if reverse else (lambda i: i)

    def local(i, hcar):
        hr, hi = hcar
        r0 = pl.multiple_of(order(i) * 8, 8)
        hr, hi = mr * hr - mi * hi + xre[pl.ds(r0, 8), :], mr * hi + mi * hr + xim[pl.ds(r0, 8), :]
        xre[pl.ds(r0, 8), :] = hr
        xim[pl.ds(r0, 8), :] = hi
        return hr, hi

    z8 = jnp.zeros((8, SW), F32)
    er, ei = lax.fori_loop(0, NJ, local, (z8, z8), unroll=4)
    c0r, c0i = car[0], car[1]
    er, ei = _tile_scan(er, ei, a[2:], c0r, c0i, reverse)
    rowid = lax.broadcasted_iota(jnp.int32, (8, SW), 0)
    first, sh, last = (7, 7, 0) if reverse else (0, 1, 7)
    cvr = jnp.where(rowid == first, c0r, pltpu.roll(er, sh, 0))
    cvi = jnp.where(rowid == first, c0i, pltpu.roll(ei, sh, 0))
    car[0] = jnp.broadcast_to(er[last:last + 1, :], (8, SW))
    car[1] = jnp.broadcast_to(ei[last:last + 1, :], (8, SW))

    def fix(i, carry):
        j = order(i)
        r0 = pl.multiple_of(j * 8, 8)
        pr, pi = pexp[0, j], pexp[1, j]
        sr = xre[pl.ds(r0, 8), :] + pr * cvr - pi * cvi
        si = xim[pl.ds(r0, 8), :] + pr * cvi + pi * cvr
        xre[pl.ds(r0, 8), :] = sr
        xim[pl.ds(r0, 8), :] = si
        if extra is None:
            return carry
        return (sr, si, extra(r0, sr, si, carry[0], carry[1], carry[2]))

    init = (cvr, cvi, extra(None, None, None, None, None, None)) if extra is not None else 0
    return lax.fori_loop(0, NJ, fix, init, unroll=4)


def _s5_time_block(z, s, t, adjoint):
    flip = (1 - z) if adjoint else z
    return s * (L // TT) + t + flip * (L // TT - 1 - 2 * t)


def _s5_fwd(h, bre, bim, cre, cim, tab):
    nt = L // TT
    taba, tabp = tab

    def body(u_ref, bre_ref, bim_ref, cre_ref, cim_ref, a_ref, p_ref, hre_ref, him_ref, y_ref, car, pexp):
        z = pl.program_id(1)
        s = pl.program_id(2)
        tc = pl.program_id(3)

        @pl.when(tc == 0)
        def _():
            car[...] = jnp.zeros_like(car)

        @pl.when((tc == 0) & (s == 0))
        def _():
            _expand_powers(p_ref, pexp)

        u = _to_lockstep(u_ref)
        hre_ref[0] = _mm(u, bre_ref[0, 0])
        him_ref[0] = _mm(u, bim_ref[0, 0])

        @pl.when(z == 0)
        def _():
            _lockstep_scan(hre_ref.at[0], him_ref.at[0], a_ref, pexp, car, False)

        @pl.when(z == 1)
        def _():
            _lockstep_scan(hre_ref.at[0], him_ref.at[0], a_ref, pexp, car, True)

        _from_lockstep(_mm(hre_ref[0], cre_ref[0, 0]) - _mm(him_ref[0], cim_ref[0, 0]), y_ref, 0)

    tb = lambda b, z, s, t: _s5_time_block(z, s, t, False)
    wspec = lambda r, c: pl.BlockSpec((1, 1, r, c), lambda b, z, s, t: (z, b, 0, 0))
    return pl.pallas_call(
        body, grid=(2, 2, NSEQ, nt),
        in_specs=[pl.BlockSpec((TT, 128), lambda b, z, s, t: (tb(b, z, s, t), b)),
                  wspec(128, SW), wspec(128, SW), wspec(SW, 128), wspec(SW, 128),
                  pl.BlockSpec((1, 1, 10, 8, SW), lambda b, z, s, t: (z, b, 0, 0, 0)),
                  pl.BlockSpec((1, 1, 2, NJ, SW), lambda b, z, s, t: (z, b, 0, 0, 0))],
        out_specs=[pl.BlockSpec((1, TT, SW), lambda b, z, s, t: (z, tb(b, z, s, t), b)),
                   pl.BlockSpec((1, TT, SW), lambda b, z, s, t: (z, tb(b, z, s, t), b)),
                   pl.BlockSpec((1, TT, 128), lambda b, z, s, t: (z, tb(b, z, s, t), b))],
        out_shape=[_sds((2, N, 2 * SW)), _sds((2, N, 2 * SW)), _sds((2, N, 256))],
        scratch_shapes=[pltpu.VMEM((2, 8, SW), F32), pltpu.VMEM((2, NJ, 8, SW), F32)],
        name="s5_fwd", compiler_params=_cp(("arbitrary",) * 4))(h, bre, bim, cre, cim, taba, tabp)


def _s5_bwd(h, dyp, hre, him, bre, bim, cre, cim, tabc):
    nt = L // TT
    taba, tabp = tabc

    def body(u_ref, dy_ref, hre_ref, him_ref, bre_ref, bim_ref, cre_ref, cim_ref, a_ref, p_ref,
             du_ref, dbre_ref, dbim_ref, dcre_ref, dcim_ref, dmu_ref, gre, gim, car, acc, macc, pexp):
        z = pl.program_id(1)
        s = pl.program_id(2)
        tc = pl.program_id(3)

        @pl.when(tc == 0)
        def _():
            car[...] = jnp.zeros_like(car)

        @pl.when((tc == 0) & (s == 0))
        def _():
            acc[...] = jnp.zeros_like(acc)
            macc[...] = jnp.zeros_like(macc)
            _expand_powers(p_ref, pexp)

        dy = _to_lockstep(dy_ref)
        gre[...] = _mm_nt(dy, cre_ref[0, 0])
        gim[...] = -_mm_nt(dy, cim_ref[0, 0])

        def run(reverse):
            def pair(r0, gr_, gi_, pvr, pvi, m):
                if r0 is None:
                    return (macc[0], macc[1])
                hr = hre_ref[0, pl.ds(r0, 8), :]
                hi = him_ref[0, pl.ds(r0, 8), :]
                return (m[0] + pvr * hr + pvi * hi, m[1] + pvi * hr - pvr * hi)

            _, _, (dmr, dmi) = _lockstep_scan(gre, gim, a_ref, pexp, car, reverse, pair)
            macc[0] = dmr
            macc[1] = dmi

        @pl.when(z == 0)
        def _():
            run(True)

        @pl.when(z == 1)
        def _():
            run(False)

        gr = gre[...]
        gi = gim[...]
        u = _to_lockstep(u_ref)
        _from_lockstep(_mm_nt(gr, bre_ref[0, 0]) + _mm_nt(gi, bim_ref[0, 0]), du_ref, 0)
        acc[0] += _mm_tn(u, gr)
        acc[1] += _mm_tn(u, gi)
        acc[2] += _mm_tn(dy, hre_ref[0])
        acc[3] -= _mm_tn(dy, him_ref[0])

        @pl.when((tc == nt - 1) & (s == NSEQ - 1))
        def _():
            grp = lax.broadcasted_iota(jnp.int32, (S5_H, SW), 1) // S5_P
            for k, out in enumerate((dbre_ref, dbim_ref, dcre_ref, dcim_ref)):
                c = jnp.zeros((S5_H, SW), F32)
                for i in range(8):
                    c = c + jnp.where(grp == i, acc[k, i * S5_H:(i + 1) * S5_H, :], 0.0)
                out[0, 0] = c
            dmu_ref[0, 0] = jnp.concatenate([jnp.sum(macc[0], axis=0, keepdims=True),
                                             jnp.sum(macc[1], axis=0, keepdims=True)], axis=0)

    tb = lambda b, z, s, t: _s5_time_block(z, s, t, True)
    wspec = lambda r, c: pl.BlockSpec((1, 1, r, c), lambda b, z, s, t: (z, b, 0, 0))
    tok = lambda w_: pl.BlockSpec((TT, w_), lambda b, z, s, t: (tb(b, z, s, t), b))
    st = pl.BlockSpec((1, TT, SW), lambda b, z, s, t: (z, tb(b, z, s, t), b))
    return pl.pallas_call(
        body, grid=(2, 2, NSEQ, nt),
        in_specs=[tok(128), tok(128), st, st, wspec(128, SW), wspec(128, SW), wspec(SW, 128), wspec(SW, 128),
                  pl.BlockSpec((1, 1, 10, 8, SW), lambda b, z, s, t: (z, b, 0, 0, 0)),
                  pl.BlockSpec((1, 1, 2, NJ, SW), lambda b, z, s, t: (z, b, 0, 0, 0))],
        out_specs=[pl.BlockSpec((1, TT, 128), lambda b, z, s, t: (z, tb(b, z, s, t), b)),
                   wspec(S5_H, SW), wspec(S5_H, SW), wspec(S5_H, SW), wspec(S5_H, SW),
                   wspec(2, SW)],
        out_shape=[_sds((2, N, 256))] + [_sds((2, 2, S5_H, SW))] * 4 + [_sds((2, 2, 2, SW))],
        scratch_shapes=[pltpu.VMEM((TT, SW), F32), pltpu.VMEM((TT, SW), F32), pltpu.VMEM((2, 8, SW), F32),
                        pltpu.VMEM((4, 128, SW), F32), pltpu.VMEM((2, 8, SW), F32), pltpu.VMEM((2, NJ, 8, SW), F32)],
        name="s5_bwd", compiler_params=_cp(("arbitrary",) * 4))(h, dyp, hre, him, bre, bim, cre, cim, taba, tabp)


_GELU_C = math.sqrt(2.0 / math.pi)


def _gelu(y):
    return 0.5 * y * (1.0 + jnp.tanh(_GELU_C * (y + 0.044715 * y * y * y)))


def _gelu_grad(y):
    t = jnp.tanh(_GELU_C * (y + 0.044715 * y * y * y))
    return 0.5 * (1.0 + t) + 0.5 * y * (1.0 - t * t) * _GELU_C * (1.0 + 3 * 0.044715 * y * y)


def _glu_halves(w4_ref):
    return (jnp.concatenate([w4_ref[0], w4_ref[1]], axis=1), jnp.concatenate([w4_ref[2], w4_ref[3]], axis=1))


def _s5_glu_fwd(y2, h, dsk, w4, bv, bg):
    tm = 512

    def body(y2_ref, u_ref, d_ref, w4_ref, bv_ref, bg_ref, ya_ref):
        wv, wg = _glu_halves(w4_ref)
        z = _gelu(y2_ref[0] + y2_ref[1] + d_ref[...] * u_ref[...])
        val = _mm(z, wv) + bv_ref[...]
        gate = _mm(z, wg) + bg_ref[...]
        ya_ref[...] = (val * jax.nn.sigmoid(gate)).astype(MX)

    full = lambda r, c: pl.BlockSpec((r, c), lambda i: (0, 0))
    return pl.pallas_call(
        body, grid=(N // tm,),
        in_specs=[pl.BlockSpec((2, tm, 256), lambda i: (0, i, 0)), pl.BlockSpec((tm, 256), lambda i: (i, 0)),
                  full(1, 256), pl.BlockSpec((NSHARD, 256, 128), lambda i: (0, 0, 0)), full(1, 256), full(1, 256)],
        out_specs=pl.BlockSpec((tm, 256), lambda i: (i, 0)),
        out_shape=_sds((N, 256), MX), name="s5_glu_fwd", compiler_params=_cp(("parallel",)))(y2, h, dsk, w4, bv, bg)


def _s5_glu_bwd(y2, h, dsk, w4, bv, bg, dya):
    tm = 512
    nt = N // tm

    def body(y2_ref, u_ref, d_ref, w4_ref, bv_ref, bg_ref, dya_ref,
             dyp_ref, dud_ref, dd_ref, dw4_ref, dbv_ref, dbg_ref, accv, accg):
        i = pl.program_id(0)

        @pl.when(i == 0)
        def _():
            for r in (dd_ref, accv, accg, dbv_ref, dbg_ref):
                r[...] = jnp.zeros_like(r)

        wv, wg = _glu_halves(w4_ref)
        u = u_ref[...]
        y = y2_ref[0] + y2_ref[1] + d_ref[...] * u
        z = _gelu(y)
        val = _mm(z, wv) + bv_ref[...]
        sig = jax.nn.sigmoid(_mm(z, wg) + bg_ref[...])
        dya = dya_ref[...]
        dval = dya * sig
        dgate = dya * val * sig * (1.0 - sig)
        dz = _mm_nt(dval, wv) + _mm_nt(dgate, wg)
        dy = dz * _gelu_grad(y)
        dyp_ref[...] = dy
        dud_ref[...] = (dy * d_ref[...]).astype(MX)
        dd_ref[...] += jnp.sum(dy * u, axis=0, keepdims=True)
        accv[...] += _mm_tn(z, dval)
        accg[...] += _mm_tn(z, dgate)
        dbv_ref[...] += jnp.sum(dval, axis=0, keepdims=True)
        dbg_ref[...] += jnp.sum(dgate, axis=0, keepdims=True)

        @pl.when(i == nt - 1)
        def _():
            dw4_ref[0] = accv[:, 0:128].astype(MX)
            dw4_ref[1] = accv[:, 128:256].astype(MX)
            dw4_ref[2] = accg[:, 0:128].astype(MX)
            dw4_ref[3] = accg[:, 128:256].astype(MX)

    full = lambda r, c: pl.BlockSpec((r, c), lambda i: (0, 0))
    row = pl.BlockSpec((tm, 256), lambda i: (i, 0))
    wspec = pl.BlockSpec((NSHARD, 256, 128), lambda i: (0, 0, 0))
    return pl.pallas_call(
        body, grid=(nt,),
        in_specs=[pl.BlockSpec((2, tm, 256), lambda i: (0, i, 0)), row, full(1, 256), wspec, full(1, 256), full(1, 256),
                  row],
        out_specs=[row, row, full(1, 256), wspec, full(1, 256), full(1, 256)],
        out_shape=[_sds((N, 256)), _sds((N, 256), MX), _sds((1, 256)), _sds((NSHARD, 256, 128), MX), _sds((1, 256)),
                   _sds((1, 256))],
        scratch_shapes=[pltpu.VMEM((256, 256), F32), pltpu.VMEM((256, 256), F32)],
        name="s5_glu_bwd", compiler_params=_cp(("arbitrary",)))(y2, h, dsk, w4, bv, bg, dya)


def _logsig(x):
    return jnp.minimum(x, 0.0) - jnp.log(1.0 + jnp.exp(-jnp.abs(x)))


def _gla_gate_bwd(h, wa, ba, dla_f, dla_b):
    tm = 512

    def body(hl_ref, wa_ref, ba_ref, df_ref, db_ref, dhl_ref, dwa_ref, dba_ref):
        i = pl.program_id(0)

        @pl.when(i == 0)
        def _():
            dwa_ref[...] = jnp.zeros_like(dwa_ref)
            dba_ref[...] = jnp.zeros_like(dba_ref)

        hl = hl_ref[...]
        pre = _mm(hl, wa_ref[...]) + ba_ref[...]
        dpre = jnp.concatenate([df_ref[...], db_ref[...]], axis=1) * (1.0 / 16.0) * jax.nn.sigmoid(-pre)
        dhl_ref[...] = _mm_nt(dpre, wa_ref[...]).astype(MX)
        dwa_ref[...] += _mm_tn(hl, dpre)[0:32]
        dba_ref[...] += jnp.sum(dpre, axis=0, keepdims=True)

    row = pl.BlockSpec((tm, 128), lambda i: (i, 0))
    return pl.pallas_call(
        body, grid=(N // tm,),
        in_specs=[pl.BlockSpec((tm, 128), lambda i: (i, 14)), pl.BlockSpec((128, 256), lambda i: (0, 0)),
                  pl.BlockSpec((1, 256), lambda i: (0, 0)), row, row],
        out_specs=[row, pl.BlockSpec((32, 256), lambda i: (0, 0)), pl.BlockSpec((1, 256), lambda i: (0, 0))],
        out_shape=[_sds((N, 128), MX), _sds((32, 256)), _sds((1, 256))],
        name="gla_gate_bwd", compiler_params=_cp(("arbitrary",)))(h, wa, ba, dla_f, dla_b)


def _gla_chunk(q, k, v, la, st, rev):
    c = GLA_CHUNK
    rows = q.shape[0]
    nch = rows // c
    b = _cums(la, rev)
    blc = [jnp.sum(la[i * c:(i + 1) * c], axis=0, keepdims=True) for i in range(nch)]
    bl = jnp.concatenate([jnp.broadcast_to(t, (c, 128)) for t in blc], axis=0)
    q_in = q * (32.0 ** -0.5) * jnp.exp(b)
    k_in = k * jnp.exp(-b)
    k_st = k * jnp.exp(bl - b)
    lane_k = lax.broadcasted_iota(jnp.int32, (1, 128), 1) // 32
    lane_v = lax.broadcasted_iota(jnp.int32, (1, 256), 1) // 64
    qs = jnp.concatenate([jnp.where(lane_k == hd, q_in, 0.0) for hd in range(4)], axis=0)
    a = _dmm_nt(qs, k_in)
    a = jnp.where(jnp.concatenate([_chunk_pairs(rows, rev, rev)] * 4, axis=0), a, 0.0)
    o4 = _dmm(a, v)
    o = jnp.zeros((rows, 256), F32)
    for hd in range(4):
        o = o + jnp.where(lane_v == hd, o4[hd * rows:(hd + 1) * rows], 0.0)
    bd = (lax.broadcasted_iota(jnp.int32, (256, 128), 0) // 64) == (lax.broadcasted_iota(jnp.int32, (256, 128), 1) // 32)
    inter = [None] * nch
    for i in (reversed(range(nch)) if rev else range(nch)):
        sl = slice(i * c, (i + 1) * c)
        inter[i] = _dmm_nt(q_in[sl], st)
        st = jnp.exp(blc[i]) * st + jnp.where(bd, _dmm_tn(v[sl], k_st[sl]), 0.0)
    return o + jnp.concatenate(inter, axis=0), st


def _gla_chunk_of(c, rev):
    return NGROUP - 1 - c if rev else c


def _gla_fwd(h, la2):
    c = GLA_GROUP * GLA_CHUNK

    def body(qf, kf, vf, laf, qb, kb, vb, lab, of_ref, ob_ref, sf_ref, sb_ref, stf, stb):
        @pl.when(pl.program_id(0) == 0)
        def _():
            stf[...] = jnp.zeros_like(stf)
            stb[...] = jnp.zeros_like(stb)

        ins = [(qf[s], kf[s], vf[s], laf[s], stf[s], qb[s], kb[s], vb[s], lab[s], stb[s]) for s in range(NSEQ)]
        outs = [(_gla_chunk(*t[:5], False), _gla_chunk(*t[5:], True)) for t in ins]
        for s in range(NSEQ):
            sf_ref[s, 0] = ins[s][4]
            sb_ref[s, 0] = ins[s][9]
            (of_ref[s], stf[s]), (ob_ref[s], stb[s]) = outs[s]

    def specs(rev):
        ch = lambda i: _gla_chunk_of(i, rev)
        return [pl.BlockSpec((NSEQ, c, 128), lambda i: (0, ch(i), 2)), pl.BlockSpec((NSEQ, c, 128), lambda i: (0, ch(i), 3)),
                pl.BlockSpec((NSEQ, c, 256), lambda i: (0, ch(i), 2)),
                pl.BlockSpec((NSEQ, c, 128), lambda i: (0, ch(i), 1 if rev else 0))]

    orow = lambda rev: pl.BlockSpec((NSEQ, c, 256), lambda i: (0, _gla_chunk_of(i, rev), 0))
    srow = lambda rev: pl.BlockSpec((NSEQ, 1, 256, 128), lambda i: (0, _gla_chunk_of(i, rev), 0, 0))
    h3, la3 = h.reshape(NSEQ, L, DINP), la2.reshape(NSEQ, L, 256)
    of, ob, sf, sb = pl.pallas_call(
        body, grid=(NGROUP,),
        in_specs=specs(False) + specs(True),
        out_specs=[orow(False), orow(True), srow(False), srow(True)],
        out_shape=[_sds((NSEQ, L, 256)), _sds((NSEQ, L, 256)), _sds((NSEQ, NGROUP, 256, 128)),
                   _sds((NSEQ, NGROUP, 256, 128))],
        scratch_shapes=[pltpu.VMEM((NSEQ, 256, 128), F32), pltpu.VMEM((NSEQ, 256, 128), F32)],
        name="gla_fwd", compiler_params=_cp(("arbitrary",)))(h3, h3, h3, la3, h3, h3, h3, la3)
    return of.reshape(N, 256), ob.reshape(N, 256), sf, sb


def _gla_bwd(h, la2, do, sf, sb):
    c = GLA_GROUP * GLA_CHUNK

    def body(qf, kf, vf, laf, dof, sfr, qb, kb, vb, lab, dob, sbr,
             dqf, dkf, dvf, dlf, dqb, dkb, dvb, dlb, dstf, dstb):
        @pl.when(pl.program_id(0) == 0)
        def _():
            dstf[...] = jnp.zeros_like(dstf)
            dstb[...] = jnp.zeros_like(dstb)

        def one(s, q, k, v, la, do_, st, dst, rev):
            _, vjp = jax.vjp(functools.partial(_gla_chunk, rev=rev), q[s], k[s], v[s], la[s], st[s, 0])
            return vjp((do_[s], dst[s]))

        res = [(one(s, qf, kf, vf, laf, dof, sfr, dstf, False), one(s, qb, kb, vb, lab, dob, sbr, dstb, True))
               for s in range(NSEQ)]
        for s in range(NSEQ):
            for (gq, gk, gv, gl, gs), (dq, dk, dv, dl, dst) in ((res[s][0], (dqf, dkf, dvf, dlf, dstf)),
                                                                  (res[s][1], (dqb, dkb, dvb, dlb, dstb))):
                dq[s], dk[s], dv[s] = gq.astype(MX), gk.astype(MX), gv.astype(MX)
                dl[s], dst[s] = gl, gs

    def specs(rev):
        ch = lambda i: _gla_chunk_of(i, not rev)
        return [pl.BlockSpec((NSEQ, c, 128), lambda i: (0, ch(i), 2)), pl.BlockSpec((NSEQ, c, 128), lambda i: (0, ch(i), 3)),
                pl.BlockSpec((NSEQ, c, 256), lambda i: (0, ch(i), 2)),
                pl.BlockSpec((NSEQ, c, 128), lambda i: (0, ch(i), 1 if rev else 0)),
                pl.BlockSpec((NSEQ, c, 256), lambda i: (0, ch(i), 0)),
                pl.BlockSpec((NSEQ, 1, 256, 128), lambda i: (0, ch(i), 0, 0))]

    def ospecs(rev):
        ch = lambda i: _gla_chunk_of(i, not rev)
        n = pl.BlockSpec((NSEQ, c, 128), lambda i: (0, ch(i), 0))
        return [n, n, pl.BlockSpec((NSEQ, c, 256), lambda i: (0, ch(i), 0)), n]

    oshape = [_sds((NSEQ, L, 128), MX), _sds((NSEQ, L, 128), MX), _sds((NSEQ, L, 256), MX), _sds((NSEQ, L, 128))]
    h3, la3, do3 = h.reshape(NSEQ, L, DINP), la2.reshape(NSEQ, L, 256), do.reshape(NSEQ, L, 256)
    res = pl.pallas_call(
        body, grid=(NGROUP,),
        in_specs=specs(False) + specs(True),
        out_specs=ospecs(False) + ospecs(True),
        out_shape=oshape + oshape,
        scratch_shapes=[pltpu.VMEM((NSEQ, 256, 128), F32), pltpu.VMEM((NSEQ, 256, 128), F32)],
        name="gla_bwd", compiler_params=_cp(("arbitrary",)))(h3, h3, h3, la3, do3, sf, h3, h3, h3, la3, do3, sb)
    return [r.reshape(N, r.shape[-1]) for r in res]


def _gla_post(of, ob, r, g):
    o = of + ob
    head = lax.broadcasted_iota(jnp.int32, (1, 256), 1) // 64
    mu = jnp.zeros_like(o)
    for hd in range(4):
        mu = mu + jnp.where(head == hd, jnp.sum(jnp.where(head == hd, o, 0.0), axis=-1, keepdims=True) * (1.0 / 64.0), 0.0)
    xc = o - mu
    var = jnp.zeros_like(o)
    for hd in range(4):
        var = var + jnp.where(head == hd, jnp.sum(jnp.where(head == hd, xc * xc, 0.0), axis=-1, keepdims=True) * (1.0 / 64.0), 0.0)
    return xc * lax.rsqrt(var + LN_EPS) * g * (r * jax.nn.sigmoid(r))


def _gla_post_fwd(of, ob, h, g):
    tm = 512

    def body(of_ref, ob_ref, r_ref, g_ref, y_ref):
        y_ref[...] = _gla_post(of_ref[...], ob_ref[...], r_ref[...], g_ref[...]).astype(MX)

    row = pl.BlockSpec((tm, 256), lambda i: (i, 0))
    return pl.pallas_call(
        body, grid=(N // tm,),
        in_specs=[row, row, pl.BlockSpec((tm, 256), lambda i: (i, 3)), pl.BlockSpec((1, 256), lambda i: (0, 0))],
        out_specs=row, out_shape=_sds((N, 256), MX), name="gla_post_fwd", compiler_params=_cp(("parallel",)))(of, ob, h, g)


def _gla_post_bwd(of, ob, h, g, dyb):
    tm = 512

    def body(of_ref, ob_ref, r_ref, g_ref, dy_ref, do_ref, dr_ref, dg_ref):
        @pl.when(pl.program_id(0) == 0)
        def _():
            dg_ref[...] = jnp.zeros_like(dg_ref)

        _, vjp = jax.vjp(_gla_post, of_ref[...], ob_ref[...], r_ref[...], g_ref[...])
        go, _, gr, gg = vjp(dy_ref[...])
        do_ref[...] = go
        dr_ref[...] = gr.astype(MX)
        dg_ref[...] += gg

    row = pl.BlockSpec((tm, 256), lambda i: (i, 0))
    one = pl.BlockSpec((1, 256), lambda i: (0, 0))
    return pl.pallas_call(
        body, grid=(N // tm,),
        in_specs=[row, row, pl.BlockSpec((tm, 256), lambda i: (i, 3)), one, row],
        out_specs=[row, row, one], out_shape=[_sds((N, 256)), _sds((N, 256), MX), _sds((1, 256))],
        name="gla_post_bwd", compiler_params=_cp(("arbitrary",)))(of, ob, h, g, dyb)


def _rope_tables(width):
    pos = jnp.arange(L, dtype=F32)
    inv_freq = ROPE_THETA ** (-jnp.arange(0, ROT, 2, dtype=F32) / ROT)
    ang = pos[:, None] * inv_freq[None, :]
    cos, sin = jnp.cos(ang), jnp.sin(ang)
    one = jnp.ones((L, 64 - ROT), F32)
    zero = jnp.zeros((L, 64 - ROT), F32)
    z8 = jnp.zeros((L, ROT // 2), F32)
    c = jnp.concatenate([cos, cos, one], axis=1)
    sa = jnp.concatenate([z8, sin, zero], axis=1)
    sb = jnp.concatenate([-sin, z8, zero], axis=1)
    rep = width // 64
    return jnp.stack([jnp.tile(c, (1, rep)), jnp.tile(sa, (1, rep)), jnp.tile(sb, (1, rep))])


def _pieces(t, f):
    out = [f(t[:, c * 128:(c + 1) * 128]) for c in range(t.shape[-1] // 128)]
    return out[0] if len(out) == 1 else jnp.concatenate(out, axis=1)


def _rope(t, tab):
    return _pieces(t, lambda x: x * tab[0] + pltpu.roll(x, ROT // 2, 1) * tab[1] + pltpu.roll(x, 128 - ROT // 2, 1) * tab[2])


def _rope_t(g, tab):
    return _pieces(g, lambda x: x * tab[0] + pltpu.roll(x * tab[1], 128 - ROT // 2, 1) + pltpu.roll(x * tab[2], ROT // 2, 1))


def _swa_pad_kv(kv_ref, tk_ref, kexp, vexp):
    z = jnp.zeros((SWA_BLK, 256), F32)
    kr = _rope(kv_ref[:, 0:128], tk_ref[...])
    for hk in range(2):
        for pad in (kexp, vexp):
            pad[hk, 0:SWA_BLK] = z
            pad[hk, SWA_BLK + L:] = z
        kexp[hk, SWA_BLK:SWA_BLK + L] = _swa_expand(kr, hk)
        vexp[hk, SWA_BLK:SWA_BLK + L] = _swa_expand(kv_ref[:, 128:256], hk)


def _swa_expand(x, hk):
    lane = lax.broadcasted_iota(jnp.int32, x.shape, 1)
    sw = pltpu.roll(x, 64, 1)
    pair = jnp.where(lane < 64, x, sw) if hk == 0 else jnp.where(lane < 64, sw, x)
    return jnp.concatenate([pair, pair], axis=1)


def _swa_fold(x, hk):
    a = x[:, 0:128] + x[:, 128:256]
    t = a + pltpu.roll(a, 64, 1)
    lane = lax.broadcasted_iota(jnp.int32, a.shape, 1)
    return jnp.where((lane < 64) if hk == 0 else (lane >= 64), t, 0.0)


def _swa_probs(q2, kexp, n, sink_ref, hk):
    slot = lax.broadcasted_iota(jnp.int32, (1, 256), 1) // 64
    qs = jnp.concatenate([jnp.where(slot == g, q2, 0.0) for g in range(4)], axis=0)
    s = _mm_nt(qs, kexp) * 0.125
    i = lax.broadcasted_iota(jnp.int32, (SWA_BLK, 3 * SWA_BLK), 0)
    j = lax.broadcasted_iota(jnp.int32, (SWA_BLK, 3 * SWA_BLK), 1)
    kpos = n * SWA_BLK - SWA_BLK + j
    ok = (j - i >= 0) & (j - i <= 2 * SWA_BLK) & (kpos >= 0) & (kpos < L)
    s = jnp.where(jnp.concatenate([ok] * 4, axis=0), s, NEG_BIG)
    rowg = lax.broadcasted_iota(jnp.int32, (4 * SWA_BLK, 1), 0) // SWA_BLK
    sink = jnp.zeros((4 * SWA_BLK, 1), F32)
    for g in range(4):
        sink = jnp.where(rowg == g, sink_ref[hk * 4 + g], sink)
    m = jnp.maximum(jnp.max(s, axis=-1, keepdims=True), sink)
    p = jnp.exp(s - m)
    ps = jnp.exp(sink - m)
    inv = 1.0 / (jnp.sum(p, axis=-1, keepdims=True) + ps)
    return qs, p * inv, ps * inv, slot, rowg


def _swa_qtab(tk_ref, r0):
    return [tk_ref[i, pl.ds(r0, SWA_BLK), :] for i in range(3)]


def _swa_fwd(h, tk, sink):
    def body(sink_ref, q_ref, kv_ref, tk_ref, y_ref, kexp, vexp):
        n = pl.program_id(1)

        @pl.when(n == 0)
        def _():
            _swa_pad_kv(kv_ref, tk_ref, kexp, vexp)

        for t in range(SWA_PER):
            blk = n * SWA_PER + t
            rows = slice(t * SWA_BLK, (t + 1) * SWA_BLK)
            r0 = pl.multiple_of(blk * SWA_BLK, SWA_BLK)
            q = _rope(q_ref[rows, :], _swa_qtab(tk_ref, r0))
            for hk in range(2):
                _, p, _, slot, _ = _swa_probs(q[:, hk * 256:(hk + 1) * 256], kexp[hk, pl.ds(r0, 3 * SWA_BLK), :], blk,
                                              sink_ref, hk)
                o4 = _mm(p, vexp[hk, pl.ds(r0, 3 * SWA_BLK), :])
                o = jnp.zeros((SWA_BLK, 256), F32)
                for g in range(4):
                    o = o + jnp.where(slot == g, o4[g * SWA_BLK:(g + 1) * SWA_BLK], 0.0)
                y_ref[rows, hk * 256:(hk + 1) * 256] = o.astype(MX)

    tm = SWA_PER * SWA_BLK
    return pl.pallas_call(
        body,
        grid_spec=pltpu.PrefetchScalarGridSpec(
            num_scalar_prefetch=1, grid=(NSEQ, L // tm),
            in_specs=[pl.BlockSpec((tm, 512), lambda s, n, sk: (s * (L // tm) + n, 2)),
                      pl.BlockSpec((L, 256), lambda s, n, sk: (s, 6)),
                      pl.BlockSpec((3, L, 128), lambda s, n, sk: (0, 0, 0))],
            out_specs=pl.BlockSpec((tm, 512), lambda s, n, sk: (s * (L // tm) + n, 0)),
            scratch_shapes=[pltpu.VMEM((2, L + 2 * SWA_BLK, 256), F32), pltpu.VMEM((2, L + 2 * SWA_BLK, 256), F32)]),
        out_shape=_sds((N, 512), MX), name="swa_fwd", compiler_params=_cp(("arbitrary", "arbitrary")))(sink, h, h, tk)


def _swa_bwd(h, tk, sink, dyc):
    tm = SWA_PER * SWA_BLK

    def body(sink_ref, q_ref, kv_ref, tk_ref, dy_ref, dq_ref, dkv_ref, dsink_ref, kexp_all, vexp_all, dkacc, dvacc):
        sq = pl.program_id(0)
        n = pl.program_id(1)

        @pl.when(n == 0)
        def _():
            _swa_pad_kv(kv_ref, tk_ref, kexp_all, vexp_all)
            dkacc[...] = jnp.zeros_like(dkacc)
            dvacc[...] = jnp.zeros_like(dvacc)

        @pl.when((n == 0) & (sq == 0))
        def _():
            dsink_ref[...] = jnp.zeros_like(dsink_ref)

        hrow = lax.broadcasted_iota(jnp.int32, (8, 128), 0)
        dsk = jnp.zeros((8, 128), F32)
        for t in range(SWA_PER):
            blk = n * SWA_PER + t
            rows = slice(t * SWA_BLK, (t + 1) * SWA_BLK)
            r0 = pl.multiple_of(blk * SWA_BLK, SWA_BLK)
            tq = _swa_qtab(tk_ref, r0)
            q = _rope(q_ref[rows, :], tq)
            for hk in range(2):
                kexp = kexp_all[hk, pl.ds(r0, 3 * SWA_BLK), :]
                vexp = vexp_all[hk, pl.ds(r0, 3 * SWA_BLK), :]
                qs, p, ps, slot, rowg = _swa_probs(q[:, hk * 256:(hk + 1) * 256], kexp, blk, sink_ref, hk)
                dy2 = dy_ref[rows, hk * 256:(hk + 1) * 256]
                dos = jnp.concatenate([jnp.where(slot == g, dy2, 0.0) for g in range(4)], axis=0)
                dp = _mm_nt(dos, vexp)
                delta = jnp.sum(p * dp, axis=-1, keepdims=True)
                ds = p * (dp - delta) * 0.125
                dsr = -ps * delta
                for g in range(4):
                    dsk = dsk + jnp.where(hrow == hk * 4 + g,
                                          jnp.sum(jnp.where(rowg == g, dsr, 0.0), axis=0, keepdims=True), 0.0)
                dq4 = _mm(ds, kexp)
                dq2 = jnp.zeros((SWA_BLK, 256), F32)
                for g in range(4):
                    dq2 = dq2 + jnp.where(slot == g, dq4[g * SWA_BLK:(g + 1) * SWA_BLK], 0.0)
                dq_ref[rows, hk * 256:(hk + 1) * 256] = _rope_t(dq2, tq).astype(MX)
                dkacc[hk, pl.ds(r0, 3 * SWA_BLK), :] += _mm_tn(ds, qs)
                dvacc[hk, pl.ds(r0, 3 * SWA_BLK), :] += _mm_tn(p, dos)
        dsink_ref[...] += dsk

        @pl.when(n == L // tm - 1)
        def _():
            seq = slice(SWA_BLK, SWA_BLK + L)
            dk = _rope_t(_swa_fold(dkacc[0, seq], 0) + _swa_fold(dkacc[1, seq], 1), tk_ref[...])
            dkv_ref[:, 0:128] = dk.astype(MX)
            dkv_ref[:, 128:256] = (_swa_fold(dvacc[0, seq], 0) + _swa_fold(dvacc[1, seq], 1)).astype(MX)

    blk = lambda col: pl.BlockSpec((tm, 512), lambda s, n, sk: (s * (L // tm) + n, col))
    pad = pltpu.VMEM((2, L + 2 * SWA_BLK, 256), F32)
    return pl.pallas_call(
        body,
        grid_spec=pltpu.PrefetchScalarGridSpec(
            num_scalar_prefetch=1, grid=(NSEQ, L // tm),
            in_specs=[blk(2), pl.BlockSpec((L, 256), lambda s, n, sk: (s, 6)),
                      pl.BlockSpec((3, L, 128), lambda s, n, sk: (0, 0, 0)), blk(0)],
            out_specs=[blk(0), pl.BlockSpec((L, 256), lambda s, n, sk: (s, 0)),
                       pl.BlockSpec((8, 128), lambda s, n, sk: (0, 0))],
            scratch_shapes=[pad, pad, pad, pad]),
        out_shape=[_sds((N, 512), MX), _sds((N, 256), MX), _sds((8, 128))],
        name="swa_bwd", compiler_params=_cp(("arbitrary", "arbitrary")))(sink, h, h, tk, dyc)


def _outproj_bwd(dx1, s1, ya, yb, yc, wo, g):
    tm = 512
    nt = N // tm

    def body(dx1_ref, s_ref, ya_ref, yb_ref, yc_ref, wo_ref, g_ref,
             dya_ref, dyb_ref, dyc_ref, dxp_ref, dwo_ref, dg_ref, db_ref, acc):
        i = pl.program_id(0)

        @pl.when(i == 0)
        def _():
            acc[...] = jnp.zeros_like(acc)
            dg_ref[...] = jnp.zeros_like(dg_ref)
            db_ref[...] = jnp.zeros_like(db_ref)

        ds, dg, db = _ln_bwd(dx1_ref[...], s_ref[...], g_ref[...])
        dg_ref[...] += dg
        db_ref[...] += db
        dxp_ref[...] = ALPHA * ds
        dy = _mm_nt(ds, wo_ref[...])
        dya_ref[...] = dy[:, 0:256]
        dyb_ref[...] = dy[:, 256:512]
        dyc_ref[...] = dy[:, 512:1024]
        acc[0:256] += _mm_tn(ya_ref[...], ds)
        acc[256:512] += _mm_tn(yb_ref[...], ds)
        acc[512:1024] += _mm_tn(yc_ref[...], ds)

        @pl.when(i == nt - 1)
        def _():
            dwo_ref[...] = acc[...].astype(MX)

    row = lambda w_: pl.BlockSpec((tm, w_), lambda i: (i, 0))
    one = pl.BlockSpec((1, D), lambda i: (0, 0))
    full = pl.BlockSpec((D, D), lambda i: (0, 0))
    return pl.pallas_call(
        body, grid=(nt,),
        in_specs=[row(D), row(D), row(256), row(256), row(512), full, one],
        out_specs=[row(256), row(256), row(512), row(D), full, one, one],
        out_shape=[_sds((N, 256)), _sds((N, 256)), _sds((N, 512)), _sds((N, D)), _sds((D, D), MX), _sds((1, D)), _sds((1, D))],
        scratch_shapes=[pltpu.VMEM((D, D), F32)],
        name="outproj_bwd", compiler_params=_cp(("arbitrary",)))(dx1, s1, ya, yb, yc, wo, g)


def _mix_ffn_fwd(ya, yb, yc, x, wo, g1, b1, w1, w2, g, b, target=None):
    tm = FFN_TM
    head = target is not None

    def body(*refs):
        ya_ref, yb_ref, yc_ref, xin_ref, wo_ref, g1_ref, b1_ref, w1_ref, w2_ref, g_ref, b_ref = refs[:11]
        s1_ref, x1_ref, a_ref, s_ref, y_ref = refs[11 + head:16 + head]
        mix = _mm(ya_ref[...], wo_ref[0:256]) + _mm(yb_ref[...], wo_ref[256:512]) + _mm(yc_ref[...], wo_ref[512:1024])
        s1 = ALPHA * xin_ref[...] + mix
        s1_ref[...] = s1
        x = _ln_fwd(s1, g1_ref[...], b1_ref[...])
        x1_ref[...] = x
        xb = x.astype(MX)
        s = ALPHA * x
        for j in range(NSHARD):
            a = _mm(xb, w1_ref[j])
            a_ref[:, j * D:(j + 1) * D] = a.astype(MX)
            s = s + _mm(jnp.square(jnp.maximum(a, 0.0)), w2_ref[j])
        s_ref[...] = s
        x2 = _ln_fwd(s, g_ref[...], b_ref[...])
        if not head:
            y_ref[...] = x2
            return
        l_ref = refs[16 + head]

        @pl.when(pl.program_id(0) == 0)
        def _():
            l_ref[...] = jnp.zeros_like(l_ref)

        e = x2 - refs[11][...]
        y_ref[...] = e * (1.0 / D)
        l_ref[...] += jnp.sum(jnp.sum(e * e, axis=1, keepdims=True), axis=0, keepdims=True) * (0.5 / D)

    rw = lambda w_: pl.BlockSpec((tm, w_), lambda i: (i, 0))
    row = rw(D)
    once = dict(pipeline_mode=pl.Buffered(1))
    wall = pl.BlockSpec((NSHARD, D, D), lambda i: (0, 0, 0), **once)
    one = pl.BlockSpec((1, D), lambda i: (0, 0))
    acc = pl.BlockSpec((8, 128), lambda i: (0, 0))
    return pl.pallas_call(
        body, grid=(N // tm,),
        in_specs=[rw(256), rw(256), rw(512), row, pl.BlockSpec((D, D), lambda i: (0, 0), **once), one, one,
                  wall, wall, one, one] + [row] * head,
        out_specs=[row, row, pl.BlockSpec((tm, DFF), lambda i: (i, 0)), row, row] + [acc] * head,
        out_shape=[_sds((N, D)), _sds((N, D)), _sds((N, DFF), MX), _sds((N, D)), _sds((N, D))] + [_sds((8, 128))] * head,
        name="mix_ffn_fwd", compiler_params=_cp(("arbitrary",), FFN_VMEM))(
            ya, yb, yc, x, wo, g1, b1, w1, w2, g, b, *([target] * head))


def _ffn_bwd_act(dy, s2, a, w1, w2, g):
    tm = FFN_TM

    def body(dy_ref, s_ref, a_ref, w1_ref, w2_ref, g_ref, da_ref, ds_ref, dx1_ref, dg_ref, db_ref):
        @pl.when(pl.program_id(0) == 0)
        def _():
            dg_ref[...] = jnp.zeros_like(dg_ref)
            db_ref[...] = jnp.zeros_like(db_ref)

        ds, dg, db = _ln_bwd(dy_ref[...], s_ref[...], g_ref[...])
        dsb = ds.astype(MX)
        ds_ref[...] = dsb
        dg_ref[...] += dg
        db_ref[...] += db
        dx1 = ALPHA * ds
        for j in range(NSHARD):
            da = (_mm_nt(dsb, w2_ref[j]) * 2.0 * jnp.maximum(a_ref[:, j * D:(j + 1) * D].astype(F32), 0.0)).astype(MX)
            da_ref[:, j * D:(j + 1) * D] = da
            dx1 = dx1 + _mm_nt(da, w1_ref[j])
        dx1_ref[...] = dx1

    row = pl.BlockSpec((tm, D), lambda i: (i, 0))
    wide = pl.BlockSpec((tm, DFF), lambda i: (i, 0))
    wall = pl.BlockSpec((NSHARD, D, D), lambda i: (0, 0, 0))
    one = pl.BlockSpec((1, D), lambda i: (0, 0))
    return pl.pallas_call(
        body, grid=(N // tm,),
        in_specs=[row, row, wide, wall, wall, one],
        out_specs=[wide, row, row, one, one],
        out_shape=[_sds((N, DFF), MX), _sds((N, D), MX), _sds((N, D)), _sds((1, D)), _sds((1, D))],
        name="ffn_bwd_act", compiler_params=_cp(("arbitrary",), FFN_VMEM))(dy, s2, a, w1, w2, g)


def _ffn_bwd_w(x1, da, a, ds):
    tm, nb = FFN_TM_W, FFN_WB
    nt = N // tm

    def body(x_ref, da_ref, a_ref, ds_ref, dw1_ref, dw2_ref, acc1, acc2):
        i = pl.program_id(1)

        @pl.when(i == 0)
        def _():
            acc1[...] = jnp.zeros_like(acc1)
            acc2[...] = jnp.zeros_like(acc2)

        x, ds_ = x_ref[...], ds_ref[...]
        for k in range(nb):
            cols = slice(k * D, (k + 1) * D)
            acc1[k] += _mm_tn(x, da_ref[:, cols])
            acc2[k] += _mm_tn(jnp.square(jnp.maximum(a_ref[:, cols].astype(F32), 0.0)), ds_)

        @pl.when(i == nt - 1)
        def _():
            dw1_ref[...] = acc1[...].astype(MX)
            dw2_ref[...] = acc2[...].astype(MX)

    row = pl.BlockSpec((tm, D), lambda j, i: (i, 0))
    col = pl.BlockSpec((tm, nb * D), lambda j, i: (i, j))
    wj = pl.BlockSpec((nb, D, D), lambda j, i: (j, 0, 0))
    return pl.pallas_call(
        body, grid=(NSHARD // nb, nt),
        in_specs=[row, col, col, row], out_specs=[wj, wj],
        out_shape=[_sds((NSHARD, D, D), MX), _sds((NSHARD, D, D), MX)],
        scratch_shapes=[pltpu.VMEM((nb, D, D), F32), pltpu.VMEM((nb, D, D), F32)],
        name="ffn_bwd_w", compiler_params=_cp(("parallel", "arbitrary"), FFN_VMEM))(x1, da, a, ds)


def _s5_discretize(a_re, a_im, log_step, b_re, b_im):
    lam = lax.complex(a_re, a_im)
    lam_bar = jnp.exp(lam * jnp.exp(log_step))
    b_bar = ((lam_bar - 1.0) / lam)[..., None] * lax.complex(b_re, b_im)
    return jnp.real(lam_bar), jnp.imag(lam_bar), jnp.real(b_bar), jnp.imag(b_bar)


def _s5_in_blocks(b):
    e = jnp.eye(8, dtype=F32)
    return jnp.einsum('ij,zbjph->zbihjp', e, b.reshape(2, 2, 8, S5_P, S5_H)).reshape(2, 2, 128, SW)


def _s5_out_blocks(c):
    e = jnp.eye(8, dtype=F32)
    return jnp.einsum('ij,zbjhp->zbjpih', e, c.reshape(2, 2, 8, S5_H, S5_P)).reshape(2, 2, SW, 128)


def _gate_weight(w_a):
    z = jnp.zeros((16, 128), F32)
    top = jnp.concatenate([w_a[0], z], axis=1)
    bot = jnp.concatenate([z, w_a[1]], axis=1)
    return jnp.concatenate([top, bot, jnp.zeros((96, 256), F32)], axis=0)


def _layer_prep(p):
    lr, li, br, bi = _s5_discretize(p["s5_a_re"], p["s5_a_im"], p["s5_log_step"], p["s5_b_re"], p["s5_b_im"])
    q = dict(p)
    q["bre"] = _s5_in_blocks(br).astype(MX)
    q["bim"] = _s5_in_blocks(bi).astype(MX)
    q["cre"] = _s5_out_blocks(p["s5_c_re"]).astype(MX)
    q["cim"] = _s5_out_blocks(p["s5_c_im"]).astype(MX)
    mr, mi = lr.reshape(2, 1024), li.reshape(2, 1024)
    q["tab"], q["tabc"] = _lockstep_tables(mr, mi)
    q["dsk"] = p["s5_d"].reshape(1, 256)
    q["wa"] = _gate_weight(p["gla_w_a"]).astype(MX)
    q["ba"] = p["gla_b_a"].reshape(1, 256)
    q["lng"] = p["gla_ln_g"].reshape(1, 256)
    q["bv"] = p["s5_b_glu"][:256].reshape(1, 256)
    q["bg"] = p["s5_b_glu"][256:].reshape(1, 256)
    for k in ("ln1_g", "ln1_b", "ln2_g", "ln2_b"):
        q[k] = p[k].reshape(1, D)
    return q


def _layer_fwd(x, q, tk, fetch, target=None):
    q["w_in"] = fetch("w_in", x)
    h, la2 = _inproj_fwd(x, q["w_in"], q["wa"], q["ba"])
    hre, him, y2 = _s5_fwd(h, q["bre"], q["bim"], q["cre"], q["cim"], q["tab"])
    q["w4"] = fetch("s5_w_glu", y2)
    ya = _s5_glu_fwd(y2, h, q["dsk"], q["w4"], q["bv"], q["bg"])
    of, ob, sf, sb = _gla_fwd(h, la2)
    yb = _gla_post_fwd(of, ob, h, q["lng"])
    yc = _swa_fwd(h, tk, q["swa_sink"])
    mixed = ya[:8, :128] + yb[:8, :128] + yc[:8, :128]
    q["w_out"] = fetch("w_out", mixed)
    q["w_ff1"] = fetch("w_ff1", mixed)
    q["w_ff2"] = fetch("w_ff2", mixed)
    s1, x1, a, s2, *out = _mix_ffn_fwd(ya, yb, yc, x, q["w_out"], q["ln1_g"], q["ln1_b"], q["w_ff1"], q["w_ff2"],
                                       q["ln2_g"], q["ln2_b"], target)
    saved = dict(x=x, h=h, hre=hre, him=him, y2=y2, ya=ya, la2=la2, of=of, ob=ob, sf=sf, sb=sb, yb=yb, yc=yc,
                 s1=s1, x1=x1, a=a, s2=s2)
    return (out[0] if target is None else tuple(out)), saved


def _layer_bwd(dy, q, sv, tk, emit):
    g = {}
    da, ds2, dx1, g["dg2"], g["db2"] = _ffn_bwd_act(dy, sv["s2"], sv["a"], q["w_ff1"], q["w_ff2"], q["ln2_g"])
    dw1, dw2 = _ffn_bwd_w(sv["x1"], da, sv["a"], ds2)
    tie = emit(dict(w_ff1=dw1, w_ff2=dw2))
    dya, dyb, dyc, dxp, dwo, g["dg1"], g["db1"] = _outproj_bwd(dx1, sv["s1"], sv["ya"], sv["yb"], sv["yc"],
                                                               q["w_out"], q["ln1_g"] + tie)
    h = sv["h"]
    daq, dakv, g["dsink"] = _swa_bwd(h, tk, q["swa_sink"], dyc)
    do, gr, g["dlng"] = _gla_post_bwd(sv["of"], sv["ob"], h, q["lng"], dyb)
    gq_f, gk_f, gv_f, gl_f, gq_b, gk_b, gv_b, gl_b = _gla_bwd(h, sv["la2"], do, sv["sf"], sv["sb"])
    dhl, g["dwa"], g["dba"] = _gla_gate_bwd(h, q["wa"], q["ba"], gl_f, gl_b)
    dyp, dud, g["dd"], dw4, g["dbv"], g["dbg"] = _s5_glu_bwd(sv["y2"], h, q["dsk"], q["w4"], q["bv"], q["bg"], dya)
    tie = emit(dict(w_out=dwo.reshape(NSHARD, D // NSHARD, D), s5_w_glu=dw4))
    du2, g["dbre"], g["dbim"], g["dcre"], g["dcim"], g["dmu"] = _s5_bwd(
        h, dyp, sv["hre"], sv["him"], q["bre"], q["bim"], q["cre"], q["cim"], (q["tabc"][0], q["tabc"][1] + tie))
    dx, dwt = _inproj_bwd(sv["x"], q["w_in"], dxp, du2, dud, gq_f, gq_b, gk_f, gk_b, gv_f, gv_b, gr, daq, dakv, dhl)
    tie = emit(dict(w_in=dwt))
    return dx, g, tie


NATIVE = ("dmu", "dbre", "dbim", "dcre", "dcim", "dd", "dbv", "dbg", "dwa", "dba", "dlng", "dsink",
          "dg1", "db1", "dg2", "db2", "loss")
ICI_CORE = (0, 0, 0, 1, 1, 0, 0, 0, 1, 1, 1, 1, 0, 0, 1, 1, 0)
Y_FIRST = (0, 0, 1, 0, 1, 1, 0, 1, 0, 1, 0, 1, 0, 1, 0, 1, 0)


def _finish_small(n, w):
    g = {}
    dmu = n["dmu"]
    dlr = dmu[:, :, :, 0].reshape(DEPTH, 2, S5_G, S5_P)
    dli = dmu[:, :, :, 1].reshape(DEPTH, 2, S5_G, S5_P)

    def unblock(c, perm, shape):
        return c.reshape(DEPTH, 2, 2, S5_H, 8, S5_P).transpose(perm).reshape(shape)

    b_shape, c_shape = (DEPTH, 2, S5_G, S5_P, S5_H), (DEPTH, 2, S5_G, S5_H, S5_P)
    _, vjp = jax.vjp(_s5_discretize, w["s5_a_re"], w["s5_a_im"], w["s5_log_step"], w["s5_b_re"], w["s5_b_im"])
    (g["s5_a_re"], g["s5_a_im"], g["s5_log_step"], g["s5_b_re"], g["s5_b_im"]) = vjp(
        (dlr, dli, unblock(n["dbre"], (0, 1, 2, 4, 5, 3), b_shape), unblock(n["dbim"], (0, 1, 2, 4, 5, 3), b_shape)))
    g["s5_c_re"] = unblock(n["dcre"], (0, 1, 2, 4, 3, 5), c_shape)
    g["s5_c_im"] = unblock(n["dcim"], (0, 1, 2, 4, 3, 5), c_shape)
    g["s5_d"] = n["dd"].reshape(DEPTH, S5_G, S5_H)
    g["s5_b_glu"] = jnp.concatenate([n["dbv"], n["dbg"]], axis=2).reshape(DEPTH, 512)
    g["gla_w_a"] = jnp.stack([n["dwa"][:, 0:16, 0:128], n["dwa"][:, 16:32, 128:256]], axis=1)
    g["gla_b_a"] = n["dba"].reshape(DEPTH, 2, 128)
    g["gla_ln_g"] = n["dlng"].reshape(DEPTH, 256)
    g["swa_sink"] = n["dsink"][:, :, 0]
    for k, s in (("ln1_g", "dg1"), ("ln1_b", "db1"), ("ln2_g", "dg2"), ("ln2_b", "db2")):
        g[k] = n[s].reshape(DEPTH, D)
    return g


def _local_step(x, target, qs, tk, fetch, emit):
    saved = []
    for l, q in enumerate(qs):
        x, sv = _layer_fwd(x, q, tk, functools.partial(fetch, l), target if l == DEPTH - 1 else None)
        saved.append(sv)
    dy, lacc = x
    smalls = [None] * DEPTH
    tie = 0.0
    for l in reversed(range(DEPTH)):
        qs[l]["ln2_g"] = qs[l]["ln2_g"] + tie
        dy, smalls[l], tie = _layer_bwd(dy, qs[l], saved[l], tk, functools.partial(emit, l))
    smalls[0]["db2"] = smalls[0]["db2"] + tie
    for l in range(DEPTH):
        smalls[l]["loss"] = lacc if l == 0 else jnp.zeros_like(lacc)
    return lacc[0, 0], dy, smalls


BIG = ("w_in", "s5_w_glu", "w_out", "w_ff1", "w_ff2")
SMALL = ("s5_a_re", "s5_a_im", "s5_log_step", "s5_b_re", "s5_b_im", "s5_c_re", "s5_c_im", "s5_d", "s5_b_glu",
         "gla_w_a", "gla_b_a", "gla_ln_g", "swa_sink", "ln1_g", "ln1_b", "ln2_g", "ln2_b")
ANY = pl.BlockSpec(memory_space=pl.ANY)


def _place():
    x, y, c = lax.axis_index("x"), lax.axis_index("y"), lax.axis_index("c")
    return x, y, c, [(1 - x, y), (x, 1 - y), (1 - x, 1 - y)]


HBM = pl.BlockSpec(memory_space=pltpu.HBM)
SEMS = pl.BlockSpec(memory_space=pltpu.SEMAPHORE)
EFFECT = pltpu.SideEffectType.DATAFLOW_SIDE_EFFECTING


def _push_copies(ins, lands, send, recv, gather, sending):
    x, y, c, chips = _place()
    me = 2 * x + y
    if gather == "sibling":
        return [pltpu.make_async_remote_copy(src_ref=ins[a], dst_ref=lands[a], send_sem=send.at[a], recv_sem=recv.at[a],
                                             device_id=(x, y, 1 - c), device_id_type=MESH) for a in range(len(lands))]
    out = []
    for a in range(len(lands)):
        for j, (px, py) in enumerate(chips):
            peer = 2 * px + py
            src = lands[a].at[me] if gather else ins[a].at[peer if sending else me]
            dst = lands[a].at[me if sending else peer]
            out.append(pltpu.make_async_remote_copy(src_ref=src, dst_ref=dst, send_sem=send.at[3 * a + j],
                                                    recv_sem=recv.at[3 * a + j], device_id=(px, py, c),
                                                    device_id_type=MESH))
    return out


def _push_start(name, arrs, gather):
    n = len(arrs)
    ops = list(arrs) if gather is True else list(arrs) + [lax.empty(s.shape, s.dtype) for s in arrs]
    m = len(ops)

    def body(*refs):
        ins, lnd = (refs[:n], refs[:n]) if gather is True else (refs[:n], refs[n:m])
        for cp in _push_copies(ins, lnd, refs[m], refs[m + 1], gather, True):
            cp.start()
        refs[-1][...] = jnp.zeros((8, 128), F32)

    ops = [pltpu.with_memory_space_constraint(t, pltpu.HBM) for t in ops]
    res = pl.pallas_call(
        body, name=name,
        out_shape=(pltpu.SemaphoreType.DMA((3 * n,)), pltpu.SemaphoreType.DMA((3 * n,)),
                   *[pltpu.HBM(t.shape, t.dtype) for t in ops], _sds((8, 128))),
        in_specs=[HBM] * m,
        out_specs=(SEMS, SEMS, *[HBM] * m, pl.BlockSpec(memory_space=pltpu.VMEM)),
        input_output_aliases={i: 2 + i for i in range(m)},
        compiler_params=pltpu.CompilerParams(has_side_effects=EFFECT))(*ops)
    return res[0], res[1], list(res[2:2 + m]), res[-1]


def _push_wait(name, started, after, gather):
    send, recv, ops, _ = started
    m = len(ops)
    n = m if gather is True else m // 2

    def body(*refs):
        ins, lnd = (refs[:n], refs[:n]) if gather is True else (refs[:n], refs[n:m])
        for cp in _push_copies(ins, lnd, refs[m], refs[m + 1], gather, False):
            cp.wait_send()
            cp.wait_recv()

    res = pl.pallas_call(
        body, name=name,
        out_shape=[pltpu.HBM(t.shape, t.dtype) for t in ops],
        in_specs=[HBM] * m + [SEMS, SEMS, ANY], out_specs=[HBM] * m,
        input_output_aliases={i: i for i in range(m)},
        compiler_params=pltpu.CompilerParams(has_side_effects=EFFECT))(*ops, send, recv, after)
    return list(res)


def _row_tile(rows):
    return max(t for t in range(8, min(rows, 512) + 1, 8) if rows % t == 0)


def _cast_to_slot(me, w, l):
    _, rows, cols = w.shape
    tr = _row_tile(rows)

    def body(me_ref, w_ref, o_ref):
        o_ref[0] = w_ref[0].astype(MX)

    return pl.pallas_call(
        body,
        grid_spec=pltpu.PrefetchScalarGridSpec(
            num_scalar_prefetch=1, grid=(rows // tr,),
            in_specs=[pl.BlockSpec((1, tr, cols), lambda i, me_: (l, i, 0))],
            out_specs=pl.BlockSpec((1, tr, cols), lambda i, me_: (me_[0], i, 0))),
        out_shape=_sds((NSHARD, rows, cols), MX), name="cast_to_slot", compiler_params=_cp(("arbitrary",)))(me, w)


def _sum_sources(me, recv, own):
    _, rows, cols = recv[0].shape
    tr = min(_row_tile(rows), 256) if rows % 256 == 0 else _row_tile(rows)
    nt = rows // tr

    def body(me_ref, *refs):
        o_ref = refs[-1]
        for l in range(DEPTH):
            @pl.when(pl.program_id(0) == l)
            def _():
                r_ref, own_ref = refs[2 * l], refs[2 * l + 1]
                part = [jnp.where(me_ref[0] == s, own_ref[0], r_ref[s]).astype(F32) for s in range(NSHARD)]
                o_ref[...] = ((part[0] + part[1]) + part[2]) + part[3]

    in_specs = []
    for l in range(DEPTH):
        pick = lambda g, i, me_, l=l: jnp.where(g == l, i, jnp.where(g < l, 0, nt - 1))
        in_specs += [pl.BlockSpec((NSHARD, tr, cols), lambda g, i, me_, pick=pick: (0, pick(g, i, me_), 0)),
                     pl.BlockSpec((1, tr, cols), lambda g, i, me_, pick=pick: (me_[0], pick(g, i, me_), 0))]
    return pl.pallas_call(
        body,
        grid_spec=pltpu.PrefetchScalarGridSpec(
            num_scalar_prefetch=1, grid=(DEPTH, nt), in_specs=in_specs,
            out_specs=pl.BlockSpec((tr, cols), lambda g, i, me_: (g * nt + i, 0))),
        out_shape=_sds((DEPTH * rows, cols)), name="sum_sources",
        compiler_params=_cp(("arbitrary", "arbitrary")))(me, *[t for l in range(DEPTH) for t in (recv[l], own[l])])


def _allreduce_small(per_layer):
    nk = len(per_layer[0])
    n = DEPTH * nk
    shapes = [a.shape for a in per_layer[0]]

    def body(*refs):
        ins, outs = refs[:n], refs[n:n + nk]
        sibs, slots = refs[n + nk:n + 2 * nk], refs[n + 2 * nk:n + 3 * nk]
        send, recv = refs[n + 3 * nk:]
        x, y, c, chips = _place()
        me = 2 * x + y
        d2d = [pltpu.make_async_remote_copy(src_ref=ins[l * nk + k], dst_ref=sibs[k].at[l], send_sem=send.at[l * nk + k],
                                            recv_sem=recv.at[l * nk + k], device_id=(x, y, 1 - c), device_id_type=MESH)
               for l in range(DEPTH) for k in range(nk)]
        for cp in d2d:
            cp.start()
        for cp in d2d:
            cp.wait()
        for l in range(DEPTH):
            for k in range(nk):
                slots[k][0, l] = ins[l * nk + k][...] + sibs[k][l]

        def swap(k, stage):
            peer = (1 - x, y, c) if stage == Y_FIRST[k] else (x, 1 - y, c)
            return pltpu.make_async_remote_copy(src_ref=slots[k].at[2 * stage], dst_ref=slots[k].at[2 * stage + 1],
                                                send_sem=send.at[n + 3 * k + stage], recv_sem=recv.at[n + 3 * k + stage],
                                                device_id=peer, device_id_type=MESH)

        def handover(k):
            return pltpu.make_async_remote_copy(src_ref=outs[k], dst_ref=outs[k], send_sem=send.at[n + 3 * nk + k],
                                                recv_sem=recv.at[n + 3 * nk + k], device_id=(x, y, 1 - c),
                                                device_id_type=MESH)

        halves = (tuple(k for k in range(nk) if ICI_CORE[k] == 0), tuple(k for k in range(nk) if ICI_CORE[k] == 1))
        for cc in range(2):
            @pl.when(c == cc)
            def _():
                mine, theirs = halves[cc], halves[1 - cc]
                for stage in range(2):
                    cps = [swap(k, stage) for k in mine]
                    for cp in cps:
                        cp.start()
                    for cp in cps:
                        cp.wait()
                    for k in mine:
                        if stage == 0:
                            slots[k][2] = slots[k][0] + slots[k][1]
                        else:
                            outs[k][...] = slots[k][2] + slots[k][3]
                over = [handover(k) for k in mine]
                for cp in over:
                    cp.start()
                for k in theirs:
                    handover(k).wait_recv()
                for cp in over:
                    cp.wait_send()

    vm = pl.BlockSpec(memory_space=pltpu.VMEM)
    return pl.pallas_call(
        body, in_specs=[vm] * n, out_specs=[vm] * nk, out_shape=[_sds((DEPTH,) + s) for s in shapes],
        scratch_shapes=([pltpu.VMEM((DEPTH,) + s, F32) for s in shapes]
                        + [pltpu.VMEM((NSHARD, DEPTH) + s, F32) for s in shapes]
                        + [pltpu.SemaphoreType.DMA((n + 4 * nk,)), pltpu.SemaphoreType.DMA((n + 4 * nk,))]),
        name="allreduce_small", compiler_params=pltpu.CompilerParams(vmem_limit_bytes=VMEM_LIMIT))(
            *[a for layer in per_layer for a in layer])


def _adamw_math(w, g, m, v):
    m = ADAM_B1 * m + (1.0 - ADAM_B1) * g
    v = ADAM_B2 * v + (1.0 - ADAM_B2) * jnp.square(g)
    m_hat = m / (1.0 - ADAM_B1 ** ADAM_STEP)
    v_hat = v / (1.0 - ADAM_B2 ** ADAM_STEP)
    delta = -ADAM_LR * (m_hat / (jnp.sqrt(v_hat) + ADAM_EPS) + ADAM_WD * w)
    return delta, m, v


def _adamw(g_parts, w, m, v):
    rows, cols = w.shape
    tr = 256 if rows % 256 == 0 else _row_tile(rows)
    k = len(g_parts)

    def body(*refs):
        g = refs[0][...]
        for r in refs[1:k]:
            g = g + r[...]
        w_ref, m_ref, v_ref, go, do, mo, vo = refs[k:]
        d, mn, vn = _adamw_math(w_ref[...], g, m_ref[...], v_ref[...])
        go[...] = g
        do[...] = d
        mo[...] = mn
        vo[...] = vn

    spec = pl.BlockSpec((tr, cols), lambda i: (i, 0))
    return pl.pallas_call(
        body, grid=(rows // tr,), in_specs=[spec] * (k + 3), out_specs=[spec] * 4,
        out_shape=[_sds((rows, cols))] * 4, name="adamw", compiler_params=_cp(("parallel",)))(*g_parts, w, m, v)


def _adamw_small(gs, ws, ms, vs):
    n = len(gs)

    def body(*refs):
        for k in range(n):
            d, mn, vn = _adamw_math(refs[n + k][...], refs[k][...], refs[2 * n + k][...], refs[3 * n + k][...])
            refs[4 * n + k][...] = d
            refs[5 * n + k][...] = mn
            refs[6 * n + k][...] = vn

    vm = pl.BlockSpec(memory_space=pltpu.VMEM)
    shapes = [_sds(a.shape) for a in ws]
    res = pl.pallas_call(
        body, in_specs=[vm] * (4 * n), out_specs=[vm] * (3 * n), out_shape=shapes * 3, name="adamw_small",
        compiler_params=pltpu.CompilerParams(vmem_limit_bytes=VMEM_LIMIT))(*gs, *ws, *ms, *vs)
    return res[:n], res[n:2 * n], res[2 * n:]


_ARGS = ("x", "w_in", "s5_a_re", "s5_a_im", "s5_log_step", "s5_b_re", "s5_b_im", "s5_c_re", "s5_c_im", "s5_d",
         "s5_w_glu", "s5_b_glu", "gla_w_a", "gla_b_a", "gla_ln_g", "swa_sink", "w_out", "ln1_g", "ln1_b", "w_ff1",
         "w_ff2", "ln2_g", "ln2_b")
_WEIGHTS = _ARGS[1:]


def kernel(x, w_in, s5_a_re, s5_a_im, s5_log_step, s5_b_re, s5_b_im, s5_c_re, s5_c_im, s5_d, s5_w_glu, s5_b_glu, gla_w_a, gla_b_a, gla_ln_g, swa_sink, w_out, ln1_g, ln1_b, w_ff1, w_ff2, ln2_g, ln2_b, loss_target, m_w_in, m_s5_a_re, m_s5_a_im, m_s5_log_step, m_s5_b_re, m_s5_b_im, m_s5_c_re, m_s5_c_im, m_s5_d, m_s5_w_glu, m_s5_b_glu, m_gla_w_a, m_gla_b_a, m_gla_ln_g, m_swa_sink, m_w_out, m_ln1_g, m_ln1_b, m_w_ff1, m_w_ff2, m_ln2_g, m_ln2_b, v_w_in, v_s5_a_re, v_s5_a_im, v_s5_log_step, v_s5_b_re, v_s5_b_im, v_s5_c_re, v_s5_c_im, v_s5_d, v_s5_w_glu, v_s5_b_glu, v_gla_w_a, v_gla_b_a, v_gla_ln_g, v_swa_sink, v_w_out, v_ln1_g, v_ln1_b, v_w_ff1, v_w_ff2, v_ln2_g, v_ln2_b):
    given = dict(locals())
    w = {k: given[k] for k in _WEIGHTS}
    mom = {k: given["m_" + k] for k in _WEIGHTS}
    var = {k: given["v_" + k] for k in _WEIGHTS}

    me = (2 * lax.axis_index("x") + lax.axis_index("y")).astype(jnp.int32).reshape(1)
    tr = lambda t: t.transpose(0, 2, 1)
    shard = {k: (tr(w[k]) if k == "w_in" else w[k]) for k in BIG}
    qs = [None] * DEPTH

    first = ("w_in", "s5_w_glu", "w_out")
    follow = {(0, "w_in"): [(0, BIG[3:])], (0, "s5_w_glu"): [(1, first)], (0, "w_ff1"): [(1, BIG[3:])]}
    gathers = {}

    casts = {}

    def start_gather(l, names, behind=None):
        lands = [casts.pop((l, k)) if (l, k) in casts else _cast_to_slot(me, shard[k], l) for k in names]
        if behind is not None:
            lands, behind = lax.optimization_barrier((lands, behind))
        st = _push_start(f"gather_start_{l}_{names[0]}", lands, True)
        for k in names:
            gathers[l, k] = [names, st, None]
        return st[-1], behind

    token = start_gather(0, first[:1])[0] + start_gather(0, first[1:])[0]
    zero = token[0, 0]
    for l in range(DEPTH):
        for k in BIG:
            if (l, k) not in gathers:
                casts[l, k] = _cast_to_slot(me, lax.optimization_barrier((shard[k], token))[0], l)
        qs[l] = _layer_prep({k: (w[k][l] + zero if k == "s5_a_re" else w[k][l]) for k in SMALL})
    token, casts, qs = lax.optimization_barrier((token, casts, qs))

    def fetch(l, name, after):
        names, st, got = gathers[l, name]
        tie = None
        if got is None:
            if l == 0 and name == "w_in":
                after = token
            lands = _push_wait(f"gather_wait_{l}_{names[0]}", st, after, True)
            for l2, names2 in follow.get((l, name), ()):
                tok, lands[0] = start_gather(l2, names2, lands[0])
                tie = tok if tie is None else tie + tok
            got = dict(zip(names, lands))
            for k in names:
                gathers[l, k][2] = got
        full = got[name]
        if name == "w_in":
            return _in_rows(full, token if tie is None else tie)
        if tie is not None:
            near = "bv" if name == "s5_w_glu" else "ln2_b"
            qs[l][near] = qs[l][near] + tie[0, 0]
        return full.reshape(D, D) if name == "w_out" else full

    scatters, held = [], {}

    def emit(l, grads):
        if l > 0:
            held.update(grads)
            if "w_in" not in grads:
                return 0.0
            grads = dict(held)
            held.clear()
        names = tuple(grads)
        st = _push_start(f"scatter_start_{l}_{names[0]}", [grads[k] for k in names], False)
        scatters.append((l, names, st))
        return st[-1][0, 0]

    loss, dx, smalls = _local_step(x.reshape(N, D), loss_target.reshape(N, D), qs, _rope_tables(128), fetch, emit)

    out, recv, own = {}, {}, {}

    def collect(keys, after):
        for l, names, st in scatters:
            if names[0] in keys:
                ops = _push_wait(f"scatter_wait_{l}_{names[0]}", st, after, False)
                for i, k in enumerate(names):
                    own[l, k], recv[l, k] = ops[i], ops[len(names) + i]

    def shard_sums(keys):
        return [_sum_sources(me, [recv[l, k] for l in range(DEPTH)], [own[l, k] for l in range(DEPTH)]) for k in keys]

    def to_sibling(keys, sums):
        return _push_start(f"swap_start_{keys[0]}", sums, "sibling")

    def apply(keys, started, after):
        ops = _push_wait(f"swap_wait_{keys[0]}", started, after, "sibling")
        for i, k in enumerate(keys):
            mine, other = ops[i], ops[len(keys) + i]
            shp = shard[k].shape
            r = _adamw([mine, other], *((tr(t[k]) if k == "w_in" else t[k]).reshape(-1, shp[-1]) for t in (w, mom, var)))
            r = [t.reshape(shp) for t in r]
            out[k] = [tr(t) for t in r] if k == "w_in" else r
        return out[keys[-1]][1]

    collect(("w_ff1", "w_ff2", "w_out", "s5_w_glu"), dx)
    sums = shard_sums(("w_ff1", "w_ff2", "w_out", "s5_w_glu"))
    sums, smalls[0]["db1"] = lax.optimization_barrier((sums, smalls[0]["db1"]))
    native = _allreduce_small([[smalls[l][k] for k in NATIVE] for l in range(DEPTH)])
    sums, native = lax.optimization_barrier((sums, native))
    ff = to_sibling(("w_ff1", "w_ff2"), sums[:2])
    mix = to_sibling(("w_out", "s5_w_glu"), sums[2:])
    native = dict(zip(NATIVE, native))
    native["db1"] = native["db1"] + (ff[-1][0, 0] + mix[-1][0, 0])
    loss = native["loss"][0, 0, 0] + native["loss"][1, 0, 0]
    gsmall = _finish_small(native, w)
    view = lambda k, t: t.transpose(0, 1, 2, 4, 3) if k in ("s5_b_re", "s5_b_im") else t
    res = _adamw_small(*([view(k, t[k]) for k in SMALL] for t in (gsmall, w, mom, var)))
    for i, k in enumerate(SMALL):
        out[k] = [gsmall[k]] + [view(k, r[i]) for r in res]
    last = apply(("w_ff1", "w_ff2"), ff, res[0][-1])
    collect(("w_in",), last)
    win = to_sibling(("w_in",), shard_sums(("w_in",)))
    last = apply(("w_out", "s5_w_glu"), mix, win[-1])
    apply(("w_in",), win, last)

    return (loss, dx.reshape(NSEQ, L, D), *[out[k][0] for k in _WEIGHTS], *[out[k][1] for k in _WEIGHTS],
            *[out[k][2] for k in _WEIGHTS], *[out[k][3] for k in _WEIGHTS])
```

```python
import functools
import math

import jax
import jax.numpy as jnp
from jax import lax
from jax.experimental import pallas as pl
from jax.experimental.pallas import tpu as pltpu

F32 = jnp.float32
MX = jnp.bfloat16
MESH = pl.DeviceIdType.MESH

DEPTH = 2
NSEQ = 2
L = 2048
N = NSEQ * L
D = 1024
DFF = 4096
NSHARD = 4
S5_G, S5_H, S5_P = 16, 16, 64
GLA_CHUNK = 64
NCHUNK = L // GLA_CHUNK
GLA_GROUP = 4
NGROUP = NCHUNK // GLA_GROUP
SWA_BLK = 128
NBLK = L // SWA_BLK
SWA_PER = 2
ROT = 16
ROPE_THETA = 500000.0
LN_EPS = 1e-5
ALPHA = (2 * DEPTH) ** 0.25
NEG_BIG = -1e30
DIN = 1824
DINP = 1920
ADAM_LR, ADAM_B1, ADAM_B2, ADAM_EPS, ADAM_WD, ADAM_STEP = 0.001, 0.9, 0.999, 1e-08, 0.01, 10
VMEM_LIMIT = 56 * 1024 * 1024
TT = 512
SW = 512
FFN_TM = 512
FFN_TM_W = 1024
FFN_WB = 1
FFN_VMEM = 60 * 1024 * 1024
INPROJ_BWD_TM = 512


def _cp(sem, vmem=VMEM_LIMIT):
    return pltpu.CompilerParams(dimension_semantics=sem, vmem_limit_bytes=vmem)


def _mm(a, b):
    return jnp.dot(a.astype(MX), b.astype(MX), preferred_element_type=F32)


def _mm_nt(a, b):
    return lax.dot_general(a.astype(MX), b.astype(MX), (((1,), (1,)), ((), ())), preferred_element_type=F32)


def _mm_tn(a, b):
    return lax.dot_general(a.astype(MX), b.astype(MX), (((0,), (0,)), ((), ())), preferred_element_type=F32)


@jax.custom_vjp
def _dmm(a, b):
    return _mm(a, b)


_dmm.defvjp(lambda a, b: (_mm(a, b), (a, b)), lambda r, g: (_mm_nt(g, r[1]), _mm_tn(r[0], g)))


@jax.custom_vjp
def _dmm_nt(a, b):
    return _mm_nt(a, b)


_dmm_nt.defvjp(lambda a, b: (_mm_nt(a, b), (a, b)), lambda r, g: (_mm(g, r[1]), _mm_tn(g, r[0])))


@jax.custom_vjp
def _dmm_tn(a, b):
    return _mm_tn(a, b)


_dmm_tn.defvjp(lambda a, b: (_mm_tn(a, b), (a, b)), lambda r, g: (_mm_nt(r[1], g), _mm(r[0], g)))


def _split3(x):
    hi = x.astype(MX)
    r1 = x - hi.astype(F32)
    mid = r1.astype(MX)
    lo = (r1 - mid.astype(F32)).astype(MX)
    return hi, mid, lo


def _chunk_pairs(rows, rev, strict):
    r = lax.broadcasted_iota(jnp.int32, (rows, rows), 0)
    c = lax.broadcasted_iota(jnp.int32, (rows, rows), 1)
    order = ((c > r) if strict else (c >= r)) if rev else ((c < r) if strict else (c <= r))
    return (r // GLA_CHUNK == c // GLA_CHUNK) & order


def _cums_impl(x, rev):
    rows, w = x.shape
    t = jnp.where(_chunk_pairs(rows, rev, False), 1.0, 0.0).astype(MX)
    s = jnp.dot(t, jnp.concatenate(_split3(x), axis=1), preferred_element_type=F32)
    return s[:, 0:w] + s[:, w:2 * w] + s[:, 2 * w:3 * w]


@functools.partial(jax.custom_vjp, nondiff_argnums=(1,))
def _cums(x, rev):
    return _cums_impl(x, rev)


_cums.defvjp(lambda x, rev: (_cums_impl(x, rev), None), lambda rev, r, g: (_cums_impl(g, not rev),))


def _ln_fwd(s, g, b):
    mu = jnp.mean(s, axis=-1, keepdims=True)
    xc = s - mu
    var = jnp.mean(xc * xc, axis=-1, keepdims=True)
    return xc * lax.rsqrt(var + LN_EPS) * g + b


def _ln_bwd(dy, s, g):
    mu = jnp.mean(s, axis=-1, keepdims=True)
    xc = s - mu
    var = jnp.mean(xc * xc, axis=-1, keepdims=True)
    rstd = lax.rsqrt(var + LN_EPS)
    xhat = xc * rstd
    dxh = dy * g
    ds = rstd * (dxh - jnp.mean(dxh, axis=-1, keepdims=True) - xhat * jnp.mean(dxh * xhat, axis=-1, keepdims=True))
    return ds, jnp.sum(dy * xhat, axis=0, keepdims=True), jnp.sum(dy, axis=0, keepdims=True)


def _sds(shape, dtype=F32):
    return jax.ShapeDtypeStruct(shape, dtype)


_IN_ROW_PIECES = (((0, 0), (0, 456)), ((1, 0), (456, 456)), ((2, 0), (912, 112)), ((2, 112), (1792, 32)),
                  ((2, 144), (1024, 312)), ((3, 0), (1336, 456)))


def _in_rows(g4, behind):
    def body(g_ref, behind_ref, o_ref, tmp):
        tmp[DIN:DINP] = jnp.zeros((DINP - DIN, D), F32)
        for (j, s0), (d0, n_) in _IN_ROW_PIECES:
            tmp[d0:d0 + n_] = g_ref[j, s0:s0 + n_].astype(F32)
        o_ref[...] = tmp[...].astype(MX)

    vm = pl.BlockSpec(memory_space=pltpu.VMEM)
    return pl.pallas_call(body, in_specs=[vm, pl.BlockSpec(memory_space=pl.ANY)], out_specs=vm,
                          out_shape=_sds((DINP, D), MX), scratch_shapes=[pltpu.VMEM((DINP, D), F32)], name="in_rows",
                          compiler_params=pltpu.CompilerParams(vmem_limit_bytes=VMEM_LIMIT))(g4, behind)


def _inproj_fwd(x, wt, wa, ba):
    tm = 512

    def body(x_ref, w_ref, wa_ref, ba_ref, h_ref, la_ref):
        h = _mm_nt(x_ref[...], w_ref[...])
        h_ref[...] = h
        la_ref[...] = _logsig(_mm(h[:, DINP - 128:], wa_ref[...]) + ba_ref[...]) * (1.0 / 16.0)

    return pl.pallas_call(
        body, grid=(N // tm,),
        in_specs=[pl.BlockSpec((tm, D), lambda i: (i, 0)), pl.BlockSpec((DINP, D), lambda i: (0, 0)),
                  pl.BlockSpec((128, 256), lambda i: (0, 0)), pl.BlockSpec((1, 256), lambda i: (0, 0))],
        out_specs=[pl.BlockSpec((tm, DINP), lambda i: (i, 0)), pl.BlockSpec((tm, 256), lambda i: (i, 0))],
        out_shape=[_sds((N, DINP)), _sds((N, 256))], name="inproj_fwd", compiler_params=_cp(("parallel",)))(x, wt, wa, ba)


def _inproj_bwd(x, w, dxp, du2, dud, gq_f, gq_b, gk_f, gk_b, gv_f, gv_b, gr, daq, dakv, dhl):
    tm = INPROJ_BWD_TM
    nt = N // tm

    def body(x_ref, w_ref, dxp_ref, du2_ref, dud_ref, gqf, gqb, gkf, gkb, gvf, gvb, gr_ref, daq_ref, dakv_ref, dhl_ref,
             dx_ref, dw_ref, acc):
        i = pl.program_id(0)
        f = lambda r: r[...].astype(F32)
        dh = jnp.concatenate([
            du2_ref[0] + du2_ref[1] + f(dud_ref), f(gqf) + f(gqb), f(gkf) + f(gkb), f(gvf) + f(gvb),
            f(gr_ref), f(daq_ref), f(dakv_ref), f(dhl_ref)], axis=1)
        dx_ref[...] = dxp_ref[...] + _mm(dh, w_ref[...])
        contrib = _mm_tn(dh, x_ref[...])

        @pl.when(i == 0)
        def _():
            acc[...] = contrib

        @pl.when(i > 0)
        def _():
            acc[...] += contrib

        @pl.when(i == nt - 1)
        def _():
            for (j, d0), (s0, n_) in _IN_ROW_PIECES:
                dw_ref[j, d0:d0 + n_] = acc[s0:s0 + n_].astype(MX)

    row = lambda w_: pl.BlockSpec((tm, w_), lambda i: (i, 0))
    return pl.pallas_call(
        body, grid=(nt,),
        in_specs=[row(D), pl.BlockSpec((DINP, D), lambda i: (0, 0)), row(D),
                  pl.BlockSpec((2, tm, 256), lambda i: (0, i, 0)), row(256), row(128), row(128), row(128), row(128),
                  row(256), row(256), row(256), row(512), row(256), row(128)],
        out_specs=[row(D), pl.BlockSpec((NSHARD, DIN // NSHARD, D), lambda i: (0, 0, 0))],
        out_shape=[_sds((N, D)), _sds((NSHARD, DIN // NSHARD, D), MX)],
        scratch_shapes=[pltpu.VMEM((DINP, D), F32)],
        name="inproj_bwd", compiler_params=_cp(("arbitrary",)))(
            x, w, dxp, du2, dud, gq_f, gq_b, gk_f, gk_b, gv_f, gv_b, gr, daq, dakv, dhl)


def _tile_scan(xr, xi, a, cr, ci, reverse):
    for lvl, d in enumerate((1, 2, 4)):
        sh = 8 - d if reverse else d
        sr = pltpu.roll(xr, sh, 0)
        si = pltpu.roll(xi, sh, 0)
        ar, ai = a[2 * lvl], a[2 * lvl + 1]
        xr, xi = xr + ar * sr - ai * si, xi + ar * si + ai * sr
    pr, pi = a[6], a[7]
    return xr + pr * cr - pi * ci, xi + pr * ci + pi * cr


NJ = TT // 8


def _lockstep_tables(mr, mi):
    def body(mr_ref, mi_ref, a_ref, p_ref, ac_ref, pc_ref):
        rowid = lax.broadcasted_iota(jnp.int32, (8, 2 * SW), 0)

        def mul(a, b):
            return a[0] * b[0] - a[1] * b[1], a[0] * b[1] + a[1] * b[0]

        for z in range(2):
            for sign, reverse, a_out, p_out in ((1.0, z == 1, a_ref, p_ref), (-1.0, z == 0, ac_ref, pc_ref)):
                m = (mr_ref[z:z + 1, :], sign * mi_ref[z:z + 1, :])
                pw = [m]
                for _ in range(NJ - 1):
                    pw.append(mul(pw[-1], m))
                n = pw[-1]
                link = [n]
                for _ in range(7):
                    link.append(mul(link[-1], n))
                tiles = [jnp.broadcast_to(m[0], (8, 2 * SW)), jnp.broadcast_to(m[1], (8, 2 * SW))]
                for d in (1, 2, 4):
                    keep = (rowid <= 7 - d) if reverse else (rowid >= d)
                    tiles += [jnp.where(keep, link[d - 1][c], 0.0) for c in range(2)]
                for c in range(2):
                    t = jnp.zeros((8, 2 * SW), F32)
                    for i in range(8):
                        t = jnp.where(rowid == (7 - i if reverse else i), link[i][c], t)
                    tiles.append(t)
                for blk in range(2):
                    lanes = slice(blk * SW, (blk + 1) * SW)
                    for k, t in enumerate(tiles):
                        a_out[z, blk, k] = t[:, lanes]
                    for j in range(NJ):
                        src = pw[NJ - 1 - j] if reverse else pw[j]
                        for c in range(2):
                            p_out[z, blk, c, j:j + 1, :] = src[c][:, lanes]

    vm = pl.BlockSpec(memory_space=pltpu.VMEM)
    a_shape, p_shape = _sds((2, 2, 10, 8, SW)), _sds((2, 2, 2, NJ, SW))
    a, p, ac, pc = pl.pallas_call(body, in_specs=[vm, vm], out_specs=[vm] * 4, out_shape=[a_shape, p_shape] * 2,
                                  name="s5_tables")(mr, mi)
    return (a, p), (ac, pc)


def _to_lockstep(ref, *lead):
    return jnp.concatenate([ref[(*lead, pl.ds(j, 8, stride=NJ), slice(None))] for j in range(NJ)], axis=0)


def _from_lockstep(val, ref, *lead):
    for j in range(NJ):
        ref[(*lead, pl.ds(j, 8, stride=NJ), slice(None))] = val[8 * j:8 * j + 8]


def _expand_powers(p_ref, pexp):
    for c in range(2):
        for j in range(NJ):
            pexp[c, j] = jnp.broadcast_to(p_ref[0, 0, c, j:j + 1, :], (8, SW))


def _lockstep_scan(xre, xim, a_ref, pexp, car, reverse, extra=None):
    a = [a_ref[0, 0, k] for k in range(10)]
    mr, mi = a[0], a[1]
    order = (lambda i: NJ - 1 - i) if reverse else (lambda i: i)

    def local(i, hcar):
        hr, hi = hcar
        r0 = pl.multiple_of(order(i) * 8, 8)
        hr, hi = mr * hr - mi * hi + xre[pl.ds(r0, 8), :], mr * hi + mi * hr + xim[pl.ds(r0, 8), :]
        xre[pl.ds(r0, 8), :] = hr
        xim[pl.ds(r0, 8), :] = hi
        return hr, hi

    z8 = jnp.zeros((8, SW), F32)
    er, ei = lax.fori_loop(0, NJ, local, (z8, z8), unroll=4)
    c0r, c0i = car[0], car[1]
    er, ei = _tile_scan(er, ei, a[2:], c0r, c0i, reverse)
    rowid = lax.broadcasted_iota(jnp.int32, (8, SW), 0)
    first, sh, last = (7, 7, 0) if reverse else (0, 1, 7)
    cvr = jnp.where(rowid == first, c0r, pltpu.roll(er, sh, 0))
    cvi = jnp.where(rowid == first, c0i, pltpu.roll(ei, sh, 0))
    car[0] = jnp.broadcast_to(er[last:last + 1, :], (8, SW))
    car[1] = jnp.broadcast_to(ei[last:last + 1, :], (8, SW))

    def fix(i, carry):
        j = order(i)
        r0 = pl.multiple_of(j * 8, 8)
        pr, pi = pexp[0, j], pexp[1, j]
        sr = xre[pl.ds(r0, 8), :] + pr * cvr - pi * cvi
        si = xim[pl.ds(r0, 8), :] + pr * cvi + pi * cvr
        xre[pl.ds(r0, 8), :] = sr
        xim[pl.ds(r0, 8), :] = si
        if extra is None:
            return carry
        return (sr, si, extra(r0, sr, si, carry[0], carry[1], carry[2]))

    init = (cvr, cvi, extra(None, None, None, None, None, None)) if extra is not None else 0
    return lax.fori_loop(0, NJ, fix, init, unroll=4)


def _s5_time_block(z, s, t, adjoint):
    flip = (1 - z) if adjoint else z
    return s * (L // TT) + t + flip * (L // TT - 1 - 2 * t)


def _s5_fwd(h, bre, bim, cre, cim, tab):
    nt = L // TT
    taba, tabp = tab

    def body(u_ref, bre_ref, bim_ref, cre_ref, cim_ref, a_ref, p_ref, hre_ref, him_ref, y_ref, car, pexp):
        z = pl.program_id(1)
        s = pl.program_id(2)
        tc = pl.program_id(3)

        @pl.when(tc == 0)
        def _():
            car[...] = jnp.zeros_like(car)

        @pl.when((tc == 0) & (s == 0))
        def _():
            _expand_powers(p_ref, pexp)

        u = _to_lockstep(u_ref)
        hre_ref[0] = _mm(u, bre_ref[0, 0])
        him_ref[0] = _mm(u, bim_ref[0, 0])

        @pl.when(z == 0)
        def _():
            _lockstep_scan(hre_ref.at[0], him_ref.at[0], a_ref, pexp, car, False)

        @pl.when(z == 1)
        def _():
            _lockstep_scan(hre_ref.at[0], him_ref.at[0], a_ref, pexp, car, True)

        _from_lockstep(_mm(hre_ref[0], cre_ref[0, 0]) - _mm(him_ref[0], cim_ref[0, 0]), y_ref, 0)

    tb = lambda b, z, s, t: _s5_time_block(z, s, t, False)
    wspec = lambda r, c: pl.BlockSpec((1, 1, r, c), lambda b, z, s, t: (z, b, 0, 0))
    return pl.pallas_call(
        body, grid=(2, 2, NSEQ, nt),
        in_specs=[pl.BlockSpec((TT, 128), lambda b, z, s, t: (tb(b, z, s, t), b)),
                  wspec(128, SW), wspec(128, SW), wspec(SW, 128), wspec(SW, 128),
                  pl.BlockSpec((1, 1, 10, 8, SW), lambda b, z, s, t: (z, b, 0, 0, 0)),
                  pl.BlockSpec((1, 1, 2, NJ, SW), lambda b, z, s, t: (z, b, 0, 0, 0))],
        out_specs=[pl.BlockSpec((1, TT, SW), lambda b, z, s, t: (z, tb(b, z, s, t), b)),
                   pl.BlockSpec((1, TT, SW), lambda b, z, s, t: (z, tb(b, z, s, t), b)),
                   pl.BlockSpec((1, TT, 128), lambda b, z, s, t: (z, tb(b, z, s, t), b))],
        out_shape=[_sds((2, N, 2 * SW)), _sds((2, N, 2 * SW)), _sds((2, N, 256))],
        scratch_shapes=[pltpu.VMEM((2, 8, SW), F32), pltpu.VMEM((2, NJ, 8, SW), F32)],
        name="s5_fwd", compiler_params=_cp(("arbitrary",) * 4))(h, bre, bim, cre, cim, taba, tabp)


def _s5_bwd(h, dyp, hre, him, bre, bim, cre, cim, tabc):
    nt = L // TT
    taba, tabp = tabc

    def body(u_ref, dy_ref, hre_ref, him_ref, bre_ref, bim_ref, cre_ref, cim_ref, a_ref, p_ref,
             du_ref, dbre_ref, dbim_ref, dcre_ref, dcim_ref, dmu_ref, gre, gim, car, acc, macc, pexp):
        z = pl.program_id(1)
        s = pl.program_id(2)
        tc = pl.program_id(3)

        @pl.when(tc == 0)
        def _():
            car[...] = jnp.zeros_like(car)

        @pl.when((tc == 0) & (s == 0))
        def _():
            acc[...] = jnp.zeros_like(acc)
            macc[...] = jnp.zeros_like(macc)
            _expand_powers(p_ref, pexp)

        dy = _to_lockstep(dy_ref)
        gre[...] = _mm_nt(dy, cre_ref[0, 0])
        gim[...] = -_mm_nt(dy, cim_ref[0, 0])

        def run(reverse):
            def pair(r0, gr_, gi_, pvr, pvi, m):
                if r0 is None:
                    return (macc[0], macc[1])
                hr = hre_ref[0, pl.ds(r0, 8), :]
                hi = him_ref[0, pl.ds(r0, 8), :]
                return (m[0] + pvr * hr + pvi * hi, m[1] + pvi * hr - pvr * hi)

            _, _, (dmr, dmi) = _lockstep_scan(gre, gim, a_ref, pexp, car, reverse, pair)
            macc[0] = dmr
            macc[1] = dmi

        @pl.when(z == 0)
        def _():
            run(True)

        @pl.when(z == 1)
        def _():
            run(False)

        gr = gre[...]
        gi = gim[...]
        u = _to_lockstep(u_ref)
        _from_lockstep(_mm_nt(gr, bre_ref[0, 0]) + _mm_nt(gi, bim_ref[0, 0]), du_ref, 0)
        acc[0] += _mm_tn(u, gr)
        acc[1] += _mm_tn(u, gi)
        acc[2] += _mm_tn(dy, hre_ref[0])
        acc[3] -= _mm_tn(dy, him_ref[0])

        @pl.when((tc == nt - 1) & (s == NSEQ - 1))
        def _():
            grp = lax.broadcasted_iota(jnp.int32, (S5_H, SW), 1) // S5_P
            for k, out in enumerate((dbre_ref, dbim_ref, dcre_ref, dcim_ref)):
                c = jnp.zeros((S5_H, SW), F32)
                for i in range(8):
                    c = c + jnp.where(grp == i, acc[k, i * S5_H:(i + 1) * S5_H, :], 0.0)
                out[0, 0] = c
            dmu_ref[0, 0] = jnp.concatenate([jnp.sum(macc[0], axis=0, keepdims=True),
                                             jnp.sum(macc[1], axis=0, keepdims=True)], axis=0)

    tb = lambda b, z, s, t: _s5_time_block(z, s, t, True)
    wspec = lambda r, c: pl.BlockSpec((1, 1, r, c), lambda b, z, s, t: (z, b, 0, 0))
    tok = lambda w_: pl.BlockSpec((TT, w_), lambda b, z, s, t: (tb(b, z, s, t), b))
    st = pl.BlockSpec((1, TT, SW), lambda b, z, s, t: (z, tb(b, z, s, t), b))
    return pl.pallas_call(
        body, grid=(2, 2, NSEQ, nt),
        in_specs=[tok(128), tok(128), st, st, wspec(128, SW), wspec(128, SW), wspec(SW, 128), wspec(SW, 128),
                  pl.BlockSpec((1, 1, 10, 8, SW), lambda b, z, s, t: (z, b, 0, 0, 0)),
                  pl.BlockSpec((1, 1, 2, NJ, SW), lambda b, z, s, t: (z, b, 0, 0, 0))],
        out_specs=[pl.BlockSpec((1, TT, 128), lambda b, z, s, t: (z, tb(b, z, s, t), b)),
                   wspec(S5_H, SW), wspec(S5_H, SW), wspec(S5_H, SW), wspec(S5_H, SW),
                   wspec(2, SW)],
        out_shape=[_sds((2, N, 256))] + [_sds((2, 2, S5_H, SW))] * 4 + [_sds((2, 2, 2, SW))],
        scratch_shapes=[pltpu.VMEM((TT, SW), F32), pltpu.VMEM((TT, SW), F32), pltpu.VMEM((2, 8, SW), F32),
                        pltpu.VMEM((4, 128, SW), F32), pltpu.VMEM((2, 8, SW), F32), pltpu.VMEM((2, NJ, 8, SW), F32)],
        name="s5_bwd", compiler_params=_cp(("arbitrary",) * 4))(h, dyp, hre, him, bre, bim, cre, cim, taba, tabp)


_GELU_C = math.sqrt(2.0 / math.pi)


def _gelu(y):
    return 0.5 * y * (1.0 + jnp.tanh(_GELU_C * (y + 0.044715 * y * y * y)))


def _gelu_grad(y):
    t = jnp.tanh(_GELU_C * (y + 0.044715 * y * y * y))
    return 0.5 * (1.0 + t) + 0.5 * y * (1.0 - t * t) * _GELU_C * (1.0 + 3 * 0.044715 * y * y)


def _glu_halves(w4_ref):
    return (jnp.concatenate([w4_ref[0], w4_ref[1]], axis=1), jnp.concatenate([w4_ref[2], w4_ref[3]], axis=1))


def _s5_glu_fwd(y2, h, dsk, w4, bv, bg):
    tm = 512

    def body(y2_ref, u_ref, d_ref, w4_ref, bv_ref, bg_ref, ya_ref):
        wv, wg = _glu_halves(w4_ref)
        z = _gelu(y2_ref[0] + y2_ref[1] + d_ref[...] * u_ref[...])
        val = _mm(z, wv) + bv_ref[...]
        gate = _mm(z, wg) + bg_ref[...]
        ya_ref[...] = (val * jax.nn.sigmoid(gate)).astype(MX)

    full = lambda r, c: pl.BlockSpec((r, c), lambda i: (0, 0))
    return pl.pallas_call(
        body, grid=(N // tm,),
        in_specs=[pl.BlockSpec((2, tm, 256), lambda i: (0, i, 0)), pl.BlockSpec((tm, 256), lambda i: (i, 0)),
                  full(1, 256), pl.BlockSpec((NSHARD, 256, 128), lambda i: (0, 0, 0)), full(1, 256), full(1, 256)],
        out_specs=pl.BlockSpec((tm, 256), lambda i: (i, 0)),
        out_shape=_sds((N, 256), MX), name="s5_glu_fwd", compiler_params=_cp(("parallel",)))(y2, h, dsk, w4, bv, bg)


def _s5_glu_bwd(y2, h, dsk, w4, bv, bg, dya):
    tm = 512
    nt = N // tm

    def body(y2_ref, u_ref, d_ref, w4_ref, bv_ref, bg_ref, dya_ref,
             dyp_ref, dud_ref, dd_ref, dw4_ref, dbv_ref, dbg_ref, accv, accg):
        i = pl.program_id(0)

        @pl.when(i == 0)
        def _():
            for r in (dd_ref, accv, accg, dbv_ref, dbg_ref):
                r[...] = jnp.zeros_like(r)

        wv, wg = _glu_halves(w4_ref)
        u = u_ref[...]
        y = y2_ref[0] + y2_ref[1] + d_ref[...] * u
        z = _gelu(y)
        val = _mm(z, wv) + bv_ref[...]
        sig = jax.nn.sigmoid(_mm(z, wg) + bg_ref[...])
        dya = dya_ref[...]
        dval = dya * sig
        dgate = dya * val * sig * (1.0 - sig)
        dz = _mm_nt(dval, wv) + _mm_nt(dgate, wg)
        dy = dz * _gelu_grad(y)
        dyp_ref[...] = dy
        dud_ref[...] = (dy * d_ref[...]).astype(MX)
        dd_ref[...] += jnp.sum(dy * u, axis=0, keepdims=True)
        accv[...] += _mm_tn(z, dval)
        accg[...] += _mm_tn(z, dgate)
        dbv_ref[...] += jnp.sum(dval, axis=0, keepdims=True)
        dbg_ref[...] += jnp.sum(dgate, axis=0, keepdims=True)

        @pl.when(i == nt - 1)
        def _():
            dw4_ref[0] = accv[:, 0:128].astype(MX)
            dw4_ref[1] = accv[:, 128:256].astype(MX)
            dw4_ref[2] = accg[:, 0:128].astype(MX)
            dw4_ref[3] = accg[:, 128:256].astype(MX)

    full = lambda r, c: pl.BlockSpec((r, c), lambda i: (0, 0))
    row = pl.BlockSpec((tm, 256), lambda i: (i, 0))
    wspec = pl.BlockSpec((NSHARD, 256, 128), lambda i: (0, 0, 0))
    return pl.pallas_call(
        body, grid=(nt,),
        in_specs=[pl.BlockSpec((2, tm, 256), lambda i: (0, i, 0)), row, full(1, 256), wspec, full(1, 256), full(1, 256),
                  row],
        out_specs=[row, row, full(1, 256), wspec, full(1, 256), full(1, 256)],
        out_shape=[_sds((N, 256)), _sds((N, 256), MX), _sds((1, 256)), _sds((NSHARD, 256, 128), MX), _sds((1, 256)),
                   _sds((1, 256))],
        scratch_shapes=[pltpu.VMEM((256, 256), F32), pltpu.VMEM((256, 256), F32)],
        name="s5_glu_bwd", compiler_params=_cp(("arbitrary",)))(y2, h, dsk, w4, bv, bg, dya)


def _logsig(x):
    return jnp.minimum(x, 0.0) - jnp.log(1.0 + jnp.exp(-jnp.abs(x)))


def _gla_gate_bwd(h, wa, ba, dla_f, dla_b):
    tm = 512

    def body(hl_ref, wa_ref, ba_ref, df_ref, db_ref, dhl_ref, dwa_ref, dba_ref):
        i = pl.program_id(0)

        @pl.when(i == 0)
        def _():
            dwa_ref[...] = jnp.zeros_like(dwa_ref)
            dba_ref[...] = jnp.zeros_like(dba_ref)

        hl = hl_ref[...]
        pre = _mm(hl, wa_ref[...]) + ba_ref[...]
        dpre = jnp.concatenate([df_ref[...], db_ref[...]], axis=1) * (1.0 / 16.0) * jax.nn.sigmoid(-pre)
        dhl_ref[...] = _mm_nt(dpre, wa_ref[...]).astype(MX)
        dwa_ref[...] += _mm_tn(hl, dpre)[0:32]
        dba_ref[...] += jnp.sum(dpre, axis=0, keepdims=True)

    row = pl.BlockSpec((tm, 128), lambda i: (i, 0))
    return pl.pallas_call(
        body, grid=(N // tm,),
        in_specs=[pl.BlockSpec((tm, 128), lambda i: (i, 14)), pl.BlockSpec((128, 256), lambda i: (0, 0)),
                  pl.BlockSpec((1, 256), lambda i: (0, 0)), row, row],
        out_specs=[row, pl.BlockSpec((32, 256), lambda i: (0, 0)), pl.BlockSpec((1, 256), lambda i: (0, 0))],
        out_shape=[_sds((N, 128), MX), _sds((32, 256)), _sds((1, 256))],
        name="gla_gate_bwd", compiler_params=_cp(("arbitrary",)))(h, wa, ba, dla_f, dla_b)


def _gla_chunk(q, k, v, la, st, rev):
    c = GLA_CHUNK
    rows = q.shape[0]
    nch = rows // c
    b = _cums(la, rev)
    blc = [jnp.sum(la[i * c:(i + 1) * c], axis=0, keepdims=True) for i in range(nch)]
    bl = jnp.concatenate([jnp.broadcast_to(t, (c, 128)) for t in blc], axis=0)
    q_in = q * (32.0 ** -0.5) * jnp.exp(b)
    k_in = k * jnp.exp(-b)
    k_st = k * jnp.exp(bl - b)
    lane_k = lax.broadcasted_iota(jnp.int32, (1, 128), 1) // 32
    lane_v = lax.broadcasted_iota(jnp.int32, (1, 256), 1) // 64
    qs = jnp.concatenate([jnp.where(lane_k == hd, q_in, 0.0) for hd in range(4)], axis=0)
    a = _dmm_nt(qs, k_in)
    a = jnp.where(jnp.concatenate([_chunk_pairs(rows, rev, rev)] * 4, axis=0), a, 0.0)
    o4 = _dmm(a, v)
    o = jnp.zeros((rows, 256), F32)
    for hd in range(4):
        o = o + jnp.where(lane_v == hd, o4[hd * rows:(hd + 1) * rows], 0.0)
    bd = (lax.broadcasted_iota(jnp.int32, (256, 128), 0) // 64) == (lax.broadcasted_iota(jnp.int32, (256, 128), 1) // 32)
    inter = [None] * nch
    for i in (reversed(range(nch)) if rev else range(nch)):
        sl = slice(i * c, (i + 1) * c)
        inter[i] = _dmm_nt(q_in[sl], st)
        st = jnp.exp(blc[i]) * st + jnp.where(bd, _dmm_tn(v[sl], k_st[sl]), 0.0)
    return o + jnp.concatenate(inter, axis=0), st


def _gla_chunk_of(c, rev):
    return NGROUP - 1 - c if rev else c


def _gla_fwd(h, la2):
    c = GLA_GROUP * GLA_CHUNK

    def body(qf, kf, vf, laf, qb, kb, vb, lab, of_ref, ob_ref, sf_ref, sb_ref, stf, stb):
        @pl.when(pl.program_id(0) == 0)
        def _():
            stf[...] = jnp.zeros_like(stf)
            stb[...] = jnp.zeros_like(stb)

        ins = [(qf[s], kf[s], vf[s], laf[s], stf[s], qb[s], kb[s], vb[s], lab[s], stb[s]) for s in range(NSEQ)]
        outs = [(_gla_chunk(*t[:5], False), _gla_chunk(*t[5:], True)) for t in ins]
        for s in range(NSEQ):
            sf_ref[s, 0] = ins[s][4]
            sb_ref[s, 0] = ins[s][9]
            (of_ref[s], stf[s]), (ob_ref[s], stb[s]) = outs[s]

    def specs(rev):
        ch = lambda i: _gla_chunk_of(i, rev)
        return [pl.BlockSpec((NSEQ, c, 128), lambda i: (0, ch(i), 2)), pl.BlockSpec((NSEQ, c, 128), lambda i: (0, ch(i), 3)),
                pl.BlockSpec((NSEQ, c, 256), lambda i: (0, ch(i), 2)),
                pl.BlockSpec((NSEQ, c, 128), lambda i: (0, ch(i), 1 if rev else 0))]

    orow = lambda rev: pl.BlockSpec((NSEQ, c, 256), lambda i: (0, _gla_chunk_of(i, rev), 0))
    srow = lambda rev: pl.BlockSpec((NSEQ, 1, 256, 128), lambda i: (0, _gla_chunk_of(i, rev), 0, 0))
    h3, la3 = h.reshape(NSEQ, L, DINP), la2.reshape(NSEQ, L, 256)
    of, ob, sf, sb = pl.pallas_call(
        body, grid=(NGROUP,),
        in_specs=specs(False) + specs(True),
        out_specs=[orow(False), orow(True), srow(False), srow(True)],
        out_shape=[_sds((NSEQ, L, 256)), _sds((NSEQ, L, 256)), _sds((NSEQ, NGROUP, 256, 128)),
                   _sds((NSEQ, NGROUP, 256, 128))],
        scratch_shapes=[pltpu.VMEM((NSEQ, 256, 128), F32), pltpu.VMEM((NSEQ, 256, 128), F32)],
        name="gla_fwd", compiler_params=_cp(("arbitrary",)))(h3, h3, h3, la3, h3, h3, h3, la3)
    return of.reshape(N, 256), ob.reshape(N, 256), sf, sb


def _gla_bwd(h, la2, do, sf, sb):
    c = GLA_GROUP * GLA_CHUNK

    def body(qf, kf, vf, laf, dof, sfr, qb, kb, vb, lab, dob, sbr,
             dqf, dkf, dvf, dlf, dqb, dkb, dvb, dlb, dstf, dstb):
        @pl.when(pl.program_id(0) == 0)
        def _():
            dstf[...] = jnp.zeros_like(dstf)
            dstb[...] = jnp.zeros_like(dstb)

        def one(s, q, k, v, la, do_, st, dst, rev):
            _, vjp = jax.vjp(functools.partial(_gla_chunk, rev=rev), q[s], k[s], v[s], la[s], st[s, 0])
            return vjp((do_[s], dst[s]))

        res = [(one(s, qf, kf, vf, laf, dof, sfr, dstf, False), one(s, qb, kb, vb, lab, dob, sbr, dstb, True))
               for s in range(NSEQ)]
        for s in range(NSEQ):
            for (gq, gk, gv, gl, gs), (dq, dk, dv, dl, dst) in ((res[s][0], (dqf, dkf, dvf, dlf, dstf)),
                                                                  (res[s][1], (dqb, dkb, dvb, dlb, dstb))):
                dq[s], dk[s], dv[s] = gq.astype(MX), gk.astype(MX), gv.astype(MX)
                dl[s], dst[s] = gl, gs

    def specs(rev):
        ch = lambda i: _gla_chunk_of(i, not rev)
        return [pl.BlockSpec((NSEQ, c, 128), lambda i: (0, ch(i), 2)), pl.BlockSpec((NSEQ, c, 128), lambda i: (0, ch(i), 3)),
                pl.BlockSpec((NSEQ, c, 256), lambda i: (0, ch(i), 2)),
                pl.BlockSpec((NSEQ, c, 128), lambda i: (0, ch(i), 1 if rev else 0)),
                pl.BlockSpec((NSEQ, c, 256), lambda i: (0, ch(i), 0)),
                pl.BlockSpec((NSEQ, 1, 256, 128), lambda i: (0, ch(i), 0, 0))]

    def ospecs(rev):
        ch = lambda i: _gla_chunk_of(i, not rev)
        n = pl.BlockSpec((NSEQ, c, 128), lambda i: (0, ch(i), 0))
        return [n, n, pl.BlockSpec((NSEQ, c, 256), lambda i: (0, ch(i), 0)), n]

    oshape = [_sds((NSEQ, L, 128), MX), _sds((NSEQ, L, 128), MX), _sds((NSEQ, L, 256), MX), _sds((NSEQ, L, 128))]
    h3, la3, do3 = h.reshape(NSEQ, L, DINP), la2.reshape(NSEQ, L, 256), do.reshape(NSEQ, L, 256)
    res = pl.pallas_call(
        body, grid=(NGROUP,),
        in_specs=specs(False) + specs(True),
        out_specs=ospecs(False) + ospecs(True),
        out_shape=oshape + oshape,
        scratch_shapes=[pltpu.VMEM((NSEQ, 256, 128), F32), pltpu.VMEM((NSEQ, 256, 128), F32)],
        name="gla_bwd", compiler_params=_cp(("arbitrary",)))(h3, h3, h3, la3, do3, sf, h3, h3, h3, la3, do3, sb)
    return [r.reshape(N, r.shape[-1]) for r in res]


def _gla_post(of, ob, r, g):
    o = of + ob
    head = lax.broadcasted_iota(jnp.int32, (1, 256), 1) // 64
    mu = jnp.zeros_like(o)
    for hd in range(4):
        mu = mu + jnp.where(head == hd, jnp.sum(jnp.where(head == hd, o, 0.0), axis=-1, keepdims=True) * (1.0 / 64.0), 0.0)
    xc = o - mu
    var = jnp.zeros_like(o)
    for hd in range(4):
        var = var + jnp.where(head == hd, jnp.sum(jnp.where(head == hd, xc * xc, 0.0), axis=-1, keepdims=True) * (1.0 / 64.0), 0.0)
    return xc * lax.rsqrt(var + LN_EPS) * g * (r * jax.nn.sigmoid(r))


def _gla_post_fwd(of, ob, h, g):
    tm = 512

    def body(of_ref, ob_ref, r_ref, g_ref, y_ref):
        y_ref[...] = _gla_post(of_ref[...], ob_ref[...], r_ref[...], g_ref[...]).astype(MX)

    row = pl.BlockSpec((tm, 256), lambda i: (i, 0))
    return pl.pallas_call(
        body, grid=(N // tm,),
        in_specs=[row, row, pl.BlockSpec((tm, 256), lambda i: (i, 3)), pl.BlockSpec((1, 256), lambda i: (0, 0))],
        out_specs=row, out_shape=_sds((N, 256), MX), name="gla_post_fwd", compiler_params=_cp(("parallel",)))(of, ob, h, g)


def _gla_post_bwd(of, ob, h, g, dyb):
    tm = 512

    def body(of_ref, ob_ref, r_ref, g_ref, dy_ref, do_ref, dr_ref, dg_ref):
        @pl.when(pl.program_id(0) == 0)
        def _():
            dg_ref[...] = jnp.zeros_like(dg_ref)

        _, vjp = jax.vjp(_gla_post, of_ref[...], ob_ref[...], r_ref[...], g_ref[...])
        go, _, gr, gg = vjp(dy_ref[...])
        do_ref[...] = go
        dr_ref[...] = gr.astype(MX)
        dg_ref[...] += gg

    row = pl.BlockSpec((tm, 256), lambda i: (i, 0))
    one = pl.BlockSpec((1, 256), lambda i: (0, 0))
    return pl.pallas_call(
        body, grid=(N // tm,),
        in_specs=[row, row, pl.BlockSpec((tm, 256), lambda i: (i, 3)), one, row],
        out_specs=[row, row, one], out_shape=[_sds((N, 256)), _sds((N, 256), MX), _sds((1, 256))],
        name="gla_post_bwd", compiler_params=_cp(("arbitrary",)))(of, ob, h, g, dyb)


def _rope_tables(width):
    pos = jnp.arange(L, dtype=F32)
    inv_freq = ROPE_THETA ** (-jnp.arange(0, ROT, 2, dtype=F32) / ROT)
    ang = pos[:, None] * inv_freq[None, :]
    cos, sin = jnp.cos(ang), jnp.sin(ang)
    one = jnp.ones((L, 64 - ROT), F32)
    zero = jnp.zeros((L, 64 - ROT), F32)
    z8 = jnp.zeros((L, ROT // 2), F32)
    c = jnp.concatenate([cos, cos, one], axis=1)
    sa = jnp.concatenate([z8, sin, zero], axis=1)
    sb = jnp.concatenate([-sin, z8, zero], axis=1)
    rep = width // 64
    return jnp.stack([jnp.tile(c, (1, rep)), jnp.tile(sa, (1, rep)), jnp.tile(sb, (1, rep))])


def _pieces(t, f):
    out = [f(t[:, c * 128:(c + 1) * 128]) for c in range(t.shape[-1] // 128)]
    return out[0] if len(out) == 1 else jnp.concatenate(out, axis=1)


def _rope(t, tab):
    return _pieces(t, lambda x: x * tab[0] + pltpu.roll(x, ROT // 2, 1) * tab[1] + pltpu.roll(x, 128 - ROT // 2, 1) * tab[2])


def _rope_t(g, tab):
    return _pieces(g, lambda x: x * tab[0] + pltpu.roll(x * tab[1], 128 - ROT // 2, 1) + pltpu.roll(x * tab[2], ROT // 2, 1))


def _swa_pad_kv(kv_ref, tk_ref, kexp, vexp):
    z = jnp.zeros((SWA_BLK, 256), F32)
    kr = _rope(kv_ref[:, 0:128], tk_ref[...])
    for hk in range(2):
        for pad in (kexp, vexp):
            pad[hk, 0:SWA_BLK] = z
            pad[hk, SWA_BLK + L:] = z
        kexp[hk, SWA_BLK:SWA_BLK + L] = _swa_expand(kr, hk)
        vexp[hk, SWA_BLK:SWA_BLK + L] = _swa_expand(kv_ref[:, 128:256], hk)


def _swa_expand(x, hk):
    lane = lax.broadcasted_iota(jnp.int32, x.shape, 1)
    sw = pltpu.roll(x, 64, 1)
    pair = jnp.where(lane < 64, x, sw) if hk == 0 else jnp.where(lane < 64, sw, x)
    return jnp.concatenate([pair, pair], axis=1)


def _swa_fold(x, hk):
    a = x[:, 0:128] + x[:, 128:256]
    t = a + pltpu.roll(a, 64, 1)
    lane = lax.broadcasted_iota(jnp.int32, a.shape, 1)
    return jnp.where((lane < 64) if hk == 0 else (lane >= 64), t, 0.0)


def _swa_probs(q2, kexp, n, sink_ref, hk):
    slot = lax.broadcasted_iota(jnp.int32, (1, 256), 1) // 64
    qs = jnp.concatenate([jnp.where(slot == g, q2, 0.0) for g in range(4)], axis=0)
    s = _mm_nt(qs, kexp) * 0.125
    i = lax.broadcasted_iota(jnp.int32, (SWA_BLK, 3 * SWA_BLK), 0)
    j = lax.broadcasted_iota(jnp.int32, (SWA_BLK, 3 * SWA_BLK), 1)
    kpos = n * SWA_BLK - SWA_BLK + j
    ok = (j - i >= 0) & (j - i <= 2 * SWA_BLK) & (kpos >= 0) & (kpos < L)
    s = jnp.where(jnp.concatenate([ok] * 4, axis=0), s, NEG_BIG)
    rowg = lax.broadcasted_iota(jnp.int32, (4 * SWA_BLK, 1), 0) // SWA_BLK
    sink = jnp.zeros((4 * SWA_BLK, 1), F32)
    for g in range(4):
        sink = jnp.where(rowg == g, sink_ref[hk * 4 + g], sink)
    m = jnp.maximum(jnp.max(s, axis=-1, keepdims=True), sink)
    p = jnp.exp(s - m)
    ps = jnp.exp(sink - m)
    inv = 1.0 / (jnp.sum(p, axis=-1, keepdims=True) + ps)
    return qs, p * inv, ps * inv, slot, rowg


def _swa_qtab(tk_ref, r0):
    return [tk_ref[i, pl.ds(r0, SWA_BLK), :] for i in range(3)]


def _swa_fwd(h, tk, sink):
    def body(sink_ref, q_ref, kv_ref, tk_ref, y_ref, kexp, vexp):
        n = pl.program_id(1)

        @pl.when(n == 0)
        def _():
            _swa_pad_kv(kv_ref, tk_ref, kexp, vexp)

        for t in range(SWA_PER):
            blk = n * SWA_PER + t
            rows = slice(t * SWA_BLK, (t + 1) * SWA_BLK)
            r0 = pl.multiple_of(blk * SWA_BLK, SWA_BLK)
            q = _rope(q_ref[rows, :], _swa_qtab(tk_ref, r0))
            for hk in range(2):
                _, p, _, slot, _ = _swa_probs(q[:, hk * 256:(hk + 1) * 256], kexp[hk, pl.ds(r0, 3 * SWA_BLK), :], blk,
                                              sink_ref, hk)
                o4 = _mm(p, vexp[hk, pl.ds(r0, 3 * SWA_BLK), :])
                o = jnp.zeros((SWA_BLK, 256), F32)
                for g in range(4):
                    o = o + jnp.where(slot == g, o4[g * SWA_BLK:(g + 1) * SWA_BLK], 0.0)
                y_ref[rows, hk * 256:(hk + 1) * 256] = o.astype(MX)

    tm = SWA_PER * SWA_BLK
    return pl.pallas_call(
        body,
        grid_spec=pltpu.PrefetchScalarGridSpec(
            num_scalar_prefetch=1, grid=(NSEQ, L // tm),
            in_specs=[pl.BlockSpec((tm, 512), lambda s, n, sk: (s * (L // tm) + n, 2)),
                      pl.BlockSpec((L, 256), lambda s, n, sk: (s, 6)),
                      pl.BlockSpec((3, L, 128), lambda s, n, sk: (0, 0, 0))],
            out_specs=pl.BlockSpec((tm, 512), lambda s, n, sk: (s * (L // tm) + n, 0)),
            scratch_shapes=[pltpu.VMEM((2, L + 2 * SWA_BLK, 256), F32), pltpu.VMEM((2, L + 2 * SWA_BLK, 256), F32)]),
        out_shape=_sds((N, 512), MX), name="swa_fwd", compiler_params=_cp(("arbitrary", "arbitrary")))(sink, h, h, tk)


def _swa_bwd(h, tk, sink, dyc):
    tm = SWA_PER * SWA_BLK

    def body(sink_ref, q_ref, kv_ref, tk_ref, dy_ref, dq_ref, dkv_ref, dsink_ref, kexp_all, vexp_all, dkacc, dvacc):
        sq = pl.program_id(0)
        n = pl.program_id(1)

        @pl.when(n == 0)
        def _():
            _swa_pad_kv(kv_ref, tk_ref, kexp_all, vexp_all)
            dkacc[...] = jnp.zeros_like(dkacc)
            dvacc[...] = jnp.zeros_like(dvacc)

        @pl.when((n == 0) & (sq == 0))
        def _():
            dsink_ref[...] = jnp.zeros_like(dsink_ref)

        hrow = lax.broadcasted_iota(jnp.int32, (8, 128), 0)
        dsk = jnp.zeros((8, 128), F32)
        for t in range(SWA_PER):
            blk = n * SWA_PER + t
            rows = slice(t * SWA_BLK, (t + 1) * SWA_BLK)
            r0 = pl.multiple_of(blk * SWA_BLK, SWA_BLK)
            tq = _swa_qtab(tk_ref, r0)
            q = _rope(q_ref[rows, :], tq)
            for hk in range(2):
                kexp = kexp_all[hk, pl.ds(r0, 3 * SWA_BLK), :]
                vexp = vexp_all[hk, pl.ds(r0, 3 * SWA_BLK), :]
                qs, p, ps, slot, rowg = _swa_probs(q[:, hk * 256:(hk + 1) * 256], kexp, blk, sink_ref, hk)
                dy2 = dy_ref[rows, hk * 256:(hk + 1) * 256]
                dos = jnp.concatenate([jnp.where(slot == g, dy2, 0.0) for g in range(4)], axis=0)
                dp = _mm_nt(dos, vexp)
                delta = jnp.sum(p * dp, axis=-1, keepdims=True)
                ds = p * (dp - delta) * 0.125
                dsr = -ps * delta
                for g in range(4):
                    dsk = dsk + jnp.where(hrow == hk * 4 + g,
                                          jnp.sum(jnp.where(rowg == g, dsr, 0.0), axis=0, keepdims=True), 0.0)
                dq4 = _mm(ds, kexp)
                dq2 = jnp.zeros((SWA_BLK, 256), F32)
                for g in range(4):
                    dq2 = dq2 + jnp.where(slot == g, dq4[g * SWA_BLK:(g + 1) * SWA_BLK], 0.0)
                dq_ref[rows, hk * 256:(hk + 1) * 256] = _rope_t(dq2, tq).astype(MX)
                dkacc[hk, pl.ds(r0, 3 * SWA_BLK), :] += _mm_tn(ds, qs)
                dvacc[hk, pl.ds(r0, 3 * SWA_BLK), :] += _mm_tn(p, dos)
        dsink_ref[...] += dsk

        @pl.when(n == L // tm - 1)
        def _():
            seq = slice(SWA_BLK, SWA_BLK + L)
            dk = _rope_t(_swa_fold(dkacc[0, seq], 0) + _swa_fold(dkacc[1, seq], 1), tk_ref[...])
            dkv_ref[:, 0:128] = dk.astype(MX)
            dkv_ref[:, 128:256] = (_swa_fold(dvacc[0, seq], 0) + _swa_fold(dvacc[1, seq], 1)).astype(MX)

    blk = lambda col: pl.BlockSpec((tm, 512), lambda s, n, sk: (s * (L // tm) + n, col))
    pad = pltpu.VMEM((2, L + 2 * SWA_BLK, 256), F32)
    return pl.pallas_call(
        body,
        grid_spec=pltpu.PrefetchScalarGridSpec(
            num_scalar_prefetch=1, grid=(NSEQ, L // tm),
            in_specs=[blk(2), pl.BlockSpec((L, 256), lambda s, n, sk: (s, 6)),
                      pl.BlockSpec((3, L, 128), lambda s, n, sk: (0, 0, 0)), blk(0)],
            out_specs=[blk(0), pl.BlockSpec((L, 256), lambda s, n, sk: (s, 0)),
                       pl.BlockSpec((8, 128), lambda s, n, sk: (0, 0))],
            scratch_shapes=[pad, pad, pad, pad]),
        out_shape=[_sds((N, 512), MX), _sds((N, 256), MX), _sds((8, 128))],
        name="swa_bwd", compiler_params=_cp(("arbitrary", "arbitrary")))(sink, h, h, tk, dyc)


def _outproj_bwd(dx1, s1, ya, yb, yc, wo, g):
    tm = 512
    nt = N // tm

    def body(dx1_ref, s_ref, ya_ref, yb_ref, yc_ref, wo_ref, g_ref,
             dya_ref, dyb_ref, dyc_ref, dxp_ref, dwo_ref, dg_ref, db_ref, acc):
        i = pl.program_id(0)

        @pl.when(i == 0)
        def _():
            acc[...] = jnp.zeros_like(acc)
            dg_ref[...] = jnp.zeros_like(dg_ref)
            db_ref[...] = jnp.zeros_like(db_ref)

        ds, dg, db = _ln_bwd(dx1_ref[...], s_ref[...], g_ref[...])
        dg_ref[...] += dg
        db_ref[...] += db
        dxp_ref[...] = ALPHA * ds
        dy = _mm_nt(ds, wo_ref[...])
        dya_ref[...] = dy[:, 0:256]
        dyb_ref[...] = dy[:, 256:512]
        dyc_ref[...] = dy[:, 512:1024]
        acc[0:256] += _mm_tn(ya_ref[...], ds)
        acc[256:512] += _mm_tn(yb_ref[...], ds)
        acc[512:1024] += _mm_tn(yc_ref[...], ds)

        @pl.when(i == nt - 1)
        def _():
            dwo_ref[...] = acc[...].astype(MX)

    row = lambda w_: pl.BlockSpec((tm, w_), lambda i: (i, 0))
    one = pl.BlockSpec((1, D), lambda i: (0, 0))
    full = pl.BlockSpec((D, D), lambda i: (0, 0))
    return pl.pallas_call(
        body, grid=(nt,),
        in_specs=[row(D), row(D), row(256), row(256), row(512), full, one],
        out_specs=[row(256), row(256), row(512), row(D), full, one, one],
        out_shape=[_sds((N, 256)), _sds((N, 256)), _sds((N, 512)), _sds((N, D)), _sds((D, D), MX), _sds((1, D)), _sds((1, D))],
        scratch_shapes=[pltpu.VMEM((D, D), F32)],
        name="outproj_bwd", compiler_params=_cp(("arbitrary",)))(dx1, s1, ya, yb, yc, wo, g)


def _mix_ffn_fwd(ya, yb, yc, x, wo, g1, b1, w1, w2, g, b, target=None):
    tm = FFN_TM
    head = target is not None

    def body(*refs):
        ya_ref, yb_ref, yc_ref, xin_ref, wo_ref, g1_ref, b1_ref, w1_ref, w2_ref, g_ref, b_ref = refs[:11]
        s1_ref, x1_ref, a_ref, s_ref, y_ref = refs[11 + head:16 + head]
        mix = _mm(ya_ref[...], wo_ref[0:256]) + _mm(yb_ref[...], wo_ref[256:512]) + _mm(yc_ref[...], wo_ref[512:1024])
        s1 = ALPHA * xin_ref[...] + mix
        s1_ref[...] = s1
        x = _ln_fwd(s1, g1_ref[...], b1_ref[...])
        x1_ref[...] = x
        xb = x.astype(MX)
        s = ALPHA * x
        for j in range(NSHARD):
            a = _mm(xb, w1_ref[j])
            a_ref[:, j * D:(j + 1) * D] = a.astype(MX)
            s = s + _mm(jnp.square(jnp.maximum(a, 0.0)), w2_ref[j])
        s_ref[...] = s
        x2 = _ln_fwd(s, g_ref[...], b_ref[...])
        if not head:
            y_ref[...] = x2
            return
        l_ref = refs[16 + head]

        @pl.when(pl.program_id(0) == 0)
        def _():
            l_ref[...] = jnp.zeros_like(l_ref)

        e = x2 - refs[11][...]
        y_ref[...] = e * (1.0 / D)
        l_ref[...] += jnp.sum(jnp.sum(e * e, axis=1, keepdims=True), axis=0, keepdims=True) * (0.5 / D)

    rw = lambda w_: pl.BlockSpec((tm, w_), lambda i: (i, 0))
    row = rw(D)
    once = dict(pipeline_mode=pl.Buffered(1))
    wall = pl.BlockSpec((NSHARD, D, D), lambda i: (0, 0, 0), **once)
    one = pl.BlockSpec((1, D), lambda i: (0, 0))
    acc = pl.BlockSpec((8, 128), lambda i: (0, 0))
    return pl.pallas_call(
        body, grid=(N // tm,),
        in_specs=[rw(256), rw(256), rw(512), row, pl.BlockSpec((D, D), lambda i: (0, 0), **once), one, one,
                  wall, wall, one, one] + [row] * head,
        out_specs=[row, row, pl.BlockSpec((tm, DFF), lambda i: (i, 0)), row, row] + [acc] * head,
        out_shape=[_sds((N, D)), _sds((N, D)), _sds((N, DFF), MX), _sds((N, D)), _sds((N, D))] + [_sds((8, 128))] * head,
        name="mix_ffn_fwd", compiler_params=_cp(("arbitrary",), FFN_VMEM))(
            ya, yb, yc, x, wo, g1, b1, w1, w2, g, b, *([target] * head))


def _ffn_bwd_act(dy, s2, a, w1, w2, g):
    tm = FFN_TM

    def body(dy_ref, s_ref, a_ref, w1_ref, w2_ref, g_ref, da_ref, ds_ref, dx1_ref, dg_ref, db_ref):
        @pl.when(pl.program_id(0) == 0)
        def _():
            dg_ref[...] = jnp.zeros_like(dg_ref)
            db_ref[...] = jnp.zeros_like(db_ref)

        ds, dg, db = _ln_bwd(dy_ref[...], s_ref[...], g_ref[...])
        dsb = ds.astype(MX)
        ds_ref[...] = dsb
        dg_ref[...] += dg
        db_ref[...] += db
        dx1 = ALPHA * ds
        for j in range(NSHARD):
            da = (_mm_nt(dsb, w2_ref[j]) * 2.0 * jnp.maximum(a_ref[:, j * D:(j + 1) * D].astype(F32), 0.0)).astype(MX)
            da_ref[:, j * D:(j + 1) * D] = da
            dx1 = dx1 + _mm_nt(da, w1_ref[j])
        dx1_ref[...] = dx1

    row = pl.BlockSpec((tm, D), lambda i: (i, 0))
    wide = pl.BlockSpec((tm, DFF), lambda i: (i, 0))
    wall = pl.BlockSpec((NSHARD, D, D), lambda i: (0, 0, 0))
    one = pl.BlockSpec((1, D), lambda i: (0, 0))
    return pl.pallas_call(
        body, grid=(N // tm,),
        in_specs=[row, row, wide, wall, wall, one],
        out_specs=[wide, row, row, one, one],
        out_shape=[_sds((N, DFF), MX), _sds((N, D), MX), _sds((N, D)), _sds((1, D)), _sds((1, D))],
        name="ffn_bwd_act", compiler_params=_cp(("arbitrary",), FFN_VMEM))(dy, s2, a, w1, w2, g)


def _ffn_bwd_w(x1, da, a, ds):
    tm, nb = FFN_TM_W, FFN_WB
    nt = N // tm

    def body(x_ref, da_ref, a_ref, ds_ref, dw1_ref, dw2_ref, acc1, acc2):
        i = pl.program_id(1)

        @pl.when(i == 0)
        def _():
            acc1[...] = jnp.zeros_like(acc1)
            acc2[...] = jnp.zeros_like(acc2)

        x, ds_ = x_ref[...], ds_ref[...]
        for k in range(nb):
            cols = slice(k * D, (k + 1) * D)
            acc1[k] += _mm_tn(x, da_ref[:, cols])
            acc2[k] += _mm_tn(jnp.square(jnp.maximum(a_ref[:, cols].astype(F32), 0.0)), ds_)

        @pl.when(i == nt - 1)
        def _():
            dw1_ref[...] = acc1[...].astype(MX)
            dw2_ref[...] = acc2[...].astype(MX)

    row = pl.BlockSpec((tm, D), lambda j, i: (i, 0))
    col = pl.BlockSpec((tm, nb * D), lambda j, i: (i, j))
    wj = pl.BlockSpec((nb, D, D), lambda j, i: (j, 0, 0))
    return pl.pallas_call(
        body, grid=(NSHARD // nb, nt),
        in_specs=[row, col, col, row], out_specs=[wj, wj],
        out_shape=[_sds((NSHARD, D, D), MX), _sds((NSHARD, D, D), MX)],
        scratch_shapes=[pltpu.VMEM((nb, D, D), F32), pltpu.VMEM((nb, D, D), F32)],
        name="ffn_bwd_w", compiler_params=_cp(("parallel", "arbitrary"), FFN_VMEM))(x1, da, a, ds)


def _s5_discretize(a_re, a_im, log_step, b_re, b_im):
    lam = lax.complex(a_re, a_im)
    lam_bar = jnp.exp(lam * jnp.exp(log_step))
    b_bar = ((lam_bar - 1.0) / lam)[..., None] * lax.complex(b_re, b_im)
    return jnp.real(lam_bar), jnp.imag(lam_bar), jnp.real(b_bar), jnp.imag(b_bar)


def _s5_in_blocks(b):
    e = jnp.eye(8, dtype=F32)
    return jnp.einsum('ij,zbjph->zbihjp', e, b.reshape(2, 2, 8, S5_P, S5_H)).reshape(2, 2, 128, SW)


def _s5_out_blocks(c):
    e = jnp.eye(8, dtype=F32)
    return jnp.einsum('ij,zbjhp->zbjpih', e, c.reshape(2, 2, 8, S5_H, S5_P)).reshape(2, 2, SW, 128)


def _gate_weight(w_a):
    z = jnp.zeros((16, 128), F32)
    top = jnp.concatenate([w_a[0], z], axis=1)
    bot = jnp.concatenate([z, w_a[1]], axis=1)
    return jnp.concatenate([top, bot, jnp.zeros((96, 256), F32)], axis=0)


def _layer_prep(p):
    lr, li, br, bi = _s5_discretize(p["s5_a_re"], p["s5_a_im"], p["s5_log_step"], p["s5_b_re"], p["s5_b_im"])
    q = dict(p)
    q["bre"] = _s5_in_blocks(br).astype(MX)
    q["bim"] = _s5_in_blocks(bi).astype(MX)
    q["cre"] = _s5_out_blocks(p["s5_c_re"]).astype(MX)
    q["cim"] = _s5_out_blocks(p["s5_c_im"]).astype(MX)
    mr, mi = lr.reshape(2, 1024), li.reshape(2, 1024)
    q["tab"], q["tabc"] = _lockstep_tables(mr, mi)
    q["dsk"] = p["s5_d"].reshape(1, 256)
    q["wa"] = _gate_weight(p["gla_w_a"]).astype(MX)
    q["ba"] = p["gla_b_a"].reshape(1, 256)
    q["lng"] = p["gla_ln_g"].reshape(1, 256)
    q["bv"] = p["s5_b_glu"][:256].reshape(1, 256)
    q["bg"] = p["s5_b_glu"][256:].reshape(1, 256)
    for k in ("ln1_g", "ln1_b", "ln2_g", "ln2_b"):
        q[k] = p[k].reshape(1, D)
    return q


def _layer_fwd(x, q, tk, fetch, target=None):
    q["w_in"] = fetch("w_in", x)
    h, la2 = _inproj_fwd(x, q["w_in"], q["wa"], q["ba"])
    hre, him, y2 = _s5_fwd(h, q["bre"], q["bim"], q["cre"], q["cim"], q["tab"])
    q["w4"] = fetch("s5_w_glu", y2)
    ya = _s5_glu_fwd(y2, h, q["dsk"], q["w4"], q["bv"], q["bg"])
    of, ob, sf, sb = _gla_fwd(h, la2)
    yb = _gla_post_fwd(of, ob, h, q["lng"])
    yc = _swa_fwd(h, tk, q["swa_sink"])
    mixed = ya[:8, :128] + yb[:8, :128] + yc[:8, :128]
    q["w_out"] = fetch("w_out", mixed)
    q["w_ff1"] = fetch("w_ff1", mixed)
    q["w_ff2"] = fetch("w_ff2", mixed)
    s1, x1, a, s2, *out = _mix_ffn_fwd(ya, yb, yc, x, q["w_out"], q["ln1_g"], q["ln1_b"], q["w_ff1"], q["w_ff2"],
                                       q["ln2_g"], q["ln2_b"], target)
    saved = dict(x=x, h=h, hre=hre, him=him, y2=y2, ya=ya, la2=la2, of=of, ob=ob, sf=sf, sb=sb, yb=yb, yc=yc,
                 s1=s1, x1=x1, a=a, s2=s2)
    return (out[0] if target is None else tuple(out)), saved


def _layer_bwd(dy, q, sv, tk, emit):
    g = {}
    da, ds2, dx1, g["dg2"], g["db2"] = _ffn_bwd_act(dy, sv["s2"], sv["a"], q["w_ff1"], q["w_ff2"], q["ln2_g"])
    dw1, dw2 = _ffn_bwd_w(sv["x1"], da, sv["a"], ds2)
    tie = emit(dict(w_ff1=dw1, w_ff2=dw2))
    dya, dyb, dyc, dxp, dwo, g["dg1"], g["db1"] = _outproj_bwd(dx1, sv["s1"], sv["ya"], sv["yb"], sv["yc"],
                                                               q["w_out"], q["ln1_g"] + tie)
    h = sv["h"]
    daq, dakv, g["dsink"] = _swa_bwd(h, tk, q["swa_sink"], dyc)
    do, gr, g["dlng"] = _gla_post_bwd(sv["of"], sv["ob"], h, q["lng"], dyb)
    gq_f, gk_f, gv_f, gl_f, gq_b, gk_b, gv_b, gl_b = _gla_bwd(h, sv["la2"], do, sv["sf"], sv["sb"])
    dhl, g["dwa"], g["dba"] = _gla_gate_bwd(h, q["wa"], q["ba"], gl_f, gl_b)
    dyp, dud, g["dd"], dw4, g["dbv"], g["dbg"] = _s5_glu_bwd(sv["y2"], h, q["dsk"], q["w4"], q["bv"], q["bg"], dya)
    tie = emit(dict(w_out=dwo.reshape(NSHARD, D // NSHARD, D), s5_w_glu=dw4))
    du2, g["dbre"], g["dbim"], g["dcre"], g["dcim"], g["dmu"] = _s5_bwd(
        h, dyp, sv["hre"], sv["him"], q["bre"], q["bim"], q["cre"], q["cim"], (q["tabc"][0], q["tabc"][1] + tie))
    dx, dwt = _inproj_bwd(sv["x"], q["w_in"], dxp, du2, dud, gq_f, gq_b, gk_f, gk_b, gv_f, gv_b, gr, daq, dakv, dhl)
    tie = emit(dict(w_in=dwt))
    return dx, g, tie


NATIVE = ("dmu", "dbre", "dbim", "dcre", "dcim", "dd", "dbv", "dbg", "dwa", "dba", "dlng", "dsink",
          "dg1", "db1", "dg2", "db2", "loss")
ICI_CORE = (0, 0, 0, 1, 1, 0, 0, 0, 1, 1, 1, 1, 0, 0, 1, 1, 0)
Y_FIRST = (0, 0, 1, 0, 1, 1, 0, 1, 0, 1, 0, 1, 0, 1, 0, 1, 0)


def _finish_small(n, w):
    g = {}
    dmu = n["dmu"]
    dlr = dmu[:, :, :, 0].reshape(DEPTH, 2, S5_G, S5_P)
    dli = dmu[:, :, :, 1].reshape(DEPTH, 2, S5_G, S5_P)

    def unblock(c, perm, shape):
        return c.reshape(DEPTH, 2, 2, S5_H, 8, S5_P).transpose(perm).reshape(shape)

    b_shape, c_shape = (DEPTH, 2, S5_G, S5_P, S5_H), (DEPTH, 2, S5_G, S5_H, S5_P)
    _, vjp = jax.vjp(_s5_discretize, w["s5_a_re"], w["s5_a_im"], w["s5_log_step"], w["s5_b_re"], w["s5_b_im"])
    (g["s5_a_re"], g["s5_a_im"], g["s5_log_step"], g["s5_b_re"], g["s5_b_im"]) = vjp(
        (dlr, dli, unblock(n["dbre"], (0, 1, 2, 4, 5, 3), b_shape), unblock(n["dbim"], (0, 1, 2, 4, 5, 3), b_shape)))
    g["s5_c_re"] = unblock(n["dcre"], (0, 1, 2, 4, 3, 5), c_shape)
    g["s5_c_im"] = unblock(n["dcim"], (0, 1, 2, 4, 3, 5), c_shape)
    g["s5_d"] = n["dd"].reshape(DEPTH, S5_G, S5_H)
    g["s5_b_glu"] = jnp.concatenate([n["dbv"], n["dbg"]], axis=2).reshape(DEPTH, 512)
    g["gla_w_a"] = jnp.stack([n["dwa"][:, 0:16, 0:128], n["dwa"][:, 16:32, 128:256]], axis=1)
    g["gla_b_a"] = n["dba"].reshape(DEPTH, 2, 128)
    g["gla_ln_g"] = n["dlng"].reshape(DEPTH, 256)
    g["swa_sink"] = n["dsink"][:, :, 0]
    for k, s in (("ln1_g", "dg1"), ("ln1_b", "db1"), ("ln2_g", "dg2"), ("ln2_b", "db2")):
        g[k] = n[s].reshape(DEPTH, D)
    return g


def _local_step(x, target, qs, tk, fetch, emit):
    saved = []
    for l, q in enumerate(qs):
        x, sv = _layer_fwd(x, q, tk, functools.partial(fetch, l), target if l == DEPTH - 1 else None)
        saved.append(sv)
    dy, lacc = x
    smalls = [None] * DEPTH
    tie = 0.0
    for l in reversed(range(DEPTH)):
        qs[l]["ln2_g"] = qs[l]["ln2_g"] + tie
        dy, smalls[l], tie = _layer_bwd(dy, qs[l], saved[l], tk, functools.partial(emit, l))
    smalls[0]["db2"] = smalls[0]["db2"] + tie
    for l in range(DEPTH):
        smalls[l]["loss"] = lacc if l == 0 else jnp.zeros_like(lacc)
    return lacc[0, 0], dy, smalls


BIG = ("w_in", "s5_w_glu", "w_out", "w_ff1", "w_ff2")
SMALL = ("s5_a_re", "s5_a_im", "s5_log_step", "s5_b_re", "s5_b_im", "s5_c_re", "s5_c_im", "s5_d", "s5_b_glu",
         "gla_w_a", "gla_b_a", "gla_ln_g", "swa_sink", "ln1_g", "ln1_b", "ln2_g", "ln2_b")
ANY = pl.BlockSpec(memory_space=pl.ANY)


def _place():
    x, y, c = lax.axis_index("x"), lax.axis_index("y"), lax.axis_index("c")
    return x, y, c, [(1 - x, y), (x, 1 - y), (1 - x, 1 - y)]


HBM = pl.BlockSpec(memory_space=pltpu.HBM)
SEMS = pl.BlockSpec(memory_space=pltpu.SEMAPHORE)
EFFECT = pltpu.SideEffectType.DATAFLOW_SIDE_EFFECTING


def _push_copies(ins, lands, send, recv, gather, sending):
    x, y, c, chips = _place()
    me = 2 * x + y
    if gather == "sibling":
        return [pltpu.make_async_remote_copy(src_ref=ins[a], dst_ref=lands[a], send_sem=send.at[a], recv_sem=recv.at[a],
                                             device_id=(x, y, 1 - c), device_id_type=MESH) for a in range(len(lands))]
    out = []
    for a in range(len(lands)):
        for j, (px, py) in enumerate(chips):
            peer = 2 * px + py
            src = lands[a].at[me] if gather else ins[a].at[peer if sending else me]
            dst = lands[a].at[me if sending else peer]
            out.append(pltpu.make_async_remote_copy(src_ref=src, dst_ref=dst, send_sem=send.at[3 * a + j],
                                                    recv_sem=recv.at[3 * a + j], device_id=(px, py, c),
                                                    device_id_type=MESH))
    return out


def _push_start(name, arrs, gather):
    n = len(arrs)
    ops = list(arrs) if gather is True else list(arrs) + [lax.empty(s.shape, s.dtype) for s in arrs]
    m = len(ops)

    def body(*refs):
        ins, lnd = (refs[:n], refs[:n]) if gather is True else (refs[:n], refs[n:m])
        for cp in _push_copies(ins, lnd, refs[m], refs[m + 1], gather, True):
            cp.start()
        refs[-1][...] = jnp.zeros((8, 128), F32)

    ops = [pltpu.with_memory_space_constraint(t, pltpu.HBM) for t in ops]
    res = pl.pallas_call(
        body, name=name,
        out_shape=(pltpu.SemaphoreType.DMA((3 * n,)), pltpu.SemaphoreType.DMA((3 * n,)),
                   *[pltpu.HBM(t.shape, t.dtype) for t in ops], _sds((8, 128))),
        in_specs=[HBM] * m,
        out_specs=(SEMS, SEMS, *[HBM] * m, pl.BlockSpec(memory_space=pltpu.VMEM)),
        input_output_aliases={i: 2 + i for i in range(m)},
        compiler_params=pltpu.CompilerParams(has_side_effects=EFFECT))(*ops)
    return res[0], res[1], list(res[2:2 + m]), res[-1]


def _push_wait(name, started, after, gather):
    send, recv, ops, _ = started
    m = len(ops)
    n = m if gather is True else m // 2

    def body(*refs):
        ins, lnd = (refs[:n], refs[:n]) if gather is True else (refs[:n], refs[n:m])
        for cp in _push_copies(ins, lnd, refs[m], refs[m + 1], gather, False):
            cp.wait_send()
            cp.wait_recv()

    res = pl.pallas_call(
        body, name=name,
        out_shape=[pltpu.HBM(t.shape, t.dtype) for t in ops],
        in_specs=[HBM] * m + [SEMS, SEMS, ANY], out_specs=[HBM] * m,
        input_output_aliases={i: i for i in range(m)},
        compiler_params=pltpu.CompilerParams(has_side_effects=EFFECT))(*ops, send, recv, after)
    return list(res)


def _row_tile(rows):
    return max(t for t in range(8, min(rows, 512) + 1, 8) if rows % t == 0)


def _cast_to_slot(me, w, l):
    _, rows, cols = w.shape
    tr = _row_tile(rows)

    def body(me_ref, w_ref, o_ref):
        o_ref[0] = w_ref[0].astype(MX)

    return pl.pallas_call(
        body,
        grid_spec=pltpu.PrefetchScalarGridSpec(
            num_scalar_prefetch=1, grid=(rows // tr,),
            in_specs=[pl.BlockSpec((1, tr, cols), lambda i, me_: (l, i, 0))],
            out_specs=pl.BlockSpec((1, tr, cols), lambda i, me_: (me_[0], i, 0))),
        out_shape=_sds((NSHARD, rows, cols), MX), name="cast_to_slot", compiler_params=_cp(("arbitrary",)))(me, w)


def _sum_sources(me, recv, own):
    _, rows, cols = recv[0].shape
    tr = min(_row_tile(rows), 256) if rows % 256 == 0 else _row_tile(rows)
    nt = rows // tr

    def body(me_ref, *refs):
        o_ref = refs[-1]
        for l in range(DEPTH):
            @pl.when(pl.program_id(0) == l)
            def _():
                r_ref, own_ref = refs[2 * l], refs[2 * l + 1]
                part = [jnp.where(me_ref[0] == s, own_ref[0], r_ref[s]).astype(F32) for s in range(NSHARD)]
                o_ref[...] = ((part[0] + part[1]) + part[2]) + part[3]

    in_specs = []
    for l in range(DEPTH):
        pick = lambda g, i, me_, l=l: jnp.where(g == l, i, jnp.where(g < l, 0, nt - 1))
        in_specs += [pl.BlockSpec((NSHARD, tr, cols), lambda g, i, me_, pick=pick: (0, pick(g, i, me_), 0)),
                     pl.BlockSpec((1, tr, cols), lambda g, i, me_, pick=pick: (me_[0], pick(g, i, me_), 0))]
    return pl.pallas_call(
        body,
        grid_spec=pltpu.PrefetchScalarGridSpec(
            num_scalar_prefetch=1, grid=(DEPTH, nt), in_specs=in_specs,
            out_specs=pl.BlockSpec((tr, cols), lambda g, i, me_: (g * nt + i, 0))),
        out_shape=_sds((DEPTH * rows, cols)), name="sum_sources",
        compiler_params=_cp(("arbitrary", "arbitrary")))(me, *[t for l in range(DEPTH) for t in (recv[l], own[l])])


def _allreduce_small(per_layer):
    nk = len(per_layer[0])
    n = DEPTH * nk
    shapes = [a.shape for a in per_layer[0]]

    def body(*refs):
        ins, outs = refs[:n], refs[n:n + nk]
        sibs, slots = refs[n + nk:n + 2 * nk], refs[n + 2 * nk:n + 3 * nk]
        send, recv = refs[n + 3 * nk:]
        x, y, c, chips = _place()
        me = 2 * x + y
        d2d = [pltpu.make_async_remote_copy(src_ref=ins[l * nk + k], dst_ref=sibs[k].at[l], send_sem=send.at[l * nk + k],
                                            recv_sem=recv.at[l * nk + k], device_id=(x, y, 1 - c), device_id_type=MESH)
               for l in range(DEPTH) for k in range(nk)]
        for cp in d2d:
            cp.start()
        for cp in d2d:
            cp.wait()
        for l in range(DEPTH):
            for k in range(nk):
                slots[k][0, l] = ins[l * nk + k][...] + sibs[k][l]

        def swap(k, stage):
            peer = (1 - x, y, c) if stage == Y_FIRST[k] else (x, 1 - y, c)
            return pltpu.make_async_remote_copy(src_ref=slots[k].at[2 * stage], dst_ref=slots[k].at[2 * stage + 1],
                                                send_sem=send.at[n + 3 * k + stage], recv_sem=recv.at[n + 3 * k + stage],
                                                device_id=peer, device_id_type=MESH)

        def handover(k):
            return pltpu.make_async_remote_copy(src_ref=outs[k], dst_ref=outs[k], send_sem=send.at[n + 3 * nk + k],
                                                recv_sem=recv.at[n + 3 * nk + k], device_id=(x, y, 1 - c),
                                                device_id_type=MESH)

        halves = (tuple(k for k in range(nk) if ICI_CORE[k] == 0), tuple(k for k in range(nk) if ICI_CORE[k] == 1))
        for cc in range(2):
            @pl.when(c == cc)
            def _():
                mine, theirs = halves[cc], halves[1 - cc]
                for stage in range(2):
                    cps = [swap(k, stage) for k in mine]
                    for cp in cps:
                        cp.start()
                    for cp in cps:
                        cp.wait()
                    for k in mine:
                        if stage == 0:
                            slots[k][2] = slots[k][0] + slots[k][1]
                        else:
                            outs[k][...] = slots[k][2] + slots[k][3]
                over = [handover(k) for k in mine]
                for cp in over:
                    cp.start()
                for k in theirs:
                    handover(k).wait_recv()
                for cp in over:
                    cp.wait_send()

    vm = pl.BlockSpec(memory_space=pltpu.VMEM)
    return pl.pallas_call(
        body, in_specs=[vm] * n, out_specs=[vm] * nk, out_shape=[_sds((DEPTH,) + s) for s in shapes],
        scratch_shapes=([pltpu.VMEM((DEPTH,) + s, F32) for s in shapes]
                        + [pltpu.VMEM((NSHARD, DEPTH) + s, F32) for s in shapes]
                        + [pltpu.SemaphoreType.DMA((n + 4 * nk,)), pltpu.SemaphoreType.DMA((n + 4 * nk,))]),
        name="allreduce_small", compiler_params=pltpu.CompilerParams(vmem_limit_bytes=VMEM_LIMIT))(
            *[a for layer in per_layer for a in layer])


def _adamw_math(w, g, m, v):
    m = ADAM_B1 * m + (1.0 - ADAM_B1) * g
    v = ADAM_B2 * v + (1.0 - ADAM_B2) * jnp.square(g)
    m_hat = m / (1.0 - ADAM_B1 ** ADAM_STEP)
    v_hat = v / (1.0 - ADAM_B2 ** ADAM_STEP)
    delta = -ADAM_LR * (m_hat / (jnp.sqrt(v_hat) + ADAM_EPS) + ADAM_WD * w)
    return delta, m, v


def _adamw(g_parts, w, m, v):
    rows, cols = w.shape
    tr = 256 if rows % 256 == 0 else _row_tile(rows)
    k = len(g_parts)

    def body(*refs):
        g = refs[0][...]
        for r in refs[1:k]:
            g = g + r[...]
        w_ref, m_ref, v_ref, go, do, mo, vo = refs[k:]
        d, mn, vn = _adamw_math(w_ref[...], g, m_ref[...], v_ref[...])
        go[...] = g
        do[...] = d
        mo[...] = mn
        vo[...] = vn

    spec = pl.BlockSpec((tr, cols), lambda i: (i, 0))
    return pl.pallas_call(
        body, grid=(rows // tr,), in_specs=[spec] * (k + 3), out_specs=[spec] * 4,
        out_shape=[_sds((rows, cols))] * 4, name="adamw", compiler_params=_cp(("parallel",)))(*g_parts, w, m, v)


def _adamw_small(gs, ws, ms, vs):
    n = len(gs)

    def body(*refs):
        for k in range(n):
            d, mn, vn = _adamw_math(refs[n + k][...], refs[k][...], refs[2 * n + k][...], refs[3 * n + k][...])
            refs[4 * n + k][...] = d
            refs[5 * n + k][...] = mn
            refs[6 * n + k][...] = vn

    vm = pl.BlockSpec(memory_space=pltpu.VMEM)
    shapes = [_sds(a.shape) for a in ws]
    res = pl.pallas_call(
        body, in_specs=[vm] * (4 * n), out_specs=[vm] * (3 * n), out_shape=shapes * 3, name="adamw_small",
        compiler_params=pltpu.CompilerParams(vmem_limit_bytes=VMEM_LIMIT))(*gs, *ws, *ms, *vs)
    return res[:n], res[n:2 * n], res[2 * n:]


_ARGS = ("x", "w_in", "s5_a_re", "s5_a_im", "s5_log_step", "s5_b_re", "s5_b_im", "s5_c_re", "s5_c_im", "s5_d",
         "s5_w_glu", "s5_b_glu", "gla_w_a", "gla_b_a", "gla_ln_g", "swa_sink", "w_out", "ln1_g", "ln1_b", "w_ff1",
         "w_ff2", "ln2_g", "ln2_b")
_WEIGHTS = _ARGS[1:]


def kernel(x, w_in, s5_a_re, s5_a_im, s5_log_step, s5_b_re, s5_b_im, s5_c_re, s5_c_im, s5_d, s5_w_glu, s5_b_glu, gla_w_a, gla_b_a, gla_ln_g, swa_sink, w_out, ln1_g, ln1_b, w_ff1, w_ff2, ln2_g, ln2_b, loss_target, m_w_in, m_s5_a_re, m_s5_a_im, m_s5_log_step, m_s5_b_re, m_s5_b_im, m_s5_c_re, m_s5_c_im, m_s5_d, m_s5_w_glu, m_s5_b_glu, m_gla_w_a, m_gla_b_a, m_gla_ln_g, m_swa_sink, m_w_out, m_ln1_g, m_ln1_b, m_w_ff1, m_w_ff2, m_ln2_g, m_ln2_b, v_w_in, v_s5_a_re, v_s5_a_im, v_s5_log_step, v_s5_b_re, v_s5_b_im, v_s5_c_re, v_s5_c_im, v_s5_d, v_s5_w_glu, v_s5_b_glu, v_gla_w_a, v_gla_b_a, v_gla_ln_g, v_swa_sink, v_w_out, v_ln1_g, v_ln1_b, v_w_ff1, v_w_ff2, v_ln2_g, v_ln2_b):
    given = dict(locals())
    w = {k: given[k] for k in _WEIGHTS}
    mom = {k: given["m_" + k] for k in _WEIGHTS}
    var = {k: given["v_" + k] for k in _WEIGHTS}

    me = (2 * lax.axis_index("x") + lax.axis_index("y")).astype(jnp.int32).reshape(1)
    tr = lambda t: t.transpose(0, 2, 1)
    shard = {k: (tr(w[k]) if k == "w_in" else w[k]) for k in BIG}
    qs = [None] * DEPTH

    first = ("w_in", "s5_w_glu", "w_out")
    follow = {(0, "w_in"): [(0, first[1:]), (0, BIG[3:])], (0, "s5_w_glu"): [(1, first)], (0, "w_ff1"): [(1, BIG[3:])]}
    gathers = {}

    casts = {}

    def start_gather(l, names, behind=None):
        lands = [casts.pop((l, k)) if (l, k) in casts else _cast_to_slot(me, shard[k], l) for k in names]
        if behind is not None:
            lands, behind = lax.optimization_barrier((lands, behind))
        st = _push_start(f"gather_start_{l}_{names[0]}", lands, True)
        for k in names:
            gathers[l, k] = [names, st, None]
        return st[-1], behind

    token = start_gather(0, first[:1])[0]
    zero = token[0, 0]
    for l in range(DEPTH):
        for k in BIG:
            if (l, k) not in gathers:
                casts[l, k] = _cast_to_slot(me, lax.optimization_barrier((shard[k], token))[0], l)
        qs[l] = _layer_prep({k: (w[k][l] + zero if k == "s5_a_re" else w[k][l]) for k in SMALL})
    token, casts, qs = lax.optimization_barrier((token, casts, qs))

    def fetch(l, name, after):
        names, st, got = gathers[l, name]
        tie = None
        if got is None:
            if l == 0 and name == "w_in":
                after = token
            lands = _push_wait(f"gather_wait_{l}_{names[0]}", st, after, True)
            for l2, names2 in follow.get((l, name), ()):
                tok, lands[0] = start_gather(l2, names2, lands[0])
                tie = tok if tie is None else tie + tok
            got = dict(zip(names, lands))
            for k in names:
                gathers[l, k][2] = got
        full = got[name]
        if name == "w_in":
            return _in_rows(full, token if tie is None else tie)
        if tie is not None:
            near = "bv" if name == "s5_w_glu" else "ln2_b"
            qs[l][near] = qs[l][near] + tie[0, 0]
        return full.reshape(D, D) if name == "w_out" else full

    scatters, held = [], {}

    def emit(l, grads):
        if l > 0:
            held.update(grads)
            if "w_in" not in grads:
                return 0.0
            grads = dict(held)
            held.clear()
        names = tuple(grads)
        st = _push_start(f"scatter_start_{l}_{names[0]}", [grads[k] for k in names], False)
        scatters.append((l, names, st))
        return st[-1][0, 0]

    loss, dx, smalls = _local_step(x.reshape(N, D), loss_target.reshape(N, D), qs, _rope_tables(128), fetch, emit)

    out, recv, own = {}, {}, {}

    def collect(keys, after):
        for l, names, st in scatters:
            if names[0] in keys:
                ops = _push_wait(f"scatter_wait_{l}_{names[0]}", st, after, False)
                for i, k in enumerate(names):
                    own[l, k], recv[l, k] = ops[i], ops[len(names) + i]

    def shard_sums(keys):
        return [_sum_sources(me, [recv[l, k] for l in range(DEPTH)], [own[l, k] for l in range(DEPTH)]) for k in keys]

    def to_sibling(keys, sums):
        return _push_start(f"swap_start_{keys[0]}", sums, "sibling")

    def apply(keys, started, after):
        ops = _push_wait(f"swap_wait_{keys[0]}", started, after, "sibling")
        for i, k in enumerate(keys):
            mine, other = ops[i], ops[len(keys) + i]
            shp = shard[k].shape
            r = _adamw([mine, other], *((tr(t[k]) if k == "w_in" else t[k]).reshape(-1, shp[-1]) for t in (w, mom, var)))
            r = [t.reshape(shp) for t in r]
            out[k] = [tr(t) for t in r] if k == "w_in" else r
        return out[keys[-1]][1]

    collect(("w_ff1", "w_ff2", "w_out", "s5_w_glu"), dx)
    sums = shard_sums(("w_ff1", "w_ff2", "w_out", "s5_w_glu"))
    sums, smalls[0]["db1"] = lax.optimization_barrier((sums, smalls[0]["db1"]))
    native = _allreduce_small([[smalls[l][k] for k in NATIVE] for l in range(DEPTH)])
    sums, native = lax.optimization_barrier((sums, native))
    ff = to_sibling(("w_ff1", "w_ff2"), sums[:2])
    mix = to_sibling(("w_out", "s5_w_glu"), sums[2:])
    native = dict(zip(NATIVE, native))
    native["db1"] = native["db1"] + (ff[-1][0, 0] + mix[-1][0, 0])
    loss = native["loss"][0, 0, 0] + native["loss"][1, 0, 0]
    gsmall = _finish_small(native, w)
    view = lambda k, t: t.transpose(0, 1, 2, 4, 3) if k in ("s5_b_re", "s5_b_im") else t
    res = _adamw_small(*([view(k, t[k]) for k in SMALL] for t in (gsmall, w, mom, var)))
    for i, k in enumerate(SMALL):
        out[k] = [gsmall[k]] + [view(k, r[i]) for r in res]
    last = apply(("w_ff1", "w_ff2"), ff, res[0][-1])
    collect(("w_in",), last)
    win = to_sibling(("w_in",), shard_sums(("w_in",)))
    last = apply(("w_out", "s5_w_glu"), mix, win[-1])
    apply(("w_in",), win, last)

    return (loss, dx.reshape(NSEQ, L, D), *[out[k][0] for k in _WEIGHTS], *[out[k][1] for k in _WEIGHTS],
            *[out[k][2] for k in _WEIGHTS], *[out[k][3] for k in _WEIGHTS])
```

```python
import functools
import math

import jax
import jax.numpy as jnp
from jax import lax
from jax.experimental import pallas as pl
from jax.experimental.pallas import tpu as pltpu

F32 = jnp.float32
MX = jnp.bfloat16
MESH = pl.DeviceIdType.MESH

DEPTH = 2
NSEQ = 2
L = 2048
N = NSEQ * L
D = 1024
DFF = 4096
NSHARD = 4
S5_G, S5_H, S5_P = 16, 16, 64
GLA_CHUNK = 64
NCHUNK = L // GLA_CHUNK
GLA_GROUP = 4
NGROUP = NCHUNK // GLA_GROUP
SWA_BLK = 128
NBLK = L // SWA_BLK
SWA_PER = 2
ROT = 16
ROPE_THETA = 500000.0
LN_EPS = 1e-5
ALPHA = (2 * DEPTH) ** 0.25
NEG_BIG = -1e30
DIN = 1824
DINP = 1920
ADAM_LR, ADAM_B1, ADAM_B2, ADAM_EPS, ADAM_WD, ADAM_STEP = 0.001, 0.9, 0.999, 1e-08, 0.01, 10
VMEM_LIMIT = 56 * 1024 * 1024
TT = 512
SW = 512
FFN_TM = 512
FFN_TM_W = 1024
FFN_WB = 1
FFN_VMEM = 60 * 1024 * 1024
INPROJ_BWD_TM = 512


def _cp(sem, vmem=VMEM_LIMIT):
    return pltpu.CompilerParams(dimension_semantics=sem, vmem_limit_bytes=vmem)


def _mm(a, b):
    return jnp.dot(a.astype(MX), b.astype(MX), preferred_element_type=F32)


def _mm_nt(a, b):
    return lax.dot_general(a.astype(MX), b.astype(MX), (((1,), (1,)), ((), ())), preferred_element_type=F32)


def _mm_tn(a, b):
    return lax.dot_general(a.astype(MX), b.astype(MX), (((0,), (0,)), ((), ())), preferred_element_type=F32)


@jax.custom_vjp
def _dmm(a, b):
    return _mm(a, b)


_dmm.defvjp(lambda a, b: (_mm(a, b), (a, b)), lambda r, g: (_mm_nt(g, r[1]), _mm_tn(r[0], g)))


@jax.custom_vjp
def _dmm_nt(a, b):
    return _mm_nt(a, b)


_dmm_nt.defvjp(lambda a, b: (_mm_nt(a, b), (a, b)), lambda r, g: (_mm(g, r[1]), _mm_tn(g, r[0])))


@jax.custom_vjp
def _dmm_tn(a, b):
    return _mm_tn(a, b)


_dmm_tn.defvjp(lambda a, b: (_mm_tn(a, b), (a, b)), lambda r, g: (_mm_nt(r[1], g), _mm(r[0], g)))


def _split3(x):
    hi = x.astype(MX)
    r1 = x - hi.astype(F32)
    mid = r1.astype(MX)
    lo = (r1 - mid.astype(F32)).astype(MX)
    return hi, mid, lo


def _chunk_pairs(rows, rev, strict):
    r = lax.broadcasted_iota(jnp.int32, (rows, rows), 0)
    c = lax.broadcasted_iota(jnp.int32, (rows, rows), 1)
    order = ((c > r) if strict else (c >= r)) if rev else ((c < r) if strict else (c <= r))
    return (r // GLA_CHUNK == c // GLA_CHUNK) & order


def _cums_impl(x, rev):
    rows, w = x.shape
    t = jnp.where(_chunk_pairs(rows, rev, False), 1.0, 0.0).astype(MX)
    s = jnp.dot(t, jnp.concatenate(_split3(x), axis=1), preferred_element_type=F32)
    return s[:, 0:w] + s[:, w:2 * w] + s[:, 2 * w:3 * w]


@functools.partial(jax.custom_vjp, nondiff_argnums=(1,))
def _cums(x, rev):
    return _cums_impl(x, rev)


_cums.defvjp(lambda x, rev: (_cums_impl(x, rev), None), lambda rev, r, g: (_cums_impl(g, not rev),))


def _ln_fwd(s, g, b):
    mu = jnp.mean(s, axis=-1, keepdims=True)
    xc = s - mu
    var = jnp.mean(xc * xc, axis=-1, keepdims=True)
    return xc * lax.rsqrt(var + LN_EPS) * g + b


def _ln_bwd(dy, s, g):
    mu = jnp.mean(s, axis=-1, keepdims=True)
    xc = s - mu
    var = jnp.mean(xc * xc, axis=-1, keepdims=True)
    rstd = lax.rsqrt(var + LN_EPS)
    xhat = xc * rstd
    dxh = dy * g
    ds = rstd * (dxh - jnp.mean(dxh, axis=-1, keepdims=True) - xhat * jnp.mean(dxh * xhat, axis=-1, keepdims=True))
    return ds, jnp.sum(dy * xhat, axis=0, keepdims=True), jnp.sum(dy, axis=0, keepdims=True)


def _sds(shape, dtype=F32):
    return jax.ShapeDtypeStruct(shape, dtype)


_IN_ROW_PIECES = (((0, 0), (0, 456)), ((1, 0), (456, 456)), ((2, 0), (912, 112)), ((2, 112), (1792, 32)),
                  ((2, 144), (1024, 312)), ((3, 0), (1336, 456)))


def _in_rows(g4, behind):
    def body(g_ref, behind_ref, o_ref, tmp):
        tmp[DIN:DINP] = jnp.zeros((DINP - DIN, D), F32)
        for (j, s0), (d0, n_) in _IN_ROW_PIECES:
            tmp[d0:d0 + n_] = g_ref[j, s0:s0 + n_].astype(F32)
        o_ref[...] = tmp[...].astype(MX)

    vm = pl.BlockSpec(memory_space=pltpu.VMEM)
    return pl.pallas_call(body, in_specs=[vm, pl.BlockSpec(memory_space=pl.ANY)], out_specs=vm,
                          out_shape=_sds((DINP, D), MX), scratch_shapes=[pltpu.VMEM((DINP, D), F32)], name="in_rows",
                          compiler_params=pltpu.CompilerParams(vmem_limit_bytes=VMEM_LIMIT))(g4, behind)


def _inproj_fwd(x, wt, wa, ba):
    tm = 512

    def body(x_ref, w_ref, wa_ref, ba_ref, h_ref, la_ref):
        h = _mm_nt(x_ref[...], w_ref[...])
        h_ref[...] = h
        la_ref[...] = _logsig(_mm(h[:, DINP - 128:], wa_ref[...]) + ba_ref[...]) * (1.0 / 16.0)

    return pl.pallas_call(
        body, grid=(N // tm,),
        in_specs=[pl.BlockSpec((tm, D), lambda i: (i, 0)), pl.BlockSpec((DINP, D), lambda i: (0, 0)),
                  pl.BlockSpec((128, 256), lambda i: (0, 0)), pl.BlockSpec((1, 256), lambda i: (0, 0))],
        out_specs=[pl.BlockSpec((tm, DINP), lambda i: (i, 0)), pl.BlockSpec((tm, 256), lambda i: (i, 0))],
        out_shape=[_sds((N, DINP)), _sds((N, 256))], name="inproj_fwd", compiler_params=_cp(("parallel",)))(x, wt, wa, ba)


def _inproj_bwd(x, w, dxp, du2, dud, gq_f, gq_b, gk_f, gk_b, gv_f, gv_b, gr, daq, dakv, dhl):
    tm = INPROJ_BWD_TM
    nt = N // tm

    def body(x_ref, w_ref, dxp_ref, du2_ref, dud_ref, gqf, gqb, gkf, gkb, gvf, gvb, gr_ref, daq_ref, dakv_ref, dhl_ref,
             dx_ref, dw_ref, acc):
        i = pl.program_id(0)
        f = lambda r: r[...].astype(F32)
        dh = jnp.concatenate([
            du2_ref[0] + du2_ref[1] + f(dud_ref), f(gqf) + f(gqb), f(gkf) + f(gkb), f(gvf) + f(gvb),
            f(gr_ref), f(daq_ref), f(dakv_ref), f(dhl_ref)], axis=1)
        dx_ref[...] = dxp_ref[...] + _mm(dh, w_ref[...])
        contrib = _mm_tn(dh, x_ref[...])

        @pl.when(i == 0)
        def _():
            acc[...] = contrib

        @pl.when(i > 0)
        def _():
            acc[...] += contrib

        @pl.when(i == nt - 1)
        def _():
            for (j, d0), (s0, n_) in _IN_ROW_PIECES:
                dw_ref[j, d0:d0 + n_] = acc[s0:s0 + n_].astype(MX)

    row = lambda w_: pl.BlockSpec((tm, w_), lambda i: (i, 0))
    return pl.pallas_call(
        body, grid=(nt,),
        in_specs=[row(D), pl.BlockSpec((DINP, D), lambda i: (0, 0)), row(D),
                  pl.BlockSpec((2, tm, 256), lambda i: (0, i, 0)), row(256), row(128), row(128), row(128), row(128),
                  row(256), row(256), row(256), row(512), row(256), row(128)],
        out_specs=[row(D), pl.BlockSpec((NSHARD, DIN // NSHARD, D), lambda i: (0, 0, 0))],
        out_shape=[_sds((N, D)), _sds((NSHARD, DIN // NSHARD, D), MX)],
        scratch_shapes=[pltpu.VMEM((DINP, D), F32)],
        name="inproj_bwd", compiler_params=_cp(("arbitrary",)))(
            x, w, dxp, du2, dud, gq_f, gq_b, gk_f, gk_b, gv_f, gv_b, gr, daq, dakv, dhl)


def _tile_scan(xr, xi, a, cr, ci, reverse):
    for lvl, d in enumerate((1, 2, 4)):
        sh = 8 - d if reverse else d
        sr = pltpu.roll(xr, sh, 0)
        si = pltpu.roll(xi, sh, 0)
        ar, ai = a[2 * lvl], a[2 * lvl + 1]
        xr, xi = xr + ar * sr - ai * si, xi + ar * si + ai * sr
    pr, pi = a[6], a[7]
    return xr + pr * cr - pi * ci, xi + pr * ci + pi * cr


NJ = TT // 8


def _lockstep_tables(mr, mi):
    def body(mr_ref, mi_ref, a_ref, p_ref, ac_ref, pc_ref):
        rowid = lax.broadcasted_iota(jnp.int32, (8, 2 * SW), 0)

        def mul(a, b):
            return a[0] * b[0] - a[1] * b[1], a[0] * b[1] + a[1] * b[0]

        for z in range(2):
            for sign, reverse, a_out, p_out in ((1.0, z == 1, a_ref, p_ref), (-1.0, z == 0, ac_ref, pc_ref)):
                m = (mr_ref[z:z + 1, :], sign * mi_ref[z:z + 1, :])
                pw = [m]
                for _ in range(NJ - 1):
                    pw.append(mul(pw[-1], m))
                n = pw[-1]
                link = [n]
                for _ in range(7):
                    link.append(mul(link[-1], n))
                tiles = [jnp.broadcast_to(m[0], (8, 2 * SW)), jnp.broadcast_to(m[1], (8, 2 * SW))]
                for d in (1, 2, 4):
                    keep = (rowid <= 7 - d) if reverse else (rowid >= d)
                    tiles += [jnp.where(keep, link[d - 1][c], 0.0) for c in range(2)]
                for c in range(2):
                    t = jnp.zeros((8, 2 * SW), F32)
                    for i in range(8):
                        t = jnp.where(rowid == (7 - i if reverse else i), link[i][c], t)
                    tiles.append(t)
                for blk in range(2):
                    lanes = slice(blk * SW, (blk + 1) * SW)
                    for k, t in enumerate(tiles):
                        a_out[z, blk, k] = t[:, lanes]
                    for j in range(NJ):
                        src = pw[NJ - 1 - j] if reverse else pw[j]
                        for c in range(2):
                            p_out[z, blk, c, j:j + 1, :] = src[c][:, lanes]

    vm = pl.BlockSpec(memory_space=pltpu.VMEM)
    a_shape, p_shape = _sds((2, 2, 10, 8, SW)), _sds((2, 2, 2, NJ, SW))
    a, p, ac, pc = pl.pallas_call(body, in_specs=[vm, vm], out_specs=[vm] * 4, out_shape=[a_shape, p_shape] * 2,
                                  name="s5_tables")(mr, mi)
    return (a, p), (ac, pc)


def _to_lockstep(ref, *lead):
    return jnp.concatenate([ref[(*lead, pl.ds(j, 8, stride=NJ), slice(None))] for j in range(NJ)], axis=0)


def _from_lockstep(val, ref, *lead):
    for j in range(NJ):
        ref[(*lead, pl.ds(j, 8, stride=NJ), slice(None))] = val[8 * j:8 * j + 8]


def _expand_powers(p_ref, pexp):
    for c in range(2):
        for j in range(NJ):
            pexp[c, j] = jnp.broadcast_to(p_ref[0, 0, c, j:j + 1, :], (8, SW))


def _lockstep_scan(xre, xim, a_ref, pexp, car, reverse, extra=None):
    a = [a_ref[0, 0, k] for k in range(10)]
    mr, mi = a[0], a[1]
    order = (lambda i: NJ - 1 - i) if reverse else (lambda i: i)

    def local(i, hcar):
        hr, hi = hcar
        r0 = pl.multiple_of(order(i) * 8, 8)
        hr, hi = mr * hr - mi * hi + xre[pl.ds(r0, 8), :], mr * hi + mi * hr + xim[pl.ds(r0, 8), :]
        xre[pl.ds(r0, 8), :] = hr
        xim[pl.ds(r0, 8), :] = hi
        return hr, hi

    z8 = jnp.zeros((8, SW), F32)
    er, ei = lax.fori_loop(0, NJ, local, (z8, z8), unroll=4)
    c0r, c0i = car[0], car[1]
    er, ei = _tile_scan(er, ei, a[2:], c0r, c0i, reverse)
    rowid = lax.broadcasted_iota(jnp.int32, (8, SW), 0)
    first, sh, last = (7, 7, 0) if reverse else (0, 1, 7)
    cvr = jnp.where(rowid == first, c0r, pltpu.roll(er, sh, 0))
    cvi = jnp.where(rowid == first, c0i, pltpu.roll(ei, sh, 0))
    car[0] = jnp.broadcast_to(er[last:last + 1, :], (8, SW))
    car[1] = jnp.broadcast_to(ei[last:last + 1, :], (8, SW))

    def fix(i, carry):
        j = order(i)
        r0 = pl.multiple_of(j * 8, 8)
        pr, pi = pexp[0, j], pexp[1, j]
        sr = xre[pl.ds(r0, 8), :] + pr * cvr - pi * cvi
        si = xim[pl.ds(r0, 8), :] + pr * cvi + pi * cvr
        xre[pl.ds(r0, 8), :] = sr
        xim[pl.ds(r0, 8), :] = si
        if extra is None:
            return carry
        return (sr, si, extra(r0, sr, si, carry[0], carry[1], carry[2]))

    init = (cvr, cvi, extra(None, None, None, None, None, None)) if extra is not None else 0
    return lax.fori_loop(0, NJ, fix, init, unroll=4)


def _s5_time_block(z, s, t, adjoint):
    flip = (1 - z) if adjoint else z
    return s * (L // TT) + t + flip * (L // TT - 1 - 2 * t)


def _s5_fwd(h, bre, bim, cre, cim, tab):
    nt = L // TT
    taba, tabp = tab

    def body(u_ref, bre_ref, bim_ref, cre_ref, cim_ref, a_ref, p_ref, hre_ref, him_ref, y_ref, car, pexp):
        z = pl.program_id(1)
        s = pl.program_id(2)
        tc = pl.program_id(3)

        @pl.when(tc == 0)
        def _():
            car[...] = jnp.zeros_like(car)

        @pl.when((tc == 0) & (s == 0))
        def _():
            _expand_powers(p_ref, pexp)

        u = _to_lockstep(u_ref)
        hre_ref[0] = _mm(u, bre_ref[0, 0])
        him_ref[0] = _mm(u, bim_ref[0, 0])

        @pl.when(z == 0)
        def _():
            _lockstep_scan(hre_ref.at[0], him_ref.at[0], a_ref, pexp, car, False)

        @pl.when(z == 1)
        def _():
            _lockstep_scan(hre_ref.at[0], him_ref.at[0], a_ref, pexp, car, True)

        _from_lockstep(_mm(hre_ref[0], cre_ref[0, 0]) - _mm(him_ref[0], cim_ref[0, 0]), y_ref, 0)

    tb = lambda b, z, s, t: _s5_time_block(z, s, t, False)
    wspec = lambda r, c: pl.BlockSpec((1, 1, r, c), lambda b, z, s, t: (z, b, 0, 0))
    return pl.pallas_call(
        body, grid=(2, 2, NSEQ, nt),
        in_specs=[pl.BlockSpec((TT, 128), lambda b, z, s, t: (tb(b, z, s, t), b)),
                  wspec(128, SW), wspec(128, SW), wspec(SW, 128), wspec(SW, 128),
                  pl.BlockSpec((1, 1, 10, 8, SW), lambda b, z, s, t: (z, b, 0, 0, 0)),
                  pl.BlockSpec((1, 1, 2, NJ, SW), lambda b, z, s, t: (z, b, 0, 0, 0))],
        out_specs=[pl.BlockSpec((1, TT, SW), lambda b, z, s, t: (z, tb(b, z, s, t), b)),
                   pl.BlockSpec((1, TT, SW), lambda b, z, s, t: (z, tb(b, z, s, t), b)),
                   pl.BlockSpec((1, TT, 128), lambda b, z, s, t: (z, tb(b, z, s, t), b))],
        out_shape=[_sds((2, N, 2 * SW)), _sds((2, N, 2 * SW)), _sds((2, N, 256))],
        scratch_shapes=[pltpu.VMEM((2, 8, SW), F32), pltpu.VMEM((2, NJ, 8, SW), F32)],
        name="s5_fwd", compiler_params=_cp(("arbitrary",) * 4))(h, bre, bim, cre, cim, taba, tabp)


def _s5_bwd(h, dyp, hre, him, bre, bim, cre, cim, tabc):
    nt = L // TT
    taba, tabp = tabc

    def body(u_ref, dy_ref, hre_ref, him_ref, bre_ref, bim_ref, cre_ref, cim_ref, a_ref, p_ref,
             du_ref, dbre_ref, dbim_ref, dcre_ref, dcim_ref, dmu_ref, gre, gim, car, acc, macc, pexp):
        z = pl.program_id(1)
        s = pl.program_id(2)
        tc = pl.program_id(3)

        @pl.when(tc == 0)
        def _():
            car[...] = jnp.zeros_like(car)

        @pl.when((tc == 0) & (s == 0))
        def _():
            acc[...] = jnp.zeros_like(acc)
            macc[...] = jnp.zeros_like(macc)
            _expand_powers(p_ref, pexp)

        dy = _to_lockstep(dy_ref)
        gre[...] = _mm_nt(dy, cre_ref[0, 0])
        gim[...] = -_mm_nt(dy, cim_ref[0, 0])

        def run(reverse):
            def pair(r0, gr_, gi_, pvr, pvi, m):
                if r0 is None:
                    return (macc[0], macc[1])
                hr = hre_ref[0, pl.ds(r0, 8), :]
                hi = him_ref[0, pl.ds(r0, 8), :]
                return (m[0] + pvr * hr + pvi * hi, m[1] + pvi * hr - pvr * hi)

            _, _, (dmr, dmi) = _lockstep_scan(gre, gim, a_ref, pexp, car, reverse, pair)
            macc[0] = dmr
            macc[1] = dmi

        @pl.when(z == 0)
        def _():
            run(True)

        @pl.when(z == 1)
        def _():
            run(False)

        gr = gre[...]
        gi = gim[...]
        u = _to_lockstep(u_ref)
        _from_lockstep(_mm_nt(gr, bre_ref[0, 0]) + _mm_nt(gi, bim_ref[0, 0]), du_ref, 0)
        acc[0] += _mm_tn(u, gr)
        acc[1] += _mm_tn(u, gi)
        acc[2] += _mm_tn(dy, hre_ref[0])
        acc[3] -= _mm_tn(dy, him_ref[0])

        @pl.when((tc == nt - 1) & (s == NSEQ - 1))
        def _():
            grp = lax.broadcasted_iota(jnp.int32, (S5_H, SW), 1) // S5_P
            for k, out in enumerate((dbre_ref, dbim_ref, dcre_ref, dcim_ref)):
                c = jnp.zeros((S5_H, SW), F32)
                for i in range(8):
                    c = c + jnp.where(grp == i, acc[k, i * S5_H:(i + 1) * S5_H, :], 0.0)
                out[0, 0] = c
            dmu_ref[0, 0] = jnp.concatenate([jnp.sum(macc[0], axis=0, keepdims=True),
                                             jnp.sum(macc[1], axis=0, keepdims=True)], axis=0)

    tb = lambda b, z, s, t: _s5_time_block(z, s, t, True)
    wspec = lambda r, c: pl.BlockSpec((1, 1, r, c), lambda b, z, s, t: (z, b, 0, 0))
    tok = lambda w_: pl.BlockSpec((TT, w_), lambda b, z, s, t: (tb(b, z, s, t), b))
    st = pl.BlockSpec((1, TT, SW), lambda b, z, s, t: (z, tb(b, z, s, t), b))
    return pl.pallas_call(
        body, grid=(2, 2, NSEQ, nt),
        in_specs=[tok(128), tok(128), st, st, wspec(128, SW), wspec(128, SW), wspec(SW, 128), wspec(SW, 128),
                  pl.BlockSpec((1, 1, 10, 8, SW), lambda b, z, s, t: (z, b, 0, 0, 0)),
                  pl.BlockSpec((1, 1, 2, NJ, SW), lambda b, z, s, t: (z, b, 0, 0, 0))],
        out_specs=[pl.BlockSpec((1, TT, 128), lambda b, z, s, t: (z, tb(b, z, s, t), b)),
                   wspec(S5_H, SW), wspec(S5_H, SW), wspec(S5_H, SW), wspec(S5_H, SW),
                   wspec(2, SW)],
        out_shape=[_sds((2, N, 256))] + [_sds((2, 2, S5_H, SW))] * 4 + [_sds((2, 2, 2, SW))],
        scratch_shapes=[pltpu.VMEM((TT, SW), F32), pltpu.VMEM((TT, SW), F32), pltpu.VMEM((2, 8, SW), F32),
                        pltpu.VMEM((4, 128, SW), F32), pltpu.VMEM((2, 8, SW), F32), pltpu.VMEM((2, NJ, 8, SW), F32)],
        name="s5_bwd", compiler_params=_cp(("arbitrary",) * 4))(h, dyp, hre, him, bre, bim, cre, cim, taba, tabp)


_GELU_C = math.sqrt(2.0 / math.pi)


def _gelu(y):
    return 0.5 * y * (1.0 + jnp.tanh(_GELU_C * (y + 0.044715 * y * y * y)))


def _gelu_grad(y):
    t = jnp.tanh(_GELU_C * (y + 0.044715 * y * y * y))
    return 0.5 * (1.0 + t) + 0.5 * y * (1.0 - t * t) * _GELU_C * (1.0 + 3 * 0.044715 * y * y)


def _glu_halves(w4_ref):
    return (jnp.concatenate([w4_ref[0], w4_ref[1]], axis=1), jnp.concatenate([w4_ref[2], w4_ref[3]], axis=1))


def _s5_glu_fwd(y2, h, dsk, w4, bv, bg):
    tm = 512

    def body(y2_ref, u_ref, d_ref, w4_ref, bv_ref, bg_ref, ya_ref):
        wv, wg = _glu_halves(w4_ref)
        z = _gelu(y2_ref[0] + y2_ref[1] + d_ref[...] * u_ref[...])
        val = _mm(z, wv) + bv_ref[...]
        gate = _mm(z, wg) + bg_ref[...]
        ya_ref[...] = (val * jax.nn.sigmoid(gate)).astype(MX)

    full = lambda r, c: pl.BlockSpec((r, c), lambda i: (0, 0))
    return pl.pallas_call(
        body, grid=(N // tm,),
        in_specs=[pl.BlockSpec((2, tm, 256), lambda i: (0, i, 0)), pl.BlockSpec((tm, 256), lambda i: (i, 0)),
                  full(1, 256), pl.BlockSpec((NSHARD, 256, 128), lambda i: (0, 0, 0)), full(1, 256), full(1, 256)],
        out_specs=pl.BlockSpec((tm, 256), lambda i: (i, 0)),
        out_shape=_sds((N, 256), MX), name="s5_glu_fwd", compiler_params=_cp(("parallel",)))(y2, h, dsk, w4, bv, bg)


def _s5_glu_bwd(y2, h, dsk, w4, bv, bg, dya):
    tm = 512
    nt = N // tm

    def body(y2_ref, u_ref, d_ref, w4_ref, bv_ref, bg_ref, dya_ref,
             dyp_ref, dud_ref, dd_ref, dw4_ref, dbv_ref, dbg_ref, accv, accg):
        i = pl.program_id(0)

        @pl.when(i == 0)
        def _():
            for r in (dd_ref, accv, accg, dbv_ref, dbg_ref):
                r[...] = jnp.zeros_like(r)

        wv, wg = _glu_halves(w4_ref)
        u = u_ref[...]
        y = y2_ref[0] + y2_ref[1] + d_ref[...] * u
        z = _gelu(y)
        val = _mm(z, wv) + bv_ref[...]
        sig = jax.nn.sigmoid(_mm(z, wg) + bg_ref[...])
        dya = dya_ref[...]
        dval = dya * sig
        dgate = dya * val * sig * (1.0 - sig)
        dz = _mm_nt(dval, wv) + _mm_nt(dgate, wg)
        dy = dz * _gelu_grad(y)
        dyp_ref[...] = dy
        dud_ref[...] = (dy * d_ref[...]).astype(MX)
        dd_ref[...] += jnp.sum(dy * u, axis=0, keepdims=True)
        accv[...] += _mm_tn(z, dval)
        accg[...] += _mm_tn(z, dgate)
        dbv_ref[...] += jnp.sum(dval, axis=0, keepdims=True)
        dbg_ref[...] += jnp.sum(dgate, axis=0, keepdims=True)

        @pl.when(i == nt - 1)
        def _():
            dw4_ref[0] = accv[:, 0:128].astype(MX)
            dw4_ref[1] = accv[:, 128:256].astype(MX)
            dw4_ref[2] = accg[:, 0:128].astype(MX)
            dw4_ref[3] = accg[:, 128:256].astype(MX)

    full = lambda r, c: pl.BlockSpec((r, c), lambda i: (0, 0))
    row = pl.BlockSpec((tm, 256), lambda i: (i, 0))
    wspec = pl.BlockSpec((NSHARD, 256, 128), lambda i: (0, 0, 0))
    return pl.pallas_call(
        body, grid=(nt,),
        in_specs=[pl.BlockSpec((2, tm, 256), lambda i: (0, i, 0)), row, full(1, 256), wspec, full(1, 256), full(1, 256),
                  row],
        out_specs=[row, row, full(1, 256), wspec, full(1, 256), full(1, 256)],
        out_shape=[_sds((N, 256)), _sds((N, 256), MX), _sds((1, 256)), _sds((NSHARD, 256, 128), MX), _sds((1, 256)),
                   _sds((1, 256))],
        scratch_shapes=[pltpu.VMEM((256, 256), F32), pltpu.VMEM((256, 256), F32)],
        name="s5_glu_bwd", compiler_params=_cp(("arbitrary",)))(y2, h, dsk, w4, bv, bg, dya)


def _logsig(x):
    return jnp.minimum(x, 0.0) - jnp.log(1.0 + jnp.exp(-jnp.abs(x)))


def _gla_gate_bwd(h, wa, ba, dla_f, dla_b):
    tm = 512

    def body(hl_ref, wa_ref, ba_ref, df_ref, db_ref, dhl_ref, dwa_ref, dba_ref):
        i = pl.program_id(0)

        @pl.when(i == 0)
        def _():
            dwa_ref[...] = jnp.zeros_like(dwa_ref)
            dba_ref[...] = jnp.zeros_like(dba_ref)

        hl = hl_ref[...]
        pre = _mm(hl, wa_ref[...]) + ba_ref[...]
        dpre = jnp.concatenate([df_ref[...], db_ref[...]], axis=1) * (1.0 / 16.0) * jax.nn.sigmoid(-pre)
        dhl_ref[...] = _mm_nt(dpre, wa_ref[...]).astype(MX)
        dwa_ref[...] += _mm_tn(hl, dpre)[0:32]
        dba_ref[...] += jnp.sum(dpre, axis=0, keepdims=True)

    row = pl.BlockSpec((tm, 128), lambda i: (i, 0))
    return pl.pallas_call(
        body, grid=(N // tm,),
        in_specs=[pl.BlockSpec((tm, 128), lambda i: (i, 14)), pl.BlockSpec((128, 256), lambda i: (0, 0)),
                  pl.BlockSpec((1, 256), lambda i: (0, 0)), row, row],
        out_specs=[row, pl.BlockSpec((32, 256), lambda i: (0, 0)), pl.BlockSpec((1, 256), lambda i: (0, 0))],
        out_shape=[_sds((N, 128), MX), _sds((32, 256)), _sds((1, 256))],
        name="gla_gate_bwd", compiler_params=_cp(("arbitrary",)))(h, wa, ba, dla_f, dla_b)


def _gla_chunk(q, k, v, la, st, rev):
    c = GLA_CHUNK
    rows = q.shape[0]
    nch = rows // c
    b = _cums(la, rev)
    blc = [jnp.sum(la[i * c:(i + 1) * c], axis=0, keepdims=True) for i in range(nch)]
    bl = jnp.concatenate([jnp.broadcast_to(t, (c, 128)) for t in blc], axis=0)
    q_in = q * (32.0 ** -0.5) * jnp.exp(b)
    k_in = k * jnp.exp(-b)
    k_st = k * jnp.exp(bl - b)
    lane_k = lax.broadcasted_iota(jnp.int32, (1, 128), 1) // 32
    lane_v = lax.broadcasted_iota(jnp.int32, (1, 256), 1) // 64
    qs = jnp.concatenate([jnp.where(lane_k == hd, q_in, 0.0) for hd in range(4)], axis=0)
    a = _dmm_nt(qs, k_in)
    a = jnp.where(jnp.concatenate([_chunk_pairs(rows, rev, rev)] * 4, axis=0), a, 0.0)
    o4 = _dmm(a, v)
    o = jnp.zeros((rows, 256), F32)
    for hd in range(4):
        o = o + jnp.where(lane_v == hd, o4[hd * rows:(hd + 1) * rows], 0.0)
    bd = (lax.broadcasted_iota(jnp.int32, (256, 128), 0) // 64) == (lax.broadcasted_iota(jnp.int32, (256, 128), 1) // 32)
    inter = [None] * nch
    for i in (reversed(range(nch)) if rev else range(nch)):
        sl = slice(i * c, (i + 1) * c)
        inter[i] = _dmm_nt(q_in[sl], st)
        st = jnp.exp(blc[i]) * st + jnp.where(bd, _dmm_tn(v[sl], k_st[sl]), 0.0)
    return o + jnp.concatenate(inter, axis=0), st


def _gla_chunk_of(c, rev):
    return NGROUP - 1 - c if rev else c


def _gla_fwd(h, la2):
    c = GLA_GROUP * GLA_CHUNK

    def body(qf, kf, vf, laf, qb, kb, vb, lab, of_ref, ob_ref, sf_ref, sb_ref, stf, stb):
        @pl.when(pl.program_id(0) == 0)
        def _():
            stf[...] = jnp.zeros_like(stf)
            stb[...] = jnp.zeros_like(stb)

        ins = [(qf[s], kf[s], vf[s], laf[s], stf[s], qb[s], kb[s], vb[s], lab[s], stb[s]) for s in range(NSEQ)]
        outs = [(_gla_chunk(*t[:5], False), _gla_chunk(*t[5:], True)) for t in ins]
        for s in range(NSEQ):
            sf_ref[s, 0] = ins[s][4]
            sb_ref[s, 0] = ins[s][9]
            (of_ref[s], stf[s]), (ob_ref[s], stb[s]) = outs[s]

    def specs(rev):
        ch = lambda i: _gla_chunk_of(i, rev)
        return [pl.BlockSpec((NSEQ, c, 128), lambda i: (0, ch(i), 2)), pl.BlockSpec((NSEQ, c, 128), lambda i: (0, ch(i), 3)),
                pl.BlockSpec((NSEQ, c, 256), lambda i: (0, ch(i), 2)),
                pl.BlockSpec((NSEQ, c, 128), lambda i: (0, ch(i), 1 if rev else 0))]

    orow = lambda rev: pl.BlockSpec((NSEQ, c, 256), lambda i: (0, _gla_chunk_of(i, rev), 0))
    srow = lambda rev: pl.BlockSpec((NSEQ, 1, 256, 128), lambda i: (0, _gla_chunk_of(i, rev), 0, 0))
    h3, la3 = h.reshape(NSEQ, L, DINP), la2.reshape(NSEQ, L, 256)
    of, ob, sf, sb = pl.pallas_call(
        body, grid=(NGROUP,),
        in_specs=specs(False) + specs(True),
        out_specs=[orow(False), orow(True), srow(False), srow(True)],
        out_shape=[_sds((NSEQ, L, 256)), _sds((NSEQ, L, 256)), _sds((NSEQ, NGROUP, 256, 128)),
                   _sds((NSEQ, NGROUP, 256, 128))],
        scratch_shapes=[pltpu.VMEM((NSEQ, 256, 128), F32), pltpu.VMEM((NSEQ, 256, 128), F32)],
        name="gla_fwd", compiler_params=_cp(("arbitrary",)))(h3, h3, h3, la3, h3, h3, h3, la3)
    return of.reshape(N, 256), ob.reshape(N, 256), sf, sb


def _gla_bwd(h, la2, do, sf, sb):
    c = GLA_GROUP * GLA_CHUNK

    def body(qf, kf, vf, laf, dof, sfr, qb, kb, vb, lab, dob, sbr,
             dqf, dkf, dvf, dlf, dqb, dkb, dvb, dlb, dstf, dstb):
        @pl.when(pl.program_id(0) == 0)
        def _():
            dstf[...] = jnp.zeros_like(dstf)
            dstb[...] = jnp.zeros_like(dstb)

        def one(s, q, k, v, la, do_, st, dst, rev):
            _, vjp = jax.vjp(functools.partial(_gla_chunk, rev=rev), q[s], k[s], v[s], la[s], st[s, 0])
            return vjp((do_[s], dst[s]))

        res = [(one(s, qf, kf, vf, laf, dof, sfr, dstf, False), one(s, qb, kb, vb, lab, dob, sbr, dstb, True))
               for s in range(NSEQ)]
        for s in range(NSEQ):
            for (gq, gk, gv, gl, gs), (dq, dk, dv, dl, dst) in ((res[s][0], (dqf, dkf, dvf, dlf, dstf)),
                                                                  (res[s][1], (dqb, dkb, dvb, dlb, dstb))):
                dq[s], dk[s], dv[s] = gq.astype(MX), gk.astype(MX), gv.astype(MX)
                dl[s], dst[s] = gl, gs

    def specs(rev):
        ch = lambda i: _gla_chunk_of(i, not rev)
        return [pl.BlockSpec((NSEQ, c, 128), lambda i: (0, ch(i), 2)), pl.BlockSpec((NSEQ, c, 128), lambda i: (0, ch(i), 3)),
                pl.BlockSpec((NSEQ, c, 256), lambda i: (0, ch(i), 2)),
                pl.BlockSpec((NSEQ, c, 128), lambda i: (0, ch(i), 1 if rev else 0)),
                pl.BlockSpec((NSEQ, c, 256), lambda i: (0, ch(i), 0)),
                pl.BlockSpec((NSEQ, 1, 256, 128), lambda i: (0, ch(i), 0, 0))]

    def ospecs(rev):
        ch = lambda i: _gla_chunk_of(i, not rev)
        n = pl.BlockSpec((NSEQ, c, 128), lambda i: (0, ch(i), 0))
        return [n, n, pl.BlockSpec((NSEQ, c, 256), lambda i: (0, ch(i), 0)), n]

    oshape = [_sds((NSEQ, L, 128), MX), _sds((NSEQ, L, 128), MX), _sds((NSEQ, L, 256), MX), _sds((NSEQ, L, 128))]
    h3, la3, do3 = h.reshape(NSEQ, L, DINP), la2.reshape(NSEQ, L, 256), do.reshape(NSEQ, L, 256)
    res = pl.pallas_call(
        body, grid=(NGROUP,),
        in_specs=specs(False) + specs(True),
        out_specs=ospecs(False) + ospecs(True),
        out_shape=oshape + oshape,
        scratch_shapes=[pltpu.VMEM((NSEQ, 256, 128), F32), pltpu.VMEM((NSEQ, 256, 128), F32)],
        name="gla_bwd", compiler_params=_cp(("arbitrary",)))(h3, h3, h3, la3, do3, sf, h3, h3, h3, la3, do3, sb)
    return [r.reshape(N, r.shape[-1]) for r in res]


def _gla_post(of, ob, r, g):
    o = of + ob
    head = lax.broadcasted_iota(jnp.int32, (1, 256), 1) // 64
    mu = jnp.zeros_like(o)
    for hd in range(4):
        mu = mu + jnp.where(head == hd, jnp.sum(jnp.where(head == hd, o, 0.0), axis=-1, keepdims=True) * (1.0 / 64.0), 0.0)
    xc = o - mu
    var = jnp.zeros_like(o)
    for hd in range(4):
        var = var + jnp.where(head == hd, jnp.sum(jnp.where(head == hd, xc * xc, 0.0), axis=-1, keepdims=True) * (1.0 / 64.0), 0.0)
    return xc * lax.rsqrt(var + LN_EPS) * g * (r * jax.nn.sigmoid(r))


def _gla_post_fwd(of, ob, h, g):
    tm = 512

    def body(of_ref, ob_ref, r_ref, g_ref, y_ref):
        y_ref[...] = _gla_post(of_ref[...], ob_ref[...], r_ref[...], g_ref[...]).astype(MX)

    row = pl.BlockSpec((tm, 256), lambda i: (i, 0))
    return pl.pallas_call(
        body, grid=(N // tm,),
        in_specs=[row, row, pl.BlockSpec((tm, 256), lambda i: (i, 3)), pl.BlockSpec((1, 256), lambda i: (0, 0))],
        out_specs=row, out_shape=_sds((N, 256), MX), name="gla_post_fwd", compiler_params=_cp(("parallel",)))(of, ob, h, g)


def _gla_post_bwd(of, ob, h, g, dyb):
    tm = 512

    def body(of_ref, ob_ref, r_ref, g_ref, dy_ref, do_ref, dr_ref, dg_ref):
        @pl.when(pl.program_id(0) == 0)
        def _():
            dg_ref[...] = jnp.zeros_like(dg_ref)

        _, vjp = jax.vjp(_gla_post, of_ref[...], ob_ref[...], r_ref[...], g_ref[...])
        go, _, gr, gg = vjp(dy_ref[...])
        do_ref[...] = go
        dr_ref[...] = gr.astype(MX)
        dg_ref[...] += gg

    row = pl.BlockSpec((tm, 256), lambda i: (i, 0))
    one = pl.BlockSpec((1, 256), lambda i: (0, 0))
    return pl.pallas_call(
        body, grid=(N // tm,),
        in_specs=[row, row, pl.BlockSpec((tm, 256), lambda i: (i, 3)), one, row],
        out_specs=[row, row, one], out_shape=[_sds((N, 256)), _sds((N, 256), MX), _sds((1, 256))],
        name="gla_post_bwd", compiler_params=_cp(("arbitrary",)))(of, ob, h, g, dyb)


def _rope_tables(width):
    pos = jnp.arange(L, dtype=F32)
    inv_freq = ROPE_THETA ** (-jnp.arange(0, ROT, 2, dtype=F32) / ROT)
    ang = pos[:, None] * inv_freq[None, :]
    cos, sin = jnp.cos(ang), jnp.sin(ang)
    one = jnp.ones((L, 64 - ROT), F32)
    zero = jnp.zeros((L, 64 - ROT), F32)
    z8 = jnp.zeros((L, ROT // 2), F32)
    c = jnp.concatenate([cos, cos, one], axis=1)
    sa = jnp.concatenate([z8, sin, zero], axis=1)
    sb = jnp.concatenate([-sin, z8, zero], axis=1)
    rep = width // 64
    return jnp.stack([jnp.tile(c, (1, rep)), jnp.tile(sa, (1, rep)), jnp.tile(sb, (1, rep))])


def _pieces(t, f):
    out = [f(t[:, c * 128:(c + 1) * 128]) for c in range(t.shape[-1] // 128)]
    return out[0] if len(out) == 1 else jnp.concatenate(out, axis=1)


def _rope(t, tab):
    return _pieces(t, lambda x: x * tab[0] + pltpu.roll(x, ROT // 2, 1) * tab[1] + pltpu.roll(x, 128 - ROT // 2, 1) * tab[2])


def _rope_t(g, tab):
    return _pieces(g, lambda x: x * tab[0] + pltpu.roll(x * tab[1], 128 - ROT // 2, 1) + pltpu.roll(x * tab[2], ROT // 2, 1))


def _swa_pad_kv(kv_ref, tk_ref, kexp, vexp):
    z = jnp.zeros((SWA_BLK, 256), F32)
    kr = _rope(kv_ref[:, 0:128], tk_ref[...])
    for hk in range(2):
        for pad in (kexp, vexp):
            pad[hk, 0:SWA_BLK] = z
            pad[hk, SWA_BLK + L:] = z
        kexp[hk, SWA_BLK:SWA_BLK + L] = _swa_expand(kr, hk)
        vexp[hk, SWA_BLK:SWA_BLK + L] = _swa_expand(kv_ref[:, 128:256], hk)


def _swa_expand(x, hk):
    lane = lax.broadcasted_iota(jnp.int32, x.shape, 1)
    sw = pltpu.roll(x, 64, 1)
    pair = jnp.where(lane < 64, x, sw) if hk == 0 else jnp.where(lane < 64, sw, x)
    return jnp.concatenate([pair, pair], axis=1)


def _swa_fold(x, hk):
    a = x[:, 0:128] + x[:, 128:256]
    t = a + pltpu.roll(a, 64, 1)
    lane = lax.broadcasted_iota(jnp.int32, a.shape, 1)
    return jnp.where((lane < 64) if hk == 0 else (lane >= 64), t, 0.0)


def _swa_bias_tables(bias):
    i = lax.broadcasted_iota(jnp.int32, (SWA_BLK, 3 * SWA_BLK), 0)
    j = lax.broadcasted_iota(jnp.int32, (SWA_BLK, 3 * SWA_BLK), 1)
    band = (j - i >= 0) & (j - i <= 2 * SWA_BLK)
    for v, inside in enumerate((j >= SWA_BLK, True, j < 2 * SWA_BLK)):
        bias[v] = jnp.where(band & inside, 0.0, NEG_BIG)


def _swa_bias(bias, blk):
    return bias[jnp.where(blk == 0, 0, jnp.where(blk == NBLK - 1, 2, 1))]


def _swa_probs(q2, kexp, bias, sink_ref, hk):
    slot = lax.broadcasted_iota(jnp.int32, (1, 256), 1) // 64
    qs = jnp.concatenate([jnp.where(slot == g, q2, 0.0) for g in range(4)], axis=0)
    s = _mm_nt(qs, kexp) + jnp.concatenate([bias] * 4, axis=0)
    rowg = lax.broadcasted_iota(jnp.int32, (4 * SWA_BLK, 1), 0) // SWA_BLK
    sink = jnp.zeros((4 * SWA_BLK, 1), F32)
    for g in range(4):
        sink = jnp.where(rowg == g, sink_ref[hk * 4 + g], sink)
    m = jnp.maximum(jnp.max(s, axis=-1, keepdims=True), sink)
    p = jnp.exp(s - m)
    ps = jnp.exp(sink - m)
    inv = 1.0 / (jnp.sum(p, axis=-1, keepdims=True) + ps)
    return qs, p * inv, ps * inv, slot, rowg


def _swa_qtab(tk_ref, r0):
    return [tk_ref[i, pl.ds(r0, SWA_BLK), :] for i in range(3)]


def _swa_fwd(h, tk, sink):
    def body(sink_ref, q_ref, kv_ref, tk_ref, y_ref, kexp, vexp, bias):
        n = pl.program_id(1)

        @pl.when(n == 0)
        def _():
            _swa_pad_kv(kv_ref, tk_ref, kexp, vexp)
            _swa_bias_tables(bias)

        for t in range(SWA_PER):
            blk = n * SWA_PER + t
            rows = slice(t * SWA_BLK, (t + 1) * SWA_BLK)
            r0 = pl.multiple_of(blk * SWA_BLK, SWA_BLK)
            q = _rope(q_ref[rows, :], _swa_qtab(tk_ref, r0)) * 0.125
            for hk in range(2):
                _, p, _, slot, _ = _swa_probs(q[:, hk * 256:(hk + 1) * 256], kexp[hk, pl.ds(r0, 3 * SWA_BLK), :],
                                              _swa_bias(bias, blk), sink_ref, hk)
                o4 = _mm(p, vexp[hk, pl.ds(r0, 3 * SWA_BLK), :])
                o = jnp.zeros((SWA_BLK, 256), F32)
                for g in range(4):
                    o = o + jnp.where(slot == g, o4[g * SWA_BLK:(g + 1) * SWA_BLK], 0.0)
                y_ref[rows, hk * 256:(hk + 1) * 256] = o.astype(MX)

    tm = SWA_PER * SWA_BLK
    return pl.pallas_call(
        body,
        grid_spec=pltpu.PrefetchScalarGridSpec(
            num_scalar_prefetch=1, grid=(NSEQ, L // tm),
            in_specs=[pl.BlockSpec((tm, 512), lambda s, n, sk: (s * (L // tm) + n, 2)),
                      pl.BlockSpec((L, 256), lambda s, n, sk: (s, 6)),
                      pl.BlockSpec((3, L, 128), lambda s, n, sk: (0, 0, 0))],
            out_specs=pl.BlockSpec((tm, 512), lambda s, n, sk: (s * (L // tm) + n, 0)),
            scratch_shapes=[pltpu.VMEM((2, L + 2 * SWA_BLK, 256), F32), pltpu.VMEM((2, L + 2 * SWA_BLK, 256), F32),
                            pltpu.VMEM((3, SWA_BLK, 3 * SWA_BLK), F32)]),
        out_shape=_sds((N, 512), MX), name="swa_fwd", compiler_params=_cp(("arbitrary", "arbitrary")))(sink, h, h, tk)


def _swa_bwd(h, tk, sink, dyc):
    tm = SWA_PER * SWA_BLK

    def body(sink_ref, q_ref, kv_ref, tk_ref, dy_ref, dq_ref, dkv_ref, dsink_ref, kexp_all, vexp_all, dkacc, dvacc, bias):
        sq = pl.program_id(0)
        n = pl.program_id(1)

        @pl.when(n == 0)
        def _():
            _swa_pad_kv(kv_ref, tk_ref, kexp_all, vexp_all)
            _swa_bias_tables(bias)
            dkacc[...] = jnp.zeros_like(dkacc)
            dvacc[...] = jnp.zeros_like(dvacc)

        @pl.when((n == 0) & (sq == 0))
        def _():
            dsink_ref[...] = jnp.zeros_like(dsink_ref)

        hrow = lax.broadcasted_iota(jnp.int32, (8, 128), 0)
        dsk = jnp.zeros((8, 128), F32)
        for t in range(SWA_PER):
            blk = n * SWA_PER + t
            rows = slice(t * SWA_BLK, (t + 1) * SWA_BLK)
            r0 = pl.multiple_of(blk * SWA_BLK, SWA_BLK)
            tq = _swa_qtab(tk_ref, r0)
            q = _rope(q_ref[rows, :], tq) * 0.125
            band = _swa_bias(bias, blk)
            for hk in range(2):
                kexp = kexp_all[hk, pl.ds(r0, 3 * SWA_BLK), :]
                vexp = vexp_all[hk, pl.ds(r0, 3 * SWA_BLK), :]
                qs, p, ps, slot, rowg = _swa_probs(q[:, hk * 256:(hk + 1) * 256], kexp, band, sink_ref, hk)
                dy2 = dy_ref[rows, hk * 256:(hk + 1) * 256]
                dos = jnp.concatenate([jnp.where(slot == g, dy2, 0.0) for g in range(4)], axis=0)
                dp = _mm_nt(dos, vexp)
                delta = jnp.sum(p * dp, axis=-1, keepdims=True)
                ds = p * (dp - delta)
                dsr = -ps * delta
                for g in range(4):
                    dsk = dsk + jnp.where(hrow == hk * 4 + g,
                                          jnp.sum(jnp.where(rowg == g, dsr, 0.0), axis=0, keepdims=True), 0.0)
                dq4 = _mm(ds, kexp)
                dq2 = jnp.zeros((SWA_BLK, 256), F32)
                for g in range(4):
                    dq2 = dq2 + jnp.where(slot == g, dq4[g * SWA_BLK:(g + 1) * SWA_BLK], 0.0)
                dq_ref[rows, hk * 256:(hk + 1) * 256] = _rope_t(dq2 * 0.125, tq).astype(MX)
                dkacc[hk, pl.ds(r0, 3 * SWA_BLK), :] += _mm_tn(ds, qs)
                dvacc[hk, pl.ds(r0, 3 * SWA_BLK), :] += _mm_tn(p, dos)
        dsink_ref[...] += dsk

        @pl.when(n == L // tm - 1)
        def _():
            seq = slice(SWA_BLK, SWA_BLK + L)
            dk = _rope_t(_swa_fold(dkacc[0, seq], 0) + _swa_fold(dkacc[1, seq], 1), tk_ref[...])
            dkv_ref[:, 0:128] = dk.astype(MX)
            dkv_ref[:, 128:256] = (_swa_fold(dvacc[0, seq], 0) + _swa_fold(dvacc[1, seq], 1)).astype(MX)

    blk = lambda col: pl.BlockSpec((tm, 512), lambda s, n, sk: (s * (L // tm) + n, col))
    pad = pltpu.VMEM((2, L + 2 * SWA_BLK, 256), F32)
    return pl.pallas_call(
        body,
        grid_spec=pltpu.PrefetchScalarGridSpec(
            num_scalar_prefetch=1, grid=(NSEQ, L // tm),
            in_specs=[blk(2), pl.BlockSpec((L, 256), lambda s, n, sk: (s, 6)),
                      pl.BlockSpec((3, L, 128), lambda s, n, sk: (0, 0, 0)), blk(0)],
            out_specs=[blk(0), pl.BlockSpec((L, 256), lambda s, n, sk: (s, 0)),
                       pl.BlockSpec((8, 128), lambda s, n, sk: (0, 0))],
            scratch_shapes=[pad, pad, pad, pad, pltpu.VMEM((3, SWA_BLK, 3 * SWA_BLK), F32)]),
        out_shape=[_sds((N, 512), MX), _sds((N, 256), MX), _sds((8, 128))],
        name="swa_bwd", compiler_params=_cp(("arbitrary", "arbitrary")))(sink, h, h, tk, dyc)


def _outproj_bwd(dx1, s1, ya, yb, yc, wo, g):
    tm = 512
    nt = N // tm

    def body(dx1_ref, s_ref, ya_ref, yb_ref, yc_ref, wo_ref, g_ref,
             dya_ref, dyb_ref, dyc_ref, dxp_ref, dwo_ref, dg_ref, db_ref, acc):
        i = pl.program_id(0)

        @pl.when(i == 0)
        def _():
            acc[...] = jnp.zeros_like(acc)
            dg_ref[...] = jnp.zeros_like(dg_ref)
            db_ref[...] = jnp.zeros_like(db_ref)

        ds, dg, db = _ln_bwd(dx1_ref[...], s_ref[...], g_ref[...])
        dg_ref[...] += dg
        db_ref[...] += db
        dxp_ref[...] = ALPHA * ds
        dy = _mm_nt(ds, wo_ref[...])
        dya_ref[...] = dy[:, 0:256]
        dyb_ref[...] = dy[:, 256:512]
        dyc_ref[...] = dy[:, 512:1024]
        acc[0:256] += _mm_tn(ya_ref[...], ds)
        acc[256:512] += _mm_tn(yb_ref[...], ds)
        acc[512:1024] += _mm_tn(yc_ref[...], ds)

        @pl.when(i == nt - 1)
        def _():
            dwo_ref[...] = acc[...].astype(MX)

    row = lambda w_: pl.BlockSpec((tm, w_), lambda i: (i, 0))
    one = pl.BlockSpec((1, D), lambda i: (0, 0))
    full = pl.BlockSpec((D, D), lambda i: (0, 0))
    return pl.pallas_call(
        body, grid=(nt,),
        in_specs=[row(D), row(D), row(256), row(256), row(512), full, one],
        out_specs=[row(256), row(256), row(512), row(D), full, one, one],
        out_shape=[_sds((N, 256)), _sds((N, 256)), _sds((N, 512)), _sds((N, D)), _sds((D, D), MX), _sds((1, D)), _sds((1, D))],
        scratch_shapes=[pltpu.VMEM((D, D), F32)],
        name="outproj_bwd", compiler_params=_cp(("arbitrary",)))(dx1, s1, ya, yb, yc, wo, g)


def _mix_ffn_fwd(ya, yb, yc, x, wo, g1, b1, w1, w2, g, b, target=None):
    tm = FFN_TM
    head = target is not None

    def body(*refs):
        ya_ref, yb_ref, yc_ref, xin_ref, wo_ref, g1_ref, b1_ref, w1_ref, w2_ref, g_ref, b_ref = refs[:11]
        s1_ref, x1_ref, a_ref, s_ref, y_ref = refs[11 + head:16 + head]
        mix = _mm(ya_ref[...], wo_ref[0:256]) + _mm(yb_ref[...], wo_ref[256:512]) + _mm(yc_ref[...], wo_ref[512:1024])
        s1 = ALPHA * xin_ref[...] + mix
        s1_ref[...] = s1
        x = _ln_fwd(s1, g1_ref[...], b1_ref[...])
        x1_ref[...] = x
        xb = x.astype(MX)
        s = ALPHA * x
        for j in range(NSHARD):
            a = _mm(xb, w1_ref[j])
            a_ref[:, j * D:(j + 1) * D] = a.astype(MX)
            s = s + _mm(jnp.square(jnp.maximum(a, 0.0)), w2_ref[j])
        s_ref[...] = s
        x2 = _ln_fwd(s, g_ref[...], b_ref[...])
        if not head:
            y_ref[...] = x2
            return
        l_ref = refs[16 + head]

        @pl.when(pl.program_id(0) == 0)
        def _():
            l_ref[...] = jnp.zeros_like(l_ref)

        e = x2 - refs[11][...]
        y_ref[...] = e * (1.0 / D)
        l_ref[...] += jnp.sum(jnp.sum(e * e, axis=1, keepdims=True), axis=0, keepdims=True) * (0.5 / D)

    rw = lambda w_: pl.BlockSpec((tm, w_), lambda i: (i, 0))
    row = rw(D)
    once = dict(pipeline_mode=pl.Buffered(1))
    wall = pl.BlockSpec((NSHARD, D, D), lambda i: (0, 0, 0), **once)
    one = pl.BlockSpec((1, D), lambda i: (0, 0))
    acc = pl.BlockSpec((8, 128), lambda i: (0, 0))
    return pl.pallas_call(
        body, grid=(N // tm,),
        in_specs=[rw(256), rw(256), rw(512), row, pl.BlockSpec((D, D), lambda i: (0, 0), **once), one, one,
                  wall, wall, one, one] + [row] * head,
        out_specs=[row, row, pl.BlockSpec((tm, DFF), lambda i: (i, 0)), row, row] + [acc] * head,
        out_shape=[_sds((N, D)), _sds((N, D)), _sds((N, DFF), MX), _sds((N, D)), _sds((N, D))] + [_sds((8, 128))] * head,
        name="mix_ffn_fwd", compiler_params=_cp(("arbitrary",), FFN_VMEM))(
            ya, yb, yc, x, wo, g1, b1, w1, w2, g, b, *([target] * head))


def _ffn_bwd_act(dy, s2, a, w1, w2, g):
    tm = FFN_TM

    def body(dy_ref, s_ref, a_ref, w1_ref, w2_ref, g_ref, da_ref, ds_ref, dx1_ref, dg_ref, db_ref):
        @pl.when(pl.program_id(0) == 0)
        def _():
            dg_ref[...] = jnp.zeros_like(dg_ref)
            db_ref[...] = jnp.zeros_like(db_ref)

        ds, dg, db = _ln_bwd(dy_ref[...], s_ref[...], g_ref[...])
        dsb = ds.astype(MX)
        ds_ref[...] = dsb
        dg_ref[...] += dg
        db_ref[...] += db
        dx1 = ALPHA * ds
        for j in range(NSHARD):
            da = (_mm_nt(dsb, w2_ref[j]) * 2.0 * jnp.maximum(a_ref[:, j * D:(j + 1) * D].astype(F32), 0.0)).astype(MX)
            da_ref[:, j * D:(j + 1) * D] = da
            dx1 = dx1 + _mm_nt(da, w1_ref[j])
        dx1_ref[...] = dx1

    row = pl.BlockSpec((tm, D), lambda i: (i, 0))
    wide = pl.BlockSpec((tm, DFF), lambda i: (i, 0))
    wall = pl.BlockSpec((NSHARD, D, D), lambda i: (0, 0, 0))
    one = pl.BlockSpec((1, D), lambda i: (0, 0))
    return pl.pallas_call(
        body, grid=(N // tm,),
        in_specs=[row, row, wide, wall, wall, one],
        out_specs=[wide, row, row, one, one],
        out_shape=[_sds((N, DFF), MX), _sds((N, D), MX), _sds((N, D)), _sds((1, D)), _sds((1, D))],
        name="ffn_bwd_act", compiler_params=_cp(("arbitrary",), FFN_VMEM))(dy, s2, a, w1, w2, g)


def _ffn_bwd_w(x1, da, a, ds):
    tm, nb = FFN_TM_W, FFN_WB
    nt = N // tm

    def body(x_ref, da_ref, a_ref, ds_ref, dw1_ref, dw2_ref, acc1, acc2):
        i = pl.program_id(1)

        @pl.when(i == 0)
        def _():
            acc1[...] = jnp.zeros_like(acc1)
            acc2[...] = jnp.zeros_like(acc2)

        x, ds_ = x_ref[...], ds_ref[...]
        for k in range(nb):
            cols = slice(k * D, (k + 1) * D)
            acc1[k] += _mm_tn(x, da_ref[:, cols])
            acc2[k] += _mm_tn(jnp.square(jnp.maximum(a_ref[:, cols].astype(F32), 0.0)), ds_)

        @pl.when(i == nt - 1)
        def _():
            dw1_ref[...] = acc1[...].astype(MX)
            dw2_ref[...] = acc2[...].astype(MX)

    row = pl.BlockSpec((tm, D), lambda j, i: (i, 0))
    col = pl.BlockSpec((tm, nb * D), lambda j, i: (i, j))
    wj = pl.BlockSpec((nb, D, D), lambda j, i: (j, 0, 0))
    return pl.pallas_call(
        body, grid=(NSHARD // nb, nt),
        in_specs=[row, col, col, row], out_specs=[wj, wj],
        out_shape=[_sds((NSHARD, D, D), MX), _sds((NSHARD, D, D), MX)],
        scratch_shapes=[pltpu.VMEM((nb, D, D), F32), pltpu.VMEM((nb, D, D), F32)],
        name="ffn_bwd_w", compiler_params=_cp(("parallel", "arbitrary"), FFN_VMEM))(x1, da, a, ds)


def _s5_discretize(a_re, a_im, log_step, b_re, b_im):
    lam = lax.complex(a_re, a_im)
    lam_bar = jnp.exp(lam * jnp.exp(log_step))
    b_bar = ((lam_bar - 1.0) / lam)[..., None] * lax.complex(b_re, b_im)
    return jnp.real(lam_bar), jnp.imag(lam_bar), jnp.real(b_bar), jnp.imag(b_bar)


def _s5_in_blocks(b):
    e = jnp.eye(8, dtype=F32)
    return jnp.einsum('ij,zbjph->zbihjp', e, b.reshape(2, 2, 8, S5_P, S5_H)).reshape(2, 2, 128, SW)


def _s5_out_blocks(c):
    e = jnp.eye(8, dtype=F32)
    return jnp.einsum('ij,zbjhp->zbjpih', e, c.reshape(2, 2, 8, S5_H, S5_P)).reshape(2, 2, SW, 128)


def _gate_weight(w_a):
    z = jnp.zeros((16, 128), F32)
    top = jnp.concatenate([w_a[0], z], axis=1)
    bot = jnp.concatenate([z, w_a[1]], axis=1)
    return jnp.concatenate([top, bot, jnp.zeros((96, 256), F32)], axis=0)


def _layer_prep(p):
    lr, li, br, bi = _s5_discretize(p["s5_a_re"], p["s5_a_im"], p["s5_log_step"], p["s5_b_re"], p["s5_b_im"])
    q = dict(p)
    q["bre"] = _s5_in_blocks(br).astype(MX)
    q["bim"] = _s5_in_blocks(bi).astype(MX)
    q["cre"] = _s5_out_blocks(p["s5_c_re"]).astype(MX)
    q["cim"] = _s5_out_blocks(p["s5_c_im"]).astype(MX)
    mr, mi = lr.reshape(2, 1024), li.reshape(2, 1024)
    q["tab"], q["tabc"] = _lockstep_tables(mr, mi)
    q["dsk"] = p["s5_d"].reshape(1, 256)
    q["wa"] = _gate_weight(p["gla_w_a"]).astype(MX)
    q["ba"] = p["gla_b_a"].reshape(1, 256)
    q["lng"] = p["gla_ln_g"].reshape(1, 256)
    q["bv"] = p["s5_b_glu"][:256].reshape(1, 256)
    q["bg"] = p["s5_b_glu"][256:].reshape(1, 256)
    for k in ("ln1_g", "ln1_b", "ln2_g", "ln2_b"):
        q[k] = p[k].reshape(1, D)
    return q


def _layer_fwd(x, q, tk, fetch, target=None):
    q["w_in"] = fetch("w_in", x)
    h, la2 = _inproj_fwd(x, q["w_in"], q["wa"], q["ba"])
    hre, him, y2 = _s5_fwd(h, q["bre"], q["bim"], q["cre"], q["cim"], q["tab"])
    q["w4"] = fetch("s5_w_glu", y2)
    ya = _s5_glu_fwd(y2, h, q["dsk"], q["w4"], q["bv"], q["bg"])
    of, ob, sf, sb = _gla_fwd(h, la2)
    yb = _gla_post_fwd(of, ob, h, q["lng"])
    yc = _swa_fwd(h, tk, q["swa_sink"])
    mixed = ya[:8, :128] + yb[:8, :128] + yc[:8, :128]
    q["w_out"] = fetch("w_out", mixed)
    q["w_ff1"] = fetch("w_ff1", mixed)
    q["w_ff2"] = fetch("w_ff2", mixed)
    s1, x1, a, s2, *out = _mix_ffn_fwd(ya, yb, yc, x, q["w_out"], q["ln1_g"], q["ln1_b"], q["w_ff1"], q["w_ff2"],
                                       q["ln2_g"], q["ln2_b"], target)
    saved = dict(x=x, h=h, hre=hre, him=him, y2=y2, ya=ya, la2=la2, of=of, ob=ob, sf=sf, sb=sb, yb=yb, yc=yc,
                 s1=s1, x1=x1, a=a, s2=s2)
    return (out[0] if target is None else tuple(out)), saved


def _layer_bwd(dy, q, sv, tk, emit):
    g = {}
    da, ds2, dx1, g["dg2"], g["db2"] = _ffn_bwd_act(dy, sv["s2"], sv["a"], q["w_ff1"], q["w_ff2"], q["ln2_g"])
    dw1, dw2 = _ffn_bwd_w(sv["x1"], da, sv["a"], ds2)
    tie = emit(dict(w_ff1=dw1, w_ff2=dw2))
    dya, dyb, dyc, dxp, dwo, g["dg1"], g["db1"] = _outproj_bwd(dx1, sv["s1"], sv["ya"], sv["yb"], sv["yc"],
                                                               q["w_out"], q["ln1_g"] + tie)
    h = sv["h"]
    daq, dakv, g["dsink"] = _swa_bwd(h, tk, q["swa_sink"], dyc)
    do, gr, g["dlng"] = _gla_post_bwd(sv["of"], sv["ob"], h, q["lng"], dyb)
    gq_f, gk_f, gv_f, gl_f, gq_b, gk_b, gv_b, gl_b = _gla_bwd(h, sv["la2"], do, sv["sf"], sv["sb"])
    dhl, g["dwa"], g["dba"] = _gla_gate_bwd(h, q["wa"], q["ba"], gl_f, gl_b)
    dyp, dud, g["dd"], dw4, g["dbv"], g["dbg"] = _s5_glu_bwd(sv["y2"], h, q["dsk"], q["w4"], q["bv"], q["bg"], dya)
    tie = emit(dict(w_out=dwo.reshape(NSHARD, D // NSHARD, D), s5_w_glu=dw4))
    du2, g["dbre"], g["dbim"], g["dcre"], g["dcim"], g["dmu"] = _s5_bwd(
        h, dyp, sv["hre"], sv["him"], q["bre"], q["bim"], q["cre"], q["cim"], (q["tabc"][0], q["tabc"][1] + tie))
    dx, dwt = _inproj_bwd(sv["x"], q["w_in"], dxp, du2, dud, gq_f, gq_b, gk_f, gk_b, gv_f, gv_b, gr, daq, dakv, dhl)
    tie = emit(dict(w_in=dwt))
    return dx, g, tie


NATIVE = ("dmu", "dbre", "dbim", "dcre", "dcim", "dd", "dbv", "dbg", "dwa", "dba", "dlng", "dsink",
          "dg1", "db1", "dg2", "db2", "loss")
ICI_CORE = (0, 0, 0, 1, 1, 0, 0, 0, 1, 1, 1, 1, 0, 0, 1, 1, 0)
Y_FIRST = (0, 0, 1, 0, 1, 1, 0, 1, 0, 1, 0, 1, 0, 1, 0, 1, 0)


def _finish_small(n, w):
    g = {}
    dmu = n["dmu"]
    dlr = dmu[:, :, :, 0].reshape(DEPTH, 2, S5_G, S5_P)
    dli = dmu[:, :, :, 1].reshape(DEPTH, 2, S5_G, S5_P)

    def unblock(c, perm, shape):
        return c.reshape(DEPTH, 2, 2, S5_H, 8, S5_P).transpose(perm).reshape(shape)

    b_shape, c_shape = (DEPTH, 2, S5_G, S5_P, S5_H), (DEPTH, 2, S5_G, S5_H, S5_P)
    _, vjp = jax.vjp(_s5_discretize, w["s5_a_re"], w["s5_a_im"], w["s5_log_step"], w["s5_b_re"], w["s5_b_im"])
    (g["s5_a_re"], g["s5_a_im"], g["s5_log_step"], g["s5_b_re"], g["s5_b_im"]) = vjp(
        (dlr, dli, unblock(n["dbre"], (0, 1, 2, 4, 5, 3), b_shape), unblock(n["dbim"], (0, 1, 2, 4, 5, 3), b_shape)))
    g["s5_c_re"] = unblock(n["dcre"], (0, 1, 2, 4, 3, 5), c_shape)
    g["s5_c_im"] = unblock(n["dcim"], (0, 1, 2, 4, 3, 5), c_shape)
    g["s5_d"] = n["dd"].reshape(DEPTH, S5_G, S5_H)
    g["s5_b_glu"] = jnp.concatenate([n["dbv"], n["dbg"]], axis=2).reshape(DEPTH, 512)
    g["gla_w_a"] = jnp.stack([n["dwa"][:, 0:16, 0:128], n["dwa"][:, 16:32, 128:256]], axis=1)
    g["gla_b_a"] = n["dba"].reshape(DEPTH, 2, 128)
    g["gla_ln_g"] = n["dlng"].reshape(DEPTH, 256)
    g["swa_sink"] = n["dsink"][:, :, 0]
    for k, s in (("ln1_g", "dg1"), ("ln1_b", "db1"), ("ln2_g", "dg2"), ("ln2_b", "db2")):
        g[k] = n[s].reshape(DEPTH, D)
    return g


def _local_step(x, target, qs, tk, fetch, emit):
    saved = []
    for l, q in enumerate(qs):
        x, sv = _layer_fwd(x, q, tk, functools.partial(fetch, l), target if l == DEPTH - 1 else None)
        saved.append(sv)
    dy, lacc = x
    smalls = [None] * DEPTH
    tie = 0.0
    for l in reversed(range(DEPTH)):
        qs[l]["ln2_g"] = qs[l]["ln2_g"] + tie
        dy, smalls[l], tie = _layer_bwd(dy, qs[l], saved[l], tk, functools.partial(emit, l))
    smalls[0]["db2"] = smalls[0]["db2"] + tie
    for l in range(DEPTH):
        smalls[l]["loss"] = lacc if l == 0 else jnp.zeros_like(lacc)
    return lacc[0, 0], dy, smalls


BIG = ("w_in", "s5_w_glu", "w_out", "w_ff1", "w_ff2")
SMALL = ("s5_a_re", "s5_a_im", "s5_log_step", "s5_b_re", "s5_b_im", "s5_c_re", "s5_c_im", "s5_d", "s5_b_glu",
         "gla_w_a", "gla_b_a", "gla_ln_g", "swa_sink", "ln1_g", "ln1_b", "ln2_g", "ln2_b")
ANY = pl.BlockSpec(memory_space=pl.ANY)


def _place():
    x, y, c = lax.axis_index("x"), lax.axis_index("y"), lax.axis_index("c")
    return x, y, c, [(1 - x, y), (x, 1 - y), (1 - x, 1 - y)]


HBM = pl.BlockSpec(memory_space=pltpu.HBM)
SEMS = pl.BlockSpec(memory_space=pltpu.SEMAPHORE)
EFFECT = pltpu.SideEffectType.DATAFLOW_SIDE_EFFECTING


def _push_copies(ins, lands, send, recv, gather, sending):
    x, y, c, chips = _place()
    me = 2 * x + y
    if gather == "sibling":
        return [pltpu.make_async_remote_copy(src_ref=ins[a], dst_ref=lands[a], send_sem=send.at[a], recv_sem=recv.at[a],
                                             device_id=(x, y, 1 - c), device_id_type=MESH) for a in range(len(lands))]
    out = []
    for a in range(len(lands)):
        for j, (px, py) in enumerate(chips):
            peer = 2 * px + py
            src = lands[a].at[me] if gather else ins[a].at[peer if sending else me]
            dst = lands[a].at[me if sending else peer]
            out.append(pltpu.make_async_remote_copy(src_ref=src, dst_ref=dst, send_sem=send.at[3 * a + j],
                                                    recv_sem=recv.at[3 * a + j], device_id=(px, py, c),
                                                    device_id_type=MESH))
    return out


def _push_start(name, arrs, gather):
    n = len(arrs)
    ops = list(arrs) if gather is True else list(arrs) + [lax.empty(s.shape, s.dtype) for s in arrs]
    m = len(ops)

    def body(*refs):
        ins, lnd = (refs[:n], refs[:n]) if gather is True else (refs[:n], refs[n:m])
        for cp in _push_copies(ins, lnd, refs[m], refs[m + 1], gather, True):
            cp.start()
        refs[-1][...] = jnp.zeros((8, 128), F32)

    ops = [pltpu.with_memory_space_constraint(t, pltpu.HBM) for t in ops]
    res = pl.pallas_call(
        body, name=name,
        out_shape=(pltpu.SemaphoreType.DMA((3 * n,)), pltpu.SemaphoreType.DMA((3 * n,)),
                   *[pltpu.HBM(t.shape, t.dtype) for t in ops], _sds((8, 128))),
        in_specs=[HBM] * m,
        out_specs=(SEMS, SEMS, *[HBM] * m, pl.BlockSpec(memory_space=pltpu.VMEM)),
        input_output_aliases={i: 2 + i for i in range(m)},
        compiler_params=pltpu.CompilerParams(has_side_effects=EFFECT))(*ops)
    return res[0], res[1], list(res[2:2 + m]), res[-1]


def _push_wait(name, started, after, gather):
    send, recv, ops, _ = started
    m = len(ops)
    n = m if gather is True else m // 2

    def body(*refs):
        ins, lnd = (refs[:n], refs[:n]) if gather is True else (refs[:n], refs[n:m])
        for cp in _push_copies(ins, lnd, refs[m], refs[m + 1], gather, False):
            cp.wait_send()
            cp.wait_recv()

    res = pl.pallas_call(
        body, name=name,
        out_shape=[pltpu.HBM(t.shape, t.dtype) for t in ops],
        in_specs=[HBM] * m + [SEMS, SEMS, ANY], out_specs=[HBM] * m,
        input_output_aliases={i: i for i in range(m)},
        compiler_params=pltpu.CompilerParams(has_side_effects=EFFECT))(*ops, send, recv, after)
    return list(res)


def _row_tile(rows):
    return max(t for t in range(8, min(rows, 512) + 1, 8) if rows % t == 0)


def _cast_to_slot(me, w, l):
    _, rows, cols = w.shape
    tr = _row_tile(rows)

    def body(me_ref, w_ref, o_ref):
        o_ref[0] = w_ref[0].astype(MX)

    return pl.pallas_call(
        body,
        grid_spec=pltpu.PrefetchScalarGridSpec(
            num_scalar_prefetch=1, grid=(rows // tr,),
            in_specs=[pl.BlockSpec((1, tr, cols), lambda i, me_: (l, i, 0))],
            out_specs=pl.BlockSpec((1, tr, cols), lambda i, me_: (me_[0], i, 0))),
        out_shape=_sds((NSHARD, rows, cols), MX), name="cast_to_slot", compiler_params=_cp(("arbitrary",)))(me, w)


def _sum_sources(me, recv, own):
    _, rows, cols = recv[0].shape
    tr = min(_row_tile(rows), 256) if rows % 256 == 0 else _row_tile(rows)
    nt = rows // tr

    def body(me_ref, *refs):
        o_ref = refs[-1]
        for l in range(DEPTH):
            @pl.when(pl.program_id(0) == l)
            def _():
                r_ref, own_ref = refs[2 * l], refs[2 * l + 1]
                part = [jnp.where(me_ref[0] == s, own_ref[0], r_ref[s]).astype(F32) for s in range(NSHARD)]
                o_ref[...] = ((part[0] + part[1]) + part[2]) + part[3]

    in_specs = []
    for l in range(DEPTH):
        pick = lambda g, i, me_, l=l: jnp.where(g == l, i, jnp.where(g < l, 0, nt - 1))
        in_specs += [pl.BlockSpec((NSHARD, tr, cols), lambda g, i, me_, pick=pick: (0, pick(g, i, me_), 0)),
                     pl.BlockSpec((1, tr, cols), lambda g, i, me_, pick=pick: (me_[0], pick(g, i, me_), 0))]
    return pl.pallas_call(
        body,
        grid_spec=pltpu.PrefetchScalarGridSpec(
            num_scalar_prefetch=1, grid=(DEPTH, nt), in_specs=in_specs,
            out_specs=pl.BlockSpec((tr, cols), lambda g, i, me_: (g * nt + i, 0))),
        out_shape=_sds((DEPTH * rows, cols)), name="sum_sources",
        compiler_params=_cp(("arbitrary", "arbitrary")))(me, *[t for l in range(DEPTH) for t in (recv[l], own[l])])


def _allreduce_small(per_layer):
    nk = len(per_layer[0])
    n = DEPTH * nk
    shapes = [a.shape for a in per_layer[0]]

    def body(*refs):
        ins, outs = refs[:n], refs[n:n + nk]
        sibs, slots = refs[n + nk:n + 2 * nk], refs[n + 2 * nk:n + 3 * nk]
        send, recv = refs[n + 3 * nk:]
        x, y, c, chips = _place()
        me = 2 * x + y
        d2d = [pltpu.make_async_remote_copy(src_ref=ins[l * nk + k], dst_ref=sibs[k].at[l], send_sem=send.at[l * nk + k],
                                            recv_sem=recv.at[l * nk + k], device_id=(x, y, 1 - c), device_id_type=MESH)
               for l in range(DEPTH) for k in range(nk)]
        for cp in d2d:
            cp.start()
        for cp in d2d:
            cp.wait()
        for l in range(DEPTH):
            for k in range(nk):
                slots[k][0, l] = ins[l * nk + k][...] + sibs[k][l]

        def swap(k, stage):
            peer = (1 - x, y, c) if stage == Y_FIRST[k] else (x, 1 - y, c)
            return pltpu.make_async_remote_copy(src_ref=slots[k].at[2 * stage], dst_ref=slots[k].at[2 * stage + 1],
                                                send_sem=send.at[n + 3 * k + stage], recv_sem=recv.at[n + 3 * k + stage],
                                                device_id=peer, device_id_type=MESH)

        def handover(k):
            return pltpu.make_async_remote_copy(src_ref=outs[k], dst_ref=outs[k], send_sem=send.at[n + 3 * nk + k],
                                                recv_sem=recv.at[n + 3 * nk + k], device_id=(x, y, 1 - c),
                                                device_id_type=MESH)

        halves = (tuple(k for k in range(nk) if ICI_CORE[k] == 0), tuple(k for k in range(nk) if ICI_CORE[k] == 1))
        for cc in range(2):
            @pl.when(c == cc)
            def _():
                mine, theirs = halves[cc], halves[1 - cc]
                for stage in range(2):
                    cps = [swap(k, stage) for k in mine]
                    for cp in cps:
                        cp.start()
                    for cp in cps:
                        cp.wait()
                    for k in mine:
                        if stage == 0:
                            slots[k][2] = slots[k][0] + slots[k][1]
                        else:
                            outs[k][...] = slots[k][2] + slots[k][3]
                over = [handover(k) for k in mine]
                for cp in over:
                    cp.start()
                for k in theirs:
                    handover(k).wait_recv()
                for cp in over:
                    cp.wait_send()

    vm = pl.BlockSpec(memory_space=pltpu.VMEM)
    return pl.pallas_call(
        body, in_specs=[vm] * n, out_specs=[vm] * nk, out_shape=[_sds((DEPTH,) + s) for s in shapes],
        scratch_shapes=([pltpu.VMEM((DEPTH,) + s, F32) for s in shapes]
                        + [pltpu.VMEM((NSHARD, DEPTH) + s, F32) for s in shapes]
                        + [pltpu.SemaphoreType.DMA((n + 4 * nk,)), pltpu.SemaphoreType.DMA((n + 4 * nk,))]),
        name="allreduce_small", compiler_params=pltpu.CompilerParams(vmem_limit_bytes=VMEM_LIMIT))(
            *[a for layer in per_layer for a in layer])


def _adamw_math(w, g, m, v):
    m = ADAM_B1 * m + (1.0 - ADAM_B1) * g
    v = ADAM_B2 * v + (1.0 - ADAM_B2) * jnp.square(g)
    m_hat = m / (1.0 - ADAM_B1 ** ADAM_STEP)
    v_hat = v / (1.0 - ADAM_B2 ** ADAM_STEP)
    delta = -ADAM_LR * (m_hat / (jnp.sqrt(v_hat) + ADAM_EPS) + ADAM_WD * w)
    return delta, m, v


def _adamw(g_parts, w, m, v):
    rows, cols = w.shape
    tr = 256 if rows % 256 == 0 else _row_tile(rows)
    k = len(g_parts)

    def body(*refs):
        g = refs[0][...]
        for r in refs[1:k]:
            g = g + r[...]
        w_ref, m_ref, v_ref, go, do, mo, vo = refs[k:]
        d, mn, vn = _adamw_math(w_ref[...], g, m_ref[...], v_ref[...])
        go[...] = g
        do[...] = d
        mo[...] = mn
        vo[...] = vn

    spec = pl.BlockSpec((tr, cols), lambda i: (i, 0))
    return pl.pallas_call(
        body, grid=(rows // tr,), in_specs=[spec] * (k + 3), out_specs=[spec] * 4,
        out_shape=[_sds((rows, cols))] * 4, name="adamw", compiler_params=_cp(("parallel",)))(*g_parts, w, m, v)


def _adamw_small(gs, ws, ms, vs):
    n = len(gs)

    def body(*refs):
        for k in range(n):
            d, mn, vn = _adamw_math(refs[n + k][...], refs[k][...], refs[2 * n + k][...], refs[3 * n + k][...])
            refs[4 * n + k][...] = d
            refs[5 * n + k][...] = mn
            refs[6 * n + k][...] = vn

    vm = pl.BlockSpec(memory_space=pltpu.VMEM)
    shapes = [_sds(a.shape) for a in ws]
    res = pl.pallas_call(
        body, in_specs=[vm] * (4 * n), out_specs=[vm] * (3 * n), out_shape=shapes * 3, name="adamw_small",
        compiler_params=pltpu.CompilerParams(vmem_limit_bytes=VMEM_LIMIT))(*gs, *ws, *ms, *vs)
    return res[:n], res[n:2 * n], res[2 * n:]


_ARGS = ("x", "w_in", "s5_a_re", "s5_a_im", "s5_log_step", "s5_b_re", "s5_b_im", "s5_c_re", "s5_c_im", "s5_d",
         "s5_w_glu", "s5_b_glu", "gla_w_a", "gla_b_a", "gla_ln_g", "swa_sink", "w_out", "ln1_g", "ln1_b", "w_ff1",
         "w_ff2", "ln2_g", "ln2_b")
_WEIGHTS = _ARGS[1:]


def kernel(x, w_in, s5_a_re, s5_a_im, s5_log_step, s5_b_re, s5_b_im, s5_c_re, s5_c_im, s5_d, s5_w_glu, s5_b_glu, gla_w_a, gla_b_a, gla_ln_g, swa_sink, w_out, ln1_g, ln1_b, w_ff1, w_ff2, ln2_g, ln2_b, loss_target, m_w_in, m_s5_a_re, m_s5_a_im, m_s5_log_step, m_s5_b_re, m_s5_b_im, m_s5_c_re, m_s5_c_im, m_s5_d, m_s5_w_glu, m_s5_b_glu, m_gla_w_a, m_gla_b_a, m_gla_ln_g, m_swa_sink, m_w_out, m_ln1_g, m_ln1_b, m_w_ff1, m_w_ff2, m_ln2_g, m_ln2_b, v_w_in, v_s5_a_re, v_s5_a_im, v_s5_log_step, v_s5_b_re, v_s5_b_im, v_s5_c_re, v_s5_c_im, v_s5_d, v_s5_w_glu, v_s5_b_glu, v_gla_w_a, v_gla_b_a, v_gla_ln_g, v_swa_sink, v_w_out, v_ln1_g, v_ln1_b, v_w_ff1, v_w_ff2, v_ln2_g, v_ln2_b):
    given = dict(locals())
    w = {k: given[k] for k in _WEIGHTS}
    mom = {k: given["m_" + k] for k in _WEIGHTS}
    var = {k: given["v_" + k] for k in _WEIGHTS}

    me = (2 * lax.axis_index("x") + lax.axis_index("y")).astype(jnp.int32).reshape(1)
    tr = lambda t: t.transpose(0, 2, 1)
    shard = {k: (tr(w[k]) if k == "w_in" else w[k]) for k in BIG}
    qs = [None] * DEPTH

    first = ("w_in", "s5_w_glu", "w_out")
    follow = {(0, "w_in"): [(0, first[1:]), (0, BIG[3:])], (0, "s5_w_glu"): [(1, first)], (0, "w_ff1"): [(1, BIG[3:])]}
    gathers = {}

    casts = {}

    def start_gather(l, names, behind=None):
        lands = [casts.pop((l, k)) if (l, k) in casts else _cast_to_slot(me, shard[k], l) for k in names]
        if behind is not None:
            lands, behind = lax.optimization_barrier((lands, behind))
        st = _push_start(f"gather_start_{l}_{names[0]}", lands, True)
        for k in names:
            gathers[l, k] = [names, st, None]
        return st[-1], behind

    token = start_gather(0, first[:1])[0]
    zero = token[0, 0]
    for l in range(DEPTH):
        for k in BIG:
            if (l, k) not in gathers:
                casts[l, k] = _cast_to_slot(me, lax.optimization_barrier((shard[k], token))[0], l)
        qs[l] = _layer_prep({k: (w[k][l] + zero if k == "s5_a_re" else w[k][l]) for k in SMALL})
    token, casts, qs = lax.optimization_barrier((token, casts, qs))

    def fetch(l, name, after):
        names, st, got = gathers[l, name]
        tie = None
        if got is None:
            if l == 0 and name == "w_in":
                after = token
            lands = _push_wait(f"gather_wait_{l}_{names[0]}", st, after, True)
            for l2, names2 in follow.get((l, name), ()):
                tok, lands[0] = start_gather(l2, names2, lands[0])
                tie = tok if tie is None else tie + tok
            got = dict(zip(names, lands))
            for k in names:
                gathers[l, k][2] = got
        full = got[name]
        if name == "w_in":
            return _in_rows(full, token if tie is None else tie)
        if tie is not None:
            near = "bv" if name == "s5_w_glu" else "ln2_b"
            qs[l][near] = qs[l][near] + tie[0, 0]
        return full.reshape(D, D) if name == "w_out" else full

    scatters, held = [], {}

    def emit(l, grads):
        if l > 0:
            held.update(grads)
            if "w_in" not in grads:
                return 0.0
            grads = dict(held)
            held.clear()
        names = tuple(grads)
        st = _push_start(f"scatter_start_{l}_{names[0]}", [grads[k] for k in names], False)
        scatters.append((l, names, st))
        return st[-1][0, 0]

    loss, dx, smalls = _local_step(x.reshape(N, D), loss_target.reshape(N, D), qs, _rope_tables(128), fetch, emit)

    out, recv, own = {}, {}, {}

    def collect(keys, after):
        for l, names, st in scatters:
            if names[0] in keys:
                ops = _push_wait(f"scatter_wait_{l}_{names[0]}", st, after, False)
                for i, k in enumerate(names):
                    own[l, k], recv[l, k] = ops[i], ops[len(names) + i]

    def shard_sums(keys):
        return [_sum_sources(me, [recv[l, k] for l in range(DEPTH)], [own[l, k] for l in range(DEPTH)]) for k in keys]

    def to_sibling(keys, sums):
        return _push_start(f"swap_start_{keys[0]}", sums, "sibling")

    def apply(keys, started, after):
        ops = _push_wait(f"swap_wait_{keys[0]}", started, after, "sibling")
        for i, k in enumerate(keys):
            mine, other = ops[i], ops[len(keys) + i]
            shp = shard[k].shape
            r = _adamw([mine, other], *((tr(t[k]) if k == "w_in" else t[k]).reshape(-1, shp[-1]) for t in (w, mom, var)))
            r = [t.reshape(shp) for t in r]
            out[k] = [tr(t) for t in r] if k == "w_in" else r
        return out[keys[-1]][1]

    collect(("w_ff1", "w_ff2", "w_out", "s5_w_glu"), dx)
    sums = shard_sums(("w_ff1", "w_ff2", "w_out", "s5_w_glu"))
    sums, smalls[0]["db1"] = lax.optimization_barrier((sums, smalls[0]["db1"]))
    native = _allreduce_small([[smalls[l][k] for k in NATIVE] for l in range(DEPTH)])
    sums, native = lax.optimization_barrier((sums, native))
    ff = to_sibling(("w_ff1", "w_ff2"), sums[:2])
    mix = to_sibling(("w_out", "s5_w_glu"), sums[2:])
    native = dict(zip(NATIVE, native))
    native["db1"] = native["db1"] + (ff[-1][0, 0] + mix[-1][0, 0])
    loss = native["loss"][0, 0, 0] + native["loss"][1, 0, 0]
    gsmall = _finish_small(native, w)
    view = lambda k, t: t.transpose(0, 1, 2, 4, 3) if k in ("s5_b_re", "s5_b_im") else t
    res = _adamw_small(*([view(k, t[k]) for k in SMALL] for t in (gsmall, w, mom, var)))
    for i, k in enumerate(SMALL):
        out[k] = [gsmall[k]] + [view(k, r[i]) for r in res]
    last = apply(("w_ff1", "w_ff2"), ff, res[0][-1])
    collect(("w_in",), last)
    win = to_sibling(("w_in",), shard_sums(("w_in",)))
    last = apply(("w_out", "s5_w_glu"), mix, win[-1])
    apply(("w_in",), win, last)

    return (loss, dx.reshape(NSEQ, L, D), *[out[k][0] for k in _WEIGHTS], *[out[k][1] for k in _WEIGHTS],
            *[out[k][2] for k in _WEIGHTS], *[out[k][3] for k in _WEIGHTS])
```

```python
import functools
import math

import jax
import jax.numpy as jnp
from jax import lax
from jax.experimental import pallas as pl
from jax.experimental.pallas import tpu as pltpu

F32 = jnp.float32
MX = jnp.bfloat16
MESH = pl.DeviceIdType.MESH

DEPTH = 2
NSEQ = 2
L = 2048
N = NSEQ * L
D = 1024
DFF = 4096
NSHARD = 4
S5_G, S5_H, S5_P = 16, 16, 64
GLA_CHUNK = 64
NCHUNK = L // GLA_CHUNK
GLA_GROUP = 4
NGROUP = NCHUNK // GLA_GROUP
SWA_BLK = 128
NBLK = L // SWA_BLK
SWA_PER = 2
ROT = 16
ROPE_THETA = 500000.0
LN_EPS = 1e-5
ALPHA = (2 * DEPTH) ** 0.25
NEG_BIG = -1e30
DIN = 1824
DINP = 1920
ADAM_LR, ADAM_B1, ADAM_B2, ADAM_EPS, ADAM_WD, ADAM_STEP = 0.001, 0.9, 0.999, 1e-08, 0.01, 10
VMEM_LIMIT = 56 * 1024 * 1024
TT = 512
SW = 512
FFN_TM = 512
FFN_TM_W = 1024
FFN_WB = 1
FFN_VMEM = 60 * 1024 * 1024
INPROJ_BWD_TM = 512


def _cp(sem, vmem=VMEM_LIMIT):
    return pltpu.CompilerParams(dimension_semantics=sem, vmem_limit_bytes=vmem)


def _mm(a, b):
    return jnp.dot(a.astype(MX), b.astype(MX), preferred_element_type=F32)


def _mm_nt(a, b):
    return lax.dot_general(a.astype(MX), b.astype(MX), (((1,), (1,)), ((), ())), preferred_element_type=F32)


def _mm_tn(a, b):
    return lax.dot_general(a.astype(MX), b.astype(MX), (((0,), (0,)), ((), ())), preferred_element_type=F32)


@jax.custom_vjp
def _dmm(a, b):
    return _mm(a, b)


_dmm.defvjp(lambda a, b: (_mm(a, b), (a, b)), lambda r, g: (_mm_nt(g, r[1]), _mm_tn(r[0], g)))


@jax.custom_vjp
def _dmm_nt(a, b):
    return _mm_nt(a, b)


_dmm_nt.defvjp(lambda a, b: (_mm_nt(a, b), (a, b)), lambda r, g: (_mm(g, r[1]), _mm_tn(g, r[0])))


@jax.custom_vjp
def _dmm_tn(a, b):
    return _mm_tn(a, b)


_dmm_tn.defvjp(lambda a, b: (_mm_tn(a, b), (a, b)), lambda r, g: (_mm_nt(r[1], g), _mm(r[0], g)))


def _split3(x):
    hi = x.astype(MX)
    r1 = x - hi.astype(F32)
    mid = r1.astype(MX)
    lo = (r1 - mid.astype(F32)).astype(MX)
    return hi, mid, lo


def _chunk_pairs(rows, rev, strict):
    r = lax.broadcasted_iota(jnp.int32, (rows, rows), 0)
    c = lax.broadcasted_iota(jnp.int32, (rows, rows), 1)
    order = ((c > r) if strict else (c >= r)) if rev else ((c < r) if strict else (c <= r))
    return (r // GLA_CHUNK == c // GLA_CHUNK) & order


def _cums_impl(x, rev):
    rows, w = x.shape
    t = jnp.where(_chunk_pairs(rows, rev, False), 1.0, 0.0).astype(MX)
    s = jnp.dot(t, jnp.concatenate(_split3(x), axis=1), preferred_element_type=F32)
    return s[:, 0:w] + s[:, w:2 * w] + s[:, 2 * w:3 * w]


@functools.partial(jax.custom_vjp, nondiff_argnums=(1,))
def _cums(x, rev):
    return _cums_impl(x, rev)


_cums.defvjp(lambda x, rev: (_cums_impl(x, rev), None), lambda rev, r, g: (_cums_impl(g, not rev),))


def _ln_fwd(s, g, b):
    mu = jnp.mean(s, axis=-1, keepdims=True)
    xc = s - mu
    var = jnp.mean(xc * xc, axis=-1, keepdims=True)
    return xc * lax.rsqrt(var + LN_EPS) * g + b


def _ln_bwd(dy, s, g):
    mu = jnp.mean(s, axis=-1, keepdims=True)
    xc = s - mu
    var = jnp.mean(xc * xc, axis=-1, keepdims=True)
    rstd = lax.rsqrt(var + LN_EPS)
    xhat = xc * rstd
    dxh = dy * g
    ds = rstd * (dxh - jnp.mean(dxh, axis=-1, keepdims=True) - xhat * jnp.mean(dxh * xhat, axis=-1, keepdims=True))
    return ds, jnp.sum(dy * xhat, axis=0, keepdims=True), jnp.sum(dy, axis=0, keepdims=True)


def _sds(shape, dtype=F32):
    return jax.ShapeDtypeStruct(shape, dtype)


_IN_ROW_PIECES = (((0, 0), (0, 456)), ((1, 0), (456, 456)), ((2, 0), (912, 112)), ((2, 112), (1792, 32)),
                  ((2, 144), (1024, 312)), ((3, 0), (1336, 456)))


def _in_rows(g4, behind):
    def body(g_ref, behind_ref, o_ref, tmp):
        tmp[DIN:DINP] = jnp.zeros((DINP - DIN, D), F32)
        for (j, s0), (d0, n_) in _IN_ROW_PIECES:
            tmp[d0:d0 + n_] = g_ref[j, s0:s0 + n_].astype(F32)
        o_ref[...] = tmp[...].astype(MX)

    vm = pl.BlockSpec(memory_space=pltpu.VMEM)
    return pl.pallas_call(body, in_specs=[vm, pl.BlockSpec(memory_space=pl.ANY)], out_specs=vm,
                          out_shape=_sds((DINP, D), MX), scratch_shapes=[pltpu.VMEM((DINP, D), F32)], name="in_rows",
                          compiler_params=pltpu.CompilerParams(vmem_limit_bytes=VMEM_LIMIT))(g4, behind)


def _inproj_fwd(x, wt, wa, ba):
    tm = 512

    def body(x_ref, w_ref, wa_ref, ba_ref, h_ref, la_ref):
        h = _mm_nt(x_ref[...], w_ref[...])
        h_ref[...] = h
        la_ref[...] = _logsig(_mm(h[:, DINP - 128:], wa_ref[...]) + ba_ref[...]) * (1.0 / 16.0)

    return pl.pallas_call(
        body, grid=(N // tm,),
        in_specs=[pl.BlockSpec((tm, D), lambda i: (i, 0)), pl.BlockSpec((DINP, D), lambda i: (0, 0)),
                  pl.BlockSpec((128, 256), lambda i: (0, 0)), pl.BlockSpec((1, 256), lambda i: (0, 0))],
        out_specs=[pl.BlockSpec((tm, DINP), lambda i: (i, 0)), pl.BlockSpec((tm, 256), lambda i: (i, 0))],
        out_shape=[_sds((N, DINP)), _sds((N, 256))], name="inproj_fwd", compiler_params=_cp(("parallel",)))(x, wt, wa, ba)


def _inproj_bwd(x, w, dxp, du2, dud, gq_f, gq_b, gk_f, gk_b, gv_f, gv_b, gr, daq, dakv, h, wa, ba, dla_f, dla_b):
    tm = INPROJ_BWD_TM
    nt = N // tm

    def body(x_ref, w_ref, dxp_ref, du2_ref, dud_ref, gqf, gqb, gkf, gkb, gvf, gvb, gr_ref, daq_ref, dakv_ref,
             hl_ref, wa_ref, ba_ref, df_ref, db_ref, dx_ref, dw_ref, dwa_ref, dba_ref, acc):
        i = pl.program_id(0)
        f = lambda r: r[...].astype(F32)
        hl = hl_ref[...]
        pre = _mm(hl, wa_ref[...]) + ba_ref[...]
        dpre = jnp.concatenate([df_ref[...], db_ref[...]], axis=1) * (1.0 / 16.0) * jax.nn.sigmoid(-pre)
        dwa = _mm_tn(hl, dpre)[0:32]
        dba = jnp.sum(dpre, axis=0, keepdims=True)
        dh = jnp.concatenate([
            du2_ref[0] + du2_ref[1] + f(dud_ref), f(gqf) + f(gqb), f(gkf) + f(gkb), f(gvf) + f(gvb),
            f(gr_ref), f(daq_ref), f(dakv_ref), _mm_nt(dpre, wa_ref[...])], axis=1)
        dx_ref[...] = dxp_ref[...] + _mm(dh, w_ref[...])
        contrib = _mm_tn(dh, x_ref[...])

        @pl.when(i == 0)
        def _():
            acc[...] = contrib
            dwa_ref[...] = dwa
            dba_ref[...] = dba

        @pl.when(i > 0)
        def _():
            acc[...] += contrib
            dwa_ref[...] += dwa
            dba_ref[...] += dba

        @pl.when(i == nt - 1)
        def _():
            for (j, d0), (s0, n_) in _IN_ROW_PIECES:
                dw_ref[j, d0:d0 + n_] = acc[s0:s0 + n_].astype(MX)

    row = lambda w_: pl.BlockSpec((tm, w_), lambda i: (i, 0))
    return pl.pallas_call(
        body, grid=(nt,),
        in_specs=[row(D), pl.BlockSpec((DINP, D), lambda i: (0, 0)), row(D),
                  pl.BlockSpec((2, tm, 256), lambda i: (0, i, 0)), row(256), row(128), row(128), row(128), row(128),
                  row(256), row(256), row(256), row(512), row(256),
                  pl.BlockSpec((tm, 128), lambda i: (i, 14)), pl.BlockSpec((128, 256), lambda i: (0, 0)),
                  pl.BlockSpec((1, 256), lambda i: (0, 0)), row(128), row(128)],
        out_specs=[row(D), pl.BlockSpec((NSHARD, DIN // NSHARD, D), lambda i: (0, 0, 0)),
                   pl.BlockSpec((32, 256), lambda i: (0, 0)), pl.BlockSpec((1, 256), lambda i: (0, 0))],
        out_shape=[_sds((N, D)), _sds((NSHARD, DIN // NSHARD, D), MX), _sds((32, 256)), _sds((1, 256))],
        scratch_shapes=[pltpu.VMEM((DINP, D), F32)],
        name="inproj_bwd", compiler_params=_cp(("arbitrary",)))(
            x, w, dxp, du2, dud, gq_f, gq_b, gk_f, gk_b, gv_f, gv_b, gr, daq, dakv, h, wa, ba, dla_f, dla_b)


def _tile_scan(xr, xi, a, cr, ci, reverse):
    for lvl, d in enumerate((1, 2, 4)):
        sh = 8 - d if reverse else d
        sr = pltpu.roll(xr, sh, 0)
        si = pltpu.roll(xi, sh, 0)
        ar, ai = a[2 * lvl], a[2 * lvl + 1]
        xr, xi = xr + ar * sr - ai * si, xi + ar * si + ai * sr
    pr, pi = a[6], a[7]
    return xr + pr * cr - pi * ci, xi + pr * ci + pi * cr


NJ = TT // 8


def _lockstep_tables(mr, mi):
    def body(mr_ref, mi_ref, a_ref, p_ref, ac_ref, pc_ref):
        rowid = lax.broadcasted_iota(jnp.int32, (8, 2 * SW), 0)

        def mul(a, b):
            return a[0] * b[0] - a[1] * b[1], a[0] * b[1] + a[1] * b[0]

        for z in range(2):
            for sign, reverse, a_out, p_out in ((1.0, z == 1, a_ref, p_ref), (-1.0, z == 0, ac_ref, pc_ref)):
                m = (mr_ref[z:z + 1, :], sign * mi_ref[z:z + 1, :])
                pw = [m]
                for _ in range(NJ - 1):
                    pw.append(mul(pw[-1], m))
                n = pw[-1]
                link = [n]
                for _ in range(7):
                    link.append(mul(link[-1], n))
                tiles = [jnp.broadcast_to(m[0], (8, 2 * SW)), jnp.broadcast_to(m[1], (8, 2 * SW))]
                for d in (1, 2, 4):
                    keep = (rowid <= 7 - d) if reverse else (rowid >= d)
                    tiles += [jnp.where(keep, link[d - 1][c], 0.0) for c in range(2)]
                for c in range(2):
                    t = jnp.zeros((8, 2 * SW), F32)
                    for i in range(8):
                        t = jnp.where(rowid == (7 - i if reverse else i), link[i][c], t)
                    tiles.append(t)
                for blk in range(2):
                    lanes = slice(blk * SW, (blk + 1) * SW)
                    for k, t in enumerate(tiles):
                        a_out[z, blk, k] = t[:, lanes]
                    for j in range(NJ):
                        src = pw[NJ - 1 - j] if reverse else pw[j]
                        for c in range(2):
                            p_out[z, blk, c, j:j + 1, :] = src[c][:, lanes]

    vm = pl.BlockSpec(memory_space=pltpu.VMEM)
    a_shape, p_shape = _sds((2, 2, 10, 8, SW)), _sds((2, 2, 2, NJ, SW))
    a, p, ac, pc = pl.pallas_call(body, in_specs=[vm, vm], out_specs=[vm] * 4, out_shape=[a_shape, p_shape] * 2,
                                  name="s5_tables")(mr, mi)
    return (a, p), (ac, pc)


def _to_lockstep(ref, *lead):
    return jnp.concatenate([ref[(*lead, pl.ds(j, 8, stride=NJ), slice(None))] for j in range(NJ)], axis=0)


def _from_lockstep(val, ref, *lead):
    for j in range(NJ):
        ref[(*lead, pl.ds(j, 8, stride=NJ), slice(None))] = val[8 * j:8 * j + 8]


def _expand_powers(p_ref, pexp):
    for c in range(2):
        for j in range(NJ):
            pexp[c, j] = jnp.broadcast_to(p_ref[0, 0, c, j:j + 1, :], (8, SW))


def _lockstep_scan(xre, xim, a_ref, pexp, car, reverse, extra=None):
    a = [a_ref[0, 0, k] for k in range(10)]
    mr, mi = a[0], a[1]
    order = (lambda i: NJ - 1 - i) if reverse else (lambda i: i)

    def local(i, hcar):
        hr, hi = hcar
        r0 = pl.multiple_of(order(i) * 8, 8)
        hr, hi = mr * hr - mi * hi + xre[pl.ds(r0, 8), :], mr * hi + mi * hr + xim[pl.ds(r0, 8), :]
        xre[pl.ds(r0, 8), :] = hr
        xim[pl.ds(r0, 8), :] = hi
        return hr, hi

    z8 = jnp.zeros((8, SW), F32)
    er, ei = lax.fori_loop(0, NJ, local, (z8, z8), unroll=4)
    c0r, c0i = car[0], car[1]
    er, ei = _tile_scan(er, ei, a[2:], c0r, c0i, reverse)
    rowid = lax.broadcasted_iota(jnp.int32, (8, SW), 0)
    first, sh, last = (7, 7, 0) if reverse else (0, 1, 7)
    cvr = jnp.where(rowid == first, c0r, pltpu.roll(er, sh, 0))
    cvi = jnp.where(rowid == first, c0i, pltpu.roll(ei, sh, 0))
    car[0] = jnp.broadcast_to(er[last:last + 1, :], (8, SW))
    car[1] = jnp.broadcast_to(ei[last:last + 1, :], (8, SW))

    def fix(i, carry):
        j = order(i)
        r0 = pl.multiple_of(j * 8, 8)
        pr, pi = pexp[0, j], pexp[1, j]
        sr = xre[pl.ds(r0, 8), :] + pr * cvr - pi * cvi
        si = xim[pl.ds(r0, 8), :] + pr * cvi + pi * cvr
        xre[pl.ds(r0, 8), :] = sr
        xim[pl.ds(r0, 8), :] = si
        if extra is None:
            return carry
        return (sr, si, extra(r0, sr, si, carry[0], carry[1], carry[2]))

    init = (cvr, cvi, extra(None, None, None, None, None, None)) if extra is not None else 0
    return lax.fori_loop(0, NJ, fix, init, unroll=4)


def _s5_time_block(z, s, t, adjoint):
    flip = (1 - z) if adjoint else z
    return s * (L // TT) + t + flip * (L // TT - 1 - 2 * t)


def _s5_fwd(h, bre, bim, cre, cim, tab):
    nt = L // TT
    taba, tabp = tab

    def body(u_ref, bre_ref, bim_ref, cre_ref, cim_ref, a_ref, p_ref, hre_ref, him_ref, y_ref, car, pexp):
        z = pl.program_id(1)
        s = pl.program_id(2)
        tc = pl.program_id(3)

        @pl.when(tc == 0)
        def _():
            car[...] = jnp.zeros_like(car)

        @pl.when((tc == 0) & (s == 0))
        def _():
            _expand_powers(p_ref, pexp)

        u = _to_lockstep(u_ref)
        hre_ref[0] = _mm(u, bre_ref[0, 0])
        him_ref[0] = _mm(u, bim_ref[0, 0])

        @pl.when(z == 0)
        def _():
            _lockstep_scan(hre_ref.at[0], him_ref.at[0], a_ref, pexp, car, False)

        @pl.when(z == 1)
        def _():
            _lockstep_scan(hre_ref.at[0], him_ref.at[0], a_ref, pexp, car, True)

        _from_lockstep(_mm(hre_ref[0], cre_ref[0, 0]) - _mm(him_ref[0], cim_ref[0, 0]), y_ref, 0)

    tb = lambda b, z, s, t: _s5_time_block(z, s, t, False)
    wspec = lambda r, c: pl.BlockSpec((1, 1, r, c), lambda b, z, s, t: (z, b, 0, 0))
    return pl.pallas_call(
        body, grid=(2, 2, NSEQ, nt),
        in_specs=[pl.BlockSpec((TT, 128), lambda b, z, s, t: (tb(b, z, s, t), b)),
                  wspec(128, SW), wspec(128, SW), wspec(SW, 128), wspec(SW, 128),
                  pl.BlockSpec((1, 1, 10, 8, SW), lambda b, z, s, t: (z, b, 0, 0, 0)),
                  pl.BlockSpec((1, 1, 2, NJ, SW), lambda b, z, s, t: (z, b, 0, 0, 0))],
        out_specs=[pl.BlockSpec((1, TT, SW), lambda b, z, s, t: (z, tb(b, z, s, t), b)),
                   pl.BlockSpec((1, TT, SW), lambda b, z, s, t: (z, tb(b, z, s, t), b)),
                   pl.BlockSpec((1, TT, 128), lambda b, z, s, t: (z, tb(b, z, s, t), b))],
        out_shape=[_sds((2, N, 2 * SW)), _sds((2, N, 2 * SW)), _sds((2, N, 256))],
        scratch_shapes=[pltpu.VMEM((2, 8, SW), F32), pltpu.VMEM((2, NJ, 8, SW), F32)],
        name="s5_fwd", compiler_params=_cp(("arbitrary",) * 4))(h, bre, bim, cre, cim, taba, tabp)


def _s5_bwd(h, dyp, hre, him, bre, bim, cre, cim, tabc):
    nt = L // TT
    taba, tabp = tabc

    def body(u_ref, dy_ref, hre_ref, him_ref, bre_ref, bim_ref, cre_ref, cim_ref, a_ref, p_ref,
             du_ref, dbre_ref, dbim_ref, dcre_ref, dcim_ref, dmu_ref, gre, gim, car, acc, macc, pexp):
        z = pl.program_id(1)
        s = pl.program_id(2)
        tc = pl.program_id(3)

        @pl.when(tc == 0)
        def _():
            car[...] = jnp.zeros_like(car)

        @pl.when((tc == 0) & (s == 0))
        def _():
            acc[...] = jnp.zeros_like(acc)
            macc[...] = jnp.zeros_like(macc)
            _expand_powers(p_ref, pexp)

        dy = _to_lockstep(dy_ref)
        gre[...] = _mm_nt(dy, cre_ref[0, 0])
        gim[...] = -_mm_nt(dy, cim_ref[0, 0])

        def run(reverse):
            def pair(r0, gr_, gi_, pvr, pvi, m):
                if r0 is None:
                    return (macc[0], macc[1])
                hr = hre_ref[0, pl.ds(r0, 8), :]
                hi = him_ref[0, pl.ds(r0, 8), :]
                return (m[0] + pvr * hr + pvi * hi, m[1] + pvi * hr - pvr * hi)

            _, _, (dmr, dmi) = _lockstep_scan(gre, gim, a_ref, pexp, car, reverse, pair)
            macc[0] = dmr
            macc[1] = dmi

        @pl.when(z == 0)
        def _():
            run(True)

        @pl.when(z == 1)
        def _():
            run(False)

        gr = gre[...]
        gi = gim[...]
        u = _to_lockstep(u_ref)
        _from_lockstep(_mm_nt(gr, bre_ref[0, 0]) + _mm_nt(gi, bim_ref[0, 0]), du_ref, 0)
        acc[0] += _mm_tn(u, gr)
        acc[1] += _mm_tn(u, gi)
        acc[2] += _mm_tn(dy, hre_ref[0])
        acc[3] -= _mm_tn(dy, him_ref[0])

        @pl.when((tc == nt - 1) & (s == NSEQ - 1))
        def _():
            grp = lax.broadcasted_iota(jnp.int32, (S5_H, SW), 1) // S5_P
            for k, out in enumerate((dbre_ref, dbim_ref, dcre_ref, dcim_ref)):
                c = jnp.zeros((S5_H, SW), F32)
                for i in range(8):
                    c = c + jnp.where(grp == i, acc[k, i * S5_H:(i + 1) * S5_H, :], 0.0)
                out[0, 0] = c
            dmu_ref[0, 0] = jnp.concatenate([jnp.sum(macc[0], axis=0, keepdims=True),
                                             jnp.sum(macc[1], axis=0, keepdims=True)], axis=0)

    tb = lambda b, z, s, t: _s5_time_block(z, s, t, True)
    wspec = lambda r, c: pl.BlockSpec((1, 1, r, c), lambda b, z, s, t: (z, b, 0, 0))
    tok = lambda w_: pl.BlockSpec((TT, w_), lambda b, z, s, t: (tb(b, z, s, t), b))
    st = pl.BlockSpec((1, TT, SW), lambda b, z, s, t: (z, tb(b, z, s, t), b))
    return pl.pallas_call(
        body, grid=(2, 2, NSEQ, nt),
        in_specs=[tok(128), tok(128), st, st, wspec(128, SW), wspec(128, SW), wspec(SW, 128), wspec(SW, 128),
                  pl.BlockSpec((1, 1, 10, 8, SW), lambda b, z, s, t: (z, b, 0, 0, 0)),
                  pl.BlockSpec((1, 1, 2, NJ, SW), lambda b, z, s, t: (z, b, 0, 0, 0))],
        out_specs=[pl.BlockSpec((1, TT, 128), lambda b, z, s, t: (z, tb(b, z, s, t), b)),
                   wspec(S5_H, SW), wspec(S5_H, SW), wspec(S5_H, SW), wspec(S5_H, SW),
                   wspec(2, SW)],
        out_shape=[_sds((2, N, 256))] + [_sds((2, 2, S5_H, SW))] * 4 + [_sds((2, 2, 2, SW))],
        scratch_shapes=[pltpu.VMEM((TT, SW), F32), pltpu.VMEM((TT, SW), F32), pltpu.VMEM((2, 8, SW), F32),
                        pltpu.VMEM((4, 128, SW), F32), pltpu.VMEM((2, 8, SW), F32), pltpu.VMEM((2, NJ, 8, SW), F32)],
        name="s5_bwd", compiler_params=_cp(("arbitrary",) * 4))(h, dyp, hre, him, bre, bim, cre, cim, taba, tabp)


_GELU_C = math.sqrt(2.0 / math.pi)


def _gelu(y):
    return 0.5 * y * (1.0 + jnp.tanh(_GELU_C * (y + 0.044715 * y * y * y)))


def _gelu_grad(y):
    t = jnp.tanh(_GELU_C * (y + 0.044715 * y * y * y))
    return 0.5 * (1.0 + t) + 0.5 * y * (1.0 - t * t) * _GELU_C * (1.0 + 3 * 0.044715 * y * y)


def _glu_halves(w4_ref):
    return (jnp.concatenate([w4_ref[0], w4_ref[1]], axis=1), jnp.concatenate([w4_ref[2], w4_ref[3]], axis=1))


def _s5_glu_fwd(y2, h, dsk, w4, bv, bg):
    tm = 512

    def body(y2_ref, u_ref, d_ref, w4_ref, bv_ref, bg_ref, ya_ref):
        wv, wg = _glu_halves(w4_ref)
        z = _gelu(y2_ref[0] + y2_ref[1] + d_ref[...] * u_ref[...])
        val = _mm(z, wv) + bv_ref[...]
        gate = _mm(z, wg) + bg_ref[...]
        ya_ref[...] = (val * jax.nn.sigmoid(gate)).astype(MX)

    full = lambda r, c: pl.BlockSpec((r, c), lambda i: (0, 0))
    return pl.pallas_call(
        body, grid=(N // tm,),
        in_specs=[pl.BlockSpec((2, tm, 256), lambda i: (0, i, 0)), pl.BlockSpec((tm, 256), lambda i: (i, 0)),
                  full(1, 256), pl.BlockSpec((NSHARD, 256, 128), lambda i: (0, 0, 0)), full(1, 256), full(1, 256)],
        out_specs=pl.BlockSpec((tm, 256), lambda i: (i, 0)),
        out_shape=_sds((N, 256), MX), name="s5_glu_fwd", compiler_params=_cp(("parallel",)))(y2, h, dsk, w4, bv, bg)


def _s5_glu_bwd(y2, h, dsk, w4, bv, bg, dya):
    tm = 512
    nt = N // tm

    def body(y2_ref, u_ref, d_ref, w4_ref, bv_ref, bg_ref, dya_ref,
             dyp_ref, dud_ref, dd_ref, dw4_ref, dbv_ref, dbg_ref, accv, accg):
        i = pl.program_id(0)

        @pl.when(i == 0)
        def _():
            for r in (dd_ref, accv, accg, dbv_ref, dbg_ref):
                r[...] = jnp.zeros_like(r)

        wv, wg = _glu_halves(w4_ref)
        u = u_ref[...]
        y = y2_ref[0] + y2_ref[1] + d_ref[...] * u
        z = _gelu(y)
        val = _mm(z, wv) + bv_ref[...]
        sig = jax.nn.sigmoid(_mm(z, wg) + bg_ref[...])
        dya = dya_ref[...]
        dval = dya * sig
        dgate = dya * val * sig * (1.0 - sig)
        dz = _mm_nt(dval, wv) + _mm_nt(dgate, wg)
        dy = dz * _gelu_grad(y)
        dyp_ref[...] = dy
        dud_ref[...] = (dy * d_ref[...]).astype(MX)
        dd_ref[...] += jnp.sum(dy * u, axis=0, keepdims=True)
        accv[...] += _mm_tn(z, dval)
        accg[...] += _mm_tn(z, dgate)
        dbv_ref[...] += jnp.sum(dval, axis=0, keepdims=True)
        dbg_ref[...] += jnp.sum(dgate, axis=0, keepdims=True)

        @pl.when(i == nt - 1)
        def _():
            dw4_ref[0] = accv[:, 0:128].astype(MX)
            dw4_ref[1] = accv[:, 128:256].astype(MX)
            dw4_ref[2] = accg[:, 0:128].astype(MX)
            dw4_ref[3] = accg[:, 128:256].astype(MX)

    full = lambda r, c: pl.BlockSpec((r, c), lambda i: (0, 0))
    row = pl.BlockSpec((tm, 256), lambda i: (i, 0))
    wspec = pl.BlockSpec((NSHARD, 256, 128), lambda i: (0, 0, 0))
    return pl.pallas_call(
        body, grid=(nt,),
        in_specs=[pl.BlockSpec((2, tm, 256), lambda i: (0, i, 0)), row, full(1, 256), wspec, full(1, 256), full(1, 256),
                  row],
        out_specs=[row, row, full(1, 256), wspec, full(1, 256), full(1, 256)],
        out_shape=[_sds((N, 256)), _sds((N, 256), MX), _sds((1, 256)), _sds((NSHARD, 256, 128), MX), _sds((1, 256)),
                   _sds((1, 256))],
        scratch_shapes=[pltpu.VMEM((256, 256), F32), pltpu.VMEM((256, 256), F32)],
        name="s5_glu_bwd", compiler_params=_cp(("arbitrary",)))(y2, h, dsk, w4, bv, bg, dya)


def _logsig(x):
    return jnp.minimum(x, 0.0) - jnp.log(1.0 + jnp.exp(-jnp.abs(x)))


def _gla_chunk(q, k, v, la, st, rev):
    c = GLA_CHUNK
    rows = q.shape[0]
    nch = rows // c
    b = _cums(la, rev)
    blc = [jnp.sum(la[i * c:(i + 1) * c], axis=0, keepdims=True) for i in range(nch)]
    bl = jnp.concatenate([jnp.broadcast_to(t, (c, 128)) for t in blc], axis=0)
    q_in = q * (32.0 ** -0.5) * jnp.exp(b)
    k_in = k * jnp.exp(-b)
    k_st = k * jnp.exp(bl - b)
    lane_k = lax.broadcasted_iota(jnp.int32, (1, 128), 1) // 32
    lane_v = lax.broadcasted_iota(jnp.int32, (1, 256), 1) // 64
    qs = jnp.concatenate([jnp.where(lane_k == hd, q_in, 0.0) for hd in range(4)], axis=0)
    a = _dmm_nt(qs, k_in)
    a = jnp.where(jnp.concatenate([_chunk_pairs(rows, rev, rev)] * 4, axis=0), a, 0.0)
    o4 = _dmm(a, v)
    o = jnp.zeros((rows, 256), F32)
    for hd in range(4):
        o = o + jnp.where(lane_v == hd, o4[hd * rows:(hd + 1) * rows], 0.0)
    bd = (lax.broadcasted_iota(jnp.int32, (256, 128), 0) // 64) == (lax.broadcasted_iota(jnp.int32, (256, 128), 1) // 32)
    inter = [None] * nch
    for i in (reversed(range(nch)) if rev else range(nch)):
        sl = slice(i * c, (i + 1) * c)
        inter[i] = _dmm_nt(q_in[sl], st)
        st = jnp.exp(blc[i]) * st + jnp.where(bd, _dmm_tn(v[sl], k_st[sl]), 0.0)
    return o + jnp.concatenate(inter, axis=0), st


def _gla_chunk_of(c, rev):
    return NGROUP - 1 - c if rev else c


def _gla_fwd(h, la2):
    c = GLA_GROUP * GLA_CHUNK

    def body(qf, kf, vf, laf, qb, kb, vb, lab, of_ref, ob_ref, sf_ref, sb_ref, stf, stb):
        @pl.when(pl.program_id(0) == 0)
        def _():
            stf[...] = jnp.zeros_like(stf)
            stb[...] = jnp.zeros_like(stb)

        ins = [(qf[s], kf[s], vf[s], laf[s], stf[s], qb[s], kb[s], vb[s], lab[s], stb[s]) for s in range(NSEQ)]
        outs = [(_gla_chunk(*t[:5], False), _gla_chunk(*t[5:], True)) for t in ins]
        for s in range(NSEQ):
            sf_ref[s, 0] = ins[s][4]
            sb_ref[s, 0] = ins[s][9]
            (of_ref[s], stf[s]), (ob_ref[s], stb[s]) = outs[s]

    def specs(rev):
        ch = lambda i: _gla_chunk_of(i, rev)
        return [pl.BlockSpec((NSEQ, c, 128), lambda i: (0, ch(i), 2)), pl.BlockSpec((NSEQ, c, 128), lambda i: (0, ch(i), 3)),
                pl.BlockSpec((NSEQ, c, 256), lambda i: (0, ch(i), 2)),
                pl.BlockSpec((NSEQ, c, 128), lambda i: (0, ch(i), 1 if rev else 0))]

    orow = lambda rev: pl.BlockSpec((NSEQ, c, 256), lambda i: (0, _gla_chunk_of(i, rev), 0))
    srow = lambda rev: pl.BlockSpec((NSEQ, 1, 256, 128), lambda i: (0, _gla_chunk_of(i, rev), 0, 0))
    h3, la3 = h.reshape(NSEQ, L, DINP), la2.reshape(NSEQ, L, 256)
    of, ob, sf, sb = pl.pallas_call(
        body, grid=(NGROUP,),
        in_specs=specs(False) + specs(True),
        out_specs=[orow(False), orow(True), srow(False), srow(True)],
        out_shape=[_sds((NSEQ, L, 256)), _sds((NSEQ, L, 256)), _sds((NSEQ, NGROUP, 256, 128)),
                   _sds((NSEQ, NGROUP, 256, 128))],
        scratch_shapes=[pltpu.VMEM((NSEQ, 256, 128), F32), pltpu.VMEM((NSEQ, 256, 128), F32)],
        name="gla_fwd", compiler_params=_cp(("arbitrary",)))(h3, h3, h3, la3, h3, h3, h3, la3)
    return of.reshape(N, 256), ob.reshape(N, 256), sf, sb


def _gla_bwd(h, la2, do, sf, sb):
    c = GLA_GROUP * GLA_CHUNK

    def body(qf, kf, vf, laf, dof, sfr, qb, kb, vb, lab, dob, sbr,
             dqf, dkf, dvf, dlf, dqb, dkb, dvb, dlb, dstf, dstb):
        @pl.when(pl.program_id(0) == 0)
        def _():
            dstf[...] = jnp.zeros_like(dstf)
            dstb[...] = jnp.zeros_like(dstb)

        def one(s, q, k, v, la, do_, st, dst, rev):
            _, vjp = jax.vjp(functools.partial(_gla_chunk, rev=rev), q[s], k[s], v[s], la[s], st[s, 0])
            return vjp((do_[s], dst[s]))

        res = [(one(s, qf, kf, vf, laf, dof, sfr, dstf, False), one(s, qb, kb, vb, lab, dob, sbr, dstb, True))
               for s in range(NSEQ)]
        for s in range(NSEQ):
            for (gq, gk, gv, gl, gs), (dq, dk, dv, dl, dst) in ((res[s][0], (dqf, dkf, dvf, dlf, dstf)),
                                                                  (res[s][1], (dqb, dkb, dvb, dlb, dstb))):
                dq[s], dk[s], dv[s] = gq.astype(MX), gk.astype(MX), gv.astype(MX)
                dl[s], dst[s] = gl, gs

    def specs(rev):
        ch = lambda i: _gla_chunk_of(i, not rev)
        return [pl.BlockSpec((NSEQ, c, 128), lambda i: (0, ch(i), 2)), pl.BlockSpec((NSEQ, c, 128), lambda i: (0, ch(i), 3)),
                pl.BlockSpec((NSEQ, c, 256), lambda i: (0, ch(i), 2)),
                pl.BlockSpec((NSEQ, c, 128), lambda i: (0, ch(i), 1 if rev else 0)),
                pl.BlockSpec((NSEQ, c, 256), lambda i: (0, ch(i), 0)),
                pl.BlockSpec((NSEQ, 1, 256, 128), lambda i: (0, ch(i), 0, 0))]

    def ospecs(rev):
        ch = lambda i: _gla_chunk_of(i, not rev)
        n = pl.BlockSpec((NSEQ, c, 128), lambda i: (0, ch(i), 0))
        return [n, n, pl.BlockSpec((NSEQ, c, 256), lambda i: (0, ch(i), 0)), n]

    oshape = [_sds((NSEQ, L, 128), MX), _sds((NSEQ, L, 128), MX), _sds((NSEQ, L, 256), MX), _sds((NSEQ, L, 128))]
    h3, la3, do3 = h.reshape(NSEQ, L, DINP), la2.reshape(NSEQ, L, 256), do.reshape(NSEQ, L, 256)
    res = pl.pallas_call(
        body, grid=(NGROUP,),
        in_specs=specs(False) + specs(True),
        out_specs=ospecs(False) + ospecs(True),
        out_shape=oshape + oshape,
        scratch_shapes=[pltpu.VMEM((NSEQ, 256, 128), F32), pltpu.VMEM((NSEQ, 256, 128), F32)],
        name="gla_bwd", compiler_params=_cp(("arbitrary",)))(h3, h3, h3, la3, do3, sf, h3, h3, h3, la3, do3, sb)
    return [r.reshape(N, r.shape[-1]) for r in res]


def _gla_post(of, ob, r, g):
    o = of + ob
    head = lax.broadcasted_iota(jnp.int32, (1, 256), 1) // 64
    mu = jnp.zeros_like(o)
    for hd in range(4):
        mu = mu + jnp.where(head == hd, jnp.sum(jnp.where(head == hd, o, 0.0), axis=-1, keepdims=True) * (1.0 / 64.0), 0.0)
    xc = o - mu
    var = jnp.zeros_like(o)
    for hd in range(4):
        var = var + jnp.where(head == hd, jnp.sum(jnp.where(head == hd, xc * xc, 0.0), axis=-1, keepdims=True) * (1.0 / 64.0), 0.0)
    return xc * lax.rsqrt(var + LN_EPS) * g * (r * jax.nn.sigmoid(r))


def _gla_post_fwd(of, ob, h, g):
    tm = 512

    def body(of_ref, ob_ref, r_ref, g_ref, y_ref):
        y_ref[...] = _gla_post(of_ref[...], ob_ref[...], r_ref[...], g_ref[...]).astype(MX)

    row = pl.BlockSpec((tm, 256), lambda i: (i, 0))
    return pl.pallas_call(
        body, grid=(N // tm,),
        in_specs=[row, row, pl.BlockSpec((tm, 256), lambda i: (i, 3)), pl.BlockSpec((1, 256), lambda i: (0, 0))],
        out_specs=row, out_shape=_sds((N, 256), MX), name="gla_post_fwd", compiler_params=_cp(("parallel",)))(of, ob, h, g)


def _gla_post_bwd(of, ob, h, g, dyb):
    tm = 512

    def body(of_ref, ob_ref, r_ref, g_ref, dy_ref, do_ref, dr_ref, dg_ref):
        @pl.when(pl.program_id(0) == 0)
        def _():
            dg_ref[...] = jnp.zeros_like(dg_ref)

        _, vjp = jax.vjp(_gla_post, of_ref[...], ob_ref[...], r_ref[...], g_ref[...])
        go, _, gr, gg = vjp(dy_ref[...])
        do_ref[...] = go
        dr_ref[...] = gr.astype(MX)
        dg_ref[...] += gg

    row = pl.BlockSpec((tm, 256), lambda i: (i, 0))
    one = pl.BlockSpec((1, 256), lambda i: (0, 0))
    return pl.pallas_call(
        body, grid=(N // tm,),
        in_specs=[row, row, pl.BlockSpec((tm, 256), lambda i: (i, 3)), one, row],
        out_specs=[row, row, one], out_shape=[_sds((N, 256)), _sds((N, 256), MX), _sds((1, 256))],
        name="gla_post_bwd", compiler_params=_cp(("arbitrary",)))(of, ob, h, g, dyb)


def _rope_tables(width):
    pos = jnp.arange(L, dtype=F32)
    inv_freq = ROPE_THETA ** (-jnp.arange(0, ROT, 2, dtype=F32) / ROT)
    ang = pos[:, None] * inv_freq[None, :]
    cos, sin = jnp.cos(ang), jnp.sin(ang)
    one = jnp.ones((L, 64 - ROT), F32)
    zero = jnp.zeros((L, 64 - ROT), F32)
    z8 = jnp.zeros((L, ROT // 2), F32)
    c = jnp.concatenate([cos, cos, one], axis=1)
    sa = jnp.concatenate([z8, sin, zero], axis=1)
    sb = jnp.concatenate([-sin, z8, zero], axis=1)
    rep = width // 64
    return jnp.stack([jnp.tile(c, (1, rep)), jnp.tile(sa, (1, rep)), jnp.tile(sb, (1, rep))])


def _pieces(t, f):
    out = [f(t[:, c * 128:(c + 1) * 128]) for c in range(t.shape[-1] // 128)]
    return out[0] if len(out) == 1 else jnp.concatenate(out, axis=1)


def _rope(t, tab):
    return _pieces(t, lambda x: x * tab[0] + pltpu.roll(x, ROT // 2, 1) * tab[1] + pltpu.roll(x, 128 - ROT // 2, 1) * tab[2])


def _rope_t(g, tab):
    return _pieces(g, lambda x: x * tab[0] + pltpu.roll(x * tab[1], 128 - ROT // 2, 1) + pltpu.roll(x * tab[2], ROT // 2, 1))


def _swa_pad_kv(kv_ref, tk_ref, kexp, vexp):
    z = jnp.zeros((SWA_BLK, 256), F32)
    kr = _rope(kv_ref[:, 0:128], tk_ref[...])
    for hk in range(2):
        for pad in (kexp, vexp):
            pad[hk, 0:SWA_BLK] = z
            pad[hk, SWA_BLK + L:] = z
        kexp[hk, SWA_BLK:SWA_BLK + L] = _swa_expand(kr, hk)
        vexp[hk, SWA_BLK:SWA_BLK + L] = _swa_expand(kv_ref[:, 128:256], hk)


def _swa_expand(x, hk):
    lane = lax.broadcasted_iota(jnp.int32, x.shape, 1)
    sw = pltpu.roll(x, 64, 1)
    pair = jnp.where(lane < 64, x, sw) if hk == 0 else jnp.where(lane < 64, sw, x)
    return jnp.concatenate([pair, pair], axis=1)


def _swa_fold(x, hk):
    a = x[:, 0:128] + x[:, 128:256]
    t = a + pltpu.roll(a, 64, 1)
    lane = lax.broadcasted_iota(jnp.int32, a.shape, 1)
    return jnp.where((lane < 64) if hk == 0 else (lane >= 64), t, 0.0)


def _swa_bias_tables(bias):
    i = lax.broadcasted_iota(jnp.int32, (SWA_BLK, 3 * SWA_BLK), 0)
    j = lax.broadcasted_iota(jnp.int32, (SWA_BLK, 3 * SWA_BLK), 1)
    band = (j - i >= 0) & (j - i <= 2 * SWA_BLK)
    for v, inside in enumerate((j >= SWA_BLK, True, j < 2 * SWA_BLK)):
        bias[v] = jnp.where(band & inside, 0.0, NEG_BIG)


def _swa_bias(bias, blk):
    return bias[jnp.where(blk == 0, 0, jnp.where(blk == NBLK - 1, 2, 1))]


def _swa_probs(q2, kexp, bias, sink_ref, hk):
    slot = lax.broadcasted_iota(jnp.int32, (1, 256), 1) // 64
    qs = jnp.concatenate([jnp.where(slot == g, q2, 0.0) for g in range(4)], axis=0)
    s = _mm_nt(qs, kexp) + jnp.concatenate([bias] * 4, axis=0)
    rowg = lax.broadcasted_iota(jnp.int32, (4 * SWA_BLK, 1), 0) // SWA_BLK
    sink = jnp.zeros((4 * SWA_BLK, 1), F32)
    for g in range(4):
        sink = jnp.where(rowg == g, sink_ref[hk * 4 + g], sink)
    m = jnp.maximum(jnp.max(s, axis=-1, keepdims=True), sink)
    p = jnp.exp(s - m)
    ps = jnp.exp(sink - m)
    inv = 1.0 / (jnp.sum(p, axis=-1, keepdims=True) + ps)
    return qs, p * inv, ps * inv, slot, rowg


def _swa_qtab(tk_ref, r0):
    return [tk_ref[i, pl.ds(r0, SWA_BLK), :] for i in range(3)]


def _swa_fwd(h, tk, sink):
    def body(sink_ref, q_ref, kv_ref, tk_ref, y_ref, kexp, vexp, bias):
        n = pl.program_id(1)

        @pl.when(n == 0)
        def _():
            _swa_pad_kv(kv_ref, tk_ref, kexp, vexp)
            _swa_bias_tables(bias)

        for t in range(SWA_PER):
            blk = n * SWA_PER + t
            rows = slice(t * SWA_BLK, (t + 1) * SWA_BLK)
            r0 = pl.multiple_of(blk * SWA_BLK, SWA_BLK)
            q = _rope(q_ref[rows, :], _swa_qtab(tk_ref, r0)) * 0.125
            for hk in range(2):
                _, p, _, slot, _ = _swa_probs(q[:, hk * 256:(hk + 1) * 256], kexp[hk, pl.ds(r0, 3 * SWA_BLK), :],
                                              _swa_bias(bias, blk), sink_ref, hk)
                o4 = _mm(p, vexp[hk, pl.ds(r0, 3 * SWA_BLK), :])
                o = jnp.zeros((SWA_BLK, 256), F32)
                for g in range(4):
                    o = o + jnp.where(slot == g, o4[g * SWA_BLK:(g + 1) * SWA_BLK], 0.0)
                y_ref[rows, hk * 256:(hk + 1) * 256] = o.astype(MX)

    tm = SWA_PER * SWA_BLK
    return pl.pallas_call(
        body,
        grid_spec=pltpu.PrefetchScalarGridSpec(
            num_scalar_prefetch=1, grid=(NSEQ, L // tm),
            in_specs=[pl.BlockSpec((tm, 512), lambda s, n, sk: (s * (L // tm) + n, 2)),
                      pl.BlockSpec((L, 256), lambda s, n, sk: (s, 6)),
                      pl.BlockSpec((3, L, 128), lambda s, n, sk: (0, 0, 0))],
            out_specs=pl.BlockSpec((tm, 512), lambda s, n, sk: (s * (L // tm) + n, 0)),
            scratch_shapes=[pltpu.VMEM((2, L + 2 * SWA_BLK, 256), F32), pltpu.VMEM((2, L + 2 * SWA_BLK, 256), F32),
                            pltpu.VMEM((3, SWA_BLK, 3 * SWA_BLK), F32)]),
        out_shape=_sds((N, 512), MX), name="swa_fwd", compiler_params=_cp(("arbitrary", "arbitrary")))(sink, h, h, tk)


def _swa_bwd(h, tk, sink, dyc):
    tm = SWA_PER * SWA_BLK

    def body(sink_ref, q_ref, kv_ref, tk_ref, dy_ref, dq_ref, dkv_ref, dsink_ref, kexp_all, vexp_all, dkacc, dvacc, bias):
        sq = pl.program_id(0)
        n = pl.program_id(1)

        @pl.when(n == 0)
        def _():
            _swa_pad_kv(kv_ref, tk_ref, kexp_all, vexp_all)
            _swa_bias_tables(bias)
            dkacc[...] = jnp.zeros_like(dkacc)
            dvacc[...] = jnp.zeros_like(dvacc)

        @pl.when((n == 0) & (sq == 0))
        def _():
            dsink_ref[...] = jnp.zeros_like(dsink_ref)

        hrow = lax.broadcasted_iota(jnp.int32, (8, 128), 0)
        dsk = jnp.zeros((8, 128), F32)
        for t in range(SWA_PER):
            blk = n * SWA_PER + t
            rows = slice(t * SWA_BLK, (t + 1) * SWA_BLK)
            r0 = pl.multiple_of(blk * SWA_BLK, SWA_BLK)
            tq = _swa_qtab(tk_ref, r0)
            q = _rope(q_ref[rows, :], tq) * 0.125
            band = _swa_bias(bias, blk)
            for hk in range(2):
                kexp = kexp_all[hk, pl.ds(r0, 3 * SWA_BLK), :]
                vexp = vexp_all[hk, pl.ds(r0, 3 * SWA_BLK), :]
                qs, p, ps, slot, rowg = _swa_probs(q[:, hk * 256:(hk + 1) * 256], kexp, band, sink_ref, hk)
                dy2 = dy_ref[rows, hk * 256:(hk + 1) * 256]
                dos = jnp.concatenate([jnp.where(slot == g, dy2, 0.0) for g in range(4)], axis=0)
                dp = _mm_nt(dos, vexp)
                delta = jnp.sum(p * dp, axis=-1, keepdims=True)
                ds = p * (dp - delta)
                dsr = -ps * delta
                for g in range(4):
                    dsk = dsk + jnp.where(hrow == hk * 4 + g,
                                          jnp.sum(jnp.where(rowg == g, dsr, 0.0), axis=0, keepdims=True), 0.0)
                dq4 = _mm(ds, kexp)
                dq2 = jnp.zeros((SWA_BLK, 256), F32)
                for g in range(4):
                    dq2 = dq2 + jnp.where(slot == g, dq4[g * SWA_BLK:(g + 1) * SWA_BLK], 0.0)
                dq_ref[rows, hk * 256:(hk + 1) * 256] = _rope_t(dq2 * 0.125, tq).astype(MX)
                dkacc[hk, pl.ds(r0, 3 * SWA_BLK), :] += _mm_tn(ds, qs)
                dvacc[hk, pl.ds(r0, 3 * SWA_BLK), :] += _mm_tn(p, dos)
        dsink_ref[...] += dsk

        @pl.when(n == L // tm - 1)
        def _():
            seq = slice(SWA_BLK, SWA_BLK + L)
            dk = _rope_t(_swa_fold(dkacc[0, seq], 0) + _swa_fold(dkacc[1, seq], 1), tk_ref[...])
            dkv_ref[:, 0:128] = dk.astype(MX)
            dkv_ref[:, 128:256] = (_swa_fold(dvacc[0, seq], 0) + _swa_fold(dvacc[1, seq], 1)).astype(MX)

    blk = lambda col: pl.BlockSpec((tm, 512), lambda s, n, sk: (s * (L // tm) + n, col))
    pad = pltpu.VMEM((2, L + 2 * SWA_BLK, 256), F32)
    return pl.pallas_call(
        body,
        grid_spec=pltpu.PrefetchScalarGridSpec(
            num_scalar_prefetch=1, grid=(NSEQ, L // tm),
            in_specs=[blk(2), pl.BlockSpec((L, 256), lambda s, n, sk: (s, 6)),
                      pl.BlockSpec((3, L, 128), lambda s, n, sk: (0, 0, 0)), blk(0)],
            out_specs=[blk(0), pl.BlockSpec((L, 256), lambda s, n, sk: (s, 0)),
                       pl.BlockSpec((8, 128), lambda s, n, sk: (0, 0))],
            scratch_shapes=[pad, pad, pad, pad, pltpu.VMEM((3, SWA_BLK, 3 * SWA_BLK), F32)]),
        out_shape=[_sds((N, 512), MX), _sds((N, 256), MX), _sds((8, 128))],
        name="swa_bwd", compiler_params=_cp(("arbitrary", "arbitrary")))(sink, h, h, tk, dyc)


def _outproj_bwd(dx1, s1, ya, yb, yc, wo, g):
    tm = 512
    nt = N // tm

    def body(dx1_ref, s_ref, ya_ref, yb_ref, yc_ref, wo_ref, g_ref,
             dya_ref, dyb_ref, dyc_ref, dxp_ref, dwo_ref, dg_ref, db_ref, acc):
        i = pl.program_id(0)

        @pl.when(i == 0)
        def _():
            acc[...] = jnp.zeros_like(acc)
            dg_ref[...] = jnp.zeros_like(dg_ref)
            db_ref[...] = jnp.zeros_like(db_ref)

        ds, dg, db = _ln_bwd(dx1_ref[...], s_ref[...], g_ref[...])
        dg_ref[...] += dg
        db_ref[...] += db
        dxp_ref[...] = ALPHA * ds
        dy = _mm_nt(ds, wo_ref[...])
        dya_ref[...] = dy[:, 0:256]
        dyb_ref[...] = dy[:, 256:512]
        dyc_ref[...] = dy[:, 512:1024]
        acc[0:256] += _mm_tn(ya_ref[...], ds)
        acc[256:512] += _mm_tn(yb_ref[...], ds)
        acc[512:1024] += _mm_tn(yc_ref[...], ds)

        @pl.when(i == nt - 1)
        def _():
            dwo_ref[...] = acc[...].astype(MX)

    row = lambda w_: pl.BlockSpec((tm, w_), lambda i: (i, 0))
    one = pl.BlockSpec((1, D), lambda i: (0, 0))
    full = pl.BlockSpec((D, D), lambda i: (0, 0))
    return pl.pallas_call(
        body, grid=(nt,),
        in_specs=[row(D), row(D), row(256), row(256), row(512), full, one],
        out_specs=[row(256), row(256), row(512), row(D), full, one, one],
        out_shape=[_sds((N, 256)), _sds((N, 256)), _sds((N, 512)), _sds((N, D)), _sds((D, D), MX), _sds((1, D)), _sds((1, D))],
        scratch_shapes=[pltpu.VMEM((D, D), F32)],
        name="outproj_bwd", compiler_params=_cp(("arbitrary",)))(dx1, s1, ya, yb, yc, wo, g)


def _mix_ffn_fwd(ya, yb, yc, x, wo, g1, b1, w1, w2, g, b, target=None):
    tm = FFN_TM
    head = target is not None

    def body(*refs):
        ya_ref, yb_ref, yc_ref, xin_ref, wo_ref, g1_ref, b1_ref, w1_ref, w2_ref, g_ref, b_ref = refs[:11]
        s1_ref, x1_ref, a_ref, s_ref, y_ref = refs[11 + head:16 + head]
        mix = _mm(ya_ref[...], wo_ref[0:256]) + _mm(yb_ref[...], wo_ref[256:512]) + _mm(yc_ref[...], wo_ref[512:1024])
        s1 = ALPHA * xin_ref[...] + mix
        s1_ref[...] = s1
        x = _ln_fwd(s1, g1_ref[...], b1_ref[...])
        x1_ref[...] = x
        xb = x.astype(MX)
        s = ALPHA * x
        for j in range(NSHARD):
            a = _mm(xb, w1_ref[j])
            a_ref[:, j * D:(j + 1) * D] = a.astype(MX)
            s = s + _mm(jnp.square(jnp.maximum(a, 0.0)), w2_ref[j])
        s_ref[...] = s
        x2 = _ln_fwd(s, g_ref[...], b_ref[...])
        if not head:
            y_ref[...] = x2
            return
        l_ref = refs[16 + head]

        @pl.when(pl.program_id(0) == 0)
        def _():
            l_ref[...] = jnp.zeros_like(l_ref)

        e = x2 - refs[11][...]
        y_ref[...] = e * (1.0 / D)
        l_ref[...] += jnp.sum(jnp.sum(e * e, axis=1, keepdims=True), axis=0, keepdims=True) * (0.5 / D)

    rw = lambda w_: pl.BlockSpec((tm, w_), lambda i: (i, 0))
    row = rw(D)
    once = dict(pipeline_mode=pl.Buffered(1))
    wall = pl.BlockSpec((NSHARD, D, D), lambda i: (0, 0, 0), **once)
    one = pl.BlockSpec((1, D), lambda i: (0, 0))
    acc = pl.BlockSpec((8, 128), lambda i: (0, 0))
    return pl.pallas_call(
        body, grid=(N // tm,),
        in_specs=[rw(256), rw(256), rw(512), row, pl.BlockSpec((D, D), lambda i: (0, 0), **once), one, one,
                  wall, wall, one, one] + [row] * head,
        out_specs=[row, row, pl.BlockSpec((tm, DFF), lambda i: (i, 0)), row, row] + [acc] * head,
        out_shape=[_sds((N, D)), _sds((N, D)), _sds((N, DFF), MX), _sds((N, D)), _sds((N, D))] + [_sds((8, 128))] * head,
        name="mix_ffn_fwd", compiler_params=_cp(("arbitrary",), FFN_VMEM))(
            ya, yb, yc, x, wo, g1, b1, w1, w2, g, b, *([target] * head))


def _ffn_bwd_act(dy, s2, a, w1, w2, g):
    tm = FFN_TM

    def body(dy_ref, s_ref, a_ref, w1_ref, w2_ref, g_ref, da_ref, ds_ref, dx1_ref, dg_ref, db_ref):
        @pl.when(pl.program_id(0) == 0)
        def _():
            dg_ref[...] = jnp.zeros_like(dg_ref)
            db_ref[...] = jnp.zeros_like(db_ref)

        ds, dg, db = _ln_bwd(dy_ref[...], s_ref[...], g_ref[...])
        dsb = ds.astype(MX)
        ds_ref[...] = dsb
        dg_ref[...] += dg
        db_ref[...] += db
        dx1 = ALPHA * ds
        for j in range(NSHARD):
            da = (_mm_nt(dsb, w2_ref[j]) * 2.0 * jnp.maximum(a_ref[:, j * D:(j + 1) * D].astype(F32), 0.0)).astype(MX)
            da_ref[:, j * D:(j + 1) * D] = da
            dx1 = dx1 + _mm_nt(da, w1_ref[j])
        dx1_ref[...] = dx1

    row = pl.BlockSpec((tm, D), lambda i: (i, 0))
    wide = pl.BlockSpec((tm, DFF), lambda i: (i, 0))
    wall = pl.BlockSpec((NSHARD, D, D), lambda i: (0, 0, 0))
    one = pl.BlockSpec((1, D), lambda i: (0, 0))
    return pl.pallas_call(
        body, grid=(N // tm,),
        in_specs=[row, row, wide, wall, wall, one],
        out_specs=[wide, row, row, one, one],
        out_shape=[_sds((N, DFF), MX), _sds((N, D), MX), _sds((N, D)), _sds((1, D)), _sds((1, D))],
        name="ffn_bwd_act", compiler_params=_cp(("arbitrary",), FFN_VMEM))(dy, s2, a, w1, w2, g)


def _ffn_bwd_w(x1, da, a, ds):
    tm, nb = FFN_TM_W, FFN_WB
    nt = N // tm

    def body(x_ref, da_ref, a_ref, ds_ref, dw1_ref, dw2_ref, acc1, acc2):
        i = pl.program_id(1)

        @pl.when(i == 0)
        def _():
            acc1[...] = jnp.zeros_like(acc1)
            acc2[...] = jnp.zeros_like(acc2)

        x, ds_ = x_ref[...], ds_ref[...]
        for k in range(nb):
            cols = slice(k * D, (k + 1) * D)
            acc1[k] += _mm_tn(x, da_ref[:, cols])
            acc2[k] += _mm_tn(jnp.square(jnp.maximum(a_ref[:, cols].astype(F32), 0.0)), ds_)

        @pl.when(i == nt - 1)
        def _():
            dw1_ref[...] = acc1[...].astype(MX)
            dw2_ref[...] = acc2[...].astype(MX)

    row = pl.BlockSpec((tm, D), lambda j, i: (i, 0))
    col = pl.BlockSpec((tm, nb * D), lambda j, i: (i, j))
    wj = pl.BlockSpec((nb, D, D), lambda j, i: (j, 0, 0))
    return pl.pallas_call(
        body, grid=(NSHARD // nb, nt),
        in_specs=[row, col, col, row], out_specs=[wj, wj],
        out_shape=[_sds((NSHARD, D, D), MX), _sds((NSHARD, D, D), MX)],
        scratch_shapes=[pltpu.VMEM((nb, D, D), F32), pltpu.VMEM((nb, D, D), F32)],
        name="ffn_bwd_w", compiler_params=_cp(("parallel", "arbitrary"), FFN_VMEM))(x1, da, a, ds)


def _s5_discretize(a_re, a_im, log_step, b_re, b_im):
    lam = lax.complex(a_re, a_im)
    lam_bar = jnp.exp(lam * jnp.exp(log_step))
    b_bar = ((lam_bar - 1.0) / lam)[..., None] * lax.complex(b_re, b_im)
    return jnp.real(lam_bar), jnp.imag(lam_bar), jnp.real(b_bar), jnp.imag(b_bar)


def _s5_in_blocks(b):
    e = jnp.eye(8, dtype=F32)
    return jnp.einsum('ij,zbjph->zbihjp', e, b.reshape(2, 2, 8, S5_P, S5_H)).reshape(2, 2, 128, SW)


def _s5_out_blocks(c):
    e = jnp.eye(8, dtype=F32)
    return jnp.einsum('ij,zbjhp->zbjpih', e, c.reshape(2, 2, 8, S5_H, S5_P)).reshape(2, 2, SW, 128)


def _gate_weight(w_a):
    z = jnp.zeros((16, 128), F32)
    top = jnp.concatenate([w_a[0], z], axis=1)
    bot = jnp.concatenate([z, w_a[1]], axis=1)
    return jnp.concatenate([top, bot, jnp.zeros((96, 256), F32)], axis=0)


def _layer_prep(p):
    lr, li, br, bi = _s5_discretize(p["s5_a_re"], p["s5_a_im"], p["s5_log_step"], p["s5_b_re"], p["s5_b_im"])
    q = dict(p)
    q["bre"] = _s5_in_blocks(br).astype(MX)
    q["bim"] = _s5_in_blocks(bi).astype(MX)
    q["cre"] = _s5_out_blocks(p["s5_c_re"]).astype(MX)
    q["cim"] = _s5_out_blocks(p["s5_c_im"]).astype(MX)
    mr, mi = lr.reshape(2, 1024), li.reshape(2, 1024)
    q["tab"], q["tabc"] = _lockstep_tables(mr, mi)
    q["dsk"] = p["s5_d"].reshape(1, 256)
    q["wa"] = _gate_weight(p["gla_w_a"]).astype(MX)
    q["ba"] = p["gla_b_a"].reshape(1, 256)
    q["lng"] = p["gla_ln_g"].reshape(1, 256)
    q["bv"] = p["s5_b_glu"][:256].reshape(1, 256)
    q["bg"] = p["s5_b_glu"][256:].reshape(1, 256)
    for k in ("ln1_g", "ln1_b", "ln2_g", "ln2_b"):
        q[k] = p[k].reshape(1, D)
    return q


def _layer_fwd(x, q, tk, fetch, target=None):
    q["w_in"] = fetch("w_in", x)
    h, la2 = _inproj_fwd(x, q["w_in"], q["wa"], q["ba"])
    hre, him, y2 = _s5_fwd(h, q["bre"], q["bim"], q["cre"], q["cim"], q["tab"])
    q["w4"] = fetch("s5_w_glu", y2)
    ya = _s5_glu_fwd(y2, h, q["dsk"], q["w4"], q["bv"], q["bg"])
    of, ob, sf, sb = _gla_fwd(h, la2)
    yb = _gla_post_fwd(of, ob, h, q["lng"])
    yc = _swa_fwd(h, tk, q["swa_sink"])
    mixed = ya[:8, :128] + yb[:8, :128] + yc[:8, :128]
    q["w_out"] = fetch("w_out", mixed)
    q["w_ff1"] = fetch("w_ff1", mixed)
    q["w_ff2"] = fetch("w_ff2", mixed)
    s1, x1, a, s2, *out = _mix_ffn_fwd(ya, yb, yc, x, q["w_out"], q["ln1_g"], q["ln1_b"], q["w_ff1"], q["w_ff2"],
                                       q["ln2_g"], q["ln2_b"], target)
    saved = dict(x=x, h=h, hre=hre, him=him, y2=y2, ya=ya, la2=la2, of=of, ob=ob, sf=sf, sb=sb, yb=yb, yc=yc,
                 s1=s1, x1=x1, a=a, s2=s2)
    return (out[0] if target is None else tuple(out)), saved


def _layer_bwd(dy, q, sv, tk, emit):
    g = {}
    da, ds2, dx1, g["dg2"], g["db2"] = _ffn_bwd_act(dy, sv["s2"], sv["a"], q["w_ff1"], q["w_ff2"], q["ln2_g"])
    dw1, dw2 = _ffn_bwd_w(sv["x1"], da, sv["a"], ds2)
    tie = emit(dict(w_ff1=dw1, w_ff2=dw2))
    dya, dyb, dyc, dxp, dwo, g["dg1"], g["db1"] = _outproj_bwd(dx1, sv["s1"], sv["ya"], sv["yb"], sv["yc"],
                                                               q["w_out"], q["ln1_g"] + tie)
    h = sv["h"]
    daq, dakv, g["dsink"] = _swa_bwd(h, tk, q["swa_sink"], dyc)
    do, gr, g["dlng"] = _gla_post_bwd(sv["of"], sv["ob"], h, q["lng"], dyb)
    gq_f, gk_f, gv_f, gl_f, gq_b, gk_b, gv_b, gl_b = _gla_bwd(h, sv["la2"], do, sv["sf"], sv["sb"])
    dyp, dud, g["dd"], dw4, g["dbv"], g["dbg"] = _s5_glu_bwd(sv["y2"], h, q["dsk"], q["w4"], q["bv"], q["bg"], dya)
    tie = emit(dict(w_out=dwo.reshape(NSHARD, D // NSHARD, D), s5_w_glu=dw4))
    du2, g["dbre"], g["dbim"], g["dcre"], g["dcim"], g["dmu"] = _s5_bwd(
        h, dyp, sv["hre"], sv["him"], q["bre"], q["bim"], q["cre"], q["cim"], (q["tabc"][0], q["tabc"][1] + tie))
    dx, dwt, g["dwa"], g["dba"] = _inproj_bwd(sv["x"], q["w_in"], dxp, du2, dud, gq_f, gq_b, gk_f, gk_b, gv_f, gv_b, gr,
                                              daq, dakv, h, q["wa"], q["ba"], gl_f, gl_b)
    tie = emit(dict(w_in=dwt))
    return dx, g, tie


NATIVE = ("dmu", "dbre", "dbim", "dcre", "dcim", "dd", "dbv", "dbg", "dwa", "dba", "dlng", "dsink",
          "dg1", "db1", "dg2", "db2", "loss")
ICI_CORE = (0, 0, 0, 1, 1, 0, 0, 0, 1, 1, 1, 1, 0, 0, 1, 1, 0)
Y_FIRST = (0, 0, 1, 0, 1, 1, 0, 1, 0, 1, 0, 1, 0, 1, 0, 1, 0)


def _finish_small(n, w):
    g = {}
    dmu = n["dmu"]
    dlr = dmu[:, :, :, 0].reshape(DEPTH, 2, S5_G, S5_P)
    dli = dmu[:, :, :, 1].reshape(DEPTH, 2, S5_G, S5_P)

    def unblock(c, perm, shape):
        return c.reshape(DEPTH, 2, 2, S5_H, 8, S5_P).transpose(perm).reshape(shape)

    b_shape, c_shape = (DEPTH, 2, S5_G, S5_P, S5_H), (DEPTH, 2, S5_G, S5_H, S5_P)
    _, vjp = jax.vjp(_s5_discretize, w["s5_a_re"], w["s5_a_im"], w["s5_log_step"], w["s5_b_re"], w["s5_b_im"])
    (g["s5_a_re"], g["s5_a_im"], g["s5_log_step"], g["s5_b_re"], g["s5_b_im"]) = vjp(
        (dlr, dli, unblock(n["dbre"], (0, 1, 2, 4, 5, 3), b_shape), unblock(n["dbim"], (0, 1, 2, 4, 5, 3), b_shape)))
    g["s5_c_re"] = unblock(n["dcre"], (0, 1, 2, 4, 3, 5), c_shape)
    g["s5_c_im"] = unblock(n["dcim"], (0, 1, 2, 4, 3, 5), c_shape)
    g["s5_d"] = n["dd"].reshape(DEPTH, S5_G, S5_H)
    g["s5_b_glu"] = jnp.concatenate([n["dbv"], n["dbg"]], axis=2).reshape(DEPTH, 512)
    g["gla_w_a"] = jnp.stack([n["dwa"][:, 0:16, 0:128], n["dwa"][:, 16:32, 128:256]], axis=1)
    g["gla_b_a"] = n["dba"].reshape(DEPTH, 2, 128)
    g["gla_ln_g"] = n["dlng"].reshape(DEPTH, 256)
    g["swa_sink"] = n["dsink"][:, :, 0]
    for k, s in (("ln1_g", "dg1"), ("ln1_b", "db1"), ("ln2_g", "dg2"), ("ln2_b", "db2")):
        g[k] = n[s].reshape(DEPTH, D)
    return g


def _local_step(x, target, qs, tk, fetch, emit):
    saved = []
    for l, q in enumerate(qs):
        x, sv = _layer_fwd(x, q, tk, functools.partial(fetch, l), target if l == DEPTH - 1 else None)
        saved.append(sv)
    dy, lacc = x
    smalls = [None] * DEPTH
    tie = 0.0
    for l in reversed(range(DEPTH)):
        qs[l]["ln2_g"] = qs[l]["ln2_g"] + tie
        dy, smalls[l], tie = _layer_bwd(dy, qs[l], saved[l], tk, functools.partial(emit, l))
    smalls[0]["db2"] = smalls[0]["db2"] + tie
    for l in range(DEPTH):
        smalls[l]["loss"] = lacc if l == 0 else jnp.zeros_like(lacc)
    return lacc[0, 0], dy, smalls


BIG = ("w_in", "s5_w_glu", "w_out", "w_ff1", "w_ff2")
SMALL = ("s5_a_re", "s5_a_im", "s5_log_step", "s5_b_re", "s5_b_im", "s5_c_re", "s5_c_im", "s5_d", "s5_b_glu",
         "gla_w_a", "gla_b_a", "gla_ln_g", "swa_sink", "ln1_g", "ln1_b", "ln2_g", "ln2_b")
ANY = pl.BlockSpec(memory_space=pl.ANY)


def _place():
    x, y, c = lax.axis_index("x"), lax.axis_index("y"), lax.axis_index("c")
    return x, y, c, [(1 - x, y), (x, 1 - y), (1 - x, 1 - y)]


HBM = pl.BlockSpec(memory_space=pltpu.HBM)
SEMS = pl.BlockSpec(memory_space=pltpu.SEMAPHORE)
EFFECT = pltpu.SideEffectType.DATAFLOW_SIDE_EFFECTING


def _push_copies(ins, lands, send, recv, gather, sending):
    x, y, c, chips = _place()
    me = 2 * x + y
    if gather == "sibling":
        return [pltpu.make_async_remote_copy(src_ref=ins[a], dst_ref=lands[a], send_sem=send.at[a], recv_sem=recv.at[a],
                                             device_id=(x, y, 1 - c), device_id_type=MESH) for a in range(len(lands))]
    out = []
    for a in range(len(lands)):
        for j, (px, py) in enumerate(chips):
            peer = 2 * px + py
            src = lands[a].at[me] if gather else ins[a].at[peer if sending else me]
            dst = lands[a].at[me if sending else peer]
            out.append(pltpu.make_async_remote_copy(src_ref=src, dst_ref=dst, send_sem=send.at[3 * a + j],
                                                    recv_sem=recv.at[3 * a + j], device_id=(px, py, c),
                                                    device_id_type=MESH))
    return out


def _push_start(name, arrs, gather):
    n = len(arrs)
    ops = list(arrs) if gather is True else list(arrs) + [lax.empty(s.shape, s.dtype) for s in arrs]
    m = len(ops)

    def body(*refs):
        ins, lnd = (refs[:n], refs[:n]) if gather is True else (refs[:n], refs[n:m])
        for cp in _push_copies(ins, lnd, refs[m], refs[m + 1], gather, True):
            cp.start()
        refs[-1][...] = jnp.zeros((8, 128), F32)

    ops = [pltpu.with_memory_space_constraint(t, pltpu.HBM) for t in ops]
    res = pl.pallas_call(
        body, name=name,
        out_shape=(pltpu.SemaphoreType.DMA((3 * n,)), pltpu.SemaphoreType.DMA((3 * n,)),
                   *[pltpu.HBM(t.shape, t.dtype) for t in ops], _sds((8, 128))),
        in_specs=[HBM] * m,
        out_specs=(SEMS, SEMS, *[HBM] * m, pl.BlockSpec(memory_space=pltpu.VMEM)),
        input_output_aliases={i: 2 + i for i in range(m)},
        compiler_params=pltpu.CompilerParams(has_side_effects=EFFECT))(*ops)
    return res[0], res[1], list(res[2:2 + m]), res[-1]


def _push_wait(name, started, after, gather):
    send, recv, ops, _ = started
    m = len(ops)
    n = m if gather is True else m // 2

    def body(*refs):
        ins, lnd = (refs[:n], refs[:n]) if gather is True else (refs[:n], refs[n:m])
        for cp in _push_copies(ins, lnd, refs[m], refs[m + 1], gather, False):
            cp.wait_send()
            cp.wait_recv()

    res = pl.pallas_call(
        body, name=name,
        out_shape=[pltpu.HBM(t.shape, t.dtype) for t in ops],
        in_specs=[HBM] * m + [SEMS, SEMS, ANY], out_specs=[HBM] * m,
        input_output_aliases={i: i for i in range(m)},
        compiler_params=pltpu.CompilerParams(has_side_effects=EFFECT))(*ops, send, recv, after)
    return list(res)


def _row_tile(rows):
    return max(t for t in range(8, min(rows, 512) + 1, 8) if rows % t == 0)


def _cast_to_slot(me, w, l):
    _, rows, cols = w.shape
    tr = _row_tile(rows)

    def body(me_ref, w_ref, o_ref):
        o_ref[0] = w_ref[0].astype(MX)

    return pl.pallas_call(
        body,
        grid_spec=pltpu.PrefetchScalarGridSpec(
            num_scalar_prefetch=1, grid=(rows // tr,),
            in_specs=[pl.BlockSpec((1, tr, cols), lambda i, me_: (l, i, 0))],
            out_specs=pl.BlockSpec((1, tr, cols), lambda i, me_: (me_[0], i, 0))),
        out_shape=_sds((NSHARD, rows, cols), MX), name="cast_to_slot", compiler_params=_cp(("arbitrary",)))(me, w)


def _sum_sources(me, recv, own):
    _, rows, cols = recv[0].shape
    tr = min(_row_tile(rows), 256) if rows % 256 == 0 else _row_tile(rows)
    nt = rows // tr

    def body(me_ref, *refs):
        o_ref = refs[-1]
        for l in range(DEPTH):
            @pl.when(pl.program_id(0) == l)
            def _():
                r_ref, own_ref = refs[2 * l], refs[2 * l + 1]
                part = [jnp.where(me_ref[0] == s, own_ref[0], r_ref[s]).astype(F32) for s in range(NSHARD)]
                o_ref[...] = ((part[0] + part[1]) + part[2]) + part[3]

    in_specs = []
    for l in range(DEPTH):
        pick = lambda g, i, me_, l=l: jnp.where(g == l, i, jnp.where(g < l, 0, nt - 1))
        in_specs += [pl.BlockSpec((NSHARD, tr, cols), lambda g, i, me_, pick=pick: (0, pick(g, i, me_), 0)),
                     pl.BlockSpec((1, tr, cols), lambda g, i, me_, pick=pick: (me_[0], pick(g, i, me_), 0))]
    return pl.pallas_call(
        body,
        grid_spec=pltpu.PrefetchScalarGridSpec(
            num_scalar_prefetch=1, grid=(DEPTH, nt), in_specs=in_specs,
            out_specs=pl.BlockSpec((tr, cols), lambda g, i, me_: (g * nt + i, 0))),
        out_shape=_sds((DEPTH * rows, cols)), name="sum_sources",
        compiler_params=_cp(("arbitrary", "arbitrary")))(me, *[t for l in range(DEPTH) for t in (recv[l], own[l])])


def _allreduce_small(per_layer):
    nk = len(per_layer[0])
    n = DEPTH * nk
    shapes = [a.shape for a in per_layer[0]]

    def body(*refs):
        ins, outs = refs[:n], refs[n:n + nk]
        sibs, slots = refs[n + nk:n + 2 * nk], refs[n + 2 * nk:n + 3 * nk]
        send, recv = refs[n + 3 * nk:]
        x, y, c, chips = _place()
        me = 2 * x + y
        d2d = [pltpu.make_async_remote_copy(src_ref=ins[l * nk + k], dst_ref=sibs[k].at[l], send_sem=send.at[l * nk + k],
                                            recv_sem=recv.at[l * nk + k], device_id=(x, y, 1 - c), device_id_type=MESH)
               for l in range(DEPTH) for k in range(nk)]
        for cp in d2d:
            cp.start()
        for cp in d2d:
            cp.wait()
        for l in range(DEPTH):
            for k in range(nk):
                slots[k][0, l] = ins[l * nk + k][...] + sibs[k][l]

        def swap(k, stage):
            peer = (1 - x, y, c) if stage == Y_FIRST[k] else (x, 1 - y, c)
            return pltpu.make_async_remote_copy(src_ref=slots[k].at[2 * stage], dst_ref=slots[k].at[2 * stage + 1],
                                                send_sem=send.at[n + 3 * k + stage], recv_sem=recv.at[n + 3 * k + stage],
                                                device_id=peer, device_id_type=MESH)

        def handover(k):
            return pltpu.make_async_remote_copy(src_ref=outs[k], dst_ref=outs[k], send_sem=send.at[n + 3 * nk + k],
                                                recv_sem=recv.at[n + 3 * nk + k], device_id=(x, y, 1 - c),
                                                device_id_type=MESH)

        halves = (tuple(k for k in range(nk) if ICI_CORE[k] == 0), tuple(k for k in range(nk) if ICI_CORE[k] == 1))
        for cc in range(2):
            @pl.when(c == cc)
            def _():
                mine, theirs = halves[cc], halves[1 - cc]
                for stage in range(2):
                    cps = [swap(k, stage) for k in mine]
                    for cp in cps:
                        cp.start()
                    for cp in cps:
                        cp.wait()
                    for k in mine:
                        if stage == 0:
                            slots[k][2] = slots[k][0] + slots[k][1]
                        else:
                            outs[k][...] = slots[k][2] + slots[k][3]
                over = [handover(k) for k in mine]
                for cp in over:
                    cp.start()
                for k in theirs:
                    handover(k).wait_recv()
                for cp in over:
                    cp.wait_send()

    vm = pl.BlockSpec(memory_space=pltpu.VMEM)
    return pl.pallas_call(
        body, in_specs=[vm] * n, out_specs=[vm] * nk, out_shape=[_sds((DEPTH,) + s) for s in shapes],
        scratch_shapes=([pltpu.VMEM((DEPTH,) + s, F32) for s in shapes]
                        + [pltpu.VMEM((NSHARD, DEPTH) + s, F32) for s in shapes]
                        + [pltpu.SemaphoreType.DMA((n + 4 * nk,)), pltpu.SemaphoreType.DMA((n + 4 * nk,))]),
        name="allreduce_small", compiler_params=pltpu.CompilerParams(vmem_limit_bytes=VMEM_LIMIT))(
            *[a for layer in per_layer for a in layer])


def _adamw_math(w, g, m, v):
    m = ADAM_B1 * m + (1.0 - ADAM_B1) * g
    v = ADAM_B2 * v + (1.0 - ADAM_B2) * jnp.square(g)
    m_hat = m / (1.0 - ADAM_B1 ** ADAM_STEP)
    v_hat = v / (1.0 - ADAM_B2 ** ADAM_STEP)
    delta = -ADAM_LR * (m_hat / (jnp.sqrt(v_hat) + ADAM_EPS) + ADAM_WD * w)
    return delta, m, v


def _adamw(g_parts, w, m, v):
    rows, cols = w.shape
    tr = 256 if rows % 256 == 0 else _row_tile(rows)
    k = len(g_parts)

    def body(*refs):
        g = refs[0][...]
        for r in refs[1:k]:
            g = g + r[...]
        w_ref, m_ref, v_ref, go, do, mo, vo = refs[k:]
        d, mn, vn = _adamw_math(w_ref[...], g, m_ref[...], v_ref[...])
        go[...] = g
        do[...] = d
        mo[...] = mn
        vo[...] = vn

    spec = pl.BlockSpec((tr, cols), lambda i: (i, 0))
    return pl.pallas_call(
        body, grid=(rows // tr,), in_specs=[spec] * (k + 3), out_specs=[spec] * 4,
        out_shape=[_sds((rows, cols))] * 4, name="adamw", compiler_params=_cp(("parallel",)))(*g_parts, w, m, v)


def _adamw_small(gs, ws, ms, vs):
    n = len(gs)

    def body(*refs):
        for k in range(n):
            d, mn, vn = _adamw_math(refs[n + k][...], refs[k][...], refs[2 * n + k][...], refs[3 * n + k][...])
            refs[4 * n + k][...] = d
            refs[5 * n + k][...] = mn
            refs[6 * n + k][...] = vn

    vm = pl.BlockSpec(memory_space=pltpu.VMEM)
    shapes = [_sds(a.shape) for a in ws]
    res = pl.pallas_call(
        body, in_specs=[vm] * (4 * n), out_specs=[vm] * (3 * n), out_shape=shapes * 3, name="adamw_small",
        compiler_params=pltpu.CompilerParams(vmem_limit_bytes=VMEM_LIMIT))(*gs, *ws, *ms, *vs)
    return res[:n], res[n:2 * n], res[2 * n:]


_ARGS = ("x", "w_in", "s5_a_re", "s5_a_im", "s5_log_step", "s5_b_re", "s5_b_im", "s5_c_re", "s5_c_im", "s5_d",
         "s5_w_glu", "s5_b_glu", "gla_w_a", "gla_b_a", "gla_ln_g", "swa_sink", "w_out", "ln1_g", "ln1_b", "w_ff1",
         "w_ff2", "ln2_g", "ln2_b")
_WEIGHTS = _ARGS[1:]


def kernel(x, w_in, s5_a_re, s5_a_im, s5_log_step, s5_b_re, s5_b_im, s5_c_re, s5_c_im, s5_d, s5_w_glu, s5_b_glu, gla_w_a, gla_b_a, gla_ln_g, swa_sink, w_out, ln1_g, ln1_b, w_ff1, w_ff2, ln2_g, ln2_b, loss_target, m_w_in, m_s5_a_re, m_s5_a_im, m_s5_log_step, m_s5_b_re, m_s5_b_im, m_s5_c_re, m_s5_c_im, m_s5_d, m_s5_w_glu, m_s5_b_glu, m_gla_w_a, m_gla_b_a, m_gla_ln_g, m_swa_sink, m_w_out, m_ln1_g, m_ln1_b, m_w_ff1, m_w_ff2, m_ln2_g, m_ln2_b, v_w_in, v_s5_a_re, v_s5_a_im, v_s5_log_step, v_s5_b_re, v_s5_b_im, v_s5_c_re, v_s5_c_im, v_s5_d, v_s5_w_glu, v_s5_b_glu, v_gla_w_a, v_gla_b_a, v_gla_ln_g, v_swa_sink, v_w_out, v_ln1_g, v_ln1_b, v_w_ff1, v_w_ff2, v_ln2_g, v_ln2_b):
    given = dict(locals())
    w = {k: given[k] for k in _WEIGHTS}
    mom = {k: given["m_" + k] for k in _WEIGHTS}
    var = {k: given["v_" + k] for k in _WEIGHTS}

    me = (2 * lax.axis_index("x") + lax.axis_index("y")).astype(jnp.int32).reshape(1)
    tr = lambda t: t.transpose(0, 2, 1)
    shard = {k: (tr(w[k]) if k == "w_in" else w[k]) for k in BIG}
    qs = [None] * DEPTH

    first = ("w_in", "s5_w_glu", "w_out")
    follow = {(0, "w_in"): [(0, first[1:]), (0, BIG[3:])], (0, "s5_w_glu"): [(1, first)], (0, "w_ff1"): [(1, BIG[3:])]}
    gathers = {}

    casts = {}

    def start_gather(l, names, behind=None):
        lands = [casts.pop((l, k)) if (l, k) in casts else _cast_to_slot(me, shard[k], l) for k in names]
        if behind is not None:
            lands, behind = lax.optimization_barrier((lands, behind))
        st = _push_start(f"gather_start_{l}_{names[0]}", lands, True)
        for k in names:
            gathers[l, k] = [names, st, None]
        return st[-1], behind

    token = start_gather(0, first[:1])[0]
    zero = token[0, 0]
    for l in range(DEPTH):
        for k in BIG:
            if (l, k) not in gathers:
                casts[l, k] = _cast_to_slot(me, lax.optimization_barrier((shard[k], token))[0], l)
        qs[l] = _layer_prep({k: (w[k][l] + zero if k == "s5_a_re" else w[k][l]) for k in SMALL})
    token, casts, qs = lax.optimization_barrier((token, casts, qs))

    def fetch(l, name, after):
        names, st, got = gathers[l, name]
        tie = None
        if got is None:
            if l == 0 and name == "w_in":
                after = token
            lands = _push_wait(f"gather_wait_{l}_{names[0]}", st, after, True)
            for l2, names2 in follow.get((l, name), ()):
                tok, lands[0] = start_gather(l2, names2, lands[0])
                tie = tok if tie is None else tie + tok
            got = dict(zip(names, lands))
            for k in names:
                gathers[l, k][2] = got
        full = got[name]
        if name == "w_in":
            return _in_rows(full, token if tie is None else tie)
        if tie is not None:
            near = "bv" if name == "s5_w_glu" else "ln2_b"
            qs[l][near] = qs[l][near] + tie[0, 0]
        return full.reshape(D, D) if name == "w_out" else full

    scatters, held = [], {}

    def emit(l, grads):
        if l > 0:
            held.update(grads)
            if "w_in" not in grads:
                return 0.0
            grads = dict(held)
            held.clear()
        names = tuple(grads)
        st = _push_start(f"scatter_start_{l}_{names[0]}", [grads[k] for k in names], False)
        scatters.append((l, names, st))
        return st[-1][0, 0]

    loss, dx, smalls = _local_step(x.reshape(N, D), loss_target.reshape(N, D), qs, _rope_tables(128), fetch, emit)

    out, recv, own = {}, {}, {}

    def collect(keys, after):
        for l, names, st in scatters:
            if names[0] in keys:
                ops = _push_wait(f"scatter_wait_{l}_{names[0]}", st, after, False)
                for i, k in enumerate(names):
                    own[l, k], recv[l, k] = ops[i], ops[len(names) + i]

    def shard_sums(keys):
        return [_sum_sources(me, [recv[l, k] for l in range(DEPTH)], [own[l, k] for l in range(DEPTH)]) for k in keys]

    def to_sibling(keys, sums):
        return _push_start(f"swap_start_{keys[0]}", sums, "sibling")

    def apply(keys, started, after):
        ops = _push_wait(f"swap_wait_{keys[0]}", started, after, "sibling")
        for i, k in enumerate(keys):
            mine, other = ops[i], ops[len(keys) + i]
            shp = shard[k].shape
            r = _adamw([mine, other], *((tr(t[k]) if k == "w_in" else t[k]).reshape(-1, shp[-1]) for t in (w, mom, var)))
            r = [t.reshape(shp) for t in r]
            out[k] = [tr(t) for t in r] if k == "w_in" else r
        return out[keys[-1]][1]

    collect(("w_ff1", "w_ff2", "w_out", "s5_w_glu"), dx)
    sums = shard_sums(("w_ff1", "w_ff2", "w_out", "s5_w_glu"))
    sums, smalls[0]["db1"] = lax.optimization_barrier((sums, smalls[0]["db1"]))
    native = _allreduce_small([[smalls[l][k] for k in NATIVE] for l in range(DEPTH)])
    sums, native = lax.optimization_barrier((sums, native))
    ff = to_sibling(("w_ff1", "w_ff2"), sums[:2])
    mix = to_sibling(("w_out", "s5_w_glu"), sums[2:])
    native = dict(zip(NATIVE, native))
    native["db1"] = native["db1"] + (ff[-1][0, 0] + mix[-1][0, 0])
    loss = native["loss"][0, 0, 0] + native["loss"][1, 0, 0]
    gsmall = _finish_small(native, w)
    view = lambda k, t: t.transpose(0, 1, 2, 4, 3) if k in ("s5_b_re", "s5_b_im") else t
    res = _adamw_small(*([view(k, t[k]) for k in SMALL] for t in (gsmall, w, mom, var)))
    for i, k in enumerate(SMALL):
        out[k] = [gsmall[k]] + [view(k, r[i]) for r in res]
    last = apply(("w_ff1", "w_ff2"), ff, res[0][-1])
    collect(("w_in",), last)
    win = to_sibling(("w_in",), shard_sums(("w_in",)))
    last = apply(("w_out", "s5_w_glu"), mix, win[-1])
    apply(("w_in",), win, last)

    return (loss, dx.reshape(NSEQ, L, D), *[out[k][0] for k in _WEIGHTS], *[out[k][1] for k in _WEIGHTS],
            *[out[k][2] for k in _WEIGHTS], *[out[k][3] for k in _WEIGHTS])
```

```python
import functools
import math

import jax
import jax.numpy as jnp
from jax import lax
from jax.experimental import pallas as pl
from jax.experimental.pallas import tpu as pltpu

F32 = jnp.float32
MX = jnp.bfloat16
MESH = pl.DeviceIdType.MESH

DEPTH = 2
NSEQ = 2
L = 2048
N = NSEQ * L
D = 1024
DFF = 4096
NSHARD = 4
S5_G, S5_H, S5_P = 16, 16, 64
GLA_CHUNK = 64
NCHUNK = L // GLA_CHUNK
GLA_GROUP = 4
NGROUP = NCHUNK // GLA_GROUP
SWA_BLK = 128
NBLK = L // SWA_BLK
SWA_PER = 2
ROT = 16
ROPE_THETA = 500000.0
LN_EPS = 1e-5
ALPHA = (2 * DEPTH) ** 0.25
NEG_BIG = -1e30
DIN = 1824
DINP = 1920
ADAM_LR, ADAM_B1, ADAM_B2, ADAM_EPS, ADAM_WD, ADAM_STEP = 0.001, 0.9, 0.999, 1e-08, 0.01, 10
VMEM_LIMIT = 56 * 1024 * 1024
TT = 512
SW = 512
FFN_TM = 512
FFN_TM_W = 1024
FFN_WB = 1
FFN_VMEM = 60 * 1024 * 1024
INPROJ_BWD_TM = 512


def _cp(sem, vmem=VMEM_LIMIT):
    return pltpu.CompilerParams(dimension_semantics=sem, vmem_limit_bytes=vmem)


def _mm(a, b):
    return jnp.dot(a.astype(MX), b.astype(MX), preferred_element_type=F32)


def _mm_nt(a, b):
    return lax.dot_general(a.astype(MX), b.astype(MX), (((1,), (1,)), ((), ())), preferred_element_type=F32)


def _mm_tn(a, b):
    return lax.dot_general(a.astype(MX), b.astype(MX), (((0,), (0,)), ((), ())), preferred_element_type=F32)


@jax.custom_vjp
def _dmm(a, b):
    return _mm(a, b)


_dmm.defvjp(lambda a, b: (_mm(a, b), (a, b)), lambda r, g: (_mm_nt(g, r[1]), _mm_tn(r[0], g)))


@jax.custom_vjp
def _dmm_nt(a, b):
    return _mm_nt(a, b)


_dmm_nt.defvjp(lambda a, b: (_mm_nt(a, b), (a, b)), lambda r, g: (_mm(g, r[1]), _mm_tn(g, r[0])))


@jax.custom_vjp
def _dmm_tn(a, b):
    return _mm_tn(a, b)


_dmm_tn.defvjp(lambda a, b: (_mm_tn(a, b), (a, b)), lambda r, g: (_mm_nt(r[1], g), _mm(r[0], g)))


def _split3(x):
    hi = x.astype(MX)
    r1 = x - hi.astype(F32)
    mid = r1.astype(MX)
    lo = (r1 - mid.astype(F32)).astype(MX)
    return hi, mid, lo


def _chunk_pairs(rows, rev, strict):
    r = lax.broadcasted_iota(jnp.int32, (rows, rows), 0)
    c = lax.broadcasted_iota(jnp.int32, (rows, rows), 1)
    order = ((c > r) if strict else (c >= r)) if rev else ((c < r) if strict else (c <= r))
    return (r // GLA_CHUNK == c // GLA_CHUNK) & order


def _cums_impl(x, rev):
    rows, w = x.shape
    t = jnp.where(_chunk_pairs(rows, rev, False), 1.0, 0.0).astype(MX)
    s = jnp.dot(t, jnp.concatenate(_split3(x), axis=1), preferred_element_type=F32)
    return s[:, 0:w] + s[:, w:2 * w] + s[:, 2 * w:3 * w]


@functools.partial(jax.custom_vjp, nondiff_argnums=(1,))
def _cums(x, rev):
    return _cums_impl(x, rev)


_cums.defvjp(lambda x, rev: (_cums_impl(x, rev), None), lambda rev, r, g: (_cums_impl(g, not rev),))


def _ln_fwd(s, g, b):
    mu = jnp.mean(s, axis=-1, keepdims=True)
    xc = s - mu
    var = jnp.mean(xc * xc, axis=-1, keepdims=True)
    return xc * lax.rsqrt(var + LN_EPS) * g + b


def _ln_bwd(dy, s, g):
    mu = jnp.mean(s, axis=-1, keepdims=True)
    xc = s - mu
    var = jnp.mean(xc * xc, axis=-1, keepdims=True)
    rstd = lax.rsqrt(var + LN_EPS)
    xhat = xc * rstd
    dxh = dy * g
    ds = rstd * (dxh - jnp.mean(dxh, axis=-1, keepdims=True) - xhat * jnp.mean(dxh * xhat, axis=-1, keepdims=True))
    return ds, jnp.sum(dy * xhat, axis=0, keepdims=True), jnp.sum(dy, axis=0, keepdims=True)


def _sds(shape, dtype=F32):
    return jax.ShapeDtypeStruct(shape, dtype)


_IN_ROW_PIECES = (((0, 0), (0, 456)), ((1, 0), (456, 456)), ((2, 0), (912, 112)), ((2, 112), (1792, 32)),
                  ((2, 144), (1024, 312)), ((3, 0), (1336, 456)))


def _in_rows(g4, behind):
    def body(g_ref, behind_ref, o_ref, tmp):
        tmp[DIN:DINP] = jnp.zeros((DINP - DIN, D), F32)
        for (j, s0), (d0, n_) in _IN_ROW_PIECES:
            tmp[d0:d0 + n_] = g_ref[j, s0:s0 + n_].astype(F32)
        o_ref[...] = tmp[...].astype(MX)

    vm = pl.BlockSpec(memory_space=pltpu.VMEM)
    return pl.pallas_call(body, in_specs=[vm, pl.BlockSpec(memory_space=pl.ANY)], out_specs=vm,
                          out_shape=_sds((DINP, D), MX), scratch_shapes=[pltpu.VMEM((DINP, D), F32)], name="in_rows",
                          compiler_params=pltpu.CompilerParams(vmem_limit_bytes=VMEM_LIMIT))(g4, behind)


def _inproj_fwd(x, wt, wa, ba):
    tm = 512

    def body(x_ref, w_ref, wa_ref, ba_ref, h_ref, la_ref):
        h = _mm_nt(x_ref[...], w_ref[...])
        h_ref[...] = h
        la_ref[...] = _logsig(_mm(h[:, DINP - 128:], wa_ref[...]) + ba_ref[...]) * (1.0 / 16.0)

    return pl.pallas_call(
        body, grid=(N // tm,),
        in_specs=[pl.BlockSpec((tm, D), lambda i: (i, 0)), pl.BlockSpec((DINP, D), lambda i: (0, 0)),
                  pl.BlockSpec((128, 256), lambda i: (0, 0)), pl.BlockSpec((1, 256), lambda i: (0, 0))],
        out_specs=[pl.BlockSpec((tm, DINP), lambda i: (i, 0)), pl.BlockSpec((tm, 256), lambda i: (i, 0))],
        out_shape=[_sds((N, DINP)), _sds((N, 256))], name="inproj_fwd", compiler_params=_cp(("parallel",)))(x, wt, wa, ba)


def _inproj_bwd(x, w, dxp, du2, dud, gq_f, gq_b, gk_f, gk_b, gv_f, gv_b, gr, daq, dakv, h, wa, ba, dla_f, dla_b):
    tm = INPROJ_BWD_TM
    nt = N // tm

    def body(x_ref, w_ref, dxp_ref, du2_ref, dud_ref, gqf, gqb, gkf, gkb, gvf, gvb, gr_ref, daq_ref, dakv_ref,
             hl_ref, wa_ref, ba_ref, df_ref, db_ref, dx_ref, dw_ref, dwa_ref, dba_ref, acc):
        i = pl.program_id(0)
        f = lambda r: r[...].astype(F32)
        hl = hl_ref[...]
        pre = _mm(hl, wa_ref[...]) + ba_ref[...]
        dpre = jnp.concatenate([df_ref[...], db_ref[...]], axis=1) * (1.0 / 16.0) * jax.nn.sigmoid(-pre)
        dwa = _mm_tn(hl, dpre)[0:32]
        dba = jnp.sum(dpre, axis=0, keepdims=True)
        dh = jnp.concatenate([
            du2_ref[0] + du2_ref[1] + f(dud_ref), f(gqf) + f(gqb), f(gkf) + f(gkb), f(gvf) + f(gvb),
            f(gr_ref), f(daq_ref), f(dakv_ref), _mm_nt(dpre, wa_ref[...])], axis=1)
        dx_ref[...] = dxp_ref[...] + _mm(dh, w_ref[...])
        contrib = _mm_tn(dh, x_ref[...])

        @pl.when(i == 0)
        def _():
            acc[...] = contrib
            dwa_ref[...] = dwa
            dba_ref[...] = dba

        @pl.when(i > 0)
        def _():
            acc[...] += contrib
            dwa_ref[...] += dwa
            dba_ref[...] += dba

        @pl.when(i == nt - 1)
        def _():
            for (j, d0), (s0, n_) in _IN_ROW_PIECES:
                dw_ref[j, d0:d0 + n_] = acc[s0:s0 + n_].astype(MX)

    row = lambda w_: pl.BlockSpec((tm, w_), lambda i: (i, 0))
    return pl.pallas_call(
        body, grid=(nt,),
        in_specs=[row(D), pl.BlockSpec((DINP, D), lambda i: (0, 0)), row(D),
                  pl.BlockSpec((2, tm, 256), lambda i: (0, i, 0)), row(256), row(128), row(128), row(128), row(128),
                  row(256), row(256), row(256), row(512), row(256),
                  pl.BlockSpec((tm, 128), lambda i: (i, 14)), pl.BlockSpec((128, 256), lambda i: (0, 0)),
                  pl.BlockSpec((1, 256), lambda i: (0, 0)), row(128), row(128)],
        out_specs=[row(D), pl.BlockSpec((NSHARD, DIN // NSHARD, D), lambda i: (0, 0, 0)),
                   pl.BlockSpec((32, 256), lambda i: (0, 0)), pl.BlockSpec((1, 256), lambda i: (0, 0))],
        out_shape=[_sds((N, D)), _sds((NSHARD, DIN // NSHARD, D), MX), _sds((32, 256)), _sds((1, 256))],
        scratch_shapes=[pltpu.VMEM((DINP, D), F32)],
        name="inproj_bwd", compiler_params=_cp(("arbitrary",)))(
            x, w, dxp, du2, dud, gq_f, gq_b, gk_f, gk_b, gv_f, gv_b, gr, daq, dakv, h, wa, ba, dla_f, dla_b)


def _tile_scan(xr, xi, a, cr, ci, reverse):
    for lvl, d in enumerate((1, 2, 4)):
        sh = 8 - d if reverse else d
        sr = pltpu.roll(xr, sh, 0)
        si = pltpu.roll(xi, sh, 0)
        ar, ai = a[2 * lvl], a[2 * lvl + 1]
        xr, xi = xr + ar * sr - ai * si, xi + ar * si + ai * sr
    pr, pi = a[6], a[7]
    return xr + pr * cr - pi * ci, xi + pr * ci + pi * cr


NJ = TT // 8


def _lockstep_tables(mr, mi):
    def body(mr_ref, mi_ref, a_ref, p_ref, ac_ref, pc_ref):
        rowid = lax.broadcasted_iota(jnp.int32, (8, 2 * SW), 0)

        def mul(a, b):
            return a[0] * b[0] - a[1] * b[1], a[0] * b[1] + a[1] * b[0]

        for z in range(2):
            for sign, reverse, a_out, p_out in ((1.0, z == 1, a_ref, p_ref), (-1.0, z == 0, ac_ref, pc_ref)):
                m = (mr_ref[z:z + 1, :], sign * mi_ref[z:z + 1, :])
                pw = [m]
                for _ in range(NJ - 1):
                    pw.append(mul(pw[-1], m))
                n = pw[-1]
                link = [n]
                for _ in range(7):
                    link.append(mul(link[-1], n))
                tiles = [jnp.broadcast_to(m[0], (8, 2 * SW)), jnp.broadcast_to(m[1], (8, 2 * SW))]
                for d in (1, 2, 4):
                    keep = (rowid <= 7 - d) if reverse else (rowid >= d)
                    tiles += [jnp.where(keep, link[d - 1][c], 0.0) for c in range(2)]
                for c in range(2):
                    t = jnp.zeros((8, 2 * SW), F32)
                    for i in range(8):
                        t = jnp.where(rowid == (7 - i if reverse else i), link[i][c], t)
                    tiles.append(t)
                for blk in range(2):
                    lanes = slice(blk * SW, (blk + 1) * SW)
                    for k, t in enumerate(tiles):
                        a_out[z, blk, k] = t[:, lanes]
                    for j in range(NJ):
                        src = pw[NJ - 1 - j] if reverse else pw[j]
                        for c in range(2):
                            p_out[z, blk, c, j:j + 1, :] = src[c][:, lanes]

    vm = pl.BlockSpec(memory_space=pltpu.VMEM)
    a_shape, p_shape = _sds((2, 2, 10, 8, SW)), _sds((2, 2, 2, NJ, SW))
    a, p, ac, pc = pl.pallas_call(body, in_specs=[vm, vm], out_specs=[vm] * 4, out_shape=[a_shape, p_shape] * 2,
                                  name="s5_tables")(mr, mi)
    return (a, p), (ac, pc)


def _to_lockstep(ref, *lead):
    return jnp.concatenate([ref[(*lead, pl.ds(j, 8, stride=NJ), slice(None))] for j in range(NJ)], axis=0)


def _from_lockstep(val, ref, *lead):
    for j in range(NJ):
        ref[(*lead, pl.ds(j, 8, stride=NJ), slice(None))] = val[8 * j:8 * j + 8]


def _expand_powers(p_ref, pexp):
    for c in range(2):
        for j in range(NJ):
            pexp[c, j] = jnp.broadcast_to(p_ref[0, 0, c, j:j + 1, :], (8, SW))


def _lockstep_scan(xre, xim, a_ref, pexp, car, reverse, extra=None):
    a = [a_ref[0, 0, k] for k in range(10)]
    mr, mi = a[0], a[1]
    order = (lambda i: NJ - 1 - i) if reverse else (lambda i: i)

    def local(i, hcar):
        hr, hi = hcar
        r0 = pl.multiple_of(order(i) * 8, 8)
        hr, hi = mr * hr - mi * hi + xre[pl.ds(r0, 8), :], mr * hi + mi * hr + xim[pl.ds(r0, 8), :]
        xre[pl.ds(r0, 8), :] = hr
        xim[pl.ds(r0, 8), :] = hi
        return hr, hi

    z8 = jnp.zeros((8, SW), F32)
    er, ei = lax.fori_loop(0, NJ, local, (z8, z8), unroll=4)
    c0r, c0i = car[0], car[1]
    er, ei = _tile_scan(er, ei, a[2:], c0r, c0i, reverse)
    rowid = lax.broadcasted_iota(jnp.int32, (8, SW), 0)
    first, sh, last = (7, 7, 0) if reverse else (0, 1, 7)
    cvr = jnp.where(rowid == first, c0r, pltpu.roll(er, sh, 0))
    cvi = jnp.where(rowid == first, c0i, pltpu.roll(ei, sh, 0))
    car[0] = jnp.broadcast_to(er[last:last + 1, :], (8, SW))
    car[1] = jnp.broadcast_to(ei[last:last + 1, :], (8, SW))

    def fix(i, carry):
        j = order(i)
        r0 = pl.multiple_of(j * 8, 8)
        pr, pi = pexp[0, j], pexp[1, j]
        sr = xre[pl.ds(r0, 8), :] + pr * cvr - pi * cvi
        si = xim[pl.ds(r0, 8), :] + pr * cvi + pi * cvr
        xre[pl.ds(r0, 8), :] = sr
        xim[pl.ds(r0, 8), :] = si
        if extra is None:
            return carry
        return (sr, si, extra(r0, sr, si, carry[0], carry[1], carry[2]))

    init = (cvr, cvi, extra(None, None, None, None, None, None)) if extra is not None else 0
    return lax.fori_loop(0, NJ, fix, init, unroll=4)


def _s5_time_block(z, s, t, adjoint):
    flip = (1 - z) if adjoint else z
    return s * (L // TT) + t + flip * (L // TT - 1 - 2 * t)


def _s5_fwd(h, bre, bim, cre, cim, tab):
    nt = L // TT
    taba, tabp = tab

    def body(u_ref, bre_ref, bim_ref, cre_ref, cim_ref, a_ref, p_ref, hre_ref, him_ref, y_ref, car, pexp):
        z = pl.program_id(1)
        s = pl.program_id(2)
        tc = pl.program_id(3)

        @pl.when(tc == 0)
        def _():
            car[...] = jnp.zeros_like(car)

        @pl.when((tc == 0) & (s == 0))
        def _():
            _expand_powers(p_ref, pexp)

        u = _to_lockstep(u_ref)
        hre_ref[0] = _mm(u, bre_ref[0, 0])
        him_ref[0] = _mm(u, bim_ref[0, 0])

        @pl.when(z == 0)
        def _():
            _lockstep_scan(hre_ref.at[0], him_ref.at[0], a_ref, pexp, car, False)

        @pl.when(z == 1)
        def _():
            _lockstep_scan(hre_ref.at[0], him_ref.at[0], a_ref, pexp, car, True)

        _from_lockstep(_mm(hre_ref[0], cre_ref[0, 0]) - _mm(him_ref[0], cim_ref[0, 0]), y_ref, 0)

    tb = lambda b, z, s, t: _s5_time_block(z, s, t, False)
    wspec = lambda r, c: pl.BlockSpec((1, 1, r, c), lambda b, z, s, t: (z, b, 0, 0))
    return pl.pallas_call(
        body, grid=(2, 2, NSEQ, nt),
        in_specs=[pl.BlockSpec((TT, 128), lambda b, z, s, t: (tb(b, z, s, t), b)),
                  wspec(128, SW), wspec(128, SW), wspec(SW, 128), wspec(SW, 128),
                  pl.BlockSpec((1, 1, 10, 8, SW), lambda b, z, s, t: (z, b, 0, 0, 0)),
                  pl.BlockSpec((1, 1, 2, NJ, SW), lambda b, z, s, t: (z, b, 0, 0, 0))],
        out_specs=[pl.BlockSpec((1, TT, SW), lambda b, z, s, t: (z, tb(b, z, s, t), b)),
                   pl.BlockSpec((1, TT, SW), lambda b, z, s, t: (z, tb(b, z, s, t), b)),
                   pl.BlockSpec((1, TT, 128), lambda b, z, s, t: (z, tb(b, z, s, t), b))],
        out_shape=[_sds((2, N, 2 * SW)), _sds((2, N, 2 * SW)), _sds((2, N, 256))],
        scratch_shapes=[pltpu.VMEM((2, 8, SW), F32), pltpu.VMEM((2, NJ, 8, SW), F32)],
        name="s5_fwd", compiler_params=_cp(("arbitrary",) * 4))(h, bre, bim, cre, cim, taba, tabp)


def _s5_bwd(h, dyp, hre, him, bre, bim, cre, cim, tabc):
    nt = L // TT
    taba, tabp = tabc

    def body(u_ref, dy_ref, hre_ref, him_ref, bre_ref, bim_ref, cre_ref, cim_ref, a_ref, p_ref,
             du_ref, dbre_ref, dbim_ref, dcre_ref, dcim_ref, dmu_ref, gre, gim, car, acc, macc, pexp):
        z = pl.program_id(1)
        s = pl.program_id(2)
        tc = pl.program_id(3)

        @pl.when(tc == 0)
        def _():
            car[...] = jnp.zeros_like(car)

        @pl.when((tc == 0) & (s == 0))
        def _():
            acc[...] = jnp.zeros_like(acc)
            macc[...] = jnp.zeros_like(macc)
            _expand_powers(p_ref, pexp)

        dy = _to_lockstep(dy_ref)
        gre[...] = _mm_nt(dy, cre_ref[0, 0])
        gim[...] = -_mm_nt(dy, cim_ref[0, 0])

        def run(reverse):
            def pair(r0, gr_, gi_, pvr, pvi, m):
                if r0 is None:
                    return (macc[0], macc[1])
                hr = hre_ref[0, pl.ds(r0, 8), :]
                hi = him_ref[0, pl.ds(r0, 8), :]
                return (m[0] + pvr * hr + pvi * hi, m[1] + pvi * hr - pvr * hi)

            _, _, (dmr, dmi) = _lockstep_scan(gre, gim, a_ref, pexp, car, reverse, pair)
            macc[0] = dmr
            macc[1] = dmi

        @pl.when(z == 0)
        def _():
            run(True)

        @pl.when(z == 1)
        def _():
            run(False)

        gr = gre[...]
        gi = gim[...]
        u = _to_lockstep(u_ref)
        _from_lockstep(_mm_nt(gr, bre_ref[0, 0]) + _mm_nt(gi, bim_ref[0, 0]), du_ref, 0)
        acc[0] += _mm_tn(u, gr)
        acc[1] += _mm_tn(u, gi)
        acc[2] += _mm_tn(dy, hre_ref[0])
        acc[3] -= _mm_tn(dy, him_ref[0])

        @pl.when((tc == nt - 1) & (s == NSEQ - 1))
        def _():
            grp = lax.broadcasted_iota(jnp.int32, (S5_H, SW), 1) // S5_P
            for k, out in enumerate((dbre_ref, dbim_ref, dcre_ref, dcim_ref)):
                c = jnp.zeros((S5_H, SW), F32)
                for i in range(8):
                    c = c + jnp.where(grp == i, acc[k, i * S5_H:(i + 1) * S5_H, :], 0.0)
                out[0, 0] = c
            dmu_ref[0, 0] = jnp.concatenate([jnp.sum(macc[0], axis=0, keepdims=True),
                                             jnp.sum(macc[1], axis=0, keepdims=True)], axis=0)

    tb = lambda b, z, s, t: _s5_time_block(z, s, t, True)
    wspec = lambda r, c: pl.BlockSpec((1, 1, r, c), lambda b, z, s, t: (z, b, 0, 0))
    tok = lambda w_: pl.BlockSpec((TT, w_), lambda b, z, s, t: (tb(b, z, s, t), b))
    st = pl.BlockSpec((1, TT, SW), lambda b, z, s, t: (z, tb(b, z, s, t), b))
    return pl.pallas_call(
        body, grid=(2, 2, NSEQ, nt),
        in_specs=[tok(128), tok(128), st, st, wspec(128, SW), wspec(128, SW), wspec(SW, 128), wspec(SW, 128),
                  pl.BlockSpec((1, 1, 10, 8, SW), lambda b, z, s, t: (z, b, 0, 0, 0)),
                  pl.BlockSpec((1, 1, 2, NJ, SW), lambda b, z, s, t: (z, b, 0, 0, 0))],
        out_specs=[pl.BlockSpec((1, TT, 128), lambda b, z, s, t: (z, tb(b, z, s, t), b)),
                   wspec(S5_H, SW), wspec(S5_H, SW), wspec(S5_H, SW), wspec(S5_H, SW),
                   wspec(2, SW)],
        out_shape=[_sds((2, N, 256))] + [_sds((2, 2, S5_H, SW))] * 4 + [_sds((2, 2, 2, SW))],
        scratch_shapes=[pltpu.VMEM((TT, SW), F32), pltpu.VMEM((TT, SW), F32), pltpu.VMEM((2, 8, SW), F32),
                        pltpu.VMEM((4, 128, SW), F32), pltpu.VMEM((2, 8, SW), F32), pltpu.VMEM((2, NJ, 8, SW), F32)],
        name="s5_bwd", compiler_params=_cp(("arbitrary",) * 4))(h, dyp, hre, him, bre, bim, cre, cim, taba, tabp)


_GELU_C = math.sqrt(2.0 / math.pi)


def _gelu(y):
    return 0.5 * y * (1.0 + jnp.tanh(_GELU_C * (y + 0.044715 * y * y * y)))


def _gelu_grad(y):
    t = jnp.tanh(_GELU_C * (y + 0.044715 * y * y * y))
    return 0.5 * (1.0 + t) + 0.5 * y * (1.0 - t * t) * _GELU_C * (1.0 + 3 * 0.044715 * y * y)


def _glu_halves(w4_ref):
    return (jnp.concatenate([w4_ref[0], w4_ref[1]], axis=1), jnp.concatenate([w4_ref[2], w4_ref[3]], axis=1))


def _s5_glu_fwd(y2, h, dsk, w4, bv, bg):
    tm = 512

    def body(y2_ref, u_ref, d_ref, w4_ref, bv_ref, bg_ref, ya_ref):
        wv, wg = _glu_halves(w4_ref)
        z = _gelu(y2_ref[0] + y2_ref[1] + d_ref[...] * u_ref[...])
        val = _mm(z, wv) + bv_ref[...]
        gate = _mm(z, wg) + bg_ref[...]
        ya_ref[...] = (val * jax.nn.sigmoid(gate)).astype(MX)

    full = lambda r, c: pl.BlockSpec((r, c), lambda i: (0, 0))
    return pl.pallas_call(
        body, grid=(N // tm,),
        in_specs=[pl.BlockSpec((2, tm, 256), lambda i: (0, i, 0)), pl.BlockSpec((tm, 256), lambda i: (i, 0)),
                  full(1, 256), pl.BlockSpec((NSHARD, 256, 128), lambda i: (0, 0, 0)), full(1, 256), full(1, 256)],
        out_specs=pl.BlockSpec((tm, 256), lambda i: (i, 0)),
        out_shape=_sds((N, 256), MX), name="s5_glu_fwd", compiler_params=_cp(("parallel",)))(y2, h, dsk, w4, bv, bg)


def _s5_glu_bwd(y2, h, dsk, w4, bv, bg, dya):
    tm = 512
    nt = N // tm

    def body(y2_ref, u_ref, d_ref, w4_ref, bv_ref, bg_ref, dya_ref,
             dyp_ref, dud_ref, dd_ref, dw4_ref, dbv_ref, dbg_ref, accv, accg):
        i = pl.program_id(0)

        @pl.when(i == 0)
        def _():
            for r in (dd_ref, accv, accg, dbv_ref, dbg_ref):
                r[...] = jnp.zeros_like(r)

        wv, wg = _glu_halves(w4_ref)
        u = u_ref[...]
        y = y2_ref[0] + y2_ref[1] + d_ref[...] * u
        z = _gelu(y)
        val = _mm(z, wv) + bv_ref[...]
        sig = jax.nn.sigmoid(_mm(z, wg) + bg_ref[...])
        dya = dya_ref[...]
        dval = dya * sig
        dgate = dya * val * sig * (1.0 - sig)
        dz = _mm_nt(dval, wv) + _mm_nt(dgate, wg)
        dy = dz * _gelu_grad(y)
        dyp_ref[...] = dy
        dud_ref[...] = (dy * d_ref[...]).astype(MX)
        dd_ref[...] += jnp.sum(dy * u, axis=0, keepdims=True)
        accv[...] += _mm_tn(z, dval)
        accg[...] += _mm_tn(z, dgate)
        dbv_ref[...] += jnp.sum(dval, axis=0, keepdims=True)
        dbg_ref[...] += jnp.sum(dgate, axis=0, keepdims=True)

        @pl.when(i == nt - 1)
        def _():
            dw4_ref[0] = accv[:, 0:128].astype(MX)
            dw4_ref[1] = accv[:, 128:256].astype(MX)
            dw4_ref[2] = accg[:, 0:128].astype(MX)
            dw4_ref[3] = accg[:, 128:256].astype(MX)

    full = lambda r, c: pl.BlockSpec((r, c), lambda i: (0, 0))
    row = pl.BlockSpec((tm, 256), lambda i: (i, 0))
    wspec = pl.BlockSpec((NSHARD, 256, 128), lambda i: (0, 0, 0))
    return pl.pallas_call(
        body, grid=(nt,),
        in_specs=[pl.BlockSpec((2, tm, 256), lambda i: (0, i, 0)), row, full(1, 256), wspec, full(1, 256), full(1, 256),
                  row],
        out_specs=[row, row, full(1, 256), wspec, full(1, 256), full(1, 256)],
        out_shape=[_sds((N, 256)), _sds((N, 256), MX), _sds((1, 256)), _sds((NSHARD, 256, 128), MX), _sds((1, 256)),
                   _sds((1, 256))],
        scratch_shapes=[pltpu.VMEM((256, 256), F32), pltpu.VMEM((256, 256), F32)],
        name="s5_glu_bwd", compiler_params=_cp(("arbitrary",)))(y2, h, dsk, w4, bv, bg, dya)


def _logsig(x):
    return jnp.minimum(x, 0.0) - jnp.log(1.0 + jnp.exp(-jnp.abs(x)))


def _gla_chunk(q, k, v, la, st, rev):
    c = GLA_CHUNK
    rows = q.shape[0]
    nch = rows // c
    b = _cums(la, rev)
    blc = [jnp.sum(la[i * c:(i + 1) * c], axis=0, keepdims=True) for i in range(nch)]
    bl = jnp.concatenate([jnp.broadcast_to(t, (c, 128)) for t in blc], axis=0)
    q_in = q * (32.0 ** -0.5) * jnp.exp(b)
    k_in = k * jnp.exp(-b)
    k_st = k * jnp.exp(bl - b)
    lane_k = lax.broadcasted_iota(jnp.int32, (1, 128), 1) // 32
    lane_v = lax.broadcasted_iota(jnp.int32, (1, 256), 1) // 64
    qs = jnp.concatenate([jnp.where(lane_k == hd, q_in, 0.0) for hd in range(4)], axis=0)
    a = _dmm_nt(qs, k_in)
    a = jnp.where(jnp.concatenate([_chunk_pairs(rows, rev, rev)] * 4, axis=0), a, 0.0)
    o4 = _dmm(a, v)
    o = jnp.zeros((rows, 256), F32)
    for hd in range(4):
        o = o + jnp.where(lane_v == hd, o4[hd * rows:(hd + 1) * rows], 0.0)
    bd = (lax.broadcasted_iota(jnp.int32, (256, 128), 0) // 64) == (lax.broadcasted_iota(jnp.int32, (256, 128), 1) // 32)
    inter = [None] * nch
    for i in (reversed(range(nch)) if rev else range(nch)):
        sl = slice(i * c, (i + 1) * c)
        inter[i] = _dmm_nt(q_in[sl], st)
        st = jnp.exp(blc[i]) * st + jnp.where(bd, _dmm_tn(v[sl], k_st[sl]), 0.0)
    return o + jnp.concatenate(inter, axis=0), st


def _gla_chunk_of(c, rev):
    return NGROUP - 1 - c if rev else c


def _gla_fwd(h, la2):
    c = GLA_GROUP * GLA_CHUNK

    def body(qf, kf, vf, laf, qb, kb, vb, lab, of_ref, ob_ref, sf_ref, sb_ref, stf, stb):
        @pl.when(pl.program_id(0) == 0)
        def _():
            stf[...] = jnp.zeros_like(stf)
            stb[...] = jnp.zeros_like(stb)

        ins = [(qf[s], kf[s], vf[s], laf[s], stf[s], qb[s], kb[s], vb[s], lab[s], stb[s]) for s in range(NSEQ)]
        outs = [(_gla_chunk(*t[:5], False), _gla_chunk(*t[5:], True)) for t in ins]
        for s in range(NSEQ):
            sf_ref[s, 0] = ins[s][4]
            sb_ref[s, 0] = ins[s][9]
            (of_ref[s], stf[s]), (ob_ref[s], stb[s]) = outs[s]

    def specs(rev):
        ch = lambda i: _gla_chunk_of(i, rev)
        return [pl.BlockSpec((NSEQ, c, 128), lambda i: (0, ch(i), 2)), pl.BlockSpec((NSEQ, c, 128), lambda i: (0, ch(i), 3)),
                pl.BlockSpec((NSEQ, c, 256), lambda i: (0, ch(i), 2)),
                pl.BlockSpec((NSEQ, c, 128), lambda i: (0, ch(i), 1 if rev else 0))]

    orow = lambda rev: pl.BlockSpec((NSEQ, c, 256), lambda i: (0, _gla_chunk_of(i, rev), 0))
    srow = lambda rev: pl.BlockSpec((NSEQ, 1, 256, 128), lambda i: (0, _gla_chunk_of(i, rev), 0, 0))
    h3, la3 = h.reshape(NSEQ, L, DINP), la2.reshape(NSEQ, L, 256)
    of, ob, sf, sb = pl.pallas_call(
        body, grid=(NGROUP,),
        in_specs=specs(False) + specs(True),
        out_specs=[orow(False), orow(True), srow(False), srow(True)],
        out_shape=[_sds((NSEQ, L, 256)), _sds((NSEQ, L, 256)), _sds((NSEQ, NGROUP, 256, 128)),
                   _sds((NSEQ, NGROUP, 256, 128))],
        scratch_shapes=[pltpu.VMEM((NSEQ, 256, 128), F32), pltpu.VMEM((NSEQ, 256, 128), F32)],
        name="gla_fwd", compiler_params=_cp(("arbitrary",)))(h3, h3, h3, la3, h3, h3, h3, la3)
    return of.reshape(N, 256), ob.reshape(N, 256), sf, sb


def _gla_bwd(h, la2, do, sf, sb):
    c = GLA_GROUP * GLA_CHUNK

    def body(qf, kf, vf, laf, dof, sfr, qb, kb, vb, lab, dob, sbr,
             dqf, dkf, dvf, dlf, dqb, dkb, dvb, dlb, dstf, dstb):
        @pl.when(pl.program_id(0) == 0)
        def _():
            dstf[...] = jnp.zeros_like(dstf)
            dstb[...] = jnp.zeros_like(dstb)

        def one(s, q, k, v, la, do_, st, dst, rev):
            _, vjp = jax.vjp(functools.partial(_gla_chunk, rev=rev), q[s], k[s], v[s], la[s], st[s, 0])
            return vjp((do_[s], dst[s]))

        res = [(one(s, qf, kf, vf, laf, dof, sfr, dstf, False), one(s, qb, kb, vb, lab, dob, sbr, dstb, True))
               for s in range(NSEQ)]
        for s in range(NSEQ):
            for (gq, gk, gv, gl, gs), (dq, dk, dv, dl, dst) in ((res[s][0], (dqf, dkf, dvf, dlf, dstf)),
                                                                  (res[s][1], (dqb, dkb, dvb, dlb, dstb))):
                dq[s], dk[s], dv[s] = gq.astype(MX), gk.astype(MX), gv.astype(MX)
                dl[s], dst[s] = gl, gs

    def specs(rev):
        ch = lambda i: _gla_chunk_of(i, not rev)
        return [pl.BlockSpec((NSEQ, c, 128), lambda i: (0, ch(i), 2)), pl.BlockSpec((NSEQ, c, 128), lambda i: (0, ch(i), 3)),
                pl.BlockSpec((NSEQ, c, 256), lambda i: (0, ch(i), 2)),
                pl.BlockSpec((NSEQ, c, 128), lambda i: (0, ch(i), 1 if rev else 0)),
                pl.BlockSpec((NSEQ, c, 256), lambda i: (0, ch(i), 0)),
                pl.BlockSpec((NSEQ, 1, 256, 128), lambda i: (0, ch(i), 0, 0))]

    def ospecs(rev):
        ch = lambda i: _gla_chunk_of(i, not rev)
        n = pl.BlockSpec((NSEQ, c, 128), lambda i: (0, ch(i), 0))
        return [n, n, pl.BlockSpec((NSEQ, c, 256), lambda i: (0, ch(i), 0)), n]

    oshape = [_sds((NSEQ, L, 128), MX), _sds((NSEQ, L, 128), MX), _sds((NSEQ, L, 256), MX), _sds((NSEQ, L, 128))]
    h3, la3, do3 = h.reshape(NSEQ, L, DINP), la2.reshape(NSEQ, L, 256), do.reshape(NSEQ, L, 256)
    res = pl.pallas_call(
        body, grid=(NGROUP,),
        in_specs=specs(False) + specs(True),
        out_specs=ospecs(False) + ospecs(True),
        out_shape=oshape + oshape,
        scratch_shapes=[pltpu.VMEM((NSEQ, 256, 128), F32), pltpu.VMEM((NSEQ, 256, 128), F32)],
        name="gla_bwd", compiler_params=_cp(("arbitrary",)))(h3, h3, h3, la3, do3, sf, h3, h3, h3, la3, do3, sb)
    return [r.reshape(N, r.shape[-1]) for r in res]


def _gla_post(of, ob, r, g):
    o = of + ob
    head = lax.broadcasted_iota(jnp.int32, (1, 256), 1) // 64
    mu = jnp.zeros_like(o)
    for hd in range(4):
        mu = mu + jnp.where(head == hd, jnp.sum(jnp.where(head == hd, o, 0.0), axis=-1, keepdims=True) * (1.0 / 64.0), 0.0)
    xc = o - mu
    var = jnp.zeros_like(o)
    for hd in range(4):
        var = var + jnp.where(head == hd, jnp.sum(jnp.where(head == hd, xc * xc, 0.0), axis=-1, keepdims=True) * (1.0 / 64.0), 0.0)
    return xc * lax.rsqrt(var + LN_EPS) * g * (r * jax.nn.sigmoid(r))


def _gla_post_fwd(of, ob, h, g):
    tm = 512

    def body(of_ref, ob_ref, r_ref, g_ref, y_ref):
        y_ref[...] = _gla_post(of_ref[...], ob_ref[...], r_ref[...], g_ref[...]).astype(MX)

    row = pl.BlockSpec((tm, 256), lambda i: (i, 0))
    return pl.pallas_call(
        body, grid=(N // tm,),
        in_specs=[row, row, pl.BlockSpec((tm, 256), lambda i: (i, 3)), pl.BlockSpec((1, 256), lambda i: (0, 0))],
        out_specs=row, out_shape=_sds((N, 256), MX), name="gla_post_fwd", compiler_params=_cp(("parallel",)))(of, ob, h, g)


def _gla_post_bwd(of, ob, h, g, dyb):
    tm = 512

    def body(of_ref, ob_ref, r_ref, g_ref, dy_ref, do_ref, dr_ref, dg_ref):
        @pl.when(pl.program_id(0) == 0)
        def _():
            dg_ref[...] = jnp.zeros_like(dg_ref)

        _, vjp = jax.vjp(_gla_post, of_ref[...], ob_ref[...], r_ref[...], g_ref[...])
        go, _, gr, gg = vjp(dy_ref[...])
        do_ref[...] = go
        dr_ref[...] = gr.astype(MX)
        dg_ref[...] += gg

    row = pl.BlockSpec((tm, 256), lambda i: (i, 0))
    one = pl.BlockSpec((1, 256), lambda i: (0, 0))
    return pl.pallas_call(
        body, grid=(N // tm,),
        in_specs=[row, row, pl.BlockSpec((tm, 256), lambda i: (i, 3)), one, row],
        out_specs=[row, row, one], out_shape=[_sds((N, 256)), _sds((N, 256), MX), _sds((1, 256))],
        name="gla_post_bwd", compiler_params=_cp(("arbitrary",)))(of, ob, h, g, dyb)


def _rope_tables(width):
    pos = jnp.arange(L, dtype=F32)
    inv_freq = ROPE_THETA ** (-jnp.arange(0, ROT, 2, dtype=F32) / ROT)
    ang = pos[:, None] * inv_freq[None, :]
    cos, sin = jnp.cos(ang), jnp.sin(ang)
    one = jnp.ones((L, 64 - ROT), F32)
    zero = jnp.zeros((L, 64 - ROT), F32)
    z8 = jnp.zeros((L, ROT // 2), F32)
    c = jnp.concatenate([cos, cos, one], axis=1)
    sa = jnp.concatenate([z8, sin, zero], axis=1)
    sb = jnp.concatenate([-sin, z8, zero], axis=1)
    rep = width // 64
    return jnp.stack([jnp.tile(c, (1, rep)), jnp.tile(sa, (1, rep)), jnp.tile(sb, (1, rep))])


def _pieces(t, f):
    out = [f(t[:, c * 128:(c + 1) * 128]) for c in range(t.shape[-1] // 128)]
    return out[0] if len(out) == 1 else jnp.concatenate(out, axis=1)


def _rope(t, tab):
    return _pieces(t, lambda x: x * tab[0] + pltpu.roll(x, ROT // 2, 1) * tab[1] + pltpu.roll(x, 128 - ROT // 2, 1) * tab[2])


def _rope_t(g, tab):
    return _pieces(g, lambda x: x * tab[0] + pltpu.roll(x * tab[1], 128 - ROT // 2, 1) + pltpu.roll(x * tab[2], ROT // 2, 1))


def _swa_pad_kv(kv_ref, tk_ref, kexp, vexp):
    z = jnp.zeros((SWA_BLK, 256), F32)
    kr = _rope(kv_ref[:, 0:128], tk_ref[...])
    for hk in range(2):
        for pad in (kexp, vexp):
            pad[hk, 0:SWA_BLK] = z
            pad[hk, SWA_BLK + L:] = z
        kexp[hk, SWA_BLK:SWA_BLK + L] = _swa_expand(kr, hk)
        vexp[hk, SWA_BLK:SWA_BLK + L] = _swa_expand(kv_ref[:, 128:256], hk)


def _swa_expand(x, hk):
    lane = lax.broadcasted_iota(jnp.int32, x.shape, 1)
    sw = pltpu.roll(x, 64, 1)
    pair = jnp.where(lane < 64, x, sw) if hk == 0 else jnp.where(lane < 64, sw, x)
    return jnp.concatenate([pair, pair], axis=1)


def _swa_fold(x, hk):
    a = x[:, 0:128] + x[:, 128:256]
    t = a + pltpu.roll(a, 64, 1)
    lane = lax.broadcasted_iota(jnp.int32, a.shape, 1)
    return jnp.where((lane < 64) if hk == 0 else (lane >= 64), t, 0.0)


def _swa_bias_tables(bias):
    i = lax.broadcasted_iota(jnp.int32, (SWA_BLK, 3 * SWA_BLK), 0)
    j = lax.broadcasted_iota(jnp.int32, (SWA_BLK, 3 * SWA_BLK), 1)
    band = (j - i >= 0) & (j - i <= 2 * SWA_BLK)
    for v, inside in enumerate((j >= SWA_BLK, True, j < 2 * SWA_BLK)):
        bias[v] = jnp.where(band & inside, 0.0, NEG_BIG)


def _swa_bias(bias, blk):
    return bias[jnp.where(blk == 0, 0, jnp.where(blk == NBLK - 1, 2, 1))]


def _swa_probs(q2, kexp, bias, sink_ref, hk):
    slot = lax.broadcasted_iota(jnp.int32, (1, 256), 1) // 64
    qs = jnp.concatenate([jnp.where(slot == g, q2, 0.0) for g in range(4)], axis=0)
    s = _mm_nt(qs, kexp) + jnp.concatenate([bias] * 4, axis=0)
    rowg = lax.broadcasted_iota(jnp.int32, (4 * SWA_BLK, 1), 0) // SWA_BLK
    sink = jnp.zeros((4 * SWA_BLK, 1), F32)
    for g in range(4):
        sink = jnp.where(rowg == g, sink_ref[hk * 4 + g], sink)
    m = jnp.maximum(jnp.max(s, axis=-1, keepdims=True), sink)
    p = jnp.exp(s - m)
    ps = jnp.exp(sink - m)
    inv = 1.0 / (jnp.sum(p, axis=-1, keepdims=True) + ps)
    return qs, p * inv, ps * inv, slot, rowg


def _swa_qtab(tk_ref, r0):
    return [tk_ref[i, pl.ds(r0, SWA_BLK), :] for i in range(3)]


def _swa_fwd(h, tk, sink):
    def body(sink_ref, q_ref, kv_ref, tk_ref, y_ref, kexp, vexp, bias):
        n = pl.program_id(1)

        @pl.when(n == 0)
        def _():
            _swa_pad_kv(kv_ref, tk_ref, kexp, vexp)
            _swa_bias_tables(bias)

        for t in range(SWA_PER):
            blk = n * SWA_PER + t
            rows = slice(t * SWA_BLK, (t + 1) * SWA_BLK)
            r0 = pl.multiple_of(blk * SWA_BLK, SWA_BLK)
            q = _rope(q_ref[rows, :], _swa_qtab(tk_ref, r0)) * 0.125
            for hk in range(2):
                _, p, _, slot, _ = _swa_probs(q[:, hk * 256:(hk + 1) * 256], kexp[hk, pl.ds(r0, 3 * SWA_BLK), :],
                                              _swa_bias(bias, blk), sink_ref, hk)
                o4 = _mm(p, vexp[hk, pl.ds(r0, 3 * SWA_BLK), :])
                o = jnp.zeros((SWA_BLK, 256), F32)
                for g in range(4):
                    o = o + jnp.where(slot == g, o4[g * SWA_BLK:(g + 1) * SWA_BLK], 0.0)
                y_ref[rows, hk * 256:(hk + 1) * 256] = o.astype(MX)

    tm = SWA_PER * SWA_BLK
    return pl.pallas_call(
        body,
        grid_spec=pltpu.PrefetchScalarGridSpec(
            num_scalar_prefetch=1, grid=(NSEQ, L // tm),
            in_specs=[pl.BlockSpec((tm, 512), lambda s, n, sk: (s * (L // tm) + n, 2)),
                      pl.BlockSpec((L, 256), lambda s, n, sk: (s, 6)),
                      pl.BlockSpec((3, L, 128), lambda s, n, sk: (0, 0, 0))],
            out_specs=pl.BlockSpec((tm, 512), lambda s, n, sk: (s * (L // tm) + n, 0)),
            scratch_shapes=[pltpu.VMEM((2, L + 2 * SWA_BLK, 256), F32), pltpu.VMEM((2, L + 2 * SWA_BLK, 256), F32),
                            pltpu.VMEM((3, SWA_BLK, 3 * SWA_BLK), F32)]),
        out_shape=_sds((N, 512), MX), name="swa_fwd", compiler_params=_cp(("arbitrary", "arbitrary")))(sink, h, h, tk)


def _swa_bwd(h, tk, sink, dyc):
    tm = SWA_PER * SWA_BLK

    def body(sink_ref, q_ref, kv_ref, tk_ref, dy_ref, dq_ref, dkv_ref, dsink_ref, kexp_all, vexp_all, dkacc, dvacc, bias):
        sq = pl.program_id(0)
        n = pl.program_id(1)

        @pl.when(n == 0)
        def _():
            _swa_pad_kv(kv_ref, tk_ref, kexp_all, vexp_all)
            _swa_bias_tables(bias)
            dkacc[...] = jnp.zeros_like(dkacc)
            dvacc[...] = jnp.zeros_like(dvacc)

        @pl.when((n == 0) & (sq == 0))
        def _():
            dsink_ref[...] = jnp.zeros_like(dsink_ref)

        hrow = lax.broadcasted_iota(jnp.int32, (8, 128), 0)
        dsk = jnp.zeros((8, 128), F32)
        for t in range(SWA_PER):
            blk = n * SWA_PER + t
            rows = slice(t * SWA_BLK, (t + 1) * SWA_BLK)
            r0 = pl.multiple_of(blk * SWA_BLK, SWA_BLK)
            tq = _swa_qtab(tk_ref, r0)
            q = _rope(q_ref[rows, :], tq) * 0.125
            band = _swa_bias(bias, blk)
            for hk in range(2):
                kexp = kexp_all[hk, pl.ds(r0, 3 * SWA_BLK), :]
                vexp = vexp_all[hk, pl.ds(r0, 3 * SWA_BLK), :]
                qs, p, ps, slot, rowg = _swa_probs(q[:, hk * 256:(hk + 1) * 256], kexp, band, sink_ref, hk)
                dy2 = dy_ref[rows, hk * 256:(hk + 1) * 256]
                dos = jnp.concatenate([jnp.where(slot == g, dy2, 0.0) for g in range(4)], axis=0)
                dp = _mm_nt(dos, vexp)
                delta = jnp.sum(p * dp, axis=-1, keepdims=True)
                ds = p * (dp - delta)
                dsr = -ps * delta
                for g in range(4):
                    dsk = dsk + jnp.where(hrow == hk * 4 + g,
                                          jnp.sum(jnp.where(rowg == g, dsr, 0.0), axis=0, keepdims=True), 0.0)
                dq4 = _mm(ds, kexp)
                dq2 = jnp.zeros((SWA_BLK, 256), F32)
                for g in range(4):
                    dq2 = dq2 + jnp.where(slot == g, dq4[g * SWA_BLK:(g + 1) * SWA_BLK], 0.0)
                dq_ref[rows, hk * 256:(hk + 1) * 256] = _rope_t(dq2 * 0.125, tq).astype(MX)
                dkacc[hk, pl.ds(r0, 3 * SWA_BLK), :] += _mm_tn(ds, qs)
                dvacc[hk, pl.ds(r0, 3 * SWA_BLK), :] += _mm_tn(p, dos)
        dsink_ref[...] += dsk

        @pl.when(n == L // tm - 1)
        def _():
            seq = slice(SWA_BLK, SWA_BLK + L)
            dk = _rope_t(_swa_fold(dkacc[0, seq], 0) + _swa_fold(dkacc[1, seq], 1), tk_ref[...])
            dkv_ref[:, 0:128] = dk.astype(MX)
            dkv_ref[:, 128:256] = (_swa_fold(dvacc[0, seq], 0) + _swa_fold(dvacc[1, seq], 1)).astype(MX)

    blk = lambda col: pl.BlockSpec((tm, 512), lambda s, n, sk: (s * (L // tm) + n, col))
    pad = pltpu.VMEM((2, L + 2 * SWA_BLK, 256), F32)
    return pl.pallas_call(
        body,
        grid_spec=pltpu.PrefetchScalarGridSpec(
            num_scalar_prefetch=1, grid=(NSEQ, L // tm),
            in_specs=[blk(2), pl.BlockSpec((L, 256), lambda s, n, sk: (s, 6)),
                      pl.BlockSpec((3, L, 128), lambda s, n, sk: (0, 0, 0)), blk(0)],
            out_specs=[blk(0), pl.BlockSpec((L, 256), lambda s, n, sk: (s, 0)),
                       pl.BlockSpec((8, 128), lambda s, n, sk: (0, 0))],
            scratch_shapes=[pad, pad, pad, pad, pltpu.VMEM((3, SWA_BLK, 3 * SWA_BLK), F32)]),
        out_shape=[_sds((N, 512), MX), _sds((N, 256), MX), _sds((8, 128))],
        name="swa_bwd", compiler_params=_cp(("arbitrary", "arbitrary")))(sink, h, h, tk, dyc)


def _outproj_bwd(dx1, s1, ya, yb, yc, wo, g):
    tm = 512
    nt = N // tm

    def body(dx1_ref, s_ref, ya_ref, yb_ref, yc_ref, wo_ref, g_ref,
             dya_ref, dyb_ref, dyc_ref, dxp_ref, dwo_ref, dg_ref, db_ref, acc):
        i = pl.program_id(0)

        @pl.when(i == 0)
        def _():
            acc[...] = jnp.zeros_like(acc)
            dg_ref[...] = jnp.zeros_like(dg_ref)
            db_ref[...] = jnp.zeros_like(db_ref)

        ds, dg, db = _ln_bwd(dx1_ref[...], s_ref[...], g_ref[...])
        dg_ref[...] += dg
        db_ref[...] += db
        dxp_ref[...] = ALPHA * ds
        dy = _mm_nt(ds, wo_ref[...])
        dya_ref[...] = dy[:, 0:256]
        dyb_ref[...] = dy[:, 256:512]
        dyc_ref[...] = dy[:, 512:1024].astype(MX)
        acc[0:256] += _mm_tn(ya_ref[...], ds)
        acc[256:512] += _mm_tn(yb_ref[...], ds)
        acc[512:1024] += _mm_tn(yc_ref[...], ds)

        @pl.when(i == nt - 1)
        def _():
            dwo_ref[...] = acc[...].astype(MX)

    row = lambda w_: pl.BlockSpec((tm, w_), lambda i: (i, 0))
    one = pl.BlockSpec((1, D), lambda i: (0, 0))
    full = pl.BlockSpec((D, D), lambda i: (0, 0))
    return pl.pallas_call(
        body, grid=(nt,),
        in_specs=[row(D), row(D), row(256), row(256), row(512), full, one],
        out_specs=[row(256), row(256), row(512), row(D), full, one, one],
        out_shape=[_sds((N, 256)), _sds((N, 256)), _sds((N, 512), MX), _sds((N, D)), _sds((D, D), MX), _sds((1, D)),
                   _sds((1, D))],
        scratch_shapes=[pltpu.VMEM((D, D), F32)],
        name="outproj_bwd", compiler_params=_cp(("arbitrary",)))(dx1, s1, ya, yb, yc, wo, g)


def _mix_ffn_fwd(ya, yb, yc, x, wo, g1, b1, w1, w2, g, b, target=None):
    tm = FFN_TM
    head = target is not None

    def body(*refs):
        ya_ref, yb_ref, yc_ref, xin_ref, wo_ref, g1_ref, b1_ref, w1_ref, w2_ref, g_ref, b_ref = refs[:11]
        s1_ref, x1_ref, a_ref, s_ref, y_ref = refs[11 + head:16 + head]
        mix = _mm(ya_ref[...], wo_ref[0:256]) + _mm(yb_ref[...], wo_ref[256:512]) + _mm(yc_ref[...], wo_ref[512:1024])
        s1 = ALPHA * xin_ref[...] + mix
        s1_ref[...] = s1
        x = _ln_fwd(s1, g1_ref[...], b1_ref[...])
        x1_ref[...] = x
        xb = x.astype(MX)
        s = ALPHA * x
        for j in range(NSHARD):
            a = _mm(xb, w1_ref[j])
            a_ref[:, j * D:(j + 1) * D] = a.astype(MX)
            s = s + _mm(jnp.square(jnp.maximum(a, 0.0)), w2_ref[j])
        s_ref[...] = s
        x2 = _ln_fwd(s, g_ref[...], b_ref[...])
        if not head:
            y_ref[...] = x2
            return
        l_ref = refs[16 + head]

        @pl.when(pl.program_id(0) == 0)
        def _():
            l_ref[...] = jnp.zeros_like(l_ref)

        e = x2 - refs[11][...]
        y_ref[...] = e * (1.0 / D)
        l_ref[...] += jnp.sum(jnp.sum(e * e, axis=1, keepdims=True), axis=0, keepdims=True) * (0.5 / D)

    rw = lambda w_: pl.BlockSpec((tm, w_), lambda i: (i, 0))
    row = rw(D)
    once = dict(pipeline_mode=pl.Buffered(1))
    wall = pl.BlockSpec((NSHARD, D, D), lambda i: (0, 0, 0), **once)
    one = pl.BlockSpec((1, D), lambda i: (0, 0))
    acc = pl.BlockSpec((8, 128), lambda i: (0, 0))
    return pl.pallas_call(
        body, grid=(N // tm,),
        in_specs=[rw(256), rw(256), rw(512), row, pl.BlockSpec((D, D), lambda i: (0, 0), **once), one, one,
                  wall, wall, one, one] + [row] * head,
        out_specs=[row, row, pl.BlockSpec((tm, DFF), lambda i: (i, 0)), row, row] + [acc] * head,
        out_shape=[_sds((N, D)), _sds((N, D)), _sds((N, DFF), MX), _sds((N, D)), _sds((N, D))] + [_sds((8, 128))] * head,
        name="mix_ffn_fwd", compiler_params=_cp(("arbitrary",), FFN_VMEM))(
            ya, yb, yc, x, wo, g1, b1, w1, w2, g, b, *([target] * head))


def _ffn_bwd_act(dy, s2, a, w1, w2, g):
    tm = FFN_TM

    def body(dy_ref, s_ref, a_ref, w1_ref, w2_ref, g_ref, da_ref, ds_ref, dx1_ref, dg_ref, db_ref):
        @pl.when(pl.program_id(0) == 0)
        def _():
            dg_ref[...] = jnp.zeros_like(dg_ref)
            db_ref[...] = jnp.zeros_like(db_ref)

        ds, dg, db = _ln_bwd(dy_ref[...], s_ref[...], g_ref[...])
        dsb = ds.astype(MX)
        ds_ref[...] = dsb
        dg_ref[...] += dg
        db_ref[...] += db
        dx1 = ALPHA * ds
        for j in range(NSHARD):
            da = (_mm_nt(dsb, w2_ref[j]) * 2.0 * jnp.maximum(a_ref[:, j * D:(j + 1) * D].astype(F32), 0.0)).astype(MX)
            da_ref[:, j * D:(j + 1) * D] = da
            dx1 = dx1 + _mm_nt(da, w1_ref[j])
        dx1_ref[...] = dx1

    row = pl.BlockSpec((tm, D), lambda i: (i, 0))
    wide = pl.BlockSpec((tm, DFF), lambda i: (i, 0))
    wall = pl.BlockSpec((NSHARD, D, D), lambda i: (0, 0, 0))
    one = pl.BlockSpec((1, D), lambda i: (0, 0))
    return pl.pallas_call(
        body, grid=(N // tm,),
        in_specs=[row, row, wide, wall, wall, one],
        out_specs=[wide, row, row, one, one],
        out_shape=[_sds((N, DFF), MX), _sds((N, D), MX), _sds((N, D)), _sds((1, D)), _sds((1, D))],
        name="ffn_bwd_act", compiler_params=_cp(("arbitrary",), FFN_VMEM))(dy, s2, a, w1, w2, g)


def _ffn_bwd_w(x1, da, a, ds):
    tm, nb = FFN_TM_W, FFN_WB
    nt = N // tm

    def body(x_ref, da_ref, a_ref, ds_ref, dw1_ref, dw2_ref, acc1, acc2):
        i = pl.program_id(1)

        @pl.when(i == 0)
        def _():
            acc1[...] = jnp.zeros_like(acc1)
            acc2[...] = jnp.zeros_like(acc2)

        x, ds_ = x_ref[...], ds_ref[...]
        for k in range(nb):
            cols = slice(k * D, (k + 1) * D)
            acc1[k] += _mm_tn(x, da_ref[:, cols])
            acc2[k] += _mm_tn(jnp.square(jnp.maximum(a_ref[:, cols].astype(F32), 0.0)), ds_)

        @pl.when(i == nt - 1)
        def _():
            dw1_ref[...] = acc1[...].astype(MX)
            dw2_ref[...] = acc2[...].astype(MX)

    row = pl.BlockSpec((tm, D), lambda j, i: (i, 0))
    col = pl.BlockSpec((tm, nb * D), lambda j, i: (i, j))
    wj = pl.BlockSpec((nb, D, D), lambda j, i: (j, 0, 0))
    return pl.pallas_call(
        body, grid=(NSHARD // nb, nt),
        in_specs=[row, col, col, row], out_specs=[wj, wj],
        out_shape=[_sds((NSHARD, D, D), MX), _sds((NSHARD, D, D), MX)],
        scratch_shapes=[pltpu.VMEM((nb, D, D), F32), pltpu.VMEM((nb, D, D), F32)],
        name="ffn_bwd_w", compiler_params=_cp(("parallel", "arbitrary"), FFN_VMEM))(x1, da, a, ds)


def _s5_discretize(a_re, a_im, log_step, b_re, b_im):
    lam = lax.complex(a_re, a_im)
    lam_bar = jnp.exp(lam * jnp.exp(log_step))
    b_bar = ((lam_bar - 1.0) / lam)[..., None] * lax.complex(b_re, b_im)
    return jnp.real(lam_bar), jnp.imag(lam_bar), jnp.real(b_bar), jnp.imag(b_bar)


def _s5_in_blocks(b):
    e = jnp.eye(8, dtype=F32)
    return jnp.einsum('ij,zbjph->zbihjp', e, b.reshape(2, 2, 8, S5_P, S5_H)).reshape(2, 2, 128, SW)


def _s5_out_blocks(c):
    e = jnp.eye(8, dtype=F32)
    return jnp.einsum('ij,zbjhp->zbjpih', e, c.reshape(2, 2, 8, S5_H, S5_P)).reshape(2, 2, SW, 128)


def _gate_weight(w_a):
    z = jnp.zeros((16, 128), F32)
    top = jnp.concatenate([w_a[0], z], axis=1)
    bot = jnp.concatenate([z, w_a[1]], axis=1)
    return jnp.concatenate([top, bot, jnp.zeros((96, 256), F32)], axis=0)


def _layer_prep(p):
    lr, li, br, bi = _s5_discretize(p["s5_a_re"], p["s5_a_im"], p["s5_log_step"], p["s5_b_re"], p["s5_b_im"])
    q = dict(p)
    q["bre"] = _s5_in_blocks(br).astype(MX)
    q["bim"] = _s5_in_blocks(bi).astype(MX)
    q["cre"] = _s5_out_blocks(p["s5_c_re"]).astype(MX)
    q["cim"] = _s5_out_blocks(p["s5_c_im"]).astype(MX)
    mr, mi = lr.reshape(2, 1024), li.reshape(2, 1024)
    q["tab"], q["tabc"] = _lockstep_tables(mr, mi)
    q["dsk"] = p["s5_d"].reshape(1, 256)
    q["wa"] = _gate_weight(p["gla_w_a"]).astype(MX)
    q["ba"] = p["gla_b_a"].reshape(1, 256)
    q["lng"] = p["gla_ln_g"].reshape(1, 256)
    q["bv"] = p["s5_b_glu"][:256].reshape(1, 256)
    q["bg"] = p["s5_b_glu"][256:].reshape(1, 256)
    for k in ("ln1_g", "ln1_b", "ln2_g", "ln2_b"):
        q[k] = p[k].reshape(1, D)
    return q


def _layer_fwd(x, q, tk, fetch, target=None):
    q["w_in"] = fetch("w_in", x)
    h, la2 = _inproj_fwd(x, q["w_in"], q["wa"], q["ba"])
    hre, him, y2 = _s5_fwd(h, q["bre"], q["bim"], q["cre"], q["cim"], q["tab"])
    q["w4"] = fetch("s5_w_glu", y2)
    ya = _s5_glu_fwd(y2, h, q["dsk"], q["w4"], q["bv"], q["bg"])
    of, ob, sf, sb = _gla_fwd(h, la2)
    yb = _gla_post_fwd(of, ob, h, q["lng"])
    yc = _swa_fwd(h, tk, q["swa_sink"])
    mixed = ya[:8, :128] + yb[:8, :128] + yc[:8, :128]
    q["w_out"] = fetch("w_out", mixed)
    q["w_ff1"] = fetch("w_ff1", mixed)
    q["w_ff2"] = fetch("w_ff2", mixed)
    s1, x1, a, s2, *out = _mix_ffn_fwd(ya, yb, yc, x, q["w_out"], q["ln1_g"], q["ln1_b"], q["w_ff1"], q["w_ff2"],
                                       q["ln2_g"], q["ln2_b"], target)
    saved = dict(x=x, h=h, hre=hre, him=him, y2=y2, ya=ya, la2=la2, of=of, ob=ob, sf=sf, sb=sb, yb=yb, yc=yc,
                 s1=s1, x1=x1, a=a, s2=s2)
    return (out[0] if target is None else tuple(out)), saved


def _layer_bwd(dy, q, sv, tk, emit):
    g = {}
    da, ds2, dx1, g["dg2"], g["db2"] = _ffn_bwd_act(dy, sv["s2"], sv["a"], q["w_ff1"], q["w_ff2"], q["ln2_g"])
    dw1, dw2 = _ffn_bwd_w(sv["x1"], da, sv["a"], ds2)
    tie = emit(dict(w_ff1=dw1, w_ff2=dw2))
    dya, dyb, dyc, dxp, dwo, g["dg1"], g["db1"] = _outproj_bwd(dx1, sv["s1"], sv["ya"], sv["yb"], sv["yc"],
                                                               q["w_out"], q["ln1_g"] + tie)
    h = sv["h"]
    daq, dakv, g["dsink"] = _swa_bwd(h, tk, q["swa_sink"], dyc)
    do, gr, g["dlng"] = _gla_post_bwd(sv["of"], sv["ob"], h, q["lng"], dyb)
    gq_f, gk_f, gv_f, gl_f, gq_b, gk_b, gv_b, gl_b = _gla_bwd(h, sv["la2"], do, sv["sf"], sv["sb"])
    dyp, dud, g["dd"], dw4, g["dbv"], g["dbg"] = _s5_glu_bwd(sv["y2"], h, q["dsk"], q["w4"], q["bv"], q["bg"], dya)
    tie = emit(dict(w_out=dwo.reshape(NSHARD, D // NSHARD, D), s5_w_glu=dw4))
    du2, g["dbre"], g["dbim"], g["dcre"], g["dcim"], g["dmu"] = _s5_bwd(
        h, dyp, sv["hre"], sv["him"], q["bre"], q["bim"], q["cre"], q["cim"], (q["tabc"][0], q["tabc"][1] + tie))
    dx, dwt, g["dwa"], g["dba"] = _inproj_bwd(sv["x"], q["w_in"], dxp, du2, dud, gq_f, gq_b, gk_f, gk_b, gv_f, gv_b, gr,
                                              daq, dakv, h, q["wa"], q["ba"], gl_f, gl_b)
    tie = emit(dict(w_in=dwt))
    return dx, g, tie


NATIVE = ("dmu", "dbre", "dbim", "dcre", "dcim", "dd", "dbv", "dbg", "dwa", "dba", "dlng", "dsink",
          "dg1", "db1", "dg2", "db2", "loss")
ICI_CORE = (0, 0, 0, 1, 1, 0, 0, 0, 1, 1, 1, 1, 0, 0, 1, 1, 0)
Y_FIRST = (0, 0, 1, 0, 1, 1, 0, 1, 0, 1, 0, 1, 0, 1, 0, 1, 0)


def _finish_small(n, w):
    g = {}
    dmu = n["dmu"]
    dlr = dmu[:, :, :, 0].reshape(DEPTH, 2, S5_G, S5_P)
    dli = dmu[:, :, :, 1].reshape(DEPTH, 2, S5_G, S5_P)

    def unblock(c, perm, shape):
        return c.reshape(DEPTH, 2, 2, S5_H, 8, S5_P).transpose(perm).reshape(shape)

    b_shape, c_shape = (DEPTH, 2, S5_G, S5_P, S5_H), (DEPTH, 2, S5_G, S5_H, S5_P)
    _, vjp = jax.vjp(_s5_discretize, w["s5_a_re"], w["s5_a_im"], w["s5_log_step"], w["s5_b_re"], w["s5_b_im"])
    (g["s5_a_re"], g["s5_a_im"], g["s5_log_step"], g["s5_b_re"], g["s5_b_im"]) = vjp(
        (dlr, dli, unblock(n["dbre"], (0, 1, 2, 4, 5, 3), b_shape), unblock(n["dbim"], (0, 1, 2, 4, 5, 3), b_shape)))
    g["s5_c_re"] = unblock(n["dcre"], (0, 1, 2, 4, 3, 5), c_shape)
    g["s5_c_im"] = unblock(n["dcim"], (0, 1, 2, 4, 3, 5), c_shape)
    g["s5_d"] = n["dd"].reshape(DEPTH, S5_G, S5_H)
    g["s5_b_glu"] = jnp.concatenate([n["dbv"], n["dbg"]], axis=2).reshape(DEPTH, 512)
    g["gla_w_a"] = jnp.stack([n["dwa"][:, 0:16, 0:128], n["dwa"][:, 16:32, 128:256]], axis=1)
    g["gla_b_a"] = n["dba"].reshape(DEPTH, 2, 128)
    g["gla_ln_g"] = n["dlng"].reshape(DEPTH, 256)
    g["swa_sink"] = n["dsink"][:, :, 0]
    for k, s in (("ln1_g", "dg1"), ("ln1_b", "db1"), ("ln2_g", "dg2"), ("ln2_b", "db2")):
        g[k] = n[s].reshape(DEPTH, D)
    return g


def _local_step(x, target, qs, tk, fetch, emit):
    saved = []
    for l, q in enumerate(qs):
        x, sv = _layer_fwd(x, q, tk, functools.partial(fetch, l), target if l == DEPTH - 1 else None)
        saved.append(sv)
    dy, lacc = x
    smalls = [None] * DEPTH
    tie = 0.0
    for l in reversed(range(DEPTH)):
        qs[l]["ln2_g"] = qs[l]["ln2_g"] + tie
        dy, smalls[l], tie = _layer_bwd(dy, qs[l], saved[l], tk, functools.partial(emit, l))
    smalls[0]["db2"] = smalls[0]["db2"] + tie
    for l in range(DEPTH):
        smalls[l]["loss"] = lacc if l == 0 else jnp.zeros_like(lacc)
    return lacc[0, 0], dy, smalls


BIG = ("w_in", "s5_w_glu", "w_out", "w_ff1", "w_ff2")
SMALL = ("s5_a_re", "s5_a_im", "s5_log_step", "s5_b_re", "s5_b_im", "s5_c_re", "s5_c_im", "s5_d", "s5_b_glu",
         "gla_w_a", "gla_b_a", "gla_ln_g", "swa_sink", "ln1_g", "ln1_b", "ln2_g", "ln2_b")
ANY = pl.BlockSpec(memory_space=pl.ANY)


def _place():
    x, y, c = lax.axis_index("x"), lax.axis_index("y"), lax.axis_index("c")
    return x, y, c, [(1 - x, y), (x, 1 - y), (1 - x, 1 - y)]


HBM = pl.BlockSpec(memory_space=pltpu.HBM)
SEMS = pl.BlockSpec(memory_space=pltpu.SEMAPHORE)
EFFECT = pltpu.SideEffectType.DATAFLOW_SIDE_EFFECTING


def _push_copies(ins, lands, send, recv, gather, sending):
    x, y, c, chips = _place()
    me = 2 * x + y
    if gather == "sibling":
        return [pltpu.make_async_remote_copy(src_ref=ins[a], dst_ref=lands[a], send_sem=send.at[a], recv_sem=recv.at[a],
                                             device_id=(x, y, 1 - c), device_id_type=MESH) for a in range(len(lands))]
    out = []
    for a in range(len(lands)):
        for j, (px, py) in enumerate(chips):
            peer = 2 * px + py
            src = lands[a].at[me] if gather else ins[a].at[peer if sending else me]
            dst = lands[a].at[me if sending else peer]
            out.append(pltpu.make_async_remote_copy(src_ref=src, dst_ref=dst, send_sem=send.at[3 * a + j],
                                                    recv_sem=recv.at[3 * a + j], device_id=(px, py, c),
                                                    device_id_type=MESH))
    return out


def _push_start(name, arrs, gather):
    n = len(arrs)
    ops = list(arrs) if gather is True else list(arrs) + [lax.empty(s.shape, s.dtype) for s in arrs]
    m = len(ops)

    def body(*refs):
        ins, lnd = (refs[:n], refs[:n]) if gather is True else (refs[:n], refs[n:m])
        for cp in _push_copies(ins, lnd, refs[m], refs[m + 1], gather, True):
            cp.start()
        refs[-1][...] = jnp.zeros((8, 128), F32)

    ops = [pltpu.with_memory_space_constraint(t, pltpu.HBM) for t in ops]
    res = pl.pallas_call(
        body, name=name,
        out_shape=(pltpu.SemaphoreType.DMA((3 * n,)), pltpu.SemaphoreType.DMA((3 * n,)),
                   *[pltpu.HBM(t.shape, t.dtype) for t in ops], _sds((8, 128))),
        in_specs=[HBM] * m,
        out_specs=(SEMS, SEMS, *[HBM] * m, pl.BlockSpec(memory_space=pltpu.VMEM)),
        input_output_aliases={i: 2 + i for i in range(m)},
        compiler_params=pltpu.CompilerParams(has_side_effects=EFFECT))(*ops)
    return res[0], res[1], list(res[2:2 + m]), res[-1]


def _push_wait(name, started, after, gather):
    send, recv, ops, _ = started
    m = len(ops)
    n = m if gather is True else m // 2

    def body(*refs):
        ins, lnd = (refs[:n], refs[:n]) if gather is True else (refs[:n], refs[n:m])
        for cp in _push_copies(ins, lnd, refs[m], refs[m + 1], gather, False):
            cp.wait_send()
            cp.wait_recv()

    res = pl.pallas_call(
        body, name=name,
        out_shape=[pltpu.HBM(t.shape, t.dtype) for t in ops],
        in_specs=[HBM] * m + [SEMS, SEMS, ANY], out_specs=[HBM] * m,
        input_output_aliases={i: i for i in range(m)},
        compiler_params=pltpu.CompilerParams(has_side_effects=EFFECT))(*ops, send, recv, after)
    return list(res)


def _row_tile(rows):
    return max(t for t in range(8, min(rows, 512) + 1, 8) if rows % t == 0)


def _cast_to_slot(me, w, l):
    _, rows, cols = w.shape
    tr = _row_tile(rows)

    def body(me_ref, w_ref, o_ref):
        o_ref[0] = w_ref[0].astype(MX)

    return pl.pallas_call(
        body,
        grid_spec=pltpu.PrefetchScalarGridSpec(
            num_scalar_prefetch=1, grid=(rows // tr,),
            in_specs=[pl.BlockSpec((1, tr, cols), lambda i, me_: (l, i, 0))],
            out_specs=pl.BlockSpec((1, tr, cols), lambda i, me_: (me_[0], i, 0))),
        out_shape=_sds((NSHARD, rows, cols), MX), name="cast_to_slot", compiler_params=_cp(("arbitrary",)))(me, w)


def _sum_sources(me, recv, own):
    _, rows, cols = recv[0].shape
    tr = min(_row_tile(rows), 256) if rows % 256 == 0 else _row_tile(rows)
    nt = rows // tr

    def body(me_ref, *refs):
        o_ref = refs[-1]
        for l in range(DEPTH):
            @pl.when(pl.program_id(0) == l)
            def _():
                r_ref, own_ref = refs[2 * l], refs[2 * l + 1]
                part = [jnp.where(me_ref[0] == s, own_ref[0], r_ref[s]).astype(F32) for s in range(NSHARD)]
                o_ref[...] = ((part[0] + part[1]) + part[2]) + part[3]

    in_specs = []
    for l in range(DEPTH):
        pick = lambda g, i, me_, l=l: jnp.where(g == l, i, jnp.where(g < l, 0, nt - 1))
        in_specs += [pl.BlockSpec((NSHARD, tr, cols), lambda g, i, me_, pick=pick: (0, pick(g, i, me_), 0)),
                     pl.BlockSpec((1, tr, cols), lambda g, i, me_, pick=pick: (me_[0], pick(g, i, me_), 0))]
    return pl.pallas_call(
        body,
        grid_spec=pltpu.PrefetchScalarGridSpec(
            num_scalar_prefetch=1, grid=(DEPTH, nt), in_specs=in_specs,
            out_specs=pl.BlockSpec((tr, cols), lambda g, i, me_: (g * nt + i, 0))),
        out_shape=_sds((DEPTH * rows, cols)), name="sum_sources",
        compiler_params=_cp(("arbitrary", "arbitrary")))(me, *[t for l in range(DEPTH) for t in (recv[l], own[l])])


def _allreduce_small(per_layer):
    nk = len(per_layer[0])
    n = DEPTH * nk
    shapes = [a.shape for a in per_layer[0]]

    def body(*refs):
        ins, outs = refs[:n], refs[n:n + nk]
        sibs, slots = refs[n + nk:n + 2 * nk], refs[n + 2 * nk:n + 3 * nk]
        send, recv = refs[n + 3 * nk:]
        x, y, c, chips = _place()
        me = 2 * x + y
        d2d = [pltpu.make_async_remote_copy(src_ref=ins[l * nk + k], dst_ref=sibs[k].at[l], send_sem=send.at[l * nk + k],
                                            recv_sem=recv.at[l * nk + k], device_id=(x, y, 1 - c), device_id_type=MESH)
               for l in range(DEPTH) for k in range(nk)]
        for cp in d2d:
            cp.start()
        for cp in d2d:
            cp.wait()
        for l in range(DEPTH):
            for k in range(nk):
                slots[k][0, l] = ins[l * nk + k][...] + sibs[k][l]

        def swap(k, stage):
            peer = (1 - x, y, c) if stage == Y_FIRST[k] else (x, 1 - y, c)
            return pltpu.make_async_remote_copy(src_ref=slots[k].at[2 * stage], dst_ref=slots[k].at[2 * stage + 1],
                                                send_sem=send.at[n + 3 * k + stage], recv_sem=recv.at[n + 3 * k + stage],
                                                device_id=peer, device_id_type=MESH)

        def handover(k):
            return pltpu.make_async_remote_copy(src_ref=outs[k], dst_ref=outs[k], send_sem=send.at[n + 3 * nk + k],
                                                recv_sem=recv.at[n + 3 * nk + k], device_id=(x, y, 1 - c),
                                                device_id_type=MESH)

        halves = (tuple(k for k in range(nk) if ICI_CORE[k] == 0), tuple(k for k in range(nk) if ICI_CORE[k] == 1))
        for cc in range(2):
            @pl.when(c == cc)
            def _():
                mine, theirs = halves[cc], halves[1 - cc]
                for stage in range(2):
                    cps = [swap(k, stage) for k in mine]
                    for cp in cps:
                        cp.start()
                    for cp in cps:
                        cp.wait()
                    for k in mine:
                        if stage == 0:
                            slots[k][2] = slots[k][0] + slots[k][1]
                        else:
                            outs[k][...] = slots[k][2] + slots[k][3]
                over = [handover(k) for k in mine]
                for cp in over:
                    cp.start()
                for k in theirs:
                    handover(k).wait_recv()
                for cp in over:
                    cp.wait_send()

    vm = pl.BlockSpec(memory_space=pltpu.VMEM)
    return pl.pallas_call(
        body, in_specs=[vm] * n, out_specs=[vm] * nk, out_shape=[_sds((DEPTH,) + s) for s in shapes],
        scratch_shapes=([pltpu.VMEM((DEPTH,) + s, F32) for s in shapes]
                        + [pltpu.VMEM((NSHARD, DEPTH) + s, F32) for s in shapes]
                        + [pltpu.SemaphoreType.DMA((n + 4 * nk,)), pltpu.SemaphoreType.DMA((n + 4 * nk,))]),
        name="allreduce_small", compiler_params=pltpu.CompilerParams(vmem_limit_bytes=VMEM_LIMIT))(
            *[a for layer in per_layer for a in layer])


def _adamw_math(w, g, m, v):
    m = ADAM_B1 * m + (1.0 - ADAM_B1) * g
    v = ADAM_B2 * v + (1.0 - ADAM_B2) * jnp.square(g)
    m_hat = m / (1.0 - ADAM_B1 ** ADAM_STEP)
    v_hat = v / (1.0 - ADAM_B2 ** ADAM_STEP)
    delta = -ADAM_LR * (m_hat / (jnp.sqrt(v_hat) + ADAM_EPS) + ADAM_WD * w)
    return delta, m, v


def _adamw(g_parts, w, m, v):
    rows, cols = w.shape
    tr = 256 if rows % 256 == 0 else _row_tile(rows)
    k = len(g_parts)

    def body(*refs):
        g = refs[0][...]
        for r in refs[1:k]:
            g = g + r[...]
        w_ref, m_ref, v_ref, go, do, mo, vo = refs[k:]
        d, mn, vn = _adamw_math(w_ref[...], g, m_ref[...], v_ref[...])
        go[...] = g
        do[...] = d
        mo[...] = mn
        vo[...] = vn

    spec = pl.BlockSpec((tr, cols), lambda i: (i, 0))
    return pl.pallas_call(
        body, grid=(rows // tr,), in_specs=[spec] * (k + 3), out_specs=[spec] * 4,
        out_shape=[_sds((rows, cols))] * 4, name="adamw", compiler_params=_cp(("parallel",)))(*g_parts, w, m, v)


def _adamw_small(gs, ws, ms, vs):
    n = len(gs)

    def body(*refs):
        for k in range(n):
            d, mn, vn = _adamw_math(refs[n + k][...], refs[k][...], refs[2 * n + k][...], refs[3 * n + k][...])
            refs[4 * n + k][...] = d
            refs[5 * n + k][...] = mn
            refs[6 * n + k][...] = vn

    vm = pl.BlockSpec(memory_space=pltpu.VMEM)
    shapes = [_sds(a.shape) for a in ws]
    res = pl.pallas_call(
        body, in_specs=[vm] * (4 * n), out_specs=[vm] * (3 * n), out_shape=shapes * 3, name="adamw_small",
        compiler_params=pltpu.CompilerParams(vmem_limit_bytes=VMEM_LIMIT))(*gs, *ws, *ms, *vs)
    return res[:n], res[n:2 * n], res[2 * n:]


_ARGS = ("x", "w_in", "s5_a_re", "s5_a_im", "s5_log_step", "s5_b_re", "s5_b_im", "s5_c_re", "s5_c_im", "s5_d",
         "s5_w_glu", "s5_b_glu", "gla_w_a", "gla_b_a", "gla_ln_g", "swa_sink", "w_out", "ln1_g", "ln1_b", "w_ff1",
         "w_ff2", "ln2_g", "ln2_b")
_WEIGHTS = _ARGS[1:]


def kernel(x, w_in, s5_a_re, s5_a_im, s5_log_step, s5_b_re, s5_b_im, s5_c_re, s5_c_im, s5_d, s5_w_glu, s5_b_glu, gla_w_a, gla_b_a, gla_ln_g, swa_sink, w_out, ln1_g, ln1_b, w_ff1, w_ff2, ln2_g, ln2_b, loss_target, m_w_in, m_s5_a_re, m_s5_a_im, m_s5_log_step, m_s5_b_re, m_s5_b_im, m_s5_c_re, m_s5_c_im, m_s5_d, m_s5_w_glu, m_s5_b_glu, m_gla_w_a, m_gla_b_a, m_gla_ln_g, m_swa_sink, m_w_out, m_ln1_g, m_ln1_b, m_w_ff1, m_w_ff2, m_ln2_g, m_ln2_b, v_w_in, v_s5_a_re, v_s5_a_im, v_s5_log_step, v_s5_b_re, v_s5_b_im, v_s5_c_re, v_s5_c_im, v_s5_d, v_s5_w_glu, v_s5_b_glu, v_gla_w_a, v_gla_b_a, v_gla_ln_g, v_swa_sink, v_w_out, v_ln1_g, v_ln1_b, v_w_ff1, v_w_ff2, v_ln2_g, v_ln2_b):
    given = dict(locals())
    w = {k: given[k] for k in _WEIGHTS}
    mom = {k: given["m_" + k] for k in _WEIGHTS}
    var = {k: given["v_" + k] for k in _WEIGHTS}

    me = (2 * lax.axis_index("x") + lax.axis_index("y")).astype(jnp.int32).reshape(1)
    tr = lambda t: t.transpose(0, 2, 1)
    shard = {k: (tr(w[k]) if k == "w_in" else w[k]) for k in BIG}
    qs = [None] * DEPTH

    first = ("w_in", "s5_w_glu", "w_out")
    follow = {(0, "w_in"): [(0, first[1:]), (0, BIG[3:])], (0, "s5_w_glu"): [(1, first)], (0, "w_ff1"): [(1, BIG[3:])]}
    gathers = {}

    casts = {}

    def start_gather(l, names, behind=None):
        lands = [casts.pop((l, k)) if (l, k) in casts else _cast_to_slot(me, shard[k], l) for k in names]
        if behind is not None:
            lands, behind = lax.optimization_barrier((lands, behind))
        st = _push_start(f"gather_start_{l}_{names[0]}", lands, True)
        for k in names:
            gathers[l, k] = [names, st, None]
        return st[-1], behind

    token = start_gather(0, first[:1])[0]
    zero = token[0, 0]
    for l in range(DEPTH):
        for k in BIG:
            if (l, k) not in gathers:
                casts[l, k] = _cast_to_slot(me, lax.optimization_barrier((shard[k], token))[0], l)
        qs[l] = _layer_prep({k: (w[k][l] + zero if k == "s5_a_re" else w[k][l]) for k in SMALL})
    token, casts, qs = lax.optimization_barrier((token, casts, qs))

    def fetch(l, name, after):
        names, st, got = gathers[l, name]
        tie = None
        if got is None:
            if l == 0 and name == "w_in":
                after = token
            lands = _push_wait(f"gather_wait_{l}_{names[0]}", st, after, True)
            for l2, names2 in follow.get((l, name), ()):
                tok, lands[0] = start_gather(l2, names2, lands[0])
                tie = tok if tie is None else tie + tok
            got = dict(zip(names, lands))
            for k in names:
                gathers[l, k][2] = got
        full = got[name]
        if name == "w_in":
            return _in_rows(full, token if tie is None else tie)
        if tie is not None:
            near = "bv" if name == "s5_w_glu" else "ln2_b"
            qs[l][near] = qs[l][near] + tie[0, 0]
        return full.reshape(D, D) if name == "w_out" else full

    scatters, held = [], {}

    def emit(l, grads):
        if l > 0:
            held.update(grads)
            if "w_in" not in grads:
                return 0.0
            grads = dict(held)
            held.clear()
        names = tuple(grads)
        st = _push_start(f"scatter_start_{l}_{names[0]}", [grads[k] for k in names], False)
        scatters.append((l, names, st))
        return st[-1][0, 0]

    loss, dx, smalls = _local_step(x.reshape(N, D), loss_target.reshape(N, D), qs, _rope_tables(128), fetch, emit)

    out, recv, own = {}, {}, {}

    def collect(keys, after):
        for l, names, st in scatters:
            if names[0] in keys:
                ops = _push_wait(f"scatter_wait_{l}_{names[0]}", st, after, False)
                for i, k in enumerate(names):
                    own[l, k], recv[l, k] = ops[i], ops[len(names) + i]

    def shard_sums(keys):
        return [_sum_sources(me, [recv[l, k] for l in range(DEPTH)], [own[l, k] for l in range(DEPTH)]) for k in keys]

    def to_sibling(keys, sums):
        return _push_start(f"swap_start_{keys[0]}", sums, "sibling")

    def apply(keys, started, after):
        ops = _push_wait(f"swap_wait_{keys[0]}", started, after, "sibling")
        for i, k in enumerate(keys):
            mine, other = ops[i], ops[len(keys) + i]
            shp = shard[k].shape
            r = _adamw([mine, other], *((tr(t[k]) if k == "w_in" else t[k]).reshape(-1, shp[-1]) for t in (w, mom, var)))
            r = [t.reshape(shp) for t in r]
            out[k] = [tr(t) for t in r] if k == "w_in" else r
        return out[keys[-1]][1]

    collect(("w_ff1", "w_ff2", "w_out", "s5_w_glu"), dx)
    sums = shard_sums(("w_ff1", "w_ff2", "w_out", "s5_w_glu"))
    sums, smalls[0]["db1"] = lax.optimization_barrier((sums, smalls[0]["db1"]))
    native = _allreduce_small([[smalls[l][k] for k in NATIVE] for l in range(DEPTH)])
    sums, native = lax.optimization_barrier((sums, native))
    ff = to_sibling(("w_ff1", "w_ff2"), sums[:2])
    mix = to_sibling(("w_out", "s5_w_glu"), sums[2:])
    native = dict(zip(NATIVE, native))
    native["db1"] = native["db1"] + (ff[-1][0, 0] + mix[-1][0, 0])
    loss = native["loss"][0, 0, 0] + native["loss"][1, 0, 0]
    gsmall = _finish_small(native, w)
    view = lambda k, t: t.transpose(0, 1, 2, 4, 3) if k in ("s5_b_re", "s5_b_im") else t
    res = _adamw_small(*([view(k, t[k]) for k in SMALL] for t in (gsmall, w, mom, var)))
    for i, k in enumerate(SMALL):
        out[k] = [gsmall[k]] + [view(k, r[i]) for r in res]
    last = apply(("w_ff1", "w_ff2"), ff, res[0][-1])
    collect(("w_in",), last)
    win = to_sibling(("w_in",), shard_sums(("w_in",)))
    last = apply(("w_out", "s5_w_glu"), mix, win[-1])
    apply(("w_in",), win, last)

    return (loss, dx.reshape(NSEQ, L, D), *[out[k][0] for k in _WEIGHTS], *[out[k][1] for k in _WEIGHTS],
            *[out[k][2] for k in _WEIGHTS], *[out[k][3] for k in _WEIGHTS])
```

```python
import functools
import math

import jax
import jax.numpy as jnp
from jax import lax
from jax.experimental import pallas as pl
from jax.experimental.pallas import tpu as pltpu

F32 = jnp.float32
MX = jnp.bfloat16
MESH = pl.DeviceIdType.MESH

DEPTH = 2
NSEQ = 2
L = 2048
N = NSEQ * L
D = 1024
DFF = 4096
NSHARD = 4
S5_G, S5_H, S5_P = 16, 16, 64
GLA_CHUNK = 64
NCHUNK = L // GLA_CHUNK
GLA_GROUP = 4
NGROUP = NCHUNK // GLA_GROUP
SWA_BLK = 128
NBLK = L // SWA_BLK
SWA_PER = 2
ROT = 16
ROPE_THETA = 500000.0
LN_EPS = 1e-5
ALPHA = (2 * DEPTH) ** 0.25
NEG_BIG = -1e30
DIN = 1824
DINP = 1920
ADAM_LR, ADAM_B1, ADAM_B2, ADAM_EPS, ADAM_WD, ADAM_STEP = 0.001, 0.9, 0.999, 1e-08, 0.01, 10
VMEM_LIMIT = 56 * 1024 * 1024
TT = 512
SW = 512
FFN_TM = 512
FFN_TM_W = 1024
FFN_WB = 1
FFN_VMEM = 60 * 1024 * 1024
INPROJ_BWD_TM = 512


def _cp(sem, vmem=VMEM_LIMIT):
    return pltpu.CompilerParams(dimension_semantics=sem, vmem_limit_bytes=vmem)


def _mm(a, b):
    return jnp.dot(a.astype(MX), b.astype(MX), preferred_element_type=F32)


def _mm_nt(a, b):
    return lax.dot_general(a.astype(MX), b.astype(MX), (((1,), (1,)), ((), ())), preferred_element_type=F32)


def _mm_tn(a, b):
    return lax.dot_general(a.astype(MX), b.astype(MX), (((0,), (0,)), ((), ())), preferred_element_type=F32)


@jax.custom_vjp
def _dmm(a, b):
    return _mm(a, b)


_dmm.defvjp(lambda a, b: (_mm(a, b), (a, b)), lambda r, g: (_mm_nt(g, r[1]), _mm_tn(r[0], g)))


@jax.custom_vjp
def _dmm_nt(a, b):
    return _mm_nt(a, b)


_dmm_nt.defvjp(lambda a, b: (_mm_nt(a, b), (a, b)), lambda r, g: (_mm(g, r[1]), _mm_tn(g, r[0])))


@jax.custom_vjp
def _dmm_tn(a, b):
    return _mm_tn(a, b)


_dmm_tn.defvjp(lambda a, b: (_mm_tn(a, b), (a, b)), lambda r, g: (_mm_nt(r[1], g), _mm(r[0], g)))


def _split3(x):
    hi = x.astype(MX)
    r1 = x - hi.astype(F32)
    mid = r1.astype(MX)
    lo = (r1 - mid.astype(F32)).astype(MX)
    return hi, mid, lo


def _chunk_pairs(rows, rev, strict):
    r = lax.broadcasted_iota(jnp.int32, (rows, rows), 0)
    c = lax.broadcasted_iota(jnp.int32, (rows, rows), 1)
    order = ((c > r) if strict else (c >= r)) if rev else ((c < r) if strict else (c <= r))
    return (r // GLA_CHUNK == c // GLA_CHUNK) & order


def _cums_impl(x, rev):
    rows, w = x.shape
    t = jnp.where(_chunk_pairs(rows, rev, False), 1.0, 0.0).astype(MX)
    s = jnp.dot(t, jnp.concatenate(_split3(x), axis=1), preferred_element_type=F32)
    return s[:, 0:w] + s[:, w:2 * w] + s[:, 2 * w:3 * w]


@functools.partial(jax.custom_vjp, nondiff_argnums=(1,))
def _cums(x, rev):
    return _cums_impl(x, rev)


_cums.defvjp(lambda x, rev: (_cums_impl(x, rev), None), lambda rev, r, g: (_cums_impl(g, not rev),))


def _ln_fwd(s, g, b):
    mu = jnp.mean(s, axis=-1, keepdims=True)
    xc = s - mu
    var = jnp.mean(xc * xc, axis=-1, keepdims=True)
    return xc * lax.rsqrt(var + LN_EPS) * g + b


def _ln_bwd(dy, s, g):
    mu = jnp.mean(s, axis=-1, keepdims=True)
    xc = s - mu
    var = jnp.mean(xc * xc, axis=-1, keepdims=True)
    rstd = lax.rsqrt(var + LN_EPS)
    xhat = xc * rstd
    dxh = dy * g
    ds = rstd * (dxh - jnp.mean(dxh, axis=-1, keepdims=True) - xhat * jnp.mean(dxh * xhat, axis=-1, keepdims=True))
    return ds, jnp.sum(dy * xhat, axis=0, keepdims=True), jnp.sum(dy, axis=0, keepdims=True)


def _sds(shape, dtype=F32):
    return jax.ShapeDtypeStruct(shape, dtype)


_IN_ROW_PIECES = (((0, 0), (0, 456)), ((1, 0), (456, 456)), ((2, 0), (912, 112)), ((2, 112), (1792, 32)),
                  ((2, 144), (1024, 312)), ((3, 0), (1336, 456)))


def _in_rows(g4, behind):
    def body(g_ref, behind_ref, o_ref, tmp):
        tmp[DIN:DINP] = jnp.zeros((DINP - DIN, D), F32)
        for (j, s0), (d0, n_) in _IN_ROW_PIECES:
            tmp[d0:d0 + n_] = g_ref[j, s0:s0 + n_].astype(F32)
        o_ref[...] = tmp[...].astype(MX)

    vm = pl.BlockSpec(memory_space=pltpu.VMEM)
    return pl.pallas_call(body, in_specs=[vm, pl.BlockSpec(memory_space=pl.ANY)], out_specs=vm,
                          out_shape=_sds((DINP, D), MX), scratch_shapes=[pltpu.VMEM((DINP, D), F32)], name="in_rows",
                          compiler_params=pltpu.CompilerParams(vmem_limit_bytes=VMEM_LIMIT))(g4, behind)


def _inproj_fwd(x, wt, wa, ba):
    tm = 512

    def body(x_ref, w_ref, wa_ref, ba_ref, h_ref, la_ref):
        h = _mm_nt(x_ref[...], w_ref[...])
        h_ref[...] = h
        la_ref[...] = _logsig(_mm(h[:, DINP - 128:], wa_ref[...]) + ba_ref[...]) * (1.0 / 16.0)

    return pl.pallas_call(
        body, grid=(N // tm,),
        in_specs=[pl.BlockSpec((tm, D), lambda i: (i, 0)), pl.BlockSpec((DINP, D), lambda i: (0, 0)),
                  pl.BlockSpec((128, 256), lambda i: (0, 0)), pl.BlockSpec((1, 256), lambda i: (0, 0))],
        out_specs=[pl.BlockSpec((tm, DINP), lambda i: (i, 0)), pl.BlockSpec((tm, 256), lambda i: (i, 0))],
        out_shape=[_sds((N, DINP)), _sds((N, 256))], name="inproj_fwd", compiler_params=_cp(("parallel",)))(x, wt, wa, ba)


def _inproj_bwd(x, w, dxp, du2, dud, gq_f, gq_b, gk_f, gk_b, gv_f, gv_b, gr, daq, dakv, h, wa, ba, dla_f, dla_b):
    tm = INPROJ_BWD_TM
    nt = N // tm

    def body(x_ref, w_ref, dxp_ref, du2_ref, dud_ref, gqf, gqb, gkf, gkb, gvf, gvb, gr_ref, daq_ref, dakv_ref,
             hl_ref, wa_ref, ba_ref, df_ref, db_ref, dx_ref, dw_ref, dwa_ref, dba_ref, acc):
        i = pl.program_id(0)
        f = lambda r: r[...].astype(F32)
        hl = hl_ref[...]
        pre = _mm(hl, wa_ref[...]) + ba_ref[...]
        dpre = jnp.concatenate([df_ref[...], db_ref[...]], axis=1) * (1.0 / 16.0) * jax.nn.sigmoid(-pre)
        dwa = _mm_tn(hl, dpre)[0:32]
        dba = jnp.sum(dpre, axis=0, keepdims=True)
        dh = jnp.concatenate([
            du2_ref[0] + du2_ref[1] + f(dud_ref), f(gqf) + f(gqb), f(gkf) + f(gkb), f(gvf) + f(gvb),
            f(gr_ref), f(daq_ref), f(dakv_ref), _mm_nt(dpre, wa_ref[...])], axis=1)
        dx_ref[...] = dxp_ref[...] + _mm(dh, w_ref[...])
        contrib = _mm_tn(dh, x_ref[...])

        @pl.when(i == 0)
        def _():
            acc[...] = contrib
            dwa_ref[...] = dwa
            dba_ref[...] = dba

        @pl.when(i > 0)
        def _():
            acc[...] += contrib
            dwa_ref[...] += dwa
            dba_ref[...] += dba

        @pl.when(i == nt - 1)
        def _():
            for (j, d0), (s0, n_) in _IN_ROW_PIECES:
                dw_ref[j, d0:d0 + n_] = acc[s0:s0 + n_].astype(MX)

    row = lambda w_: pl.BlockSpec((tm, w_), lambda i: (i, 0))
    return pl.pallas_call(
        body, grid=(nt,),
        in_specs=[row(D), pl.BlockSpec((DINP, D), lambda i: (0, 0)), row(D),
                  pl.BlockSpec((2, tm, 256), lambda i: (0, i, 0)), row(256), row(128), row(128), row(128), row(128),
                  row(256), row(256), row(256), row(512), row(256),
                  pl.BlockSpec((tm, 128), lambda i: (i, 14)), pl.BlockSpec((128, 256), lambda i: (0, 0)),
                  pl.BlockSpec((1, 256), lambda i: (0, 0)), row(128), row(128)],
        out_specs=[row(D), pl.BlockSpec((NSHARD, DIN // NSHARD, D), lambda i: (0, 0, 0)),
                   pl.BlockSpec((32, 256), lambda i: (0, 0)), pl.BlockSpec((1, 256), lambda i: (0, 0))],
        out_shape=[_sds((N, D)), _sds((NSHARD, DIN // NSHARD, D), MX), _sds((32, 256)), _sds((1, 256))],
        scratch_shapes=[pltpu.VMEM((DINP, D), F32)],
        name="inproj_bwd", compiler_params=_cp(("arbitrary",)))(
            x, w, dxp, du2, dud, gq_f, gq_b, gk_f, gk_b, gv_f, gv_b, gr, daq, dakv, h, wa, ba, dla_f, dla_b)


def _tile_scan(xr, xi, a, cr, ci, reverse):
    for lvl, d in enumerate((1, 2, 4)):
        sh = 8 - d if reverse else d
        sr = pltpu.roll(xr, sh, 0)
        si = pltpu.roll(xi, sh, 0)
        ar, ai = a[2 * lvl], a[2 * lvl + 1]
        xr, xi = xr + ar * sr - ai * si, xi + ar * si + ai * sr
    pr, pi = a[6], a[7]
    return xr + pr * cr - pi * ci, xi + pr * ci + pi * cr


NJ = TT // 8


def _lockstep_tables(mr, mi):
    def body(mr_ref, mi_ref, a_ref, p_ref, ac_ref, pc_ref):
        rowid = lax.broadcasted_iota(jnp.int32, (8, 2 * SW), 0)

        def mul(a, b):
            return a[0] * b[0] - a[1] * b[1], a[0] * b[1] + a[1] * b[0]

        for z in range(2):
            for sign, reverse, a_out, p_out in ((1.0, z == 1, a_ref, p_ref), (-1.0, z == 0, ac_ref, pc_ref)):
                m = (mr_ref[z:z + 1, :], sign * mi_ref[z:z + 1, :])
                pw = [m]
                for _ in range(NJ - 1):
                    pw.append(mul(pw[-1], m))
                n = pw[-1]
                link = [n]
                for _ in range(7):
                    link.append(mul(link[-1], n))
                tiles = [jnp.broadcast_to(m[0], (8, 2 * SW)), jnp.broadcast_to(m[1], (8, 2 * SW))]
                for d in (1, 2, 4):
                    keep = (rowid <= 7 - d) if reverse else (rowid >= d)
                    tiles += [jnp.where(keep, link[d - 1][c], 0.0) for c in range(2)]
                for c in range(2):
                    t = jnp.zeros((8, 2 * SW), F32)
                    for i in range(8):
                        t = jnp.where(rowid == (7 - i if reverse else i), link[i][c], t)
                    tiles.append(t)
                for blk in range(2):
                    lanes = slice(blk * SW, (blk + 1) * SW)
                    for k, t in enumerate(tiles):
                        a_out[z, blk, k] = t[:, lanes]
                    for j in range(NJ):
                        src = pw[NJ - 1 - j] if reverse else pw[j]
                        for c in range(2):
                            p_out[z, blk, c, j:j + 1, :] = src[c][:, lanes]

    vm = pl.BlockSpec(memory_space=pltpu.VMEM)
    a_shape, p_shape = _sds((2, 2, 10, 8, SW)), _sds((2, 2, 2, NJ, SW))
    a, p, ac, pc = pl.pallas_call(body, in_specs=[vm, vm], out_specs=[vm] * 4, out_shape=[a_shape, p_shape] * 2,
                                  name="s5_tables")(mr, mi)
    return (a, p), (ac, pc)


def _to_lockstep(ref, *lead):
    return jnp.concatenate([ref[(*lead, pl.ds(j, 8, stride=NJ), slice(None))] for j in range(NJ)], axis=0)


def _from_lockstep(val, ref, *lead):
    for j in range(NJ):
        ref[(*lead, pl.ds(j, 8, stride=NJ), slice(None))] = val[8 * j:8 * j + 8]


def _expand_powers(p_ref, pexp):
    for c in range(2):
        for j in range(NJ):
            pexp[c, j] = jnp.broadcast_to(p_ref[0, 0, c, j:j + 1, :], (8, SW))


def _lockstep_scan(xre, xim, a_ref, pexp, car, reverse, extra=None):
    a = [a_ref[0, 0, k] for k in range(10)]
    mr, mi = a[0], a[1]
    order = (lambda i: NJ - 1 - i) if reverse else (lambda i: i)

    def local(i, hcar):
        hr, hi = hcar
        r0 = pl.multiple_of(order(i) * 8, 8)
        hr, hi = mr * hr - mi * hi + xre[pl.ds(r0, 8), :], mr * hi + mi * hr + xim[pl.ds(r0, 8), :]
        xre[pl.ds(r0, 8), :] = hr
        xim[pl.ds(r0, 8), :] = hi
        return hr, hi

    z8 = jnp.zeros((8, SW), F32)
    er, ei = lax.fori_loop(0, NJ, local, (z8, z8), unroll=4)
    c0r, c0i = car[0], car[1]
    er, ei = _tile_scan(er, ei, a[2:], c0r, c0i, reverse)
    rowid = lax.broadcasted_iota(jnp.int32, (8, SW), 0)
    first, sh, last = (7, 7, 0) if reverse else (0, 1, 7)
    cvr = jnp.where(rowid == first, c0r, pltpu.roll(er, sh, 0))
    cvi = jnp.where(rowid == first, c0i, pltpu.roll(ei, sh, 0))
    car[0] = jnp.broadcast_to(er[last:last + 1, :], (8, SW))
    car[1] = jnp.broadcast_to(ei[last:last + 1, :], (8, SW))

    def fix(i, carry):
        j = order(i)
        r0 = pl.multiple_of(j * 8, 8)
        pr, pi = pexp[0, j], pexp[1, j]
        sr = xre[pl.ds(r0, 8), :] + pr * cvr - pi * cvi
        si = xim[pl.ds(r0, 8), :] + pr * cvi + pi * cvr
        xre[pl.ds(r0, 8), :] = sr
        xim[pl.ds(r0, 8), :] = si
        if extra is None:
            return carry
        return (sr, si, extra(r0, sr, si, carry[0], carry[1], carry[2]))

    init = (cvr, cvi, extra(None, None, None, None, None, None)) if extra is not None else 0
    return lax.fori_loop(0, NJ, fix, init, unroll=4)


def _s5_time_block(z, s, t, adjoint):
    flip = (1 - z) if adjoint else z
    return s * (L // TT) + t + flip * (L // TT - 1 - 2 * t)


def _s5_fwd(h, bre, bim, cre, cim, tab):
    nt = L // TT
    taba, tabp = tab

    def body(u_ref, bre_ref, bim_ref, cre_ref, cim_ref, a_ref, p_ref, hre_ref, him_ref, y_ref, car, pexp):
        z = pl.program_id(1)
        s = pl.program_id(2)
        tc = pl.program_id(3)

        @pl.when(tc == 0)
        def _():
            car[...] = jnp.zeros_like(car)

        @pl.when((tc == 0) & (s == 0))
        def _():
            _expand_powers(p_ref, pexp)

        u = _to_lockstep(u_ref)
        hre_ref[0] = _mm(u, bre_ref[0, 0])
        him_ref[0] = _mm(u, bim_ref[0, 0])

        @pl.when(z == 0)
        def _():
            _lockstep_scan(hre_ref.at[0], him_ref.at[0], a_ref, pexp, car, False)

        @pl.when(z == 1)
        def _():
            _lockstep_scan(hre_ref.at[0], him_ref.at[0], a_ref, pexp, car, True)

        _from_lockstep(_mm(hre_ref[0], cre_ref[0, 0]) - _mm(him_ref[0], cim_ref[0, 0]), y_ref, 0)

    tb = lambda b, z, s, t: _s5_time_block(z, s, t, False)
    wspec = lambda r, c: pl.BlockSpec((1, 1, r, c), lambda b, z, s, t: (z, b, 0, 0))
    return pl.pallas_call(
        body, grid=(2, 2, NSEQ, nt),
        in_specs=[pl.BlockSpec((TT, 128), lambda b, z, s, t: (tb(b, z, s, t), b)),
                  wspec(128, SW), wspec(128, SW), wspec(SW, 128), wspec(SW, 128),
                  pl.BlockSpec((1, 1, 10, 8, SW), lambda b, z, s, t: (z, b, 0, 0, 0)),
                  pl.BlockSpec((1, 1, 2, NJ, SW), lambda b, z, s, t: (z, b, 0, 0, 0))],
        out_specs=[pl.BlockSpec((1, TT, SW), lambda b, z, s, t: (z, tb(b, z, s, t), b)),
                   pl.BlockSpec((1, TT, SW), lambda b, z, s, t: (z, tb(b, z, s, t), b)),
                   pl.BlockSpec((1, TT, 128), lambda b, z, s, t: (z, tb(b, z, s, t), b))],
        out_shape=[_sds((2, N, 2 * SW)), _sds((2, N, 2 * SW)), _sds((2, N, 256))],
        scratch_shapes=[pltpu.VMEM((2, 8, SW), F32), pltpu.VMEM((2, NJ, 8, SW), F32)],
        name="s5_fwd", compiler_params=_cp(("arbitrary",) * 4))(h, bre, bim, cre, cim, taba, tabp)


def _s5_bwd(h, dyp, hre, him, bre, bim, cre, cim, tabc):
    nt = L // TT
    taba, tabp = tabc

    def body(u_ref, dy_ref, hre_ref, him_ref, bre_ref, bim_ref, cre_ref, cim_ref, a_ref, p_ref,
             du_ref, dbre_ref, dbim_ref, dcre_ref, dcim_ref, dmu_ref, gre, gim, car, acc, macc, pexp):
        z = pl.program_id(1)
        s = pl.program_id(2)
        tc = pl.program_id(3)

        @pl.when(tc == 0)
        def _():
            car[...] = jnp.zeros_like(car)

        @pl.when((tc == 0) & (s == 0))
        def _():
            acc[...] = jnp.zeros_like(acc)
            macc[...] = jnp.zeros_like(macc)
            _expand_powers(p_ref, pexp)

        dy = _to_lockstep(dy_ref)
        gre[...] = _mm_nt(dy, cre_ref[0, 0])
        gim[...] = -_mm_nt(dy, cim_ref[0, 0])

        def run(reverse):
            def pair(r0, gr_, gi_, pvr, pvi, m):
                if r0 is None:
                    return (macc[0], macc[1])
                hr = hre_ref[0, pl.ds(r0, 8), :]
                hi = him_ref[0, pl.ds(r0, 8), :]
                return (m[0] + pvr * hr + pvi * hi, m[1] + pvi * hr - pvr * hi)

            _, _, (dmr, dmi) = _lockstep_scan(gre, gim, a_ref, pexp, car, reverse, pair)
            macc[0] = dmr
            macc[1] = dmi

        @pl.when(z == 0)
        def _():
            run(True)

        @pl.when(z == 1)
        def _():
            run(False)

        gr = gre[...]
        gi = gim[...]
        u = _to_lockstep(u_ref)
        _from_lockstep(_mm_nt(gr, bre_ref[0, 0]) + _mm_nt(gi, bim_ref[0, 0]), du_ref, 0)
        acc[0] += _mm_tn(u, gr)
        acc[1] += _mm_tn(u, gi)
        acc[2] += _mm_tn(dy, hre_ref[0])
        acc[3] -= _mm_tn(dy, him_ref[0])

        @pl.when((tc == nt - 1) & (s == NSEQ - 1))
        def _():
            grp = lax.broadcasted_iota(jnp.int32, (S5_H, SW), 1) // S5_P
            for k, out in enumerate((dbre_ref, dbim_ref, dcre_ref, dcim_ref)):
                c = jnp.zeros((S5_H, SW), F32)
                for i in range(8):
                    c = c + jnp.where(grp == i, acc[k, i * S5_H:(i + 1) * S5_H, :], 0.0)
                out[0, 0] = c
            dmu_ref[0, 0] = jnp.concatenate([jnp.sum(macc[0], axis=0, keepdims=True),
                                             jnp.sum(macc[1], axis=0, keepdims=True)], axis=0)

    tb = lambda b, z, s, t: _s5_time_block(z, s, t, True)
    wspec = lambda r, c: pl.BlockSpec((1, 1, r, c), lambda b, z, s, t: (z, b, 0, 0))
    tok = lambda w_: pl.BlockSpec((TT, w_), lambda b, z, s, t: (tb(b, z, s, t), b))
    st = pl.BlockSpec((1, TT, SW), lambda b, z, s, t: (z, tb(b, z, s, t), b))
    return pl.pallas_call(
        body, grid=(2, 2, NSEQ, nt),
        in_specs=[tok(128), tok(128), st, st, wspec(128, SW), wspec(128, SW), wspec(SW, 128), wspec(SW, 128),
                  pl.BlockSpec((1, 1, 10, 8, SW), lambda b, z, s, t: (z, b, 0, 0, 0)),
                  pl.BlockSpec((1, 1, 2, NJ, SW), lambda b, z, s, t: (z, b, 0, 0, 0))],
        out_specs=[pl.BlockSpec((1, TT, 128), lambda b, z, s, t: (z, tb(b, z, s, t), b)),
                   wspec(S5_H, SW), wspec(S5_H, SW), wspec(S5_H, SW), wspec(S5_H, SW),
                   wspec(2, SW)],
        out_shape=[_sds((2, N, 256))] + [_sds((2, 2, S5_H, SW))] * 4 + [_sds((2, 2, 2, SW))],
        scratch_shapes=[pltpu.VMEM((TT, SW), F32), pltpu.VMEM((TT, SW), F32), pltpu.VMEM((2, 8, SW), F32),
                        pltpu.VMEM((4, 128, SW), F32), pltpu.VMEM((2, 8, SW), F32), pltpu.VMEM((2, NJ, 8, SW), F32)],
        name="s5_bwd", compiler_params=_cp(("arbitrary",) * 4))(h, dyp, hre, him, bre, bim, cre, cim, taba, tabp)


_GELU_C = math.sqrt(2.0 / math.pi)


def _gelu(y):
    return 0.5 * y * (1.0 + jnp.tanh(_GELU_C * (y + 0.044715 * y * y * y)))


def _gelu_grad(y):
    t = jnp.tanh(_GELU_C * (y + 0.044715 * y * y * y))
    return 0.5 * (1.0 + t) + 0.5 * y * (1.0 - t * t) * _GELU_C * (1.0 + 3 * 0.044715 * y * y)


def _glu_halves(w4_ref):
    return (jnp.concatenate([w4_ref[0], w4_ref[1]], axis=1), jnp.concatenate([w4_ref[2], w4_ref[3]], axis=1))


def _s5_glu_fwd(y2, h, dsk, w4, bv, bg):
    tm = 512

    def body(y2_ref, u_ref, d_ref, w4_ref, bv_ref, bg_ref, ya_ref):
        wv, wg = _glu_halves(w4_ref)
        z = _gelu(y2_ref[0] + y2_ref[1] + d_ref[...] * u_ref[...])
        val = _mm(z, wv) + bv_ref[...]
        gate = _mm(z, wg) + bg_ref[...]
        ya_ref[...] = (val * jax.nn.sigmoid(gate)).astype(MX)

    full = lambda r, c: pl.BlockSpec((r, c), lambda i: (0, 0))
    return pl.pallas_call(
        body, grid=(N // tm,),
        in_specs=[pl.BlockSpec((2, tm, 256), lambda i: (0, i, 0)), pl.BlockSpec((tm, 256), lambda i: (i, 0)),
                  full(1, 256), pl.BlockSpec((NSHARD, 256, 128), lambda i: (0, 0, 0)), full(1, 256), full(1, 256)],
        out_specs=pl.BlockSpec((tm, 256), lambda i: (i, 0)),
        out_shape=_sds((N, 256), MX), name="s5_glu_fwd", compiler_params=_cp(("parallel",)))(y2, h, dsk, w4, bv, bg)


def _s5_glu_bwd(y2, h, dsk, w4, bv, bg, dya):
    tm = 512
    nt = N // tm

    def body(y2_ref, u_ref, d_ref, w4_ref, bv_ref, bg_ref, dya_ref,
             dyp_ref, dud_ref, dd_ref, dw4_ref, dbv_ref, dbg_ref, accv, accg):
        i = pl.program_id(0)

        @pl.when(i == 0)
        def _():
            for r in (dd_ref, accv, accg, dbv_ref, dbg_ref):
                r[...] = jnp.zeros_like(r)

        wv, wg = _glu_halves(w4_ref)
        u = u_ref[...]
        y = y2_ref[0] + y2_ref[1] + d_ref[...] * u
        z = _gelu(y)
        val = _mm(z, wv) + bv_ref[...]
        sig = jax.nn.sigmoid(_mm(z, wg) + bg_ref[...])
        dya = dya_ref[...]
        dval = dya * sig
        dgate = dya * val * sig * (1.0 - sig)
        dz = _mm_nt(dval, wv) + _mm_nt(dgate, wg)
        dy = dz * _gelu_grad(y)
        dyp_ref[...] = dy
        dud_ref[...] = (dy * d_ref[...]).astype(MX)
        dd_ref[...] += jnp.sum(dy * u, axis=0, keepdims=True)
        accv[...] += _mm_tn(z, dval)
        accg[...] += _mm_tn(z, dgate)
        dbv_ref[...] += jnp.sum(dval, axis=0, keepdims=True)
        dbg_ref[...] += jnp.sum(dgate, axis=0, keepdims=True)

        @pl.when(i == nt - 1)
        def _():
            dw4_ref[0] = accv[:, 0:128].astype(MX)
            dw4_ref[1] = accv[:, 128:256].astype(MX)
            dw4_ref[2] = accg[:, 0:128].astype(MX)
            dw4_ref[3] = accg[:, 128:256].astype(MX)

    full = lambda r, c: pl.BlockSpec((r, c), lambda i: (0, 0))
    row = pl.BlockSpec((tm, 256), lambda i: (i, 0))
    wspec = pl.BlockSpec((NSHARD, 256, 128), lambda i: (0, 0, 0))
    return pl.pallas_call(
        body, grid=(nt,),
        in_specs=[pl.BlockSpec((2, tm, 256), lambda i: (0, i, 0)), row, full(1, 256), wspec, full(1, 256), full(1, 256),
                  row],
        out_specs=[row, row, full(1, 256), wspec, full(1, 256), full(1, 256)],
        out_shape=[_sds((N, 256)), _sds((N, 256), MX), _sds((1, 256)), _sds((NSHARD, 256, 128), MX), _sds((1, 256)),
                   _sds((1, 256))],
        scratch_shapes=[pltpu.VMEM((256, 256), F32), pltpu.VMEM((256, 256), F32)],
        name="s5_glu_bwd", compiler_params=_cp(("arbitrary",)))(y2, h, dsk, w4, bv, bg, dya)


def _logsig(x):
    return jnp.minimum(x, 0.0) - jnp.log(1.0 + jnp.exp(-jnp.abs(x)))


def _gla_chunk(q, k, v, la, st, rev):
    c = GLA_CHUNK
    rows = q.shape[0]
    nch = rows // c
    b = _cums(la, rev)
    blc = [jnp.sum(la[i * c:(i + 1) * c], axis=0, keepdims=True) for i in range(nch)]
    bl = jnp.concatenate([jnp.broadcast_to(t, (c, 128)) for t in blc], axis=0)
    q_in = q * (32.0 ** -0.5) * jnp.exp(b)
    k_in = k * jnp.exp(-b)
    k_st = k * jnp.exp(bl - b)
    lane_k = lax.broadcasted_iota(jnp.int32, (1, 128), 1) // 32
    lane_v = lax.broadcasted_iota(jnp.int32, (1, 256), 1) // 64
    qs = jnp.concatenate([jnp.where(lane_k == hd, q_in, 0.0) for hd in range(4)], axis=0)
    a = _dmm_nt(qs, k_in)
    a = jnp.where(jnp.concatenate([_chunk_pairs(rows, rev, rev)] * 4, axis=0), a, 0.0)
    o4 = _dmm(a, v)
    o = jnp.zeros((rows, 256), F32)
    for hd in range(4):
        o = o + jnp.where(lane_v == hd, o4[hd * rows:(hd + 1) * rows], 0.0)
    bd = (lax.broadcasted_iota(jnp.int32, (256, 128), 0) // 64) == (lax.broadcasted_iota(jnp.int32, (256, 128), 1) // 32)
    inter = [None] * nch
    for i in (reversed(range(nch)) if rev else range(nch)):
        sl = slice(i * c, (i + 1) * c)
        inter[i] = _dmm_nt(q_in[sl], st)
        st = jnp.exp(blc[i]) * st + jnp.where(bd, _dmm_tn(v[sl], k_st[sl]), 0.0)
    return o + jnp.concatenate(inter, axis=0), st


def _gla_chunk_of(c, rev):
    return NGROUP - 1 - c if rev else c


def _gla_fwd(h, la2):
    c = GLA_GROUP * GLA_CHUNK

    def body(qf, kf, vf, laf, qb, kb, vb, lab, of_ref, ob_ref, sf_ref, sb_ref, stf, stb):
        @pl.when(pl.program_id(0) == 0)
        def _():
            stf[...] = jnp.zeros_like(stf)
            stb[...] = jnp.zeros_like(stb)

        ins = [(qf[s], kf[s], vf[s], laf[s], stf[s], qb[s], kb[s], vb[s], lab[s], stb[s]) for s in range(NSEQ)]
        outs = [(_gla_chunk(*t[:5], False), _gla_chunk(*t[5:], True)) for t in ins]
        for s in range(NSEQ):
            sf_ref[s, 0] = ins[s][4]
            sb_ref[s, 0] = ins[s][9]
            (of_ref[s], stf[s]), (ob_ref[s], stb[s]) = outs[s]

    def specs(rev):
        ch = lambda i: _gla_chunk_of(i, rev)
        return [pl.BlockSpec((NSEQ, c, 128), lambda i: (0, ch(i), 2)), pl.BlockSpec((NSEQ, c, 128), lambda i: (0, ch(i), 3)),
                pl.BlockSpec((NSEQ, c, 256), lambda i: (0, ch(i), 2)),
                pl.BlockSpec((NSEQ, c, 128), lambda i: (0, ch(i), 1 if rev else 0))]

    orow = lambda rev: pl.BlockSpec((NSEQ, c, 256), lambda i: (0, _gla_chunk_of(i, rev), 0))
    srow = lambda rev: pl.BlockSpec((NSEQ, 1, 256, 128), lambda i: (0, _gla_chunk_of(i, rev), 0, 0))
    h3, la3 = h.reshape(NSEQ, L, DINP), la2.reshape(NSEQ, L, 256)
    of, ob, sf, sb = pl.pallas_call(
        body, grid=(NGROUP,),
        in_specs=specs(False) + specs(True),
        out_specs=[orow(False), orow(True), srow(False), srow(True)],
        out_shape=[_sds((NSEQ, L, 256)), _sds((NSEQ, L, 256)), _sds((NSEQ, NGROUP, 256, 128)),
                   _sds((NSEQ, NGROUP, 256, 128))],
        scratch_shapes=[pltpu.VMEM((NSEQ, 256, 128), F32), pltpu.VMEM((NSEQ, 256, 128), F32)],
        name="gla_fwd", compiler_params=_cp(("arbitrary",)))(h3, h3, h3, la3, h3, h3, h3, la3)
    return of.reshape(N, 256), ob.reshape(N, 256), sf, sb


def _gla_bwd(h, la2, do, sf, sb):
    c = GLA_GROUP * GLA_CHUNK

    def body(qf, kf, vf, laf, dof, sfr, qb, kb, vb, lab, dob, sbr,
             dqf, dkf, dvf, dlf, dqb, dkb, dvb, dlb, dstf, dstb):
        @pl.when(pl.program_id(0) == 0)
        def _():
            dstf[...] = jnp.zeros_like(dstf)
            dstb[...] = jnp.zeros_like(dstb)

        def one(s, q, k, v, la, do_, st, dst, rev):
            _, vjp = jax.vjp(functools.partial(_gla_chunk, rev=rev), q[s], k[s], v[s], la[s], st[s, 0])
            return vjp((do_[s].astype(F32), dst[s]))

        res = [(one(s, qf, kf, vf, laf, dof, sfr, dstf, False), one(s, qb, kb, vb, lab, dob, sbr, dstb, True))
               for s in range(NSEQ)]
        for s in range(NSEQ):
            for (gq, gk, gv, gl, gs), (dq, dk, dv, dl, dst) in ((res[s][0], (dqf, dkf, dvf, dlf, dstf)),
                                                                  (res[s][1], (dqb, dkb, dvb, dlb, dstb))):
                dq[s], dk[s], dv[s] = gq.astype(MX), gk.astype(MX), gv.astype(MX)
                dl[s], dst[s] = gl, gs

    def specs(rev):
        ch = lambda i: _gla_chunk_of(i, not rev)
        return [pl.BlockSpec((NSEQ, c, 128), lambda i: (0, ch(i), 2)), pl.BlockSpec((NSEQ, c, 128), lambda i: (0, ch(i), 3)),
                pl.BlockSpec((NSEQ, c, 256), lambda i: (0, ch(i), 2)),
                pl.BlockSpec((NSEQ, c, 128), lambda i: (0, ch(i), 1 if rev else 0)),
                pl.BlockSpec((NSEQ, c, 256), lambda i: (0, ch(i), 0)),
                pl.BlockSpec((NSEQ, 1, 256, 128), lambda i: (0, ch(i), 0, 0))]

    def ospecs(rev):
        ch = lambda i: _gla_chunk_of(i, not rev)
        n = pl.BlockSpec((NSEQ, c, 128), lambda i: (0, ch(i), 0))
        return [n, n, pl.BlockSpec((NSEQ, c, 256), lambda i: (0, ch(i), 0)), n]

    oshape = [_sds((NSEQ, L, 128), MX), _sds((NSEQ, L, 128), MX), _sds((NSEQ, L, 256), MX), _sds((NSEQ, L, 128))]
    h3, la3, do3 = h.reshape(NSEQ, L, DINP), la2.reshape(NSEQ, L, 256), do.reshape(NSEQ, L, 256)
    res = pl.pallas_call(
        body, grid=(NGROUP,),
        in_specs=specs(False) + specs(True),
        out_specs=ospecs(False) + ospecs(True),
        out_shape=oshape + oshape,
        scratch_shapes=[pltpu.VMEM((NSEQ, 256, 128), F32), pltpu.VMEM((NSEQ, 256, 128), F32)],
        name="gla_bwd", compiler_params=_cp(("arbitrary",)))(h3, h3, h3, la3, do3, sf, h3, h3, h3, la3, do3, sb)
    return [r.reshape(N, r.shape[-1]) for r in res]


def _gla_post(of, ob, r, g):
    o = of + ob
    head = lax.broadcasted_iota(jnp.int32, (1, 256), 1) // 64
    mu = jnp.zeros_like(o)
    for hd in range(4):
        mu = mu + jnp.where(head == hd, jnp.sum(jnp.where(head == hd, o, 0.0), axis=-1, keepdims=True) * (1.0 / 64.0), 0.0)
    xc = o - mu
    var = jnp.zeros_like(o)
    for hd in range(4):
        var = var + jnp.where(head == hd, jnp.sum(jnp.where(head == hd, xc * xc, 0.0), axis=-1, keepdims=True) * (1.0 / 64.0), 0.0)
    return xc * lax.rsqrt(var + LN_EPS) * g * (r * jax.nn.sigmoid(r))


def _gla_post_fwd(of, ob, h, g):
    tm = 512

    def body(of_ref, ob_ref, r_ref, g_ref, y_ref):
        y_ref[...] = _gla_post(of_ref[...], ob_ref[...], r_ref[...], g_ref[...]).astype(MX)

    row = pl.BlockSpec((tm, 256), lambda i: (i, 0))
    return pl.pallas_call(
        body, grid=(N // tm,),
        in_specs=[row, row, pl.BlockSpec((tm, 256), lambda i: (i, 3)), pl.BlockSpec((1, 256), lambda i: (0, 0))],
        out_specs=row, out_shape=_sds((N, 256), MX), name="gla_post_fwd", compiler_params=_cp(("parallel",)))(of, ob, h, g)


def _gla_post_bwd(of, ob, h, g, dyb):
    tm = 512

    def body(of_ref, ob_ref, r_ref, g_ref, dy_ref, do_ref, dr_ref, dg_ref):
        @pl.when(pl.program_id(0) == 0)
        def _():
            dg_ref[...] = jnp.zeros_like(dg_ref)

        _, vjp = jax.vjp(_gla_post, of_ref[...], ob_ref[...], r_ref[...], g_ref[...])
        go, _, gr, gg = vjp(dy_ref[...])
        do_ref[...] = go.astype(MX)
        dr_ref[...] = gr.astype(MX)
        dg_ref[...] += gg

    row = pl.BlockSpec((tm, 256), lambda i: (i, 0))
    one = pl.BlockSpec((1, 256), lambda i: (0, 0))
    return pl.pallas_call(
        body, grid=(N // tm,),
        in_specs=[row, row, pl.BlockSpec((tm, 256), lambda i: (i, 3)), one, row],
        out_specs=[row, row, one], out_shape=[_sds((N, 256), MX), _sds((N, 256), MX), _sds((1, 256))],
        name="gla_post_bwd", compiler_params=_cp(("arbitrary",)))(of, ob, h, g, dyb)


def _rope_tables(width):
    pos = jnp.arange(L, dtype=F32)
    inv_freq = ROPE_THETA ** (-jnp.arange(0, ROT, 2, dtype=F32) / ROT)
    ang = pos[:, None] * inv_freq[None, :]
    cos, sin = jnp.cos(ang), jnp.sin(ang)
    one = jnp.ones((L, 64 - ROT), F32)
    zero = jnp.zeros((L, 64 - ROT), F32)
    z8 = jnp.zeros((L, ROT // 2), F32)
    c = jnp.concatenate([cos, cos, one], axis=1)
    sa = jnp.concatenate([z8, sin, zero], axis=1)
    sb = jnp.concatenate([-sin, z8, zero], axis=1)
    rep = width // 64
    return jnp.stack([jnp.tile(c, (1, rep)), jnp.tile(sa, (1, rep)), jnp.tile(sb, (1, rep))])


def _pieces(t, f):
    out = [f(t[:, c * 128:(c + 1) * 128]) for c in range(t.shape[-1] // 128)]
    return out[0] if len(out) == 1 else jnp.concatenate(out, axis=1)


def _rope(t, tab):
    return _pieces(t, lambda x: x * tab[0] + pltpu.roll(x, ROT // 2, 1) * tab[1] + pltpu.roll(x, 128 - ROT // 2, 1) * tab[2])


def _rope_t(g, tab):
    return _pieces(g, lambda x: x * tab[0] + pltpu.roll(x * tab[1], 128 - ROT // 2, 1) + pltpu.roll(x * tab[2], ROT // 2, 1))


def _swa_pad_kv(kv_ref, tk_ref, kexp, vexp):
    z = jnp.zeros((SWA_BLK, 256), F32)
    kr = _rope(kv_ref[:, 0:128], tk_ref[...])
    for hk in range(2):
        for pad in (kexp, vexp):
            pad[hk, 0:SWA_BLK] = z
            pad[hk, SWA_BLK + L:] = z
        kexp[hk, SWA_BLK:SWA_BLK + L] = _swa_expand(kr, hk)
        vexp[hk, SWA_BLK:SWA_BLK + L] = _swa_expand(kv_ref[:, 128:256], hk)


def _swa_expand(x, hk):
    lane = lax.broadcasted_iota(jnp.int32, x.shape, 1)
    sw = pltpu.roll(x, 64, 1)
    pair = jnp.where(lane < 64, x, sw) if hk == 0 else jnp.where(lane < 64, sw, x)
    return jnp.concatenate([pair, pair], axis=1)


def _swa_fold(x, hk):
    a = x[:, 0:128] + x[:, 128:256]
    t = a + pltpu.roll(a, 64, 1)
    lane = lax.broadcasted_iota(jnp.int32, a.shape, 1)
    return jnp.where((lane < 64) if hk == 0 else (lane >= 64), t, 0.0)


def _swa_bias_tables(bias):
    i = lax.broadcasted_iota(jnp.int32, (SWA_BLK, 3 * SWA_BLK), 0)
    j = lax.broadcasted_iota(jnp.int32, (SWA_BLK, 3 * SWA_BLK), 1)
    band = (j - i >= 0) & (j - i <= 2 * SWA_BLK)
    for v, inside in enumerate((j >= SWA_BLK, True, j < 2 * SWA_BLK)):
        bias[v] = jnp.where(band & inside, 0.0, NEG_BIG)


def _swa_bias(bias, blk):
    return bias[jnp.where(blk == 0, 0, jnp.where(blk == NBLK - 1, 2, 1))]


def _swa_probs(q2, kexp, bias, sink_ref, hk):
    slot = lax.broadcasted_iota(jnp.int32, (1, 256), 1) // 64
    qs = jnp.concatenate([jnp.where(slot == g, q2, 0.0) for g in range(4)], axis=0)
    s = _mm_nt(qs, kexp) + jnp.concatenate([bias] * 4, axis=0)
    rowg = lax.broadcasted_iota(jnp.int32, (4 * SWA_BLK, 1), 0) // SWA_BLK
    sink = jnp.zeros((4 * SWA_BLK, 1), F32)
    for g in range(4):
        sink = jnp.where(rowg == g, sink_ref[hk * 4 + g], sink)
    m = jnp.maximum(jnp.max(s, axis=-1, keepdims=True), sink)
    p = jnp.exp(s - m)
    ps = jnp.exp(sink - m)
    inv = 1.0 / (jnp.sum(p, axis=-1, keepdims=True) + ps)
    return qs, p * inv, ps * inv, slot, rowg


def _swa_qtab(tk_ref, r0):
    return [tk_ref[i, pl.ds(r0, SWA_BLK), :] for i in range(3)]


def _swa_fwd(h, tk, sink):
    def body(sink_ref, q_ref, kv_ref, tk_ref, y_ref, kexp, vexp, bias):
        n = pl.program_id(1)

        @pl.when(n == 0)
        def _():
            _swa_pad_kv(kv_ref, tk_ref, kexp, vexp)
            _swa_bias_tables(bias)

        for t in range(SWA_PER):
            blk = n * SWA_PER + t
            rows = slice(t * SWA_BLK, (t + 1) * SWA_BLK)
            r0 = pl.multiple_of(blk * SWA_BLK, SWA_BLK)
            q = _rope(q_ref[rows, :], _swa_qtab(tk_ref, r0)) * 0.125
            for hk in range(2):
                _, p, _, slot, _ = _swa_probs(q[:, hk * 256:(hk + 1) * 256], kexp[hk, pl.ds(r0, 3 * SWA_BLK), :],
                                              _swa_bias(bias, blk), sink_ref, hk)
                o4 = _mm(p, vexp[hk, pl.ds(r0, 3 * SWA_BLK), :])
                o = jnp.zeros((SWA_BLK, 256), F32)
                for g in range(4):
                    o = o + jnp.where(slot == g, o4[g * SWA_BLK:(g + 1) * SWA_BLK], 0.0)
                y_ref[rows, hk * 256:(hk + 1) * 256] = o.astype(MX)

    tm = SWA_PER * SWA_BLK
    return pl.pallas_call(
        body,
        grid_spec=pltpu.PrefetchScalarGridSpec(
            num_scalar_prefetch=1, grid=(NSEQ, L // tm),
            in_specs=[pl.BlockSpec((tm, 512), lambda s, n, sk: (s * (L // tm) + n, 2)),
                      pl.BlockSpec((L, 256), lambda s, n, sk: (s, 6)),
                      pl.BlockSpec((3, L, 128), lambda s, n, sk: (0, 0, 0))],
            out_specs=pl.BlockSpec((tm, 512), lambda s, n, sk: (s * (L // tm) + n, 0)),
            scratch_shapes=[pltpu.VMEM((2, L + 2 * SWA_BLK, 256), F32), pltpu.VMEM((2, L + 2 * SWA_BLK, 256), F32),
                            pltpu.VMEM((3, SWA_BLK, 3 * SWA_BLK), F32)]),
        out_shape=_sds((N, 512), MX), name="swa_fwd", compiler_params=_cp(("arbitrary", "arbitrary")))(sink, h, h, tk)


def _swa_bwd(h, tk, sink, dyc):
    tm = SWA_PER * SWA_BLK

    def body(sink_ref, q_ref, kv_ref, tk_ref, dy_ref, dq_ref, dkv_ref, dsink_ref, kexp_all, vexp_all, dkacc, dvacc, bias):
        sq = pl.program_id(0)
        n = pl.program_id(1)

        @pl.when(n == 0)
        def _():
            _swa_pad_kv(kv_ref, tk_ref, kexp_all, vexp_all)
            _swa_bias_tables(bias)
            dkacc[...] = jnp.zeros_like(dkacc)
            dvacc[...] = jnp.zeros_like(dvacc)

        @pl.when((n == 0) & (sq == 0))
        def _():
            dsink_ref[...] = jnp.zeros_like(dsink_ref)

        hrow = lax.broadcasted_iota(jnp.int32, (8, 128), 0)
        dsk = jnp.zeros((8, 128), F32)
        for t in range(SWA_PER):
            blk = n * SWA_PER + t
            rows = slice(t * SWA_BLK, (t + 1) * SWA_BLK)
            r0 = pl.multiple_of(blk * SWA_BLK, SWA_BLK)
            tq = _swa_qtab(tk_ref, r0)
            q = _rope(q_ref[rows, :], tq) * 0.125
            band = _swa_bias(bias, blk)
            for hk in range(2):
                kexp = kexp_all[hk, pl.ds(r0, 3 * SWA_BLK), :]
                vexp = vexp_all[hk, pl.ds(r0, 3 * SWA_BLK), :]
                qs, p, ps, slot, rowg = _swa_probs(q[:, hk * 256:(hk + 1) * 256], kexp, band, sink_ref, hk)
                dy2 = dy_ref[rows, hk * 256:(hk + 1) * 256]
                dos = jnp.concatenate([jnp.where(slot == g, dy2, 0.0) for g in range(4)], axis=0)
                dp = _mm_nt(dos, vexp)
                delta = jnp.sum(p * dp, axis=-1, keepdims=True)
                ds = p * (dp - delta)
                dsr = -ps * delta
                for g in range(4):
                    dsk = dsk + jnp.where(hrow == hk * 4 + g,
                                          jnp.sum(jnp.where(rowg == g, dsr, 0.0), axis=0, keepdims=True), 0.0)
                dq4 = _mm(ds, kexp)
                dq2 = jnp.zeros((SWA_BLK, 256), F32)
                for g in range(4):
                    dq2 = dq2 + jnp.where(slot == g, dq4[g * SWA_BLK:(g + 1) * SWA_BLK], 0.0)
                dq_ref[rows, hk * 256:(hk + 1) * 256] = _rope_t(dq2 * 0.125, tq).astype(MX)
                dkacc[hk, pl.ds(r0, 3 * SWA_BLK), :] += _mm_tn(ds, qs)
                dvacc[hk, pl.ds(r0, 3 * SWA_BLK), :] += _mm_tn(p, dos)
        dsink_ref[...] += dsk

        @pl.when(n == L // tm - 1)
        def _():
            seq = slice(SWA_BLK, SWA_BLK + L)
            dk = _rope_t(_swa_fold(dkacc[0, seq], 0) + _swa_fold(dkacc[1, seq], 1), tk_ref[...])
            dkv_ref[:, 0:128] = dk.astype(MX)
            dkv_ref[:, 128:256] = (_swa_fold(dvacc[0, seq], 0) + _swa_fold(dvacc[1, seq], 1)).astype(MX)

    blk = lambda col: pl.BlockSpec((tm, 512), lambda s, n, sk: (s * (L // tm) + n, col))
    pad = pltpu.VMEM((2, L + 2 * SWA_BLK, 256), F32)
    return pl.pallas_call(
        body,
        grid_spec=pltpu.PrefetchScalarGridSpec(
            num_scalar_prefetch=1, grid=(NSEQ, L // tm),
            in_specs=[blk(2), pl.BlockSpec((L, 256), lambda s, n, sk: (s, 6)),
                      pl.BlockSpec((3, L, 128), lambda s, n, sk: (0, 0, 0)), blk(0)],
            out_specs=[blk(0), pl.BlockSpec((L, 256), lambda s, n, sk: (s, 0)),
                       pl.BlockSpec((8, 128), lambda s, n, sk: (0, 0))],
            scratch_shapes=[pad, pad, pad, pad, pltpu.VMEM((3, SWA_BLK, 3 * SWA_BLK), F32)]),
        out_shape=[_sds((N, 512), MX), _sds((N, 256), MX), _sds((8, 128))],
        name="swa_bwd", compiler_params=_cp(("arbitrary", "arbitrary")))(sink, h, h, tk, dyc)


def _outproj_bwd(dx1, s1, ya, yb, yc, wo, g):
    tm = 512
    nt = N // tm

    def body(dx1_ref, s_ref, ya_ref, yb_ref, yc_ref, wo_ref, g_ref,
             dya_ref, dyb_ref, dyc_ref, dxp_ref, dwo_ref, dg_ref, db_ref, acc):
        i = pl.program_id(0)

        @pl.when(i == 0)
        def _():
            acc[...] = jnp.zeros_like(acc)
            dg_ref[...] = jnp.zeros_like(dg_ref)
            db_ref[...] = jnp.zeros_like(db_ref)

        ds, dg, db = _ln_bwd(dx1_ref[...], s_ref[...], g_ref[...])
        dg_ref[...] += dg
        db_ref[...] += db
        dxp_ref[...] = ALPHA * ds
        dy = _mm_nt(ds, wo_ref[...])
        dya_ref[...] = dy[:, 0:256]
        dyb_ref[...] = dy[:, 256:512]
        dyc_ref[...] = dy[:, 512:1024].astype(MX)
        acc[0:256] += _mm_tn(ya_ref[...], ds)
        acc[256:512] += _mm_tn(yb_ref[...], ds)
        acc[512:1024] += _mm_tn(yc_ref[...], ds)

        @pl.when(i == nt - 1)
        def _():
            dwo_ref[...] = acc[...].astype(MX)

    row = lambda w_: pl.BlockSpec((tm, w_), lambda i: (i, 0))
    one = pl.BlockSpec((1, D), lambda i: (0, 0))
    full = pl.BlockSpec((D, D), lambda i: (0, 0))
    return pl.pallas_call(
        body, grid=(nt,),
        in_specs=[row(D), row(D), row(256), row(256), row(512), full, one],
        out_specs=[row(256), row(256), row(512), row(D), full, one, one],
        out_shape=[_sds((N, 256)), _sds((N, 256)), _sds((N, 512), MX), _sds((N, D)), _sds((D, D), MX), _sds((1, D)),
                   _sds((1, D))],
        scratch_shapes=[pltpu.VMEM((D, D), F32)],
        name="outproj_bwd", compiler_params=_cp(("arbitrary",)))(dx1, s1, ya, yb, yc, wo, g)


def _mix_ffn_fwd(ya, yb, yc, x, wo, g1, b1, w1, w2, g, b, target=None):
    tm = FFN_TM
    head = target is not None

    def body(*refs):
        ya_ref, yb_ref, yc_ref, xin_ref, wo_ref, g1_ref, b1_ref, w1_ref, w2_ref, g_ref, b_ref = refs[:11]
        s1_ref, x1_ref, a_ref, s_ref, y_ref = refs[11 + head:16 + head]
        mix = _mm(ya_ref[...], wo_ref[0:256]) + _mm(yb_ref[...], wo_ref[256:512]) + _mm(yc_ref[...], wo_ref[512:1024])
        s1 = ALPHA * xin_ref[...] + mix
        s1_ref[...] = s1
        x = _ln_fwd(s1, g1_ref[...], b1_ref[...])
        x1_ref[...] = x
        xb = x.astype(MX)
        s = ALPHA * x
        for j in range(NSHARD):
            a = _mm(xb, w1_ref[j])
            a_ref[:, j * D:(j + 1) * D] = a.astype(MX)
            s = s + _mm(jnp.square(jnp.maximum(a, 0.0)), w2_ref[j])
        s_ref[...] = s
        x2 = _ln_fwd(s, g_ref[...], b_ref[...])
        if not head:
            y_ref[...] = x2
            return
        l_ref = refs[16 + head]

        @pl.when(pl.program_id(0) == 0)
        def _():
            l_ref[...] = jnp.zeros_like(l_ref)

        e = x2 - refs[11][...]
        y_ref[...] = e * (1.0 / D)
        l_ref[...] += jnp.sum(jnp.sum(e * e, axis=1, keepdims=True), axis=0, keepdims=True) * (0.5 / D)

    rw = lambda w_: pl.BlockSpec((tm, w_), lambda i: (i, 0))
    row = rw(D)
    once = dict(pipeline_mode=pl.Buffered(1))
    wall = pl.BlockSpec((NSHARD, D, D), lambda i: (0, 0, 0), **once)
    one = pl.BlockSpec((1, D), lambda i: (0, 0))
    acc = pl.BlockSpec((8, 128), lambda i: (0, 0))
    return pl.pallas_call(
        body, grid=(N // tm,),
        in_specs=[rw(256), rw(256), rw(512), row, pl.BlockSpec((D, D), lambda i: (0, 0), **once), one, one,
                  wall, wall, one, one] + [row] * head,
        out_specs=[row, row, pl.BlockSpec((tm, DFF), lambda i: (i, 0)), row, row] + [acc] * head,
        out_shape=[_sds((N, D)), _sds((N, D)), _sds((N, DFF), MX), _sds((N, D)), _sds((N, D))] + [_sds((8, 128))] * head,
        name="mix_ffn_fwd", compiler_params=_cp(("arbitrary",), FFN_VMEM))(
            ya, yb, yc, x, wo, g1, b1, w1, w2, g, b, *([target] * head))


def _ffn_bwd_act(dy, s2, a, w1, w2, g):
    tm = FFN_TM

    def body(dy_ref, s_ref, a_ref, w1_ref, w2_ref, g_ref, da_ref, ds_ref, dx1_ref, dg_ref, db_ref):
        @pl.when(pl.program_id(0) == 0)
        def _():
            dg_ref[...] = jnp.zeros_like(dg_ref)
            db_ref[...] = jnp.zeros_like(db_ref)

        ds, dg, db = _ln_bwd(dy_ref[...], s_ref[...], g_ref[...])
        dsb = ds.astype(MX)
        ds_ref[...] = dsb
        dg_ref[...] += dg
        db_ref[...] += db
        dx1 = ALPHA * ds
        for j in range(NSHARD):
            da = (_mm_nt(dsb, w2_ref[j]) * 2.0 * jnp.maximum(a_ref[:, j * D:(j + 1) * D].astype(F32), 0.0)).astype(MX)
            da_ref[:, j * D:(j + 1) * D] = da
            dx1 = dx1 + _mm_nt(da, w1_ref[j])
        dx1_ref[...] = dx1

    row = pl.BlockSpec((tm, D), lambda i: (i, 0))
    wide = pl.BlockSpec((tm, DFF), lambda i: (i, 0))
    wall = pl.BlockSpec((NSHARD, D, D), lambda i: (0, 0, 0))
    one = pl.BlockSpec((1, D), lambda i: (0, 0))
    return pl.pallas_call(
        body, grid=(N // tm,),
        in_specs=[row, row, wide, wall, wall, one],
        out_specs=[wide, row, row, one, one],
        out_shape=[_sds((N, DFF), MX), _sds((N, D), MX), _sds((N, D)), _sds((1, D)), _sds((1, D))],
        name="ffn_bwd_act", compiler_params=_cp(("arbitrary",), FFN_VMEM))(dy, s2, a, w1, w2, g)


def _ffn_bwd_w(x1, da, a, ds):
    tm, nb = FFN_TM_W, FFN_WB
    nt = N // tm

    def body(x_ref, da_ref, a_ref, ds_ref, dw1_ref, dw2_ref, acc1, acc2):
        i = pl.program_id(1)

        @pl.when(i == 0)
        def _():
            acc1[...] = jnp.zeros_like(acc1)
            acc2[...] = jnp.zeros_like(acc2)

        x, ds_ = x_ref[...], ds_ref[...]
        for k in range(nb):
            cols = slice(k * D, (k + 1) * D)
            acc1[k] += _mm_tn(x, da_ref[:, cols])
            acc2[k] += _mm_tn(jnp.square(jnp.maximum(a_ref[:, cols].astype(F32), 0.0)), ds_)

        @pl.when(i == nt - 1)
        def _():
            dw1_ref[...] = acc1[...].astype(MX)
            dw2_ref[...] = acc2[...].astype(MX)

    row = pl.BlockSpec((tm, D), lambda j, i: (i, 0))
    col = pl.BlockSpec((tm, nb * D), lambda j, i: (i, j))
    wj = pl.BlockSpec((nb, D, D), lambda j, i: (j, 0, 0))
    return pl.pallas_call(
        body, grid=(NSHARD // nb, nt),
        in_specs=[row, col, col, row], out_specs=[wj, wj],
        out_shape=[_sds((NSHARD, D, D), MX), _sds((NSHARD, D, D), MX)],
        scratch_shapes=[pltpu.VMEM((nb, D, D), F32), pltpu.VMEM((nb, D, D), F32)],
        name="ffn_bwd_w", compiler_params=_cp(("parallel", "arbitrary"), FFN_VMEM))(x1, da, a, ds)


def _s5_discretize(a_re, a_im, log_step, b_re, b_im):
    lam = lax.complex(a_re, a_im)
    lam_bar = jnp.exp(lam * jnp.exp(log_step))
    b_bar = ((lam_bar - 1.0) / lam)[..., None] * lax.complex(b_re, b_im)
    return jnp.real(lam_bar), jnp.imag(lam_bar), jnp.real(b_bar), jnp.imag(b_bar)


def _s5_in_blocks(b):
    e = jnp.eye(8, dtype=F32)
    return jnp.einsum('ij,zbjph->zbihjp', e, b.reshape(2, 2, 8, S5_P, S5_H)).reshape(2, 2, 128, SW)


def _s5_out_blocks(c):
    e = jnp.eye(8, dtype=F32)
    return jnp.einsum('ij,zbjhp->zbjpih', e, c.reshape(2, 2, 8, S5_H, S5_P)).reshape(2, 2, SW, 128)


def _gate_weight(w_a):
    z = jnp.zeros((16, 128), F32)
    top = jnp.concatenate([w_a[0], z], axis=1)
    bot = jnp.concatenate([z, w_a[1]], axis=1)
    return jnp.concatenate([top, bot, jnp.zeros((96, 256), F32)], axis=0)


def _layer_prep(p):
    lr, li, br, bi = _s5_discretize(p["s5_a_re"], p["s5_a_im"], p["s5_log_step"], p["s5_b_re"], p["s5_b_im"])
    q = dict(p)
    q["bre"] = _s5_in_blocks(br).astype(MX)
    q["bim"] = _s5_in_blocks(bi).astype(MX)
    q["cre"] = _s5_out_blocks(p["s5_c_re"]).astype(MX)
    q["cim"] = _s5_out_blocks(p["s5_c_im"]).astype(MX)
    mr, mi = lr.reshape(2, 1024), li.reshape(2, 1024)
    q["tab"], q["tabc"] = _lockstep_tables(mr, mi)
    q["dsk"] = p["s5_d"].reshape(1, 256)
    q["wa"] = _gate_weight(p["gla_w_a"]).astype(MX)
    q["ba"] = p["gla_b_a"].reshape(1, 256)
    q["lng"] = p["gla_ln_g"].reshape(1, 256)
    q["bv"] = p["s5_b_glu"][:256].reshape(1, 256)
    q["bg"] = p["s5_b_glu"][256:].reshape(1, 256)
    for k in ("ln1_g", "ln1_b", "ln2_g", "ln2_b"):
        q[k] = p[k].reshape(1, D)
    return q


def _layer_fwd(x, q, tk, fetch, target=None):
    q["w_in"] = fetch("w_in", x)
    h, la2 = _inproj_fwd(x, q["w_in"], q["wa"], q["ba"])
    hre, him, y2 = _s5_fwd(h, q["bre"], q["bim"], q["cre"], q["cim"], q["tab"])
    q["w4"] = fetch("s5_w_glu", y2)
    ya = _s5_glu_fwd(y2, h, q["dsk"], q["w4"], q["bv"], q["bg"])
    of, ob, sf, sb = _gla_fwd(h, la2)
    yb = _gla_post_fwd(of, ob, h, q["lng"])
    yc = _swa_fwd(h, tk, q["swa_sink"])
    mixed = ya[:8, :128] + yb[:8, :128] + yc[:8, :128]
    q["w_out"] = fetch("w_out", mixed)
    q["w_ff1"] = fetch("w_ff1", mixed)
    q["w_ff2"] = fetch("w_ff2", mixed)
    s1, x1, a, s2, *out = _mix_ffn_fwd(ya, yb, yc, x, q["w_out"], q["ln1_g"], q["ln1_b"], q["w_ff1"], q["w_ff2"],
                                       q["ln2_g"], q["ln2_b"], target)
    saved = dict(x=x, h=h, hre=hre, him=him, y2=y2, ya=ya, la2=la2, of=of, ob=ob, sf=sf, sb=sb, yb=yb, yc=yc,
                 s1=s1, x1=x1, a=a, s2=s2)
    return (out[0] if target is None else tuple(out)), saved


def _layer_bwd(dy, q, sv, tk, emit):
    g = {}
    da, ds2, dx1, g["dg2"], g["db2"] = _ffn_bwd_act(dy, sv["s2"], sv["a"], q["w_ff1"], q["w_ff2"], q["ln2_g"])
    dw1, dw2 = _ffn_bwd_w(sv["x1"], da, sv["a"], ds2)
    tie = emit(dict(w_ff1=dw1, w_ff2=dw2))
    dya, dyb, dyc, dxp, dwo, g["dg1"], g["db1"] = _outproj_bwd(dx1, sv["s1"], sv["ya"], sv["yb"], sv["yc"],
                                                               q["w_out"], q["ln1_g"] + tie)
    h = sv["h"]
    daq, dakv, g["dsink"] = _swa_bwd(h, tk, q["swa_sink"], dyc)
    do, gr, g["dlng"] = _gla_post_bwd(sv["of"], sv["ob"], h, q["lng"], dyb)
    gq_f, gk_f, gv_f, gl_f, gq_b, gk_b, gv_b, gl_b = _gla_bwd(h, sv["la2"], do, sv["sf"], sv["sb"])
    dyp, dud, g["dd"], dw4, g["dbv"], g["dbg"] = _s5_glu_bwd(sv["y2"], h, q["dsk"], q["w4"], q["bv"], q["bg"], dya)
    tie = emit(dict(w_out=dwo.reshape(NSHARD, D // NSHARD, D), s5_w_glu=dw4))
    du2, g["dbre"], g["dbim"], g["dcre"], g["dcim"], g["dmu"] = _s5_bwd(
        h, dyp, sv["hre"], sv["him"], q["bre"], q["bim"], q["cre"], q["cim"], (q["tabc"][0], q["tabc"][1] + tie))
    dx, dwt, g["dwa"], g["dba"] = _inproj_bwd(sv["x"], q["w_in"], dxp, du2, dud, gq_f, gq_b, gk_f, gk_b, gv_f, gv_b, gr,
                                              daq, dakv, h, q["wa"], q["ba"], gl_f, gl_b)
    tie = emit(dict(w_in=dwt))
    return dx, g, tie


NATIVE = ("dmu", "dbre", "dbim", "dcre", "dcim", "dd", "dbv", "dbg", "dwa", "dba", "dlng", "dsink",
          "dg1", "db1", "dg2", "db2", "loss")
ICI_CORE = (0, 0, 0, 1, 1, 0, 0, 0, 1, 1, 1, 1, 0, 0, 1, 1, 0)
Y_FIRST = (0, 0, 1, 0, 1, 1, 0, 1, 0, 1, 0, 1, 0, 1, 0, 1, 0)


def _finish_small(n, w):
    g = {}
    dmu = n["dmu"]
    dlr = dmu[:, :, :, 0].reshape(DEPTH, 2, S5_G, S5_P)
    dli = dmu[:, :, :, 1].reshape(DEPTH, 2, S5_G, S5_P)

    def unblock(c, perm, shape):
        return c.reshape(DEPTH, 2, 2, S5_H, 8, S5_P).transpose(perm).reshape(shape)

    b_shape, c_shape = (DEPTH, 2, S5_G, S5_P, S5_H), (DEPTH, 2, S5_G, S5_H, S5_P)
    _, vjp = jax.vjp(_s5_discretize, w["s5_a_re"], w["s5_a_im"], w["s5_log_step"], w["s5_b_re"], w["s5_b_im"])
    (g["s5_a_re"], g["s5_a_im"], g["s5_log_step"], g["s5_b_re"], g["s5_b_im"]) = vjp(
        (dlr, dli, unblock(n["dbre"], (0, 1, 2, 4, 5, 3), b_shape), unblock(n["dbim"], (0, 1, 2, 4, 5, 3), b_shape)))
    g["s5_c_re"] = unblock(n["dcre"], (0, 1, 2, 4, 3, 5), c_shape)
    g["s5_c_im"] = unblock(n["dcim"], (0, 1, 2, 4, 3, 5), c_shape)
    g["s5_d"] = n["dd"].reshape(DEPTH, S5_G, S5_H)
    g["s5_b_glu"] = jnp.concatenate([n["dbv"], n["dbg"]], axis=2).reshape(DEPTH, 512)
    g["gla_w_a"] = jnp.stack([n["dwa"][:, 0:16, 0:128], n["dwa"][:, 16:32, 128:256]], axis=1)
    g["gla_b_a"] = n["dba"].reshape(DEPTH, 2, 128)
    g["gla_ln_g"] = n["dlng"].reshape(DEPTH, 256)
    g["swa_sink"] = n["dsink"][:, :, 0]
    for k, s in (("ln1_g", "dg1"), ("ln1_b", "db1"), ("ln2_g", "dg2"), ("ln2_b", "db2")):
        g[k] = n[s].reshape(DEPTH, D)
    return g


def _local_step(x, target, qs, tk, fetch, emit):
    saved = []
    for l, q in enumerate(qs):
        x, sv = _layer_fwd(x, q, tk, functools.partial(fetch, l), target if l == DEPTH - 1 else None)
        saved.append(sv)
    dy, lacc = x
    smalls = [None] * DEPTH
    tie = 0.0
    for l in reversed(range(DEPTH)):
        qs[l]["ln2_g"] = qs[l]["ln2_g"] + tie
        dy, smalls[l], tie = _layer_bwd(dy, qs[l], saved[l], tk, functools.partial(emit, l))
    smalls[0]["db2"] = smalls[0]["db2"] + tie
    for l in range(DEPTH):
        smalls[l]["loss"] = lacc if l == 0 else jnp.zeros_like(lacc)
    return lacc[0, 0], dy, smalls


BIG = ("w_in", "s5_w_glu", "w_out", "w_ff1", "w_ff2")
SMALL = ("s5_a_re", "s5_a_im", "s5_log_step", "s5_b_re", "s5_b_im", "s5_c_re", "s5_c_im", "s5_d", "s5_b_glu",
         "gla_w_a", "gla_b_a", "gla_ln_g", "swa_sink", "ln1_g", "ln1_b", "ln2_g", "ln2_b")
ANY = pl.BlockSpec(memory_space=pl.ANY)


def _place():
    x, y, c = lax.axis_index("x"), lax.axis_index("y"), lax.axis_index("c")
    return x, y, c, [(1 - x, y), (x, 1 - y), (1 - x, 1 - y)]


HBM = pl.BlockSpec(memory_space=pltpu.HBM)
SEMS = pl.BlockSpec(memory_space=pltpu.SEMAPHORE)
EFFECT = pltpu.SideEffectType.DATAFLOW_SIDE_EFFECTING


def _push_copies(ins, lands, send, recv, gather, sending):
    x, y, c, chips = _place()
    me = 2 * x + y
    if gather == "sibling":
        return [pltpu.make_async_remote_copy(src_ref=ins[a], dst_ref=lands[a], send_sem=send.at[a], recv_sem=recv.at[a],
                                             device_id=(x, y, 1 - c), device_id_type=MESH) for a in range(len(lands))]
    out = []
    for a in range(len(lands)):
        for j, (px, py) in enumerate(chips):
            peer = 2 * px + py
            src = lands[a].at[me] if gather else ins[a].at[peer if sending else me]
            dst = lands[a].at[me if sending else peer]
            out.append(pltpu.make_async_remote_copy(src_ref=src, dst_ref=dst, send_sem=send.at[3 * a + j],
                                                    recv_sem=recv.at[3 * a + j], device_id=(px, py, c),
                                                    device_id_type=MESH))
    return out


def _push_start(name, arrs, gather):
    n = len(arrs)
    ops = list(arrs) if gather is True else list(arrs) + [lax.empty(s.shape, s.dtype) for s in arrs]
    m = len(ops)

    def body(*refs):
        ins, lnd = (refs[:n], refs[:n]) if gather is True else (refs[:n], refs[n:m])
        for cp in _push_copies(ins, lnd, refs[m], refs[m + 1], gather, True):
            cp.start()
        refs[-1][...] = jnp.zeros((8, 128), F32)

    ops = [pltpu.with_memory_space_constraint(t, pltpu.HBM) for t in ops]
    res = pl.pallas_call(
        body, name=name,
        out_shape=(pltpu.SemaphoreType.DMA((3 * n,)), pltpu.SemaphoreType.DMA((3 * n,)),
                   *[pltpu.HBM(t.shape, t.dtype) for t in ops], _sds((8, 128))),
        in_specs=[HBM] * m,
        out_specs=(SEMS, SEMS, *[HBM] * m, pl.BlockSpec(memory_space=pltpu.VMEM)),
        input_output_aliases={i: 2 + i for i in range(m)},
        compiler_params=pltpu.CompilerParams(has_side_effects=EFFECT))(*ops)
    return res[0], res[1], list(res[2:2 + m]), res[-1]


def _push_wait(name, started, after, gather):
    send, recv, ops, _ = started
    m = len(ops)
    n = m if gather is True else m // 2

    def body(*refs):
        ins, lnd = (refs[:n], refs[:n]) if gather is True else (refs[:n], refs[n:m])
        for cp in _push_copies(ins, lnd, refs[m], refs[m + 1], gather, False):
            cp.wait_send()
            cp.wait_recv()

    res = pl.pallas_call(
        body, name=name,
        out_shape=[pltpu.HBM(t.shape, t.dtype) for t in ops],
        in_specs=[HBM] * m + [SEMS, SEMS, ANY], out_specs=[HBM] * m,
        input_output_aliases={i: i for i in range(m)},
        compiler_params=pltpu.CompilerParams(has_side_effects=EFFECT))(*ops, send, recv, after)
    return list(res)


def _row_tile(rows):
    return max(t for t in range(8, min(rows, 512) + 1, 8) if rows % t == 0)


def _cast_to_slot(me, w, l):
    _, rows, cols = w.shape
    tr = _row_tile(rows)

    def body(me_ref, w_ref, o_ref):
        o_ref[0] = w_ref[0].astype(MX)

    return pl.pallas_call(
        body,
        grid_spec=pltpu.PrefetchScalarGridSpec(
            num_scalar_prefetch=1, grid=(rows // tr,),
            in_specs=[pl.BlockSpec((1, tr, cols), lambda i, me_: (l, i, 0))],
            out_specs=pl.BlockSpec((1, tr, cols), lambda i, me_: (me_[0], i, 0))),
        out_shape=_sds((NSHARD, rows, cols), MX), name="cast_to_slot", compiler_params=_cp(("arbitrary",)))(me, w)


def _sum_sources(me, recv, own):
    _, rows, cols = recv[0].shape
    tr = min(_row_tile(rows), 256) if rows % 256 == 0 else _row_tile(rows)
    nt = rows // tr

    def body(me_ref, *refs):
        o_ref = refs[-1]
        for l in range(DEPTH):
            @pl.when(pl.program_id(0) == l)
            def _():
                r_ref, own_ref = refs[2 * l], refs[2 * l + 1]
                part = [jnp.where(me_ref[0] == s, own_ref[0], r_ref[s]).astype(F32) for s in range(NSHARD)]
                o_ref[...] = ((part[0] + part[1]) + part[2]) + part[3]

    in_specs = []
    for l in range(DEPTH):
        pick = lambda g, i, me_, l=l: jnp.where(g == l, i, jnp.where(g < l, 0, nt - 1))
        in_specs += [pl.BlockSpec((NSHARD, tr, cols), lambda g, i, me_, pick=pick: (0, pick(g, i, me_), 0)),
                     pl.BlockSpec((1, tr, cols), lambda g, i, me_, pick=pick: (me_[0], pick(g, i, me_), 0))]
    return pl.pallas_call(
        body,
        grid_spec=pltpu.PrefetchScalarGridSpec(
            num_scalar_prefetch=1, grid=(DEPTH, nt), in_specs=in_specs,
            out_specs=pl.BlockSpec((tr, cols), lambda g, i, me_: (g * nt + i, 0))),
        out_shape=_sds((DEPTH * rows, cols)), name="sum_sources",
        compiler_params=_cp(("arbitrary", "arbitrary")))(me, *[t for l in range(DEPTH) for t in (recv[l], own[l])])


def _allreduce_small(per_layer):
    nk = len(per_layer[0])
    n = DEPTH * nk
    shapes = [a.shape for a in per_layer[0]]

    def body(*refs):
        ins, outs = refs[:n], refs[n:n + nk]
        sibs, slots = refs[n + nk:n + 2 * nk], refs[n + 2 * nk:n + 3 * nk]
        send, recv = refs[n + 3 * nk:]
        x, y, c, chips = _place()
        me = 2 * x + y
        d2d = [pltpu.make_async_remote_copy(src_ref=ins[l * nk + k], dst_ref=sibs[k].at[l], send_sem=send.at[l * nk + k],
                                            recv_sem=recv.at[l * nk + k], device_id=(x, y, 1 - c), device_id_type=MESH)
               for l in range(DEPTH) for k in range(nk)]
        for cp in d2d:
            cp.start()
        for cp in d2d:
            cp.wait()
        for l in range(DEPTH):
            for k in range(nk):
                slots[k][0, l] = ins[l * nk + k][...] + sibs[k][l]

        def swap(k, stage):
            peer = (1 - x, y, c) if stage == Y_FIRST[k] else (x, 1 - y, c)
            return pltpu.make_async_remote_copy(src_ref=slots[k].at[2 * stage], dst_ref=slots[k].at[2 * stage + 1],
                                                send_sem=send.at[n + 3 * k + stage], recv_sem=recv.at[n + 3 * k + stage],
                                                device_id=peer, device_id_type=MESH)

        def handover(k):
            return pltpu.make_async_remote_copy(src_ref=outs[k], dst_ref=outs[k], send_sem=send.at[n + 3 * nk + k],
                                                recv_sem=recv.at[n + 3 * nk + k], device_id=(x, y, 1 - c),
                                                device_id_type=MESH)

        halves = (tuple(k for k in range(nk) if ICI_CORE[k] == 0), tuple(k for k in range(nk) if ICI_CORE[k] == 1))
        for cc in range(2):
            @pl.when(c == cc)
            def _():
                mine, theirs = halves[cc], halves[1 - cc]
                for stage in range(2):
                    cps = [swap(k, stage) for k in mine]
                    for cp in cps:
                        cp.start()
                    for cp in cps:
                        cp.wait()
                    for k in mine:
                        if stage == 0:
                            slots[k][2] = slots[k][0] + slots[k][1]
                        else:
                            outs[k][...] = slots[k][2] + slots[k][3]
                over = [handover(k) for k in mine]
                for cp in over:
                    cp.start()
                for k in theirs:
                    handover(k).wait_recv()
                for cp in over:
                    cp.wait_send()

    vm = pl.BlockSpec(memory_space=pltpu.VMEM)
    return pl.pallas_call(
        body, in_specs=[vm] * n, out_specs=[vm] * nk, out_shape=[_sds((DEPTH,) + s) for s in shapes],
        scratch_shapes=([pltpu.VMEM((DEPTH,) + s, F32) for s in shapes]
                        + [pltpu.VMEM((NSHARD, DEPTH) + s, F32) for s in shapes]
                        + [pltpu.SemaphoreType.DMA((n + 4 * nk,)), pltpu.SemaphoreType.DMA((n + 4 * nk,))]),
        name="allreduce_small", compiler_params=pltpu.CompilerParams(vmem_limit_bytes=VMEM_LIMIT))(
            *[a for layer in per_layer for a in layer])


def _adamw_math(w, g, m, v):
    m = ADAM_B1 * m + (1.0 - ADAM_B1) * g
    v = ADAM_B2 * v + (1.0 - ADAM_B2) * jnp.square(g)
    m_hat = m / (1.0 - ADAM_B1 ** ADAM_STEP)
    v_hat = v / (1.0 - ADAM_B2 ** ADAM_STEP)
    delta = -ADAM_LR * (m_hat / (jnp.sqrt(v_hat) + ADAM_EPS) + ADAM_WD * w)
    return delta, m, v


def _adamw(g_parts, w, m, v):
    rows, cols = w.shape
    tr = 256 if rows % 256 == 0 else _row_tile(rows)
    k = len(g_parts)

    def body(*refs):
        g = refs[0][...]
        for r in refs[1:k]:
            g = g + r[...]
        w_ref, m_ref, v_ref, go, do, mo, vo = refs[k:]
        d, mn, vn = _adamw_math(w_ref[...], g, m_ref[...], v_ref[...])
        go[...] = g
        do[...] = d
        mo[...] = mn
        vo[...] = vn

    spec = pl.BlockSpec((tr, cols), lambda i: (i, 0))
    return pl.pallas_call(
        body, grid=(rows // tr,), in_specs=[spec] * (k + 3), out_specs=[spec] * 4,
        out_shape=[_sds((rows, cols))] * 4, name="adamw", compiler_params=_cp(("parallel",)))(*g_parts, w, m, v)


def _adamw_small(gs, ws, ms, vs):
    n = len(gs)

    def body(*refs):
        for k in range(n):
            d, mn, vn = _adamw_math(refs[n + k][...], refs[k][...], refs[2 * n + k][...], refs[3 * n + k][...])
            refs[4 * n + k][...] = d
            refs[5 * n + k][...] = mn
            refs[6 * n + k][...] = vn

    vm = pl.BlockSpec(memory_space=pltpu.VMEM)
    shapes = [_sds(a.shape) for a in ws]
    res = pl.pallas_call(
        body, in_specs=[vm] * (4 * n), out_specs=[vm] * (3 * n), out_shape=shapes * 3, name="adamw_small",
        compiler_params=pltpu.CompilerParams(vmem_limit_bytes=VMEM_LIMIT))(*gs, *ws, *ms, *vs)
    return res[:n], res[n:2 * n], res[2 * n:]


_ARGS = ("x", "w_in", "s5_a_re", "s5_a_im", "s5_log_step", "s5_b_re", "s5_b_im", "s5_c_re", "s5_c_im", "s5_d",
         "s5_w_glu", "s5_b_glu", "gla_w_a", "gla_b_a", "gla_ln_g", "swa_sink", "w_out", "ln1_g", "ln1_b", "w_ff1",
         "w_ff2", "ln2_g", "ln2_b")
_WEIGHTS = _ARGS[1:]


def kernel(x, w_in, s5_a_re, s5_a_im, s5_log_step, s5_b_re, s5_b_im, s5_c_re, s5_c_im, s5_d, s5_w_glu, s5_b_glu, gla_w_a, gla_b_a, gla_ln_g, swa_sink, w_out, ln1_g, ln1_b, w_ff1, w_ff2, ln2_g, ln2_b, loss_target, m_w_in, m_s5_a_re, m_s5_a_im, m_s5_log_step, m_s5_b_re, m_s5_b_im, m_s5_c_re, m_s5_c_im, m_s5_d, m_s5_w_glu, m_s5_b_glu, m_gla_w_a, m_gla_b_a, m_gla_ln_g, m_swa_sink, m_w_out, m_ln1_g, m_ln1_b, m_w_ff1, m_w_ff2, m_ln2_g, m_ln2_b, v_w_in, v_s5_a_re, v_s5_a_im, v_s5_log_step, v_s5_b_re, v_s5_b_im, v_s5_c_re, v_s5_c_im, v_s5_d, v_s5_w_glu, v_s5_b_glu, v_gla_w_a, v_gla_b_a, v_gla_ln_g, v_swa_sink, v_w_out, v_ln1_g, v_ln1_b, v_w_ff1, v_w_ff2, v_ln2_g, v_ln2_b):
    given = dict(locals())
    w = {k: given[k] for k in _WEIGHTS}
    mom = {k: given["m_" + k] for k in _WEIGHTS}
    var = {k: given["v_" + k] for k in _WEIGHTS}

    me = (2 * lax.axis_index("x") + lax.axis_index("y")).astype(jnp.int32).reshape(1)
    tr = lambda t: t.transpose(0, 2, 1)
    shard = {k: (tr(w[k]) if k == "w_in" else w[k]) for k in BIG}
    qs = [None] * DEPTH

    first = ("w_in", "s5_w_glu", "w_out")
    follow = {(0, "w_in"): [(0, first[1:]), (0, BIG[3:])], (0, "s5_w_glu"): [(1, first)], (0, "w_ff1"): [(1, BIG[3:])]}
    gathers = {}

    casts = {}

    def start_gather(l, names, behind=None):
        lands = [casts.pop((l, k)) if (l, k) in casts else _cast_to_slot(me, shard[k], l) for k in names]
        if behind is not None:
            lands, behind = lax.optimization_barrier((lands, behind))
        st = _push_start(f"gather_start_{l}_{names[0]}", lands, True)
        for k in names:
            gathers[l, k] = [names, st, None]
        return st[-1], behind

    token = start_gather(0, first[:1])[0]
    zero = token[0, 0]
    for l in range(DEPTH):
        for k in BIG:
            if (l, k) not in gathers:
                casts[l, k] = _cast_to_slot(me, lax.optimization_barrier((shard[k], token))[0], l)
        qs[l] = _layer_prep({k: (w[k][l] + zero if k == "s5_a_re" else w[k][l]) for k in SMALL})
    token, casts, qs = lax.optimization_barrier((token, casts, qs))

    def fetch(l, name, after):
        names, st, got = gathers[l, name]
        tie = None
        if got is None:
            if l == 0 and name == "w_in":
                after = token
            lands = _push_wait(f"gather_wait_{l}_{names[0]}", st, after, True)
            for l2, names2 in follow.get((l, name), ()):
                tok, lands[0] = start_gather(l2, names2, lands[0])
                tie = tok if tie is None else tie + tok
            got = dict(zip(names, lands))
            for k in names:
                gathers[l, k][2] = got
        full = got[name]
        if name == "w_in":
            return _in_rows(full, token if tie is None else tie)
        if tie is not None:
            near = "bv" if name == "s5_w_glu" else "ln2_b"
            qs[l][near] = qs[l][near] + tie[0, 0]
        return full.reshape(D, D) if name == "w_out" else full

    scatters, held = [], {}

    def emit(l, grads):
        if l > 0:
            held.update(grads)
            if "w_in" not in grads:
                return 0.0
            grads = dict(held)
            held.clear()
        names = tuple(grads)
        st = _push_start(f"scatter_start_{l}_{names[0]}", [grads[k] for k in names], False)
        scatters.append((l, names, st))
        return st[-1][0, 0]

    loss, dx, smalls = _local_step(x.reshape(N, D), loss_target.reshape(N, D), qs, _rope_tables(128), fetch, emit)

    out, recv, own = {}, {}, {}

    def collect(keys, after):
        for l, names, st in scatters:
            if names[0] in keys:
                ops = _push_wait(f"scatter_wait_{l}_{names[0]}", st, after, False)
                for i, k in enumerate(names):
                    own[l, k], recv[l, k] = ops[i], ops[len(names) + i]

    def shard_sums(keys):
        return [_sum_sources(me, [recv[l, k] for l in range(DEPTH)], [own[l, k] for l in range(DEPTH)]) for k in keys]

    def to_sibling(keys, sums):
        return _push_start(f"swap_start_{keys[0]}", sums, "sibling")

    def apply(keys, started, after):
        ops = _push_wait(f"swap_wait_{keys[0]}", started, after, "sibling")
        for i, k in enumerate(keys):
            mine, other = ops[i], ops[len(keys) + i]
            shp = shard[k].shape
            r = _adamw([mine, other], *((tr(t[k]) if k == "w_in" else t[k]).reshape(-1, shp[-1]) for t in (w, mom, var)))
            r = [t.reshape(shp) for t in r]
            out[k] = [tr(t) for t in r] if k == "w_in" else r
        return out[keys[-1]][1]

    collect(("w_ff1", "w_ff2", "w_out", "s5_w_glu"), dx)
    sums = shard_sums(("w_ff1", "w_ff2", "w_out", "s5_w_glu"))
    sums, smalls[0]["db1"] = lax.optimization_barrier((sums, smalls[0]["db1"]))
    native = _allreduce_small([[smalls[l][k] for k in NATIVE] for l in range(DEPTH)])
    sums, native = lax.optimization_barrier((sums, native))
    ff = to_sibling(("w_ff1", "w_ff2"), sums[:2])
    mix = to_sibling(("w_out", "s5_w_glu"), sums[2:])
    native = dict(zip(NATIVE, native))
    native["db1"] = native["db1"] + (ff[-1][0, 0] + mix[-1][0, 0])
    loss = native["loss"][0, 0, 0] + native["loss"][1, 0, 0]
    gsmall = _finish_small(native, w)
    view = lambda k, t: t.transpose(0, 1, 2, 4, 3) if k in ("s5_b_re", "s5_b_im") else t
    res = _adamw_small(*([view(k, t[k]) for k in SMALL] for t in (gsmall, w, mom, var)))
    for i, k in enumerate(SMALL):
        out[k] = [gsmall[k]] + [view(k, r[i]) for r in res]
    last = apply(("w_ff1", "w_ff2"), ff, res[0][-1])
    collect(("w_in",), last)
    win = to_sibling(("w_in",), shard_sums(("w_in",)))
    last = apply(("w_out", "s5_w_glu"), mix, win[-1])
    apply(("w_in",), win, last)

    return (loss, dx.reshape(NSEQ, L, D), *[out[k][0] for k in _WEIGHTS], *[out[k][1] for k in _WEIGHTS],
            *[out[k][2] for k in _WEIGHTS], *[out[k][3] for k in _WEIGHTS])
```

```python
import functools
import math

import jax
import jax.numpy as jnp
from jax import lax
from jax.experimental import pallas as pl
from jax.experimental.pallas import tpu as pltpu

F32 = jnp.float32
MX = jnp.bfloat16
MESH = pl.DeviceIdType.MESH

DEPTH = 2
NSEQ = 2
L = 2048
N = NSEQ * L
D = 1024
DFF = 4096
NSHARD = 4
S5_G, S5_H, S5_P = 16, 16, 64
GLA_CHUNK = 64
NCHUNK = L // GLA_CHUNK
GLA_GROUP = 4
NGROUP = NCHUNK // GLA_GROUP
SWA_BLK = 128
NBLK = L // SWA_BLK
SWA_PER = 2
ROT = 16
ROPE_THETA = 500000.0
LN_EPS = 1e-5
ALPHA = (2 * DEPTH) ** 0.25
NEG_BIG = -1e30
DIN = 1824
DINP = 1920
ADAM_LR, ADAM_B1, ADAM_B2, ADAM_EPS, ADAM_WD, ADAM_STEP = 0.001, 0.9, 0.999, 1e-08, 0.01, 10
VMEM_LIMIT = 56 * 1024 * 1024
TT = 512
SW = 512
FFN_TM = 512
FFN_TM_W = 1024
FFN_WB = 1
FFN_VMEM = 60 * 1024 * 1024
INPROJ_BWD_TM = 512


def _cp(sem, vmem=VMEM_LIMIT):
    return pltpu.CompilerParams(dimension_semantics=sem, vmem_limit_bytes=vmem)


def _mm(a, b):
    return jnp.dot(a.astype(MX), b.astype(MX), preferred_element_type=F32)


def _mm_nt(a, b):
    return lax.dot_general(a.astype(MX), b.astype(MX), (((1,), (1,)), ((), ())), preferred_element_type=F32)


def _mm_tn(a, b):
    return lax.dot_general(a.astype(MX), b.astype(MX), (((0,), (0,)), ((), ())), preferred_element_type=F32)


@jax.custom_vjp
def _dmm(a, b):
    return _mm(a, b)


_dmm.defvjp(lambda a, b: (_mm(a, b), (a, b)), lambda r, g: (_mm_nt(g, r[1]), _mm_tn(r[0], g)))


@jax.custom_vjp
def _dmm_nt(a, b):
    return _mm_nt(a, b)


_dmm_nt.defvjp(lambda a, b: (_mm_nt(a, b), (a, b)), lambda r, g: (_mm(g, r[1]), _mm_tn(g, r[0])))


@jax.custom_vjp
def _dmm_tn(a, b):
    return _mm_tn(a, b)


_dmm_tn.defvjp(lambda a, b: (_mm_tn(a, b), (a, b)), lambda r, g: (_mm_nt(r[1], g), _mm(r[0], g)))


def _split3(x):
    hi = x.astype(MX)
    r1 = x - hi.astype(F32)
    mid = r1.astype(MX)
    lo = (r1 - mid.astype(F32)).astype(MX)
    return hi, mid, lo


def _chunk_pairs(rows, rev, strict):
    r = lax.broadcasted_iota(jnp.int32, (rows, rows), 0)
    c = lax.broadcasted_iota(jnp.int32, (rows, rows), 1)
    order = ((c > r) if strict else (c >= r)) if rev else ((c < r) if strict else (c <= r))
    return (r // GLA_CHUNK == c // GLA_CHUNK) & order


def _cums_impl(x, rev):
    rows, w = x.shape
    t = jnp.where(_chunk_pairs(rows, rev, False), 1.0, 0.0).astype(MX)
    s = jnp.dot(t, jnp.concatenate(_split3(x), axis=1), preferred_element_type=F32)
    return s[:, 0:w] + s[:, w:2 * w] + s[:, 2 * w:3 * w]


@functools.partial(jax.custom_vjp, nondiff_argnums=(1,))
def _cums(x, rev):
    return _cums_impl(x, rev)


_cums.defvjp(lambda x, rev: (_cums_impl(x, rev), None), lambda rev, r, g: (_cums_impl(g, not rev),))


def _ln_fwd(s, g, b):
    mu = jnp.mean(s, axis=-1, keepdims=True)
    xc = s - mu
    var = jnp.mean(xc * xc, axis=-1, keepdims=True)
    return xc * lax.rsqrt(var + LN_EPS) * g + b


def _ln_bwd(dy, s, g):
    mu = jnp.mean(s, axis=-1, keepdims=True)
    xc = s - mu
    var = jnp.mean(xc * xc, axis=-1, keepdims=True)
    rstd = lax.rsqrt(var + LN_EPS)
    xhat = xc * rstd
    dxh = dy * g
    ds = rstd * (dxh - jnp.mean(dxh, axis=-1, keepdims=True) - xhat * jnp.mean(dxh * xhat, axis=-1, keepdims=True))
    return ds, jnp.sum(dy * xhat, axis=0, keepdims=True), jnp.sum(dy, axis=0, keepdims=True)


def _sds(shape, dtype=F32):
    return jax.ShapeDtypeStruct(shape, dtype)


_IN_ROW_PIECES = (((0, 0), (0, 456)), ((1, 0), (456, 456)), ((2, 0), (912, 112)), ((2, 112), (1792, 32)),
                  ((2, 144), (1024, 312)), ((3, 0), (1336, 456)))


def _in_rows(g4, behind):
    def body(g_ref, behind_ref, o_ref, tmp):
        tmp[DIN:DINP] = jnp.zeros((DINP - DIN, D), F32)
        for (j, s0), (d0, n_) in _IN_ROW_PIECES:
            tmp[d0:d0 + n_] = g_ref[j, s0:s0 + n_].astype(F32)
        o_ref[...] = tmp[...].astype(MX)

    vm = pl.BlockSpec(memory_space=pltpu.VMEM)
    return pl.pallas_call(body, in_specs=[vm, pl.BlockSpec(memory_space=pl.ANY)], out_specs=vm,
                          out_shape=_sds((DINP, D), MX), scratch_shapes=[pltpu.VMEM((DINP, D), F32)], name="in_rows",
                          compiler_params=pltpu.CompilerParams(vmem_limit_bytes=VMEM_LIMIT))(g4, behind)


def _inproj_fwd(x, wt, wa, ba):
    tm = 512

    def body(x_ref, w_ref, wa_ref, ba_ref, h_ref, la_ref):
        h = _mm_nt(x_ref[...], w_ref[...])
        h_ref[...] = h
        la_ref[...] = _logsig(_mm(h[:, DINP - 128:], wa_ref[...]) + ba_ref[...]) * (1.0 / 16.0)

    return pl.pallas_call(
        body, grid=(N // tm,),
        in_specs=[pl.BlockSpec((tm, D), lambda i: (i, 0)), pl.BlockSpec((DINP, D), lambda i: (0, 0)),
                  pl.BlockSpec((128, 256), lambda i: (0, 0)), pl.BlockSpec((1, 256), lambda i: (0, 0))],
        out_specs=[pl.BlockSpec((tm, DINP), lambda i: (i, 0)), pl.BlockSpec((tm, 256), lambda i: (i, 0))],
        out_shape=[_sds((N, DINP)), _sds((N, 256))], name="inproj_fwd", compiler_params=_cp(("parallel",)))(x, wt, wa, ba)


def _inproj_bwd(x, w, dxp, du2, dud, gq_f, gq_b, gk_f, gk_b, gv_f, gv_b, gr, daq, dakv, h, wa, ba, dla_f, dla_b):
    tm = INPROJ_BWD_TM
    nt = N // tm

    def body(x_ref, w_ref, dxp_ref, du2_ref, dud_ref, gqf, gqb, gkf, gkb, gvf, gvb, gr_ref, daq_ref, dakv_ref,
             hl_ref, wa_ref, ba_ref, df_ref, db_ref, dx_ref, dw_ref, dwa_ref, dba_ref, acc):
        i = pl.program_id(0)
        f = lambda r: r[...].astype(F32)
        hl = hl_ref[...]
        pre = _mm(hl, wa_ref[...]) + ba_ref[...]
        dpre = jnp.concatenate([df_ref[...], db_ref[...]], axis=1) * (1.0 / 16.0) * jax.nn.sigmoid(-pre)
        dwa = _mm_tn(hl, dpre)[0:32]
        dba = jnp.sum(dpre, axis=0, keepdims=True)
        dh = jnp.concatenate([
            du2_ref[0] + du2_ref[1] + f(dud_ref), f(gqf) + f(gqb), f(gkf) + f(gkb), f(gvf) + f(gvb),
            f(gr_ref), f(daq_ref), f(dakv_ref), _mm_nt(dpre, wa_ref[...])], axis=1)
        dx_ref[...] = dxp_ref[...] + _mm(dh, w_ref[...])
        contrib = _mm_tn(dh, x_ref[...])

        @pl.when(i == 0)
        def _():
            acc[...] = contrib
            dwa_ref[...] = dwa
            dba_ref[...] = dba

        @pl.when(i > 0)
        def _():
            acc[...] += contrib
            dwa_ref[...] += dwa
            dba_ref[...] += dba

        @pl.when(i == nt - 1)
        def _():
            for (j, d0), (s0, n_) in _IN_ROW_PIECES:
                dw_ref[j, d0:d0 + n_] = acc[s0:s0 + n_].astype(MX)

    row = lambda w_: pl.BlockSpec((tm, w_), lambda i: (i, 0))
    return pl.pallas_call(
        body, grid=(nt,),
        in_specs=[row(D), pl.BlockSpec((DINP, D), lambda i: (0, 0)), row(D),
                  pl.BlockSpec((2, tm, 256), lambda i: (0, i, 0)), row(256), row(128), row(128), row(128), row(128),
                  row(256), row(256), row(256), row(512), row(256),
                  pl.BlockSpec((tm, 128), lambda i: (i, 14)), pl.BlockSpec((128, 256), lambda i: (0, 0)),
                  pl.BlockSpec((1, 256), lambda i: (0, 0)), row(128), row(128)],
        out_specs=[row(D), pl.BlockSpec((NSHARD, DIN // NSHARD, D), lambda i: (0, 0, 0)),
                   pl.BlockSpec((32, 256), lambda i: (0, 0)), pl.BlockSpec((1, 256), lambda i: (0, 0))],
        out_shape=[_sds((N, D)), _sds((NSHARD, DIN // NSHARD, D), MX), _sds((32, 256)), _sds((1, 256))],
        scratch_shapes=[pltpu.VMEM((DINP, D), F32)],
        name="inproj_bwd", compiler_params=_cp(("arbitrary",)))(
            x, w, dxp, du2, dud, gq_f, gq_b, gk_f, gk_b, gv_f, gv_b, gr, daq, dakv, h, wa, ba, dla_f, dla_b)


def _tile_scan(xr, xi, a, cr, ci, reverse):
    for lvl, d in enumerate((1, 2, 4)):
        sh = 8 - d if reverse else d
        sr = pltpu.roll(xr, sh, 0)
        si = pltpu.roll(xi, sh, 0)
        ar, ai = a[2 * lvl], a[2 * lvl + 1]
        xr, xi = xr + ar * sr - ai * si, xi + ar * si + ai * sr
    pr, pi = a[6], a[7]
    return xr + pr * cr - pi * ci, xi + pr * ci + pi * cr


NJ = TT // 8


def _lockstep_tables(mr, mi):
    def body(mr_ref, mi_ref, a_ref, p_ref, ac_ref, pc_ref):
        rowid = lax.broadcasted_iota(jnp.int32, (8, 2 * SW), 0)

        def mul(a, b):
            return a[0] * b[0] - a[1] * b[1], a[0] * b[1] + a[1] * b[0]

        for z in range(2):
            for sign, reverse, a_out, p_out in ((1.0, z == 1, a_ref, p_ref), (-1.0, z == 0, ac_ref, pc_ref)):
                m = (mr_ref[z:z + 1, :], sign * mi_ref[z:z + 1, :])
                pw = [m]
                for _ in range(NJ - 1):
                    pw.append(mul(pw[-1], m))
                n = pw[-1]
                link = [n]
                for _ in range(7):
                    link.append(mul(link[-1], n))
                tiles = [jnp.broadcast_to(m[0], (8, 2 * SW)), jnp.broadcast_to(m[1], (8, 2 * SW))]
                for d in (1, 2, 4):
                    keep = (rowid <= 7 - d) if reverse else (rowid >= d)
                    tiles += [jnp.where(keep, link[d - 1][c], 0.0) for c in range(2)]
                for c in range(2):
                    t = jnp.zeros((8, 2 * SW), F32)
                    for i in range(8):
                        t = jnp.where(rowid == (7 - i if reverse else i), link[i][c], t)
                    tiles.append(t)
                for blk in range(2):
                    lanes = slice(blk * SW, (blk + 1) * SW)
                    for k, t in enumerate(tiles):
                        a_out[z, blk, k] = t[:, lanes]
                    for j in range(NJ):
                        src = pw[NJ - 1 - j] if reverse else pw[j]
                        for c in range(2):
                            p_out[z, blk, c, j:j + 1, :] = src[c][:, lanes]

    vm = pl.BlockSpec(memory_space=pltpu.VMEM)
    a_shape, p_shape = _sds((2, 2, 10, 8, SW)), _sds((2, 2, 2, NJ, SW))
    a, p, ac, pc = pl.pallas_call(body, in_specs=[vm, vm], out_specs=[vm] * 4, out_shape=[a_shape, p_shape] * 2,
                                  name="s5_tables")(mr, mi)
    return (a, p), (ac, pc)


def _to_lockstep(ref, *lead):
    return jnp.concatenate([ref[(*lead, pl.ds(j, 8, stride=NJ), slice(None))] for j in range(NJ)], axis=0)


def _from_lockstep(val, ref, *lead):
    for j in range(NJ):
        ref[(*lead, pl.ds(j, 8, stride=NJ), slice(None))] = val[8 * j:8 * j + 8]


def _expand_powers(p_ref, pexp):
    for c in range(2):
        for j in range(NJ):
            pexp[c, j] = jnp.broadcast_to(p_ref[0, 0, c, j:j + 1, :], (8, SW))


def _lockstep_scan(xre, xim, a_ref, pexp, car, reverse, extra=None):
    a = [a_ref[0, 0, k] for k in range(10)]
    mr, mi = a[0], a[1]
    order = (lambda i: NJ - 1 - i) if reverse else (lambda i: i)

    def local(i, hcar):
        hr, hi = hcar
        r0 = pl.multiple_of(order(i) * 8, 8)
        hr, hi = mr * hr - mi * hi + xre[pl.ds(r0, 8), :], mr * hi + mi * hr + xim[pl.ds(r0, 8), :]
        xre[pl.ds(r0, 8), :] = hr
        xim[pl.ds(r0, 8), :] = hi
        return hr, hi

    z8 = jnp.zeros((8, SW), F32)
    er, ei = lax.fori_loop(0, NJ, local, (z8, z8), unroll=4)
    c0r, c0i = car[0], car[1]
    er, ei = _tile_scan(er, ei, a[2:], c0r, c0i, reverse)
    rowid = lax.broadcasted_iota(jnp.int32, (8, SW), 0)
    first, sh, last = (7, 7, 0) if reverse else (0, 1, 7)
    cvr = jnp.where(rowid == first, c0r, pltpu.roll(er, sh, 0))
    cvi = jnp.where(rowid == first, c0i, pltpu.roll(ei, sh, 0))
    car[0] = jnp.broadcast_to(er[last:last + 1, :], (8, SW))
    car[1] = jnp.broadcast_to(ei[last:last + 1, :], (8, SW))

    def fix(i, carry):
        j = order(i)
        r0 = pl.multiple_of(j * 8, 8)
        pr, pi = pexp[0, j], pexp[1, j]
        sr = xre[pl.ds(r0, 8), :] + pr * cvr - pi * cvi
        si = xim[pl.ds(r0, 8), :] + pr * cvi + pi * cvr
        xre[pl.ds(r0, 8), :] = sr
        xim[pl.ds(r0, 8), :] = si
        if extra is None:
            return carry
        return (sr, si, extra(r0, sr, si, carry[0], carry[1], carry[2]))

    init = (cvr, cvi, extra(None, None, None, None, None, None)) if extra is not None else 0
    return lax.fori_loop(0, NJ, fix, init, unroll=4)


def _s5_time_block(z, s, t, adjoint):
    flip = (1 - z) if adjoint else z
    return s * (L // TT) + t + flip * (L // TT - 1 - 2 * t)


def _s5_fwd(h, bre, bim, cre, cim, tab):
    nt = L // TT
    taba, tabp = tab

    def body(u_ref, bre_ref, bim_ref, cre_ref, cim_ref, a_ref, p_ref, hre_ref, him_ref, y_ref, car, pexp):
        z = pl.program_id(1)
        s = pl.program_id(2)
        tc = pl.program_id(3)

        @pl.when(tc == 0)
        def _():
            car[...] = jnp.zeros_like(car)

        @pl.when((tc == 0) & (s == 0))
        def _():
            _expand_powers(p_ref, pexp)

        u = _to_lockstep(u_ref)
        hre_ref[0] = _mm(u, bre_ref[0, 0])
        him_ref[0] = _mm(u, bim_ref[0, 0])

        @pl.when(z == 0)
        def _():
            _lockstep_scan(hre_ref.at[0], him_ref.at[0], a_ref, pexp, car, False)

        @pl.when(z == 1)
        def _():
            _lockstep_scan(hre_ref.at[0], him_ref.at[0], a_ref, pexp, car, True)

        _from_lockstep(_mm(hre_ref[0], cre_ref[0, 0]) - _mm(him_ref[0], cim_ref[0, 0]), y_ref, 0)

    tb = lambda b, z, s, t: _s5_time_block(z, s, t, False)
    wspec = lambda r, c: pl.BlockSpec((1, 1, r, c), lambda b, z, s, t: (z, b, 0, 0))
    return pl.pallas_call(
        body, grid=(2, 2, NSEQ, nt),
        in_specs=[pl.BlockSpec((TT, 128), lambda b, z, s, t: (tb(b, z, s, t), b)),
                  wspec(128, SW), wspec(128, SW), wspec(SW, 128), wspec(SW, 128),
                  pl.BlockSpec((1, 1, 10, 8, SW), lambda b, z, s, t: (z, b, 0, 0, 0)),
                  pl.BlockSpec((1, 1, 2, NJ, SW), lambda b, z, s, t: (z, b, 0, 0, 0))],
        out_specs=[pl.BlockSpec((1, TT, SW), lambda b, z, s, t: (z, tb(b, z, s, t), b)),
                   pl.BlockSpec((1, TT, SW), lambda b, z, s, t: (z, tb(b, z, s, t), b)),
                   pl.BlockSpec((1, TT, 128), lambda b, z, s, t: (z, tb(b, z, s, t), b))],
        out_shape=[_sds((2, N, 2 * SW)), _sds((2, N, 2 * SW)), _sds((2, N, 256))],
        scratch_shapes=[pltpu.VMEM((2, 8, SW), F32), pltpu.VMEM((2, NJ, 8, SW), F32)],
        name="s5_fwd", compiler_params=_cp(("arbitrary",) * 4))(h, bre, bim, cre, cim, taba, tabp)


def _s5_bwd(h, dyp, hre, him, bre, bim, cre, cim, tabc):
    nt = L // TT
    taba, tabp = tabc

    def body(u_ref, dy_ref, hre_ref, him_ref, bre_ref, bim_ref, cre_ref, cim_ref, a_ref, p_ref,
             du_ref, dbre_ref, dbim_ref, dcre_ref, dcim_ref, dmu_ref, gre, gim, car, acc, macc, pexp):
        z = pl.program_id(1)
        s = pl.program_id(2)
        tc = pl.program_id(3)

        @pl.when(tc == 0)
        def _():
            car[...] = jnp.zeros_like(car)

        @pl.when((tc == 0) & (s == 0))
        def _():
            acc[...] = jnp.zeros_like(acc)
            macc[...] = jnp.zeros_like(macc)
            _expand_powers(p_ref, pexp)

        dy = _to_lockstep(dy_ref)
        gre[...] = _mm_nt(dy, cre_ref[0, 0])
        gim[...] = -_mm_nt(dy, cim_ref[0, 0])

        def run(reverse):
            def pair(r0, gr_, gi_, pvr, pvi, m):
                if r0 is None:
                    return (macc[0], macc[1])
                hr = hre_ref[0, pl.ds(r0, 8), :]
                hi = him_ref[0, pl.ds(r0, 8), :]
                return (m[0] + pvr * hr + pvi * hi, m[1] + pvi * hr - pvr * hi)

            _, _, (dmr, dmi) = _lockstep_scan(gre, gim, a_ref, pexp, car, reverse, pair)
            macc[0] = dmr
            macc[1] = dmi

        @pl.when(z == 0)
        def _():
            run(True)

        @pl.when(z == 1)
        def _():
            run(False)

        gr = gre[...]
        gi = gim[...]
        u = _to_lockstep(u_ref)
        _from_lockstep(_mm_nt(gr, bre_ref[0, 0]) + _mm_nt(gi, bim_ref[0, 0]), du_ref, 0)
        acc[0] += _mm_tn(u, gr)
        acc[1] += _mm_tn(u, gi)
        acc[2] += _mm_tn(dy, hre_ref[0])
        acc[3] -= _mm_tn(dy, him_ref[0])

        @pl.when((tc == nt - 1) & (s == NSEQ - 1))
        def _():
            grp = lax.broadcasted_iota(jnp.int32, (S5_H, SW), 1) // S5_P
            for k, out in enumerate((dbre_ref, dbim_ref, dcre_ref, dcim_ref)):
                c = jnp.zeros((S5_H, SW), F32)
                for i in range(8):
                    c = c + jnp.where(grp == i, acc[k, i * S5_H:(i + 1) * S5_H, :], 0.0)
                out[0, 0] = c
            dmu_ref[0, 0] = jnp.concatenate([jnp.sum(macc[0], axis=0, keepdims=True),
                                             jnp.sum(macc[1], axis=0, keepdims=True)], axis=0)

    tb = lambda b, z, s, t: _s5_time_block(z, s, t, True)
    wspec = lambda r, c: pl.BlockSpec((1, 1, r, c), lambda b, z, s, t: (z, b, 0, 0))
    tok = lambda w_: pl.BlockSpec((TT, w_), lambda b, z, s, t: (tb(b, z, s, t), b))
    st = pl.BlockSpec((1, TT, SW), lambda b, z, s, t: (z, tb(b, z, s, t), b))
    return pl.pallas_call(
        body, grid=(2, 2, NSEQ, nt),
        in_specs=[tok(128), tok(128), st, st, wspec(128, SW), wspec(128, SW), wspec(SW, 128), wspec(SW, 128),
                  pl.BlockSpec((1, 1, 10, 8, SW), lambda b, z, s, t: (z, b, 0, 0, 0)),
                  pl.BlockSpec((1, 1, 2, NJ, SW), lambda b, z, s, t: (z, b, 0, 0, 0))],
        out_specs=[pl.BlockSpec((1, TT, 128), lambda b, z, s, t: (z, tb(b, z, s, t), b)),
                   wspec(S5_H, SW), wspec(S5_H, SW), wspec(S5_H, SW), wspec(S5_H, SW),
                   wspec(2, SW)],
        out_shape=[_sds((2, N, 256))] + [_sds((2, 2, S5_H, SW))] * 4 + [_sds((2, 2, 2, SW))],
        scratch_shapes=[pltpu.VMEM((TT, SW), F32), pltpu.VMEM((TT, SW), F32), pltpu.VMEM((2, 8, SW), F32),
                        pltpu.VMEM((4, 128, SW), F32), pltpu.VMEM((2, 8, SW), F32), pltpu.VMEM((2, NJ, 8, SW), F32)],
        name="s5_bwd", compiler_params=_cp(("arbitrary",) * 4))(h, dyp, hre, him, bre, bim, cre, cim, taba, tabp)


_GELU_C = math.sqrt(2.0 / math.pi)


def _gelu(y):
    return 0.5 * y * (1.0 + jnp.tanh(_GELU_C * (y + 0.044715 * y * y * y)))


def _gelu_grad(y):
    t = jnp.tanh(_GELU_C * (y + 0.044715 * y * y * y))
    return 0.5 * (1.0 + t) + 0.5 * y * (1.0 - t * t) * _GELU_C * (1.0 + 3 * 0.044715 * y * y)


def _glu_halves(w4_ref):
    return (jnp.concatenate([w4_ref[0], w4_ref[1]], axis=1), jnp.concatenate([w4_ref[2], w4_ref[3]], axis=1))


def _s5_glu_fwd(y2, h, dsk, w4, bv, bg):
    tm = 512

    def body(y2_ref, u_ref, d_ref, w4_ref, bv_ref, bg_ref, ya_ref):
        wv, wg = _glu_halves(w4_ref)
        z = _gelu(y2_ref[0] + y2_ref[1] + d_ref[...] * u_ref[...])
        val = _mm(z, wv) + bv_ref[...]
        gate = _mm(z, wg) + bg_ref[...]
        ya_ref[...] = (val * jax.nn.sigmoid(gate)).astype(MX)

    full = lambda r, c: pl.BlockSpec((r, c), lambda i: (0, 0))
    return pl.pallas_call(
        body, grid=(N // tm,),
        in_specs=[pl.BlockSpec((2, tm, 256), lambda i: (0, i, 0)), pl.BlockSpec((tm, 256), lambda i: (i, 0)),
                  full(1, 256), pl.BlockSpec((NSHARD, 256, 128), lambda i: (0, 0, 0)), full(1, 256), full(1, 256)],
        out_specs=pl.BlockSpec((tm, 256), lambda i: (i, 0)),
        out_shape=_sds((N, 256), MX), name="s5_glu_fwd", compiler_params=_cp(("parallel",)))(y2, h, dsk, w4, bv, bg)


def _s5_glu_bwd(y2, h, dsk, w4, bv, bg, dya):
    tm = 512
    nt = N // tm

    def body(y2_ref, u_ref, d_ref, w4_ref, bv_ref, bg_ref, dya_ref,
             dyp_ref, dud_ref, dd_ref, dw4_ref, dbv_ref, dbg_ref, accv, accg):
        i = pl.program_id(0)

        @pl.when(i == 0)
        def _():
            for r in (dd_ref, accv, accg, dbv_ref, dbg_ref):
                r[...] = jnp.zeros_like(r)

        wv, wg = _glu_halves(w4_ref)
        u = u_ref[...]
        y = y2_ref[0] + y2_ref[1] + d_ref[...] * u
        z = _gelu(y)
        val = _mm(z, wv) + bv_ref[...]
        sig = jax.nn.sigmoid(_mm(z, wg) + bg_ref[...])
        dya = dya_ref[...]
        dval = dya * sig
        dgate = dya * val * sig * (1.0 - sig)
        dz = _mm_nt(dval, wv) + _mm_nt(dgate, wg)
        dy = dz * _gelu_grad(y)
        dyp_ref[...] = dy
        dud_ref[...] = (dy * d_ref[...]).astype(MX)
        dd_ref[...] += jnp.sum(dy * u, axis=0, keepdims=True)
        accv[...] += _mm_tn(z, dval)
        accg[...] += _mm_tn(z, dgate)
        dbv_ref[...] += jnp.sum(dval, axis=0, keepdims=True)
        dbg_ref[...] += jnp.sum(dgate, axis=0, keepdims=True)

        @pl.when(i == nt - 1)
        def _():
            dw4_ref[0] = accv[:, 0:128].astype(MX)
            dw4_ref[1] = accv[:, 128:256].astype(MX)
            dw4_ref[2] = accg[:, 0:128].astype(MX)
            dw4_ref[3] = accg[:, 128:256].astype(MX)

    full = lambda r, c: pl.BlockSpec((r, c), lambda i: (0, 0))
    row = pl.BlockSpec((tm, 256), lambda i: (i, 0))
    wspec = pl.BlockSpec((NSHARD, 256, 128), lambda i: (0, 0, 0))
    return pl.pallas_call(
        body, grid=(nt,),
        in_specs=[pl.BlockSpec((2, tm, 256), lambda i: (0, i, 0)), row, full(1, 256), wspec, full(1, 256), full(1, 256),
                  row],
        out_specs=[row, row, full(1, 256), wspec, full(1, 256), full(1, 256)],
        out_shape=[_sds((N, 256)), _sds((N, 256), MX), _sds((1, 256)), _sds((NSHARD, 256, 128), MX), _sds((1, 256)),
                   _sds((1, 256))],
        scratch_shapes=[pltpu.VMEM((256, 256), F32), pltpu.VMEM((256, 256), F32)],
        name="s5_glu_bwd", compiler_params=_cp(("arbitrary",)))(y2, h, dsk, w4, bv, bg, dya)


def _logsig(x):
    return jnp.minimum(x, 0.0) - jnp.log(1.0 + jnp.exp(-jnp.abs(x)))


def _gla_chunk(q, k, v, la, st, rev):
    c = GLA_CHUNK
    rows = q.shape[0]
    nch = rows // c
    b = _cums(la, rev)
    blc = [jnp.sum(la[i * c:(i + 1) * c], axis=0, keepdims=True) for i in range(nch)]
    bl = jnp.concatenate([jnp.broadcast_to(t, (c, 128)) for t in blc], axis=0)
    q_in = q * (32.0 ** -0.5) * jnp.exp(b)
    k_in = k * jnp.exp(-b)
    k_st = k * jnp.exp(bl - b)
    lane_k = lax.broadcasted_iota(jnp.int32, (1, 128), 1) // 32
    lane_v = lax.broadcasted_iota(jnp.int32, (1, 256), 1) // 64
    qs = jnp.concatenate([jnp.where(lane_k == hd, q_in, 0.0) for hd in range(4)], axis=0)
    a = _dmm_nt(qs, k_in)
    a = jnp.where(jnp.concatenate([_chunk_pairs(rows, rev, rev)] * 4, axis=0), a, 0.0)
    o4 = _dmm(a, v)
    o = jnp.zeros((rows, 256), F32)
    for hd in range(4):
        o = o + jnp.where(lane_v == hd, o4[hd * rows:(hd + 1) * rows], 0.0)
    bd = (lax.broadcasted_iota(jnp.int32, (256, 128), 0) // 64) == (lax.broadcasted_iota(jnp.int32, (256, 128), 1) // 32)
    inter = [None] * nch
    for i in (reversed(range(nch)) if rev else range(nch)):
        sl = slice(i * c, (i + 1) * c)
        inter[i] = _dmm_nt(q_in[sl], st)
        st = jnp.exp(blc[i]) * st + jnp.where(bd, _dmm_tn(v[sl], k_st[sl]), 0.0)
    return o + jnp.concatenate(inter, axis=0), st


def _gla_chunk_of(c, rev):
    return NGROUP - 1 - c if rev else c


def _gla_fwd(h, la2):
    c = GLA_GROUP * GLA_CHUNK

    def body(qf, kf, vf, laf, qb, kb, vb, lab, of_ref, ob_ref, sf_ref, sb_ref, stf, stb):
        @pl.when(pl.program_id(0) == 0)
        def _():
            stf[...] = jnp.zeros_like(stf)
            stb[...] = jnp.zeros_like(stb)

        ins = [(qf[s], kf[s], vf[s], laf[s], stf[s], qb[s], kb[s], vb[s], lab[s], stb[s]) for s in range(NSEQ)]
        outs = [(_gla_chunk(*t[:5], False), _gla_chunk(*t[5:], True)) for t in ins]
        for s in range(NSEQ):
            sf_ref[s, 0] = ins[s][4]
            sb_ref[s, 0] = ins[s][9]
            (of_ref[s], stf[s]), (ob_ref[s], stb[s]) = outs[s]

    def specs(rev):
        ch = lambda i: _gla_chunk_of(i, rev)
        return [pl.BlockSpec((NSEQ, c, 128), lambda i: (0, ch(i), 2)), pl.BlockSpec((NSEQ, c, 128), lambda i: (0, ch(i), 3)),
                pl.BlockSpec((NSEQ, c, 256), lambda i: (0, ch(i), 2)),
                pl.BlockSpec((NSEQ, c, 128), lambda i: (0, ch(i), 1 if rev else 0))]

    orow = lambda rev: pl.BlockSpec((NSEQ, c, 256), lambda i: (0, _gla_chunk_of(i, rev), 0))
    srow = lambda rev: pl.BlockSpec((NSEQ, 1, 256, 128), lambda i: (0, _gla_chunk_of(i, rev), 0, 0))
    h3, la3 = h.reshape(NSEQ, L, DINP), la2.reshape(NSEQ, L, 256)
    of, ob, sf, sb = pl.pallas_call(
        body, grid=(NGROUP,),
        in_specs=specs(False) + specs(True),
        out_specs=[orow(False), orow(True), srow(False), srow(True)],
        out_shape=[_sds((NSEQ, L, 256)), _sds((NSEQ, L, 256)), _sds((NSEQ, NGROUP, 256, 128)),
                   _sds((NSEQ, NGROUP, 256, 128))],
        scratch_shapes=[pltpu.VMEM((NSEQ, 256, 128), F32), pltpu.VMEM((NSEQ, 256, 128), F32)],
        name="gla_fwd", compiler_params=_cp(("arbitrary",)))(h3, h3, h3, la3, h3, h3, h3, la3)
    return of.reshape(N, 256), ob.reshape(N, 256), sf, sb


def _gla_bwd(h, la2, do, sf, sb):
    c = GLA_GROUP * GLA_CHUNK

    def body(qf, kf, vf, laf, dof, sfr, qb, kb, vb, lab, dob, sbr,
             dqf, dkf, dvf, dlf, dqb, dkb, dvb, dlb, dstf, dstb):
        @pl.when(pl.program_id(0) == 0)
        def _():
            dstf[...] = jnp.zeros_like(dstf)
            dstb[...] = jnp.zeros_like(dstb)

        def one(s, q, k, v, la, do_, st, dst, rev):
            _, vjp = jax.vjp(functools.partial(_gla_chunk, rev=rev), q[s], k[s], v[s], la[s], st[s, 0])
            return vjp((do_[s], dst[s]))

        res = [(one(s, qf, kf, vf, laf, dof, sfr, dstf, False), one(s, qb, kb, vb, lab, dob, sbr, dstb, True))
               for s in range(NSEQ)]
        for s in range(NSEQ):
            for (gq, gk, gv, gl, gs), (dq, dk, dv, dl, dst) in ((res[s][0], (dqf, dkf, dvf, dlf, dstf)),
                                                                  (res[s][1], (dqb, dkb, dvb, dlb, dstb))):
                dq[s], dk[s], dv[s] = gq.astype(MX), gk.astype(MX), gv.astype(MX)
                dl[s], dst[s] = gl, gs

    def specs(rev):
        ch = lambda i: _gla_chunk_of(i, not rev)
        return [pl.BlockSpec((NSEQ, c, 128), lambda i: (0, ch(i), 2)), pl.BlockSpec((NSEQ, c, 128), lambda i: (0, ch(i), 3)),
                pl.BlockSpec((NSEQ, c, 256), lambda i: (0, ch(i), 2)),
                pl.BlockSpec((NSEQ, c, 128), lambda i: (0, ch(i), 1 if rev else 0)),
                pl.BlockSpec((NSEQ, c, 256), lambda i: (0, ch(i), 0)),
                pl.BlockSpec((NSEQ, 1, 256, 128), lambda i: (0, ch(i), 0, 0))]

    def ospecs(rev):
        ch = lambda i: _gla_chunk_of(i, not rev)
        n = pl.BlockSpec((NSEQ, c, 128), lambda i: (0, ch(i), 0))
        return [n, n, pl.BlockSpec((NSEQ, c, 256), lambda i: (0, ch(i), 0)), n]

    oshape = [_sds((NSEQ, L, 128), MX), _sds((NSEQ, L, 128), MX), _sds((NSEQ, L, 256), MX), _sds((NSEQ, L, 128))]
    h3, la3, do3 = h.reshape(NSEQ, L, DINP), la2.reshape(NSEQ, L, 256), do.reshape(NSEQ, L, 256)
    res = pl.pallas_call(
        body, grid=(NGROUP,),
        in_specs=specs(False) + specs(True),
        out_specs=ospecs(False) + ospecs(True),
        out_shape=oshape + oshape,
        scratch_shapes=[pltpu.VMEM((NSEQ, 256, 128), F32), pltpu.VMEM((NSEQ, 256, 128), F32)],
        name="gla_bwd", compiler_params=_cp(("arbitrary",)))(h3, h3, h3, la3, do3, sf, h3, h3, h3, la3, do3, sb)
    return [r.reshape(N, r.shape[-1]) for r in res]


def _gla_post(of, ob, r, g):
    o = of + ob
    head = lax.broadcasted_iota(jnp.int32, (1, 256), 1) // 64
    mu = jnp.zeros_like(o)
    for hd in range(4):
        mu = mu + jnp.where(head == hd, jnp.sum(jnp.where(head == hd, o, 0.0), axis=-1, keepdims=True) * (1.0 / 64.0), 0.0)
    xc = o - mu
    var = jnp.zeros_like(o)
    for hd in range(4):
        var = var + jnp.where(head == hd, jnp.sum(jnp.where(head == hd, xc * xc, 0.0), axis=-1, keepdims=True) * (1.0 / 64.0), 0.0)
    return xc * lax.rsqrt(var + LN_EPS) * g * (r * jax.nn.sigmoid(r))


def _gla_post_fwd(of, ob, h, g):
    tm = 512

    def body(of_ref, ob_ref, r_ref, g_ref, y_ref):
        y_ref[...] = _gla_post(of_ref[...], ob_ref[...], r_ref[...], g_ref[...]).astype(MX)

    row = pl.BlockSpec((tm, 256), lambda i: (i, 0))
    return pl.pallas_call(
        body, grid=(N // tm,),
        in_specs=[row, row, pl.BlockSpec((tm, 256), lambda i: (i, 3)), pl.BlockSpec((1, 256), lambda i: (0, 0))],
        out_specs=row, out_shape=_sds((N, 256), MX), name="gla_post_fwd", compiler_params=_cp(("parallel",)))(of, ob, h, g)


def _rope_tables(width):
    pos = jnp.arange(L, dtype=F32)
    inv_freq = ROPE_THETA ** (-jnp.arange(0, ROT, 2, dtype=F32) / ROT)
    ang = pos[:, None] * inv_freq[None, :]
    cos, sin = jnp.cos(ang), jnp.sin(ang)
    one = jnp.ones((L, 64 - ROT), F32)
    zero = jnp.zeros((L, 64 - ROT), F32)
    z8 = jnp.zeros((L, ROT // 2), F32)
    c = jnp.concatenate([cos, cos, one], axis=1)
    sa = jnp.concatenate([z8, sin, zero], axis=1)
    sb = jnp.concatenate([-sin, z8, zero], axis=1)
    rep = width // 64
    return jnp.stack([jnp.tile(c, (1, rep)), jnp.tile(sa, (1, rep)), jnp.tile(sb, (1, rep))])


def _pieces(t, f):
    out = [f(t[:, c * 128:(c + 1) * 128]) for c in range(t.shape[-1] // 128)]
    return out[0] if len(out) == 1 else jnp.concatenate(out, axis=1)


def _rope(t, tab):
    return _pieces(t, lambda x: x * tab[0] + pltpu.roll(x, ROT // 2, 1) * tab[1] + pltpu.roll(x, 128 - ROT // 2, 1) * tab[2])


def _rope_t(g, tab):
    return _pieces(g, lambda x: x * tab[0] + pltpu.roll(x * tab[1], 128 - ROT // 2, 1) + pltpu.roll(x * tab[2], ROT // 2, 1))


def _swa_pad_kv(kv_ref, tk_ref, kexp, vexp):
    z = jnp.zeros((SWA_BLK, 256), F32)
    kr = _rope(kv_ref[:, 0:128], tk_ref[...])
    for hk in range(2):
        for pad in (kexp, vexp):
            pad[hk, 0:SWA_BLK] = z
            pad[hk, SWA_BLK + L:] = z
        kexp[hk, SWA_BLK:SWA_BLK + L] = _swa_expand(kr, hk)
        vexp[hk, SWA_BLK:SWA_BLK + L] = _swa_expand(kv_ref[:, 128:256], hk)


def _swa_expand(x, hk):
    lane = lax.broadcasted_iota(jnp.int32, x.shape, 1)
    sw = pltpu.roll(x, 64, 1)
    pair = jnp.where(lane < 64, x, sw) if hk == 0 else jnp.where(lane < 64, sw, x)
    return jnp.concatenate([pair, pair], axis=1)


def _swa_fold(x, hk):
    a = x[:, 0:128] + x[:, 128:256]
    t = a + pltpu.roll(a, 64, 1)
    lane = lax.broadcasted_iota(jnp.int32, a.shape, 1)
    return jnp.where((lane < 64) if hk == 0 else (lane >= 64), t, 0.0)


def _swa_bias_tables(bias):
    i = lax.broadcasted_iota(jnp.int32, (SWA_BLK, 3 * SWA_BLK), 0)
    j = lax.broadcasted_iota(jnp.int32, (SWA_BLK, 3 * SWA_BLK), 1)
    band = (j - i >= 0) & (j - i <= 2 * SWA_BLK)
    for v, inside in enumerate((j >= SWA_BLK, True, j < 2 * SWA_BLK)):
        bias[v] = jnp.where(band & inside, 0.0, NEG_BIG)


def _swa_bias(bias, blk):
    return bias[jnp.where(blk == 0, 0, jnp.where(blk == NBLK - 1, 2, 1))]


def _swa_probs(q2, kexp, bias, sink_ref, hk):
    slot = lax.broadcasted_iota(jnp.int32, (1, 256), 1) // 64
    qs = jnp.concatenate([jnp.where(slot == g, q2, 0.0) for g in range(4)], axis=0)
    s = _mm_nt(qs, kexp) + jnp.concatenate([bias] * 4, axis=0)
    rowg = lax.broadcasted_iota(jnp.int32, (4 * SWA_BLK, 1), 0) // SWA_BLK
    sink = jnp.zeros((4 * SWA_BLK, 1), F32)
    for g in range(4):
        sink = jnp.where(rowg == g, sink_ref[hk * 4 + g], sink)
    m = jnp.maximum(jnp.max(s, axis=-1, keepdims=True), sink)
    p = jnp.exp(s - m)
    ps = jnp.exp(sink - m)
    inv = 1.0 / (jnp.sum(p, axis=-1, keepdims=True) + ps)
    return qs, p * inv, ps * inv, slot, rowg


def _swa_qtab(tk_ref, r0):
    return [tk_ref[i, pl.ds(r0, SWA_BLK), :] for i in range(3)]


def _swa_fwd(h, tk, sink):
    def body(sink_ref, q_ref, kv_ref, tk_ref, y_ref, kexp, vexp, bias):
        n = pl.program_id(1)

        @pl.when(n == 0)
        def _():
            _swa_pad_kv(kv_ref, tk_ref, kexp, vexp)
            _swa_bias_tables(bias)

        for t in range(SWA_PER):
            blk = n * SWA_PER + t
            rows = slice(t * SWA_BLK, (t + 1) * SWA_BLK)
            r0 = pl.multiple_of(blk * SWA_BLK, SWA_BLK)
            q = _rope(q_ref[rows, :], _swa_qtab(tk_ref, r0)) * 0.125
            for hk in range(2):
                _, p, _, slot, _ = _swa_probs(q[:, hk * 256:(hk + 1) * 256], kexp[hk, pl.ds(r0, 3 * SWA_BLK), :],
                                              _swa_bias(bias, blk), sink_ref, hk)
                o4 = _mm(p, vexp[hk, pl.ds(r0, 3 * SWA_BLK), :])
                o = jnp.zeros((SWA_BLK, 256), F32)
                for g in range(4):
                    o = o + jnp.where(slot == g, o4[g * SWA_BLK:(g + 1) * SWA_BLK], 0.0)
                y_ref[rows, hk * 256:(hk + 1) * 256] = o.astype(MX)

    tm = SWA_PER * SWA_BLK
    return pl.pallas_call(
        body,
        grid_spec=pltpu.PrefetchScalarGridSpec(
            num_scalar_prefetch=1, grid=(NSEQ, L // tm),
            in_specs=[pl.BlockSpec((tm, 512), lambda s, n, sk: (s * (L // tm) + n, 2)),
                      pl.BlockSpec((L, 256), lambda s, n, sk: (s, 6)),
                      pl.BlockSpec((3, L, 128), lambda s, n, sk: (0, 0, 0))],
            out_specs=pl.BlockSpec((tm, 512), lambda s, n, sk: (s * (L // tm) + n, 0)),
            scratch_shapes=[pltpu.VMEM((2, L + 2 * SWA_BLK, 256), F32), pltpu.VMEM((2, L + 2 * SWA_BLK, 256), F32),
                            pltpu.VMEM((3, SWA_BLK, 3 * SWA_BLK), F32)]),
        out_shape=_sds((N, 512), MX), name="swa_fwd", compiler_params=_cp(("arbitrary", "arbitrary")))(sink, h, h, tk)


def _swa_bwd(h, tk, sink, dyc):
    tm = SWA_PER * SWA_BLK

    def body(sink_ref, q_ref, kv_ref, tk_ref, dy_ref, dq_ref, dkv_ref, dsink_ref, kexp_all, vexp_all, dkacc, dvacc, bias):
        sq = pl.program_id(0)
        n = pl.program_id(1)

        @pl.when(n == 0)
        def _():
            _swa_pad_kv(kv_ref, tk_ref, kexp_all, vexp_all)
            _swa_bias_tables(bias)
            dkacc[...] = jnp.zeros_like(dkacc)
            dvacc[...] = jnp.zeros_like(dvacc)

        @pl.when((n == 0) & (sq == 0))
        def _():
            dsink_ref[...] = jnp.zeros_like(dsink_ref)

        hrow = lax.broadcasted_iota(jnp.int32, (8, 128), 0)
        dsk = jnp.zeros((8, 128), F32)
        for t in range(SWA_PER):
            blk = n * SWA_PER + t
            rows = slice(t * SWA_BLK, (t + 1) * SWA_BLK)
            r0 = pl.multiple_of(blk * SWA_BLK, SWA_BLK)
            tq = _swa_qtab(tk_ref, r0)
            q = _rope(q_ref[rows, :], tq) * 0.125
            band = _swa_bias(bias, blk)
            for hk in range(2):
                kexp = kexp_all[hk, pl.ds(r0, 3 * SWA_BLK), :]
                vexp = vexp_all[hk, pl.ds(r0, 3 * SWA_BLK), :]
                qs, p, ps, slot, rowg = _swa_probs(q[:, hk * 256:(hk + 1) * 256], kexp, band, sink_ref, hk)
                dy2 = dy_ref[rows, hk * 256:(hk + 1) * 256]
                dos = jnp.concatenate([jnp.where(slot == g, dy2, 0.0) for g in range(4)], axis=0)
                dp = _mm_nt(dos, vexp)
                delta = jnp.sum(p * dp, axis=-1, keepdims=True)
                ds = p * (dp - delta)
                dsr = -ps * delta
                for g in range(4):
                    dsk = dsk + jnp.where(hrow == hk * 4 + g,
                                          jnp.sum(jnp.where(rowg == g, dsr, 0.0), axis=0, keepdims=True), 0.0)
                dq4 = _mm(ds, kexp)
                dq2 = jnp.zeros((SWA_BLK, 256), F32)
                for g in range(4):
                    dq2 = dq2 + jnp.where(slot == g, dq4[g * SWA_BLK:(g + 1) * SWA_BLK], 0.0)
                dq_ref[rows, hk * 256:(hk + 1) * 256] = _rope_t(dq2 * 0.125, tq).astype(MX)
                dkacc[hk, pl.ds(r0, 3 * SWA_BLK), :] += _mm_tn(ds, qs)
                dvacc[hk, pl.ds(r0, 3 * SWA_BLK), :] += _mm_tn(p, dos)
        dsink_ref[...] += dsk

        @pl.when(n == L // tm - 1)
        def _():
            seq = slice(SWA_BLK, SWA_BLK + L)
            dk = _rope_t(_swa_fold(dkacc[0, seq], 0) + _swa_fold(dkacc[1, seq], 1), tk_ref[...])
            dkv_ref[:, 0:128] = dk.astype(MX)
            dkv_ref[:, 128:256] = (_swa_fold(dvacc[0, seq], 0) + _swa_fold(dvacc[1, seq], 1)).astype(MX)

    blk = lambda col: pl.BlockSpec((tm, 512), lambda s, n, sk: (s * (L // tm) + n, col))
    pad = pltpu.VMEM((2, L + 2 * SWA_BLK, 256), F32)
    return pl.pallas_call(
        body,
        grid_spec=pltpu.PrefetchScalarGridSpec(
            num_scalar_prefetch=1, grid=(NSEQ, L // tm),
            in_specs=[blk(2), pl.BlockSpec((L, 256), lambda s, n, sk: (s, 6)),
                      pl.BlockSpec((3, L, 128), lambda s, n, sk: (0, 0, 0)), blk(0)],
            out_specs=[blk(0), pl.BlockSpec((L, 256), lambda s, n, sk: (s, 0)),
                       pl.BlockSpec((8, 128), lambda s, n, sk: (0, 0))],
            scratch_shapes=[pad, pad, pad, pad, pltpu.VMEM((3, SWA_BLK, 3 * SWA_BLK), F32)]),
        out_shape=[_sds((N, 512), MX), _sds((N, 256), MX), _sds((8, 128))],
        name="swa_bwd", compiler_params=_cp(("arbitrary", "arbitrary")))(sink, h, h, tk, dyc)


def _outproj_bwd(dx1, s1, ya, yb, yc, wo, g, of, ob, h, lng):
    tm = 512
    nt = N // tm

    def body(dx1_ref, s_ref, ya_ref, yb_ref, yc_ref, wo_ref, g_ref, of_ref, ob_ref, r_ref, lng_ref,
             dya_ref, do_ref, dr_ref, dyc_ref, dxp_ref, dwo_ref, dg_ref, db_ref, dlng_ref, acc):
        i = pl.program_id(0)

        @pl.when(i == 0)
        def _():
            acc[...] = jnp.zeros_like(acc)
            dg_ref[...] = jnp.zeros_like(dg_ref)
            db_ref[...] = jnp.zeros_like(db_ref)
            dlng_ref[...] = jnp.zeros_like(dlng_ref)

        ds, dg, db = _ln_bwd(dx1_ref[...], s_ref[...], g_ref[...])
        dg_ref[...] += dg
        db_ref[...] += db
        dxp_ref[...] = ALPHA * ds
        dy = _mm_nt(ds, wo_ref[...])
        dya_ref[...] = dy[:, 0:256]
        _, vjp = jax.vjp(_gla_post, of_ref[...], ob_ref[...], r_ref[...], lng_ref[...])
        go, _, gr, gg = vjp(dy[:, 256:512])
        do_ref[...] = go
        dr_ref[...] = gr.astype(MX)
        dlng_ref[...] += gg
        dyc_ref[...] = dy[:, 512:1024].astype(MX)
        acc[0:256] += _mm_tn(ya_ref[...], ds)
        acc[256:512] += _mm_tn(yb_ref[...], ds)
        acc[512:1024] += _mm_tn(yc_ref[...], ds)

        @pl.when(i == nt - 1)
        def _():
            dwo_ref[...] = acc[...].astype(MX)

    row = lambda w_: pl.BlockSpec((tm, w_), lambda i: (i, 0))
    one = pl.BlockSpec((1, D), lambda i: (0, 0))
    full = pl.BlockSpec((D, D), lambda i: (0, 0))
    return pl.pallas_call(
        body, grid=(nt,),
        in_specs=[row(D), row(D), row(256), row(256), row(512), full, one,
                  row(256), row(256), pl.BlockSpec((tm, 256), lambda i: (i, 3)), pl.BlockSpec((1, 256), lambda i: (0, 0))],
        out_specs=[row(256), row(256), row(256), row(512), row(D), full, one, one, pl.BlockSpec((1, 256), lambda i: (0, 0))],
        out_shape=[_sds((N, 256)), _sds((N, 256)), _sds((N, 256), MX), _sds((N, 512), MX), _sds((N, D)), _sds((D, D), MX),
                   _sds((1, D)), _sds((1, D)), _sds((1, 256))],
        scratch_shapes=[pltpu.VMEM((D, D), F32)],
        name="outproj_bwd", compiler_params=_cp(("arbitrary",)))(dx1, s1, ya, yb, yc, wo, g, of, ob, h, lng)


def _mix_ffn_fwd(ya, yb, yc, x, wo, g1, b1, w1, w2, g, b, target=None):
    tm = FFN_TM
    head = target is not None

    def body(*refs):
        ya_ref, yb_ref, yc_ref, xin_ref, wo_ref, g1_ref, b1_ref, w1_ref, w2_ref, g_ref, b_ref = refs[:11]
        s1_ref, x1_ref, a_ref, s_ref, y_ref = refs[11 + head:16 + head]
        mix = _mm(ya_ref[...], wo_ref[0:256]) + _mm(yb_ref[...], wo_ref[256:512]) + _mm(yc_ref[...], wo_ref[512:1024])
        s1 = ALPHA * xin_ref[...] + mix
        s1_ref[...] = s1
        x = _ln_fwd(s1, g1_ref[...], b1_ref[...])
        x1_ref[...] = x
        xb = x.astype(MX)
        s = ALPHA * x
        for j in range(NSHARD):
            a = _mm(xb, w1_ref[j])
            a_ref[:, j * D:(j + 1) * D] = a.astype(MX)
            s = s + _mm(jnp.square(jnp.maximum(a, 0.0)), w2_ref[j])
        s_ref[...] = s
        x2 = _ln_fwd(s, g_ref[...], b_ref[...])
        if not head:
            y_ref[...] = x2
            return
        l_ref = refs[16 + head]

        @pl.when(pl.program_id(0) == 0)
        def _():
            l_ref[...] = jnp.zeros_like(l_ref)

        e = x2 - refs[11][...]
        y_ref[...] = e * (1.0 / D)
        l_ref[...] += jnp.sum(jnp.sum(e * e, axis=1, keepdims=True), axis=0, keepdims=True) * (0.5 / D)

    rw = lambda w_: pl.BlockSpec((tm, w_), lambda i: (i, 0))
    row = rw(D)
    once = dict(pipeline_mode=pl.Buffered(1))
    wall = pl.BlockSpec((NSHARD, D, D), lambda i: (0, 0, 0), **once)
    one = pl.BlockSpec((1, D), lambda i: (0, 0))
    acc = pl.BlockSpec((8, 128), lambda i: (0, 0))
    return pl.pallas_call(
        body, grid=(N // tm,),
        in_specs=[rw(256), rw(256), rw(512), row, pl.BlockSpec((D, D), lambda i: (0, 0), **once), one, one,
                  wall, wall, one, one] + [row] * head,
        out_specs=[row, row, pl.BlockSpec((tm, DFF), lambda i: (i, 0)), row, row] + [acc] * head,
        out_shape=[_sds((N, D)), _sds((N, D)), _sds((N, DFF), MX), _sds((N, D)), _sds((N, D))] + [_sds((8, 128))] * head,
        name="mix_ffn_fwd", compiler_params=_cp(("arbitrary",), FFN_VMEM))(
            ya, yb, yc, x, wo, g1, b1, w1, w2, g, b, *([target] * head))


def _ffn_bwd_act(dy, s2, a, w1, w2, g):
    tm = FFN_TM

    def body(dy_ref, s_ref, a_ref, w1_ref, w2_ref, g_ref, da_ref, ds_ref, dx1_ref, dg_ref, db_ref):
        @pl.when(pl.program_id(0) == 0)
        def _():
            dg_ref[...] = jnp.zeros_like(dg_ref)
            db_ref[...] = jnp.zeros_like(db_ref)

        ds, dg, db = _ln_bwd(dy_ref[...], s_ref[...], g_ref[...])
        dsb = ds.astype(MX)
        ds_ref[...] = dsb
        dg_ref[...] += dg
        db_ref[...] += db
        dx1 = ALPHA * ds
        for j in range(NSHARD):
            da = (_mm_nt(dsb, w2_ref[j]) * 2.0 * jnp.maximum(a_ref[:, j * D:(j + 1) * D].astype(F32), 0.0)).astype(MX)
            da_ref[:, j * D:(j + 1) * D] = da
            dx1 = dx1 + _mm_nt(da, w1_ref[j])
        dx1_ref[...] = dx1

    row = pl.BlockSpec((tm, D), lambda i: (i, 0))
    wide = pl.BlockSpec((tm, DFF), lambda i: (i, 0))
    wall = pl.BlockSpec((NSHARD, D, D), lambda i: (0, 0, 0))
    one = pl.BlockSpec((1, D), lambda i: (0, 0))
    return pl.pallas_call(
        body, grid=(N // tm,),
        in_specs=[row, row, wide, wall, wall, one],
        out_specs=[wide, row, row, one, one],
        out_shape=[_sds((N, DFF), MX), _sds((N, D), MX), _sds((N, D)), _sds((1, D)), _sds((1, D))],
        name="ffn_bwd_act", compiler_params=_cp(("arbitrary",), FFN_VMEM))(dy, s2, a, w1, w2, g)


def _ffn_bwd_w(x1, da, a, ds):
    tm, nb = FFN_TM_W, FFN_WB
    nt = N // tm

    def body(x_ref, da_ref, a_ref, ds_ref, dw1_ref, dw2_ref, acc1, acc2):
        i = pl.program_id(1)

        @pl.when(i == 0)
        def _():
            acc1[...] = jnp.zeros_like(acc1)
            acc2[...] = jnp.zeros_like(acc2)

        x, ds_ = x_ref[...], ds_ref[...]
        for k in range(nb):
            cols = slice(k * D, (k + 1) * D)
            acc1[k] += _mm_tn(x, da_ref[:, cols])
            acc2[k] += _mm_tn(jnp.square(jnp.maximum(a_ref[:, cols].astype(F32), 0.0)), ds_)

        @pl.when(i == nt - 1)
        def _():
            dw1_ref[...] = acc1[...].astype(MX)
            dw2_ref[...] = acc2[...].astype(MX)

    row = pl.BlockSpec((tm, D), lambda j, i: (i, 0))
    col = pl.BlockSpec((tm, nb * D), lambda j, i: (i, j))
    wj = pl.BlockSpec((nb, D, D), lambda j, i: (j, 0, 0))
    return pl.pallas_call(
        body, grid=(NSHARD // nb, nt),
        in_specs=[row, col, col, row], out_specs=[wj, wj],
        out_shape=[_sds((NSHARD, D, D), MX), _sds((NSHARD, D, D), MX)],
        scratch_shapes=[pltpu.VMEM((nb, D, D), F32), pltpu.VMEM((nb, D, D), F32)],
        name="ffn_bwd_w", compiler_params=_cp(("parallel", "arbitrary"), FFN_VMEM))(x1, da, a, ds)


def _s5_discretize(a_re, a_im, log_step, b_re, b_im):
    lam = lax.complex(a_re, a_im)
    lam_bar = jnp.exp(lam * jnp.exp(log_step))
    b_bar = ((lam_bar - 1.0) / lam)[..., None] * lax.complex(b_re, b_im)
    return jnp.real(lam_bar), jnp.imag(lam_bar), jnp.real(b_bar), jnp.imag(b_bar)


def _s5_in_blocks(b):
    e = jnp.eye(8, dtype=F32)
    return jnp.einsum('ij,zbjph->zbihjp', e, b.reshape(2, 2, 8, S5_P, S5_H)).reshape(2, 2, 128, SW)


def _s5_out_blocks(c):
    e = jnp.eye(8, dtype=F32)
    return jnp.einsum('ij,zbjhp->zbjpih', e, c.reshape(2, 2, 8, S5_H, S5_P)).reshape(2, 2, SW, 128)


def _gate_weight(w_a):
    z = jnp.zeros((16, 128), F32)
    top = jnp.concatenate([w_a[0], z], axis=1)
    bot = jnp.concatenate([z, w_a[1]], axis=1)
    return jnp.concatenate([top, bot, jnp.zeros((96, 256), F32)], axis=0)


def _layer_prep(p):
    lr, li, br, bi = _s5_discretize(p["s5_a_re"], p["s5_a_im"], p["s5_log_step"], p["s5_b_re"], p["s5_b_im"])
    q = dict(p)
    q["bre"] = _s5_in_blocks(br).astype(MX)
    q["bim"] = _s5_in_blocks(bi).astype(MX)
    q["cre"] = _s5_out_blocks(p["s5_c_re"]).astype(MX)
    q["cim"] = _s5_out_blocks(p["s5_c_im"]).astype(MX)
    mr, mi = lr.reshape(2, 1024), li.reshape(2, 1024)
    q["tab"], q["tabc"] = _lockstep_tables(mr, mi)
    q["dsk"] = p["s5_d"].reshape(1, 256)
    q["wa"] = _gate_weight(p["gla_w_a"]).astype(MX)
    q["ba"] = p["gla_b_a"].reshape(1, 256)
    q["lng"] = p["gla_ln_g"].reshape(1, 256)
    q["bv"] = p["s5_b_glu"][:256].reshape(1, 256)
    q["bg"] = p["s5_b_glu"][256:].reshape(1, 256)
    for k in ("ln1_g", "ln1_b", "ln2_g", "ln2_b"):
        q[k] = p[k].reshape(1, D)
    return q


def _layer_fwd(x, q, tk, fetch, target=None):
    q["w_in"] = fetch("w_in", x)
    h, la2 = _inproj_fwd(x, q["w_in"], q["wa"], q["ba"])
    hre, him, y2 = _s5_fwd(h, q["bre"], q["bim"], q["cre"], q["cim"], q["tab"])
    q["w4"] = fetch("s5_w_glu", y2)
    ya = _s5_glu_fwd(y2, h, q["dsk"], q["w4"], q["bv"], q["bg"])
    of, ob, sf, sb = _gla_fwd(h, la2)
    yb = _gla_post_fwd(of, ob, h, q["lng"])
    yc = _swa_fwd(h, tk, q["swa_sink"])
    mixed = ya[:8, :128] + yb[:8, :128] + yc[:8, :128]
    q["w_out"] = fetch("w_out", mixed)
    q["w_ff1"] = fetch("w_ff1", mixed)
    q["w_ff2"] = fetch("w_ff2", mixed)
    s1, x1, a, s2, *out = _mix_ffn_fwd(ya, yb, yc, x, q["w_out"], q["ln1_g"], q["ln1_b"], q["w_ff1"], q["w_ff2"],
                                       q["ln2_g"], q["ln2_b"], target)
    saved = dict(x=x, h=h, hre=hre, him=him, y2=y2, ya=ya, la2=la2, of=of, ob=ob, sf=sf, sb=sb, yb=yb, yc=yc,
                 s1=s1, x1=x1, a=a, s2=s2)
    return (out[0] if target is None else tuple(out)), saved


def _layer_bwd(dy, q, sv, tk, emit):
    g = {}
    da, ds2, dx1, g["dg2"], g["db2"] = _ffn_bwd_act(dy, sv["s2"], sv["a"], q["w_ff1"], q["w_ff2"], q["ln2_g"])
    dw1, dw2 = _ffn_bwd_w(sv["x1"], da, sv["a"], ds2)
    tie = emit(dict(w_ff1=dw1, w_ff2=dw2))
    h = sv["h"]
    dya, do, gr, dyc, dxp, dwo, g["dg1"], g["db1"], g["dlng"] = _outproj_bwd(
        dx1, sv["s1"], sv["ya"], sv["yb"], sv["yc"], q["w_out"], q["ln1_g"] + tie, sv["of"], sv["ob"], h, q["lng"])
    daq, dakv, g["dsink"] = _swa_bwd(h, tk, q["swa_sink"], dyc)
    gq_f, gk_f, gv_f, gl_f, gq_b, gk_b, gv_b, gl_b = _gla_bwd(h, sv["la2"], do, sv["sf"], sv["sb"])
    dyp, dud, g["dd"], dw4, g["dbv"], g["dbg"] = _s5_glu_bwd(sv["y2"], h, q["dsk"], q["w4"], q["bv"], q["bg"], dya)
    tie = emit(dict(w_out=dwo.reshape(NSHARD, D // NSHARD, D), s5_w_glu=dw4))
    du2, g["dbre"], g["dbim"], g["dcre"], g["dcim"], g["dmu"] = _s5_bwd(
        h, dyp, sv["hre"], sv["him"], q["bre"], q["bim"], q["cre"], q["cim"], (q["tabc"][0], q["tabc"][1] + tie))
    dx, dwt, g["dwa"], g["dba"] = _inproj_bwd(sv["x"], q["w_in"], dxp, du2, dud, gq_f, gq_b, gk_f, gk_b, gv_f, gv_b, gr,
                                              daq, dakv, h, q["wa"], q["ba"], gl_f, gl_b)
    tie = emit(dict(w_in=dwt))
    return dx, g, tie


NATIVE = ("dmu", "dbre", "dbim", "dcre", "dcim", "dd", "dbv", "dbg", "dwa", "dba", "dlng", "dsink",
          "dg1", "db1", "dg2", "db2", "loss")
ICI_CORE = (0, 0, 0, 1, 1, 0, 0, 0, 1, 1, 1, 1, 0, 0, 1, 1, 0)
Y_FIRST = (0, 0, 1, 0, 1, 1, 0, 1, 0, 1, 0, 1, 0, 1, 0, 1, 0)


def _finish_small(n, w):
    g = {}
    dmu = n["dmu"]
    dlr = dmu[:, :, :, 0].reshape(DEPTH, 2, S5_G, S5_P)
    dli = dmu[:, :, :, 1].reshape(DEPTH, 2, S5_G, S5_P)

    def unblock(c, perm, shape):
        return c.reshape(DEPTH, 2, 2, S5_H, 8, S5_P).transpose(perm).reshape(shape)

    b_shape, c_shape = (DEPTH, 2, S5_G, S5_P, S5_H), (DEPTH, 2, S5_G, S5_H, S5_P)
    _, vjp = jax.vjp(_s5_discretize, w["s5_a_re"], w["s5_a_im"], w["s5_log_step"], w["s5_b_re"], w["s5_b_im"])
    (g["s5_a_re"], g["s5_a_im"], g["s5_log_step"], g["s5_b_re"], g["s5_b_im"]) = vjp(
        (dlr, dli, unblock(n["dbre"], (0, 1, 2, 4, 5, 3), b_shape), unblock(n["dbim"], (0, 1, 2, 4, 5, 3), b_shape)))
    g["s5_c_re"] = unblock(n["dcre"], (0, 1, 2, 4, 3, 5), c_shape)
    g["s5_c_im"] = unblock(n["dcim"], (0, 1, 2, 4, 3, 5), c_shape)
    g["s5_d"] = n["dd"].reshape(DEPTH, S5_G, S5_H)
    g["s5_b_glu"] = jnp.concatenate([n["dbv"], n["dbg"]], axis=2).reshape(DEPTH, 512)
    g["gla_w_a"] = jnp.stack([n["dwa"][:, 0:16, 0:128], n["dwa"][:, 16:32, 128:256]], axis=1)
    g["gla_b_a"] = n["dba"].reshape(DEPTH, 2, 128)
    g["gla_ln_g"] = n["dlng"].reshape(DEPTH, 256)
    g["swa_sink"] = n["dsink"][:, :, 0]
    for k, s in (("ln1_g", "dg1"), ("ln1_b", "db1"), ("ln2_g", "dg2"), ("ln2_b", "db2")):
        g[k] = n[s].reshape(DEPTH, D)
    return g


def _local_step(x, target, qs, tk, fetch, emit):
    saved = []
    for l, q in enumerate(qs):
        x, sv = _layer_fwd(x, q, tk, functools.partial(fetch, l), target if l == DEPTH - 1 else None)
        saved.append(sv)
    dy, lacc = x
    smalls = [None] * DEPTH
    tie = 0.0
    for l in reversed(range(DEPTH)):
        qs[l]["ln2_g"] = qs[l]["ln2_g"] + tie
        dy, smalls[l], tie = _layer_bwd(dy, qs[l], saved[l], tk, functools.partial(emit, l))
    smalls[0]["db2"] = smalls[0]["db2"] + tie
    for l in range(DEPTH):
        smalls[l]["loss"] = lacc if l == 0 else jnp.zeros_like(lacc)
    return lacc[0, 0], dy, smalls


BIG = ("w_in", "s5_w_glu", "w_out", "w_ff1", "w_ff2")
SMALL = ("s5_a_re", "s5_a_im", "s5_log_step", "s5_b_re", "s5_b_im", "s5_c_re", "s5_c_im", "s5_d", "s5_b_glu",
         "gla_w_a", "gla_b_a", "gla_ln_g", "swa_sink", "ln1_g", "ln1_b", "ln2_g", "ln2_b")
ANY = pl.BlockSpec(memory_space=pl.ANY)


def _place():
    x, y, c = lax.axis_index("x"), lax.axis_index("y"), lax.axis_index("c")
    return x, y, c, [(1 - x, y), (x, 1 - y), (1 - x, 1 - y)]


HBM = pl.BlockSpec(memory_space=pltpu.HBM)
SEMS = pl.BlockSpec(memory_space=pltpu.SEMAPHORE)
EFFECT = pltpu.SideEffectType.DATAFLOW_SIDE_EFFECTING


def _push_copies(ins, lands, send, recv, gather, sending):
    x, y, c, chips = _place()
    me = 2 * x + y
    if gather == "sibling":
        return [pltpu.make_async_remote_copy(src_ref=ins[a], dst_ref=lands[a], send_sem=send.at[a], recv_sem=recv.at[a],
                                             device_id=(x, y, 1 - c), device_id_type=MESH) for a in range(len(lands))]
    out = []
    for a in range(len(lands)):
        for j, (px, py) in enumerate(chips):
            peer = 2 * px + py
            src = lands[a].at[me] if gather else ins[a].at[peer if sending else me]
            dst = lands[a].at[me if sending else peer]
            out.append(pltpu.make_async_remote_copy(src_ref=src, dst_ref=dst, send_sem=send.at[3 * a + j],
                                                    recv_sem=recv.at[3 * a + j], device_id=(px, py, c),
                                                    device_id_type=MESH))
    return out


def _push_start(name, arrs, gather):
    n = len(arrs)
    ops = list(arrs) if gather is True else list(arrs) + [lax.empty(s.shape, s.dtype) for s in arrs]
    m = len(ops)

    def body(*refs):
        ins, lnd = (refs[:n], refs[:n]) if gather is True else (refs[:n], refs[n:m])
        for cp in _push_copies(ins, lnd, refs[m], refs[m + 1], gather, True):
            cp.start()
        refs[-1][...] = jnp.zeros((8, 128), F32)

    ops = [pltpu.with_memory_space_constraint(t, pltpu.HBM) for t in ops]
    res = pl.pallas_call(
        body, name=name,
        out_shape=(pltpu.SemaphoreType.DMA((3 * n,)), pltpu.SemaphoreType.DMA((3 * n,)),
                   *[pltpu.HBM(t.shape, t.dtype) for t in ops], _sds((8, 128))),
        in_specs=[HBM] * m,
        out_specs=(SEMS, SEMS, *[HBM] * m, pl.BlockSpec(memory_space=pltpu.VMEM)),
        input_output_aliases={i: 2 + i for i in range(m)},
        compiler_params=pltpu.CompilerParams(has_side_effects=EFFECT))(*ops)
    return res[0], res[1], list(res[2:2 + m]), res[-1]


def _push_wait(name, started, after, gather):
    send, recv, ops, _ = started
    m = len(ops)
    n = m if gather is True else m // 2

    def body(*refs):
        ins, lnd = (refs[:n], refs[:n]) if gather is True else (refs[:n], refs[n:m])
        for cp in _push_copies(ins, lnd, refs[m], refs[m + 1], gather, False):
            cp.wait_send()
            cp.wait_recv()

    res = pl.pallas_call(
        body, name=name,
        out_shape=[pltpu.HBM(t.shape, t.dtype) for t in ops],
        in_specs=[HBM] * m + [SEMS, SEMS, ANY], out_specs=[HBM] * m,
        input_output_aliases={i: i for i in range(m)},
        compiler_params=pltpu.CompilerParams(has_side_effects=EFFECT))(*ops, send, recv, after)
    return list(res)


def _row_tile(rows):
    return max(t for t in range(8, min(rows, 512) + 1, 8) if rows % t == 0)


def _cast_to_slot(me, w, l):
    _, rows, cols = w.shape
    tr = _row_tile(rows)

    def body(me_ref, w_ref, o_ref):
        o_ref[0] = w_ref[0].astype(MX)

    return pl.pallas_call(
        body,
        grid_spec=pltpu.PrefetchScalarGridSpec(
            num_scalar_prefetch=1, grid=(rows // tr,),
            in_specs=[pl.BlockSpec((1, tr, cols), lambda i, me_: (l, i, 0))],
            out_specs=pl.BlockSpec((1, tr, cols), lambda i, me_: (me_[0], i, 0))),
        out_shape=_sds((NSHARD, rows, cols), MX), name="cast_to_slot", compiler_params=_cp(("arbitrary",)))(me, w)


def _sum_sources(me, recv, own):
    _, rows, cols = recv[0].shape
    tr = min(_row_tile(rows), 256) if rows % 256 == 0 else _row_tile(rows)
    nt = rows // tr

    def body(me_ref, *refs):
        o_ref = refs[-1]
        for l in range(DEPTH):
            @pl.when(pl.program_id(0) == l)
            def _():
                r_ref, own_ref = refs[2 * l], refs[2 * l + 1]
                part = [jnp.where(me_ref[0] == s, own_ref[0], r_ref[s]).astype(F32) for s in range(NSHARD)]
                o_ref[...] = ((part[0] + part[1]) + part[2]) + part[3]

    in_specs = []
    for l in range(DEPTH):
        pick = lambda g, i, me_, l=l: jnp.where(g == l, i, jnp.where(g < l, 0, nt - 1))
        in_specs += [pl.BlockSpec((NSHARD, tr, cols), lambda g, i, me_, pick=pick: (0, pick(g, i, me_), 0)),
                     pl.BlockSpec((1, tr, cols), lambda g, i, me_, pick=pick: (me_[0], pick(g, i, me_), 0))]
    return pl.pallas_call(
        body,
        grid_spec=pltpu.PrefetchScalarGridSpec(
            num_scalar_prefetch=1, grid=(DEPTH, nt), in_specs=in_specs,
            out_specs=pl.BlockSpec((tr, cols), lambda g, i, me_: (g * nt + i, 0))),
        out_shape=_sds((DEPTH * rows, cols)), name="sum_sources",
        compiler_params=_cp(("arbitrary", "arbitrary")))(me, *[t for l in range(DEPTH) for t in (recv[l], own[l])])


def _allreduce_small(per_layer):
    nk = len(per_layer[0])
    n = DEPTH * nk
    shapes = [a.shape for a in per_layer[0]]

    def body(*refs):
        ins, outs = refs[:n], refs[n:n + nk]
        sibs, slots = refs[n + nk:n + 2 * nk], refs[n + 2 * nk:n + 3 * nk]
        send, recv = refs[n + 3 * nk:]
        x, y, c, chips = _place()
        me = 2 * x + y
        d2d = [pltpu.make_async_remote_copy(src_ref=ins[l * nk + k], dst_ref=sibs[k].at[l], send_sem=send.at[l * nk + k],
                                            recv_sem=recv.at[l * nk + k], device_id=(x, y, 1 - c), device_id_type=MESH)
               for l in range(DEPTH) for k in range(nk)]
        for cp in d2d:
            cp.start()
        for cp in d2d:
            cp.wait()
        for l in range(DEPTH):
            for k in range(nk):
                slots[k][0, l] = ins[l * nk + k][...] + sibs[k][l]

        def swap(k, stage):
            peer = (1 - x, y, c) if stage == Y_FIRST[k] else (x, 1 - y, c)
            return pltpu.make_async_remote_copy(src_ref=slots[k].at[2 * stage], dst_ref=slots[k].at[2 * stage + 1],
                                                send_sem=send.at[n + 3 * k + stage], recv_sem=recv.at[n + 3 * k + stage],
                                                device_id=peer, device_id_type=MESH)

        def handover(k):
            return pltpu.make_async_remote_copy(src_ref=outs[k], dst_ref=outs[k], send_sem=send.at[n + 3 * nk + k],
                                                recv_sem=recv.at[n + 3 * nk + k], device_id=(x, y, 1 - c),
                                                device_id_type=MESH)

        halves = (tuple(k for k in range(nk) if ICI_CORE[k] == 0), tuple(k for k in range(nk) if ICI_CORE[k] == 1))
        for cc in range(2):
            @pl.when(c == cc)
            def _():
                mine, theirs = halves[cc], halves[1 - cc]
                for stage in range(2):
                    cps = [swap(k, stage) for k in mine]
                    for cp in cps:
                        cp.start()
                    for cp in cps:
                        cp.wait()
                    for k in mine:
                        if stage == 0:
                            slots[k][2] = slots[k][0] + slots[k][1]
                        else:
                            outs[k][...] = slots[k][2] + slots[k][3]
                over = [handover(k) for k in mine]
                for cp in over:
                    cp.start()
                for k in theirs:
                    handover(k).wait_recv()
                for cp in over:
                    cp.wait_send()

    vm = pl.BlockSpec(memory_space=pltpu.VMEM)
    return pl.pallas_call(
        body, in_specs=[vm] * n, out_specs=[vm] * nk, out_shape=[_sds((DEPTH,) + s) for s in shapes],
        scratch_shapes=([pltpu.VMEM((DEPTH,) + s, F32) for s in shapes]
                        + [pltpu.VMEM((NSHARD, DEPTH) + s, F32) for s in shapes]
                        + [pltpu.SemaphoreType.DMA((n + 4 * nk,)), pltpu.SemaphoreType.DMA((n + 4 * nk,))]),
        name="allreduce_small", compiler_params=pltpu.CompilerParams(vmem_limit_bytes=VMEM_LIMIT))(
            *[a for layer in per_layer for a in layer])


def _adamw_math(w, g, m, v):
    m = ADAM_B1 * m + (1.0 - ADAM_B1) * g
    v = ADAM_B2 * v + (1.0 - ADAM_B2) * jnp.square(g)
    m_hat = m / (1.0 - ADAM_B1 ** ADAM_STEP)
    v_hat = v / (1.0 - ADAM_B2 ** ADAM_STEP)
    delta = -ADAM_LR * (m_hat / (jnp.sqrt(v_hat) + ADAM_EPS) + ADAM_WD * w)
    return delta, m, v


def _adamw(g_parts, w, m, v):
    rows, cols = w.shape
    tr = 256 if rows % 256 == 0 else _row_tile(rows)
    k = len(g_parts)

    def body(*refs):
        g = refs[0][...]
        for r in refs[1:k]:
            g = g + r[...]
        w_ref, m_ref, v_ref, go, do, mo, vo = refs[k:]
        d, mn, vn = _adamw_math(w_ref[...], g, m_ref[...], v_ref[...])
        go[...] = g
        do[...] = d
        mo[...] = mn
        vo[...] = vn

    spec = pl.BlockSpec((tr, cols), lambda i: (i, 0))
    return pl.pallas_call(
        body, grid=(rows // tr,), in_specs=[spec] * (k + 3), out_specs=[spec] * 4,
        out_shape=[_sds((rows, cols))] * 4, name="adamw", compiler_params=_cp(("parallel",)))(*g_parts, w, m, v)


def _adamw_small(gs, ws, ms, vs):
    n = len(gs)

    def body(*refs):
        for k in range(n):
            d, mn, vn = _adamw_math(refs[n + k][...], refs[k][...], refs[2 * n + k][...], refs[3 * n + k][...])
            refs[4 * n + k][...] = d
            refs[5 * n + k][...] = mn
            refs[6 * n + k][...] = vn

    vm = pl.BlockSpec(memory_space=pltpu.VMEM)
    shapes = [_sds(a.shape) for a in ws]
    res = pl.pallas_call(
        body, in_specs=[vm] * (4 * n), out_specs=[vm] * (3 * n), out_shape=shapes * 3, name="adamw_small",
        compiler_params=pltpu.CompilerParams(vmem_limit_bytes=VMEM_LIMIT))(*gs, *ws, *ms, *vs)
    return res[:n], res[n:2 * n], res[2 * n:]


_ARGS = ("x", "w_in", "s5_a_re", "s5_a_im", "s5_log_step", "s5_b_re", "s5_b_im", "s5_c_re", "s5_c_im", "s5_d",
         "s5_w_glu", "s5_b_glu", "gla_w_a", "gla_b_a", "gla_ln_g", "swa_sink", "w_out", "ln1_g", "ln1_b", "w_ff1",
         "w_ff2", "ln2_g", "ln2_b")
_WEIGHTS = _ARGS[1:]


def kernel(x, w_in, s5_a_re, s5_a_im, s5_log_step, s5_b_re, s5_b_im, s5_c_re, s5_c_im, s5_d, s5_w_glu, s5_b_glu, gla_w_a, gla_b_a, gla_ln_g, swa_sink, w_out, ln1_g, ln1_b, w_ff1, w_ff2, ln2_g, ln2_b, loss_target, m_w_in, m_s5_a_re, m_s5_a_im, m_s5_log_step, m_s5_b_re, m_s5_b_im, m_s5_c_re, m_s5_c_im, m_s5_d, m_s5_w_glu, m_s5_b_glu, m_gla_w_a, m_gla_b_a, m_gla_ln_g, m_swa_sink, m_w_out, m_ln1_g, m_ln1_b, m_w_ff1, m_w_ff2, m_ln2_g, m_ln2_b, v_w_in, v_s5_a_re, v_s5_a_im, v_s5_log_step, v_s5_b_re, v_s5_b_im, v_s5_c_re, v_s5_c_im, v_s5_d, v_s5_w_glu, v_s5_b_glu, v_gla_w_a, v_gla_b_a, v_gla_ln_g, v_swa_sink, v_w_out, v_ln1_g, v_ln1_b, v_w_ff1, v_w_ff2, v_ln2_g, v_ln2_b):
    given = dict(locals())
    w = {k: given[k] for k in _WEIGHTS}
    mom = {k: given["m_" + k] for k in _WEIGHTS}
    var = {k: given["v_" + k] for k in _WEIGHTS}

    me = (2 * lax.axis_index("x") + lax.axis_index("y")).astype(jnp.int32).reshape(1)
    tr = lambda t: t.transpose(0, 2, 1)
    shard = {k: (tr(w[k]) if k == "w_in" else w[k]) for k in BIG}
    qs = [None] * DEPTH

    first = ("w_in", "s5_w_glu", "w_out")
    follow = {(0, "w_in"): [(0, first[1:]), (0, BIG[3:])], (0, "s5_w_glu"): [(1, first)], (0, "w_ff1"): [(1, BIG[3:])]}
    gathers = {}

    casts = {}

    def start_gather(l, names, behind=None):
        lands = [casts.pop((l, k)) if (l, k) in casts else _cast_to_slot(me, shard[k], l) for k in names]
        if behind is not None:
            lands, behind = lax.optimization_barrier((lands, behind))
        st = _push_start(f"gather_start_{l}_{names[0]}", lands, True)
        for k in names:
            gathers[l, k] = [names, st, None]
        return st[-1], behind

    token = start_gather(0, first[:1])[0]
    zero = token[0, 0]
    for l in range(DEPTH):
        for k in BIG:
            if (l, k) not in gathers:
                casts[l, k] = _cast_to_slot(me, lax.optimization_barrier((shard[k], token))[0], l)
        qs[l] = _layer_prep({k: (w[k][l] + zero if k == "s5_a_re" else w[k][l]) for k in SMALL})
    token, casts, qs = lax.optimization_barrier((token, casts, qs))

    def fetch(l, name, after):
        names, st, got = gathers[l, name]
        tie = None
        if got is None:
            if l == 0 and name == "w_in":
                after = token
            lands = _push_wait(f"gather_wait_{l}_{names[0]}", st, after, True)
            for l2, names2 in follow.get((l, name), ()):
                tok, lands[0] = start_gather(l2, names2, lands[0])
                tie = tok if tie is None else tie + tok
            got = dict(zip(names, lands))
            for k in names:
                gathers[l, k][2] = got
        full = got[name]
        if name == "w_in":
            return _in_rows(full, token if tie is None else tie)
        if tie is not None:
            near = "bv" if name == "s5_w_glu" else "ln2_b"
            qs[l][near] = qs[l][near] + tie[0, 0]
        return full.reshape(D, D) if name == "w_out" else full

    scatters, held = [], {}

    def emit(l, grads):
        if l > 0:
            held.update(grads)
            if "w_in" not in grads:
                return 0.0
            grads = dict(held)
            held.clear()
        names = tuple(grads)
        st = _push_start(f"scatter_start_{l}_{names[0]}", [grads[k] for k in names], False)
        scatters.append((l, names, st))
        return st[-1][0, 0]

    loss, dx, smalls = _local_step(x.reshape(N, D), loss_target.reshape(N, D), qs, _rope_tables(128), fetch, emit)

    out, recv, own = {}, {}, {}

    def collect(keys, after):
        for l, names, st in scatters:
            if names[0] in keys:
                ops = _push_wait(f"scatter_wait_{l}_{names[0]}", st, after, False)
                for i, k in enumerate(names):
                    own[l, k], recv[l, k] = ops[i], ops[len(names) + i]

    def shard_sums(keys):
        return [_sum_sources(me, [recv[l, k] for l in range(DEPTH)], [own[l, k] for l in range(DEPTH)]) for k in keys]

    def to_sibling(keys, sums):
        return _push_start(f"swap_start_{keys[0]}", sums, "sibling")

    def apply(keys, started, after):
        ops = _push_wait(f"swap_wait_{keys[0]}", started, after, "sibling")
        for i, k in enumerate(keys):
            mine, other = ops[i], ops[len(keys) + i]
            shp = shard[k].shape
            r = _adamw([mine, other], *((tr(t[k]) if k == "w_in" else t[k]).reshape(-1, shp[-1]) for t in (w, mom, var)))
            r = [t.reshape(shp) for t in r]
            out[k] = [tr(t) for t in r] if k == "w_in" else r
        return out[keys[-1]][1]

    collect(("w_ff1", "w_ff2", "w_out", "s5_w_glu"), dx)
    sums = shard_sums(("w_ff1", "w_ff2", "w_out", "s5_w_glu"))
    sums, smalls[0]["db1"] = lax.optimization_barrier((sums, smalls[0]["db1"]))
    native = _allreduce_small([[smalls[l][k] for k in NATIVE] for l in range(DEPTH)])
    sums, native = lax.optimization_barrier((sums, native))
    ff = to_sibling(("w_ff1", "w_ff2"), sums[:2])
    mix = to_sibling(("w_out", "s5_w_glu"), sums[2:])
    native = dict(zip(NATIVE, native))
    native["db1"] = native["db1"] + (ff[-1][0, 0] + mix[-1][0, 0])
    loss = native["loss"][0, 0, 0] + native["loss"][1, 0, 0]
    gsmall = _finish_small(native, w)
    view = lambda k, t: t.transpose(0, 1, 2, 4, 3) if k in ("s5_b_re", "s5_b_im") else t
    res = _adamw_small(*([view(k, t[k]) for k in SMALL] for t in (gsmall, w, mom, var)))
    for i, k in enumerate(SMALL):
        out[k] = [gsmall[k]] + [view(k, r[i]) for r in res]
    last = apply(("w_ff1", "w_ff2"), ff, res[0][-1])
    collect(("w_in",), last)
    win = to_sibling(("w_in",), shard_sums(("w_in",)))
    last = apply(("w_out", "s5_w_glu"), mix, win[-1])
    apply(("w_in",), win, last)

    return (loss, dx.reshape(NSEQ, L, D), *[out[k][0] for k in _WEIGHTS], *[out[k][1] for k in _WEIGHTS],
            *[out[k][2] for k in _WEIGHTS], *[out[k][3] for k in _WEIGHTS])
```

```python
import functools
import math

import jax
import jax.numpy as jnp
from jax import lax
from jax.experimental import pallas as pl
from jax.experimental.pallas import tpu as pltpu

F32 = jnp.float32
MX = jnp.bfloat16
MESH = pl.DeviceIdType.MESH

DEPTH = 2
NSEQ = 2
L = 2048
N = NSEQ * L
D = 1024
DFF = 4096
NSHARD = 4
S5_G, S5_H, S5_P = 16, 16, 64
GLA_CHUNK = 64
NCHUNK = L // GLA_CHUNK
GLA_GROUP = 4
NGROUP = NCHUNK // GLA_GROUP
SWA_BLK = 128
NBLK = L // SWA_BLK
SWA_PER = 2
ROT = 16
ROPE_THETA = 500000.0
LN_EPS = 1e-5
ALPHA = (2 * DEPTH) ** 0.25
NEG_BIG = -1e30
DIN = 1824
DINP = 1920
ADAM_LR, ADAM_B1, ADAM_B2, ADAM_EPS, ADAM_WD, ADAM_STEP = 0.001, 0.9, 0.999, 1e-08, 0.01, 10
VMEM_LIMIT = 56 * 1024 * 1024
TT = 512
SW = 512
FFN_TM = 512
FFN_TM_W = 1024
FFN_WB = 1
FFN_VMEM = 60 * 1024 * 1024
INPROJ_BWD_TM = 512


def _cp(sem, vmem=VMEM_LIMIT):
    return pltpu.CompilerParams(dimension_semantics=sem, vmem_limit_bytes=vmem)


def _mm(a, b):
    return jnp.dot(a.astype(MX), b.astype(MX), preferred_element_type=F32)


def _mm_nt(a, b):
    return lax.dot_general(a.astype(MX), b.astype(MX), (((1,), (1,)), ((), ())), preferred_element_type=F32)


def _mm_tn(a, b):
    return lax.dot_general(a.astype(MX), b.astype(MX), (((0,), (0,)), ((), ())), preferred_element_type=F32)


@jax.custom_vjp
def _dmm(a, b):
    return _mm(a, b)


_dmm.defvjp(lambda a, b: (_mm(a, b), (a, b)), lambda r, g: (_mm_nt(g, r[1]), _mm_tn(r[0], g)))


@jax.custom_vjp
def _dmm_nt(a, b):
    return _mm_nt(a, b)


_dmm_nt.defvjp(lambda a, b: (_mm_nt(a, b), (a, b)), lambda r, g: (_mm(g, r[1]), _mm_tn(g, r[0])))


@jax.custom_vjp
def _dmm_tn(a, b):
    return _mm_tn(a, b)


_dmm_tn.defvjp(lambda a, b: (_mm_tn(a, b), (a, b)), lambda r, g: (_mm_nt(r[1], g), _mm(r[0], g)))


def _split3(x):
    hi = x.astype(MX)
    r1 = x - hi.astype(F32)
    mid = r1.astype(MX)
    lo = (r1 - mid.astype(F32)).astype(MX)
    return hi, mid, lo


def _chunk_pairs(rows, rev, strict):
    r = lax.broadcasted_iota(jnp.int32, (rows, rows), 0)
    c = lax.broadcasted_iota(jnp.int32, (rows, rows), 1)
    order = ((c > r) if strict else (c >= r)) if rev else ((c < r) if strict else (c <= r))
    return (r // GLA_CHUNK == c // GLA_CHUNK) & order


def _cums_impl(x, rev):
    rows, w = x.shape
    t = jnp.where(_chunk_pairs(rows, rev, False), 1.0, 0.0).astype(MX)
    s = jnp.dot(t, jnp.concatenate(_split3(x), axis=1), preferred_element_type=F32)
    return s[:, 0:w] + s[:, w:2 * w] + s[:, 2 * w:3 * w]


@functools.partial(jax.custom_vjp, nondiff_argnums=(1,))
def _cums(x, rev):
    return _cums_impl(x, rev)


_cums.defvjp(lambda x, rev: (_cums_impl(x, rev), None), lambda rev, r, g: (_cums_impl(g, not rev),))


def _ln_fwd(s, g, b):
    mu = jnp.mean(s, axis=-1, keepdims=True)
    xc = s - mu
    var = jnp.mean(xc * xc, axis=-1, keepdims=True)
    return xc * lax.rsqrt(var + LN_EPS) * g + b


def _ln_bwd(dy, s, g):
    mu = jnp.mean(s, axis=-1, keepdims=True)
    xc = s - mu
    var = jnp.mean(xc * xc, axis=-1, keepdims=True)
    rstd = lax.rsqrt(var + LN_EPS)
    xhat = xc * rstd
    dxh = dy * g
    ds = rstd * (dxh - jnp.mean(dxh, axis=-1, keepdims=True) - xhat * jnp.mean(dxh * xhat, axis=-1, keepdims=True))
    return ds, jnp.sum(dy * xhat, axis=0, keepdims=True), jnp.sum(dy, axis=0, keepdims=True)


def _sds(shape, dtype=F32):
    return jax.ShapeDtypeStruct(shape, dtype)


_IN_ROW_PIECES = (((0, 0), (0, 456)), ((1, 0), (456, 456)), ((2, 0), (912, 112)), ((2, 112), (1792, 32)),
                  ((2, 144), (1024, 312)), ((3, 0), (1336, 456)))


def _in_rows(g4, behind):
    def body(g_ref, behind_ref, o_ref, tmp):
        tmp[DIN:DINP] = jnp.zeros((DINP - DIN, D), F32)
        for (j, s0), (d0, n_) in _IN_ROW_PIECES:
            tmp[d0:d0 + n_] = g_ref[j, s0:s0 + n_].astype(F32)
        o_ref[...] = tmp[...].astype(MX)

    vm = pl.BlockSpec(memory_space=pltpu.VMEM)
    return pl.pallas_call(body, in_specs=[vm, pl.BlockSpec(memory_space=pl.ANY)], out_specs=vm,
                          out_shape=_sds((DINP, D), MX), scratch_shapes=[pltpu.VMEM((DINP, D), F32)], name="in_rows",
                          compiler_params=pltpu.CompilerParams(vmem_limit_bytes=VMEM_LIMIT))(g4, behind)


def _inproj_fwd(x, wt, wa, ba):
    tm = 512

    def body(x_ref, w_ref, wa_ref, ba_ref, h_ref, la_ref):
        h = _mm_nt(x_ref[...], w_ref[...])
        h_ref[...] = h
        la_ref[...] = _logsig(_mm(h[:, DINP - 128:], wa_ref[...]) + ba_ref[...]) * (1.0 / 16.0)

    return pl.pallas_call(
        body, grid=(N // tm,),
        in_specs=[pl.BlockSpec((tm, D), lambda i: (i, 0)), pl.BlockSpec((DINP, D), lambda i: (0, 0)),
                  pl.BlockSpec((128, 256), lambda i: (0, 0)), pl.BlockSpec((1, 256), lambda i: (0, 0))],
        out_specs=[pl.BlockSpec((tm, DINP), lambda i: (i, 0)), pl.BlockSpec((tm, 256), lambda i: (i, 0))],
        out_shape=[_sds((N, DINP)), _sds((N, 256))], name="inproj_fwd", compiler_params=_cp(("parallel",)))(x, wt, wa, ba)


def _inproj_bwd(x, w, dxp, du2, dud, gq_f, gq_b, gk_f, gk_b, gv_f, gv_b, gr, daq, dakv, h, wa, ba, dla_f, dla_b):
    tm = INPROJ_BWD_TM
    nt = N // tm

    def body(x_ref, w_ref, dxp_ref, du2_ref, dud_ref, gqf, gqb, gkf, gkb, gvf, gvb, gr_ref, daq_ref, dakv_ref,
             hl_ref, wa_ref, ba_ref, df_ref, db_ref, dx_ref, dw_ref, dwa_ref, dba_ref, acc):
        i = pl.program_id(0)
        f = lambda r: r[...].astype(F32)
        hl = hl_ref[...]
        pre = _mm(hl, wa_ref[...]) + ba_ref[...]
        dpre = jnp.concatenate([df_ref[...], db_ref[...]], axis=1) * (1.0 / 16.0) * jax.nn.sigmoid(-pre)
        dwa = _mm_tn(hl, dpre)[0:32]
        dba = jnp.sum(dpre, axis=0, keepdims=True)
        dh = jnp.concatenate([
            du2_ref[0] + du2_ref[1] + f(dud_ref), f(gqf) + f(gqb), f(gkf) + f(gkb), f(gvf) + f(gvb),
            f(gr_ref), f(daq_ref), f(dakv_ref), _mm_nt(dpre, wa_ref[...])], axis=1)
        dx_ref[...] = dxp_ref[...] + _mm(dh, w_ref[...])
        contrib = _mm_tn(dh, x_ref[...])

        @pl.when(i == 0)
        def _():
            acc[...] = contrib
            dwa_ref[...] = dwa
            dba_ref[...] = dba

        @pl.when(i > 0)
        def _():
            acc[...] += contrib
            dwa_ref[...] += dwa
            dba_ref[...] += dba

        @pl.when(i == nt - 1)
        def _():
            for (j, d0), (s0, n_) in _IN_ROW_PIECES:
                dw_ref[j, d0:d0 + n_] = acc[s0:s0 + n_].astype(MX)

    row = lambda w_: pl.BlockSpec((tm, w_), lambda i: (i, 0))
    return pl.pallas_call(
        body, grid=(nt,),
        in_specs=[row(D), pl.BlockSpec((DINP, D), lambda i: (0, 0)), row(D),
                  pl.BlockSpec((2, tm, 256), lambda i: (0, i, 0)), row(256), row(128), row(128), row(128), row(128),
                  row(256), row(256), row(256), row(512), row(256),
                  pl.BlockSpec((tm, 128), lambda i: (i, 14)), pl.BlockSpec((128, 256), lambda i: (0, 0)),
                  pl.BlockSpec((1, 256), lambda i: (0, 0)), row(128), row(128)],
        out_specs=[row(D), pl.BlockSpec((NSHARD, DIN // NSHARD, D), lambda i: (0, 0, 0)),
                   pl.BlockSpec((32, 256), lambda i: (0, 0)), pl.BlockSpec((1, 256), lambda i: (0, 0))],
        out_shape=[_sds((N, D)), _sds((NSHARD, DIN // NSHARD, D), MX), _sds((32, 256)), _sds((1, 256))],
        scratch_shapes=[pltpu.VMEM((DINP, D), F32)],
        name="inproj_bwd", compiler_params=_cp(("arbitrary",)))(
            x, w, dxp, du2, dud, gq_f, gq_b, gk_f, gk_b, gv_f, gv_b, gr, daq, dakv, h, wa, ba, dla_f, dla_b)


def _tile_scan(xr, xi, a, cr, ci, reverse):
    for lvl, d in enumerate((1, 2, 4)):
        sh = 8 - d if reverse else d
        sr = pltpu.roll(xr, sh, 0)
        si = pltpu.roll(xi, sh, 0)
        ar, ai = a[2 * lvl], a[2 * lvl + 1]
        xr, xi = xr + ar * sr - ai * si, xi + ar * si + ai * sr
    pr, pi = a[6], a[7]
    return xr + pr * cr - pi * ci, xi + pr * ci + pi * cr


NJ = TT // 8


def _lockstep_tables(mr, mi):
    def body(mr_ref, mi_ref, a_ref, p_ref, ac_ref, pc_ref):
        rowid = lax.broadcasted_iota(jnp.int32, (8, 2 * SW), 0)

        def mul(a, b):
            return a[0] * b[0] - a[1] * b[1], a[0] * b[1] + a[1] * b[0]

        for z in range(2):
            for sign, reverse, a_out, p_out in ((1.0, z == 1, a_ref, p_ref), (-1.0, z == 0, ac_ref, pc_ref)):
                m = (mr_ref[z:z + 1, :], sign * mi_ref[z:z + 1, :])
                pw = [m]
                for _ in range(NJ - 1):
                    pw.append(mul(pw[-1], m))
                n = pw[-1]
                link = [n]
                for _ in range(7):
                    link.append(mul(link[-1], n))
                tiles = [jnp.broadcast_to(m[0], (8, 2 * SW)), jnp.broadcast_to(m[1], (8, 2 * SW))]
                for d in (1, 2, 4):
                    keep = (rowid <= 7 - d) if reverse else (rowid >= d)
                    tiles += [jnp.where(keep, link[d - 1][c], 0.0) for c in range(2)]
                for c in range(2):
                    t = jnp.zeros((8, 2 * SW), F32)
                    for i in range(8):
                        t = jnp.where(rowid == (7 - i if reverse else i), link[i][c], t)
                    tiles.append(t)
                for blk in range(2):
                    lanes = slice(blk * SW, (blk + 1) * SW)
                    for k, t in enumerate(tiles):
                        a_out[z, blk, k] = t[:, lanes]
                    for j in range(NJ):
                        src = pw[NJ - 1 - j] if reverse else pw[j]
                        for c in range(2):
                            p_out[z, blk, c, j:j + 1, :] = src[c][:, lanes]

    vm = pl.BlockSpec(memory_space=pltpu.VMEM)
    a_shape, p_shape = _sds((2, 2, 10, 8, SW)), _sds((2, 2, 2, NJ, SW))
    a, p, ac, pc = pl.pallas_call(body, in_specs=[vm, vm], out_specs=[vm] * 4, out_shape=[a_shape, p_shape] * 2,
                                  name="s5_tables")(mr, mi)
    return (a, p), (ac, pc)


def _to_lockstep(ref, *lead):
    return jnp.concatenate([ref[(*lead, pl.ds(j, 8, stride=NJ), slice(None))] for j in range(NJ)], axis=0)


def _from_lockstep(val, ref, *lead):
    for j in range(NJ):
        ref[(*lead, pl.ds(j, 8, stride=NJ), slice(None))] = val[8 * j:8 * j + 8]


def _expand_powers(p_ref, pexp):
    for c in range(2):
        for j in range(NJ):
            pexp[c, j] = jnp.broadcast_to(p_ref[0, 0, c, j:j + 1, :], (8, SW))


def _lockstep_scan(xre, xim, a_ref, pexp, car, reverse, extra=None):
    a = [a_ref[0, 0, k] for k in range(10)]
    mr, mi = a[0], a[1]
    order = (lambda i: NJ - 1 - i) if reverse else (lambda i: i)

    def local(i, hcar):
        hr, hi = hcar
        r0 = pl.multiple_of(order(i) * 8, 8)
        hr, hi = mr * hr - mi * hi + xre[pl.ds(r0, 8), :], mr * hi + mi * hr + xim[pl.ds(r0, 8), :]
        xre[pl.ds(r0, 8), :] = hr
        xim[pl.ds(r0, 8), :] = hi
        return hr, hi

    z8 = jnp.zeros((8, SW), F32)
    er, ei = lax.fori_loop(0, NJ, local, (z8, z8), unroll=4)
    c0r, c0i = car[0], car[1]
    er, ei = _tile_scan(er, ei, a[2:], c0r, c0i, reverse)
    rowid = lax.broadcasted_iota(jnp.int32, (8, SW), 0)
    first, sh, last = (7, 7, 0) if reverse else (0, 1, 7)
    cvr = jnp.where(rowid == first, c0r, pltpu.roll(er, sh, 0))
    cvi = jnp.where(rowid == first, c0i, pltpu.roll(ei, sh, 0))
    car[0] = jnp.broadcast_to(er[last:last + 1, :], (8, SW))
    car[1] = jnp.broadcast_to(ei[last:last + 1, :], (8, SW))

    def fix(i, carry):
        j = order(i)
        r0 = pl.multiple_of(j * 8, 8)
        pr, pi = pexp[0, j], pexp[1, j]
        sr = xre[pl.ds(r0, 8), :] + pr * cvr - pi * cvi
        si = xim[pl.ds(r0, 8), :] + pr * cvi + pi * cvr
        xre[pl.ds(r0, 8), :] = sr
        xim[pl.ds(r0, 8), :] = si
        if extra is None:
            return carry
        return (sr, si, extra(r0, sr, si, carry[0], carry[1], carry[2]))

    init = (cvr, cvi, extra(None, None, None, None, None, None)) if extra is not None else 0
    return lax.fori_loop(0, NJ, fix, init, unroll=4)


def _s5_time_block(z, s, t, adjoint):
    flip = (1 - z) if adjoint else z
    return s * (L // TT) + t + flip * (L // TT - 1 - 2 * t)


def _s5_fwd(h, bre, bim, cre, cim, tab):
    nt = L // TT
    taba, tabp = tab

    def body(u_ref, bre_ref, bim_ref, cre_ref, cim_ref, a_ref, p_ref, hre_ref, him_ref, y_ref, car, pexp):
        z = pl.program_id(1)
        s = pl.program_id(2)
        tc = pl.program_id(3)

        @pl.when(tc == 0)
        def _():
            car[...] = jnp.zeros_like(car)

        @pl.when((tc == 0) & (s == 0))
        def _():
            _expand_powers(p_ref, pexp)

        u = _to_lockstep(u_ref)
        hre_ref[0] = _mm(u, bre_ref[0, 0])
        him_ref[0] = _mm(u, bim_ref[0, 0])

        @pl.when(z == 0)
        def _():
            _lockstep_scan(hre_ref.at[0], him_ref.at[0], a_ref, pexp, car, False)

        @pl.when(z == 1)
        def _():
            _lockstep_scan(hre_ref.at[0], him_ref.at[0], a_ref, pexp, car, True)

        _from_lockstep(_mm(hre_ref[0], cre_ref[0, 0]) - _mm(him_ref[0], cim_ref[0, 0]), y_ref, 0)

    tb = lambda b, z, s, t: _s5_time_block(z, s, t, False)
    wspec = lambda r, c: pl.BlockSpec((1, 1, r, c), lambda b, z, s, t: (z, b, 0, 0))
    return pl.pallas_call(
        body, grid=(2, 2, NSEQ, nt),
        in_specs=[pl.BlockSpec((TT, 128), lambda b, z, s, t: (tb(b, z, s, t), b)),
                  wspec(128, SW), wspec(128, SW), wspec(SW, 128), wspec(SW, 128),
                  pl.BlockSpec((1, 1, 10, 8, SW), lambda b, z, s, t: (z, b, 0, 0, 0)),
                  pl.BlockSpec((1, 1, 2, NJ, SW), lambda b, z, s, t: (z, b, 0, 0, 0))],
        out_specs=[pl.BlockSpec((1, TT, SW), lambda b, z, s, t: (z, tb(b, z, s, t), b)),
                   pl.BlockSpec((1, TT, SW), lambda b, z, s, t: (z, tb(b, z, s, t), b)),
                   pl.BlockSpec((1, TT, 128), lambda b, z, s, t: (z, tb(b, z, s, t), b))],
        out_shape=[_sds((2, N, 2 * SW)), _sds((2, N, 2 * SW)), _sds((2, N, 256))],
        scratch_shapes=[pltpu.VMEM((2, 8, SW), F32), pltpu.VMEM((2, NJ, 8, SW), F32)],
        name="s5_fwd", compiler_params=_cp(("arbitrary",) * 4))(h, bre, bim, cre, cim, taba, tabp)


def _s5_bwd(h, dyp, hre, him, bre, bim, cre, cim, tabc):
    nt = L // TT
    taba, tabp = tabc

    def body(u_ref, dy_ref, hre_ref, him_ref, bre_ref, bim_ref, cre_ref, cim_ref, a_ref, p_ref,
             du_ref, dbre_ref, dbim_ref, dcre_ref, dcim_ref, dmu_ref, gre, gim, car, acc, macc, pexp):
        z = pl.program_id(1)
        s = pl.program_id(2)
        tc = pl.program_id(3)

        @pl.when(tc == 0)
        def _():
            car[...] = jnp.zeros_like(car)

        @pl.when((tc == 0) & (s == 0))
        def _():
            acc[...] = jnp.zeros_like(acc)
            macc[...] = jnp.zeros_like(macc)
            _expand_powers(p_ref, pexp)

        dy = _to_lockstep(dy_ref)
        gre[...] = _mm_nt(dy, cre_ref[0, 0])
        gim[...] = -_mm_nt(dy, cim_ref[0, 0])

        def run(reverse):
            def pair(r0, gr_, gi_, pvr, pvi, m):
                if r0 is None:
                    return (macc[0], macc[1])
                hr = hre_ref[0, pl.ds(r0, 8), :]
                hi = him_ref[0, pl.ds(r0, 8), :]
                return (m[0] + pvr * hr + pvi * hi, m[1] + pvi * hr - pvr * hi)

            _, _, (dmr, dmi) = _lockstep_scan(gre, gim, a_ref, pexp, car, reverse, pair)
            macc[0] = dmr
            macc[1] = dmi

        @pl.when(z == 0)
        def _():
            run(True)

        @pl.when(z == 1)
        def _():
            run(False)

        gr = gre[...]
        gi = gim[...]
        u = _to_lockstep(u_ref)
        _from_lockstep(_mm_nt(gr, bre_ref[0, 0]) + _mm_nt(gi, bim_ref[0, 0]), du_ref, 0)
        acc[0] += _mm_tn(u, gr)
        acc[1] += _mm_tn(u, gi)
        acc[2] += _mm_tn(dy, hre_ref[0])
        acc[3] -= _mm_tn(dy, him_ref[0])

        @pl.when((tc == nt - 1) & (s == NSEQ - 1))
        def _():
            grp = lax.broadcasted_iota(jnp.int32, (S5_H, SW), 1) // S5_P
            for k, out in enumerate((dbre_ref, dbim_ref, dcre_ref, dcim_ref)):
                c = jnp.zeros((S5_H, SW), F32)
                for i in range(8):
                    c = c + jnp.where(grp == i, acc[k, i * S5_H:(i + 1) * S5_H, :], 0.0)
                out[0, 0] = c
            dmu_ref[0, 0] = jnp.concatenate([jnp.sum(macc[0], axis=0, keepdims=True),
                                             jnp.sum(macc[1], axis=0, keepdims=True)], axis=0)

    tb = lambda b, z, s, t: _s5_time_block(z, s, t, True)
    wspec = lambda r, c: pl.BlockSpec((1, 1, r, c), lambda b, z, s, t: (z, b, 0, 0))
    tok = lambda w_: pl.BlockSpec((TT, w_), lambda b, z, s, t: (tb(b, z, s, t), b))
    st = pl.BlockSpec((1, TT, SW), lambda b, z, s, t: (z, tb(b, z, s, t), b))
    return pl.pallas_call(
        body, grid=(2, 2, NSEQ, nt),
        in_specs=[tok(128), tok(128), st, st, wspec(128, SW), wspec(128, SW), wspec(SW, 128), wspec(SW, 128),
                  pl.BlockSpec((1, 1, 10, 8, SW), lambda b, z, s, t: (z, b, 0, 0, 0)),
                  pl.BlockSpec((1, 1, 2, NJ, SW), lambda b, z, s, t: (z, b, 0, 0, 0))],
        out_specs=[pl.BlockSpec((1, TT, 128), lambda b, z, s, t: (z, tb(b, z, s, t), b)),
                   wspec(S5_H, SW), wspec(S5_H, SW), wspec(S5_H, SW), wspec(S5_H, SW),
                   wspec(2, SW)],
        out_shape=[_sds((2, N, 256))] + [_sds((2, 2, S5_H, SW))] * 4 + [_sds((2, 2, 2, SW))],
        scratch_shapes=[pltpu.VMEM((TT, SW), F32), pltpu.VMEM((TT, SW), F32), pltpu.VMEM((2, 8, SW), F32),
                        pltpu.VMEM((4, 128, SW), F32), pltpu.VMEM((2, 8, SW), F32), pltpu.VMEM((2, NJ, 8, SW), F32)],
        name="s5_bwd", compiler_params=_cp(("arbitrary",) * 4))(h, dyp, hre, him, bre, bim, cre, cim, taba, tabp)


_GELU_C = math.sqrt(2.0 / math.pi)


def _gelu(y):
    return 0.5 * y * (1.0 + jnp.tanh(_GELU_C * (y + 0.044715 * y * y * y)))


def _gelu_grad(y):
    t = jnp.tanh(_GELU_C * (y + 0.044715 * y * y * y))
    return 0.5 * (1.0 + t) + 0.5 * y * (1.0 - t * t) * _GELU_C * (1.0 + 3 * 0.044715 * y * y)


def _glu_halves(w4_ref):
    return (jnp.concatenate([w4_ref[0], w4_ref[1]], axis=1), jnp.concatenate([w4_ref[2], w4_ref[3]], axis=1))


def _s5_glu_fwd(y2, h, dsk, w4, bv, bg):
    tm = 512

    def body(y2_ref, u_ref, d_ref, w4_ref, bv_ref, bg_ref, ya_ref):
        wv, wg = _glu_halves(w4_ref)
        z = _gelu(y2_ref[0] + y2_ref[1] + d_ref[...] * u_ref[...])
        val = _mm(z, wv) + bv_ref[...]
        gate = _mm(z, wg) + bg_ref[...]
        ya_ref[...] = (val * jax.nn.sigmoid(gate)).astype(MX)

    full = lambda r, c: pl.BlockSpec((r, c), lambda i: (0, 0))
    return pl.pallas_call(
        body, grid=(N // tm,),
        in_specs=[pl.BlockSpec((2, tm, 256), lambda i: (0, i, 0)), pl.BlockSpec((tm, 256), lambda i: (i, 0)),
                  full(1, 256), pl.BlockSpec((NSHARD, 256, 128), lambda i: (0, 0, 0)), full(1, 256), full(1, 256)],
        out_specs=pl.BlockSpec((tm, 256), lambda i: (i, 0)),
        out_shape=_sds((N, 256), MX), name="s5_glu_fwd", compiler_params=_cp(("parallel",)))(y2, h, dsk, w4, bv, bg)


def _s5_glu_bwd_tile(i, nt, dya, y2_ref, u_ref, d_ref, w4_ref, bv_ref, bg_ref,
                     dyp_ref, dud_ref, dd_ref, dw4_ref, dbv_ref, dbg_ref, accv, accg):
    @pl.when(i == 0)
    def _():
        for r in (dd_ref, accv, accg, dbv_ref, dbg_ref):
            r[...] = jnp.zeros_like(r)

    wv, wg = _glu_halves(w4_ref)
    u = u_ref[...]
    y = y2_ref[0] + y2_ref[1] + d_ref[...] * u
    z = _gelu(y)
    val = _mm(z, wv) + bv_ref[...]
    sig = jax.nn.sigmoid(_mm(z, wg) + bg_ref[...])
    dval = dya * sig
    dgate = dya * val * sig * (1.0 - sig)
    dz = _mm_nt(dval, wv) + _mm_nt(dgate, wg)
    dy = dz * _gelu_grad(y)
    dyp_ref[...] = dy
    dud_ref[...] = (dy * d_ref[...]).astype(MX)
    dd_ref[...] += jnp.sum(dy * u, axis=0, keepdims=True)
    accv[...] += _mm_tn(z, dval)
    accg[...] += _mm_tn(z, dgate)
    dbv_ref[...] += jnp.sum(dval, axis=0, keepdims=True)
    dbg_ref[...] += jnp.sum(dgate, axis=0, keepdims=True)

    @pl.when(i == nt - 1)
    def _():
        dw4_ref[0] = accv[:, 0:128].astype(MX)
        dw4_ref[1] = accv[:, 128:256].astype(MX)
        dw4_ref[2] = accg[:, 0:128].astype(MX)
        dw4_ref[3] = accg[:, 128:256].astype(MX)


def _logsig(x):
    return jnp.minimum(x, 0.0) - jnp.log(1.0 + jnp.exp(-jnp.abs(x)))


def _gla_chunk(q, k, v, la, st, rev):
    c = GLA_CHUNK
    rows = q.shape[0]
    nch = rows // c
    b = _cums(la, rev)
    blc = [jnp.sum(la[i * c:(i + 1) * c], axis=0, keepdims=True) for i in range(nch)]
    bl = jnp.concatenate([jnp.broadcast_to(t, (c, 128)) for t in blc], axis=0)
    q_in = q * (32.0 ** -0.5) * jnp.exp(b)
    k_in = k * jnp.exp(-b)
    k_st = k * jnp.exp(bl - b)
    lane_k = lax.broadcasted_iota(jnp.int32, (1, 128), 1) // 32
    lane_v = lax.broadcasted_iota(jnp.int32, (1, 256), 1) // 64
    qs = jnp.concatenate([jnp.where(lane_k == hd, q_in, 0.0) for hd in range(4)], axis=0)
    a = _dmm_nt(qs, k_in)
    a = jnp.where(jnp.concatenate([_chunk_pairs(rows, rev, rev)] * 4, axis=0), a, 0.0)
    o4 = _dmm(a, v)
    o = jnp.zeros((rows, 256), F32)
    for hd in range(4):
        o = o + jnp.where(lane_v == hd, o4[hd * rows:(hd + 1) * rows], 0.0)
    bd = (lax.broadcasted_iota(jnp.int32, (256, 128), 0) // 64) == (lax.broadcasted_iota(jnp.int32, (256, 128), 1) // 32)
    inter = [None] * nch
    for i in (reversed(range(nch)) if rev else range(nch)):
        sl = slice(i * c, (i + 1) * c)
        inter[i] = _dmm_nt(q_in[sl], st)
        st = jnp.exp(blc[i]) * st + jnp.where(bd, _dmm_tn(v[sl], k_st[sl]), 0.0)
    return o + jnp.concatenate(inter, axis=0), st


def _gla_chunk_of(c, rev):
    return NGROUP - 1 - c if rev else c


def _gla_fwd(h, la2):
    c = GLA_GROUP * GLA_CHUNK

    def body(qf, kf, vf, laf, qb, kb, vb, lab, of_ref, ob_ref, sf_ref, sb_ref, stf, stb):
        @pl.when(pl.program_id(0) == 0)
        def _():
            stf[...] = jnp.zeros_like(stf)
            stb[...] = jnp.zeros_like(stb)

        ins = [(qf[s], kf[s], vf[s], laf[s], stf[s], qb[s], kb[s], vb[s], lab[s], stb[s]) for s in range(NSEQ)]
        outs = [(_gla_chunk(*t[:5], False), _gla_chunk(*t[5:], True)) for t in ins]
        for s in range(NSEQ):
            sf_ref[s, 0] = ins[s][4]
            sb_ref[s, 0] = ins[s][9]
            (of_ref[s], stf[s]), (ob_ref[s], stb[s]) = outs[s]

    def specs(rev):
        ch = lambda i: _gla_chunk_of(i, rev)
        return [pl.BlockSpec((NSEQ, c, 128), lambda i: (0, ch(i), 2)), pl.BlockSpec((NSEQ, c, 128), lambda i: (0, ch(i), 3)),
                pl.BlockSpec((NSEQ, c, 256), lambda i: (0, ch(i), 2)),
                pl.BlockSpec((NSEQ, c, 128), lambda i: (0, ch(i), 1 if rev else 0))]

    orow = lambda rev: pl.BlockSpec((NSEQ, c, 256), lambda i: (0, _gla_chunk_of(i, rev), 0))
    srow = lambda rev: pl.BlockSpec((NSEQ, 1, 256, 128), lambda i: (0, _gla_chunk_of(i, rev), 0, 0))
    h3, la3 = h.reshape(NSEQ, L, DINP), la2.reshape(NSEQ, L, 256)
    of, ob, sf, sb = pl.pallas_call(
        body, grid=(NGROUP,),
        in_specs=specs(False) + specs(True),
        out_specs=[orow(False), orow(True), srow(False), srow(True)],
        out_shape=[_sds((NSEQ, L, 256)), _sds((NSEQ, L, 256)), _sds((NSEQ, NGROUP, 256, 128)),
                   _sds((NSEQ, NGROUP, 256, 128))],
        scratch_shapes=[pltpu.VMEM((NSEQ, 256, 128), F32), pltpu.VMEM((NSEQ, 256, 128), F32)],
        name="gla_fwd", compiler_params=_cp(("arbitrary",)))(h3, h3, h3, la3, h3, h3, h3, la3)
    return of.reshape(N, 256), ob.reshape(N, 256), sf, sb


def _gla_bwd(h, la2, do, sf, sb):
    c = GLA_GROUP * GLA_CHUNK

    def body(qf, kf, vf, laf, dof, sfr, qb, kb, vb, lab, dob, sbr,
             dqf, dkf, dvf, dlf, dqb, dkb, dvb, dlb, dstf, dstb):
        @pl.when(pl.program_id(0) == 0)
        def _():
            dstf[...] = jnp.zeros_like(dstf)
            dstb[...] = jnp.zeros_like(dstb)

        def one(s, q, k, v, la, do_, st, dst, rev):
            _, vjp = jax.vjp(functools.partial(_gla_chunk, rev=rev), q[s], k[s], v[s], la[s], st[s, 0])
            return vjp((do_[s], dst[s]))

        res = [(one(s, qf, kf, vf, laf, dof, sfr, dstf, False), one(s, qb, kb, vb, lab, dob, sbr, dstb, True))
               for s in range(NSEQ)]
        for s in range(NSEQ):
            for (gq, gk, gv, gl, gs), (dq, dk, dv, dl, dst) in ((res[s][0], (dqf, dkf, dvf, dlf, dstf)),
                                                                  (res[s][1], (dqb, dkb, dvb, dlb, dstb))):
                dq[s], dk[s], dv[s] = gq.astype(MX), gk.astype(MX), gv.astype(MX)
                dl[s], dst[s] = gl, gs

    def specs(rev):
        ch = lambda i: _gla_chunk_of(i, not rev)
        return [pl.BlockSpec((NSEQ, c, 128), lambda i: (0, ch(i), 2)), pl.BlockSpec((NSEQ, c, 128), lambda i: (0, ch(i), 3)),
                pl.BlockSpec((NSEQ, c, 256), lambda i: (0, ch(i), 2)),
                pl.BlockSpec((NSEQ, c, 128), lambda i: (0, ch(i), 1 if rev else 0)),
                pl.BlockSpec((NSEQ, c, 256), lambda i: (0, ch(i), 0)),
                pl.BlockSpec((NSEQ, 1, 256, 128), lambda i: (0, ch(i), 0, 0))]

    def ospecs(rev):
        ch = lambda i: _gla_chunk_of(i, not rev)
        n = pl.BlockSpec((NSEQ, c, 128), lambda i: (0, ch(i), 0))
        return [n, n, pl.BlockSpec((NSEQ, c, 256), lambda i: (0, ch(i), 0)), n]

    oshape = [_sds((NSEQ, L, 128), MX), _sds((NSEQ, L, 128), MX), _sds((NSEQ, L, 256), MX), _sds((NSEQ, L, 128))]
    h3, la3, do3 = h.reshape(NSEQ, L, DINP), la2.reshape(NSEQ, L, 256), do.reshape(NSEQ, L, 256)
    res = pl.pallas_call(
        body, grid=(NGROUP,),
        in_specs=specs(False) + specs(True),
        out_specs=ospecs(False) + ospecs(True),
        out_shape=oshape + oshape,
        scratch_shapes=[pltpu.VMEM((NSEQ, 256, 128), F32), pltpu.VMEM((NSEQ, 256, 128), F32)],
        name="gla_bwd", compiler_params=_cp(("arbitrary",)))(h3, h3, h3, la3, do3, sf, h3, h3, h3, la3, do3, sb)
    return [r.reshape(N, r.shape[-1]) for r in res]


def _gla_post(of, ob, r, g):
    o = of + ob
    head = lax.broadcasted_iota(jnp.int32, (1, 256), 1) // 64
    mu = jnp.zeros_like(o)
    for hd in range(4):
        mu = mu + jnp.where(head == hd, jnp.sum(jnp.where(head == hd, o, 0.0), axis=-1, keepdims=True) * (1.0 / 64.0), 0.0)
    xc = o - mu
    var = jnp.zeros_like(o)
    for hd in range(4):
        var = var + jnp.where(head == hd, jnp.sum(jnp.where(head == hd, xc * xc, 0.0), axis=-1, keepdims=True) * (1.0 / 64.0), 0.0)
    return xc * lax.rsqrt(var + LN_EPS) * g * (r * jax.nn.sigmoid(r))


def _gla_post_fwd(of, ob, h, g):
    tm = 512

    def body(of_ref, ob_ref, r_ref, g_ref, y_ref):
        y_ref[...] = _gla_post(of_ref[...], ob_ref[...], r_ref[...], g_ref[...]).astype(MX)

    row = pl.BlockSpec((tm, 256), lambda i: (i, 0))
    return pl.pallas_call(
        body, grid=(N // tm,),
        in_specs=[row, row, pl.BlockSpec((tm, 256), lambda i: (i, 3)), pl.BlockSpec((1, 256), lambda i: (0, 0))],
        out_specs=row, out_shape=_sds((N, 256), MX), name="gla_post_fwd", compiler_params=_cp(("parallel",)))(of, ob, h, g)


def _rope_tables(width):
    pos = jnp.arange(L, dtype=F32)
    inv_freq = ROPE_THETA ** (-jnp.arange(0, ROT, 2, dtype=F32) / ROT)
    ang = pos[:, None] * inv_freq[None, :]
    cos, sin = jnp.cos(ang), jnp.sin(ang)
    one = jnp.ones((L, 64 - ROT), F32)
    zero = jnp.zeros((L, 64 - ROT), F32)
    z8 = jnp.zeros((L, ROT // 2), F32)
    c = jnp.concatenate([cos, cos, one], axis=1)
    sa = jnp.concatenate([z8, sin, zero], axis=1)
    sb = jnp.concatenate([-sin, z8, zero], axis=1)
    rep = width // 64
    return jnp.stack([jnp.tile(c, (1, rep)), jnp.tile(sa, (1, rep)), jnp.tile(sb, (1, rep))])


def _pieces(t, f):
    out = [f(t[:, c * 128:(c + 1) * 128]) for c in range(t.shape[-1] // 128)]
    return out[0] if len(out) == 1 else jnp.concatenate(out, axis=1)


def _rope(t, tab):
    return _pieces(t, lambda x: x * tab[0] + pltpu.roll(x, ROT // 2, 1) * tab[1] + pltpu.roll(x, 128 - ROT // 2, 1) * tab[2])


def _rope_t(g, tab):
    return _pieces(g, lambda x: x * tab[0] + pltpu.roll(x * tab[1], 128 - ROT // 2, 1) + pltpu.roll(x * tab[2], ROT // 2, 1))


def _swa_pad_kv(kv_ref, tk_ref, kexp, vexp):
    z = jnp.zeros((SWA_BLK, 256), F32)
    kr = _rope(kv_ref[:, 0:128], tk_ref[...])
    for hk in range(2):
        for pad in (kexp, vexp):
            pad[hk, 0:SWA_BLK] = z
            pad[hk, SWA_BLK + L:] = z
        kexp[hk, SWA_BLK:SWA_BLK + L] = _swa_expand(kr, hk)
        vexp[hk, SWA_BLK:SWA_BLK + L] = _swa_expand(kv_ref[:, 128:256], hk)


def _swa_expand(x, hk):
    lane = lax.broadcasted_iota(jnp.int32, x.shape, 1)
    sw = pltpu.roll(x, 64, 1)
    pair = jnp.where(lane < 64, x, sw) if hk == 0 else jnp.where(lane < 64, sw, x)
    return jnp.concatenate([pair, pair], axis=1)


def _swa_fold(x, hk):
    a = x[:, 0:128] + x[:, 128:256]
    t = a + pltpu.roll(a, 64, 1)
    lane = lax.broadcasted_iota(jnp.int32, a.shape, 1)
    return jnp.where((lane < 64) if hk == 0 else (lane >= 64), t, 0.0)


def _swa_bias_tables(bias):
    i = lax.broadcasted_iota(jnp.int32, (SWA_BLK, 3 * SWA_BLK), 0)
    j = lax.broadcasted_iota(jnp.int32, (SWA_BLK, 3 * SWA_BLK), 1)
    band = (j - i >= 0) & (j - i <= 2 * SWA_BLK)
    for v, inside in enumerate((j >= SWA_BLK, True, j < 2 * SWA_BLK)):
        bias[v] = jnp.where(band & inside, 0.0, NEG_BIG)


def _swa_bias(bias, blk):
    return bias[jnp.where(blk == 0, 0, jnp.where(blk == NBLK - 1, 2, 1))]


def _swa_probs(q2, kexp, bias, sink_ref, hk):
    slot = lax.broadcasted_iota(jnp.int32, (1, 256), 1) // 64
    qs = jnp.concatenate([jnp.where(slot == g, q2, 0.0) for g in range(4)], axis=0)
    s = _mm_nt(qs, kexp) + jnp.concatenate([bias] * 4, axis=0)
    rowg = lax.broadcasted_iota(jnp.int32, (4 * SWA_BLK, 1), 0) // SWA_BLK
    sink = jnp.zeros((4 * SWA_BLK, 1), F32)
    for g in range(4):
        sink = jnp.where(rowg == g, sink_ref[hk * 4 + g], sink)
    m = jnp.maximum(jnp.max(s, axis=-1, keepdims=True), sink)
    p = jnp.exp(s - m)
    ps = jnp.exp(sink - m)
    inv = 1.0 / (jnp.sum(p, axis=-1, keepdims=True) + ps)
    return qs, p * inv, ps * inv, slot, rowg


def _swa_qtab(tk_ref, r0):
    return [tk_ref[i, pl.ds(r0, SWA_BLK), :] for i in range(3)]


def _swa_fwd(h, tk, sink):
    def body(sink_ref, q_ref, kv_ref, tk_ref, y_ref, kexp, vexp, bias):
        n = pl.program_id(1)

        @pl.when(n == 0)
        def _():
            _swa_pad_kv(kv_ref, tk_ref, kexp, vexp)
            _swa_bias_tables(bias)

        for t in range(SWA_PER):
            blk = n * SWA_PER + t
            rows = slice(t * SWA_BLK, (t + 1) * SWA_BLK)
            r0 = pl.multiple_of(blk * SWA_BLK, SWA_BLK)
            q = _rope(q_ref[rows, :], _swa_qtab(tk_ref, r0)) * 0.125
            for hk in range(2):
                _, p, _, slot, _ = _swa_probs(q[:, hk * 256:(hk + 1) * 256], kexp[hk, pl.ds(r0, 3 * SWA_BLK), :],
                                              _swa_bias(bias, blk), sink_ref, hk)
                o4 = _mm(p, vexp[hk, pl.ds(r0, 3 * SWA_BLK), :])
                o = jnp.zeros((SWA_BLK, 256), F32)
                for g in range(4):
                    o = o + jnp.where(slot == g, o4[g * SWA_BLK:(g + 1) * SWA_BLK], 0.0)
                y_ref[rows, hk * 256:(hk + 1) * 256] = o.astype(MX)

    tm = SWA_PER * SWA_BLK
    return pl.pallas_call(
        body,
        grid_spec=pltpu.PrefetchScalarGridSpec(
            num_scalar_prefetch=1, grid=(NSEQ, L // tm),
            in_specs=[pl.BlockSpec((tm, 512), lambda s, n, sk: (s * (L // tm) + n, 2)),
                      pl.BlockSpec((L, 256), lambda s, n, sk: (s, 6)),
                      pl.BlockSpec((3, L, 128), lambda s, n, sk: (0, 0, 0))],
            out_specs=pl.BlockSpec((tm, 512), lambda s, n, sk: (s * (L // tm) + n, 0)),
            scratch_shapes=[pltpu.VMEM((2, L + 2 * SWA_BLK, 256), F32), pltpu.VMEM((2, L + 2 * SWA_BLK, 256), F32),
                            pltpu.VMEM((3, SWA_BLK, 3 * SWA_BLK), F32)]),
        out_shape=_sds((N, 512), MX), name="swa_fwd", compiler_params=_cp(("arbitrary", "arbitrary")))(sink, h, h, tk)


def _swa_bwd(h, tk, sink, dyc):
    tm = SWA_PER * SWA_BLK

    def body(sink_ref, q_ref, kv_ref, tk_ref, dy_ref, dq_ref, dkv_ref, dsink_ref, kexp_all, vexp_all, dkacc, dvacc, bias):
        sq = pl.program_id(0)
        n = pl.program_id(1)

        @pl.when(n == 0)
        def _():
            _swa_pad_kv(kv_ref, tk_ref, kexp_all, vexp_all)
            _swa_bias_tables(bias)
            dkacc[...] = jnp.zeros_like(dkacc)
            dvacc[...] = jnp.zeros_like(dvacc)

        @pl.when((n == 0) & (sq == 0))
        def _():
            dsink_ref[...] = jnp.zeros_like(dsink_ref)

        hrow = lax.broadcasted_iota(jnp.int32, (8, 128), 0)
        dsk = jnp.zeros((8, 128), F32)
        for t in range(SWA_PER):
            blk = n * SWA_PER + t
            rows = slice(t * SWA_BLK, (t + 1) * SWA_BLK)
            r0 = pl.multiple_of(blk * SWA_BLK, SWA_BLK)
            tq = _swa_qtab(tk_ref, r0)
            q = _rope(q_ref[rows, :], tq) * 0.125
            band = _swa_bias(bias, blk)
            for hk in range(2):
                kexp = kexp_all[hk, pl.ds(r0, 3 * SWA_BLK), :]
                vexp = vexp_all[hk, pl.ds(r0, 3 * SWA_BLK), :]
                qs, p, ps, slot, rowg = _swa_probs(q[:, hk * 256:(hk + 1) * 256], kexp, band, sink_ref, hk)
                dy2 = dy_ref[rows, hk * 256:(hk + 1) * 256]
                dos = jnp.concatenate([jnp.where(slot == g, dy2, 0.0) for g in range(4)], axis=0)
                dp = _mm_nt(dos, vexp)
                delta = jnp.sum(p * dp, axis=-1, keepdims=True)
                ds = p * (dp - delta)
                dsr = -ps * delta
                for g in range(4):
                    dsk = dsk + jnp.where(hrow == hk * 4 + g,
                                          jnp.sum(jnp.where(rowg == g, dsr, 0.0), axis=0, keepdims=True), 0.0)
                dq4 = _mm(ds, kexp)
                dq2 = jnp.zeros((SWA_BLK, 256), F32)
                for g in range(4):
                    dq2 = dq2 + jnp.where(slot == g, dq4[g * SWA_BLK:(g + 1) * SWA_BLK], 0.0)
                dq_ref[rows, hk * 256:(hk + 1) * 256] = _rope_t(dq2 * 0.125, tq).astype(MX)
                dkacc[hk, pl.ds(r0, 3 * SWA_BLK), :] += _mm_tn(ds, qs)
                dvacc[hk, pl.ds(r0, 3 * SWA_BLK), :] += _mm_tn(p, dos)
        dsink_ref[...] += dsk

        @pl.when(n == L // tm - 1)
        def _():
            seq = slice(SWA_BLK, SWA_BLK + L)
            dk = _rope_t(_swa_fold(dkacc[0, seq], 0) + _swa_fold(dkacc[1, seq], 1), tk_ref[...])
            dkv_ref[:, 0:128] = dk.astype(MX)
            dkv_ref[:, 128:256] = (_swa_fold(dvacc[0, seq], 0) + _swa_fold(dvacc[1, seq], 1)).astype(MX)

    blk = lambda col: pl.BlockSpec((tm, 512), lambda s, n, sk: (s * (L // tm) + n, col))
    pad = pltpu.VMEM((2, L + 2 * SWA_BLK, 256), F32)
    return pl.pallas_call(
        body,
        grid_spec=pltpu.PrefetchScalarGridSpec(
            num_scalar_prefetch=1, grid=(NSEQ, L // tm),
            in_specs=[blk(2), pl.BlockSpec((L, 256), lambda s, n, sk: (s, 6)),
                      pl.BlockSpec((3, L, 128), lambda s, n, sk: (0, 0, 0)), blk(0)],
            out_specs=[blk(0), pl.BlockSpec((L, 256), lambda s, n, sk: (s, 0)),
                       pl.BlockSpec((8, 128), lambda s, n, sk: (0, 0))],
            scratch_shapes=[pad, pad, pad, pad, pltpu.VMEM((3, SWA_BLK, 3 * SWA_BLK), F32)]),
        out_shape=[_sds((N, 512), MX), _sds((N, 256), MX), _sds((8, 128))],
        name="swa_bwd", compiler_params=_cp(("arbitrary", "arbitrary")))(sink, h, h, tk, dyc)


def _outproj_bwd(dx1, s1, ya, yb, yc, wo, g, of, ob, h, lng, y2, dsk, w4, bv, bg):
    tm = 512
    nt = N // tm

    def body(dx1_ref, s_ref, ya_ref, yb_ref, yc_ref, wo_ref, g_ref, of_ref, ob_ref, r_ref, lng_ref,
             y2_ref, u_ref, d_ref, w4_ref, bv_ref, bg_ref,
             dyp_ref, dud_ref, do_ref, dr_ref, dyc_ref, dxp_ref, dwo_ref, dg_ref, db_ref, dlng_ref,
             dd_ref, dw4_ref, dbv_ref, dbg_ref, acc, accv, accg):
        i = pl.program_id(0)

        @pl.when(i == 0)
        def _():
            acc[...] = jnp.zeros_like(acc)
            dg_ref[...] = jnp.zeros_like(dg_ref)
            db_ref[...] = jnp.zeros_like(db_ref)
            dlng_ref[...] = jnp.zeros_like(dlng_ref)

        ds, dg, db = _ln_bwd(dx1_ref[...], s_ref[...], g_ref[...])
        dg_ref[...] += dg
        db_ref[...] += db
        dxp_ref[...] = ALPHA * ds
        dy = _mm_nt(ds, wo_ref[...])
        _s5_glu_bwd_tile(i, nt, dy[:, 0:256], y2_ref, u_ref, d_ref, w4_ref, bv_ref, bg_ref,
                         dyp_ref, dud_ref, dd_ref, dw4_ref, dbv_ref, dbg_ref, accv, accg)
        _, vjp = jax.vjp(_gla_post, of_ref[...], ob_ref[...], r_ref[...], lng_ref[...])
        go, _, gr, gg = vjp(dy[:, 256:512])
        do_ref[...] = go
        dr_ref[...] = gr.astype(MX)
        dlng_ref[...] += gg
        dyc_ref[...] = dy[:, 512:1024].astype(MX)
        acc[0:256] += _mm_tn(ya_ref[...], ds)
        acc[256:512] += _mm_tn(yb_ref[...], ds)
        acc[512:1024] += _mm_tn(yc_ref[...], ds)

        @pl.when(i == nt - 1)
        def _():
            dwo_ref[...] = acc[...].astype(MX)

    row = lambda w_: pl.BlockSpec((tm, w_), lambda i: (i, 0))
    one = pl.BlockSpec((1, D), lambda i: (0, 0))
    full = pl.BlockSpec((D, D), lambda i: (0, 0))
    vec = pl.BlockSpec((1, 256), lambda i: (0, 0))
    wspec = pl.BlockSpec((NSHARD, 256, 128), lambda i: (0, 0, 0))
    return pl.pallas_call(
        body, grid=(nt,),
        in_specs=[row(D), row(D), row(256), row(256), row(512), full, one,
                  row(256), row(256), pl.BlockSpec((tm, 256), lambda i: (i, 3)), vec,
                  pl.BlockSpec((2, tm, 256), lambda i: (0, i, 0)), row(256), vec, wspec, vec, vec],
        out_specs=[row(256), row(256), row(256), row(256), row(512), row(D), full, one, one, vec, vec, wspec, vec, vec],
        out_shape=[_sds((N, 256)), _sds((N, 256), MX), _sds((N, 256)), _sds((N, 256), MX), _sds((N, 512), MX), _sds((N, D)),
                   _sds((D, D), MX), _sds((1, D)), _sds((1, D)), _sds((1, 256)), _sds((1, 256)),
                   _sds((NSHARD, 256, 128), MX), _sds((1, 256)), _sds((1, 256))],
        scratch_shapes=[pltpu.VMEM((D, D), F32), pltpu.VMEM((256, 256), F32), pltpu.VMEM((256, 256), F32)],
        name="outproj_bwd", compiler_params=_cp(("arbitrary",)))(
            dx1, s1, ya, yb, yc, wo, g, of, ob, h, lng, y2, h, dsk, w4, bv, bg)


def _mix_ffn_fwd(ya, yb, yc, x, wo, g1, b1, w1, w2, g, b, target=None):
    tm = FFN_TM
    head = target is not None

    def body(*refs):
        ya_ref, yb_ref, yc_ref, xin_ref, wo_ref, g1_ref, b1_ref, w1_ref, w2_ref, g_ref, b_ref = refs[:11]
        s1_ref, x1_ref, a_ref, s_ref, y_ref = refs[11 + head:16 + head]
        mix = _mm(ya_ref[...], wo_ref[0:256]) + _mm(yb_ref[...], wo_ref[256:512]) + _mm(yc_ref[...], wo_ref[512:1024])
        s1 = ALPHA * xin_ref[...] + mix
        s1_ref[...] = s1
        x = _ln_fwd(s1, g1_ref[...], b1_ref[...])
        x1_ref[...] = x
        xb = x.astype(MX)
        s = ALPHA * x
        for j in range(NSHARD):
            a = _mm(xb, w1_ref[j])
            a_ref[:, j * D:(j + 1) * D] = a.astype(MX)
            s = s + _mm(jnp.square(jnp.maximum(a, 0.0)), w2_ref[j])
        s_ref[...] = s
        x2 = _ln_fwd(s, g_ref[...], b_ref[...])
        if not head:
            y_ref[...] = x2
            return
        l_ref = refs[16 + head]

        @pl.when(pl.program_id(0) == 0)
        def _():
            l_ref[...] = jnp.zeros_like(l_ref)

        e = x2 - refs[11][...]
        y_ref[...] = e * (1.0 / D)
        l_ref[...] += jnp.sum(jnp.sum(e * e, axis=1, keepdims=True), axis=0, keepdims=True) * (0.5 / D)

    rw = lambda w_: pl.BlockSpec((tm, w_), lambda i: (i, 0))
    row = rw(D)
    once = dict(pipeline_mode=pl.Buffered(1))
    wall = pl.BlockSpec((NSHARD, D, D), lambda i: (0, 0, 0), **once)
    one = pl.BlockSpec((1, D), lambda i: (0, 0))
    acc = pl.BlockSpec((8, 128), lambda i: (0, 0))
    return pl.pallas_call(
        body, grid=(N // tm,),
        in_specs=[rw(256), rw(256), rw(512), row, pl.BlockSpec((D, D), lambda i: (0, 0), **once), one, one,
                  wall, wall, one, one] + [row] * head,
        out_specs=[row, row, pl.BlockSpec((tm, DFF), lambda i: (i, 0)), row, row] + [acc] * head,
        out_shape=[_sds((N, D)), _sds((N, D)), _sds((N, DFF), MX), _sds((N, D)), _sds((N, D))] + [_sds((8, 128))] * head,
        name="mix_ffn_fwd", compiler_params=_cp(("arbitrary",), FFN_VMEM))(
            ya, yb, yc, x, wo, g1, b1, w1, w2, g, b, *([target] * head))


def _ffn_bwd_act(dy, s2, a, w1, w2, g):
    tm = FFN_TM

    def body(dy_ref, s_ref, a_ref, w1_ref, w2_ref, g_ref, da_ref, ds_ref, dx1_ref, dg_ref, db_ref):
        @pl.when(pl.program_id(0) == 0)
        def _():
            dg_ref[...] = jnp.zeros_like(dg_ref)
            db_ref[...] = jnp.zeros_like(db_ref)

        ds, dg, db = _ln_bwd(dy_ref[...], s_ref[...], g_ref[...])
        dsb = ds.astype(MX)
        ds_ref[...] = dsb
        dg_ref[...] += dg
        db_ref[...] += db
        dx1 = ALPHA * ds
        for j in range(NSHARD):
            da = (_mm_nt(dsb, w2_ref[j]) * 2.0 * jnp.maximum(a_ref[:, j * D:(j + 1) * D].astype(F32), 0.0)).astype(MX)
            da_ref[:, j * D:(j + 1) * D] = da
            dx1 = dx1 + _mm_nt(da, w1_ref[j])
        dx1_ref[...] = dx1

    row = pl.BlockSpec((tm, D), lambda i: (i, 0))
    wide = pl.BlockSpec((tm, DFF), lambda i: (i, 0))
    wall = pl.BlockSpec((NSHARD, D, D), lambda i: (0, 0, 0))
    one = pl.BlockSpec((1, D), lambda i: (0, 0))
    return pl.pallas_call(
        body, grid=(N // tm,),
        in_specs=[row, row, wide, wall, wall, one],
        out_specs=[wide, row, row, one, one],
        out_shape=[_sds((N, DFF), MX), _sds((N, D), MX), _sds((N, D)), _sds((1, D)), _sds((1, D))],
        name="ffn_bwd_act", compiler_params=_cp(("arbitrary",), FFN_VMEM))(dy, s2, a, w1, w2, g)


def _ffn_bwd_w(x1, da, a, ds):
    tm, nb = FFN_TM_W, FFN_WB
    nt = N // tm

    def body(x_ref, da_ref, a_ref, ds_ref, dw1_ref, dw2_ref, acc1, acc2):
        i = pl.program_id(1)

        @pl.when(i == 0)
        def _():
            acc1[...] = jnp.zeros_like(acc1)
            acc2[...] = jnp.zeros_like(acc2)

        x, ds_ = x_ref[...], ds_ref[...]
        for k in range(nb):
            cols = slice(k * D, (k + 1) * D)
            acc1[k] += _mm_tn(x, da_ref[:, cols])
            acc2[k] += _mm_tn(jnp.square(jnp.maximum(a_ref[:, cols].astype(F32), 0.0)), ds_)

        @pl.when(i == nt - 1)
        def _():
            dw1_ref[...] = acc1[...].astype(MX)
            dw2_ref[...] = acc2[...].astype(MX)

    row = pl.BlockSpec((tm, D), lambda j, i: (i, 0))
    col = pl.BlockSpec((tm, nb * D), lambda j, i: (i, j))
    wj = pl.BlockSpec((nb, D, D), lambda j, i: (j, 0, 0))
    return pl.pallas_call(
        body, grid=(NSHARD // nb, nt),
        in_specs=[row, col, col, row], out_specs=[wj, wj],
        out_shape=[_sds((NSHARD, D, D), MX), _sds((NSHARD, D, D), MX)],
        scratch_shapes=[pltpu.VMEM((nb, D, D), F32), pltpu.VMEM((nb, D, D), F32)],
        name="ffn_bwd_w", compiler_params=_cp(("parallel", "arbitrary"), FFN_VMEM))(x1, da, a, ds)


def _s5_discretize(a_re, a_im, log_step, b_re, b_im):
    lam = lax.complex(a_re, a_im)
    lam_bar = jnp.exp(lam * jnp.exp(log_step))
    b_bar = ((lam_bar - 1.0) / lam)[..., None] * lax.complex(b_re, b_im)
    return jnp.real(lam_bar), jnp.imag(lam_bar), jnp.real(b_bar), jnp.imag(b_bar)


def _s5_in_blocks(b):
    e = jnp.eye(8, dtype=F32)
    return jnp.einsum('ij,zbjph->zbihjp', e, b.reshape(2, 2, 8, S5_P, S5_H)).reshape(2, 2, 128, SW)


def _s5_out_blocks(c):
    e = jnp.eye(8, dtype=F32)
    return jnp.einsum('ij,zbjhp->zbjpih', e, c.reshape(2, 2, 8, S5_H, S5_P)).reshape(2, 2, SW, 128)


def _gate_weight(w_a):
    z = jnp.zeros((16, 128), F32)
    top = jnp.concatenate([w_a[0], z], axis=1)
    bot = jnp.concatenate([z, w_a[1]], axis=1)
    return jnp.concatenate([top, bot, jnp.zeros((96, 256), F32)], axis=0)


def _layer_prep(p):
    lr, li, br, bi = _s5_discretize(p["s5_a_re"], p["s5_a_im"], p["s5_log_step"], p["s5_b_re"], p["s5_b_im"])
    q = dict(p)
    q["bre"] = _s5_in_blocks(br).astype(MX)
    q["bim"] = _s5_in_blocks(bi).astype(MX)
    q["cre"] = _s5_out_blocks(p["s5_c_re"]).astype(MX)
    q["cim"] = _s5_out_blocks(p["s5_c_im"]).astype(MX)
    mr, mi = lr.reshape(2, 1024), li.reshape(2, 1024)
    q["tab"], q["tabc"] = _lockstep_tables(mr, mi)
    q["dsk"] = p["s5_d"].reshape(1, 256)
    q["wa"] = _gate_weight(p["gla_w_a"]).astype(MX)
    q["ba"] = p["gla_b_a"].reshape(1, 256)
    q["lng"] = p["gla_ln_g"].reshape(1, 256)
    q["bv"] = p["s5_b_glu"][:256].reshape(1, 256)
    q["bg"] = p["s5_b_glu"][256:].reshape(1, 256)
    for k in ("ln1_g", "ln1_b", "ln2_g", "ln2_b"):
        q[k] = p[k].reshape(1, D)
    return q


def _layer_fwd(x, q, tk, fetch, target=None):
    q["w_in"] = fetch("w_in", x)
    h, la2 = _inproj_fwd(x, q["w_in"], q["wa"], q["ba"])
    hre, him, y2 = _s5_fwd(h, q["bre"], q["bim"], q["cre"], q["cim"], q["tab"])
    q["w4"] = fetch("s5_w_glu", y2)
    ya = _s5_glu_fwd(y2, h, q["dsk"], q["w4"], q["bv"], q["bg"])
    of, ob, sf, sb = _gla_fwd(h, la2)
    yb = _gla_post_fwd(of, ob, h, q["lng"])
    yc = _swa_fwd(h, tk, q["swa_sink"])
    mixed = ya[:8, :128] + yb[:8, :128] + yc[:8, :128]
    q["w_out"] = fetch("w_out", mixed)
    q["w_ff1"] = fetch("w_ff1", mixed)
    q["w_ff2"] = fetch("w_ff2", mixed)
    s1, x1, a, s2, *out = _mix_ffn_fwd(ya, yb, yc, x, q["w_out"], q["ln1_g"], q["ln1_b"], q["w_ff1"], q["w_ff2"],
                                       q["ln2_g"], q["ln2_b"], target)
    saved = dict(x=x, h=h, hre=hre, him=him, y2=y2, ya=ya, la2=la2, of=of, ob=ob, sf=sf, sb=sb, yb=yb, yc=yc,
                 s1=s1, x1=x1, a=a, s2=s2)
    return (out[0] if target is None else tuple(out)), saved


def _layer_bwd(dy, q, sv, tk, emit):
    g = {}
    da, ds2, dx1, g["dg2"], g["db2"] = _ffn_bwd_act(dy, sv["s2"], sv["a"], q["w_ff1"], q["w_ff2"], q["ln2_g"])
    dw1, dw2 = _ffn_bwd_w(sv["x1"], da, sv["a"], ds2)
    tie = emit(dict(w_ff1=dw1, w_ff2=dw2))
    h = sv["h"]
    (dyp, dud, do, gr, dyc, dxp, dwo, g["dg1"], g["db1"], g["dlng"], g["dd"], dw4, g["dbv"], g["dbg"]) = _outproj_bwd(
        dx1, sv["s1"], sv["ya"], sv["yb"], sv["yc"], q["w_out"], q["ln1_g"] + tie, sv["of"], sv["ob"], h, q["lng"],
        sv["y2"], q["dsk"], q["w4"], q["bv"], q["bg"])
    daq, dakv, g["dsink"] = _swa_bwd(h, tk, q["swa_sink"], dyc)
    gq_f, gk_f, gv_f, gl_f, gq_b, gk_b, gv_b, gl_b = _gla_bwd(h, sv["la2"], do, sv["sf"], sv["sb"])
    tie = emit(dict(w_out=dwo.reshape(NSHARD, D // NSHARD, D), s5_w_glu=dw4))
    du2, g["dbre"], g["dbim"], g["dcre"], g["dcim"], g["dmu"] = _s5_bwd(
        h, dyp, sv["hre"], sv["him"], q["bre"], q["bim"], q["cre"], q["cim"], (q["tabc"][0], q["tabc"][1] + tie))
    dx, dwt, g["dwa"], g["dba"] = _inproj_bwd(sv["x"], q["w_in"], dxp, du2, dud, gq_f, gq_b, gk_f, gk_b, gv_f, gv_b, gr,
                                              daq, dakv, h, q["wa"], q["ba"], gl_f, gl_b)
    tie = emit(dict(w_in=dwt))
    return dx, g, tie


NATIVE = ("dmu", "dbre", "dbim", "dcre", "dcim", "dd", "dbv", "dbg", "dwa", "dba", "dlng", "dsink",
          "dg1", "db1", "dg2", "db2", "loss")
ICI_CORE = (0, 0, 0, 1, 1, 0, 0, 0, 1, 1, 1, 1, 0, 0, 1, 1, 0)
Y_FIRST = (0, 0, 1, 0, 1, 1, 0, 1, 0, 1, 0, 1, 0, 1, 0, 1, 0)


def _finish_small(n, w):
    g = {}
    dmu = n["dmu"]
    dlr = dmu[:, :, :, 0].reshape(DEPTH, 2, S5_G, S5_P)
    dli = dmu[:, :, :, 1].reshape(DEPTH, 2, S5_G, S5_P)

    def unblock(c, perm, shape):
        return c.reshape(DEPTH, 2, 2, S5_H, 8, S5_P).transpose(perm).reshape(shape)

    b_shape, c_shape = (DEPTH, 2, S5_G, S5_P, S5_H), (DEPTH, 2, S5_G, S5_H, S5_P)
    _, vjp = jax.vjp(_s5_discretize, w["s5_a_re"], w["s5_a_im"], w["s5_log_step"], w["s5_b_re"], w["s5_b_im"])
    (g["s5_a_re"], g["s5_a_im"], g["s5_log_step"], g["s5_b_re"], g["s5_b_im"]) = vjp(
        (dlr, dli, unblock(n["dbre"], (0, 1, 2, 4, 5, 3), b_shape), unblock(n["dbim"], (0, 1, 2, 4, 5, 3), b_shape)))
    g["s5_c_re"] = unblock(n["dcre"], (0, 1, 2, 4, 3, 5), c_shape)
    g["s5_c_im"] = unblock(n["dcim"], (0, 1, 2, 4, 3, 5), c_shape)
    g["s5_d"] = n["dd"].reshape(DEPTH, S5_G, S5_H)
    g["s5_b_glu"] = jnp.concatenate([n["dbv"], n["dbg"]], axis=2).reshape(DEPTH, 512)
    g["gla_w_a"] = jnp.stack([n["dwa"][:, 0:16, 0:128], n["dwa"][:, 16:32, 128:256]], axis=1)
    g["gla_b_a"] = n["dba"].reshape(DEPTH, 2, 128)
    g["gla_ln_g"] = n["dlng"].reshape(DEPTH, 256)
    g["swa_sink"] = n["dsink"][:, :, 0]
    for k, s in (("ln1_g", "dg1"), ("ln1_b", "db1"), ("ln2_g", "dg2"), ("ln2_b", "db2")):
        g[k] = n[s].reshape(DEPTH, D)
    return g


def _local_step(x, target, qs, tk, fetch, emit):
    saved = []
    for l, q in enumerate(qs):
        x, sv = _layer_fwd(x, q, tk, functools.partial(fetch, l), target if l == DEPTH - 1 else None)
        saved.append(sv)
    dy, lacc = x
    smalls = [None] * DEPTH
    tie = 0.0
    for l in reversed(range(DEPTH)):
        qs[l]["ln2_g"] = qs[l]["ln2_g"] + tie
        dy, smalls[l], tie = _layer_bwd(dy, qs[l], saved[l], tk, functools.partial(emit, l))
    smalls[0]["db2"] = smalls[0]["db2"] + tie
    for l in range(DEPTH):
        smalls[l]["loss"] = lacc if l == 0 else jnp.zeros_like(lacc)
    return lacc[0, 0], dy, smalls


BIG = ("w_in", "s5_w_glu", "w_out", "w_ff1", "w_ff2")
SMALL = ("s5_a_re", "s5_a_im", "s5_log_step", "s5_b_re", "s5_b_im", "s5_c_re", "s5_c_im", "s5_d", "s5_b_glu",
         "gla_w_a", "gla_b_a", "gla_ln_g", "swa_sink", "ln1_g", "ln1_b", "ln2_g", "ln2_b")
ANY = pl.BlockSpec(memory_space=pl.ANY)


def _place():
    x, y, c = lax.axis_index("x"), lax.axis_index("y"), lax.axis_index("c")
    return x, y, c, [(1 - x, y), (x, 1 - y), (1 - x, 1 - y)]


HBM = pl.BlockSpec(memory_space=pltpu.HBM)
SEMS = pl.BlockSpec(memory_space=pltpu.SEMAPHORE)
EFFECT = pltpu.SideEffectType.DATAFLOW_SIDE_EFFECTING


def _push_copies(ins, lands, send, recv, gather, sending):
    x, y, c, chips = _place()
    me = 2 * x + y
    if gather == "sibling":
        return [pltpu.make_async_remote_copy(src_ref=ins[a], dst_ref=lands[a], send_sem=send.at[a], recv_sem=recv.at[a],
                                             device_id=(x, y, 1 - c), device_id_type=MESH) for a in range(len(lands))]
    out = []
    for a in range(len(lands)):
        for j, (px, py) in enumerate(chips):
            peer = 2 * px + py
            src = lands[a].at[me] if gather else ins[a].at[peer if sending else me]
            dst = lands[a].at[me if sending else peer]
            out.append(pltpu.make_async_remote_copy(src_ref=src, dst_ref=dst, send_sem=send.at[3 * a + j],
                                                    recv_sem=recv.at[3 * a + j], device_id=(px, py, c),
                                                    device_id_type=MESH))
    return out


def _push_start(name, arrs, gather):
    n = len(arrs)
    ops = list(arrs) if gather is True else list(arrs) + [lax.empty(s.shape, s.dtype) for s in arrs]
    m = len(ops)

    def body(*refs):
        ins, lnd = (refs[:n], refs[:n]) if gather is True else (refs[:n], refs[n:m])
        for cp in _push_copies(ins, lnd, refs[m], refs[m + 1], gather, True):
            cp.start()
        refs[-1][...] = jnp.zeros((8, 128), F32)

    ops = [pltpu.with_memory_space_constraint(t, pltpu.HBM) for t in ops]
    res = pl.pallas_call(
        body, name=name,
        out_shape=(pltpu.SemaphoreType.DMA((3 * n,)), pltpu.SemaphoreType.DMA((3 * n,)),
                   *[pltpu.HBM(t.shape, t.dtype) for t in ops], _sds((8, 128))),
        in_specs=[HBM] * m,
        out_specs=(SEMS, SEMS, *[HBM] * m, pl.BlockSpec(memory_space=pltpu.VMEM)),
        input_output_aliases={i: 2 + i for i in range(m)},
        compiler_params=pltpu.CompilerParams(has_side_effects=EFFECT))(*ops)
    return res[0], res[1], list(res[2:2 + m]), res[-1]


def _push_wait(name, started, after, gather):
    send, recv, ops, _ = started
    m = len(ops)
    n = m if gather is True else m // 2

    def body(*refs):
        ins, lnd = (refs[:n], refs[:n]) if gather is True else (refs[:n], refs[n:m])
        for cp in _push_copies(ins, lnd, refs[m], refs[m + 1], gather, False):
            cp.wait_send()
            cp.wait_recv()

    res = pl.pallas_call(
        body, name=name,
        out_shape=[pltpu.HBM(t.shape, t.dtype) for t in ops],
        in_specs=[HBM] * m + [SEMS, SEMS, ANY], out_specs=[HBM] * m,
        input_output_aliases={i: i for i in range(m)},
        compiler_params=pltpu.CompilerParams(has_side_effects=EFFECT))(*ops, send, recv, after)
    return list(res)


def _row_tile(rows):
    return max(t for t in range(8, min(rows, 512) + 1, 8) if rows % t == 0)


def _cast_to_slot(me, w, l):
    _, rows, cols = w.shape
    tr = _row_tile(rows)

    def body(me_ref, w_ref, o_ref):
        o_ref[0] = w_ref[0].astype(MX)

    return pl.pallas_call(
        body,
        grid_spec=pltpu.PrefetchScalarGridSpec(
            num_scalar_prefetch=1, grid=(rows // tr,),
            in_specs=[pl.BlockSpec((1, tr, cols), lambda i, me_: (l, i, 0))],
            out_specs=pl.BlockSpec((1, tr, cols), lambda i, me_: (me_[0], i, 0))),
        out_shape=_sds((NSHARD, rows, cols), MX), name="cast_to_slot", compiler_params=_cp(("arbitrary",)))(me, w)


def _sum_sources(me, recv, own):
    _, rows, cols = recv[0].shape
    tr = min(_row_tile(rows), 256) if rows % 256 == 0 else _row_tile(rows)
    nt = rows // tr

    def body(me_ref, *refs):
        o_ref = refs[-1]
        for l in range(DEPTH):
            @pl.when(pl.program_id(0) == l)
            def _():
                r_ref, own_ref = refs[2 * l], refs[2 * l + 1]
                part = [jnp.where(me_ref[0] == s, own_ref[0], r_ref[s]).astype(F32) for s in range(NSHARD)]
                o_ref[...] = ((part[0] + part[1]) + part[2]) + part[3]

    in_specs = []
    for l in range(DEPTH):
        pick = lambda g, i, me_, l=l: jnp.where(g == l, i, jnp.where(g < l, 0, nt - 1))
        in_specs += [pl.BlockSpec((NSHARD, tr, cols), lambda g, i, me_, pick=pick: (0, pick(g, i, me_), 0)),
                     pl.BlockSpec((1, tr, cols), lambda g, i, me_, pick=pick: (me_[0], pick(g, i, me_), 0))]
    return pl.pallas_call(
        body,
        grid_spec=pltpu.PrefetchScalarGridSpec(
            num_scalar_prefetch=1, grid=(DEPTH, nt), in_specs=in_specs,
            out_specs=pl.BlockSpec((tr, cols), lambda g, i, me_: (g * nt + i, 0))),
        out_shape=_sds((DEPTH * rows, cols)), name="sum_sources",
        compiler_params=_cp(("arbitrary", "arbitrary")))(me, *[t for l in range(DEPTH) for t in (recv[l], own[l])])


def _allreduce_small(per_layer):
    nk = len(per_layer[0])
    n = DEPTH * nk
    shapes = [a.shape for a in per_layer[0]]

    def body(*refs):
        ins, outs = refs[:n], refs[n:n + nk]
        sibs, slots = refs[n + nk:n + 2 * nk], refs[n + 2 * nk:n + 3 * nk]
        send, recv = refs[n + 3 * nk:]
        x, y, c, chips = _place()
        me = 2 * x + y
        d2d = [pltpu.make_async_remote_copy(src_ref=ins[l * nk + k], dst_ref=sibs[k].at[l], send_sem=send.at[l * nk + k],
                                            recv_sem=recv.at[l * nk + k], device_id=(x, y, 1 - c), device_id_type=MESH)
               for l in range(DEPTH) for k in range(nk)]
        for cp in d2d:
            cp.start()
        for cp in d2d:
            cp.wait()
        for l in range(DEPTH):
            for k in range(nk):
                slots[k][0, l] = ins[l * nk + k][...] + sibs[k][l]

        def swap(k, stage):
            peer = (1 - x, y, c) if stage == Y_FIRST[k] else (x, 1 - y, c)
            return pltpu.make_async_remote_copy(src_ref=slots[k].at[2 * stage], dst_ref=slots[k].at[2 * stage + 1],
                                                send_sem=send.at[n + 3 * k + stage], recv_sem=recv.at[n + 3 * k + stage],
                                                device_id=peer, device_id_type=MESH)

        def handover(k):
            return pltpu.make_async_remote_copy(src_ref=outs[k], dst_ref=outs[k], send_sem=send.at[n + 3 * nk + k],
                                                recv_sem=recv.at[n + 3 * nk + k], device_id=(x, y, 1 - c),
                                                device_id_type=MESH)

        halves = (tuple(k for k in range(nk) if ICI_CORE[k] == 0), tuple(k for k in range(nk) if ICI_CORE[k] == 1))
        for cc in range(2):
            @pl.when(c == cc)
            def _():
                mine, theirs = halves[cc], halves[1 - cc]
                for stage in range(2):
                    cps = [swap(k, stage) for k in mine]
                    for cp in cps:
                        cp.start()
                    for cp in cps:
                        cp.wait()
                    for k in mine:
                        if stage == 0:
                            slots[k][2] = slots[k][0] + slots[k][1]
                        else:
                            outs[k][...] = slots[k][2] + slots[k][3]
                over = [handover(k) for k in mine]
                for cp in over:
                    cp.start()
                for k in theirs:
                    handover(k).wait_recv()
                for cp in over:
                    cp.wait_send()

    vm = pl.BlockSpec(memory_space=pltpu.VMEM)
    return pl.pallas_call(
        body, in_specs=[vm] * n, out_specs=[vm] * nk, out_shape=[_sds((DEPTH,) + s) for s in shapes],
        scratch_shapes=([pltpu.VMEM((DEPTH,) + s, F32) for s in shapes]
                        + [pltpu.VMEM((NSHARD, DEPTH) + s, F32) for s in shapes]
                        + [pltpu.SemaphoreType.DMA((n + 4 * nk,)), pltpu.SemaphoreType.DMA((n + 4 * nk,))]),
        name="allreduce_small", compiler_params=pltpu.CompilerParams(vmem_limit_bytes=VMEM_LIMIT))(
            *[a for layer in per_layer for a in layer])


def _adamw_math(w, g, m, v):
    m = ADAM_B1 * m + (1.0 - ADAM_B1) * g
    v = ADAM_B2 * v + (1.0 - ADAM_B2) * jnp.square(g)
    m_hat = m / (1.0 - ADAM_B1 ** ADAM_STEP)
    v_hat = v / (1.0 - ADAM_B2 ** ADAM_STEP)
    delta = -ADAM_LR * (m_hat / (jnp.sqrt(v_hat) + ADAM_EPS) + ADAM_WD * w)
    return delta, m, v


def _adamw(g_parts, w, m, v):
    rows, cols = w.shape
    tr = 256 if rows % 256 == 0 else _row_tile(rows)
    k = len(g_parts)

    def body(*refs):
        g = refs[0][...]
        for r in refs[1:k]:
            g = g + r[...]
        w_ref, m_ref, v_ref, go, do, mo, vo = refs[k:]
        d, mn, vn = _adamw_math(w_ref[...], g, m_ref[...], v_ref[...])
        go[...] = g
        do[...] = d
        mo[...] = mn
        vo[...] = vn

    spec = pl.BlockSpec((tr, cols), lambda i: (i, 0))
    return pl.pallas_call(
        body, grid=(rows // tr,), in_specs=[spec] * (k + 3), out_specs=[spec] * 4,
        out_shape=[_sds((rows, cols))] * 4, name="adamw", compiler_params=_cp(("parallel",)))(*g_parts, w, m, v)


def _adamw_small(gs, ws, ms, vs):
    n = len(gs)

    def body(*refs):
        for k in range(n):
            d, mn, vn = _adamw_math(refs[n + k][...], refs[k][...], refs[2 * n + k][...], refs[3 * n + k][...])
            refs[4 * n + k][...] = d
            refs[5 * n + k][...] = mn
            refs[6 * n + k][...] = vn

    vm = pl.BlockSpec(memory_space=pltpu.VMEM)
    shapes = [_sds(a.shape) for a in ws]
    res = pl.pallas_call(
        body, in_specs=[vm] * (4 * n), out_specs=[vm] * (3 * n), out_shape=shapes * 3, name="adamw_small",
        compiler_params=pltpu.CompilerParams(vmem_limit_bytes=VMEM_LIMIT))(*gs, *ws, *ms, *vs)
    return res[:n], res[n:2 * n], res[2 * n:]


_ARGS = ("x", "w_in", "s5_a_re", "s5_a_im", "s5_log_step", "s5_b_re", "s5_b_im", "s5_c_re", "s5_c_im", "s5_d",
         "s5_w_glu", "s5_b_glu", "gla_w_a", "gla_b_a", "gla_ln_g", "swa_sink", "w_out", "ln1_g", "ln1_b", "w_ff1",
         "w_ff2", "ln2_g", "ln2_b")
_WEIGHTS = _ARGS[1:]


def kernel(x, w_in, s5_a_re, s5_a_im, s5_log_step, s5_b_re, s5_b_im, s5_c_re, s5_c_im, s5_d, s5_w_glu, s5_b_glu, gla_w_a, gla_b_a, gla_ln_g, swa_sink, w_out, ln1_g, ln1_b, w_ff1, w_ff2, ln2_g, ln2_b, loss_target, m_w_in, m_s5_a_re, m_s5_a_im, m_s5_log_step, m_s5_b_re, m_s5_b_im, m_s5_c_re, m_s5_c_im, m_s5_d, m_s5_w_glu, m_s5_b_glu, m_gla_w_a, m_gla_b_a, m_gla_ln_g, m_swa_sink, m_w_out, m_ln1_g, m_ln1_b, m_w_ff1, m_w_ff2, m_ln2_g, m_ln2_b, v_w_in, v_s5_a_re, v_s5_a_im, v_s5_log_step, v_s5_b_re, v_s5_b_im, v_s5_c_re, v_s5_c_im, v_s5_d, v_s5_w_glu, v_s5_b_glu, v_gla_w_a, v_gla_b_a, v_gla_ln_g, v_swa_sink, v_w_out, v_ln1_g, v_ln1_b, v_w_ff1, v_w_ff2, v_ln2_g, v_ln2_b):
    given = dict(locals())
    w = {k: given[k] for k in _WEIGHTS}
    mom = {k: given["m_" + k] for k in _WEIGHTS}
    var = {k: given["v_" + k] for k in _WEIGHTS}

    me = (2 * lax.axis_index("x") + lax.axis_index("y")).astype(jnp.int32).reshape(1)
    tr = lambda t: t.transpose(0, 2, 1)
    shard = {k: (tr(w[k]) if k == "w_in" else w[k]) for k in BIG}
    qs = [None] * DEPTH

    first = ("w_in", "s5_w_glu", "w_out")
    follow = {(0, "w_in"): [(0, first[1:]), (0, BIG[3:])], (0, "s5_w_glu"): [(1, first)], (0, "w_ff1"): [(1, BIG[3:])]}
    gathers = {}

    casts = {}

    def start_gather(l, names, behind=None):
        lands = [casts.pop((l, k)) if (l, k) in casts else _cast_to_slot(me, shard[k], l) for k in names]
        if behind is not None:
            lands, behind = lax.optimization_barrier((lands, behind))
        st = _push_start(f"gather_start_{l}_{names[0]}", lands, True)
        for k in names:
            gathers[l, k] = [names, st, None]
        return st[-1], behind

    token = start_gather(0, first[:1])[0]
    zero = token[0, 0]
    for l in range(DEPTH):
        for k in BIG:
            if (l, k) not in gathers:
                casts[l, k] = _cast_to_slot(me, lax.optimization_barrier((shard[k], token))[0], l)
        qs[l] = _layer_prep({k: (w[k][l] + zero if k == "s5_a_re" else w[k][l]) for k in SMALL})
    token, casts, qs = lax.optimization_barrier((token, casts, qs))

    def fetch(l, name, after):
        names, st, got = gathers[l, name]
        tie = None
        if got is None:
            if l == 0 and name == "w_in":
                after = token
            lands = _push_wait(f"gather_wait_{l}_{names[0]}", st, after, True)
            for l2, names2 in follow.get((l, name), ()):
                tok, lands[0] = start_gather(l2, names2, lands[0])
                tie = tok if tie is None else tie + tok
            got = dict(zip(names, lands))
            for k in names:
                gathers[l, k][2] = got
        full = got[name]
        if name == "w_in":
            return _in_rows(full, token if tie is None else tie)
        if tie is not None:
            near = "bv" if name == "s5_w_glu" else "ln2_b"
            qs[l][near] = qs[l][near] + tie[0, 0]
        return full.reshape(D, D) if name == "w_out" else full

    scatters, held = [], {}

    def emit(l, grads):
        if l > 0:
            held.update(grads)
            if "w_in" not in grads:
                return 0.0
            grads = dict(held)
            held.clear()
        names = tuple(grads)
        st = _push_start(f"scatter_start_{l}_{names[0]}", [grads[k] for k in names], False)
        scatters.append((l, names, st))
        return st[-1][0, 0]

    loss, dx, smalls = _local_step(x.reshape(N, D), loss_target.reshape(N, D), qs, _rope_tables(128), fetch, emit)

    out, recv, own = {}, {}, {}

    def collect(keys, after):
        for l, names, st in scatters:
            if names[0] in keys:
                ops = _push_wait(f"scatter_wait_{l}_{names[0]}", st, after, False)
                for i, k in enumerate(names):
                    own[l, k], recv[l, k] = ops[i], ops[len(names) + i]

    def shard_sums(keys):
        return [_sum_sources(me, [recv[l, k] for l in range(DEPTH)], [own[l, k] for l in range(DEPTH)]) for k in keys]

    def to_sibling(keys, sums):
        return _push_start(f"swap_start_{keys[0]}", sums, "sibling")

    def apply(keys, started, after):
        ops = _push_wait(f"swap_wait_{keys[0]}", started, after, "sibling")
        for i, k in enumerate(keys):
            mine, other = ops[i], ops[len(keys) + i]
            shp = shard[k].shape
            r = _adamw([mine, other], *((tr(t[k]) if k == "w_in" else t[k]).reshape(-1, shp[-1]) for t in (w, mom, var)))
            r = [t.reshape(shp) for t in r]
            out[k] = [tr(t) for t in r] if k == "w_in" else r
        return out[keys[-1]][1]

    collect(("w_ff1", "w_ff2", "w_out", "s5_w_glu"), dx)
    sums = shard_sums(("w_ff1", "w_ff2", "w_out", "s5_w_glu"))
    sums, smalls[0]["db1"] = lax.optimization_barrier((sums, smalls[0]["db1"]))
    native = _allreduce_small([[smalls[l][k] for k in NATIVE] for l in range(DEPTH)])
    sums, native = lax.optimization_barrier((sums, native))
    ff = to_sibling(("w_ff1", "w_ff2"), sums[:2])
    mix = to_sibling(("w_out", "s5_w_glu"), sums[2:])
    native = dict(zip(NATIVE, native))
    native["db1"] = native["db1"] + (ff[-1][0, 0] + mix[-1][0, 0])
    loss = native["loss"][0, 0, 0] + native["loss"][1, 0, 0]
    gsmall = _finish_small(native, w)
    view = lambda k, t: t.transpose(0, 1, 2, 4, 3) if k in ("s5_b_re", "s5_b_im") else t
    res = _adamw_small(*([view(k, t[k]) for k in SMALL] for t in (gsmall, w, mom, var)))
    for i, k in enumerate(SMALL):
        out[k] = [gsmall[k]] + [view(k, r[i]) for r in res]
    last = apply(("w_ff1", "w_ff2"), ff, res[0][-1])
    collect(("w_in",), last)
    win = to_sibling(("w_in",), shard_sums(("w_in",)))
    last = apply(("w_out", "s5_w_glu"), mix, win[-1])
    apply(("w_in",), win, last)

    return (loss, dx.reshape(NSEQ, L, D), *[out[k][0] for k in _WEIGHTS], *[out[k][1] for k in _WEIGHTS],
            *[out[k][2] for k in _WEIGHTS], *[out[k][3] for k in _WEIGHTS])
```
